```python
import jax, jax.numpy as jnp
from jax import lax
import numpy as np

D_MODEL = 2048
BATCH = 8
SEQ = 2048
DEPTH = 1

RMS_EPS = 1e-6
LN_EPS = 1e-5

RWKV_WIDTH = D_MODEL // 2
RWKV_HEAD = 64
RWKV_HEADS = RWKV_WIDTH // RWKV_HEAD
LNX_EPS = 64e-5


def _lora_dim(factor, power):
    return max(32, int(round(factor * D_MODEL ** power / 32)) * 32)


DECAY_LORA = _lora_dim(1.8, 0.5)
A_LORA = _lora_dim(1.8, 0.5)
GATE_LORA = _lora_dim(0.6, 0.8)

SGU_WIDTH = D_MODEL // 2
CHUNK = 128
SGU_GROUP_DIM = 128
SGU_GROUPS = SGU_WIDTH // SGU_GROUP_DIM

FFN_HIDDEN = ((8 * D_MODEL // 3 + 255) // 256) * 256

RWKV_COLS = 3 * RWKV_WIDTH + DECAY_LORA + A_LORA + GATE_LORA
SGU_COLS = 2 * SGU_WIDTH
GATE_COLS = 2 * D_MODEL
IN_COLS = RWKV_COLS + SGU_COLS + GATE_COLS
IN_SPLITS = [RWKV_COLS, RWKV_COLS + SGU_COLS, RWKV_COLS + SGU_COLS + D_MODEL]
RWKV_SPLITS = [RWKV_WIDTH, 2 * RWKV_WIDTH, 3 * RWKV_WIDTH,
               3 * RWKV_WIDTH + DECAY_LORA, 3 * RWKV_WIDTH + DECAY_LORA + A_LORA]

kernel_name = "rwkv7_sgu_gated_hybrid_block"


def _rmsnorm(x, g):
    xf = x.astype(jnp.float32)
    y = xf * lax.rsqrt(jnp.mean(xf * xf, axis=-1, keepdims=True) + RMS_EPS)
    return (y * g.astype(jnp.float32)).astype(x.dtype)


def _token_shift(p, mu):
    prev = jnp.pad(p, ((0, 0), (1, 0), (0, 0)))[:, :-1]
    return p + (prev - p) * mu


def _wkv7(r, decay, k, v, a, b):
    Bb, T, H, N = r.shape

    def step(S, inp):
        r_t, w_t, k_t, v_t, a_t, b_t = inp
        sa = jnp.einsum('bhij,bhj->bhi', S, a_t)
        S = (S * w_t[:, :, None, :] + sa[..., None] * b_t[:, :, None, :]
             + v_t[..., None] * k_t[:, :, None, :])
        y_t = jnp.einsum('bhij,bhj->bhi', S, r_t)
        return S, y_t

    xs = tuple(jnp.moveaxis(t, 1, 0) for t in (r, decay, k, v, a, b))
    S0 = jnp.zeros((Bb, H, N, N), jnp.float32)
    _, y = lax.scan(step, S0, xs)
    return jnp.moveaxis(y, 0, 1)


def _rwkv7_time_mix(p, mu, w0, w_lora_up, a0, a_lora_up, g_lora_up, k_k, k_a, r_k, lnx_g, lnx_b):
    Bb, T, _ = p.shape
    H, N = RWKV_HEADS, RWKV_HEAD
    f32 = jnp.float32
    p = _token_shift(p, mu)
    r, k, v, xw, xa, xg = jnp.split(p, RWKV_SPLITS, axis=-1)
    w_log = -jax.nn.softplus(-(w0 + jnp.tanh(xw) @ w_lora_up)) - 0.5
    decay = jnp.exp(-jnp.exp(w_log.astype(f32)))
    a = jax.nn.sigmoid(a0 + xa @ a_lora_up)
    g = jax.nn.sigmoid(xg) @ g_lora_up

    heads = lambda t: t.astype(f32).reshape(Bb, T, H, N)
    kk = heads(k * k_k)
    kk = kk / jnp.maximum(jnp.sqrt(jnp.sum(kk * kk, axis=-1, keepdims=True)), 1e-12)
    k = k * (1.0 + (a - 1.0) * k_a)
    r_h, k_h, v_h, a_h, w_h = heads(r), heads(k), heads(v), heads(a), heads(decay)

    y = _wkv7(r_h, w_h, k_h, v_h, -kk, kk * a_h)
    mean = jnp.mean(y, axis=-1, keepdims=True)
    var = jnp.mean(jnp.square(y - mean), axis=-1, keepdims=True)
    y = ((y - mean) * lax.rsqrt(var + LNX_EPS)).reshape(Bb, T, RWKV_WIDTH)
    y = y * lnx_g.astype(f32) + lnx_b.astype(f32)
    bonus = jnp.sum(r_h * k_h * r_k.astype(f32), axis=-1, keepdims=True) * v_h
    y = y + bonus.reshape(Bb, T, RWKV_WIDTH)
    return (y * g.astype(f32)).astype(p.dtype)


def _chunked_sgu(z, ln_g, ln_b, w_s, b_s):
    Bb, T, _ = z.shape
    z = jax.nn.gelu(z, approximate=False)
    u, v = jnp.split(z, 2, axis=-1)
    vf = v.astype(jnp.float32)
    mean = jnp.mean(vf, axis=-1, keepdims=True)
    var = jnp.mean(jnp.square(vf - mean), axis=-1, keepdims=True)
    v = (((vf - mean) * lax.rsqrt(var + LN_EPS)) * ln_g.astype(jnp.float32)
         + ln_b.astype(jnp.float32)).astype(z.dtype)
    vc = v.reshape(Bb, T // CHUNK, CHUNK, SGU_GROUPS, SGU_GROUP_DIM)
    w_causal = w_s * jnp.tril(jnp.ones((CHUNK, CHUNK), w_s.dtype))
    mixed = jnp.einsum('gts,bnsgc->bntgc', w_causal, vc) + b_s.T[None, None, :, :, None]
    return u * mixed.reshape(Bb, T, SGU_WIDTH)


def _fwd_setup_inputs(seed: int = 0) -> dict:
    key = jax.random.key(seed)
    ks = iter(jax.random.split(key, 32))
    nrm = lambda shape, scale: jax.random.normal(next(ks), shape, jnp.float32) * scale
    L, RW = DEPTH, RWKV_WIDTH
    return {
        "x": nrm((BATCH, SEQ, D_MODEL), 1.0),
        "norm_mix_g": 1.0 + nrm((L, D_MODEL), 0.02),
        "w_in": nrm((L, D_MODEL, IN_COLS), D_MODEL ** -0.5),
        "shift_mu": jax.random.uniform(next(ks), (L, RWKV_COLS), jnp.float32),
        "w0": -2.0 + nrm((L, RW), 1.0),
        "w_lora_up": nrm((L, DECAY_LORA, RW), DECAY_LORA ** -0.5),
        "a0": nrm((L, RW), 0.5),
        "a_lora_up": nrm((L, A_LORA, RW), A_LORA ** -0.5),
        "g_lora_up": nrm((L, GATE_LORA, RW), GATE_LORA ** -0.5),
        "k_k": 1.0 + nrm((L, RW), 0.1),
        "k_a": 1.0 + nrm((L, RW), 0.1),
        "r_k": nrm((L, RWKV_HEADS, RWKV_HEAD), 0.1),
        "lnx_g": 1.0 + nrm((L, RW), 0.02),
        "lnx_b": nrm((L, RW), 0.02),
        "w_proj_rwkv": nrm((L, RW, D_MODEL), RW ** -0.5),
        "sgu_ln_g": 1.0 + nrm((L, SGU_WIDTH), 0.02),
        "sgu_ln_b": nrm((L, SGU_WIDTH), 0.02),
        "sgu_w": nrm((L, SGU_GROUPS, CHUNK, CHUNK), CHUNK ** -0.5),
        "sgu_b": 1.0 + nrm((L, SGU_GROUPS, CHUNK), 0.02),
        "w_proj_sgu": nrm((L, SGU_WIDTH, D_MODEL), SGU_WIDTH ** -0.5),
        "w_out": nrm((L, D_MODEL, D_MODEL), D_MODEL ** -0.5),
        "norm_ffn_g": 1.0 + nrm((L, D_MODEL), 0.02),
        "w_ffn_gate": nrm((L, D_MODEL, FFN_HIDDEN), D_MODEL ** -0.5),
        "w_ffn_up": nrm((L, D_MODEL, FFN_HIDDEN), D_MODEL ** -0.5),
        "w_ffn_down": nrm((L, FFN_HIDDEN, D_MODEL), FFN_HIDDEN ** -0.5),
        "norm_final_g": 1.0 + nrm((D_MODEL,), 0.02),
    }


def _fwd_reference(x, norm_mix_g, w_in, shift_mu, w0, w_lora_up, a0, a_lora_up, g_lora_up,
              k_k, k_a, r_k, lnx_g, lnx_b, w_proj_rwkv, sgu_ln_g, sgu_ln_b, sgu_w, sgu_b,
              w_proj_sgu, w_out, norm_ffn_g, w_ffn_gate, w_ffn_up, w_ffn_down, norm_final_g):
    h = x
    for l in range(DEPTH):
        n = _rmsnorm(h, norm_mix_g[l])
        proj = n @ w_in[l]
        p_rwkv, z_sgu, gate_a, gate_b = jnp.split(proj, IN_SPLITS, axis=-1)
        y_a = _rwkv7_time_mix(p_rwkv, shift_mu[l], w0[l], w_lora_up[l], a0[l], a_lora_up[l],
                              g_lora_up[l], k_k[l], k_a[l], r_k[l], lnx_g[l], lnx_b[l])
        y_b = _chunked_sgu(z_sgu, sgu_ln_g[l], sgu_ln_b[l], sgu_w[l], sgu_b[l])
        merged = (jax.nn.sigmoid(gate_a) * (y_a @ w_proj_rwkv[l])
                  + jax.nn.sigmoid(gate_b) * (y_b @ w_proj_sgu[l]))
        h = h + merged @ w_out[l]
        n = _rmsnorm(h, norm_ffn_g[l])
        h = h + (jax.nn.silu(n @ w_ffn_gate[l]) * (n @ w_ffn_up[l])) @ w_ffn_down[l]
    return _rmsnorm(h, norm_final_g)


import jax as _jax
import jax.numpy as _jnp

TWIN_FORMAT = 'train_step'
FWD_PARAMS = ['x', 'norm_mix_g', 'w_in', 'shift_mu', 'w0', 'w_lora_up', 'a0', 'a_lora_up', 'g_lora_up', 'k_k', 'k_a', 'r_k', 'lnx_g', 'lnx_b', 'w_proj_rwkv', 'sgu_ln_g', 'sgu_ln_b', 'sgu_w', 'sgu_b', 'w_proj_sgu', 'w_out', 'norm_ffn_g', 'w_ffn_gate', 'w_ffn_up', 'w_ffn_down', 'norm_final_g']
TWIN_WEIGHTS = ['norm_mix_g', 'w_in', 'shift_mu', 'w0', 'w_lora_up', 'a0', 'a_lora_up', 'g_lora_up', 'k_k', 'k_a', 'r_k', 'lnx_g', 'lnx_b', 'w_proj_rwkv', 'sgu_ln_g', 'sgu_ln_b', 'sgu_w', 'sgu_b', 'w_proj_sgu', 'w_out', 'norm_ffn_g', 'w_ffn_gate', 'w_ffn_up', 'w_ffn_down', 'norm_final_g']
TWIN_DIFF_INPUT = 'x'
TWIN_INPUTS = ['x', 'norm_mix_g', 'w_in', 'shift_mu', 'w0', 'w_lora_up', 'a0', 'a_lora_up', 'g_lora_up', 'k_k', 'k_a', 'r_k', 'lnx_g', 'lnx_b', 'w_proj_rwkv', 'sgu_ln_g', 'sgu_ln_b', 'sgu_w', 'sgu_b', 'w_proj_sgu', 'w_out', 'norm_ffn_g', 'w_ffn_gate', 'w_ffn_up', 'w_ffn_down', 'norm_final_g', 'loss_target', 'm_norm_mix_g', 'm_w_in', 'm_shift_mu', 'm_w0', 'm_w_lora_up', 'm_a0', 'm_a_lora_up', 'm_g_lora_up', 'm_k_k', 'm_k_a', 'm_r_k', 'm_lnx_g', 'm_lnx_b', 'm_w_proj_rwkv', 'm_sgu_ln_g', 'm_sgu_ln_b', 'm_sgu_w', 'm_sgu_b', 'm_w_proj_sgu', 'm_w_out', 'm_norm_ffn_g', 'm_w_ffn_gate', 'm_w_ffn_up', 'm_w_ffn_down', 'm_norm_final_g', 'v_norm_mix_g', 'v_w_in', 'v_shift_mu', 'v_w0', 'v_w_lora_up', 'v_a0', 'v_a_lora_up', 'v_g_lora_up', 'v_k_k', 'v_k_a', 'v_r_k', 'v_lnx_g', 'v_lnx_b', 'v_w_proj_rwkv', 'v_sgu_ln_g', 'v_sgu_ln_b', 'v_sgu_w', 'v_sgu_b', 'v_w_proj_sgu', 'v_w_out', 'v_norm_ffn_g', 'v_w_ffn_gate', 'v_w_ffn_up', 'v_w_ffn_down', 'v_norm_final_g']
TWIN_OUTPUTS = ['loss', 'grad_x', 'grad_norm_mix_g', 'grad_w_in', 'grad_shift_mu', 'grad_w0', 'grad_w_lora_up', 'grad_a0', 'grad_a_lora_up', 'grad_g_lora_up', 'grad_k_k', 'grad_k_a', 'grad_r_k', 'grad_lnx_g', 'grad_lnx_b', 'grad_w_proj_rwkv', 'grad_sgu_ln_g', 'grad_sgu_ln_b', 'grad_sgu_w', 'grad_sgu_b', 'grad_w_proj_sgu', 'grad_w_out', 'grad_norm_ffn_g', 'grad_w_ffn_gate', 'grad_w_ffn_up', 'grad_w_ffn_down', 'grad_norm_final_g', 'delta_norm_mix_g', 'delta_w_in', 'delta_shift_mu', 'delta_w0', 'delta_w_lora_up', 'delta_a0', 'delta_a_lora_up', 'delta_g_lora_up', 'delta_k_k', 'delta_k_a', 'delta_r_k', 'delta_lnx_g', 'delta_lnx_b', 'delta_w_proj_rwkv', 'delta_sgu_ln_g', 'delta_sgu_ln_b', 'delta_sgu_w', 'delta_sgu_b', 'delta_w_proj_sgu', 'delta_w_out', 'delta_norm_ffn_g', 'delta_w_ffn_gate', 'delta_w_ffn_up', 'delta_w_ffn_down', 'delta_norm_final_g', 'new_m_norm_mix_g', 'new_m_w_in', 'new_m_shift_mu', 'new_m_w0', 'new_m_w_lora_up', 'new_m_a0', 'new_m_a_lora_up', 'new_m_g_lora_up', 'new_m_k_k', 'new_m_k_a', 'new_m_r_k', 'new_m_lnx_g', 'new_m_lnx_b', 'new_m_w_proj_rwkv', 'new_m_sgu_ln_g', 'new_m_sgu_ln_b', 'new_m_sgu_w', 'new_m_sgu_b', 'new_m_w_proj_sgu', 'new_m_w_out', 'new_m_norm_ffn_g', 'new_m_w_ffn_gate', 'new_m_w_ffn_up', 'new_m_w_ffn_down', 'new_m_norm_final_g', 'new_v_norm_mix_g', 'new_v_w_in', 'new_v_shift_mu', 'new_v_w0', 'new_v_w_lora_up', 'new_v_a0', 'new_v_a_lora_up', 'new_v_g_lora_up', 'new_v_k_k', 'new_v_k_a', 'new_v_r_k', 'new_v_lnx_g', 'new_v_lnx_b', 'new_v_w_proj_rwkv', 'new_v_sgu_ln_g', 'new_v_sgu_ln_b', 'new_v_sgu_w', 'new_v_sgu_b', 'new_v_w_proj_sgu', 'new_v_w_out', 'new_v_norm_ffn_g', 'new_v_w_ffn_gate', 'new_v_w_ffn_up', 'new_v_w_ffn_down', 'new_v_norm_final_g']
TWIN_LEAF_KINDS = {'loss': 'loss', 'grad_x': 'grad_x', 'grad_norm_mix_g': 'grad_w', 'grad_w_in': 'grad_w', 'grad_shift_mu': 'grad_w', 'grad_w0': 'grad_w', 'grad_w_lora_up': 'grad_w', 'grad_a0': 'grad_w', 'grad_a_lora_up': 'grad_w', 'grad_g_lora_up': 'grad_w', 'grad_k_k': 'grad_w', 'grad_k_a': 'grad_w', 'grad_r_k': 'grad_w', 'grad_lnx_g': 'grad_w', 'grad_lnx_b': 'grad_w', 'grad_w_proj_rwkv': 'grad_w', 'grad_sgu_ln_g': 'grad_w', 'grad_sgu_ln_b': 'grad_w', 'grad_sgu_w': 'grad_w', 'grad_sgu_b': 'grad_w', 'grad_w_proj_sgu': 'grad_w', 'grad_w_out': 'grad_w', 'grad_norm_ffn_g': 'grad_w', 'grad_w_ffn_gate': 'grad_w', 'grad_w_ffn_up': 'grad_w', 'grad_w_ffn_down': 'grad_w', 'grad_norm_final_g': 'grad_w', 'delta_norm_mix_g': 'delta_w', 'delta_w_in': 'delta_w', 'delta_shift_mu': 'delta_w', 'delta_w0': 'delta_w', 'delta_w_lora_up': 'delta_w', 'delta_a0': 'delta_w', 'delta_a_lora_up': 'delta_w', 'delta_g_lora_up': 'delta_w', 'delta_k_k': 'delta_w', 'delta_k_a': 'delta_w', 'delta_r_k': 'delta_w', 'delta_lnx_g': 'delta_w', 'delta_lnx_b': 'delta_w', 'delta_w_proj_rwkv': 'delta_w', 'delta_sgu_ln_g': 'delta_w', 'delta_sgu_ln_b': 'delta_w', 'delta_sgu_w': 'delta_w', 'delta_sgu_b': 'delta_w', 'delta_w_proj_sgu': 'delta_w', 'delta_w_out': 'delta_w', 'delta_norm_ffn_g': 'delta_w', 'delta_w_ffn_gate': 'delta_w', 'delta_w_ffn_up': 'delta_w', 'delta_w_ffn_down': 'delta_w', 'delta_norm_final_g': 'delta_w', 'new_m_norm_mix_g': 'new_m', 'new_m_w_in': 'new_m', 'new_m_shift_mu': 'new_m', 'new_m_w0': 'new_m', 'new_m_w_lora_up': 'new_m', 'new_m_a0': 'new_m', 'new_m_a_lora_up': 'new_m', 'new_m_g_lora_up': 'new_m', 'new_m_k_k': 'new_m', 'new_m_k_a': 'new_m', 'new_m_r_k': 'new_m', 'new_m_lnx_g': 'new_m', 'new_m_lnx_b': 'new_m', 'new_m_w_proj_rwkv': 'new_m', 'new_m_sgu_ln_g': 'new_m', 'new_m_sgu_ln_b': 'new_m', 'new_m_sgu_w': 'new_m', 'new_m_sgu_b': 'new_m', 'new_m_w_proj_sgu': 'new_m', 'new_m_w_out': 'new_m', 'new_m_norm_ffn_g': 'new_m', 'new_m_w_ffn_gate': 'new_m', 'new_m_w_ffn_up': 'new_m', 'new_m_w_ffn_down': 'new_m', 'new_m_norm_final_g': 'new_m', 'new_v_norm_mix_g': 'new_v', 'new_v_w_in': 'new_v', 'new_v_shift_mu': 'new_v', 'new_v_w0': 'new_v', 'new_v_w_lora_up': 'new_v', 'new_v_a0': 'new_v', 'new_v_a_lora_up': 'new_v', 'new_v_g_lora_up': 'new_v', 'new_v_k_k': 'new_v', 'new_v_k_a': 'new_v', 'new_v_r_k': 'new_v', 'new_v_lnx_g': 'new_v', 'new_v_lnx_b': 'new_v', 'new_v_w_proj_rwkv': 'new_v', 'new_v_sgu_ln_g': 'new_v', 'new_v_sgu_ln_b': 'new_v', 'new_v_sgu_w': 'new_v', 'new_v_sgu_b': 'new_v', 'new_v_w_proj_sgu': 'new_v', 'new_v_w_out': 'new_v', 'new_v_norm_ffn_g': 'new_v', 'new_v_w_ffn_gate': 'new_v', 'new_v_w_ffn_up': 'new_v', 'new_v_w_ffn_down': 'new_v', 'new_v_norm_final_g': 'new_v'}


def _forward(args):
    return _fwd_reference(*[args[k] for k in FWD_PARAMS])


def _output_shape():
    out = _jax.eval_shape(lambda: _forward(_fwd_setup_inputs(0)))
    return out.shape, out.dtype

N_MICROBATCH = 1
ADAM_LR = 0.001
ADAM_B1 = 0.9
ADAM_B2 = 0.999
ADAM_EPS = 1e-08
ADAM_WD = 0.01
ADAM_STEP = 10
PER_EXAMPLE_BATCH_AXIS = {'x': 0, 'loss_target': 0}
SHARED_INPUTS = []
_WEIGHT_DTYPES = {'norm_mix_g': _jnp.float32, 'w_in': _jnp.float32, 'shift_mu': _jnp.float32, 'w0': _jnp.float32, 'w_lora_up': _jnp.float32, 'a0': _jnp.float32, 'a_lora_up': _jnp.float32, 'g_lora_up': _jnp.float32, 'k_k': _jnp.float32, 'k_a': _jnp.float32, 'r_k': _jnp.float32, 'lnx_g': _jnp.float32, 'lnx_b': _jnp.float32, 'w_proj_rwkv': _jnp.float32, 'sgu_ln_g': _jnp.float32, 'sgu_ln_b': _jnp.float32, 'sgu_w': _jnp.float32, 'sgu_b': _jnp.float32, 'w_proj_sgu': _jnp.float32, 'w_out': _jnp.float32, 'norm_ffn_g': _jnp.float32, 'w_ffn_gate': _jnp.float32, 'w_ffn_up': _jnp.float32, 'w_ffn_down': _jnp.float32, 'norm_final_g': _jnp.float32}
MOMENT_SCALE = {'norm_mix_g': 4.893801e-02, 'w_in': 2.266146e-02, 'shift_mu': 4.345477e-02, 'w0': 1.418826e-02, 'w_lora_up': 2.369187e-03, 'a0': 1.062474e-02, 'a_lora_up': 8.786821e-03, 'g_lora_up': 2.654006e-02, 'k_k': 1.245091e-02, 'k_a': 3.036826e-02, 'r_k': 5.854953e-02, 'lnx_g': 2.751113e-02, 'lnx_b': 2.618347e-02, 'w_proj_rwkv': 1.844682e-02, 'sgu_ln_g': 2.175277e-02, 'sgu_ln_b': 2.314298e-02, 'sgu_w': 2.173755e-02, 'sgu_b': 3.037550e-02, 'w_proj_sgu': 2.674740e-02, 'w_out': 3.253812e-02, 'norm_ffn_g': 4.262908e-02, 'w_ffn_gate': 1.822273e-02, 'w_ffn_up': 1.761905e-02, 'w_ffn_down': 2.925248e-02, 'norm_final_g': 7.999483e+00}


def _to_microbatches(a, axis):
    t = _jnp.moveaxis(a, axis, 0)
    t = t.reshape((N_MICROBATCH, t.shape[0] // N_MICROBATCH) + t.shape[1:])
    return _jnp.moveaxis(t, 1, axis + 1)


def setup_inputs(seed: int = 0) -> dict:
    inp = _fwd_setup_inputs(seed)
    key = _jax.random.fold_in(_jax.random.key(seed), 7919)
    shape, _ = _output_shape()
    out = dict(inp)
    out["loss_target"] = _jax.random.normal(_jax.random.fold_in(key, 0), shape, _jnp.float32)
    for i, name in enumerate(TWIN_WEIGHTS):
        w = inp[name].astype(_jnp.float32)
        if MOMENT_SCALE is None:
            s = _jnp.sqrt(_jnp.mean(_jnp.square(w)) + 1e-30)
        else:
            s = MOMENT_SCALE[name]
        km, kv = _jax.random.split(_jax.random.fold_in(key, i + 1))
        out[name] = w
        out["m_" + name] = s * _jax.random.normal(km, w.shape, _jnp.float32)
        out["v_" + name] = (s * s) * _jax.random.uniform(kv, w.shape, _jnp.float32, 0.5, 1.5)
    if N_MICROBATCH > 1:
        for name, axis in PER_EXAMPLE_BATCH_AXIS.items():
            out[name] = _to_microbatches(out[name], axis)
    return {'x': out['x'], 'norm_mix_g': out['norm_mix_g'], 'w_in': out['w_in'], 'shift_mu': out['shift_mu'], 'w0': out['w0'], 'w_lora_up': out['w_lora_up'], 'a0': out['a0'], 'a_lora_up': out['a_lora_up'], 'g_lora_up': out['g_lora_up'], 'k_k': out['k_k'], 'k_a': out['k_a'], 'r_k': out['r_k'], 'lnx_g': out['lnx_g'], 'lnx_b': out['lnx_b'], 'w_proj_rwkv': out['w_proj_rwkv'], 'sgu_ln_g': out['sgu_ln_g'], 'sgu_ln_b': out['sgu_ln_b'], 'sgu_w': out['sgu_w'], 'sgu_b': out['sgu_b'], 'w_proj_sgu': out['w_proj_sgu'], 'w_out': out['w_out'], 'norm_ffn_g': out['norm_ffn_g'], 'w_ffn_gate': out['w_ffn_gate'], 'w_ffn_up': out['w_ffn_up'], 'w_ffn_down': out['w_ffn_down'], 'norm_final_g': out['norm_final_g'], 'loss_target': out['loss_target'], 'm_norm_mix_g': out['m_norm_mix_g'], 'm_w_in': out['m_w_in'], 'm_shift_mu': out['m_shift_mu'], 'm_w0': out['m_w0'], 'm_w_lora_up': out['m_w_lora_up'], 'm_a0': out['m_a0'], 'm_a_lora_up': out['m_a_lora_up'], 'm_g_lora_up': out['m_g_lora_up'], 'm_k_k': out['m_k_k'], 'm_k_a': out['m_k_a'], 'm_r_k': out['m_r_k'], 'm_lnx_g': out['m_lnx_g'], 'm_lnx_b': out['m_lnx_b'], 'm_w_proj_rwkv': out['m_w_proj_rwkv'], 'm_sgu_ln_g': out['m_sgu_ln_g'], 'm_sgu_ln_b': out['m_sgu_ln_b'], 'm_sgu_w': out['m_sgu_w'], 'm_sgu_b': out['m_sgu_b'], 'm_w_proj_sgu': out['m_w_proj_sgu'], 'm_w_out': out['m_w_out'], 'm_norm_ffn_g': out['m_norm_ffn_g'], 'm_w_ffn_gate': out['m_w_ffn_gate'], 'm_w_ffn_up': out['m_w_ffn_up'], 'm_w_ffn_down': out['m_w_ffn_down'], 'm_norm_final_g': out['m_norm_final_g'], 'v_norm_mix_g': out['v_norm_mix_g'], 'v_w_in': out['v_w_in'], 'v_shift_mu': out['v_shift_mu'], 'v_w0': out['v_w0'], 'v_w_lora_up': out['v_w_lora_up'], 'v_a0': out['v_a0'], 'v_a_lora_up': out['v_a_lora_up'], 'v_g_lora_up': out['v_g_lora_up'], 'v_k_k': out['v_k_k'], 'v_k_a': out['v_k_a'], 'v_r_k': out['v_r_k'], 'v_lnx_g': out['v_lnx_g'], 'v_lnx_b': out['v_lnx_b'], 'v_w_proj_rwkv': out['v_w_proj_rwkv'], 'v_sgu_ln_g': out['v_sgu_ln_g'], 'v_sgu_ln_b': out['v_sgu_ln_b'], 'v_sgu_w': out['v_sgu_w'], 'v_sgu_b': out['v_sgu_b'], 'v_w_proj_sgu': out['v_w_proj_sgu'], 'v_w_out': out['v_w_out'], 'v_norm_ffn_g': out['v_norm_ffn_g'], 'v_w_ffn_gate': out['v_w_ffn_gate'], 'v_w_ffn_up': out['v_w_ffn_up'], 'v_w_ffn_down': out['v_w_ffn_down'], 'v_norm_final_g': out['v_norm_final_g']}


def _loss(weights, diff, rest, loss_target):
    with _jax.named_scope("forward"):
        args = {**rest, TWIN_DIFF_INPUT: diff, **{k: w.astype(_WEIGHT_DTYPES[k]) for k, w in weights.items()}}
        y = _forward(args)
    with _jax.named_scope("loss_head"):
        err = _jnp.square(y.astype(_jnp.float32) - loss_target)
        return 0.5 * _jnp.sum(_jnp.mean(err, axis=-1)) if err.ndim else 0.5 * err


def _adamw(w, g, m, v):
    m = ADAM_B1 * m + (1.0 - ADAM_B1) * g
    v = ADAM_B2 * v + (1.0 - ADAM_B2) * _jnp.square(g)
    m_hat = m / (1.0 - ADAM_B1 ** ADAM_STEP)
    v_hat = v / (1.0 - ADAM_B2 ** ADAM_STEP)
    delta = -ADAM_LR * (m_hat / (_jnp.sqrt(v_hat) + ADAM_EPS) + ADAM_WD * w)
    return delta, m, v


def reference(x, norm_mix_g, w_in, shift_mu, w0, w_lora_up, a0, a_lora_up, g_lora_up, k_k, k_a, r_k, lnx_g, lnx_b, w_proj_rwkv, sgu_ln_g, sgu_ln_b, sgu_w, sgu_b, w_proj_sgu, w_out, norm_ffn_g, w_ffn_gate, w_ffn_up, w_ffn_down, norm_final_g, loss_target, m_norm_mix_g, m_w_in, m_shift_mu, m_w0, m_w_lora_up, m_a0, m_a_lora_up, m_g_lora_up, m_k_k, m_k_a, m_r_k, m_lnx_g, m_lnx_b, m_w_proj_rwkv, m_sgu_ln_g, m_sgu_ln_b, m_sgu_w, m_sgu_b, m_w_proj_sgu, m_w_out, m_norm_ffn_g, m_w_ffn_gate, m_w_ffn_up, m_w_ffn_down, m_norm_final_g, v_norm_mix_g, v_w_in, v_shift_mu, v_w0, v_w_lora_up, v_a0, v_a_lora_up, v_g_lora_up, v_k_k, v_k_a, v_r_k, v_lnx_g, v_lnx_b, v_w_proj_rwkv, v_sgu_ln_g, v_sgu_ln_b, v_sgu_w, v_sgu_b, v_w_proj_sgu, v_w_out, v_norm_ffn_g, v_w_ffn_gate, v_w_ffn_up, v_w_ffn_down, v_norm_final_g):
    given = dict(x=x, norm_mix_g=norm_mix_g, w_in=w_in, shift_mu=shift_mu, w0=w0, w_lora_up=w_lora_up, a0=a0, a_lora_up=a_lora_up, g_lora_up=g_lora_up, k_k=k_k, k_a=k_a, r_k=r_k, lnx_g=lnx_g, lnx_b=lnx_b, w_proj_rwkv=w_proj_rwkv, sgu_ln_g=sgu_ln_g, sgu_ln_b=sgu_ln_b, sgu_w=sgu_w, sgu_b=sgu_b, w_proj_sgu=w_proj_sgu, w_out=w_out, norm_ffn_g=norm_ffn_g, w_ffn_gate=w_ffn_gate, w_ffn_up=w_ffn_up, w_ffn_down=w_ffn_down, norm_final_g=norm_final_g, loss_target=loss_target, m_norm_mix_g=m_norm_mix_g, m_w_in=m_w_in, m_shift_mu=m_shift_mu, m_w0=m_w0, m_w_lora_up=m_w_lora_up, m_a0=m_a0, m_a_lora_up=m_a_lora_up, m_g_lora_up=m_g_lora_up, m_k_k=m_k_k, m_k_a=m_k_a, m_r_k=m_r_k, m_lnx_g=m_lnx_g, m_lnx_b=m_lnx_b, m_w_proj_rwkv=m_w_proj_rwkv, m_sgu_ln_g=m_sgu_ln_g, m_sgu_ln_b=m_sgu_ln_b, m_sgu_w=m_sgu_w, m_sgu_b=m_sgu_b, m_w_proj_sgu=m_w_proj_sgu, m_w_out=m_w_out, m_norm_ffn_g=m_norm_ffn_g, m_w_ffn_gate=m_w_ffn_gate, m_w_ffn_up=m_w_ffn_up, m_w_ffn_down=m_w_ffn_down, m_norm_final_g=m_norm_final_g, v_norm_mix_g=v_norm_mix_g, v_w_in=v_w_in, v_shift_mu=v_shift_mu, v_w0=v_w0, v_w_lora_up=v_w_lora_up, v_a0=v_a0, v_a_lora_up=v_a_lora_up, v_g_lora_up=v_g_lora_up, v_k_k=v_k_k, v_k_a=v_k_a, v_r_k=v_r_k, v_lnx_g=v_lnx_g, v_lnx_b=v_lnx_b, v_w_proj_rwkv=v_w_proj_rwkv, v_sgu_ln_g=v_sgu_ln_g, v_sgu_ln_b=v_sgu_ln_b, v_sgu_w=v_sgu_w, v_sgu_b=v_sgu_b, v_w_proj_sgu=v_w_proj_sgu, v_w_out=v_w_out, v_norm_ffn_g=v_norm_ffn_g, v_w_ffn_gate=v_w_ffn_gate, v_w_ffn_up=v_w_ffn_up, v_w_ffn_down=v_w_ffn_down, v_norm_final_g=v_norm_final_g)
    weights = {n: given[n] for n in TWIN_WEIGHTS}
    shared = {n: given[n] for n in SHARED_INPUTS}
    per_example = {n: given[n] for n in ['x']}
    grad_fn = _jax.value_and_grad(_loss, argnums=(0, 1))

    def one_microbatch(ex, loss_target):
        ex = dict(ex)
        diff = ex.pop(TWIN_DIFF_INPUT)
        return grad_fn(weights, diff, {**shared, **ex}, loss_target)

    if N_MICROBATCH == 1:
        loss, (grad_w, grad_x) = one_microbatch(per_example, given["loss_target"])
    else:
        def body(carry, xs):
            loss_sum, grad_sum = carry
            l_k, (gw_k, gx_k) = one_microbatch(xs[0], xs[1])
            with _jax.named_scope("update"):
                return (loss_sum + l_k, _jax.tree.map(_jnp.add, grad_sum, gw_k)), gx_k

        init = (_jnp.zeros((), _jnp.float32), _jax.tree.map(_jnp.zeros_like, weights))
        (loss, grad_w), grad_x = _jax.lax.scan(body, init, (per_example, given["loss_target"]))
    with _jax.named_scope("update"):
        delta_w, new_m, new_v = {}, {}, {}
        for n in TWIN_WEIGHTS:
            delta_w[n], new_m[n], new_v[n] = _adamw(weights[n], grad_w[n], given["m_" + n], given["v_" + n])
    return (loss, grad_x, *[grad_w[n] for n in TWIN_WEIGHTS], *[delta_w[n] for n in TWIN_WEIGHTS],
            *[new_m[n] for n in TWIN_WEIGHTS], *[new_v[n] for n in TWIN_WEIGHTS])
```

```python
import jax
import jax.numpy as jnp
from jax import lax
from jax.experimental import pallas as pl
from jax.experimental.pallas import tpu as pltpu

F32 = jnp.float32
BF16 = jnp.bfloat16

N_DEV = 8
LANE = 128
SUBLANE = 8
HEAD = 64
SGU_CHUNK = 128
SGU_GROUP = 128
WKV_CHUNK = 64
RMS_EPS = 1e-6
LN_EPS = 1e-5
LNX_EPS = 64e-5
ADAM_LR, ADAM_B1, ADAM_B2, ADAM_EPS, ADAM_WD, ADAM_STEP = 0.001, 0.9, 0.999, 1e-08, 0.01, 10
VMEM_LIMIT_BYTES = 48 * 1024 * 1024
_SQRT_HALF = 0.7071067811865476
_INV_SQRT_2PI = 0.3989422804014327
_HI = lax.Precision.HIGHEST


def _pick(n, cands):
    for c in cands:
        if n % c == 0:
            return c
    return n


def _ceil_to(n, m):
    return -(-n // m) * m


def _params():
    return pltpu.CompilerParams(vmem_limit_bytes=VMEM_LIMIT_BYTES)


def _matmul(a, b, *, mode, out_dtype, name, add=None):
    if mode == "nn":
        (M, K), (K2, N) = a.shape, b.shape
    elif mode == "nt":
        (M, K), (N, K2) = a.shape, b.shape
    else:
        (K, M), (K2, N) = a.shape, b.shape
    assert K == K2, (a.shape, b.shape, mode)
    tm = _pick(M, (512, 256, 128))
    tn = _pick(N, (512, 256, 128))
    tk = K if K <= 2048 else _pick(K, (2048, 1024, 512, 256, 128))
    nk = K // tk
    dn = {"nn": (((1,), (0,)), ((), ())), "nt": (((1,), (1,)), ((), ())), "tn": (((0,), (0,)), ((), ()))}[mode]
    a_spec = pl.BlockSpec((tk, tm), lambda i, j, k: (k, i)) if mode == "tn" else pl.BlockSpec((tm, tk), lambda i, j, k: (i, k))
    b_spec = pl.BlockSpec((tn, tk), lambda i, j, k: (j, k)) if mode == "nt" else pl.BlockSpec((tk, tn), lambda i, j, k: (k, j))
    o_spec = pl.BlockSpec((tm, tn), lambda i, j, k: (i, j))
    has_add = add is not None

    def body(*refs):
        a_ref, b_ref = refs[0], refs[1]
        add_ref = refs[2] if has_add else None
        o_ref = refs[3] if has_add else refs[2]
        part = lax.dot_general(a_ref[...].astype(BF16), b_ref[...].astype(BF16), dn, preferred_element_type=F32)
        if nk == 1:
            if has_add:
                part = part + add_ref[...]
            o_ref[...] = part.astype(out_dtype)
            return
        acc_ref = refs[-1]
        kk = pl.program_id(2)

        @pl.when(kk == 0)
        def _():
            acc_ref[...] = part + add_ref[...] if has_add else part

        @pl.when(kk > 0)
        def _():
            acc_ref[...] += part

        @pl.when(kk == nk - 1)
        def _():
            o_ref[...] = acc_ref[...].astype(out_dtype)

    ins = [a, b] + ([add] if has_add else [])
    in_specs = [a_spec, b_spec] + ([o_spec] if has_add else [])
    return pl.pallas_call(
        body, name=name, grid=(M // tm, N // tn, nk), in_specs=in_specs, out_specs=o_spec,
        out_shape=jax.ShapeDtypeStruct((M, N), out_dtype),
        scratch_shapes=[pltpu.VMEM((tm, tn), F32)] if nk > 1 else [],
        compiler_params=_params())(*ins)


def _rowwise(fn, rows, pars, row_outs, acc_outs, *, name, tm=256, hpars=(), hacc_outs=(), heads=1):
    R = rows[0].shape[0]
    rph = R // heads
    if max(r.shape[1] for r in rows) > 4096:
        tm = tm // 2
    tm = min(tm, rph)
    assert rph % tm == 0 and R % heads == 0
    tph = rph // tm
    nr, npar, nh = len(rows), len(pars), len(hpars)
    nro, nao = len(row_outs), len(acc_outs)

    def body(*refs):
        rv = [r[...] for r in refs[:nr]]
        pv = [p[...] for p in refs[nr:nr + npar]]
        hv = [h[0] for h in refs[nr + npar:nr + npar + nh]]
        outs = refs[nr + npar + nh:]
        ro, ao, ho = fn(rv, pv, hv)
        i = pl.program_id(0)
        for o_ref, val in zip(outs[:nro], ro):
            o_ref[...] = val.astype(o_ref.dtype)

        def accumulate(o_ref, val, first):
            @pl.when(first)
            def _():
                o_ref[...] = val

            @pl.when(jnp.logical_not(first))
            def _():
                o_ref[...] += val

        for o_ref, val in zip(outs[nro:nro + nao], ao):
            accumulate(o_ref, val, i == 0)
        for o_ref, val in zip(outs[nro + nao:], ho):
            accumulate(o_ref, val[None], i % tph == 0)

    in_specs = ([pl.BlockSpec((tm, r.shape[1]), lambda i: (i, 0)) for r in rows]
                + [pl.BlockSpec(p.shape, lambda i, nd=p.ndim: (0,) * nd) for p in pars]
                + [pl.BlockSpec((1, 1, h.shape[2]), lambda i: (i // tph, 0, 0)) for h in hpars])
    out_shape = ([jax.ShapeDtypeStruct((R, f), dt) for f, dt in row_outs]
                 + [jax.ShapeDtypeStruct(s, F32) for s in acc_outs]
                 + [jax.ShapeDtypeStruct((heads, 1, f), F32) for f in hacc_outs])
    out_specs = ([pl.BlockSpec((tm, f), lambda i: (i, 0)) for f, _ in row_outs]
                 + [pl.BlockSpec(s, lambda i, nd=len(s): (0,) * nd) for s in acc_outs]
                 + [pl.BlockSpec((1, 1, f), lambda i: (i // tph, 0, 0)) for f in hacc_outs])
    res = pl.pallas_call(body, name=name, grid=(R // tm,), in_specs=in_specs, out_specs=out_specs, out_shape=out_shape,
                         compiler_params=_params())(*rows, *pars, *hpars)
    return list(res)


def _bdot(a, b, mode="nn"):
    dn = {"nn": (((1,), (0,)), ((), ())), "nt": (((1,), (1,)), ((), ())), "tn": (((0,), (0,)), ((), ()))}[mode]
    return lax.dot_general(a.astype(BF16), b.astype(BF16), dn, preferred_element_type=F32)


def _sigmoid(x):
    return jax.nn.sigmoid(x)


def _softplus(x):
    return jnp.maximum(x, 0.0) + jnp.log1p(jnp.exp(-jnp.abs(x)))


def _gelu(z):
    return 0.5 * z * (1.0 + lax.erf(z * _SQRT_HALF))


def _gelu_grad(z):
    return 0.5 * (1.0 + lax.erf(z * _SQRT_HALF)) + z * jnp.exp(-0.5 * z * z) * _INV_SQRT_2PI


def _mean(x):
    return jnp.mean(x, axis=-1, keepdims=True)


def _colsum(x):
    return jnp.sum(x, axis=0, keepdims=True)


def _rms_fwd(x, g, name):
    def fn(rv, pv, hv):
        (xv,), (gv,) = rv, pv
        r = lax.rsqrt(_mean(xv * xv) + RMS_EPS)
        return [xv * r * gv], [], []
    return _rowwise(fn, [x], [g], [(x.shape[1], BF16)], [], name=name)[0]


def _rms_bwd(dn, x, dres, g, name):
    def fn(rv, pv, hv):
        (dnv, xv, drv), (gv,) = rv, pv
        r = lax.rsqrt(_mean(xv * xv) + RMS_EPS)
        yn = xv * r
        dyg = dnv * gv
        dx = r * (dyg - yn * _mean(dyg * yn))
        return [drv + dx], [_colsum(dnv * yn)], []
    D = x.shape[1]
    return _rowwise(fn, [dn, x, dres], [g], [(D, F32)], [(1, D)], name=name)


def _rwkv_layout(RW, Lw, La, Lg):
    widths = [RW, RW, RW, Lw, La, Lg]
    pw = [_ceil_to(w, LANE) for w in widths]
    offs = [sum(pw[:i]) for i in range(6)]
    return widths, pw, offs, sum(pw)


def _pad_rwkv_cols(a, lay):
    widths, pw, _, _ = lay
    pieces, src = [], 0
    for w, p in zip(widths, pw):
        pieces.append(a[:, src:src + w])
        if p > w:
            pieces.append(jnp.zeros((a.shape[0], p - w), a.dtype))
        src += w
    pieces.append(a[:, src:])
    return jnp.concatenate(pieces, axis=1)


def _unpad_rwkv_cols(a, lay):
    widths, _, offs, rcp = lay
    return jnp.concatenate([a[:, o:o + w] for o, w in zip(offs, widths)] + [a[:, rcp:]], axis=1)


def _pad_rows(a, rows):
    return a if a.shape[0] == rows else jnp.concatenate([a, jnp.zeros((rows - a.shape[0], a.shape[1]), a.dtype)], axis=0)


def _token_shift(p, halo, mu, i):
    tm = p.shape[0]
    hid = lax.broadcasted_iota(jnp.int32, (SUBLANE, 1), 0)
    before = jnp.sum(jnp.where(hid == SUBLANE - 1, halo, 0.0), axis=0, keepdims=True)
    before = jnp.where(i == 0, 0.0, before)
    rid = lax.broadcasted_iota(jnp.int32, (tm, 1), 0)
    prev = jnp.where(rid == 0, before, pltpu.roll(p, 1, 0))
    d = prev - p
    return p + d * mu, d


def _rwkv_math(ps, w0, a0, k_k, k_a, wlw, wla, wlg, lay):
    _, pw, offs, _ = lay
    r, k, v, xw, xa, xg = (ps[:, offs[j]:offs[j] + pw[j]] for j in range(6))
    tw = jnp.tanh(xw)
    ww = w0 + _bdot(tw, wlw)
    lw = -jnp.exp(-_softplus(-ww) - 0.5)
    a = _sigmoid(a0 + _bdot(xa, wla))
    sg = _sigmoid(xg)
    g = _bdot(sg, wlg)
    return dict(r=r, k=k, v=v, xa=xa, tw=tw, ww=ww, lw=lw, a=a, sg=sg, g=g, kkp=k * k_k, k2=k * (1.0 + (a - 1.0) * k_a))


def _halo_specs(T, tm, width, after):
    hb = tm // SUBLANE
    last = T // SUBLANE - 1
    if after:
        return pl.BlockSpec((SUBLANE, width), lambda i: (jnp.minimum((i + 1) * hb, last), 0))
    return pl.BlockSpec((SUBLANE, width), lambda i: (jnp.maximum(i * hb - 1, 0), 0))


def _rwkv_pre(p, mu, small, lora, lay, name):
    T, rcp = p.shape
    RW = lay[0][0]
    tm = min(256, T)

    def body(p_ref, ph_ref, mu_ref, w0_ref, a0_ref, kk_ref, ka_ref, wlw_ref, wla_ref, wlg_ref, *outs):
        ps, _ = _token_shift(p_ref[...], ph_ref[...], mu_ref[...], pl.program_id(0))
        q = _rwkv_math(ps, w0_ref[...], a0_ref[...], kk_ref[...], ka_ref[...], wlw_ref[...], wla_ref[...], wlg_ref[...], lay)
        for o_ref, key in zip(outs, ("r", "lw", "k2", "v", "kkp", "a", "g")):
            o_ref[...] = q[key]

    whole = lambda arr: pl.BlockSpec(arr.shape, lambda i: (0, 0))
    row = lambda w: pl.BlockSpec((tm, w), lambda i: (i, 0))
    return pl.pallas_call(
        body, name=name, grid=(T // tm,),
        in_specs=[row(rcp), _halo_specs(T, tm, rcp, False), whole(mu)] + [whole(s) for s in small] + [whole(w) for w in lora],
        out_specs=[row(RW)] * 7, out_shape=[jax.ShapeDtypeStruct((T, RW), F32)] * 7,
        compiler_params=_params())(p, p, mu, *small, *lora)


def _rwkv_pre_bwd(p, mu, small, lora, grads, lay, name):
    T, rcp = p.shape
    widths, pw, offs, _ = lay
    RW = widths[0]
    tm = min(256, T)

    def body(p_ref, ph_ref, mu_ref, w0_ref, a0_ref, kk_ref, ka_ref, wlw_ref, wla_ref, wlg_ref,
             dr_ref, dlw_ref, dk2_ref, dv_ref, dkkp_ref, da_ref, dg_ref,
             dps_ref, dmu_ref, dw0_ref, da0_ref, dkk_ref, dka_ref, dwlw_ref, dwla_ref, dwlg_ref):
        i = pl.program_id(0)
        ps, dprev = _token_shift(p_ref[...], ph_ref[...], mu_ref[...], i)
        k_k, k_a = kk_ref[...], ka_ref[...]
        q = _rwkv_math(ps, w0_ref[...], a0_ref[...], k_k, k_a, wlw_ref[...], wla_ref[...], wlg_ref[...], lay)
        k, a, lw, ww, tw, sg = q["k"], q["a"], q["lw"], q["ww"], q["tw"], q["sg"]
        dk2, dkkp, dg = dk2_ref[...], dkkp_ref[...], dg_ref[...]
        dk = dk2 * (1.0 + (a - 1.0) * k_a) + dkkp * k_k
        da = da_ref[...] + dk2 * k * k_a
        dpa = da * a * (1.0 - a)
        dww = dlw_ref[...] * lw * _sigmoid(-ww)
        dxa = _bdot(dpa, wla_ref[...], "nt")
        dxw = _bdot(dww, wlw_ref[...], "nt") * (1.0 - tw * tw)
        dxg = _bdot(dg, wlg_ref[...], "nt") * sg * (1.0 - sg)
        segs = (dr_ref[...], dk, dv_ref[...], dxw, dxa, dxg)
        sums = [dmu_ref, dw0_ref, da0_ref, dkk_ref, dka_ref, dwlw_ref, dwla_ref, dwlg_ref]

        @pl.when(i == 0)
        def _():
            for s in sums:
                s[...] = jnp.zeros_like(s)

        for j, seg in enumerate(segs):
            sl = slice(offs[j], offs[j] + pw[j])
            dps_ref[:, sl] = seg
            dmu_ref[:, sl] += _colsum(seg * dprev[:, sl])
        dw0_ref[...] += _colsum(dww)
        da0_ref[...] += _colsum(dpa)
        dkk_ref[...] += _colsum(dkkp * k)
        dka_ref[...] += _colsum(dk2 * k * (a - 1.0))
        dwlw_ref[...] += _bdot(tw, dww, "tn")
        dwla_ref[...] += _bdot(q["xa"], dpa, "tn")
        dwlg_ref[...] += _bdot(sg, dg, "tn")

    whole = lambda arr: pl.BlockSpec(arr.shape, lambda i: (0, 0))
    row = lambda w: pl.BlockSpec((tm, w), lambda i: (i, 0))
    acc_shapes = [(1, rcp), (1, RW), (1, RW), (1, RW), (1, RW)] + [w.shape for w in lora]
    return pl.pallas_call(
        body, name=name, grid=(T // tm,),
        in_specs=[row(rcp), _halo_specs(T, tm, rcp, False), whole(mu)] + [whole(s) for s in small] + [whole(w) for w in lora] + [row(RW)] * 7,
        out_specs=[row(rcp)] + [pl.BlockSpec(s, lambda i: (0, 0)) for s in acc_shapes],
        out_shape=[jax.ShapeDtypeStruct((T, rcp), F32)] + [jax.ShapeDtypeStruct(s, F32) for s in acc_shapes],
        compiler_params=_params())(p, p, mu, *small, *lora, *grads)


def _shift_bwd(dps, mu, name):
    T, rcp = dps.shape
    tm = min(256, T)
    nt = T // tm

    def body(d_ref, dh_ref, mu_ref, o_ref):
        i = pl.program_id(0)
        d = d_ref[...]
        hid = lax.broadcasted_iota(jnp.int32, (SUBLANE, 1), 0)
        after = jnp.sum(jnp.where(hid == 0, dh_ref[...], 0.0), axis=0, keepdims=True)
        after = jnp.where(i == nt - 1, 0.0, after)
        rid = lax.broadcasted_iota(jnp.int32, (tm, 1), 0)
        nxt = jnp.where(rid == tm - 1, after, pltpu.roll(d, tm - 1, 0))
        mu_v = mu_ref[...]
        o_ref[...] = (d * (1.0 - mu_v) + nxt * mu_v).astype(BF16)

    row = pl.BlockSpec((tm, rcp), lambda i: (i, 0))
    return pl.pallas_call(
        body, name=name, grid=(nt,), in_specs=[row, _halo_specs(T, tm, rcp, True), pl.BlockSpec(mu.shape, lambda i: (0, 0))],
        out_specs=row, out_shape=jax.ShapeDtypeStruct((T, rcp), BF16), compiler_params=_params())(dps, dps, mu)


def _to_heads(x, H):
    T = x.shape[0]
    return x.reshape(T, H, HEAD).transpose(1, 0, 2).reshape(H * T, HEAD)


def _from_heads(x, H):
    T = x.shape[0] // H
    return x.reshape(H, T, HEAD).transpose(1, 0, 2).reshape(T, H * HEAD)


def _rowsum(x):
    return jnp.sum(x, axis=-1, keepdims=True)


def _kk_math(kkp):
    nrm = jnp.sqrt(_rowsum(kkp * kkp))
    inv = 1.0 / jnp.maximum(nrm, 1e-12)
    return nrm, inv, kkp * inv


def _head_pre(kkp, a, name):
    def fn(rv, pv, hv):
        kkp_v, a_v = rv
        _, _, kk = _kk_math(kkp_v)
        return [-kk, kk * a_v], [], []
    return _rowwise(fn, [kkp, a], [], [(HEAD, F32)] * 2, [], name=name, tm=2048)


def _head_pre_bwd(kkp, a, daa, dbb, pairs, name):
    def fn(rv, pv, hv):
        kkp_v, a_v, daa_v, dbb_v = rv[:4]
        nrm, inv, kk = _kk_math(kkp_v)
        dkk = dbb_v * a_v - daa_v
        dkkp = jnp.where(nrm > 1e-12, inv * (dkk - kk * _rowsum(dkk * kk)), dkk * inv)
        return [dkkp, dbb_v * kk, rv[4] + rv[5], rv[6] + rv[7], rv[8] + rv[9]], [], []
    return _rowwise(fn, [kkp, a, daa, dbb] + list(pairs), [], [(HEAD, F32)] * 5, [], name=name, tm=1024)


def _head_post_math(y, r, k2, v, lg, lb, rk):
    yc = y - _mean(y)
    rstd = lax.rsqrt(_mean(yc * yc) + LNX_EPS)
    yn = yc * rstd
    s = _rowsum(r * k2 * rk)
    return yn, rstd, yn * lg + lb + s * v, s


def _head_post(y, r, k2, v, g, hp, H, name):
    def fn(rv, pv, hv):
        yv, r_v, k_v, v_v, g_v = rv
        _, _, t, _ = _head_post_math(yv, r_v, k_v, v_v, *hv)
        return [t * g_v], [], []
    return _rowwise(fn, [y, r, k2, v, g], [], [(HEAD, BF16)], [], name=name, tm=1024, hpars=hp, heads=H)[0]


def _head_post_bwd(dya, y, r, k2, v, g, hp, H, name):
    def fn(rv, pv, hv):
        d_v, yv, r_v, k_v, v_v, g_v = rv
        lg, lb, rk = hv
        yn, rstd, t, s = _head_post_math(yv, r_v, k_v, v_v, lg, lb, rk)
        dyo = d_v * g_v
        dyn = dyo * lg
        dy = rstd * (dyn - _mean(dyn) - yn * _mean(dyn * yn))
        ds = _rowsum(dyo * v_v)
        return ([dy, ds * k_v * rk, ds * r_v * rk, dyo * s, d_v * t], [],
                [_colsum(dyo * yn), _colsum(dyo), _colsum(ds * r_v * k_v)])
    return _rowwise(fn, [dya, y, r, k2, v, g], [], [(HEAD, F32)] * 5, [], name=name, tm=1024, hpars=hp, hacc_outs=[HEAD] * 3, heads=H)


def _bmm(x, y, mode):
    dn = {"nn": (((2,), (1,)), ((0,), (0,))), "nt": (((2,), (2,)), ((0,), (0,))), "tn": (((1,), (1,)), ((0,), (0,)))}[mode]
    return lax.dot_general(x, y, dn, precision=_HI, preferred_element_type=F32)


def _wkv_chunk(r, lw, k, v, a, b):
    hb, C, _ = r.shape
    ti = lax.broadcasted_iota(jnp.int32, (C, C), 0)
    si = lax.broadcasted_iota(jnp.int32, (C, C), 1)
    linc, lstr, eye = (ti >= si).astype(F32), (ti > si).astype(F32), (ti == si).astype(F32)
    lincb = jnp.broadcast_to(linc, (hb, C, C))
    lstrb = jnp.broadcast_to(lstr, (hb, C, C))
    ci = _bmm(lincb, lw, "nn")
    cC = jnp.sum(lw, axis=1, keepdims=True)
    gi, ge, gn, gr = jnp.exp(ci), jnp.exp(ci - lw), jnp.exp(-ci), jnp.exp(cC - ci)
    At, Rt, Bt, Kt, Bh, Kh = a * ge, r * gi, b * gn, k * gn, b * gr, k * gr
    A_ab = _bmm(At, Bt, "nt") * lstr
    A_ak = _bmm(At, Kt, "nt") * lstr
    A_rb = _bmm(Rt, Bt, "nt") * linc
    A_rk = _bmm(Rt, Kt, "nt") * linc
    Tm = eye + A_ab
    Pw = A_ab
    n = 1
    while 2 * n < C:
        Pw = _bmm(Pw, Pw, "nn")
        Tm = Tm + _bmm(Tm, Pw, "nn")
        n *= 2
    ones = jnp.ones_like(v)
    gC = jnp.exp(_bmm(lw, ones, "tn"))
    return dict(gi=gi, ge=ge, gn=gn, gr=gr, At=At, Rt=Rt, Bt=Bt, Kt=Kt, Bh=Bh, Kh=Kh, A_ab=A_ab, A_ak=A_ak, A_rb=A_rb,
                A_rk=A_rk, Tm=Tm, linc=linc, lstr=lstr, lincb=lincb, lstrb=lstrb, gC=gC, ones=ones)


def _wkv_fwd(r, lw, k, v, a, b, name):
    H, T, N = r.shape
    C = min(WKV_CHUNK, T)
    nc = T // C
    hb = _pick(H, (8, 4, 2))

    def body(r_ref, lw_ref, k_ref, v_ref, a_ref, b_ref, y_ref, st_ref, h_ref):
        @pl.when(pl.program_id(1) == 0)
        def _():
            h_ref[...] = jnp.zeros_like(h_ref)

        H0 = h_ref[...]
        st_ref[0] = H0
        vv = v_ref[...]
        q = _wkv_chunk(r_ref[...], lw_ref[...], k_ref[...], vv, a_ref[...], b_ref[...])
        U = _bmm(q["Tm"], _bmm(q["At"], H0, "nn") + _bmm(q["A_ak"], vv, "nn"), "nn")
        y_ref[...] = _bmm(q["Rt"], H0, "nn") + _bmm(q["A_rb"], U, "nn") + _bmm(q["A_rk"], vv, "nn")
        h_ref[...] = q["gC"] * H0 + _bmm(q["Bh"], U, "tn") + _bmm(q["Kh"], vv, "tn")

    blk = pl.BlockSpec((hb, C, N), lambda h, c: (h, c, 0))
    return pl.pallas_call(
        body, name=name, grid=(H // hb, nc), in_specs=[blk] * 6,
        out_specs=[blk, pl.BlockSpec((1, hb, N, N), lambda h, c: (c, h, 0, 0))],
        out_shape=[jax.ShapeDtypeStruct((H, T, N), F32), jax.ShapeDtypeStruct((nc, H, N, N), F32)],
        scratch_shapes=[pltpu.VMEM((hb, N, N), F32)], compiler_params=_params())(r, lw, k, v, a, b)


def _wkv_bwd(r, lw, k, v, a, b, states, dy, name):
    H, T, N = r.shape
    C = min(WKV_CHUNK, T)
    nc = T // C
    hb = _pick(H, (8, 4, 2))

    def body(r_ref, lw_ref, k_ref, v_ref, a_ref, b_ref, st_ref, dy_ref, dr_ref, dlw_ref, dk_ref, dv_ref, da_ref, db_ref, dh_ref):
        @pl.when(pl.program_id(1) == 0)
        def _():
            dh_ref[...] = jnp.zeros_like(dh_ref)

        dHC = dh_ref[...]
        H0 = st_ref[0]
        vv, dY = v_ref[...], dy_ref[...]
        q = _wkv_chunk(r_ref[...], lw_ref[...], k_ref[...], vv, a_ref[...], b_ref[...])
        At, Rt, Bt, Kt, Bh, Kh, gC = q["At"], q["Rt"], q["Bt"], q["Kt"], q["Bh"], q["Kh"], q["gC"]
        U = _bmm(q["Tm"], _bmm(At, H0, "nn") + _bmm(q["A_ak"], vv, "nn"), "nn")
        dU = _bmm(q["A_rb"], dY, "tn") + _bmm(Bh, dHC, "nn")
        dP = _bmm(q["Tm"], dU, "tn")
        dv_ref[...] = _bmm(q["A_rk"], dY, "tn") + _bmm(Kh, dHC, "nn") + _bmm(q["A_ak"], dP, "tn")
        dh_ref[...] = _bmm(Rt, dY, "tn") + gC * dHC + _bmm(At, dP, "tn")
        dA_rb = _bmm(dY, U, "nt") * q["linc"]
        dA_rk = _bmm(dY, vv, "nt") * q["linc"]
        dA_ab = _bmm(dP, U, "nt") * q["lstr"]
        dA_ak = _bmm(dP, vv, "nt") * q["lstr"]
        dRt = _bmm(dY, H0, "nt") + _bmm(dA_rb, Bt, "nn") + _bmm(dA_rk, Kt, "nn")
        dAt = _bmm(dP, H0, "nt") + _bmm(dA_ab, Bt, "nn") + _bmm(dA_ak, Kt, "nn")
        dBt = _bmm(dA_ab, At, "tn") + _bmm(dA_rb, Rt, "tn")
        dKt = _bmm(dA_ak, At, "tn") + _bmm(dA_rk, Rt, "tn")
        dBh = _bmm(U, dHC, "nt")
        dKh = _bmm(vv, dHC, "nt")
        dr_ref[...] = dRt * q["gi"]
        da_ref[...] = dAt * q["ge"]
        db_ref[...] = dBt * q["gn"] + dBh * q["gr"]
        dk_ref[...] = dKt * q["gn"] + dKh * q["gr"]
        tail = dBh * Bh + dKh * Kh
        dci = dRt * Rt - dBt * Bt - dKt * Kt - tail
        dcC = jnp.sum(tail, axis=1, keepdims=True) + _bmm(q["ones"], H0 * dHC * gC, "nt")
        dlw_ref[...] = _bmm(q["lincb"], dci, "tn") + _bmm(q["lstrb"], dAt * At, "tn") + dcC

    blk = pl.BlockSpec((hb, C, N), lambda h, c: (h, nc - 1 - c, 0))
    st = pl.BlockSpec((1, hb, N, N), lambda h, c: (nc - 1 - c, h, 0, 0))
    return pl.pallas_call(
        body, name=name, grid=(H // hb, nc), in_specs=[blk] * 6 + [st, blk], out_specs=[blk] * 6,
        out_shape=[jax.ShapeDtypeStruct((H, T, N), F32)] * 6,
        scratch_shapes=[pltpu.VMEM((hb, N, N), F32)], compiler_params=_params())(r, lw, k, v, a, b, states, dy)


def _sgu_ln(z, SW, lng, lnb):
    ge = _gelu(z)
    u, vv = ge[:, :SW], ge[:, SW:]
    xc = vv - _mean(vv)
    rstd = lax.rsqrt(_mean(xc * xc) + LN_EPS)
    vn = xc * rstd
    return u, vn, rstd, vn * lng + lnb


def _causal(ws_ref, g):
    ti = lax.broadcasted_iota(jnp.int32, (SGU_CHUNK, SGU_CHUNK), 0)
    si = lax.broadcasted_iota(jnp.int32, (SGU_CHUNK, SGU_CHUNK), 1)
    return ti >= si, jnp.where(ti >= si, ws_ref[g], 0.0).astype(BF16)


def _sgu_fwd(z, lng, lnb, ws, bexp, name):
    T, SW = z.shape[0], z.shape[1] // 2
    G = ws.shape[0]
    tr = min(256, T)
    nch = tr // SGU_CHUNK

    def body(z_ref, lng_ref, lnb_ref, ws_ref, be_ref, o_ref):
        u, _, _, vl = _sgu_ln(z_ref[...], SW, lng_ref[...], lnb_ref[...])
        for g in range(G):
            cs = slice(g * SGU_GROUP, (g + 1) * SGU_GROUP)
            _, wc = _causal(ws_ref, g)
            for n in range(nch):
                rs = slice(n * SGU_CHUNK, (n + 1) * SGU_CHUNK)
                m = jnp.dot(wc, vl[rs, cs].astype(BF16), preferred_element_type=F32) + be_ref[:, cs]
                o_ref[rs, cs] = (u[rs, cs] * m).astype(BF16)

    whole = lambda arr: pl.BlockSpec(arr.shape, lambda i, nd=arr.ndim: (0,) * nd)
    return pl.pallas_call(
        body, name=name, grid=(T // tr,),
        in_specs=[pl.BlockSpec((tr, 2 * SW), lambda i: (i, 0)), whole(lng), whole(lnb), whole(ws), whole(bexp)],
        out_specs=pl.BlockSpec((tr, SW), lambda i: (i, 0)), out_shape=jax.ShapeDtypeStruct((T, SW), BF16),
        compiler_params=_params())(z, lng, lnb, ws, bexp)


def _sgu_bwd(z, dyb, lng, lnb, ws, bexp, name):
    T, SW = z.shape[0], z.shape[1] // 2
    G = ws.shape[0]
    tr = min(256, T)
    nch = tr // SGU_CHUNK
    nt = T // tr

    def body(z_ref, dy_ref, lng_ref, lnb_ref, ws_ref, be_ref, dz_ref, dlg_ref, dlb_ref, dws_ref, db_ref, du_s, dvl_s, dbacc_s):
        i = pl.program_id(0)
        zv = z_ref[...]
        lng_v = lng_ref[...]
        u, vn, rstd, vl = _sgu_ln(zv, SW, lng_v, lnb_ref[...])

        @pl.when(i == 0)
        def _():
            for s in (dlg_ref, dlb_ref, dws_ref, dbacc_s):
                s[...] = jnp.zeros_like(s)

        for g in range(G):
            cs = slice(g * SGU_GROUP, (g + 1) * SGU_GROUP)
            tri, wc = _causal(ws_ref, g)
            for n in range(nch):
                rs = slice(n * SGU_CHUNK, (n + 1) * SGU_CHUNK)
                blk = vl[rs, cs].astype(BF16)
                m = jnp.dot(wc, blk, preferred_element_type=F32) + be_ref[:, cs]
                dyv = dy_ref[rs, cs]
                du_s[rs, cs] = dyv * m
                dm = dyv * u[rs, cs]
                dvl_s[rs, cs] = _bdot(wc, dm, "tn")
                dws_ref[g] += jnp.where(tri, _bdot(dm, blk, "nt"), 0.0)
                dbacc_s[:, cs] += dm

        dvl = dvl_s[...]
        dlg_ref[...] += _colsum(dvl * vn)
        dlb_ref[...] += _colsum(dvl)
        dvn = dvl * lng_v
        dvv = rstd * (dvn - _mean(dvn) - vn * _mean(dvn * vn))
        gp = _gelu_grad(zv)
        dz_ref[:, :SW] = (du_s[...] * gp[:, :SW]).astype(BF16)
        dz_ref[:, SW:] = (dvv * gp[:, SW:]).astype(BF16)

        @pl.when(i == nt - 1)
        def _():
            lane = lax.broadcasted_iota(jnp.int32, (SGU_CHUNK, LANE), 1)
            out = jnp.zeros((SGU_CHUNK, LANE), F32)
            for g in range(G):
                col = jnp.sum(dbacc_s[:, g * SGU_GROUP:(g + 1) * SGU_GROUP], axis=1, keepdims=True)
                out = jnp.where(lane == g, col, out)
            db_ref[...] = out

    whole = lambda arr: pl.BlockSpec(arr.shape, lambda i, nd=arr.ndim: (0,) * nd)
    acc_shapes = [(1, SW), (1, SW), ws.shape, (SGU_CHUNK, LANE)]
    return pl.pallas_call(
        body, name=name, grid=(nt,),
        in_specs=[pl.BlockSpec((tr, 2 * SW), lambda i: (i, 0)), pl.BlockSpec((tr, SW), lambda i: (i, 0)),
                  whole(lng), whole(lnb), whole(ws), whole(bexp)],
        out_specs=[pl.BlockSpec((tr, 2 * SW), lambda i: (i, 0))] + [pl.BlockSpec(s, lambda i, nd=len(s): (0,) * nd) for s in acc_shapes],
        out_shape=[jax.ShapeDtypeStruct((T, 2 * SW), BF16)] + [jax.ShapeDtypeStruct(s, F32) for s in acc_shapes],
        scratch_shapes=[pltpu.VMEM((tr, SW), F32), pltpu.VMEM((tr, SW), F32), pltpu.VMEM((SGU_CHUNK, SW), F32)],
        compiler_params=_params())(z, dyb, lng, lnb, ws, bexp)


def _exchange(gathers, scatters, name):
    ng, n = len(gathers), len(gathers) + len(scatters)
    HBM = pl.BlockSpec(memory_space=pltpu.HBM)

    def body(*refs):
        ins, outs = refs[:n], refs[n:2 * n]
        ssem, rsem, lsem = refs[2 * n:]
        x, y, c = lax.axis_index("x"), lax.axis_index("y"), lax.axis_index("c")
        me = 4 * x + 2 * y + c

        def src(j, slot):
            return ins[j] if j < ng else ins[j].at[slot]

        def peer_of(rel):
            px = 1 - x if rel & 4 else x
            py = 1 - y if rel & 2 else y
            pc = 1 - c if rel & 1 else c
            return (px, py, pc), 4 * px + 2 * py + pc

        def remote(j, rel, src_slot, dst_slot, dev):
            return pltpu.make_async_remote_copy(
                src_ref=src(j, src_slot), dst_ref=outs[j].at[dst_slot], send_sem=ssem.at[j * (N_DEV - 1) + rel - 1],
                recv_sem=rsem.at[j * (N_DEV - 1) + rel - 1], device_id=dev, device_id_type=pl.DeviceIdType.MESH)

        local = [pltpu.make_async_copy(src(j, me), outs[j].at[me], lsem.at[j]) for j in range(n)]
        for cp in local:
            cp.start()
        sends = []
        for rel in range(1, N_DEV):
            dev, slot = peer_of(rel)
            for j in range(n):
                sends.append(remote(j, rel, slot, me, dev))
                sends[-1].start()
        for rel in range(1, N_DEV):
            dev, slot = peer_of(rel)
            for j in range(n):
                remote(j, rel, slot, slot, dev).wait_recv()
        for cp in sends:
            cp.wait_send()
        for cp in local:
            cp.wait()

    arrays = list(gathers) + list(scatters)
    out_shape = ([jax.ShapeDtypeStruct((N_DEV,) + a.shape, a.dtype) for a in gathers]
                 + [jax.ShapeDtypeStruct(a.shape, a.dtype) for a in scatters])
    res = pl.pallas_call(
        body, name=name, in_specs=[HBM] * n, out_specs=[HBM] * n, out_shape=out_shape,
        scratch_shapes=[pltpu.SemaphoreType.DMA((n * (N_DEV - 1),)), pltpu.SemaphoreType.DMA((n * (N_DEV - 1),)),
                        pltpu.SemaphoreType.DMA((n,))])(*arrays)
    return list(res[:ng]), list(res[ng:])


def _adamw(w, m, v, gparts, name):
    R, C = w.shape
    tm = _pick(R, (256, 128, 64, 32, 16, 8))

    def body(w_ref, m_ref, v_ref, g_ref, go, do, mo, vo):
        g = g_ref[0].astype(F32)
        for j in range(1, N_DEV):
            g = g + g_ref[j].astype(F32)
        mn = ADAM_B1 * m_ref[...] + (1.0 - ADAM_B1) * g
        vn = ADAM_B2 * v_ref[...] + (1.0 - ADAM_B2) * (g * g)
        m_hat = mn / (1.0 - ADAM_B1 ** ADAM_STEP)
        v_hat = vn / (1.0 - ADAM_B2 ** ADAM_STEP)
        go[...] = g
        do[...] = -ADAM_LR * (m_hat / (jnp.sqrt(v_hat) + ADAM_EPS) + ADAM_WD * w_ref[...])
        mo[...] = mn
        vo[...] = vn

    row = pl.BlockSpec((tm, C), lambda i: (i, 0))
    return pl.pallas_call(
        body, name=name, grid=(R // tm,), in_specs=[row, row, row, pl.BlockSpec((N_DEV, tm, C), lambda i: (0, i, 0))],
        out_specs=[row] * 4, out_shape=[jax.ShapeDtypeStruct((R, C), F32)] * 4, compiler_params=_params())(w, m, v, gparts)


def _pack(arrays):
    flat = []
    for a in arrays:
        f = a.reshape(-1)
        pad = _ceil_to(f.shape[0], LANE) - f.shape[0]
        flat.append(jnp.concatenate([f, jnp.zeros((pad,), f.dtype)]) if pad else f)
    buf = jnp.concatenate(flat)
    rows = _ceil_to(buf.shape[0] // LANE, 64)
    buf = jnp.concatenate([buf, jnp.zeros((rows * LANE - buf.shape[0],), buf.dtype)])
    return buf.reshape(rows, LANE)


def _unpack(buf, shapes):
    flat, out, off = buf.reshape(-1), [], 0
    for s in shapes:
        size = 1
        for d in s:
            size *= d
        out.append(flat[off:off + size].reshape(s))
        off += _ceil_to(size, LANE)
    return out


def kernel(x, norm_mix_g, w_in, shift_mu, w0, w_lora_up, a0, a_lora_up, g_lora_up, k_k, k_a, r_k, lnx_g, lnx_b, w_proj_rwkv, sgu_ln_g, sgu_ln_b, sgu_w, sgu_b, w_proj_sgu, w_out, norm_ffn_g, w_ffn_gate, w_ffn_up, w_ffn_down, norm_final_g, loss_target, m_norm_mix_g, m_w_in, m_shift_mu, m_w0, m_w_lora_up, m_a0, m_a_lora_up, m_g_lora_up, m_k_k, m_k_a, m_r_k, m_lnx_g, m_lnx_b, m_w_proj_rwkv, m_sgu_ln_g, m_sgu_ln_b, m_sgu_w, m_sgu_b, m_w_proj_sgu, m_w_out, m_norm_ffn_g, m_w_ffn_gate, m_w_ffn_up, m_w_ffn_down, m_norm_final_g, v_norm_mix_g, v_w_in, v_shift_mu, v_w0, v_w_lora_up, v_a0, v_a_lora_up, v_g_lora_up, v_k_k, v_k_a, v_r_k, v_lnx_g, v_lnx_b, v_w_proj_rwkv, v_sgu_ln_g, v_sgu_ln_b, v_sgu_w, v_sgu_b, v_w_proj_sgu, v_w_out, v_norm_ffn_g, v_w_ffn_gate, v_w_ffn_up, v_w_ffn_down, v_norm_final_g):
    weights = dict(norm_mix_g=norm_mix_g, w_in=w_in, shift_mu=shift_mu, w0=w0, w_lora_up=w_lora_up, a0=a0, a_lora_up=a_lora_up,
                   g_lora_up=g_lora_up, k_k=k_k, k_a=k_a, r_k=r_k, lnx_g=lnx_g, lnx_b=lnx_b, w_proj_rwkv=w_proj_rwkv,
                   sgu_ln_g=sgu_ln_g, sgu_ln_b=sgu_ln_b, sgu_w=sgu_w, sgu_b=sgu_b, w_proj_sgu=w_proj_sgu, w_out=w_out,
                   norm_ffn_g=norm_ffn_g, w_ffn_gate=w_ffn_gate, w_ffn_up=w_ffn_up, w_ffn_down=w_ffn_down, norm_final_g=norm_final_g)
    m_in = dict(norm_mix_g=m_norm_mix_g, w_in=m_w_in, shift_mu=m_shift_mu, w0=m_w0, w_lora_up=m_w_lora_up, a0=m_a0,
                a_lora_up=m_a_lora_up, g_lora_up=m_g_lora_up, k_k=m_k_k, k_a=m_k_a, r_k=m_r_k, lnx_g=m_lnx_g, lnx_b=m_lnx_b,
                w_proj_rwkv=m_w_proj_rwkv, sgu_ln_g=m_sgu_ln_g, sgu_ln_b=m_sgu_ln_b, sgu_w=m_sgu_w, sgu_b=m_sgu_b,
                w_proj_sgu=m_w_proj_sgu, w_out=m_w_out, norm_ffn_g=m_norm_ffn_g, w_ffn_gate=m_w_ffn_gate, w_ffn_up=m_w_ffn_up,
                w_ffn_down=m_w_ffn_down, norm_final_g=m_norm_final_g)
    v_in = dict(norm_mix_g=v_norm_mix_g, w_in=v_w_in, shift_mu=v_shift_mu, w0=v_w0, w_lora_up=v_w_lora_up, a0=v_a0,
                a_lora_up=v_a_lora_up, g_lora_up=v_g_lora_up, k_k=v_k_k, k_a=v_k_a, r_k=v_r_k, lnx_g=v_lnx_g, lnx_b=v_lnx_b,
                w_proj_rwkv=v_w_proj_rwkv, sgu_ln_g=v_sgu_ln_g, sgu_ln_b=v_sgu_ln_b, sgu_w=v_sgu_w, sgu_b=v_sgu_b,
                w_proj_sgu=v_w_proj_sgu, w_out=v_w_out, norm_ffn_g=v_norm_ffn_g, w_ffn_gate=v_w_ffn_gate, w_ffn_up=v_w_ffn_up,
                w_ffn_down=v_w_ffn_down, norm_final_g=v_norm_final_g)
    names = list(weights)
    col_sharded = ("w_in", "w_lora_up", "a_lora_up", "g_lora_up", "w_proj_rwkv", "w_proj_sgu", "w_ffn_gate", "w_ffn_up")
    row_sharded = ("w_out", "w_ffn_down")
    sharded = [n for n in names if n in col_sharded or n in row_sharded]
    small = [n for n in names if n not in sharded]

    xs, tgt = x[0], loss_target[0]
    T, D = xs.shape
    RW = w0.shape[1]
    H = RW // HEAD
    SW = sgu_ln_g.shape[1]
    G = sgu_w.shape[1]
    lay = _rwkv_layout(RW, w_lora_up.shape[1], a_lora_up.shape[1], g_lora_up.shape[1])
    _, pw, _, rcp = lay

    gathered, _ = _exchange([weights[n][0].astype(BF16) for n in sharded], [], "gather_weights")
    full = {}
    for n, g in zip(sharded, gathered):
        full[n] = g.transpose(1, 0, 2).reshape(g.shape[1], -1) if n in col_sharded else g.reshape(-1, g.shape[2])
    W_in = _pad_rwkv_cols(full["w_in"], lay)
    o_z, o_ga, o_gb = rcp, rcp + 2 * SW, rcp + 2 * SW + D
    W_r, W_z, W_ga, W_gb = W_in[:, :o_z], W_in[:, o_z:o_ga], W_in[:, o_ga:o_gb], W_in[:, o_gb:]
    lora = [_pad_rows(full["w_lora_up"], pw[3]), _pad_rows(full["a_lora_up"], pw[4]), _pad_rows(full["g_lora_up"], pw[5])]
    mu_p = _pad_rwkv_cols(shift_mu, lay)
    rsmall = [w0, a0, k_k, k_a]
    hp = [lnx_g.reshape(H, 1, HEAD), lnx_b.reshape(H, 1, HEAD), r_k.reshape(H, 1, HEAD)]
    ws = sgu_w[0]
    bexp = jnp.repeat(sgu_b[0].T, SGU_GROUP, axis=1)
    gf = norm_final_g.reshape(1, D)

    n1 = _rms_fwd(xs, norm_mix_g, "rms_mix")
    p = _matmul(n1, W_r, mode="nn", out_dtype=F32, name="proj_rwkv")
    z = _matmul(n1, W_z, mode="nn", out_dtype=F32, name="proj_sgu")
    ga = _matmul(n1, W_ga, mode="nn", out_dtype=F32, name="proj_gate_a")
    gb = _matmul(n1, W_gb, mode="nn", out_dtype=F32, name="proj_gate_b")
    r_t, lw_t, k2_t, v_t, kkp_t, a_t, g_t = _rwkv_pre(p, mu_p, rsmall, lora, lay, "rwkv_pre")
    r_h, lw_h, k2_h, v_h, kkp_h, a_h, g_h = (_to_heads(t, H) for t in (r_t, lw_t, k2_t, v_t, kkp_t, a_t, g_t))
    aa_h, bb_h = _head_pre(kkp_h, a_h, "head_pre")
    h3 = lambda t: t.reshape(H, T, HEAD)
    wkv_in = [h3(t) for t in (r_h, lw_h, k2_h, v_h, aa_h, bb_h)]
    y_h, states = _wkv_fwd(*wkv_in, "wkv_fwd")
    y_h = y_h.reshape(H * T, HEAD)
    ya = _from_heads(_head_post(y_h, r_h, k2_h, v_h, g_h, hp, H, "head_post"), H)
    yb = _sgu_fwd(z, sgu_ln_g, sgu_ln_b, ws, bexp, "sgu_fwd")
    pa = _matmul(ya, full["w_proj_rwkv"], mode="nn", out_dtype=F32, name="proj_a")
    pb = _matmul(yb, full["w_proj_sgu"], mode="nn", out_dtype=F32, name="proj_b")

    def merge_fn(rv, pv, hv):
        ga_v, gb_v, pa_v, pb_v = rv
        return [_sigmoid(ga_v) * pa_v + _sigmoid(gb_v) * pb_v], [], []
    merged = _rowwise(merge_fn, [ga, gb, pa, pb], [], [(D, BF16)], [], name="merge")[0]
    h1 = _matmul(merged, full["w_out"], mode="nn", out_dtype=F32, name="out_proj", add=xs)
    n2 = _rms_fwd(h1, norm_ffn_g, "rms_ffn")
    gt = _matmul(n2, full["w_ffn_gate"], mode="nn", out_dtype=F32, name="ffn_gate")
    up = _matmul(n2, full["w_ffn_up"], mode="nn", out_dtype=F32, name="ffn_up")

    def act_fn(rv, pv, hv):
        gt_v, up_v = rv
        return [gt_v * _sigmoid(gt_v) * up_v], [], []
    act = _rowwise(act_fn, [gt, up], [], [(gt.shape[1], BF16)], [], name="ffn_act")[0]
    h2 = _matmul(act, full["w_ffn_down"], mode="nn", out_dtype=F32, name="ffn_down", add=h1)

    def final_fn(rv, pv, hv):
        (h_v, t_v), (g_v,) = rv, pv
        r = lax.rsqrt(_mean(h_v * h_v) + RMS_EPS)
        yn = h_v * r
        e = yn * g_v - t_v
        loss = 0.5 * jnp.sum(_mean(e * e))
        dout = e * (1.0 / D)
        dyg = dout * g_v
        return [r * (dyg - yn * _mean(dyg * yn))], [jnp.full((1, LANE), loss, F32), _colsum(dout * yn)], []
    dh2, loss_part, d_gf = _rowwise(final_fn, [h2, tgt], [gf], [(D, F32)], [(1, LANE), (1, D)], name="final_loss")

    grads = {}
    dact = _matmul(dh2, full["w_ffn_down"], mode="nt", out_dtype=F32, name="d_act")
    grads["w_ffn_down"] = _matmul(act, dh2, mode="tn", out_dtype=BF16, name="dw_ffn_down")

    def dact_fn(rv, pv, hv):
        d_v, gt_v, up_v = rv
        s = _sigmoid(gt_v)
        return [d_v * up_v * (s * (1.0 + gt_v * (1.0 - s))), d_v * gt_v * s], [], []
    dgt, dup = _rowwise(dact_fn, [dact, gt, up], [], [(gt.shape[1], BF16)] * 2, [], name="d_ffn_act")
    dn2 = _matmul(dgt, full["w_ffn_gate"], mode="nt", out_dtype=F32, name="dn2_gate")
    dn2 = _matmul(dup, full["w_ffn_up"], mode="nt", out_dtype=F32, name="dn2_up", add=dn2)
    grads["w_ffn_gate"] = _matmul(n2, dgt, mode="tn", out_dtype=BF16, name="dw_ffn_gate")
    grads["w_ffn_up"] = _matmul(n2, dup, mode="tn", out_dtype=BF16, name="dw_ffn_up")
    dh1, d_g2 = _rms_bwd(dn2, h1, dh2, norm_ffn_g, "rms_ffn_bwd")
    dmerged = _matmul(dh1, full["w_out"], mode="nt", out_dtype=F32, name="d_merged")
    grads["w_out"] = _matmul(merged, dh1, mode="tn", out_dtype=BF16, name="dw_out")

    def dmerge_fn(rv, pv, hv):
        d_v, ga_v, gb_v, pa_v, pb_v = rv
        sa, sb = _sigmoid(ga_v), _sigmoid(gb_v)
        return [d_v * pa_v * sa * (1.0 - sa), d_v * pb_v * sb * (1.0 - sb), d_v * sa, d_v * sb], [], []
    dga, dgb, dpa, dpb = _rowwise(dmerge_fn, [dmerged, ga, gb, pa, pb], [], [(D, BF16)] * 4, [], name="d_merge")
    dya = _matmul(dpa, full["w_proj_rwkv"], mode="nt", out_dtype=F32, name="d_ya")
    dyb = _matmul(dpb, full["w_proj_sgu"], mode="nt", out_dtype=F32, name="d_yb")
    grads["w_proj_rwkv"] = _matmul(ya, dpa, mode="tn", out_dtype=BF16, name="dw_proj_a")
    grads["w_proj_sgu"] = _matmul(yb, dpb, mode="tn", out_dtype=BF16, name="dw_proj_b")
    dz, d_lng, d_lnb, d_ws, d_bs = _sgu_bwd(z, dyb, sgu_ln_g, sgu_ln_b, ws, bexp, "sgu_bwd")

    dy_h, dr1, dk1, dv1, dg_h, d_lnxg, d_lnxb, d_rk = _head_post_bwd(_to_heads(dya, H), y_h, r_h, k2_h, v_h, g_h, hp, H, "head_post_bwd")
    dr2, dlw_h, dk2b, dv2, daa, dbb = (t.reshape(H * T, HEAD) for t in _wkv_bwd(*wkv_in, states, h3(dy_h), "wkv_bwd"))
    dkkp_h, da_h, dr_h, dk2_h, dv_h = _head_pre_bwd(kkp_h, a_h, daa, dbb, [dr1, dr2, dk1, dk2b, dv1, dv2], "head_pre_bwd")
    tok = [_from_heads(t, H) for t in (dr_h, dlw_h, dk2_h, dv_h, dkkp_h, da_h, dg_h)]
    dps, d_mu, d_w0, d_a0, d_kk, d_ka, d_wlw, d_wla, d_wlg = _rwkv_pre_bwd(p, mu_p, rsmall, lora, tok, lay, "rwkv_pre_bwd")
    dp = _shift_bwd(dps, mu_p, "shift_bwd")
    dproj = jnp.concatenate([dp, dz, dga, dgb], axis=1)
    dn1 = _matmul(dproj, W_in, mode="nt", out_dtype=F32, name="dn1")
    grads["w_in"] = _unpad_rwkv_cols(_matmul(n1, dproj, mode="tn", out_dtype=BF16, name="dw_in"), lay)
    dx, d_g1 = _rms_bwd(dn1, xs, dh1, norm_mix_g, "rms_mix_bwd")
    grads["w_lora_up"] = d_wlw[:w_lora_up.shape[1]].astype(BF16)
    grads["a_lora_up"] = d_wla[:a_lora_up.shape[1]].astype(BF16)
    grads["g_lora_up"] = d_wlg[:g_lora_up.shape[1]].astype(BF16)
    small_grads = dict(norm_mix_g=d_g1, shift_mu=_unpad_rwkv_cols(d_mu, lay), w0=d_w0, a0=d_a0, k_k=d_kk, k_a=d_ka, r_k=d_rk,
                       lnx_g=d_lnxg, lnx_b=d_lnxb, sgu_ln_g=d_lng, sgu_ln_b=d_lnb, sgu_w=d_ws, sgu_b=d_bs[:, :G].T,
                       norm_ffn_g=d_g2, norm_final_g=d_gf)

    blocks = []
    for n in sharded:
        g = grads[n]
        blocks.append(g.reshape(g.shape[0], N_DEV, -1).transpose(1, 0, 2) if n in col_sharded else g.reshape(N_DEV, -1, g.shape[1]))
    (small_parts,), parts = _exchange([_pack([small_grads[n] for n in small])], blocks, "exchange_grads")

    out = {}
    for n, part in zip(sharded, parts):
        shp = weights[n].shape
        res = _adamw(weights[n][0], m_in[n][0], v_in[n][0], part, "adamw_" + n)
        out[n] = [t.reshape(shp) for t in res]
    packed = [_pack([d[n] for n in small]) for d in (weights, m_in, v_in)]
    res = _adamw(*packed, small_parts, "adamw_small")
    unpacked = [_unpack(t, [weights[n].shape for n in small]) for t in res]
    for i, n in enumerate(small):
        out[n] = [u[i] for u in unpacked]

    loss = lax.psum(loss_part[0, 0], ("x", "y", "c"))
    return (loss, dx[None], *[out[n][0] for n in names], *[out[n][1] for n in names],
            *[out[n][2] for n in names], *[out[n][3] for n in names])
```

```python
import jax
import jax.numpy as jnp
from jax import lax
from jax.experimental import pallas as pl
from jax.experimental.pallas import tpu as pltpu

F32 = jnp.float32
BF16 = jnp.bfloat16

N_DEV = 8
LANE = 128
SUBLANE = 8
HEAD = 64
SGU_CHUNK = 128
SGU_GROUP = 128
WKV_CHUNK = 64
RMS_EPS = 1e-6
LN_EPS = 1e-5
LNX_EPS = 64e-5
ADAM_LR, ADAM_B1, ADAM_B2, ADAM_EPS, ADAM_WD, ADAM_STEP = 0.001, 0.9, 0.999, 1e-08, 0.01, 10
VMEM_LIMIT_BYTES = 48 * 1024 * 1024
_SQRT_HALF = 0.7071067811865476
_INV_SQRT_2PI = 0.3989422804014327
_HI = lax.Precision.HIGHEST


def _pick(n, cands):
    for c in cands:
        if n % c == 0:
            return c
    return n


def _ceil_to(n, m):
    return -(-n // m) * m


def _params():
    return pltpu.CompilerParams(vmem_limit_bytes=VMEM_LIMIT_BYTES)


def _tile(n, cap):
    best = 0
    for d in range(LANE, min(n, cap) + 1, LANE):
        if n % d == 0:
            best = d
    return best or n


def _matmul_tiles(M, N, K, a_bytes, b_bytes, o_bytes, has_add):
    tm, tn, tk = _tile(M, 1024), _tile(N, 1024), _tile(K, 2048)

    def vmem(tm, tn, tk):
        acc = tm * tn * 4 if tk < K else 0
        return 2 * (tm * tk * a_bytes + tk * tn * b_bytes + tm * tn * (o_bytes + (4 if has_add else 0))) + acc

    while vmem(tm, tn, tk) > (VMEM_LIMIT_BYTES * 3) // 4:
        if tk > 512 and _tile(K, tk // 2) < tk:
            tk = _tile(K, tk // 2)
        elif _tile(M, tm // 2) < tm:
            tm = _tile(M, tm // 2)
        else:
            break
    return tm, tn, tk


def _matmul(a, b, *, mode, out_dtype, name, add=None, deps=()):
    if mode == "nn":
        (M, K), (K2, N) = a.shape, b.shape
    elif mode == "nt":
        (M, K), (N, K2) = a.shape, b.shape
    else:
        (K, M), (K2, N) = a.shape, b.shape
    assert K == K2, (a.shape, b.shape, mode)
    has_add = add is not None
    tm, tn, tk = _matmul_tiles(M, N, K, a.dtype.itemsize, b.dtype.itemsize, jnp.dtype(out_dtype).itemsize, has_add)
    nk = K // tk
    dn = {"nn": (((1,), (0,)), ((), ())), "nt": (((1,), (1,)), ((), ())), "tn": (((0,), (0,)), ((), ()))}[mode]
    a_spec = pl.BlockSpec((tk, tm), lambda i, j, k: (k, i)) if mode == "tn" else pl.BlockSpec((tm, tk), lambda i, j, k: (i, k))
    b_spec = pl.BlockSpec((tn, tk), lambda i, j, k: (j, k)) if mode == "nt" else pl.BlockSpec((tk, tn), lambda i, j, k: (k, j))
    o_spec = pl.BlockSpec((tm, tn), lambda i, j, k: (i, j))
    n_in = 2 + has_add + len(deps)

    def body(*refs):
        a_ref, b_ref = refs[0], refs[1]
        add_ref = refs[2] if has_add else None
        o_ref = refs[n_in]
        part = lax.dot_general(a_ref[...].astype(BF16), b_ref[...].astype(BF16), dn, preferred_element_type=F32)
        if nk == 1:
            if has_add:
                part = part + add_ref[...]
            o_ref[...] = part.astype(out_dtype)
            return
        acc_ref = refs[-1]
        kk = pl.program_id(2)

        @pl.when(kk == 0)
        def _():
            acc_ref[...] = part + add_ref[...] if has_add else part

        @pl.when(kk > 0)
        def _():
            acc_ref[...] += part

        @pl.when(kk == nk - 1)
        def _():
            o_ref[...] = acc_ref[...].astype(out_dtype)

    ins = [a, b] + ([add] if has_add else []) + list(deps)
    in_specs = ([a_spec, b_spec] + ([o_spec] if has_add else [])
                + [pl.BlockSpec(d.shape, lambda i, j, k, nd=d.ndim: (0,) * nd) for d in deps])
    return pl.pallas_call(
        body, name=name, grid=(M // tm, N // tn, nk), in_specs=in_specs, out_specs=o_spec,
        out_shape=jax.ShapeDtypeStruct((M, N), out_dtype),
        scratch_shapes=[pltpu.VMEM((tm, tn), F32)] if nk > 1 else [],
        compiler_params=_params())(*ins)


def _rowwise(fn, rows, pars, row_outs, acc_outs, *, name, tm=256, hpars=(), hacc_outs=(), heads=1, deps=()):
    R = rows[0].shape[0]
    rph = R // heads
    if max(r.shape[1] for r in rows) > 4096:
        tm = tm // 2
    tm = min(tm, rph)
    assert rph % tm == 0 and R % heads == 0
    tph = rph // tm
    nr, npar, nh = len(rows), len(pars), len(hpars)
    nro, nao = len(row_outs), len(acc_outs)

    def body(*refs):
        rv = [r[...] for r in refs[:nr]]
        pv = [p[...] for p in refs[nr:nr + npar]]
        hv = [h[0] for h in refs[nr + npar:nr + npar + nh]]
        outs = refs[nr + npar + nh + len(deps):]
        ro, ao, ho = fn(rv, pv, hv)
        i = pl.program_id(0)
        for o_ref, val in zip(outs[:nro], ro):
            o_ref[...] = val.astype(o_ref.dtype)

        def accumulate(o_ref, val, first):
            @pl.when(first)
            def _():
                o_ref[...] = val

            @pl.when(jnp.logical_not(first))
            def _():
                o_ref[...] += val

        for o_ref, val in zip(outs[nro:nro + nao], ao):
            accumulate(o_ref, val, i == 0)
        for o_ref, val in zip(outs[nro + nao:], ho):
            accumulate(o_ref, val[None], i % tph == 0)

    in_specs = ([pl.BlockSpec((tm, r.shape[1]), lambda i: (i, 0)) for r in rows]
                + [pl.BlockSpec(p.shape, lambda i, nd=p.ndim: (0,) * nd) for p in pars]
                + [pl.BlockSpec((1, 1, h.shape[2]), lambda i: (i // tph, 0, 0)) for h in hpars]
                + [pl.BlockSpec(d.shape, lambda i, nd=d.ndim: (0,) * nd) for d in deps])
    out_shape = ([jax.ShapeDtypeStruct((R, f), dt) for f, dt in row_outs]
                 + [jax.ShapeDtypeStruct(s, F32) for s in acc_outs]
                 + [jax.ShapeDtypeStruct((heads, 1, f), F32) for f in hacc_outs])
    out_specs = ([pl.BlockSpec((tm, f), lambda i: (i, 0)) for f, _ in row_outs]
                 + [pl.BlockSpec(s, lambda i, nd=len(s): (0,) * nd) for s in acc_outs]
                 + [pl.BlockSpec((1, 1, f), lambda i: (i // tph, 0, 0)) for f in hacc_outs])
    res = pl.pallas_call(body, name=name, grid=(R // tm,), in_specs=in_specs, out_specs=out_specs, out_shape=out_shape,
                         compiler_params=_params())(*rows, *pars, *hpars, *deps)
    return list(res)


def _bdot(a, b, mode="nn"):
    dn = {"nn": (((1,), (0,)), ((), ())), "nt": (((1,), (1,)), ((), ())), "tn": (((0,), (0,)), ((), ()))}[mode]
    return lax.dot_general(a.astype(BF16), b.astype(BF16), dn, preferred_element_type=F32)


def _sigmoid(x):
    return jax.nn.sigmoid(x)


def _softplus(x):
    return jnp.maximum(x, 0.0) + jnp.log1p(jnp.exp(-jnp.abs(x)))


def _gelu(z):
    return 0.5 * z * (1.0 + lax.erf(z * _SQRT_HALF))


def _gelu_grad(z):
    return 0.5 * (1.0 + lax.erf(z * _SQRT_HALF)) + z * jnp.exp(-0.5 * z * z) * _INV_SQRT_2PI


def _mean(x):
    return jnp.mean(x, axis=-1, keepdims=True)


def _colsum(x):
    return jnp.sum(x, axis=0, keepdims=True)


def _rms_fwd(x, g, name, deps=()):
    def fn(rv, pv, hv):
        (xv,), (gv,) = rv, pv
        r = lax.rsqrt(_mean(xv * xv) + RMS_EPS)
        return [xv * r * gv], [], []
    return _rowwise(fn, [x], [g], [(x.shape[1], BF16)], [], name=name, deps=deps)[0]


def _rms_bwd(dn, x, dres, g, name, deps=()):
    def fn(rv, pv, hv):
        (dnv, xv, drv), (gv,) = rv, pv
        r = lax.rsqrt(_mean(xv * xv) + RMS_EPS)
        yn = xv * r
        dyg = dnv * gv
        dx = drv + r * (dyg - yn * _mean(dyg * yn))
        return [dx, dx], [_colsum(dnv * yn)], []
    D = x.shape[1]
    return _rowwise(fn, [dn, x, dres], [g], [(D, F32), (D, BF16)], [(1, D)], name=name, deps=deps)


def _rwkv_layout(RW, Lw, La, Lg):
    widths = [RW, RW, RW, Lw, La, Lg]
    pw = [_ceil_to(w, LANE) for w in widths]
    offs = [sum(pw[:i]) for i in range(6)]
    return widths, pw, offs, sum(pw)


def _pad_rwkv_cols(a, lay):
    widths, pw, _, _ = lay
    pieces, src = [], 0
    for w, p in zip(widths, pw):
        pieces.append(a[:, src:src + w])
        if p > w:
            pieces.append(jnp.zeros((a.shape[0], p - w), a.dtype))
        src += w
    pieces.append(a[:, src:])
    return jnp.concatenate(pieces, axis=1)


def _unpad_rwkv_cols(a, lay):
    widths, _, offs, rcp = lay
    return jnp.concatenate([a[:, o:o + w] for o, w in zip(offs, widths)] + [a[:, rcp:]], axis=1)


def _pad_rows(a, rows):
    return a if a.shape[0] == rows else jnp.concatenate([a, jnp.zeros((rows - a.shape[0], a.shape[1]), a.dtype)], axis=0)


def _token_shift(p, halo, mu, i):
    tm = p.shape[0]
    hid = lax.broadcasted_iota(jnp.int32, (SUBLANE, 1), 0)
    before = jnp.sum(jnp.where(hid == SUBLANE - 1, halo, 0.0), axis=0, keepdims=True)
    before = jnp.where(i == 0, 0.0, before)
    rid = lax.broadcasted_iota(jnp.int32, (tm, 1), 0)
    prev = jnp.where(rid == 0, before, pltpu.roll(p, 1, 0))
    d = prev - p
    return p + d * mu, d


def _rwkv_math(ps, w0, a0, k_k, k_a, wlw, wla, wlg, lay):
    _, pw, offs, _ = lay
    r, k, v, xw, xa, xg = (ps[:, offs[j]:offs[j] + pw[j]] for j in range(6))
    tw = jnp.tanh(xw)
    ww = w0 + _bdot(tw, wlw)
    lw = -jnp.exp(-_softplus(-ww) - 0.5)
    a = _sigmoid(a0 + _bdot(xa, wla))
    sg = _sigmoid(xg)
    g = _bdot(sg, wlg)
    return dict(r=r, k=k, v=v, xa=xa, tw=tw, ww=ww, lw=lw, a=a, sg=sg, g=g, kkp=k * k_k, k2=k * (1.0 + (a - 1.0) * k_a))


def _halo_specs(T, tm, width, after):
    hb = tm // SUBLANE
    last = T // SUBLANE - 1
    if after:
        return pl.BlockSpec((SUBLANE, width), lambda i: (jnp.minimum((i + 1) * hb, last), 0))
    return pl.BlockSpec((SUBLANE, width), lambda i: (jnp.maximum(i * hb - 1, 0), 0))


def _rwkv_pre(p, mu, small, lora, lay, name):
    T, rcp = p.shape
    RW = lay[0][0]
    tm = min(256, T)

    def body(p_ref, ph_ref, mu_ref, w0_ref, a0_ref, kk_ref, ka_ref, wlw_ref, wla_ref, wlg_ref, *outs):
        ps, _ = _token_shift(p_ref[...], ph_ref[...], mu_ref[...], pl.program_id(0))
        q = _rwkv_math(ps, w0_ref[...], a0_ref[...], kk_ref[...], ka_ref[...], wlw_ref[...], wla_ref[...], wlg_ref[...], lay)
        for o_ref, key in zip(outs, ("r", "lw", "k2", "v", "kkp", "a", "g")):
            o_ref[...] = q[key]

    whole = lambda arr: pl.BlockSpec(arr.shape, lambda i: (0, 0))
    row = lambda w: pl.BlockSpec((tm, w), lambda i: (i, 0))
    return pl.pallas_call(
        body, name=name, grid=(T // tm,),
        in_specs=[row(rcp), _halo_specs(T, tm, rcp, False), whole(mu)] + [whole(s) for s in small] + [whole(w) for w in lora],
        out_specs=[row(RW)] * 7, out_shape=[jax.ShapeDtypeStruct((T, RW), F32)] * 7,
        compiler_params=_params())(p, p, mu, *small, *lora)


def _rwkv_pre_bwd(p, mu, small, lora, grads, lay, name):
    T, rcp = p.shape
    widths, pw, offs, _ = lay
    RW = widths[0]
    tm = min(256, T)

    def body(p_ref, ph_ref, mu_ref, w0_ref, a0_ref, kk_ref, ka_ref, wlw_ref, wla_ref, wlg_ref,
             dr_ref, dlw_ref, dk2_ref, dv_ref, dkkp_ref, da_ref, dg_ref,
             dps_ref, dmu_ref, dw0_ref, da0_ref, dkk_ref, dka_ref, dwlw_ref, dwla_ref, dwlg_ref):
        i = pl.program_id(0)
        ps, dprev = _token_shift(p_ref[...], ph_ref[...], mu_ref[...], i)
        k_k, k_a = kk_ref[...], ka_ref[...]
        q = _rwkv_math(ps, w0_ref[...], a0_ref[...], k_k, k_a, wlw_ref[...], wla_ref[...], wlg_ref[...], lay)
        k, a, lw, ww, tw, sg = q["k"], q["a"], q["lw"], q["ww"], q["tw"], q["sg"]
        dk2, dkkp, dg = dk2_ref[...], dkkp_ref[...], dg_ref[...]
        dk = dk2 * (1.0 + (a - 1.0) * k_a) + dkkp * k_k
        da = da_ref[...] + dk2 * k * k_a
        dpa = da * a * (1.0 - a)
        dww = dlw_ref[...] * lw * _sigmoid(-ww)
        dxa = _bdot(dpa, wla_ref[...], "nt")
        dxw = _bdot(dww, wlw_ref[...], "nt") * (1.0 - tw * tw)
        dxg = _bdot(dg, wlg_ref[...], "nt") * sg * (1.0 - sg)
        segs = (dr_ref[...], dk, dv_ref[...], dxw, dxa, dxg)
        sums = [dmu_ref, dw0_ref, da0_ref, dkk_ref, dka_ref, dwlw_ref, dwla_ref, dwlg_ref]

        @pl.when(i == 0)
        def _():
            for s in sums:
                s[...] = jnp.zeros_like(s)

        for j, seg in enumerate(segs):
            sl = slice(offs[j], offs[j] + pw[j])
            dps_ref[:, sl] = seg
            dmu_ref[:, sl] += _colsum(seg * dprev[:, sl])
        dw0_ref[...] += _colsum(dww)
        da0_ref[...] += _colsum(dpa)
        dkk_ref[...] += _colsum(dkkp * k)
        dka_ref[...] += _colsum(dk2 * k * (a - 1.0))
        dwlw_ref[...] += _bdot(tw, dww, "tn")
        dwla_ref[...] += _bdot(q["xa"], dpa, "tn")
        dwlg_ref[...] += _bdot(sg, dg, "tn")

    whole = lambda arr: pl.BlockSpec(arr.shape, lambda i: (0, 0))
    row = lambda w: pl.BlockSpec((tm, w), lambda i: (i, 0))
    acc_shapes = [(1, rcp), (1, RW), (1, RW), (1, RW), (1, RW)] + [w.shape for w in lora]
    return pl.pallas_call(
        body, name=name, grid=(T // tm,),
        in_specs=[row(rcp), _halo_specs(T, tm, rcp, False), whole(mu)] + [whole(s) for s in small] + [whole(w) for w in lora] + [row(RW)] * 7,
        out_specs=[row(rcp)] + [pl.BlockSpec(s, lambda i: (0, 0)) for s in acc_shapes],
        out_shape=[jax.ShapeDtypeStruct((T, rcp), F32)] + [jax.ShapeDtypeStruct(s, F32) for s in acc_shapes],
        compiler_params=_params())(p, p, mu, *small, *lora, *grads)


def _shift_bwd(dps, mu, name):
    T, rcp = dps.shape
    tm = min(256, T)
    nt = T // tm

    def body(d_ref, dh_ref, mu_ref, o_ref):
        i = pl.program_id(0)
        d = d_ref[...]
        hid = lax.broadcasted_iota(jnp.int32, (SUBLANE, 1), 0)
        after = jnp.sum(jnp.where(hid == 0, dh_ref[...], 0.0), axis=0, keepdims=True)
        after = jnp.where(i == nt - 1, 0.0, after)
        rid = lax.broadcasted_iota(jnp.int32, (tm, 1), 0)
        nxt = jnp.where(rid == tm - 1, after, pltpu.roll(d, tm - 1, 0))
        mu_v = mu_ref[...]
        o_ref[...] = (d * (1.0 - mu_v) + nxt * mu_v).astype(BF16)

    row = pl.BlockSpec((tm, rcp), lambda i: (i, 0))
    return pl.pallas_call(
        body, name=name, grid=(nt,), in_specs=[row, _halo_specs(T, tm, rcp, True), pl.BlockSpec(mu.shape, lambda i: (0, 0))],
        out_specs=row, out_shape=jax.ShapeDtypeStruct((T, rcp), BF16), compiler_params=_params())(dps, dps, mu)


def _to_heads(x, H):
    T = x.shape[0]
    return x.reshape(T, H, HEAD).transpose(1, 0, 2).reshape(H * T, HEAD)


def _from_heads(x, H):
    T = x.shape[0] // H
    return x.reshape(H, T, HEAD).transpose(1, 0, 2).reshape(T, H * HEAD)


def _rowsum(x):
    return jnp.sum(x, axis=-1, keepdims=True)


def _kk_math(kkp):
    nrm = jnp.sqrt(_rowsum(kkp * kkp))
    inv = 1.0 / jnp.maximum(nrm, 1e-12)
    return nrm, inv, kkp * inv


def _head_pre(kkp, a, name):
    def fn(rv, pv, hv):
        kkp_v, a_v = rv
        _, _, kk = _kk_math(kkp_v)
        return [-kk, kk * a_v], [], []
    return _rowwise(fn, [kkp, a], [], [(HEAD, F32)] * 2, [], name=name, tm=2048)


def _head_pre_bwd(kkp, a, daa, dbb, pairs, name):
    def fn(rv, pv, hv):
        kkp_v, a_v, daa_v, dbb_v = rv[:4]
        nrm, inv, kk = _kk_math(kkp_v)
        dkk = dbb_v * a_v - daa_v
        dkkp = jnp.where(nrm > 1e-12, inv * (dkk - kk * _rowsum(dkk * kk)), dkk * inv)
        return [dkkp, dbb_v * kk, rv[4] + rv[5], rv[6] + rv[7], rv[8] + rv[9]], [], []
    return _rowwise(fn, [kkp, a, daa, dbb] + list(pairs), [], [(HEAD, F32)] * 5, [], name=name, tm=1024)


def _head_post_math(y, r, k2, v, lg, lb, rk):
    yc = y - _mean(y)
    rstd = lax.rsqrt(_mean(yc * yc) + LNX_EPS)
    yn = yc * rstd
    s = _rowsum(r * k2 * rk)
    return yn, rstd, yn * lg + lb + s * v, s


def _head_post(y, r, k2, v, g, hp, H, name):
    def fn(rv, pv, hv):
        yv, r_v, k_v, v_v, g_v = rv
        _, _, t, _ = _head_post_math(yv, r_v, k_v, v_v, *hv)
        return [t * g_v], [], []
    return _rowwise(fn, [y, r, k2, v, g], [], [(HEAD, BF16)], [], name=name, tm=1024, hpars=hp, heads=H)[0]


def _head_post_bwd(dya, y, r, k2, v, g, hp, H, name, deps=()):
    def fn(rv, pv, hv):
        d_v, yv, r_v, k_v, v_v, g_v = rv
        lg, lb, rk = hv
        yn, rstd, t, s = _head_post_math(yv, r_v, k_v, v_v, lg, lb, rk)
        dyo = d_v * g_v
        dyn = dyo * lg
        dy = rstd * (dyn - _mean(dyn) - yn * _mean(dyn * yn))
        ds = _rowsum(dyo * v_v)
        return ([dy, ds * k_v * rk, ds * r_v * rk, dyo * s, d_v * t], [],
                [_colsum(dyo * yn), _colsum(dyo), _colsum(ds * r_v * k_v)])
    return _rowwise(fn, [dya, y, r, k2, v, g], [], [(HEAD, F32)] * 5, [], name=name, tm=1024, hpars=hp, hacc_outs=[HEAD] * 3,
                    heads=H, deps=deps)


def _bmm(x, y, mode):
    dn = {"nn": (((2,), (1,)), ((0,), (0,))), "nt": (((2,), (2,)), ((0,), (0,))), "tn": (((1,), (1,)), ((0,), (0,)))}[mode]
    return lax.dot_general(x, y, dn, precision=_HI, preferred_element_type=F32)


def _wkv_chunk(r, lw, k, v, a, b):
    hb, C, _ = r.shape
    ti = lax.broadcasted_iota(jnp.int32, (C, C), 0)
    si = lax.broadcasted_iota(jnp.int32, (C, C), 1)
    linc, lstr, eye = (ti >= si).astype(F32), (ti > si).astype(F32), (ti == si).astype(F32)
    lincb = jnp.broadcast_to(linc, (hb, C, C))
    lstrb = jnp.broadcast_to(lstr, (hb, C, C))
    ci = _bmm(lincb, lw, "nn")
    cC = jnp.sum(lw, axis=1, keepdims=True)
    gi, ge, gn, gr = jnp.exp(ci), jnp.exp(ci - lw), jnp.exp(-ci), jnp.exp(cC - ci)
    At, Rt, Bt, Kt, Bh, Kh = a * ge, r * gi, b * gn, k * gn, b * gr, k * gr
    A_ab = _bmm(At, Bt, "nt") * lstr
    A_ak = _bmm(At, Kt, "nt") * lstr
    A_rb = _bmm(Rt, Bt, "nt") * linc
    A_rk = _bmm(Rt, Kt, "nt") * linc
    Tm = eye + A_ab
    Pw = A_ab
    n = 1
    while 2 * n < C:
        Pw = _bmm(Pw, Pw, "nn")
        Tm = Tm + _bmm(Tm, Pw, "nn")
        n *= 2
    ones = jnp.ones_like(v)
    gC = jnp.exp(_bmm(lw, ones, "tn"))
    return dict(gi=gi, ge=ge, gn=gn, gr=gr, At=At, Rt=Rt, Bt=Bt, Kt=Kt, Bh=Bh, Kh=Kh, A_ab=A_ab, A_ak=A_ak, A_rb=A_rb,
                A_rk=A_rk, Tm=Tm, linc=linc, lstr=lstr, lincb=lincb, lstrb=lstrb, gC=gC, ones=ones)


def _wkv_fwd(r, lw, k, v, a, b, name):
    H, T, N = r.shape
    C = min(WKV_CHUNK, T)
    nc = T // C
    hb = _pick(H, (8, 4, 2))

    def body(r_ref, lw_ref, k_ref, v_ref, a_ref, b_ref, y_ref, st_ref, h_ref):
        @pl.when(pl.program_id(1) == 0)
        def _():
            h_ref[...] = jnp.zeros_like(h_ref)

        H0 = h_ref[...]
        st_ref[0] = H0
        vv = v_ref[...]
        q = _wkv_chunk(r_ref[...], lw_ref[...], k_ref[...], vv, a_ref[...], b_ref[...])
        U = _bmm(q["Tm"], _bmm(q["At"], H0, "nn") + _bmm(q["A_ak"], vv, "nn"), "nn")
        y_ref[...] = _bmm(q["Rt"], H0, "nn") + _bmm(q["A_rb"], U, "nn") + _bmm(q["A_rk"], vv, "nn")
        h_ref[...] = q["gC"] * H0 + _bmm(q["Bh"], U, "tn") + _bmm(q["Kh"], vv, "tn")

    blk = pl.BlockSpec((hb, C, N), lambda h, c: (h, c, 0))
    return pl.pallas_call(
        body, name=name, grid=(H // hb, nc), in_specs=[blk] * 6,
        out_specs=[blk, pl.BlockSpec((1, hb, N, N), lambda h, c: (c, h, 0, 0))],
        out_shape=[jax.ShapeDtypeStruct((H, T, N), F32), jax.ShapeDtypeStruct((nc, H, N, N), F32)],
        scratch_shapes=[pltpu.VMEM((hb, N, N), F32)], compiler_params=_params())(r, lw, k, v, a, b)


def _wkv_bwd(r, lw, k, v, a, b, states, dy, name):
    H, T, N = r.shape
    C = min(WKV_CHUNK, T)
    nc = T // C
    hb = _pick(H, (8, 4, 2))

    def body(r_ref, lw_ref, k_ref, v_ref, a_ref, b_ref, st_ref, dy_ref, dr_ref, dlw_ref, dk_ref, dv_ref, da_ref, db_ref, dh_ref):
        @pl.when(pl.program_id(1) == 0)
        def _():
            dh_ref[...] = jnp.zeros_like(dh_ref)

        dHC = dh_ref[...]
        H0 = st_ref[0]
        vv, dY = v_ref[...], dy_ref[...]
        q = _wkv_chunk(r_ref[...], lw_ref[...], k_ref[...], vv, a_ref[...], b_ref[...])
        At, Rt, Bt, Kt, Bh, Kh, gC = q["At"], q["Rt"], q["Bt"], q["Kt"], q["Bh"], q["Kh"], q["gC"]
        U = _bmm(q["Tm"], _bmm(At, H0, "nn") + _bmm(q["A_ak"], vv, "nn"), "nn")
        dU = _bmm(q["A_rb"], dY, "tn") + _bmm(Bh, dHC, "nn")
        dP = _bmm(q["Tm"], dU, "tn")
        dv_ref[...] = _bmm(q["A_rk"], dY, "tn") + _bmm(Kh, dHC, "nn") + _bmm(q["A_ak"], dP, "tn")
        dh_ref[...] = _bmm(Rt, dY, "tn") + gC * dHC + _bmm(At, dP, "tn")
        dA_rb = _bmm(dY, U, "nt") * q["linc"]
        dA_rk = _bmm(dY, vv, "nt") * q["linc"]
        dA_ab = _bmm(dP, U, "nt") * q["lstr"]
        dA_ak = _bmm(dP, vv, "nt") * q["lstr"]
        dRt = _bmm(dY, H0, "nt") + _bmm(dA_rb, Bt, "nn") + _bmm(dA_rk, Kt, "nn")
        dAt = _bmm(dP, H0, "nt") + _bmm(dA_ab, Bt, "nn") + _bmm(dA_ak, Kt, "nn")
        dBt = _bmm(dA_ab, At, "tn") + _bmm(dA_rb, Rt, "tn")
        dKt = _bmm(dA_ak, At, "tn") + _bmm(dA_rk, Rt, "tn")
        dBh = _bmm(U, dHC, "nt")
        dKh = _bmm(vv, dHC, "nt")
        dr_ref[...] = dRt * q["gi"]
        da_ref[...] = dAt * q["ge"]
        db_ref[...] = dBt * q["gn"] + dBh * q["gr"]
        dk_ref[...] = dKt * q["gn"] + dKh * q["gr"]
        tail = dBh * Bh + dKh * Kh
        dci = dRt * Rt - dBt * Bt - dKt * Kt - tail
        dcC = jnp.sum(tail, axis=1, keepdims=True) + _bmm(q["ones"], H0 * dHC * gC, "nt")
        dlw_ref[...] = _bmm(q["lincb"], dci, "tn") + _bmm(q["lstrb"], dAt * At, "tn") + dcC

    blk = pl.BlockSpec((hb, C, N), lambda h, c: (h, nc - 1 - c, 0))
    st = pl.BlockSpec((1, hb, N, N), lambda h, c: (nc - 1 - c, h, 0, 0))
    return pl.pallas_call(
        body, name=name, grid=(H // hb, nc), in_specs=[blk] * 6 + [st, blk], out_specs=[blk] * 6,
        out_shape=[jax.ShapeDtypeStruct((H, T, N), F32)] * 6,
        scratch_shapes=[pltpu.VMEM((hb, N, N), F32)], compiler_params=_params())(r, lw, k, v, a, b, states, dy)


def _sgu_ln(z, SW, lng, lnb):
    ge = _gelu(z)
    u, vv = ge[:, :SW], ge[:, SW:]
    xc = vv - _mean(vv)
    rstd = lax.rsqrt(_mean(xc * xc) + LN_EPS)
    vn = xc * rstd
    return u, vn, rstd, vn * lng + lnb


def _causal(ws_ref, g):
    ti = lax.broadcasted_iota(jnp.int32, (SGU_CHUNK, SGU_CHUNK), 0)
    si = lax.broadcasted_iota(jnp.int32, (SGU_CHUNK, SGU_CHUNK), 1)
    return ti >= si, jnp.where(ti >= si, ws_ref[g], 0.0).astype(BF16)


def _sgu_fwd(z, lng, lnb, ws, bexp, name):
    T, SW = z.shape[0], z.shape[1] // 2
    G = ws.shape[0]
    tr = min(256, T)
    nch = tr // SGU_CHUNK

    def body(z_ref, lng_ref, lnb_ref, ws_ref, be_ref, o_ref):
        u, _, _, vl = _sgu_ln(z_ref[...], SW, lng_ref[...], lnb_ref[...])
        for g in range(G):
            cs = slice(g * SGU_GROUP, (g + 1) * SGU_GROUP)
            _, wc = _causal(ws_ref, g)
            for n in range(nch):
                rs = slice(n * SGU_CHUNK, (n + 1) * SGU_CHUNK)
                m = jnp.dot(wc, vl[rs, cs].astype(BF16), preferred_element_type=F32) + be_ref[:, cs]
                o_ref[rs, cs] = (u[rs, cs] * m).astype(BF16)

    whole = lambda arr: pl.BlockSpec(arr.shape, lambda i, nd=arr.ndim: (0,) * nd)
    return pl.pallas_call(
        body, name=name, grid=(T // tr,),
        in_specs=[pl.BlockSpec((tr, 2 * SW), lambda i: (i, 0)), whole(lng), whole(lnb), whole(ws), whole(bexp)],
        out_specs=pl.BlockSpec((tr, SW), lambda i: (i, 0)), out_shape=jax.ShapeDtypeStruct((T, SW), BF16),
        compiler_params=_params())(z, lng, lnb, ws, bexp)


def _sgu_bwd(z, dyb, lng, lnb, ws, bexp, name):
    T, SW = z.shape[0], z.shape[1] // 2
    G = ws.shape[0]
    tr = min(256, T)
    nch = tr // SGU_CHUNK
    nt = T // tr

    def body(z_ref, dy_ref, lng_ref, lnb_ref, ws_ref, be_ref, dz_ref, dlg_ref, dlb_ref, dws_ref, db_ref, du_s, dvl_s, dbacc_s):
        i = pl.program_id(0)
        zv = z_ref[...]
        lng_v = lng_ref[...]
        u, vn, rstd, vl = _sgu_ln(zv, SW, lng_v, lnb_ref[...])

        @pl.when(i == 0)
        def _():
            for s in (dlg_ref, dlb_ref, dws_ref, dbacc_s):
                s[...] = jnp.zeros_like(s)

        for g in range(G):
            cs = slice(g * SGU_GROUP, (g + 1) * SGU_GROUP)
            tri, wc = _causal(ws_ref, g)
            for n in range(nch):
                rs = slice(n * SGU_CHUNK, (n + 1) * SGU_CHUNK)
                blk = vl[rs, cs].astype(BF16)
                m = jnp.dot(wc, blk, preferred_element_type=F32) + be_ref[:, cs]
                dyv = dy_ref[rs, cs]
                du_s[rs, cs] = dyv * m
                dm = dyv * u[rs, cs]
                dvl_s[rs, cs] = _bdot(wc, dm, "tn")
                dws_ref[g] += jnp.where(tri, _bdot(dm, blk, "nt"), 0.0)
                dbacc_s[:, cs] += dm

        dvl = dvl_s[...]
        dlg_ref[...] += _colsum(dvl * vn)
        dlb_ref[...] += _colsum(dvl)
        dvn = dvl * lng_v
        dvv = rstd * (dvn - _mean(dvn) - vn * _mean(dvn * vn))
        gp = _gelu_grad(zv)
        dz_ref[:, :SW] = (du_s[...] * gp[:, :SW]).astype(BF16)
        dz_ref[:, SW:] = (dvv * gp[:, SW:]).astype(BF16)

        @pl.when(i == nt - 1)
        def _():
            lane = lax.broadcasted_iota(jnp.int32, (SGU_CHUNK, LANE), 1)
            out = jnp.zeros((SGU_CHUNK, LANE), F32)
            for g in range(G):
                col = jnp.sum(dbacc_s[:, g * SGU_GROUP:(g + 1) * SGU_GROUP], axis=1, keepdims=True)
                out = jnp.where(lane == g, col, out)
            db_ref[...] = out

    whole = lambda arr: pl.BlockSpec(arr.shape, lambda i, nd=arr.ndim: (0,) * nd)
    acc_shapes = [(1, SW), (1, SW), ws.shape, (SGU_CHUNK, LANE)]
    return pl.pallas_call(
        body, name=name, grid=(nt,),
        in_specs=[pl.BlockSpec((tr, 2 * SW), lambda i: (i, 0)), pl.BlockSpec((tr, SW), lambda i: (i, 0)),
                  whole(lng), whole(lnb), whole(ws), whole(bexp)],
        out_specs=[pl.BlockSpec((tr, 2 * SW), lambda i: (i, 0))] + [pl.BlockSpec(s, lambda i, nd=len(s): (0,) * nd) for s in acc_shapes],
        out_shape=[jax.ShapeDtypeStruct((T, 2 * SW), BF16)] + [jax.ShapeDtypeStruct(s, F32) for s in acc_shapes],
        scratch_shapes=[pltpu.VMEM((tr, SW), F32), pltpu.VMEM((tr, SW), F32), pltpu.VMEM((SGU_CHUNK, SW), F32)],
        compiler_params=_params())(z, dyb, lng, lnb, ws, bexp)


_HBM = pl.BlockSpec(memory_space=pltpu.HBM)
_SEM = pl.BlockSpec(memory_space=pltpu.SEMAPHORE)
_DATAFLOW = pltpu.SideEffectType.DATAFLOW_SIDE_EFFECTING


def _mesh_place():
    x, y, c = lax.axis_index("x"), lax.axis_index("y"), lax.axis_index("c")
    return x, y, c, 4 * x + 2 * y + c


def _peer(x, y, c, rel):
    px = 1 - x if rel & 4 else x
    py = 1 - y if rel & 2 else y
    pc = 1 - c if rel & 1 else c
    return (px, py, pc), 4 * px + 2 * py + pc


def _exchange_start(groups, name):
    flat = [t for g in groups for t in g]
    sizes = [len(g) for g in groups]
    n, ng = len(flat), len(groups)
    srcs = [pltpu.with_memory_space_constraint(a, pltpu.HBM) for a, _ in flat]
    lands = [pltpu.with_memory_space_constraint(lax.empty(((N_DEV,) + a.shape) if isg else a.shape, a.dtype), pltpu.HBM)
             for a, isg in flat]

    def body(*refs):
        ins, lnd, sems, token = refs[:n], refs[n:2 * n], refs[2 * n:2 * n + 2 * ng], refs[-1]
        x, y, c, me = _mesh_place()
        j0 = 0
        for gi, sz in enumerate(sizes):
            for rel in range(1, N_DEV):
                dev, slot = _peer(x, y, c, rel)
                for jj in range(sz):
                    j = j0 + jj
                    pltpu.make_async_remote_copy(
                        src_ref=ins[j] if flat[j][1] else ins[j].at[slot], dst_ref=lnd[j].at[me],
                        send_sem=sems[2 * gi].at[jj * (N_DEV - 1) + rel - 1], recv_sem=sems[2 * gi + 1].at[jj * (N_DEV - 1) + rel - 1],
                        device_id=dev, device_id_type=pl.DeviceIdType.MESH).start()
            j0 += sz
        token[...] = jnp.zeros_like(token)

    sem_shapes = [pltpu.SemaphoreType.DMA((sz * (N_DEV - 1),)) for sz in sizes for _ in range(2)]
    res = pl.pallas_call(
        body, name=name,
        out_shape=(*sem_shapes, *[pltpu.HBM(a.shape, a.dtype) for a in srcs], *[pltpu.HBM(a.shape, a.dtype) for a in lands],
                   jax.ShapeDtypeStruct((SUBLANE, LANE), F32)),
        in_specs=[_HBM] * (2 * n), out_specs=(*[_SEM] * (2 * ng), *[_HBM] * (2 * n), pl.BlockSpec(memory_space=pltpu.VMEM)),
        input_output_aliases={i: 2 * ng + i for i in range(2 * n)},
        compiler_params=pltpu.CompilerParams(has_side_effects=_DATAFLOW))(*srcs, *lands)
    sems, thru, token = res[:2 * ng], res[2 * ng:2 * ng + 2 * n], res[-1]
    handle, j0 = [], 0
    for gi, sz in enumerate(sizes):
        handle.append(dict(kinds=[k for _, k in groups[gi]], srcs=list(thru[j0:j0 + sz]), lands=list(thru[n + j0:n + j0 + sz]),
                           ssem=sems[2 * gi], rsem=sems[2 * gi + 1]))
        j0 += sz
    return handle, token


def _exchange_wait(group, after, name):
    kinds, sz = group["kinds"], len(group["kinds"])

    def body(*refs):
        ins, lnd, ssem, rsem = refs[:sz], refs[sz:2 * sz], refs[2 * sz], refs[2 * sz + 1]
        x, y, c, _ = _mesh_place()
        for rel in range(1, N_DEV):
            dev, slot = _peer(x, y, c, rel)
            for jj in range(sz):
                cp = pltpu.make_async_remote_copy(
                    src_ref=ins[jj] if kinds[jj] else ins[jj].at[slot], dst_ref=lnd[jj].at[slot],
                    send_sem=ssem.at[jj * (N_DEV - 1) + rel - 1], recv_sem=rsem.at[jj * (N_DEV - 1) + rel - 1],
                    device_id=dev, device_id_type=pl.DeviceIdType.MESH)
                cp.wait_send()
                cp.wait_recv()

    arrays = group["srcs"] + group["lands"]
    res = pl.pallas_call(
        body, name=name, out_shape=[pltpu.HBM(a.shape, a.dtype) for a in arrays],
        in_specs=[_HBM] * (2 * sz) + [_SEM, _SEM, pl.BlockSpec(memory_space=pl.ANY)], out_specs=[_HBM] * (2 * sz),
        input_output_aliases={i: i for i in range(2 * sz)},
        compiler_params=pltpu.CompilerParams(has_side_effects=_DATAFLOW))(*arrays, group["ssem"], group["rsem"], after)
    srcs, lands = res[:sz], res[sz:]

    def place(*refs):
        ins, out, sem = refs[:sz], refs[2 * sz:3 * sz], refs[-1]
        me = _mesh_place()[3]
        cps = [pltpu.make_async_copy(ins[jj] if kinds[jj] else ins[jj].at[me], out[jj].at[me], sem.at[jj]) for jj in range(sz)]
        for cp in cps:
            cp.start()
        for cp in cps:
            cp.wait()

    return list(pl.pallas_call(
        place, name=name + "_own", out_shape=[jax.ShapeDtypeStruct(a.shape, a.dtype) for a in lands],
        in_specs=[_HBM] * (2 * sz), out_specs=[_HBM] * sz, input_output_aliases={sz + i: i for i in range(sz)},
        scratch_shapes=[pltpu.SemaphoreType.DMA((sz,))])(*srcs, *lands))


def _adamw(w, m, v, gparts, name):
    R, C = w.shape
    tm = _pick(R, (256, 128, 64, 32, 16, 8))

    def body(w_ref, m_ref, v_ref, g_ref, go, do, mo, vo):
        g = g_ref[0].astype(F32)
        for j in range(1, N_DEV):
            g = g + g_ref[j].astype(F32)
        mn = ADAM_B1 * m_ref[...] + (1.0 - ADAM_B1) * g
        vn = ADAM_B2 * v_ref[...] + (1.0 - ADAM_B2) * (g * g)
        m_hat = mn / (1.0 - ADAM_B1 ** ADAM_STEP)
        v_hat = vn / (1.0 - ADAM_B2 ** ADAM_STEP)
        go[...] = g
        do[...] = -ADAM_LR * (m_hat / (jnp.sqrt(v_hat) + ADAM_EPS) + ADAM_WD * w_ref[...])
        mo[...] = mn
        vo[...] = vn

    row = pl.BlockSpec((tm, C), lambda i: (i, 0))
    return pl.pallas_call(
        body, name=name, grid=(R // tm,), in_specs=[row, row, row, pl.BlockSpec((N_DEV, tm, C), lambda i: (0, i, 0))],
        out_specs=[row] * 4, out_shape=[jax.ShapeDtypeStruct((R, C), F32)] * 4, compiler_params=_params())(w, m, v, gparts)


def _pack(arrays):
    flat = []
    for a in arrays:
        f = a.reshape(-1)
        pad = _ceil_to(f.shape[0], LANE) - f.shape[0]
        flat.append(jnp.concatenate([f, jnp.zeros((pad,), f.dtype)]) if pad else f)
    buf = jnp.concatenate(flat)
    rows = _ceil_to(buf.shape[0] // LANE, 64)
    buf = jnp.concatenate([buf, jnp.zeros((rows * LANE - buf.shape[0],), buf.dtype)])
    return buf.reshape(rows, LANE)


def _unpack(buf, shapes):
    flat, out, off = buf.reshape(-1), [], 0
    for s in shapes:
        size = 1
        for d in s:
            size *= d
        out.append(flat[off:off + size].reshape(s))
        off += _ceil_to(size, LANE)
    return out


def kernel(x, norm_mix_g, w_in, shift_mu, w0, w_lora_up, a0, a_lora_up, g_lora_up, k_k, k_a, r_k, lnx_g, lnx_b, w_proj_rwkv, sgu_ln_g, sgu_ln_b, sgu_w, sgu_b, w_proj_sgu, w_out, norm_ffn_g, w_ffn_gate, w_ffn_up, w_ffn_down, norm_final_g, loss_target, m_norm_mix_g, m_w_in, m_shift_mu, m_w0, m_w_lora_up, m_a0, m_a_lora_up, m_g_lora_up, m_k_k, m_k_a, m_r_k, m_lnx_g, m_lnx_b, m_w_proj_rwkv, m_sgu_ln_g, m_sgu_ln_b, m_sgu_w, m_sgu_b, m_w_proj_sgu, m_w_out, m_norm_ffn_g, m_w_ffn_gate, m_w_ffn_up, m_w_ffn_down, m_norm_final_g, v_norm_mix_g, v_w_in, v_shift_mu, v_w0, v_w_lora_up, v_a0, v_a_lora_up, v_g_lora_up, v_k_k, v_k_a, v_r_k, v_lnx_g, v_lnx_b, v_w_proj_rwkv, v_sgu_ln_g, v_sgu_ln_b, v_sgu_w, v_sgu_b, v_w_proj_sgu, v_w_out, v_norm_ffn_g, v_w_ffn_gate, v_w_ffn_up, v_w_ffn_down, v_norm_final_g):
    weights = dict(norm_mix_g=norm_mix_g, w_in=w_in, shift_mu=shift_mu, w0=w0, w_lora_up=w_lora_up, a0=a0, a_lora_up=a_lora_up,
                   g_lora_up=g_lora_up, k_k=k_k, k_a=k_a, r_k=r_k, lnx_g=lnx_g, lnx_b=lnx_b, w_proj_rwkv=w_proj_rwkv,
                   sgu_ln_g=sgu_ln_g, sgu_ln_b=sgu_ln_b, sgu_w=sgu_w, sgu_b=sgu_b, w_proj_sgu=w_proj_sgu, w_out=w_out,
                   norm_ffn_g=norm_ffn_g, w_ffn_gate=w_ffn_gate, w_ffn_up=w_ffn_up, w_ffn_down=w_ffn_down, norm_final_g=norm_final_g)
    m_in = dict(norm_mix_g=m_norm_mix_g, w_in=m_w_in, shift_mu=m_shift_mu, w0=m_w0, w_lora_up=m_w_lora_up, a0=m_a0,
                a_lora_up=m_a_lora_up, g_lora_up=m_g_lora_up, k_k=m_k_k, k_a=m_k_a, r_k=m_r_k, lnx_g=m_lnx_g, lnx_b=m_lnx_b,
                w_proj_rwkv=m_w_proj_rwkv, sgu_ln_g=m_sgu_ln_g, sgu_ln_b=m_sgu_ln_b, sgu_w=m_sgu_w, sgu_b=m_sgu_b,
                w_proj_sgu=m_w_proj_sgu, w_out=m_w_out, norm_ffn_g=m_norm_ffn_g, w_ffn_gate=m_w_ffn_gate, w_ffn_up=m_w_ffn_up,
                w_ffn_down=m_w_ffn_down, norm_final_g=m_norm_final_g)
    v_in = dict(norm_mix_g=v_norm_mix_g, w_in=v_w_in, shift_mu=v_shift_mu, w0=v_w0, w_lora_up=v_w_lora_up, a0=v_a0,
                a_lora_up=v_a_lora_up, g_lora_up=v_g_lora_up, k_k=v_k_k, k_a=v_k_a, r_k=v_r_k, lnx_g=v_lnx_g, lnx_b=v_lnx_b,
                w_proj_rwkv=v_w_proj_rwkv, sgu_ln_g=v_sgu_ln_g, sgu_ln_b=v_sgu_ln_b, sgu_w=v_sgu_w, sgu_b=v_sgu_b,
                w_proj_sgu=v_w_proj_sgu, w_out=v_w_out, norm_ffn_g=v_norm_ffn_g, w_ffn_gate=v_w_ffn_gate, w_ffn_up=v_w_ffn_up,
                w_ffn_down=v_w_ffn_down, norm_final_g=v_norm_final_g)
    names = list(weights)
    col_sharded = ("w_in", "w_lora_up", "a_lora_up", "g_lora_up", "w_proj_rwkv", "w_proj_sgu", "w_ffn_gate", "w_ffn_up")
    row_sharded = ("w_out", "w_ffn_down")
    sharded = [n for n in names if n in col_sharded or n in row_sharded]
    small = [n for n in names if n not in sharded]

    xs, tgt = x[0], loss_target[0]
    T, D = xs.shape
    RW = w0.shape[1]
    H = RW // HEAD
    SW = sgu_ln_g.shape[1]
    G = sgu_w.shape[1]
    lay = _rwkv_layout(RW, w_lora_up.shape[1], a_lora_up.shape[1], g_lora_up.shape[1])
    _, pw, _, rcp = lay

    gather_groups = [["w_in", "w_lora_up", "a_lora_up", "g_lora_up"], ["w_proj_rwkv", "w_proj_sgu", "w_out"],
                     ["w_ffn_gate", "w_ffn_up", "w_ffn_down"]]
    gather, gather_token = _exchange_start([[(weights[n][0].astype(BF16), True) for n in grp] for grp in gather_groups], "gather_start")
    full = {}

    def take_weights(gi, after, name):
        for n, g in zip(gather_groups[gi], _exchange_wait(gather[gi], after, name)):
            full[n] = g.transpose(1, 0, 2).reshape(g.shape[1], -1) if n in col_sharded else g.reshape(-1, g.shape[2])

    n1 = _rms_fwd(xs, norm_mix_g, "rms_mix", deps=[gather_token])
    take_weights(0, n1, "gather_wait_in")
    W_in = _pad_rwkv_cols(full["w_in"], lay)
    o_z, o_ga, o_gb = rcp, rcp + 2 * SW, rcp + 2 * SW + D
    W_r, W_z, W_ga, W_gb = W_in[:, :o_z], W_in[:, o_z:o_ga], W_in[:, o_ga:o_gb], W_in[:, o_gb:]
    lora = [_pad_rows(full["w_lora_up"], pw[3]), _pad_rows(full["a_lora_up"], pw[4]), _pad_rows(full["g_lora_up"], pw[5])]
    mu_p = _pad_rwkv_cols(shift_mu, lay)
    rsmall = [w0, a0, k_k, k_a]
    hp = [lnx_g.reshape(H, 1, HEAD), lnx_b.reshape(H, 1, HEAD), r_k.reshape(H, 1, HEAD)]
    ws = sgu_w[0]
    bexp = jnp.repeat(sgu_b[0].T, SGU_GROUP, axis=1)
    gf = norm_final_g.reshape(1, D)

    p = _matmul(n1, W_r, mode="nn", out_dtype=F32, name="proj_rwkv")
    z = _matmul(n1, W_z, mode="nn", out_dtype=F32, name="proj_sgu")
    ga = _matmul(n1, W_ga, mode="nn", out_dtype=F32, name="proj_gate_a")
    gb = _matmul(n1, W_gb, mode="nn", out_dtype=F32, name="proj_gate_b")
    r_t, lw_t, k2_t, v_t, kkp_t, a_t, g_t = _rwkv_pre(p, mu_p, rsmall, lora, lay, "rwkv_pre")
    r_h, lw_h, k2_h, v_h, kkp_h, a_h, g_h = (_to_heads(t, H) for t in (r_t, lw_t, k2_t, v_t, kkp_t, a_t, g_t))
    aa_h, bb_h = _head_pre(kkp_h, a_h, "head_pre")
    h3 = lambda t: t.reshape(H, T, HEAD)
    wkv_in = [h3(t) for t in (r_h, lw_h, k2_h, v_h, aa_h, bb_h)]
    y_h, states = _wkv_fwd(*wkv_in, "wkv_fwd")
    y_h = y_h.reshape(H * T, HEAD)
    ya = _from_heads(_head_post(y_h, r_h, k2_h, v_h, g_h, hp, H, "head_post"), H)
    yb = _sgu_fwd(z, sgu_ln_g, sgu_ln_b, ws, bexp, "sgu_fwd")
    take_weights(1, ya, "gather_wait_proj")
    pa = _matmul(ya, full["w_proj_rwkv"], mode="nn", out_dtype=F32, name="proj_a")
    pb = _matmul(yb, full["w_proj_sgu"], mode="nn", out_dtype=F32, name="proj_b")

    def merge_fn(rv, pv, hv):
        ga_v, gb_v, pa_v, pb_v = rv
        return [_sigmoid(ga_v) * pa_v + _sigmoid(gb_v) * pb_v], [], []
    merged = _rowwise(merge_fn, [ga, gb, pa, pb], [], [(D, BF16)], [], name="merge")[0]
    h1 = _matmul(merged, full["w_out"], mode="nn", out_dtype=F32, name="out_proj", add=xs)
    n2 = _rms_fwd(h1, norm_ffn_g, "rms_ffn")
    take_weights(2, n2, "gather_wait_ffn")
    gt = _matmul(n2, full["w_ffn_gate"], mode="nn", out_dtype=F32, name="ffn_gate")
    up = _matmul(n2, full["w_ffn_up"], mode="nn", out_dtype=F32, name="ffn_up")

    def act_fn(rv, pv, hv):
        gt_v, up_v = rv
        return [gt_v * _sigmoid(gt_v) * up_v], [], []
    act = _rowwise(act_fn, [gt, up], [], [(gt.shape[1], BF16)], [], name="ffn_act")[0]
    h2 = _matmul(act, full["w_ffn_down"], mode="nn", out_dtype=F32, name="ffn_down", add=h1)

    def final_fn(rv, pv, hv):
        (h_v, t_v), (g_v,) = rv, pv
        r = lax.rsqrt(_mean(h_v * h_v) + RMS_EPS)
        yn = h_v * r
        e = yn * g_v - t_v
        loss = 0.5 * jnp.sum(_mean(e * e))
        dout = e * (1.0 / D)
        dyg = dout * g_v
        dh = r * (dyg - yn * _mean(dyg * yn))
        return [dh, dh], [jnp.full((1, LANE), loss, F32), _colsum(dout * yn)], []
    dh2, dh2_bf, loss_part, d_gf = _rowwise(final_fn, [h2, tgt], [gf], [(D, F32), (D, BF16)], [(1, LANE), (1, D)], name="final_loss")

    grads = {}

    def start_scatter(group, name, extra=()):
        blocks = []
        for n in group:
            g = grads[n]
            blocks.append((g.reshape(g.shape[0], N_DEV, -1).transpose(1, 0, 2) if n in col_sharded
                           else g.reshape(N_DEV, -1, g.shape[1]), False))
        (handle,), token = _exchange_start([blocks + list(extra)], name)
        return handle, token

    dact = _matmul(dh2_bf, full["w_ffn_down"], mode="nt", out_dtype=F32, name="d_act")
    grads["w_ffn_down"] = _matmul(act, dh2_bf, mode="tn", out_dtype=BF16, name="dw_ffn_down")

    def dact_fn(rv, pv, hv):
        d_v, gt_v, up_v = rv
        s = _sigmoid(gt_v)
        return [d_v * up_v * (s * (1.0 + gt_v * (1.0 - s))), d_v * gt_v * s], [], []
    dgt, dup = _rowwise(dact_fn, [dact, gt, up], [], [(gt.shape[1], BF16)] * 2, [], name="d_ffn_act")
    dn2 = _matmul(dgt, full["w_ffn_gate"], mode="nt", out_dtype=F32, name="dn2_gate")
    dn2 = _matmul(dup, full["w_ffn_up"], mode="nt", out_dtype=F32, name="dn2_up", add=dn2)
    grads["w_ffn_gate"] = _matmul(n2, dgt, mode="tn", out_dtype=BF16, name="dw_ffn_gate")
    grads["w_ffn_up"] = _matmul(n2, dup, mode="tn", out_dtype=BF16, name="dw_ffn_up")
    scatter_groups = [["w_ffn_down", "w_ffn_gate", "w_ffn_up"], ["w_out", "w_proj_rwkv", "w_proj_sgu"],
                      ["w_in", "w_lora_up", "a_lora_up", "g_lora_up"]]
    scatter_ffn, token_ffn = start_scatter(scatter_groups[0], "scatter_start_ffn")
    dh1, dh1_bf, d_g2 = _rms_bwd(dn2, h1, dh2, norm_ffn_g, "rms_ffn_bwd", deps=[token_ffn])
    dmerged = _matmul(dh1_bf, full["w_out"], mode="nt", out_dtype=F32, name="d_merged")
    grads["w_out"] = _matmul(merged, dh1_bf, mode="tn", out_dtype=BF16, name="dw_out")

    def dmerge_fn(rv, pv, hv):
        d_v, ga_v, gb_v, pa_v, pb_v = rv
        sa, sb = _sigmoid(ga_v), _sigmoid(gb_v)
        return [d_v * pa_v * sa * (1.0 - sa), d_v * pb_v * sb * (1.0 - sb), d_v * sa, d_v * sb], [], []
    dga, dgb, dpa, dpb = _rowwise(dmerge_fn, [dmerged, ga, gb, pa, pb], [], [(D, BF16)] * 4, [], name="d_merge")
    dya = _matmul(dpa, full["w_proj_rwkv"], mode="nt", out_dtype=F32, name="d_ya")
    dyb = _matmul(dpb, full["w_proj_sgu"], mode="nt", out_dtype=F32, name="d_yb")
    grads["w_proj_rwkv"] = _matmul(ya, dpa, mode="tn", out_dtype=BF16, name="dw_proj_a")
    grads["w_proj_sgu"] = _matmul(yb, dpb, mode="tn", out_dtype=BF16, name="dw_proj_b")
    scatter_mid, token_mid = start_scatter(scatter_groups[1], "scatter_start_mid")
    dz, d_lng, d_lnb, d_ws, d_bs = _sgu_bwd(z, dyb, sgu_ln_g, sgu_ln_b, ws, bexp, "sgu_bwd")

    dy_h, dr1, dk1, dv1, dg_h, d_lnxg, d_lnxb, d_rk = _head_post_bwd(_to_heads(dya, H), y_h, r_h, k2_h, v_h, g_h, hp, H,
                                                                     "head_post_bwd", deps=[token_mid])
    dr2, dlw_h, dk2b, dv2, daa, dbb = (t.reshape(H * T, HEAD) for t in _wkv_bwd(*wkv_in, states, h3(dy_h), "wkv_bwd"))
    dkkp_h, da_h, dr_h, dk2_h, dv_h = _head_pre_bwd(kkp_h, a_h, daa, dbb, [dr1, dr2, dk1, dk2b, dv1, dv2], "head_pre_bwd")
    tok = [_from_heads(t, H) for t in (dr_h, dlw_h, dk2_h, dv_h, dkkp_h, da_h, dg_h)]
    dps, d_mu, d_w0, d_a0, d_kk, d_ka, d_wlw, d_wla, d_wlg = _rwkv_pre_bwd(p, mu_p, rsmall, lora, tok, lay, "rwkv_pre_bwd")
    dp = _shift_bwd(dps, mu_p, "shift_bwd")
    dproj = jnp.concatenate([dp, dz, dga, dgb], axis=1)
    dn1 = _matmul(dproj, W_in, mode="nt", out_dtype=F32, name="dn1")
    grads["w_in"] = _unpad_rwkv_cols(_matmul(n1, dproj, mode="tn", out_dtype=BF16, name="dw_in"), lay)
    dx, _, d_g1 = _rms_bwd(dn1, xs, dh1, norm_mix_g, "rms_mix_bwd")
    grads["w_lora_up"] = d_wlw[:w_lora_up.shape[1]].astype(BF16)
    grads["a_lora_up"] = d_wla[:a_lora_up.shape[1]].astype(BF16)
    grads["g_lora_up"] = d_wlg[:g_lora_up.shape[1]].astype(BF16)
    small_grads = dict(norm_mix_g=d_g1, shift_mu=_unpad_rwkv_cols(d_mu, lay), w0=d_w0, a0=d_a0, k_k=d_kk, k_a=d_ka, r_k=d_rk,
                       lnx_g=d_lnxg, lnx_b=d_lnxb, sgu_ln_g=d_lng, sgu_ln_b=d_lnb, sgu_w=d_ws, sgu_b=d_bs[:, :G].T,
                       norm_ffn_g=d_g2, norm_final_g=d_gf)

    scatter_in, token_in = start_scatter(scatter_groups[2], "scatter_start_in", extra=[(_pack([small_grads[n] for n in small]), True)])
    out = {}
    after = token_in
    small_parts = None
    for group, handle, name in zip(scatter_groups, (scatter_ffn, scatter_mid, scatter_in), ("ffn", "mid", "in")):
        parts = _exchange_wait(handle, after, "scatter_wait_" + name)
        for n, part in zip(group, parts):
            shp = weights[n].shape
            res = _adamw(weights[n][0], m_in[n][0], v_in[n][0], part, "adamw_" + n)
            out[n] = [t.reshape(shp) for t in res]
            after = res[0]
        small_parts = parts[-1]
    packed = [_pack([d[n] for n in small]) for d in (weights, m_in, v_in)]
    res = _adamw(*packed, small_parts, "adamw_small")
    unpacked = [_unpack(t, [weights[n].shape for n in small]) for t in res]
    for i, n in enumerate(small):
        out[n] = [u[i] for u in unpacked]

    loss = lax.psum(loss_part[0, 0], ("x", "y", "c"))
    return (loss, dx[None], *[out[n][0] for n in names], *[out[n][1] for n in names],
            *[out[n][2] for n in names], *[out[n][3] for n in names])
```

```python
import jax
import jax.numpy as jnp
from jax import lax
from jax.experimental import pallas as pl
from jax.experimental.pallas import tpu as pltpu

F32 = jnp.float32
BF16 = jnp.bfloat16

N_DEV = 8
LANE = 128
SUBLANE = 8
HEAD = 64
SGU_CHUNK = 128
SGU_GROUP = 128
WKV_CHUNK = 64
RMS_EPS = 1e-6
LN_EPS = 1e-5
LNX_EPS = 64e-5
ADAM_LR, ADAM_B1, ADAM_B2, ADAM_EPS, ADAM_WD, ADAM_STEP = 0.001, 0.9, 0.999, 1e-08, 0.01, 10
VMEM_LIMIT_BYTES = 48 * 1024 * 1024
_SQRT_HALF = 0.7071067811865476
_INV_SQRT_2PI = 0.3989422804014327


def _pick(n, cands):
    for c in cands:
        if n % c == 0:
            return c
    return n


def _ceil_to(n, m):
    return -(-n // m) * m


def _params():
    return pltpu.CompilerParams(vmem_limit_bytes=VMEM_LIMIT_BYTES)


def _tile(n, cap):
    best = 0
    for d in range(LANE, min(n, cap) + 1, LANE):
        if n % d == 0:
            best = d
    return best or n


def _matmul_tiles(M, N, K, a_bytes, b_bytes, o_bytes, has_add):
    tm, tn, tk = _tile(M, 1024), _tile(N, 1024), _tile(K, 2048)

    def vmem(tm, tn, tk):
        acc = tm * tn * 4 if tk < K else 0
        return 2 * (tm * tk * a_bytes + tk * tn * b_bytes + tm * tn * (o_bytes + (4 if has_add else 0))) + acc

    while vmem(tm, tn, tk) > (VMEM_LIMIT_BYTES * 3) // 4:
        if tk > 512 and _tile(K, tk // 2) < tk:
            tk = _tile(K, tk // 2)
        elif _tile(M, tm // 2) < tm:
            tm = _tile(M, tm // 2)
        else:
            break
    return tm, tn, tk


def _matmul(a, b, *, mode, out_dtype, name, add=None, deps=()):
    if mode == "nn":
        (M, K), (K2, N) = a.shape, b.shape
    elif mode == "nt":
        (M, K), (N, K2) = a.shape, b.shape
    else:
        (K, M), (K2, N) = a.shape, b.shape
    assert K == K2, (a.shape, b.shape, mode)
    has_add = add is not None
    tm, tn, tk = _matmul_tiles(M, N, K, a.dtype.itemsize, b.dtype.itemsize, jnp.dtype(out_dtype).itemsize, has_add)
    nk = K // tk
    dn = {"nn": (((1,), (0,)), ((), ())), "nt": (((1,), (1,)), ((), ())), "tn": (((0,), (0,)), ((), ()))}[mode]
    a_spec = pl.BlockSpec((tk, tm), lambda i, j, k: (k, i)) if mode == "tn" else pl.BlockSpec((tm, tk), lambda i, j, k: (i, k))
    b_spec = pl.BlockSpec((tn, tk), lambda i, j, k: (j, k)) if mode == "nt" else pl.BlockSpec((tk, tn), lambda i, j, k: (k, j))
    o_spec = pl.BlockSpec((tm, tn), lambda i, j, k: (i, j))
    n_in = 2 + has_add + len(deps)

    def body(*refs):
        a_ref, b_ref = refs[0], refs[1]
        add_ref = refs[2] if has_add else None
        o_ref = refs[n_in]
        part = lax.dot_general(a_ref[...].astype(BF16), b_ref[...].astype(BF16), dn, preferred_element_type=F32)
        if nk == 1:
            if has_add:
                part = part + add_ref[...]
            o_ref[...] = part.astype(out_dtype)
            return
        acc_ref = refs[-1]
        kk = pl.program_id(2)

        @pl.when(kk == 0)
        def _():
            acc_ref[...] = part + add_ref[...] if has_add else part

        @pl.when(kk > 0)
        def _():
            acc_ref[...] += part

        @pl.when(kk == nk - 1)
        def _():
            o_ref[...] = acc_ref[...].astype(out_dtype)

    ins = [a, b] + ([add] if has_add else []) + list(deps)
    in_specs = ([a_spec, b_spec] + ([o_spec] if has_add else [])
                + [pl.BlockSpec(d.shape, lambda i, j, k, nd=d.ndim: (0,) * nd) for d in deps])
    return pl.pallas_call(
        body, name=name, grid=(M // tm, N // tn, nk), in_specs=in_specs, out_specs=o_spec,
        out_shape=jax.ShapeDtypeStruct((M, N), out_dtype),
        scratch_shapes=[pltpu.VMEM((tm, tn), F32)] if nk > 1 else [],
        compiler_params=_params())(*ins)


def _rowwise(fn, rows, pars, row_outs, acc_outs, *, name, tm=256, deps=()):
    R = rows[0].shape[0]
    if max(r.shape[1] for r in rows) > 4096:
        tm = tm // 2
    tm = min(tm, R)
    assert R % tm == 0
    nr, npar = len(rows), len(pars)
    nro = len(row_outs)

    def body(*refs):
        rv = [r[...] for r in refs[:nr]]
        pv = [p[...] for p in refs[nr:nr + npar]]
        outs = refs[nr + npar + len(deps):]
        ro, ao = fn(rv, pv)
        first = pl.program_id(0) == 0
        for o_ref, val in zip(outs[:nro], ro):
            o_ref[...] = val.astype(o_ref.dtype)

        @pl.when(first)
        def _():
            for o_ref, val in zip(outs[nro:], ao):
                o_ref[...] = val

        @pl.when(jnp.logical_not(first))
        def _():
            for o_ref, val in zip(outs[nro:], ao):
                o_ref[...] += val

    in_specs = ([pl.BlockSpec((tm, r.shape[1]), lambda i: (i, 0)) for r in rows]
                + [pl.BlockSpec(p.shape, lambda i, nd=p.ndim: (0,) * nd) for p in list(pars) + list(deps)])
    out_shape = [jax.ShapeDtypeStruct((R, f), dt) for f, dt in row_outs] + [jax.ShapeDtypeStruct(s, F32) for s in acc_outs]
    out_specs = ([pl.BlockSpec((tm, f), lambda i: (i, 0)) for f, _ in row_outs]
                 + [pl.BlockSpec(s, lambda i, nd=len(s): (0,) * nd) for s in acc_outs])
    res = pl.pallas_call(body, name=name, grid=(R // tm,), in_specs=in_specs, out_specs=out_specs, out_shape=out_shape,
                         compiler_params=_params())(*rows, *pars, *deps)
    return list(res)


def _bdot(a, b, mode="nn"):
    dn = {"nn": (((1,), (0,)), ((), ())), "nt": (((1,), (1,)), ((), ())), "tn": (((0,), (0,)), ((), ()))}[mode]
    return lax.dot_general(a.astype(BF16), b.astype(BF16), dn, preferred_element_type=F32)


def _sigmoid(x):
    return jax.nn.sigmoid(x)


def _softplus(x):
    return jnp.maximum(x, 0.0) + jnp.log1p(jnp.exp(-jnp.abs(x)))


def _gelu(z):
    return 0.5 * z * (1.0 + lax.erf(z * _SQRT_HALF))


def _gelu_grad(z):
    return 0.5 * (1.0 + lax.erf(z * _SQRT_HALF)) + z * jnp.exp(-0.5 * z * z) * _INV_SQRT_2PI


def _mean(x):
    return jnp.mean(x, axis=-1, keepdims=True)


def _colsum(x):
    return jnp.sum(x, axis=0, keepdims=True)


def _rms_fwd(x, g, name, deps=()):
    def fn(rv, pv):
        (xv,), (gv,) = rv, pv
        r = lax.rsqrt(_mean(xv * xv) + RMS_EPS)
        return [xv * r * gv], []
    return _rowwise(fn, [x], [g], [(x.shape[1], BF16)], [], name=name, deps=deps)[0]


def _rms_bwd(dn, x, dres, g, name, deps=()):
    def fn(rv, pv):
        (dnv, xv, drv), (gv,) = rv, pv
        r = lax.rsqrt(_mean(xv * xv) + RMS_EPS)
        yn = xv * r
        dyg = dnv * gv
        dx = drv + r * (dyg - yn * _mean(dyg * yn))
        return [dx, dx], [_colsum(dnv * yn)]
    D = x.shape[1]
    return _rowwise(fn, [dn, x, dres], [g], [(D, F32), (D, BF16)], [(1, D)], name=name, deps=deps)


def _rwkv_layout(RW, Lw, La, Lg):
    widths = [RW, RW, RW, Lw, La, Lg]
    pw = [_ceil_to(w, LANE) for w in widths]
    offs = [sum(pw[:i]) for i in range(6)]
    return widths, pw, offs, sum(pw)


def _pad_rwkv_cols(a, lay):
    widths, pw, _, _ = lay
    pieces, src = [], 0
    for w, p in zip(widths, pw):
        pieces.append(a[:, src:src + w])
        if p > w:
            pieces.append(jnp.zeros((a.shape[0], p - w), a.dtype))
        src += w
    pieces.append(a[:, src:])
    return jnp.concatenate(pieces, axis=1)


def _unpad_rwkv_cols(a, lay):
    widths, _, offs, rcp = lay
    return jnp.concatenate([a[:, o:o + w] for o, w in zip(offs, widths)] + [a[:, rcp:]], axis=1)


def _pad_rows(a, rows):
    return a if a.shape[0] == rows else jnp.concatenate([a, jnp.zeros((rows - a.shape[0], a.shape[1]), a.dtype)], axis=0)


def _token_shift(p, halo, mu, i):
    tm = p.shape[0]
    hid = lax.broadcasted_iota(jnp.int32, (SUBLANE, 1), 0)
    before = jnp.sum(jnp.where(hid == SUBLANE - 1, halo, 0.0), axis=0, keepdims=True)
    before = jnp.where(i == 0, 0.0, before)
    rid = lax.broadcasted_iota(jnp.int32, (tm, 1), 0)
    prev = jnp.where(rid == 0, before, pltpu.roll(p, 1, 0))
    d = prev - p
    return p + d * mu, d


def _rwkv_math(ps, w0, a0, k_k, k_a, wlw, wla, wlg, lay):
    _, pw, offs, _ = lay
    r, k, v, xw, xa, xg = (ps[:, offs[j]:offs[j] + pw[j]] for j in range(6))
    tw = jnp.tanh(xw)
    ww = w0 + _bdot(tw, wlw)
    lw = -jnp.exp(-_softplus(-ww) - 0.5)
    a = _sigmoid(a0 + _bdot(xa, wla))
    sg = _sigmoid(xg)
    g = _bdot(sg, wlg)
    return dict(r=r, k=k, v=v, xa=xa, tw=tw, ww=ww, lw=lw, a=a, sg=sg, g=g, kkp=k * k_k, k2=k * (1.0 + (a - 1.0) * k_a))


def _halo_specs(T, tm, width, after):
    hb = tm // SUBLANE
    last = T // SUBLANE - 1
    if after:
        return pl.BlockSpec((SUBLANE, width), lambda i: (jnp.minimum((i + 1) * hb, last), 0))
    return pl.BlockSpec((SUBLANE, width), lambda i: (jnp.maximum(i * hb - 1, 0), 0))


def _rowsum(x):
    return jnp.sum(x, axis=-1, keepdims=True)


def _kk_math(kkp):
    nrm = jnp.sqrt(_rowsum(kkp * kkp))
    inv = 1.0 / jnp.maximum(nrm, 1e-12)
    return nrm, inv, kkp * inv


def _rwkv_pre(p, mu, small, lora, lay, name):
    T, rcp = p.shape
    H = lay[0][0] // HEAD
    tm = min(128, T)

    def body(p_ref, ph_ref, mu_ref, w0_ref, a0_ref, kk_ref, ka_ref, wlw_ref, wla_ref, wlg_ref, r_o, lw_o, k2_o, v_o, aa_o, bb_o, g_o):
        ps, _ = _token_shift(p_ref[...], ph_ref[...], mu_ref[...], pl.program_id(0))
        q = _rwkv_math(ps, w0_ref[...], a0_ref[...], kk_ref[...], ka_ref[...], wlw_ref[...], wla_ref[...], wlg_ref[...], lay)
        for h in range(H):
            sl = slice(h * HEAD, (h + 1) * HEAD)
            for o_ref, key in ((r_o, "r"), (lw_o, "lw"), (k2_o, "k2"), (v_o, "v"), (g_o, "g")):
                o_ref[h] = q[key][:, sl]
            _, _, kk = _kk_math(q["kkp"][:, sl])
            aa_o[h] = -kk
            bb_o[h] = kk * q["a"][:, sl]

    whole = lambda arr: pl.BlockSpec(arr.shape, lambda i: (0, 0))
    return pl.pallas_call(
        body, name=name, grid=(T // tm,),
        in_specs=([pl.BlockSpec((tm, rcp), lambda i: (i, 0)), _halo_specs(T, tm, rcp, False), whole(mu)]
                  + [whole(s) for s in small] + [whole(w) for w in lora]),
        out_specs=[pl.BlockSpec((H, tm, HEAD), lambda i: (0, i, 0))] * 7, out_shape=[jax.ShapeDtypeStruct((H, T, HEAD), F32)] * 7,
        compiler_params=_params())(p, p, mu, *small, *lora)


def _rwkv_pre_bwd(p, mu, small, lora, hgrads, lay, name):
    T, rcp = p.shape
    widths, pw, offs, _ = lay
    RW = widths[0]
    H = RW // HEAD
    tm = min(128, T)

    def body(p_ref, ph_ref, mu_ref, w0_ref, a0_ref, kk_ref, ka_ref, wlw_ref, wla_ref, wlg_ref,
             dr1, dr2, dk1, dk2b, dv1, dv2, dlw_h, daa, dbb, dg_h,
             dps_ref, dmu_ref, dw0_ref, da0_ref, dkk_ref, dka_ref, dwlw_ref, dwla_ref, dwlg_ref,
             s_dr, s_dk2, s_dv, s_dlw, s_dkkp, s_da, s_dg):
        i = pl.program_id(0)
        ps, dprev = _token_shift(p_ref[...], ph_ref[...], mu_ref[...], i)
        k_k, k_a = kk_ref[...], ka_ref[...]
        q = _rwkv_math(ps, w0_ref[...], a0_ref[...], k_k, k_a, wlw_ref[...], wla_ref[...], wlg_ref[...], lay)
        k, a, lw, ww, tw, sg = q["k"], q["a"], q["lw"], q["ww"], q["tw"], q["sg"]
        for h in range(H):
            sl = slice(h * HEAD, (h + 1) * HEAD)
            s_dr[:, sl] = dr1[h] + dr2[h]
            s_dk2[:, sl] = dk1[h] + dk2b[h]
            s_dv[:, sl] = dv1[h] + dv2[h]
            s_dlw[:, sl] = dlw_h[h]
            s_dg[:, sl] = dg_h[h]
            nrm, inv, kk = _kk_math(q["kkp"][:, sl])
            dbb_h = dbb[h]
            dkk = dbb_h * a[:, sl] - daa[h]
            s_dkkp[:, sl] = jnp.where(nrm > 1e-12, inv * (dkk - kk * _rowsum(dkk * kk)), dkk * inv)
            s_da[:, sl] = dbb_h * kk
        dk2, dkkp, dg = s_dk2[...], s_dkkp[...], s_dg[...]
        dk = dk2 * (1.0 + (a - 1.0) * k_a) + dkkp * k_k
        da = s_da[...] + dk2 * k * k_a
        dpa = da * a * (1.0 - a)
        dww = s_dlw[...] * lw * _sigmoid(-ww)
        dxa = _bdot(dpa, wla_ref[...], "nt")
        dxw = _bdot(dww, wlw_ref[...], "nt") * (1.0 - tw * tw)
        dxg = _bdot(dg, wlg_ref[...], "nt") * sg * (1.0 - sg)
        segs = (s_dr[...], dk, s_dv[...], dxw, dxa, dxg)
        sums = [dmu_ref, dw0_ref, da0_ref, dkk_ref, dka_ref, dwlw_ref, dwla_ref, dwlg_ref]

        @pl.when(i == 0)
        def _():
            for s in sums:
                s[...] = jnp.zeros_like(s)

        for j, seg in enumerate(segs):
            sl = slice(offs[j], offs[j] + pw[j])
            dps_ref[:, sl] = seg
            dmu_ref[:, sl] += _colsum(seg * dprev[:, sl])
        dw0_ref[...] += _colsum(dww)
        da0_ref[...] += _colsum(dpa)
        dkk_ref[...] += _colsum(dkkp * k)
        dka_ref[...] += _colsum(dk2 * k * (a - 1.0))
        dwlw_ref[...] += _bdot(tw, dww, "tn")
        dwla_ref[...] += _bdot(q["xa"], dpa, "tn")
        dwlg_ref[...] += _bdot(sg, dg, "tn")

    whole = lambda arr: pl.BlockSpec(arr.shape, lambda i: (0, 0))
    row = lambda w: pl.BlockSpec((tm, w), lambda i: (i, 0))
    acc_shapes = [(1, rcp), (1, RW), (1, RW), (1, RW), (1, RW)] + [w.shape for w in lora]
    return pl.pallas_call(
        body, name=name, grid=(T // tm,),
        in_specs=([row(rcp), _halo_specs(T, tm, rcp, False), whole(mu)] + [whole(s) for s in small] + [whole(w) for w in lora]
                  + [pl.BlockSpec((H, tm, HEAD), lambda i: (0, i, 0))] * 10),
        out_specs=[row(rcp)] + [pl.BlockSpec(s, lambda i: (0, 0)) for s in acc_shapes],
        out_shape=[jax.ShapeDtypeStruct((T, rcp), F32)] + [jax.ShapeDtypeStruct(s, F32) for s in acc_shapes],
        scratch_shapes=[pltpu.VMEM((tm, RW), F32)] * 7, compiler_params=_params())(p, p, mu, *small, *lora, *hgrads)


def _shift_bwd(dps, mu, name):
    T, rcp = dps.shape
    tm = min(256, T)
    nt = T // tm

    def body(d_ref, dh_ref, mu_ref, o_ref):
        i = pl.program_id(0)
        d = d_ref[...]
        hid = lax.broadcasted_iota(jnp.int32, (SUBLANE, 1), 0)
        after = jnp.sum(jnp.where(hid == 0, dh_ref[...], 0.0), axis=0, keepdims=True)
        after = jnp.where(i == nt - 1, 0.0, after)
        rid = lax.broadcasted_iota(jnp.int32, (tm, 1), 0)
        nxt = jnp.where(rid == tm - 1, after, pltpu.roll(d, tm - 1, 0))
        mu_v = mu_ref[...]
        o_ref[...] = (d * (1.0 - mu_v) + nxt * mu_v).astype(BF16)

    row = pl.BlockSpec((tm, rcp), lambda i: (i, 0))
    return pl.pallas_call(
        body, name=name, grid=(nt,), in_specs=[row, _halo_specs(T, tm, rcp, True), pl.BlockSpec(mu.shape, lambda i: (0, 0))],
        out_specs=row, out_shape=jax.ShapeDtypeStruct((T, rcp), BF16), compiler_params=_params())(dps, dps, mu)


def _head_post_math(y, r, k2, v, lg, lb, rk):
    yc = y - _mean(y)
    rstd = lax.rsqrt(_mean(yc * yc) + LNX_EPS)
    yn = yc * rstd
    s = _rowsum(r * k2 * rk)
    return yn, rstd, yn * lg + lb + s * v, s


def _head_post(y, r, k2, v, g, hp, name):
    H, T, _ = y.shape
    tm = min(128, T)

    def body(y_ref, r_ref, k_ref, v_ref, g_ref, lg_ref, lb_ref, rk_ref, o_ref):
        _, _, t, _ = _head_post_math(y_ref[...], r_ref[...], k_ref[...], v_ref[...], lg_ref[...], lb_ref[...], rk_ref[...])
        out = (t * g_ref[...]).astype(BF16)
        for h in range(H):
            o_ref[:, h * HEAD:(h + 1) * HEAD] = out[h]

    blk = pl.BlockSpec((H, tm, HEAD), lambda i: (0, i, 0))
    par = pl.BlockSpec((H, 1, HEAD), lambda i: (0, 0, 0))
    return pl.pallas_call(
        body, name=name, grid=(T // tm,), in_specs=[blk] * 5 + [par] * 3, out_specs=pl.BlockSpec((tm, H * HEAD), lambda i: (i, 0)),
        out_shape=jax.ShapeDtypeStruct((T, H * HEAD), BF16), compiler_params=_params())(y, r, k2, v, g, *hp)


def _head_post_bwd(dya, y, r, k2, v, g, hp, name, deps=()):
    H, T, _ = y.shape
    tm = min(128, T)
    hsum = lambda t: jnp.sum(t, axis=1, keepdims=True)

    def body(d_ref, y_ref, r_ref, k_ref, v_ref, g_ref, lg_ref, lb_ref, rk_ref, *rest):
        outs, d_s = rest[len(deps):len(deps) + 8], rest[-1]
        for h in range(H):
            d_s[h] = d_ref[:, h * HEAD:(h + 1) * HEAD]
        d_v, r_v, k_v, v_v, lg, rk = d_s[...], r_ref[...], k_ref[...], v_ref[...], lg_ref[...], rk_ref[...]
        yn, rstd, t, s = _head_post_math(y_ref[...], r_v, k_v, v_v, lg, lb_ref[...], rk)
        dyo = d_v * g_ref[...]
        dyn = dyo * lg
        ds = _rowsum(dyo * v_v)
        vals = (rstd * (dyn - _mean(dyn) - yn * _mean(dyn * yn)), ds * k_v * rk, ds * r_v * rk, dyo * s, d_v * t)
        for o_ref, val in zip(outs[:5], vals):
            o_ref[...] = val
        sums = (hsum(dyo * yn), hsum(dyo), hsum(ds * r_v * k_v))
        first = pl.program_id(0) == 0

        @pl.when(first)
        def _():
            for o_ref, val in zip(outs[5:], sums):
                o_ref[...] = val

        @pl.when(jnp.logical_not(first))
        def _():
            for o_ref, val in zip(outs[5:], sums):
                o_ref[...] += val

    blk = pl.BlockSpec((H, tm, HEAD), lambda i: (0, i, 0))
    par = pl.BlockSpec((H, 1, HEAD), lambda i: (0, 0, 0))
    return pl.pallas_call(
        body, name=name, grid=(T // tm,),
        in_specs=([pl.BlockSpec((tm, H * HEAD), lambda i: (i, 0))] + [blk] * 5 + [par] * 3
                  + [pl.BlockSpec(d.shape, lambda i, nd=d.ndim: (0,) * nd) for d in deps]),
        out_specs=[blk] * 5 + [par] * 3,
        out_shape=[jax.ShapeDtypeStruct((H, T, HEAD), F32)] * 5 + [jax.ShapeDtypeStruct((H, 1, HEAD), F32)] * 3,
        scratch_shapes=[pltpu.VMEM((H, tm, HEAD), F32)], compiler_params=_params())(dya, y, r, k2, v, g, *hp, *deps)


def _bmm(x, y, mode):
    dn = {"nn": (((2,), (1,)), ((0,), (0,))), "nt": (((2,), (2,)), ((0,), (0,))), "tn": (((1,), (1,)), ((0,), (0,)))}[mode]
    (xh, xl), (yh, yl) = _split(x), _split(y)
    dot = lambda p, q: lax.dot_general(p, q, dn, preferred_element_type=F32)
    out = dot(xh, yh)
    if yl is not None:
        out = out + dot(xh, yl)
    if xl is not None:
        out = out + dot(xl, yh)
    return out


def _split(x):
    if isinstance(x, tuple):
        return x
    hi = x.astype(BF16)
    return hi, (x - hi.astype(F32)).astype(BF16)


def _exact(x):
    return x.astype(BF16), None


def _wkv_chunk(r, lw, k, v, a, b):
    hb, C, _ = r.shape
    ti = lax.broadcasted_iota(jnp.int32, (C, C), 0)
    si = lax.broadcasted_iota(jnp.int32, (C, C), 1)
    linc, lstr, eye = (ti >= si).astype(F32), (ti > si).astype(F32), (ti == si).astype(F32)
    lincb = _exact(jnp.broadcast_to(linc, (hb, C, C)))
    lstrb = _exact(jnp.broadcast_to(lstr, (hb, C, C)))
    ones = _exact(jnp.ones_like(v))
    lws = _split(lw)
    ci = _bmm(lincb, lws, "nn")
    cC = jnp.sum(lw, axis=1, keepdims=True)
    gi, ge, gn, gr = jnp.exp(ci), jnp.exp(ci - lw), jnp.exp(-ci), jnp.exp(cC - ci)
    q = dict(At=a * ge, Rt=r * gi, Bt=b * gn, Kt=k * gn, Bh=b * gr, Kh=k * gr)
    s = {key: _split(val) for key, val in q.items()}
    s["v"] = _split(v)
    q["A_ab"] = _bmm(s["At"], s["Bt"], "nt") * lstr
    for key, lhs, rhs, mask in (("A_ak", "At", "Kt", lstr), ("A_rb", "Rt", "Bt", linc), ("A_rk", "Rt", "Kt", linc)):
        q[key] = _bmm(s[lhs], s[rhs], "nt") * mask
        s[key] = _split(q[key])
    Tm = eye + q["A_ab"]
    Pw = _split(q["A_ab"])
    n = 1
    while 2 * n < C:
        Pw = _split(_bmm(Pw, Pw, "nn"))
        Tm = Tm + _bmm(Tm, Pw, "nn")
        n *= 2
    s["Tm"] = _split(Tm)
    gC = jnp.exp(_bmm(lws, ones, "tn"))
    q.update(gi=gi, ge=ge, gn=gn, gr=gr, linc=linc, lstr=lstr, lincb=lincb, lstrb=lstrb, gC=gC, ones=ones, s=s)
    return q


def _wkv_fwd(r, lw, k, v, a, b, name):
    H, T, N = r.shape
    C = min(WKV_CHUNK, T)
    nc = T // C
    hb = _pick(H, (8, 4, 2))

    def body(r_ref, lw_ref, k_ref, v_ref, a_ref, b_ref, y_ref, st_ref, h_ref):
        @pl.when(pl.program_id(1) == 0)
        def _():
            h_ref[...] = jnp.zeros_like(h_ref)

        H0 = h_ref[...]
        st_ref[0] = H0
        q = _wkv_chunk(r_ref[...], lw_ref[...], k_ref[...], v_ref[...], a_ref[...], b_ref[...])
        s = q["s"]
        H0s = _split(H0)
        U = _split(_bmm(s["Tm"], _bmm(s["At"], H0s, "nn") + _bmm(s["A_ak"], s["v"], "nn"), "nn"))
        y_ref[...] = _bmm(s["Rt"], H0s, "nn") + _bmm(s["A_rb"], U, "nn") + _bmm(s["A_rk"], s["v"], "nn")
        h_ref[...] = q["gC"] * H0 + _bmm(s["Bh"], U, "tn") + _bmm(s["Kh"], s["v"], "tn")

    blk = pl.BlockSpec((hb, C, N), lambda h, c: (h, c, 0))
    return pl.pallas_call(
        body, name=name, grid=(H // hb, nc), in_specs=[blk] * 6,
        out_specs=[blk, pl.BlockSpec((1, hb, N, N), lambda h, c: (c, h, 0, 0))],
        out_shape=[jax.ShapeDtypeStruct((H, T, N), F32), jax.ShapeDtypeStruct((nc, H, N, N), F32)],
        scratch_shapes=[pltpu.VMEM((hb, N, N), F32)], compiler_params=_params())(r, lw, k, v, a, b)


def _wkv_bwd(r, lw, k, v, a, b, states, dy, name):
    H, T, N = r.shape
    C = min(WKV_CHUNK, T)
    nc = T // C
    hb = _pick(H, (8, 4, 2))

    def body(r_ref, lw_ref, k_ref, v_ref, a_ref, b_ref, st_ref, dy_ref, dr_ref, dlw_ref, dk_ref, dv_ref, da_ref, db_ref, dh_ref):
        @pl.when(pl.program_id(1) == 0)
        def _():
            dh_ref[...] = jnp.zeros_like(dh_ref)

        dHC = dh_ref[...]
        H0 = st_ref[0]
        q = _wkv_chunk(r_ref[...], lw_ref[...], k_ref[...], v_ref[...], a_ref[...], b_ref[...])
        s, gC = q["s"], q["gC"]
        H0s, dHs, dY = _split(H0), _split(dHC), _split(dy_ref[...])
        U = _split(_bmm(s["Tm"], _bmm(s["At"], H0s, "nn") + _bmm(s["A_ak"], s["v"], "nn"), "nn"))
        dU = _bmm(s["A_rb"], dY, "tn") + _bmm(s["Bh"], dHs, "nn")
        dP = _split(_bmm(s["Tm"], dU, "tn"))
        dv_ref[...] = _bmm(s["A_rk"], dY, "tn") + _bmm(s["Kh"], dHs, "nn") + _bmm(s["A_ak"], dP, "tn")
        dh_ref[...] = _bmm(s["Rt"], dY, "tn") + gC * dHC + _bmm(s["At"], dP, "tn")
        dA_rb = _split(_bmm(dY, U, "nt") * q["linc"])
        dA_rk = _split(_bmm(dY, s["v"], "nt") * q["linc"])
        dA_ab = _split(_bmm(dP, U, "nt") * q["lstr"])
        dA_ak = _split(_bmm(dP, s["v"], "nt") * q["lstr"])
        dRt = _bmm(dY, H0s, "nt") + _bmm(dA_rb, s["Bt"], "nn") + _bmm(dA_rk, s["Kt"], "nn")
        dAt = _bmm(dP, H0s, "nt") + _bmm(dA_ab, s["Bt"], "nn") + _bmm(dA_ak, s["Kt"], "nn")
        dBt = _bmm(dA_ab, s["At"], "tn") + _bmm(dA_rb, s["Rt"], "tn")
        dKt = _bmm(dA_ak, s["At"], "tn") + _bmm(dA_rk, s["Rt"], "tn")
        dBh = _bmm(U, dHs, "nt")
        dKh = _bmm(s["v"], dHs, "nt")
        dr_ref[...] = dRt * q["gi"]
        da_ref[...] = dAt * q["ge"]
        db_ref[...] = dBt * q["gn"] + dBh * q["gr"]
        dk_ref[...] = dKt * q["gn"] + dKh * q["gr"]
        tail = dBh * q["Bh"] + dKh * q["Kh"]
        dci = dRt * q["Rt"] - dBt * q["Bt"] - dKt * q["Kt"] - tail
        dcC = jnp.sum(tail, axis=1, keepdims=True) + _bmm(q["ones"], H0 * dHC * gC, "nt")
        dlw_ref[...] = _bmm(q["lincb"], dci, "tn") + _bmm(q["lstrb"], dAt * q["At"], "tn") + dcC

    blk = pl.BlockSpec((hb, C, N), lambda h, c: (h, nc - 1 - c, 0))
    st = pl.BlockSpec((1, hb, N, N), lambda h, c: (nc - 1 - c, h, 0, 0))
    return pl.pallas_call(
        body, name=name, grid=(H // hb, nc), in_specs=[blk] * 6 + [st, blk], out_specs=[blk] * 6,
        out_shape=[jax.ShapeDtypeStruct((H, T, N), F32)] * 6,
        scratch_shapes=[pltpu.VMEM((hb, N, N), F32)], compiler_params=_params())(r, lw, k, v, a, b, states, dy)


def _sgu_ln(z, SW, lng, lnb):
    ge = _gelu(z)
    u, vv = ge[:, :SW], ge[:, SW:]
    xc = vv - _mean(vv)
    rstd = lax.rsqrt(_mean(xc * xc) + LN_EPS)
    vn = xc * rstd
    return u, vn, rstd, vn * lng + lnb


def _causal(ws_ref, g):
    ti = lax.broadcasted_iota(jnp.int32, (SGU_CHUNK, SGU_CHUNK), 0)
    si = lax.broadcasted_iota(jnp.int32, (SGU_CHUNK, SGU_CHUNK), 1)
    return ti >= si, jnp.where(ti >= si, ws_ref[g], 0.0).astype(BF16)


def _sgu_fwd(z, lng, lnb, ws, bexp, name):
    T, SW = z.shape[0], z.shape[1] // 2
    G = ws.shape[0]
    tr = min(256, T)
    nch = tr // SGU_CHUNK

    def body(z_ref, lng_ref, lnb_ref, ws_ref, be_ref, o_ref):
        u, _, _, vl = _sgu_ln(z_ref[...], SW, lng_ref[...], lnb_ref[...])
        for g in range(G):
            cs = slice(g * SGU_GROUP, (g + 1) * SGU_GROUP)
            _, wc = _causal(ws_ref, g)
            for n in range(nch):
                rs = slice(n * SGU_CHUNK, (n + 1) * SGU_CHUNK)
                m = jnp.dot(wc, vl[rs, cs].astype(BF16), preferred_element_type=F32) + be_ref[:, cs]
                o_ref[rs, cs] = (u[rs, cs] * m).astype(BF16)

    whole = lambda arr: pl.BlockSpec(arr.shape, lambda i, nd=arr.ndim: (0,) * nd)
    return pl.pallas_call(
        body, name=name, grid=(T // tr,),
        in_specs=[pl.BlockSpec((tr, 2 * SW), lambda i: (i, 0)), whole(lng), whole(lnb), whole(ws), whole(bexp)],
        out_specs=pl.BlockSpec((tr, SW), lambda i: (i, 0)), out_shape=jax.ShapeDtypeStruct((T, SW), BF16),
        compiler_params=_params())(z, lng, lnb, ws, bexp)


def _sgu_bwd(z, dyb, lng, lnb, ws, bexp, name):
    T, SW = z.shape[0], z.shape[1] // 2
    G = ws.shape[0]
    tr = min(256, T)
    nch = tr // SGU_CHUNK
    nt = T // tr

    def body(z_ref, dy_ref, lng_ref, lnb_ref, ws_ref, be_ref, dz_ref, dlg_ref, dlb_ref, dws_ref, db_ref, du_s, dvl_s, dbacc_s):
        i = pl.program_id(0)
        zv = z_ref[...]
        lng_v = lng_ref[...]
        u, vn, rstd, vl = _sgu_ln(zv, SW, lng_v, lnb_ref[...])

        @pl.when(i == 0)
        def _():
            for s in (dlg_ref, dlb_ref, dws_ref, dbacc_s):
                s[...] = jnp.zeros_like(s)

        for g in range(G):
            cs = slice(g * SGU_GROUP, (g + 1) * SGU_GROUP)
            tri, wc = _causal(ws_ref, g)
            for n in range(nch):
                rs = slice(n * SGU_CHUNK, (n + 1) * SGU_CHUNK)
                blk = vl[rs, cs].astype(BF16)
                m = jnp.dot(wc, blk, preferred_element_type=F32) + be_ref[:, cs]
                dyv = dy_ref[rs, cs]
                du_s[rs, cs] = dyv * m
                dm = dyv * u[rs, cs]
                dvl_s[rs, cs] = _bdot(wc, dm, "tn")
                dws_ref[g] += jnp.where(tri, _bdot(dm, blk, "nt"), 0.0)
                dbacc_s[:, cs] += dm

        dvl = dvl_s[...]
        dlg_ref[...] += _colsum(dvl * vn)
        dlb_ref[...] += _colsum(dvl)
        dvn = dvl * lng_v
        dvv = rstd * (dvn - _mean(dvn) - vn * _mean(dvn * vn))
        gp = _gelu_grad(zv)
        dz_ref[:, :SW] = (du_s[...] * gp[:, :SW]).astype(BF16)
        dz_ref[:, SW:] = (dvv * gp[:, SW:]).astype(BF16)

        @pl.when(i == nt - 1)
        def _():
            lane = lax.broadcasted_iota(jnp.int32, (SGU_CHUNK, LANE), 1)
            out = jnp.zeros((SGU_CHUNK, LANE), F32)
            for g in range(G):
                col = jnp.sum(dbacc_s[:, g * SGU_GROUP:(g + 1) * SGU_GROUP], axis=1, keepdims=True)
                out = jnp.where(lane == g, col, out)
            db_ref[...] = out

    whole = lambda arr: pl.BlockSpec(arr.shape, lambda i, nd=arr.ndim: (0,) * nd)
    acc_shapes = [(1, SW), (1, SW), ws.shape, (SGU_CHUNK, LANE)]
    return pl.pallas_call(
        body, name=name, grid=(nt,),
        in_specs=[pl.BlockSpec((tr, 2 * SW), lambda i: (i, 0)), pl.BlockSpec((tr, SW), lambda i: (i, 0)),
                  whole(lng), whole(lnb), whole(ws), whole(bexp)],
        out_specs=[pl.BlockSpec((tr, 2 * SW), lambda i: (i, 0))] + [pl.BlockSpec(s, lambda i, nd=len(s): (0,) * nd) for s in acc_shapes],
        out_shape=[jax.ShapeDtypeStruct((T, 2 * SW), BF16)] + [jax.ShapeDtypeStruct(s, F32) for s in acc_shapes],
        scratch_shapes=[pltpu.VMEM((tr, SW), F32), pltpu.VMEM((tr, SW), F32), pltpu.VMEM((SGU_CHUNK, SW), F32)],
        compiler_params=_params())(z, dyb, lng, lnb, ws, bexp)


_HBM = pl.BlockSpec(memory_space=pltpu.HBM)
_SEM = pl.BlockSpec(memory_space=pltpu.SEMAPHORE)
_DATAFLOW = pltpu.SideEffectType.DATAFLOW_SIDE_EFFECTING


def _mesh_place():
    x, y, c = lax.axis_index("x"), lax.axis_index("y"), lax.axis_index("c")
    return x, y, c, 4 * x + 2 * y + c


def _peer(x, y, c, rel):
    px = 1 - x if rel & 4 else x
    py = 1 - y if rel & 2 else y
    pc = 1 - c if rel & 1 else c
    return (px, py, pc), 4 * px + 2 * py + pc


def _exchange_start(groups, name):
    flat = [t for g in groups for t in g]
    sizes = [len(g) for g in groups]
    n, ng = len(flat), len(groups)
    srcs = [pltpu.with_memory_space_constraint(a, pltpu.HBM) for a, _ in flat]
    lands = [pltpu.with_memory_space_constraint(lax.empty(((N_DEV,) + a.shape) if isg else a.shape, a.dtype), pltpu.HBM)
             for a, isg in flat]

    def body(*refs):
        ins, lnd, sems, token = refs[:n], refs[n:2 * n], refs[2 * n:2 * n + 2 * ng], refs[-1]
        x, y, c, me = _mesh_place()
        j0 = 0
        for gi, sz in enumerate(sizes):
            for rel in range(1, N_DEV):
                dev, slot = _peer(x, y, c, rel)
                for jj in range(sz):
                    j = j0 + jj
                    pltpu.make_async_remote_copy(
                        src_ref=ins[j] if flat[j][1] else ins[j].at[slot], dst_ref=lnd[j].at[me],
                        send_sem=sems[2 * gi].at[jj * (N_DEV - 1) + rel - 1], recv_sem=sems[2 * gi + 1].at[jj * (N_DEV - 1) + rel - 1],
                        device_id=dev, device_id_type=pl.DeviceIdType.MESH).start()
            j0 += sz
        token[...] = jnp.zeros_like(token)

    sem_shapes = [pltpu.SemaphoreType.DMA((sz * (N_DEV - 1),)) for sz in sizes for _ in range(2)]
    res = pl.pallas_call(
        body, name=name,
        out_shape=(*sem_shapes, *[pltpu.HBM(a.shape, a.dtype) for a in srcs], *[pltpu.HBM(a.shape, a.dtype) for a in lands],
                   jax.ShapeDtypeStruct((SUBLANE, LANE), F32)),
        in_specs=[_HBM] * (2 * n), out_specs=(*[_SEM] * (2 * ng), *[_HBM] * (2 * n), pl.BlockSpec(memory_space=pltpu.VMEM)),
        input_output_aliases={i: 2 * ng + i for i in range(2 * n)},
        compiler_params=pltpu.CompilerParams(has_side_effects=_DATAFLOW))(*srcs, *lands)
    sems, thru, token = res[:2 * ng], res[2 * ng:2 * ng + 2 * n], res[-1]
    handle, j0 = [], 0
    for gi, sz in enumerate(sizes):
        handle.append(dict(kinds=[k for _, k in groups[gi]], srcs=list(thru[j0:j0 + sz]), lands=list(thru[n + j0:n + j0 + sz]),
                           ssem=sems[2 * gi], rsem=sems[2 * gi + 1]))
        j0 += sz
    return handle, token


def _exchange_wait(group, after, name):
    kinds, sz = group["kinds"], len(group["kinds"])

    def body(*refs):
        ins, lnd, ssem, rsem = refs[:sz], refs[sz:2 * sz], refs[2 * sz], refs[2 * sz + 1]
        x, y, c, _ = _mesh_place()
        for rel in range(1, N_DEV):
            dev, slot = _peer(x, y, c, rel)
            for jj in range(sz):
                cp = pltpu.make_async_remote_copy(
                    src_ref=ins[jj] if kinds[jj] else ins[jj].at[slot], dst_ref=lnd[jj].at[slot],
                    send_sem=ssem.at[jj * (N_DEV - 1) + rel - 1], recv_sem=rsem.at[jj * (N_DEV - 1) + rel - 1],
                    device_id=dev, device_id_type=pl.DeviceIdType.MESH)
                cp.wait_send()
                cp.wait_recv()

    arrays = group["srcs"] + group["lands"]
    res = pl.pallas_call(
        body, name=name, out_shape=[pltpu.HBM(a.shape, a.dtype) for a in arrays],
        in_specs=[_HBM] * (2 * sz) + [_SEM, _SEM, pl.BlockSpec(memory_space=pl.ANY)], out_specs=[_HBM] * (2 * sz),
        input_output_aliases={i: i for i in range(2 * sz)},
        compiler_params=pltpu.CompilerParams(has_side_effects=_DATAFLOW))(*arrays, group["ssem"], group["rsem"], after)
    srcs, lands = res[:sz], res[sz:]

    chunks = []
    for jj in range(sz):
        rows = lands[jj].shape[1]
        nch = next((c for c in (16, 8, 4, 2) if rows % (c * 2 * SUBLANE) == 0), 1)
        chunks += [(jj, k * (rows // nch), rows // nch) for k in range(nch)]

    def place(*refs):
        ins, out, sem = refs[:sz], refs[2 * sz:3 * sz], refs[-1]
        me = _mesh_place()[3]
        cps = []
        for q, (jj, r0, nr) in enumerate(chunks):
            src = ins[jj].at[pl.ds(r0, nr)] if kinds[jj] else ins[jj].at[me, pl.ds(r0, nr)]
            cps.append(pltpu.make_async_copy(src, out[jj].at[me, pl.ds(r0, nr)], sem.at[q]))
        for cp in cps:
            cp.start()
        for cp in cps:
            cp.wait()

    return list(pl.pallas_call(
        place, name=name + "_own", out_shape=[jax.ShapeDtypeStruct(a.shape, a.dtype) for a in lands],
        in_specs=[_HBM] * (2 * sz), out_specs=[_HBM] * sz, input_output_aliases={sz + i: i for i in range(sz)},
        scratch_shapes=[pltpu.SemaphoreType.DMA((len(chunks),))])(*srcs, *lands))


def _adamw(w, m, v, gparts, name):
    R, C = w.shape
    tm = _pick(R, (256, 128, 64, 32, 16, 8))

    def body(w_ref, m_ref, v_ref, g_ref, go, do, mo, vo):
        g = g_ref[0].astype(F32)
        for j in range(1, N_DEV):
            g = g + g_ref[j].astype(F32)
        mn = ADAM_B1 * m_ref[...] + (1.0 - ADAM_B1) * g
        vn = ADAM_B2 * v_ref[...] + (1.0 - ADAM_B2) * (g * g)
        m_hat = mn / (1.0 - ADAM_B1 ** ADAM_STEP)
        v_hat = vn / (1.0 - ADAM_B2 ** ADAM_STEP)
        go[...] = g
        do[...] = -ADAM_LR * (m_hat / (jnp.sqrt(v_hat) + ADAM_EPS) + ADAM_WD * w_ref[...])
        mo[...] = mn
        vo[...] = vn

    row = pl.BlockSpec((tm, C), lambda i: (i, 0))
    return pl.pallas_call(
        body, name=name, grid=(R // tm,), in_specs=[row, row, row, pl.BlockSpec((N_DEV, tm, C), lambda i: (0, i, 0))],
        out_specs=[row] * 4, out_shape=[jax.ShapeDtypeStruct((R, C), F32)] * 4, compiler_params=_params())(w, m, v, gparts)


def _pack(arrays):
    flat = []
    for a in arrays:
        f = a.reshape(-1)
        pad = _ceil_to(f.shape[0], LANE) - f.shape[0]
        flat.append(jnp.concatenate([f, jnp.zeros((pad,), f.dtype)]) if pad else f)
    buf = jnp.concatenate(flat)
    rows = _ceil_to(buf.shape[0] // LANE, 64)
    buf = jnp.concatenate([buf, jnp.zeros((rows * LANE - buf.shape[0],), buf.dtype)])
    return buf.reshape(rows, LANE)


def _unpack(buf, shapes):
    flat, out, off = buf.reshape(-1), [], 0
    for s in shapes:
        size = 1
        for d in s:
            size *= d
        out.append(flat[off:off + size].reshape(s))
        off += _ceil_to(size, LANE)
    return out


def kernel(x, norm_mix_g, w_in, shift_mu, w0, w_lora_up, a0, a_lora_up, g_lora_up, k_k, k_a, r_k, lnx_g, lnx_b, w_proj_rwkv, sgu_ln_g, sgu_ln_b, sgu_w, sgu_b, w_proj_sgu, w_out, norm_ffn_g, w_ffn_gate, w_ffn_up, w_ffn_down, norm_final_g, loss_target, m_norm_mix_g, m_w_in, m_shift_mu, m_w0, m_w_lora_up, m_a0, m_a_lora_up, m_g_lora_up, m_k_k, m_k_a, m_r_k, m_lnx_g, m_lnx_b, m_w_proj_rwkv, m_sgu_ln_g, m_sgu_ln_b, m_sgu_w, m_sgu_b, m_w_proj_sgu, m_w_out, m_norm_ffn_g, m_w_ffn_gate, m_w_ffn_up, m_w_ffn_down, m_norm_final_g, v_norm_mix_g, v_w_in, v_shift_mu, v_w0, v_w_lora_up, v_a0, v_a_lora_up, v_g_lora_up, v_k_k, v_k_a, v_r_k, v_lnx_g, v_lnx_b, v_w_proj_rwkv, v_sgu_ln_g, v_sgu_ln_b, v_sgu_w, v_sgu_b, v_w_proj_sgu, v_w_out, v_norm_ffn_g, v_w_ffn_gate, v_w_ffn_up, v_w_ffn_down, v_norm_final_g):
    weights = dict(norm_mix_g=norm_mix_g, w_in=w_in, shift_mu=shift_mu, w0=w0, w_lora_up=w_lora_up, a0=a0, a_lora_up=a_lora_up,
                   g_lora_up=g_lora_up, k_k=k_k, k_a=k_a, r_k=r_k, lnx_g=lnx_g, lnx_b=lnx_b, w_proj_rwkv=w_proj_rwkv,
                   sgu_ln_g=sgu_ln_g, sgu_ln_b=sgu_ln_b, sgu_w=sgu_w, sgu_b=sgu_b, w_proj_sgu=w_proj_sgu, w_out=w_out,
                   norm_ffn_g=norm_ffn_g, w_ffn_gate=w_ffn_gate, w_ffn_up=w_ffn_up, w_ffn_down=w_ffn_down, norm_final_g=norm_final_g)
    m_in = dict(norm_mix_g=m_norm_mix_g, w_in=m_w_in, shift_mu=m_shift_mu, w0=m_w0, w_lora_up=m_w_lora_up, a0=m_a0,
                a_lora_up=m_a_lora_up, g_lora_up=m_g_lora_up, k_k=m_k_k, k_a=m_k_a, r_k=m_r_k, lnx_g=m_lnx_g, lnx_b=m_lnx_b,
                w_proj_rwkv=m_w_proj_rwkv, sgu_ln_g=m_sgu_ln_g, sgu_ln_b=m_sgu_ln_b, sgu_w=m_sgu_w, sgu_b=m_sgu_b,
                w_proj_sgu=m_w_proj_sgu, w_out=m_w_out, norm_ffn_g=m_norm_ffn_g, w_ffn_gate=m_w_ffn_gate, w_ffn_up=m_w_ffn_up,
                w_ffn_down=m_w_ffn_down, norm_final_g=m_norm_final_g)
    v_in = dict(norm_mix_g=v_norm_mix_g, w_in=v_w_in, shift_mu=v_shift_mu, w0=v_w0, w_lora_up=v_w_lora_up, a0=v_a0,
                a_lora_up=v_a_lora_up, g_lora_up=v_g_lora_up, k_k=v_k_k, k_a=v_k_a, r_k=v_r_k, lnx_g=v_lnx_g, lnx_b=v_lnx_b,
                w_proj_rwkv=v_w_proj_rwkv, sgu_ln_g=v_sgu_ln_g, sgu_ln_b=v_sgu_ln_b, sgu_w=v_sgu_w, sgu_b=v_sgu_b,
                w_proj_sgu=v_w_proj_sgu, w_out=v_w_out, norm_ffn_g=v_norm_ffn_g, w_ffn_gate=v_w_ffn_gate, w_ffn_up=v_w_ffn_up,
                w_ffn_down=v_w_ffn_down, norm_final_g=v_norm_final_g)
    names = list(weights)
    col_sharded = ("w_in", "w_lora_up", "a_lora_up", "g_lora_up", "w_proj_rwkv", "w_proj_sgu", "w_ffn_gate", "w_ffn_up")
    row_sharded = ("w_out", "w_ffn_down")
    sharded = [n for n in names if n in col_sharded or n in row_sharded]
    small = [n for n in names if n not in sharded]

    xs, tgt = x[0], loss_target[0]
    T, D = xs.shape
    RW = w0.shape[1]
    H = RW // HEAD
    SW = sgu_ln_g.shape[1]
    G = sgu_w.shape[1]
    lay = _rwkv_layout(RW, w_lora_up.shape[1], a_lora_up.shape[1], g_lora_up.shape[1])
    _, pw, _, rcp = lay

    gather_groups = [["w_in", "w_lora_up", "a_lora_up", "g_lora_up"], ["w_proj_rwkv", "w_proj_sgu", "w_out"],
                     ["w_ffn_gate", "w_ffn_up", "w_ffn_down"]]
    gather, gather_token = _exchange_start([[(weights[n][0].astype(BF16), True) for n in grp] for grp in gather_groups], "gather_start")
    full = {}

    def take_weights(gi, after, name):
        for n, g in zip(gather_groups[gi], _exchange_wait(gather[gi], after, name)):
            full[n] = g.transpose(1, 0, 2).reshape(g.shape[1], -1) if n in col_sharded else g.reshape(-1, g.shape[2])

    n1 = _rms_fwd(xs, norm_mix_g, "rms_mix", deps=[gather_token])
    take_weights(0, n1, "gather_wait_in")
    W_in = _pad_rwkv_cols(full["w_in"], lay)
    o_z, o_ga, o_gb = rcp, rcp + 2 * SW, rcp + 2 * SW + D
    W_r, W_z, W_ga, W_gb = W_in[:, :o_z], W_in[:, o_z:o_ga], W_in[:, o_ga:o_gb], W_in[:, o_gb:]
    lora = [_pad_rows(full["w_lora_up"], pw[3]), _pad_rows(full["a_lora_up"], pw[4]), _pad_rows(full["g_lora_up"], pw[5])]
    mu_p = _pad_rwkv_cols(shift_mu, lay)
    rsmall = [w0, a0, k_k, k_a]
    hp = [lnx_g.reshape(H, 1, HEAD), lnx_b.reshape(H, 1, HEAD), r_k.reshape(H, 1, HEAD)]
    ws = sgu_w[0]
    bexp = jnp.repeat(sgu_b[0].T, SGU_GROUP, axis=1)
    gf = norm_final_g.reshape(1, D)

    p = _matmul(n1, W_r, mode="nn", out_dtype=F32, name="proj_rwkv")
    z = _matmul(n1, W_z, mode="nn", out_dtype=F32, name="proj_sgu")
    ga = _matmul(n1, W_ga, mode="nn", out_dtype=F32, name="proj_gate_a")
    gb = _matmul(n1, W_gb, mode="nn", out_dtype=F32, name="proj_gate_b")
    r_h, lw_h, k2_h, v_h, aa_h, bb_h, g_h = _rwkv_pre(p, mu_p, rsmall, lora, lay, "rwkv_pre")
    wkv_in = [r_h, lw_h, k2_h, v_h, aa_h, bb_h]
    y_h, states = _wkv_fwd(*wkv_in, "wkv_fwd")
    ya = _head_post(y_h, r_h, k2_h, v_h, g_h, hp, "head_post")
    yb = _sgu_fwd(z, sgu_ln_g, sgu_ln_b, ws, bexp, "sgu_fwd")
    take_weights(1, ya, "gather_wait_proj")
    pa = _matmul(ya, full["w_proj_rwkv"], mode="nn", out_dtype=F32, name="proj_a")
    pb = _matmul(yb, full["w_proj_sgu"], mode="nn", out_dtype=F32, name="proj_b")

    def merge_fn(rv, pv):
        ga_v, gb_v, pa_v, pb_v = rv
        return [_sigmoid(ga_v) * pa_v + _sigmoid(gb_v) * pb_v], []
    merged = _rowwise(merge_fn, [ga, gb, pa, pb], [], [(D, BF16)], [], name="merge")[0]
    h1 = _matmul(merged, full["w_out"], mode="nn", out_dtype=F32, name="out_proj", add=xs)
    n2 = _rms_fwd(h1, norm_ffn_g, "rms_ffn")
    take_weights(2, n2, "gather_wait_ffn")
    gt = _matmul(n2, full["w_ffn_gate"], mode="nn", out_dtype=F32, name="ffn_gate")
    up = _matmul(n2, full["w_ffn_up"], mode="nn", out_dtype=F32, name="ffn_up")

    def act_fn(rv, pv):
        gt_v, up_v = rv
        return [gt_v * _sigmoid(gt_v) * up_v], []
    act = _rowwise(act_fn, [gt, up], [], [(gt.shape[1], BF16)], [], name="ffn_act")[0]
    h2 = _matmul(act, full["w_ffn_down"], mode="nn", out_dtype=F32, name="ffn_down", add=h1)

    def final_fn(rv, pv):
        (h_v, t_v), (g_v,) = rv, pv
        r = lax.rsqrt(_mean(h_v * h_v) + RMS_EPS)
        yn = h_v * r
        e = yn * g_v - t_v
        loss = 0.5 * jnp.sum(_mean(e * e))
        dout = e * (1.0 / D)
        dyg = dout * g_v
        dh = r * (dyg - yn * _mean(dyg * yn))
        return [dh, dh], [jnp.full((1, LANE), loss, F32), _colsum(dout * yn)]
    dh2, dh2_bf, loss_part, d_gf = _rowwise(final_fn, [h2, tgt], [gf], [(D, F32), (D, BF16)], [(1, LANE), (1, D)], name="final_loss")

    grads = {}

    def start_scatter(group, name, extra=()):
        blocks = []
        for n in group:
            g = grads[n]
            blocks.append((g.reshape(g.shape[0], N_DEV, -1).transpose(1, 0, 2) if n in col_sharded
                           else g.reshape(N_DEV, -1, g.shape[1]), False))
        (handle,), token = _exchange_start([blocks + list(extra)], name)
        return handle, token

    dact = _matmul(dh2_bf, full["w_ffn_down"], mode="nt", out_dtype=F32, name="d_act")
    grads["w_ffn_down"] = _matmul(act, dh2_bf, mode="tn", out_dtype=BF16, name="dw_ffn_down")

    def dact_fn(rv, pv):
        d_v, gt_v, up_v = rv
        s = _sigmoid(gt_v)
        return [d_v * up_v * (s * (1.0 + gt_v * (1.0 - s))), d_v * gt_v * s], []
    dgt, dup = _rowwise(dact_fn, [dact, gt, up], [], [(gt.shape[1], BF16)] * 2, [], name="d_ffn_act")
    dn2 = _matmul(dgt, full["w_ffn_gate"], mode="nt", out_dtype=F32, name="dn2_gate")
    dn2 = _matmul(dup, full["w_ffn_up"], mode="nt", out_dtype=F32, name="dn2_up", add=dn2)
    grads["w_ffn_gate"] = _matmul(n2, dgt, mode="tn", out_dtype=BF16, name="dw_ffn_gate")
    grads["w_ffn_up"] = _matmul(n2, dup, mode="tn", out_dtype=BF16, name="dw_ffn_up")
    scatter_groups = [["w_ffn_down", "w_ffn_gate", "w_ffn_up"], ["w_out", "w_proj_rwkv", "w_proj_sgu"],
                      ["w_in", "w_lora_up", "a_lora_up", "g_lora_up"]]
    scatter_ffn, token_ffn = start_scatter(scatter_groups[0], "scatter_start_ffn")
    dh1, dh1_bf, d_g2 = _rms_bwd(dn2, h1, dh2, norm_ffn_g, "rms_ffn_bwd", deps=[token_ffn])
    dmerged = _matmul(dh1_bf, full["w_out"], mode="nt", out_dtype=F32, name="d_merged")
    grads["w_out"] = _matmul(merged, dh1_bf, mode="tn", out_dtype=BF16, name="dw_out")

    def dmerge_fn(rv, pv):
        d_v, ga_v, gb_v, pa_v, pb_v = rv
        sa, sb = _sigmoid(ga_v), _sigmoid(gb_v)
        return [d_v * pa_v * sa * (1.0 - sa), d_v * pb_v * sb * (1.0 - sb), d_v * sa, d_v * sb], []
    dga, dgb, dpa, dpb = _rowwise(dmerge_fn, [dmerged, ga, gb, pa, pb], [], [(D, BF16)] * 4, [], name="d_merge")
    dya = _matmul(dpa, full["w_proj_rwkv"], mode="nt", out_dtype=F32, name="d_ya")
    dyb = _matmul(dpb, full["w_proj_sgu"], mode="nt", out_dtype=F32, name="d_yb")
    grads["w_proj_rwkv"] = _matmul(ya, dpa, mode="tn", out_dtype=BF16, name="dw_proj_a")
    grads["w_proj_sgu"] = _matmul(yb, dpb, mode="tn", out_dtype=BF16, name="dw_proj_b")
    scatter_mid, token_mid = start_scatter(scatter_groups[1], "scatter_start_mid")
    dz, d_lng, d_lnb, d_ws, d_bs = _sgu_bwd(z, dyb, sgu_ln_g, sgu_ln_b, ws, bexp, "sgu_bwd")

    dy_h, dr1, dk1, dv1, dg_h, d_lnxg, d_lnxb, d_rk = _head_post_bwd(dya, y_h, r_h, k2_h, v_h, g_h, hp, "head_post_bwd",
                                                                     deps=[token_mid])
    dr2, dlw_h, dk2b, dv2, daa, dbb = _wkv_bwd(*wkv_in, states, dy_h, "wkv_bwd")
    dps, d_mu, d_w0, d_a0, d_kk, d_ka, d_wlw, d_wla, d_wlg = _rwkv_pre_bwd(
        p, mu_p, rsmall, lora, [dr1, dr2, dk1, dk2b, dv1, dv2, dlw_h, daa, dbb, dg_h], lay, "rwkv_pre_bwd")
    dp = _shift_bwd(dps, mu_p, "shift_bwd")
    dproj = jnp.concatenate([dp, dz, dga, dgb], axis=1)
    dn1 = _matmul(dproj, W_in, mode="nt", out_dtype=F32, name="dn1")
    grads["w_in"] = _unpad_rwkv_cols(_matmul(n1, dproj, mode="tn", out_dtype=BF16, name="dw_in"), lay)
    dx, _, d_g1 = _rms_bwd(dn1, xs, dh1, norm_mix_g, "rms_mix_bwd")
    grads["w_lora_up"] = d_wlw[:w_lora_up.shape[1]].astype(BF16)
    grads["a_lora_up"] = d_wla[:a_lora_up.shape[1]].astype(BF16)
    grads["g_lora_up"] = d_wlg[:g_lora_up.shape[1]].astype(BF16)
    small_grads = dict(norm_mix_g=d_g1, shift_mu=_unpad_rwkv_cols(d_mu, lay), w0=d_w0, a0=d_a0, k_k=d_kk, k_a=d_ka, r_k=d_rk,
                       lnx_g=d_lnxg, lnx_b=d_lnxb, sgu_ln_g=d_lng, sgu_ln_b=d_lnb, sgu_w=d_ws, sgu_b=d_bs[:, :G].T,
                       norm_ffn_g=d_g2, norm_final_g=d_gf)

    scatter_in, token_in = start_scatter(scatter_groups[2], "scatter_start_in", extra=[(_pack([small_grads[n] for n in small]), True)])
    out = {}
    after = token_in
    small_parts = None
    for group, handle, name in zip(scatter_groups, (scatter_ffn, scatter_mid, scatter_in), ("ffn", "mid", "in")):
        parts = _exchange_wait(handle, after, "scatter_wait_" + name)
        for n, part in zip(group, parts):
            shp = weights[n].shape
            res = _adamw(weights[n][0], m_in[n][0], v_in[n][0], part, "adamw_" + n)
            out[n] = [t.reshape(shp) for t in res]
            after = res[0]
        small_parts = parts[-1]
    packed = [_pack([d[n] for n in small]) for d in (weights, m_in, v_in)]
    res = _adamw(*packed, small_parts, "adamw_small")
    unpacked = [_unpack(t, [weights[n].shape for n in small]) for t in res]
    for i, n in enumerate(small):
        out[n] = [u[i] for u in unpacked]

    loss = lax.psum(loss_part[0, 0], ("x", "y", "c"))
    return (loss, dx[None], *[out[n][0] for n in names], *[out[n][1] for n in names],
            *[out[n][2] for n in names], *[out[n][3] for n in names])
```

```python
import jax
import jax.numpy as jnp
from jax import lax
from jax.experimental import pallas as pl
from jax.experimental.pallas import tpu as pltpu

F32 = jnp.float32
BF16 = jnp.bfloat16

N_DEV = 8
LANE = 128
SUBLANE = 8
HEAD = 64
SGU_CHUNK = 128
SGU_GROUP = 128
WKV_CHUNK = 64
RMS_EPS = 1e-6
LN_EPS = 1e-5
LNX_EPS = 64e-5
ADAM_LR, ADAM_B1, ADAM_B2, ADAM_EPS, ADAM_WD, ADAM_STEP = 0.001, 0.9, 0.999, 1e-08, 0.01, 10
VMEM_LIMIT_BYTES = 48 * 1024 * 1024
_SQRT_HALF = 0.7071067811865476
_INV_SQRT_2PI = 0.3989422804014327


def _pick(n, cands):
    for c in cands:
        if n % c == 0:
            return c
    return n


def _ceil_to(n, m):
    return -(-n // m) * m


def _params():
    return pltpu.CompilerParams(vmem_limit_bytes=VMEM_LIMIT_BYTES)


def _tile(n, cap):
    best = 0
    for d in range(LANE, min(n, cap) + 1, LANE):
        if n % d == 0:
            best = d
    return best or n


def _matmul_tiles(M, N, K, a_bytes, b_bytes, o_bytes, has_add):
    tm, tn, tk = _tile(M, 1024), _tile(N, 1024), _tile(K, 2048)

    def vmem(tm, tn, tk):
        acc = tm * tn * 4 if tk < K else 0
        return 2 * (tm * tk * a_bytes + tk * tn * b_bytes + tm * tn * (o_bytes + (4 if has_add else 0))) + acc

    while vmem(tm, tn, tk) > (VMEM_LIMIT_BYTES * 3) // 4:
        if tk > 512 and _tile(K, tk // 2) < tk:
            tk = _tile(K, tk // 2)
        elif _tile(M, tm // 2) < tm:
            tm = _tile(M, tm // 2)
        else:
            break
    return tm, tn, tk


def _matmul(a, b, *, mode, out_dtype, name, add=None, deps=()):
    if mode == "nn":
        (M, K), (K2, N) = a.shape, b.shape
    elif mode == "nt":
        (M, K), (N, K2) = a.shape, b.shape
    else:
        (K, M), (K2, N) = a.shape, b.shape
    assert K == K2, (a.shape, b.shape, mode)
    has_add = add is not None
    tm, tn, tk = _matmul_tiles(M, N, K, a.dtype.itemsize, b.dtype.itemsize, jnp.dtype(out_dtype).itemsize, has_add)
    nk = K // tk
    dn = {"nn": (((1,), (0,)), ((), ())), "nt": (((1,), (1,)), ((), ())), "tn": (((0,), (0,)), ((), ()))}[mode]
    a_spec = pl.BlockSpec((tk, tm), lambda i, j, k: (k, i)) if mode == "tn" else pl.BlockSpec((tm, tk), lambda i, j, k: (i, k))
    b_spec = pl.BlockSpec((tn, tk), lambda i, j, k: (j, k)) if mode == "nt" else pl.BlockSpec((tk, tn), lambda i, j, k: (k, j))
    o_spec = pl.BlockSpec((tm, tn), lambda i, j, k: (i, j))
    n_in = 2 + has_add + len(deps)

    def body(*refs):
        a_ref, b_ref = refs[0], refs[1]
        add_ref = refs[2] if has_add else None
        o_ref = refs[n_in]
        part = lax.dot_general(a_ref[...].astype(BF16), b_ref[...].astype(BF16), dn, preferred_element_type=F32)
        if nk == 1:
            if has_add:
                part = part + add_ref[...]
            o_ref[...] = part.astype(out_dtype)
            return
        acc_ref = refs[-1]
        kk = pl.program_id(2)

        @pl.when(kk == 0)
        def _():
            acc_ref[...] = part + add_ref[...] if has_add else part

        @pl.when(kk > 0)
        def _():
            acc_ref[...] += part

        @pl.when(kk == nk - 1)
        def _():
            o_ref[...] = acc_ref[...].astype(out_dtype)

    ins = [a, b] + ([add] if has_add else []) + list(deps)
    in_specs = ([a_spec, b_spec] + ([o_spec] if has_add else [])
                + [pl.BlockSpec(d.shape, lambda i, j, k, nd=d.ndim: (0,) * nd) for d in deps])
    return pl.pallas_call(
        body, name=name, grid=(M // tm, N // tn, nk), in_specs=in_specs, out_specs=o_spec,
        out_shape=jax.ShapeDtypeStruct((M, N), out_dtype),
        scratch_shapes=[pltpu.VMEM((tm, tn), F32)] if nk > 1 else [],
        compiler_params=_params())(*ins)


def _rowwise(fn, rows, pars, row_outs, acc_outs, *, name, tm=256, deps=()):
    R = rows[0].shape[0]
    if max(r.shape[1] for r in rows) > 4096:
        tm = tm // 2
    tm = min(tm, R)
    assert R % tm == 0
    nr, npar = len(rows), len(pars)
    nro = len(row_outs)

    def body(*refs):
        rv = [r[...] for r in refs[:nr]]
        pv = [p[...] for p in refs[nr:nr + npar]]
        outs = refs[nr + npar + len(deps):]
        ro, ao = fn(rv, pv)
        first = pl.program_id(0) == 0
        for o_ref, val in zip(outs[:nro], ro):
            o_ref[...] = val.astype(o_ref.dtype)

        @pl.when(first)
        def _():
            for o_ref, val in zip(outs[nro:], ao):
                o_ref[...] = val

        @pl.when(jnp.logical_not(first))
        def _():
            for o_ref, val in zip(outs[nro:], ao):
                o_ref[...] += val

    in_specs = ([pl.BlockSpec((tm, r.shape[1]), lambda i: (i, 0)) for r in rows]
                + [pl.BlockSpec(p.shape, lambda i, nd=p.ndim: (0,) * nd) for p in list(pars) + list(deps)])
    out_shape = [jax.ShapeDtypeStruct((R, f), dt) for f, dt in row_outs] + [jax.ShapeDtypeStruct(s, F32) for s in acc_outs]
    out_specs = ([pl.BlockSpec((tm, f), lambda i: (i, 0)) for f, _ in row_outs]
                 + [pl.BlockSpec(s, lambda i, nd=len(s): (0,) * nd) for s in acc_outs])
    res = pl.pallas_call(body, name=name, grid=(R // tm,), in_specs=in_specs, out_specs=out_specs, out_shape=out_shape,
                         compiler_params=_params())(*rows, *pars, *deps)
    return list(res)


def _bdot(a, b, mode="nn"):
    dn = {"nn": (((1,), (0,)), ((), ())), "nt": (((1,), (1,)), ((), ())), "tn": (((0,), (0,)), ((), ()))}[mode]
    return lax.dot_general(a.astype(BF16), b.astype(BF16), dn, preferred_element_type=F32)


def _sigmoid(x):
    return jax.nn.sigmoid(x)


def _softplus(x):
    return jnp.maximum(x, 0.0) + jnp.log1p(jnp.exp(-jnp.abs(x)))


def _gelu(z):
    return 0.5 * z * (1.0 + lax.erf(z * _SQRT_HALF))


def _gelu_grad(z):
    return 0.5 * (1.0 + lax.erf(z * _SQRT_HALF)) + z * jnp.exp(-0.5 * z * z) * _INV_SQRT_2PI


def _mean(x):
    return jnp.mean(x, axis=-1, keepdims=True)


def _colsum(x):
    return jnp.sum(x, axis=0, keepdims=True)


def _rms_fwd(x, g, name, deps=()):
    def fn(rv, pv):
        (xv,), (gv,) = rv, pv
        r = lax.rsqrt(_mean(xv * xv) + RMS_EPS)
        return [xv * r * gv], []
    return _rowwise(fn, [x], [g], [(x.shape[1], BF16)], [], name=name, deps=deps)[0]


def _rms_bwd(dn, x, dres, g, name, deps=()):
    def fn(rv, pv):
        (dnv, xv, drv), (gv,) = rv, pv
        r = lax.rsqrt(_mean(xv * xv) + RMS_EPS)
        yn = xv * r
        dyg = dnv * gv
        dx = drv + r * (dyg - yn * _mean(dyg * yn))
        return [dx, dx], [_colsum(dnv * yn)]
    D = x.shape[1]
    return _rowwise(fn, [dn, x, dres], [g], [(D, F32), (D, BF16)], [(1, D)], name=name, deps=deps)


def _rwkv_layout(RW, Lw, La, Lg):
    widths = [RW, RW, RW, Lw, La, Lg]
    pw = [_ceil_to(w, LANE) for w in widths]
    offs = [sum(pw[:i]) for i in range(6)]
    return widths, pw, offs, sum(pw)


def _pad_rwkv_cols(a, lay):
    widths, pw, _, _ = lay
    pieces, src = [], 0
    for w, p in zip(widths, pw):
        pieces.append(a[:, src:src + w])
        if p > w:
            pieces.append(jnp.zeros((a.shape[0], p - w), a.dtype))
        src += w
    pieces.append(a[:, src:])
    return jnp.concatenate(pieces, axis=1)


def _unpad_rwkv_cols(a, lay):
    widths, _, offs, rcp = lay
    return jnp.concatenate([a[:, o:o + w] for o, w in zip(offs, widths)] + [a[:, rcp:]], axis=1)


def _pad_rows(a, rows):
    return a if a.shape[0] == rows else jnp.concatenate([a, jnp.zeros((rows - a.shape[0], a.shape[1]), a.dtype)], axis=0)


def _token_shift(p, halo, mu, i):
    tm = p.shape[0]
    hid = lax.broadcasted_iota(jnp.int32, (SUBLANE, 1), 0)
    before = jnp.sum(jnp.where(hid == SUBLANE - 1, halo, 0.0), axis=0, keepdims=True)
    before = jnp.where(i == 0, 0.0, before)
    rid = lax.broadcasted_iota(jnp.int32, (tm, 1), 0)
    prev = jnp.where(rid == 0, before, pltpu.roll(p, 1, 0))
    d = prev - p
    return p + d * mu, d


def _rwkv_math(ps, w0, a0, k_k, k_a, wlw, wla, wlg, lay):
    _, pw, offs, _ = lay
    r, k, v, xw, xa, xg = (ps[:, offs[j]:offs[j] + pw[j]] for j in range(6))
    tw = jnp.tanh(xw)
    ww = w0 + _bdot(tw, wlw)
    lw = -jnp.exp(-_softplus(-ww) - 0.5)
    a = _sigmoid(a0 + _bdot(xa, wla))
    sg = _sigmoid(xg)
    g = _bdot(sg, wlg)
    return dict(r=r, k=k, v=v, xa=xa, tw=tw, ww=ww, lw=lw, a=a, sg=sg, g=g, kkp=k * k_k, k2=k * (1.0 + (a - 1.0) * k_a))


def _halo_specs(T, tm, width, after):
    hb = tm // SUBLANE
    last = T // SUBLANE - 1
    if after:
        return pl.BlockSpec((SUBLANE, width), lambda i: (jnp.minimum((i + 1) * hb, last), 0))
    return pl.BlockSpec((SUBLANE, width), lambda i: (jnp.maximum(i * hb - 1, 0), 0))


def _rowsum(x):
    return jnp.sum(x, axis=-1, keepdims=True)


def _kk_math(kkp):
    nrm = jnp.sqrt(_rowsum(kkp * kkp))
    inv = 1.0 / jnp.maximum(nrm, 1e-12)
    return nrm, inv, kkp * inv


def _rwkv_pre(p, mu, small, lora, lay, name):
    T, rcp = p.shape
    H = lay[0][0] // HEAD
    tm = min(128, T)

    def body(p_ref, ph_ref, mu_ref, w0_ref, a0_ref, kk_ref, ka_ref, wlw_ref, wla_ref, wlg_ref, r_o, lw_o, k2_o, v_o, aa_o, bb_o, g_o):
        ps, _ = _token_shift(p_ref[...], ph_ref[...], mu_ref[...], pl.program_id(0))
        q = _rwkv_math(ps, w0_ref[...], a0_ref[...], kk_ref[...], ka_ref[...], wlw_ref[...], wla_ref[...], wlg_ref[...], lay)
        for h in range(H):
            sl = slice(h * HEAD, (h + 1) * HEAD)
            for o_ref, key in ((r_o, "r"), (lw_o, "lw"), (k2_o, "k2"), (v_o, "v"), (g_o, "g")):
                o_ref[h] = q[key][:, sl]
            _, _, kk = _kk_math(q["kkp"][:, sl])
            aa_o[h] = -kk
            bb_o[h] = kk * q["a"][:, sl]

    whole = lambda arr: pl.BlockSpec(arr.shape, lambda i: (0, 0))
    return pl.pallas_call(
        body, name=name, grid=(T // tm,),
        in_specs=([pl.BlockSpec((tm, rcp), lambda i: (i, 0)), _halo_specs(T, tm, rcp, False), whole(mu)]
                  + [whole(s) for s in small] + [whole(w) for w in lora]),
        out_specs=[pl.BlockSpec((H, tm, HEAD), lambda i: (0, i, 0))] * 7, out_shape=[jax.ShapeDtypeStruct((H, T, HEAD), F32)] * 7,
        compiler_params=_params())(p, p, mu, *small, *lora)


def _rwkv_pre_bwd(p, mu, small, lora, hgrads, lay, name):
    T, rcp = p.shape
    widths, pw, offs, _ = lay
    RW = widths[0]
    H = RW // HEAD
    tm = min(128, T)

    def body(p_ref, ph_ref, mu_ref, w0_ref, a0_ref, kk_ref, ka_ref, wlw_ref, wla_ref, wlg_ref,
             dr1, dr2, dk1, dk2b, dv1, dv2, dlw_h, daa, dbb, dg_h,
             dps_ref, dmu_ref, dw0_ref, da0_ref, dkk_ref, dka_ref, dwlw_ref, dwla_ref, dwlg_ref,
             s_dr, s_dk2, s_dv, s_dlw, s_dkkp, s_da, s_dg):
        i = pl.program_id(0)
        ps, dprev = _token_shift(p_ref[...], ph_ref[...], mu_ref[...], i)
        k_k, k_a = kk_ref[...], ka_ref[...]
        q = _rwkv_math(ps, w0_ref[...], a0_ref[...], k_k, k_a, wlw_ref[...], wla_ref[...], wlg_ref[...], lay)
        k, a, lw, ww, tw, sg = q["k"], q["a"], q["lw"], q["ww"], q["tw"], q["sg"]
        for h in range(H):
            sl = slice(h * HEAD, (h + 1) * HEAD)
            s_dr[:, sl] = dr1[h] + dr2[h]
            s_dk2[:, sl] = dk1[h] + dk2b[h]
            s_dv[:, sl] = dv1[h] + dv2[h]
            s_dlw[:, sl] = dlw_h[h]
            s_dg[:, sl] = dg_h[h]
            nrm, inv, kk = _kk_math(q["kkp"][:, sl])
            dbb_h = dbb[h]
            dkk = dbb_h * a[:, sl] - daa[h]
            s_dkkp[:, sl] = jnp.where(nrm > 1e-12, inv * (dkk - kk * _rowsum(dkk * kk)), dkk * inv)
            s_da[:, sl] = dbb_h * kk
        dk2, dkkp, dg = s_dk2[...], s_dkkp[...], s_dg[...]
        dk = dk2 * (1.0 + (a - 1.0) * k_a) + dkkp * k_k
        da = s_da[...] + dk2 * k * k_a
        dpa = da * a * (1.0 - a)
        dww = s_dlw[...] * lw * _sigmoid(-ww)
        dxa = _bdot(dpa, wla_ref[...], "nt")
        dxw = _bdot(dww, wlw_ref[...], "nt") * (1.0 - tw * tw)
        dxg = _bdot(dg, wlg_ref[...], "nt") * sg * (1.0 - sg)
        segs = (s_dr[...], dk, s_dv[...], dxw, dxa, dxg)
        sums = [dmu_ref, dw0_ref, da0_ref, dkk_ref, dka_ref, dwlw_ref, dwla_ref, dwlg_ref]

        @pl.when(i == 0)
        def _():
            for s in sums:
                s[...] = jnp.zeros_like(s)

        for j, seg in enumerate(segs):
            sl = slice(offs[j], offs[j] + pw[j])
            dps_ref[:, sl] = seg
            dmu_ref[:, sl] += _colsum(seg * dprev[:, sl])
        dw0_ref[...] += _colsum(dww)
        da0_ref[...] += _colsum(dpa)
        dkk_ref[...] += _colsum(dkkp * k)
        dka_ref[...] += _colsum(dk2 * k * (a - 1.0))
        dwlw_ref[...] += _bdot(tw, dww, "tn")
        dwla_ref[...] += _bdot(q["xa"], dpa, "tn")
        dwlg_ref[...] += _bdot(sg, dg, "tn")

    whole = lambda arr: pl.BlockSpec(arr.shape, lambda i: (0, 0))
    row = lambda w: pl.BlockSpec((tm, w), lambda i: (i, 0))
    acc_shapes = [(1, rcp), (1, RW), (1, RW), (1, RW), (1, RW)] + [w.shape for w in lora]
    return pl.pallas_call(
        body, name=name, grid=(T // tm,),
        in_specs=([row(rcp), _halo_specs(T, tm, rcp, False), whole(mu)] + [whole(s) for s in small] + [whole(w) for w in lora]
                  + [pl.BlockSpec((H, tm, HEAD), lambda i: (0, i, 0))] * 10),
        out_specs=[row(rcp)] + [pl.BlockSpec(s, lambda i: (0, 0)) for s in acc_shapes],
        out_shape=[jax.ShapeDtypeStruct((T, rcp), F32)] + [jax.ShapeDtypeStruct(s, F32) for s in acc_shapes],
        scratch_shapes=[pltpu.VMEM((tm, RW), F32)] * 7, compiler_params=_params())(p, p, mu, *small, *lora, *hgrads)


def _shift_bwd(dps, mu, name):
    T, rcp = dps.shape
    tm = min(256, T)
    nt = T // tm

    def body(d_ref, dh_ref, mu_ref, o_ref):
        i = pl.program_id(0)
        d = d_ref[...]
        hid = lax.broadcasted_iota(jnp.int32, (SUBLANE, 1), 0)
        after = jnp.sum(jnp.where(hid == 0, dh_ref[...], 0.0), axis=0, keepdims=True)
        after = jnp.where(i == nt - 1, 0.0, after)
        rid = lax.broadcasted_iota(jnp.int32, (tm, 1), 0)
        nxt = jnp.where(rid == tm - 1, after, pltpu.roll(d, tm - 1, 0))
        mu_v = mu_ref[...]
        o_ref[...] = (d * (1.0 - mu_v) + nxt * mu_v).astype(BF16)

    row = pl.BlockSpec((tm, rcp), lambda i: (i, 0))
    return pl.pallas_call(
        body, name=name, grid=(nt,), in_specs=[row, _halo_specs(T, tm, rcp, True), pl.BlockSpec(mu.shape, lambda i: (0, 0))],
        out_specs=row, out_shape=jax.ShapeDtypeStruct((T, rcp), BF16), compiler_params=_params())(dps, dps, mu)


def _head_post_math(y, r, k2, v, lg, lb, rk):
    yc = y - _mean(y)
    rstd = lax.rsqrt(_mean(yc * yc) + LNX_EPS)
    yn = yc * rstd
    s = _rowsum(r * k2 * rk)
    return yn, rstd, yn * lg + lb + s * v, s


def _head_post(y, r, k2, v, g, hp, name):
    H, T, _ = y.shape
    tm = min(128, T)

    def body(y_ref, r_ref, k_ref, v_ref, g_ref, lg_ref, lb_ref, rk_ref, o_ref):
        _, _, t, _ = _head_post_math(y_ref[...], r_ref[...], k_ref[...], v_ref[...], lg_ref[...], lb_ref[...], rk_ref[...])
        out = (t * g_ref[...]).astype(BF16)
        for h in range(H):
            o_ref[:, h * HEAD:(h + 1) * HEAD] = out[h]

    blk = pl.BlockSpec((H, tm, HEAD), lambda i: (0, i, 0))
    par = pl.BlockSpec((H, 1, HEAD), lambda i: (0, 0, 0))
    return pl.pallas_call(
        body, name=name, grid=(T // tm,), in_specs=[blk] * 5 + [par] * 3, out_specs=pl.BlockSpec((tm, H * HEAD), lambda i: (i, 0)),
        out_shape=jax.ShapeDtypeStruct((T, H * HEAD), BF16), compiler_params=_params())(y, r, k2, v, g, *hp)


def _head_post_bwd(dya, y, r, k2, v, g, hp, name, deps=()):
    H, T, _ = y.shape
    tm = min(128, T)
    hsum = lambda t: jnp.sum(t, axis=1, keepdims=True)

    def body(d_ref, y_ref, r_ref, k_ref, v_ref, g_ref, lg_ref, lb_ref, rk_ref, *rest):
        outs, d_s = rest[len(deps):len(deps) + 8], rest[-1]
        for h in range(H):
            d_s[h] = d_ref[:, h * HEAD:(h + 1) * HEAD]
        d_v, r_v, k_v, v_v, lg, rk = d_s[...], r_ref[...], k_ref[...], v_ref[...], lg_ref[...], rk_ref[...]
        yn, rstd, t, s = _head_post_math(y_ref[...], r_v, k_v, v_v, lg, lb_ref[...], rk)
        dyo = d_v * g_ref[...]
        dyn = dyo * lg
        ds = _rowsum(dyo * v_v)
        vals = (rstd * (dyn - _mean(dyn) - yn * _mean(dyn * yn)), ds * k_v * rk, ds * r_v * rk, dyo * s, d_v * t)
        for o_ref, val in zip(outs[:5], vals):
            o_ref[...] = val
        sums = (hsum(dyo * yn), hsum(dyo), hsum(ds * r_v * k_v))
        first = pl.program_id(0) == 0

        @pl.when(first)
        def _():
            for o_ref, val in zip(outs[5:], sums):
                o_ref[...] = val

        @pl.when(jnp.logical_not(first))
        def _():
            for o_ref, val in zip(outs[5:], sums):
                o_ref[...] += val

    blk = pl.BlockSpec((H, tm, HEAD), lambda i: (0, i, 0))
    par = pl.BlockSpec((H, 1, HEAD), lambda i: (0, 0, 0))
    return pl.pallas_call(
        body, name=name, grid=(T // tm,),
        in_specs=([pl.BlockSpec((tm, H * HEAD), lambda i: (i, 0))] + [blk] * 5 + [par] * 3
                  + [pl.BlockSpec(d.shape, lambda i, nd=d.ndim: (0,) * nd) for d in deps]),
        out_specs=[blk] * 5 + [par] * 3,
        out_shape=[jax.ShapeDtypeStruct((H, T, HEAD), F32)] * 5 + [jax.ShapeDtypeStruct((H, 1, HEAD), F32)] * 3,
        scratch_shapes=[pltpu.VMEM((H, tm, HEAD), F32)], compiler_params=_params())(dya, y, r, k2, v, g, *hp, *deps)


def _bmm(x, y, mode):
    dn = {"nn": (((2,), (1,)), ((0,), (0,))), "nt": (((2,), (2,)), ((0,), (0,))), "tn": (((1,), (1,)), ((0,), (0,)))}[mode]
    (xh, xl), (yh, yl) = _split(x), _split(y)
    dot = lambda p, q: lax.dot_general(p, q, dn, preferred_element_type=F32)
    out = dot(xh, yh)
    if yl is not None:
        out = out + dot(xh, yl)
    if xl is not None:
        out = out + dot(xl, yh)
    return out


def _split(x):
    if isinstance(x, tuple):
        return x
    hi = x.astype(BF16)
    return hi, (x - hi.astype(F32)).astype(BF16)


def _exact(x):
    return x.astype(BF16), None


def _wkv_chunk(r, lw, k, v, a, b):
    hb, C, _ = r.shape
    ti = lax.broadcasted_iota(jnp.int32, (C, C), 0)
    si = lax.broadcasted_iota(jnp.int32, (C, C), 1)
    linc, lstr, eye = (ti >= si).astype(F32), (ti > si).astype(F32), (ti == si).astype(F32)
    lincb = _exact(jnp.broadcast_to(linc, (hb, C, C)))
    lstrb = _exact(jnp.broadcast_to(lstr, (hb, C, C)))
    ones = _exact(jnp.ones_like(v))
    lws = _split(lw)
    ci = _bmm(lincb, lws, "nn")
    cC = jnp.sum(lw, axis=1, keepdims=True)
    gi, ge, gn, gr = jnp.exp(ci), jnp.exp(ci - lw), jnp.exp(-ci), jnp.exp(cC - ci)
    q = dict(At=a * ge, Rt=r * gi, Bt=b * gn, Kt=k * gn, Bh=b * gr, Kh=k * gr)
    s = {key: _split(val) for key, val in q.items()}
    s["v"] = _split(v)
    q["A_ab"] = _bmm(s["At"], s["Bt"], "nt") * lstr
    for key, lhs, rhs, mask in (("A_ak", "At", "Kt", lstr), ("A_rb", "Rt", "Bt", linc), ("A_rk", "Rt", "Kt", linc)):
        q[key] = _bmm(s[lhs], s[rhs], "nt") * mask
        s[key] = _split(q[key])
    Tm = eye + q["A_ab"]
    Pw = _split(q["A_ab"])
    n = 1
    while 2 * n < C:
        Pw = _split(_bmm(Pw, Pw, "nn"))
        Tm = Tm + _bmm(Tm, Pw, "nn")
        n *= 2
    s["Tm"] = _split(Tm)
    gC = jnp.exp(_bmm(lws, ones, "tn"))
    q.update(gi=gi, ge=ge, gn=gn, gr=gr, linc=linc, lstr=lstr, lincb=lincb, lstrb=lstrb, gC=gC, ones=ones, s=s)
    return q


def _wkv_fwd(r, lw, k, v, a, b, name):
    H, T, N = r.shape
    C = min(WKV_CHUNK, T)
    nc = T // C
    hb = _pick(H, (8, 4, 2))

    def body(r_ref, lw_ref, k_ref, v_ref, a_ref, b_ref, y_ref, st_ref, h_ref):
        @pl.when(pl.program_id(1) == 0)
        def _():
            h_ref[...] = jnp.zeros_like(h_ref)

        H0 = h_ref[...]
        st_ref[0] = H0
        q = _wkv_chunk(r_ref[...], lw_ref[...], k_ref[...], v_ref[...], a_ref[...], b_ref[...])
        s = q["s"]
        H0s = _split(H0)
        U = _split(_bmm(s["Tm"], _bmm(s["At"], H0s, "nn") + _bmm(s["A_ak"], s["v"], "nn"), "nn"))
        y_ref[...] = _bmm(s["Rt"], H0s, "nn") + _bmm(s["A_rb"], U, "nn") + _bmm(s["A_rk"], s["v"], "nn")
        h_ref[...] = q["gC"] * H0 + _bmm(s["Bh"], U, "tn") + _bmm(s["Kh"], s["v"], "tn")

    blk = pl.BlockSpec((hb, C, N), lambda h, c: (h, c, 0))
    return pl.pallas_call(
        body, name=name, grid=(H // hb, nc), in_specs=[blk] * 6,
        out_specs=[blk, pl.BlockSpec((1, hb, N, N), lambda h, c: (c, h, 0, 0))],
        out_shape=[jax.ShapeDtypeStruct((H, T, N), F32), jax.ShapeDtypeStruct((nc, H, N, N), F32)],
        scratch_shapes=[pltpu.VMEM((hb, N, N), F32)], compiler_params=_params())(r, lw, k, v, a, b)


def _wkv_bwd(r, lw, k, v, a, b, states, dy, name):
    H, T, N = r.shape
    C = min(WKV_CHUNK, T)
    nc = T // C
    hb = _pick(H, (8, 4, 2))

    def body(r_ref, lw_ref, k_ref, v_ref, a_ref, b_ref, st_ref, dy_ref, dr_ref, dlw_ref, dk_ref, dv_ref, da_ref, db_ref, dh_ref):
        @pl.when(pl.program_id(1) == 0)
        def _():
            dh_ref[...] = jnp.zeros_like(dh_ref)

        dHC = dh_ref[...]
        H0 = st_ref[0]
        q = _wkv_chunk(r_ref[...], lw_ref[...], k_ref[...], v_ref[...], a_ref[...], b_ref[...])
        s, gC = q["s"], q["gC"]
        H0s, dHs, dY = _split(H0), _split(dHC), _split(dy_ref[...])
        U = _split(_bmm(s["Tm"], _bmm(s["At"], H0s, "nn") + _bmm(s["A_ak"], s["v"], "nn"), "nn"))
        dU = _bmm(s["A_rb"], dY, "tn") + _bmm(s["Bh"], dHs, "nn")
        dP = _split(_bmm(s["Tm"], dU, "tn"))
        dv_ref[...] = _bmm(s["A_rk"], dY, "tn") + _bmm(s["Kh"], dHs, "nn") + _bmm(s["A_ak"], dP, "tn")
        dh_ref[...] = _bmm(s["Rt"], dY, "tn") + gC * dHC + _bmm(s["At"], dP, "tn")
        dA_rb = _split(_bmm(dY, U, "nt") * q["linc"])
        dA_rk = _split(_bmm(dY, s["v"], "nt") * q["linc"])
        dA_ab = _split(_bmm(dP, U, "nt") * q["lstr"])
        dA_ak = _split(_bmm(dP, s["v"], "nt") * q["lstr"])
        dRt = _bmm(dY, H0s, "nt") + _bmm(dA_rb, s["Bt"], "nn") + _bmm(dA_rk, s["Kt"], "nn")
        dAt = _bmm(dP, H0s, "nt") + _bmm(dA_ab, s["Bt"], "nn") + _bmm(dA_ak, s["Kt"], "nn")
        dBt = _bmm(dA_ab, s["At"], "tn") + _bmm(dA_rb, s["Rt"], "tn")
        dKt = _bmm(dA_ak, s["At"], "tn") + _bmm(dA_rk, s["Rt"], "tn")
        dBh = _bmm(U, dHs, "nt")
        dKh = _bmm(s["v"], dHs, "nt")
        dr_ref[...] = dRt * q["gi"]
        da_ref[...] = dAt * q["ge"]
        db_ref[...] = dBt * q["gn"] + dBh * q["gr"]
        dk_ref[...] = dKt * q["gn"] + dKh * q["gr"]
        tail = dBh * q["Bh"] + dKh * q["Kh"]
        dci = dRt * q["Rt"] - dBt * q["Bt"] - dKt * q["Kt"] - tail
        dcC = jnp.sum(tail, axis=1, keepdims=True) + _bmm(q["ones"], H0 * dHC * gC, "nt")
        dlw_ref[...] = _bmm(q["lincb"], dci, "tn") + _bmm(q["lstrb"], dAt * q["At"], "tn") + dcC

    blk = pl.BlockSpec((hb, C, N), lambda h, c: (h, nc - 1 - c, 0))
    st = pl.BlockSpec((1, hb, N, N), lambda h, c: (nc - 1 - c, h, 0, 0))
    return pl.pallas_call(
        body, name=name, grid=(H // hb, nc), in_specs=[blk] * 6 + [st, blk], out_specs=[blk] * 6,
        out_shape=[jax.ShapeDtypeStruct((H, T, N), F32)] * 6,
        scratch_shapes=[pltpu.VMEM((hb, N, N), F32)], compiler_params=_params())(r, lw, k, v, a, b, states, dy)


def _sgu_ln(z, SW, lng, lnb):
    ge = _gelu(z)
    u, vv = ge[:, :SW], ge[:, SW:]
    xc = vv - _mean(vv)
    rstd = lax.rsqrt(_mean(xc * xc) + LN_EPS)
    vn = xc * rstd
    return u, vn, rstd, vn * lng + lnb


def _causal(ws_ref, g):
    ti = lax.broadcasted_iota(jnp.int32, (SGU_CHUNK, SGU_CHUNK), 0)
    si = lax.broadcasted_iota(jnp.int32, (SGU_CHUNK, SGU_CHUNK), 1)
    return ti >= si, jnp.where(ti >= si, ws_ref[g], 0.0).astype(BF16)


def _sgu_fwd(z, lng, lnb, ws, bexp, name):
    T, SW = z.shape[0], z.shape[1] // 2
    G = ws.shape[0]
    tr = min(256, T)
    nch = tr // SGU_CHUNK

    def body(z_ref, lng_ref, lnb_ref, ws_ref, be_ref, o_ref):
        u, _, _, vl = _sgu_ln(z_ref[...], SW, lng_ref[...], lnb_ref[...])
        for g in range(G):
            cs = slice(g * SGU_GROUP, (g + 1) * SGU_GROUP)
            _, wc = _causal(ws_ref, g)
            for n in range(nch):
                rs = slice(n * SGU_CHUNK, (n + 1) * SGU_CHUNK)
                m = jnp.dot(wc, vl[rs, cs].astype(BF16), preferred_element_type=F32) + be_ref[:, cs]
                o_ref[rs, cs] = (u[rs, cs] * m).astype(BF16)

    whole = lambda arr: pl.BlockSpec(arr.shape, lambda i, nd=arr.ndim: (0,) * nd)
    return pl.pallas_call(
        body, name=name, grid=(T // tr,),
        in_specs=[pl.BlockSpec((tr, 2 * SW), lambda i: (i, 0)), whole(lng), whole(lnb), whole(ws), whole(bexp)],
        out_specs=pl.BlockSpec((tr, SW), lambda i: (i, 0)), out_shape=jax.ShapeDtypeStruct((T, SW), BF16),
        compiler_params=_params())(z, lng, lnb, ws, bexp)


def _sgu_bwd(z, dyb, lng, lnb, ws, bexp, name):
    T, SW = z.shape[0], z.shape[1] // 2
    G = ws.shape[0]
    tr = min(256, T)
    nch = tr // SGU_CHUNK
    nt = T // tr

    def body(z_ref, dy_ref, lng_ref, lnb_ref, ws_ref, be_ref, dz_ref, dlg_ref, dlb_ref, dws_ref, db_ref, du_s, dvl_s, dbacc_s):
        i = pl.program_id(0)
        zv = z_ref[...]
        lng_v = lng_ref[...]
        u, vn, rstd, vl = _sgu_ln(zv, SW, lng_v, lnb_ref[...])

        @pl.when(i == 0)
        def _():
            for s in (dlg_ref, dlb_ref, dws_ref, dbacc_s):
                s[...] = jnp.zeros_like(s)

        for g in range(G):
            cs = slice(g * SGU_GROUP, (g + 1) * SGU_GROUP)
            tri, wc = _causal(ws_ref, g)
            for n in range(nch):
                rs = slice(n * SGU_CHUNK, (n + 1) * SGU_CHUNK)
                blk = vl[rs, cs].astype(BF16)
                m = jnp.dot(wc, blk, preferred_element_type=F32) + be_ref[:, cs]
                dyv = dy_ref[rs, cs]
                du_s[rs, cs] = dyv * m
                dm = dyv * u[rs, cs]
                dvl_s[rs, cs] = _bdot(wc, dm, "tn")
                dws_ref[g] += jnp.where(tri, _bdot(dm, blk, "nt"), 0.0)
                dbacc_s[:, cs] += dm

        dvl = dvl_s[...]
        dlg_ref[...] += _colsum(dvl * vn)
        dlb_ref[...] += _colsum(dvl)
        dvn = dvl * lng_v
        dvv = rstd * (dvn - _mean(dvn) - vn * _mean(dvn * vn))
        gp = _gelu_grad(zv)
        dz_ref[:, :SW] = (du_s[...] * gp[:, :SW]).astype(BF16)
        dz_ref[:, SW:] = (dvv * gp[:, SW:]).astype(BF16)

        @pl.when(i == nt - 1)
        def _():
            lane = lax.broadcasted_iota(jnp.int32, (SGU_CHUNK, LANE), 1)
            out = jnp.zeros((SGU_CHUNK, LANE), F32)
            for g in range(G):
                col = jnp.sum(dbacc_s[:, g * SGU_GROUP:(g + 1) * SGU_GROUP], axis=1, keepdims=True)
                out = jnp.where(lane == g, col, out)
            db_ref[...] = out

    whole = lambda arr: pl.BlockSpec(arr.shape, lambda i, nd=arr.ndim: (0,) * nd)
    acc_shapes = [(1, SW), (1, SW), ws.shape, (SGU_CHUNK, LANE)]
    return pl.pallas_call(
        body, name=name, grid=(nt,),
        in_specs=[pl.BlockSpec((tr, 2 * SW), lambda i: (i, 0)), pl.BlockSpec((tr, SW), lambda i: (i, 0)),
                  whole(lng), whole(lnb), whole(ws), whole(bexp)],
        out_specs=[pl.BlockSpec((tr, 2 * SW), lambda i: (i, 0))] + [pl.BlockSpec(s, lambda i, nd=len(s): (0,) * nd) for s in acc_shapes],
        out_shape=[jax.ShapeDtypeStruct((T, 2 * SW), BF16)] + [jax.ShapeDtypeStruct(s, F32) for s in acc_shapes],
        scratch_shapes=[pltpu.VMEM((tr, SW), F32), pltpu.VMEM((tr, SW), F32), pltpu.VMEM((SGU_CHUNK, SW), F32)],
        compiler_params=_params())(z, dyb, lng, lnb, ws, bexp)


_HBM = pl.BlockSpec(memory_space=pltpu.HBM)
_SEM = pl.BlockSpec(memory_space=pltpu.SEMAPHORE)
_DATAFLOW = pltpu.SideEffectType.DATAFLOW_SIDE_EFFECTING


def _mesh_place():
    x, y, c = lax.axis_index("x"), lax.axis_index("y"), lax.axis_index("c")
    return x, y, c, 4 * x + 2 * y + c


def _peer(x, y, c, rel):
    px = 1 - x if rel & 4 else x
    py = 1 - y if rel & 2 else y
    pc = 1 - c if rel & 1 else c
    return (px, py, pc), 4 * px + 2 * py + pc


def _exchange_start(groups, name):
    flat = [t for g in groups for t in g]
    sizes = [len(g) for g in groups]
    n, ng = len(flat), len(groups)
    srcs = [pltpu.with_memory_space_constraint(a, pltpu.HBM) for a, _ in flat]
    lands = [pltpu.with_memory_space_constraint(lax.empty(((N_DEV,) + a.shape) if isg else a.shape, a.dtype), pltpu.HBM)
             for a, isg in flat]

    def body(*refs):
        ins, lnd, sems, token = refs[:n], refs[n:2 * n], refs[2 * n:2 * n + 3 * ng], refs[-1]
        x, y, c, me = _mesh_place()
        j0 = 0
        for gi, sz in enumerate(sizes):
            for rel in range(1, N_DEV):
                dev, slot = _peer(x, y, c, rel)
                for jj in range(sz):
                    j = j0 + jj
                    pltpu.make_async_remote_copy(
                        src_ref=ins[j] if flat[j][1] else ins[j].at[slot], dst_ref=lnd[j].at[me],
                        send_sem=sems[3 * gi].at[jj * (N_DEV - 1) + rel - 1], recv_sem=sems[3 * gi + 1].at[jj * (N_DEV - 1) + rel - 1],
                        device_id=dev, device_id_type=pl.DeviceIdType.MESH).start()
            for jj in range(sz):
                j = j0 + jj
                pltpu.make_async_copy(ins[j] if flat[j][1] else ins[j].at[me], lnd[j].at[me], sems[3 * gi + 2].at[jj]).start()
            j0 += sz
        token[...] = jnp.zeros_like(token)

    sem_shapes = [pltpu.SemaphoreType.DMA((k,)) for sz in sizes for k in (sz * (N_DEV - 1), sz * (N_DEV - 1), sz)]
    res = pl.pallas_call(
        body, name=name,
        out_shape=(*sem_shapes, *[pltpu.HBM(a.shape, a.dtype) for a in srcs], *[pltpu.HBM(a.shape, a.dtype) for a in lands],
                   jax.ShapeDtypeStruct((SUBLANE, LANE), F32)),
        in_specs=[_HBM] * (2 * n), out_specs=(*[_SEM] * (3 * ng), *[_HBM] * (2 * n), pl.BlockSpec(memory_space=pltpu.VMEM)),
        input_output_aliases={i: 3 * ng + i for i in range(2 * n)},
        compiler_params=pltpu.CompilerParams(has_side_effects=_DATAFLOW))(*srcs, *lands)
    sems, thru, token = res[:3 * ng], res[3 * ng:3 * ng + 2 * n], res[-1]
    handle, j0 = [], 0
    for gi, sz in enumerate(sizes):
        handle.append(dict(kinds=[k for _, k in groups[gi]], srcs=list(thru[j0:j0 + sz]), lands=list(thru[n + j0:n + j0 + sz]),
                           sems=list(sems[3 * gi:3 * gi + 3])))
        j0 += sz
    return handle, token


def _exchange_wait(group, after, name):
    kinds, sz = group["kinds"], len(group["kinds"])

    def body(*refs):
        ins, lnd, (ssem, rsem, lsem) = refs[:sz], refs[sz:2 * sz], refs[2 * sz:2 * sz + 3]
        x, y, c, me = _mesh_place()
        for rel in range(1, N_DEV):
            dev, slot = _peer(x, y, c, rel)
            for jj in range(sz):
                cp = pltpu.make_async_remote_copy(
                    src_ref=ins[jj] if kinds[jj] else ins[jj].at[slot], dst_ref=lnd[jj].at[slot],
                    send_sem=ssem.at[jj * (N_DEV - 1) + rel - 1], recv_sem=rsem.at[jj * (N_DEV - 1) + rel - 1],
                    device_id=dev, device_id_type=pl.DeviceIdType.MESH)
                cp.wait_send()
                cp.wait_recv()
        for jj in range(sz):
            pltpu.make_async_copy(ins[jj] if kinds[jj] else ins[jj].at[me], lnd[jj].at[me], lsem.at[jj]).wait()

    arrays = group["srcs"] + group["lands"]
    res = pl.pallas_call(
        body, name=name, out_shape=[pltpu.HBM(a.shape, a.dtype) for a in arrays],
        in_specs=[_HBM] * (2 * sz) + [_SEM] * 3 + [pl.BlockSpec(memory_space=pl.ANY)], out_specs=[_HBM] * (2 * sz),
        input_output_aliases={i: i for i in range(2 * sz)},
        compiler_params=pltpu.CompilerParams(has_side_effects=_DATAFLOW))(*arrays, *group["sems"], after)
    return list(res[sz:])


def _adamw(w, m, v, gparts, name):
    R, C = w.shape
    tm = _pick(R, (256, 128, 64, 32, 16, 8))

    def body(w_ref, m_ref, v_ref, g_ref, go, do, mo, vo):
        g = g_ref[0].astype(F32)
        for j in range(1, N_DEV):
            g = g + g_ref[j].astype(F32)
        mn = ADAM_B1 * m_ref[...] + (1.0 - ADAM_B1) * g
        vn = ADAM_B2 * v_ref[...] + (1.0 - ADAM_B2) * (g * g)
        m_hat = mn / (1.0 - ADAM_B1 ** ADAM_STEP)
        v_hat = vn / (1.0 - ADAM_B2 ** ADAM_STEP)
        go[...] = g
        do[...] = -ADAM_LR * (m_hat / (jnp.sqrt(v_hat) + ADAM_EPS) + ADAM_WD * w_ref[...])
        mo[...] = mn
        vo[...] = vn

    row = pl.BlockSpec((tm, C), lambda i: (i, 0))
    return pl.pallas_call(
        body, name=name, grid=(R // tm,), in_specs=[row, row, row, pl.BlockSpec((N_DEV, tm, C), lambda i: (0, i, 0))],
        out_specs=[row] * 4, out_shape=[jax.ShapeDtypeStruct((R, C), F32)] * 4, compiler_params=_params())(w, m, v, gparts)


def _pack(arrays):
    flat = []
    for a in arrays:
        f = a.reshape(-1)
        pad = _ceil_to(f.shape[0], LANE) - f.shape[0]
        flat.append(jnp.concatenate([f, jnp.zeros((pad,), f.dtype)]) if pad else f)
    buf = jnp.concatenate(flat)
    rows = _ceil_to(buf.shape[0] // LANE, 64)
    buf = jnp.concatenate([buf, jnp.zeros((rows * LANE - buf.shape[0],), buf.dtype)])
    return buf.reshape(rows, LANE)


def _unpack(buf, shapes):
    flat, out, off = buf.reshape(-1), [], 0
    for s in shapes:
        size = 1
        for d in s:
            size *= d
        out.append(flat[off:off + size].reshape(s))
        off += _ceil_to(size, LANE)
    return out


def kernel(x, norm_mix_g, w_in, shift_mu, w0, w_lora_up, a0, a_lora_up, g_lora_up, k_k, k_a, r_k, lnx_g, lnx_b, w_proj_rwkv, sgu_ln_g, sgu_ln_b, sgu_w, sgu_b, w_proj_sgu, w_out, norm_ffn_g, w_ffn_gate, w_ffn_up, w_ffn_down, norm_final_g, loss_target, m_norm_mix_g, m_w_in, m_shift_mu, m_w0, m_w_lora_up, m_a0, m_a_lora_up, m_g_lora_up, m_k_k, m_k_a, m_r_k, m_lnx_g, m_lnx_b, m_w_proj_rwkv, m_sgu_ln_g, m_sgu_ln_b, m_sgu_w, m_sgu_b, m_w_proj_sgu, m_w_out, m_norm_ffn_g, m_w_ffn_gate, m_w_ffn_up, m_w_ffn_down, m_norm_final_g, v_norm_mix_g, v_w_in, v_shift_mu, v_w0, v_w_lora_up, v_a0, v_a_lora_up, v_g_lora_up, v_k_k, v_k_a, v_r_k, v_lnx_g, v_lnx_b, v_w_proj_rwkv, v_sgu_ln_g, v_sgu_ln_b, v_sgu_w, v_sgu_b, v_w_proj_sgu, v_w_out, v_norm_ffn_g, v_w_ffn_gate, v_w_ffn_up, v_w_ffn_down, v_norm_final_g):
    weights = dict(norm_mix_g=norm_mix_g, w_in=w_in, shift_mu=shift_mu, w0=w0, w_lora_up=w_lora_up, a0=a0, a_lora_up=a_lora_up,
                   g_lora_up=g_lora_up, k_k=k_k, k_a=k_a, r_k=r_k, lnx_g=lnx_g, lnx_b=lnx_b, w_proj_rwkv=w_proj_rwkv,
                   sgu_ln_g=sgu_ln_g, sgu_ln_b=sgu_ln_b, sgu_w=sgu_w, sgu_b=sgu_b, w_proj_sgu=w_proj_sgu, w_out=w_out,
                   norm_ffn_g=norm_ffn_g, w_ffn_gate=w_ffn_gate, w_ffn_up=w_ffn_up, w_ffn_down=w_ffn_down, norm_final_g=norm_final_g)
    m_in = dict(norm_mix_g=m_norm_mix_g, w_in=m_w_in, shift_mu=m_shift_mu, w0=m_w0, w_lora_up=m_w_lora_up, a0=m_a0,
                a_lora_up=m_a_lora_up, g_lora_up=m_g_lora_up, k_k=m_k_k, k_a=m_k_a, r_k=m_r_k, lnx_g=m_lnx_g, lnx_b=m_lnx_b,
                w_proj_rwkv=m_w_proj_rwkv, sgu_ln_g=m_sgu_ln_g, sgu_ln_b=m_sgu_ln_b, sgu_w=m_sgu_w, sgu_b=m_sgu_b,
                w_proj_sgu=m_w_proj_sgu, w_out=m_w_out, norm_ffn_g=m_norm_ffn_g, w_ffn_gate=m_w_ffn_gate, w_ffn_up=m_w_ffn_up,
                w_ffn_down=m_w_ffn_down, norm_final_g=m_norm_final_g)
    v_in = dict(norm_mix_g=v_norm_mix_g, w_in=v_w_in, shift_mu=v_shift_mu, w0=v_w0, w_lora_up=v_w_lora_up, a0=v_a0,
                a_lora_up=v_a_lora_up, g_lora_up=v_g_lora_up, k_k=v_k_k, k_a=v_k_a, r_k=v_r_k, lnx_g=v_lnx_g, lnx_b=v_lnx_b,
                w_proj_rwkv=v_w_proj_rwkv, sgu_ln_g=v_sgu_ln_g, sgu_ln_b=v_sgu_ln_b, sgu_w=v_sgu_w, sgu_b=v_sgu_b,
                w_proj_sgu=v_w_proj_sgu, w_out=v_w_out, norm_ffn_g=v_norm_ffn_g, w_ffn_gate=v_w_ffn_gate, w_ffn_up=v_w_ffn_up,
                w_ffn_down=v_w_ffn_down, norm_final_g=v_norm_final_g)
    names = list(weights)
    col_sharded = ("w_in", "w_lora_up", "a_lora_up", "g_lora_up", "w_proj_rwkv", "w_proj_sgu", "w_ffn_gate", "w_ffn_up")
    row_sharded = ("w_out", "w_ffn_down")
    sharded = [n for n in names if n in col_sharded or n in row_sharded]
    small = [n for n in names if n not in sharded]

    xs, tgt = x[0], loss_target[0]
    T, D = xs.shape
    RW = w0.shape[1]
    H = RW // HEAD
    SW = sgu_ln_g.shape[1]
    G = sgu_w.shape[1]
    lay = _rwkv_layout(RW, w_lora_up.shape[1], a_lora_up.shape[1], g_lora_up.shape[1])
    _, pw, _, rcp = lay

    gather_groups = [["w_in", "w_lora_up", "a_lora_up", "g_lora_up"], ["w_proj_rwkv", "w_proj_sgu", "w_out"],
                     ["w_ffn_gate", "w_ffn_up", "w_ffn_down"]]
    gather, gather_token = _exchange_start([[(weights[n][0].astype(BF16), True) for n in grp] for grp in gather_groups], "gather_start")
    full = {}

    def take_weights(gi, after, name):
        for n, g in zip(gather_groups[gi], _exchange_wait(gather[gi], after, name)):
            full[n] = g.transpose(1, 0, 2).reshape(g.shape[1], -1) if n in col_sharded else g.reshape(-1, g.shape[2])

    n1 = _rms_fwd(xs, norm_mix_g, "rms_mix", deps=[gather_token])
    take_weights(0, n1, "gather_wait_in")
    W_in = _pad_rwkv_cols(full["w_in"], lay)
    o_z, o_ga, o_gb = rcp, rcp + 2 * SW, rcp + 2 * SW + D
    W_r, W_z, W_ga, W_gb = W_in[:, :o_z], W_in[:, o_z:o_ga], W_in[:, o_ga:o_gb], W_in[:, o_gb:]
    lora = [_pad_rows(full["w_lora_up"], pw[3]), _pad_rows(full["a_lora_up"], pw[4]), _pad_rows(full["g_lora_up"], pw[5])]
    mu_p = _pad_rwkv_cols(shift_mu, lay)
    rsmall = [w0, a0, k_k, k_a]
    hp = [lnx_g.reshape(H, 1, HEAD), lnx_b.reshape(H, 1, HEAD), r_k.reshape(H, 1, HEAD)]
    ws = sgu_w[0]
    bexp = jnp.repeat(sgu_b[0].T, SGU_GROUP, axis=1)
    gf = norm_final_g.reshape(1, D)

    p = _matmul(n1, W_r, mode="nn", out_dtype=F32, name="proj_rwkv")
    z = _matmul(n1, W_z, mode="nn", out_dtype=F32, name="proj_sgu")
    ga = _matmul(n1, W_ga, mode="nn", out_dtype=F32, name="proj_gate_a")
    gb = _matmul(n1, W_gb, mode="nn", out_dtype=F32, name="proj_gate_b")
    r_h, lw_h, k2_h, v_h, aa_h, bb_h, g_h = _rwkv_pre(p, mu_p, rsmall, lora, lay, "rwkv_pre")
    wkv_in = [r_h, lw_h, k2_h, v_h, aa_h, bb_h]
    y_h, states = _wkv_fwd(*wkv_in, "wkv_fwd")
    ya = _head_post(y_h, r_h, k2_h, v_h, g_h, hp, "head_post")
    yb = _sgu_fwd(z, sgu_ln_g, sgu_ln_b, ws, bexp, "sgu_fwd")
    take_weights(1, ya, "gather_wait_proj")
    pa = _matmul(ya, full["w_proj_rwkv"], mode="nn", out_dtype=F32, name="proj_a")
    pb = _matmul(yb, full["w_proj_sgu"], mode="nn", out_dtype=F32, name="proj_b")

    def merge_fn(rv, pv):
        ga_v, gb_v, pa_v, pb_v = rv
        return [_sigmoid(ga_v) * pa_v + _sigmoid(gb_v) * pb_v], []
    merged = _rowwise(merge_fn, [ga, gb, pa, pb], [], [(D, BF16)], [], name="merge")[0]
    h1 = _matmul(merged, full["w_out"], mode="nn", out_dtype=F32, name="out_proj", add=xs)
    n2 = _rms_fwd(h1, norm_ffn_g, "rms_ffn")
    take_weights(2, n2, "gather_wait_ffn")
    gt = _matmul(n2, full["w_ffn_gate"], mode="nn", out_dtype=F32, name="ffn_gate")
    up = _matmul(n2, full["w_ffn_up"], mode="nn", out_dtype=F32, name="ffn_up")

    def act_fn(rv, pv):
        gt_v, up_v = rv
        return [gt_v * _sigmoid(gt_v) * up_v], []
    act = _rowwise(act_fn, [gt, up], [], [(gt.shape[1], BF16)], [], name="ffn_act")[0]
    h2 = _matmul(act, full["w_ffn_down"], mode="nn", out_dtype=F32, name="ffn_down", add=h1)

    def final_fn(rv, pv):
        (h_v, t_v), (g_v,) = rv, pv
        r = lax.rsqrt(_mean(h_v * h_v) + RMS_EPS)
        yn = h_v * r
        e = yn * g_v - t_v
        loss = 0.5 * jnp.sum(_mean(e * e))
        dout = e * (1.0 / D)
        dyg = dout * g_v
        dh = r * (dyg - yn * _mean(dyg * yn))
        return [dh, dh], [jnp.full((1, LANE), loss, F32), _colsum(dout * yn)]
    dh2, dh2_bf, loss_part, d_gf = _rowwise(final_fn, [h2, tgt], [gf], [(D, F32), (D, BF16)], [(1, LANE), (1, D)], name="final_loss")

    grads = {}

    def start_scatter(group, name, extra=()):
        blocks = []
        for n in group:
            g = grads[n]
            blocks.append((g.reshape(g.shape[0], N_DEV, -1).transpose(1, 0, 2) if n in col_sharded
                           else g.reshape(N_DEV, -1, g.shape[1]), False))
        (handle,), token = _exchange_start([blocks + list(extra)], name)
        return handle, token

    dact = _matmul(dh2_bf, full["w_ffn_down"], mode="nt", out_dtype=F32, name="d_act")
    grads["w_ffn_down"] = _matmul(act, dh2_bf, mode="tn", out_dtype=BF16, name="dw_ffn_down")

    def dact_fn(rv, pv):
        d_v, gt_v, up_v = rv
        s = _sigmoid(gt_v)
        return [d_v * up_v * (s * (1.0 + gt_v * (1.0 - s))), d_v * gt_v * s], []
    dgt, dup = _rowwise(dact_fn, [dact, gt, up], [], [(gt.shape[1], BF16)] * 2, [], name="d_ffn_act")
    dn2 = _matmul(dgt, full["w_ffn_gate"], mode="nt", out_dtype=F32, name="dn2_gate")
    dn2 = _matmul(dup, full["w_ffn_up"], mode="nt", out_dtype=F32, name="dn2_up", add=dn2)
    grads["w_ffn_gate"] = _matmul(n2, dgt, mode="tn", out_dtype=BF16, name="dw_ffn_gate")
    grads["w_ffn_up"] = _matmul(n2, dup, mode="tn", out_dtype=BF16, name="dw_ffn_up")
    scatter_groups = [["w_ffn_down", "w_ffn_gate", "w_ffn_up"], ["w_out", "w_proj_rwkv", "w_proj_sgu"],
                      ["w_in", "w_lora_up", "a_lora_up", "g_lora_up"]]
    scatter_ffn, token_ffn = start_scatter(scatter_groups[0], "scatter_start_ffn")
    dh1, dh1_bf, d_g2 = _rms_bwd(dn2, h1, dh2, norm_ffn_g, "rms_ffn_bwd", deps=[token_ffn])
    dmerged = _matmul(dh1_bf, full["w_out"], mode="nt", out_dtype=F32, name="d_merged")
    grads["w_out"] = _matmul(merged, dh1_bf, mode="tn", out_dtype=BF16, name="dw_out")

    def dmerge_fn(rv, pv):
        d_v, ga_v, gb_v, pa_v, pb_v = rv
        sa, sb = _sigmoid(ga_v), _sigmoid(gb_v)
        return [d_v * pa_v * sa * (1.0 - sa), d_v * pb_v * sb * (1.0 - sb), d_v * sa, d_v * sb], []
    dga, dgb, dpa, dpb = _rowwise(dmerge_fn, [dmerged, ga, gb, pa, pb], [], [(D, BF16)] * 4, [], name="d_merge")
    dya = _matmul(dpa, full["w_proj_rwkv"], mode="nt", out_dtype=F32, name="d_ya")
    dyb = _matmul(dpb, full["w_proj_sgu"], mode="nt", out_dtype=F32, name="d_yb")
    grads["w_proj_rwkv"] = _matmul(ya, dpa, mode="tn", out_dtype=BF16, name="dw_proj_a")
    grads["w_proj_sgu"] = _matmul(yb, dpb, mode="tn", out_dtype=BF16, name="dw_proj_b")
    scatter_mid, token_mid = start_scatter(scatter_groups[1], "scatter_start_mid")
    dz, d_lng, d_lnb, d_ws, d_bs = _sgu_bwd(z, dyb, sgu_ln_g, sgu_ln_b, ws, bexp, "sgu_bwd")

    dy_h, dr1, dk1, dv1, dg_h, d_lnxg, d_lnxb, d_rk = _head_post_bwd(dya, y_h, r_h, k2_h, v_h, g_h, hp, "head_post_bwd",
                                                                     deps=[token_mid])
    dr2, dlw_h, dk2b, dv2, daa, dbb = _wkv_bwd(*wkv_in, states, dy_h, "wkv_bwd")
    dps, d_mu, d_w0, d_a0, d_kk, d_ka, d_wlw, d_wla, d_wlg = _rwkv_pre_bwd(
        p, mu_p, rsmall, lora, [dr1, dr2, dk1, dk2b, dv1, dv2, dlw_h, daa, dbb, dg_h], lay, "rwkv_pre_bwd")
    dp = _shift_bwd(dps, mu_p, "shift_bwd")
    dproj = jnp.concatenate([dp, dz, dga, dgb], axis=1)
    dn1 = _matmul(dproj, W_in, mode="nt", out_dtype=F32, name="dn1")
    grads["w_in"] = _unpad_rwkv_cols(_matmul(n1, dproj, mode="tn", out_dtype=BF16, name="dw_in"), lay)
    dx, _, d_g1 = _rms_bwd(dn1, xs, dh1, norm_mix_g, "rms_mix_bwd")
    grads["w_lora_up"] = d_wlw[:w_lora_up.shape[1]].astype(BF16)
    grads["a_lora_up"] = d_wla[:a_lora_up.shape[1]].astype(BF16)
    grads["g_lora_up"] = d_wlg[:g_lora_up.shape[1]].astype(BF16)
    small_grads = dict(norm_mix_g=d_g1, shift_mu=_unpad_rwkv_cols(d_mu, lay), w0=d_w0, a0=d_a0, k_k=d_kk, k_a=d_ka, r_k=d_rk,
                       lnx_g=d_lnxg, lnx_b=d_lnxb, sgu_ln_g=d_lng, sgu_ln_b=d_lnb, sgu_w=d_ws, sgu_b=d_bs[:, :G].T,
                       norm_ffn_g=d_g2, norm_final_g=d_gf)

    scatter_in, token_in = start_scatter(scatter_groups[2], "scatter_start_in", extra=[(_pack([small_grads[n] for n in small]), True)])
    out = {}
    after = token_in
    small_parts = None
    for group, handle, name in zip(scatter_groups, (scatter_ffn, scatter_mid, scatter_in), ("ffn", "mid", "in")):
        parts = _exchange_wait(handle, after, "scatter_wait_" + name)
        for n, part in zip(group, parts):
            shp = weights[n].shape
            res = _adamw(weights[n][0], m_in[n][0], v_in[n][0], part, "adamw_" + n)
            out[n] = [t.reshape(shp) for t in res]
            after = res[0]
        small_parts = parts[-1]
    packed = [_pack([d[n] for n in small]) for d in (weights, m_in, v_in)]
    res = _adamw(*packed, small_parts, "adamw_small")
    unpacked = [_unpack(t, [weights[n].shape for n in small]) for t in res]
    for i, n in enumerate(small):
        out[n] = [u[i] for u in unpacked]

    loss = lax.psum(loss_part[0, 0], ("x", "y", "c"))
    return (loss, dx[None], *[out[n][0] for n in names], *[out[n][1] for n in names],
            *[out[n][2] for n in names], *[out[n][3] for n in names])
```

```python
import jax
import jax.numpy as jnp
from jax import lax
from jax.experimental import pallas as pl
from jax.experimental.pallas import tpu as pltpu

F32 = jnp.float32
BF16 = jnp.bfloat16

N_DEV = 8
LANE = 128
SUBLANE = 8
HEAD = 64
SGU_CHUNK = 128
SGU_GROUP = 128
WKV_CHUNK = 64
RMS_EPS = 1e-6
LN_EPS = 1e-5
LNX_EPS = 64e-5
ADAM_LR, ADAM_B1, ADAM_B2, ADAM_EPS, ADAM_WD, ADAM_STEP = 0.001, 0.9, 0.999, 1e-08, 0.01, 10
VMEM_LIMIT_BYTES = 48 * 1024 * 1024
_SQRT_HALF = 0.7071067811865476
_INV_SQRT_2PI = 0.3989422804014327


def _pick(n, cands):
    for c in cands:
        if n % c == 0:
            return c
    return n


def _ceil_to(n, m):
    return -(-n // m) * m


def _params():
    return pltpu.CompilerParams(vmem_limit_bytes=VMEM_LIMIT_BYTES)


def _tile(n, cap):
    best = 0
    for d in range(LANE, min(n, cap) + 1, LANE):
        if n % d == 0:
            best = d
    return best or n


def _matmul_tiles(M, N, K, a_bytes, b_bytes, o_bytes, has_add, forced):
    tm = forced.get("m") or _tile(M, 1024)
    tn = forced.get("n") or _tile(N, 1024)
    tk = forced.get("k") or _tile(K, 2048)

    def vmem(tm, tn, tk):
        acc = tm * tn * 4 if tk < K else 0
        return 2 * (tm * tk * a_bytes + tk * tn * b_bytes + tm * tn * (o_bytes + (4 if has_add else 0))) + acc

    while vmem(tm, tn, tk) > (VMEM_LIMIT_BYTES * 3) // 4:
        if "k" not in forced and tk > 512 and _tile(K, tk // 2) < tk:
            tk = _tile(K, tk // 2)
        elif "m" not in forced and _tile(M, tm // 2) < tm:
            tm = _tile(M, tm // 2)
        else:
            break
    return tm, tn, tk


def _matmul(a, b, *, mode, out_dtype, name, add=None, deps=(), out_blocks=0):
    def view(x):
        return (x.shape[1], x.shape[0] * x.shape[2], x.shape[2]) if x.ndim == 3 else (x.shape[0], x.shape[1], 0)

    (ar, ac, aw), (br, bc, bw) = view(a), view(b)
    a_col, b_col = {"nn": ("k", "n"), "nt": ("k", "k"), "tn": ("m", "n")}[mode]
    if mode == "nn":
        M, K, K2, N = ar, ac, br, bc
    elif mode == "nt":
        M, K, N, K2 = ar, ac, br, bc
    else:
        K, M, K2, N = ar, ac, br, bc
    assert K == K2, (a.shape, b.shape, mode)
    forced = {}
    for dim, w in ((a_col, aw), (b_col, bw), ("n", N // out_blocks if out_blocks else 0)):
        if w:
            assert forced.get(dim, w) == w
            forced[dim] = w
    has_add = add is not None
    tm, tn, tk = _matmul_tiles(M, N, K, a.dtype.itemsize, b.dtype.itemsize, jnp.dtype(out_dtype).itemsize, has_add, forced)
    nk = K // tk
    dn = {"nn": (((1,), (0,)), ((), ())), "nt": (((1,), (1,)), ((), ())), "tn": (((0,), (0,)), ((), ()))}[mode]
    pick = {"m": lambda i, j, k: i, "n": lambda i, j, k: j, "k": lambda i, j, k: k}
    size = {"m": tm, "n": tn, "k": tk}

    def spec(blocked, row_dim, col_dim):
        rf, cf = pick[row_dim], pick[col_dim]
        if blocked:
            return pl.BlockSpec((None, size[row_dim], size[col_dim]), lambda i, j, k: (cf(i, j, k), rf(i, j, k), 0))
        return pl.BlockSpec((size[row_dim], size[col_dim]), lambda i, j, k: (rf(i, j, k), cf(i, j, k)))

    a_spec = spec(aw, "k" if mode == "tn" else "m", a_col)
    b_spec = spec(bw, "n" if mode == "nt" else "k", b_col)
    o_spec = spec(out_blocks, "m", "n")
    n_in = 2 + has_add + len(deps)

    def body(*refs):
        a_ref, b_ref = refs[0], refs[1]
        add_ref = refs[2] if has_add else None
        o_ref = refs[n_in]
        part = lax.dot_general(a_ref[...].astype(BF16), b_ref[...].astype(BF16), dn, preferred_element_type=F32)
        if nk == 1:
            if has_add:
                part = part + add_ref[...]
            o_ref[...] = part.astype(out_dtype)
            return
        acc_ref = refs[-1]
        kk = pl.program_id(2)

        @pl.when(kk == 0)
        def _():
            acc_ref[...] = part + add_ref[...] if has_add else part

        @pl.when(kk > 0)
        def _():
            acc_ref[...] += part

        @pl.when(kk == nk - 1)
        def _():
            o_ref[...] = acc_ref[...].astype(out_dtype)

    ins = [a, b] + ([add] if has_add else []) + list(deps)
    in_specs = ([a_spec, b_spec] + ([o_spec] if has_add else [])
                + [pl.BlockSpec(d.shape, lambda i, j, k, nd=d.ndim: (0,) * nd) for d in deps])
    return pl.pallas_call(
        body, name=name, grid=(M // tm, N // tn, nk), in_specs=in_specs, out_specs=o_spec,
        out_shape=jax.ShapeDtypeStruct((out_blocks, M, tn) if out_blocks else (M, N), out_dtype),
        scratch_shapes=[pltpu.VMEM((tm, tn), F32)] if nk > 1 else [],
        compiler_params=_params())(*ins)


def _rowwise(fn, rows, pars, row_outs, acc_outs, *, name, tm=256, deps=()):
    rows = [r if isinstance(r, tuple) else (r, r.shape[1], 0) for r in rows]
    row_outs = [o if len(o) == 5 else (o[0], o[1], o[0], 0, None) for o in row_outs]
    aliased = [(k, o[4]) for k, o in enumerate(row_outs) if o[4] is not None]
    R = rows[0][0].shape[0]
    if max(w for _, w, _ in rows) > 4096:
        tm = tm // 2
    tm = min(tm, R)
    assert R % tm == 0
    nr, npar = len(rows), len(pars)
    nro = len(row_outs)
    n_in = nr + npar + len(deps) + len(aliased)

    def body(*refs):
        rv = [r[...] for r in refs[:nr]]
        pv = [p[...] for p in refs[nr:nr + npar]]
        outs = refs[n_in:]
        ro, ao = fn(rv, pv)
        first = pl.program_id(0) == 0
        for o_ref, val in zip(outs[:nro], ro):
            o_ref[...] = val.astype(o_ref.dtype)

        @pl.when(first)
        def _():
            for o_ref, val in zip(outs[nro:], ao):
                o_ref[...] = val

        @pl.when(jnp.logical_not(first))
        def _():
            for o_ref, val in zip(outs[nro:], ao):
                o_ref[...] += val

    in_specs = ([pl.BlockSpec((tm, w), lambda i, cb=cb: (i, cb)) for _, w, cb in rows]
                + [pl.BlockSpec(p.shape, lambda i, nd=p.ndim: (0,) * nd) for p in list(pars) + list(deps)]
                + [pl.BlockSpec(memory_space=pl.ANY)] * len(aliased))
    out_shape = ([jax.ShapeDtypeStruct((R, full), dt) for _, dt, full, _, _ in row_outs]
                 + [jax.ShapeDtypeStruct(s, F32) for s in acc_outs])
    out_specs = ([pl.BlockSpec((tm, f), lambda i, cb=cb: (i, cb)) for f, _, _, cb, _ in row_outs]
                 + [pl.BlockSpec(s, lambda i, nd=len(s): (0,) * nd) for s in acc_outs])
    res = pl.pallas_call(body, name=name, grid=(R // tm,), in_specs=in_specs, out_specs=out_specs, out_shape=out_shape,
                         input_output_aliases={n_in - len(aliased) + q: k for q, (k, _) in enumerate(aliased)},
                         compiler_params=_params())(*[r for r, _, _ in rows], *pars, *deps, *[buf for _, buf in aliased])
    return list(res)


def _bdot(a, b, mode="nn"):
    dn = {"nn": (((1,), (0,)), ((), ())), "nt": (((1,), (1,)), ((), ())), "tn": (((0,), (0,)), ((), ()))}[mode]
    return lax.dot_general(a.astype(BF16), b.astype(BF16), dn, preferred_element_type=F32)


def _sigmoid(x):
    return jax.nn.sigmoid(x)


def _softplus(x):
    return jnp.maximum(x, 0.0) + jnp.log1p(jnp.exp(-jnp.abs(x)))


def _gelu(z):
    return 0.5 * z * (1.0 + lax.erf(z * _SQRT_HALF))


def _gelu_grad(z):
    return 0.5 * (1.0 + lax.erf(z * _SQRT_HALF)) + z * jnp.exp(-0.5 * z * z) * _INV_SQRT_2PI


def _mean(x):
    return jnp.mean(x, axis=-1, keepdims=True)


def _colsum(x):
    return jnp.sum(x, axis=0, keepdims=True)


def _rms_fwd(x, g, name, deps=()):
    def fn(rv, pv):
        (xv,), (gv,) = rv, pv
        r = lax.rsqrt(_mean(xv * xv) + RMS_EPS)
        return [xv * r * gv], []
    return _rowwise(fn, [x], [g], [(x.shape[1], BF16)], [], name=name, deps=deps)[0]


def _rms_bwd(dn, x, dres, g, name, deps=()):
    def fn(rv, pv):
        (dnv, xv, drv), (gv,) = rv, pv
        r = lax.rsqrt(_mean(xv * xv) + RMS_EPS)
        yn = xv * r
        dyg = dnv * gv
        dx = drv + r * (dyg - yn * _mean(dyg * yn))
        return [dx, dx], [_colsum(dnv * yn)]
    D = x.shape[1]
    return _rowwise(fn, [dn, x, dres], [g], [(D, F32), (D, BF16)], [(1, D)], name=name, deps=deps)


def _rwkv_layout(RW, Lw, La, Lg, D):
    widths = [RW, RW, RW, Lw, La, Lg]
    pw = [_ceil_to(w, LANE) for w in widths]
    pw[5] += _ceil_to(sum(pw), 2 * D) - sum(pw)
    offs = [sum(pw[:i]) for i in range(6)]
    return widths, pw, offs, sum(pw)


def _to_proj_layout(a, lay, D):
    widths, pw, _, _ = lay
    pieces, src = [], 0
    for w, p in zip(widths, pw):
        pieces.append(a[:, src:src + w])
        if p > w:
            pieces.append(jnp.zeros((a.shape[0], p - w), a.dtype))
        src += w
    if a.shape[1] > src:
        pieces += [a[:, src + D:src + 3 * D], a[:, src:src + D]]
    return jnp.concatenate(pieces, axis=1)


def _from_proj_layout(a, lay, D):
    widths, _, offs, rcp = lay
    rest = [a[:, rcp + 2 * D:], a[:, rcp:rcp + 2 * D]] if a.shape[1] > rcp else []
    return jnp.concatenate([a[:, o:o + w] for o, w in zip(offs, widths)] + rest, axis=1)


def _pad_rows(a, rows):
    return a if a.shape[0] == rows else jnp.concatenate([a, jnp.zeros((rows - a.shape[0], a.shape[1]), a.dtype)], axis=0)


def _token_shift(p, halo, mu, i):
    tm = p.shape[0]
    hid = lax.broadcasted_iota(jnp.int32, (SUBLANE, 1), 0)
    before = jnp.sum(jnp.where(hid == SUBLANE - 1, halo, 0.0), axis=0, keepdims=True)
    before = jnp.where(i == 0, 0.0, before)
    rid = lax.broadcasted_iota(jnp.int32, (tm, 1), 0)
    prev = jnp.where(rid == 0, before, pltpu.roll(p, 1, 0))
    d = prev - p
    return p + d * mu, d


def _rwkv_math(ps, w0, a0, k_k, k_a, wlw, wla, wlg, lay):
    _, pw, offs, _ = lay
    r, k, v, xw, xa, xg = (ps[:, offs[j]:offs[j] + pw[j]] for j in range(6))
    tw = jnp.tanh(xw)
    ww = w0 + _bdot(tw, wlw)
    lw = -jnp.exp(-_softplus(-ww) - 0.5)
    a = _sigmoid(a0 + _bdot(xa, wla))
    sg = _sigmoid(xg)
    g = _bdot(sg, wlg)
    return dict(r=r, k=k, v=v, xa=xa, tw=tw, ww=ww, lw=lw, a=a, sg=sg, g=g, kkp=k * k_k, k2=k * (1.0 + (a - 1.0) * k_a))


def _halo_specs(T, tm, width, after):
    hb = tm // SUBLANE
    last = T // SUBLANE - 1
    if after:
        return pl.BlockSpec((SUBLANE, width), lambda i: (jnp.minimum((i + 1) * hb, last), 0))
    return pl.BlockSpec((SUBLANE, width), lambda i: (jnp.maximum(i * hb - 1, 0), 0))


def _rowsum(x):
    return jnp.sum(x, axis=-1, keepdims=True)


def _kk_math(kkp):
    nrm = jnp.sqrt(_rowsum(kkp * kkp))
    inv = 1.0 / jnp.maximum(nrm, 1e-12)
    return nrm, inv, kkp * inv


def _rwkv_pre(p, mu, small, lora, lay, name):
    T, rcp = p.shape[0], lay[3]
    H = lay[0][0] // HEAD
    tm = min(128, T)

    def body(p_ref, ph_ref, mu_ref, w0_ref, a0_ref, kk_ref, ka_ref, wlw_ref, wla_ref, wlg_ref, r_o, lw_o, k2_o, v_o, aa_o, bb_o, g_o):
        ps, _ = _token_shift(p_ref[...], ph_ref[...], mu_ref[...], pl.program_id(0))
        q = _rwkv_math(ps, w0_ref[...], a0_ref[...], kk_ref[...], ka_ref[...], wlw_ref[...], wla_ref[...], wlg_ref[...], lay)
        for h in range(H):
            sl = slice(h * HEAD, (h + 1) * HEAD)
            for o_ref, key in ((r_o, "r"), (lw_o, "lw"), (k2_o, "k2"), (v_o, "v"), (g_o, "g")):
                o_ref[h] = q[key][:, sl]
            _, _, kk = _kk_math(q["kkp"][:, sl])
            aa_o[h] = -kk
            bb_o[h] = kk * q["a"][:, sl]

    whole = lambda arr: pl.BlockSpec(arr.shape, lambda i: (0, 0))
    return pl.pallas_call(
        body, name=name, grid=(T // tm,),
        in_specs=([pl.BlockSpec((tm, rcp), lambda i: (i, 0)), _halo_specs(T, tm, rcp, False), whole(mu)]
                  + [whole(s) for s in small] + [whole(w) for w in lora]),
        out_specs=[pl.BlockSpec((H, tm, HEAD), lambda i: (0, i, 0))] * 7, out_shape=[jax.ShapeDtypeStruct((H, T, HEAD), F32)] * 7,
        compiler_params=_params())(p, p, mu, *small, *lora)


def _rwkv_pre_bwd(p, mu, small, lora, hgrads, lay, name):
    T, rcp = p.shape[0], lay[3]
    widths, pw, offs, _ = lay
    RW = widths[0]
    H = RW // HEAD
    tm = min(128, T)

    def body(p_ref, ph_ref, mu_ref, w0_ref, a0_ref, kk_ref, ka_ref, wlw_ref, wla_ref, wlg_ref,
             dr1, dr2, dk1, dk2b, dv1, dv2, dlw_h, daa, dbb, dg_h,
             dps_ref, dmu_ref, dw0_ref, da0_ref, dkk_ref, dka_ref, dwlw_ref, dwla_ref, dwlg_ref,
             s_dr, s_dk2, s_dv, s_dlw, s_dkkp, s_da, s_dg):
        i = pl.program_id(0)
        ps, dprev = _token_shift(p_ref[...], ph_ref[...], mu_ref[...], i)
        k_k, k_a = kk_ref[...], ka_ref[...]
        q = _rwkv_math(ps, w0_ref[...], a0_ref[...], k_k, k_a, wlw_ref[...], wla_ref[...], wlg_ref[...], lay)
        k, a, lw, ww, tw, sg = q["k"], q["a"], q["lw"], q["ww"], q["tw"], q["sg"]
        for h in range(H):
            sl = slice(h * HEAD, (h + 1) * HEAD)
            s_dr[:, sl] = dr1[h] + dr2[h]
            s_dk2[:, sl] = dk1[h] + dk2b[h]
            s_dv[:, sl] = dv1[h] + dv2[h]
            s_dlw[:, sl] = dlw_h[h]
            s_dg[:, sl] = dg_h[h]
            nrm, inv, kk = _kk_math(q["kkp"][:, sl])
            dbb_h = dbb[h]
            dkk = dbb_h * a[:, sl] - daa[h]
            s_dkkp[:, sl] = jnp.where(nrm > 1e-12, inv * (dkk - kk * _rowsum(dkk * kk)), dkk * inv)
            s_da[:, sl] = dbb_h * kk
        dk2, dkkp, dg = s_dk2[...], s_dkkp[...], s_dg[...]
        dk = dk2 * (1.0 + (a - 1.0) * k_a) + dkkp * k_k
        da = s_da[...] + dk2 * k * k_a
        dpa = da * a * (1.0 - a)
        dww = s_dlw[...] * lw * _sigmoid(-ww)
        dxa = _bdot(dpa, wla_ref[...], "nt")
        dxw = _bdot(dww, wlw_ref[...], "nt") * (1.0 - tw * tw)
        dxg = _bdot(dg, wlg_ref[...], "nt") * sg * (1.0 - sg)
        segs = (s_dr[...], dk, s_dv[...], dxw, dxa, dxg)
        sums = [dmu_ref, dw0_ref, da0_ref, dkk_ref, dka_ref, dwlw_ref, dwla_ref, dwlg_ref]

        @pl.when(i == 0)
        def _():
            for s in sums:
                s[...] = jnp.zeros_like(s)

        for j, seg in enumerate(segs):
            sl = slice(offs[j], offs[j] + pw[j])
            dps_ref[:, sl] = seg
            dmu_ref[:, sl] += _colsum(seg * dprev[:, sl])
        dw0_ref[...] += _colsum(dww)
        da0_ref[...] += _colsum(dpa)
        dkk_ref[...] += _colsum(dkkp * k)
        dka_ref[...] += _colsum(dk2 * k * (a - 1.0))
        dwlw_ref[...] += _bdot(tw, dww, "tn")
        dwla_ref[...] += _bdot(q["xa"], dpa, "tn")
        dwlg_ref[...] += _bdot(sg, dg, "tn")

    whole = lambda arr: pl.BlockSpec(arr.shape, lambda i: (0, 0))
    row = lambda w: pl.BlockSpec((tm, w), lambda i: (i, 0))
    acc_shapes = [(1, rcp), (1, RW), (1, RW), (1, RW), (1, RW)] + [w.shape for w in lora]
    return pl.pallas_call(
        body, name=name, grid=(T // tm,),
        in_specs=([row(rcp), _halo_specs(T, tm, rcp, False), whole(mu)] + [whole(s) for s in small] + [whole(w) for w in lora]
                  + [pl.BlockSpec((H, tm, HEAD), lambda i: (0, i, 0))] * 10),
        out_specs=[row(rcp)] + [pl.BlockSpec(s, lambda i: (0, 0)) for s in acc_shapes],
        out_shape=[jax.ShapeDtypeStruct((T, rcp), F32)] + [jax.ShapeDtypeStruct(s, F32) for s in acc_shapes],
        scratch_shapes=[pltpu.VMEM((tm, RW), F32)] * 7, compiler_params=_params())(p, p, mu, *small, *lora, *hgrads)


def _shift_bwd(dps, mu, dproj, name):
    T, rcp = dps.shape
    tm = min(256, T)
    nt = T // tm

    def body(d_ref, dh_ref, mu_ref, buf_ref, o_ref):
        i = pl.program_id(0)
        d = d_ref[...]
        hid = lax.broadcasted_iota(jnp.int32, (SUBLANE, 1), 0)
        after = jnp.sum(jnp.where(hid == 0, dh_ref[...], 0.0), axis=0, keepdims=True)
        after = jnp.where(i == nt - 1, 0.0, after)
        rid = lax.broadcasted_iota(jnp.int32, (tm, 1), 0)
        nxt = jnp.where(rid == tm - 1, after, pltpu.roll(d, tm - 1, 0))
        mu_v = mu_ref[...]
        o_ref[...] = (d * (1.0 - mu_v) + nxt * mu_v).astype(BF16)

    row = pl.BlockSpec((tm, rcp), lambda i: (i, 0))
    return pl.pallas_call(
        body, name=name, grid=(nt,),
        in_specs=[row, _halo_specs(T, tm, rcp, True), pl.BlockSpec(mu.shape, lambda i: (0, 0)), pl.BlockSpec(memory_space=pl.ANY)],
        out_specs=row, out_shape=jax.ShapeDtypeStruct(dproj.shape, BF16), input_output_aliases={3: 0},
        compiler_params=_params())(dps, dps, mu, dproj)


def _head_post_math(y, r, k2, v, lg, lb, rk):
    yc = y - _mean(y)
    rstd = lax.rsqrt(_mean(yc * yc) + LNX_EPS)
    yn = yc * rstd
    s = _rowsum(r * k2 * rk)
    return yn, rstd, yn * lg + lb + s * v, s


def _head_post(y, r, k2, v, g, hp, name):
    H, T, _ = y.shape
    tm = min(128, T)

    def body(y_ref, r_ref, k_ref, v_ref, g_ref, lg_ref, lb_ref, rk_ref, o_ref):
        _, _, t, _ = _head_post_math(y_ref[...], r_ref[...], k_ref[...], v_ref[...], lg_ref[...], lb_ref[...], rk_ref[...])
        out = (t * g_ref[...]).astype(BF16)
        for h in range(H):
            o_ref[:, h * HEAD:(h + 1) * HEAD] = out[h]

    blk = pl.BlockSpec((H, tm, HEAD), lambda i: (0, i, 0))
    par = pl.BlockSpec((H, 1, HEAD), lambda i: (0, 0, 0))
    return pl.pallas_call(
        body, name=name, grid=(T // tm,), in_specs=[blk] * 5 + [par] * 3, out_specs=pl.BlockSpec((tm, H * HEAD), lambda i: (i, 0)),
        out_shape=jax.ShapeDtypeStruct((T, H * HEAD), BF16), compiler_params=_params())(y, r, k2, v, g, *hp)


def _head_post_bwd(dya, y, r, k2, v, g, hp, name, deps=()):
    H, T, _ = y.shape
    tm = min(128, T)
    hsum = lambda t: jnp.sum(t, axis=1, keepdims=True)

    def body(d_ref, y_ref, r_ref, k_ref, v_ref, g_ref, lg_ref, lb_ref, rk_ref, *rest):
        outs, d_s = rest[len(deps):len(deps) + 8], rest[-1]
        for h in range(H):
            d_s[h] = d_ref[:, h * HEAD:(h + 1) * HEAD]
        d_v, r_v, k_v, v_v, lg, rk = d_s[...], r_ref[...], k_ref[...], v_ref[...], lg_ref[...], rk_ref[...]
        yn, rstd, t, s = _head_post_math(y_ref[...], r_v, k_v, v_v, lg, lb_ref[...], rk)
        dyo = d_v * g_ref[...]
        dyn = dyo * lg
        ds = _rowsum(dyo * v_v)
        vals = (rstd * (dyn - _mean(dyn) - yn * _mean(dyn * yn)), ds * k_v * rk, ds * r_v * rk, dyo * s, d_v * t)
        for o_ref, val in zip(outs[:5], vals):
            o_ref[...] = val
        sums = (hsum(dyo * yn), hsum(dyo), hsum(ds * r_v * k_v))
        first = pl.program_id(0) == 0

        @pl.when(first)
        def _():
            for o_ref, val in zip(outs[5:], sums):
                o_ref[...] = val

        @pl.when(jnp.logical_not(first))
        def _():
            for o_ref, val in zip(outs[5:], sums):
                o_ref[...] += val

    blk = pl.BlockSpec((H, tm, HEAD), lambda i: (0, i, 0))
    par = pl.BlockSpec((H, 1, HEAD), lambda i: (0, 0, 0))
    return pl.pallas_call(
        body, name=name, grid=(T // tm,),
        in_specs=([pl.BlockSpec((tm, H * HEAD), lambda i: (i, 0))] + [blk] * 5 + [par] * 3
                  + [pl.BlockSpec(d.shape, lambda i, nd=d.ndim: (0,) * nd) for d in deps]),
        out_specs=[blk] * 5 + [par] * 3,
        out_shape=[jax.ShapeDtypeStruct((H, T, HEAD), F32)] * 5 + [jax.ShapeDtypeStruct((H, 1, HEAD), F32)] * 3,
        scratch_shapes=[pltpu.VMEM((H, tm, HEAD), F32)], compiler_params=_params())(dya, y, r, k2, v, g, *hp, *deps)


def _bmm(x, y, mode):
    dn = {"nn": (((2,), (1,)), ((0,), (0,))), "nt": (((2,), (2,)), ((0,), (0,))), "tn": (((1,), (1,)), ((0,), (0,)))}[mode]
    (xh, xl), (yh, yl) = _split(x), _split(y)
    dot = lambda p, q: lax.dot_general(p, q, dn, preferred_element_type=F32)
    out = dot(xh, yh)
    if yl is not None:
        out = out + dot(xh, yl)
    if xl is not None:
        out = out + dot(xl, yh)
    return out


def _split(x):
    if isinstance(x, tuple):
        return x
    hi = x.astype(BF16)
    return hi, (x - hi.astype(F32)).astype(BF16)


def _exact(x):
    return x.astype(BF16), None


def _wkv_chunk(r, lw, k, v, a, b):
    hb, C, _ = r.shape
    ti = lax.broadcasted_iota(jnp.int32, (C, C), 0)
    si = lax.broadcasted_iota(jnp.int32, (C, C), 1)
    linc, lstr, eye = (ti >= si).astype(F32), (ti > si).astype(F32), (ti == si).astype(F32)
    lincb = _exact(jnp.broadcast_to(linc, (hb, C, C)))
    lstrb = _exact(jnp.broadcast_to(lstr, (hb, C, C)))
    ones = _exact(jnp.ones_like(v))
    lws = _split(lw)
    ci = _bmm(lincb, lws, "nn")
    cC = jnp.sum(lw, axis=1, keepdims=True)
    gi, ge, gn, gr = jnp.exp(ci), jnp.exp(ci - lw), jnp.exp(-ci), jnp.exp(cC - ci)
    q = dict(At=a * ge, Rt=r * gi, Bt=b * gn, Kt=k * gn, Bh=b * gr, Kh=k * gr)
    s = {key: _split(val) for key, val in q.items()}
    s["v"] = _split(v)
    q["A_ab"] = _bmm(s["At"], s["Bt"], "nt") * lstr
    for key, lhs, rhs, mask in (("A_ak", "At", "Kt", lstr), ("A_rb", "Rt", "Bt", linc), ("A_rk", "Rt", "Kt", linc)):
        q[key] = _bmm(s[lhs], s[rhs], "nt") * mask
        s[key] = _split(q[key])
    Tm = eye + q["A_ab"]
    Pw = _split(q["A_ab"])
    n = 1
    while 2 * n < C:
        Pw = _split(_bmm(Pw, Pw, "nn"))
        Tm = Tm + _bmm(Tm, Pw, "nn")
        n *= 2
    s["Tm"] = _split(Tm)
    gC = jnp.exp(_bmm(lws, ones, "tn"))
    q.update(gi=gi, ge=ge, gn=gn, gr=gr, linc=linc, lstr=lstr, lincb=lincb, lstrb=lstrb, gC=gC, ones=ones, s=s)
    return q


def _wkv_fwd(r, lw, k, v, a, b, name):
    H, T, N = r.shape
    C = min(WKV_CHUNK, T)
    nc = T // C
    hb = _pick(H, (8, 4, 2))

    def body(r_ref, lw_ref, k_ref, v_ref, a_ref, b_ref, y_ref, st_ref, h_ref):
        @pl.when(pl.program_id(1) == 0)
        def _():
            h_ref[...] = jnp.zeros_like(h_ref)

        H0 = h_ref[...]
        st_ref[0] = H0
        q = _wkv_chunk(r_ref[...], lw_ref[...], k_ref[...], v_ref[...], a_ref[...], b_ref[...])
        s = q["s"]
        H0s = _split(H0)
        U = _split(_bmm(s["Tm"], _bmm(s["At"], H0s, "nn") + _bmm(s["A_ak"], s["v"], "nn"), "nn"))
        y_ref[...] = _bmm(s["Rt"], H0s, "nn") + _bmm(s["A_rb"], U, "nn") + _bmm(s["A_rk"], s["v"], "nn")
        h_ref[...] = q["gC"] * H0 + _bmm(s["Bh"], U, "tn") + _bmm(s["Kh"], s["v"], "tn")

    blk = pl.BlockSpec((hb, C, N), lambda h, c: (h, c, 0))
    return pl.pallas_call(
        body, name=name, grid=(H // hb, nc), in_specs=[blk] * 6,
        out_specs=[blk, pl.BlockSpec((1, hb, N, N), lambda h, c: (c, h, 0, 0))],
        out_shape=[jax.ShapeDtypeStruct((H, T, N), F32), jax.ShapeDtypeStruct((nc, H, N, N), F32)],
        scratch_shapes=[pltpu.VMEM((hb, N, N), F32)], compiler_params=_params())(r, lw, k, v, a, b)


def _wkv_bwd(r, lw, k, v, a, b, states, dy, name):
    H, T, N = r.shape
    C = min(WKV_CHUNK, T)
    nc = T // C
    hb = _pick(H, (8, 4, 2))

    def body(r_ref, lw_ref, k_ref, v_ref, a_ref, b_ref, st_ref, dy_ref, dr_ref, dlw_ref, dk_ref, dv_ref, da_ref, db_ref, dh_ref):
        @pl.when(pl.program_id(1) == 0)
        def _():
            dh_ref[...] = jnp.zeros_like(dh_ref)

        dHC = dh_ref[...]
        H0 = st_ref[0]
        q = _wkv_chunk(r_ref[...], lw_ref[...], k_ref[...], v_ref[...], a_ref[...], b_ref[...])
        s, gC = q["s"], q["gC"]
        H0s, dHs, dY = _split(H0), _split(dHC), _split(dy_ref[...])
        U = _split(_bmm(s["Tm"], _bmm(s["At"], H0s, "nn") + _bmm(s["A_ak"], s["v"], "nn"), "nn"))
        dU = _bmm(s["A_rb"], dY, "tn") + _bmm(s["Bh"], dHs, "nn")
        dP = _split(_bmm(s["Tm"], dU, "tn"))
        dv_ref[...] = _bmm(s["A_rk"], dY, "tn") + _bmm(s["Kh"], dHs, "nn") + _bmm(s["A_ak"], dP, "tn")
        dh_ref[...] = _bmm(s["Rt"], dY, "tn") + gC * dHC + _bmm(s["At"], dP, "tn")
        dA_rb = _split(_bmm(dY, U, "nt") * q["linc"])
        dA_rk = _split(_bmm(dY, s["v"], "nt") * q["linc"])
        dA_ab = _split(_bmm(dP, U, "nt") * q["lstr"])
        dA_ak = _split(_bmm(dP, s["v"], "nt") * q["lstr"])
        dRt = _bmm(dY, H0s, "nt") + _bmm(dA_rb, s["Bt"], "nn") + _bmm(dA_rk, s["Kt"], "nn")
        dAt = _bmm(dP, H0s, "nt") + _bmm(dA_ab, s["Bt"], "nn") + _bmm(dA_ak, s["Kt"], "nn")
        dBt = _bmm(dA_ab, s["At"], "tn") + _bmm(dA_rb, s["Rt"], "tn")
        dKt = _bmm(dA_ak, s["At"], "tn") + _bmm(dA_rk, s["Rt"], "tn")
        dBh = _bmm(U, dHs, "nt")
        dKh = _bmm(s["v"], dHs, "nt")
        dr_ref[...] = dRt * q["gi"]
        da_ref[...] = dAt * q["ge"]
        db_ref[...] = dBt * q["gn"] + dBh * q["gr"]
        dk_ref[...] = dKt * q["gn"] + dKh * q["gr"]
        tail = dBh * q["Bh"] + dKh * q["Kh"]
        dci = dRt * q["Rt"] - dBt * q["Bt"] - dKt * q["Kt"] - tail
        dcC = jnp.sum(tail, axis=1, keepdims=True) + _bmm(q["ones"], H0 * dHC * gC, "nt")
        dlw_ref[...] = _bmm(q["lincb"], dci, "tn") + _bmm(q["lstrb"], dAt * q["At"], "tn") + dcC

    blk = pl.BlockSpec((hb, C, N), lambda h, c: (h, nc - 1 - c, 0))
    st = pl.BlockSpec((1, hb, N, N), lambda h, c: (nc - 1 - c, h, 0, 0))
    return pl.pallas_call(
        body, name=name, grid=(H // hb, nc), in_specs=[blk] * 6 + [st, blk], out_specs=[blk] * 6,
        out_shape=[jax.ShapeDtypeStruct((H, T, N), F32)] * 6,
        scratch_shapes=[pltpu.VMEM((hb, N, N), F32)], compiler_params=_params())(r, lw, k, v, a, b, states, dy)


def _sgu_ln(z, SW, lng, lnb):
    ge = _gelu(z)
    u, vv = ge[:, :SW], ge[:, SW:]
    xc = vv - _mean(vv)
    rstd = lax.rsqrt(_mean(xc * xc) + LN_EPS)
    vn = xc * rstd
    return u, vn, rstd, vn * lng + lnb


def _causal(ws_ref, g):
    ti = lax.broadcasted_iota(jnp.int32, (SGU_CHUNK, SGU_CHUNK), 0)
    si = lax.broadcasted_iota(jnp.int32, (SGU_CHUNK, SGU_CHUNK), 1)
    return ti >= si, jnp.where(ti >= si, ws_ref[g], 0.0).astype(BF16)


def _sgu_fwd(proj, zblock, lng, lnb, ws, bexp, name):
    T, SW = proj.shape[0], lng.shape[1]
    G = ws.shape[0]
    tr = min(256, T)
    nch = tr // SGU_CHUNK

    def body(z_ref, lng_ref, lnb_ref, ws_ref, be_ref, o_ref):
        u, _, _, vl = _sgu_ln(z_ref[...], SW, lng_ref[...], lnb_ref[...])
        for g in range(G):
            cs = slice(g * SGU_GROUP, (g + 1) * SGU_GROUP)
            _, wc = _causal(ws_ref, g)
            for n in range(nch):
                rs = slice(n * SGU_CHUNK, (n + 1) * SGU_CHUNK)
                m = jnp.dot(wc, vl[rs, cs].astype(BF16), preferred_element_type=F32) + be_ref[:, cs]
                o_ref[rs, cs] = (u[rs, cs] * m).astype(BF16)

    whole = lambda arr: pl.BlockSpec(arr.shape, lambda i, nd=arr.ndim: (0,) * nd)
    return pl.pallas_call(
        body, name=name, grid=(T // tr,),
        in_specs=[pl.BlockSpec((tr, 2 * SW), lambda i: (i, zblock)), whole(lng), whole(lnb), whole(ws), whole(bexp)],
        out_specs=pl.BlockSpec((tr, SW), lambda i: (i, 0)), out_shape=jax.ShapeDtypeStruct((T, SW), BF16),
        compiler_params=_params())(proj, lng, lnb, ws, bexp)


def _sgu_bwd(proj, zblock, dyb, lng, lnb, ws, bexp, dproj, name):
    T, SW = proj.shape[0], lng.shape[1]
    G = ws.shape[0]
    tr = min(256, T)
    nch = tr // SGU_CHUNK
    nt = T // tr

    def body(z_ref, dy_ref, lng_ref, lnb_ref, ws_ref, be_ref, buf_ref, dz_ref, dlg_ref, dlb_ref, dws_ref, db_ref, du_s, dvl_s, dbacc_s):
        i = pl.program_id(0)
        zv = z_ref[...]
        lng_v = lng_ref[...]
        u, vn, rstd, vl = _sgu_ln(zv, SW, lng_v, lnb_ref[...])

        @pl.when(i == 0)
        def _():
            for s in (dlg_ref, dlb_ref, dws_ref, dbacc_s):
                s[...] = jnp.zeros_like(s)

        for g in range(G):
            cs = slice(g * SGU_GROUP, (g + 1) * SGU_GROUP)
            tri, wc = _causal(ws_ref, g)
            for n in range(nch):
                rs = slice(n * SGU_CHUNK, (n + 1) * SGU_CHUNK)
                blk = vl[rs, cs].astype(BF16)
                m = jnp.dot(wc, blk, preferred_element_type=F32) + be_ref[:, cs]
                dyv = dy_ref[rs, cs]
                du_s[rs, cs] = dyv * m
                dm = dyv * u[rs, cs]
                dvl_s[rs, cs] = _bdot(wc, dm, "tn")
                dws_ref[g] += jnp.where(tri, _bdot(dm, blk, "nt"), 0.0)
                dbacc_s[:, cs] += dm

        dvl = dvl_s[...]
        dlg_ref[...] += _colsum(dvl * vn)
        dlb_ref[...] += _colsum(dvl)
        dvn = dvl * lng_v
        dvv = rstd * (dvn - _mean(dvn) - vn * _mean(dvn * vn))
        gp = _gelu_grad(zv)
        dz_ref[:, :SW] = (du_s[...] * gp[:, :SW]).astype(BF16)
        dz_ref[:, SW:] = (dvv * gp[:, SW:]).astype(BF16)

        @pl.when(i == nt - 1)
        def _():
            lane = lax.broadcasted_iota(jnp.int32, (SGU_CHUNK, LANE), 1)
            out = jnp.zeros((SGU_CHUNK, LANE), F32)
            for g in range(G):
                col = jnp.sum(dbacc_s[:, g * SGU_GROUP:(g + 1) * SGU_GROUP], axis=1, keepdims=True)
                out = jnp.where(lane == g, col, out)
            db_ref[...] = out

    whole = lambda arr: pl.BlockSpec(arr.shape, lambda i, nd=arr.ndim: (0,) * nd)
    acc_shapes = [(1, SW), (1, SW), ws.shape, (SGU_CHUNK, LANE)]
    return pl.pallas_call(
        body, name=name, grid=(nt,),
        in_specs=[pl.BlockSpec((tr, 2 * SW), lambda i: (i, zblock)), pl.BlockSpec((tr, SW), lambda i: (i, 0)),
                  whole(lng), whole(lnb), whole(ws), whole(bexp), pl.BlockSpec(memory_space=pl.ANY)],
        out_specs=([pl.BlockSpec((tr, 2 * SW), lambda i: (i, zblock))]
                   + [pl.BlockSpec(s, lambda i, nd=len(s): (0,) * nd) for s in acc_shapes]),
        out_shape=[jax.ShapeDtypeStruct(dproj.shape, BF16)] + [jax.ShapeDtypeStruct(s, F32) for s in acc_shapes],
        scratch_shapes=[pltpu.VMEM((tr, SW), F32), pltpu.VMEM((tr, SW), F32), pltpu.VMEM((SGU_CHUNK, SW), F32)],
        input_output_aliases={6: 0}, compiler_params=_params())(proj, dyb, lng, lnb, ws, bexp, dproj)


_HBM = pl.BlockSpec(memory_space=pltpu.HBM)
_SEM = pl.BlockSpec(memory_space=pltpu.SEMAPHORE)
_DATAFLOW = pltpu.SideEffectType.DATAFLOW_SIDE_EFFECTING


def _mesh_place():
    x, y, c = lax.axis_index("x"), lax.axis_index("y"), lax.axis_index("c")
    return x, y, c, 4 * x + 2 * y + c


def _peer(x, y, c, rel):
    px = 1 - x if rel & 4 else x
    py = 1 - y if rel & 2 else y
    pc = 1 - c if rel & 1 else c
    return (px, py, pc), 4 * px + 2 * py + pc


def _exchange_start(groups, name):
    flat = [t for g in groups for t in g]
    sizes = [len(g) for g in groups]
    n, ng = len(flat), len(groups)
    srcs = [pltpu.with_memory_space_constraint(a, pltpu.HBM) for a, _ in flat]
    lands = [pltpu.with_memory_space_constraint(lax.empty(((N_DEV,) + a.shape) if isg else a.shape, a.dtype), pltpu.HBM)
             for a, isg in flat]

    def body(*refs):
        ins, lnd, sems, token = refs[:n], refs[n:2 * n], refs[2 * n:2 * n + 3 * ng], refs[-1]
        x, y, c, me = _mesh_place()
        j0 = 0
        for gi, sz in enumerate(sizes):
            for rel in range(1, N_DEV):
                dev, slot = _peer(x, y, c, rel)
                for jj in range(sz):
                    j = j0 + jj
                    pltpu.make_async_remote_copy(
                        src_ref=ins[j] if flat[j][1] else ins[j].at[slot], dst_ref=lnd[j].at[me],
                        send_sem=sems[3 * gi].at[jj * (N_DEV - 1) + rel - 1], recv_sem=sems[3 * gi + 1].at[jj * (N_DEV - 1) + rel - 1],
                        device_id=dev, device_id_type=pl.DeviceIdType.MESH).start()
            for jj in range(sz):
                j = j0 + jj
                pltpu.make_async_copy(ins[j] if flat[j][1] else ins[j].at[me], lnd[j].at[me], sems[3 * gi + 2].at[jj]).start()
            j0 += sz
        token[...] = jnp.zeros_like(token)

    sem_shapes = [pltpu.SemaphoreType.DMA((k,)) for sz in sizes for k in (sz * (N_DEV - 1), sz * (N_DEV - 1), sz)]
    res = pl.pallas_call(
        body, name=name,
        out_shape=(*sem_shapes, *[pltpu.HBM(a.shape, a.dtype) for a in srcs], *[pltpu.HBM(a.shape, a.dtype) for a in lands],
                   jax.ShapeDtypeStruct((SUBLANE, LANE), F32)),
        in_specs=[_HBM] * (2 * n), out_specs=(*[_SEM] * (3 * ng), *[_HBM] * (2 * n), pl.BlockSpec(memory_space=pltpu.VMEM)),
        input_output_aliases={i: 3 * ng + i for i in range(2 * n)},
        compiler_params=pltpu.CompilerParams(has_side_effects=_DATAFLOW))(*srcs, *lands)
    sems, thru, token = res[:3 * ng], res[3 * ng:3 * ng + 2 * n], res[-1]
    handle, j0 = [], 0
    for gi, sz in enumerate(sizes):
        handle.append(dict(kinds=[k for _, k in groups[gi]], srcs=list(thru[j0:j0 + sz]), lands=list(thru[n + j0:n + j0 + sz]),
                           sems=list(sems[3 * gi:3 * gi + 3])))
        j0 += sz
    return handle, token


def _exchange_wait(group, after, name):
    kinds, sz = group["kinds"], len(group["kinds"])

    def body(*refs):
        ins, lnd, (ssem, rsem, lsem) = refs[:sz], refs[sz:2 * sz], refs[2 * sz:2 * sz + 3]
        x, y, c, me = _mesh_place()
        for rel in range(1, N_DEV):
            dev, slot = _peer(x, y, c, rel)
            for jj in range(sz):
                cp = pltpu.make_async_remote_copy(
                    src_ref=ins[jj] if kinds[jj] else ins[jj].at[slot], dst_ref=lnd[jj].at[slot],
                    send_sem=ssem.at[jj * (N_DEV - 1) + rel - 1], recv_sem=rsem.at[jj * (N_DEV - 1) + rel - 1],
                    device_id=dev, device_id_type=pl.DeviceIdType.MESH)
                cp.wait_send()
                cp.wait_recv()
        for jj in range(sz):
            pltpu.make_async_copy(ins[jj] if kinds[jj] else ins[jj].at[me], lnd[jj].at[me], lsem.at[jj]).wait()

    arrays = group["srcs"] + group["lands"]
    res = pl.pallas_call(
        body, name=name, out_shape=[pltpu.HBM(a.shape, a.dtype) for a in arrays],
        in_specs=[_HBM] * (2 * sz) + [_SEM] * 3 + [pl.BlockSpec(memory_space=pl.ANY)], out_specs=[_HBM] * (2 * sz),
        input_output_aliases={i: i for i in range(2 * sz)},
        compiler_params=pltpu.CompilerParams(has_side_effects=_DATAFLOW))(*arrays, *group["sems"], after)
    return list(res[sz:])


def _adamw(w, m, v, gparts, name):
    R, C = w.shape
    tm = _pick(R, (256, 128, 64, 32, 16, 8))

    def body(w_ref, m_ref, v_ref, g_ref, go, do, mo, vo):
        g = g_ref[0].astype(F32)
        for j in range(1, N_DEV):
            g = g + g_ref[j].astype(F32)
        mn = ADAM_B1 * m_ref[...] + (1.0 - ADAM_B1) * g
        vn = ADAM_B2 * v_ref[...] + (1.0 - ADAM_B2) * (g * g)
        m_hat = mn / (1.0 - ADAM_B1 ** ADAM_STEP)
        v_hat = vn / (1.0 - ADAM_B2 ** ADAM_STEP)
        go[...] = g
        do[...] = -ADAM_LR * (m_hat / (jnp.sqrt(v_hat) + ADAM_EPS) + ADAM_WD * w_ref[...])
        mo[...] = mn
        vo[...] = vn

    row = pl.BlockSpec((tm, C), lambda i: (i, 0))
    return pl.pallas_call(
        body, name=name, grid=(R // tm,), in_specs=[row, row, row, pl.BlockSpec((N_DEV, tm, C), lambda i: (0, i, 0))],
        out_specs=[row] * 4, out_shape=[jax.ShapeDtypeStruct((R, C), F32)] * 4, compiler_params=_params())(w, m, v, gparts)


def _pack(arrays):
    flat = []
    for a in arrays:
        f = a.reshape(-1)
        pad = _ceil_to(f.shape[0], LANE) - f.shape[0]
        flat.append(jnp.concatenate([f, jnp.zeros((pad,), f.dtype)]) if pad else f)
    buf = jnp.concatenate(flat)
    rows = _ceil_to(buf.shape[0] // LANE, 64)
    buf = jnp.concatenate([buf, jnp.zeros((rows * LANE - buf.shape[0],), buf.dtype)])
    return buf.reshape(rows, LANE)


def _unpack(buf, shapes):
    flat, out, off = buf.reshape(-1), [], 0
    for s in shapes:
        size = 1
        for d in s:
            size *= d
        out.append(flat[off:off + size].reshape(s))
        off += _ceil_to(size, LANE)
    return out


def kernel(x, norm_mix_g, w_in, shift_mu, w0, w_lora_up, a0, a_lora_up, g_lora_up, k_k, k_a, r_k, lnx_g, lnx_b, w_proj_rwkv, sgu_ln_g, sgu_ln_b, sgu_w, sgu_b, w_proj_sgu, w_out, norm_ffn_g, w_ffn_gate, w_ffn_up, w_ffn_down, norm_final_g, loss_target, m_norm_mix_g, m_w_in, m_shift_mu, m_w0, m_w_lora_up, m_a0, m_a_lora_up, m_g_lora_up, m_k_k, m_k_a, m_r_k, m_lnx_g, m_lnx_b, m_w_proj_rwkv, m_sgu_ln_g, m_sgu_ln_b, m_sgu_w, m_sgu_b, m_w_proj_sgu, m_w_out, m_norm_ffn_g, m_w_ffn_gate, m_w_ffn_up, m_w_ffn_down, m_norm_final_g, v_norm_mix_g, v_w_in, v_shift_mu, v_w0, v_w_lora_up, v_a0, v_a_lora_up, v_g_lora_up, v_k_k, v_k_a, v_r_k, v_lnx_g, v_lnx_b, v_w_proj_rwkv, v_sgu_ln_g, v_sgu_ln_b, v_sgu_w, v_sgu_b, v_w_proj_sgu, v_w_out, v_norm_ffn_g, v_w_ffn_gate, v_w_ffn_up, v_w_ffn_down, v_norm_final_g):
    weights = dict(norm_mix_g=norm_mix_g, w_in=w_in, shift_mu=shift_mu, w0=w0, w_lora_up=w_lora_up, a0=a0, a_lora_up=a_lora_up,
                   g_lora_up=g_lora_up, k_k=k_k, k_a=k_a, r_k=r_k, lnx_g=lnx_g, lnx_b=lnx_b, w_proj_rwkv=w_proj_rwkv,
                   sgu_ln_g=sgu_ln_g, sgu_ln_b=sgu_ln_b, sgu_w=sgu_w, sgu_b=sgu_b, w_proj_sgu=w_proj_sgu, w_out=w_out,
                   norm_ffn_g=norm_ffn_g, w_ffn_gate=w_ffn_gate, w_ffn_up=w_ffn_up, w_ffn_down=w_ffn_down, norm_final_g=norm_final_g)
    m_in = dict(norm_mix_g=m_norm_mix_g, w_in=m_w_in, shift_mu=m_shift_mu, w0=m_w0, w_lora_up=m_w_lora_up, a0=m_a0,
                a_lora_up=m_a_lora_up, g_lora_up=m_g_lora_up, k_k=m_k_k, k_a=m_k_a, r_k=m_r_k, lnx_g=m_lnx_g, lnx_b=m_lnx_b,
                w_proj_rwkv=m_w_proj_rwkv, sgu_ln_g=m_sgu_ln_g, sgu_ln_b=m_sgu_ln_b, sgu_w=m_sgu_w, sgu_b=m_sgu_b,
                w_proj_sgu=m_w_proj_sgu, w_out=m_w_out, norm_ffn_g=m_norm_ffn_g, w_ffn_gate=m_w_ffn_gate, w_ffn_up=m_w_ffn_up,
                w_ffn_down=m_w_ffn_down, norm_final_g=m_norm_final_g)
    v_in = dict(norm_mix_g=v_norm_mix_g, w_in=v_w_in, shift_mu=v_shift_mu, w0=v_w0, w_lora_up=v_w_lora_up, a0=v_a0,
                a_lora_up=v_a_lora_up, g_lora_up=v_g_lora_up, k_k=v_k_k, k_a=v_k_a, r_k=v_r_k, lnx_g=v_lnx_g, lnx_b=v_lnx_b,
                w_proj_rwkv=v_w_proj_rwkv, sgu_ln_g=v_sgu_ln_g, sgu_ln_b=v_sgu_ln_b, sgu_w=v_sgu_w, sgu_b=v_sgu_b,
                w_proj_sgu=v_w_proj_sgu, w_out=v_w_out, norm_ffn_g=v_norm_ffn_g, w_ffn_gate=v_w_ffn_gate, w_ffn_up=v_w_ffn_up,
                w_ffn_down=v_w_ffn_down, norm_final_g=v_norm_final_g)
    names = list(weights)
    col_sharded = ("w_in", "w_lora_up", "a_lora_up", "g_lora_up", "w_proj_rwkv", "w_proj_sgu", "w_ffn_gate", "w_ffn_up")
    row_sharded = ("w_out", "w_ffn_down")
    sharded = [n for n in names if n in col_sharded or n in row_sharded]
    small = [n for n in names if n not in sharded]

    xs, tgt = x[0], loss_target[0]
    T, D = xs.shape
    RW = w0.shape[1]
    H = RW // HEAD
    SW = sgu_ln_g.shape[1]
    G = sgu_w.shape[1]
    assert 2 * SW == D, "the projection layout takes the SGU part to be as wide as a gate"
    lay = _rwkv_layout(RW, w_lora_up.shape[1], a_lora_up.shape[1], g_lora_up.shape[1], D)
    _, pw, _, rcp = lay
    icp = rcp + 3 * D
    b_ga, b_gb, b_z = rcp // D, rcp // D + 1, rcp // D + 2

    gather_groups = [["w_in", "w_lora_up", "a_lora_up", "g_lora_up"], ["w_proj_rwkv", "w_proj_sgu", "w_out"],
                     ["w_ffn_gate", "w_ffn_up", "w_ffn_down"]]
    gather, gather_token = _exchange_start([[(weights[n][0].astype(BF16), True) for n in grp] for grp in gather_groups], "gather_start")
    full = {}
    joined = lambda g: g.transpose(1, 0, 2).reshape(g.shape[1], -1)

    def take_weights(gi, after, name):
        for n, g in zip(gather_groups[gi], _exchange_wait(gather[gi], after, name)):
            full[n] = g.reshape(-1, g.shape[2]) if n in row_sharded else g

    n1 = _rms_fwd(xs, norm_mix_g, "rms_mix", deps=[gather_token])
    take_weights(0, n1, "gather_wait_in")
    W_in = _to_proj_layout(joined(full["w_in"]), lay, D)
    lora = [_pad_rows(joined(full[n]), rows) for n, rows in zip(("w_lora_up", "a_lora_up", "g_lora_up"), pw[3:])]
    mu_p = _to_proj_layout(shift_mu, lay, D)
    rsmall = [w0, a0, k_k, k_a]
    hp = [lnx_g.reshape(H, 1, HEAD), lnx_b.reshape(H, 1, HEAD), r_k.reshape(H, 1, HEAD)]
    ws = sgu_w[0]
    bexp = jnp.repeat(sgu_b[0].T, SGU_GROUP, axis=1)
    gf = norm_final_g.reshape(1, D)

    proj = _matmul(n1, W_in, mode="nn", out_dtype=F32, name="proj_in")
    ga, gb = (proj, D, b_ga), (proj, D, b_gb)
    r_h, lw_h, k2_h, v_h, aa_h, bb_h, g_h = _rwkv_pre(proj, mu_p, rsmall, lora, lay, "rwkv_pre")
    wkv_in = [r_h, lw_h, k2_h, v_h, aa_h, bb_h]
    y_h, states = _wkv_fwd(*wkv_in, "wkv_fwd")
    ya = _head_post(y_h, r_h, k2_h, v_h, g_h, hp, "head_post")
    yb = _sgu_fwd(proj, b_z, sgu_ln_g, sgu_ln_b, ws, bexp, "sgu_fwd")
    take_weights(1, ya, "gather_wait_proj")
    pa = _matmul(ya, full["w_proj_rwkv"], mode="nn", out_dtype=F32, name="proj_a")
    pb = _matmul(yb, full["w_proj_sgu"], mode="nn", out_dtype=F32, name="proj_b")

    def merge_fn(rv, pv):
        ga_v, gb_v, pa_v, pb_v = rv
        return [_sigmoid(ga_v) * pa_v + _sigmoid(gb_v) * pb_v], []
    merged = _rowwise(merge_fn, [ga, gb, pa, pb], [], [(D, BF16)], [], name="merge")[0]
    h1 = _matmul(merged, full["w_out"], mode="nn", out_dtype=F32, name="out_proj", add=xs)
    n2 = _rms_fwd(h1, norm_ffn_g, "rms_ffn")
    take_weights(2, n2, "gather_wait_ffn")
    gt = _matmul(n2, full["w_ffn_gate"], mode="nn", out_dtype=F32, name="ffn_gate", out_blocks=N_DEV)
    up = _matmul(n2, full["w_ffn_up"], mode="nn", out_dtype=F32, name="ffn_up", out_blocks=N_DEV)
    fb = gt.shape[2]
    flat = lambda t: t.reshape(N_DEV * T, fb)
    blocked = lambda t: t.reshape(N_DEV, T, fb)

    def act_fn(rv, pv):
        gt_v, up_v = rv
        return [gt_v * _sigmoid(gt_v) * up_v], []
    act = blocked(_rowwise(act_fn, [flat(gt), flat(up)], [], [(fb, BF16)], [], name="ffn_act", tm=1024)[0])
    h2 = _matmul(act, full["w_ffn_down"], mode="nn", out_dtype=F32, name="ffn_down", add=h1)

    def final_fn(rv, pv):
        (h_v, t_v), (g_v,) = rv, pv
        r = lax.rsqrt(_mean(h_v * h_v) + RMS_EPS)
        yn = h_v * r
        e = yn * g_v - t_v
        loss = 0.5 * jnp.sum(_mean(e * e))
        dout = e * (1.0 / D)
        dyg = dout * g_v
        dh = r * (dyg - yn * _mean(dyg * yn))
        return [dh, dh], [jnp.full((1, LANE), loss, F32), _colsum(dout * yn)]
    dh2, dh2_bf, loss_part, d_gf = _rowwise(final_fn, [h2, tgt], [gf], [(D, F32), (D, BF16)], [(1, LANE), (1, D)], name="final_loss")

    grads = {}

    def start_scatter(group, name, extra=()):
        blocks = [(grads[n].reshape(N_DEV, -1, grads[n].shape[1]) if n in row_sharded else grads[n], False) for n in group]
        (handle,), token = _exchange_start([blocks + list(extra)], name)
        return handle, token

    dact = _matmul(dh2_bf, full["w_ffn_down"], mode="nt", out_dtype=F32, name="d_act", out_blocks=N_DEV)
    grads["w_ffn_down"] = _matmul(act, dh2_bf, mode="tn", out_dtype=BF16, name="dw_ffn_down")

    def dact_fn(rv, pv):
        d_v, gt_v, up_v = rv
        s = _sigmoid(gt_v)
        return [d_v * up_v * (s * (1.0 + gt_v * (1.0 - s))), d_v * gt_v * s], []
    dgt, dup = (blocked(t) for t in _rowwise(dact_fn, [flat(dact), flat(gt), flat(up)], [], [(fb, BF16)] * 2, [],
                                             name="d_ffn_act", tm=1024))
    dn2 = _matmul(dgt, full["w_ffn_gate"], mode="nt", out_dtype=F32, name="dn2_gate")
    dn2 = _matmul(dup, full["w_ffn_up"], mode="nt", out_dtype=F32, name="dn2_up", add=dn2)
    grads["w_ffn_gate"] = _matmul(n2, dgt, mode="tn", out_dtype=BF16, name="dw_ffn_gate", out_blocks=N_DEV)
    grads["w_ffn_up"] = _matmul(n2, dup, mode="tn", out_dtype=BF16, name="dw_ffn_up", out_blocks=N_DEV)
    scatter_groups = [["w_ffn_down", "w_ffn_gate", "w_ffn_up"], ["w_out", "w_proj_rwkv", "w_proj_sgu"],
                      ["w_in", "w_lora_up", "a_lora_up", "g_lora_up"]]
    scatter_ffn, token_ffn = start_scatter(scatter_groups[0], "scatter_start_ffn")
    dh1, dh1_bf, d_g2 = _rms_bwd(dn2, h1, dh2, norm_ffn_g, "rms_ffn_bwd", deps=[token_ffn])
    dmerged = _matmul(dh1_bf, full["w_out"], mode="nt", out_dtype=F32, name="d_merged")
    grads["w_out"] = _matmul(merged, dh1_bf, mode="tn", out_dtype=BF16, name="dw_out")

    def dmerge_fn(rv, pv):
        d_v, ga_v, gb_v, pa_v, pb_v = rv
        sa, sb = _sigmoid(ga_v), _sigmoid(gb_v)
        dgates = jnp.concatenate([d_v * pa_v * sa * (1.0 - sa), d_v * pb_v * sb * (1.0 - sb)], axis=1)
        return [dgates, d_v * sa, d_v * sb], []
    dproj, dpa, dpb = _rowwise(dmerge_fn, [dmerged, ga, gb, pa, pb], [],
                               [(2 * D, BF16, icp, b_ga // 2, None), (D, BF16), (D, BF16)], [], name="d_merge")
    dya = _matmul(dpa, full["w_proj_rwkv"], mode="nt", out_dtype=F32, name="d_ya")
    dyb = _matmul(dpb, full["w_proj_sgu"], mode="nt", out_dtype=F32, name="d_yb")
    grads["w_proj_rwkv"] = _matmul(ya, dpa, mode="tn", out_dtype=BF16, name="dw_proj_a", out_blocks=N_DEV)
    grads["w_proj_sgu"] = _matmul(yb, dpb, mode="tn", out_dtype=BF16, name="dw_proj_b", out_blocks=N_DEV)
    scatter_mid, token_mid = start_scatter(scatter_groups[1], "scatter_start_mid")
    dproj, d_lng, d_lnb, d_ws, d_bs = _sgu_bwd(proj, b_z, dyb, sgu_ln_g, sgu_ln_b, ws, bexp, dproj, "sgu_bwd")

    dy_h, dr1, dk1, dv1, dg_h, d_lnxg, d_lnxb, d_rk = _head_post_bwd(dya, y_h, r_h, k2_h, v_h, g_h, hp, "head_post_bwd",
                                                                     deps=[token_mid])
    dr2, dlw_h, dk2b, dv2, daa, dbb = _wkv_bwd(*wkv_in, states, dy_h, "wkv_bwd")
    dps, d_mu, d_w0, d_a0, d_kk, d_ka, d_wlw, d_wla, d_wlg = _rwkv_pre_bwd(
        proj, mu_p, rsmall, lora, [dr1, dr2, dk1, dk2b, dv1, dv2, dlw_h, daa, dbb, dg_h], lay, "rwkv_pre_bwd")
    dproj = _shift_bwd(dps, mu_p, dproj, "shift_bwd")
    dn1 = _matmul(dproj, W_in, mode="nt", out_dtype=F32, name="dn1")
    split = lambda g: g.reshape(g.shape[0], N_DEV, -1).transpose(1, 0, 2)
    grads["w_in"] = split(_from_proj_layout(_matmul(n1, dproj, mode="tn", out_dtype=BF16, name="dw_in"), lay, D))
    dx, _, d_g1 = _rms_bwd(dn1, xs, dh1, norm_mix_g, "rms_mix_bwd")
    grads["w_lora_up"] = split(d_wlw[:w_lora_up.shape[1]].astype(BF16))
    grads["a_lora_up"] = split(d_wla[:a_lora_up.shape[1]].astype(BF16))
    grads["g_lora_up"] = split(d_wlg[:g_lora_up.shape[1]].astype(BF16))
    small_grads = dict(norm_mix_g=d_g1, shift_mu=_from_proj_layout(d_mu, lay, D), w0=d_w0, a0=d_a0, k_k=d_kk, k_a=d_ka, r_k=d_rk,
                       lnx_g=d_lnxg, lnx_b=d_lnxb, sgu_ln_g=d_lng, sgu_ln_b=d_lnb, sgu_w=d_ws, sgu_b=d_bs[:, :G].T,
                       norm_ffn_g=d_g2, norm_final_g=d_gf)

    scatter_in, token_in = start_scatter(scatter_groups[2], "scatter_start_in", extra=[(_pack([small_grads[n] for n in small]), True)])
    out = {}
    after = token_in
    small_parts = None
    for group, handle, name in zip(scatter_groups, (scatter_ffn, scatter_mid, scatter_in), ("ffn", "mid", "in")):
        parts = _exchange_wait(handle, after, "scatter_wait_" + name)
        for n, part in zip(group, parts):
            shp = weights[n].shape
            res = _adamw(weights[n][0], m_in[n][0], v_in[n][0], part, "adamw_" + n)
            out[n] = [t.reshape(shp) for t in res]
            after = res[0]
        small_parts = parts[-1]
    packed = [_pack([d[n] for n in small]) for d in (weights, m_in, v_in)]
    res = _adamw(*packed, small_parts, "adamw_small")
    unpacked = [_unpack(t, [weights[n].shape for n in small]) for t in res]
    for i, n in enumerate(small):
        out[n] = [u[i] for u in unpacked]

    loss = lax.psum(loss_part[0, 0], ("x", "y", "c"))
    return (loss, dx[None], *[out[n][0] for n in names], *[out[n][1] for n in names],
            *[out[n][2] for n in names], *[out[n][3] for n in names])
```

```python
import jax
import jax.numpy as jnp
from jax import lax
from jax.experimental import pallas as pl
from jax.experimental.pallas import tpu as pltpu

F32 = jnp.float32
BF16 = jnp.bfloat16

N_DEV = 8
LANE = 128
SUBLANE = 8
HEAD = 64
SGU_CHUNK = 128
SGU_GROUP = 128
WKV_CHUNK = 64
RMS_EPS = 1e-6
LN_EPS = 1e-5
LNX_EPS = 64e-5
ADAM_LR, ADAM_B1, ADAM_B2, ADAM_EPS, ADAM_WD, ADAM_STEP = 0.001, 0.9, 0.999, 1e-08, 0.01, 10
VMEM_LIMIT_BYTES = 48 * 1024 * 1024
_SQRT_HALF = 0.7071067811865476
_INV_SQRT_2PI = 0.3989422804014327


def _pick(n, cands):
    for c in cands:
        if n % c == 0:
            return c
    return n


def _ceil_to(n, m):
    return -(-n // m) * m


def _params():
    return pltpu.CompilerParams(vmem_limit_bytes=VMEM_LIMIT_BYTES)


def _tile(n, cap):
    best = 0
    for d in range(LANE, min(n, cap) + 1, LANE):
        if n % d == 0:
            best = d
    return best or n


def _matmul_tiles(M, N, K, a_bytes, b_bytes, o_bytes, has_add, forced):
    tm = forced.get("m") or _tile(M, 1024)
    tn = forced.get("n") or _tile(N, 1024)
    tk = forced.get("k") or _tile(K, 2048)

    def vmem(tm, tn, tk):
        acc = tm * tn * 4 if tk < K else 0
        return 2 * (tm * tk * a_bytes + tk * tn * b_bytes + tm * tn * (o_bytes + (4 if has_add else 0))) + acc

    while vmem(tm, tn, tk) > (VMEM_LIMIT_BYTES * 3) // 4:
        if "k" not in forced and tk > 512 and _tile(K, tk // 2) < tk:
            tk = _tile(K, tk // 2)
        elif "m" not in forced and _tile(M, tm // 2) < tm:
            tm = _tile(M, tm // 2)
        else:
            break
    return tm, tn, tk


def _matmul(a, b, *, mode, out_dtype, name, add=None, deps=(), out_blocks=0):
    def view(x):
        return (x.shape[1], x.shape[0] * x.shape[2], x.shape[2]) if x.ndim == 3 else (x.shape[0], x.shape[1], 0)

    (ar, ac, aw), (br, bc, bw) = view(a), view(b)
    a_col, b_col = {"nn": ("k", "n"), "nt": ("k", "k"), "tn": ("m", "n")}[mode]
    if mode == "nn":
        M, K, K2, N = ar, ac, br, bc
    elif mode == "nt":
        M, K, N, K2 = ar, ac, br, bc
    else:
        K, M, K2, N = ar, ac, br, bc
    assert K == K2, (a.shape, b.shape, mode)
    forced = {}
    for dim, w in ((a_col, aw), (b_col, bw), ("n", N // out_blocks if out_blocks else 0)):
        if w:
            assert forced.get(dim, w) == w
            forced[dim] = w
    has_add = add is not None
    tm, tn, tk = _matmul_tiles(M, N, K, a.dtype.itemsize, b.dtype.itemsize, jnp.dtype(out_dtype).itemsize, has_add, forced)
    nk = K // tk
    dn = {"nn": (((1,), (0,)), ((), ())), "nt": (((1,), (1,)), ((), ())), "tn": (((0,), (0,)), ((), ()))}[mode]
    pick = {"m": lambda i, j, k: i, "n": lambda i, j, k: j, "k": lambda i, j, k: k}
    size = {"m": tm, "n": tn, "k": tk}

    def spec(blocked, row_dim, col_dim):
        rf, cf = pick[row_dim], pick[col_dim]
        if blocked:
            return pl.BlockSpec((None, size[row_dim], size[col_dim]), lambda i, j, k: (cf(i, j, k), rf(i, j, k), 0))
        return pl.BlockSpec((size[row_dim], size[col_dim]), lambda i, j, k: (rf(i, j, k), cf(i, j, k)))

    a_spec = spec(aw, "k" if mode == "tn" else "m", a_col)
    b_spec = spec(bw, "n" if mode == "nt" else "k", b_col)
    o_spec = spec(out_blocks, "m", "n")
    n_in = 2 + has_add + len(deps)

    def body(*refs):
        a_ref, b_ref = refs[0], refs[1]
        add_ref = refs[2] if has_add else None
        o_ref = refs[n_in]
        part = lax.dot_general(a_ref[...].astype(BF16), b_ref[...].astype(BF16), dn, preferred_element_type=F32)
        if nk == 1:
            if has_add:
                part = part + add_ref[...]
            o_ref[...] = part.astype(out_dtype)
            return
        acc_ref = refs[-1]
        kk = pl.program_id(2)

        @pl.when(kk == 0)
        def _():
            acc_ref[...] = part + add_ref[...] if has_add else part

        @pl.when(kk > 0)
        def _():
            acc_ref[...] += part

        @pl.when(kk == nk - 1)
        def _():
            o_ref[...] = acc_ref[...].astype(out_dtype)

    ins = [a, b] + ([add] if has_add else []) + list(deps)
    in_specs = ([a_spec, b_spec] + ([o_spec] if has_add else [])
                + [pl.BlockSpec(d.shape, lambda i, j, k, nd=d.ndim: (0,) * nd) for d in deps])
    return pl.pallas_call(
        body, name=name, grid=(M // tm, N // tn, nk), in_specs=in_specs, out_specs=o_spec,
        out_shape=jax.ShapeDtypeStruct((out_blocks, M, tn) if out_blocks else (M, N), out_dtype),
        scratch_shapes=[pltpu.VMEM((tm, tn), F32)] if nk > 1 else [],
        compiler_params=_params())(*ins)


def _rowwise(fn, rows, pars, row_outs, acc_outs, *, name, tm=256, deps=()):
    rows = [r if isinstance(r, tuple) else (r, r.shape[1], 0) for r in rows]
    row_outs = [o if len(o) == 5 else (o[0], o[1], o[0], 0, None) for o in row_outs]
    aliased = [(k, o[4]) for k, o in enumerate(row_outs) if o[4] is not None]
    R = rows[0][0].shape[0]
    if max(w for _, w, _ in rows) > 4096:
        tm = tm // 2
    tm = min(tm, R)
    assert R % tm == 0
    nr, npar = len(rows), len(pars)
    nro = len(row_outs)
    n_in = nr + npar + len(deps) + len(aliased)

    def body(*refs):
        rv = [r[...] for r in refs[:nr]]
        pv = [p[...] for p in refs[nr:nr + npar]]
        outs = refs[n_in:]
        ro, ao = fn(rv, pv)
        first = pl.program_id(0) == 0
        for o_ref, val in zip(outs[:nro], ro):
            o_ref[...] = val.astype(o_ref.dtype)

        @pl.when(first)
        def _():
            for o_ref, val in zip(outs[nro:], ao):
                o_ref[...] = val

        @pl.when(jnp.logical_not(first))
        def _():
            for o_ref, val in zip(outs[nro:], ao):
                o_ref[...] += val

    in_specs = ([pl.BlockSpec((tm, w), lambda i, cb=cb: (i, cb)) for _, w, cb in rows]
                + [pl.BlockSpec(p.shape, lambda i, nd=p.ndim: (0,) * nd) for p in list(pars) + list(deps)]
                + [pl.BlockSpec(memory_space=pl.ANY)] * len(aliased))
    out_shape = ([jax.ShapeDtypeStruct((R, full), dt) for _, dt, full, _, _ in row_outs]
                 + [jax.ShapeDtypeStruct(s, F32) for s in acc_outs])
    out_specs = ([pl.BlockSpec((tm, f), lambda i, cb=cb: (i, cb)) for f, _, _, cb, _ in row_outs]
                 + [pl.BlockSpec(s, lambda i, nd=len(s): (0,) * nd) for s in acc_outs])
    res = pl.pallas_call(body, name=name, grid=(R // tm,), in_specs=in_specs, out_specs=out_specs, out_shape=out_shape,
                         input_output_aliases={n_in - len(aliased) + q: k for q, (k, _) in enumerate(aliased)},
                         compiler_params=_params())(*[r for r, _, _ in rows], *pars, *deps, *[buf for _, buf in aliased])
    return list(res)


def _bdot(a, b, mode="nn"):
    dn = {"nn": (((1,), (0,)), ((), ())), "nt": (((1,), (1,)), ((), ())), "tn": (((0,), (0,)), ((), ()))}[mode]
    return lax.dot_general(a.astype(BF16), b.astype(BF16), dn, preferred_element_type=F32)


def _sigmoid(x):
    return jax.nn.sigmoid(x)


def _softplus(x):
    return jnp.maximum(x, 0.0) + jnp.log1p(jnp.exp(-jnp.abs(x)))


def _gelu(z):
    return 0.5 * z * (1.0 + lax.erf(z * _SQRT_HALF))


def _gelu_grad(z):
    return 0.5 * (1.0 + lax.erf(z * _SQRT_HALF)) + z * jnp.exp(-0.5 * z * z) * _INV_SQRT_2PI


def _mean(x):
    return jnp.mean(x, axis=-1, keepdims=True)


def _colsum(x):
    return jnp.sum(x, axis=0, keepdims=True)


def _rms_fwd(x, g, name, deps=()):
    def fn(rv, pv):
        (xv,), (gv,) = rv, pv
        r = lax.rsqrt(_mean(xv * xv) + RMS_EPS)
        return [xv * r * gv], []
    return _rowwise(fn, [x], [g], [(x.shape[1], BF16)], [], name=name, deps=deps)[0]


def _rms_bwd(dn, x, dres, g, name, deps=()):
    def fn(rv, pv):
        (dnv, xv, drv), (gv,) = rv, pv
        r = lax.rsqrt(_mean(xv * xv) + RMS_EPS)
        yn = xv * r
        dyg = dnv * gv
        dx = drv + r * (dyg - yn * _mean(dyg * yn))
        return [dx, dx], [_colsum(dnv * yn)]
    D = x.shape[1]
    return _rowwise(fn, [dn, x, dres], [g], [(D, F32), (D, BF16)], [(1, D)], name=name, deps=deps)


def _rwkv_layout(RW, Lw, La, Lg, D):
    widths = [RW, RW, RW, Lw, La, Lg]
    pw = [_ceil_to(w, LANE) for w in widths]
    pw[5] += _ceil_to(sum(pw), 2 * D) - sum(pw)
    offs = [sum(pw[:i]) for i in range(6)]
    return widths, pw, offs, sum(pw)


def _to_proj_layout(a, lay, D):
    widths, pw, _, _ = lay
    pieces, src = [], 0
    for w, p in zip(widths, pw):
        pieces.append(a[:, src:src + w])
        if p > w:
            pieces.append(jnp.zeros((a.shape[0], p - w), a.dtype))
        src += w
    if a.shape[1] > src:
        pieces += [a[:, src + D:src + 3 * D], a[:, src:src + D]]
    return jnp.concatenate(pieces, axis=1)


def _from_proj_layout(a, lay, D):
    widths, _, offs, rcp = lay
    rest = [a[:, rcp + 2 * D:], a[:, rcp:rcp + 2 * D]] if a.shape[1] > rcp else []
    return jnp.concatenate([a[:, o:o + w] for o, w in zip(offs, widths)] + rest, axis=1)


def _pad_rows(a, rows):
    return a if a.shape[0] == rows else jnp.concatenate([a, jnp.zeros((rows - a.shape[0], a.shape[1]), a.dtype)], axis=0)


def _token_shift(p, halo, mu, i):
    tm = p.shape[0]
    hid = lax.broadcasted_iota(jnp.int32, (SUBLANE, 1), 0)
    before = jnp.sum(jnp.where(hid == SUBLANE - 1, halo, 0.0), axis=0, keepdims=True)
    before = jnp.where(i == 0, 0.0, before)
    rid = lax.broadcasted_iota(jnp.int32, (tm, 1), 0)
    prev = jnp.where(rid == 0, before, pltpu.roll(p, 1, 0))
    d = prev - p
    return p + d * mu, d


def _rwkv_math(ps, w0, a0, k_k, k_a, wlw, wla, wlg, lay):
    _, pw, offs, _ = lay
    r, k, v, xw, xa, xg = (ps[:, offs[j]:offs[j] + pw[j]] for j in range(6))
    tw = jnp.tanh(xw)
    ww = w0 + _bdot(tw, wlw)
    lw = -jnp.exp(-_softplus(-ww) - 0.5)
    a = _sigmoid(a0 + _bdot(xa, wla))
    sg = _sigmoid(xg)
    g = _bdot(sg, wlg)
    return dict(r=r, k=k, v=v, xa=xa, tw=tw, ww=ww, lw=lw, a=a, sg=sg, g=g, kkp=k * k_k, k2=k * (1.0 + (a - 1.0) * k_a))


def _halo_specs(T, tm, width, after):
    hb = tm // SUBLANE
    last = T // SUBLANE - 1
    if after:
        return pl.BlockSpec((SUBLANE, width), lambda i: (jnp.minimum((i + 1) * hb, last), 0))
    return pl.BlockSpec((SUBLANE, width), lambda i: (jnp.maximum(i * hb - 1, 0), 0))


def _rowsum(x):
    return jnp.sum(x, axis=-1, keepdims=True)


def _kk_math(kkp):
    nrm = jnp.sqrt(_rowsum(kkp * kkp))
    inv = 1.0 / jnp.maximum(nrm, 1e-12)
    return nrm, inv, kkp * inv


def _rwkv_pre(p, mu, small, lora, lay, name):
    T, rcp = p.shape[0], lay[3]
    H = lay[0][0] // HEAD
    tm = min(128, T)

    def body(p_ref, ph_ref, mu_ref, w0_ref, a0_ref, kk_ref, ka_ref, wlw_ref, wla_ref, wlg_ref, r_o, lw_o, k2_o, v_o, aa_o, bb_o, g_o):
        ps, _ = _token_shift(p_ref[...], ph_ref[...], mu_ref[...], pl.program_id(0))
        q = _rwkv_math(ps, w0_ref[...], a0_ref[...], kk_ref[...], ka_ref[...], wlw_ref[...], wla_ref[...], wlg_ref[...], lay)
        for h in range(H):
            sl = slice(h * HEAD, (h + 1) * HEAD)
            for o_ref, key in ((r_o, "r"), (lw_o, "lw"), (k2_o, "k2"), (v_o, "v"), (g_o, "g")):
                o_ref[h] = q[key][:, sl]
            _, _, kk = _kk_math(q["kkp"][:, sl])
            aa_o[h] = -kk
            bb_o[h] = kk * q["a"][:, sl]

    whole = lambda arr: pl.BlockSpec(arr.shape, lambda i: (0, 0))
    return pl.pallas_call(
        body, name=name, grid=(T // tm,),
        in_specs=([pl.BlockSpec((tm, rcp), lambda i: (i, 0)), _halo_specs(T, tm, rcp, False), whole(mu)]
                  + [whole(s) for s in small] + [whole(w) for w in lora]),
        out_specs=[pl.BlockSpec((H, tm, HEAD), lambda i: (0, i, 0))] * 7, out_shape=[jax.ShapeDtypeStruct((H, T, HEAD), F32)] * 7,
        compiler_params=_params())(p, p, mu, *small, *lora)


def _rwkv_pre_bwd(p, mu, small, lora, hgrads, lay, name):
    T, rcp = p.shape[0], lay[3]
    widths, pw, offs, _ = lay
    RW = widths[0]
    H = RW // HEAD
    tm = min(128, T)

    def body(p_ref, ph_ref, mu_ref, w0_ref, a0_ref, kk_ref, ka_ref, wlw_ref, wla_ref, wlg_ref,
             dr1, dr2, dk1, dk2b, dv1, dv2, dlw_h, daa, dbb, dg_h,
             dps_ref, dmu_ref, dw0_ref, da0_ref, dkk_ref, dka_ref, dwlw_ref, dwla_ref, dwlg_ref,
             s_dr, s_dk2, s_dv, s_dlw, s_dkkp, s_da, s_dg):
        i = pl.program_id(0)
        ps, dprev = _token_shift(p_ref[...], ph_ref[...], mu_ref[...], i)
        k_k, k_a = kk_ref[...], ka_ref[...]
        q = _rwkv_math(ps, w0_ref[...], a0_ref[...], k_k, k_a, wlw_ref[...], wla_ref[...], wlg_ref[...], lay)
        k, a, lw, ww, tw, sg = q["k"], q["a"], q["lw"], q["ww"], q["tw"], q["sg"]
        for h in range(H):
            sl = slice(h * HEAD, (h + 1) * HEAD)
            s_dr[:, sl] = dr1[h] + dr2[h]
            s_dk2[:, sl] = dk1[h] + dk2b[h]
            s_dv[:, sl] = dv1[h] + dv2[h]
            s_dlw[:, sl] = dlw_h[h]
            s_dg[:, sl] = dg_h[h]
            nrm, inv, kk = _kk_math(q["kkp"][:, sl])
            dbb_h = dbb[h]
            dkk = dbb_h * a[:, sl] - daa[h]
            s_dkkp[:, sl] = jnp.where(nrm > 1e-12, inv * (dkk - kk * _rowsum(dkk * kk)), dkk * inv)
            s_da[:, sl] = dbb_h * kk
        dk2, dkkp, dg = s_dk2[...], s_dkkp[...], s_dg[...]
        dk = dk2 * (1.0 + (a - 1.0) * k_a) + dkkp * k_k
        da = s_da[...] + dk2 * k * k_a
        dpa = da * a * (1.0 - a)
        dww = s_dlw[...] * lw * _sigmoid(-ww)
        dxa = _bdot(dpa, wla_ref[...], "nt")
        dxw = _bdot(dww, wlw_ref[...], "nt") * (1.0 - tw * tw)
        dxg = _bdot(dg, wlg_ref[...], "nt") * sg * (1.0 - sg)
        segs = (s_dr[...], dk, s_dv[...], dxw, dxa, dxg)
        sums = [dmu_ref, dw0_ref, da0_ref, dkk_ref, dka_ref, dwlw_ref, dwla_ref, dwlg_ref]

        @pl.when(i == 0)
        def _():
            for s in sums:
                s[...] = jnp.zeros_like(s)

        for j, seg in enumerate(segs):
            sl = slice(offs[j], offs[j] + pw[j])
            dps_ref[:, sl] = seg
            dmu_ref[:, sl] += _colsum(seg * dprev[:, sl])
        dw0_ref[...] += _colsum(dww)
        da0_ref[...] += _colsum(dpa)
        dkk_ref[...] += _colsum(dkkp * k)
        dka_ref[...] += _colsum(dk2 * k * (a - 1.0))
        dwlw_ref[...] += _bdot(tw, dww, "tn")
        dwla_ref[...] += _bdot(q["xa"], dpa, "tn")
        dwlg_ref[...] += _bdot(sg, dg, "tn")

    whole = lambda arr: pl.BlockSpec(arr.shape, lambda i: (0, 0))
    row = lambda w: pl.BlockSpec((tm, w), lambda i: (i, 0))
    acc_shapes = [(1, rcp), (1, RW), (1, RW), (1, RW), (1, RW)] + [w.shape for w in lora]
    return pl.pallas_call(
        body, name=name, grid=(T // tm,),
        in_specs=([row(rcp), _halo_specs(T, tm, rcp, False), whole(mu)] + [whole(s) for s in small] + [whole(w) for w in lora]
                  + [pl.BlockSpec((H, tm, HEAD), lambda i: (0, i, 0))] * 10),
        out_specs=[row(rcp)] + [pl.BlockSpec(s, lambda i: (0, 0)) for s in acc_shapes],
        out_shape=[jax.ShapeDtypeStruct((T, rcp), F32)] + [jax.ShapeDtypeStruct(s, F32) for s in acc_shapes],
        scratch_shapes=[pltpu.VMEM((tm, RW), F32)] * 7, compiler_params=_params())(p, p, mu, *small, *lora, *hgrads)


def _shift_bwd(dps, mu, dproj, name):
    T, rcp = dps.shape
    tm = min(256, T)
    nt = T // tm

    def body(d_ref, dh_ref, mu_ref, buf_ref, o_ref):
        i = pl.program_id(0)
        d = d_ref[...]
        hid = lax.broadcasted_iota(jnp.int32, (SUBLANE, 1), 0)
        after = jnp.sum(jnp.where(hid == 0, dh_ref[...], 0.0), axis=0, keepdims=True)
        after = jnp.where(i == nt - 1, 0.0, after)
        rid = lax.broadcasted_iota(jnp.int32, (tm, 1), 0)
        nxt = jnp.where(rid == tm - 1, after, pltpu.roll(d, tm - 1, 0))
        mu_v = mu_ref[...]
        o_ref[...] = (d * (1.0 - mu_v) + nxt * mu_v).astype(BF16)

    row = pl.BlockSpec((tm, rcp), lambda i: (i, 0))
    return pl.pallas_call(
        body, name=name, grid=(nt,),
        in_specs=[row, _halo_specs(T, tm, rcp, True), pl.BlockSpec(mu.shape, lambda i: (0, 0)), pl.BlockSpec(memory_space=pl.ANY)],
        out_specs=row, out_shape=jax.ShapeDtypeStruct(dproj.shape, BF16), input_output_aliases={3: 0},
        compiler_params=_params())(dps, dps, mu, dproj)


def _head_post_math(y, r, k2, v, lg, lb, rk):
    yc = y - _mean(y)
    rstd = lax.rsqrt(_mean(yc * yc) + LNX_EPS)
    yn = yc * rstd
    s = _rowsum(r * k2 * rk)
    return yn, rstd, yn * lg + lb + s * v, s


def _head_post(y, r, k2, v, g, hp, name):
    H, T, _ = y.shape
    tm = min(128, T)

    def body(y_ref, r_ref, k_ref, v_ref, g_ref, lg_ref, lb_ref, rk_ref, o_ref):
        _, _, t, _ = _head_post_math(y_ref[...], r_ref[...], k_ref[...], v_ref[...], lg_ref[...], lb_ref[...], rk_ref[...])
        out = (t * g_ref[...]).astype(BF16)
        for h in range(H):
            o_ref[:, h * HEAD:(h + 1) * HEAD] = out[h]

    blk = pl.BlockSpec((H, tm, HEAD), lambda i: (0, i, 0))
    par = pl.BlockSpec((H, 1, HEAD), lambda i: (0, 0, 0))
    return pl.pallas_call(
        body, name=name, grid=(T // tm,), in_specs=[blk] * 5 + [par] * 3, out_specs=pl.BlockSpec((tm, H * HEAD), lambda i: (i, 0)),
        out_shape=jax.ShapeDtypeStruct((T, H * HEAD), BF16), compiler_params=_params())(y, r, k2, v, g, *hp)


def _head_post_bwd(dya, y, r, k2, v, g, hp, name, deps=()):
    H, T, _ = y.shape
    tm = min(128, T)
    hsum = lambda t: jnp.sum(t, axis=1, keepdims=True)

    def body(d_ref, y_ref, r_ref, k_ref, v_ref, g_ref, lg_ref, lb_ref, rk_ref, *rest):
        outs, d_s = rest[len(deps):len(deps) + 8], rest[-1]
        for h in range(H):
            d_s[h] = d_ref[:, h * HEAD:(h + 1) * HEAD]
        d_v, r_v, k_v, v_v, lg, rk = d_s[...], r_ref[...], k_ref[...], v_ref[...], lg_ref[...], rk_ref[...]
        yn, rstd, t, s = _head_post_math(y_ref[...], r_v, k_v, v_v, lg, lb_ref[...], rk)
        dyo = d_v * g_ref[...]
        dyn = dyo * lg
        ds = _rowsum(dyo * v_v)
        vals = (rstd * (dyn - _mean(dyn) - yn * _mean(dyn * yn)), ds * k_v * rk, ds * r_v * rk, dyo * s, d_v * t)
        for o_ref, val in zip(outs[:5], vals):
            o_ref[...] = val
        sums = (hsum(dyo * yn), hsum(dyo), hsum(ds * r_v * k_v))
        first = pl.program_id(0) == 0

        @pl.when(first)
        def _():
            for o_ref, val in zip(outs[5:], sums):
                o_ref[...] = val

        @pl.when(jnp.logical_not(first))
        def _():
            for o_ref, val in zip(outs[5:], sums):
                o_ref[...] += val

    blk = pl.BlockSpec((H, tm, HEAD), lambda i: (0, i, 0))
    par = pl.BlockSpec((H, 1, HEAD), lambda i: (0, 0, 0))
    return pl.pallas_call(
        body, name=name, grid=(T // tm,),
        in_specs=([pl.BlockSpec((tm, H * HEAD), lambda i: (i, 0))] + [blk] * 5 + [par] * 3
                  + [pl.BlockSpec(d.shape, lambda i, nd=d.ndim: (0,) * nd) for d in deps]),
        out_specs=[blk] * 5 + [par] * 3,
        out_shape=[jax.ShapeDtypeStruct((H, T, HEAD), F32)] * 5 + [jax.ShapeDtypeStruct((H, 1, HEAD), F32)] * 3,
        scratch_shapes=[pltpu.VMEM((H, tm, HEAD), F32)], compiler_params=_params())(dya, y, r, k2, v, g, *hp, *deps)


def _bmm(x, y, mode):
    dn = {"nn": (((2,), (1,)), ((0,), (0,))), "nt": (((2,), (2,)), ((0,), (0,))), "tn": (((1,), (1,)), ((0,), (0,)))}[mode]
    (xh, xl), (yh, yl) = _split(x), _split(y)
    dot = lambda p, q: lax.dot_general(p, q, dn, preferred_element_type=F32)
    out = dot(xh, yh)
    if yl is not None:
        out = out + dot(xh, yl)
    if xl is not None:
        out = out + dot(xl, yh)
    return out


def _split(x):
    if isinstance(x, tuple):
        return x
    hi = x.astype(BF16)
    return hi, (x - hi.astype(F32)).astype(BF16)


def _exact(x):
    return x.astype(BF16), None


def _wkv_chunk(r, lw, k, v, a, b):
    hb, C, _ = r.shape
    ti = lax.broadcasted_iota(jnp.int32, (C, C), 0)
    si = lax.broadcasted_iota(jnp.int32, (C, C), 1)
    linc, lstr, eye = (ti >= si).astype(F32), (ti > si).astype(F32), (ti == si).astype(F32)
    lincb = _exact(jnp.broadcast_to(linc, (hb, C, C)))
    lstrb = _exact(jnp.broadcast_to(lstr, (hb, C, C)))
    ones = _exact(jnp.ones_like(v))
    lws = _split(lw)
    ci = _bmm(lincb, lws, "nn")
    cC = jnp.sum(lw, axis=1, keepdims=True)
    gi, ge, gn, gr = jnp.exp(ci), jnp.exp(ci - lw), jnp.exp(-ci), jnp.exp(cC - ci)
    q = dict(At=a * ge, Rt=r * gi, Bt=b * gn, Kt=k * gn, Bh=b * gr, Kh=k * gr)
    s = {key: _split(val) for key, val in q.items()}
    s["v"] = _split(v)
    q["A_ab"] = _bmm(s["At"], s["Bt"], "nt") * lstr
    for key, lhs, rhs, mask in (("A_ak", "At", "Kt", lstr), ("A_rb", "Rt", "Bt", linc), ("A_rk", "Rt", "Kt", linc)):
        q[key] = _bmm(s[lhs], s[rhs], "nt") * mask
        s[key] = _split(q[key])
    Tm = eye + q["A_ab"]
    Pw = _split(q["A_ab"])
    n = 1
    while 2 * n < C:
        Pw = _split(_bmm(Pw, Pw, "nn"))
        Tm = Tm + _bmm(Tm, Pw, "nn")
        n *= 2
    s["Tm"] = _split(Tm)
    gC = jnp.exp(_bmm(lws, ones, "tn"))
    q.update(gi=gi, ge=ge, gn=gn, gr=gr, linc=linc, lstr=lstr, lincb=lincb, lstrb=lstrb, gC=gC, ones=ones, s=s)
    return q


def _wkv_fwd(r, lw, k, v, a, b, name):
    H, T, N = r.shape
    C = min(WKV_CHUNK, T)
    nc = T // C
    hb = _pick(H, (8, 4, 2))

    def body(r_ref, lw_ref, k_ref, v_ref, a_ref, b_ref, y_ref, st_ref, h_ref):
        @pl.when(pl.program_id(1) == 0)
        def _():
            h_ref[...] = jnp.zeros_like(h_ref)

        H0 = h_ref[...]
        st_ref[0] = H0
        q = _wkv_chunk(r_ref[...], lw_ref[...], k_ref[...], v_ref[...], a_ref[...], b_ref[...])
        s = q["s"]
        H0s = _split(H0)
        U = _split(_bmm(s["Tm"], _bmm(s["At"], H0s, "nn") + _bmm(s["A_ak"], s["v"], "nn"), "nn"))
        y_ref[...] = _bmm(s["Rt"], H0s, "nn") + _bmm(s["A_rb"], U, "nn") + _bmm(s["A_rk"], s["v"], "nn")
        h_ref[...] = q["gC"] * H0 + _bmm(s["Bh"], U, "tn") + _bmm(s["Kh"], s["v"], "tn")

    blk = pl.BlockSpec((hb, C, N), lambda h, c: (h, c, 0))
    return pl.pallas_call(
        body, name=name, grid=(H // hb, nc), in_specs=[blk] * 6,
        out_specs=[blk, pl.BlockSpec((1, hb, N, N), lambda h, c: (c, h, 0, 0))],
        out_shape=[jax.ShapeDtypeStruct((H, T, N), F32), jax.ShapeDtypeStruct((nc, H, N, N), F32)],
        scratch_shapes=[pltpu.VMEM((hb, N, N), F32)], compiler_params=_params())(r, lw, k, v, a, b)


def _wkv_bwd(r, lw, k, v, a, b, states, dy, name):
    H, T, N = r.shape
    C = min(WKV_CHUNK, T)
    nc = T // C
    hb = _pick(H, (8, 4, 2))

    def body(r_ref, lw_ref, k_ref, v_ref, a_ref, b_ref, st_ref, dy_ref, dr_ref, dlw_ref, dk_ref, dv_ref, da_ref, db_ref, dh_ref):
        @pl.when(pl.program_id(1) == 0)
        def _():
            dh_ref[...] = jnp.zeros_like(dh_ref)

        dHC = dh_ref[...]
        H0 = st_ref[0]
        q = _wkv_chunk(r_ref[...], lw_ref[...], k_ref[...], v_ref[...], a_ref[...], b_ref[...])
        s, gC = q["s"], q["gC"]
        H0s, dHs, dY = _split(H0), _split(dHC), _split(dy_ref[...])
        U = _split(_bmm(s["Tm"], _bmm(s["At"], H0s, "nn") + _bmm(s["A_ak"], s["v"], "nn"), "nn"))
        dU = _bmm(s["A_rb"], dY, "tn") + _bmm(s["Bh"], dHs, "nn")
        dP = _split(_bmm(s["Tm"], dU, "tn"))
        dv_ref[...] = _bmm(s["A_rk"], dY, "tn") + _bmm(s["Kh"], dHs, "nn") + _bmm(s["A_ak"], dP, "tn")
        dh_ref[...] = _bmm(s["Rt"], dY, "tn") + gC * dHC + _bmm(s["At"], dP, "tn")
        dA_rb = _split(_bmm(dY, U, "nt") * q["linc"])
        dA_rk = _split(_bmm(dY, s["v"], "nt") * q["linc"])
        dA_ab = _split(_bmm(dP, U, "nt") * q["lstr"])
        dA_ak = _split(_bmm(dP, s["v"], "nt") * q["lstr"])
        dRt = _bmm(dY, H0s, "nt") + _bmm(dA_rb, s["Bt"], "nn") + _bmm(dA_rk, s["Kt"], "nn")
        dAt = _bmm(dP, H0s, "nt") + _bmm(dA_ab, s["Bt"], "nn") + _bmm(dA_ak, s["Kt"], "nn")
        dBt = _bmm(dA_ab, s["At"], "tn") + _bmm(dA_rb, s["Rt"], "tn")
        dKt = _bmm(dA_ak, s["At"], "tn") + _bmm(dA_rk, s["Rt"], "tn")
        dBh = _bmm(U, dHs, "nt")
        dKh = _bmm(s["v"], dHs, "nt")
        dr_ref[...] = dRt * q["gi"]
        da_ref[...] = dAt * q["ge"]
        db_ref[...] = dBt * q["gn"] + dBh * q["gr"]
        dk_ref[...] = dKt * q["gn"] + dKh * q["gr"]
        tail = dBh * q["Bh"] + dKh * q["Kh"]
        dci = dRt * q["Rt"] - dBt * q["Bt"] - dKt * q["Kt"] - tail
        dcC = jnp.sum(tail, axis=1, keepdims=True) + _bmm(q["ones"], H0 * dHC * gC, "nt")
        dlw_ref[...] = _bmm(q["lincb"], dci, "tn") + _bmm(q["lstrb"], dAt * q["At"], "tn") + dcC

    blk = pl.BlockSpec((hb, C, N), lambda h, c: (h, nc - 1 - c, 0))
    st = pl.BlockSpec((1, hb, N, N), lambda h, c: (nc - 1 - c, h, 0, 0))
    return pl.pallas_call(
        body, name=name, grid=(H // hb, nc), in_specs=[blk] * 6 + [st, blk], out_specs=[blk] * 6,
        out_shape=[jax.ShapeDtypeStruct((H, T, N), F32)] * 6,
        scratch_shapes=[pltpu.VMEM((hb, N, N), F32)], compiler_params=_params())(r, lw, k, v, a, b, states, dy)


def _sgu_ln(z, SW, lng, lnb):
    ge = _gelu(z)
    u, vv = ge[:, :SW], ge[:, SW:]
    xc = vv - _mean(vv)
    rstd = lax.rsqrt(_mean(xc * xc) + LN_EPS)
    vn = xc * rstd
    return u, vn, rstd, vn * lng + lnb


def _causal(ws_ref, g):
    ti = lax.broadcasted_iota(jnp.int32, (SGU_CHUNK, SGU_CHUNK), 0)
    si = lax.broadcasted_iota(jnp.int32, (SGU_CHUNK, SGU_CHUNK), 1)
    return ti >= si, jnp.where(ti >= si, ws_ref[g], 0.0).astype(BF16)


def _sgu_fwd(proj, zblock, lng, lnb, ws, bexp, name):
    T, SW = proj.shape[0], lng.shape[1]
    G = ws.shape[0]
    tr = min(256, T)
    nch = tr // SGU_CHUNK

    def body(z_ref, lng_ref, lnb_ref, ws_ref, be_ref, o_ref):
        u, _, _, vl = _sgu_ln(z_ref[...], SW, lng_ref[...], lnb_ref[...])
        for g in range(G):
            cs = slice(g * SGU_GROUP, (g + 1) * SGU_GROUP)
            _, wc = _causal(ws_ref, g)
            for n in range(nch):
                rs = slice(n * SGU_CHUNK, (n + 1) * SGU_CHUNK)
                m = jnp.dot(wc, vl[rs, cs].astype(BF16), preferred_element_type=F32) + be_ref[:, cs]
                o_ref[rs, cs] = (u[rs, cs] * m).astype(BF16)

    whole = lambda arr: pl.BlockSpec(arr.shape, lambda i, nd=arr.ndim: (0,) * nd)
    return pl.pallas_call(
        body, name=name, grid=(T // tr,),
        in_specs=[pl.BlockSpec((tr, 2 * SW), lambda i: (i, zblock)), whole(lng), whole(lnb), whole(ws), whole(bexp)],
        out_specs=pl.BlockSpec((tr, SW), lambda i: (i, 0)), out_shape=jax.ShapeDtypeStruct((T, SW), BF16),
        compiler_params=_params())(proj, lng, lnb, ws, bexp)


def _sgu_bwd(proj, zblock, dyb, lng, lnb, ws, bexp, dproj, name):
    T, SW = proj.shape[0], lng.shape[1]
    G = ws.shape[0]
    tr = min(256, T)
    nch = tr // SGU_CHUNK
    nt = T // tr

    def body(z_ref, dy_ref, lng_ref, lnb_ref, ws_ref, be_ref, buf_ref, dz_ref, dlg_ref, dlb_ref, dws_ref, db_ref, du_s, dvl_s, dbacc_s):
        i = pl.program_id(0)
        zv = z_ref[...]
        lng_v = lng_ref[...]
        u, vn, rstd, vl = _sgu_ln(zv, SW, lng_v, lnb_ref[...])

        @pl.when(i == 0)
        def _():
            for s in (dlg_ref, dlb_ref, dws_ref, dbacc_s):
                s[...] = jnp.zeros_like(s)

        for g in range(G):
            cs = slice(g * SGU_GROUP, (g + 1) * SGU_GROUP)
            tri, wc = _causal(ws_ref, g)
            for n in range(nch):
                rs = slice(n * SGU_CHUNK, (n + 1) * SGU_CHUNK)
                blk = vl[rs, cs].astype(BF16)
                m = jnp.dot(wc, blk, preferred_element_type=F32) + be_ref[:, cs]
                dyv = dy_ref[rs, cs]
                du_s[rs, cs] = dyv * m
                dm = dyv * u[rs, cs]
                dvl_s[rs, cs] = _bdot(wc, dm, "tn")
                dws_ref[g] += jnp.where(tri, _bdot(dm, blk, "nt"), 0.0)
                dbacc_s[:, cs] += dm

        dvl = dvl_s[...]
        dlg_ref[...] += _colsum(dvl * vn)
        dlb_ref[...] += _colsum(dvl)
        dvn = dvl * lng_v
        dvv = rstd * (dvn - _mean(dvn) - vn * _mean(dvn * vn))
        gp = _gelu_grad(zv)
        dz_ref[:, :SW] = (du_s[...] * gp[:, :SW]).astype(BF16)
        dz_ref[:, SW:] = (dvv * gp[:, SW:]).astype(BF16)

        @pl.when(i == nt - 1)
        def _():
            lane = lax.broadcasted_iota(jnp.int32, (SGU_CHUNK, LANE), 1)
            out = jnp.zeros((SGU_CHUNK, LANE), F32)
            for g in range(G):
                col = jnp.sum(dbacc_s[:, g * SGU_GROUP:(g + 1) * SGU_GROUP], axis=1, keepdims=True)
                out = jnp.where(lane == g, col, out)
            db_ref[...] = out

    whole = lambda arr: pl.BlockSpec(arr.shape, lambda i, nd=arr.ndim: (0,) * nd)
    acc_shapes = [(1, SW), (1, SW), ws.shape, (SGU_CHUNK, LANE)]
    return pl.pallas_call(
        body, name=name, grid=(nt,),
        in_specs=[pl.BlockSpec((tr, 2 * SW), lambda i: (i, zblock)), pl.BlockSpec((tr, SW), lambda i: (i, 0)),
                  whole(lng), whole(lnb), whole(ws), whole(bexp), pl.BlockSpec(memory_space=pl.ANY)],
        out_specs=([pl.BlockSpec((tr, 2 * SW), lambda i: (i, zblock))]
                   + [pl.BlockSpec(s, lambda i, nd=len(s): (0,) * nd) for s in acc_shapes]),
        out_shape=[jax.ShapeDtypeStruct(dproj.shape, BF16)] + [jax.ShapeDtypeStruct(s, F32) for s in acc_shapes],
        scratch_shapes=[pltpu.VMEM((tr, SW), F32), pltpu.VMEM((tr, SW), F32), pltpu.VMEM((SGU_CHUNK, SW), F32)],
        input_output_aliases={6: 0}, compiler_params=_params())(proj, dyb, lng, lnb, ws, bexp, dproj)


_HBM = pl.BlockSpec(memory_space=pltpu.HBM)
_SEM = pl.BlockSpec(memory_space=pltpu.SEMAPHORE)
_DATAFLOW = pltpu.SideEffectType.DATAFLOW_SIDE_EFFECTING


def _mesh_place():
    x, y, c = lax.axis_index("x"), lax.axis_index("y"), lax.axis_index("c")
    return x, y, c, 4 * x + 2 * y + c


def _peer(x, y, c, rel):
    px = 1 - x if rel & 4 else x
    py = 1 - y if rel & 2 else y
    pc = 1 - c if rel & 1 else c
    return (px, py, pc), 4 * px + 2 * py + pc


ALL_PEERS = tuple(range(1, N_DEV))
SIBLING = (1,)
SAME_CORE = (2, 4, 6)
SIBLINGS_CORE = (3, 5, 7)


def _exchange_start(groups, name, rels=ALL_PEERS):
    flat = [t for g in groups for t in g]
    sizes = [len(g) for g in groups]
    n, ng = len(flat), len(groups)
    srcs = [pltpu.with_memory_space_constraint(a, pltpu.HBM) for a, _ in flat]
    lands = [pltpu.with_memory_space_constraint(lax.empty(((N_DEV,) + a.shape) if isg else a.shape, a.dtype), pltpu.HBM)
             for a, isg in flat]

    def body(*refs):
        ins, lnd, sems, token = refs[:n], refs[n:2 * n], refs[2 * n:2 * n + 3 * ng], refs[-1]
        x, y, c, me = _mesh_place()
        j0 = 0
        for gi, sz in enumerate(sizes):
            for rel in rels:
                dev, slot = _peer(x, y, c, rel)
                for jj in range(sz):
                    j = j0 + jj
                    pltpu.make_async_remote_copy(
                        src_ref=ins[j] if flat[j][1] else ins[j].at[slot], dst_ref=lnd[j].at[me],
                        send_sem=sems[3 * gi].at[jj * (N_DEV - 1) + rel - 1], recv_sem=sems[3 * gi + 1].at[jj * (N_DEV - 1) + rel - 1],
                        device_id=dev, device_id_type=pl.DeviceIdType.MESH).start()
            for jj in range(sz):
                j = j0 + jj
                pltpu.make_async_copy(ins[j] if flat[j][1] else ins[j].at[me], lnd[j].at[me], sems[3 * gi + 2].at[jj]).start()
            j0 += sz
        token[...] = jnp.zeros_like(token)

    sem_shapes = [pltpu.SemaphoreType.DMA((k,)) for sz in sizes for k in (sz * (N_DEV - 1), sz * (N_DEV - 1), sz)]
    res = pl.pallas_call(
        body, name=name,
        out_shape=(*sem_shapes, *[pltpu.HBM(a.shape, a.dtype) for a in srcs], *[pltpu.HBM(a.shape, a.dtype) for a in lands],
                   jax.ShapeDtypeStruct((SUBLANE, LANE), F32)),
        in_specs=[_HBM] * (2 * n), out_specs=(*[_SEM] * (3 * ng), *[_HBM] * (2 * n), pl.BlockSpec(memory_space=pltpu.VMEM)),
        input_output_aliases={i: 3 * ng + i for i in range(2 * n)},
        compiler_params=pltpu.CompilerParams(has_side_effects=_DATAFLOW))(*srcs, *lands)
    sems, thru, token = res[:3 * ng], res[3 * ng:3 * ng + 2 * n], res[-1]
    handle, j0 = [], 0
    for gi, sz in enumerate(sizes):
        handle.append(dict(kinds=[k for _, k in groups[gi]], srcs=list(thru[j0:j0 + sz]), lands=list(thru[n + j0:n + j0 + sz]),
                           sems=list(sems[3 * gi:3 * gi + 3])))
        j0 += sz
    return handle, token


def _exchange_wait(group, after, name, rels=ALL_PEERS, local=True):
    kinds, sz = group["kinds"], len(group["kinds"])
    relay = group.get("relay", [])

    def body(*refs):
        ins, lnd, (ssem, rsem, lsem) = refs[:sz], refs[sz:2 * sz], refs[2 * sz:2 * sz + 3]
        x, y, c, me = _mesh_place()
        for rel in rels:
            dev, slot = _peer(x, y, c, rel)
            for jj in range(sz):
                cp = pltpu.make_async_remote_copy(
                    src_ref=ins[jj] if kinds[jj] else ins[jj].at[slot], dst_ref=lnd[jj].at[slot],
                    send_sem=ssem.at[jj * (N_DEV - 1) + rel - 1], recv_sem=rsem.at[jj * (N_DEV - 1) + rel - 1],
                    device_id=dev, device_id_type=pl.DeviceIdType.MESH)
                cp.wait_send()
                cp.wait_recv()
        if local:
            for jj in range(sz):
                pltpu.make_async_copy(ins[jj] if kinds[jj] else ins[jj].at[me], lnd[jj].at[me], lsem.at[jj]).wait()
        if relay:
            fsend, frecv = refs[2 * sz + 3:2 * sz + 5]
            dev = _peer(x, y, c, 1)[0]
            for q, (mine, theirs) in enumerate(zip(SAME_CORE, SIBLINGS_CORE)):
                for jj in range(sz):
                    cp = pltpu.make_async_remote_copy(
                        src_ref=lnd[jj].at[_peer(x, y, c, mine)[1]], dst_ref=lnd[jj].at[_peer(x, y, c, theirs)[1]],
                        send_sem=fsend.at[jj * len(SAME_CORE) + q], recv_sem=frecv.at[jj * len(SAME_CORE) + q],
                        device_id=dev, device_id_type=pl.DeviceIdType.MESH)
                    cp.wait_send()
                    cp.wait_recv()

    arrays = group["srcs"] + group["lands"]
    sems = group["sems"] + relay
    res = pl.pallas_call(
        body, name=name, out_shape=[pltpu.HBM(a.shape, a.dtype) for a in arrays],
        in_specs=[_HBM] * (2 * sz) + [_SEM] * len(sems) + [pl.BlockSpec(memory_space=pl.ANY)], out_specs=[_HBM] * (2 * sz),
        input_output_aliases={i: i for i in range(2 * sz)},
        compiler_params=pltpu.CompilerParams(has_side_effects=_DATAFLOW))(*arrays, *sems, after)
    return dict(group, srcs=list(res[:sz]), lands=list(res[sz:]), relay=[])


def _relay_start(group, name):
    sz = len(group["kinds"])
    nq = len(SAME_CORE)

    def body(*refs):
        lnd, fsend, frecv, token = refs[:sz], refs[sz], refs[sz + 1], refs[-1]
        x, y, c, _ = _mesh_place()
        dev = _peer(x, y, c, 1)[0]
        for q, rel in enumerate(SAME_CORE):
            slot = _peer(x, y, c, rel)[1]
            for jj in range(sz):
                pltpu.make_async_remote_copy(
                    src_ref=lnd[jj].at[slot], dst_ref=lnd[jj].at[slot], send_sem=fsend.at[jj * nq + q], recv_sem=frecv.at[jj * nq + q],
                    device_id=dev, device_id_type=pl.DeviceIdType.MESH).start()
        token[...] = jnp.zeros_like(token)

    lands = group["lands"]
    res = pl.pallas_call(
        body, name=name,
        out_shape=(pltpu.SemaphoreType.DMA((sz * nq,)), pltpu.SemaphoreType.DMA((sz * nq,)), *[pltpu.HBM(a.shape, a.dtype) for a in lands],
                   jax.ShapeDtypeStruct((SUBLANE, LANE), F32)),
        in_specs=[_HBM] * sz, out_specs=(_SEM, _SEM, *[_HBM] * sz, pl.BlockSpec(memory_space=pltpu.VMEM)),
        input_output_aliases={i: 2 + i for i in range(sz)},
        compiler_params=pltpu.CompilerParams(has_side_effects=_DATAFLOW))(*lands)
    return dict(group, lands=list(res[2:2 + sz]), relay=[res[0], res[1]]), res[-1]


def _adamw(w, m, v, gparts, name):
    R, C = w.shape
    tm = _pick(R, (256, 128, 64, 32, 16, 8))

    def body(w_ref, m_ref, v_ref, g_ref, go, do, mo, vo):
        g = g_ref[0].astype(F32)
        for j in range(1, N_DEV):
            g = g + g_ref[j].astype(F32)
        mn = ADAM_B1 * m_ref[...] + (1.0 - ADAM_B1) * g
        vn = ADAM_B2 * v_ref[...] + (1.0 - ADAM_B2) * (g * g)
        m_hat = mn / (1.0 - ADAM_B1 ** ADAM_STEP)
        v_hat = vn / (1.0 - ADAM_B2 ** ADAM_STEP)
        go[...] = g
        do[...] = -ADAM_LR * (m_hat / (jnp.sqrt(v_hat) + ADAM_EPS) + ADAM_WD * w_ref[...])
        mo[...] = mn
        vo[...] = vn

    row = pl.BlockSpec((tm, C), lambda i: (i, 0))
    return pl.pallas_call(
        body, name=name, grid=(R // tm,), in_specs=[row, row, row, pl.BlockSpec((N_DEV, tm, C), lambda i: (0, i, 0))],
        out_specs=[row] * 4, out_shape=[jax.ShapeDtypeStruct((R, C), F32)] * 4, compiler_params=_params())(w, m, v, gparts)


def _pack(arrays):
    flat = []
    for a in arrays:
        f = a.reshape(-1)
        pad = _ceil_to(f.shape[0], LANE) - f.shape[0]
        flat.append(jnp.concatenate([f, jnp.zeros((pad,), f.dtype)]) if pad else f)
    buf = jnp.concatenate(flat)
    rows = _ceil_to(buf.shape[0] // LANE, 64)
    buf = jnp.concatenate([buf, jnp.zeros((rows * LANE - buf.shape[0],), buf.dtype)])
    return buf.reshape(rows, LANE)


def _unpack(buf, shapes):
    flat, out, off = buf.reshape(-1), [], 0
    for s in shapes:
        size = 1
        for d in s:
            size *= d
        out.append(flat[off:off + size].reshape(s))
        off += _ceil_to(size, LANE)
    return out


def kernel(x, norm_mix_g, w_in, shift_mu, w0, w_lora_up, a0, a_lora_up, g_lora_up, k_k, k_a, r_k, lnx_g, lnx_b, w_proj_rwkv, sgu_ln_g, sgu_ln_b, sgu_w, sgu_b, w_proj_sgu, w_out, norm_ffn_g, w_ffn_gate, w_ffn_up, w_ffn_down, norm_final_g, loss_target, m_norm_mix_g, m_w_in, m_shift_mu, m_w0, m_w_lora_up, m_a0, m_a_lora_up, m_g_lora_up, m_k_k, m_k_a, m_r_k, m_lnx_g, m_lnx_b, m_w_proj_rwkv, m_sgu_ln_g, m_sgu_ln_b, m_sgu_w, m_sgu_b, m_w_proj_sgu, m_w_out, m_norm_ffn_g, m_w_ffn_gate, m_w_ffn_up, m_w_ffn_down, m_norm_final_g, v_norm_mix_g, v_w_in, v_shift_mu, v_w0, v_w_lora_up, v_a0, v_a_lora_up, v_g_lora_up, v_k_k, v_k_a, v_r_k, v_lnx_g, v_lnx_b, v_w_proj_rwkv, v_sgu_ln_g, v_sgu_ln_b, v_sgu_w, v_sgu_b, v_w_proj_sgu, v_w_out, v_norm_ffn_g, v_w_ffn_gate, v_w_ffn_up, v_w_ffn_down, v_norm_final_g):
    weights = dict(norm_mix_g=norm_mix_g, w_in=w_in, shift_mu=shift_mu, w0=w0, w_lora_up=w_lora_up, a0=a0, a_lora_up=a_lora_up,
                   g_lora_up=g_lora_up, k_k=k_k, k_a=k_a, r_k=r_k, lnx_g=lnx_g, lnx_b=lnx_b, w_proj_rwkv=w_proj_rwkv,
                   sgu_ln_g=sgu_ln_g, sgu_ln_b=sgu_ln_b, sgu_w=sgu_w, sgu_b=sgu_b, w_proj_sgu=w_proj_sgu, w_out=w_out,
                   norm_ffn_g=norm_ffn_g, w_ffn_gate=w_ffn_gate, w_ffn_up=w_ffn_up, w_ffn_down=w_ffn_down, norm_final_g=norm_final_g)
    m_in = dict(norm_mix_g=m_norm_mix_g, w_in=m_w_in, shift_mu=m_shift_mu, w0=m_w0, w_lora_up=m_w_lora_up, a0=m_a0,
                a_lora_up=m_a_lora_up, g_lora_up=m_g_lora_up, k_k=m_k_k, k_a=m_k_a, r_k=m_r_k, lnx_g=m_lnx_g, lnx_b=m_lnx_b,
                w_proj_rwkv=m_w_proj_rwkv, sgu_ln_g=m_sgu_ln_g, sgu_ln_b=m_sgu_ln_b, sgu_w=m_sgu_w, sgu_b=m_sgu_b,
                w_proj_sgu=m_w_proj_sgu, w_out=m_w_out, norm_ffn_g=m_norm_ffn_g, w_ffn_gate=m_w_ffn_gate, w_ffn_up=m_w_ffn_up,
                w_ffn_down=m_w_ffn_down, norm_final_g=m_norm_final_g)
    v_in = dict(norm_mix_g=v_norm_mix_g, w_in=v_w_in, shift_mu=v_shift_mu, w0=v_w0, w_lora_up=v_w_lora_up, a0=v_a0,
                a_lora_up=v_a_lora_up, g_lora_up=v_g_lora_up, k_k=v_k_k, k_a=v_k_a, r_k=v_r_k, lnx_g=v_lnx_g, lnx_b=v_lnx_b,
                w_proj_rwkv=v_w_proj_rwkv, sgu_ln_g=v_sgu_ln_g, sgu_ln_b=v_sgu_ln_b, sgu_w=v_sgu_w, sgu_b=v_sgu_b,
                w_proj_sgu=v_w_proj_sgu, w_out=v_w_out, norm_ffn_g=v_norm_ffn_g, w_ffn_gate=v_w_ffn_gate, w_ffn_up=v_w_ffn_up,
                w_ffn_down=v_w_ffn_down, norm_final_g=v_norm_final_g)
    names = list(weights)
    col_sharded = ("w_in", "w_lora_up", "a_lora_up", "g_lora_up", "w_proj_rwkv", "w_proj_sgu", "w_ffn_gate", "w_ffn_up")
    row_sharded = ("w_out", "w_ffn_down")
    sharded = [n for n in names if n in col_sharded or n in row_sharded]
    small = [n for n in names if n not in sharded]

    xs, tgt = x[0], loss_target[0]
    T, D = xs.shape
    RW = w0.shape[1]
    H = RW // HEAD
    SW = sgu_ln_g.shape[1]
    G = sgu_w.shape[1]
    assert 2 * SW == D, "the projection layout takes the SGU part to be as wide as a gate"
    lay = _rwkv_layout(RW, w_lora_up.shape[1], a_lora_up.shape[1], g_lora_up.shape[1], D)
    _, pw, _, rcp = lay
    icp = rcp + 3 * D
    b_ga, b_gb, b_z = rcp // D, rcp // D + 1, rcp // D + 2

    gather_groups = [["w_in", "w_lora_up", "a_lora_up", "g_lora_up"], ["w_proj_rwkv", "w_proj_sgu", "w_out"],
                     ["w_ffn_gate", "w_ffn_up", "w_ffn_down"]]
    gather, gather_token = _exchange_start([[(weights[n][0].astype(BF16), True) for n in grp] for grp in gather_groups],
                                           "gather_start", rels=SIBLING + SAME_CORE)
    full = {}
    relay_tokens = {}
    joined = lambda g: g.transpose(1, 0, 2).reshape(g.shape[1], -1)

    def relay_weights(gi, after, name):
        arrived = _exchange_wait(gather[gi], after, "gather_wait_ici_" + name, rels=SAME_CORE, local=False)
        gather[gi], relay_tokens[gi] = _relay_start(arrived, "gather_relay_" + name)

    def take_weights(gi, after, name):
        done = _exchange_wait(gather[gi], after, "gather_wait_d2d_" + name, rels=SIBLING)
        for n, g in zip(gather_groups[gi], done["lands"]):
            full[n] = g.reshape(-1, g.shape[2]) if n in row_sharded else g

    n1 = _rms_fwd(xs, norm_mix_g, "rms_mix", deps=[gather_token])
    relay_weights(0, n1, "in")
    take_weights(0, relay_tokens[0], "in")
    W_in = _to_proj_layout(joined(full["w_in"]), lay, D)
    lora = [_pad_rows(joined(full[n]), rows) for n, rows in zip(("w_lora_up", "a_lora_up", "g_lora_up"), pw[3:])]
    mu_p = _to_proj_layout(shift_mu, lay, D)
    rsmall = [w0, a0, k_k, k_a]
    hp = [lnx_g.reshape(H, 1, HEAD), lnx_b.reshape(H, 1, HEAD), r_k.reshape(H, 1, HEAD)]
    ws = sgu_w[0]
    bexp = jnp.repeat(sgu_b[0].T, SGU_GROUP, axis=1)
    gf = norm_final_g.reshape(1, D)

    proj = _matmul(n1, W_in, mode="nn", out_dtype=F32, name="proj_in")
    ga, gb = (proj, D, b_ga), (proj, D, b_gb)
    r_h, lw_h, k2_h, v_h, aa_h, bb_h, g_h = _rwkv_pre(proj, mu_p, rsmall, lora, lay, "rwkv_pre")
    wkv_in = [r_h, lw_h, k2_h, v_h, aa_h, bb_h]
    y_h, states = _wkv_fwd(*wkv_in, "wkv_fwd")
    relay_weights(1, y_h, "proj")
    relay_weights(2, relay_tokens[1], "ffn")
    ya = _head_post(y_h, r_h, k2_h, v_h, g_h, hp, "head_post")
    yb = _sgu_fwd(proj, b_z, sgu_ln_g, sgu_ln_b, ws, bexp, "sgu_fwd")
    take_weights(1, ya, "proj")
    pa = _matmul(ya, full["w_proj_rwkv"], mode="nn", out_dtype=F32, name="proj_a")
    pb = _matmul(yb, full["w_proj_sgu"], mode="nn", out_dtype=F32, name="proj_b")

    def merge_fn(rv, pv):
        ga_v, gb_v, pa_v, pb_v = rv
        return [_sigmoid(ga_v) * pa_v + _sigmoid(gb_v) * pb_v], []
    merged = _rowwise(merge_fn, [ga, gb, pa, pb], [], [(D, BF16)], [], name="merge")[0]
    h1 = _matmul(merged, full["w_out"], mode="nn", out_dtype=F32, name="out_proj", add=xs)
    n2 = _rms_fwd(h1, norm_ffn_g, "rms_ffn")
    take_weights(2, n2, "ffn")
    gt = _matmul(n2, full["w_ffn_gate"], mode="nn", out_dtype=F32, name="ffn_gate", out_blocks=N_DEV)
    up = _matmul(n2, full["w_ffn_up"], mode="nn", out_dtype=F32, name="ffn_up", out_blocks=N_DEV)
    fb = gt.shape[2]
    flat = lambda t: t.reshape(N_DEV * T, fb)
    blocked = lambda t: t.reshape(N_DEV, T, fb)

    def act_fn(rv, pv):
        gt_v, up_v = rv
        return [gt_v * _sigmoid(gt_v) * up_v], []
    act = blocked(_rowwise(act_fn, [flat(gt), flat(up)], [], [(fb, BF16)], [], name="ffn_act", tm=1024)[0])
    h2 = _matmul(act, full["w_ffn_down"], mode="nn", out_dtype=F32, name="ffn_down", add=h1)

    def final_fn(rv, pv):
        (h_v, t_v), (g_v,) = rv, pv
        r = lax.rsqrt(_mean(h_v * h_v) + RMS_EPS)
        yn = h_v * r
        e = yn * g_v - t_v
        loss = 0.5 * jnp.sum(_mean(e * e))
        dout = e * (1.0 / D)
        dyg = dout * g_v
        dh = r * (dyg - yn * _mean(dyg * yn))
        return [dh, dh], [jnp.full((1, LANE), loss, F32), _colsum(dout * yn)]
    dh2, dh2_bf, loss_part, d_gf = _rowwise(final_fn, [h2, tgt], [gf], [(D, F32), (D, BF16)], [(1, LANE), (1, D)], name="final_loss")

    grads = {}

    def start_scatter(group, name, extra=()):
        blocks = [(grads[n].reshape(N_DEV, -1, grads[n].shape[1]) if n in row_sharded else grads[n], False) for n in group]
        (handle,), token = _exchange_start([blocks + list(extra)], name)
        return handle, token

    dact = _matmul(dh2_bf, full["w_ffn_down"], mode="nt", out_dtype=F32, name="d_act", out_blocks=N_DEV)
    grads["w_ffn_down"] = _matmul(act, dh2_bf, mode="tn", out_dtype=BF16, name="dw_ffn_down")

    def dact_fn(rv, pv):
        d_v, gt_v, up_v = rv
        s = _sigmoid(gt_v)
        return [d_v * up_v * (s * (1.0 + gt_v * (1.0 - s))), d_v * gt_v * s], []
    dgt, dup = (blocked(t) for t in _rowwise(dact_fn, [flat(dact), flat(gt), flat(up)], [], [(fb, BF16)] * 2, [],
                                             name="d_ffn_act", tm=1024))
    dn2 = _matmul(dgt, full["w_ffn_gate"], mode="nt", out_dtype=F32, name="dn2_gate")
    dn2 = _matmul(dup, full["w_ffn_up"], mode="nt", out_dtype=F32, name="dn2_up", add=dn2)
    grads["w_ffn_gate"] = _matmul(n2, dgt, mode="tn", out_dtype=BF16, name="dw_ffn_gate", out_blocks=N_DEV)
    grads["w_ffn_up"] = _matmul(n2, dup, mode="tn", out_dtype=BF16, name="dw_ffn_up", out_blocks=N_DEV)
    scatter_groups = [["w_ffn_down", "w_ffn_gate", "w_ffn_up"], ["w_out", "w_proj_rwkv", "w_proj_sgu"],
                      ["w_in", "w_lora_up", "a_lora_up", "g_lora_up"]]
    scatter_ffn, token_ffn = start_scatter(scatter_groups[0], "scatter_start_ffn")
    dh1, dh1_bf, d_g2 = _rms_bwd(dn2, h1, dh2, norm_ffn_g, "rms_ffn_bwd", deps=[token_ffn])
    dmerged = _matmul(dh1_bf, full["w_out"], mode="nt", out_dtype=F32, name="d_merged")
    grads["w_out"] = _matmul(merged, dh1_bf, mode="tn", out_dtype=BF16, name="dw_out")

    def dmerge_fn(rv, pv):
        d_v, ga_v, gb_v, pa_v, pb_v = rv
        sa, sb = _sigmoid(ga_v), _sigmoid(gb_v)
        dgates = jnp.concatenate([d_v * pa_v * sa * (1.0 - sa), d_v * pb_v * sb * (1.0 - sb)], axis=1)
        return [dgates, d_v * sa, d_v * sb], []
    dproj, dpa, dpb = _rowwise(dmerge_fn, [dmerged, ga, gb, pa, pb], [],
                               [(2 * D, BF16, icp, b_ga // 2, None), (D, BF16), (D, BF16)], [], name="d_merge")
    dya = _matmul(dpa, full["w_proj_rwkv"], mode="nt", out_dtype=F32, name="d_ya")
    dyb = _matmul(dpb, full["w_proj_sgu"], mode="nt", out_dtype=F32, name="d_yb")
    grads["w_proj_rwkv"] = _matmul(ya, dpa, mode="tn", out_dtype=BF16, name="dw_proj_a", out_blocks=N_DEV)
    grads["w_proj_sgu"] = _matmul(yb, dpb, mode="tn", out_dtype=BF16, name="dw_proj_b", out_blocks=N_DEV)
    scatter_mid, token_mid = start_scatter(scatter_groups[1], "scatter_start_mid")
    dproj, d_lng, d_lnb, d_ws, d_bs = _sgu_bwd(proj, b_z, dyb, sgu_ln_g, sgu_ln_b, ws, bexp, dproj, "sgu_bwd")

    dy_h, dr1, dk1, dv1, dg_h, d_lnxg, d_lnxb, d_rk = _head_post_bwd(dya, y_h, r_h, k2_h, v_h, g_h, hp, "head_post_bwd",
                                                                     deps=[token_mid])
    dr2, dlw_h, dk2b, dv2, daa, dbb = _wkv_bwd(*wkv_in, states, dy_h, "wkv_bwd")
    dps, d_mu, d_w0, d_a0, d_kk, d_ka, d_wlw, d_wla, d_wlg = _rwkv_pre_bwd(
        proj, mu_p, rsmall, lora, [dr1, dr2, dk1, dk2b, dv1, dv2, dlw_h, daa, dbb, dg_h], lay, "rwkv_pre_bwd")
    dproj = _shift_bwd(dps, mu_p, dproj, "shift_bwd")
    split = lambda g: g.reshape(g.shape[0], N_DEV, -1).transpose(1, 0, 2)
    grads["w_in"] = split(_from_proj_layout(_matmul(n1, dproj, mode="tn", out_dtype=BF16, name="dw_in"), lay, D))
    grads["w_lora_up"] = split(d_wlw[:w_lora_up.shape[1]].astype(BF16))
    grads["a_lora_up"] = split(d_wla[:a_lora_up.shape[1]].astype(BF16))
    grads["g_lora_up"] = split(d_wlg[:g_lora_up.shape[1]].astype(BF16))
    scatter_in, token_in = start_scatter(scatter_groups[2], "scatter_start_in")
    dn1 = _matmul(dproj, W_in, mode="nt", out_dtype=F32, name="dn1", deps=[token_in])
    dx, _, d_g1 = _rms_bwd(dn1, xs, dh1, norm_mix_g, "rms_mix_bwd")
    small_grads = dict(norm_mix_g=d_g1, shift_mu=_from_proj_layout(d_mu, lay, D), w0=d_w0, a0=d_a0, k_k=d_kk, k_a=d_ka, r_k=d_rk,
                       lnx_g=d_lnxg, lnx_b=d_lnxb, sgu_ln_g=d_lng, sgu_ln_b=d_lnb, sgu_w=d_ws, sgu_b=d_bs[:, :G].T,
                       norm_ffn_g=d_g2, norm_final_g=d_gf)

    (gather_small,), after = _exchange_start([[(_pack([small_grads[n] for n in small]), True)]], "gather_small_start")
    out = {}
    for group, handle, name in zip(scatter_groups, (scatter_ffn, scatter_mid, scatter_in), ("ffn", "mid", "in")):
        parts = _exchange_wait(handle, after, "scatter_wait_" + name)["lands"]
        for n, part in zip(group, parts):
            shp = weights[n].shape
            res = _adamw(weights[n][0], m_in[n][0], v_in[n][0], part, "adamw_" + n)
            out[n] = [t.reshape(shp) for t in res]
            after = res[0]
    packed = [_pack([d[n] for n in small]) for d in (weights, m_in, v_in)]
    small_parts = _exchange_wait(gather_small, after, "gather_small_wait")["lands"][0]
    res = _adamw(*packed, small_parts, "adamw_small")
    unpacked = [_unpack(t, [weights[n].shape for n in small]) for t in res]
    for i, n in enumerate(small):
        out[n] = [u[i] for u in unpacked]

    loss = lax.psum(loss_part[0, 0], ("x", "y", "c"))
    return (loss, dx[None], *[out[n][0] for n in names], *[out[n][1] for n in names],
            *[out[n][2] for n in names], *[out[n][3] for n in names])
```

```python
import jax
import jax.numpy as jnp
from jax import lax
from jax.experimental import pallas as pl
from jax.experimental.pallas import tpu as pltpu

F32 = jnp.float32
BF16 = jnp.bfloat16

N_DEV = 8
LANE = 128
SUBLANE = 8
HEAD = 64
SGU_CHUNK = 128
SGU_GROUP = 128
WKV_CHUNK = 64
RMS_EPS = 1e-6
LN_EPS = 1e-5
LNX_EPS = 64e-5
ADAM_LR, ADAM_B1, ADAM_B2, ADAM_EPS, ADAM_WD, ADAM_STEP = 0.001, 0.9, 0.999, 1e-08, 0.01, 10
VMEM_LIMIT_BYTES = 48 * 1024 * 1024
_SQRT_HALF = 0.7071067811865476
_INV_SQRT_2PI = 0.3989422804014327


def _pick(n, cands):
    for c in cands:
        if n % c == 0:
            return c
    return n


def _ceil_to(n, m):
    return -(-n // m) * m


def _params():
    return pltpu.CompilerParams(vmem_limit_bytes=VMEM_LIMIT_BYTES)


def _tile(n, cap):
    best = 0
    for d in range(LANE, min(n, cap) + 1, LANE):
        if n % d == 0:
            best = d
    return best or n


def _matmul_tiles(M, N, K, a_bytes, b_bytes, o_bytes, has_add, forced):
    tm = forced.get("m") or _tile(M, 1024)
    tn = forced.get("n") or _tile(N, 1024)
    tk = forced.get("k") or _tile(K, 2048)

    def vmem(tm, tn, tk):
        acc = tm * tn * 4 if tk < K else 0
        return 2 * (tm * tk * a_bytes + tk * tn * b_bytes + tm * tn * (o_bytes + (4 if has_add else 0))) + acc

    while vmem(tm, tn, tk) > (VMEM_LIMIT_BYTES * 3) // 4:
        if "k" not in forced and tk > 512 and _tile(K, tk // 2) < tk:
            tk = _tile(K, tk // 2)
        elif "m" not in forced and _tile(M, tm // 2) < tm:
            tm = _tile(M, tm // 2)
        else:
            break
    return tm, tn, tk


def _matmul(a, b, *, mode, out_dtype, name, add=None, deps=(), out_blocks=0):
    def view(x):
        return (x.shape[1], x.shape[0] * x.shape[2], x.shape[2]) if x.ndim == 3 else (x.shape[0], x.shape[1], 0)

    (ar, ac, aw), (br, bc, bw) = view(a), view(b)
    a_col, b_col = {"nn": ("k", "n"), "nt": ("k", "k"), "tn": ("m", "n")}[mode]
    if mode == "nn":
        M, K, K2, N = ar, ac, br, bc
    elif mode == "nt":
        M, K, N, K2 = ar, ac, br, bc
    else:
        K, M, K2, N = ar, ac, br, bc
    assert K == K2, (a.shape, b.shape, mode)
    forced = {}
    for dim, w in ((a_col, aw), (b_col, bw), ("n", N // out_blocks if out_blocks else 0)):
        if w:
            assert forced.get(dim, w) == w
            forced[dim] = w
    has_add = add is not None
    tm, tn, tk = _matmul_tiles(M, N, K, a.dtype.itemsize, b.dtype.itemsize, jnp.dtype(out_dtype).itemsize, has_add, forced)
    nk = K // tk
    dn = {"nn": (((1,), (0,)), ((), ())), "nt": (((1,), (1,)), ((), ())), "tn": (((0,), (0,)), ((), ()))}[mode]
    pick = {"m": lambda i, j, k: i, "n": lambda i, j, k: j, "k": lambda i, j, k: k}
    size = {"m": tm, "n": tn, "k": tk}

    def spec(blocked, row_dim, col_dim):
        rf, cf = pick[row_dim], pick[col_dim]
        if blocked:
            return pl.BlockSpec((None, size[row_dim], size[col_dim]), lambda i, j, k: (cf(i, j, k), rf(i, j, k), 0))
        return pl.BlockSpec((size[row_dim], size[col_dim]), lambda i, j, k: (rf(i, j, k), cf(i, j, k)))

    a_spec = spec(aw, "k" if mode == "tn" else "m", a_col)
    b_spec = spec(bw, "n" if mode == "nt" else "k", b_col)
    o_spec = spec(out_blocks, "m", "n")
    n_in = 2 + has_add + len(deps)

    def body(*refs):
        a_ref, b_ref = refs[0], refs[1]
        add_ref = refs[2] if has_add else None
        o_ref = refs[n_in]
        part = lax.dot_general(a_ref[...].astype(BF16), b_ref[...].astype(BF16), dn, preferred_element_type=F32)
        if nk == 1:
            if has_add:
                part = part + add_ref[...]
            o_ref[...] = part.astype(out_dtype)
            return
        acc_ref = refs[-1]
        kk = pl.program_id(2)

        @pl.when(kk == 0)
        def _():
            acc_ref[...] = part + add_ref[...] if has_add else part

        @pl.when(kk > 0)
        def _():
            acc_ref[...] += part

        @pl.when(kk == nk - 1)
        def _():
            o_ref[...] = acc_ref[...].astype(out_dtype)

    ins = [a, b] + ([add] if has_add else []) + list(deps)
    in_specs = ([a_spec, b_spec] + ([o_spec] if has_add else [])
                + [pl.BlockSpec(d.shape, lambda i, j, k, nd=d.ndim: (0,) * nd) for d in deps])
    return pl.pallas_call(
        body, name=name, grid=(M // tm, N // tn, nk), in_specs=in_specs, out_specs=o_spec,
        out_shape=jax.ShapeDtypeStruct((out_blocks, M, tn) if out_blocks else (M, N), out_dtype),
        scratch_shapes=[pltpu.VMEM((tm, tn), F32)] if nk > 1 else [],
        compiler_params=_params())(*ins)


def _rowwise(fn, rows, pars, row_outs, acc_outs, *, name, tm=256, deps=()):
    rows = [r if isinstance(r, tuple) else (r, r.shape[1], 0) for r in rows]
    row_outs = [o if len(o) == 5 else (o[0], o[1], o[0], 0, None) for o in row_outs]
    aliased = [(k, o[4]) for k, o in enumerate(row_outs) if o[4] is not None]
    R = rows[0][0].shape[0]
    if max(w for _, w, _ in rows) > 4096:
        tm = tm // 2
    tm = min(tm, R)
    assert R % tm == 0
    nr, npar = len(rows), len(pars)
    nro = len(row_outs)
    n_in = nr + npar + len(deps) + len(aliased)

    def body(*refs):
        rv = [r[...] for r in refs[:nr]]
        pv = [p[...] for p in refs[nr:nr + npar]]
        outs = refs[n_in:]
        ro, ao = fn(rv, pv)
        first = pl.program_id(0) == 0
        for o_ref, val in zip(outs[:nro], ro):
            o_ref[...] = val.astype(o_ref.dtype)

        @pl.when(first)
        def _():
            for o_ref, val in zip(outs[nro:], ao):
                o_ref[...] = val

        @pl.when(jnp.logical_not(first))
        def _():
            for o_ref, val in zip(outs[nro:], ao):
                o_ref[...] += val

    in_specs = ([pl.BlockSpec((tm, w), lambda i, cb=cb: (i, cb)) for _, w, cb in rows]
                + [pl.BlockSpec(p.shape, lambda i, nd=p.ndim: (0,) * nd) for p in list(pars) + list(deps)]
                + [pl.BlockSpec(memory_space=pl.ANY)] * len(aliased))
    out_shape = ([jax.ShapeDtypeStruct((R, full), dt) for _, dt, full, _, _ in row_outs]
                 + [jax.ShapeDtypeStruct(s, F32) for s in acc_outs])
    out_specs = ([pl.BlockSpec((tm, f), lambda i, cb=cb: (i, cb)) for f, _, _, cb, _ in row_outs]
                 + [pl.BlockSpec(s, lambda i, nd=len(s): (0,) * nd) for s in acc_outs])
    res = pl.pallas_call(body, name=name, grid=(R // tm,), in_specs=in_specs, out_specs=out_specs, out_shape=out_shape,
                         input_output_aliases={n_in - len(aliased) + q: k for q, (k, _) in enumerate(aliased)},
                         compiler_params=_params())(*[r for r, _, _ in rows], *pars, *deps, *[buf for _, buf in aliased])
    return list(res)


def _bdot(a, b, mode="nn"):
    dn = {"nn": (((1,), (0,)), ((), ())), "nt": (((1,), (1,)), ((), ())), "tn": (((0,), (0,)), ((), ()))}[mode]
    return lax.dot_general(a.astype(BF16), b.astype(BF16), dn, preferred_element_type=F32)


def _sigmoid(x):
    return jax.nn.sigmoid(x)


def _softplus(x):
    return jnp.maximum(x, 0.0) + jnp.log1p(jnp.exp(-jnp.abs(x)))


def _gelu(z):
    return 0.5 * z * (1.0 + lax.erf(z * _SQRT_HALF))


def _gelu_grad(z):
    return 0.5 * (1.0 + lax.erf(z * _SQRT_HALF)) + z * jnp.exp(-0.5 * z * z) * _INV_SQRT_2PI


def _mean(x):
    return jnp.mean(x, axis=-1, keepdims=True)


def _colsum(x):
    return jnp.sum(x, axis=0, keepdims=True)


def _rms_fwd(x, g, name, deps=()):
    def fn(rv, pv):
        (xv,), (gv,) = rv, pv
        r = lax.rsqrt(_mean(xv * xv) + RMS_EPS)
        return [xv * r * gv], []
    return _rowwise(fn, [x], [g], [(x.shape[1], BF16)], [], name=name, deps=deps)[0]


def _rms_bwd(dn, x, dres, g, name, deps=()):
    def fn(rv, pv):
        (dnv, xv, drv), (gv,) = rv, pv
        r = lax.rsqrt(_mean(xv * xv) + RMS_EPS)
        yn = xv * r
        dyg = dnv * gv
        dx = drv + r * (dyg - yn * _mean(dyg * yn))
        return [dx, dx], [_colsum(dnv * yn)]
    D = x.shape[1]
    return _rowwise(fn, [dn, x, dres], [g], [(D, F32), (D, BF16)], [(1, D)], name=name, deps=deps)


def _rwkv_layout(RW, Lw, La, Lg, D):
    widths = [RW, RW, RW, Lw, La, Lg]
    pw = [_ceil_to(w, LANE) for w in widths]
    pw[5] += _ceil_to(sum(pw), 2 * D) - sum(pw)
    offs = [sum(pw[:i]) for i in range(6)]
    return widths, pw, offs, sum(pw)


def _pad_rwkv_cols(a, lay):
    widths, pw, _, _ = lay
    pieces, src = [], 0
    for w, p in zip(widths, pw):
        pieces.append(a[:, src:src + w])
        if p > w:
            pieces.append(jnp.zeros((a.shape[0], p - w), a.dtype))
        src += w
    return jnp.concatenate(pieces, axis=1)


def _unpad_rwkv_cols(a, lay):
    widths, _, offs, _ = lay
    return jnp.concatenate([a[:, o:o + w] for o, w in zip(offs, widths)], axis=1)


def _proj_pieces(lay, D, cs):
    widths, _, offs, rcp = lay
    rc = sum(widths)
    segs = [(sum(widths[:j]), widths[j], offs[j]) for j in range(6)] + [(rc, D, rcp + 2 * D), (rc + D, D, rcp), (rc + 2 * D, D, rcp + D)]
    pieces = []
    for start, width, dst in segs:
        n = start
        while n < start + width:
            d, off = divmod(n, cs)
            take = min(cs - off, start + width - n)
            pieces.append((d, off, dst + n - start, take))
            n += take
    return pieces


def _w_in_to_proj(g, lay, D, name):
    nb, rows, cs = g.shape
    icp = lay[3] + 3 * D
    pieces = _proj_pieces(lay, D, cs)
    tm = _pick(rows, (256, 128, 64, 32, 16))

    def body(i_ref, o_ref):
        o_ref[...] = jnp.zeros_like(o_ref)
        for d, src, dst, w in pieces:
            o_ref[:, dst:dst + w] = i_ref[d, :, src:src + w]

    return pl.pallas_call(
        body, name=name, grid=(rows // tm,), in_specs=[pl.BlockSpec((nb, tm, cs), lambda i: (0, i, 0))],
        out_specs=pl.BlockSpec((tm, icp), lambda i: (i, 0)), out_shape=jax.ShapeDtypeStruct((rows, icp), g.dtype),
        compiler_params=_params())(g)


def _dw_in_from_proj(a, lay, D, cs, name):
    rows, icp = a.shape
    pieces = _proj_pieces(lay, D, cs)
    tm = _pick(rows, (256, 128, 64, 32, 16))

    def body(i_ref, o_ref):
        for d, src, dst, w in pieces:
            o_ref[d, :, src:src + w] = i_ref[:, dst:dst + w]

    return pl.pallas_call(
        body, name=name, grid=(rows // tm,), in_specs=[pl.BlockSpec((tm, icp), lambda i: (i, 0))],
        out_specs=pl.BlockSpec((N_DEV, tm, cs), lambda i: (0, i, 0)), out_shape=jax.ShapeDtypeStruct((N_DEV, rows, cs), a.dtype),
        compiler_params=_params())(a)


def _pad_rows(a, rows):
    return a if a.shape[0] == rows else jnp.concatenate([a, jnp.zeros((rows - a.shape[0], a.shape[1]), a.dtype)], axis=0)


def _token_shift(p, halo, mu, i):
    tm = p.shape[0]
    hid = lax.broadcasted_iota(jnp.int32, (SUBLANE, 1), 0)
    before = jnp.sum(jnp.where(hid == SUBLANE - 1, halo, 0.0), axis=0, keepdims=True)
    before = jnp.where(i == 0, 0.0, before)
    rid = lax.broadcasted_iota(jnp.int32, (tm, 1), 0)
    prev = jnp.where(rid == 0, before, pltpu.roll(p, 1, 0))
    d = prev - p
    return p + d * mu, d


def _rwkv_math(ps, w0, a0, k_k, k_a, wlw, wla, wlg, lay):
    _, pw, offs, _ = lay
    r, k, v, xw, xa, xg = (ps[:, offs[j]:offs[j] + pw[j]] for j in range(6))
    tw = jnp.tanh(xw)
    ww = w0 + _bdot(tw, wlw)
    lw = -jnp.exp(-_softplus(-ww) - 0.5)
    a = _sigmoid(a0 + _bdot(xa, wla))
    sg = _sigmoid(xg)
    g = _bdot(sg, wlg)
    return dict(r=r, k=k, v=v, xa=xa, tw=tw, ww=ww, lw=lw, a=a, sg=sg, g=g, kkp=k * k_k, k2=k * (1.0 + (a - 1.0) * k_a))


def _halo_specs(T, tm, width, after):
    hb = tm // SUBLANE
    last = T // SUBLANE - 1
    if after:
        return pl.BlockSpec((SUBLANE, width), lambda i: (jnp.minimum((i + 1) * hb, last), 0))
    return pl.BlockSpec((SUBLANE, width), lambda i: (jnp.maximum(i * hb - 1, 0), 0))


def _rowsum(x):
    return jnp.sum(x, axis=-1, keepdims=True)


def _kk_math(kkp):
    nrm = jnp.sqrt(_rowsum(kkp * kkp))
    inv = 1.0 / jnp.maximum(nrm, 1e-12)
    return nrm, inv, kkp * inv


def _rwkv_pre(p, mu, small, lora, lay, name):
    T, rcp = p.shape[0], lay[3]
    H = lay[0][0] // HEAD
    tm = min(128, T)

    def body(p_ref, ph_ref, mu_ref, w0_ref, a0_ref, kk_ref, ka_ref, wlw_ref, wla_ref, wlg_ref, r_o, lw_o, k2_o, v_o, aa_o, bb_o, g_o):
        ps, _ = _token_shift(p_ref[...], ph_ref[...], mu_ref[...], pl.program_id(0))
        q = _rwkv_math(ps, w0_ref[...], a0_ref[...], kk_ref[...], ka_ref[...], wlw_ref[...], wla_ref[...], wlg_ref[...], lay)
        for h in range(H):
            sl = slice(h * HEAD, (h + 1) * HEAD)
            for o_ref, key in ((r_o, "r"), (lw_o, "lw"), (k2_o, "k2"), (v_o, "v"), (g_o, "g")):
                o_ref[h] = q[key][:, sl]
            _, _, kk = _kk_math(q["kkp"][:, sl])
            aa_o[h] = -kk
            bb_o[h] = kk * q["a"][:, sl]

    whole = lambda arr: pl.BlockSpec(arr.shape, lambda i: (0, 0))
    return pl.pallas_call(
        body, name=name, grid=(T // tm,),
        in_specs=([pl.BlockSpec((tm, rcp), lambda i: (i, 0)), _halo_specs(T, tm, rcp, False), whole(mu)]
                  + [whole(s) for s in small] + [whole(w) for w in lora]),
        out_specs=[pl.BlockSpec((H, tm, HEAD), lambda i: (0, i, 0))] * 7, out_shape=[jax.ShapeDtypeStruct((H, T, HEAD), F32)] * 7,
        compiler_params=_params())(p, p, mu, *small, *lora)


def _rwkv_pre_bwd(p, mu, small, lora, hgrads, lay, name):
    T, rcp = p.shape[0], lay[3]
    widths, pw, offs, _ = lay
    RW = widths[0]
    H = RW // HEAD
    tm = min(128, T)

    def body(p_ref, ph_ref, mu_ref, w0_ref, a0_ref, kk_ref, ka_ref, wlw_ref, wla_ref, wlg_ref,
             dr1, dr2, dk1, dk2b, dv1, dv2, dlw_h, daa, dbb, dg_h,
             dps_ref, dmu_ref, dw0_ref, da0_ref, dkk_ref, dka_ref, dwlw_ref, dwla_ref, dwlg_ref,
             s_dr, s_dk2, s_dv, s_dlw, s_dkkp, s_da, s_dg):
        i = pl.program_id(0)
        ps, dprev = _token_shift(p_ref[...], ph_ref[...], mu_ref[...], i)
        k_k, k_a = kk_ref[...], ka_ref[...]
        q = _rwkv_math(ps, w0_ref[...], a0_ref[...], k_k, k_a, wlw_ref[...], wla_ref[...], wlg_ref[...], lay)
        k, a, lw, ww, tw, sg = q["k"], q["a"], q["lw"], q["ww"], q["tw"], q["sg"]
        for h in range(H):
            sl = slice(h * HEAD, (h + 1) * HEAD)
            s_dr[:, sl] = dr1[h] + dr2[h]
            s_dk2[:, sl] = dk1[h] + dk2b[h]
            s_dv[:, sl] = dv1[h] + dv2[h]
            s_dlw[:, sl] = dlw_h[h]
            s_dg[:, sl] = dg_h[h]
            nrm, inv, kk = _kk_math(q["kkp"][:, sl])
            dbb_h = dbb[h]
            dkk = dbb_h * a[:, sl] - daa[h]
            s_dkkp[:, sl] = jnp.where(nrm > 1e-12, inv * (dkk - kk * _rowsum(dkk * kk)), dkk * inv)
            s_da[:, sl] = dbb_h * kk
        dk2, dkkp, dg = s_dk2[...], s_dkkp[...], s_dg[...]
        dk = dk2 * (1.0 + (a - 1.0) * k_a) + dkkp * k_k
        da = s_da[...] + dk2 * k * k_a
        dpa = da * a * (1.0 - a)
        dww = s_dlw[...] * lw * _sigmoid(-ww)
        dxa = _bdot(dpa, wla_ref[...], "nt")
        dxw = _bdot(dww, wlw_ref[...], "nt") * (1.0 - tw * tw)
        dxg = _bdot(dg, wlg_ref[...], "nt") * sg * (1.0 - sg)
        segs = (s_dr[...], dk, s_dv[...], dxw, dxa, dxg)
        sums = [dmu_ref, dw0_ref, da0_ref, dkk_ref, dka_ref, dwlw_ref, dwla_ref, dwlg_ref]

        @pl.when(i == 0)
        def _():
            for s in sums:
                s[...] = jnp.zeros_like(s)

        for j, seg in enumerate(segs):
            sl = slice(offs[j], offs[j] + pw[j])
            dps_ref[:, sl] = seg
            dmu_ref[:, sl] += _colsum(seg * dprev[:, sl])
        dw0_ref[...] += _colsum(dww)
        da0_ref[...] += _colsum(dpa)
        dkk_ref[...] += _colsum(dkkp * k)
        dka_ref[...] += _colsum(dk2 * k * (a - 1.0))
        dwlw_ref[...] += _bdot(tw, dww, "tn")
        dwla_ref[...] += _bdot(q["xa"], dpa, "tn")
        dwlg_ref[...] += _bdot(sg, dg, "tn")

    whole = lambda arr: pl.BlockSpec(arr.shape, lambda i: (0, 0))
    row = lambda w: pl.BlockSpec((tm, w), lambda i: (i, 0))
    acc_shapes = [(1, rcp), (1, RW), (1, RW), (1, RW), (1, RW)] + [w.shape for w in lora]
    return pl.pallas_call(
        body, name=name, grid=(T // tm,),
        in_specs=([row(rcp), _halo_specs(T, tm, rcp, False), whole(mu)] + [whole(s) for s in small] + [whole(w) for w in lora]
                  + [pl.BlockSpec((H, tm, HEAD), lambda i: (0, i, 0))] * 10),
        out_specs=[row(rcp)] + [pl.BlockSpec(s, lambda i: (0, 0)) for s in acc_shapes],
        out_shape=[jax.ShapeDtypeStruct((T, rcp), F32)] + [jax.ShapeDtypeStruct(s, F32) for s in acc_shapes],
        scratch_shapes=[pltpu.VMEM((tm, RW), F32)] * 7, compiler_params=_params())(p, p, mu, *small, *lora, *hgrads)


def _shift_bwd(dps, mu, dproj, name):
    T, rcp = dps.shape
    tm = min(256, T)
    nt = T // tm

    def body(d_ref, dh_ref, mu_ref, buf_ref, o_ref):
        i = pl.program_id(0)
        d = d_ref[...]
        hid = lax.broadcasted_iota(jnp.int32, (SUBLANE, 1), 0)
        after = jnp.sum(jnp.where(hid == 0, dh_ref[...], 0.0), axis=0, keepdims=True)
        after = jnp.where(i == nt - 1, 0.0, after)
        rid = lax.broadcasted_iota(jnp.int32, (tm, 1), 0)
        nxt = jnp.where(rid == tm - 1, after, pltpu.roll(d, tm - 1, 0))
        mu_v = mu_ref[...]
        o_ref[...] = (d * (1.0 - mu_v) + nxt * mu_v).astype(BF16)

    row = pl.BlockSpec((tm, rcp), lambda i: (i, 0))
    return pl.pallas_call(
        body, name=name, grid=(nt,),
        in_specs=[row, _halo_specs(T, tm, rcp, True), pl.BlockSpec(mu.shape, lambda i: (0, 0)), pl.BlockSpec(memory_space=pl.ANY)],
        out_specs=row, out_shape=jax.ShapeDtypeStruct(dproj.shape, BF16), input_output_aliases={3: 0},
        compiler_params=_params())(dps, dps, mu, dproj)


def _head_post_math(y, r, k2, v, lg, lb, rk):
    yc = y - _mean(y)
    rstd = lax.rsqrt(_mean(yc * yc) + LNX_EPS)
    yn = yc * rstd
    s = _rowsum(r * k2 * rk)
    return yn, rstd, yn * lg + lb + s * v, s


def _head_post(y, r, k2, v, g, hp, name):
    H, T, _ = y.shape
    tm = min(128, T)

    def body(y_ref, r_ref, k_ref, v_ref, g_ref, lg_ref, lb_ref, rk_ref, o_ref):
        _, _, t, _ = _head_post_math(y_ref[...], r_ref[...], k_ref[...], v_ref[...], lg_ref[...], lb_ref[...], rk_ref[...])
        out = (t * g_ref[...]).astype(BF16)
        for h in range(H):
            o_ref[:, h * HEAD:(h + 1) * HEAD] = out[h]

    blk = pl.BlockSpec((H, tm, HEAD), lambda i: (0, i, 0))
    par = pl.BlockSpec((H, 1, HEAD), lambda i: (0, 0, 0))
    return pl.pallas_call(
        body, name=name, grid=(T // tm,), in_specs=[blk] * 5 + [par] * 3, out_specs=pl.BlockSpec((tm, H * HEAD), lambda i: (i, 0)),
        out_shape=jax.ShapeDtypeStruct((T, H * HEAD), BF16), compiler_params=_params())(y, r, k2, v, g, *hp)


def _head_post_bwd(dya, y, r, k2, v, g, hp, name, deps=()):
    H, T, _ = y.shape
    tm = min(128, T)
    hsum = lambda t: jnp.sum(t, axis=1, keepdims=True)

    def body(d_ref, y_ref, r_ref, k_ref, v_ref, g_ref, lg_ref, lb_ref, rk_ref, *rest):
        outs, d_s = rest[len(deps):len(deps) + 8], rest[-1]
        for h in range(H):
            d_s[h] = d_ref[:, h * HEAD:(h + 1) * HEAD]
        d_v, r_v, k_v, v_v, lg, rk = d_s[...], r_ref[...], k_ref[...], v_ref[...], lg_ref[...], rk_ref[...]
        yn, rstd, t, s = _head_post_math(y_ref[...], r_v, k_v, v_v, lg, lb_ref[...], rk)
        dyo = d_v * g_ref[...]
        dyn = dyo * lg
        ds = _rowsum(dyo * v_v)
        vals = (rstd * (dyn - _mean(dyn) - yn * _mean(dyn * yn)), ds * k_v * rk, ds * r_v * rk, dyo * s, d_v * t)
        for o_ref, val in zip(outs[:5], vals):
            o_ref[...] = val
        sums = (hsum(dyo * yn), hsum(dyo), hsum(ds * r_v * k_v))
        first = pl.program_id(0) == 0

        @pl.when(first)
        def _():
            for o_ref, val in zip(outs[5:], sums):
                o_ref[...] = val

        @pl.when(jnp.logical_not(first))
        def _():
            for o_ref, val in zip(outs[5:], sums):
                o_ref[...] += val

    blk = pl.BlockSpec((H, tm, HEAD), lambda i: (0, i, 0))
    par = pl.BlockSpec((H, 1, HEAD), lambda i: (0, 0, 0))
    return pl.pallas_call(
        body, name=name, grid=(T // tm,),
        in_specs=([pl.BlockSpec((tm, H * HEAD), lambda i: (i, 0))] + [blk] * 5 + [par] * 3
                  + [pl.BlockSpec(d.shape, lambda i, nd=d.ndim: (0,) * nd) for d in deps]),
        out_specs=[blk] * 5 + [par] * 3,
        out_shape=[jax.ShapeDtypeStruct((H, T, HEAD), F32)] * 5 + [jax.ShapeDtypeStruct((H, 1, HEAD), F32)] * 3,
        scratch_shapes=[pltpu.VMEM((H, tm, HEAD), F32)], compiler_params=_params())(dya, y, r, k2, v, g, *hp, *deps)


def _bmm(x, y, mode):
    dn = {"nn": (((2,), (1,)), ((0,), (0,))), "nt": (((2,), (2,)), ((0,), (0,))), "tn": (((1,), (1,)), ((0,), (0,)))}[mode]
    (xh, xl), (yh, yl) = _split(x), _split(y)
    dot = lambda p, q: lax.dot_general(p, q, dn, preferred_element_type=F32)
    out = dot(xh, yh)
    if yl is not None:
        out = out + dot(xh, yl)
    if xl is not None:
        out = out + dot(xl, yh)
    return out


def _split(x):
    if isinstance(x, tuple):
        return x
    hi = x.astype(BF16)
    return hi, (x - hi.astype(F32)).astype(BF16)


def _exact(x):
    return x.astype(BF16), None


def _wkv_chunk(r, lw, k, v, a, b):
    hb, C, _ = r.shape
    ti = lax.broadcasted_iota(jnp.int32, (C, C), 0)
    si = lax.broadcasted_iota(jnp.int32, (C, C), 1)
    linc, lstr, eye = (ti >= si).astype(F32), (ti > si).astype(F32), (ti == si).astype(F32)
    lincb = _exact(jnp.broadcast_to(linc, (hb, C, C)))
    lstrb = _exact(jnp.broadcast_to(lstr, (hb, C, C)))
    ones = _exact(jnp.ones_like(v))
    lws = _split(lw)
    ci = _bmm(lincb, lws, "nn")
    cC = jnp.sum(lw, axis=1, keepdims=True)
    gi, ge, gn, gr = jnp.exp(ci), jnp.exp(ci - lw), jnp.exp(-ci), jnp.exp(cC - ci)
    q = dict(At=a * ge, Rt=r * gi, Bt=b * gn, Kt=k * gn, Bh=b * gr, Kh=k * gr)
    s = {key: _split(val) for key, val in q.items()}
    s["v"] = _split(v)
    q["A_ab"] = _bmm(s["At"], s["Bt"], "nt") * lstr
    for key, lhs, rhs, mask in (("A_ak", "At", "Kt", lstr), ("A_rb", "Rt", "Bt", linc), ("A_rk", "Rt", "Kt", linc)):
        q[key] = _bmm(s[lhs], s[rhs], "nt") * mask
        s[key] = _split(q[key])
    Tm = eye + q["A_ab"]
    Pw = _split(q["A_ab"])
    n = 1
    while 2 * n < C:
        Pw = _split(_bmm(Pw, Pw, "nn"))
        Tm = Tm + _bmm(Tm, Pw, "nn")
        n *= 2
    s["Tm"] = _split(Tm)
    gC = jnp.exp(_bmm(lws, ones, "tn"))
    q.update(gi=gi, ge=ge, gn=gn, gr=gr, linc=linc, lstr=lstr, lincb=lincb, lstrb=lstrb, gC=gC, ones=ones, s=s)
    return q


def _wkv_fwd(r, lw, k, v, a, b, name):
    H, T, N = r.shape
    C = min(WKV_CHUNK, T)
    nc = T // C
    hb = _pick(H, (16, 8, 4, 2))

    def body(r_ref, lw_ref, k_ref, v_ref, a_ref, b_ref, y_ref, st_ref, h_ref):
        @pl.when(pl.program_id(1) == 0)
        def _():
            h_ref[...] = jnp.zeros_like(h_ref)

        H0 = h_ref[...]
        st_ref[0] = H0
        q = _wkv_chunk(r_ref[...], lw_ref[...], k_ref[...], v_ref[...], a_ref[...], b_ref[...])
        s = q["s"]
        H0s = _split(H0)
        U = _split(_bmm(s["Tm"], _bmm(s["At"], H0s, "nn") + _bmm(s["A_ak"], s["v"], "nn"), "nn"))
        y_ref[...] = _bmm(s["Rt"], H0s, "nn") + _bmm(s["A_rb"], U, "nn") + _bmm(s["A_rk"], s["v"], "nn")
        h_ref[...] = q["gC"] * H0 + _bmm(s["Bh"], U, "tn") + _bmm(s["Kh"], s["v"], "tn")

    blk = pl.BlockSpec((hb, C, N), lambda h, c: (h, c, 0))
    return pl.pallas_call(
        body, name=name, grid=(H // hb, nc), in_specs=[blk] * 6,
        out_specs=[blk, pl.BlockSpec((1, hb, N, N), lambda h, c: (c, h, 0, 0))],
        out_shape=[jax.ShapeDtypeStruct((H, T, N), F32), jax.ShapeDtypeStruct((nc, H, N, N), F32)],
        scratch_shapes=[pltpu.VMEM((hb, N, N), F32)], compiler_params=_params())(r, lw, k, v, a, b)


def _wkv_bwd(r, lw, k, v, a, b, states, dy, name):
    H, T, N = r.shape
    C = min(WKV_CHUNK, T)
    nc = T // C
    hb = _pick(H, (16, 8, 4, 2))

    def body(r_ref, lw_ref, k_ref, v_ref, a_ref, b_ref, st_ref, dy_ref, dr_ref, dlw_ref, dk_ref, dv_ref, da_ref, db_ref, dh_ref):
        @pl.when(pl.program_id(1) == 0)
        def _():
            dh_ref[...] = jnp.zeros_like(dh_ref)

        dHC = dh_ref[...]
        H0 = st_ref[0]
        q = _wkv_chunk(r_ref[...], lw_ref[...], k_ref[...], v_ref[...], a_ref[...], b_ref[...])
        s, gC = q["s"], q["gC"]
        H0s, dHs, dY = _split(H0), _split(dHC), _split(dy_ref[...])
        U = _split(_bmm(s["Tm"], _bmm(s["At"], H0s, "nn") + _bmm(s["A_ak"], s["v"], "nn"), "nn"))
        dU = _bmm(s["A_rb"], dY, "tn") + _bmm(s["Bh"], dHs, "nn")
        dP = _split(_bmm(s["Tm"], dU, "tn"))
        dv_ref[...] = _bmm(s["A_rk"], dY, "tn") + _bmm(s["Kh"], dHs, "nn") + _bmm(s["A_ak"], dP, "tn")
        dh_ref[...] = _bmm(s["Rt"], dY, "tn") + gC * dHC + _bmm(s["At"], dP, "tn")
        dA_rb = _split(_bmm(dY, U, "nt") * q["linc"])
        dA_rk = _split(_bmm(dY, s["v"], "nt") * q["linc"])
        dA_ab = _split(_bmm(dP, U, "nt") * q["lstr"])
        dA_ak = _split(_bmm(dP, s["v"], "nt") * q["lstr"])
        dRt = _bmm(dY, H0s, "nt") + _bmm(dA_rb, s["Bt"], "nn") + _bmm(dA_rk, s["Kt"], "nn")
        dAt = _bmm(dP, H0s, "nt") + _bmm(dA_ab, s["Bt"], "nn") + _bmm(dA_ak, s["Kt"], "nn")
        dBt = _bmm(dA_ab, s["At"], "tn") + _bmm(dA_rb, s["Rt"], "tn")
        dKt = _bmm(dA_ak, s["At"], "tn") + _bmm(dA_rk, s["Rt"], "tn")
        dBh = _bmm(U, dHs, "nt")
        dKh = _bmm(s["v"], dHs, "nt")
        dr_ref[...] = dRt * q["gi"]
        da_ref[...] = dAt * q["ge"]
        db_ref[...] = dBt * q["gn"] + dBh * q["gr"]
        dk_ref[...] = dKt * q["gn"] + dKh * q["gr"]
        tail = dBh * q["Bh"] + dKh * q["Kh"]
        dci = dRt * q["Rt"] - dBt * q["Bt"] - dKt * q["Kt"] - tail
        dcC = jnp.sum(tail, axis=1, keepdims=True) + _bmm(q["ones"], H0 * dHC * gC, "nt")
        dlw_ref[...] = _bmm(q["lincb"], dci, "tn") + _bmm(q["lstrb"], dAt * q["At"], "tn") + dcC

    blk = pl.BlockSpec((hb, C, N), lambda h, c: (h, nc - 1 - c, 0))
    st = pl.BlockSpec((1, hb, N, N), lambda h, c: (nc - 1 - c, h, 0, 0))
    return pl.pallas_call(
        body, name=name, grid=(H // hb, nc), in_specs=[blk] * 6 + [st, blk], out_specs=[blk] * 6,
        out_shape=[jax.ShapeDtypeStruct((H, T, N), F32)] * 6,
        scratch_shapes=[pltpu.VMEM((hb, N, N), F32)], compiler_params=_params())(r, lw, k, v, a, b, states, dy)


def _sgu_ln(z, SW, lng, lnb):
    ge = _gelu(z)
    u, vv = ge[:, :SW], ge[:, SW:]
    xc = vv - _mean(vv)
    rstd = lax.rsqrt(_mean(xc * xc) + LN_EPS)
    vn = xc * rstd
    return u, vn, rstd, vn * lng + lnb


def _causal(ws_ref, g):
    ti = lax.broadcasted_iota(jnp.int32, (SGU_CHUNK, SGU_CHUNK), 0)
    si = lax.broadcasted_iota(jnp.int32, (SGU_CHUNK, SGU_CHUNK), 1)
    return ti >= si, jnp.where(ti >= si, ws_ref[g], 0.0).astype(BF16)


def _sgu_fwd(proj, zblock, lng, lnb, ws, bexp, name):
    T, SW = proj.shape[0], lng.shape[1]
    G = ws.shape[0]
    tr = min(256, T)
    nch = tr // SGU_CHUNK

    def body(z_ref, lng_ref, lnb_ref, ws_ref, be_ref, o_ref):
        u, _, _, vl = _sgu_ln(z_ref[...], SW, lng_ref[...], lnb_ref[...])
        for g in range(G):
            cs = slice(g * SGU_GROUP, (g + 1) * SGU_GROUP)
            _, wc = _causal(ws_ref, g)
            for n in range(nch):
                rs = slice(n * SGU_CHUNK, (n + 1) * SGU_CHUNK)
                m = jnp.dot(wc, vl[rs, cs].astype(BF16), preferred_element_type=F32) + be_ref[:, cs]
                o_ref[rs, cs] = (u[rs, cs] * m).astype(BF16)

    whole = lambda arr: pl.BlockSpec(arr.shape, lambda i, nd=arr.ndim: (0,) * nd)
    return pl.pallas_call(
        body, name=name, grid=(T // tr,),
        in_specs=[pl.BlockSpec((tr, 2 * SW), lambda i: (i, zblock)), whole(lng), whole(lnb), whole(ws), whole(bexp)],
        out_specs=pl.BlockSpec((tr, SW), lambda i: (i, 0)), out_shape=jax.ShapeDtypeStruct((T, SW), BF16),
        compiler_params=_params())(proj, lng, lnb, ws, bexp)


def _sgu_bwd(proj, zblock, dyb, lng, lnb, ws, bexp, dproj, name):
    T, SW = proj.shape[0], lng.shape[1]
    G = ws.shape[0]
    tr = min(256, T)
    nch = tr // SGU_CHUNK
    nt = T // tr

    def body(z_ref, dy_ref, lng_ref, lnb_ref, ws_ref, be_ref, buf_ref, dz_ref, dlg_ref, dlb_ref, dws_ref, db_ref, du_s, dvl_s, dbacc_s):
        i = pl.program_id(0)
        zv = z_ref[...]
        lng_v = lng_ref[...]
        u, vn, rstd, vl = _sgu_ln(zv, SW, lng_v, lnb_ref[...])

        @pl.when(i == 0)
        def _():
            for s in (dlg_ref, dlb_ref, dws_ref, dbacc_s):
                s[...] = jnp.zeros_like(s)

        for g in range(G):
            cs = slice(g * SGU_GROUP, (g + 1) * SGU_GROUP)
            tri, wc = _causal(ws_ref, g)
            for n in range(nch):
                rs = slice(n * SGU_CHUNK, (n + 1) * SGU_CHUNK)
                blk = vl[rs, cs].astype(BF16)
                m = jnp.dot(wc, blk, preferred_element_type=F32) + be_ref[:, cs]
                dyv = dy_ref[rs, cs]
                du_s[rs, cs] = dyv * m
                dm = dyv * u[rs, cs]
                dvl_s[rs, cs] = _bdot(wc, dm, "tn")
                dws_ref[g] += jnp.where(tri, _bdot(dm, blk, "nt"), 0.0)
                dbacc_s[:, cs] += dm

        dvl = dvl_s[...]
        dlg_ref[...] += _colsum(dvl * vn)
        dlb_ref[...] += _colsum(dvl)
        dvn = dvl * lng_v
        dvv = rstd * (dvn - _mean(dvn) - vn * _mean(dvn * vn))
        gp = _gelu_grad(zv)
        dz_ref[:, :SW] = (du_s[...] * gp[:, :SW]).astype(BF16)
        dz_ref[:, SW:] = (dvv * gp[:, SW:]).astype(BF16)

        @pl.when(i == nt - 1)
        def _():
            lane = lax.broadcasted_iota(jnp.int32, (SGU_CHUNK, LANE), 1)
            out = jnp.zeros((SGU_CHUNK, LANE), F32)
            for g in range(G):
                col = jnp.sum(dbacc_s[:, g * SGU_GROUP:(g + 1) * SGU_GROUP], axis=1, keepdims=True)
                out = jnp.where(lane == g, col, out)
            db_ref[...] = out

    whole = lambda arr: pl.BlockSpec(arr.shape, lambda i, nd=arr.ndim: (0,) * nd)
    acc_shapes = [(1, SW), (1, SW), ws.shape, (SGU_CHUNK, LANE)]
    return pl.pallas_call(
        body, name=name, grid=(nt,),
        in_specs=[pl.BlockSpec((tr, 2 * SW), lambda i: (i, zblock)), pl.BlockSpec((tr, SW), lambda i: (i, 0)),
                  whole(lng), whole(lnb), whole(ws), whole(bexp), pl.BlockSpec(memory_space=pl.ANY)],
        out_specs=([pl.BlockSpec((tr, 2 * SW), lambda i: (i, zblock))]
                   + [pl.BlockSpec(s, lambda i, nd=len(s): (0,) * nd) for s in acc_shapes]),
        out_shape=[jax.ShapeDtypeStruct(dproj.shape, BF16)] + [jax.ShapeDtypeStruct(s, F32) for s in acc_shapes],
        scratch_shapes=[pltpu.VMEM((tr, SW), F32), pltpu.VMEM((tr, SW), F32), pltpu.VMEM((SGU_CHUNK, SW), F32)],
        input_output_aliases={6: 0}, compiler_params=_params())(proj, dyb, lng, lnb, ws, bexp, dproj)


_HBM = pl.BlockSpec(memory_space=pltpu.HBM)
_SEM = pl.BlockSpec(memory_space=pltpu.SEMAPHORE)
_DATAFLOW = pltpu.SideEffectType.DATAFLOW_SIDE_EFFECTING


def _mesh_place():
    x, y, c = lax.axis_index("x"), lax.axis_index("y"), lax.axis_index("c")
    return x, y, c, 4 * x + 2 * y + c


def _peer(x, y, c, rel):
    px = 1 - x if rel & 4 else x
    py = 1 - y if rel & 2 else y
    pc = 1 - c if rel & 1 else c
    return (px, py, pc), 4 * px + 2 * py + pc


ALL_PEERS = tuple(range(1, N_DEV))
SIBLING = (1,)
SAME_CORE = (2, 4, 6)
SIBLINGS_CORE = (3, 5, 7)


def _exchange_start(groups, name, rels=ALL_PEERS):
    flat = [t for g in groups for t in g]
    sizes = [len(g) for g in groups]
    n, ng = len(flat), len(groups)
    srcs = [pltpu.with_memory_space_constraint(a, pltpu.HBM) for a, _ in flat]
    lands = [pltpu.with_memory_space_constraint(lax.empty(((N_DEV,) + a.shape) if isg else a.shape, a.dtype), pltpu.HBM)
             for a, isg in flat]

    def body(*refs):
        ins, lnd, sems, token = refs[:n], refs[n:2 * n], refs[2 * n:2 * n + 3 * ng], refs[-1]
        x, y, c, me = _mesh_place()
        j0 = 0
        for gi, sz in enumerate(sizes):
            for rel in rels:
                dev, slot = _peer(x, y, c, rel)
                for jj in range(sz):
                    j = j0 + jj
                    pltpu.make_async_remote_copy(
                        src_ref=ins[j] if flat[j][1] else ins[j].at[slot], dst_ref=lnd[j].at[me],
                        send_sem=sems[3 * gi].at[jj * (N_DEV - 1) + rel - 1], recv_sem=sems[3 * gi + 1].at[jj * (N_DEV - 1) + rel - 1],
                        device_id=dev, device_id_type=pl.DeviceIdType.MESH).start()
            for jj in range(sz):
                j = j0 + jj
                pltpu.make_async_copy(ins[j] if flat[j][1] else ins[j].at[me], lnd[j].at[me], sems[3 * gi + 2].at[jj]).start()
            j0 += sz
        token[...] = jnp.zeros_like(token)

    sem_shapes = [pltpu.SemaphoreType.DMA((k,)) for sz in sizes for k in (sz * (N_DEV - 1), sz * (N_DEV - 1), sz)]
    res = pl.pallas_call(
        body, name=name,
        out_shape=(*sem_shapes, *[pltpu.HBM(a.shape, a.dtype) for a in srcs], *[pltpu.HBM(a.shape, a.dtype) for a in lands],
                   jax.ShapeDtypeStruct((SUBLANE, LANE), F32)),
        in_specs=[_HBM] * (2 * n), out_specs=(*[_SEM] * (3 * ng), *[_HBM] * (2 * n), pl.BlockSpec(memory_space=pltpu.VMEM)),
        input_output_aliases={i: 3 * ng + i for i in range(2 * n)},
        compiler_params=pltpu.CompilerParams(has_side_effects=_DATAFLOW))(*srcs, *lands)
    sems, thru, token = res[:3 * ng], res[3 * ng:3 * ng + 2 * n], res[-1]
    handle, j0 = [], 0
    for gi, sz in enumerate(sizes):
        handle.append(dict(kinds=[k for _, k in groups[gi]], srcs=list(thru[j0:j0 + sz]), lands=list(thru[n + j0:n + j0 + sz]),
                           sems=list(sems[3 * gi:3 * gi + 3])))
        j0 += sz
    return handle, token


def _exchange_wait(group, after, name, rels=ALL_PEERS, local=True):
    kinds, sz = group["kinds"], len(group["kinds"])
    relay = group.get("relay", [])

    def body(*refs):
        ins, lnd, (ssem, rsem, lsem) = refs[:sz], refs[sz:2 * sz], refs[2 * sz:2 * sz + 3]
        x, y, c, me = _mesh_place()
        for rel in rels:
            dev, slot = _peer(x, y, c, rel)
            for jj in range(sz):
                cp = pltpu.make_async_remote_copy(
                    src_ref=ins[jj] if kinds[jj] else ins[jj].at[slot], dst_ref=lnd[jj].at[slot],
                    send_sem=ssem.at[jj * (N_DEV - 1) + rel - 1], recv_sem=rsem.at[jj * (N_DEV - 1) + rel - 1],
                    device_id=dev, device_id_type=pl.DeviceIdType.MESH)
                cp.wait_send()
                cp.wait_recv()
        if local:
            for jj in range(sz):
                pltpu.make_async_copy(ins[jj] if kinds[jj] else ins[jj].at[me], lnd[jj].at[me], lsem.at[jj]).wait()
        if relay:
            fsend, frecv = refs[2 * sz + 3:2 * sz + 5]
            dev = _peer(x, y, c, 1)[0]
            for q, (mine, theirs) in enumerate(zip(SAME_CORE, SIBLINGS_CORE)):
                for jj in range(sz):
                    cp = pltpu.make_async_remote_copy(
                        src_ref=lnd[jj].at[_peer(x, y, c, mine)[1]], dst_ref=lnd[jj].at[_peer(x, y, c, theirs)[1]],
                        send_sem=fsend.at[jj * len(SAME_CORE) + q], recv_sem=frecv.at[jj * len(SAME_CORE) + q],
                        device_id=dev, device_id_type=pl.DeviceIdType.MESH)
                    cp.wait_send()
                    cp.wait_recv()

    arrays = group["srcs"] + group["lands"]
    sems = group["sems"] + relay
    res = pl.pallas_call(
        body, name=name, out_shape=[pltpu.HBM(a.shape, a.dtype) for a in arrays],
        in_specs=[_HBM] * (2 * sz) + [_SEM] * len(sems) + [pl.BlockSpec(memory_space=pl.ANY)], out_specs=[_HBM] * (2 * sz),
        input_output_aliases={i: i for i in range(2 * sz)},
        compiler_params=pltpu.CompilerParams(has_side_effects=_DATAFLOW))(*arrays, *sems, after)
    return dict(group, srcs=list(res[:sz]), lands=list(res[sz:]), relay=[])


def _relay_start(group, name):
    sz = len(group["kinds"])
    nq = len(SAME_CORE)

    def body(*refs):
        lnd, fsend, frecv, token = refs[:sz], refs[sz], refs[sz + 1], refs[-1]
        x, y, c, _ = _mesh_place()
        dev = _peer(x, y, c, 1)[0]
        for q, rel in enumerate(SAME_CORE):
            slot = _peer(x, y, c, rel)[1]
            for jj in range(sz):
                pltpu.make_async_remote_copy(
                    src_ref=lnd[jj].at[slot], dst_ref=lnd[jj].at[slot], send_sem=fsend.at[jj * nq + q], recv_sem=frecv.at[jj * nq + q],
                    device_id=dev, device_id_type=pl.DeviceIdType.MESH).start()
        token[...] = jnp.zeros_like(token)

    lands = group["lands"]
    res = pl.pallas_call(
        body, name=name,
        out_shape=(pltpu.SemaphoreType.DMA((sz * nq,)), pltpu.SemaphoreType.DMA((sz * nq,)), *[pltpu.HBM(a.shape, a.dtype) for a in lands],
                   jax.ShapeDtypeStruct((SUBLANE, LANE), F32)),
        in_specs=[_HBM] * sz, out_specs=(_SEM, _SEM, *[_HBM] * sz, pl.BlockSpec(memory_space=pltpu.VMEM)),
        input_output_aliases={i: 2 + i for i in range(sz)},
        compiler_params=pltpu.CompilerParams(has_side_effects=_DATAFLOW))(*lands)
    return dict(group, lands=list(res[2:2 + sz]), relay=[res[0], res[1]]), res[-1]


def _adamw(w, m, v, gparts, name):
    R, C = w.shape
    tm = _pick(R, (256, 128, 64, 32, 16, 8))

    def body(w_ref, m_ref, v_ref, g_ref, go, do, mo, vo):
        g = g_ref[0].astype(F32)
        for j in range(1, N_DEV):
            g = g + g_ref[j].astype(F32)
        mn = ADAM_B1 * m_ref[...] + (1.0 - ADAM_B1) * g
        vn = ADAM_B2 * v_ref[...] + (1.0 - ADAM_B2) * (g * g)
        m_hat = mn / (1.0 - ADAM_B1 ** ADAM_STEP)
        v_hat = vn / (1.0 - ADAM_B2 ** ADAM_STEP)
        go[...] = g
        do[...] = -ADAM_LR * (m_hat / (jnp.sqrt(v_hat) + ADAM_EPS) + ADAM_WD * w_ref[...])
        mo[...] = mn
        vo[...] = vn

    row = pl.BlockSpec((tm, C), lambda i: (i, 0))
    return pl.pallas_call(
        body, name=name, grid=(R // tm,), in_specs=[row, row, row, pl.BlockSpec((N_DEV, tm, C), lambda i: (0, i, 0))],
        out_specs=[row] * 4, out_shape=[jax.ShapeDtypeStruct((R, C), F32)] * 4, compiler_params=_params())(w, m, v, gparts)


def _pack(arrays):
    parts = []
    for a in arrays:
        f = a.reshape(1, -1)
        pad = _ceil_to(f.shape[1], SUBLANE * LANE) - f.shape[1]
        f = jnp.concatenate([f, jnp.zeros((1, pad), f.dtype)], axis=1) if pad else f
        parts.append(f.reshape(-1, LANE))
    rows = sum(p.shape[0] for p in parts)
    pad = _ceil_to(rows, 64) - rows
    return jnp.concatenate(parts + ([jnp.zeros((pad, LANE), parts[0].dtype)] if pad else []), axis=0)


def _unpack(buf, shapes):
    out, row = [], 0
    for s in shapes:
        size = 1
        for d in s:
            size *= d
        rows = _ceil_to(size, SUBLANE * LANE) // LANE
        out.append(buf[row:row + rows].reshape(1, -1)[:, :size].reshape(s))
        row += rows
    return out


def kernel(x, norm_mix_g, w_in, shift_mu, w0, w_lora_up, a0, a_lora_up, g_lora_up, k_k, k_a, r_k, lnx_g, lnx_b, w_proj_rwkv, sgu_ln_g, sgu_ln_b, sgu_w, sgu_b, w_proj_sgu, w_out, norm_ffn_g, w_ffn_gate, w_ffn_up, w_ffn_down, norm_final_g, loss_target, m_norm_mix_g, m_w_in, m_shift_mu, m_w0, m_w_lora_up, m_a0, m_a_lora_up, m_g_lora_up, m_k_k, m_k_a, m_r_k, m_lnx_g, m_lnx_b, m_w_proj_rwkv, m_sgu_ln_g, m_sgu_ln_b, m_sgu_w, m_sgu_b, m_w_proj_sgu, m_w_out, m_norm_ffn_g, m_w_ffn_gate, m_w_ffn_up, m_w_ffn_down, m_norm_final_g, v_norm_mix_g, v_w_in, v_shift_mu, v_w0, v_w_lora_up, v_a0, v_a_lora_up, v_g_lora_up, v_k_k, v_k_a, v_r_k, v_lnx_g, v_lnx_b, v_w_proj_rwkv, v_sgu_ln_g, v_sgu_ln_b, v_sgu_w, v_sgu_b, v_w_proj_sgu, v_w_out, v_norm_ffn_g, v_w_ffn_gate, v_w_ffn_up, v_w_ffn_down, v_norm_final_g):
    weights = dict(norm_mix_g=norm_mix_g, w_in=w_in, shift_mu=shift_mu, w0=w0, w_lora_up=w_lora_up, a0=a0, a_lora_up=a_lora_up,
                   g_lora_up=g_lora_up, k_k=k_k, k_a=k_a, r_k=r_k, lnx_g=lnx_g, lnx_b=lnx_b, w_proj_rwkv=w_proj_rwkv,
                   sgu_ln_g=sgu_ln_g, sgu_ln_b=sgu_ln_b, sgu_w=sgu_w, sgu_b=sgu_b, w_proj_sgu=w_proj_sgu, w_out=w_out,
                   norm_ffn_g=norm_ffn_g, w_ffn_gate=w_ffn_gate, w_ffn_up=w_ffn_up, w_ffn_down=w_ffn_down, norm_final_g=norm_final_g)
    m_in = dict(norm_mix_g=m_norm_mix_g, w_in=m_w_in, shift_mu=m_shift_mu, w0=m_w0, w_lora_up=m_w_lora_up, a0=m_a0,
                a_lora_up=m_a_lora_up, g_lora_up=m_g_lora_up, k_k=m_k_k, k_a=m_k_a, r_k=m_r_k, lnx_g=m_lnx_g, lnx_b=m_lnx_b,
                w_proj_rwkv=m_w_proj_rwkv, sgu_ln_g=m_sgu_ln_g, sgu_ln_b=m_sgu_ln_b, sgu_w=m_sgu_w, sgu_b=m_sgu_b,
                w_proj_sgu=m_w_proj_sgu, w_out=m_w_out, norm_ffn_g=m_norm_ffn_g, w_ffn_gate=m_w_ffn_gate, w_ffn_up=m_w_ffn_up,
                w_ffn_down=m_w_ffn_down, norm_final_g=m_norm_final_g)
    v_in = dict(norm_mix_g=v_norm_mix_g, w_in=v_w_in, shift_mu=v_shift_mu, w0=v_w0, w_lora_up=v_w_lora_up, a0=v_a0,
                a_lora_up=v_a_lora_up, g_lora_up=v_g_lora_up, k_k=v_k_k, k_a=v_k_a, r_k=v_r_k, lnx_g=v_lnx_g, lnx_b=v_lnx_b,
                w_proj_rwkv=v_w_proj_rwkv, sgu_ln_g=v_sgu_ln_g, sgu_ln_b=v_sgu_ln_b, sgu_w=v_sgu_w, sgu_b=v_sgu_b,
                w_proj_sgu=v_w_proj_sgu, w_out=v_w_out, norm_ffn_g=v_norm_ffn_g, w_ffn_gate=v_w_ffn_gate, w_ffn_up=v_w_ffn_up,
                w_ffn_down=v_w_ffn_down, norm_final_g=v_norm_final_g)
    names = list(weights)
    col_sharded = ("w_in", "w_lora_up", "a_lora_up", "g_lora_up", "w_proj_rwkv", "w_proj_sgu", "w_ffn_gate", "w_ffn_up")
    row_sharded = ("w_out", "w_ffn_down")
    sharded = [n for n in names if n in col_sharded or n in row_sharded]
    small = [n for n in names if n not in sharded]

    xs, tgt = x[0], loss_target[0]
    T, D = xs.shape
    RW = w0.shape[1]
    H = RW // HEAD
    SW = sgu_ln_g.shape[1]
    G = sgu_w.shape[1]
    assert 2 * SW == D, "the projection layout takes the SGU part to be as wide as a gate"
    lay = _rwkv_layout(RW, w_lora_up.shape[1], a_lora_up.shape[1], g_lora_up.shape[1], D)
    _, pw, _, rcp = lay
    icp = rcp + 3 * D
    b_ga, b_gb, b_z = rcp // D, rcp // D + 1, rcp // D + 2

    gather_groups = [["w_in", "w_lora_up", "a_lora_up", "g_lora_up"], ["w_proj_rwkv", "w_proj_sgu", "w_out"],
                     ["w_ffn_gate", "w_ffn_up", "w_ffn_down"]]
    gather, gather_token = _exchange_start([[(weights[n][0].astype(BF16), True) for n in grp] for grp in gather_groups],
                                           "gather_start", rels=SIBLING + SAME_CORE)
    full = {}
    relay_tokens = {}
    joined = lambda g: g.transpose(1, 0, 2).reshape(g.shape[1], -1)

    def relay_weights(gi, after, name):
        arrived = _exchange_wait(gather[gi], after, "gather_wait_ici_" + name, rels=SAME_CORE, local=False)
        gather[gi], relay_tokens[gi] = _relay_start(arrived, "gather_relay_" + name)

    def take_weights(gi, after, name):
        done = _exchange_wait(gather[gi], after, "gather_wait_d2d_" + name, rels=SIBLING)
        for n, g in zip(gather_groups[gi], done["lands"]):
            full[n] = g.reshape(-1, g.shape[2]) if n in row_sharded else g

    n1 = _rms_fwd(xs, norm_mix_g, "rms_mix", deps=[gather_token])
    relay_weights(0, n1, "in")
    take_weights(0, relay_tokens[0], "in")
    W_in = _w_in_to_proj(full["w_in"], lay, D, "w_in_layout")
    lora = [_pad_rows(joined(full[n]), rows) for n, rows in zip(("w_lora_up", "a_lora_up", "g_lora_up"), pw[3:])]
    mu_p = _pad_rwkv_cols(shift_mu, lay)
    rsmall = [w0, a0, k_k, k_a]
    hp = [lnx_g.reshape(H, 1, HEAD), lnx_b.reshape(H, 1, HEAD), r_k.reshape(H, 1, HEAD)]
    ws = sgu_w[0]
    bexp = jnp.repeat(sgu_b[0].T, SGU_GROUP, axis=1)
    gf = norm_final_g.reshape(1, D)

    proj = _matmul(n1, W_in, mode="nn", out_dtype=F32, name="proj_in")
    ga, gb = (proj, D, b_ga), (proj, D, b_gb)
    r_h, lw_h, k2_h, v_h, aa_h, bb_h, g_h = _rwkv_pre(proj, mu_p, rsmall, lora, lay, "rwkv_pre")
    wkv_in = [r_h, lw_h, k2_h, v_h, aa_h, bb_h]
    y_h, states = _wkv_fwd(*wkv_in, "wkv_fwd")
    relay_weights(1, y_h, "proj")
    relay_weights(2, relay_tokens[1], "ffn")
    ya = _head_post(y_h, r_h, k2_h, v_h, g_h, hp, "head_post")
    yb = _sgu_fwd(proj, b_z, sgu_ln_g, sgu_ln_b, ws, bexp, "sgu_fwd")
    take_weights(1, ya, "proj")
    pa = _matmul(ya, full["w_proj_rwkv"], mode="nn", out_dtype=F32, name="proj_a")
    pb = _matmul(yb, full["w_proj_sgu"], mode="nn", out_dtype=F32, name="proj_b")

    def merge_fn(rv, pv):
        ga_v, gb_v, pa_v, pb_v = rv
        return [_sigmoid(ga_v) * pa_v + _sigmoid(gb_v) * pb_v], []
    merged = _rowwise(merge_fn, [ga, gb, pa, pb], [], [(D, BF16)], [], name="merge")[0]
    h1 = _matmul(merged, full["w_out"], mode="nn", out_dtype=F32, name="out_proj", add=xs)
    n2 = _rms_fwd(h1, norm_ffn_g, "rms_ffn")
    take_weights(2, n2, "ffn")
    gt = _matmul(n2, full["w_ffn_gate"], mode="nn", out_dtype=F32, name="ffn_gate", out_blocks=N_DEV)
    up = _matmul(n2, full["w_ffn_up"], mode="nn", out_dtype=F32, name="ffn_up", out_blocks=N_DEV)
    fb = gt.shape[2]
    flat = lambda t: t.reshape(N_DEV * T, fb)
    blocked = lambda t: t.reshape(N_DEV, T, fb)

    def act_fn(rv, pv):
        gt_v, up_v = rv
        return [gt_v * _sigmoid(gt_v) * up_v], []
    act = blocked(_rowwise(act_fn, [flat(gt), flat(up)], [], [(fb, BF16)], [], name="ffn_act", tm=1024)[0])
    h2 = _matmul(act, full["w_ffn_down"], mode="nn", out_dtype=F32, name="ffn_down", add=h1)

    def final_fn(rv, pv):
        (h_v, t_v), (g_v,) = rv, pv
        r = lax.rsqrt(_mean(h_v * h_v) + RMS_EPS)
        yn = h_v * r
        e = yn * g_v - t_v
        loss = 0.5 * jnp.sum(_mean(e * e))
        dout = e * (1.0 / D)
        dyg = dout * g_v
        dh = r * (dyg - yn * _mean(dyg * yn))
        return [dh, dh], [jnp.full((1, LANE), loss, F32), _colsum(dout * yn)]
    dh2, dh2_bf, loss_part, d_gf = _rowwise(final_fn, [h2, tgt], [gf], [(D, F32), (D, BF16)], [(1, LANE), (1, D)], name="final_loss")

    grads = {}

    def start_scatter(group, name, extra=()):
        blocks = [(grads[n].reshape(N_DEV, -1, grads[n].shape[1]) if n in row_sharded else grads[n], False) for n in group]
        (handle,), token = _exchange_start([blocks + list(extra)], name)
        return handle, token

    dact = _matmul(dh2_bf, full["w_ffn_down"], mode="nt", out_dtype=F32, name="d_act", out_blocks=N_DEV)
    grads["w_ffn_down"] = _matmul(act, dh2_bf, mode="tn", out_dtype=BF16, name="dw_ffn_down")

    def dact_fn(rv, pv):
        d_v, gt_v, up_v = rv
        s = _sigmoid(gt_v)
        return [d_v * up_v * (s * (1.0 + gt_v * (1.0 - s))), d_v * gt_v * s], []
    dgt, dup = (blocked(t) for t in _rowwise(dact_fn, [flat(dact), flat(gt), flat(up)], [], [(fb, BF16)] * 2, [],
                                             name="d_ffn_act", tm=1024))
    dn2 = _matmul(dgt, full["w_ffn_gate"], mode="nt", out_dtype=F32, name="dn2_gate")
    dn2 = _matmul(dup, full["w_ffn_up"], mode="nt", out_dtype=F32, name="dn2_up", add=dn2)
    grads["w_ffn_gate"] = _matmul(n2, dgt, mode="tn", out_dtype=BF16, name="dw_ffn_gate", out_blocks=N_DEV)
    grads["w_ffn_up"] = _matmul(n2, dup, mode="tn", out_dtype=BF16, name="dw_ffn_up", out_blocks=N_DEV)
    scatter_groups = [["w_ffn_down", "w_ffn_gate", "w_ffn_up"], ["w_out", "w_proj_rwkv", "w_proj_sgu"],
                      ["w_in", "w_lora_up", "a_lora_up", "g_lora_up"]]
    scatter_ffn, token_ffn = start_scatter(scatter_groups[0], "scatter_start_ffn")
    dh1, dh1_bf, d_g2 = _rms_bwd(dn2, h1, dh2, norm_ffn_g, "rms_ffn_bwd", deps=[token_ffn])
    dmerged = _matmul(dh1_bf, full["w_out"], mode="nt", out_dtype=F32, name="d_merged")
    grads["w_out"] = _matmul(merged, dh1_bf, mode="tn", out_dtype=BF16, name="dw_out")

    def dmerge_fn(rv, pv):
        d_v, ga_v, gb_v, pa_v, pb_v = rv
        sa, sb = _sigmoid(ga_v), _sigmoid(gb_v)
        dgates = jnp.concatenate([d_v * pa_v * sa * (1.0 - sa), d_v * pb_v * sb * (1.0 - sb)], axis=1)
        return [dgates, d_v * sa, d_v * sb], []
    dproj, dpa, dpb = _rowwise(dmerge_fn, [dmerged, ga, gb, pa, pb], [],
                               [(2 * D, BF16, icp, b_ga // 2, None), (D, BF16), (D, BF16)], [], name="d_merge")
    dya = _matmul(dpa, full["w_proj_rwkv"], mode="nt", out_dtype=F32, name="d_ya")
    dyb = _matmul(dpb, full["w_proj_sgu"], mode="nt", out_dtype=F32, name="d_yb")
    grads["w_proj_rwkv"] = _matmul(ya, dpa, mode="tn", out_dtype=BF16, name="dw_proj_a", out_blocks=N_DEV)
    grads["w_proj_sgu"] = _matmul(yb, dpb, mode="tn", out_dtype=BF16, name="dw_proj_b", out_blocks=N_DEV)
    scatter_mid, token_mid = start_scatter(scatter_groups[1], "scatter_start_mid")
    dproj, d_lng, d_lnb, d_ws, d_bs = _sgu_bwd(proj, b_z, dyb, sgu_ln_g, sgu_ln_b, ws, bexp, dproj, "sgu_bwd")

    dy_h, dr1, dk1, dv1, dg_h, d_lnxg, d_lnxb, d_rk = _head_post_bwd(dya, y_h, r_h, k2_h, v_h, g_h, hp, "head_post_bwd",
                                                                     deps=[token_mid])
    dr2, dlw_h, dk2b, dv2, daa, dbb = _wkv_bwd(*wkv_in, states, dy_h, "wkv_bwd")
    dps, d_mu, d_w0, d_a0, d_kk, d_ka, d_wlw, d_wla, d_wlg = _rwkv_pre_bwd(
        proj, mu_p, rsmall, lora, [dr1, dr2, dk1, dk2b, dv1, dv2, dlw_h, daa, dbb, dg_h], lay, "rwkv_pre_bwd")
    dproj = _shift_bwd(dps, mu_p, dproj, "shift_bwd")
    split = lambda g: g.reshape(g.shape[0], N_DEV, -1).transpose(1, 0, 2)
    grads["w_in"] = _dw_in_from_proj(_matmul(n1, dproj, mode="tn", out_dtype=BF16, name="dw_in"), lay, D, w_in.shape[2], "dw_in_layout")
    grads["w_lora_up"] = split(d_wlw[:w_lora_up.shape[1]].astype(BF16))
    grads["a_lora_up"] = split(d_wla[:a_lora_up.shape[1]].astype(BF16))
    grads["g_lora_up"] = split(d_wlg[:g_lora_up.shape[1]].astype(BF16))
    scatter_in, token_in = start_scatter(scatter_groups[2], "scatter_start_in")
    dn1 = _matmul(dproj, W_in, mode="nt", out_dtype=F32, name="dn1", deps=[token_in])
    dx, _, d_g1 = _rms_bwd(dn1, xs, dh1, norm_mix_g, "rms_mix_bwd")
    small_grads = dict(norm_mix_g=d_g1, shift_mu=_unpad_rwkv_cols(d_mu, lay), w0=d_w0, a0=d_a0, k_k=d_kk, k_a=d_ka, r_k=d_rk,
                       lnx_g=d_lnxg, lnx_b=d_lnxb, sgu_ln_g=d_lng, sgu_ln_b=d_lnb, sgu_w=d_ws, sgu_b=d_bs[:, :G].T,
                       norm_ffn_g=d_g2, norm_final_g=d_gf)

    (gather_small,), after = _exchange_start([[(_pack([small_grads[n] for n in small]), True)]], "gather_small_start")
    out = {}
    for group, handle, name in zip(scatter_groups, (scatter_ffn, scatter_mid, scatter_in), ("ffn", "mid", "in")):
        parts = _exchange_wait(handle, after, "scatter_wait_" + name)["lands"]
        for n, part in zip(group, parts):
            shp = weights[n].shape
            res = _adamw(weights[n][0], m_in[n][0], v_in[n][0], part, "adamw_" + n)
            out[n] = [t.reshape(shp) for t in res]
            after = res[0]
    packed = [_pack([d[n] for n in small]) for d in (weights, m_in, v_in)]
    small_parts = _exchange_wait(gather_small, after, "gather_small_wait")["lands"][0]
    res = _adamw(*packed, small_parts, "adamw_small")
    unpacked = [_unpack(t, [weights[n].shape for n in small]) for t in res]
    for i, n in enumerate(small):
        out[n] = [u[i] for u in unpacked]

    loss = lax.psum(loss_part[0, 0], ("x", "y", "c"))
    return (loss, dx[None], *[out[n][0] for n in names], *[out[n][1] for n in names],
            *[out[n][2] for n in names], *[out[n][3] for n in names])
```

```python
import jax
import jax.numpy as jnp
from jax import lax
from jax.experimental import pallas as pl
from jax.experimental.pallas import tpu as pltpu

F32 = jnp.float32
BF16 = jnp.bfloat16

N_DEV = 8
LANE = 128
SUBLANE = 8
HEAD = 64
SGU_CHUNK = 128
SGU_GROUP = 128
WKV_CHUNK = 64
RMS_EPS = 1e-6
LN_EPS = 1e-5
LNX_EPS = 64e-5
ADAM_LR, ADAM_B1, ADAM_B2, ADAM_EPS, ADAM_WD, ADAM_STEP = 0.001, 0.9, 0.999, 1e-08, 0.01, 10
VMEM_LIMIT_BYTES = 48 * 1024 * 1024
_SQRT_HALF = 0.7071067811865476
_INV_SQRT_2PI = 0.3989422804014327


def _pick(n, cands):
    for c in cands:
        if n % c == 0:
            return c
    return n


def _ceil_to(n, m):
    return -(-n // m) * m


def _params():
    return pltpu.CompilerParams(vmem_limit_bytes=VMEM_LIMIT_BYTES)


def _tile(n, cap):
    best = 0
    for d in range(LANE, min(n, cap) + 1, LANE):
        if n % d == 0:
            best = d
    return best or n


def _matmul_tiles(M, N, K, a_bytes, b_bytes, o_bytes, has_add, forced):
    tm = forced.get("m") or _tile(M, 1024)
    tn = forced.get("n") or _tile(N, 1024)
    tk = forced.get("k") or _tile(K, 2048)

    def vmem(tm, tn, tk):
        acc = tm * tn * 4 if tk < K else 0
        return 2 * (tm * tk * a_bytes + tk * tn * b_bytes + tm * tn * (o_bytes + (4 if has_add else 0))) + acc

    while vmem(tm, tn, tk) > (VMEM_LIMIT_BYTES * 3) // 4:
        if "k" not in forced and tk > 512 and _tile(K, tk // 2) < tk:
            tk = _tile(K, tk // 2)
        elif "m" not in forced and _tile(M, tm // 2) < tm:
            tm = _tile(M, tm // 2)
        else:
            break
    return tm, tn, tk


def _matmul(a, b, *, mode, out_dtype, name, add=None, deps=(), out_blocks=0):
    def view(x):
        return (x.shape[1], x.shape[0] * x.shape[2], x.shape[2]) if x.ndim == 3 else (x.shape[0], x.shape[1], 0)

    (ar, ac, aw), (br, bc, bw) = view(a), view(b)
    a_col, b_col = {"nn": ("k", "n"), "nt": ("k", "k"), "tn": ("m", "n")}[mode]
    if mode == "nn":
        M, K, K2, N = ar, ac, br, bc
    elif mode == "nt":
        M, K, N, K2 = ar, ac, br, bc
    else:
        K, M, K2, N = ar, ac, br, bc
    assert K == K2, (a.shape, b.shape, mode)
    forced = {}
    for dim, w in ((a_col, aw), (b_col, bw), ("n", N // out_blocks if out_blocks else 0)):
        if w:
            assert forced.get(dim, w) == w
            forced[dim] = w
    has_add = add is not None
    tm, tn, tk = _matmul_tiles(M, N, K, a.dtype.itemsize, b.dtype.itemsize, jnp.dtype(out_dtype).itemsize, has_add, forced)
    nk = K // tk
    dn = {"nn": (((1,), (0,)), ((), ())), "nt": (((1,), (1,)), ((), ())), "tn": (((0,), (0,)), ((), ()))}[mode]
    pick = {"m": lambda i, j, k: i, "n": lambda i, j, k: j, "k": lambda i, j, k: k}
    size = {"m": tm, "n": tn, "k": tk}

    def spec(blocked, row_dim, col_dim):
        rf, cf = pick[row_dim], pick[col_dim]
        if blocked:
            return pl.BlockSpec((None, size[row_dim], size[col_dim]), lambda i, j, k: (cf(i, j, k), rf(i, j, k), 0))
        return pl.BlockSpec((size[row_dim], size[col_dim]), lambda i, j, k: (rf(i, j, k), cf(i, j, k)))

    a_spec = spec(aw, "k" if mode == "tn" else "m", a_col)
    b_spec = spec(bw, "n" if mode == "nt" else "k", b_col)
    o_spec = spec(out_blocks, "m", "n")
    n_in = 2 + has_add + len(deps)

    def body(*refs):
        a_ref, b_ref = refs[0], refs[1]
        add_ref = refs[2] if has_add else None
        o_ref = refs[n_in]
        part = lax.dot_general(a_ref[...].astype(BF16), b_ref[...].astype(BF16), dn, preferred_element_type=F32)
        if nk == 1:
            if has_add:
                part = part + add_ref[...]
            o_ref[...] = part.astype(out_dtype)
            return
        acc_ref = refs[-1]
        kk = pl.program_id(2)

        @pl.when(kk == 0)
        def _():
            acc_ref[...] = part + add_ref[...] if has_add else part

        @pl.when(kk > 0)
        def _():
            acc_ref[...] += part

        @pl.when(kk == nk - 1)
        def _():
            o_ref[...] = acc_ref[...].astype(out_dtype)

    ins = [a, b] + ([add] if has_add else []) + list(deps)
    in_specs = ([a_spec, b_spec] + ([o_spec] if has_add else [])
                + [pl.BlockSpec(d.shape, lambda i, j, k, nd=d.ndim: (0,) * nd) for d in deps])
    return pl.pallas_call(
        body, name=name, grid=(M // tm, N // tn, nk), in_specs=in_specs, out_specs=o_spec,
        out_shape=jax.ShapeDtypeStruct((out_blocks, M, tn) if out_blocks else (M, N), out_dtype),
        scratch_shapes=[pltpu.VMEM((tm, tn), F32)] if nk > 1 else [],
        compiler_params=_params())(*ins)


def _rowwise(fn, rows, pars, row_outs, acc_outs, *, name, tm=256, deps=()):
    rows = [r if isinstance(r, tuple) else (r, r.shape[1], 0) for r in rows]
    row_outs = [o if len(o) == 5 else (o[0], o[1], o[0], 0, None) for o in row_outs]
    aliased = [(k, o[4]) for k, o in enumerate(row_outs) if o[4] is not None]
    R = rows[0][0].shape[0]
    if max(w for _, w, _ in rows) > 4096:
        tm = tm // 2
    tm = min(tm, R)
    assert R % tm == 0
    nr, npar = len(rows), len(pars)
    nro = len(row_outs)
    n_in = nr + npar + len(deps) + len(aliased)

    def body(*refs):
        rv = [r[...] for r in refs[:nr]]
        pv = [p[...] for p in refs[nr:nr + npar]]
        outs = refs[n_in:]
        ro, ao = fn(rv, pv)
        first = pl.program_id(0) == 0
        for o_ref, val in zip(outs[:nro], ro):
            o_ref[...] = val.astype(o_ref.dtype)

        @pl.when(first)
        def _():
            for o_ref, val in zip(outs[nro:], ao):
                o_ref[...] = val

        @pl.when(jnp.logical_not(first))
        def _():
            for o_ref, val in zip(outs[nro:], ao):
                o_ref[...] += val

    in_specs = ([pl.BlockSpec((tm, w), lambda i, cb=cb: (i, cb)) for _, w, cb in rows]
                + [pl.BlockSpec(p.shape, lambda i, nd=p.ndim: (0,) * nd) for p in list(pars) + list(deps)]
                + [pl.BlockSpec(memory_space=pl.ANY)] * len(aliased))
    out_shape = ([jax.ShapeDtypeStruct((R, full), dt) for _, dt, full, _, _ in row_outs]
                 + [jax.ShapeDtypeStruct(s, F32) for s in acc_outs])
    out_specs = ([pl.BlockSpec((tm, f), lambda i, cb=cb: (i, cb)) for f, _, _, cb, _ in row_outs]
                 + [pl.BlockSpec(s, lambda i, nd=len(s): (0,) * nd) for s in acc_outs])
    res = pl.pallas_call(body, name=name, grid=(R // tm,), in_specs=in_specs, out_specs=out_specs, out_shape=out_shape,
                         input_output_aliases={n_in - len(aliased) + q: k for q, (k, _) in enumerate(aliased)},
                         compiler_params=_params())(*[r for r, _, _ in rows], *pars, *deps, *[buf for _, buf in aliased])
    return list(res)


def _bdot(a, b, mode="nn"):
    dn = {"nn": (((1,), (0,)), ((), ())), "nt": (((1,), (1,)), ((), ())), "tn": (((0,), (0,)), ((), ()))}[mode]
    return lax.dot_general(a.astype(BF16), b.astype(BF16), dn, preferred_element_type=F32)


def _sigmoid(x):
    return jax.nn.sigmoid(x)


def _softplus(x):
    return jnp.maximum(x, 0.0) + jnp.log1p(jnp.exp(-jnp.abs(x)))


def _gelu(z):
    return 0.5 * z * (1.0 + lax.erf(z * _SQRT_HALF))


def _gelu_grad(z):
    return 0.5 * (1.0 + lax.erf(z * _SQRT_HALF)) + z * jnp.exp(-0.5 * z * z) * _INV_SQRT_2PI


def _mean(x):
    return jnp.mean(x, axis=-1, keepdims=True)


def _colsum(x):
    return jnp.sum(x, axis=0, keepdims=True)


def _rms_fwd(x, g, name, deps=()):
    def fn(rv, pv):
        (xv,), (gv,) = rv, pv
        r = lax.rsqrt(_mean(xv * xv) + RMS_EPS)
        return [xv * r * gv], []
    return _rowwise(fn, [x], [g], [(x.shape[1], BF16)], [], name=name, deps=deps)[0]


def _rms_bwd(dn, x, dres, g, name, deps=()):
    def fn(rv, pv):
        (dnv, xv, drv), (gv,) = rv, pv
        r = lax.rsqrt(_mean(xv * xv) + RMS_EPS)
        yn = xv * r
        dyg = dnv * gv
        dx = drv + r * (dyg - yn * _mean(dyg * yn))
        return [dx, dx], [_colsum(dnv * yn)]
    D = x.shape[1]
    return _rowwise(fn, [dn, x, dres], [g], [(D, F32), (D, BF16)], [(1, D)], name=name, deps=deps)


def _rwkv_layout(RW, Lw, La, Lg, D):
    widths = [RW, RW, RW, Lw, La, Lg]
    pw = [_ceil_to(w, LANE) for w in widths]
    pw[5] += _ceil_to(sum(pw), 2 * D) - sum(pw)
    offs = [sum(pw[:i]) for i in range(6)]
    return widths, pw, offs, sum(pw)


def _pad_rwkv_cols(a, lay):
    widths, pw, _, _ = lay
    pieces, src = [], 0
    for w, p in zip(widths, pw):
        pieces.append(a[:, src:src + w])
        if p > w:
            pieces.append(jnp.zeros((a.shape[0], p - w), a.dtype))
        src += w
    return jnp.concatenate(pieces, axis=1)


def _unpad_rwkv_cols(a, lay):
    widths, _, offs, _ = lay
    return jnp.concatenate([a[:, o:o + w] for o, w in zip(offs, widths)], axis=1)


def _proj_pieces(lay, D, cs):
    widths, _, offs, rcp = lay
    rc = sum(widths)
    segs = [(sum(widths[:j]), widths[j], offs[j]) for j in range(6)] + [(rc, D, rcp + 2 * D), (rc + D, D, rcp), (rc + 2 * D, D, rcp + D)]
    pieces = []
    for start, width, dst in segs:
        n = start
        while n < start + width:
            d, off = divmod(n, cs)
            take = min(cs - off, start + width - n)
            pieces.append((d, off, dst + n - start, take))
            n += take
    return pieces


def _w_in_to_proj(g, lay, D, name):
    nb, rows, cs = g.shape
    icp = lay[3] + 3 * D
    pieces = _proj_pieces(lay, D, cs)
    tm = _pick(rows, (256, 128, 64, 32, 16))

    def body(i_ref, o_ref):
        o_ref[...] = jnp.zeros_like(o_ref)
        for d, src, dst, w in pieces:
            o_ref[:, dst:dst + w] = i_ref[d, :, src:src + w]

    return pl.pallas_call(
        body, name=name, grid=(rows // tm,), in_specs=[pl.BlockSpec((nb, tm, cs), lambda i: (0, i, 0))],
        out_specs=pl.BlockSpec((tm, icp), lambda i: (i, 0)), out_shape=jax.ShapeDtypeStruct((rows, icp), g.dtype),
        compiler_params=_params())(g)


def _dw_in_from_proj(a, lay, D, cs, name):
    rows, icp = a.shape
    pieces = _proj_pieces(lay, D, cs)
    tm = _pick(rows, (256, 128, 64, 32, 16))

    def body(i_ref, o_ref):
        for d, src, dst, w in pieces:
            o_ref[d, :, src:src + w] = i_ref[:, dst:dst + w]

    return pl.pallas_call(
        body, name=name, grid=(rows // tm,), in_specs=[pl.BlockSpec((tm, icp), lambda i: (i, 0))],
        out_specs=pl.BlockSpec((N_DEV, tm, cs), lambda i: (0, i, 0)), out_shape=jax.ShapeDtypeStruct((N_DEV, rows, cs), a.dtype),
        compiler_params=_params())(a)


def _pad_rows(a, rows):
    return a if a.shape[0] == rows else jnp.concatenate([a, jnp.zeros((rows - a.shape[0], a.shape[1]), a.dtype)], axis=0)


def _token_shift(p, halo, mu, i):
    tm = p.shape[0]
    hid = lax.broadcasted_iota(jnp.int32, (SUBLANE, 1), 0)
    before = jnp.sum(jnp.where(hid == SUBLANE - 1, halo, 0.0), axis=0, keepdims=True)
    before = jnp.where(i == 0, 0.0, before)
    rid = lax.broadcasted_iota(jnp.int32, (tm, 1), 0)
    prev = jnp.where(rid == 0, before, pltpu.roll(p, 1, 0))
    d = prev - p
    return p + d * mu, d


def _rwkv_math(ps, w0, a0, k_k, k_a, wlw, wla, wlg, lay):
    _, pw, offs, _ = lay
    r, k, v, xw, xa, xg = (ps[:, offs[j]:offs[j] + pw[j]] for j in range(6))
    tw = jnp.tanh(xw)
    ww = w0 + _bdot(tw, wlw)
    lw = -jnp.exp(-_softplus(-ww) - 0.5)
    a = _sigmoid(a0 + _bdot(xa, wla))
    sg = _sigmoid(xg)
    g = _bdot(sg, wlg)
    return dict(r=r, k=k, v=v, xa=xa, tw=tw, ww=ww, lw=lw, a=a, sg=sg, g=g, kkp=k * k_k, k2=k * (1.0 + (a - 1.0) * k_a))


def _halo_specs(T, tm, width, after):
    hb = tm // SUBLANE
    last = T // SUBLANE - 1
    if after:
        return pl.BlockSpec((SUBLANE, width), lambda i: (jnp.minimum((i + 1) * hb, last), 0))
    return pl.BlockSpec((SUBLANE, width), lambda i: (jnp.maximum(i * hb - 1, 0), 0))


def _rowsum(x):
    return jnp.sum(x, axis=-1, keepdims=True)


def _kk_math(kkp):
    nrm = jnp.sqrt(_rowsum(kkp * kkp))
    inv = 1.0 / jnp.maximum(nrm, 1e-12)
    return nrm, inv, kkp * inv


def _rwkv_pre(p, mu, small, lora, lay, name):
    T, rcp = p.shape[0], lay[3]
    H = lay[0][0] // HEAD
    tm = min(128, T)

    def body(p_ref, ph_ref, mu_ref, w0_ref, a0_ref, kk_ref, ka_ref, wlw_ref, wla_ref, wlg_ref, r_o, lw_o, k2_o, v_o, aa_o, bb_o, g_o):
        ps, _ = _token_shift(p_ref[...], ph_ref[...], mu_ref[...], pl.program_id(0))
        q = _rwkv_math(ps, w0_ref[...], a0_ref[...], kk_ref[...], ka_ref[...], wlw_ref[...], wla_ref[...], wlg_ref[...], lay)
        for h in range(H):
            sl = slice(h * HEAD, (h + 1) * HEAD)
            for o_ref, key in ((r_o, "r"), (lw_o, "lw"), (k2_o, "k2"), (v_o, "v"), (g_o, "g")):
                o_ref[h] = q[key][:, sl]
            _, _, kk = _kk_math(q["kkp"][:, sl])
            aa_o[h] = -kk
            bb_o[h] = kk * q["a"][:, sl]

    whole = lambda arr: pl.BlockSpec(arr.shape, lambda i: (0, 0))
    return pl.pallas_call(
        body, name=name, grid=(T // tm,),
        in_specs=([pl.BlockSpec((tm, rcp), lambda i: (i, 0)), _halo_specs(T, tm, rcp, False), whole(mu)]
                  + [whole(s) for s in small] + [whole(w) for w in lora]),
        out_specs=[pl.BlockSpec((H, tm, HEAD), lambda i: (0, i, 0))] * 7, out_shape=[jax.ShapeDtypeStruct((H, T, HEAD), F32)] * 7,
        compiler_params=_params())(p, p, mu, *small, *lora)


def _rwkv_pre_bwd(p, mu, small, lora, hgrads, lay, name):
    T, rcp = p.shape[0], lay[3]
    widths, pw, offs, _ = lay
    RW = widths[0]
    H = RW // HEAD
    tm = min(128, T)

    def body(p_ref, ph_ref, mu_ref, w0_ref, a0_ref, kk_ref, ka_ref, wlw_ref, wla_ref, wlg_ref,
             dr1, dr2, dk1, dk2b, dv1, dv2, dlw_h, daa, dbb, dg_h,
             dps_ref, dmu_ref, dw0_ref, da0_ref, dkk_ref, dka_ref, dwlw_ref, dwla_ref, dwlg_ref,
             s_dr, s_dk2, s_dv, s_dlw, s_dkkp, s_da, s_dg):
        i = pl.program_id(0)
        ps, dprev = _token_shift(p_ref[...], ph_ref[...], mu_ref[...], i)
        k_k, k_a = kk_ref[...], ka_ref[...]
        q = _rwkv_math(ps, w0_ref[...], a0_ref[...], k_k, k_a, wlw_ref[...], wla_ref[...], wlg_ref[...], lay)
        k, a, lw, ww, tw, sg = q["k"], q["a"], q["lw"], q["ww"], q["tw"], q["sg"]
        for h in range(H):
            sl = slice(h * HEAD, (h + 1) * HEAD)
            s_dr[:, sl] = dr1[h] + dr2[h]
            s_dk2[:, sl] = dk1[h] + dk2b[h]
            s_dv[:, sl] = dv1[h] + dv2[h]
            s_dlw[:, sl] = dlw_h[h]
            s_dg[:, sl] = dg_h[h]
            nrm, inv, kk = _kk_math(q["kkp"][:, sl])
            dbb_h = dbb[h]
            dkk = dbb_h * a[:, sl] - daa[h]
            s_dkkp[:, sl] = jnp.where(nrm > 1e-12, inv * (dkk - kk * _rowsum(dkk * kk)), dkk * inv)
            s_da[:, sl] = dbb_h * kk
        dk2, dkkp, dg = s_dk2[...], s_dkkp[...], s_dg[...]
        dk = dk2 * (1.0 + (a - 1.0) * k_a) + dkkp * k_k
        da = s_da[...] + dk2 * k * k_a
        dpa = da * a * (1.0 - a)
        dww = s_dlw[...] * lw * _sigmoid(-ww)
        dxa = _bdot(dpa, wla_ref[...], "nt")
        dxw = _bdot(dww, wlw_ref[...], "nt") * (1.0 - tw * tw)
        dxg = _bdot(dg, wlg_ref[...], "nt") * sg * (1.0 - sg)
        segs = (s_dr[...], dk, s_dv[...], dxw, dxa, dxg)
        sums = [dmu_ref, dw0_ref, da0_ref, dkk_ref, dka_ref, dwlw_ref, dwla_ref, dwlg_ref]

        @pl.when(i == 0)
        def _():
            for s in sums:
                s[...] = jnp.zeros_like(s)

        for j, seg in enumerate(segs):
            sl = slice(offs[j], offs[j] + pw[j])
            dps_ref[:, sl] = seg
            dmu_ref[:, sl] += _colsum(seg * dprev[:, sl])
        dw0_ref[...] += _colsum(dww)
        da0_ref[...] += _colsum(dpa)
        dkk_ref[...] += _colsum(dkkp * k)
        dka_ref[...] += _colsum(dk2 * k * (a - 1.0))
        dwlw_ref[...] += _bdot(tw, dww, "tn")
        dwla_ref[...] += _bdot(q["xa"], dpa, "tn")
        dwlg_ref[...] += _bdot(sg, dg, "tn")

    whole = lambda arr: pl.BlockSpec(arr.shape, lambda i: (0, 0))
    row = lambda w: pl.BlockSpec((tm, w), lambda i: (i, 0))
    acc_shapes = [(1, rcp), (1, RW), (1, RW), (1, RW), (1, RW)] + [w.shape for w in lora]
    return pl.pallas_call(
        body, name=name, grid=(T // tm,),
        in_specs=([row(rcp), _halo_specs(T, tm, rcp, False), whole(mu)] + [whole(s) for s in small] + [whole(w) for w in lora]
                  + [pl.BlockSpec((H, tm, HEAD), lambda i: (0, i, 0))] * 10),
        out_specs=[row(rcp)] + [pl.BlockSpec(s, lambda i: (0, 0)) for s in acc_shapes],
        out_shape=[jax.ShapeDtypeStruct((T, rcp), F32)] + [jax.ShapeDtypeStruct(s, F32) for s in acc_shapes],
        scratch_shapes=[pltpu.VMEM((tm, RW), F32)] * 7, compiler_params=_params())(p, p, mu, *small, *lora, *hgrads)


def _shift_bwd(dps, mu, dproj, name):
    T, rcp = dps.shape
    tm = min(256, T)
    nt = T // tm

    def body(d_ref, dh_ref, mu_ref, buf_ref, o_ref):
        i = pl.program_id(0)
        d = d_ref[...]
        hid = lax.broadcasted_iota(jnp.int32, (SUBLANE, 1), 0)
        after = jnp.sum(jnp.where(hid == 0, dh_ref[...], 0.0), axis=0, keepdims=True)
        after = jnp.where(i == nt - 1, 0.0, after)
        rid = lax.broadcasted_iota(jnp.int32, (tm, 1), 0)
        nxt = jnp.where(rid == tm - 1, after, pltpu.roll(d, tm - 1, 0))
        mu_v = mu_ref[...]
        o_ref[...] = (d * (1.0 - mu_v) + nxt * mu_v).astype(BF16)

    row = pl.BlockSpec((tm, rcp), lambda i: (i, 0))
    return pl.pallas_call(
        body, name=name, grid=(nt,),
        in_specs=[row, _halo_specs(T, tm, rcp, True), pl.BlockSpec(mu.shape, lambda i: (0, 0)), pl.BlockSpec(memory_space=pl.ANY)],
        out_specs=row, out_shape=jax.ShapeDtypeStruct(dproj.shape, BF16), input_output_aliases={3: 0},
        compiler_params=_params())(dps, dps, mu, dproj)


def _head_post_math(y, r, k2, v, lg, lb, rk):
    yc = y - _mean(y)
    rstd = lax.rsqrt(_mean(yc * yc) + LNX_EPS)
    yn = yc * rstd
    s = _rowsum(r * k2 * rk)
    return yn, rstd, yn * lg + lb + s * v, s


def _head_post(y, r, k2, v, g, hp, name):
    H, T, _ = y.shape
    tm = min(128, T)

    def body(y_ref, r_ref, k_ref, v_ref, g_ref, lg_ref, lb_ref, rk_ref, o_ref):
        _, _, t, _ = _head_post_math(y_ref[...], r_ref[...], k_ref[...], v_ref[...], lg_ref[...], lb_ref[...], rk_ref[...])
        out = (t * g_ref[...]).astype(BF16)
        for h in range(H):
            o_ref[:, h * HEAD:(h + 1) * HEAD] = out[h]

    blk = pl.BlockSpec((H, tm, HEAD), lambda i: (0, i, 0))
    par = pl.BlockSpec((H, 1, HEAD), lambda i: (0, 0, 0))
    return pl.pallas_call(
        body, name=name, grid=(T // tm,), in_specs=[blk] * 5 + [par] * 3, out_specs=pl.BlockSpec((tm, H * HEAD), lambda i: (i, 0)),
        out_shape=jax.ShapeDtypeStruct((T, H * HEAD), BF16), compiler_params=_params())(y, r, k2, v, g, *hp)


def _head_post_bwd(dya, y, r, k2, v, g, hp, name, deps=()):
    H, T, _ = y.shape
    tm = min(128, T)
    hsum = lambda t: jnp.sum(t, axis=1, keepdims=True)

    def body(d_ref, y_ref, r_ref, k_ref, v_ref, g_ref, lg_ref, lb_ref, rk_ref, *rest):
        outs, d_s = rest[len(deps):len(deps) + 8], rest[-1]
        for h in range(H):
            d_s[h] = d_ref[:, h * HEAD:(h + 1) * HEAD]
        d_v, r_v, k_v, v_v, lg, rk = d_s[...], r_ref[...], k_ref[...], v_ref[...], lg_ref[...], rk_ref[...]
        yn, rstd, t, s = _head_post_math(y_ref[...], r_v, k_v, v_v, lg, lb_ref[...], rk)
        dyo = d_v * g_ref[...]
        dyn = dyo * lg
        ds = _rowsum(dyo * v_v)
        vals = (rstd * (dyn - _mean(dyn) - yn * _mean(dyn * yn)), ds * k_v * rk, ds * r_v * rk, dyo * s, d_v * t)
        for o_ref, val in zip(outs[:5], vals):
            o_ref[...] = val
        sums = (hsum(dyo * yn), hsum(dyo), hsum(ds * r_v * k_v))
        first = pl.program_id(0) == 0

        @pl.when(first)
        def _():
            for o_ref, val in zip(outs[5:], sums):
                o_ref[...] = val

        @pl.when(jnp.logical_not(first))
        def _():
            for o_ref, val in zip(outs[5:], sums):
                o_ref[...] += val

    blk = pl.BlockSpec((H, tm, HEAD), lambda i: (0, i, 0))
    par = pl.BlockSpec((H, 1, HEAD), lambda i: (0, 0, 0))
    return pl.pallas_call(
        body, name=name, grid=(T // tm,),
        in_specs=([pl.BlockSpec((tm, H * HEAD), lambda i: (i, 0))] + [blk] * 5 + [par] * 3
                  + [pl.BlockSpec(d.shape, lambda i, nd=d.ndim: (0,) * nd) for d in deps]),
        out_specs=[blk] * 5 + [par] * 3,
        out_shape=[jax.ShapeDtypeStruct((H, T, HEAD), F32)] * 5 + [jax.ShapeDtypeStruct((H, 1, HEAD), F32)] * 3,
        scratch_shapes=[pltpu.VMEM((H, tm, HEAD), F32)], compiler_params=_params())(dya, y, r, k2, v, g, *hp, *deps)


def _bmm(x, y, mode):
    dn = {"nn": (((2,), (1,)), ((0,), (0,))), "nt": (((2,), (2,)), ((0,), (0,))), "tn": (((1,), (1,)), ((0,), (0,)))}[mode]
    (xh, xl), (yh, yl) = _split(x), _split(y)
    dot = lambda p, q: lax.dot_general(p, q, dn, preferred_element_type=F32)
    out = dot(xh, yh)
    if yl is not None:
        out = out + dot(xh, yl)
    if xl is not None:
        out = out + dot(xl, yh)
    return out


def _split(x):
    if isinstance(x, tuple):
        return x
    hi = x.astype(BF16)
    return hi, (x - hi.astype(F32)).astype(BF16)


def _exact(x):
    return x.astype(BF16), None


def _wkv_chunk(r, lw, k, v, a, b):
    hb, C, _ = r.shape
    ti = lax.broadcasted_iota(jnp.int32, (C, C), 0)
    si = lax.broadcasted_iota(jnp.int32, (C, C), 1)
    linc, lstr, eye = (ti >= si).astype(F32), (ti > si).astype(F32), (ti == si).astype(F32)
    lincb = _exact(jnp.broadcast_to(linc, (hb, C, C)))
    lstrb = _exact(jnp.broadcast_to(lstr, (hb, C, C)))
    ones = _exact(jnp.ones_like(v))
    lws = _split(lw)
    ci = _bmm(lincb, lws, "nn")
    cC = jnp.sum(lw, axis=1, keepdims=True)
    gi, ge, gn, gr = jnp.exp(ci), jnp.exp(ci - lw), jnp.exp(-ci), jnp.exp(cC - ci)
    q = dict(At=a * ge, Rt=r * gi, Bt=b * gn, Kt=k * gn, Bh=b * gr, Kh=k * gr)
    s = {key: _split(val) for key, val in q.items()}
    s["v"] = _split(v)
    q["A_ab"] = _bmm(s["At"], s["Bt"], "nt") * lstr
    for key, lhs, rhs, mask in (("A_ak", "At", "Kt", lstr), ("A_rb", "Rt", "Bt", linc), ("A_rk", "Rt", "Kt", linc)):
        q[key] = _bmm(s[lhs], s[rhs], "nt") * mask
        s[key] = _split(q[key])
    Tm = eye + q["A_ab"]
    Pw = _split(q["A_ab"])
    n = 1
    while 2 * n < C:
        Pw = _split(_bmm(Pw, Pw, "nn"))
        Tm = Tm + _bmm(Tm, Pw, "nn")
        n *= 2
    s["Tm"] = _split(Tm)
    gC = jnp.exp(_bmm(lws, ones, "tn"))
    q.update(gi=gi, ge=ge, gn=gn, gr=gr, linc=linc, lstr=lstr, lincb=lincb, lstrb=lstrb, gC=gC, ones=ones, s=s)
    return q


def _wkv_fwd(r, lw, k, v, a, b, name):
    H, T, N = r.shape
    C = min(WKV_CHUNK, T)
    nc = T // C
    hb = _pick(H, (16, 8, 4, 2))

    def body(r_ref, lw_ref, k_ref, v_ref, a_ref, b_ref, y_ref, st_ref, h_ref):
        @pl.when(pl.program_id(1) == 0)
        def _():
            h_ref[...] = jnp.zeros_like(h_ref)

        H0 = h_ref[...]
        st_ref[0] = H0
        q = _wkv_chunk(r_ref[...], lw_ref[...], k_ref[...], v_ref[...], a_ref[...], b_ref[...])
        s = q["s"]
        H0s = _split(H0)
        U = _split(_bmm(s["Tm"], _bmm(s["At"], H0s, "nn") + _bmm(s["A_ak"], s["v"], "nn"), "nn"))
        y_ref[...] = _bmm(s["Rt"], H0s, "nn") + _bmm(s["A_rb"], U, "nn") + _bmm(s["A_rk"], s["v"], "nn")
        h_ref[...] = q["gC"] * H0 + _bmm(s["Bh"], U, "tn") + _bmm(s["Kh"], s["v"], "tn")

    blk = pl.BlockSpec((hb, C, N), lambda h, c: (h, c, 0))
    return pl.pallas_call(
        body, name=name, grid=(H // hb, nc), in_specs=[blk] * 6,
        out_specs=[blk, pl.BlockSpec((1, hb, N, N), lambda h, c: (c, h, 0, 0))],
        out_shape=[jax.ShapeDtypeStruct((H, T, N), F32), jax.ShapeDtypeStruct((nc, H, N, N), F32)],
        scratch_shapes=[pltpu.VMEM((hb, N, N), F32)], compiler_params=_params())(r, lw, k, v, a, b)


def _wkv_bwd(r, lw, k, v, a, b, states, dy, name):
    H, T, N = r.shape
    C = min(WKV_CHUNK, T)
    nc = T // C
    hb = _pick(H, (16, 8, 4, 2))

    def body(r_ref, lw_ref, k_ref, v_ref, a_ref, b_ref, st_ref, dy_ref, dr_ref, dlw_ref, dk_ref, dv_ref, da_ref, db_ref, dh_ref):
        @pl.when(pl.program_id(1) == 0)
        def _():
            dh_ref[...] = jnp.zeros_like(dh_ref)

        dHC = dh_ref[...]
        H0 = st_ref[0]
        q = _wkv_chunk(r_ref[...], lw_ref[...], k_ref[...], v_ref[...], a_ref[...], b_ref[...])
        s, gC = q["s"], q["gC"]
        H0s, dHs, dY = _split(H0), _split(dHC), _split(dy_ref[...])
        U = _split(_bmm(s["Tm"], _bmm(s["At"], H0s, "nn") + _bmm(s["A_ak"], s["v"], "nn"), "nn"))
        dU = _bmm(s["A_rb"], dY, "tn") + _bmm(s["Bh"], dHs, "nn")
        dP = _split(_bmm(s["Tm"], dU, "tn"))
        dv_ref[...] = _bmm(s["A_rk"], dY, "tn") + _bmm(s["Kh"], dHs, "nn") + _bmm(s["A_ak"], dP, "tn")
        dh_ref[...] = _bmm(s["Rt"], dY, "tn") + gC * dHC + _bmm(s["At"], dP, "tn")
        dA_rb = _split(_bmm(dY, U, "nt") * q["linc"])
        dA_rk = _split(_bmm(dY, s["v"], "nt") * q["linc"])
        dA_ab = _split(_bmm(dP, U, "nt") * q["lstr"])
        dA_ak = _split(_bmm(dP, s["v"], "nt") * q["lstr"])
        dRt = _bmm(dY, H0s, "nt") + _bmm(dA_rb, s["Bt"], "nn") + _bmm(dA_rk, s["Kt"], "nn")
        dAt = _bmm(dP, H0s, "nt") + _bmm(dA_ab, s["Bt"], "nn") + _bmm(dA_ak, s["Kt"], "nn")
        dBt = _bmm(dA_ab, s["At"], "tn") + _bmm(dA_rb, s["Rt"], "tn")
        dKt = _bmm(dA_ak, s["At"], "tn") + _bmm(dA_rk, s["Rt"], "tn")
        dBh = _bmm(U, dHs, "nt")
        dKh = _bmm(s["v"], dHs, "nt")
        dr_ref[...] = dRt * q["gi"]
        da_ref[...] = dAt * q["ge"]
        db_ref[...] = dBt * q["gn"] + dBh * q["gr"]
        dk_ref[...] = dKt * q["gn"] + dKh * q["gr"]
        tail = dBh * q["Bh"] + dKh * q["Kh"]
        dci = dRt * q["Rt"] - dBt * q["Bt"] - dKt * q["Kt"] - tail
        dcC = jnp.sum(tail, axis=1, keepdims=True) + _bmm(q["ones"], H0 * dHC * gC, "nt")
        dlw_ref[...] = _bmm(q["lincb"], dci, "tn") + _bmm(q["lstrb"], dAt * q["At"], "tn") + dcC

    blk = pl.BlockSpec((hb, C, N), lambda h, c: (h, nc - 1 - c, 0))
    st = pl.BlockSpec((1, hb, N, N), lambda h, c: (nc - 1 - c, h, 0, 0))
    return pl.pallas_call(
        body, name=name, grid=(H // hb, nc), in_specs=[blk] * 6 + [st, blk], out_specs=[blk] * 6,
        out_shape=[jax.ShapeDtypeStruct((H, T, N), F32)] * 6,
        scratch_shapes=[pltpu.VMEM((hb, N, N), F32)], compiler_params=_params())(r, lw, k, v, a, b, states, dy)


def _sgu_ln(z, SW, lng, lnb):
    ge = _gelu(z)
    u, vv = ge[:, :SW], ge[:, SW:]
    xc = vv - _mean(vv)
    rstd = lax.rsqrt(_mean(xc * xc) + LN_EPS)
    vn = xc * rstd
    return u, vn, rstd, vn * lng + lnb


def _causal(ws_ref, g):
    ti = lax.broadcasted_iota(jnp.int32, (SGU_CHUNK, SGU_CHUNK), 0)
    si = lax.broadcasted_iota(jnp.int32, (SGU_CHUNK, SGU_CHUNK), 1)
    return ti >= si, jnp.where(ti >= si, ws_ref[g], 0.0).astype(BF16)


def _sgu_fwd(proj, zblock, lng, lnb, ws, bexp, name):
    T, SW = proj.shape[0], lng.shape[1]
    G = ws.shape[0]
    tr = min(256, T)
    nch = tr // SGU_CHUNK

    def body(z_ref, lng_ref, lnb_ref, ws_ref, be_ref, o_ref):
        u, _, _, vl = _sgu_ln(z_ref[...], SW, lng_ref[...], lnb_ref[...])
        for g in range(G):
            cs = slice(g * SGU_GROUP, (g + 1) * SGU_GROUP)
            _, wc = _causal(ws_ref, g)
            for n in range(nch):
                rs = slice(n * SGU_CHUNK, (n + 1) * SGU_CHUNK)
                m = jnp.dot(wc, vl[rs, cs].astype(BF16), preferred_element_type=F32) + be_ref[:, cs]
                o_ref[rs, cs] = (u[rs, cs] * m).astype(BF16)

    whole = lambda arr: pl.BlockSpec(arr.shape, lambda i, nd=arr.ndim: (0,) * nd)
    return pl.pallas_call(
        body, name=name, grid=(T // tr,),
        in_specs=[pl.BlockSpec((tr, 2 * SW), lambda i: (i, zblock)), whole(lng), whole(lnb), whole(ws), whole(bexp)],
        out_specs=pl.BlockSpec((tr, SW), lambda i: (i, 0)), out_shape=jax.ShapeDtypeStruct((T, SW), BF16),
        compiler_params=_params())(proj, lng, lnb, ws, bexp)


def _sgu_bwd(proj, zblock, dyb, lng, lnb, ws, bexp, dproj, name):
    T, SW = proj.shape[0], lng.shape[1]
    G = ws.shape[0]
    tr = min(256, T)
    nch = tr // SGU_CHUNK
    nt = T // tr

    def body(z_ref, dy_ref, lng_ref, lnb_ref, ws_ref, be_ref, buf_ref, dz_ref, dlg_ref, dlb_ref, dws_ref, db_ref, du_s, dvl_s, dbacc_s):
        i = pl.program_id(0)
        zv = z_ref[...]
        lng_v = lng_ref[...]
        u, vn, rstd, vl = _sgu_ln(zv, SW, lng_v, lnb_ref[...])

        @pl.when(i == 0)
        def _():
            for s in (dlg_ref, dlb_ref, dws_ref, dbacc_s):
                s[...] = jnp.zeros_like(s)

        for g in range(G):
            cs = slice(g * SGU_GROUP, (g + 1) * SGU_GROUP)
            tri, wc = _causal(ws_ref, g)
            for n in range(nch):
                rs = slice(n * SGU_CHUNK, (n + 1) * SGU_CHUNK)
                blk = vl[rs, cs].astype(BF16)
                m = jnp.dot(wc, blk, preferred_element_type=F32) + be_ref[:, cs]
                dyv = dy_ref[rs, cs]
                du_s[rs, cs] = dyv * m
                dm = dyv * u[rs, cs]
                dvl_s[rs, cs] = _bdot(wc, dm, "tn")
                dws_ref[g] += jnp.where(tri, _bdot(dm, blk, "nt"), 0.0)
                dbacc_s[:, cs] += dm

        dvl = dvl_s[...]
        dlg_ref[...] += _colsum(dvl * vn)
        dlb_ref[...] += _colsum(dvl)
        dvn = dvl * lng_v
        dvv = rstd * (dvn - _mean(dvn) - vn * _mean(dvn * vn))
        gp = _gelu_grad(zv)
        dz_ref[:, :SW] = (du_s[...] * gp[:, :SW]).astype(BF16)
        dz_ref[:, SW:] = (dvv * gp[:, SW:]).astype(BF16)

        @pl.when(i == nt - 1)
        def _():
            lane = lax.broadcasted_iota(jnp.int32, (SGU_CHUNK, LANE), 1)
            out = jnp.zeros((SGU_CHUNK, LANE), F32)
            for g in range(G):
                col = jnp.sum(dbacc_s[:, g * SGU_GROUP:(g + 1) * SGU_GROUP], axis=1, keepdims=True)
                out = jnp.where(lane == g, col, out)
            db_ref[...] = out

    whole = lambda arr: pl.BlockSpec(arr.shape, lambda i, nd=arr.ndim: (0,) * nd)
    acc_shapes = [(1, SW), (1, SW), ws.shape, (SGU_CHUNK, LANE)]
    return pl.pallas_call(
        body, name=name, grid=(nt,),
        in_specs=[pl.BlockSpec((tr, 2 * SW), lambda i: (i, zblock)), pl.BlockSpec((tr, SW), lambda i: (i, 0)),
                  whole(lng), whole(lnb), whole(ws), whole(bexp), pl.BlockSpec(memory_space=pl.ANY)],
        out_specs=([pl.BlockSpec((tr, 2 * SW), lambda i: (i, zblock))]
                   + [pl.BlockSpec(s, lambda i, nd=len(s): (0,) * nd) for s in acc_shapes]),
        out_shape=[jax.ShapeDtypeStruct(dproj.shape, BF16)] + [jax.ShapeDtypeStruct(s, F32) for s in acc_shapes],
        scratch_shapes=[pltpu.VMEM((tr, SW), F32), pltpu.VMEM((tr, SW), F32), pltpu.VMEM((SGU_CHUNK, SW), F32)],
        input_output_aliases={6: 0}, compiler_params=_params())(proj, dyb, lng, lnb, ws, bexp, dproj)


_HBM = pl.BlockSpec(memory_space=pltpu.HBM)
_SEM = pl.BlockSpec(memory_space=pltpu.SEMAPHORE)
_DATAFLOW = pltpu.SideEffectType.DATAFLOW_SIDE_EFFECTING


def _mesh_place(chips=False):
    x, y, c = lax.axis_index("x"), lax.axis_index("y"), lax.axis_index("c")
    return x, y, c, (2 * x + y if chips else 4 * x + 2 * y + c)


def _peer(x, y, c, rel, chips=False):
    px = 1 - x if rel & 4 else x
    py = 1 - y if rel & 2 else y
    pc = 1 - c if rel & 1 else c
    return (px, py, pc), (2 * px + py if chips else 4 * px + 2 * py + pc)


ALL_PEERS = tuple(range(1, N_DEV))
SIBLING = (1,)
SAME_CORE = (2, 4, 6)
SIBLINGS_CORE = (3, 5, 7)


def _exchange_start(groups, name, rels=ALL_PEERS, chips=False):
    flat = [t for g in groups for t in g]
    sizes = [len(g) for g in groups]
    n, ng = len(flat), len(groups)
    srcs = [pltpu.with_memory_space_constraint(a, pltpu.HBM) for a, _ in flat]
    lands = [pltpu.with_memory_space_constraint(lax.empty(((N_DEV,) + a.shape) if isg else a.shape, a.dtype), pltpu.HBM)
             for a, isg in flat]

    def body(*refs):
        ins, lnd, sems, token = refs[:n], refs[n:2 * n], refs[2 * n:2 * n + 3 * ng], refs[-1]
        x, y, c, me = _mesh_place(chips)
        j0 = 0
        for gi, sz in enumerate(sizes):
            for rel in rels:
                dev, slot = _peer(x, y, c, rel, chips)
                for jj in range(sz):
                    j = j0 + jj
                    pltpu.make_async_remote_copy(
                        src_ref=ins[j] if flat[j][1] else ins[j].at[slot], dst_ref=lnd[j].at[me],
                        send_sem=sems[3 * gi].at[jj * (N_DEV - 1) + rel - 1], recv_sem=sems[3 * gi + 1].at[jj * (N_DEV - 1) + rel - 1],
                        device_id=dev, device_id_type=pl.DeviceIdType.MESH).start()
            for jj in range(sz):
                j = j0 + jj
                pltpu.make_async_copy(ins[j] if flat[j][1] else ins[j].at[me], lnd[j].at[me], sems[3 * gi + 2].at[jj]).start()
            j0 += sz
        token[...] = jnp.zeros_like(token)

    sem_shapes = [pltpu.SemaphoreType.DMA((k,)) for sz in sizes for k in (sz * (N_DEV - 1), sz * (N_DEV - 1), sz)]
    res = pl.pallas_call(
        body, name=name,
        out_shape=(*sem_shapes, *[pltpu.HBM(a.shape, a.dtype) for a in srcs], *[pltpu.HBM(a.shape, a.dtype) for a in lands],
                   jax.ShapeDtypeStruct((SUBLANE, LANE), F32)),
        in_specs=[_HBM] * (2 * n), out_specs=(*[_SEM] * (3 * ng), *[_HBM] * (2 * n), pl.BlockSpec(memory_space=pltpu.VMEM)),
        input_output_aliases={i: 3 * ng + i for i in range(2 * n)},
        compiler_params=pltpu.CompilerParams(has_side_effects=_DATAFLOW))(*srcs, *lands)
    sems, thru, token = res[:3 * ng], res[3 * ng:3 * ng + 2 * n], res[-1]
    handle, j0 = [], 0
    for gi, sz in enumerate(sizes):
        handle.append(dict(kinds=[k for _, k in groups[gi]], chips=chips, srcs=list(thru[j0:j0 + sz]), lands=list(thru[n + j0:n + j0 + sz]),
                           sems=list(sems[3 * gi:3 * gi + 3])))
        j0 += sz
    return handle, token


def _exchange_wait(group, after, name, rels=ALL_PEERS, local=True):
    kinds, sz = group["kinds"], len(group["kinds"])
    relay = group.get("relay", [])

    def body(*refs):
        ins, lnd, (ssem, rsem, lsem) = refs[:sz], refs[sz:2 * sz], refs[2 * sz:2 * sz + 3]
        x, y, c, me = _mesh_place(group["chips"])
        for rel in rels:
            dev, slot = _peer(x, y, c, rel, group["chips"])
            for jj in range(sz):
                cp = pltpu.make_async_remote_copy(
                    src_ref=ins[jj] if kinds[jj] else ins[jj].at[slot], dst_ref=lnd[jj].at[slot],
                    send_sem=ssem.at[jj * (N_DEV - 1) + rel - 1], recv_sem=rsem.at[jj * (N_DEV - 1) + rel - 1],
                    device_id=dev, device_id_type=pl.DeviceIdType.MESH)
                cp.wait_send()
                cp.wait_recv()
        if local:
            for jj in range(sz):
                pltpu.make_async_copy(ins[jj] if kinds[jj] else ins[jj].at[me], lnd[jj].at[me], lsem.at[jj]).wait()
        if relay:
            fsend, frecv = refs[2 * sz + 3:2 * sz + 5]
            dev = _peer(x, y, c, 1)[0]
            for q, (mine, theirs) in enumerate(zip(SAME_CORE, SIBLINGS_CORE)):
                for jj in range(sz):
                    cp = pltpu.make_async_remote_copy(
                        src_ref=lnd[jj].at[_peer(x, y, c, mine)[1]], dst_ref=lnd[jj].at[_peer(x, y, c, theirs)[1]],
                        send_sem=fsend.at[jj * len(SAME_CORE) + q], recv_sem=frecv.at[jj * len(SAME_CORE) + q],
                        device_id=dev, device_id_type=pl.DeviceIdType.MESH)
                    cp.wait_send()
                    cp.wait_recv()

    arrays = group["srcs"] + group["lands"]
    sems = group["sems"] + relay
    res = pl.pallas_call(
        body, name=name, out_shape=[pltpu.HBM(a.shape, a.dtype) for a in arrays],
        in_specs=[_HBM] * (2 * sz) + [_SEM] * len(sems) + [pl.BlockSpec(memory_space=pl.ANY)], out_specs=[_HBM] * (2 * sz),
        input_output_aliases={i: i for i in range(2 * sz)},
        compiler_params=pltpu.CompilerParams(has_side_effects=_DATAFLOW))(*arrays, *sems, after)
    return dict(group, srcs=list(res[:sz]), lands=list(res[sz:]), relay=[])


def _relay_start(group, name):
    sz = len(group["kinds"])
    nq = len(SAME_CORE)

    def body(*refs):
        lnd, fsend, frecv, token = refs[:sz], refs[sz], refs[sz + 1], refs[-1]
        x, y, c, _ = _mesh_place()
        dev = _peer(x, y, c, 1)[0]
        for q, rel in enumerate(SAME_CORE):
            slot = _peer(x, y, c, rel)[1]
            for jj in range(sz):
                pltpu.make_async_remote_copy(
                    src_ref=lnd[jj].at[slot], dst_ref=lnd[jj].at[slot], send_sem=fsend.at[jj * nq + q], recv_sem=frecv.at[jj * nq + q],
                    device_id=dev, device_id_type=pl.DeviceIdType.MESH).start()
        token[...] = jnp.zeros_like(token)

    lands = group["lands"]
    res = pl.pallas_call(
        body, name=name,
        out_shape=(pltpu.SemaphoreType.DMA((sz * nq,)), pltpu.SemaphoreType.DMA((sz * nq,)), *[pltpu.HBM(a.shape, a.dtype) for a in lands],
                   jax.ShapeDtypeStruct((SUBLANE, LANE), F32)),
        in_specs=[_HBM] * sz, out_specs=(_SEM, _SEM, *[_HBM] * sz, pl.BlockSpec(memory_space=pltpu.VMEM)),
        input_output_aliases={i: 2 + i for i in range(sz)},
        compiler_params=pltpu.CompilerParams(has_side_effects=_DATAFLOW))(*lands)
    return dict(group, lands=list(res[2:2 + sz]), relay=[res[0], res[1]]), res[-1]


def _sibling_swap(arrays, handle, after, name):
    start = handle is None
    n = len(arrays) if start else len(handle["srcs"])
    chips = N_DEV // 2
    if start:
        srcs = [pltpu.with_memory_space_constraint(a.reshape(chips, 2, *a.shape[1:]), pltpu.HBM) for a in arrays]
        lands = [pltpu.with_memory_space_constraint(lax.empty((chips,) + a.shape[1:], a.dtype), pltpu.HBM) for a in arrays]
    else:
        srcs, lands = handle["srcs"], handle["lands"]

    def body(*refs):
        ins, lnd, ssem, rsem = refs[:n], refs[n:2 * n], refs[2 * n], refs[2 * n + 1]
        x, y, c, _ = _mesh_place()
        dev = _peer(x, y, c, 1)[0]
        for q in range(chips):
            for j in range(n):
                cp = pltpu.make_async_remote_copy(
                    src_ref=ins[j].at[q, 1 - c], dst_ref=lnd[j].at[q], send_sem=ssem.at[j * chips + q], recv_sem=rsem.at[j * chips + q],
                    device_id=dev, device_id_type=pl.DeviceIdType.MESH)
                if start:
                    cp.start()
                else:
                    cp.wait_send()
                    cp.wait_recv()
        if start:
            refs[-1][...] = jnp.zeros_like(refs[-1])

    thru = [pltpu.HBM(a.shape, a.dtype) for a in srcs + lands]
    effect = pltpu.CompilerParams(has_side_effects=_DATAFLOW)
    if start:
        res = pl.pallas_call(
            body, name=name, out_shape=(pltpu.SemaphoreType.DMA((n * chips,)), pltpu.SemaphoreType.DMA((n * chips,)), *thru,
                                        jax.ShapeDtypeStruct((SUBLANE, LANE), F32)),
            in_specs=[_HBM] * (2 * n), out_specs=(_SEM, _SEM, *[_HBM] * (2 * n), pl.BlockSpec(memory_space=pltpu.VMEM)),
            input_output_aliases={i: 2 + i for i in range(2 * n)}, compiler_params=effect)(*srcs, *lands)
        return dict(srcs=list(res[2:2 + n]), lands=list(res[2 + n:2 + 2 * n]), sems=[res[0], res[1]]), res[-1]
    res = pl.pallas_call(
        body, name=name, out_shape=thru, in_specs=[_HBM] * (2 * n) + [_SEM, _SEM, pl.BlockSpec(memory_space=pl.ANY)],
        out_specs=[_HBM] * (2 * n), input_output_aliases={i: i for i in range(2 * n)}, compiler_params=effect)(
            *srcs, *lands, *handle["sems"], after)
    return dict(handle, srcs=list(res[:n]), lands=list(res[n:]))


def _pair_add(mine, theirs, core, name):
    chips, _, rows, w = mine.shape
    tm = _pick(rows, (256, 128, 64, 32, 16))

    def body(core_ref, a_ref, b_ref, o_ref):
        o_ref[...] = (a_ref[...].astype(F32) + b_ref[...].astype(F32)).astype(o_ref.dtype)

    return pl.pallas_call(
        body, name=name, out_shape=jax.ShapeDtypeStruct(theirs.shape, theirs.dtype),
        grid_spec=pltpu.PrefetchScalarGridSpec(
            num_scalar_prefetch=1, grid=(chips, rows // tm),
            in_specs=[pl.BlockSpec((None, None, tm, w), lambda q, i, core_ref: (q, core_ref[0], i, 0)),
                      pl.BlockSpec((None, tm, w), lambda q, i, core_ref: (q, i, 0))],
            out_specs=pl.BlockSpec((None, tm, w), lambda q, i, core_ref: (q, i, 0))),
        compiler_params=_params())(core, mine, theirs)


def _adamw(w, m, v, gparts, name):
    R, C = w.shape
    tm = _pick(R, (256, 128, 64, 32, 16, 8))

    def body(w_ref, m_ref, v_ref, g_ref, go, do, mo, vo):
        g = g_ref[0].astype(F32)
        for j in range(1, gparts.shape[0]):
            g = g + g_ref[j].astype(F32)
        mn = ADAM_B1 * m_ref[...] + (1.0 - ADAM_B1) * g
        vn = ADAM_B2 * v_ref[...] + (1.0 - ADAM_B2) * (g * g)
        m_hat = mn / (1.0 - ADAM_B1 ** ADAM_STEP)
        v_hat = vn / (1.0 - ADAM_B2 ** ADAM_STEP)
        go[...] = g
        do[...] = -ADAM_LR * (m_hat / (jnp.sqrt(v_hat) + ADAM_EPS) + ADAM_WD * w_ref[...])
        mo[...] = mn
        vo[...] = vn

    row = pl.BlockSpec((tm, C), lambda i: (i, 0))
    return pl.pallas_call(
        body, name=name, grid=(R // tm,), in_specs=[row, row, row, pl.BlockSpec((gparts.shape[0], tm, C), lambda i: (0, i, 0))],
        out_specs=[row] * 4, out_shape=[jax.ShapeDtypeStruct((R, C), F32)] * 4, compiler_params=_params())(w, m, v, gparts)


def _pack(arrays):
    parts = []
    for a in arrays:
        f = a.reshape(1, -1)
        pad = _ceil_to(f.shape[1], SUBLANE * LANE) - f.shape[1]
        f = jnp.concatenate([f, jnp.zeros((1, pad), f.dtype)], axis=1) if pad else f
        parts.append(f.reshape(-1, LANE))
    rows = sum(p.shape[0] for p in parts)
    pad = _ceil_to(rows, 64) - rows
    return jnp.concatenate(parts + ([jnp.zeros((pad, LANE), parts[0].dtype)] if pad else []), axis=0)


def _unpack(buf, shapes):
    out, row = [], 0
    for s in shapes:
        size = 1
        for d in s:
            size *= d
        rows = _ceil_to(size, SUBLANE * LANE) // LANE
        out.append(buf[row:row + rows].reshape(1, -1)[:, :size].reshape(s))
        row += rows
    return out


def kernel(x, norm_mix_g, w_in, shift_mu, w0, w_lora_up, a0, a_lora_up, g_lora_up, k_k, k_a, r_k, lnx_g, lnx_b, w_proj_rwkv, sgu_ln_g, sgu_ln_b, sgu_w, sgu_b, w_proj_sgu, w_out, norm_ffn_g, w_ffn_gate, w_ffn_up, w_ffn_down, norm_final_g, loss_target, m_norm_mix_g, m_w_in, m_shift_mu, m_w0, m_w_lora_up, m_a0, m_a_lora_up, m_g_lora_up, m_k_k, m_k_a, m_r_k, m_lnx_g, m_lnx_b, m_w_proj_rwkv, m_sgu_ln_g, m_sgu_ln_b, m_sgu_w, m_sgu_b, m_w_proj_sgu, m_w_out, m_norm_ffn_g, m_w_ffn_gate, m_w_ffn_up, m_w_ffn_down, m_norm_final_g, v_norm_mix_g, v_w_in, v_shift_mu, v_w0, v_w_lora_up, v_a0, v_a_lora_up, v_g_lora_up, v_k_k, v_k_a, v_r_k, v_lnx_g, v_lnx_b, v_w_proj_rwkv, v_sgu_ln_g, v_sgu_ln_b, v_sgu_w, v_sgu_b, v_w_proj_sgu, v_w_out, v_norm_ffn_g, v_w_ffn_gate, v_w_ffn_up, v_w_ffn_down, v_norm_final_g):
    weights = dict(norm_mix_g=norm_mix_g, w_in=w_in, shift_mu=shift_mu, w0=w0, w_lora_up=w_lora_up, a0=a0, a_lora_up=a_lora_up,
                   g_lora_up=g_lora_up, k_k=k_k, k_a=k_a, r_k=r_k, lnx_g=lnx_g, lnx_b=lnx_b, w_proj_rwkv=w_proj_rwkv,
                   sgu_ln_g=sgu_ln_g, sgu_ln_b=sgu_ln_b, sgu_w=sgu_w, sgu_b=sgu_b, w_proj_sgu=w_proj_sgu, w_out=w_out,
                   norm_ffn_g=norm_ffn_g, w_ffn_gate=w_ffn_gate, w_ffn_up=w_ffn_up, w_ffn_down=w_ffn_down, norm_final_g=norm_final_g)
    m_in = dict(norm_mix_g=m_norm_mix_g, w_in=m_w_in, shift_mu=m_shift_mu, w0=m_w0, w_lora_up=m_w_lora_up, a0=m_a0,
                a_lora_up=m_a_lora_up, g_lora_up=m_g_lora_up, k_k=m_k_k, k_a=m_k_a, r_k=m_r_k, lnx_g=m_lnx_g, lnx_b=m_lnx_b,
                w_proj_rwkv=m_w_proj_rwkv, sgu_ln_g=m_sgu_ln_g, sgu_ln_b=m_sgu_ln_b, sgu_w=m_sgu_w, sgu_b=m_sgu_b,
                w_proj_sgu=m_w_proj_sgu, w_out=m_w_out, norm_ffn_g=m_norm_ffn_g, w_ffn_gate=m_w_ffn_gate, w_ffn_up=m_w_ffn_up,
                w_ffn_down=m_w_ffn_down, norm_final_g=m_norm_final_g)
    v_in = dict(norm_mix_g=v_norm_mix_g, w_in=v_w_in, shift_mu=v_shift_mu, w0=v_w0, w_lora_up=v_w_lora_up, a0=v_a0,
                a_lora_up=v_a_lora_up, g_lora_up=v_g_lora_up, k_k=v_k_k, k_a=v_k_a, r_k=v_r_k, lnx_g=v_lnx_g, lnx_b=v_lnx_b,
                w_proj_rwkv=v_w_proj_rwkv, sgu_ln_g=v_sgu_ln_g, sgu_ln_b=v_sgu_ln_b, sgu_w=v_sgu_w, sgu_b=v_sgu_b,
                w_proj_sgu=v_w_proj_sgu, w_out=v_w_out, norm_ffn_g=v_norm_ffn_g, w_ffn_gate=v_w_ffn_gate, w_ffn_up=v_w_ffn_up,
                w_ffn_down=v_w_ffn_down, norm_final_g=v_norm_final_g)
    names = list(weights)
    col_sharded = ("w_in", "w_lora_up", "a_lora_up", "g_lora_up", "w_proj_rwkv", "w_proj_sgu", "w_ffn_gate", "w_ffn_up")
    row_sharded = ("w_out", "w_ffn_down")
    sharded = [n for n in names if n in col_sharded or n in row_sharded]
    small = [n for n in names if n not in sharded]

    xs, tgt = x[0], loss_target[0]
    T, D = xs.shape
    RW = w0.shape[1]
    H = RW // HEAD
    SW = sgu_ln_g.shape[1]
    G = sgu_w.shape[1]
    assert 2 * SW == D, "the projection layout takes the SGU part to be as wide as a gate"
    lay = _rwkv_layout(RW, w_lora_up.shape[1], a_lora_up.shape[1], g_lora_up.shape[1], D)
    _, pw, _, rcp = lay
    icp = rcp + 3 * D
    b_ga, b_gb, b_z = rcp // D, rcp // D + 1, rcp // D + 2

    gather_groups = [["w_in", "w_lora_up", "a_lora_up", "g_lora_up"], ["w_proj_rwkv", "w_proj_sgu", "w_out"],
                     ["w_ffn_gate", "w_ffn_up", "w_ffn_down"]]
    gather, gather_token = _exchange_start([[(weights[n][0].astype(BF16), True) for n in grp] for grp in gather_groups],
                                           "gather_start", rels=SIBLING + SAME_CORE)
    full = {}
    relay_tokens = {}
    joined = lambda g: g.transpose(1, 0, 2).reshape(g.shape[1], -1)

    def relay_weights(gi, after, name):
        arrived = _exchange_wait(gather[gi], after, "gather_wait_ici_" + name, rels=SAME_CORE, local=False)
        gather[gi], relay_tokens[gi] = _relay_start(arrived, "gather_relay_" + name)

    def take_weights(gi, after, name):
        done = _exchange_wait(gather[gi], after, "gather_wait_d2d_" + name, rels=SIBLING)
        for n, g in zip(gather_groups[gi], done["lands"]):
            full[n] = g.reshape(-1, g.shape[2]) if n in row_sharded else g

    n1 = _rms_fwd(xs, norm_mix_g, "rms_mix", deps=[gather_token])
    relay_weights(0, n1, "in")
    take_weights(0, relay_tokens[0], "in")
    W_in = _w_in_to_proj(full["w_in"], lay, D, "w_in_layout")
    lora = [_pad_rows(joined(full[n]), rows) for n, rows in zip(("w_lora_up", "a_lora_up", "g_lora_up"), pw[3:])]
    mu_p = _pad_rwkv_cols(shift_mu, lay)
    rsmall = [w0, a0, k_k, k_a]
    hp = [lnx_g.reshape(H, 1, HEAD), lnx_b.reshape(H, 1, HEAD), r_k.reshape(H, 1, HEAD)]
    ws = sgu_w[0]
    bexp = jnp.repeat(sgu_b[0].T, SGU_GROUP, axis=1)
    gf = norm_final_g.reshape(1, D)

    proj = _matmul(n1, W_in, mode="nn", out_dtype=F32, name="proj_in")
    ga, gb = (proj, D, b_ga), (proj, D, b_gb)
    r_h, lw_h, k2_h, v_h, aa_h, bb_h, g_h = _rwkv_pre(proj, mu_p, rsmall, lora, lay, "rwkv_pre")
    wkv_in = [r_h, lw_h, k2_h, v_h, aa_h, bb_h]
    y_h, states = _wkv_fwd(*wkv_in, "wkv_fwd")
    relay_weights(1, y_h, "proj")
    relay_weights(2, relay_tokens[1], "ffn")
    ya = _head_post(y_h, r_h, k2_h, v_h, g_h, hp, "head_post")
    yb = _sgu_fwd(proj, b_z, sgu_ln_g, sgu_ln_b, ws, bexp, "sgu_fwd")
    take_weights(1, ya, "proj")
    pa = _matmul(ya, full["w_proj_rwkv"], mode="nn", out_dtype=F32, name="proj_a")
    pb = _matmul(yb, full["w_proj_sgu"], mode="nn", out_dtype=F32, name="proj_b")

    def merge_fn(rv, pv):
        ga_v, gb_v, pa_v, pb_v = rv
        return [_sigmoid(ga_v) * pa_v + _sigmoid(gb_v) * pb_v], []
    merged = _rowwise(merge_fn, [ga, gb, pa, pb], [], [(D, BF16)], [], name="merge")[0]
    h1 = _matmul(merged, full["w_out"], mode="nn", out_dtype=F32, name="out_proj", add=xs)
    n2 = _rms_fwd(h1, norm_ffn_g, "rms_ffn")
    take_weights(2, n2, "ffn")
    gt = _matmul(n2, full["w_ffn_gate"], mode="nn", out_dtype=F32, name="ffn_gate", out_blocks=N_DEV)
    up = _matmul(n2, full["w_ffn_up"], mode="nn", out_dtype=F32, name="ffn_up", out_blocks=N_DEV)
    fb = gt.shape[2]
    flat = lambda t: t.reshape(N_DEV * T, fb)
    blocked = lambda t: t.reshape(N_DEV, T, fb)

    def act_fn(rv, pv):
        gt_v, up_v = rv
        return [gt_v * _sigmoid(gt_v) * up_v], []
    act = blocked(_rowwise(act_fn, [flat(gt), flat(up)], [], [(fb, BF16)], [], name="ffn_act", tm=1024)[0])
    h2 = _matmul(act, full["w_ffn_down"], mode="nn", out_dtype=F32, name="ffn_down", add=h1)

    def final_fn(rv, pv):
        (h_v, t_v), (g_v,) = rv, pv
        r = lax.rsqrt(_mean(h_v * h_v) + RMS_EPS)
        yn = h_v * r
        e = yn * g_v - t_v
        loss = 0.5 * jnp.sum(_mean(e * e))
        dout = e * (1.0 / D)
        dyg = dout * g_v
        dh = r * (dyg - yn * _mean(dyg * yn))
        return [dh, dh], [jnp.full((1, LANE), loss, F32), _colsum(dout * yn)]
    dh2, dh2_bf, loss_part, d_gf = _rowwise(final_fn, [h2, tgt], [gf], [(D, F32), (D, BF16)], [(1, LANE), (1, D)], name="final_loss")

    grads = {}

    def start_scatter(group, name, extra=()):
        blocks = [(grads[n].reshape(N_DEV, -1, grads[n].shape[1]) if n in row_sharded else grads[n], False) for n in group]
        (handle,), token = _exchange_start([blocks + list(extra)], name)
        return handle, token

    dact = _matmul(dh2_bf, full["w_ffn_down"], mode="nt", out_dtype=F32, name="d_act", out_blocks=N_DEV)
    grads["w_ffn_down"] = _matmul(act, dh2_bf, mode="tn", out_dtype=BF16, name="dw_ffn_down")

    def dact_fn(rv, pv):
        d_v, gt_v, up_v = rv
        s = _sigmoid(gt_v)
        return [d_v * up_v * (s * (1.0 + gt_v * (1.0 - s))), d_v * gt_v * s], []
    dgt, dup = (blocked(t) for t in _rowwise(dact_fn, [flat(dact), flat(gt), flat(up)], [], [(fb, BF16)] * 2, [],
                                             name="d_ffn_act", tm=1024))
    dn2 = _matmul(dgt, full["w_ffn_gate"], mode="nt", out_dtype=F32, name="dn2_gate")
    dn2 = _matmul(dup, full["w_ffn_up"], mode="nt", out_dtype=F32, name="dn2_up", add=dn2)
    grads["w_ffn_gate"] = _matmul(n2, dgt, mode="tn", out_dtype=BF16, name="dw_ffn_gate", out_blocks=N_DEV)
    grads["w_ffn_up"] = _matmul(n2, dup, mode="tn", out_dtype=BF16, name="dw_ffn_up", out_blocks=N_DEV)
    scatter_groups = [["w_ffn_down", "w_ffn_gate", "w_ffn_up"], ["w_out", "w_proj_rwkv", "w_proj_sgu"],
                      ["w_in", "w_lora_up", "a_lora_up", "g_lora_up"]]
    scatter_ffn, token_ffn = start_scatter(scatter_groups[0], "scatter_start_ffn")
    dh1, dh1_bf, d_g2 = _rms_bwd(dn2, h1, dh2, norm_ffn_g, "rms_ffn_bwd", deps=[token_ffn])
    dmerged = _matmul(dh1_bf, full["w_out"], mode="nt", out_dtype=F32, name="d_merged")
    grads["w_out"] = _matmul(merged, dh1_bf, mode="tn", out_dtype=BF16, name="dw_out")

    def dmerge_fn(rv, pv):
        d_v, ga_v, gb_v, pa_v, pb_v = rv
        sa, sb = _sigmoid(ga_v), _sigmoid(gb_v)
        dgates = jnp.concatenate([d_v * pa_v * sa * (1.0 - sa), d_v * pb_v * sb * (1.0 - sb)], axis=1)
        return [dgates, d_v * sa, d_v * sb], []
    dproj, dpa, dpb = _rowwise(dmerge_fn, [dmerged, ga, gb, pa, pb], [],
                               [(2 * D, BF16, icp, b_ga // 2, None), (D, BF16), (D, BF16)], [], name="d_merge")
    dya = _matmul(dpa, full["w_proj_rwkv"], mode="nt", out_dtype=F32, name="d_ya")
    dyb = _matmul(dpb, full["w_proj_sgu"], mode="nt", out_dtype=F32, name="d_yb")
    grads["w_proj_rwkv"] = _matmul(ya, dpa, mode="tn", out_dtype=BF16, name="dw_proj_a", out_blocks=N_DEV)
    grads["w_proj_sgu"] = _matmul(yb, dpb, mode="tn", out_dtype=BF16, name="dw_proj_b", out_blocks=N_DEV)
    scatter_mid, token_mid = start_scatter(scatter_groups[1], "scatter_start_mid")
    dproj, d_lng, d_lnb, d_ws, d_bs = _sgu_bwd(proj, b_z, dyb, sgu_ln_g, sgu_ln_b, ws, bexp, dproj, "sgu_bwd")

    dy_h, dr1, dk1, dv1, dg_h, d_lnxg, d_lnxb, d_rk = _head_post_bwd(dya, y_h, r_h, k2_h, v_h, g_h, hp, "head_post_bwd",
                                                                     deps=[token_mid])
    dr2, dlw_h, dk2b, dv2, daa, dbb = _wkv_bwd(*wkv_in, states, dy_h, "wkv_bwd")
    dps, d_mu, d_w0, d_a0, d_kk, d_ka, d_wlw, d_wla, d_wlg = _rwkv_pre_bwd(
        proj, mu_p, rsmall, lora, [dr1, dr2, dk1, dk2b, dv1, dv2, dlw_h, daa, dbb, dg_h], lay, "rwkv_pre_bwd")
    dproj = _shift_bwd(dps, mu_p, dproj, "shift_bwd")
    split = lambda g: g.reshape(g.shape[0], N_DEV, -1).transpose(1, 0, 2)
    grads["w_in"] = _dw_in_from_proj(_matmul(n1, dproj, mode="tn", out_dtype=BF16, name="dw_in"), lay, D, w_in.shape[2], "dw_in_layout")
    grads["w_lora_up"] = split(d_wlw[:w_lora_up.shape[1]].astype(BF16))
    grads["a_lora_up"] = split(d_wla[:a_lora_up.shape[1]].astype(BF16))
    grads["g_lora_up"] = split(d_wlg[:g_lora_up.shape[1]].astype(BF16))
    swap, token_swap = _sibling_swap([grads[n] for n in scatter_groups[2]], None, None, "scatter_in_swap_start")
    dn1 = _matmul(dproj, W_in, mode="nt", out_dtype=F32, name="dn1", deps=[token_swap])
    swap = _sibling_swap(None, swap, dn1, "scatter_in_swap_wait")
    core = lax.axis_index("c").astype(jnp.int32).reshape(1)
    chip_sums = [_pair_add(mine, theirs, core, "scatter_in_add_" + n)
                 for n, mine, theirs in zip(scatter_groups[2], swap["srcs"], swap["lands"])]
    (scatter_in,), token_in = _exchange_start([[(s, False) for s in chip_sums]], "scatter_start_in", rels=SAME_CORE, chips=True)
    dx, _, d_g1 = _rms_bwd(dn1, xs, dh1, norm_mix_g, "rms_mix_bwd", deps=[token_in])
    small_grads = dict(norm_mix_g=d_g1, shift_mu=_unpad_rwkv_cols(d_mu, lay), w0=d_w0, a0=d_a0, k_k=d_kk, k_a=d_ka, r_k=d_rk,
                       lnx_g=d_lnxg, lnx_b=d_lnxb, sgu_ln_g=d_lng, sgu_ln_b=d_lnb, sgu_w=d_ws, sgu_b=d_bs[:, :G].T,
                       norm_ffn_g=d_g2, norm_final_g=d_gf)

    (gather_small,), after = _exchange_start([[(_pack([small_grads[n] for n in small]), True)]], "gather_small_start")
    out = {}
    for group, handle, name in zip(scatter_groups, (scatter_ffn, scatter_mid, scatter_in), ("ffn", "mid", "in")):
        parts = _exchange_wait(handle, after, "scatter_wait_" + name, rels=SAME_CORE if handle["chips"] else ALL_PEERS)["lands"]
        for n, part in zip(group, parts):
            shp = weights[n].shape
            res = _adamw(weights[n][0], m_in[n][0], v_in[n][0], part, "adamw_" + n)
            out[n] = [t.reshape(shp) for t in res]
            after = res[0]
    packed = [_pack([d[n] for n in small]) for d in (weights, m_in, v_in)]
    small_parts = _exchange_wait(gather_small, after, "gather_small_wait")["lands"][0]
    res = _adamw(*packed, small_parts, "adamw_small")
    unpacked = [_unpack(t, [weights[n].shape for n in small]) for t in res]
    for i, n in enumerate(small):
        out[n] = [u[i] for u in unpacked]

    loss = lax.psum(loss_part[0, 0], ("x", "y", "c"))
    return (loss, dx[None], *[out[n][0] for n in names], *[out[n][1] for n in names],
            *[out[n][2] for n in names], *[out[n][3] for n in names])
```

```python
import jax
import jax.numpy as jnp
from jax import lax
from jax.experimental import pallas as pl
from jax.experimental.pallas import tpu as pltpu

F32 = jnp.float32
BF16 = jnp.bfloat16

N_DEV = 8
LANE = 128
SUBLANE = 8
HEAD = 64
SGU_CHUNK = 128
SGU_GROUP = 128
WKV_CHUNK = 64
RMS_EPS = 1e-6
LN_EPS = 1e-5
LNX_EPS = 64e-5
ADAM_LR, ADAM_B1, ADAM_B2, ADAM_EPS, ADAM_WD, ADAM_STEP = 0.001, 0.9, 0.999, 1e-08, 0.01, 10
VMEM_LIMIT_BYTES = 48 * 1024 * 1024
_SQRT_HALF = 0.7071067811865476
_INV_SQRT_2PI = 0.3989422804014327


def _pick(n, cands):
    for c in cands:
        if n % c == 0:
            return c
    return n


def _ceil_to(n, m):
    return -(-n // m) * m


def _params():
    return pltpu.CompilerParams(vmem_limit_bytes=VMEM_LIMIT_BYTES)


def _tile(n, cap):
    best = 0
    for d in range(LANE, min(n, cap) + 1, LANE):
        if n % d == 0:
            best = d
    return best or n


def _matmul_tiles(M, N, K, a_bytes, b_bytes, o_bytes, has_add, forced):
    tm = forced.get("m") or _tile(M, 1024)
    tn = forced.get("n") or _tile(N, 1024)
    tk = forced.get("k") or _tile(K, 2048)

    def vmem(tm, tn, tk):
        acc = tm * tn * 4 if tk < K else 0
        return 2 * (tm * tk * a_bytes + tk * tn * b_bytes + tm * tn * (o_bytes + (4 if has_add else 0))) + acc

    while vmem(tm, tn, tk) > (VMEM_LIMIT_BYTES * 3) // 4:
        if "k" not in forced and tk > 512 and _tile(K, tk // 2) < tk:
            tk = _tile(K, tk // 2)
        elif "m" not in forced and _tile(M, tm // 2) < tm:
            tm = _tile(M, tm // 2)
        else:
            break
    return tm, tn, tk


def _matmul(a, b, *, mode, out_dtype=F32, name, add=None, deps=(), out_blocks=0, epi=None):
    def view(x):
        return (x.shape[1], x.shape[0] * x.shape[2], x.shape[2]) if x.ndim == 3 else (x.shape[0], x.shape[1], 0)

    (ar, ac, aw), (br, bc, bw) = view(a), view(b)
    a_col, b_col = {"nn": ("k", "n"), "nt": ("k", "k"), "tn": ("m", "n")}[mode]
    if mode == "nn":
        M, K, K2, N = ar, ac, br, bc
    elif mode == "nt":
        M, K, N, K2 = ar, ac, br, bc
    else:
        K, M, K2, N = ar, ac, br, bc
    assert K == K2, (a.shape, b.shape, mode)
    forced = {}
    for dim, w in ((a_col, aw), (b_col, bw), ("n", N // out_blocks if out_blocks else 0)):
        if w:
            assert forced.get(dim, w) == w
            forced[dim] = w
    has_add = add is not None
    tile_bytes = (sum(jnp.dtype(d).itemsize for d in epi[2]) + sum((e[0] if isinstance(e, tuple) else e).dtype.itemsize for e in epi[1])
                  if epi is not None else jnp.dtype(out_dtype).itemsize)
    tm, tn, tk = _matmul_tiles(M, N, K, a.dtype.itemsize, b.dtype.itemsize, tile_bytes, has_add, forced)
    nk = K // tk
    dn = {"nn": (((1,), (0,)), ((), ())), "nt": (((1,), (1,)), ((), ())), "tn": (((0,), (0,)), ((), ()))}[mode]
    pick = {"m": lambda i, j, k: i, "n": lambda i, j, k: j, "k": lambda i, j, k: k}
    size = {"m": tm, "n": tn, "k": tk}

    def spec(blocked, row_dim, col_dim):
        rf, cf = pick[row_dim], pick[col_dim]
        if blocked:
            return pl.BlockSpec((None, size[row_dim], size[col_dim]), lambda i, j, k: (cf(i, j, k), rf(i, j, k), 0))
        return pl.BlockSpec((size[row_dim], size[col_dim]), lambda i, j, k: (rf(i, j, k), cf(i, j, k)))

    a_spec = spec(aw, "k" if mode == "tn" else "m", a_col)
    b_spec = spec(bw, "n" if mode == "nt" else "k", b_col)
    o_spec = spec(out_blocks, "m", "n")
    epi_fn, epi_ins, epi_dtypes = epi if epi is not None else (None, [], [out_dtype])
    epi_ins = [e if isinstance(e, tuple) else (e, None) for e in epi_ins]
    n_epi = len(epi_ins)
    n_in = 2 + has_add + n_epi + len(deps)
    n_out = len(epi_dtypes)

    def body(*refs):
        a_ref, b_ref = refs[0], refs[1]
        add_ref = refs[2] if has_add else None
        epi_refs = refs[2 + has_add:2 + has_add + n_epi]
        o_refs = refs[n_in:n_in + n_out]
        part = lax.dot_general(a_ref[...].astype(BF16), b_ref[...].astype(BF16), dn, preferred_element_type=F32)

        def finish(res):
            outs = epi_fn(res, *[e[...] for e in epi_refs]) if epi_fn is not None else (res,)
            for o_ref, val in zip(o_refs, outs):
                o_ref[...] = val.astype(o_ref.dtype)

        if nk == 1:
            finish(part + add_ref[...] if has_add else part)
            return
        acc_ref = refs[-1]
        kk = pl.program_id(2)

        @pl.when(kk == 0)
        def _():
            acc_ref[...] = part + add_ref[...] if has_add else part

        @pl.when(kk > 0)
        def _():
            acc_ref[...] += part

        @pl.when(kk == nk - 1)
        def _():
            finish(acc_ref[...])

    def epi_spec(arr, off):
        if off is None:
            return o_spec
        assert off % tn == 0
        return pl.BlockSpec((tm, tn), lambda i, j, k: (i, j + off // tn))

    ins = [a, b] + ([add] if has_add else []) + [arr for arr, _ in epi_ins] + list(deps)
    in_specs = ([a_spec, b_spec] + ([o_spec] if has_add else []) + [epi_spec(arr, off) for arr, off in epi_ins]
                + [pl.BlockSpec(d.shape, lambda i, j, k, nd=d.ndim: (0,) * nd) for d in deps])
    o_shape = (out_blocks, M, tn) if out_blocks else (M, N)
    res = pl.pallas_call(
        body, name=name, grid=(M // tm, N // tn, nk), in_specs=in_specs, out_specs=[o_spec] * n_out,
        out_shape=[jax.ShapeDtypeStruct(o_shape, dt) for dt in epi_dtypes],
        scratch_shapes=[pltpu.VMEM((tm, tn), F32)] if nk > 1 else [],
        compiler_params=_params())(*ins)
    return res[0] if epi is None else list(res)


def _rowwise(fn, rows, pars, row_outs, acc_outs, *, name, tm=256, deps=()):
    rows = [r if isinstance(r, tuple) else (r, r.shape[1], 0) for r in rows]
    row_outs = [o if len(o) == 5 else (o[0], o[1], o[0], 0, None) for o in row_outs]
    aliased = [(k, o[4]) for k, o in enumerate(row_outs) if o[4] is not None]
    R = rows[0][0].shape[0]
    if max(w for _, w, _ in rows) > 4096:
        tm = tm // 2
    tm = min(tm, R)
    assert R % tm == 0
    nr, npar = len(rows), len(pars)
    nro = len(row_outs)
    n_in = nr + npar + len(deps) + len(aliased)

    def body(*refs):
        rv = [r[...] for r in refs[:nr]]
        pv = [p[...] for p in refs[nr:nr + npar]]
        outs = refs[n_in:]
        ro, ao = fn(rv, pv)
        first = pl.program_id(0) == 0
        for o_ref, val in zip(outs[:nro], ro):
            o_ref[...] = val.astype(o_ref.dtype)

        @pl.when(first)
        def _():
            for o_ref, val in zip(outs[nro:], ao):
                o_ref[...] = val

        @pl.when(jnp.logical_not(first))
        def _():
            for o_ref, val in zip(outs[nro:], ao):
                o_ref[...] += val

    in_specs = ([pl.BlockSpec((tm, w), lambda i, cb=cb: (i, cb)) for _, w, cb in rows]
                + [pl.BlockSpec(p.shape, lambda i, nd=p.ndim: (0,) * nd) for p in list(pars) + list(deps)]
                + [pl.BlockSpec(memory_space=pl.ANY)] * len(aliased))
    out_shape = ([jax.ShapeDtypeStruct((R, full), dt) for _, dt, full, _, _ in row_outs]
                 + [jax.ShapeDtypeStruct(s, F32) for s in acc_outs])
    out_specs = ([pl.BlockSpec((tm, f), lambda i, cb=cb: (i, cb)) for f, _, _, cb, _ in row_outs]
                 + [pl.BlockSpec(s, lambda i, nd=len(s): (0,) * nd) for s in acc_outs])
    res = pl.pallas_call(body, name=name, grid=(R // tm,), in_specs=in_specs, out_specs=out_specs, out_shape=out_shape,
                         input_output_aliases={n_in - len(aliased) + q: k for q, (k, _) in enumerate(aliased)},
                         compiler_params=_params())(*[r for r, _, _ in rows], *pars, *deps, *[buf for _, buf in aliased])
    return list(res)


def _bdot(a, b, mode="nn"):
    dn = {"nn": (((1,), (0,)), ((), ())), "nt": (((1,), (1,)), ((), ())), "tn": (((0,), (0,)), ((), ()))}[mode]
    return lax.dot_general(a.astype(BF16), b.astype(BF16), dn, preferred_element_type=F32)


def _sigmoid(x):
    return jax.nn.sigmoid(x)


def _softplus(x):
    return jnp.maximum(x, 0.0) + jnp.log1p(jnp.exp(-jnp.abs(x)))


def _gelu(z):
    return 0.5 * z * (1.0 + lax.erf(z * _SQRT_HALF))


def _gelu_grad(z):
    return 0.5 * (1.0 + lax.erf(z * _SQRT_HALF)) + z * jnp.exp(-0.5 * z * z) * _INV_SQRT_2PI


def _mean(x):
    return jnp.mean(x, axis=-1, keepdims=True)


def _colsum(x):
    return jnp.sum(x, axis=0, keepdims=True)


def _rms_fwd(x, g, name, deps=()):
    def fn(rv, pv):
        (xv,), (gv,) = rv, pv
        r = lax.rsqrt(_mean(xv * xv) + RMS_EPS)
        return [xv * r * gv], []
    return _rowwise(fn, [x], [g], [(x.shape[1], BF16)], [], name=name, deps=deps)[0]


def _rms_bwd(dn, x, dres, g, name, deps=()):
    def fn(rv, pv):
        (dnv, xv, drv), (gv,) = rv, pv
        r = lax.rsqrt(_mean(xv * xv) + RMS_EPS)
        yn = xv * r
        dyg = dnv * gv
        dx = drv + r * (dyg - yn * _mean(dyg * yn))
        return [dx, dx], [_colsum(dnv * yn)]
    D = x.shape[1]
    return _rowwise(fn, [dn, x, dres], [g], [(D, F32), (D, BF16)], [(1, D)], name=name, deps=deps)


def _rwkv_layout(RW, Lw, La, Lg, D):
    widths = [RW, RW, RW, Lw, La, Lg]
    pw = [_ceil_to(w, LANE) for w in widths]
    pw[5] += _ceil_to(sum(pw), 2 * D) - sum(pw)
    offs = [sum(pw[:i]) for i in range(6)]
    return widths, pw, offs, sum(pw)


def _pad_rwkv_cols(a, lay):
    widths, pw, _, _ = lay
    pieces, src = [], 0
    for w, p in zip(widths, pw):
        pieces.append(a[:, src:src + w])
        if p > w:
            pieces.append(jnp.zeros((a.shape[0], p - w), a.dtype))
        src += w
    return jnp.concatenate(pieces, axis=1)


def _unpad_rwkv_cols(a, lay):
    widths, _, offs, _ = lay
    return jnp.concatenate([a[:, o:o + w] for o, w in zip(offs, widths)], axis=1)


def _proj_pieces(lay, D, cs):
    widths, _, offs, rcp = lay
    rc = sum(widths)
    segs = [(sum(widths[:j]), widths[j], offs[j]) for j in range(6)] + [(rc, D, rcp + 2 * D), (rc + D, D, rcp), (rc + 2 * D, D, rcp + D)]
    pieces = []
    for start, width, dst in segs:
        n = start
        while n < start + width:
            d, off = divmod(n, cs)
            take = min(cs - off, start + width - n)
            pieces.append((d, off, dst + n - start, take))
            n += take
    return pieces


def _w_in_to_proj(g, lay, D, name):
    nb, rows, cs = g.shape
    icp = lay[3] + 3 * D
    pieces = _proj_pieces(lay, D, cs)
    tm = _pick(rows, (256, 128, 64, 32, 16))

    def body(i_ref, o_ref):
        o_ref[...] = jnp.zeros_like(o_ref)
        for d, src, dst, w in pieces:
            o_ref[:, dst:dst + w] = i_ref[d, :, src:src + w]

    return pl.pallas_call(
        body, name=name, grid=(rows // tm,), in_specs=[pl.BlockSpec((nb, tm, cs), lambda i: (0, i, 0))],
        out_specs=pl.BlockSpec((tm, icp), lambda i: (i, 0)), out_shape=jax.ShapeDtypeStruct((rows, icp), g.dtype),
        compiler_params=_params())(g)


def _dw_in_from_proj(a, lay, D, cs, name):
    rows, icp = a.shape
    pieces = _proj_pieces(lay, D, cs)
    tm = _pick(rows, (256, 128, 64, 32, 16))

    def body(i_ref, o_ref):
        for d, src, dst, w in pieces:
            o_ref[d, :, src:src + w] = i_ref[:, dst:dst + w]

    return pl.pallas_call(
        body, name=name, grid=(rows // tm,), in_specs=[pl.BlockSpec((tm, icp), lambda i: (i, 0))],
        out_specs=pl.BlockSpec((N_DEV, tm, cs), lambda i: (0, i, 0)), out_shape=jax.ShapeDtypeStruct((N_DEV, rows, cs), a.dtype),
        compiler_params=_params())(a)


def _pad_rows(a, rows):
    return a if a.shape[0] == rows else jnp.concatenate([a, jnp.zeros((rows - a.shape[0], a.shape[1]), a.dtype)], axis=0)


def _token_shift(p, halo, mu, i):
    tm = p.shape[0]
    hid = lax.broadcasted_iota(jnp.int32, (SUBLANE, 1), 0)
    before = jnp.sum(jnp.where(hid == SUBLANE - 1, halo, 0.0), axis=0, keepdims=True)
    before = jnp.where(i == 0, 0.0, before)
    rid = lax.broadcasted_iota(jnp.int32, (tm, 1), 0)
    prev = jnp.where(rid == 0, before, pltpu.roll(p, 1, 0))
    d = prev - p
    return p + d * mu, d


def _rwkv_math(ps, w0, a0, k_k, k_a, wlw, wla, wlg, lay):
    _, pw, offs, _ = lay
    r, k, v, xw, xa, xg = (ps[:, offs[j]:offs[j] + pw[j]] for j in range(6))
    tw = jnp.tanh(xw)
    ww = w0 + _bdot(tw, wlw)
    lw = -jnp.exp(-_softplus(-ww) - 0.5)
    a = _sigmoid(a0 + _bdot(xa, wla))
    sg = _sigmoid(xg)
    g = _bdot(sg, wlg)
    return dict(r=r, k=k, v=v, xa=xa, tw=tw, ww=ww, lw=lw, a=a, sg=sg, g=g, kkp=k * k_k, k2=k * (1.0 + (a - 1.0) * k_a))


def _halo_specs(T, tm, width, after):
    hb = tm // SUBLANE
    last = T // SUBLANE - 1
    if after:
        return pl.BlockSpec((SUBLANE, width), lambda i: (jnp.minimum((i + 1) * hb, last), 0))
    return pl.BlockSpec((SUBLANE, width), lambda i: (jnp.maximum(i * hb - 1, 0), 0))


def _rowsum(x):
    return jnp.sum(x, axis=-1, keepdims=True)


def _kk_math(kkp):
    nrm = jnp.sqrt(_rowsum(kkp * kkp))
    inv = 1.0 / jnp.maximum(nrm, 1e-12)
    return nrm, inv, kkp * inv


def _rwkv_pre(p, mu, small, lora, lay, name):
    T, rcp = p.shape[0], lay[3]
    H = lay[0][0] // HEAD
    tm = min(128, T)

    def body(p_ref, ph_ref, mu_ref, w0_ref, a0_ref, kk_ref, ka_ref, wlw_ref, wla_ref, wlg_ref, r_o, lw_o, k2_o, v_o, aa_o, bb_o, g_o):
        ps, _ = _token_shift(p_ref[...], ph_ref[...], mu_ref[...], pl.program_id(0))
        q = _rwkv_math(ps, w0_ref[...], a0_ref[...], kk_ref[...], ka_ref[...], wlw_ref[...], wla_ref[...], wlg_ref[...], lay)
        for h in range(H):
            sl = slice(h * HEAD, (h + 1) * HEAD)
            for o_ref, key in ((r_o, "r"), (lw_o, "lw"), (k2_o, "k2"), (v_o, "v"), (g_o, "g")):
                o_ref[h] = q[key][:, sl]
            _, _, kk = _kk_math(q["kkp"][:, sl])
            aa_o[h] = -kk
            bb_o[h] = kk * q["a"][:, sl]

    whole = lambda arr: pl.BlockSpec(arr.shape, lambda i: (0, 0))
    return pl.pallas_call(
        body, name=name, grid=(T // tm,),
        in_specs=([pl.BlockSpec((tm, rcp), lambda i: (i, 0)), _halo_specs(T, tm, rcp, False), whole(mu)]
                  + [whole(s) for s in small] + [whole(w) for w in lora]),
        out_specs=[pl.BlockSpec((H, tm, HEAD), lambda i: (0, i, 0))] * 7, out_shape=[jax.ShapeDtypeStruct((H, T, HEAD), F32)] * 7,
        compiler_params=_params())(p, p, mu, *small, *lora)


def _rwkv_pre_bwd(p, mu, small, lora, hgrads, lay, name):
    T, rcp = p.shape[0], lay[3]
    widths, pw, offs, _ = lay
    RW = widths[0]
    H = RW // HEAD
    tm = min(128, T)

    def body(p_ref, ph_ref, mu_ref, w0_ref, a0_ref, kk_ref, ka_ref, wlw_ref, wla_ref, wlg_ref,
             dr1, dr2, dk1, dk2b, dv1, dv2, dlw_h, daa, dbb, dg_h,
             dps_ref, dmu_ref, dw0_ref, da0_ref, dkk_ref, dka_ref, dwlw_ref, dwla_ref, dwlg_ref,
             s_dr, s_dk2, s_dv, s_dlw, s_dkkp, s_da, s_dg):
        i = pl.program_id(0)
        ps, dprev = _token_shift(p_ref[...], ph_ref[...], mu_ref[...], i)
        k_k, k_a = kk_ref[...], ka_ref[...]
        q = _rwkv_math(ps, w0_ref[...], a0_ref[...], k_k, k_a, wlw_ref[...], wla_ref[...], wlg_ref[...], lay)
        k, a, lw, ww, tw, sg = q["k"], q["a"], q["lw"], q["ww"], q["tw"], q["sg"]
        for h in range(H):
            sl = slice(h * HEAD, (h + 1) * HEAD)
            s_dr[:, sl] = dr1[h] + dr2[h]
            s_dk2[:, sl] = dk1[h] + dk2b[h]
            s_dv[:, sl] = dv1[h] + dv2[h]
            s_dlw[:, sl] = dlw_h[h]
            s_dg[:, sl] = dg_h[h]
            nrm, inv, kk = _kk_math(q["kkp"][:, sl])
            dbb_h = dbb[h]
            dkk = dbb_h * a[:, sl] - daa[h]
            s_dkkp[:, sl] = jnp.where(nrm > 1e-12, inv * (dkk - kk * _rowsum(dkk * kk)), dkk * inv)
            s_da[:, sl] = dbb_h * kk
        dk2, dkkp, dg = s_dk2[...], s_dkkp[...], s_dg[...]
        dk = dk2 * (1.0 + (a - 1.0) * k_a) + dkkp * k_k
        da = s_da[...] + dk2 * k * k_a
        dpa = da * a * (1.0 - a)
        dww = s_dlw[...] * lw * _sigmoid(-ww)
        dxa = _bdot(dpa, wla_ref[...], "nt")
        dxw = _bdot(dww, wlw_ref[...], "nt") * (1.0 - tw * tw)
        dxg = _bdot(dg, wlg_ref[...], "nt") * sg * (1.0 - sg)
        segs = (s_dr[...], dk, s_dv[...], dxw, dxa, dxg)
        sums = [dmu_ref, dw0_ref, da0_ref, dkk_ref, dka_ref, dwlw_ref, dwla_ref, dwlg_ref]

        @pl.when(i == 0)
        def _():
            for s in sums:
                s[...] = jnp.zeros_like(s)

        for j, seg in enumerate(segs):
            sl = slice(offs[j], offs[j] + pw[j])
            dps_ref[:, sl] = seg
            dmu_ref[:, sl] += _colsum(seg * dprev[:, sl])
        dw0_ref[...] += _colsum(dww)
        da0_ref[...] += _colsum(dpa)
        dkk_ref[...] += _colsum(dkkp * k)
        dka_ref[...] += _colsum(dk2 * k * (a - 1.0))
        dwlw_ref[...] += _bdot(tw, dww, "tn")
        dwla_ref[...] += _bdot(q["xa"], dpa, "tn")
        dwlg_ref[...] += _bdot(sg, dg, "tn")

    whole = lambda arr: pl.BlockSpec(arr.shape, lambda i: (0, 0))
    row = lambda w: pl.BlockSpec((tm, w), lambda i: (i, 0))
    acc_shapes = [(1, rcp), (1, RW), (1, RW), (1, RW), (1, RW)] + [w.shape for w in lora]
    return pl.pallas_call(
        body, name=name, grid=(T // tm,),
        in_specs=([row(rcp), _halo_specs(T, tm, rcp, False), whole(mu)] + [whole(s) for s in small] + [whole(w) for w in lora]
                  + [pl.BlockSpec((H, tm, HEAD), lambda i: (0, i, 0))] * 10),
        out_specs=[row(rcp)] + [pl.BlockSpec(s, lambda i: (0, 0)) for s in acc_shapes],
        out_shape=[jax.ShapeDtypeStruct((T, rcp), F32)] + [jax.ShapeDtypeStruct(s, F32) for s in acc_shapes],
        scratch_shapes=[pltpu.VMEM((tm, RW), F32)] * 7, compiler_params=_params())(p, p, mu, *small, *lora, *hgrads)


def _shift_bwd(dps, mu, dproj, name):
    T, rcp = dps.shape
    tm = min(256, T)
    nt = T // tm

    def body(d_ref, dh_ref, mu_ref, buf_ref, o_ref):
        i = pl.program_id(0)
        d = d_ref[...]
        hid = lax.broadcasted_iota(jnp.int32, (SUBLANE, 1), 0)
        after = jnp.sum(jnp.where(hid == 0, dh_ref[...], 0.0), axis=0, keepdims=True)
        after = jnp.where(i == nt - 1, 0.0, after)
        rid = lax.broadcasted_iota(jnp.int32, (tm, 1), 0)
        nxt = jnp.where(rid == tm - 1, after, pltpu.roll(d, tm - 1, 0))
        mu_v = mu_ref[...]
        o_ref[...] = (d * (1.0 - mu_v) + nxt * mu_v).astype(BF16)

    row = pl.BlockSpec((tm, rcp), lambda i: (i, 0))
    return pl.pallas_call(
        body, name=name, grid=(nt,),
        in_specs=[row, _halo_specs(T, tm, rcp, True), pl.BlockSpec(mu.shape, lambda i: (0, 0)), pl.BlockSpec(memory_space=pl.ANY)],
        out_specs=row, out_shape=jax.ShapeDtypeStruct(dproj.shape, BF16), input_output_aliases={3: 0},
        compiler_params=_params())(dps, dps, mu, dproj)


def _head_post_math(y, r, k2, v, lg, lb, rk):
    yc = y - _mean(y)
    rstd = lax.rsqrt(_mean(yc * yc) + LNX_EPS)
    yn = yc * rstd
    s = _rowsum(r * k2 * rk)
    return yn, rstd, yn * lg + lb + s * v, s


def _head_post(y, r, k2, v, g, hp, name):
    H, T, _ = y.shape
    tm = min(128, T)

    def body(y_ref, r_ref, k_ref, v_ref, g_ref, lg_ref, lb_ref, rk_ref, o_ref):
        _, _, t, _ = _head_post_math(y_ref[...], r_ref[...], k_ref[...], v_ref[...], lg_ref[...], lb_ref[...], rk_ref[...])
        out = (t * g_ref[...]).astype(BF16)
        for h in range(H):
            o_ref[:, h * HEAD:(h + 1) * HEAD] = out[h]

    blk = pl.BlockSpec((H, tm, HEAD), lambda i: (0, i, 0))
    par = pl.BlockSpec((H, 1, HEAD), lambda i: (0, 0, 0))
    return pl.pallas_call(
        body, name=name, grid=(T // tm,), in_specs=[blk] * 5 + [par] * 3, out_specs=pl.BlockSpec((tm, H * HEAD), lambda i: (i, 0)),
        out_shape=jax.ShapeDtypeStruct((T, H * HEAD), BF16), compiler_params=_params())(y, r, k2, v, g, *hp)


def _head_post_bwd(dya, y, r, k2, v, g, hp, name, deps=()):
    H, T, _ = y.shape
    tm = min(128, T)
    hsum = lambda t: jnp.sum(t, axis=1, keepdims=True)

    def body(d_ref, y_ref, r_ref, k_ref, v_ref, g_ref, lg_ref, lb_ref, rk_ref, *rest):
        outs, d_s = rest[len(deps):len(deps) + 8], rest[-1]
        for h in range(H):
            d_s[h] = d_ref[:, h * HEAD:(h + 1) * HEAD]
        d_v, r_v, k_v, v_v, lg, rk = d_s[...], r_ref[...], k_ref[...], v_ref[...], lg_ref[...], rk_ref[...]
        yn, rstd, t, s = _head_post_math(y_ref[...], r_v, k_v, v_v, lg, lb_ref[...], rk)
        dyo = d_v * g_ref[...]
        dyn = dyo * lg
        ds = _rowsum(dyo * v_v)
        vals = (rstd * (dyn - _mean(dyn) - yn * _mean(dyn * yn)), ds * k_v * rk, ds * r_v * rk, dyo * s, d_v * t)
        for o_ref, val in zip(outs[:5], vals):
            o_ref[...] = val
        sums = (hsum(dyo * yn), hsum(dyo), hsum(ds * r_v * k_v))
        first = pl.program_id(0) == 0

        @pl.when(first)
        def _():
            for o_ref, val in zip(outs[5:], sums):
                o_ref[...] = val

        @pl.when(jnp.logical_not(first))
        def _():
            for o_ref, val in zip(outs[5:], sums):
                o_ref[...] += val

    blk = pl.BlockSpec((H, tm, HEAD), lambda i: (0, i, 0))
    par = pl.BlockSpec((H, 1, HEAD), lambda i: (0, 0, 0))
    return pl.pallas_call(
        body, name=name, grid=(T // tm,),
        in_specs=([pl.BlockSpec((tm, H * HEAD), lambda i: (i, 0))] + [blk] * 5 + [par] * 3
                  + [pl.BlockSpec(d.shape, lambda i, nd=d.ndim: (0,) * nd) for d in deps]),
        out_specs=[blk] * 5 + [par] * 3,
        out_shape=[jax.ShapeDtypeStruct((H, T, HEAD), F32)] * 5 + [jax.ShapeDtypeStruct((H, 1, HEAD), F32)] * 3,
        scratch_shapes=[pltpu.VMEM((H, tm, HEAD), F32)], compiler_params=_params())(dya, y, r, k2, v, g, *hp, *deps)


def _bmm(x, y, mode):
    dn = {"nn": (((2,), (1,)), ((0,), (0,))), "nt": (((2,), (2,)), ((0,), (0,))), "tn": (((1,), (1,)), ((0,), (0,)))}[mode]
    (xh, xl), (yh, yl) = _split(x), _split(y)
    dot = lambda p, q: lax.dot_general(p, q, dn, preferred_element_type=F32)
    out = dot(xh, yh)
    if yl is not None:
        out = out + dot(xh, yl)
    if xl is not None:
        out = out + dot(xl, yh)
    return out


def _split(x):
    if isinstance(x, tuple):
        return x
    hi = x.astype(BF16)
    return hi, (x - hi.astype(F32)).astype(BF16)


def _exact(x):
    return x.astype(BF16), None


def _wkv_chunk(r, lw, k, v, a, b):
    hb, C, _ = r.shape
    ti = lax.broadcasted_iota(jnp.int32, (C, C), 0)
    si = lax.broadcasted_iota(jnp.int32, (C, C), 1)
    linc, lstr, eye = (ti >= si).astype(F32), (ti > si).astype(F32), (ti == si).astype(F32)
    lincb = _exact(jnp.broadcast_to(linc, (hb, C, C)))
    lstrb = _exact(jnp.broadcast_to(lstr, (hb, C, C)))
    ones = _exact(jnp.ones_like(v))
    lws = _split(lw)
    ci = _bmm(lincb, lws, "nn")
    cC = jnp.sum(lw, axis=1, keepdims=True)
    gi, ge, gn, gr = jnp.exp(ci), jnp.exp(ci - lw), jnp.exp(-ci), jnp.exp(cC - ci)
    q = dict(At=a * ge, Rt=r * gi, Bt=b * gn, Kt=k * gn, Bh=b * gr, Kh=k * gr)
    s = {key: _split(val) for key, val in q.items()}
    s["v"] = _split(v)
    q["A_ab"] = _bmm(s["At"], s["Bt"], "nt") * lstr
    for key, lhs, rhs, mask in (("A_ak", "At", "Kt", lstr), ("A_rb", "Rt", "Bt", linc), ("A_rk", "Rt", "Kt", linc)):
        q[key] = _bmm(s[lhs], s[rhs], "nt") * mask
        s[key] = _split(q[key])
    Tm = eye + q["A_ab"]
    Pw = _split(q["A_ab"])
    n = 1
    while 2 * n < C:
        Pw = _split(_bmm(Pw, Pw, "nn"))
        Tm = Tm + _bmm(Tm, Pw, "nn")
        n *= 2
    s["Tm"] = _split(Tm)
    gC = jnp.exp(_bmm(lws, ones, "tn"))
    q.update(gi=gi, ge=ge, gn=gn, gr=gr, linc=linc, lstr=lstr, lincb=lincb, lstrb=lstrb, gC=gC, ones=ones, s=s)
    return q


def _wkv_fwd(r, lw, k, v, a, b, name):
    H, T, N = r.shape
    C = min(WKV_CHUNK, T)
    nc = T // C
    hb = _pick(H, (16, 8, 4, 2))

    def body(r_ref, lw_ref, k_ref, v_ref, a_ref, b_ref, y_ref, st_ref, h_ref):
        @pl.when(pl.program_id(1) == 0)
        def _():
            h_ref[...] = jnp.zeros_like(h_ref)

        H0 = h_ref[...]
        st_ref[0] = H0
        q = _wkv_chunk(r_ref[...], lw_ref[...], k_ref[...], v_ref[...], a_ref[...], b_ref[...])
        s = q["s"]
        H0s = _split(H0)
        U = _split(_bmm(s["Tm"], _bmm(s["At"], H0s, "nn") + _bmm(s["A_ak"], s["v"], "nn"), "nn"))
        y_ref[...] = _bmm(s["Rt"], H0s, "nn") + _bmm(s["A_rb"], U, "nn") + _bmm(s["A_rk"], s["v"], "nn")
        h_ref[...] = q["gC"] * H0 + _bmm(s["Bh"], U, "tn") + _bmm(s["Kh"], s["v"], "tn")

    blk = pl.BlockSpec((hb, C, N), lambda h, c: (h, c, 0))
    return pl.pallas_call(
        body, name=name, grid=(H // hb, nc), in_specs=[blk] * 6,
        out_specs=[blk, pl.BlockSpec((1, hb, N, N), lambda h, c: (c, h, 0, 0))],
        out_shape=[jax.ShapeDtypeStruct((H, T, N), F32), jax.ShapeDtypeStruct((nc, H, N, N), F32)],
        scratch_shapes=[pltpu.VMEM((hb, N, N), F32)], compiler_params=_params())(r, lw, k, v, a, b)


def _wkv_bwd(r, lw, k, v, a, b, states, dy, name):
    H, T, N = r.shape
    C = min(WKV_CHUNK, T)
    nc = T // C
    hb = _pick(H, (16, 8, 4, 2))

    def body(r_ref, lw_ref, k_ref, v_ref, a_ref, b_ref, st_ref, dy_ref, dr_ref, dlw_ref, dk_ref, dv_ref, da_ref, db_ref, dh_ref):
        @pl.when(pl.program_id(1) == 0)
        def _():
            dh_ref[...] = jnp.zeros_like(dh_ref)

        dHC = dh_ref[...]
        H0 = st_ref[0]
        q = _wkv_chunk(r_ref[...], lw_ref[...], k_ref[...], v_ref[...], a_ref[...], b_ref[...])
        s, gC = q["s"], q["gC"]
        H0s, dHs, dY = _split(H0), _split(dHC), _split(dy_ref[...])
        U = _split(_bmm(s["Tm"], _bmm(s["At"], H0s, "nn") + _bmm(s["A_ak"], s["v"], "nn"), "nn"))
        dU = _bmm(s["A_rb"], dY, "tn") + _bmm(s["Bh"], dHs, "nn")
        dP = _split(_bmm(s["Tm"], dU, "tn"))
        dv_ref[...] = _bmm(s["A_rk"], dY, "tn") + _bmm(s["Kh"], dHs, "nn") + _bmm(s["A_ak"], dP, "tn")
        dh_ref[...] = _bmm(s["Rt"], dY, "tn") + gC * dHC + _bmm(s["At"], dP, "tn")
        dA_rb = _split(_bmm(dY, U, "nt") * q["linc"])
        dA_rk = _split(_bmm(dY, s["v"], "nt") * q["linc"])
        dA_ab = _split(_bmm(dP, U, "nt") * q["lstr"])
        dA_ak = _split(_bmm(dP, s["v"], "nt") * q["lstr"])
        dRt = _bmm(dY, H0s, "nt") + _bmm(dA_rb, s["Bt"], "nn") + _bmm(dA_rk, s["Kt"], "nn")
        dAt = _bmm(dP, H0s, "nt") + _bmm(dA_ab, s["Bt"], "nn") + _bmm(dA_ak, s["Kt"], "nn")
        dBt = _bmm(dA_ab, s["At"], "tn") + _bmm(dA_rb, s["Rt"], "tn")
        dKt = _bmm(dA_ak, s["At"], "tn") + _bmm(dA_rk, s["Rt"], "tn")
        dBh = _bmm(U, dHs, "nt")
        dKh = _bmm(s["v"], dHs, "nt")
        dr_ref[...] = dRt * q["gi"]
        da_ref[...] = dAt * q["ge"]
        db_ref[...] = dBt * q["gn"] + dBh * q["gr"]
        dk_ref[...] = dKt * q["gn"] + dKh * q["gr"]
        tail = dBh * q["Bh"] + dKh * q["Kh"]
        dci = dRt * q["Rt"] - dBt * q["Bt"] - dKt * q["Kt"] - tail
        dcC = jnp.sum(tail, axis=1, keepdims=True) + _bmm(q["ones"], H0 * dHC * gC, "nt")
        dlw_ref[...] = _bmm(q["lincb"], dci, "tn") + _bmm(q["lstrb"], dAt * q["At"], "tn") + dcC

    blk = pl.BlockSpec((hb, C, N), lambda h, c: (h, nc - 1 - c, 0))
    st = pl.BlockSpec((1, hb, N, N), lambda h, c: (nc - 1 - c, h, 0, 0))
    return pl.pallas_call(
        body, name=name, grid=(H // hb, nc), in_specs=[blk] * 6 + [st, blk], out_specs=[blk] * 6,
        out_shape=[jax.ShapeDtypeStruct((H, T, N), F32)] * 6,
        scratch_shapes=[pltpu.VMEM((hb, N, N), F32)], compiler_params=_params())(r, lw, k, v, a, b, states, dy)


def _sgu_ln(z, SW, lng, lnb):
    ge = _gelu(z)
    u, vv = ge[:, :SW], ge[:, SW:]
    xc = vv - _mean(vv)
    rstd = lax.rsqrt(_mean(xc * xc) + LN_EPS)
    vn = xc * rstd
    return u, vn, rstd, vn * lng + lnb


def _causal(ws_ref, g):
    ti = lax.broadcasted_iota(jnp.int32, (SGU_CHUNK, SGU_CHUNK), 0)
    si = lax.broadcasted_iota(jnp.int32, (SGU_CHUNK, SGU_CHUNK), 1)
    return ti >= si, jnp.where(ti >= si, ws_ref[g], 0.0).astype(BF16)


def _sgu_fwd(proj, zblock, lng, lnb, ws, bexp, name):
    T, SW = proj.shape[0], lng.shape[1]
    G = ws.shape[0]
    tr = min(256, T)
    nch = tr // SGU_CHUNK

    def body(z_ref, lng_ref, lnb_ref, ws_ref, be_ref, o_ref):
        u, _, _, vl = _sgu_ln(z_ref[...], SW, lng_ref[...], lnb_ref[...])
        for g in range(G):
            cs = slice(g * SGU_GROUP, (g + 1) * SGU_GROUP)
            _, wc = _causal(ws_ref, g)
            for n in range(nch):
                rs = slice(n * SGU_CHUNK, (n + 1) * SGU_CHUNK)
                m = jnp.dot(wc, vl[rs, cs].astype(BF16), preferred_element_type=F32) + be_ref[:, cs]
                o_ref[rs, cs] = (u[rs, cs] * m).astype(BF16)

    whole = lambda arr: pl.BlockSpec(arr.shape, lambda i, nd=arr.ndim: (0,) * nd)
    return pl.pallas_call(
        body, name=name, grid=(T // tr,),
        in_specs=[pl.BlockSpec((tr, 2 * SW), lambda i: (i, zblock)), whole(lng), whole(lnb), whole(ws), whole(bexp)],
        out_specs=pl.BlockSpec((tr, SW), lambda i: (i, 0)), out_shape=jax.ShapeDtypeStruct((T, SW), BF16),
        compiler_params=_params())(proj, lng, lnb, ws, bexp)


def _sgu_bwd(proj, zblock, dyb, lng, lnb, ws, bexp, dproj, name):
    T, SW = proj.shape[0], lng.shape[1]
    G = ws.shape[0]
    tr = min(256, T)
    nch = tr // SGU_CHUNK
    nt = T // tr

    def body(z_ref, dy_ref, lng_ref, lnb_ref, ws_ref, be_ref, buf_ref, dz_ref, dlg_ref, dlb_ref, dws_ref, db_ref, du_s, dvl_s, dbacc_s):
        i = pl.program_id(0)
        zv = z_ref[...]
        lng_v = lng_ref[...]
        u, vn, rstd, vl = _sgu_ln(zv, SW, lng_v, lnb_ref[...])

        @pl.when(i == 0)
        def _():
            for s in (dlg_ref, dlb_ref, dws_ref, dbacc_s):
                s[...] = jnp.zeros_like(s)

        for g in range(G):
            cs = slice(g * SGU_GROUP, (g + 1) * SGU_GROUP)
            tri, wc = _causal(ws_ref, g)
            for n in range(nch):
                rs = slice(n * SGU_CHUNK, (n + 1) * SGU_CHUNK)
                blk = vl[rs, cs].astype(BF16)
                m = jnp.dot(wc, blk, preferred_element_type=F32) + be_ref[:, cs]
                dyv = dy_ref[rs, cs]
                du_s[rs, cs] = dyv * m
                dm = dyv * u[rs, cs]
                dvl_s[rs, cs] = _bdot(wc, dm, "tn")
                dws_ref[g] += jnp.where(tri, _bdot(dm, blk, "nt"), 0.0)
                dbacc_s[:, cs] += dm

        dvl = dvl_s[...]
        dlg_ref[...] += _colsum(dvl * vn)
        dlb_ref[...] += _colsum(dvl)
        dvn = dvl * lng_v
        dvv = rstd * (dvn - _mean(dvn) - vn * _mean(dvn * vn))
        gp = _gelu_grad(zv)
        dz_ref[:, :SW] = (du_s[...] * gp[:, :SW]).astype(BF16)
        dz_ref[:, SW:] = (dvv * gp[:, SW:]).astype(BF16)

        @pl.when(i == nt - 1)
        def _():
            lane = lax.broadcasted_iota(jnp.int32, (SGU_CHUNK, LANE), 1)
            out = jnp.zeros((SGU_CHUNK, LANE), F32)
            for g in range(G):
                col = jnp.sum(dbacc_s[:, g * SGU_GROUP:(g + 1) * SGU_GROUP], axis=1, keepdims=True)
                out = jnp.where(lane == g, col, out)
            db_ref[...] = out

    whole = lambda arr: pl.BlockSpec(arr.shape, lambda i, nd=arr.ndim: (0,) * nd)
    acc_shapes = [(1, SW), (1, SW), ws.shape, (SGU_CHUNK, LANE)]
    return pl.pallas_call(
        body, name=name, grid=(nt,),
        in_specs=[pl.BlockSpec((tr, 2 * SW), lambda i: (i, zblock)), pl.BlockSpec((tr, SW), lambda i: (i, 0)),
                  whole(lng), whole(lnb), whole(ws), whole(bexp), pl.BlockSpec(memory_space=pl.ANY)],
        out_specs=([pl.BlockSpec((tr, 2 * SW), lambda i: (i, zblock))]
                   + [pl.BlockSpec(s, lambda i, nd=len(s): (0,) * nd) for s in acc_shapes]),
        out_shape=[jax.ShapeDtypeStruct(dproj.shape, BF16)] + [jax.ShapeDtypeStruct(s, F32) for s in acc_shapes],
        scratch_shapes=[pltpu.VMEM((tr, SW), F32), pltpu.VMEM((tr, SW), F32), pltpu.VMEM((SGU_CHUNK, SW), F32)],
        input_output_aliases={6: 0}, compiler_params=_params())(proj, dyb, lng, lnb, ws, bexp, dproj)


_HBM = pl.BlockSpec(memory_space=pltpu.HBM)
_SEM = pl.BlockSpec(memory_space=pltpu.SEMAPHORE)
_DATAFLOW = pltpu.SideEffectType.DATAFLOW_SIDE_EFFECTING


def _mesh_place(chips=False):
    x, y, c = lax.axis_index("x"), lax.axis_index("y"), lax.axis_index("c")
    return x, y, c, (2 * x + y if chips else 4 * x + 2 * y + c)


def _peer(x, y, c, rel, chips=False):
    px = 1 - x if rel & 4 else x
    py = 1 - y if rel & 2 else y
    pc = 1 - c if rel & 1 else c
    return (px, py, pc), (2 * px + py if chips else 4 * px + 2 * py + pc)


ALL_PEERS = tuple(range(1, N_DEV))
SIBLING = (1,)
SAME_CORE = (2, 4, 6)
SIBLINGS_CORE = (3, 5, 7)


def _exchange_start(groups, name, rels=ALL_PEERS, chips=False):
    flat = [t for g in groups for t in g]
    sizes = [len(g) for g in groups]
    n, ng = len(flat), len(groups)
    srcs = [pltpu.with_memory_space_constraint(a, pltpu.HBM) for a, _ in flat]
    lands = [pltpu.with_memory_space_constraint(lax.empty(((N_DEV,) + a.shape) if isg else a.shape, a.dtype), pltpu.HBM)
             for a, isg in flat]

    def body(*refs):
        ins, lnd, sems, token = refs[:n], refs[n:2 * n], refs[2 * n:2 * n + 3 * ng], refs[-1]
        x, y, c, me = _mesh_place(chips)
        j0 = 0
        for gi, sz in enumerate(sizes):
            for rel in rels:
                dev, slot = _peer(x, y, c, rel, chips)
                for jj in range(sz):
                    j = j0 + jj
                    pltpu.make_async_remote_copy(
                        src_ref=ins[j] if flat[j][1] else ins[j].at[slot], dst_ref=lnd[j].at[me],
                        send_sem=sems[3 * gi].at[jj * (N_DEV - 1) + rel - 1], recv_sem=sems[3 * gi + 1].at[jj * (N_DEV - 1) + rel - 1],
                        device_id=dev, device_id_type=pl.DeviceIdType.MESH).start()
            for jj in range(sz):
                j = j0 + jj
                pltpu.make_async_copy(ins[j] if flat[j][1] else ins[j].at[me], lnd[j].at[me], sems[3 * gi + 2].at[jj]).start()
            j0 += sz
        token[...] = jnp.zeros_like(token)

    sem_shapes = [pltpu.SemaphoreType.DMA((k,)) for sz in sizes for k in (sz * (N_DEV - 1), sz * (N_DEV - 1), sz)]
    res = pl.pallas_call(
        body, name=name,
        out_shape=(*sem_shapes, *[pltpu.HBM(a.shape, a.dtype) for a in srcs], *[pltpu.HBM(a.shape, a.dtype) for a in lands],
                   jax.ShapeDtypeStruct((SUBLANE, LANE), F32)),
        in_specs=[_HBM] * (2 * n), out_specs=(*[_SEM] * (3 * ng), *[_HBM] * (2 * n), pl.BlockSpec(memory_space=pltpu.VMEM)),
        input_output_aliases={i: 3 * ng + i for i in range(2 * n)},
        compiler_params=pltpu.CompilerParams(has_side_effects=_DATAFLOW))(*srcs, *lands)
    sems, thru, token = res[:3 * ng], res[3 * ng:3 * ng + 2 * n], res[-1]
    handle, j0 = [], 0
    for gi, sz in enumerate(sizes):
        handle.append(dict(kinds=[k for _, k in groups[gi]], chips=chips, srcs=list(thru[j0:j0 + sz]), lands=list(thru[n + j0:n + j0 + sz]),
                           sems=list(sems[3 * gi:3 * gi + 3])))
        j0 += sz
    return handle, token


def _exchange_wait(group, after, name, rels=ALL_PEERS, local=True):
    kinds, sz = group["kinds"], len(group["kinds"])
    relay = group.get("relay", [])

    def body(*refs):
        ins, lnd, (ssem, rsem, lsem) = refs[:sz], refs[sz:2 * sz], refs[2 * sz:2 * sz + 3]
        x, y, c, me = _mesh_place(group["chips"])
        for rel in rels:
            dev, slot = _peer(x, y, c, rel, group["chips"])
            for jj in range(sz):
                cp = pltpu.make_async_remote_copy(
                    src_ref=ins[jj] if kinds[jj] else ins[jj].at[slot], dst_ref=lnd[jj].at[slot],
                    send_sem=ssem.at[jj * (N_DEV - 1) + rel - 1], recv_sem=rsem.at[jj * (N_DEV - 1) + rel - 1],
                    device_id=dev, device_id_type=pl.DeviceIdType.MESH)
                cp.wait_send()
                cp.wait_recv()
        if local:
            for jj in range(sz):
                pltpu.make_async_copy(ins[jj] if kinds[jj] else ins[jj].at[me], lnd[jj].at[me], lsem.at[jj]).wait()
        if relay:
            fsend, frecv = refs[2 * sz + 3:2 * sz + 5]
            dev = _peer(x, y, c, 1)[0]
            for q, (mine, theirs) in enumerate(zip(SAME_CORE, SIBLINGS_CORE)):
                for jj in range(sz):
                    cp = pltpu.make_async_remote_copy(
                        src_ref=lnd[jj].at[_peer(x, y, c, mine)[1]], dst_ref=lnd[jj].at[_peer(x, y, c, theirs)[1]],
                        send_sem=fsend.at[jj * len(SAME_CORE) + q], recv_sem=frecv.at[jj * len(SAME_CORE) + q],
                        device_id=dev, device_id_type=pl.DeviceIdType.MESH)
                    cp.wait_send()
                    cp.wait_recv()

    arrays = group["srcs"] + group["lands"]
    sems = group["sems"] + relay
    res = pl.pallas_call(
        body, name=name, out_shape=[pltpu.HBM(a.shape, a.dtype) for a in arrays],
        in_specs=[_HBM] * (2 * sz) + [_SEM] * len(sems) + [pl.BlockSpec(memory_space=pl.ANY)], out_specs=[_HBM] * (2 * sz),
        input_output_aliases={i: i for i in range(2 * sz)},
        compiler_params=pltpu.CompilerParams(has_side_effects=_DATAFLOW))(*arrays, *sems, after)
    return dict(group, srcs=list(res[:sz]), lands=list(res[sz:]), relay=[])


def _relay_start(group, name):
    sz = len(group["kinds"])
    nq = len(SAME_CORE)

    def body(*refs):
        lnd, fsend, frecv, token = refs[:sz], refs[sz], refs[sz + 1], refs[-1]
        x, y, c, _ = _mesh_place()
        dev = _peer(x, y, c, 1)[0]
        for q, rel in enumerate(SAME_CORE):
            slot = _peer(x, y, c, rel)[1]
            for jj in range(sz):
                pltpu.make_async_remote_copy(
                    src_ref=lnd[jj].at[slot], dst_ref=lnd[jj].at[slot], send_sem=fsend.at[jj * nq + q], recv_sem=frecv.at[jj * nq + q],
                    device_id=dev, device_id_type=pl.DeviceIdType.MESH).start()
        token[...] = jnp.zeros_like(token)

    lands = group["lands"]
    res = pl.pallas_call(
        body, name=name,
        out_shape=(pltpu.SemaphoreType.DMA((sz * nq,)), pltpu.SemaphoreType.DMA((sz * nq,)), *[pltpu.HBM(a.shape, a.dtype) for a in lands],
                   jax.ShapeDtypeStruct((SUBLANE, LANE), F32)),
        in_specs=[_HBM] * sz, out_specs=(_SEM, _SEM, *[_HBM] * sz, pl.BlockSpec(memory_space=pltpu.VMEM)),
        input_output_aliases={i: 2 + i for i in range(sz)},
        compiler_params=pltpu.CompilerParams(has_side_effects=_DATAFLOW))(*lands)
    return dict(group, lands=list(res[2:2 + sz]), relay=[res[0], res[1]]), res[-1]


def _sibling_swap(arrays, handle, after, name):
    start = handle is None
    n = len(arrays) if start else len(handle["srcs"])
    chips = N_DEV // 2
    if start:
        srcs = [pltpu.with_memory_space_constraint(a.reshape(chips, 2, *a.shape[1:]), pltpu.HBM) for a in arrays]
        lands = [pltpu.with_memory_space_constraint(lax.empty((chips,) + a.shape[1:], a.dtype), pltpu.HBM) for a in arrays]
    else:
        srcs, lands = handle["srcs"], handle["lands"]

    def body(*refs):
        ins, lnd, ssem, rsem = refs[:n], refs[n:2 * n], refs[2 * n], refs[2 * n + 1]
        x, y, c, _ = _mesh_place()
        dev = _peer(x, y, c, 1)[0]
        for q in range(chips):
            for j in range(n):
                cp = pltpu.make_async_remote_copy(
                    src_ref=ins[j].at[q, 1 - c], dst_ref=lnd[j].at[q], send_sem=ssem.at[j * chips + q], recv_sem=rsem.at[j * chips + q],
                    device_id=dev, device_id_type=pl.DeviceIdType.MESH)
                if start:
                    cp.start()
                else:
                    cp.wait_send()
                    cp.wait_recv()
        if start:
            refs[-1][...] = jnp.zeros_like(refs[-1])

    thru = [pltpu.HBM(a.shape, a.dtype) for a in srcs + lands]
    effect = pltpu.CompilerParams(has_side_effects=_DATAFLOW)
    if start:
        res = pl.pallas_call(
            body, name=name, out_shape=(pltpu.SemaphoreType.DMA((n * chips,)), pltpu.SemaphoreType.DMA((n * chips,)), *thru,
                                        jax.ShapeDtypeStruct((SUBLANE, LANE), F32)),
            in_specs=[_HBM] * (2 * n), out_specs=(_SEM, _SEM, *[_HBM] * (2 * n), pl.BlockSpec(memory_space=pltpu.VMEM)),
            input_output_aliases={i: 2 + i for i in range(2 * n)}, compiler_params=effect)(*srcs, *lands)
        return dict(srcs=list(res[2:2 + n]), lands=list(res[2 + n:2 + 2 * n]), sems=[res[0], res[1]]), res[-1]
    res = pl.pallas_call(
        body, name=name, out_shape=thru, in_specs=[_HBM] * (2 * n) + [_SEM, _SEM, pl.BlockSpec(memory_space=pl.ANY)],
        out_specs=[_HBM] * (2 * n), input_output_aliases={i: i for i in range(2 * n)}, compiler_params=effect)(
            *srcs, *lands, *handle["sems"], after)
    return dict(handle, srcs=list(res[:n]), lands=list(res[n:]))


def _pair_add(mine, theirs, core, name):
    chips, _, rows, w = mine.shape
    tm = _pick(rows, (256, 128, 64, 32, 16))

    def body(core_ref, a_ref, b_ref, o_ref):
        o_ref[...] = (a_ref[...].astype(F32) + b_ref[...].astype(F32)).astype(o_ref.dtype)

    return pl.pallas_call(
        body, name=name, out_shape=jax.ShapeDtypeStruct(theirs.shape, theirs.dtype),
        grid_spec=pltpu.PrefetchScalarGridSpec(
            num_scalar_prefetch=1, grid=(chips, rows // tm),
            in_specs=[pl.BlockSpec((None, None, tm, w), lambda q, i, core_ref: (q, core_ref[0], i, 0)),
                      pl.BlockSpec((None, tm, w), lambda q, i, core_ref: (q, i, 0))],
            out_specs=pl.BlockSpec((None, tm, w), lambda q, i, core_ref: (q, i, 0))),
        compiler_params=_params())(core, mine, theirs)


def _adamw(w, m, v, gparts, name):
    R, C = w.shape
    tm = _pick(R, (256, 128, 64, 32, 16, 8))

    def body(w_ref, m_ref, v_ref, g_ref, go, do, mo, vo):
        g = g_ref[0].astype(F32)
        for j in range(1, gparts.shape[0]):
            g = g + g_ref[j].astype(F32)
        mn = ADAM_B1 * m_ref[...] + (1.0 - ADAM_B1) * g
        vn = ADAM_B2 * v_ref[...] + (1.0 - ADAM_B2) * (g * g)
        m_hat = mn / (1.0 - ADAM_B1 ** ADAM_STEP)
        v_hat = vn / (1.0 - ADAM_B2 ** ADAM_STEP)
        go[...] = g
        do[...] = -ADAM_LR * (m_hat / (jnp.sqrt(v_hat) + ADAM_EPS) + ADAM_WD * w_ref[...])
        mo[...] = mn
        vo[...] = vn

    row = pl.BlockSpec((tm, C), lambda i: (i, 0))
    return pl.pallas_call(
        body, name=name, grid=(R // tm,), in_specs=[row, row, row, pl.BlockSpec((gparts.shape[0], tm, C), lambda i: (0, i, 0))],
        out_specs=[row] * 4, out_shape=[jax.ShapeDtypeStruct((R, C), F32)] * 4, compiler_params=_params())(w, m, v, gparts)


def _pack(arrays):
    parts = []
    for a in arrays:
        f = a.reshape(1, -1)
        pad = _ceil_to(f.shape[1], SUBLANE * LANE) - f.shape[1]
        f = jnp.concatenate([f, jnp.zeros((1, pad), f.dtype)], axis=1) if pad else f
        parts.append(f.reshape(-1, LANE))
    rows = sum(p.shape[0] for p in parts)
    pad = _ceil_to(rows, 64) - rows
    return jnp.concatenate(parts + ([jnp.zeros((pad, LANE), parts[0].dtype)] if pad else []), axis=0)


def _unpack(buf, shapes):
    out, row = [], 0
    for s in shapes:
        size = 1
        for d in s:
            size *= d
        rows = _ceil_to(size, SUBLANE * LANE) // LANE
        out.append(buf[row:row + rows].reshape(1, -1)[:, :size].reshape(s))
        row += rows
    return out


def kernel(x, norm_mix_g, w_in, shift_mu, w0, w_lora_up, a0, a_lora_up, g_lora_up, k_k, k_a, r_k, lnx_g, lnx_b, w_proj_rwkv, sgu_ln_g, sgu_ln_b, sgu_w, sgu_b, w_proj_sgu, w_out, norm_ffn_g, w_ffn_gate, w_ffn_up, w_ffn_down, norm_final_g, loss_target, m_norm_mix_g, m_w_in, m_shift_mu, m_w0, m_w_lora_up, m_a0, m_a_lora_up, m_g_lora_up, m_k_k, m_k_a, m_r_k, m_lnx_g, m_lnx_b, m_w_proj_rwkv, m_sgu_ln_g, m_sgu_ln_b, m_sgu_w, m_sgu_b, m_w_proj_sgu, m_w_out, m_norm_ffn_g, m_w_ffn_gate, m_w_ffn_up, m_w_ffn_down, m_norm_final_g, v_norm_mix_g, v_w_in, v_shift_mu, v_w0, v_w_lora_up, v_a0, v_a_lora_up, v_g_lora_up, v_k_k, v_k_a, v_r_k, v_lnx_g, v_lnx_b, v_w_proj_rwkv, v_sgu_ln_g, v_sgu_ln_b, v_sgu_w, v_sgu_b, v_w_proj_sgu, v_w_out, v_norm_ffn_g, v_w_ffn_gate, v_w_ffn_up, v_w_ffn_down, v_norm_final_g):
    weights = dict(norm_mix_g=norm_mix_g, w_in=w_in, shift_mu=shift_mu, w0=w0, w_lora_up=w_lora_up, a0=a0, a_lora_up=a_lora_up,
                   g_lora_up=g_lora_up, k_k=k_k, k_a=k_a, r_k=r_k, lnx_g=lnx_g, lnx_b=lnx_b, w_proj_rwkv=w_proj_rwkv,
                   sgu_ln_g=sgu_ln_g, sgu_ln_b=sgu_ln_b, sgu_w=sgu_w, sgu_b=sgu_b, w_proj_sgu=w_proj_sgu, w_out=w_out,
                   norm_ffn_g=norm_ffn_g, w_ffn_gate=w_ffn_gate, w_ffn_up=w_ffn_up, w_ffn_down=w_ffn_down, norm_final_g=norm_final_g)
    m_in = dict(norm_mix_g=m_norm_mix_g, w_in=m_w_in, shift_mu=m_shift_mu, w0=m_w0, w_lora_up=m_w_lora_up, a0=m_a0,
                a_lora_up=m_a_lora_up, g_lora_up=m_g_lora_up, k_k=m_k_k, k_a=m_k_a, r_k=m_r_k, lnx_g=m_lnx_g, lnx_b=m_lnx_b,
                w_proj_rwkv=m_w_proj_rwkv, sgu_ln_g=m_sgu_ln_g, sgu_ln_b=m_sgu_ln_b, sgu_w=m_sgu_w, sgu_b=m_sgu_b,
                w_proj_sgu=m_w_proj_sgu, w_out=m_w_out, norm_ffn_g=m_norm_ffn_g, w_ffn_gate=m_w_ffn_gate, w_ffn_up=m_w_ffn_up,
                w_ffn_down=m_w_ffn_down, norm_final_g=m_norm_final_g)
    v_in = dict(norm_mix_g=v_norm_mix_g, w_in=v_w_in, shift_mu=v_shift_mu, w0=v_w0, w_lora_up=v_w_lora_up, a0=v_a0,
                a_lora_up=v_a_lora_up, g_lora_up=v_g_lora_up, k_k=v_k_k, k_a=v_k_a, r_k=v_r_k, lnx_g=v_lnx_g, lnx_b=v_lnx_b,
                w_proj_rwkv=v_w_proj_rwkv, sgu_ln_g=v_sgu_ln_g, sgu_ln_b=v_sgu_ln_b, sgu_w=v_sgu_w, sgu_b=v_sgu_b,
                w_proj_sgu=v_w_proj_sgu, w_out=v_w_out, norm_ffn_g=v_norm_ffn_g, w_ffn_gate=v_w_ffn_gate, w_ffn_up=v_w_ffn_up,
                w_ffn_down=v_w_ffn_down, norm_final_g=v_norm_final_g)
    names = list(weights)
    col_sharded = ("w_in", "w_lora_up", "a_lora_up", "g_lora_up", "w_proj_rwkv", "w_proj_sgu", "w_ffn_gate", "w_ffn_up")
    row_sharded = ("w_out", "w_ffn_down")
    sharded = [n for n in names if n in col_sharded or n in row_sharded]
    small = [n for n in names if n not in sharded]

    xs, tgt = x[0], loss_target[0]
    T, D = xs.shape
    RW = w0.shape[1]
    H = RW // HEAD
    SW = sgu_ln_g.shape[1]
    G = sgu_w.shape[1]
    assert 2 * SW == D, "the projection layout takes the SGU part to be as wide as a gate"
    lay = _rwkv_layout(RW, w_lora_up.shape[1], a_lora_up.shape[1], g_lora_up.shape[1], D)
    _, pw, _, rcp = lay
    icp = rcp + 3 * D
    b_ga, b_gb, b_z = rcp // D, rcp // D + 1, rcp // D + 2

    gather_groups = [["w_in", "w_lora_up", "a_lora_up", "g_lora_up"], ["w_proj_rwkv", "w_proj_sgu", "w_out"],
                     ["w_ffn_gate", "w_ffn_up", "w_ffn_down"]]
    gather, gather_token = _exchange_start([[(weights[n][0].astype(BF16), True) for n in grp] for grp in gather_groups],
                                           "gather_start", rels=SIBLING + SAME_CORE)
    full = {}
    relay_tokens = {}
    joined = lambda g: g.transpose(1, 0, 2).reshape(g.shape[1], -1)

    def relay_weights(gi, after, name):
        arrived = _exchange_wait(gather[gi], after, "gather_wait_ici_" + name, rels=SAME_CORE, local=False)
        gather[gi], relay_tokens[gi] = _relay_start(arrived, "gather_relay_" + name)

    def take_weights(gi, after, name):
        done = _exchange_wait(gather[gi], after, "gather_wait_d2d_" + name, rels=SIBLING)
        for n, g in zip(gather_groups[gi], done["lands"]):
            full[n] = g.reshape(-1, g.shape[2]) if n in row_sharded else g

    n1 = _rms_fwd(xs, norm_mix_g, "rms_mix", deps=[gather_token])
    relay_weights(0, n1, "in")
    take_weights(0, relay_tokens[0], "in")
    W_in = _w_in_to_proj(full["w_in"], lay, D, "w_in_layout")
    lora = [_pad_rows(joined(full[n]), rows) for n, rows in zip(("w_lora_up", "a_lora_up", "g_lora_up"), pw[3:])]
    mu_p = _pad_rwkv_cols(shift_mu, lay)
    rsmall = [w0, a0, k_k, k_a]
    hp = [lnx_g.reshape(H, 1, HEAD), lnx_b.reshape(H, 1, HEAD), r_k.reshape(H, 1, HEAD)]
    ws = sgu_w[0]
    bexp = jnp.repeat(sgu_b[0].T, SGU_GROUP, axis=1)
    gf = norm_final_g.reshape(1, D)

    proj = _matmul(n1, W_in, mode="nn", out_dtype=F32, name="proj_in")
    ga, gb = (proj, D, b_ga), (proj, D, b_gb)
    r_h, lw_h, k2_h, v_h, aa_h, bb_h, g_h = _rwkv_pre(proj, mu_p, rsmall, lora, lay, "rwkv_pre")
    wkv_in = [r_h, lw_h, k2_h, v_h, aa_h, bb_h]
    y_h, states = _wkv_fwd(*wkv_in, "wkv_fwd")
    relay_weights(1, y_h, "proj")
    relay_weights(2, relay_tokens[1], "ffn")
    ya = _head_post(y_h, r_h, k2_h, v_h, g_h, hp, "head_post")
    yb = _sgu_fwd(proj, b_z, sgu_ln_g, sgu_ln_b, ws, bexp, "sgu_fwd")
    take_weights(1, ya, "proj")
    pa = _matmul(ya, full["w_proj_rwkv"], mode="nn", out_dtype=F32, name="proj_a")

    def merge_fn(pb_v, pa_v, ga_v, gb_v):
        return pb_v, _sigmoid(ga_v) * pa_v + _sigmoid(gb_v) * pb_v
    pb, merged = _matmul(yb, full["w_proj_sgu"], mode="nn", name="proj_b_merge",
                         epi=(merge_fn, [pa, (proj, b_ga * D), (proj, b_gb * D)], [F32, BF16]))
    h1 = _matmul(merged, full["w_out"], mode="nn", out_dtype=F32, name="out_proj", add=xs)
    n2 = _rms_fwd(h1, norm_ffn_g, "rms_ffn")
    take_weights(2, n2, "ffn")
    gt = _matmul(n2, full["w_ffn_gate"], mode="nn", out_dtype=F32, name="ffn_gate", out_blocks=N_DEV)

    def act_fn(up_v, gt_v):
        return up_v, gt_v * _sigmoid(gt_v) * up_v
    up, act = _matmul(n2, full["w_ffn_up"], mode="nn", name="ffn_up_act", out_blocks=N_DEV, epi=(act_fn, [gt], [F32, BF16]))
    h2 = _matmul(act, full["w_ffn_down"], mode="nn", out_dtype=F32, name="ffn_down", add=h1)

    def final_fn(rv, pv):
        (h_v, t_v), (g_v,) = rv, pv
        r = lax.rsqrt(_mean(h_v * h_v) + RMS_EPS)
        yn = h_v * r
        e = yn * g_v - t_v
        loss = 0.5 * jnp.sum(_mean(e * e))
        dout = e * (1.0 / D)
        dyg = dout * g_v
        dh = r * (dyg - yn * _mean(dyg * yn))
        return [dh, dh], [jnp.full((1, LANE), loss, F32), _colsum(dout * yn)]
    dh2, dh2_bf, loss_part, d_gf = _rowwise(final_fn, [h2, tgt], [gf], [(D, F32), (D, BF16)], [(1, LANE), (1, D)], name="final_loss")

    grads = {}

    def start_scatter(group, name, extra=()):
        blocks = [(grads[n].reshape(N_DEV, -1, grads[n].shape[1]) if n in row_sharded else grads[n], False) for n in group]
        (handle,), token = _exchange_start([blocks + list(extra)], name)
        return handle, token

    def dact_fn(d_v, gt_v, up_v):
        s = _sigmoid(gt_v)
        return d_v * up_v * (s * (1.0 + gt_v * (1.0 - s))), d_v * gt_v * s
    dgt, dup = _matmul(dh2_bf, full["w_ffn_down"], mode="nt", name="d_ffn_act", out_blocks=N_DEV,
                       epi=(dact_fn, [gt, up], [BF16, BF16]))
    grads["w_ffn_down"] = _matmul(act, dh2_bf, mode="tn", out_dtype=BF16, name="dw_ffn_down")
    dn2 = _matmul(dgt, full["w_ffn_gate"], mode="nt", out_dtype=F32, name="dn2_gate")
    dn2 = _matmul(dup, full["w_ffn_up"], mode="nt", out_dtype=F32, name="dn2_up", add=dn2)
    grads["w_ffn_gate"] = _matmul(n2, dgt, mode="tn", out_dtype=BF16, name="dw_ffn_gate", out_blocks=N_DEV)
    grads["w_ffn_up"] = _matmul(n2, dup, mode="tn", out_dtype=BF16, name="dw_ffn_up", out_blocks=N_DEV)
    scatter_groups = [["w_ffn_down", "w_ffn_gate", "w_ffn_up"], ["w_out", "w_proj_rwkv", "w_proj_sgu"],
                      ["w_in", "w_lora_up", "a_lora_up", "g_lora_up"]]
    scatter_ffn, token_ffn = start_scatter(scatter_groups[0], "scatter_start_ffn")
    dh1, dh1_bf, d_g2 = _rms_bwd(dn2, h1, dh2, norm_ffn_g, "rms_ffn_bwd", deps=[token_ffn])
    dmerged = _matmul(dh1_bf, full["w_out"], mode="nt", out_dtype=F32, name="d_merged")
    grads["w_out"] = _matmul(merged, dh1_bf, mode="tn", out_dtype=BF16, name="dw_out")

    def dmerge_fn(rv, pv):
        d_v, ga_v, gb_v, pa_v, pb_v = rv
        sa, sb = _sigmoid(ga_v), _sigmoid(gb_v)
        dgates = jnp.concatenate([d_v * pa_v * sa * (1.0 - sa), d_v * pb_v * sb * (1.0 - sb)], axis=1)
        return [dgates, d_v * sa, d_v * sb], []
    dproj, dpa, dpb = _rowwise(dmerge_fn, [dmerged, ga, gb, pa, pb], [],
                               [(2 * D, BF16, icp, b_ga // 2, None), (D, BF16), (D, BF16)], [], name="d_merge")
    dya = _matmul(dpa, full["w_proj_rwkv"], mode="nt", out_dtype=F32, name="d_ya")
    dyb = _matmul(dpb, full["w_proj_sgu"], mode="nt", out_dtype=F32, name="d_yb")
    grads["w_proj_rwkv"] = _matmul(ya, dpa, mode="tn", out_dtype=BF16, name="dw_proj_a", out_blocks=N_DEV)
    grads["w_proj_sgu"] = _matmul(yb, dpb, mode="tn", out_dtype=BF16, name="dw_proj_b", out_blocks=N_DEV)
    scatter_mid, token_mid = start_scatter(scatter_groups[1], "scatter_start_mid")
    dproj, d_lng, d_lnb, d_ws, d_bs = _sgu_bwd(proj, b_z, dyb, sgu_ln_g, sgu_ln_b, ws, bexp, dproj, "sgu_bwd")

    dy_h, dr1, dk1, dv1, dg_h, d_lnxg, d_lnxb, d_rk = _head_post_bwd(dya, y_h, r_h, k2_h, v_h, g_h, hp, "head_post_bwd",
                                                                     deps=[token_mid])
    dr2, dlw_h, dk2b, dv2, daa, dbb = _wkv_bwd(*wkv_in, states, dy_h, "wkv_bwd")
    dps, d_mu, d_w0, d_a0, d_kk, d_ka, d_wlw, d_wla, d_wlg = _rwkv_pre_bwd(
        proj, mu_p, rsmall, lora, [dr1, dr2, dk1, dk2b, dv1, dv2, dlw_h, daa, dbb, dg_h], lay, "rwkv_pre_bwd")
    dproj = _shift_bwd(dps, mu_p, dproj, "shift_bwd")
    split = lambda g: g.reshape(g.shape[0], N_DEV, -1).transpose(1, 0, 2)
    grads["w_in"] = _dw_in_from_proj(_matmul(n1, dproj, mode="tn", out_dtype=BF16, name="dw_in"), lay, D, w_in.shape[2], "dw_in_layout")
    grads["w_lora_up"] = split(d_wlw[:w_lora_up.shape[1]].astype(BF16))
    grads["a_lora_up"] = split(d_wla[:a_lora_up.shape[1]].astype(BF16))
    grads["g_lora_up"] = split(d_wlg[:g_lora_up.shape[1]].astype(BF16))
    swap, token_swap = _sibling_swap([grads[n] for n in scatter_groups[2]], None, None, "scatter_in_swap_start")
    dn1 = _matmul(dproj, W_in, mode="nt", out_dtype=F32, name="dn1", deps=[token_swap])
    swap = _sibling_swap(None, swap, dn1, "scatter_in_swap_wait")
    core = lax.axis_index("c").astype(jnp.int32).reshape(1)
    chip_sums = [_pair_add(mine, theirs, core, "scatter_in_add_" + n)
                 for n, mine, theirs in zip(scatter_groups[2], swap["srcs"], swap["lands"])]
    (scatter_in,), token_in = _exchange_start([[(s, False) for s in chip_sums]], "scatter_start_in", rels=SAME_CORE, chips=True)
    dx, _, d_g1 = _rms_bwd(dn1, xs, dh1, norm_mix_g, "rms_mix_bwd", deps=[token_in])
    small_grads = dict(norm_mix_g=d_g1, shift_mu=_unpad_rwkv_cols(d_mu, lay), w0=d_w0, a0=d_a0, k_k=d_kk, k_a=d_ka, r_k=d_rk,
                       lnx_g=d_lnxg, lnx_b=d_lnxb, sgu_ln_g=d_lng, sgu_ln_b=d_lnb, sgu_w=d_ws, sgu_b=d_bs[:, :G].T,
                       norm_ffn_g=d_g2, norm_final_g=d_gf)

    (gather_small,), after = _exchange_start([[(_pack([small_grads[n] for n in small]), True)]], "gather_small_start")
    out = {}
    for group, handle, name in zip(scatter_groups, (scatter_ffn, scatter_mid, scatter_in), ("ffn", "mid", "in")):
        parts = _exchange_wait(handle, after, "scatter_wait_" + name, rels=SAME_CORE if handle["chips"] else ALL_PEERS)["lands"]
        for n, part in zip(group, parts):
            shp = weights[n].shape
            res = _adamw(weights[n][0], m_in[n][0], v_in[n][0], part, "adamw_" + n)
            out[n] = [t.reshape(shp) for t in res]
            after = res[0]
    packed = [_pack([d[n] for n in small]) for d in (weights, m_in, v_in)]
    small_parts = _exchange_wait(gather_small, after, "gather_small_wait")["lands"][0]
    res = _adamw(*packed, small_parts, "adamw_small")
    unpacked = [_unpack(t, [weights[n].shape for n in small]) for t in res]
    for i, n in enumerate(small):
        out[n] = [u[i] for u in unpacked]

    loss = lax.psum(loss_part[0, 0], ("x", "y", "c"))
    return (loss, dx[None], *[out[n][0] for n in names], *[out[n][1] for n in names],
            *[out[n][2] for n in names], *[out[n][3] for n in names])
```

```python
import jax
import jax.numpy as jnp
from jax import lax
from jax.experimental import pallas as pl
from jax.experimental.pallas import tpu as pltpu

F32 = jnp.float32
BF16 = jnp.bfloat16

N_DEV = 8
LANE = 128
SUBLANE = 8
HEAD = 64
SGU_CHUNK = 128
SGU_GROUP = 128
WKV_CHUNK = 64
RMS_EPS = 1e-6
LN_EPS = 1e-5
LNX_EPS = 64e-5
ADAM_LR, ADAM_B1, ADAM_B2, ADAM_EPS, ADAM_WD, ADAM_STEP = 0.001, 0.9, 0.999, 1e-08, 0.01, 10
VMEM_LIMIT_BYTES = 48 * 1024 * 1024
_SQRT_HALF = 0.7071067811865476
_INV_SQRT_2PI = 0.3989422804014327


def _pick(n, cands):
    for c in cands:
        if n % c == 0:
            return c
    return n


def _ceil_to(n, m):
    return -(-n // m) * m


def _params():
    return pltpu.CompilerParams(vmem_limit_bytes=VMEM_LIMIT_BYTES)


def _tile(n, cap):
    best = 0
    for d in range(LANE, min(n, cap) + 1, LANE):
        if n % d == 0:
            best = d
    return best or n


def _matmul_tiles(M, N, K, a_bytes, b_bytes, o_bytes, has_add, forced):
    tm = forced.get("m") or _tile(M, 1024)
    tn = forced.get("n") or _tile(N, 1024)
    tk = forced.get("k") or _tile(K, 2048)

    def vmem(tm, tn, tk):
        acc = tm * tn * 4 if tk < K else 0
        return 2 * (tm * tk * a_bytes + tk * tn * b_bytes + tm * tn * (o_bytes + (4 if has_add else 0))) + acc

    while vmem(tm, tn, tk) > (VMEM_LIMIT_BYTES * 3) // 4:
        if "k" not in forced and tk > 512 and _tile(K, tk // 2) < tk:
            tk = _tile(K, tk // 2)
        elif "m" not in forced and _tile(M, tm // 2) < tm:
            tm = _tile(M, tm // 2)
        else:
            break
    return tm, tn, tk


def _matmul(a, b, *, mode, out_dtype=F32, name, add=None, deps=(), out_blocks=0, epi=None):
    def view(x):
        return (x.shape[1], x.shape[0] * x.shape[2], x.shape[2]) if x.ndim == 3 else (x.shape[0], x.shape[1], 0)

    (ar, ac, aw), (br, bc, bw) = view(a), view(b)
    a_col, b_col = {"nn": ("k", "n"), "nt": ("k", "k"), "tn": ("m", "n")}[mode]
    if mode == "nn":
        M, K, K2, N = ar, ac, br, bc
    elif mode == "nt":
        M, K, N, K2 = ar, ac, br, bc
    else:
        K, M, K2, N = ar, ac, br, bc
    assert K == K2, (a.shape, b.shape, mode)
    forced = {}
    for dim, w in ((a_col, aw), (b_col, bw), ("n", N // out_blocks if out_blocks else 0)):
        if w:
            assert forced.get(dim, w) == w
            forced[dim] = w
    has_add = add is not None
    tile_bytes = (sum(jnp.dtype(d).itemsize for d in epi[2]) + sum((e[0] if isinstance(e, tuple) else e).dtype.itemsize for e in epi[1])
                  if epi is not None else jnp.dtype(out_dtype).itemsize)
    tm, tn, tk = _matmul_tiles(M, N, K, a.dtype.itemsize, b.dtype.itemsize, tile_bytes, has_add, forced)
    nk = K // tk
    dn = {"nn": (((1,), (0,)), ((), ())), "nt": (((1,), (1,)), ((), ())), "tn": (((0,), (0,)), ((), ()))}[mode]
    pick = {"m": lambda i, j, k: i, "n": lambda i, j, k: j, "k": lambda i, j, k: k}
    size = {"m": tm, "n": tn, "k": tk}

    def spec(blocked, row_dim, col_dim):
        rf, cf = pick[row_dim], pick[col_dim]
        if blocked:
            return pl.BlockSpec((None, size[row_dim], size[col_dim]), lambda i, j, k: (cf(i, j, k), rf(i, j, k), 0))
        return pl.BlockSpec((size[row_dim], size[col_dim]), lambda i, j, k: (rf(i, j, k), cf(i, j, k)))

    a_spec = spec(aw, "k" if mode == "tn" else "m", a_col)
    b_spec = spec(bw, "n" if mode == "nt" else "k", b_col)
    o_spec = spec(out_blocks, "m", "n")
    epi_fn, epi_ins, epi_dtypes = epi if epi is not None else (None, [], [out_dtype])
    epi_ins = [e if isinstance(e, tuple) else (e, None) for e in epi_ins]
    n_epi = len(epi_ins)
    n_in = 2 + has_add + n_epi + len(deps)
    n_out = len(epi_dtypes)

    def body(*refs):
        a_ref, b_ref = refs[0], refs[1]
        add_ref = refs[2] if has_add else None
        epi_refs = refs[2 + has_add:2 + has_add + n_epi]
        o_refs = refs[n_in:n_in + n_out]
        part = lax.dot_general(a_ref[...].astype(BF16), b_ref[...].astype(BF16), dn, preferred_element_type=F32)

        def finish(res):
            outs = epi_fn(res, *[e[...] for e in epi_refs]) if epi_fn is not None else (res,)
            for o_ref, val in zip(o_refs, outs):
                o_ref[...] = val.astype(o_ref.dtype)

        if nk == 1:
            finish(part + add_ref[...] if has_add else part)
            return
        acc_ref = refs[-1]
        kk = pl.program_id(2)

        @pl.when(kk == 0)
        def _():
            acc_ref[...] = part + add_ref[...] if has_add else part

        @pl.when(kk > 0)
        def _():
            acc_ref[...] += part

        @pl.when(kk == nk - 1)
        def _():
            finish(acc_ref[...])

    def epi_spec(arr, off):
        if off is None:
            return o_spec
        assert off % tn == 0
        return pl.BlockSpec((tm, tn), lambda i, j, k: (i, j + off // tn))

    ins = [a, b] + ([add] if has_add else []) + [arr for arr, _ in epi_ins] + list(deps)
    in_specs = ([a_spec, b_spec] + ([o_spec] if has_add else []) + [epi_spec(arr, off) for arr, off in epi_ins]
                + [pl.BlockSpec(d.shape, lambda i, j, k, nd=d.ndim: (0,) * nd) for d in deps])
    o_shape = (out_blocks, M, tn) if out_blocks else (M, N)
    res = pl.pallas_call(
        body, name=name, grid=(M // tm, N // tn, nk), in_specs=in_specs, out_specs=[o_spec] * n_out,
        out_shape=[jax.ShapeDtypeStruct(o_shape, dt) for dt in epi_dtypes],
        scratch_shapes=[pltpu.VMEM((tm, tn), F32)] if nk > 1 else [],
        compiler_params=_params())(*ins)
    return res[0] if epi is None else list(res)


def _rowwise(fn, rows, pars, row_outs, acc_outs, *, name, tm=256, deps=()):
    rows = [r if isinstance(r, tuple) else (r, r.shape[1], 0) for r in rows]
    row_outs = [o if len(o) == 5 else (o[0], o[1], o[0], 0, None) for o in row_outs]
    aliased = [(k, o[4]) for k, o in enumerate(row_outs) if o[4] is not None]
    R = rows[0][0].shape[0]
    if max(w for _, w, _ in rows) > 4096:
        tm = tm // 2
    tm = min(tm, R)
    assert R % tm == 0
    nr, npar = len(rows), len(pars)
    nro = len(row_outs)
    n_in = nr + npar + len(deps) + len(aliased)

    def body(*refs):
        rv = [r[...] for r in refs[:nr]]
        pv = [p[...] for p in refs[nr:nr + npar]]
        outs = refs[n_in:]
        ro, ao = fn(rv, pv)
        first = pl.program_id(0) == 0
        for o_ref, val in zip(outs[:nro], ro):
            o_ref[...] = val.astype(o_ref.dtype)

        @pl.when(first)
        def _():
            for o_ref, val in zip(outs[nro:], ao):
                o_ref[...] = val

        @pl.when(jnp.logical_not(first))
        def _():
            for o_ref, val in zip(outs[nro:], ao):
                o_ref[...] += val

    in_specs = ([pl.BlockSpec((tm, w), lambda i, cb=cb: (i, cb)) for _, w, cb in rows]
                + [pl.BlockSpec(p.shape, lambda i, nd=p.ndim: (0,) * nd) for p in list(pars) + list(deps)]
                + [pl.BlockSpec(memory_space=pl.ANY)] * len(aliased))
    out_shape = ([jax.ShapeDtypeStruct((R, full), dt) for _, dt, full, _, _ in row_outs]
                 + [jax.ShapeDtypeStruct(s, F32) for s in acc_outs])
    out_specs = ([pl.BlockSpec((tm, f), lambda i, cb=cb: (i, cb)) for f, _, _, cb, _ in row_outs]
                 + [pl.BlockSpec(s, lambda i, nd=len(s): (0,) * nd) for s in acc_outs])
    res = pl.pallas_call(body, name=name, grid=(R // tm,), in_specs=in_specs, out_specs=out_specs, out_shape=out_shape,
                         input_output_aliases={n_in - len(aliased) + q: k for q, (k, _) in enumerate(aliased)},
                         compiler_params=_params())(*[r for r, _, _ in rows], *pars, *deps, *[buf for _, buf in aliased])
    return list(res)


def _bdot(a, b, mode="nn"):
    dn = {"nn": (((1,), (0,)), ((), ())), "nt": (((1,), (1,)), ((), ())), "tn": (((0,), (0,)), ((), ()))}[mode]
    return lax.dot_general(a.astype(BF16), b.astype(BF16), dn, preferred_element_type=F32)


def _sigmoid(x):
    return jax.nn.sigmoid(x)


def _softplus(x):
    return jnp.maximum(x, 0.0) + jnp.log1p(jnp.exp(-jnp.abs(x)))


def _gelu(z):
    return 0.5 * z * (1.0 + lax.erf(z * _SQRT_HALF))


def _gelu_grad(z):
    return 0.5 * (1.0 + lax.erf(z * _SQRT_HALF)) + z * jnp.exp(-0.5 * z * z) * _INV_SQRT_2PI


def _mean(x):
    return jnp.mean(x, axis=-1, keepdims=True)


def _colsum(x):
    return jnp.sum(x, axis=0, keepdims=True)


def _rms_fwd(x, g, name, deps=()):
    def fn(rv, pv):
        (xv,), (gv,) = rv, pv
        r = lax.rsqrt(_mean(xv * xv) + RMS_EPS)
        return [xv * r * gv], []
    return _rowwise(fn, [x], [g], [(x.shape[1], BF16)], [], name=name, deps=deps)[0]


def _rms_bwd(dn, x, dres, g, name, deps=()):
    def fn(rv, pv):
        (dnv, xv, drv), (gv,) = rv, pv
        r = lax.rsqrt(_mean(xv * xv) + RMS_EPS)
        yn = xv * r
        dyg = dnv * gv
        dx = drv + r * (dyg - yn * _mean(dyg * yn))
        return [dx, dx], [_colsum(dnv * yn)]
    D = x.shape[1]
    return _rowwise(fn, [dn, x, dres], [g], [(D, F32), (D, BF16)], [(1, D)], name=name, deps=deps)


def _rwkv_layout(RW, Lw, La, Lg, D):
    widths = [RW, RW, RW, Lw, La, Lg]
    pw = [_ceil_to(w, LANE) for w in widths]
    pw[5] += _ceil_to(sum(pw), 2 * D) - sum(pw)
    offs = [sum(pw[:i]) for i in range(6)]
    return widths, pw, offs, sum(pw)


def _pad_rwkv_cols(a, lay):
    widths, pw, _, _ = lay
    pieces, src = [], 0
    for w, p in zip(widths, pw):
        pieces.append(a[:, src:src + w])
        if p > w:
            pieces.append(jnp.zeros((a.shape[0], p - w), a.dtype))
        src += w
    return jnp.concatenate(pieces, axis=1)


def _unpad_rwkv_cols(a, lay):
    widths, _, offs, _ = lay
    return jnp.concatenate([a[:, o:o + w] for o, w in zip(offs, widths)], axis=1)


def _proj_pieces(lay, D, cs):
    widths, _, offs, rcp = lay
    rc = sum(widths)
    segs = [(sum(widths[:j]), widths[j], offs[j]) for j in range(6)] + [(rc, D, rcp + 2 * D), (rc + D, D, rcp), (rc + 2 * D, D, rcp + D)]
    pieces = []
    for start, width, dst in segs:
        n = start
        while n < start + width:
            d, off = divmod(n, cs)
            take = min(cs - off, start + width - n)
            pieces.append((d, off, dst + n - start, take))
            n += take
    return pieces


def _w_in_to_proj(g, lay, D, name):
    nb, rows, cs = g.shape
    icp = lay[3] + 3 * D
    pieces = _proj_pieces(lay, D, cs)
    tm = _pick(rows, (256, 128, 64, 32, 16))

    def body(i_ref, o_ref):
        o_ref[...] = jnp.zeros_like(o_ref)
        for d, src, dst, w in pieces:
            o_ref[:, dst:dst + w] = i_ref[d, :, src:src + w]

    return pl.pallas_call(
        body, name=name, grid=(rows // tm,), in_specs=[pl.BlockSpec((nb, tm, cs), lambda i: (0, i, 0))],
        out_specs=pl.BlockSpec((tm, icp), lambda i: (i, 0)), out_shape=jax.ShapeDtypeStruct((rows, icp), g.dtype),
        compiler_params=_params())(g)


def _dw_in_from_proj(a, lay, D, cs, name):
    rows, icp = a.shape
    pieces = _proj_pieces(lay, D, cs)
    tm = _pick(rows, (256, 128, 64, 32, 16))

    def body(i_ref, o_ref):
        for d, src, dst, w in pieces:
            o_ref[d, :, src:src + w] = i_ref[:, dst:dst + w]

    return pl.pallas_call(
        body, name=name, grid=(rows // tm,), in_specs=[pl.BlockSpec((tm, icp), lambda i: (i, 0))],
        out_specs=pl.BlockSpec((N_DEV, tm, cs), lambda i: (0, i, 0)), out_shape=jax.ShapeDtypeStruct((N_DEV, rows, cs), a.dtype),
        compiler_params=_params())(a)


def _pad_rows(a, rows):
    return a if a.shape[0] == rows else jnp.concatenate([a, jnp.zeros((rows - a.shape[0], a.shape[1]), a.dtype)], axis=0)


def _token_shift(p, halo, mu, i):
    tm = p.shape[0]
    hid = lax.broadcasted_iota(jnp.int32, (SUBLANE, 1), 0)
    before = jnp.sum(jnp.where(hid == SUBLANE - 1, halo, 0.0), axis=0, keepdims=True)
    before = jnp.where(i == 0, 0.0, before)
    rid = lax.broadcasted_iota(jnp.int32, (tm, 1), 0)
    prev = jnp.where(rid == 0, before, pltpu.roll(p, 1, 0))
    d = prev - p
    return p + d * mu, d


def _rwkv_math(ps, w0, a0, k_k, k_a, wlw, wla, wlg, lay):
    _, pw, offs, _ = lay
    r, k, v, xw, xa, xg = (ps[:, offs[j]:offs[j] + pw[j]] for j in range(6))
    tw = jnp.tanh(xw)
    ww = w0 + _bdot(tw, wlw)
    lw = -jnp.exp(-_softplus(-ww) - 0.5)
    a = _sigmoid(a0 + _bdot(xa, wla))
    sg = _sigmoid(xg)
    g = _bdot(sg, wlg)
    return dict(r=r, k=k, v=v, xa=xa, tw=tw, ww=ww, lw=lw, a=a, sg=sg, g=g, kkp=k * k_k, k2=k * (1.0 + (a - 1.0) * k_a))


def _halo_specs(T, tm, width, after):
    hb = tm // SUBLANE
    last = T // SUBLANE - 1
    if after:
        return pl.BlockSpec((SUBLANE, width), lambda i: (jnp.minimum((i + 1) * hb, last), 0))
    return pl.BlockSpec((SUBLANE, width), lambda i: (jnp.maximum(i * hb - 1, 0), 0))


def _rowsum(x):
    return jnp.sum(x, axis=-1, keepdims=True)


def _kk_math(kkp):
    nrm = jnp.sqrt(_rowsum(kkp * kkp))
    inv = 1.0 / jnp.maximum(nrm, 1e-12)
    return nrm, inv, kkp * inv


def _rwkv_pre(p, mu, small, lora, lay, name):
    T, rcp = p.shape[0], lay[3]
    H = lay[0][0] // HEAD
    tm = min(128, T)

    def body(p_ref, ph_ref, mu_ref, w0_ref, a0_ref, kk_ref, ka_ref, wlw_ref, wla_ref, wlg_ref, r_o, lw_o, k2_o, v_o, aa_o, bb_o, g_o):
        ps, _ = _token_shift(p_ref[...], ph_ref[...], mu_ref[...], pl.program_id(0))
        q = _rwkv_math(ps, w0_ref[...], a0_ref[...], kk_ref[...], ka_ref[...], wlw_ref[...], wla_ref[...], wlg_ref[...], lay)
        for h in range(H):
            sl = slice(h * HEAD, (h + 1) * HEAD)
            for o_ref, key in ((r_o, "r"), (lw_o, "lw"), (k2_o, "k2"), (v_o, "v"), (g_o, "g")):
                o_ref[h] = q[key][:, sl]
            _, _, kk = _kk_math(q["kkp"][:, sl])
            aa_o[h] = -kk
            bb_o[h] = kk * q["a"][:, sl]

    whole = lambda arr: pl.BlockSpec(arr.shape, lambda i: (0, 0))
    return pl.pallas_call(
        body, name=name, grid=(T // tm,),
        in_specs=([pl.BlockSpec((tm, rcp), lambda i: (i, 0)), _halo_specs(T, tm, rcp, False), whole(mu)]
                  + [whole(s) for s in small] + [whole(w) for w in lora]),
        out_specs=[pl.BlockSpec((H, tm, HEAD), lambda i: (0, i, 0))] * 7, out_shape=[jax.ShapeDtypeStruct((H, T, HEAD), F32)] * 7,
        compiler_params=_params())(p, p, mu, *small, *lora)


def _rwkv_pre_bwd(p, mu, small, lora, hgrads, lay, name):
    T, rcp = p.shape[0], lay[3]
    widths, pw, offs, _ = lay
    RW = widths[0]
    H = RW // HEAD
    tm = min(128, T)

    def body(p_ref, ph_ref, mu_ref, w0_ref, a0_ref, kk_ref, ka_ref, wlw_ref, wla_ref, wlg_ref,
             dr1, dr2, dk1, dk2b, dv1, dv2, dlw_h, daa, dbb, dg_h,
             dps_ref, dmu_ref, dw0_ref, da0_ref, dkk_ref, dka_ref, dwlw_ref, dwla_ref, dwlg_ref,
             s_dr, s_dk2, s_dv, s_dlw, s_dkkp, s_da, s_dg):
        i = pl.program_id(0)
        ps, dprev = _token_shift(p_ref[...], ph_ref[...], mu_ref[...], i)
        k_k, k_a = kk_ref[...], ka_ref[...]
        q = _rwkv_math(ps, w0_ref[...], a0_ref[...], k_k, k_a, wlw_ref[...], wla_ref[...], wlg_ref[...], lay)
        k, a, lw, ww, tw, sg = q["k"], q["a"], q["lw"], q["ww"], q["tw"], q["sg"]
        for h in range(H):
            sl = slice(h * HEAD, (h + 1) * HEAD)
            s_dr[:, sl] = dr1[h] + dr2[h]
            s_dk2[:, sl] = dk1[h] + dk2b[h]
            s_dv[:, sl] = dv1[h] + dv2[h]
            s_dlw[:, sl] = dlw_h[h]
            s_dg[:, sl] = dg_h[h]
            nrm, inv, kk = _kk_math(q["kkp"][:, sl])
            dbb_h = dbb[h]
            dkk = dbb_h * a[:, sl] - daa[h]
            s_dkkp[:, sl] = jnp.where(nrm > 1e-12, inv * (dkk - kk * _rowsum(dkk * kk)), dkk * inv)
            s_da[:, sl] = dbb_h * kk
        dk2, dkkp, dg = s_dk2[...], s_dkkp[...], s_dg[...]
        dk = dk2 * (1.0 + (a - 1.0) * k_a) + dkkp * k_k
        da = s_da[...] + dk2 * k * k_a
        dpa = da * a * (1.0 - a)
        dww = s_dlw[...] * lw * _sigmoid(-ww)
        dxa = _bdot(dpa, wla_ref[...], "nt")
        dxw = _bdot(dww, wlw_ref[...], "nt") * (1.0 - tw * tw)
        dxg = _bdot(dg, wlg_ref[...], "nt") * sg * (1.0 - sg)
        segs = (s_dr[...], dk, s_dv[...], dxw, dxa, dxg)
        sums = [dmu_ref, dw0_ref, da0_ref, dkk_ref, dka_ref, dwlw_ref, dwla_ref, dwlg_ref]

        @pl.when(i == 0)
        def _():
            for s in sums:
                s[...] = jnp.zeros_like(s)

        for j, seg in enumerate(segs):
            sl = slice(offs[j], offs[j] + pw[j])
            dps_ref[:, sl] = seg
            dmu_ref[:, sl] += _colsum(seg * dprev[:, sl])
        dw0_ref[...] += _colsum(dww)
        da0_ref[...] += _colsum(dpa)
        dkk_ref[...] += _colsum(dkkp * k)
        dka_ref[...] += _colsum(dk2 * k * (a - 1.0))
        dwlw_ref[...] += _bdot(tw, dww, "tn")
        dwla_ref[...] += _bdot(q["xa"], dpa, "tn")
        dwlg_ref[...] += _bdot(sg, dg, "tn")

    whole = lambda arr: pl.BlockSpec(arr.shape, lambda i: (0, 0))
    row = lambda w: pl.BlockSpec((tm, w), lambda i: (i, 0))
    acc_shapes = [(1, rcp), (1, RW), (1, RW), (1, RW), (1, RW)] + [w.shape for w in lora]
    return pl.pallas_call(
        body, name=name, grid=(T // tm,),
        in_specs=([row(rcp), _halo_specs(T, tm, rcp, False), whole(mu)] + [whole(s) for s in small] + [whole(w) for w in lora]
                  + [pl.BlockSpec((H, tm, HEAD), lambda i: (0, i, 0))] * 10),
        out_specs=[row(rcp)] + [pl.BlockSpec(s, lambda i: (0, 0)) for s in acc_shapes],
        out_shape=[jax.ShapeDtypeStruct((T, rcp), F32)] + [jax.ShapeDtypeStruct(s, F32) for s in acc_shapes],
        scratch_shapes=[pltpu.VMEM((tm, RW), F32)] * 7, compiler_params=_params())(p, p, mu, *small, *lora, *hgrads)


def _shift_bwd(dps, mu, dproj, name):
    T, rcp = dps.shape
    tm = min(256, T)
    nt = T // tm

    def body(d_ref, dh_ref, mu_ref, buf_ref, o_ref):
        i = pl.program_id(0)
        d = d_ref[...]
        hid = lax.broadcasted_iota(jnp.int32, (SUBLANE, 1), 0)
        after = jnp.sum(jnp.where(hid == 0, dh_ref[...], 0.0), axis=0, keepdims=True)
        after = jnp.where(i == nt - 1, 0.0, after)
        rid = lax.broadcasted_iota(jnp.int32, (tm, 1), 0)
        nxt = jnp.where(rid == tm - 1, after, pltpu.roll(d, tm - 1, 0))
        mu_v = mu_ref[...]
        o_ref[...] = (d * (1.0 - mu_v) + nxt * mu_v).astype(BF16)

    row = pl.BlockSpec((tm, rcp), lambda i: (i, 0))
    return pl.pallas_call(
        body, name=name, grid=(nt,),
        in_specs=[row, _halo_specs(T, tm, rcp, True), pl.BlockSpec(mu.shape, lambda i: (0, 0)), pl.BlockSpec(memory_space=pl.ANY)],
        out_specs=row, out_shape=jax.ShapeDtypeStruct(dproj.shape, BF16), input_output_aliases={3: 0},
        compiler_params=_params())(dps, dps, mu, dproj)


def _head_post_math(y, r, k2, v, lg, lb, rk):
    yc = y - _mean(y)
    rstd = lax.rsqrt(_mean(yc * yc) + LNX_EPS)
    yn = yc * rstd
    s = _rowsum(r * k2 * rk)
    return yn, rstd, yn * lg + lb + s * v, s


def _head_post(y, r, k2, v, g, hp, name):
    H, T, _ = y.shape
    tm = min(128, T)

    def body(y_ref, r_ref, k_ref, v_ref, g_ref, lg_ref, lb_ref, rk_ref, o_ref):
        _, _, t, _ = _head_post_math(y_ref[...], r_ref[...], k_ref[...], v_ref[...], lg_ref[...], lb_ref[...], rk_ref[...])
        out = (t * g_ref[...]).astype(BF16)
        for h in range(H):
            o_ref[:, h * HEAD:(h + 1) * HEAD] = out[h]

    blk = pl.BlockSpec((H, tm, HEAD), lambda i: (0, i, 0))
    par = pl.BlockSpec((H, 1, HEAD), lambda i: (0, 0, 0))
    return pl.pallas_call(
        body, name=name, grid=(T // tm,), in_specs=[blk] * 5 + [par] * 3, out_specs=pl.BlockSpec((tm, H * HEAD), lambda i: (i, 0)),
        out_shape=jax.ShapeDtypeStruct((T, H * HEAD), BF16), compiler_params=_params())(y, r, k2, v, g, *hp)


def _head_post_bwd(dya, y, r, k2, v, g, hp, name, deps=()):
    H, T, _ = y.shape
    tm = min(128, T)
    hsum = lambda t: jnp.sum(t, axis=1, keepdims=True)

    def body(d_ref, y_ref, r_ref, k_ref, v_ref, g_ref, lg_ref, lb_ref, rk_ref, *rest):
        outs, d_s = rest[len(deps):len(deps) + 8], rest[-1]
        for h in range(H):
            d_s[h] = d_ref[:, h * HEAD:(h + 1) * HEAD]
        d_v, r_v, k_v, v_v, lg, rk = d_s[...], r_ref[...], k_ref[...], v_ref[...], lg_ref[...], rk_ref[...]
        yn, rstd, t, s = _head_post_math(y_ref[...], r_v, k_v, v_v, lg, lb_ref[...], rk)
        dyo = d_v * g_ref[...]
        dyn = dyo * lg
        ds = _rowsum(dyo * v_v)
        vals = (rstd * (dyn - _mean(dyn) - yn * _mean(dyn * yn)), ds * k_v * rk, ds * r_v * rk, dyo * s, d_v * t)
        for o_ref, val in zip(outs[:5], vals):
            o_ref[...] = val
        sums = (hsum(dyo * yn), hsum(dyo), hsum(ds * r_v * k_v))
        first = pl.program_id(0) == 0

        @pl.when(first)
        def _():
            for o_ref, val in zip(outs[5:], sums):
                o_ref[...] = val

        @pl.when(jnp.logical_not(first))
        def _():
            for o_ref, val in zip(outs[5:], sums):
                o_ref[...] += val

    blk = pl.BlockSpec((H, tm, HEAD), lambda i: (0, i, 0))
    par = pl.BlockSpec((H, 1, HEAD), lambda i: (0, 0, 0))
    return pl.pallas_call(
        body, name=name, grid=(T // tm,),
        in_specs=([pl.BlockSpec((tm, H * HEAD), lambda i: (i, 0))] + [blk] * 5 + [par] * 3
                  + [pl.BlockSpec(d.shape, lambda i, nd=d.ndim: (0,) * nd) for d in deps]),
        out_specs=[blk] * 5 + [par] * 3,
        out_shape=[jax.ShapeDtypeStruct((H, T, HEAD), F32)] * 5 + [jax.ShapeDtypeStruct((H, 1, HEAD), F32)] * 3,
        scratch_shapes=[pltpu.VMEM((H, tm, HEAD), F32)], compiler_params=_params())(dya, y, r, k2, v, g, *hp, *deps)


def _bmm(x, y, mode):
    dn = {"nn": (((2,), (1,)), ((0,), (0,))), "nt": (((2,), (2,)), ((0,), (0,))), "tn": (((1,), (1,)), ((0,), (0,)))}[mode]
    (xh, xl), (yh, yl) = _split(x), _split(y)
    dot = lambda p, q: lax.dot_general(p, q, dn, preferred_element_type=F32)
    out = dot(xh, yh)
    if yl is not None:
        out = out + dot(xh, yl)
    if xl is not None:
        out = out + dot(xl, yh)
    return out


def _split(x):
    if isinstance(x, tuple):
        return x
    hi = x.astype(BF16)
    return hi, (x - hi.astype(F32)).astype(BF16)


def _exact(x):
    return x.astype(BF16), None


def _round(x):
    return x if isinstance(x, tuple) else (x.astype(BF16), None)


def _rows(*xs):
    if isinstance(xs[0], tuple):
        return tuple(None if any(p is None for p in parts) else jnp.concatenate(parts, axis=1) for parts in zip(*xs))
    return jnp.concatenate(xs, axis=1)


def _wkv_chunk(r, lw, k, v, a, b):
    hb, C, _ = r.shape
    ti = lax.broadcasted_iota(jnp.int32, (C, C), 0)
    si = lax.broadcasted_iota(jnp.int32, (C, C), 1)
    linc, lstr, eye = (ti >= si).astype(F32), (ti > si).astype(F32), (ti == si).astype(F32)
    qmask = jnp.concatenate([jnp.concatenate([lstr, lstr], axis=1), jnp.concatenate([linc, linc], axis=1)], axis=0)
    lincb = _exact(jnp.broadcast_to(linc, (hb, C, C)))
    both = _exact(jnp.broadcast_to(jnp.concatenate([linc, lstr], axis=0), (hb, 2 * C, C)))
    ones = _exact(jnp.ones_like(v))
    lws = _split(lw)
    ci = _bmm(lincb, lws, "nn")
    cC = jnp.sum(lw, axis=1, keepdims=True)
    gi, ge, gn, gr = jnp.exp(ci), jnp.exp(ci - lw), jnp.exp(-ci), jnp.exp(cC - ci)
    q = dict(At=a * ge, Rt=r * gi, Bt=b * gn, Kt=k * gn, Bh=b * gr, Kh=k * gr)
    s = dict(AR=_split(_rows(q["At"], q["Rt"])), BK=_round(_rows(q["Bt"], q["Kt"])), BKh=_split(_rows(q["Bh"], q["Kh"])), v=_split(v))
    quad = _bmm(s["AR"], s["BK"], "nt") * qmask
    s["top"], s["bot"] = _round(quad[:, :C]), _round(quad[:, C:])
    A_ab = quad[:, :C, :C]
    Tm = eye + A_ab
    Pw = _round(A_ab)
    n = 1
    while 2 * n < C:
        Pw = _round(_bmm(Pw, Pw, "nn"))
        Tm = Tm + _bmm(_round(Tm), Pw, "nn")
        n *= 2
    s["Tm"] = _round(Tm)
    gC = jnp.exp(_bmm(lws, ones, "tn"))
    q.update(gi=gi, ge=ge, gn=gn, gr=gr, qmask=qmask, both=both, gC=gC, ones=ones, s=s)
    return q


def _wkv_u(s, H0s, C):
    arh = _bmm(s["AR"], H0s, "nn")
    zv = _rows(tuple(jnp.zeros_like(p) for p in s["v"]), s["v"])
    U = _bmm(s["Tm"], _round(arh[:, :C] + _bmm(s["top"], zv, "nn")), "nn")
    return arh, _rows(_split(U), s["v"])


def _wkv_fwd(r, lw, k, v, a, b, name):
    H, T, N = r.shape
    C = min(WKV_CHUNK, T)
    nc = T // C
    hb = _pick(H, (16, 8, 4, 2))

    def body(r_ref, lw_ref, k_ref, v_ref, a_ref, b_ref, y_ref, st_ref, h_ref):
        @pl.when(pl.program_id(1) == 0)
        def _():
            h_ref[...] = jnp.zeros_like(h_ref)

        H0 = h_ref[...]
        st_ref[0] = H0
        q = _wkv_chunk(r_ref[...], lw_ref[...], k_ref[...], v_ref[...], a_ref[...], b_ref[...])
        s = q["s"]
        arh, UV = _wkv_u(s, _split(H0), C)
        y_ref[...] = arh[:, C:] + _bmm(s["bot"], UV, "nn")
        h_ref[...] = q["gC"] * H0 + _bmm(s["BKh"], UV, "tn")

    blk = pl.BlockSpec((hb, C, N), lambda h, c: (h, c, 0))
    return pl.pallas_call(
        body, name=name, grid=(H // hb, nc), in_specs=[blk] * 6,
        out_specs=[blk, pl.BlockSpec((1, hb, N, N), lambda h, c: (c, h, 0, 0))],
        out_shape=[jax.ShapeDtypeStruct((H, T, N), F32), jax.ShapeDtypeStruct((nc, H, N, N), F32)],
        scratch_shapes=[pltpu.VMEM((hb, N, N), F32)], compiler_params=_params())(r, lw, k, v, a, b)


def _wkv_bwd(r, lw, k, v, a, b, states, dy, name):
    H, T, N = r.shape
    C = min(WKV_CHUNK, T)
    nc = T // C
    hb = _pick(H, (16, 8, 4, 2))

    def body(r_ref, lw_ref, k_ref, v_ref, a_ref, b_ref, st_ref, dy_ref, dr_ref, dlw_ref, dk_ref, dv_ref, da_ref, db_ref, dh_ref):
        @pl.when(pl.program_id(1) == 0)
        def _():
            dh_ref[...] = jnp.zeros_like(dh_ref)

        dHC = dh_ref[...]
        H0 = st_ref[0]
        q = _wkv_chunk(r_ref[...], lw_ref[...], k_ref[...], v_ref[...], a_ref[...], b_ref[...])
        s, gC = q["s"], q["gC"]
        H0s, dHs, dY = _split(H0), _split(dHC), _round(dy_ref[...])
        _, UV = _wkv_u(s, H0s, C)
        bot_dy = _bmm(s["bot"], dY, "tn")
        bkh_dh = _bmm(s["BKh"], dHs, "nn")
        dP = _round(_bmm(s["Tm"], _round(bot_dy[:, :C] + bkh_dh[:, :C]), "tn"))
        dv_ref[...] = bot_dy[:, C:] + bkh_dh[:, C:] + _bmm(s["top"], dP, "tn")[:, C:]
        dPY = _rows(dP, dY)
        dh_ref[...] = gC * dHC + _bmm(s["AR"], dPY, "tn")
        dquad = _round(_bmm(dPY, UV, "nt") * q["qmask"])
        dAR = _bmm(dPY, H0s, "nt") + _bmm(dquad, s["BK"], "nn")
        dBK = _bmm(dquad, s["AR"], "tn")
        dBKh = _bmm(UV, dHs, "nt")
        dAt, dRt, dBt, dKt, dBh, dKh = dAR[:, :C], dAR[:, C:], dBK[:, :C], dBK[:, C:], dBKh[:, :C], dBKh[:, C:]
        dr_ref[...] = dRt * q["gi"]
        da_ref[...] = dAt * q["ge"]
        db_ref[...] = dBt * q["gn"] + dBh * q["gr"]
        dk_ref[...] = dKt * q["gn"] + dKh * q["gr"]
        tail = dBh * q["Bh"] + dKh * q["Kh"]
        dci = dRt * q["Rt"] - dBt * q["Bt"] - dKt * q["Kt"] - tail
        dcC = jnp.sum(tail, axis=1, keepdims=True) + _bmm(q["ones"], H0 * dHC * gC, "nt")
        dlw_ref[...] = _bmm(q["both"], _rows(dci, dAt * q["At"]), "tn") + dcC

    blk = pl.BlockSpec((hb, C, N), lambda h, c: (h, nc - 1 - c, 0))
    st = pl.BlockSpec((1, hb, N, N), lambda h, c: (nc - 1 - c, h, 0, 0))
    return pl.pallas_call(
        body, name=name, grid=(H // hb, nc), in_specs=[blk] * 6 + [st, blk], out_specs=[blk] * 6,
        out_shape=[jax.ShapeDtypeStruct((H, T, N), F32)] * 6,
        scratch_shapes=[pltpu.VMEM((hb, N, N), F32)], compiler_params=_params())(r, lw, k, v, a, b, states, dy)


def _sgu_ln(z, SW, lng, lnb):
    ge = _gelu(z)
    u, vv = ge[:, :SW], ge[:, SW:]
    xc = vv - _mean(vv)
    rstd = lax.rsqrt(_mean(xc * xc) + LN_EPS)
    vn = xc * rstd
    return u, vn, rstd, vn * lng + lnb


def _causal(ws_ref, g):
    ti = lax.broadcasted_iota(jnp.int32, (SGU_CHUNK, SGU_CHUNK), 0)
    si = lax.broadcasted_iota(jnp.int32, (SGU_CHUNK, SGU_CHUNK), 1)
    return ti >= si, jnp.where(ti >= si, ws_ref[g], 0.0).astype(BF16)


def _sgu_fwd(proj, zblock, lng, lnb, ws, bexp, name):
    T, SW = proj.shape[0], lng.shape[1]
    G = ws.shape[0]
    tr = min(256, T)
    nch = tr // SGU_CHUNK

    def body(z_ref, lng_ref, lnb_ref, ws_ref, be_ref, o_ref):
        u, _, _, vl = _sgu_ln(z_ref[...], SW, lng_ref[...], lnb_ref[...])
        for g in range(G):
            cs = slice(g * SGU_GROUP, (g + 1) * SGU_GROUP)
            _, wc = _causal(ws_ref, g)
            for n in range(nch):
                rs = slice(n * SGU_CHUNK, (n + 1) * SGU_CHUNK)
                m = jnp.dot(wc, vl[rs, cs].astype(BF16), preferred_element_type=F32) + be_ref[:, cs]
                o_ref[rs, cs] = (u[rs, cs] * m).astype(BF16)

    whole = lambda arr: pl.BlockSpec(arr.shape, lambda i, nd=arr.ndim: (0,) * nd)
    return pl.pallas_call(
        body, name=name, grid=(T // tr,),
        in_specs=[pl.BlockSpec((tr, 2 * SW), lambda i: (i, zblock)), whole(lng), whole(lnb), whole(ws), whole(bexp)],
        out_specs=pl.BlockSpec((tr, SW), lambda i: (i, 0)), out_shape=jax.ShapeDtypeStruct((T, SW), BF16),
        compiler_params=_params())(proj, lng, lnb, ws, bexp)


def _sgu_bwd(proj, zblock, dyb, lng, lnb, ws, bexp, dproj, name):
    T, SW = proj.shape[0], lng.shape[1]
    G = ws.shape[0]
    tr = min(256, T)
    nch = tr // SGU_CHUNK
    nt = T // tr

    def body(z_ref, dy_ref, lng_ref, lnb_ref, ws_ref, be_ref, buf_ref, dz_ref, dlg_ref, dlb_ref, dws_ref, db_ref, du_s, dvl_s, dbacc_s):
        i = pl.program_id(0)
        zv = z_ref[...]
        lng_v = lng_ref[...]
        u, vn, rstd, vl = _sgu_ln(zv, SW, lng_v, lnb_ref[...])

        @pl.when(i == 0)
        def _():
            for s in (dlg_ref, dlb_ref, dws_ref, dbacc_s):
                s[...] = jnp.zeros_like(s)

        for g in range(G):
            cs = slice(g * SGU_GROUP, (g + 1) * SGU_GROUP)
            tri, wc = _causal(ws_ref, g)
            for n in range(nch):
                rs = slice(n * SGU_CHUNK, (n + 1) * SGU_CHUNK)
                blk = vl[rs, cs].astype(BF16)
                m = jnp.dot(wc, blk, preferred_element_type=F32) + be_ref[:, cs]
                dyv = dy_ref[rs, cs]
                du_s[rs, cs] = dyv * m
                dm = dyv * u[rs, cs]
                dvl_s[rs, cs] = _bdot(wc, dm, "tn")
                dws_ref[g] += jnp.where(tri, _bdot(dm, blk, "nt"), 0.0)
                dbacc_s[:, cs] += dm

        dvl = dvl_s[...]
        dlg_ref[...] += _colsum(dvl * vn)
        dlb_ref[...] += _colsum(dvl)
        dvn = dvl * lng_v
        dvv = rstd * (dvn - _mean(dvn) - vn * _mean(dvn * vn))
        gp = _gelu_grad(zv)
        dz_ref[:, :SW] = (du_s[...] * gp[:, :SW]).astype(BF16)
        dz_ref[:, SW:] = (dvv * gp[:, SW:]).astype(BF16)

        @pl.when(i == nt - 1)
        def _():
            lane = lax.broadcasted_iota(jnp.int32, (SGU_CHUNK, LANE), 1)
            out = jnp.zeros((SGU_CHUNK, LANE), F32)
            for g in range(G):
                col = jnp.sum(dbacc_s[:, g * SGU_GROUP:(g + 1) * SGU_GROUP], axis=1, keepdims=True)
                out = jnp.where(lane == g, col, out)
            db_ref[...] = out

    whole = lambda arr: pl.BlockSpec(arr.shape, lambda i, nd=arr.ndim: (0,) * nd)
    acc_shapes = [(1, SW), (1, SW), ws.shape, (SGU_CHUNK, LANE)]
    return pl.pallas_call(
        body, name=name, grid=(nt,),
        in_specs=[pl.BlockSpec((tr, 2 * SW), lambda i: (i, zblock)), pl.BlockSpec((tr, SW), lambda i: (i, 0)),
                  whole(lng), whole(lnb), whole(ws), whole(bexp), pl.BlockSpec(memory_space=pl.ANY)],
        out_specs=([pl.BlockSpec((tr, 2 * SW), lambda i: (i, zblock))]
                   + [pl.BlockSpec(s, lambda i, nd=len(s): (0,) * nd) for s in acc_shapes]),
        out_shape=[jax.ShapeDtypeStruct(dproj.shape, BF16)] + [jax.ShapeDtypeStruct(s, F32) for s in acc_shapes],
        scratch_shapes=[pltpu.VMEM((tr, SW), F32), pltpu.VMEM((tr, SW), F32), pltpu.VMEM((SGU_CHUNK, SW), F32)],
        input_output_aliases={6: 0}, compiler_params=_params())(proj, dyb, lng, lnb, ws, bexp, dproj)


_HBM = pl.BlockSpec(memory_space=pltpu.HBM)
_SEM = pl.BlockSpec(memory_space=pltpu.SEMAPHORE)
_DATAFLOW = pltpu.SideEffectType.DATAFLOW_SIDE_EFFECTING


def _mesh_place(chips=False):
    x, y, c = lax.axis_index("x"), lax.axis_index("y"), lax.axis_index("c")
    return x, y, c, (2 * x + y if chips else 4 * x + 2 * y + c)


def _peer(x, y, c, rel, chips=False):
    px = 1 - x if rel & 4 else x
    py = 1 - y if rel & 2 else y
    pc = 1 - c if rel & 1 else c
    return (px, py, pc), (2 * px + py if chips else 4 * px + 2 * py + pc)


ALL_PEERS = tuple(range(1, N_DEV))
SIBLING = (1,)
SAME_CORE = (2, 4, 6)
SIBLINGS_CORE = (3, 5, 7)


def _exchange_start(groups, name, rels=ALL_PEERS, chips=False):
    flat = [t for g in groups for t in g]
    sizes = [len(g) for g in groups]
    n, ng = len(flat), len(groups)
    srcs = [pltpu.with_memory_space_constraint(a, pltpu.HBM) for a, _ in flat]
    lands = [pltpu.with_memory_space_constraint(lax.empty(((N_DEV,) + a.shape) if isg else a.shape, a.dtype), pltpu.HBM)
             for a, isg in flat]

    def body(*refs):
        ins, lnd, sems, token = refs[:n], refs[n:2 * n], refs[2 * n:2 * n + 3 * ng], refs[-1]
        x, y, c, me = _mesh_place(chips)
        j0 = 0
        for gi, sz in enumerate(sizes):
            for rel in rels:
                dev, slot = _peer(x, y, c, rel, chips)
                for jj in range(sz):
                    j = j0 + jj
                    pltpu.make_async_remote_copy(
                        src_ref=ins[j] if flat[j][1] else ins[j].at[slot], dst_ref=lnd[j].at[me],
                        send_sem=sems[3 * gi].at[jj * (N_DEV - 1) + rel - 1], recv_sem=sems[3 * gi + 1].at[jj * (N_DEV - 1) + rel - 1],
                        device_id=dev, device_id_type=pl.DeviceIdType.MESH).start()
            for jj in range(sz):
                j = j0 + jj
                pltpu.make_async_copy(ins[j] if flat[j][1] else ins[j].at[me], lnd[j].at[me], sems[3 * gi + 2].at[jj]).start()
            j0 += sz
        token[...] = jnp.zeros_like(token)

    sem_shapes = [pltpu.SemaphoreType.DMA((k,)) for sz in sizes for k in (sz * (N_DEV - 1), sz * (N_DEV - 1), sz)]
    res = pl.pallas_call(
        body, name=name,
        out_shape=(*sem_shapes, *[pltpu.HBM(a.shape, a.dtype) for a in srcs], *[pltpu.HBM(a.shape, a.dtype) for a in lands],
                   jax.ShapeDtypeStruct((SUBLANE, LANE), F32)),
        in_specs=[_HBM] * (2 * n), out_specs=(*[_SEM] * (3 * ng), *[_HBM] * (2 * n), pl.BlockSpec(memory_space=pltpu.VMEM)),
        input_output_aliases={i: 3 * ng + i for i in range(2 * n)},
        compiler_params=pltpu.CompilerParams(has_side_effects=_DATAFLOW))(*srcs, *lands)
    sems, thru, token = res[:3 * ng], res[3 * ng:3 * ng + 2 * n], res[-1]
    handle, j0 = [], 0
    for gi, sz in enumerate(sizes):
        handle.append(dict(kinds=[k for _, k in groups[gi]], chips=chips, srcs=list(thru[j0:j0 + sz]), lands=list(thru[n + j0:n + j0 + sz]),
                           sems=list(sems[3 * gi:3 * gi + 3])))
        j0 += sz
    return handle, token


def _exchange_wait(group, after, name, rels=ALL_PEERS, local=True):
    kinds, sz = group["kinds"], len(group["kinds"])
    relay = group.get("relay", [])

    def body(*refs):
        ins, lnd, (ssem, rsem, lsem) = refs[:sz], refs[sz:2 * sz], refs[2 * sz:2 * sz + 3]
        x, y, c, me = _mesh_place(group["chips"])
        for rel in rels:
            dev, slot = _peer(x, y, c, rel, group["chips"])
            for jj in range(sz):
                cp = pltpu.make_async_remote_copy(
                    src_ref=ins[jj] if kinds[jj] else ins[jj].at[slot], dst_ref=lnd[jj].at[slot],
                    send_sem=ssem.at[jj * (N_DEV - 1) + rel - 1], recv_sem=rsem.at[jj * (N_DEV - 1) + rel - 1],
                    device_id=dev, device_id_type=pl.DeviceIdType.MESH)
                cp.wait_send()
                cp.wait_recv()
        if local:
            for jj in range(sz):
                pltpu.make_async_copy(ins[jj] if kinds[jj] else ins[jj].at[me], lnd[jj].at[me], lsem.at[jj]).wait()
        if relay:
            fsend, frecv = refs[2 * sz + 3:2 * sz + 5]
            dev = _peer(x, y, c, 1)[0]
            for q, (mine, theirs) in enumerate(zip(SAME_CORE, SIBLINGS_CORE)):
                for jj in range(sz):
                    cp = pltpu.make_async_remote_copy(
                        src_ref=lnd[jj].at[_peer(x, y, c, mine)[1]], dst_ref=lnd[jj].at[_peer(x, y, c, theirs)[1]],
                        send_sem=fsend.at[jj * len(SAME_CORE) + q], recv_sem=frecv.at[jj * len(SAME_CORE) + q],
                        device_id=dev, device_id_type=pl.DeviceIdType.MESH)
                    cp.wait_send()
                    cp.wait_recv()

    arrays = group["srcs"] + group["lands"]
    sems = group["sems"] + relay
    res = pl.pallas_call(
        body, name=name, out_shape=[pltpu.HBM(a.shape, a.dtype) for a in arrays],
        in_specs=[_HBM] * (2 * sz) + [_SEM] * len(sems) + [pl.BlockSpec(memory_space=pl.ANY)], out_specs=[_HBM] * (2 * sz),
        input_output_aliases={i: i for i in range(2 * sz)},
        compiler_params=pltpu.CompilerParams(has_side_effects=_DATAFLOW))(*arrays, *sems, after)
    return dict(group, srcs=list(res[:sz]), lands=list(res[sz:]), relay=[])


def _relay_start(group, name):
    sz = len(group["kinds"])
    nq = len(SAME_CORE)

    def body(*refs):
        lnd, fsend, frecv, token = refs[:sz], refs[sz], refs[sz + 1], refs[-1]
        x, y, c, _ = _mesh_place()
        dev = _peer(x, y, c, 1)[0]
        for q, rel in enumerate(SAME_CORE):
            slot = _peer(x, y, c, rel)[1]
            for jj in range(sz):
                pltpu.make_async_remote_copy(
                    src_ref=lnd[jj].at[slot], dst_ref=lnd[jj].at[slot], send_sem=fsend.at[jj * nq + q], recv_sem=frecv.at[jj * nq + q],
                    device_id=dev, device_id_type=pl.DeviceIdType.MESH).start()
        token[...] = jnp.zeros_like(token)

    lands = group["lands"]
    res = pl.pallas_call(
        body, name=name,
        out_shape=(pltpu.SemaphoreType.DMA((sz * nq,)), pltpu.SemaphoreType.DMA((sz * nq,)), *[pltpu.HBM(a.shape, a.dtype) for a in lands],
                   jax.ShapeDtypeStruct((SUBLANE, LANE), F32)),
        in_specs=[_HBM] * sz, out_specs=(_SEM, _SEM, *[_HBM] * sz, pl.BlockSpec(memory_space=pltpu.VMEM)),
        input_output_aliases={i: 2 + i for i in range(sz)},
        compiler_params=pltpu.CompilerParams(has_side_effects=_DATAFLOW))(*lands)
    return dict(group, lands=list(res[2:2 + sz]), relay=[res[0], res[1]]), res[-1]


def _sibling_swap(arrays, handle, after, name):
    start = handle is None
    n = len(arrays) if start else len(handle["srcs"])
    chips = N_DEV // 2
    if start:
        srcs = [pltpu.with_memory_space_constraint(a.reshape(chips, 2, *a.shape[1:]), pltpu.HBM) for a in arrays]
        lands = [pltpu.with_memory_space_constraint(lax.empty((chips,) + a.shape[1:], a.dtype), pltpu.HBM) for a in arrays]
    else:
        srcs, lands = handle["srcs"], handle["lands"]

    def body(*refs):
        ins, lnd, ssem, rsem = refs[:n], refs[n:2 * n], refs[2 * n], refs[2 * n + 1]
        x, y, c, _ = _mesh_place()
        dev = _peer(x, y, c, 1)[0]
        for q in range(chips):
            for j in range(n):
                cp = pltpu.make_async_remote_copy(
                    src_ref=ins[j].at[q, 1 - c], dst_ref=lnd[j].at[q], send_sem=ssem.at[j * chips + q], recv_sem=rsem.at[j * chips + q],
                    device_id=dev, device_id_type=pl.DeviceIdType.MESH)
                if start:
                    cp.start()
                else:
                    cp.wait_send()
                    cp.wait_recv()
        if start:
            refs[-1][...] = jnp.zeros_like(refs[-1])

    thru = [pltpu.HBM(a.shape, a.dtype) for a in srcs + lands]
    effect = pltpu.CompilerParams(has_side_effects=_DATAFLOW)
    if start:
        res = pl.pallas_call(
            body, name=name, out_shape=(pltpu.SemaphoreType.DMA((n * chips,)), pltpu.SemaphoreType.DMA((n * chips,)), *thru,
                                        jax.ShapeDtypeStruct((SUBLANE, LANE), F32)),
            in_specs=[_HBM] * (2 * n), out_specs=(_SEM, _SEM, *[_HBM] * (2 * n), pl.BlockSpec(memory_space=pltpu.VMEM)),
            input_output_aliases={i: 2 + i for i in range(2 * n)}, compiler_params=effect)(*srcs, *lands)
        return dict(srcs=list(res[2:2 + n]), lands=list(res[2 + n:2 + 2 * n]), sems=[res[0], res[1]]), res[-1]
    res = pl.pallas_call(
        body, name=name, out_shape=thru, in_specs=[_HBM] * (2 * n) + [_SEM, _SEM, pl.BlockSpec(memory_space=pl.ANY)],
        out_specs=[_HBM] * (2 * n), input_output_aliases={i: i for i in range(2 * n)}, compiler_params=effect)(
            *srcs, *lands, *handle["sems"], after)
    return dict(handle, srcs=list(res[:n]), lands=list(res[n:]))


def _pair_add(mine, theirs, core, name):
    chips, _, rows, w = mine.shape
    tm = _pick(rows, (256, 128, 64, 32, 16))

    def body(core_ref, a_ref, b_ref, o_ref):
        o_ref[...] = (a_ref[...].astype(F32) + b_ref[...].astype(F32)).astype(o_ref.dtype)

    return pl.pallas_call(
        body, name=name, out_shape=jax.ShapeDtypeStruct(theirs.shape, theirs.dtype),
        grid_spec=pltpu.PrefetchScalarGridSpec(
            num_scalar_prefetch=1, grid=(chips, rows // tm),
            in_specs=[pl.BlockSpec((None, None, tm, w), lambda q, i, core_ref: (q, core_ref[0], i, 0)),
                      pl.BlockSpec((None, tm, w), lambda q, i, core_ref: (q, i, 0))],
            out_specs=pl.BlockSpec((None, tm, w), lambda q, i, core_ref: (q, i, 0))),
        compiler_params=_params())(core, mine, theirs)


def _adamw(w, m, v, gparts, name):
    R, C = w.shape
    tm = _pick(R, (256, 128, 64, 32, 16, 8))

    def body(w_ref, m_ref, v_ref, g_ref, go, do, mo, vo):
        g = g_ref[0].astype(F32)
        for j in range(1, gparts.shape[0]):
            g = g + g_ref[j].astype(F32)
        mn = ADAM_B1 * m_ref[...] + (1.0 - ADAM_B1) * g
        vn = ADAM_B2 * v_ref[...] + (1.0 - ADAM_B2) * (g * g)
        m_hat = mn / (1.0 - ADAM_B1 ** ADAM_STEP)
        v_hat = vn / (1.0 - ADAM_B2 ** ADAM_STEP)
        go[...] = g
        do[...] = -ADAM_LR * (m_hat / (jnp.sqrt(v_hat) + ADAM_EPS) + ADAM_WD * w_ref[...])
        mo[...] = mn
        vo[...] = vn

    row = pl.BlockSpec((tm, C), lambda i: (i, 0))
    return pl.pallas_call(
        body, name=name, grid=(R // tm,), in_specs=[row, row, row, pl.BlockSpec((gparts.shape[0], tm, C), lambda i: (0, i, 0))],
        out_specs=[row] * 4, out_shape=[jax.ShapeDtypeStruct((R, C), F32)] * 4, compiler_params=_params())(w, m, v, gparts)


def _pack(arrays):
    parts = []
    for a in arrays:
        f = a.reshape(1, -1)
        pad = _ceil_to(f.shape[1], SUBLANE * LANE) - f.shape[1]
        f = jnp.concatenate([f, jnp.zeros((1, pad), f.dtype)], axis=1) if pad else f
        parts.append(f.reshape(-1, LANE))
    rows = sum(p.shape[0] for p in parts)
    pad = _ceil_to(rows, 64) - rows
    return jnp.concatenate(parts + ([jnp.zeros((pad, LANE), parts[0].dtype)] if pad else []), axis=0)


def _unpack(buf, shapes):
    out, row = [], 0
    for s in shapes:
        size = 1
        for d in s:
            size *= d
        rows = _ceil_to(size, SUBLANE * LANE) // LANE
        out.append(buf[row:row + rows].reshape(1, -1)[:, :size].reshape(s))
        row += rows
    return out


def kernel(x, norm_mix_g, w_in, shift_mu, w0, w_lora_up, a0, a_lora_up, g_lora_up, k_k, k_a, r_k, lnx_g, lnx_b, w_proj_rwkv, sgu_ln_g, sgu_ln_b, sgu_w, sgu_b, w_proj_sgu, w_out, norm_ffn_g, w_ffn_gate, w_ffn_up, w_ffn_down, norm_final_g, loss_target, m_norm_mix_g, m_w_in, m_shift_mu, m_w0, m_w_lora_up, m_a0, m_a_lora_up, m_g_lora_up, m_k_k, m_k_a, m_r_k, m_lnx_g, m_lnx_b, m_w_proj_rwkv, m_sgu_ln_g, m_sgu_ln_b, m_sgu_w, m_sgu_b, m_w_proj_sgu, m_w_out, m_norm_ffn_g, m_w_ffn_gate, m_w_ffn_up, m_w_ffn_down, m_norm_final_g, v_norm_mix_g, v_w_in, v_shift_mu, v_w0, v_w_lora_up, v_a0, v_a_lora_up, v_g_lora_up, v_k_k, v_k_a, v_r_k, v_lnx_g, v_lnx_b, v_w_proj_rwkv, v_sgu_ln_g, v_sgu_ln_b, v_sgu_w, v_sgu_b, v_w_proj_sgu, v_w_out, v_norm_ffn_g, v_w_ffn_gate, v_w_ffn_up, v_w_ffn_down, v_norm_final_g):
    weights = dict(norm_mix_g=norm_mix_g, w_in=w_in, shift_mu=shift_mu, w0=w0, w_lora_up=w_lora_up, a0=a0, a_lora_up=a_lora_up,
                   g_lora_up=g_lora_up, k_k=k_k, k_a=k_a, r_k=r_k, lnx_g=lnx_g, lnx_b=lnx_b, w_proj_rwkv=w_proj_rwkv,
                   sgu_ln_g=sgu_ln_g, sgu_ln_b=sgu_ln_b, sgu_w=sgu_w, sgu_b=sgu_b, w_proj_sgu=w_proj_sgu, w_out=w_out,
                   norm_ffn_g=norm_ffn_g, w_ffn_gate=w_ffn_gate, w_ffn_up=w_ffn_up, w_ffn_down=w_ffn_down, norm_final_g=norm_final_g)
    m_in = dict(norm_mix_g=m_norm_mix_g, w_in=m_w_in, shift_mu=m_shift_mu, w0=m_w0, w_lora_up=m_w_lora_up, a0=m_a0,
                a_lora_up=m_a_lora_up, g_lora_up=m_g_lora_up, k_k=m_k_k, k_a=m_k_a, r_k=m_r_k, lnx_g=m_lnx_g, lnx_b=m_lnx_b,
                w_proj_rwkv=m_w_proj_rwkv, sgu_ln_g=m_sgu_ln_g, sgu_ln_b=m_sgu_ln_b, sgu_w=m_sgu_w, sgu_b=m_sgu_b,
                w_proj_sgu=m_w_proj_sgu, w_out=m_w_out, norm_ffn_g=m_norm_ffn_g, w_ffn_gate=m_w_ffn_gate, w_ffn_up=m_w_ffn_up,
                w_ffn_down=m_w_ffn_down, norm_final_g=m_norm_final_g)
    v_in = dict(norm_mix_g=v_norm_mix_g, w_in=v_w_in, shift_mu=v_shift_mu, w0=v_w0, w_lora_up=v_w_lora_up, a0=v_a0,
                a_lora_up=v_a_lora_up, g_lora_up=v_g_lora_up, k_k=v_k_k, k_a=v_k_a, r_k=v_r_k, lnx_g=v_lnx_g, lnx_b=v_lnx_b,
                w_proj_rwkv=v_w_proj_rwkv, sgu_ln_g=v_sgu_ln_g, sgu_ln_b=v_sgu_ln_b, sgu_w=v_sgu_w, sgu_b=v_sgu_b,
                w_proj_sgu=v_w_proj_sgu, w_out=v_w_out, norm_ffn_g=v_norm_ffn_g, w_ffn_gate=v_w_ffn_gate, w_ffn_up=v_w_ffn_up,
                w_ffn_down=v_w_ffn_down, norm_final_g=v_norm_final_g)
    names = list(weights)
    col_sharded = ("w_in", "w_lora_up", "a_lora_up", "g_lora_up", "w_proj_rwkv", "w_proj_sgu", "w_ffn_gate", "w_ffn_up")
    row_sharded = ("w_out", "w_ffn_down")
    sharded = [n for n in names if n in col_sharded or n in row_sharded]
    small = [n for n in names if n not in sharded]

    xs, tgt = x[0], loss_target[0]
    T, D = xs.shape
    RW = w0.shape[1]
    H = RW // HEAD
    SW = sgu_ln_g.shape[1]
    G = sgu_w.shape[1]
    assert 2 * SW == D, "the projection layout takes the SGU part to be as wide as a gate"
    lay = _rwkv_layout(RW, w_lora_up.shape[1], a_lora_up.shape[1], g_lora_up.shape[1], D)
    _, pw, _, rcp = lay
    icp = rcp + 3 * D
    b_ga, b_gb, b_z = rcp // D, rcp // D + 1, rcp // D + 2

    gather_groups = [["w_in", "w_lora_up", "a_lora_up", "g_lora_up"], ["w_proj_rwkv", "w_proj_sgu", "w_out"],
                     ["w_ffn_gate", "w_ffn_up", "w_ffn_down"]]
    gather, gather_token = _exchange_start([[(weights[n][0].astype(BF16), True) for n in grp] for grp in gather_groups],
                                           "gather_start", rels=SIBLING + SAME_CORE)
    full = {}
    relay_tokens = {}
    joined = lambda g: g.transpose(1, 0, 2).reshape(g.shape[1], -1)

    def relay_weights(gi, after, name):
        arrived = _exchange_wait(gather[gi], after, "gather_wait_ici_" + name, rels=SAME_CORE, local=False)
        gather[gi], relay_tokens[gi] = _relay_start(arrived, "gather_relay_" + name)

    def take_weights(gi, after, name):
        done = _exchange_wait(gather[gi], after, "gather_wait_d2d_" + name, rels=SIBLING)
        for n, g in zip(gather_groups[gi], done["lands"]):
            full[n] = g.reshape(-1, g.shape[2]) if n in row_sharded else g

    n1 = _rms_fwd(xs, norm_mix_g, "rms_mix", deps=[gather_token])
    relay_weights(0, n1, "in")
    take_weights(0, relay_tokens[0], "in")
    W_in = _w_in_to_proj(full["w_in"], lay, D, "w_in_layout")
    lora = [_pad_rows(joined(full[n]), rows) for n, rows in zip(("w_lora_up", "a_lora_up", "g_lora_up"), pw[3:])]
    mu_p = _pad_rwkv_cols(shift_mu, lay)
    rsmall = [w0, a0, k_k, k_a]
    hp = [lnx_g.reshape(H, 1, HEAD), lnx_b.reshape(H, 1, HEAD), r_k.reshape(H, 1, HEAD)]
    ws = sgu_w[0]
    bexp = jnp.repeat(sgu_b[0].T, SGU_GROUP, axis=1)
    gf = norm_final_g.reshape(1, D)

    proj = _matmul(n1, W_in, mode="nn", out_dtype=F32, name="proj_in")
    ga, gb = (proj, D, b_ga), (proj, D, b_gb)
    r_h, lw_h, k2_h, v_h, aa_h, bb_h, g_h = _rwkv_pre(proj, mu_p, rsmall, lora, lay, "rwkv_pre")
    wkv_in = [r_h, lw_h, k2_h, v_h, aa_h, bb_h]
    y_h, states = _wkv_fwd(*wkv_in, "wkv_fwd")
    relay_weights(1, y_h, "proj")
    relay_weights(2, relay_tokens[1], "ffn")
    ya = _head_post(y_h, r_h, k2_h, v_h, g_h, hp, "head_post")
    yb = _sgu_fwd(proj, b_z, sgu_ln_g, sgu_ln_b, ws, bexp, "sgu_fwd")
    take_weights(1, ya, "proj")
    pa = _matmul(ya, full["w_proj_rwkv"], mode="nn", out_dtype=F32, name="proj_a")

    def merge_fn(pb_v, pa_v, ga_v, gb_v):
        return pb_v, _sigmoid(ga_v) * pa_v + _sigmoid(gb_v) * pb_v
    pb, merged = _matmul(yb, full["w_proj_sgu"], mode="nn", name="proj_b_merge",
                         epi=(merge_fn, [pa, (proj, b_ga * D), (proj, b_gb * D)], [F32, BF16]))
    h1 = _matmul(merged, full["w_out"], mode="nn", out_dtype=F32, name="out_proj", add=xs)
    n2 = _rms_fwd(h1, norm_ffn_g, "rms_ffn")
    take_weights(2, n2, "ffn")
    gt = _matmul(n2, full["w_ffn_gate"], mode="nn", out_dtype=F32, name="ffn_gate", out_blocks=N_DEV)

    def act_fn(up_v, gt_v):
        return up_v, gt_v * _sigmoid(gt_v) * up_v
    up, act = _matmul(n2, full["w_ffn_up"], mode="nn", name="ffn_up_act", out_blocks=N_DEV, epi=(act_fn, [gt], [F32, BF16]))
    h2 = _matmul(act, full["w_ffn_down"], mode="nn", out_dtype=F32, name="ffn_down", add=h1)

    def final_fn(rv, pv):
        (h_v, t_v), (g_v,) = rv, pv
        r = lax.rsqrt(_mean(h_v * h_v) + RMS_EPS)
        yn = h_v * r
        e = yn * g_v - t_v
        loss = 0.5 * jnp.sum(_mean(e * e))
        dout = e * (1.0 / D)
        dyg = dout * g_v
        dh = r * (dyg - yn * _mean(dyg * yn))
        return [dh, dh], [jnp.full((1, LANE), loss, F32), _colsum(dout * yn)]
    dh2, dh2_bf, loss_part, d_gf = _rowwise(final_fn, [h2, tgt], [gf], [(D, F32), (D, BF16)], [(1, LANE), (1, D)], name="final_loss")

    grads = {}

    def start_scatter(group, name, extra=()):
        blocks = [(grads[n].reshape(N_DEV, -1, grads[n].shape[1]) if n in row_sharded else grads[n], False) for n in group]
        (handle,), token = _exchange_start([blocks + list(extra)], name)
        return handle, token

    def dact_fn(d_v, gt_v, up_v):
        s = _sigmoid(gt_v)
        return d_v * up_v * (s * (1.0 + gt_v * (1.0 - s))), d_v * gt_v * s
    dgt, dup = _matmul(dh2_bf, full["w_ffn_down"], mode="nt", name="d_ffn_act", out_blocks=N_DEV,
                       epi=(dact_fn, [gt, up], [BF16, BF16]))
    grads["w_ffn_down"] = _matmul(act, dh2_bf, mode="tn", out_dtype=BF16, name="dw_ffn_down")
    dn2 = _matmul(dgt, full["w_ffn_gate"], mode="nt", out_dtype=F32, name="dn2_gate")
    dn2 = _matmul(dup, full["w_ffn_up"], mode="nt", out_dtype=F32, name="dn2_up", add=dn2)
    grads["w_ffn_gate"] = _matmul(n2, dgt, mode="tn", out_dtype=BF16, name="dw_ffn_gate", out_blocks=N_DEV)
    grads["w_ffn_up"] = _matmul(n2, dup, mode="tn", out_dtype=BF16, name="dw_ffn_up", out_blocks=N_DEV)
    scatter_groups = [["w_ffn_down", "w_ffn_gate", "w_ffn_up"], ["w_out", "w_proj_rwkv", "w_proj_sgu"],
                      ["w_in", "w_lora_up", "a_lora_up", "g_lora_up"]]
    scatter_ffn, token_ffn = start_scatter(scatter_groups[0], "scatter_start_ffn")
    dh1, dh1_bf, d_g2 = _rms_bwd(dn2, h1, dh2, norm_ffn_g, "rms_ffn_bwd", deps=[token_ffn])
    dmerged = _matmul(dh1_bf, full["w_out"], mode="nt", out_dtype=F32, name="d_merged")
    grads["w_out"] = _matmul(merged, dh1_bf, mode="tn", out_dtype=BF16, name="dw_out")

    def dmerge_fn(rv, pv):
        d_v, ga_v, gb_v, pa_v, pb_v = rv
        sa, sb = _sigmoid(ga_v), _sigmoid(gb_v)
        dgates = jnp.concatenate([d_v * pa_v * sa * (1.0 - sa), d_v * pb_v * sb * (1.0 - sb)], axis=1)
        return [dgates, d_v * sa, d_v * sb], []
    dproj, dpa, dpb = _rowwise(dmerge_fn, [dmerged, ga, gb, pa, pb], [],
                               [(2 * D, BF16, icp, b_ga // 2, None), (D, BF16), (D, BF16)], [], name="d_merge")
    dya = _matmul(dpa, full["w_proj_rwkv"], mode="nt", out_dtype=F32, name="d_ya")
    dyb = _matmul(dpb, full["w_proj_sgu"], mode="nt", out_dtype=F32, name="d_yb")
    grads["w_proj_rwkv"] = _matmul(ya, dpa, mode="tn", out_dtype=BF16, name="dw_proj_a", out_blocks=N_DEV)
    grads["w_proj_sgu"] = _matmul(yb, dpb, mode="tn", out_dtype=BF16, name="dw_proj_b", out_blocks=N_DEV)
    scatter_mid, token_mid = start_scatter(scatter_groups[1], "scatter_start_mid")
    dproj, d_lng, d_lnb, d_ws, d_bs = _sgu_bwd(proj, b_z, dyb, sgu_ln_g, sgu_ln_b, ws, bexp, dproj, "sgu_bwd")

    dy_h, dr1, dk1, dv1, dg_h, d_lnxg, d_lnxb, d_rk = _head_post_bwd(dya, y_h, r_h, k2_h, v_h, g_h, hp, "head_post_bwd",
                                                                     deps=[token_mid])
    dr2, dlw_h, dk2b, dv2, daa, dbb = _wkv_bwd(*wkv_in, states, dy_h, "wkv_bwd")
    dps, d_mu, d_w0, d_a0, d_kk, d_ka, d_wlw, d_wla, d_wlg = _rwkv_pre_bwd(
        proj, mu_p, rsmall, lora, [dr1, dr2, dk1, dk2b, dv1, dv2, dlw_h, daa, dbb, dg_h], lay, "rwkv_pre_bwd")
    dproj = _shift_bwd(dps, mu_p, dproj, "shift_bwd")
    split = lambda g: g.reshape(g.shape[0], N_DEV, -1).transpose(1, 0, 2)
    grads["w_in"] = _dw_in_from_proj(_matmul(n1, dproj, mode="tn", out_dtype=BF16, name="dw_in"), lay, D, w_in.shape[2], "dw_in_layout")
    grads["w_lora_up"] = split(d_wlw[:w_lora_up.shape[1]].astype(BF16))
    grads["a_lora_up"] = split(d_wla[:a_lora_up.shape[1]].astype(BF16))
    grads["g_lora_up"] = split(d_wlg[:g_lora_up.shape[1]].astype(BF16))
    swap, token_swap = _sibling_swap([grads[n] for n in scatter_groups[2]], None, None, "scatter_in_swap_start")
    dn1 = _matmul(dproj, W_in, mode="nt", out_dtype=F32, name="dn1", deps=[token_swap])
    swap = _sibling_swap(None, swap, dn1, "scatter_in_swap_wait")
    core = lax.axis_index("c").astype(jnp.int32).reshape(1)
    chip_sums = [_pair_add(mine, theirs, core, "scatter_in_add_" + n)
                 for n, mine, theirs in zip(scatter_groups[2], swap["srcs"], swap["lands"])]
    (scatter_in,), token_in = _exchange_start([[(s, False) for s in chip_sums]], "scatter_start_in", rels=SAME_CORE, chips=True)
    dx, _, d_g1 = _rms_bwd(dn1, xs, dh1, norm_mix_g, "rms_mix_bwd", deps=[token_in])
    small_grads = dict(norm_mix_g=d_g1, shift_mu=_unpad_rwkv_cols(d_mu, lay), w0=d_w0, a0=d_a0, k_k=d_kk, k_a=d_ka, r_k=d_rk,
                       lnx_g=d_lnxg, lnx_b=d_lnxb, sgu_ln_g=d_lng, sgu_ln_b=d_lnb, sgu_w=d_ws, sgu_b=d_bs[:, :G].T,
                       norm_ffn_g=d_g2, norm_final_g=d_gf)

    (gather_small,), after = _exchange_start([[(_pack([small_grads[n] for n in small]), True)]], "gather_small_start")
    out = {}
    for group, handle, name in zip(scatter_groups, (scatter_ffn, scatter_mid, scatter_in), ("ffn", "mid", "in")):
        parts = _exchange_wait(handle, after, "scatter_wait_" + name, rels=SAME_CORE if handle["chips"] else ALL_PEERS)["lands"]
        for n, part in zip(group, parts):
            shp = weights[n].shape
            res = _adamw(weights[n][0], m_in[n][0], v_in[n][0], part, "adamw_" + n)
            out[n] = [t.reshape(shp) for t in res]
            after = res[0]
    packed = [_pack([d[n] for n in small]) for d in (weights, m_in, v_in)]
    small_parts = _exchange_wait(gather_small, after, "gather_small_wait")["lands"][0]
    res = _adamw(*packed, small_parts, "adamw_small")
    unpacked = [_unpack(t, [weights[n].shape for n in small]) for t in res]
    for i, n in enumerate(small):
        out[n] = [u[i] for u in unpacked]

    loss = lax.psum(loss_part[0, 0], ("x", "y", "c"))
    return (loss, dx[None], *[out[n][0] for n in names], *[out[n][1] for n in names],
            *[out[n][2] for n in names], *[out[n][3] for n in names])
```

```python
import jax
import jax.numpy as jnp
from jax import lax
from jax.experimental import pallas as pl
from jax.experimental.pallas import tpu as pltpu

F32 = jnp.float32
BF16 = jnp.bfloat16

N_DEV = 8
LANE = 128
SUBLANE = 8
HEAD = 64
SGU_CHUNK = 128
SGU_GROUP = 128
WKV_CHUNK = 64
RMS_EPS = 1e-6
LN_EPS = 1e-5
LNX_EPS = 64e-5
ADAM_LR, ADAM_B1, ADAM_B2, ADAM_EPS, ADAM_WD, ADAM_STEP = 0.001, 0.9, 0.999, 1e-08, 0.01, 10
VMEM_LIMIT_BYTES = 48 * 1024 * 1024
_SQRT_HALF = 0.7071067811865476
_INV_SQRT_2PI = 0.3989422804014327


def _pick(n, cands):
    for c in cands:
        if n % c == 0:
            return c
    return n


def _ceil_to(n, m):
    return -(-n // m) * m


def _params():
    return pltpu.CompilerParams(vmem_limit_bytes=VMEM_LIMIT_BYTES)


def _tile(n, cap):
    best = 0
    for d in range(LANE, min(n, cap) + 1, LANE):
        if n % d == 0:
            best = d
    return best or n


def _matmul_tiles(M, N, K, a_bytes, b_bytes, o_bytes, has_add, forced):
    tm = forced.get("m") or _tile(M, 1024)
    tn = forced.get("n") or _tile(N, 1024)
    tk = forced.get("k") or _tile(K, 2048)

    def vmem(tm, tn, tk):
        acc = tm * tn * 4 if tk < K else 0
        return 2 * (tm * tk * a_bytes + tk * tn * b_bytes + tm * tn * (o_bytes + (4 if has_add else 0))) + acc

    while vmem(tm, tn, tk) > (VMEM_LIMIT_BYTES * 3) // 4:
        if "k" not in forced and tk > 512 and _tile(K, tk // 2) < tk:
            tk = _tile(K, tk // 2)
        elif "m" not in forced and _tile(M, tm // 2) < tm:
            tm = _tile(M, tm // 2)
        else:
            break
    return tm, tn, tk


def _matmul(a, b, *, mode, out_dtype=F32, name, add=None, deps=(), out_blocks=0, epi=None):
    def view(x):
        return (x.shape[1], x.shape[0] * x.shape[2], x.shape[2]) if x.ndim == 3 else (x.shape[0], x.shape[1], 0)

    (ar, ac, aw), (br, bc, bw) = view(a), view(b)
    a_col, b_col = {"nn": ("k", "n"), "nt": ("k", "k"), "tn": ("m", "n")}[mode]
    if mode == "nn":
        M, K, K2, N = ar, ac, br, bc
    elif mode == "nt":
        M, K, N, K2 = ar, ac, br, bc
    else:
        K, M, K2, N = ar, ac, br, bc
    assert K == K2, (a.shape, b.shape, mode)
    forced = {}
    for dim, w in ((a_col, aw), (b_col, bw), ("n", N // out_blocks if out_blocks else 0)):
        if w:
            assert forced.get(dim, w) == w
            forced[dim] = w
    has_add = add is not None
    tile_bytes = (sum(jnp.dtype(d).itemsize for d in epi[2]) + sum((e[0] if isinstance(e, tuple) else e).dtype.itemsize for e in epi[1])
                  if epi is not None else jnp.dtype(out_dtype).itemsize)
    tm, tn, tk = _matmul_tiles(M, N, K, a.dtype.itemsize, b.dtype.itemsize, tile_bytes, has_add, forced)
    nk = K // tk
    dn = {"nn": (((1,), (0,)), ((), ())), "nt": (((1,), (1,)), ((), ())), "tn": (((0,), (0,)), ((), ()))}[mode]
    pick = {"m": lambda i, j, k: i, "n": lambda i, j, k: j, "k": lambda i, j, k: k}
    size = {"m": tm, "n": tn, "k": tk}

    def spec(blocked, row_dim, col_dim):
        rf, cf = pick[row_dim], pick[col_dim]
        if blocked:
            return pl.BlockSpec((None, size[row_dim], size[col_dim]), lambda i, j, k: (cf(i, j, k), rf(i, j, k), 0))
        return pl.BlockSpec((size[row_dim], size[col_dim]), lambda i, j, k: (rf(i, j, k), cf(i, j, k)))

    a_spec = spec(aw, "k" if mode == "tn" else "m", a_col)
    b_spec = spec(bw, "n" if mode == "nt" else "k", b_col)
    o_spec = spec(out_blocks, "m", "n")
    epi_fn, epi_ins, epi_dtypes = epi if epi is not None else (None, [], [out_dtype])
    epi_ins = [e if isinstance(e, tuple) else (e, None) for e in epi_ins]
    n_epi = len(epi_ins)
    n_in = 2 + has_add + n_epi + len(deps)
    n_out = len(epi_dtypes)

    def body(*refs):
        a_ref, b_ref = refs[0], refs[1]
        add_ref = refs[2] if has_add else None
        epi_refs = refs[2 + has_add:2 + has_add + n_epi]
        o_refs = refs[n_in:n_in + n_out]
        part = lax.dot_general(a_ref[...].astype(BF16), b_ref[...].astype(BF16), dn, preferred_element_type=F32)

        def finish(res):
            outs = epi_fn(res, *[e[...] for e in epi_refs]) if epi_fn is not None else (res,)
            for o_ref, val in zip(o_refs, outs):
                o_ref[...] = val.astype(o_ref.dtype)

        if nk == 1:
            finish(part + add_ref[...] if has_add else part)
            return
        acc_ref = refs[-1]
        kk = pl.program_id(2)

        @pl.when(kk == 0)
        def _():
            acc_ref[...] = part + add_ref[...] if has_add else part

        @pl.when(kk > 0)
        def _():
            acc_ref[...] += part

        @pl.when(kk == nk - 1)
        def _():
            finish(acc_ref[...])

    def epi_spec(arr, off):
        if off is None:
            return o_spec
        assert off % tn == 0
        return pl.BlockSpec((tm, tn), lambda i, j, k: (i, j + off // tn))

    ins = [a, b] + ([add] if has_add else []) + [arr for arr, _ in epi_ins] + list(deps)
    in_specs = ([a_spec, b_spec] + ([o_spec] if has_add else []) + [epi_spec(arr, off) for arr, off in epi_ins]
                + [pl.BlockSpec(d.shape, lambda i, j, k, nd=d.ndim: (0,) * nd) for d in deps])
    o_shape = (out_blocks, M, tn) if out_blocks else (M, N)
    res = pl.pallas_call(
        body, name=name, grid=(M // tm, N // tn, nk), in_specs=in_specs, out_specs=[o_spec] * n_out,
        out_shape=[jax.ShapeDtypeStruct(o_shape, dt) for dt in epi_dtypes],
        scratch_shapes=[pltpu.VMEM((tm, tn), F32)] if nk > 1 else [],
        compiler_params=_params())(*ins)
    return res[0] if epi is None else list(res)


def _rowwise(fn, rows, pars, row_outs, acc_outs, *, name, tm=256, deps=()):
    rows = [r if isinstance(r, tuple) else (r, r.shape[1], 0) for r in rows]
    row_outs = [o if len(o) == 5 else (o[0], o[1], o[0], 0, None) for o in row_outs]
    aliased = [(k, o[4]) for k, o in enumerate(row_outs) if o[4] is not None]
    R = rows[0][0].shape[0]
    if max(w for _, w, _ in rows) > 4096:
        tm = tm // 2
    tm = min(tm, R)
    assert R % tm == 0
    nr, npar = len(rows), len(pars)
    nro = len(row_outs)
    n_in = nr + npar + len(deps) + len(aliased)

    def body(*refs):
        rv = [r[...] for r in refs[:nr]]
        pv = [p[...] for p in refs[nr:nr + npar]]
        outs = refs[n_in:]
        ro, ao = fn(rv, pv)
        first = pl.program_id(0) == 0
        for o_ref, val in zip(outs[:nro], ro):
            o_ref[...] = val.astype(o_ref.dtype)

        @pl.when(first)
        def _():
            for o_ref, val in zip(outs[nro:], ao):
                o_ref[...] = val

        @pl.when(jnp.logical_not(first))
        def _():
            for o_ref, val in zip(outs[nro:], ao):
                o_ref[...] += val

    in_specs = ([pl.BlockSpec((tm, w), lambda i, cb=cb: (i, cb)) for _, w, cb in rows]
                + [pl.BlockSpec(p.shape, lambda i, nd=p.ndim: (0,) * nd) for p in list(pars) + list(deps)]
                + [pl.BlockSpec(memory_space=pl.ANY)] * len(aliased))
    out_shape = ([jax.ShapeDtypeStruct((R, full), dt) for _, dt, full, _, _ in row_outs]
                 + [jax.ShapeDtypeStruct(s, F32) for s in acc_outs])
    out_specs = ([pl.BlockSpec((tm, f), lambda i, cb=cb: (i, cb)) for f, _, _, cb, _ in row_outs]
                 + [pl.BlockSpec(s, lambda i, nd=len(s): (0,) * nd) for s in acc_outs])
    res = pl.pallas_call(body, name=name, grid=(R // tm,), in_specs=in_specs, out_specs=out_specs, out_shape=out_shape,
                         input_output_aliases={n_in - len(aliased) + q: k for q, (k, _) in enumerate(aliased)},
                         compiler_params=_params())(*[r for r, _, _ in rows], *pars, *deps, *[buf for _, buf in aliased])
    return list(res)


def _bdot(a, b, mode="nn"):
    dn = {"nn": (((1,), (0,)), ((), ())), "nt": (((1,), (1,)), ((), ())), "tn": (((0,), (0,)), ((), ()))}[mode]
    return lax.dot_general(a.astype(BF16), b.astype(BF16), dn, preferred_element_type=F32)


def _sigmoid(x):
    return jax.nn.sigmoid(x)


def _softplus(x):
    return jnp.maximum(x, 0.0) + jnp.log1p(jnp.exp(-jnp.abs(x)))


def _gelu(z):
    return 0.5 * z * (1.0 + lax.erf(z * _SQRT_HALF))


def _gelu_grad(z):
    return 0.5 * (1.0 + lax.erf(z * _SQRT_HALF)) + z * jnp.exp(-0.5 * z * z) * _INV_SQRT_2PI


def _mean(x):
    return jnp.mean(x, axis=-1, keepdims=True)


def _colsum(x):
    return jnp.sum(x, axis=0, keepdims=True)


def _rms_fwd(x, g, name, deps=()):
    def fn(rv, pv):
        (xv,), (gv,) = rv, pv
        r = lax.rsqrt(_mean(xv * xv) + RMS_EPS)
        return [xv * r * gv], []
    return _rowwise(fn, [x], [g], [(x.shape[1], BF16)], [], name=name, deps=deps)[0]


def _rms_bwd(dn, x, dres, g, name, deps=()):
    def fn(rv, pv):
        (dnv, xv, drv), (gv,) = rv, pv
        r = lax.rsqrt(_mean(xv * xv) + RMS_EPS)
        yn = xv * r
        dyg = dnv * gv
        dx = drv + r * (dyg - yn * _mean(dyg * yn))
        return [dx, dx], [_colsum(dnv * yn)]
    D = x.shape[1]
    return _rowwise(fn, [dn, x, dres], [g], [(D, F32), (D, BF16)], [(1, D)], name=name, deps=deps)


def _rwkv_layout(RW, Lw, La, Lg, D):
    widths = [RW, RW, RW, Lw, La, Lg]
    pw = [_ceil_to(w, LANE) for w in widths]
    pw[5] += _ceil_to(sum(pw), 2 * D) - sum(pw)
    offs = [sum(pw[:i]) for i in range(6)]
    return widths, pw, offs, sum(pw)


def _pad_rwkv_cols(a, lay):
    widths, pw, _, _ = lay
    pieces, src = [], 0
    for w, p in zip(widths, pw):
        pieces.append(a[:, src:src + w])
        if p > w:
            pieces.append(jnp.zeros((a.shape[0], p - w), a.dtype))
        src += w
    return jnp.concatenate(pieces, axis=1)


def _unpad_rwkv_cols(a, lay):
    widths, _, offs, _ = lay
    return jnp.concatenate([a[:, o:o + w] for o, w in zip(offs, widths)], axis=1)


def _proj_pieces(lay, D, cs):
    widths, _, offs, rcp = lay
    rc = sum(widths)
    segs = [(sum(widths[:j]), widths[j], offs[j]) for j in range(6)] + [(rc, D, rcp + 2 * D), (rc + D, D, rcp), (rc + 2 * D, D, rcp + D)]
    pieces = []
    for start, width, dst in segs:
        n = start
        while n < start + width:
            d, off = divmod(n, cs)
            take = min(cs - off, start + width - n)
            pieces.append((d, off, dst + n - start, take))
            n += take
    return pieces


def _w_in_to_proj(g, lay, D, name):
    nb, rows, cs = g.shape
    icp = lay[3] + 3 * D
    pieces = _proj_pieces(lay, D, cs)
    tm = _pick(rows, (256, 128, 64, 32, 16))

    def body(i_ref, o_ref):
        o_ref[...] = jnp.zeros_like(o_ref)
        for d, src, dst, w in pieces:
            o_ref[:, dst:dst + w] = i_ref[d, :, src:src + w]

    return pl.pallas_call(
        body, name=name, grid=(rows // tm,), in_specs=[pl.BlockSpec((nb, tm, cs), lambda i: (0, i, 0))],
        out_specs=pl.BlockSpec((tm, icp), lambda i: (i, 0)), out_shape=jax.ShapeDtypeStruct((rows, icp), g.dtype),
        compiler_params=_params())(g)


def _dw_in_from_proj(a, lay, D, cs, name):
    rows, icp = a.shape
    pieces = _proj_pieces(lay, D, cs)
    tm = _pick(rows, (256, 128, 64, 32, 16))

    def body(i_ref, o_ref):
        for d, src, dst, w in pieces:
            o_ref[d, :, src:src + w] = i_ref[:, dst:dst + w]

    return pl.pallas_call(
        body, name=name, grid=(rows // tm,), in_specs=[pl.BlockSpec((tm, icp), lambda i: (i, 0))],
        out_specs=pl.BlockSpec((N_DEV, tm, cs), lambda i: (0, i, 0)), out_shape=jax.ShapeDtypeStruct((N_DEV, rows, cs), a.dtype),
        compiler_params=_params())(a)


def _pad_rows(a, rows):
    return a if a.shape[0] == rows else jnp.concatenate([a, jnp.zeros((rows - a.shape[0], a.shape[1]), a.dtype)], axis=0)


def _token_shift(p, halo, mu, i):
    tm = p.shape[0]
    hid = lax.broadcasted_iota(jnp.int32, (SUBLANE, 1), 0)
    before = jnp.sum(jnp.where(hid == SUBLANE - 1, halo, 0.0), axis=0, keepdims=True)
    before = jnp.where(i == 0, 0.0, before)
    rid = lax.broadcasted_iota(jnp.int32, (tm, 1), 0)
    prev = jnp.where(rid == 0, before, pltpu.roll(p, 1, 0))
    d = prev - p
    return p + d * mu, d


def _rwkv_math(ps, w0, a0, k_k, k_a, wlw, wla, wlg, lay):
    _, pw, offs, _ = lay
    r, k, v, xw, xa, xg = (ps[:, offs[j]:offs[j] + pw[j]] for j in range(6))
    tw = jnp.tanh(xw)
    ww = w0 + _bdot(tw, wlw)
    lw = -jnp.exp(-_softplus(-ww) - 0.5)
    a = _sigmoid(a0 + _bdot(xa, wla))
    sg = _sigmoid(xg)
    g = _bdot(sg, wlg)
    return dict(r=r, k=k, v=v, xa=xa, tw=tw, ww=ww, lw=lw, a=a, sg=sg, g=g, kkp=k * k_k, k2=k * (1.0 + (a - 1.0) * k_a))


def _halo_specs(T, tm, width, after):
    hb = tm // SUBLANE
    last = T // SUBLANE - 1
    if after:
        return pl.BlockSpec((SUBLANE, width), lambda i: (jnp.minimum((i + 1) * hb, last), 0))
    return pl.BlockSpec((SUBLANE, width), lambda i: (jnp.maximum(i * hb - 1, 0), 0))


def _rowsum(x):
    return jnp.sum(x, axis=-1, keepdims=True)


def _kk_math(kkp):
    nrm = jnp.sqrt(_rowsum(kkp * kkp))
    inv = 1.0 / jnp.maximum(nrm, 1e-12)
    return nrm, inv, kkp * inv


def _rwkv_pre(p, mu, small, lora, lay, name):
    T, rcp = p.shape[0], lay[3]
    H = lay[0][0] // HEAD
    tm = min(128, T)

    def body(p_ref, ph_ref, mu_ref, w0_ref, a0_ref, kk_ref, ka_ref, wlw_ref, wla_ref, wlg_ref, r_o, lw_o, k2_o, v_o, aa_o, bb_o, g_o):
        ps, _ = _token_shift(p_ref[...], ph_ref[...], mu_ref[...], pl.program_id(0))
        q = _rwkv_math(ps, w0_ref[...], a0_ref[...], kk_ref[...], ka_ref[...], wlw_ref[...], wla_ref[...], wlg_ref[...], lay)
        for h in range(H):
            sl = slice(h * HEAD, (h + 1) * HEAD)
            for o_ref, key in ((r_o, "r"), (lw_o, "lw"), (k2_o, "k2"), (v_o, "v"), (g_o, "g")):
                o_ref[h] = q[key][:, sl]
            _, _, kk = _kk_math(q["kkp"][:, sl])
            aa_o[h] = -kk
            bb_o[h] = kk * q["a"][:, sl]

    whole = lambda arr: pl.BlockSpec(arr.shape, lambda i: (0, 0))
    return pl.pallas_call(
        body, name=name, grid=(T // tm,),
        in_specs=([pl.BlockSpec((tm, rcp), lambda i: (i, 0)), _halo_specs(T, tm, rcp, False), whole(mu)]
                  + [whole(s) for s in small] + [whole(w) for w in lora]),
        out_specs=[pl.BlockSpec((H, tm, HEAD), lambda i: (0, i, 0))] * 7, out_shape=[jax.ShapeDtypeStruct((H, T, HEAD), F32)] * 7,
        compiler_params=_params())(p, p, mu, *small, *lora)


def _rwkv_pre_bwd(p, mu, small, lora, hgrads, lay, name):
    T, rcp = p.shape[0], lay[3]
    widths, pw, offs, _ = lay
    RW = widths[0]
    H = RW // HEAD
    tm = min(128, T)

    def body(p_ref, ph_ref, mu_ref, w0_ref, a0_ref, kk_ref, ka_ref, wlw_ref, wla_ref, wlg_ref,
             dr1, dr2, dk1, dk2b, dv1, dv2, dlw_h, daa, dbb, dg_h,
             dps_ref, dmu_ref, dw0_ref, da0_ref, dkk_ref, dka_ref, dwlw_ref, dwla_ref, dwlg_ref,
             s_dr, s_dk2, s_dv, s_dlw, s_dkkp, s_da, s_dg):
        i = pl.program_id(0)
        ps, dprev = _token_shift(p_ref[...], ph_ref[...], mu_ref[...], i)
        k_k, k_a = kk_ref[...], ka_ref[...]
        q = _rwkv_math(ps, w0_ref[...], a0_ref[...], k_k, k_a, wlw_ref[...], wla_ref[...], wlg_ref[...], lay)
        k, a, lw, ww, tw, sg = q["k"], q["a"], q["lw"], q["ww"], q["tw"], q["sg"]
        for h in range(H):
            sl = slice(h * HEAD, (h + 1) * HEAD)
            s_dr[:, sl] = dr1[h] + dr2[h]
            s_dk2[:, sl] = dk1[h] + dk2b[h]
            s_dv[:, sl] = dv1[h] + dv2[h]
            s_dlw[:, sl] = dlw_h[h]
            s_dg[:, sl] = dg_h[h]
            nrm, inv, kk = _kk_math(q["kkp"][:, sl])
            dbb_h = dbb[h]
            dkk = dbb_h * a[:, sl] - daa[h]
            s_dkkp[:, sl] = jnp.where(nrm > 1e-12, inv * (dkk - kk * _rowsum(dkk * kk)), dkk * inv)
            s_da[:, sl] = dbb_h * kk
        dk2, dkkp, dg = s_dk2[...], s_dkkp[...], s_dg[...]
        dk = dk2 * (1.0 + (a - 1.0) * k_a) + dkkp * k_k
        da = s_da[...] + dk2 * k * k_a
        dpa = da * a * (1.0 - a)
        dww = s_dlw[...] * lw * _sigmoid(-ww)
        dxa = _bdot(dpa, wla_ref[...], "nt")
        dxw = _bdot(dww, wlw_ref[...], "nt") * (1.0 - tw * tw)
        dxg = _bdot(dg, wlg_ref[...], "nt") * sg * (1.0 - sg)
        segs = (s_dr[...], dk, s_dv[...], dxw, dxa, dxg)
        sums = [dmu_ref, dw0_ref, da0_ref, dkk_ref, dka_ref, dwlw_ref, dwla_ref, dwlg_ref]

        @pl.when(i == 0)
        def _():
            for s in sums:
                s[...] = jnp.zeros_like(s)

        for j, seg in enumerate(segs):
            sl = slice(offs[j], offs[j] + pw[j])
            dps_ref[:, sl] = seg
            dmu_ref[:, sl] += _colsum(seg * dprev[:, sl])
        dw0_ref[...] += _colsum(dww)
        da0_ref[...] += _colsum(dpa)
        dkk_ref[...] += _colsum(dkkp * k)
        dka_ref[...] += _colsum(dk2 * k * (a - 1.0))
        dwlw_ref[...] += _bdot(tw, dww, "tn")
        dwla_ref[...] += _bdot(q["xa"], dpa, "tn")
        dwlg_ref[...] += _bdot(sg, dg, "tn")

    whole = lambda arr: pl.BlockSpec(arr.shape, lambda i: (0, 0))
    row = lambda w: pl.BlockSpec((tm, w), lambda i: (i, 0))
    acc_shapes = [(1, rcp), (1, RW), (1, RW), (1, RW), (1, RW)] + [w.shape for w in lora]
    return pl.pallas_call(
        body, name=name, grid=(T // tm,),
        in_specs=([row(rcp), _halo_specs(T, tm, rcp, False), whole(mu)] + [whole(s) for s in small] + [whole(w) for w in lora]
                  + [pl.BlockSpec((H, tm, HEAD), lambda i: (0, i, 0))] * 10),
        out_specs=[row(rcp)] + [pl.BlockSpec(s, lambda i: (0, 0)) for s in acc_shapes],
        out_shape=[jax.ShapeDtypeStruct((T, rcp), F32)] + [jax.ShapeDtypeStruct(s, F32) for s in acc_shapes],
        scratch_shapes=[pltpu.VMEM((tm, RW), F32)] * 7, compiler_params=_params())(p, p, mu, *small, *lora, *hgrads)


def _shift_bwd(dps, mu, dproj, name):
    T, rcp = dps.shape
    tm = min(256, T)
    nt = T // tm

    def body(d_ref, dh_ref, mu_ref, buf_ref, o_ref):
        i = pl.program_id(0)
        d = d_ref[...]
        hid = lax.broadcasted_iota(jnp.int32, (SUBLANE, 1), 0)
        after = jnp.sum(jnp.where(hid == 0, dh_ref[...], 0.0), axis=0, keepdims=True)
        after = jnp.where(i == nt - 1, 0.0, after)
        rid = lax.broadcasted_iota(jnp.int32, (tm, 1), 0)
        nxt = jnp.where(rid == tm - 1, after, pltpu.roll(d, tm - 1, 0))
        mu_v = mu_ref[...]
        o_ref[...] = (d * (1.0 - mu_v) + nxt * mu_v).astype(BF16)

    row = pl.BlockSpec((tm, rcp), lambda i: (i, 0))
    return pl.pallas_call(
        body, name=name, grid=(nt,),
        in_specs=[row, _halo_specs(T, tm, rcp, True), pl.BlockSpec(mu.shape, lambda i: (0, 0)), pl.BlockSpec(memory_space=pl.ANY)],
        out_specs=row, out_shape=jax.ShapeDtypeStruct(dproj.shape, BF16), input_output_aliases={3: 0},
        compiler_params=_params())(dps, dps, mu, dproj)


def _head_post_math(y, r, k2, v, lg, lb, rk):
    yc = y - _mean(y)
    rstd = lax.rsqrt(_mean(yc * yc) + LNX_EPS)
    yn = yc * rstd
    s = _rowsum(r * k2 * rk)
    return yn, rstd, yn * lg + lb + s * v, s


def _head_post(y, r, k2, v, g, hp, name, deps=()):
    H, T, _ = y.shape
    tm = min(128, T)

    def body(y_ref, r_ref, k_ref, v_ref, g_ref, lg_ref, lb_ref, rk_ref, *rest):
        o_ref = rest[-1]
        _, _, t, _ = _head_post_math(y_ref[...], r_ref[...], k_ref[...], v_ref[...], lg_ref[...], lb_ref[...], rk_ref[...])
        out = (t * g_ref[...]).astype(BF16)
        for h in range(H):
            o_ref[:, h * HEAD:(h + 1) * HEAD] = out[h]

    blk = pl.BlockSpec((H, tm, HEAD), lambda i: (0, i, 0))
    par = pl.BlockSpec((H, 1, HEAD), lambda i: (0, 0, 0))
    return pl.pallas_call(
        body, name=name, grid=(T // tm,),
        in_specs=[blk] * 5 + [par] * 3 + [pl.BlockSpec(d.shape, lambda i, nd=d.ndim: (0,) * nd) for d in deps],
        out_specs=pl.BlockSpec((tm, H * HEAD), lambda i: (i, 0)),
        out_shape=jax.ShapeDtypeStruct((T, H * HEAD), BF16), compiler_params=_params())(y, r, k2, v, g, *hp, *deps)


def _head_post_bwd(dya, y, r, k2, v, g, hp, name, deps=()):
    H, T, _ = y.shape
    tm = min(128, T)
    hsum = lambda t: jnp.sum(t, axis=1, keepdims=True)

    def body(d_ref, y_ref, r_ref, k_ref, v_ref, g_ref, lg_ref, lb_ref, rk_ref, *rest):
        outs, d_s = rest[len(deps):len(deps) + 8], rest[-1]
        for h in range(H):
            d_s[h] = d_ref[:, h * HEAD:(h + 1) * HEAD]
        d_v, r_v, k_v, v_v, lg, rk = d_s[...], r_ref[...], k_ref[...], v_ref[...], lg_ref[...], rk_ref[...]
        yn, rstd, t, s = _head_post_math(y_ref[...], r_v, k_v, v_v, lg, lb_ref[...], rk)
        dyo = d_v * g_ref[...]
        dyn = dyo * lg
        ds = _rowsum(dyo * v_v)
        vals = (rstd * (dyn - _mean(dyn) - yn * _mean(dyn * yn)), ds * k_v * rk, ds * r_v * rk, dyo * s, d_v * t)
        for o_ref, val in zip(outs[:5], vals):
            o_ref[...] = val
        sums = (hsum(dyo * yn), hsum(dyo), hsum(ds * r_v * k_v))
        first = pl.program_id(0) == 0

        @pl.when(first)
        def _():
            for o_ref, val in zip(outs[5:], sums):
                o_ref[...] = val

        @pl.when(jnp.logical_not(first))
        def _():
            for o_ref, val in zip(outs[5:], sums):
                o_ref[...] += val

    blk = pl.BlockSpec((H, tm, HEAD), lambda i: (0, i, 0))
    par = pl.BlockSpec((H, 1, HEAD), lambda i: (0, 0, 0))
    return pl.pallas_call(
        body, name=name, grid=(T // tm,),
        in_specs=([pl.BlockSpec((tm, H * HEAD), lambda i: (i, 0))] + [blk] * 5 + [par] * 3
                  + [pl.BlockSpec(d.shape, lambda i, nd=d.ndim: (0,) * nd) for d in deps]),
        out_specs=[blk] * 5 + [par] * 3,
        out_shape=[jax.ShapeDtypeStruct((H, T, HEAD), F32)] * 5 + [jax.ShapeDtypeStruct((H, 1, HEAD), F32)] * 3,
        scratch_shapes=[pltpu.VMEM((H, tm, HEAD), F32)], compiler_params=_params())(dya, y, r, k2, v, g, *hp, *deps)


def _bmm(x, y, mode):
    dn = {"nn": (((2,), (1,)), ((0,), (0,))), "nt": (((2,), (2,)), ((0,), (0,))), "tn": (((1,), (1,)), ((0,), (0,)))}[mode]
    (xh, xl), (yh, yl) = _split(x), _split(y)
    dot = lambda p, q: lax.dot_general(p, q, dn, preferred_element_type=F32)
    out = dot(xh, yh)
    if yl is not None:
        out = out + dot(xh, yl)
    if xl is not None:
        out = out + dot(xl, yh)
    return out


def _split(x):
    if isinstance(x, tuple):
        return x
    hi = x.astype(BF16)
    return hi, (x - hi.astype(F32)).astype(BF16)


def _exact(x):
    return x.astype(BF16), None


def _round(x):
    return x if isinstance(x, tuple) else (x.astype(BF16), None)


def _rows(*xs):
    if isinstance(xs[0], tuple):
        return tuple(None if any(p is None for p in parts) else jnp.concatenate(parts, axis=1) for parts in zip(*xs))
    return jnp.concatenate(xs, axis=1)


def _wkv_chunk(r, lw, k, v, a, b):
    hb, C, _ = r.shape
    ti = lax.broadcasted_iota(jnp.int32, (C, C), 0)
    si = lax.broadcasted_iota(jnp.int32, (C, C), 1)
    linc, lstr, eye = (ti >= si).astype(F32), (ti > si).astype(F32), (ti == si).astype(F32)
    qmask = jnp.concatenate([jnp.concatenate([lstr, lstr], axis=1), jnp.concatenate([linc, linc], axis=1)], axis=0)
    lincb = _exact(jnp.broadcast_to(linc, (hb, C, C)))
    both = _exact(jnp.broadcast_to(jnp.concatenate([linc, lstr], axis=0), (hb, 2 * C, C)))
    ones = _exact(jnp.ones_like(v))
    lws = _split(lw)
    ci = _bmm(lincb, lws, "nn")
    cC = jnp.sum(lw, axis=1, keepdims=True)
    gi, ge, gn, gr = jnp.exp(ci), jnp.exp(ci - lw), jnp.exp(-ci), jnp.exp(cC - ci)
    q = dict(At=a * ge, Rt=r * gi, Bt=b * gn, Kt=k * gn, Bh=b * gr, Kh=k * gr)
    s = dict(AR=_round(_rows(q["At"], q["Rt"])), BK=_round(_rows(q["Bt"], q["Kt"])), BKh=_round(_rows(q["Bh"], q["Kh"])), v=_round(v))
    quad = _bmm(s["AR"], s["BK"], "nt") * qmask
    s["top"], s["bot"] = _round(quad[:, :C]), _round(quad[:, C:])
    A_ab = quad[:, :C, :C]
    Tm = eye + A_ab
    Pw = _round(A_ab)
    n = 1
    while 2 * n < C:
        Pw = _round(_bmm(Pw, Pw, "nn"))
        Tm = Tm + _bmm(_round(Tm), Pw, "nn")
        n *= 2
    s["Tm"] = _round(Tm)
    gC = jnp.exp(_bmm(lws, ones, "tn"))
    q.update(gi=gi, ge=ge, gn=gn, gr=gr, qmask=qmask, both=both, gC=gC, ones=ones, s=s)
    return q


def _wkv_u(s, H0s, C):
    arh = _bmm(s["AR"], H0s, "nn")
    zv = _rows(tuple(None if p is None else jnp.zeros_like(p) for p in s["v"]), s["v"])
    U = _bmm(s["Tm"], _round(arh[:, :C] + _bmm(s["top"], zv, "nn")), "nn")
    return arh, _rows(_round(U), s["v"])


def _wkv_fwd(r, lw, k, v, a, b, name):
    H, T, N = r.shape
    C = min(WKV_CHUNK, T)
    nc = T // C
    hb = _pick(H, (16, 8, 4, 2))

    def body(r_ref, lw_ref, k_ref, v_ref, a_ref, b_ref, y_ref, st_ref, h_ref):
        @pl.when(pl.program_id(1) == 0)
        def _():
            h_ref[...] = jnp.zeros_like(h_ref)

        H0 = h_ref[...]
        st_ref[0] = H0
        q = _wkv_chunk(r_ref[...], lw_ref[...], k_ref[...], v_ref[...], a_ref[...], b_ref[...])
        s = q["s"]
        arh, UV = _wkv_u(s, _round(H0), C)
        y_ref[...] = arh[:, C:] + _bmm(s["bot"], UV, "nn")
        h_ref[...] = q["gC"] * H0 + _bmm(s["BKh"], UV, "tn")

    blk = pl.BlockSpec((hb, C, N), lambda h, c: (h, c, 0))
    return pl.pallas_call(
        body, name=name, grid=(H // hb, nc), in_specs=[blk] * 6,
        out_specs=[blk, pl.BlockSpec((1, hb, N, N), lambda h, c: (c, h, 0, 0))],
        out_shape=[jax.ShapeDtypeStruct((H, T, N), F32), jax.ShapeDtypeStruct((nc, H, N, N), F32)],
        scratch_shapes=[pltpu.VMEM((hb, N, N), F32)], compiler_params=_params())(r, lw, k, v, a, b)


def _wkv_bwd(r, lw, k, v, a, b, states, dy, name):
    H, T, N = r.shape
    C = min(WKV_CHUNK, T)
    nc = T // C
    hb = _pick(H, (16, 8, 4, 2))

    def body(r_ref, lw_ref, k_ref, v_ref, a_ref, b_ref, st_ref, dy_ref, dr_ref, dlw_ref, dk_ref, dv_ref, da_ref, db_ref, dh_ref):
        @pl.when(pl.program_id(1) == 0)
        def _():
            dh_ref[...] = jnp.zeros_like(dh_ref)

        dHC = dh_ref[...]
        H0 = st_ref[0]
        q = _wkv_chunk(r_ref[...], lw_ref[...], k_ref[...], v_ref[...], a_ref[...], b_ref[...])
        s, gC = q["s"], q["gC"]
        H0s, dHs, dY = _round(H0), _round(dHC), _round(dy_ref[...])
        _, UV = _wkv_u(s, H0s, C)
        bot_dy = _bmm(s["bot"], dY, "tn")
        bkh_dh = _bmm(s["BKh"], dHs, "nn")
        dP = _round(_bmm(s["Tm"], _round(bot_dy[:, :C] + bkh_dh[:, :C]), "tn"))
        dv_ref[...] = bot_dy[:, C:] + bkh_dh[:, C:] + _bmm(s["top"], dP, "tn")[:, C:]
        dPY = _rows(dP, dY)
        dh_ref[...] = gC * dHC + _bmm(s["AR"], dPY, "tn")
        dquad = _round(_bmm(dPY, UV, "nt") * q["qmask"])
        dAR = _bmm(dPY, H0s, "nt") + _bmm(dquad, s["BK"], "nn")
        dBK = _bmm(dquad, s["AR"], "tn")
        dBKh = _bmm(UV, dHs, "nt")
        dAt, dRt, dBt, dKt, dBh, dKh = dAR[:, :C], dAR[:, C:], dBK[:, :C], dBK[:, C:], dBKh[:, :C], dBKh[:, C:]
        dr_ref[...] = dRt * q["gi"]
        da_ref[...] = dAt * q["ge"]
        db_ref[...] = dBt * q["gn"] + dBh * q["gr"]
        dk_ref[...] = dKt * q["gn"] + dKh * q["gr"]
        tail = dBh * q["Bh"] + dKh * q["Kh"]
        dci = dRt * q["Rt"] - dBt * q["Bt"] - dKt * q["Kt"] - tail
        dcC = jnp.sum(tail, axis=1, keepdims=True) + _bmm(q["ones"], H0 * dHC * gC, "nt")
        dlw_ref[...] = _bmm(q["both"], _rows(dci, dAt * q["At"]), "tn") + dcC

    blk = pl.BlockSpec((hb, C, N), lambda h, c: (h, nc - 1 - c, 0))
    st = pl.BlockSpec((1, hb, N, N), lambda h, c: (nc - 1 - c, h, 0, 0))
    return pl.pallas_call(
        body, name=name, grid=(H // hb, nc), in_specs=[blk] * 6 + [st, blk], out_specs=[blk] * 6,
        out_shape=[jax.ShapeDtypeStruct((H, T, N), F32)] * 6,
        scratch_shapes=[pltpu.VMEM((hb, N, N), F32)], compiler_params=_params())(r, lw, k, v, a, b, states, dy)


def _sgu_ln(z, SW, lng, lnb):
    ge = _gelu(z)
    u, vv = ge[:, :SW], ge[:, SW:]
    xc = vv - _mean(vv)
    rstd = lax.rsqrt(_mean(xc * xc) + LN_EPS)
    vn = xc * rstd
    return u, vn, rstd, vn * lng + lnb


def _causal(ws_ref, g):
    ti = lax.broadcasted_iota(jnp.int32, (SGU_CHUNK, SGU_CHUNK), 0)
    si = lax.broadcasted_iota(jnp.int32, (SGU_CHUNK, SGU_CHUNK), 1)
    return ti >= si, jnp.where(ti >= si, ws_ref[g], 0.0).astype(BF16)


def _sgu_fwd(proj, zblock, lng, lnb, ws, bexp, name):
    T, SW = proj.shape[0], lng.shape[1]
    G = ws.shape[0]
    tr = min(256, T)
    nch = tr // SGU_CHUNK

    def body(z_ref, lng_ref, lnb_ref, ws_ref, be_ref, o_ref):
        u, _, _, vl = _sgu_ln(z_ref[...], SW, lng_ref[...], lnb_ref[...])
        for g in range(G):
            cs = slice(g * SGU_GROUP, (g + 1) * SGU_GROUP)
            _, wc = _causal(ws_ref, g)
            for n in range(nch):
                rs = slice(n * SGU_CHUNK, (n + 1) * SGU_CHUNK)
                m = jnp.dot(wc, vl[rs, cs].astype(BF16), preferred_element_type=F32) + be_ref[:, cs]
                o_ref[rs, cs] = (u[rs, cs] * m).astype(BF16)

    whole = lambda arr: pl.BlockSpec(arr.shape, lambda i, nd=arr.ndim: (0,) * nd)
    return pl.pallas_call(
        body, name=name, grid=(T // tr,),
        in_specs=[pl.BlockSpec((tr, 2 * SW), lambda i: (i, zblock)), whole(lng), whole(lnb), whole(ws), whole(bexp)],
        out_specs=pl.BlockSpec((tr, SW), lambda i: (i, 0)), out_shape=jax.ShapeDtypeStruct((T, SW), BF16),
        compiler_params=_params())(proj, lng, lnb, ws, bexp)


def _sgu_bwd(proj, zblock, dyb, lng, lnb, ws, bexp, dproj, name):
    T, SW = proj.shape[0], lng.shape[1]
    G = ws.shape[0]
    tr = min(256, T)
    nch = tr // SGU_CHUNK
    nt = T // tr

    def body(z_ref, dy_ref, lng_ref, lnb_ref, ws_ref, be_ref, buf_ref, dz_ref, dlg_ref, dlb_ref, dws_ref, db_ref, du_s, dvl_s, dbacc_s):
        i = pl.program_id(0)
        zv = z_ref[...]
        lng_v = lng_ref[...]
        u, vn, rstd, vl = _sgu_ln(zv, SW, lng_v, lnb_ref[...])

        @pl.when(i == 0)
        def _():
            for s in (dlg_ref, dlb_ref, dws_ref, dbacc_s):
                s[...] = jnp.zeros_like(s)

        for g in range(G):
            cs = slice(g * SGU_GROUP, (g + 1) * SGU_GROUP)
            tri, wc = _causal(ws_ref, g)
            for n in range(nch):
                rs = slice(n * SGU_CHUNK, (n + 1) * SGU_CHUNK)
                blk = vl[rs, cs].astype(BF16)
                m = jnp.dot(wc, blk, preferred_element_type=F32) + be_ref[:, cs]
                dyv = dy_ref[rs, cs]
                du_s[rs, cs] = dyv * m
                dm = dyv * u[rs, cs]
                dvl_s[rs, cs] = _bdot(wc, dm, "tn")
                dws_ref[g] += jnp.where(tri, _bdot(dm, blk, "nt"), 0.0)
                dbacc_s[:, cs] += dm

        dvl = dvl_s[...]
        dlg_ref[...] += _colsum(dvl * vn)
        dlb_ref[...] += _colsum(dvl)
        dvn = dvl * lng_v
        dvv = rstd * (dvn - _mean(dvn) - vn * _mean(dvn * vn))
        gp = _gelu_grad(zv)
        dz_ref[:, :SW] = (du_s[...] * gp[:, :SW]).astype(BF16)
        dz_ref[:, SW:] = (dvv * gp[:, SW:]).astype(BF16)

        @pl.when(i == nt - 1)
        def _():
            lane = lax.broadcasted_iota(jnp.int32, (SGU_CHUNK, LANE), 1)
            out = jnp.zeros((SGU_CHUNK, LANE), F32)
            for g in range(G):
                col = jnp.sum(dbacc_s[:, g * SGU_GROUP:(g + 1) * SGU_GROUP], axis=1, keepdims=True)
                out = jnp.where(lane == g, col, out)
            db_ref[...] = out

    whole = lambda arr: pl.BlockSpec(arr.shape, lambda i, nd=arr.ndim: (0,) * nd)
    acc_shapes = [(1, SW), (1, SW), ws.shape, (SGU_CHUNK, LANE)]
    return pl.pallas_call(
        body, name=name, grid=(nt,),
        in_specs=[pl.BlockSpec((tr, 2 * SW), lambda i: (i, zblock)), pl.BlockSpec((tr, SW), lambda i: (i, 0)),
                  whole(lng), whole(lnb), whole(ws), whole(bexp), pl.BlockSpec(memory_space=pl.ANY)],
        out_specs=([pl.BlockSpec((tr, 2 * SW), lambda i: (i, zblock))]
                   + [pl.BlockSpec(s, lambda i, nd=len(s): (0,) * nd) for s in acc_shapes]),
        out_shape=[jax.ShapeDtypeStruct(dproj.shape, BF16)] + [jax.ShapeDtypeStruct(s, F32) for s in acc_shapes],
        scratch_shapes=[pltpu.VMEM((tr, SW), F32), pltpu.VMEM((tr, SW), F32), pltpu.VMEM((SGU_CHUNK, SW), F32)],
        input_output_aliases={6: 0}, compiler_params=_params())(proj, dyb, lng, lnb, ws, bexp, dproj)


_HBM = pl.BlockSpec(memory_space=pltpu.HBM)
_SEM = pl.BlockSpec(memory_space=pltpu.SEMAPHORE)
_DATAFLOW = pltpu.SideEffectType.DATAFLOW_SIDE_EFFECTING


def _mesh_place(chips=False):
    x, y, c = lax.axis_index("x"), lax.axis_index("y"), lax.axis_index("c")
    return x, y, c, (2 * x + y if chips else 4 * x + 2 * y + c)


def _peer(x, y, c, rel, chips=False):
    px = 1 - x if rel & 4 else x
    py = 1 - y if rel & 2 else y
    pc = 1 - c if rel & 1 else c
    return (px, py, pc), (2 * px + py if chips else 4 * px + 2 * py + pc)


ALL_PEERS = tuple(range(1, N_DEV))
SIBLING = (1,)
SAME_CORE = (2, 4, 6)
SIBLINGS_CORE = (3, 5, 7)


def _exchange_start(groups, name, rels=ALL_PEERS, chips=False):
    flat = [t for g in groups for t in g]
    sizes = [len(g) for g in groups]
    n, ng = len(flat), len(groups)
    srcs = [pltpu.with_memory_space_constraint(a, pltpu.HBM) for a, _ in flat]
    lands = [pltpu.with_memory_space_constraint(lax.empty(((N_DEV,) + a.shape) if isg else a.shape, a.dtype), pltpu.HBM)
             for a, isg in flat]

    def body(*refs):
        ins, lnd, sems, token = refs[:n], refs[n:2 * n], refs[2 * n:2 * n + 3 * ng], refs[-1]
        x, y, c, me = _mesh_place(chips)
        j0 = 0
        for gi, sz in enumerate(sizes):
            for rel in rels:
                dev, slot = _peer(x, y, c, rel, chips)
                for jj in range(sz):
                    j = j0 + jj
                    pltpu.make_async_remote_copy(
                        src_ref=ins[j] if flat[j][1] else ins[j].at[slot], dst_ref=lnd[j].at[me],
                        send_sem=sems[3 * gi].at[jj * (N_DEV - 1) + rel - 1], recv_sem=sems[3 * gi + 1].at[jj * (N_DEV - 1) + rel - 1],
                        device_id=dev, device_id_type=pl.DeviceIdType.MESH).start()
            for jj in range(sz):
                j = j0 + jj
                pltpu.make_async_copy(ins[j] if flat[j][1] else ins[j].at[me], lnd[j].at[me], sems[3 * gi + 2].at[jj]).start()
            j0 += sz
        token[...] = jnp.zeros_like(token)

    sem_shapes = [pltpu.SemaphoreType.DMA((k,)) for sz in sizes for k in (sz * (N_DEV - 1), sz * (N_DEV - 1), sz)]
    res = pl.pallas_call(
        body, name=name,
        out_shape=(*sem_shapes, *[pltpu.HBM(a.shape, a.dtype) for a in srcs], *[pltpu.HBM(a.shape, a.dtype) for a in lands],
                   jax.ShapeDtypeStruct((SUBLANE, LANE), F32)),
        in_specs=[_HBM] * (2 * n), out_specs=(*[_SEM] * (3 * ng), *[_HBM] * (2 * n), pl.BlockSpec(memory_space=pltpu.VMEM)),
        input_output_aliases={i: 3 * ng + i for i in range(2 * n)},
        compiler_params=pltpu.CompilerParams(has_side_effects=_DATAFLOW))(*srcs, *lands)
    sems, thru, token = res[:3 * ng], res[3 * ng:3 * ng + 2 * n], res[-1]
    handle, j0 = [], 0
    for gi, sz in enumerate(sizes):
        handle.append(dict(kinds=[k for _, k in groups[gi]], chips=chips, srcs=list(thru[j0:j0 + sz]), lands=list(thru[n + j0:n + j0 + sz]),
                           sems=list(sems[3 * gi:3 * gi + 3])))
        j0 += sz
    return handle, token


def _exchange_wait(group, after, name, rels=ALL_PEERS, local=True):
    kinds, sz = group["kinds"], len(group["kinds"])
    relay = group.get("relay", [])

    def body(*refs):
        ins, lnd, (ssem, rsem, lsem) = refs[:sz], refs[sz:2 * sz], refs[2 * sz:2 * sz + 3]
        x, y, c, me = _mesh_place(group["chips"])
        for rel in rels:
            dev, slot = _peer(x, y, c, rel, group["chips"])
            for jj in range(sz):
                cp = pltpu.make_async_remote_copy(
                    src_ref=ins[jj] if kinds[jj] else ins[jj].at[slot], dst_ref=lnd[jj].at[slot],
                    send_sem=ssem.at[jj * (N_DEV - 1) + rel - 1], recv_sem=rsem.at[jj * (N_DEV - 1) + rel - 1],
                    device_id=dev, device_id_type=pl.DeviceIdType.MESH)
                cp.wait_send()
                cp.wait_recv()
        if local:
            for jj in range(sz):
                pltpu.make_async_copy(ins[jj] if kinds[jj] else ins[jj].at[me], lnd[jj].at[me], lsem.at[jj]).wait()
        if relay:
            fsend, frecv = refs[2 * sz + 3:2 * sz + 5]
            dev = _peer(x, y, c, 1)[0]
            for q, (mine, theirs) in enumerate(zip(SAME_CORE, SIBLINGS_CORE)):
                for jj in range(sz):
                    cp = pltpu.make_async_remote_copy(
                        src_ref=lnd[jj].at[_peer(x, y, c, mine)[1]], dst_ref=lnd[jj].at[_peer(x, y, c, theirs)[1]],
                        send_sem=fsend.at[jj * len(SAME_CORE) + q], recv_sem=frecv.at[jj * len(SAME_CORE) + q],
                        device_id=dev, device_id_type=pl.DeviceIdType.MESH)
                    cp.wait_send()
                    cp.wait_recv()

    arrays = group["srcs"] + group["lands"]
    sems = group["sems"] + relay
    res = pl.pallas_call(
        body, name=name, out_shape=[pltpu.HBM(a.shape, a.dtype) for a in arrays],
        in_specs=[_HBM] * (2 * sz) + [_SEM] * len(sems) + [pl.BlockSpec(memory_space=pl.ANY)], out_specs=[_HBM] * (2 * sz),
        input_output_aliases={i: i for i in range(2 * sz)},
        compiler_params=pltpu.CompilerParams(has_side_effects=_DATAFLOW))(*arrays, *sems, after)
    return dict(group, srcs=list(res[:sz]), lands=list(res[sz:]), relay=[])


def _relay_start(group, name):
    sz = len(group["kinds"])
    nq = len(SAME_CORE)

    def body(*refs):
        lnd, fsend, frecv, token = refs[:sz], refs[sz], refs[sz + 1], refs[-1]
        x, y, c, _ = _mesh_place()
        dev = _peer(x, y, c, 1)[0]
        for q, rel in enumerate(SAME_CORE):
            slot = _peer(x, y, c, rel)[1]
            for jj in range(sz):
                pltpu.make_async_remote_copy(
                    src_ref=lnd[jj].at[slot], dst_ref=lnd[jj].at[slot], send_sem=fsend.at[jj * nq + q], recv_sem=frecv.at[jj * nq + q],
                    device_id=dev, device_id_type=pl.DeviceIdType.MESH).start()
        token[...] = jnp.zeros_like(token)

    lands = group["lands"]
    res = pl.pallas_call(
        body, name=name,
        out_shape=(pltpu.SemaphoreType.DMA((sz * nq,)), pltpu.SemaphoreType.DMA((sz * nq,)), *[pltpu.HBM(a.shape, a.dtype) for a in lands],
                   jax.ShapeDtypeStruct((SUBLANE, LANE), F32)),
        in_specs=[_HBM] * sz, out_specs=(_SEM, _SEM, *[_HBM] * sz, pl.BlockSpec(memory_space=pltpu.VMEM)),
        input_output_aliases={i: 2 + i for i in range(sz)},
        compiler_params=pltpu.CompilerParams(has_side_effects=_DATAFLOW))(*lands)
    return dict(group, lands=list(res[2:2 + sz]), relay=[res[0], res[1]]), res[-1]


def _sibling_swap(arrays, handle, after, name):
    start = handle is None
    n = len(arrays) if start else len(handle["srcs"])
    chips = N_DEV // 2
    if start:
        srcs = [pltpu.with_memory_space_constraint(a.reshape(chips, 2, *a.shape[1:]), pltpu.HBM) for a in arrays]
        lands = [pltpu.with_memory_space_constraint(lax.empty((chips,) + a.shape[1:], a.dtype), pltpu.HBM) for a in arrays]
    else:
        srcs, lands = handle["srcs"], handle["lands"]

    def body(*refs):
        ins, lnd, ssem, rsem = refs[:n], refs[n:2 * n], refs[2 * n], refs[2 * n + 1]
        x, y, c, _ = _mesh_place()
        dev = _peer(x, y, c, 1)[0]
        for q in range(chips):
            for j in range(n):
                cp = pltpu.make_async_remote_copy(
                    src_ref=ins[j].at[q, 1 - c], dst_ref=lnd[j].at[q], send_sem=ssem.at[j * chips + q], recv_sem=rsem.at[j * chips + q],
                    device_id=dev, device_id_type=pl.DeviceIdType.MESH)
                if start:
                    cp.start()
                else:
                    cp.wait_send()
                    cp.wait_recv()
        if start:
            refs[-1][...] = jnp.zeros_like(refs[-1])

    thru = [pltpu.HBM(a.shape, a.dtype) for a in srcs + lands]
    effect = pltpu.CompilerParams(has_side_effects=_DATAFLOW)
    if start:
        res = pl.pallas_call(
            body, name=name, out_shape=(pltpu.SemaphoreType.DMA((n * chips,)), pltpu.SemaphoreType.DMA((n * chips,)), *thru,
                                        jax.ShapeDtypeStruct((SUBLANE, LANE), F32)),
            in_specs=[_HBM] * (2 * n), out_specs=(_SEM, _SEM, *[_HBM] * (2 * n), pl.BlockSpec(memory_space=pltpu.VMEM)),
            input_output_aliases={i: 2 + i for i in range(2 * n)}, compiler_params=effect)(*srcs, *lands)
        return dict(srcs=list(res[2:2 + n]), lands=list(res[2 + n:2 + 2 * n]), sems=[res[0], res[1]]), res[-1]
    res = pl.pallas_call(
        body, name=name, out_shape=thru, in_specs=[_HBM] * (2 * n) + [_SEM, _SEM, pl.BlockSpec(memory_space=pl.ANY)],
        out_specs=[_HBM] * (2 * n), input_output_aliases={i: i for i in range(2 * n)}, compiler_params=effect)(
            *srcs, *lands, *handle["sems"], after)
    return dict(handle, srcs=list(res[:n]), lands=list(res[n:]))


def _pair_add(mine, theirs, core, name):
    chips, _, rows, w = mine.shape
    tm = _pick(rows, (256, 128, 64, 32, 16))

    def body(core_ref, a_ref, b_ref, o_ref):
        o_ref[...] = (a_ref[...].astype(F32) + b_ref[...].astype(F32)).astype(o_ref.dtype)

    return pl.pallas_call(
        body, name=name, out_shape=jax.ShapeDtypeStruct(theirs.shape, theirs.dtype),
        grid_spec=pltpu.PrefetchScalarGridSpec(
            num_scalar_prefetch=1, grid=(chips, rows // tm),
            in_specs=[pl.BlockSpec((None, None, tm, w), lambda q, i, core_ref: (q, core_ref[0], i, 0)),
                      pl.BlockSpec((None, tm, w), lambda q, i, core_ref: (q, i, 0))],
            out_specs=pl.BlockSpec((None, tm, w), lambda q, i, core_ref: (q, i, 0))),
        compiler_params=_params())(core, mine, theirs)


def _adamw(w, m, v, gparts, name, after=None):
    R, C = w.shape
    tm = _pick(R, (256, 128, 64, 32, 16, 8))
    order = [] if after is None else [after]

    def body(w_ref, m_ref, v_ref, g_ref, *rest):
        go, do, mo, vo = rest[len(order):]
        g = g_ref[0].astype(F32)
        for j in range(1, gparts.shape[0]):
            g = g + g_ref[j].astype(F32)
        mn = ADAM_B1 * m_ref[...] + (1.0 - ADAM_B1) * g
        vn = ADAM_B2 * v_ref[...] + (1.0 - ADAM_B2) * (g * g)
        m_hat = mn / (1.0 - ADAM_B1 ** ADAM_STEP)
        v_hat = vn / (1.0 - ADAM_B2 ** ADAM_STEP)
        go[...] = g
        do[...] = -ADAM_LR * (m_hat / (jnp.sqrt(v_hat) + ADAM_EPS) + ADAM_WD * w_ref[...])
        mo[...] = mn
        vo[...] = vn

    row = pl.BlockSpec((tm, C), lambda i: (i, 0))
    return pl.pallas_call(
        body, name=name, grid=(R // tm,),
        in_specs=[row, row, row, pl.BlockSpec((gparts.shape[0], tm, C), lambda i: (0, i, 0))] + [pl.BlockSpec(memory_space=pl.ANY)] * len(order),
        out_specs=[row] * 4, out_shape=[jax.ShapeDtypeStruct((R, C), F32)] * 4, compiler_params=_params())(w, m, v, gparts, *order)


def _pack(arrays):
    parts = []
    for a in arrays:
        f = a.reshape(1, -1)
        pad = _ceil_to(f.shape[1], SUBLANE * LANE) - f.shape[1]
        f = jnp.concatenate([f, jnp.zeros((1, pad), f.dtype)], axis=1) if pad else f
        parts.append(f.reshape(-1, LANE))
    rows = sum(p.shape[0] for p in parts)
    pad = _ceil_to(rows, 64) - rows
    return jnp.concatenate(parts + ([jnp.zeros((pad, LANE), parts[0].dtype)] if pad else []), axis=0)


def _unpack(buf, shapes):
    out, row = [], 0
    for s in shapes:
        size = 1
        for d in s:
            size *= d
        rows = _ceil_to(size, SUBLANE * LANE) // LANE
        out.append(buf[row:row + rows].reshape(1, -1)[:, :size].reshape(s))
        row += rows
    return out


def kernel(x, norm_mix_g, w_in, shift_mu, w0, w_lora_up, a0, a_lora_up, g_lora_up, k_k, k_a, r_k, lnx_g, lnx_b, w_proj_rwkv, sgu_ln_g, sgu_ln_b, sgu_w, sgu_b, w_proj_sgu, w_out, norm_ffn_g, w_ffn_gate, w_ffn_up, w_ffn_down, norm_final_g, loss_target, m_norm_mix_g, m_w_in, m_shift_mu, m_w0, m_w_lora_up, m_a0, m_a_lora_up, m_g_lora_up, m_k_k, m_k_a, m_r_k, m_lnx_g, m_lnx_b, m_w_proj_rwkv, m_sgu_ln_g, m_sgu_ln_b, m_sgu_w, m_sgu_b, m_w_proj_sgu, m_w_out, m_norm_ffn_g, m_w_ffn_gate, m_w_ffn_up, m_w_ffn_down, m_norm_final_g, v_norm_mix_g, v_w_in, v_shift_mu, v_w0, v_w_lora_up, v_a0, v_a_lora_up, v_g_lora_up, v_k_k, v_k_a, v_r_k, v_lnx_g, v_lnx_b, v_w_proj_rwkv, v_sgu_ln_g, v_sgu_ln_b, v_sgu_w, v_sgu_b, v_w_proj_sgu, v_w_out, v_norm_ffn_g, v_w_ffn_gate, v_w_ffn_up, v_w_ffn_down, v_norm_final_g):
    weights = dict(norm_mix_g=norm_mix_g, w_in=w_in, shift_mu=shift_mu, w0=w0, w_lora_up=w_lora_up, a0=a0, a_lora_up=a_lora_up,
                   g_lora_up=g_lora_up, k_k=k_k, k_a=k_a, r_k=r_k, lnx_g=lnx_g, lnx_b=lnx_b, w_proj_rwkv=w_proj_rwkv,
                   sgu_ln_g=sgu_ln_g, sgu_ln_b=sgu_ln_b, sgu_w=sgu_w, sgu_b=sgu_b, w_proj_sgu=w_proj_sgu, w_out=w_out,
                   norm_ffn_g=norm_ffn_g, w_ffn_gate=w_ffn_gate, w_ffn_up=w_ffn_up, w_ffn_down=w_ffn_down, norm_final_g=norm_final_g)
    m_in = dict(norm_mix_g=m_norm_mix_g, w_in=m_w_in, shift_mu=m_shift_mu, w0=m_w0, w_lora_up=m_w_lora_up, a0=m_a0,
                a_lora_up=m_a_lora_up, g_lora_up=m_g_lora_up, k_k=m_k_k, k_a=m_k_a, r_k=m_r_k, lnx_g=m_lnx_g, lnx_b=m_lnx_b,
                w_proj_rwkv=m_w_proj_rwkv, sgu_ln_g=m_sgu_ln_g, sgu_ln_b=m_sgu_ln_b, sgu_w=m_sgu_w, sgu_b=m_sgu_b,
                w_proj_sgu=m_w_proj_sgu, w_out=m_w_out, norm_ffn_g=m_norm_ffn_g, w_ffn_gate=m_w_ffn_gate, w_ffn_up=m_w_ffn_up,
                w_ffn_down=m_w_ffn_down, norm_final_g=m_norm_final_g)
    v_in = dict(norm_mix_g=v_norm_mix_g, w_in=v_w_in, shift_mu=v_shift_mu, w0=v_w0, w_lora_up=v_w_lora_up, a0=v_a0,
                a_lora_up=v_a_lora_up, g_lora_up=v_g_lora_up, k_k=v_k_k, k_a=v_k_a, r_k=v_r_k, lnx_g=v_lnx_g, lnx_b=v_lnx_b,
                w_proj_rwkv=v_w_proj_rwkv, sgu_ln_g=v_sgu_ln_g, sgu_ln_b=v_sgu_ln_b, sgu_w=v_sgu_w, sgu_b=v_sgu_b,
                w_proj_sgu=v_w_proj_sgu, w_out=v_w_out, norm_ffn_g=v_norm_ffn_g, w_ffn_gate=v_w_ffn_gate, w_ffn_up=v_w_ffn_up,
                w_ffn_down=v_w_ffn_down, norm_final_g=v_norm_final_g)
    names = list(weights)
    col_sharded = ("w_in", "w_lora_up", "a_lora_up", "g_lora_up", "w_proj_rwkv", "w_proj_sgu", "w_ffn_gate", "w_ffn_up")
    row_sharded = ("w_out", "w_ffn_down")
    sharded = [n for n in names if n in col_sharded or n in row_sharded]
    small = [n for n in names if n not in sharded]

    xs, tgt = x[0], loss_target[0]
    T, D = xs.shape
    RW = w0.shape[1]
    H = RW // HEAD
    SW = sgu_ln_g.shape[1]
    G = sgu_w.shape[1]
    assert 2 * SW == D, "the projection layout takes the SGU part to be as wide as a gate"
    lay = _rwkv_layout(RW, w_lora_up.shape[1], a_lora_up.shape[1], g_lora_up.shape[1], D)
    _, pw, _, rcp = lay
    icp = rcp + 3 * D
    b_ga, b_gb, b_z = rcp // D, rcp // D + 1, rcp // D + 2

    gather_groups = [["w_in", "w_lora_up", "a_lora_up", "g_lora_up"], ["w_proj_rwkv", "w_proj_sgu", "w_out"],
                     ["w_ffn_gate", "w_ffn_up", "w_ffn_down"]]
    gather, gather_token = _exchange_start([[(weights[n][0].astype(BF16), True) for n in grp] for grp in gather_groups],
                                           "gather_start", rels=SIBLING + SAME_CORE)
    full = {}
    relay_tokens = {}
    joined = lambda g: g.transpose(1, 0, 2).reshape(g.shape[1], -1)

    def relay_weights(gi, after, name):
        arrived = _exchange_wait(gather[gi], after, "gather_wait_ici_" + name, rels=SAME_CORE, local=False)
        gather[gi], relay_tokens[gi] = _relay_start(arrived, "gather_relay_" + name)

    def take_weights(gi, after, name):
        done = _exchange_wait(gather[gi], after, "gather_wait_d2d_" + name, rels=SIBLING)
        for n, g in zip(gather_groups[gi], done["lands"]):
            full[n] = g.reshape(-1, g.shape[2]) if n in row_sharded else g

    n1 = _rms_fwd(xs, norm_mix_g, "rms_mix", deps=[gather_token])
    relay_weights(0, n1, "in")
    take_weights(0, relay_tokens[0], "in")
    W_in = _w_in_to_proj(full["w_in"], lay, D, "w_in_layout")
    lora = [_pad_rows(joined(full[n]), rows) for n, rows in zip(("w_lora_up", "a_lora_up", "g_lora_up"), pw[3:])]
    mu_p = _pad_rwkv_cols(shift_mu, lay)
    rsmall = [w0, a0, k_k, k_a]
    hp = [lnx_g.reshape(H, 1, HEAD), lnx_b.reshape(H, 1, HEAD), r_k.reshape(H, 1, HEAD)]
    ws = sgu_w[0]
    bexp = jnp.repeat(sgu_b[0].T, SGU_GROUP, axis=1)
    gf = norm_final_g.reshape(1, D)

    proj = _matmul(n1, W_in, mode="nn", out_dtype=F32, name="proj_in")
    ga, gb = (proj, D, b_ga), (proj, D, b_gb)
    r_h, lw_h, k2_h, v_h, aa_h, bb_h, g_h = _rwkv_pre(proj, mu_p, rsmall, lora, lay, "rwkv_pre")
    wkv_in = [r_h, lw_h, k2_h, v_h, aa_h, bb_h]
    y_h, states = _wkv_fwd(*wkv_in, "wkv_fwd")
    relay_weights(1, y_h, "proj")
    relay_weights(2, relay_tokens[1], "ffn")
    ya = _head_post(y_h, r_h, k2_h, v_h, g_h, hp, "head_post", deps=[relay_tokens[2]])
    yb = _sgu_fwd(proj, b_z, sgu_ln_g, sgu_ln_b, ws, bexp, "sgu_fwd")
    take_weights(1, ya, "proj")
    pa = _matmul(ya, full["w_proj_rwkv"], mode="nn", out_dtype=F32, name="proj_a")

    def merge_fn(pb_v, pa_v, ga_v, gb_v):
        return pb_v, _sigmoid(ga_v) * pa_v + _sigmoid(gb_v) * pb_v
    pb, merged = _matmul(yb, full["w_proj_sgu"], mode="nn", name="proj_b_merge",
                         epi=(merge_fn, [pa, (proj, b_ga * D), (proj, b_gb * D)], [F32, BF16]))
    h1 = _matmul(merged, full["w_out"], mode="nn", out_dtype=F32, name="out_proj", add=xs)
    n2 = _rms_fwd(h1, norm_ffn_g, "rms_ffn")
    take_weights(2, n2, "ffn")
    gt = _matmul(n2, full["w_ffn_gate"], mode="nn", out_dtype=F32, name="ffn_gate", out_blocks=N_DEV)

    def act_fn(up_v, gt_v):
        return up_v, gt_v * _sigmoid(gt_v) * up_v
    up, act = _matmul(n2, full["w_ffn_up"], mode="nn", name="ffn_up_act", out_blocks=N_DEV, epi=(act_fn, [gt], [F32, BF16]))
    h2 = _matmul(act, full["w_ffn_down"], mode="nn", out_dtype=F32, name="ffn_down", add=h1)

    def final_fn(rv, pv):
        (h_v, t_v), (g_v,) = rv, pv
        r = lax.rsqrt(_mean(h_v * h_v) + RMS_EPS)
        yn = h_v * r
        e = yn * g_v - t_v
        loss = 0.5 * jnp.sum(_mean(e * e))
        dout = e * (1.0 / D)
        dyg = dout * g_v
        dh = r * (dyg - yn * _mean(dyg * yn))
        return [dh, dh], [jnp.full((1, LANE), loss, F32), _colsum(dout * yn)]
    dh2, dh2_bf, loss_part, d_gf = _rowwise(final_fn, [h2, tgt], [gf], [(D, F32), (D, BF16)], [(1, LANE), (1, D)], name="final_loss")

    grads = {}

    def start_scatter(group, name, extra=()):
        blocks = [(grads[n].reshape(N_DEV, -1, grads[n].shape[1]) if n in row_sharded else grads[n], False) for n in group]
        (handle,), token = _exchange_start([blocks + list(extra)], name)
        return handle, token

    def dact_fn(d_v, gt_v, up_v):
        s = _sigmoid(gt_v)
        return d_v * up_v * (s * (1.0 + gt_v * (1.0 - s))), d_v * gt_v * s
    dgt, dup = _matmul(dh2_bf, full["w_ffn_down"], mode="nt", name="d_ffn_act", out_blocks=N_DEV,
                       epi=(dact_fn, [gt, up], [BF16, BF16]))
    grads["w_ffn_down"] = _matmul(act, dh2_bf, mode="tn", out_dtype=BF16, name="dw_ffn_down")
    dn2 = _matmul(dgt, full["w_ffn_gate"], mode="nt", out_dtype=F32, name="dn2_gate")
    dn2 = _matmul(dup, full["w_ffn_up"], mode="nt", out_dtype=F32, name="dn2_up", add=dn2)
    grads["w_ffn_gate"] = _matmul(n2, dgt, mode="tn", out_dtype=BF16, name="dw_ffn_gate", out_blocks=N_DEV)
    grads["w_ffn_up"] = _matmul(n2, dup, mode="tn", out_dtype=BF16, name="dw_ffn_up", out_blocks=N_DEV)
    scatter_groups = [["w_ffn_down", "w_ffn_gate", "w_ffn_up"], ["w_out", "w_proj_rwkv", "w_proj_sgu"],
                      ["w_in", "w_lora_up", "a_lora_up", "g_lora_up"]]
    scatter_ffn, token_ffn = start_scatter(scatter_groups[0], "scatter_start_ffn")
    dh1, dh1_bf, d_g2 = _rms_bwd(dn2, h1, dh2, norm_ffn_g, "rms_ffn_bwd", deps=[token_ffn])
    dmerged = _matmul(dh1_bf, full["w_out"], mode="nt", out_dtype=F32, name="d_merged")
    grads["w_out"] = _matmul(merged, dh1_bf, mode="tn", out_dtype=BF16, name="dw_out")

    def dmerge_fn(rv, pv):
        d_v, ga_v, gb_v, pa_v, pb_v = rv
        sa, sb = _sigmoid(ga_v), _sigmoid(gb_v)
        dgates = jnp.concatenate([d_v * pa_v * sa * (1.0 - sa), d_v * pb_v * sb * (1.0 - sb)], axis=1)
        return [dgates, d_v * sa, d_v * sb], []
    dproj, dpa, dpb = _rowwise(dmerge_fn, [dmerged, ga, gb, pa, pb], [],
                               [(2 * D, BF16, icp, b_ga // 2, None), (D, BF16), (D, BF16)], [], name="d_merge")
    dya = _matmul(dpa, full["w_proj_rwkv"], mode="nt", out_dtype=F32, name="d_ya")
    dyb = _matmul(dpb, full["w_proj_sgu"], mode="nt", out_dtype=F32, name="d_yb")
    grads["w_proj_rwkv"] = _matmul(ya, dpa, mode="tn", out_dtype=BF16, name="dw_proj_a", out_blocks=N_DEV)
    grads["w_proj_sgu"] = _matmul(yb, dpb, mode="tn", out_dtype=BF16, name="dw_proj_b", out_blocks=N_DEV)
    scatter_mid, token_mid = start_scatter(scatter_groups[1], "scatter_start_mid")
    dproj, d_lng, d_lnb, d_ws, d_bs = _sgu_bwd(proj, b_z, dyb, sgu_ln_g, sgu_ln_b, ws, bexp, dproj, "sgu_bwd")

    dy_h, dr1, dk1, dv1, dg_h, d_lnxg, d_lnxb, d_rk = _head_post_bwd(dya, y_h, r_h, k2_h, v_h, g_h, hp, "head_post_bwd",
                                                                     deps=[token_mid])
    dr2, dlw_h, dk2b, dv2, daa, dbb = _wkv_bwd(*wkv_in, states, dy_h, "wkv_bwd")
    dps, d_mu, d_w0, d_a0, d_kk, d_ka, d_wlw, d_wla, d_wlg = _rwkv_pre_bwd(
        proj, mu_p, rsmall, lora, [dr1, dr2, dk1, dk2b, dv1, dv2, dlw_h, daa, dbb, dg_h], lay, "rwkv_pre_bwd")
    dproj = _shift_bwd(dps, mu_p, dproj, "shift_bwd")
    split = lambda g: g.reshape(g.shape[0], N_DEV, -1).transpose(1, 0, 2)
    grads["w_in"] = _dw_in_from_proj(_matmul(n1, dproj, mode="tn", out_dtype=BF16, name="dw_in"), lay, D, w_in.shape[2], "dw_in_layout")
    grads["w_lora_up"] = split(d_wlw[:w_lora_up.shape[1]].astype(BF16))
    grads["a_lora_up"] = split(d_wla[:a_lora_up.shape[1]].astype(BF16))
    grads["g_lora_up"] = split(d_wlg[:g_lora_up.shape[1]].astype(BF16))
    swap, token_swap = _sibling_swap([grads[n] for n in scatter_groups[2]], None, None, "scatter_in_swap_start")
    dn1 = _matmul(dproj, W_in, mode="nt", out_dtype=F32, name="dn1", deps=[token_swap])
    swap = _sibling_swap(None, swap, dn1, "scatter_in_swap_wait")
    core = lax.axis_index("c").astype(jnp.int32).reshape(1)
    chip_sums = [_pair_add(mine, theirs, core, "scatter_in_add_" + n)
                 for n, mine, theirs in zip(scatter_groups[2], swap["srcs"], swap["lands"])]
    (scatter_in,), token_in = _exchange_start([[(s, False) for s in chip_sums]], "scatter_start_in", rels=SAME_CORE, chips=True)
    dx, _, d_g1 = _rms_bwd(dn1, xs, dh1, norm_mix_g, "rms_mix_bwd", deps=[token_in])
    small_grads = dict(norm_mix_g=d_g1, shift_mu=_unpad_rwkv_cols(d_mu, lay), w0=d_w0, a0=d_a0, k_k=d_kk, k_a=d_ka, r_k=d_rk,
                       lnx_g=d_lnxg, lnx_b=d_lnxb, sgu_ln_g=d_lng, sgu_ln_b=d_lnb, sgu_w=d_ws, sgu_b=d_bs[:, :G].T,
                       norm_ffn_g=d_g2, norm_final_g=d_gf)

    (gather_small,), after = _exchange_start([[(_pack([small_grads[n] for n in small]), True)]], "gather_small_start")
    out = {}
    for group, handle, name in zip(scatter_groups, (scatter_ffn, scatter_mid, scatter_in), ("ffn", "mid", "in")):
        parts = _exchange_wait(handle, after, "scatter_wait_" + name, rels=SAME_CORE if handle["chips"] else ALL_PEERS)["lands"]
        for n, part in zip(group, parts):
            shp = weights[n].shape
            res = _adamw(weights[n][0], m_in[n][0], v_in[n][0], part, "adamw_" + n, after=after)
            out[n] = [t.reshape(shp) for t in res]
            after = res[0]
    packed = [_pack([d[n] for n in small]) for d in (weights, m_in, v_in)]
    small_parts = _exchange_wait(gather_small, after, "gather_small_wait")["lands"][0]
    res = _adamw(*packed, small_parts, "adamw_small")
    unpacked = [_unpack(t, [weights[n].shape for n in small]) for t in res]
    for i, n in enumerate(small):
        out[n] = [u[i] for u in unpacked]

    loss = lax.psum(loss_part[0, 0], ("x", "y", "c"))
    return (loss, dx[None], *[out[n][0] for n in names], *[out[n][1] for n in names],
            *[out[n][2] for n in names], *[out[n][3] for n in names])
```

```python
import jax
import jax.numpy as jnp
from jax import lax
from jax.experimental import pallas as pl
from jax.experimental.pallas import tpu as pltpu

F32 = jnp.float32
BF16 = jnp.bfloat16

N_DEV = 8
LANE = 128
SUBLANE = 8
HEAD = 64
SGU_CHUNK = 128
SGU_GROUP = 128
WKV_CHUNK = 64
RMS_EPS = 1e-6
LN_EPS = 1e-5
LNX_EPS = 64e-5
ADAM_LR, ADAM_B1, ADAM_B2, ADAM_EPS, ADAM_WD, ADAM_STEP = 0.001, 0.9, 0.999, 1e-08, 0.01, 10
VMEM_LIMIT_BYTES = 48 * 1024 * 1024
_SQRT_HALF = 0.7071067811865476
_INV_SQRT_2PI = 0.3989422804014327


def _pick(n, cands):
    for c in cands:
        if n % c == 0:
            return c
    return n


def _ceil_to(n, m):
    return -(-n // m) * m


def _params():
    return pltpu.CompilerParams(vmem_limit_bytes=VMEM_LIMIT_BYTES)


def _tile(n, cap):
    best = 0
    for d in range(LANE, min(n, cap) + 1, LANE):
        if n % d == 0:
            best = d
    return best or n


def _matmul_tiles(M, N, K, a_bytes, b_bytes, o_bytes, has_add, forced):
    tm = forced.get("m") or _tile(M, 1024)
    tn = forced.get("n") or _tile(N, 1024)
    tk = forced.get("k") or _tile(K, 2048)

    def vmem(tm, tn, tk):
        acc = tm * tn * 4 if tk < K else 0
        return 2 * (tm * tk * a_bytes + tk * tn * b_bytes + tm * tn * (o_bytes + (4 if has_add else 0))) + acc

    while vmem(tm, tn, tk) > (VMEM_LIMIT_BYTES * 3) // 4:
        if "k" not in forced and tk > 512 and _tile(K, tk // 2) < tk:
            tk = _tile(K, tk // 2)
        elif "m" not in forced and _tile(M, tm // 2) < tm:
            tm = _tile(M, tm // 2)
        else:
            break
    return tm, tn, tk


def _matmul(a, b, *, mode, out_dtype=F32, name, add=None, deps=(), out_blocks=0, epi=None):
    def view(x):
        return (x.shape[1], x.shape[0] * x.shape[2], x.shape[2]) if x.ndim == 3 else (x.shape[0], x.shape[1], 0)

    (ar, ac, aw), (br, bc, bw) = view(a), view(b)
    a_col, b_col = {"nn": ("k", "n"), "nt": ("k", "k"), "tn": ("m", "n")}[mode]
    if mode == "nn":
        M, K, K2, N = ar, ac, br, bc
    elif mode == "nt":
        M, K, N, K2 = ar, ac, br, bc
    else:
        K, M, K2, N = ar, ac, br, bc
    assert K == K2, (a.shape, b.shape, mode)
    forced = {}
    for dim, w in ((a_col, aw), (b_col, bw), ("n", N // out_blocks if out_blocks else 0)):
        if w:
            assert forced.get(dim, w) == w
            forced[dim] = w
    has_add = add is not None
    tile_bytes = (sum(jnp.dtype(d).itemsize for d in epi[2]) + sum((e[0] if isinstance(e, tuple) else e).dtype.itemsize for e in epi[1])
                  if epi is not None else jnp.dtype(out_dtype).itemsize)
    tm, tn, tk = _matmul_tiles(M, N, K, a.dtype.itemsize, b.dtype.itemsize, tile_bytes, has_add, forced)
    nk = K // tk
    dn = {"nn": (((1,), (0,)), ((), ())), "nt": (((1,), (1,)), ((), ())), "tn": (((0,), (0,)), ((), ()))}[mode]
    pick = {"m": lambda i, j, k: i, "n": lambda i, j, k: j, "k": lambda i, j, k: k}
    size = {"m": tm, "n": tn, "k": tk}

    def spec(blocked, row_dim, col_dim):
        rf, cf = pick[row_dim], pick[col_dim]
        if blocked:
            return pl.BlockSpec((None, size[row_dim], size[col_dim]), lambda i, j, k: (cf(i, j, k), rf(i, j, k), 0))
        return pl.BlockSpec((size[row_dim], size[col_dim]), lambda i, j, k: (rf(i, j, k), cf(i, j, k)))

    a_spec = spec(aw, "k" if mode == "tn" else "m", a_col)
    b_spec = spec(bw, "n" if mode == "nt" else "k", b_col)
    o_spec = spec(out_blocks, "m", "n")
    epi_fn, epi_ins, epi_dtypes = epi if epi is not None else (None, [], [out_dtype])
    epi_ins = [e if isinstance(e, tuple) else (e, None) for e in epi_ins]
    n_epi = len(epi_ins)
    n_in = 2 + has_add + n_epi + len(deps)
    n_out = len(epi_dtypes)

    def body(*refs):
        a_ref, b_ref = refs[0], refs[1]
        add_ref = refs[2] if has_add else None
        epi_refs = refs[2 + has_add:2 + has_add + n_epi]
        o_refs = refs[n_in:n_in + n_out]
        part = lax.dot_general(a_ref[...].astype(BF16), b_ref[...].astype(BF16), dn, preferred_element_type=F32)

        def finish(res):
            outs = epi_fn(res, *[e[...] for e in epi_refs]) if epi_fn is not None else (res,)
            for o_ref, val in zip(o_refs, outs):
                o_ref[...] = val.astype(o_ref.dtype)

        if nk == 1:
            finish(part + add_ref[...] if has_add else part)
            return
        acc_ref = refs[-1]
        kk = pl.program_id(2)

        @pl.when(kk == 0)
        def _():
            acc_ref[...] = part + add_ref[...] if has_add else part

        @pl.when(kk > 0)
        def _():
            acc_ref[...] += part

        @pl.when(kk == nk - 1)
        def _():
            finish(acc_ref[...])

    def epi_spec(arr, off):
        if off is None:
            return o_spec
        assert off % tn == 0
        return pl.BlockSpec((tm, tn), lambda i, j, k: (i, j + off // tn))

    ins = [a, b] + ([add] if has_add else []) + [arr for arr, _ in epi_ins] + list(deps)
    in_specs = ([a_spec, b_spec] + ([o_spec] if has_add else []) + [epi_spec(arr, off) for arr, off in epi_ins]
                + [pl.BlockSpec(d.shape, lambda i, j, k, nd=d.ndim: (0,) * nd) for d in deps])
    o_shape = (out_blocks, M, tn) if out_blocks else (M, N)
    res = pl.pallas_call(
        body, name=name, grid=(M // tm, N // tn, nk), in_specs=in_specs, out_specs=[o_spec] * n_out,
        out_shape=[jax.ShapeDtypeStruct(o_shape, dt) for dt in epi_dtypes],
        scratch_shapes=[pltpu.VMEM((tm, tn), F32)] if nk > 1 else [],
        compiler_params=_params())(*ins)
    return res[0] if epi is None else list(res)


def _rowwise(fn, rows, pars, row_outs, acc_outs, *, name, tm=256, deps=()):
    rows = [r if isinstance(r, tuple) else (r, r.shape[1], 0) for r in rows]
    row_outs = [o if len(o) == 5 else (o[0], o[1], o[0], 0, None) for o in row_outs]
    aliased = [(k, o[4]) for k, o in enumerate(row_outs) if o[4] is not None]
    R = rows[0][0].shape[0]
    if max(w for _, w, _ in rows) > 4096:
        tm = tm // 2
    tm = min(tm, R)
    assert R % tm == 0
    nr, npar = len(rows), len(pars)
    nro = len(row_outs)
    n_in = nr + npar + len(deps) + len(aliased)

    def body(*refs):
        rv = [r[...] for r in refs[:nr]]
        pv = [p[...] for p in refs[nr:nr + npar]]
        outs = refs[n_in:]
        ro, ao = fn(rv, pv)
        first = pl.program_id(0) == 0
        for o_ref, val in zip(outs[:nro], ro):
            o_ref[...] = val.astype(o_ref.dtype)

        @pl.when(first)
        def _():
            for o_ref, val in zip(outs[nro:], ao):
                o_ref[...] = val

        @pl.when(jnp.logical_not(first))
        def _():
            for o_ref, val in zip(outs[nro:], ao):
                o_ref[...] += val

    in_specs = ([pl.BlockSpec((tm, w), lambda i, cb=cb: (i, cb)) for _, w, cb in rows]
                + [pl.BlockSpec(p.shape, lambda i, nd=p.ndim: (0,) * nd) for p in list(pars) + list(deps)]
                + [pl.BlockSpec(memory_space=pl.ANY)] * len(aliased))
    out_shape = ([jax.ShapeDtypeStruct((R, full), dt) for _, dt, full, _, _ in row_outs]
                 + [jax.ShapeDtypeStruct(s, F32) for s in acc_outs])
    out_specs = ([pl.BlockSpec((tm, f), lambda i, cb=cb: (i, cb)) for f, _, _, cb, _ in row_outs]
                 + [pl.BlockSpec(s, lambda i, nd=len(s): (0,) * nd) for s in acc_outs])
    res = pl.pallas_call(body, name=name, grid=(R // tm,), in_specs=in_specs, out_specs=out_specs, out_shape=out_shape,
                         input_output_aliases={n_in - len(aliased) + q: k for q, (k, _) in enumerate(aliased)},
                         compiler_params=_params())(*[r for r, _, _ in rows], *pars, *deps, *[buf for _, buf in aliased])
    return list(res)


def _bdot(a, b, mode="nn"):
    dn = {"nn": (((1,), (0,)), ((), ())), "nt": (((1,), (1,)), ((), ())), "tn": (((0,), (0,)), ((), ()))}[mode]
    return lax.dot_general(a.astype(BF16), b.astype(BF16), dn, preferred_element_type=F32)


def _sigmoid(x):
    return jax.nn.sigmoid(x)


def _softplus(x):
    return jnp.maximum(x, 0.0) + jnp.log1p(jnp.exp(-jnp.abs(x)))


def _gelu(z):
    return 0.5 * z * (1.0 + lax.erf(z * _SQRT_HALF))


def _gelu_grad(z):
    return 0.5 * (1.0 + lax.erf(z * _SQRT_HALF)) + z * jnp.exp(-0.5 * z * z) * _INV_SQRT_2PI


def _mean(x):
    return jnp.mean(x, axis=-1, keepdims=True)


def _colsum(x):
    return jnp.sum(x, axis=0, keepdims=True)


def _rms_fwd(x, g, name, deps=()):
    def fn(rv, pv):
        (xv,), (gv,) = rv, pv
        r = lax.rsqrt(_mean(xv * xv) + RMS_EPS)
        return [xv * r * gv], []
    return _rowwise(fn, [x], [g], [(x.shape[1], BF16)], [], name=name, deps=deps)[0]


def _rms_bwd(dn, x, dres, g, name, deps=()):
    def fn(rv, pv):
        (dnv, xv, drv), (gv,) = rv, pv
        r = lax.rsqrt(_mean(xv * xv) + RMS_EPS)
        yn = xv * r
        dyg = dnv * gv
        dx = drv + r * (dyg - yn * _mean(dyg * yn))
        return [dx, dx], [_colsum(dnv * yn)]
    D = x.shape[1]
    return _rowwise(fn, [dn, x, dres], [g], [(D, F32), (D, BF16)], [(1, D)], name=name, deps=deps)


def _rwkv_layout(RW, Lw, La, Lg, D):
    widths = [RW, RW, RW, Lw, La, Lg]
    pw = [_ceil_to(w, LANE) for w in widths]
    pw[5] += _ceil_to(sum(pw), 2 * D) - sum(pw)
    offs = [sum(pw[:i]) for i in range(6)]
    return widths, pw, offs, sum(pw)


def _pad_rwkv_cols(a, lay):
    widths, pw, _, _ = lay
    pieces, src = [], 0
    for w, p in zip(widths, pw):
        pieces.append(a[:, src:src + w])
        if p > w:
            pieces.append(jnp.zeros((a.shape[0], p - w), a.dtype))
        src += w
    return jnp.concatenate(pieces, axis=1)


def _unpad_rwkv_cols(a, lay):
    widths, _, offs, _ = lay
    return jnp.concatenate([a[:, o:o + w] for o, w in zip(offs, widths)], axis=1)


def _proj_pieces(lay, D, cs):
    widths, _, offs, rcp = lay
    rc = sum(widths)
    segs = [(sum(widths[:j]), widths[j], offs[j]) for j in range(6)] + [(rc, D, rcp + 2 * D), (rc + D, D, rcp), (rc + 2 * D, D, rcp + D)]
    pieces = []
    for start, width, dst in segs:
        n = start
        while n < start + width:
            d, off = divmod(n, cs)
            take = min(cs - off, start + width - n)
            pieces.append((d, off, dst + n - start, take))
            n += take
    return pieces


def _w_in_to_proj(g, lay, D, name):
    nb, rows, cs = g.shape
    icp = lay[3] + 3 * D
    pieces = _proj_pieces(lay, D, cs)
    tm = _pick(rows, (256, 128, 64, 32, 16))

    def body(i_ref, o_ref):
        o_ref[...] = jnp.zeros_like(o_ref)
        for d, src, dst, w in pieces:
            o_ref[:, dst:dst + w] = i_ref[d, :, src:src + w]

    return pl.pallas_call(
        body, name=name, grid=(rows // tm,), in_specs=[pl.BlockSpec((nb, tm, cs), lambda i: (0, i, 0))],
        out_specs=pl.BlockSpec((tm, icp), lambda i: (i, 0)), out_shape=jax.ShapeDtypeStruct((rows, icp), g.dtype),
        compiler_params=_params())(g)


def _dw_in_from_proj(a, lay, D, cs, name):
    rows, icp = a.shape
    pieces = _proj_pieces(lay, D, cs)
    tm = _pick(rows, (256, 128, 64, 32, 16))

    def body(i_ref, o_ref):
        for d, src, dst, w in pieces:
            o_ref[d, :, src:src + w] = i_ref[:, dst:dst + w]

    return pl.pallas_call(
        body, name=name, grid=(rows // tm,), in_specs=[pl.BlockSpec((tm, icp), lambda i: (i, 0))],
        out_specs=pl.BlockSpec((N_DEV, tm, cs), lambda i: (0, i, 0)), out_shape=jax.ShapeDtypeStruct((N_DEV, rows, cs), a.dtype),
        compiler_params=_params())(a)


def _pad_rows(a, rows):
    return a if a.shape[0] == rows else jnp.concatenate([a, jnp.zeros((rows - a.shape[0], a.shape[1]), a.dtype)], axis=0)


def _token_shift(p, halo, mu, i):
    tm = p.shape[0]
    hid = lax.broadcasted_iota(jnp.int32, (SUBLANE, 1), 0)
    before = jnp.sum(jnp.where(hid == SUBLANE - 1, halo, 0.0), axis=0, keepdims=True)
    before = jnp.where(i == 0, 0.0, before)
    rid = lax.broadcasted_iota(jnp.int32, (tm, 1), 0)
    prev = jnp.where(rid == 0, before, pltpu.roll(p, 1, 0))
    d = prev - p
    return p + d * mu, d


def _rwkv_math(ps, w0, a0, k_k, k_a, wlw, wla, wlg, lay):
    _, pw, offs, _ = lay
    r, k, v, xw, xa, xg = (ps[:, offs[j]:offs[j] + pw[j]] for j in range(6))
    tw = jnp.tanh(xw)
    ww = w0 + _bdot(tw, wlw)
    lw = -jnp.exp(-_softplus(-ww) - 0.5)
    a = _sigmoid(a0 + _bdot(xa, wla))
    sg = _sigmoid(xg)
    g = _bdot(sg, wlg)
    return dict(r=r, k=k, v=v, xa=xa, tw=tw, ww=ww, lw=lw, a=a, sg=sg, g=g, kkp=k * k_k, k2=k * (1.0 + (a - 1.0) * k_a))


def _halo_specs(T, tm, width, after):
    hb = tm // SUBLANE
    last = T // SUBLANE - 1
    if after:
        return pl.BlockSpec((SUBLANE, width), lambda i: (jnp.minimum((i + 1) * hb, last), 0))
    return pl.BlockSpec((SUBLANE, width), lambda i: (jnp.maximum(i * hb - 1, 0), 0))


def _rowsum(x):
    return jnp.sum(x, axis=-1, keepdims=True)


def _kk_math(kkp):
    nrm = jnp.sqrt(_rowsum(kkp * kkp))
    inv = 1.0 / jnp.maximum(nrm, 1e-12)
    return nrm, inv, kkp * inv


def _rwkv_pre(p, mu, small, lora, lay, name):
    T, rcp = p.shape[0], lay[3]
    H = lay[0][0] // HEAD
    tm = min(128, T)

    def body(p_ref, ph_ref, mu_ref, w0_ref, a0_ref, kk_ref, ka_ref, wlw_ref, wla_ref, wlg_ref, r_o, lw_o, k2_o, v_o, aa_o, bb_o, g_o):
        ps, _ = _token_shift(p_ref[...], ph_ref[...], mu_ref[...], pl.program_id(0))
        q = _rwkv_math(ps, w0_ref[...], a0_ref[...], kk_ref[...], ka_ref[...], wlw_ref[...], wla_ref[...], wlg_ref[...], lay)
        for h in range(H):
            sl = slice(h * HEAD, (h + 1) * HEAD)
            for o_ref, key in ((r_o, "r"), (lw_o, "lw"), (k2_o, "k2"), (v_o, "v"), (g_o, "g")):
                o_ref[h] = q[key][:, sl]
            _, _, kk = _kk_math(q["kkp"][:, sl])
            aa_o[h] = -kk
            bb_o[h] = kk * q["a"][:, sl]

    whole = lambda arr: pl.BlockSpec(arr.shape, lambda i: (0, 0))
    return pl.pallas_call(
        body, name=name, grid=(T // tm,),
        in_specs=([pl.BlockSpec((tm, rcp), lambda i: (i, 0)), _halo_specs(T, tm, rcp, False), whole(mu)]
                  + [whole(s) for s in small] + [whole(w) for w in lora]),
        out_specs=[pl.BlockSpec((H, tm, HEAD), lambda i: (0, i, 0))] * 7, out_shape=[jax.ShapeDtypeStruct((H, T, HEAD), F32)] * 7,
        compiler_params=_params())(p, p, mu, *small, *lora)


def _rwkv_pre_bwd(p, mu, small, lora, hgrads, lay, name):
    T, rcp = p.shape[0], lay[3]
    widths, pw, offs, _ = lay
    RW = widths[0]
    H = RW // HEAD
    tm = min(128, T)

    def body(p_ref, ph_ref, mu_ref, w0_ref, a0_ref, kk_ref, ka_ref, wlw_ref, wla_ref, wlg_ref,
             dr1, dr2, dk1, dk2b, dv1, dv2, dlw_h, daa, dbb, dg_h,
             dps_ref, dmu_ref, dw0_ref, da0_ref, dkk_ref, dka_ref, dwlw_ref, dwla_ref, dwlg_ref,
             s_dr, s_dk2, s_dv, s_dlw, s_dkkp, s_da, s_dg):
        i = pl.program_id(0)
        ps, dprev = _token_shift(p_ref[...], ph_ref[...], mu_ref[...], i)
        k_k, k_a = kk_ref[...], ka_ref[...]
        q = _rwkv_math(ps, w0_ref[...], a0_ref[...], k_k, k_a, wlw_ref[...], wla_ref[...], wlg_ref[...], lay)
        k, a, lw, ww, tw, sg = q["k"], q["a"], q["lw"], q["ww"], q["tw"], q["sg"]
        for h in range(H):
            sl = slice(h * HEAD, (h + 1) * HEAD)
            s_dr[:, sl] = dr1[h] + dr2[h]
            s_dk2[:, sl] = dk1[h] + dk2b[h]
            s_dv[:, sl] = dv1[h] + dv2[h]
            s_dlw[:, sl] = dlw_h[h]
            s_dg[:, sl] = dg_h[h]
            nrm, inv, kk = _kk_math(q["kkp"][:, sl])
            dbb_h = dbb[h]
            dkk = dbb_h * a[:, sl] - daa[h]
            s_dkkp[:, sl] = jnp.where(nrm > 1e-12, inv * (dkk - kk * _rowsum(dkk * kk)), dkk * inv)
            s_da[:, sl] = dbb_h * kk
        dk2, dkkp, dg = s_dk2[...], s_dkkp[...], s_dg[...]
        dk = dk2 * (1.0 + (a - 1.0) * k_a) + dkkp * k_k
        da = s_da[...] + dk2 * k * k_a
        dpa = da * a * (1.0 - a)
        dww = s_dlw[...] * lw * _sigmoid(-ww)
        dxa = _bdot(dpa, wla_ref[...], "nt")
        dxw = _bdot(dww, wlw_ref[...], "nt") * (1.0 - tw * tw)
        dxg = _bdot(dg, wlg_ref[...], "nt") * sg * (1.0 - sg)
        segs = (s_dr[...], dk, s_dv[...], dxw, dxa, dxg)
        sums = [dmu_ref, dw0_ref, da0_ref, dkk_ref, dka_ref, dwlw_ref, dwla_ref, dwlg_ref]

        @pl.when(i == 0)
        def _():
            for s in sums:
                s[...] = jnp.zeros_like(s)

        for j, seg in enumerate(segs):
            sl = slice(offs[j], offs[j] + pw[j])
            dps_ref[:, sl] = seg
            dmu_ref[:, sl] += _colsum(seg * dprev[:, sl])
        dw0_ref[...] += _colsum(dww)
        da0_ref[...] += _colsum(dpa)
        dkk_ref[...] += _colsum(dkkp * k)
        dka_ref[...] += _colsum(dk2 * k * (a - 1.0))
        dwlw_ref[...] += _bdot(tw, dww, "tn")
        dwla_ref[...] += _bdot(q["xa"], dpa, "tn")
        dwlg_ref[...] += _bdot(sg, dg, "tn")

    whole = lambda arr: pl.BlockSpec(arr.shape, lambda i: (0, 0))
    row = lambda w: pl.BlockSpec((tm, w), lambda i: (i, 0))
    acc_shapes = [(1, rcp), (1, RW), (1, RW), (1, RW), (1, RW)] + [w.shape for w in lora]
    return pl.pallas_call(
        body, name=name, grid=(T // tm,),
        in_specs=([row(rcp), _halo_specs(T, tm, rcp, False), whole(mu)] + [whole(s) for s in small] + [whole(w) for w in lora]
                  + [pl.BlockSpec((H, tm, HEAD), lambda i: (0, i, 0))] * 10),
        out_specs=[row(rcp)] + [pl.BlockSpec(s, lambda i: (0, 0)) for s in acc_shapes],
        out_shape=[jax.ShapeDtypeStruct((T, rcp), F32)] + [jax.ShapeDtypeStruct(s, F32) for s in acc_shapes],
        scratch_shapes=[pltpu.VMEM((tm, RW), F32)] * 7, compiler_params=_params())(p, p, mu, *small, *lora, *hgrads)


def _shift_bwd(dps, mu, dproj, name):
    T, rcp = dps.shape
    tm = min(256, T)
    nt = T // tm

    def body(d_ref, dh_ref, mu_ref, buf_ref, o_ref):
        i = pl.program_id(0)
        d = d_ref[...]
        hid = lax.broadcasted_iota(jnp.int32, (SUBLANE, 1), 0)
        after = jnp.sum(jnp.where(hid == 0, dh_ref[...], 0.0), axis=0, keepdims=True)
        after = jnp.where(i == nt - 1, 0.0, after)
        rid = lax.broadcasted_iota(jnp.int32, (tm, 1), 0)
        nxt = jnp.where(rid == tm - 1, after, pltpu.roll(d, tm - 1, 0))
        mu_v = mu_ref[...]
        o_ref[...] = (d * (1.0 - mu_v) + nxt * mu_v).astype(BF16)

    row = pl.BlockSpec((tm, rcp), lambda i: (i, 0))
    return pl.pallas_call(
        body, name=name, grid=(nt,),
        in_specs=[row, _halo_specs(T, tm, rcp, True), pl.BlockSpec(mu.shape, lambda i: (0, 0)), pl.BlockSpec(memory_space=pl.ANY)],
        out_specs=row, out_shape=jax.ShapeDtypeStruct(dproj.shape, BF16), input_output_aliases={3: 0},
        compiler_params=_params())(dps, dps, mu, dproj)


def _head_post_math(y, r, k2, v, lg, lb, rk):
    yc = y - _mean(y)
    rstd = lax.rsqrt(_mean(yc * yc) + LNX_EPS)
    yn = yc * rstd
    s = _rowsum(r * k2 * rk)
    return yn, rstd, yn * lg + lb + s * v, s


def _head_post(y, r, k2, v, g, hp, name, deps=()):
    H, T, _ = y.shape
    tm = min(128, T)

    def body(y_ref, r_ref, k_ref, v_ref, g_ref, lg_ref, lb_ref, rk_ref, *rest):
        o_ref = rest[-1]
        _, _, t, _ = _head_post_math(y_ref[...], r_ref[...], k_ref[...], v_ref[...], lg_ref[...], lb_ref[...], rk_ref[...])
        out = (t * g_ref[...]).astype(BF16)
        for h in range(H):
            o_ref[:, h * HEAD:(h + 1) * HEAD] = out[h]

    blk = pl.BlockSpec((H, tm, HEAD), lambda i: (0, i, 0))
    par = pl.BlockSpec((H, 1, HEAD), lambda i: (0, 0, 0))
    return pl.pallas_call(
        body, name=name, grid=(T // tm,),
        in_specs=[blk] * 5 + [par] * 3 + [pl.BlockSpec(d.shape, lambda i, nd=d.ndim: (0,) * nd) for d in deps],
        out_specs=pl.BlockSpec((tm, H * HEAD), lambda i: (i, 0)),
        out_shape=jax.ShapeDtypeStruct((T, H * HEAD), BF16), compiler_params=_params())(y, r, k2, v, g, *hp, *deps)


def _head_post_bwd(dya, y, r, k2, v, g, hp, name, deps=()):
    H, T, _ = y.shape
    tm = min(128, T)
    hsum = lambda t: jnp.sum(t, axis=1, keepdims=True)

    def body(d_ref, y_ref, r_ref, k_ref, v_ref, g_ref, lg_ref, lb_ref, rk_ref, *rest):
        outs, d_s = rest[len(deps):len(deps) + 8], rest[-1]
        for h in range(H):
            d_s[h] = d_ref[:, h * HEAD:(h + 1) * HEAD]
        d_v, r_v, k_v, v_v, lg, rk = d_s[...], r_ref[...], k_ref[...], v_ref[...], lg_ref[...], rk_ref[...]
        yn, rstd, t, s = _head_post_math(y_ref[...], r_v, k_v, v_v, lg, lb_ref[...], rk)
        dyo = d_v * g_ref[...]
        dyn = dyo * lg
        ds = _rowsum(dyo * v_v)
        vals = (rstd * (dyn - _mean(dyn) - yn * _mean(dyn * yn)), ds * k_v * rk, ds * r_v * rk, dyo * s, d_v * t)
        for o_ref, val in zip(outs[:5], vals):
            o_ref[...] = val
        sums = (hsum(dyo * yn), hsum(dyo), hsum(ds * r_v * k_v))
        first = pl.program_id(0) == 0

        @pl.when(first)
        def _():
            for o_ref, val in zip(outs[5:], sums):
                o_ref[...] = val

        @pl.when(jnp.logical_not(first))
        def _():
            for o_ref, val in zip(outs[5:], sums):
                o_ref[...] += val

    blk = pl.BlockSpec((H, tm, HEAD), lambda i: (0, i, 0))
    par = pl.BlockSpec((H, 1, HEAD), lambda i: (0, 0, 0))
    return pl.pallas_call(
        body, name=name, grid=(T // tm,),
        in_specs=([pl.BlockSpec((tm, H * HEAD), lambda i: (i, 0))] + [blk] * 5 + [par] * 3
                  + [pl.BlockSpec(d.shape, lambda i, nd=d.ndim: (0,) * nd) for d in deps]),
        out_specs=[blk] * 5 + [par] * 3,
        out_shape=[jax.ShapeDtypeStruct((H, T, HEAD), F32)] * 5 + [jax.ShapeDtypeStruct((H, 1, HEAD), F32)] * 3,
        scratch_shapes=[pltpu.VMEM((H, tm, HEAD), F32)], compiler_params=_params())(dya, y, r, k2, v, g, *hp, *deps)


def _bmm(x, y, mode):
    dn = {"nn": (((2,), (1,)), ((0,), (0,))), "nt": (((2,), (2,)), ((0,), (0,))), "tn": (((1,), (1,)), ((0,), (0,)))}[mode]
    (xh, xl), (yh, yl) = _split(x), _split(y)
    dot = lambda p, q: lax.dot_general(p, q, dn, preferred_element_type=F32)
    out = dot(xh, yh)
    if yl is not None:
        out = out + dot(xh, yl)
    if xl is not None:
        out = out + dot(xl, yh)
    return out


def _split(x):
    if isinstance(x, tuple):
        return x
    hi = x.astype(BF16)
    return hi, (x - hi.astype(F32)).astype(BF16)


def _exact(x):
    return x.astype(BF16), None


def _round(x):
    return x if isinstance(x, tuple) else (x.astype(BF16), None)


def _rows(*xs):
    if isinstance(xs[0], tuple):
        return tuple(None if any(p is None for p in parts) else jnp.concatenate(parts, axis=1) for parts in zip(*xs))
    return jnp.concatenate(xs, axis=1)


def _wkv_chunk(r, lw, k, v, a, b):
    hb, C, _ = r.shape
    ti = lax.broadcasted_iota(jnp.int32, (C, C), 0)
    si = lax.broadcasted_iota(jnp.int32, (C, C), 1)
    linc, lstr, eye = (ti >= si).astype(F32), (ti > si).astype(F32), (ti == si).astype(F32)
    qmask = jnp.concatenate([jnp.concatenate([lstr, lstr], axis=1), jnp.concatenate([linc, linc], axis=1)], axis=0)
    lincb = _exact(jnp.broadcast_to(linc, (hb, C, C)))
    both = _exact(jnp.broadcast_to(jnp.concatenate([linc, lstr], axis=0), (hb, 2 * C, C)))
    ones = _exact(jnp.ones_like(v))
    lws = _split(lw)
    ci = _bmm(lincb, lws, "nn")
    cC = jnp.sum(lw, axis=1, keepdims=True)
    gi, ge, gn, gr = jnp.exp(ci), jnp.exp(ci - lw), jnp.exp(-ci), jnp.exp(cC - ci)
    q = dict(At=a * ge, Rt=r * gi, Bt=b * gn, Kt=k * gn, Bh=b * gr, Kh=k * gr)
    s = dict(AR=_round(_rows(q["At"], q["Rt"])), BK=_round(_rows(q["Bt"], q["Kt"])), BKh=_round(_rows(q["Bh"], q["Kh"])), v=_round(v))
    quad = _bmm(s["AR"], s["BK"], "nt") * qmask
    s["top"], s["bot"] = _round(quad[:, :C]), _round(quad[:, C:])
    A_ab = quad[:, :C, :C]
    Tm = eye + A_ab
    Pw = _round(A_ab)
    n = 1
    while 2 * n < C:
        Pw = _round(_bmm(Pw, Pw, "nn"))
        Tm = Tm + _bmm(_round(Tm), Pw, "nn")
        n *= 2
    s["Tm"] = _round(Tm)
    gC = jnp.exp(_bmm(lws, ones, "tn"))
    q.update(gi=gi, ge=ge, gn=gn, gr=gr, qmask=qmask, both=both, gC=gC, ones=ones, s=s)
    return q


def _wkv_u(s, H0s, C):
    arh = _bmm(s["AR"], H0s, "nn")
    zv = _rows(tuple(None if p is None else jnp.zeros_like(p) for p in s["v"]), s["v"])
    U = _bmm(s["Tm"], _round(arh[:, :C] + _bmm(s["top"], zv, "nn")), "nn")
    return arh, _rows(_round(U), s["v"])


def _wkv_fwd(r, lw, k, v, a, b, name):
    H, T, N = r.shape
    C = min(WKV_CHUNK, T)
    nc = T // C
    hb = _pick(H, (16, 8, 4, 2))

    def body(r_ref, lw_ref, k_ref, v_ref, a_ref, b_ref, y_ref, st_ref, h_ref):
        @pl.when(pl.program_id(1) == 0)
        def _():
            h_ref[...] = jnp.zeros_like(h_ref)

        H0 = h_ref[...]
        st_ref[0] = H0
        q = _wkv_chunk(r_ref[...], lw_ref[...], k_ref[...], v_ref[...], a_ref[...], b_ref[...])
        s = q["s"]
        arh, UV = _wkv_u(s, _round(H0), C)
        y_ref[...] = arh[:, C:] + _bmm(s["bot"], UV, "nn")
        h_ref[...] = q["gC"] * H0 + _bmm(s["BKh"], UV, "tn")

    blk = pl.BlockSpec((hb, C, N), lambda h, c: (h, c, 0))
    return pl.pallas_call(
        body, name=name, grid=(H // hb, nc), in_specs=[blk] * 6,
        out_specs=[blk, pl.BlockSpec((1, hb, N, N), lambda h, c: (c, h, 0, 0))],
        out_shape=[jax.ShapeDtypeStruct((H, T, N), F32), jax.ShapeDtypeStruct((nc, H, N, N), F32)],
        scratch_shapes=[pltpu.VMEM((hb, N, N), F32)], compiler_params=_params())(r, lw, k, v, a, b)


def _wkv_bwd(r, lw, k, v, a, b, states, dy, name):
    H, T, N = r.shape
    C = min(WKV_CHUNK, T)
    nc = T // C
    hb = _pick(H, (16, 8, 4, 2))

    def body(r_ref, lw_ref, k_ref, v_ref, a_ref, b_ref, st_ref, dy_ref, dr_ref, dlw_ref, dk_ref, dv_ref, da_ref, db_ref, dh_ref):
        @pl.when(pl.program_id(1) == 0)
        def _():
            dh_ref[...] = jnp.zeros_like(dh_ref)

        dHC = dh_ref[...]
        H0 = st_ref[0]
        q = _wkv_chunk(r_ref[...], lw_ref[...], k_ref[...], v_ref[...], a_ref[...], b_ref[...])
        s, gC = q["s"], q["gC"]
        H0s, dHs, dY = _round(H0), _round(dHC), _round(dy_ref[...])
        _, UV = _wkv_u(s, H0s, C)
        bot_dy = _bmm(s["bot"], dY, "tn")
        bkh_dh = _bmm(s["BKh"], dHs, "nn")
        dP = _round(_bmm(s["Tm"], _round(bot_dy[:, :C] + bkh_dh[:, :C]), "tn"))
        dv_ref[...] = bot_dy[:, C:] + bkh_dh[:, C:] + _bmm(s["top"], dP, "tn")[:, C:]
        dPY = _rows(dP, dY)
        dh_ref[...] = gC * dHC + _bmm(s["AR"], dPY, "tn")
        dquad = _round(_bmm(dPY, UV, "nt") * q["qmask"])
        dAR = _bmm(dPY, H0s, "nt") + _bmm(dquad, s["BK"], "nn")
        dBK = _bmm(dquad, s["AR"], "tn")
        dBKh = _bmm(UV, dHs, "nt")
        dAt, dRt, dBt, dKt, dBh, dKh = dAR[:, :C], dAR[:, C:], dBK[:, :C], dBK[:, C:], dBKh[:, :C], dBKh[:, C:]
        dr_ref[...] = dRt * q["gi"]
        da_ref[...] = dAt * q["ge"]
        db_ref[...] = dBt * q["gn"] + dBh * q["gr"]
        dk_ref[...] = dKt * q["gn"] + dKh * q["gr"]
        tail = dBh * q["Bh"] + dKh * q["Kh"]
        dci = dRt * q["Rt"] - dBt * q["Bt"] - dKt * q["Kt"] - tail
        dcC = jnp.sum(tail, axis=1, keepdims=True) + _bmm(q["ones"], H0 * dHC * gC, "nt")
        dlw_ref[...] = _bmm(q["both"], _rows(dci, dAt * q["At"]), "tn") + dcC

    blk = pl.BlockSpec((hb, C, N), lambda h, c: (h, nc - 1 - c, 0))
    st = pl.BlockSpec((1, hb, N, N), lambda h, c: (nc - 1 - c, h, 0, 0))
    return pl.pallas_call(
        body, name=name, grid=(H // hb, nc), in_specs=[blk] * 6 + [st, blk], out_specs=[blk] * 6,
        out_shape=[jax.ShapeDtypeStruct((H, T, N), F32)] * 6,
        scratch_shapes=[pltpu.VMEM((hb, N, N), F32)], compiler_params=_params())(r, lw, k, v, a, b, states, dy)


def _sgu_ln(z, SW, lng, lnb):
    ge = _gelu(z)
    u, vv = ge[:, :SW], ge[:, SW:]
    xc = vv - _mean(vv)
    rstd = lax.rsqrt(_mean(xc * xc) + LN_EPS)
    vn = xc * rstd
    return u, vn, rstd, vn * lng + lnb


def _causal(ws_ref, g):
    ti = lax.broadcasted_iota(jnp.int32, (SGU_CHUNK, SGU_CHUNK), 0)
    si = lax.broadcasted_iota(jnp.int32, (SGU_CHUNK, SGU_CHUNK), 1)
    return ti >= si, jnp.where(ti >= si, ws_ref[g], 0.0).astype(BF16)


def _sgu_fwd(proj, zblock, lng, lnb, ws, bexp, name):
    T, SW = proj.shape[0], lng.shape[1]
    G = ws.shape[0]
    tr = min(256, T)
    nch = tr // SGU_CHUNK

    def body(z_ref, lng_ref, lnb_ref, ws_ref, be_ref, o_ref):
        u, _, _, vl = _sgu_ln(z_ref[...], SW, lng_ref[...], lnb_ref[...])
        for g in range(G):
            cs = slice(g * SGU_GROUP, (g + 1) * SGU_GROUP)
            _, wc = _causal(ws_ref, g)
            for n in range(nch):
                rs = slice(n * SGU_CHUNK, (n + 1) * SGU_CHUNK)
                m = jnp.dot(wc, vl[rs, cs].astype(BF16), preferred_element_type=F32) + be_ref[:, cs]
                o_ref[rs, cs] = (u[rs, cs] * m).astype(BF16)

    whole = lambda arr: pl.BlockSpec(arr.shape, lambda i, nd=arr.ndim: (0,) * nd)
    return pl.pallas_call(
        body, name=name, grid=(T // tr,),
        in_specs=[pl.BlockSpec((tr, 2 * SW), lambda i: (i, zblock)), whole(lng), whole(lnb), whole(ws), whole(bexp)],
        out_specs=pl.BlockSpec((tr, SW), lambda i: (i, 0)), out_shape=jax.ShapeDtypeStruct((T, SW), BF16),
        compiler_params=_params())(proj, lng, lnb, ws, bexp)


def _sgu_bwd(proj, zblock, dyb, lng, lnb, ws, bexp, dproj, name):
    T, SW = proj.shape[0], lng.shape[1]
    G = ws.shape[0]
    tr = min(256, T)
    nch = tr // SGU_CHUNK
    nt = T // tr

    def body(z_ref, dy_ref, lng_ref, lnb_ref, ws_ref, be_ref, buf_ref, dz_ref, dlg_ref, dlb_ref, dws_ref, db_ref, du_s, dvl_s, dbacc_s):
        i = pl.program_id(0)
        zv = z_ref[...]
        lng_v = lng_ref[...]
        u, vn, rstd, vl = _sgu_ln(zv, SW, lng_v, lnb_ref[...])

        @pl.when(i == 0)
        def _():
            for s in (dlg_ref, dlb_ref, dws_ref, dbacc_s):
                s[...] = jnp.zeros_like(s)

        for g in range(G):
            cs = slice(g * SGU_GROUP, (g + 1) * SGU_GROUP)
            tri, wc = _causal(ws_ref, g)
            for n in range(nch):
                rs = slice(n * SGU_CHUNK, (n + 1) * SGU_CHUNK)
                blk = vl[rs, cs].astype(BF16)
                m = jnp.dot(wc, blk, preferred_element_type=F32) + be_ref[:, cs]
                dyv = dy_ref[rs, cs]
                du_s[rs, cs] = dyv * m
                dm = dyv * u[rs, cs]
                dvl_s[rs, cs] = _bdot(wc, dm, "tn")
                dws_ref[g] += jnp.where(tri, _bdot(dm, blk, "nt"), 0.0)
                dbacc_s[:, cs] += dm

        dvl = dvl_s[...]
        dlg_ref[...] += _colsum(dvl * vn)
        dlb_ref[...] += _colsum(dvl)
        dvn = dvl * lng_v
        dvv = rstd * (dvn - _mean(dvn) - vn * _mean(dvn * vn))
        gp = _gelu_grad(zv)
        dz_ref[:, :SW] = (du_s[...] * gp[:, :SW]).astype(BF16)
        dz_ref[:, SW:] = (dvv * gp[:, SW:]).astype(BF16)

        @pl.when(i == nt - 1)
        def _():
            lane = lax.broadcasted_iota(jnp.int32, (SGU_CHUNK, LANE), 1)
            out = jnp.zeros((SGU_CHUNK, LANE), F32)
            for g in range(G):
                col = jnp.sum(dbacc_s[:, g * SGU_GROUP:(g + 1) * SGU_GROUP], axis=1, keepdims=True)
                out = jnp.where(lane == g, col, out)
            db_ref[...] = out

    whole = lambda arr: pl.BlockSpec(arr.shape, lambda i, nd=arr.ndim: (0,) * nd)
    acc_shapes = [(1, SW), (1, SW), ws.shape, (SGU_CHUNK, LANE)]
    return pl.pallas_call(
        body, name=name, grid=(nt,),
        in_specs=[pl.BlockSpec((tr, 2 * SW), lambda i: (i, zblock)), pl.BlockSpec((tr, SW), lambda i: (i, 0)),
                  whole(lng), whole(lnb), whole(ws), whole(bexp), pl.BlockSpec(memory_space=pl.ANY)],
        out_specs=([pl.BlockSpec((tr, 2 * SW), lambda i: (i, zblock))]
                   + [pl.BlockSpec(s, lambda i, nd=len(s): (0,) * nd) for s in acc_shapes]),
        out_shape=[jax.ShapeDtypeStruct(dproj.shape, BF16)] + [jax.ShapeDtypeStruct(s, F32) for s in acc_shapes],
        scratch_shapes=[pltpu.VMEM((tr, SW), F32), pltpu.VMEM((tr, SW), F32), pltpu.VMEM((SGU_CHUNK, SW), F32)],
        input_output_aliases={6: 0}, compiler_params=_params())(proj, dyb, lng, lnb, ws, bexp, dproj)


_HBM = pl.BlockSpec(memory_space=pltpu.HBM)
_SEM = pl.BlockSpec(memory_space=pltpu.SEMAPHORE)
_DATAFLOW = pltpu.SideEffectType.DATAFLOW_SIDE_EFFECTING


def _mesh_place(chips=False):
    x, y, c = lax.axis_index("x"), lax.axis_index("y"), lax.axis_index("c")
    return x, y, c, (2 * x + y if chips else 4 * x + 2 * y + c)


def _peer(x, y, c, rel, chips=False):
    px = 1 - x if rel & 4 else x
    py = 1 - y if rel & 2 else y
    pc = 1 - c if rel & 1 else c
    return (px, py, pc), (2 * px + py if chips else 4 * px + 2 * py + pc)


ALL_PEERS = tuple(range(1, N_DEV))
SIBLING = (1,)
SAME_CORE = (2, 4, 6)
SIBLINGS_CORE = (3, 5, 7)


def _exchange_start(groups, name, rels=ALL_PEERS, chips=False):
    flat = [t for g in groups for t in g]
    sizes = [len(g) for g in groups]
    n, ng = len(flat), len(groups)
    srcs = [pltpu.with_memory_space_constraint(a, pltpu.HBM) for a, _ in flat]
    lands = [pltpu.with_memory_space_constraint(lax.empty(((N_DEV,) + a.shape) if isg else a.shape, a.dtype), pltpu.HBM)
             for a, isg in flat]

    def body(*refs):
        ins, lnd, sems, token = refs[:n], refs[n:2 * n], refs[2 * n:2 * n + 3 * ng], refs[-1]
        x, y, c, me = _mesh_place(chips)
        j0 = 0
        for gi, sz in enumerate(sizes):
            for rel in rels:
                dev, slot = _peer(x, y, c, rel, chips)
                for jj in range(sz):
                    j = j0 + jj
                    pltpu.make_async_remote_copy(
                        src_ref=ins[j] if flat[j][1] else ins[j].at[slot], dst_ref=lnd[j].at[me],
                        send_sem=sems[3 * gi].at[jj * (N_DEV - 1) + rel - 1], recv_sem=sems[3 * gi + 1].at[jj * (N_DEV - 1) + rel - 1],
                        device_id=dev, device_id_type=pl.DeviceIdType.MESH).start()
            for jj in range(sz):
                j = j0 + jj
                pltpu.make_async_copy(ins[j] if flat[j][1] else ins[j].at[me], lnd[j].at[me], sems[3 * gi + 2].at[jj]).start()
            j0 += sz
        token[...] = jnp.zeros_like(token)

    sem_shapes = [pltpu.SemaphoreType.DMA((k,)) for sz in sizes for k in (sz * (N_DEV - 1), sz * (N_DEV - 1), sz)]
    res = pl.pallas_call(
        body, name=name,
        out_shape=(*sem_shapes, *[pltpu.HBM(a.shape, a.dtype) for a in srcs], *[pltpu.HBM(a.shape, a.dtype) for a in lands],
                   jax.ShapeDtypeStruct((SUBLANE, LANE), F32)),
        in_specs=[_HBM] * (2 * n), out_specs=(*[_SEM] * (3 * ng), *[_HBM] * (2 * n), pl.BlockSpec(memory_space=pltpu.VMEM)),
        input_output_aliases={i: 3 * ng + i for i in range(2 * n)},
        compiler_params=pltpu.CompilerParams(has_side_effects=_DATAFLOW))(*srcs, *lands)
    sems, thru, token = res[:3 * ng], res[3 * ng:3 * ng + 2 * n], res[-1]
    handle, j0 = [], 0
    for gi, sz in enumerate(sizes):
        handle.append(dict(kinds=[k for _, k in groups[gi]], chips=chips, srcs=list(thru[j0:j0 + sz]), lands=list(thru[n + j0:n + j0 + sz]),
                           sems=list(sems[3 * gi:3 * gi + 3])))
        j0 += sz
    return handle, token


def _exchange_wait(group, after, name, rels=ALL_PEERS, local=True):
    kinds, sz = group["kinds"], len(group["kinds"])
    relay = group.get("relay", [])

    def body(*refs):
        ins, lnd, (ssem, rsem, lsem) = refs[:sz], refs[sz:2 * sz], refs[2 * sz:2 * sz + 3]
        x, y, c, me = _mesh_place(group["chips"])
        for rel in rels:
            dev, slot = _peer(x, y, c, rel, group["chips"])
            for jj in range(sz):
                cp = pltpu.make_async_remote_copy(
                    src_ref=ins[jj] if kinds[jj] else ins[jj].at[slot], dst_ref=lnd[jj].at[slot],
                    send_sem=ssem.at[jj * (N_DEV - 1) + rel - 1], recv_sem=rsem.at[jj * (N_DEV - 1) + rel - 1],
                    device_id=dev, device_id_type=pl.DeviceIdType.MESH)
                cp.wait_send()
                cp.wait_recv()
        if local:
            for jj in range(sz):
                pltpu.make_async_copy(ins[jj] if kinds[jj] else ins[jj].at[me], lnd[jj].at[me], lsem.at[jj]).wait()
        if relay:
            fsend, frecv = refs[2 * sz + 3:2 * sz + 5]
            dev = _peer(x, y, c, 1)[0]
            for q, (mine, theirs) in enumerate(zip(SAME_CORE, SIBLINGS_CORE)):
                for jj in range(sz):
                    cp = pltpu.make_async_remote_copy(
                        src_ref=lnd[jj].at[_peer(x, y, c, mine)[1]], dst_ref=lnd[jj].at[_peer(x, y, c, theirs)[1]],
                        send_sem=fsend.at[jj * len(SAME_CORE) + q], recv_sem=frecv.at[jj * len(SAME_CORE) + q],
                        device_id=dev, device_id_type=pl.DeviceIdType.MESH)
                    cp.wait_send()
                    cp.wait_recv()

    arrays = group["srcs"] + group["lands"]
    sems = group["sems"] + relay
    res = pl.pallas_call(
        body, name=name, out_shape=[pltpu.HBM(a.shape, a.dtype) for a in arrays],
        in_specs=[_HBM] * (2 * sz) + [_SEM] * len(sems) + [pl.BlockSpec(memory_space=pl.ANY)], out_specs=[_HBM] * (2 * sz),
        input_output_aliases={i: i for i in range(2 * sz)},
        compiler_params=pltpu.CompilerParams(has_side_effects=_DATAFLOW))(*arrays, *sems, after)
    return dict(group, srcs=list(res[:sz]), lands=list(res[sz:]), relay=[])


def _relay_start(group, name):
    sz = len(group["kinds"])
    nq = len(SAME_CORE)

    def body(*refs):
        lnd, fsend, frecv, token = refs[:sz], refs[sz], refs[sz + 1], refs[-1]
        x, y, c, _ = _mesh_place()
        dev = _peer(x, y, c, 1)[0]
        for q, rel in enumerate(SAME_CORE):
            slot = _peer(x, y, c, rel)[1]
            for jj in range(sz):
                pltpu.make_async_remote_copy(
                    src_ref=lnd[jj].at[slot], dst_ref=lnd[jj].at[slot], send_sem=fsend.at[jj * nq + q], recv_sem=frecv.at[jj * nq + q],
                    device_id=dev, device_id_type=pl.DeviceIdType.MESH).start()
        token[...] = jnp.zeros_like(token)

    lands = group["lands"]
    res = pl.pallas_call(
        body, name=name,
        out_shape=(pltpu.SemaphoreType.DMA((sz * nq,)), pltpu.SemaphoreType.DMA((sz * nq,)), *[pltpu.HBM(a.shape, a.dtype) for a in lands],
                   jax.ShapeDtypeStruct((SUBLANE, LANE), F32)),
        in_specs=[_HBM] * sz, out_specs=(_SEM, _SEM, *[_HBM] * sz, pl.BlockSpec(memory_space=pltpu.VMEM)),
        input_output_aliases={i: 2 + i for i in range(sz)},
        compiler_params=pltpu.CompilerParams(has_side_effects=_DATAFLOW))(*lands)
    return dict(group, lands=list(res[2:2 + sz]), relay=[res[0], res[1]]), res[-1]


def _sibling_swap(arrays, handle, after, name):
    start = handle is None
    n = len(arrays) if start else len(handle["srcs"])
    chips = N_DEV // 2
    if start:
        srcs = [pltpu.with_memory_space_constraint(a.reshape(chips, 2, *a.shape[1:]), pltpu.HBM) for a in arrays]
        lands = [pltpu.with_memory_space_constraint(lax.empty((chips,) + a.shape[1:], a.dtype), pltpu.HBM) for a in arrays]
    else:
        srcs, lands = handle["srcs"], handle["lands"]

    def body(*refs):
        ins, lnd, ssem, rsem = refs[:n], refs[n:2 * n], refs[2 * n], refs[2 * n + 1]
        x, y, c, _ = _mesh_place()
        dev = _peer(x, y, c, 1)[0]
        for q in range(chips):
            for j in range(n):
                cp = pltpu.make_async_remote_copy(
                    src_ref=ins[j].at[q, 1 - c], dst_ref=lnd[j].at[q], send_sem=ssem.at[j * chips + q], recv_sem=rsem.at[j * chips + q],
                    device_id=dev, device_id_type=pl.DeviceIdType.MESH)
                if start:
                    cp.start()
                else:
                    cp.wait_send()
                    cp.wait_recv()
        if start:
            refs[-1][...] = jnp.zeros_like(refs[-1])

    thru = [pltpu.HBM(a.shape, a.dtype) for a in srcs + lands]
    effect = pltpu.CompilerParams(has_side_effects=_DATAFLOW)
    if start:
        res = pl.pallas_call(
            body, name=name, out_shape=(pltpu.SemaphoreType.DMA((n * chips,)), pltpu.SemaphoreType.DMA((n * chips,)), *thru,
                                        jax.ShapeDtypeStruct((SUBLANE, LANE), F32)),
            in_specs=[_HBM] * (2 * n), out_specs=(_SEM, _SEM, *[_HBM] * (2 * n), pl.BlockSpec(memory_space=pltpu.VMEM)),
            input_output_aliases={i: 2 + i for i in range(2 * n)}, compiler_params=effect)(*srcs, *lands)
        return dict(srcs=list(res[2:2 + n]), lands=list(res[2 + n:2 + 2 * n]), sems=[res[0], res[1]]), res[-1]
    res = pl.pallas_call(
        body, name=name, out_shape=thru, in_specs=[_HBM] * (2 * n) + [_SEM, _SEM, pl.BlockSpec(memory_space=pl.ANY)],
        out_specs=[_HBM] * (2 * n), input_output_aliases={i: i for i in range(2 * n)}, compiler_params=effect)(
            *srcs, *lands, *handle["sems"], after)
    return dict(handle, srcs=list(res[:n]), lands=list(res[n:]))


def _pair_add(mine, theirs, core, name):
    chips, _, rows, w = mine.shape
    tm = _pick(rows, (256, 128, 64, 32, 16))

    def body(core_ref, a_ref, b_ref, o_ref):
        o_ref[...] = (a_ref[...].astype(F32) + b_ref[...].astype(F32)).astype(o_ref.dtype)

    return pl.pallas_call(
        body, name=name, out_shape=jax.ShapeDtypeStruct(theirs.shape, theirs.dtype),
        grid_spec=pltpu.PrefetchScalarGridSpec(
            num_scalar_prefetch=1, grid=(chips, rows // tm),
            in_specs=[pl.BlockSpec((None, None, tm, w), lambda q, i, core_ref: (q, core_ref[0], i, 0)),
                      pl.BlockSpec((None, tm, w), lambda q, i, core_ref: (q, i, 0))],
            out_specs=pl.BlockSpec((None, tm, w), lambda q, i, core_ref: (q, i, 0))),
        compiler_params=_params())(core, mine, theirs)


def _adamw(w, m, v, gparts, name, after=None):
    R, C = w.shape
    tm = _pick(R, (256, 128, 64, 32, 16, 8))
    order = [] if after is None else [after]

    def body(w_ref, m_ref, v_ref, g_ref, *rest):
        go, do, mo, vo = rest[len(order):]
        g = g_ref[0].astype(F32)
        for j in range(1, gparts.shape[0]):
            g = g + g_ref[j].astype(F32)
        mn = ADAM_B1 * m_ref[...] + (1.0 - ADAM_B1) * g
        vn = ADAM_B2 * v_ref[...] + (1.0 - ADAM_B2) * (g * g)
        m_hat = mn / (1.0 - ADAM_B1 ** ADAM_STEP)
        v_hat = vn / (1.0 - ADAM_B2 ** ADAM_STEP)
        go[...] = g
        do[...] = -ADAM_LR * (m_hat / (jnp.sqrt(v_hat) + ADAM_EPS) + ADAM_WD * w_ref[...])
        mo[...] = mn
        vo[...] = vn

    row = pl.BlockSpec((tm, C), lambda i: (i, 0))
    return pl.pallas_call(
        body, name=name, grid=(R // tm,),
        in_specs=[row, row, row, pl.BlockSpec((gparts.shape[0], tm, C), lambda i: (0, i, 0))] + [pl.BlockSpec(memory_space=pl.ANY)] * len(order),
        out_specs=[row] * 4, out_shape=[jax.ShapeDtypeStruct((R, C), F32)] * 4, compiler_params=_params())(w, m, v, gparts, *order)


def _pack(arrays):
    parts = []
    for a in arrays:
        f = a.reshape(1, -1)
        pad = _ceil_to(f.shape[1], SUBLANE * LANE) - f.shape[1]
        f = jnp.concatenate([f, jnp.zeros((1, pad), f.dtype)], axis=1) if pad else f
        parts.append(f.reshape(-1, LANE))
    rows = sum(p.shape[0] for p in parts)
    pad = _ceil_to(rows, 64) - rows
    return jnp.concatenate(parts + ([jnp.zeros((pad, LANE), parts[0].dtype)] if pad else []), axis=0)


def _unpack(buf, shapes):
    out, row = [], 0
    for s in shapes:
        size = 1
        for d in s:
            size *= d
        rows = _ceil_to(size, SUBLANE * LANE) // LANE
        out.append(buf[row:row + rows].reshape(1, -1)[:, :size].reshape(s))
        row += rows
    return out


def kernel(x, norm_mix_g, w_in, shift_mu, w0, w_lora_up, a0, a_lora_up, g_lora_up, k_k, k_a, r_k, lnx_g, lnx_b, w_proj_rwkv, sgu_ln_g, sgu_ln_b, sgu_w, sgu_b, w_proj_sgu, w_out, norm_ffn_g, w_ffn_gate, w_ffn_up, w_ffn_down, norm_final_g, loss_target, m_norm_mix_g, m_w_in, m_shift_mu, m_w0, m_w_lora_up, m_a0, m_a_lora_up, m_g_lora_up, m_k_k, m_k_a, m_r_k, m_lnx_g, m_lnx_b, m_w_proj_rwkv, m_sgu_ln_g, m_sgu_ln_b, m_sgu_w, m_sgu_b, m_w_proj_sgu, m_w_out, m_norm_ffn_g, m_w_ffn_gate, m_w_ffn_up, m_w_ffn_down, m_norm_final_g, v_norm_mix_g, v_w_in, v_shift_mu, v_w0, v_w_lora_up, v_a0, v_a_lora_up, v_g_lora_up, v_k_k, v_k_a, v_r_k, v_lnx_g, v_lnx_b, v_w_proj_rwkv, v_sgu_ln_g, v_sgu_ln_b, v_sgu_w, v_sgu_b, v_w_proj_sgu, v_w_out, v_norm_ffn_g, v_w_ffn_gate, v_w_ffn_up, v_w_ffn_down, v_norm_final_g):
    weights = dict(norm_mix_g=norm_mix_g, w_in=w_in, shift_mu=shift_mu, w0=w0, w_lora_up=w_lora_up, a0=a0, a_lora_up=a_lora_up,
                   g_lora_up=g_lora_up, k_k=k_k, k_a=k_a, r_k=r_k, lnx_g=lnx_g, lnx_b=lnx_b, w_proj_rwkv=w_proj_rwkv,
                   sgu_ln_g=sgu_ln_g, sgu_ln_b=sgu_ln_b, sgu_w=sgu_w, sgu_b=sgu_b, w_proj_sgu=w_proj_sgu, w_out=w_out,
                   norm_ffn_g=norm_ffn_g, w_ffn_gate=w_ffn_gate, w_ffn_up=w_ffn_up, w_ffn_down=w_ffn_down, norm_final_g=norm_final_g)
    m_in = dict(norm_mix_g=m_norm_mix_g, w_in=m_w_in, shift_mu=m_shift_mu, w0=m_w0, w_lora_up=m_w_lora_up, a0=m_a0,
                a_lora_up=m_a_lora_up, g_lora_up=m_g_lora_up, k_k=m_k_k, k_a=m_k_a, r_k=m_r_k, lnx_g=m_lnx_g, lnx_b=m_lnx_b,
                w_proj_rwkv=m_w_proj_rwkv, sgu_ln_g=m_sgu_ln_g, sgu_ln_b=m_sgu_ln_b, sgu_w=m_sgu_w, sgu_b=m_sgu_b,
                w_proj_sgu=m_w_proj_sgu, w_out=m_w_out, norm_ffn_g=m_norm_ffn_g, w_ffn_gate=m_w_ffn_gate, w_ffn_up=m_w_ffn_up,
                w_ffn_down=m_w_ffn_down, norm_final_g=m_norm_final_g)
    v_in = dict(norm_mix_g=v_norm_mix_g, w_in=v_w_in, shift_mu=v_shift_mu, w0=v_w0, w_lora_up=v_w_lora_up, a0=v_a0,
                a_lora_up=v_a_lora_up, g_lora_up=v_g_lora_up, k_k=v_k_k, k_a=v_k_a, r_k=v_r_k, lnx_g=v_lnx_g, lnx_b=v_lnx_b,
                w_proj_rwkv=v_w_proj_rwkv, sgu_ln_g=v_sgu_ln_g, sgu_ln_b=v_sgu_ln_b, sgu_w=v_sgu_w, sgu_b=v_sgu_b,
                w_proj_sgu=v_w_proj_sgu, w_out=v_w_out, norm_ffn_g=v_norm_ffn_g, w_ffn_gate=v_w_ffn_gate, w_ffn_up=v_w_ffn_up,
                w_ffn_down=v_w_ffn_down, norm_final_g=v_norm_final_g)
    names = list(weights)
    col_sharded = ("w_in", "w_lora_up", "a_lora_up", "g_lora_up", "w_proj_rwkv", "w_proj_sgu", "w_ffn_gate", "w_ffn_up")
    row_sharded = ("w_out", "w_ffn_down")
    sharded = [n for n in names if n in col_sharded or n in row_sharded]
    small = [n for n in names if n not in sharded]

    xs, tgt = x[0], loss_target[0]
    T, D = xs.shape
    RW = w0.shape[1]
    H = RW // HEAD
    SW = sgu_ln_g.shape[1]
    G = sgu_w.shape[1]
    assert 2 * SW == D, "the projection layout takes the SGU part to be as wide as a gate"
    lay = _rwkv_layout(RW, w_lora_up.shape[1], a_lora_up.shape[1], g_lora_up.shape[1], D)
    _, pw, _, rcp = lay
    icp = rcp + 3 * D
    b_ga, b_gb, b_z = rcp // D, rcp // D + 1, rcp // D + 2

    gather_groups = [["w_in", "w_lora_up", "a_lora_up", "g_lora_up"], ["w_proj_rwkv", "w_proj_sgu", "w_out"],
                     ["w_ffn_gate"], ["w_ffn_up"], ["w_ffn_down"]]
    gather, gather_token = _exchange_start([[(weights[n][0].astype(BF16), True) for n in grp] for grp in gather_groups],
                                           "gather_start", rels=SIBLING + SAME_CORE)
    full = {}
    relay_tokens = {}
    joined = lambda g: g.transpose(1, 0, 2).reshape(g.shape[1], -1)

    def relay_weights(gi, after, name):
        arrived = _exchange_wait(gather[gi], after, "gather_wait_ici_" + name, rels=SAME_CORE, local=False)
        gather[gi], relay_tokens[gi] = _relay_start(arrived, "gather_relay_" + name)

    def take_weights(gi, after, name):
        done = _exchange_wait(gather[gi], after, "gather_wait_d2d_" + name, rels=SIBLING)
        for n, g in zip(gather_groups[gi], done["lands"]):
            full[n] = g.reshape(-1, g.shape[2]) if n in row_sharded else g

    n1 = _rms_fwd(xs, norm_mix_g, "rms_mix", deps=[gather_token])
    relay_weights(0, n1, "in")
    take_weights(0, relay_tokens[0], "in")
    W_in = _w_in_to_proj(full["w_in"], lay, D, "w_in_layout")
    lora = [_pad_rows(joined(full[n]), rows) for n, rows in zip(("w_lora_up", "a_lora_up", "g_lora_up"), pw[3:])]
    mu_p = _pad_rwkv_cols(shift_mu, lay)
    rsmall = [w0, a0, k_k, k_a]
    hp = [lnx_g.reshape(H, 1, HEAD), lnx_b.reshape(H, 1, HEAD), r_k.reshape(H, 1, HEAD)]
    ws = sgu_w[0]
    bexp = jnp.repeat(sgu_b[0].T, SGU_GROUP, axis=1)
    gf = norm_final_g.reshape(1, D)

    proj = _matmul(n1, W_in, mode="nn", out_dtype=F32, name="proj_in")
    ga, gb = (proj, D, b_ga), (proj, D, b_gb)
    r_h, lw_h, k2_h, v_h, aa_h, bb_h, g_h = _rwkv_pre(proj, mu_p, rsmall, lora, lay, "rwkv_pre")
    wkv_in = [r_h, lw_h, k2_h, v_h, aa_h, bb_h]
    y_h, states = _wkv_fwd(*wkv_in, "wkv_fwd")
    relay_weights(1, y_h, "proj")
    relay_weights(2, relay_tokens[1], "ffn_gate")
    ya = _head_post(y_h, r_h, k2_h, v_h, g_h, hp, "head_post", deps=[relay_tokens[2]])
    relay_weights(3, ya, "ffn_up")
    yb = _sgu_fwd(proj, b_z, sgu_ln_g, sgu_ln_b, ws, bexp, "sgu_fwd")
    take_weights(1, ya, "proj")
    pa = _matmul(ya, full["w_proj_rwkv"], mode="nn", out_dtype=F32, name="proj_a", deps=[relay_tokens[3]])

    def merge_fn(pb_v, pa_v, ga_v, gb_v):
        return pb_v, _sigmoid(ga_v) * pa_v + _sigmoid(gb_v) * pb_v
    pb, merged = _matmul(yb, full["w_proj_sgu"], mode="nn", name="proj_b_merge",
                         epi=(merge_fn, [pa, (proj, b_ga * D), (proj, b_gb * D)], [F32, BF16]))
    h1 = _matmul(merged, full["w_out"], mode="nn", out_dtype=F32, name="out_proj", add=xs)
    n2 = _rms_fwd(h1, norm_ffn_g, "rms_ffn")
    relay_weights(4, n2, "ffn_down")
    take_weights(2, n2, "ffn_gate")
    gt = _matmul(n2, full["w_ffn_gate"], mode="nn", out_dtype=F32, name="ffn_gate", out_blocks=N_DEV, deps=[relay_tokens[4]])
    take_weights(3, gt, "ffn_up")

    def act_fn(up_v, gt_v):
        return up_v, gt_v * _sigmoid(gt_v) * up_v
    up, act = _matmul(n2, full["w_ffn_up"], mode="nn", name="ffn_up_act", out_blocks=N_DEV, epi=(act_fn, [gt], [F32, BF16]))
    take_weights(4, act, "ffn_down")
    h2 = _matmul(act, full["w_ffn_down"], mode="nn", out_dtype=F32, name="ffn_down", add=h1)

    def final_fn(rv, pv):
        (h_v, t_v), (g_v,) = rv, pv
        r = lax.rsqrt(_mean(h_v * h_v) + RMS_EPS)
        yn = h_v * r
        e = yn * g_v - t_v
        loss = 0.5 * jnp.sum(_mean(e * e))
        dout = e * (1.0 / D)
        dyg = dout * g_v
        dh = r * (dyg - yn * _mean(dyg * yn))
        return [dh, dh], [jnp.full((1, LANE), loss, F32), _colsum(dout * yn)]
    dh2, dh2_bf, loss_part, d_gf = _rowwise(final_fn, [h2, tgt], [gf], [(D, F32), (D, BF16)], [(1, LANE), (1, D)], name="final_loss")

    grads = {}

    def start_scatter(group, name, extra=()):
        blocks = [(grads[n].reshape(N_DEV, -1, grads[n].shape[1]) if n in row_sharded else grads[n], False) for n in group]
        (handle,), token = _exchange_start([blocks + list(extra)], name)
        return handle, token

    def dact_fn(d_v, gt_v, up_v):
        s = _sigmoid(gt_v)
        return d_v * up_v * (s * (1.0 + gt_v * (1.0 - s))), d_v * gt_v * s
    dgt, dup = _matmul(dh2_bf, full["w_ffn_down"], mode="nt", name="d_ffn_act", out_blocks=N_DEV,
                       epi=(dact_fn, [gt, up], [BF16, BF16]))
    grads["w_ffn_down"] = _matmul(act, dh2_bf, mode="tn", out_dtype=BF16, name="dw_ffn_down")
    dn2 = _matmul(dgt, full["w_ffn_gate"], mode="nt", out_dtype=F32, name="dn2_gate")
    dn2 = _matmul(dup, full["w_ffn_up"], mode="nt", out_dtype=F32, name="dn2_up", add=dn2)
    grads["w_ffn_gate"] = _matmul(n2, dgt, mode="tn", out_dtype=BF16, name="dw_ffn_gate", out_blocks=N_DEV)
    grads["w_ffn_up"] = _matmul(n2, dup, mode="tn", out_dtype=BF16, name="dw_ffn_up", out_blocks=N_DEV)
    scatter_groups = [["w_ffn_down", "w_ffn_gate", "w_ffn_up"], ["w_out", "w_proj_rwkv", "w_proj_sgu"],
                      ["w_in", "w_lora_up", "a_lora_up", "g_lora_up"]]
    scatter_ffn, token_ffn = start_scatter(scatter_groups[0], "scatter_start_ffn")
    dh1, dh1_bf, d_g2 = _rms_bwd(dn2, h1, dh2, norm_ffn_g, "rms_ffn_bwd", deps=[token_ffn])
    dmerged = _matmul(dh1_bf, full["w_out"], mode="nt", out_dtype=F32, name="d_merged")
    grads["w_out"] = _matmul(merged, dh1_bf, mode="tn", out_dtype=BF16, name="dw_out")

    def dmerge_fn(rv, pv):
        d_v, ga_v, gb_v, pa_v, pb_v = rv
        sa, sb = _sigmoid(ga_v), _sigmoid(gb_v)
        dgates = jnp.concatenate([d_v * pa_v * sa * (1.0 - sa), d_v * pb_v * sb * (1.0 - sb)], axis=1)
        return [dgates, d_v * sa, d_v * sb], []
    dproj, dpa, dpb = _rowwise(dmerge_fn, [dmerged, ga, gb, pa, pb], [],
                               [(2 * D, BF16, icp, b_ga // 2, None), (D, BF16), (D, BF16)], [], name="d_merge")
    dya = _matmul(dpa, full["w_proj_rwkv"], mode="nt", out_dtype=F32, name="d_ya")
    dyb = _matmul(dpb, full["w_proj_sgu"], mode="nt", out_dtype=F32, name="d_yb")
    grads["w_proj_rwkv"] = _matmul(ya, dpa, mode="tn", out_dtype=BF16, name="dw_proj_a", out_blocks=N_DEV)
    grads["w_proj_sgu"] = _matmul(yb, dpb, mode="tn", out_dtype=BF16, name="dw_proj_b", out_blocks=N_DEV)
    scatter_mid, token_mid = start_scatter(scatter_groups[1], "scatter_start_mid")
    dproj, d_lng, d_lnb, d_ws, d_bs = _sgu_bwd(proj, b_z, dyb, sgu_ln_g, sgu_ln_b, ws, bexp, dproj, "sgu_bwd")

    dy_h, dr1, dk1, dv1, dg_h, d_lnxg, d_lnxb, d_rk = _head_post_bwd(dya, y_h, r_h, k2_h, v_h, g_h, hp, "head_post_bwd",
                                                                     deps=[token_mid])
    dr2, dlw_h, dk2b, dv2, daa, dbb = _wkv_bwd(*wkv_in, states, dy_h, "wkv_bwd")
    dps, d_mu, d_w0, d_a0, d_kk, d_ka, d_wlw, d_wla, d_wlg = _rwkv_pre_bwd(
        proj, mu_p, rsmall, lora, [dr1, dr2, dk1, dk2b, dv1, dv2, dlw_h, daa, dbb, dg_h], lay, "rwkv_pre_bwd")
    dproj = _shift_bwd(dps, mu_p, dproj, "shift_bwd")
    split = lambda g: g.reshape(g.shape[0], N_DEV, -1).transpose(1, 0, 2)
    grads["w_in"] = _dw_in_from_proj(_matmul(n1, dproj, mode="tn", out_dtype=BF16, name="dw_in"), lay, D, w_in.shape[2], "dw_in_layout")
    grads["w_lora_up"] = split(d_wlw[:w_lora_up.shape[1]].astype(BF16))
    grads["a_lora_up"] = split(d_wla[:a_lora_up.shape[1]].astype(BF16))
    grads["g_lora_up"] = split(d_wlg[:g_lora_up.shape[1]].astype(BF16))
    swap, token_swap = _sibling_swap([grads[n] for n in scatter_groups[2]], None, None, "scatter_in_swap_start")
    dn1 = _matmul(dproj, W_in, mode="nt", out_dtype=F32, name="dn1", deps=[token_swap])
    swap = _sibling_swap(None, swap, dn1, "scatter_in_swap_wait")
    core = lax.axis_index("c").astype(jnp.int32).reshape(1)
    chip_sums = [_pair_add(mine, theirs, core, "scatter_in_add_" + n)
                 for n, mine, theirs in zip(scatter_groups[2], swap["srcs"], swap["lands"])]
    (scatter_in,), token_in = _exchange_start([[(s, False) for s in chip_sums]], "scatter_start_in", rels=SAME_CORE, chips=True)
    dx, _, d_g1 = _rms_bwd(dn1, xs, dh1, norm_mix_g, "rms_mix_bwd", deps=[token_in])
    small_grads = dict(norm_mix_g=d_g1, shift_mu=_unpad_rwkv_cols(d_mu, lay), w0=d_w0, a0=d_a0, k_k=d_kk, k_a=d_ka, r_k=d_rk,
                       lnx_g=d_lnxg, lnx_b=d_lnxb, sgu_ln_g=d_lng, sgu_ln_b=d_lnb, sgu_w=d_ws, sgu_b=d_bs[:, :G].T,
                       norm_ffn_g=d_g2, norm_final_g=d_gf)

    (gather_small,), after = _exchange_start([[(_pack([small_grads[n] for n in small]), True)]], "gather_small_start")
    out = {}
    for group, handle, name in zip(scatter_groups, (scatter_ffn, scatter_mid, scatter_in), ("ffn", "mid", "in")):
        parts = _exchange_wait(handle, after, "scatter_wait_" + name, rels=SAME_CORE if handle["chips"] else ALL_PEERS)["lands"]
        for n, part in zip(group, parts):
            shp = weights[n].shape
            res = _adamw(weights[n][0], m_in[n][0], v_in[n][0], part, "adamw_" + n, after=after)
            out[n] = [t.reshape(shp) for t in res]
            after = res[0]
    packed = [_pack([d[n] for n in small]) for d in (weights, m_in, v_in)]
    small_parts = _exchange_wait(gather_small, after, "gather_small_wait")["lands"][0]
    res = _adamw(*packed, small_parts, "adamw_small")
    unpacked = [_unpack(t, [weights[n].shape for n in small]) for t in res]
    for i, n in enumerate(small):
        out[n] = [u[i] for u in unpacked]

    loss = lax.psum(loss_part[0, 0], ("x", "y", "c"))
    return (loss, dx[None], *[out[n][0] for n in names], *[out[n][1] for n in names],
            *[out[n][2] for n in names], *[out[n][3] for n in names])
```

```python
import jax
import jax.numpy as jnp
from jax import lax
from jax.experimental import pallas as pl
from jax.experimental.pallas import tpu as pltpu

F32 = jnp.float32
BF16 = jnp.bfloat16

N_DEV = 8
LANE = 128
SUBLANE = 8
HEAD = 64
SGU_CHUNK = 128
SGU_GROUP = 128
WKV_CHUNK = 64
RMS_EPS = 1e-6
LN_EPS = 1e-5
LNX_EPS = 64e-5
ADAM_LR, ADAM_B1, ADAM_B2, ADAM_EPS, ADAM_WD, ADAM_STEP = 0.001, 0.9, 0.999, 1e-08, 0.01, 10
VMEM_LIMIT_BYTES = 48 * 1024 * 1024
_SQRT_HALF = 0.7071067811865476
_INV_SQRT_2PI = 0.3989422804014327


def _pick(n, cands):
    for c in cands:
        if n % c == 0:
            return c
    return n


def _ceil_to(n, m):
    return -(-n // m) * m


def _params():
    return pltpu.CompilerParams(vmem_limit_bytes=VMEM_LIMIT_BYTES)


def _tile(n, cap):
    best = 0
    for d in range(LANE, min(n, cap) + 1, LANE):
        if n % d == 0:
            best = d
    return best or n


def _matmul_tiles(M, N, K, a_bytes, b_bytes, o_bytes, has_add, forced):
    tm = forced.get("m") or _tile(M, 1024)
    tn = forced.get("n") or _tile(N, 1024)
    tk = forced.get("k") or _tile(K, 2048)

    def vmem(tm, tn, tk):
        acc = tm * tn * 4 if tk < K else 0
        return 2 * (tm * tk * a_bytes + tk * tn * b_bytes + tm * tn * (o_bytes + (4 if has_add else 0))) + acc

    while vmem(tm, tn, tk) > (VMEM_LIMIT_BYTES * 3) // 4:
        if "k" not in forced and tk > 512 and _tile(K, tk // 2) < tk:
            tk = _tile(K, tk // 2)
        elif "m" not in forced and _tile(M, tm // 2) < tm:
            tm = _tile(M, tm // 2)
        else:
            break
    return tm, tn, tk


def _matmul(a, b, *, mode, out_dtype=F32, name, add=None, deps=(), out_blocks=0, epi=None):
    def view(x):
        return (x.shape[1], x.shape[0] * x.shape[2], x.shape[2]) if x.ndim == 3 else (x.shape[0], x.shape[1], 0)

    (ar, ac, aw), (br, bc, bw) = view(a), view(b)
    a_col, b_col = {"nn": ("k", "n"), "nt": ("k", "k"), "tn": ("m", "n")}[mode]
    if mode == "nn":
        M, K, K2, N = ar, ac, br, bc
    elif mode == "nt":
        M, K, N, K2 = ar, ac, br, bc
    else:
        K, M, K2, N = ar, ac, br, bc
    assert K == K2, (a.shape, b.shape, mode)
    forced = {}
    for dim, w in ((a_col, aw), (b_col, bw), ("n", N // out_blocks if out_blocks else 0)):
        if w:
            assert forced.get(dim, w) == w
            forced[dim] = w
    has_add = add is not None
    tile_bytes = (sum(jnp.dtype(d).itemsize for d in epi[2]) + sum((e[0] if isinstance(e, tuple) else e).dtype.itemsize for e in epi[1])
                  if epi is not None else jnp.dtype(out_dtype).itemsize)
    tm, tn, tk = _matmul_tiles(M, N, K, a.dtype.itemsize, b.dtype.itemsize, tile_bytes, has_add, forced)
    nk = K // tk
    dn = {"nn": (((1,), (0,)), ((), ())), "nt": (((1,), (1,)), ((), ())), "tn": (((0,), (0,)), ((), ()))}[mode]
    pick = {"m": lambda i, j, k: i, "n": lambda i, j, k: j, "k": lambda i, j, k: k}
    size = {"m": tm, "n": tn, "k": tk}

    def spec(blocked, row_dim, col_dim):
        rf, cf = pick[row_dim], pick[col_dim]
        if blocked:
            return pl.BlockSpec((None, size[row_dim], size[col_dim]), lambda i, j, k: (cf(i, j, k), rf(i, j, k), 0))
        return pl.BlockSpec((size[row_dim], size[col_dim]), lambda i, j, k: (rf(i, j, k), cf(i, j, k)))

    a_spec = spec(aw, "k" if mode == "tn" else "m", a_col)
    b_spec = spec(bw, "n" if mode == "nt" else "k", b_col)
    o_spec = spec(out_blocks, "m", "n")
    epi_fn, epi_ins, epi_dtypes = epi if epi is not None else (None, [], [out_dtype])
    epi_ins = [e if isinstance(e, tuple) else (e, None) for e in epi_ins]
    n_epi = len(epi_ins)
    n_in = 2 + has_add + n_epi + len(deps)
    n_out = len(epi_dtypes)

    def body(*refs):
        a_ref, b_ref = refs[0], refs[1]
        add_ref = refs[2] if has_add else None
        epi_refs = refs[2 + has_add:2 + has_add + n_epi]
        o_refs = refs[n_in:n_in + n_out]
        part = lax.dot_general(a_ref[...].astype(BF16), b_ref[...].astype(BF16), dn, preferred_element_type=F32)

        def finish(res):
            outs = epi_fn(res, *[e[...] for e in epi_refs]) if epi_fn is not None else (res,)
            for o_ref, val in zip(o_refs, outs):
                o_ref[...] = val.astype(o_ref.dtype)

        if nk == 1:
            finish(part + add_ref[...] if has_add else part)
            return
        acc_ref = refs[-1]
        kk = pl.program_id(2)

        @pl.when(kk == 0)
        def _():
            acc_ref[...] = part + add_ref[...] if has_add else part

        @pl.when(kk > 0)
        def _():
            acc_ref[...] += part

        @pl.when(kk == nk - 1)
        def _():
            finish(acc_ref[...])

    def epi_spec(arr, off):
        if off is None:
            return o_spec
        assert off % tn == 0
        return pl.BlockSpec((tm, tn), lambda i, j, k: (i, j + off // tn))

    ins = [a, b] + ([add] if has_add else []) + [arr for arr, _ in epi_ins] + list(deps)
    in_specs = ([a_spec, b_spec] + ([o_spec] if has_add else []) + [epi_spec(arr, off) for arr, off in epi_ins]
                + [pl.BlockSpec(d.shape, lambda i, j, k, nd=d.ndim: (0,) * nd) for d in deps])
    o_shape = (out_blocks, M, tn) if out_blocks else (M, N)
    res = pl.pallas_call(
        body, name=name, grid=(M // tm, N // tn, nk), in_specs=in_specs, out_specs=[o_spec] * n_out,
        out_shape=[jax.ShapeDtypeStruct(o_shape, dt) for dt in epi_dtypes],
        scratch_shapes=[pltpu.VMEM((tm, tn), F32)] if nk > 1 else [],
        compiler_params=_params())(*ins)
    return res[0] if epi is None else list(res)


def _rowwise(fn, rows, pars, row_outs, acc_outs, *, name, tm=256, deps=()):
    rows = [r if isinstance(r, tuple) else (r, r.shape[1], 0) for r in rows]
    row_outs = [o if len(o) == 5 else (o[0], o[1], o[0], 0, None) for o in row_outs]
    aliased = [(k, o[4]) for k, o in enumerate(row_outs) if o[4] is not None]
    R = rows[0][0].shape[0]
    if max(w for _, w, _ in rows) > 4096:
        tm = tm // 2
    tm = min(tm, R)
    assert R % tm == 0
    nr, npar = len(rows), len(pars)
    nro = len(row_outs)
    n_in = nr + npar + len(deps) + len(aliased)

    def body(*refs):
        rv = [r[...] for r in refs[:nr]]
        pv = [p[...] for p in refs[nr:nr + npar]]
        outs = refs[n_in:]
        ro, ao = fn(rv, pv)
        first = pl.program_id(0) == 0
        for o_ref, val in zip(outs[:nro], ro):
            o_ref[...] = val.astype(o_ref.dtype)

        @pl.when(first)
        def _():
            for o_ref, val in zip(outs[nro:], ao):
                o_ref[...] = val

        @pl.when(jnp.logical_not(first))
        def _():
            for o_ref, val in zip(outs[nro:], ao):
                o_ref[...] += val

    in_specs = ([pl.BlockSpec((tm, w), lambda i, cb=cb: (i, cb)) for _, w, cb in rows]
                + [pl.BlockSpec(p.shape, lambda i, nd=p.ndim: (0,) * nd) for p in list(pars) + list(deps)]
                + [pl.BlockSpec(memory_space=pl.ANY)] * len(aliased))
    out_shape = ([jax.ShapeDtypeStruct((R, full), dt) for _, dt, full, _, _ in row_outs]
                 + [jax.ShapeDtypeStruct(s, F32) for s in acc_outs])
    out_specs = ([pl.BlockSpec((tm, f), lambda i, cb=cb: (i, cb)) for f, _, _, cb, _ in row_outs]
                 + [pl.BlockSpec(s, lambda i, nd=len(s): (0,) * nd) for s in acc_outs])
    res = pl.pallas_call(body, name=name, grid=(R // tm,), in_specs=in_specs, out_specs=out_specs, out_shape=out_shape,
                         input_output_aliases={n_in - len(aliased) + q: k for q, (k, _) in enumerate(aliased)},
                         compiler_params=_params())(*[r for r, _, _ in rows], *pars, *deps, *[buf for _, buf in aliased])
    return list(res)


def _bdot(a, b, mode="nn"):
    dn = {"nn": (((1,), (0,)), ((), ())), "nt": (((1,), (1,)), ((), ())), "tn": (((0,), (0,)), ((), ()))}[mode]
    return lax.dot_general(a.astype(BF16), b.astype(BF16), dn, preferred_element_type=F32)


def _sigmoid(x):
    return jax.nn.sigmoid(x)


def _softplus(x):
    return jnp.maximum(x, 0.0) + jnp.log1p(jnp.exp(-jnp.abs(x)))


def _gelu(z):
    return 0.5 * z * (1.0 + lax.erf(z * _SQRT_HALF))


def _gelu_grad(z):
    return 0.5 * (1.0 + lax.erf(z * _SQRT_HALF)) + z * jnp.exp(-0.5 * z * z) * _INV_SQRT_2PI


def _mean(x):
    return jnp.mean(x, axis=-1, keepdims=True)


def _colsum(x):
    return jnp.sum(x, axis=0, keepdims=True)


def _rms_fwd(x, g, name, deps=()):
    def fn(rv, pv):
        (xv,), (gv,) = rv, pv
        r = lax.rsqrt(_mean(xv * xv) + RMS_EPS)
        return [xv * r * gv], []
    return _rowwise(fn, [x], [g], [(x.shape[1], BF16)], [], name=name, deps=deps)[0]


def _rms_bwd(dn, x, dres, g, name, deps=()):
    def fn(rv, pv):
        (dnv, xv, drv), (gv,) = rv, pv
        r = lax.rsqrt(_mean(xv * xv) + RMS_EPS)
        yn = xv * r
        dyg = dnv * gv
        dx = drv + r * (dyg - yn * _mean(dyg * yn))
        return [dx, dx], [_colsum(dnv * yn)]
    D = x.shape[1]
    return _rowwise(fn, [dn, x, dres], [g], [(D, F32), (D, BF16)], [(1, D)], name=name, deps=deps)


def _rwkv_layout(RW, Lw, La, Lg, D):
    widths = [RW, RW, RW, Lw, La, Lg]
    pw = [_ceil_to(w, LANE) for w in widths]
    pw[5] += _ceil_to(sum(pw), 2 * D) - sum(pw)
    offs = [sum(pw[:i]) for i in range(6)]
    return widths, pw, offs, sum(pw)


def _pad_rwkv_cols(a, lay):
    widths, pw, _, _ = lay
    pieces, src = [], 0
    for w, p in zip(widths, pw):
        pieces.append(a[:, src:src + w])
        if p > w:
            pieces.append(jnp.zeros((a.shape[0], p - w), a.dtype))
        src += w
    return jnp.concatenate(pieces, axis=1)


def _unpad_rwkv_cols(a, lay):
    widths, _, offs, _ = lay
    return jnp.concatenate([a[:, o:o + w] for o, w in zip(offs, widths)], axis=1)


def _proj_pieces(lay, D, cs):
    widths, _, offs, rcp = lay
    rc = sum(widths)
    segs = [(sum(widths[:j]), widths[j], offs[j]) for j in range(6)] + [(rc, D, rcp + 2 * D), (rc + D, D, rcp), (rc + 2 * D, D, rcp + D)]
    pieces = []
    for start, width, dst in segs:
        n = start
        while n < start + width:
            d, off = divmod(n, cs)
            take = min(cs - off, start + width - n)
            pieces.append((d, off, dst + n - start, take))
            n += take
    return pieces


def _w_in_to_proj(g, lay, D, name):
    nb, rows, cs = g.shape
    icp = lay[3] + 3 * D
    pieces = _proj_pieces(lay, D, cs)
    tm = _pick(rows, (256, 128, 64, 32, 16))

    def body(i_ref, o_ref):
        o_ref[...] = jnp.zeros_like(o_ref)
        for d, src, dst, w in pieces:
            o_ref[:, dst:dst + w] = i_ref[d, :, src:src + w]

    return pl.pallas_call(
        body, name=name, grid=(rows // tm,), in_specs=[pl.BlockSpec((nb, tm, cs), lambda i: (0, i, 0))],
        out_specs=pl.BlockSpec((tm, icp), lambda i: (i, 0)), out_shape=jax.ShapeDtypeStruct((rows, icp), g.dtype),
        compiler_params=_params())(g)


def _dw_in_from_proj(a, lay, D, cs, name):
    rows, icp = a.shape
    pieces = _proj_pieces(lay, D, cs)
    tm = _pick(rows, (256, 128, 64, 32, 16))

    def body(i_ref, o_ref):
        for d, src, dst, w in pieces:
            o_ref[d, :, src:src + w] = i_ref[:, dst:dst + w]

    return pl.pallas_call(
        body, name=name, grid=(rows // tm,), in_specs=[pl.BlockSpec((tm, icp), lambda i: (i, 0))],
        out_specs=pl.BlockSpec((N_DEV, tm, cs), lambda i: (0, i, 0)), out_shape=jax.ShapeDtypeStruct((N_DEV, rows, cs), a.dtype),
        compiler_params=_params())(a)


def _pad_rows(a, rows):
    return a if a.shape[0] == rows else jnp.concatenate([a, jnp.zeros((rows - a.shape[0], a.shape[1]), a.dtype)], axis=0)


def _token_shift(p, halo, mu, i):
    tm = p.shape[0]
    hid = lax.broadcasted_iota(jnp.int32, (SUBLANE, 1), 0)
    before = jnp.sum(jnp.where(hid == SUBLANE - 1, halo, 0.0), axis=0, keepdims=True)
    before = jnp.where(i == 0, 0.0, before)
    rid = lax.broadcasted_iota(jnp.int32, (tm, 1), 0)
    prev = jnp.where(rid == 0, before, pltpu.roll(p, 1, 0))
    d = prev - p
    return p + d * mu, d


def _rwkv_math(ps, w0, a0, k_k, k_a, wlw, wla, wlg, lay):
    _, pw, offs, _ = lay
    r, k, v, xw, xa, xg = (ps[:, offs[j]:offs[j] + pw[j]] for j in range(6))
    tw = jnp.tanh(xw)
    ww = w0 + _bdot(tw, wlw)
    lw = -jnp.exp(-_softplus(-ww) - 0.5)
    a = _sigmoid(a0 + _bdot(xa, wla))
    sg = _sigmoid(xg)
    g = _bdot(sg, wlg)
    return dict(r=r, k=k, v=v, xa=xa, tw=tw, ww=ww, lw=lw, a=a, sg=sg, g=g, kkp=k * k_k, k2=k * (1.0 + (a - 1.0) * k_a))


def _halo_specs(T, tm, width, after):
    hb = tm // SUBLANE
    last = T // SUBLANE - 1
    if after:
        return pl.BlockSpec((SUBLANE, width), lambda i: (jnp.minimum((i + 1) * hb, last), 0))
    return pl.BlockSpec((SUBLANE, width), lambda i: (jnp.maximum(i * hb - 1, 0), 0))


def _rowsum(x):
    return jnp.sum(x, axis=-1, keepdims=True)


def _kk_math(kkp):
    nrm = jnp.sqrt(_rowsum(kkp * kkp))
    inv = 1.0 / jnp.maximum(nrm, 1e-12)
    return nrm, inv, kkp * inv


def _rwkv_pre(p, mu, small, lora, lay, name):
    T, rcp = p.shape[0], lay[3]
    H = lay[0][0] // HEAD
    tm = min(128, T)

    def body(p_ref, ph_ref, mu_ref, w0_ref, a0_ref, kk_ref, ka_ref, wlw_ref, wla_ref, wlg_ref, r_o, lw_o, k2_o, v_o, aa_o, bb_o, g_o):
        ps, _ = _token_shift(p_ref[...], ph_ref[...], mu_ref[...], pl.program_id(0))
        q = _rwkv_math(ps, w0_ref[...], a0_ref[...], kk_ref[...], ka_ref[...], wlw_ref[...], wla_ref[...], wlg_ref[...], lay)
        for h in range(H):
            sl = slice(h * HEAD, (h + 1) * HEAD)
            for o_ref, key in ((r_o, "r"), (lw_o, "lw"), (k2_o, "k2"), (v_o, "v"), (g_o, "g")):
                o_ref[h] = q[key][:, sl]
            _, _, kk = _kk_math(q["kkp"][:, sl])
            aa_o[h] = -kk
            bb_o[h] = kk * q["a"][:, sl]

    whole = lambda arr: pl.BlockSpec(arr.shape, lambda i: (0, 0))
    return pl.pallas_call(
        body, name=name, grid=(T // tm,),
        in_specs=([pl.BlockSpec((tm, rcp), lambda i: (i, 0)), _halo_specs(T, tm, rcp, False), whole(mu)]
                  + [whole(s) for s in small] + [whole(w) for w in lora]),
        out_specs=[pl.BlockSpec((H, tm, HEAD), lambda i: (0, i, 0))] * 7, out_shape=[jax.ShapeDtypeStruct((H, T, HEAD), F32)] * 7,
        compiler_params=_params())(p, p, mu, *small, *lora)


def _rwkv_pre_bwd(p, mu, small, lora, hgrads, lay, name):
    T, rcp = p.shape[0], lay[3]
    widths, pw, offs, _ = lay
    RW = widths[0]
    H = RW // HEAD
    tm = min(128, T)

    def body(p_ref, ph_ref, mu_ref, w0_ref, a0_ref, kk_ref, ka_ref, wlw_ref, wla_ref, wlg_ref,
             dr1, dr2, dk1, dk2b, dv1, dv2, dlw_h, daa, dbb, dg_h,
             dps_ref, dmu_ref, dw0_ref, da0_ref, dkk_ref, dka_ref, dwlw_ref, dwla_ref, dwlg_ref,
             s_dr, s_dk2, s_dv, s_dlw, s_dkkp, s_da, s_dg):
        i = pl.program_id(0)
        ps, dprev = _token_shift(p_ref[...], ph_ref[...], mu_ref[...], i)
        k_k, k_a = kk_ref[...], ka_ref[...]
        q = _rwkv_math(ps, w0_ref[...], a0_ref[...], k_k, k_a, wlw_ref[...], wla_ref[...], wlg_ref[...], lay)
        k, a, lw, ww, tw, sg = q["k"], q["a"], q["lw"], q["ww"], q["tw"], q["sg"]
        for h in range(H):
            sl = slice(h * HEAD, (h + 1) * HEAD)
            s_dr[:, sl] = dr1[h] + dr2[h]
            s_dk2[:, sl] = dk1[h] + dk2b[h]
            s_dv[:, sl] = dv1[h] + dv2[h]
            s_dlw[:, sl] = dlw_h[h]
            s_dg[:, sl] = dg_h[h]
            nrm, inv, kk = _kk_math(q["kkp"][:, sl])
            dbb_h = dbb[h]
            dkk = dbb_h * a[:, sl] - daa[h]
            s_dkkp[:, sl] = jnp.where(nrm > 1e-12, inv * (dkk - kk * _rowsum(dkk * kk)), dkk * inv)
            s_da[:, sl] = dbb_h * kk
        dk2, dkkp, dg = s_dk2[...], s_dkkp[...], s_dg[...]
        dk = dk2 * (1.0 + (a - 1.0) * k_a) + dkkp * k_k
        da = s_da[...] + dk2 * k * k_a
        dpa = da * a * (1.0 - a)
        dww = s_dlw[...] * lw * _sigmoid(-ww)
        dxa = _bdot(dpa, wla_ref[...], "nt")
        dxw = _bdot(dww, wlw_ref[...], "nt") * (1.0 - tw * tw)
        dxg = _bdot(dg, wlg_ref[...], "nt") * sg * (1.0 - sg)
        segs = (s_dr[...], dk, s_dv[...], dxw, dxa, dxg)
        sums = [dmu_ref, dw0_ref, da0_ref, dkk_ref, dka_ref, dwlw_ref, dwla_ref, dwlg_ref]

        @pl.when(i == 0)
        def _():
            for s in sums:
                s[...] = jnp.zeros_like(s)

        for j, seg in enumerate(segs):
            sl = slice(offs[j], offs[j] + pw[j])
            dps_ref[:, sl] = seg
            dmu_ref[:, sl] += _colsum(seg * dprev[:, sl])
        dw0_ref[...] += _colsum(dww)
        da0_ref[...] += _colsum(dpa)
        dkk_ref[...] += _colsum(dkkp * k)
        dka_ref[...] += _colsum(dk2 * k * (a - 1.0))
        dwlw_ref[...] += _bdot(tw, dww, "tn")
        dwla_ref[...] += _bdot(q["xa"], dpa, "tn")
        dwlg_ref[...] += _bdot(sg, dg, "tn")

    whole = lambda arr: pl.BlockSpec(arr.shape, lambda i: (0, 0))
    row = lambda w: pl.BlockSpec((tm, w), lambda i: (i, 0))
    acc_shapes = [(1, rcp), (1, RW), (1, RW), (1, RW), (1, RW)] + [w.shape for w in lora]
    return pl.pallas_call(
        body, name=name, grid=(T // tm,),
        in_specs=([row(rcp), _halo_specs(T, tm, rcp, False), whole(mu)] + [whole(s) for s in small] + [whole(w) for w in lora]
                  + [pl.BlockSpec((H, tm, HEAD), lambda i: (0, i, 0))] * 10),
        out_specs=[row(rcp)] + [pl.BlockSpec(s, lambda i: (0, 0)) for s in acc_shapes],
        out_shape=[jax.ShapeDtypeStruct((T, rcp), F32)] + [jax.ShapeDtypeStruct(s, F32) for s in acc_shapes],
        scratch_shapes=[pltpu.VMEM((tm, RW), F32)] * 7, compiler_params=_params())(p, p, mu, *small, *lora, *hgrads)


def _shift_bwd(dps, mu, dproj, name):
    T, rcp = dps.shape
    tm = min(256, T)
    nt = T // tm

    def body(d_ref, dh_ref, mu_ref, buf_ref, o_ref):
        i = pl.program_id(0)
        d = d_ref[...]
        hid = lax.broadcasted_iota(jnp.int32, (SUBLANE, 1), 0)
        after = jnp.sum(jnp.where(hid == 0, dh_ref[...], 0.0), axis=0, keepdims=True)
        after = jnp.where(i == nt - 1, 0.0, after)
        rid = lax.broadcasted_iota(jnp.int32, (tm, 1), 0)
        nxt = jnp.where(rid == tm - 1, after, pltpu.roll(d, tm - 1, 0))
        mu_v = mu_ref[...]
        o_ref[...] = (d * (1.0 - mu_v) + nxt * mu_v).astype(BF16)

    row = pl.BlockSpec((tm, rcp), lambda i: (i, 0))
    return pl.pallas_call(
        body, name=name, grid=(nt,),
        in_specs=[row, _halo_specs(T, tm, rcp, True), pl.BlockSpec(mu.shape, lambda i: (0, 0)), pl.BlockSpec(memory_space=pl.ANY)],
        out_specs=row, out_shape=jax.ShapeDtypeStruct(dproj.shape, BF16), input_output_aliases={3: 0},
        compiler_params=_params())(dps, dps, mu, dproj)


def _head_post_math(y, r, k2, v, lg, lb, rk):
    yc = y - _mean(y)
    rstd = lax.rsqrt(_mean(yc * yc) + LNX_EPS)
    yn = yc * rstd
    s = _rowsum(r * k2 * rk)
    return yn, rstd, yn * lg + lb + s * v, s


def _head_post(y, r, k2, v, g, hp, name, deps=()):
    H, T, _ = y.shape
    tm = min(128, T)

    def body(y_ref, r_ref, k_ref, v_ref, g_ref, lg_ref, lb_ref, rk_ref, *rest):
        o_ref = rest[-1]
        _, _, t, _ = _head_post_math(y_ref[...], r_ref[...], k_ref[...], v_ref[...], lg_ref[...], lb_ref[...], rk_ref[...])
        out = (t * g_ref[...]).astype(BF16)
        for h in range(H):
            o_ref[:, h * HEAD:(h + 1) * HEAD] = out[h]

    blk = pl.BlockSpec((H, tm, HEAD), lambda i: (0, i, 0))
    par = pl.BlockSpec((H, 1, HEAD), lambda i: (0, 0, 0))
    return pl.pallas_call(
        body, name=name, grid=(T // tm,),
        in_specs=[blk] * 5 + [par] * 3 + [pl.BlockSpec(d.shape, lambda i, nd=d.ndim: (0,) * nd) for d in deps],
        out_specs=pl.BlockSpec((tm, H * HEAD), lambda i: (i, 0)),
        out_shape=jax.ShapeDtypeStruct((T, H * HEAD), BF16), compiler_params=_params())(y, r, k2, v, g, *hp, *deps)


def _head_post_bwd(dya, y, r, k2, v, g, hp, name, deps=()):
    H, T, _ = y.shape
    tm = min(128, T)
    hsum = lambda t: jnp.sum(t, axis=1, keepdims=True)

    def body(d_ref, y_ref, r_ref, k_ref, v_ref, g_ref, lg_ref, lb_ref, rk_ref, *rest):
        outs, d_s = rest[len(deps):len(deps) + 8], rest[-1]
        for h in range(H):
            d_s[h] = d_ref[:, h * HEAD:(h + 1) * HEAD]
        d_v, r_v, k_v, v_v, lg, rk = d_s[...], r_ref[...], k_ref[...], v_ref[...], lg_ref[...], rk_ref[...]
        yn, rstd, t, s = _head_post_math(y_ref[...], r_v, k_v, v_v, lg, lb_ref[...], rk)
        dyo = d_v * g_ref[...]
        dyn = dyo * lg
        ds = _rowsum(dyo * v_v)
        vals = (rstd * (dyn - _mean(dyn) - yn * _mean(dyn * yn)), ds * k_v * rk, ds * r_v * rk, dyo * s, d_v * t)
        for o_ref, val in zip(outs[:5], vals):
            o_ref[...] = val
        sums = (hsum(dyo * yn), hsum(dyo), hsum(ds * r_v * k_v))
        first = pl.program_id(0) == 0

        @pl.when(first)
        def _():
            for o_ref, val in zip(outs[5:], sums):
                o_ref[...] = val

        @pl.when(jnp.logical_not(first))
        def _():
            for o_ref, val in zip(outs[5:], sums):
                o_ref[...] += val

    blk = pl.BlockSpec((H, tm, HEAD), lambda i: (0, i, 0))
    par = pl.BlockSpec((H, 1, HEAD), lambda i: (0, 0, 0))
    return pl.pallas_call(
        body, name=name, grid=(T // tm,),
        in_specs=([pl.BlockSpec((tm, H * HEAD), lambda i: (i, 0))] + [blk] * 5 + [par] * 3
                  + [pl.BlockSpec(d.shape, lambda i, nd=d.ndim: (0,) * nd) for d in deps]),
        out_specs=[blk] * 5 + [par] * 3,
        out_shape=[jax.ShapeDtypeStruct((H, T, HEAD), F32)] * 5 + [jax.ShapeDtypeStruct((H, 1, HEAD), F32)] * 3,
        scratch_shapes=[pltpu.VMEM((H, tm, HEAD), F32)], compiler_params=_params())(dya, y, r, k2, v, g, *hp, *deps)


def _bmm(x, y, mode):
    dn = {"nn": (((2,), (1,)), ((0,), (0,))), "nt": (((2,), (2,)), ((0,), (0,))), "tn": (((1,), (1,)), ((0,), (0,)))}[mode]
    (xh, xl), (yh, yl) = _split(x), _split(y)
    dot = lambda p, q: lax.dot_general(p, q, dn, preferred_element_type=F32)
    out = dot(xh, yh)
    if yl is not None:
        out = out + dot(xh, yl)
    if xl is not None:
        out = out + dot(xl, yh)
    return out


def _split(x):
    if isinstance(x, tuple):
        return x
    hi = x.astype(BF16)
    return hi, (x - hi.astype(F32)).astype(BF16)


def _exact(x):
    return x.astype(BF16), None


def _round(x):
    return x if isinstance(x, tuple) else (x.astype(BF16), None)


def _rows(*xs):
    if isinstance(xs[0], tuple):
        return tuple(None if any(p is None for p in parts) else jnp.concatenate(parts, axis=1) for parts in zip(*xs))
    return jnp.concatenate(xs, axis=1)


def _wkv_chunk(r, lw, k, v, a, b):
    hb, C, _ = r.shape
    ti = lax.broadcasted_iota(jnp.int32, (C, C), 0)
    si = lax.broadcasted_iota(jnp.int32, (C, C), 1)
    linc, lstr, eye = (ti >= si).astype(F32), (ti > si).astype(F32), (ti == si).astype(F32)
    qmask = jnp.concatenate([jnp.concatenate([lstr, lstr], axis=1), jnp.concatenate([linc, linc], axis=1)], axis=0)
    lincb = _exact(jnp.broadcast_to(linc, (hb, C, C)))
    both = _exact(jnp.broadcast_to(jnp.concatenate([linc, lstr], axis=0), (hb, 2 * C, C)))
    ones = _exact(jnp.ones_like(v))
    lws = _split(lw)
    ci = _bmm(lincb, lws, "nn")
    cC = jnp.sum(lw, axis=1, keepdims=True)
    gi, ge, gn, gr = jnp.exp(ci), jnp.exp(ci - lw), jnp.exp(-ci), jnp.exp(cC - ci)
    q = dict(At=a * ge, Rt=r * gi, Bt=b * gn, Kt=k * gn, Bh=b * gr, Kh=k * gr)
    s = dict(AR=_round(_rows(q["At"], q["Rt"])), BK=_round(_rows(q["Bt"], q["Kt"])), BKh=_round(_rows(q["Bh"], q["Kh"])), v=_round(v))
    quad = _bmm(s["AR"], s["BK"], "nt") * qmask
    s["top"], s["bot"] = _round(quad[:, :C]), _round(quad[:, C:])
    A_ab = quad[:, :C, :C]
    Tm = eye + A_ab
    Pw = _round(A_ab)
    n = 1
    while 2 * n < C:
        Pw = _round(_bmm(Pw, Pw, "nn"))
        Tm = Tm + _bmm(_round(Tm), Pw, "nn")
        n *= 2
    s["Tm"] = _round(Tm)
    gC = jnp.exp(_bmm(lws, ones, "tn"))
    q.update(gi=gi, ge=ge, gn=gn, gr=gr, qmask=qmask, both=both, gC=gC, ones=ones, s=s)
    return q


def _wkv_u(s, H0s, C):
    arh = _bmm(s["AR"], H0s, "nn")
    zv = _rows(tuple(None if p is None else jnp.zeros_like(p) for p in s["v"]), s["v"])
    U = _bmm(s["Tm"], _round(arh[:, :C] + _bmm(s["top"], zv, "nn")), "nn")
    return arh, _rows(_round(U), s["v"])


def _wkv_fwd(r, lw, k, v, a, b, name):
    H, T, N = r.shape
    C = min(WKV_CHUNK, T)
    nc = T // C
    hb = _pick(H, (16, 8, 4, 2))

    def body(r_ref, lw_ref, k_ref, v_ref, a_ref, b_ref, y_ref, st_ref, h_ref):
        @pl.when(pl.program_id(1) == 0)
        def _():
            h_ref[...] = jnp.zeros_like(h_ref)

        H0 = h_ref[...]
        st_ref[0] = H0
        q = _wkv_chunk(r_ref[...], lw_ref[...], k_ref[...], v_ref[...], a_ref[...], b_ref[...])
        s = q["s"]
        arh, UV = _wkv_u(s, _round(H0), C)
        y_ref[...] = arh[:, C:] + _bmm(s["bot"], UV, "nn")
        h_ref[...] = q["gC"] * H0 + _bmm(s["BKh"], UV, "tn")

    blk = pl.BlockSpec((hb, C, N), lambda h, c: (h, c, 0))
    return pl.pallas_call(
        body, name=name, grid=(H // hb, nc), in_specs=[blk] * 6,
        out_specs=[blk, pl.BlockSpec((1, hb, N, N), lambda h, c: (c, h, 0, 0))],
        out_shape=[jax.ShapeDtypeStruct((H, T, N), F32), jax.ShapeDtypeStruct((nc, H, N, N), F32)],
        scratch_shapes=[pltpu.VMEM((hb, N, N), F32)], compiler_params=_params())(r, lw, k, v, a, b)


def _wkv_bwd(r, lw, k, v, a, b, states, dy, name):
    H, T, N = r.shape
    C = min(WKV_CHUNK, T)
    nc = T // C
    hb = _pick(H, (16, 8, 4, 2))

    def body(r_ref, lw_ref, k_ref, v_ref, a_ref, b_ref, st_ref, dy_ref, dr_ref, dlw_ref, dk_ref, dv_ref, da_ref, db_ref, dh_ref):
        @pl.when(pl.program_id(1) == 0)
        def _():
            dh_ref[...] = jnp.zeros_like(dh_ref)

        dHC = dh_ref[...]
        H0 = st_ref[0]
        q = _wkv_chunk(r_ref[...], lw_ref[...], k_ref[...], v_ref[...], a_ref[...], b_ref[...])
        s, gC = q["s"], q["gC"]
        H0s, dHs, dY = _round(H0), _round(dHC), _round(dy_ref[...])
        _, UV = _wkv_u(s, H0s, C)
        bot_dy = _bmm(s["bot"], dY, "tn")
        bkh_dh = _bmm(s["BKh"], dHs, "nn")
        dP = _round(_bmm(s["Tm"], _round(bot_dy[:, :C] + bkh_dh[:, :C]), "tn"))
        dv_ref[...] = bot_dy[:, C:] + bkh_dh[:, C:] + _bmm(s["top"], dP, "tn")[:, C:]
        dPY = _rows(dP, dY)
        dh_ref[...] = gC * dHC + _bmm(s["AR"], dPY, "tn")
        dquad = _round(_bmm(dPY, UV, "nt") * q["qmask"])
        dAR = _bmm(dPY, H0s, "nt") + _bmm(dquad, s["BK"], "nn")
        dBK = _bmm(dquad, s["AR"], "tn")
        dBKh = _bmm(UV, dHs, "nt")
        dAt, dRt, dBt, dKt, dBh, dKh = dAR[:, :C], dAR[:, C:], dBK[:, :C], dBK[:, C:], dBKh[:, :C], dBKh[:, C:]
        dr_ref[...] = dRt * q["gi"]
        da_ref[...] = dAt * q["ge"]
        db_ref[...] = dBt * q["gn"] + dBh * q["gr"]
        dk_ref[...] = dKt * q["gn"] + dKh * q["gr"]
        tail = dBh * q["Bh"] + dKh * q["Kh"]
        dci = dRt * q["Rt"] - dBt * q["Bt"] - dKt * q["Kt"] - tail
        dcC = jnp.sum(tail, axis=1, keepdims=True) + _bmm(q["ones"], H0 * dHC * gC, "nt")
        dlw_ref[...] = _bmm(q["both"], _rows(dci, dAt * q["At"]), "tn") + dcC

    blk = pl.BlockSpec((hb, C, N), lambda h, c: (h, nc - 1 - c, 0))
    st = pl.BlockSpec((1, hb, N, N), lambda h, c: (nc - 1 - c, h, 0, 0))
    return pl.pallas_call(
        body, name=name, grid=(H // hb, nc), in_specs=[blk] * 6 + [st, blk], out_specs=[blk] * 6,
        out_shape=[jax.ShapeDtypeStruct((H, T, N), F32)] * 6,
        scratch_shapes=[pltpu.VMEM((hb, N, N), F32)], compiler_params=_params())(r, lw, k, v, a, b, states, dy)


def _sgu_ln(z, SW, lng, lnb):
    ge = _gelu(z)
    u, vv = ge[:, :SW], ge[:, SW:]
    xc = vv - _mean(vv)
    rstd = lax.rsqrt(_mean(xc * xc) + LN_EPS)
    vn = xc * rstd
    return u, vn, rstd, vn * lng + lnb


def _causal(ws_ref, g):
    ti = lax.broadcasted_iota(jnp.int32, (SGU_CHUNK, SGU_CHUNK), 0)
    si = lax.broadcasted_iota(jnp.int32, (SGU_CHUNK, SGU_CHUNK), 1)
    return ti >= si, jnp.where(ti >= si, ws_ref[g], 0.0).astype(BF16)


def _sgu_fwd(proj, zblock, lng, lnb, ws, bexp, name):
    T, SW = proj.shape[0], lng.shape[1]
    G = ws.shape[0]
    tr = min(256, T)
    nch = tr // SGU_CHUNK

    def body(z_ref, lng_ref, lnb_ref, ws_ref, be_ref, o_ref):
        u, _, _, vl = _sgu_ln(z_ref[...], SW, lng_ref[...], lnb_ref[...])
        for g in range(G):
            cs = slice(g * SGU_GROUP, (g + 1) * SGU_GROUP)
            _, wc = _causal(ws_ref, g)
            for n in range(nch):
                rs = slice(n * SGU_CHUNK, (n + 1) * SGU_CHUNK)
                m = jnp.dot(wc, vl[rs, cs].astype(BF16), preferred_element_type=F32) + be_ref[:, cs]
                o_ref[rs, cs] = (u[rs, cs] * m).astype(BF16)

    whole = lambda arr: pl.BlockSpec(arr.shape, lambda i, nd=arr.ndim: (0,) * nd)
    return pl.pallas_call(
        body, name=name, grid=(T // tr,),
        in_specs=[pl.BlockSpec((tr, 2 * SW), lambda i: (i, zblock)), whole(lng), whole(lnb), whole(ws), whole(bexp)],
        out_specs=pl.BlockSpec((tr, SW), lambda i: (i, 0)), out_shape=jax.ShapeDtypeStruct((T, SW), BF16),
        compiler_params=_params())(proj, lng, lnb, ws, bexp)


def _sgu_bwd(proj, zblock, dyb, lng, lnb, ws, bexp, dproj, name):
    T, SW = proj.shape[0], lng.shape[1]
    G = ws.shape[0]
    tr = min(256, T)
    nch = tr // SGU_CHUNK
    nt = T // tr

    def body(z_ref, dy_ref, lng_ref, lnb_ref, ws_ref, be_ref, buf_ref, dz_ref, dlg_ref, dlb_ref, dws_ref, db_ref, du_s, dvl_s, dbacc_s):
        i = pl.program_id(0)
        zv = z_ref[...]
        lng_v = lng_ref[...]
        u, vn, rstd, vl = _sgu_ln(zv, SW, lng_v, lnb_ref[...])

        @pl.when(i == 0)
        def _():
            for s in (dlg_ref, dlb_ref, dws_ref, dbacc_s):
                s[...] = jnp.zeros_like(s)

        for g in range(G):
            cs = slice(g * SGU_GROUP, (g + 1) * SGU_GROUP)
            tri, wc = _causal(ws_ref, g)
            for n in range(nch):
                rs = slice(n * SGU_CHUNK, (n + 1) * SGU_CHUNK)
                blk = vl[rs, cs].astype(BF16)
                m = jnp.dot(wc, blk, preferred_element_type=F32) + be_ref[:, cs]
                dyv = dy_ref[rs, cs]
                du_s[rs, cs] = dyv * m
                dm = dyv * u[rs, cs]
                dvl_s[rs, cs] = _bdot(wc, dm, "tn")
                dws_ref[g] += jnp.where(tri, _bdot(dm, blk, "nt"), 0.0)
                dbacc_s[:, cs] += dm

        dvl = dvl_s[...]
        dlg_ref[...] += _colsum(dvl * vn)
        dlb_ref[...] += _colsum(dvl)
        dvn = dvl * lng_v
        dvv = rstd * (dvn - _mean(dvn) - vn * _mean(dvn * vn))
        gp = _gelu_grad(zv)
        dz_ref[:, :SW] = (du_s[...] * gp[:, :SW]).astype(BF16)
        dz_ref[:, SW:] = (dvv * gp[:, SW:]).astype(BF16)

        @pl.when(i == nt - 1)
        def _():
            lane = lax.broadcasted_iota(jnp.int32, (SGU_CHUNK, LANE), 1)
            out = jnp.zeros((SGU_CHUNK, LANE), F32)
            for g in range(G):
                col = jnp.sum(dbacc_s[:, g * SGU_GROUP:(g + 1) * SGU_GROUP], axis=1, keepdims=True)
                out = jnp.where(lane == g, col, out)
            db_ref[...] = out

    whole = lambda arr: pl.BlockSpec(arr.shape, lambda i, nd=arr.ndim: (0,) * nd)
    acc_shapes = [(1, SW), (1, SW), ws.shape, (SGU_CHUNK, LANE)]
    return pl.pallas_call(
        body, name=name, grid=(nt,),
        in_specs=[pl.BlockSpec((tr, 2 * SW), lambda i: (i, zblock)), pl.BlockSpec((tr, SW), lambda i: (i, 0)),
                  whole(lng), whole(lnb), whole(ws), whole(bexp), pl.BlockSpec(memory_space=pl.ANY)],
        out_specs=([pl.BlockSpec((tr, 2 * SW), lambda i: (i, zblock))]
                   + [pl.BlockSpec(s, lambda i, nd=len(s): (0,) * nd) for s in acc_shapes]),
        out_shape=[jax.ShapeDtypeStruct(dproj.shape, BF16)] + [jax.ShapeDtypeStruct(s, F32) for s in acc_shapes],
        scratch_shapes=[pltpu.VMEM((tr, SW), F32), pltpu.VMEM((tr, SW), F32), pltpu.VMEM((SGU_CHUNK, SW), F32)],
        input_output_aliases={6: 0}, compiler_params=_params())(proj, dyb, lng, lnb, ws, bexp, dproj)


_HBM = pl.BlockSpec(memory_space=pltpu.HBM)
_SEM = pl.BlockSpec(memory_space=pltpu.SEMAPHORE)
_DATAFLOW = pltpu.SideEffectType.DATAFLOW_SIDE_EFFECTING


def _mesh_place(chips=False):
    x, y, c = lax.axis_index("x"), lax.axis_index("y"), lax.axis_index("c")
    return x, y, c, (2 * x + y if chips else 4 * x + 2 * y + c)


def _peer(x, y, c, rel, chips=False):
    px = 1 - x if rel & 4 else x
    py = 1 - y if rel & 2 else y
    pc = 1 - c if rel & 1 else c
    return (px, py, pc), (2 * px + py if chips else 4 * px + 2 * py + pc)


ALL_PEERS = tuple(range(1, N_DEV))
SIBLING = (1,)
SAME_CORE = (2, 4, 6)
SIBLINGS_CORE = (3, 5, 7)


def _exchange_start(groups, name, rels=ALL_PEERS, chips=False):
    flat = [t for g in groups for t in g]
    sizes = [len(g) for g in groups]
    n, ng = len(flat), len(groups)
    srcs = [pltpu.with_memory_space_constraint(a, pltpu.HBM) for a, _ in flat]
    lands = [pltpu.with_memory_space_constraint(lax.empty(((N_DEV,) + a.shape) if isg else a.shape, a.dtype), pltpu.HBM)
             for a, isg in flat]

    def body(*refs):
        ins, lnd, sems, token = refs[:n], refs[n:2 * n], refs[2 * n:2 * n + 3 * ng], refs[-1]
        x, y, c, me = _mesh_place(chips)
        j0 = 0
        for gi, sz in enumerate(sizes):
            for rel in rels:
                dev, slot = _peer(x, y, c, rel, chips)
                for jj in range(sz):
                    j = j0 + jj
                    pltpu.make_async_remote_copy(
                        src_ref=ins[j] if flat[j][1] else ins[j].at[slot], dst_ref=lnd[j].at[me],
                        send_sem=sems[3 * gi].at[jj * (N_DEV - 1) + rel - 1], recv_sem=sems[3 * gi + 1].at[jj * (N_DEV - 1) + rel - 1],
                        device_id=dev, device_id_type=pl.DeviceIdType.MESH).start()
            for jj in range(sz):
                j = j0 + jj
                pltpu.make_async_copy(ins[j] if flat[j][1] else ins[j].at[me], lnd[j].at[me], sems[3 * gi + 2].at[jj]).start()
            j0 += sz
        token[...] = jnp.zeros_like(token)

    sem_shapes = [pltpu.SemaphoreType.DMA((k,)) for sz in sizes for k in (sz * (N_DEV - 1), sz * (N_DEV - 1), sz)]
    res = pl.pallas_call(
        body, name=name,
        out_shape=(*sem_shapes, *[pltpu.HBM(a.shape, a.dtype) for a in srcs], *[pltpu.HBM(a.shape, a.dtype) for a in lands],
                   jax.ShapeDtypeStruct((SUBLANE, LANE), F32)),
        in_specs=[_HBM] * (2 * n), out_specs=(*[_SEM] * (3 * ng), *[_HBM] * (2 * n), pl.BlockSpec(memory_space=pltpu.VMEM)),
        input_output_aliases={i: 3 * ng + i for i in range(2 * n)},
        compiler_params=pltpu.CompilerParams(has_side_effects=_DATAFLOW))(*srcs, *lands)
    sems, thru, token = res[:3 * ng], res[3 * ng:3 * ng + 2 * n], res[-1]
    handle, j0 = [], 0
    for gi, sz in enumerate(sizes):
        handle.append(dict(kinds=[k for _, k in groups[gi]], chips=chips, srcs=list(thru[j0:j0 + sz]), lands=list(thru[n + j0:n + j0 + sz]),
                           sems=list(sems[3 * gi:3 * gi + 3])))
        j0 += sz
    return handle, token


def _exchange_wait(group, after, name, rels=ALL_PEERS, local=True):
    kinds, sz = group["kinds"], len(group["kinds"])
    relay = group.get("relay", [])

    def body(*refs):
        ins, lnd, (ssem, rsem, lsem) = refs[:sz], refs[sz:2 * sz], refs[2 * sz:2 * sz + 3]
        x, y, c, me = _mesh_place(group["chips"])
        for rel in rels:
            dev, slot = _peer(x, y, c, rel, group["chips"])
            for jj in range(sz):
                cp = pltpu.make_async_remote_copy(
                    src_ref=ins[jj] if kinds[jj] else ins[jj].at[slot], dst_ref=lnd[jj].at[slot],
                    send_sem=ssem.at[jj * (N_DEV - 1) + rel - 1], recv_sem=rsem.at[jj * (N_DEV - 1) + rel - 1],
                    device_id=dev, device_id_type=pl.DeviceIdType.MESH)
                cp.wait_send()
                cp.wait_recv()
        if local:
            for jj in range(sz):
                pltpu.make_async_copy(ins[jj] if kinds[jj] else ins[jj].at[me], lnd[jj].at[me], lsem.at[jj]).wait()
        if relay:
            fsend, frecv = refs[2 * sz + 3:2 * sz + 5]
            dev = _peer(x, y, c, 1)[0]
            for q, (mine, theirs) in enumerate(zip(SAME_CORE, SIBLINGS_CORE)):
                for jj in range(sz):
                    cp = pltpu.make_async_remote_copy(
                        src_ref=lnd[jj].at[_peer(x, y, c, mine)[1]], dst_ref=lnd[jj].at[_peer(x, y, c, theirs)[1]],
                        send_sem=fsend.at[jj * len(SAME_CORE) + q], recv_sem=frecv.at[jj * len(SAME_CORE) + q],
                        device_id=dev, device_id_type=pl.DeviceIdType.MESH)
                    cp.wait_send()
                    cp.wait_recv()

    arrays = group["srcs"] + group["lands"]
    sems = group["sems"] + relay
    res = pl.pallas_call(
        body, name=name, out_shape=[pltpu.HBM(a.shape, a.dtype) for a in arrays],
        in_specs=[_HBM] * (2 * sz) + [_SEM] * len(sems) + [pl.BlockSpec(memory_space=pl.ANY)], out_specs=[_HBM] * (2 * sz),
        input_output_aliases={i: i for i in range(2 * sz)},
        compiler_params=pltpu.CompilerParams(has_side_effects=_DATAFLOW))(*arrays, *sems, after)
    return dict(group, srcs=list(res[:sz]), lands=list(res[sz:]), relay=[])


def _relay_start(group, name):
    sz = len(group["kinds"])
    nq = len(SAME_CORE)

    def body(*refs):
        lnd, fsend, frecv, token = refs[:sz], refs[sz], refs[sz + 1], refs[-1]
        x, y, c, _ = _mesh_place()
        dev = _peer(x, y, c, 1)[0]
        for q, rel in enumerate(SAME_CORE):
            slot = _peer(x, y, c, rel)[1]
            for jj in range(sz):
                pltpu.make_async_remote_copy(
                    src_ref=lnd[jj].at[slot], dst_ref=lnd[jj].at[slot], send_sem=fsend.at[jj * nq + q], recv_sem=frecv.at[jj * nq + q],
                    device_id=dev, device_id_type=pl.DeviceIdType.MESH).start()
        token[...] = jnp.zeros_like(token)

    lands = group["lands"]
    res = pl.pallas_call(
        body, name=name,
        out_shape=(pltpu.SemaphoreType.DMA((sz * nq,)), pltpu.SemaphoreType.DMA((sz * nq,)), *[pltpu.HBM(a.shape, a.dtype) for a in lands],
                   jax.ShapeDtypeStruct((SUBLANE, LANE), F32)),
        in_specs=[_HBM] * sz, out_specs=(_SEM, _SEM, *[_HBM] * sz, pl.BlockSpec(memory_space=pltpu.VMEM)),
        input_output_aliases={i: 2 + i for i in range(sz)},
        compiler_params=pltpu.CompilerParams(has_side_effects=_DATAFLOW))(*lands)
    return dict(group, lands=list(res[2:2 + sz]), relay=[res[0], res[1]]), res[-1]


def _sibling_swap(arrays, handle, after, name):
    start = handle is None
    n = len(arrays) if start else len(handle["srcs"])
    chips = N_DEV // 2
    if start:
        srcs = [pltpu.with_memory_space_constraint(a.reshape(chips, 2, *a.shape[1:]), pltpu.HBM) for a in arrays]
        lands = [pltpu.with_memory_space_constraint(lax.empty((chips,) + a.shape[1:], a.dtype), pltpu.HBM) for a in arrays]
    else:
        srcs, lands = handle["srcs"], handle["lands"]

    def body(*refs):
        ins, lnd, ssem, rsem = refs[:n], refs[n:2 * n], refs[2 * n], refs[2 * n + 1]
        x, y, c, _ = _mesh_place()
        dev = _peer(x, y, c, 1)[0]
        for q in range(chips):
            for j in range(n):
                cp = pltpu.make_async_remote_copy(
                    src_ref=ins[j].at[q, 1 - c], dst_ref=lnd[j].at[q], send_sem=ssem.at[j * chips + q], recv_sem=rsem.at[j * chips + q],
                    device_id=dev, device_id_type=pl.DeviceIdType.MESH)
                if start:
                    cp.start()
                else:
                    cp.wait_send()
                    cp.wait_recv()
        if start:
            refs[-1][...] = jnp.zeros_like(refs[-1])

    thru = [pltpu.HBM(a.shape, a.dtype) for a in srcs + lands]
    effect = pltpu.CompilerParams(has_side_effects=_DATAFLOW)
    if start:
        res = pl.pallas_call(
            body, name=name, out_shape=(pltpu.SemaphoreType.DMA((n * chips,)), pltpu.SemaphoreType.DMA((n * chips,)), *thru,
                                        jax.ShapeDtypeStruct((SUBLANE, LANE), F32)),
            in_specs=[_HBM] * (2 * n), out_specs=(_SEM, _SEM, *[_HBM] * (2 * n), pl.BlockSpec(memory_space=pltpu.VMEM)),
            input_output_aliases={i: 2 + i for i in range(2 * n)}, compiler_params=effect)(*srcs, *lands)
        return dict(srcs=list(res[2:2 + n]), lands=list(res[2 + n:2 + 2 * n]), sems=[res[0], res[1]]), res[-1]
    res = pl.pallas_call(
        body, name=name, out_shape=thru, in_specs=[_HBM] * (2 * n) + [_SEM, _SEM, pl.BlockSpec(memory_space=pl.ANY)],
        out_specs=[_HBM] * (2 * n), input_output_aliases={i: i for i in range(2 * n)}, compiler_params=effect)(
            *srcs, *lands, *handle["sems"], after)
    return dict(handle, srcs=list(res[:n]), lands=list(res[n:]))


def _pair_add(mine, theirs, core, name):
    chips, _, rows, w = mine.shape
    tm = _pick(rows, (256, 128, 64, 32, 16))

    def body(core_ref, a_ref, b_ref, o_ref):
        o_ref[...] = (a_ref[...].astype(F32) + b_ref[...].astype(F32)).astype(o_ref.dtype)

    return pl.pallas_call(
        body, name=name, out_shape=jax.ShapeDtypeStruct(theirs.shape, theirs.dtype),
        grid_spec=pltpu.PrefetchScalarGridSpec(
            num_scalar_prefetch=1, grid=(chips, rows // tm),
            in_specs=[pl.BlockSpec((None, None, tm, w), lambda q, i, core_ref: (q, core_ref[0], i, 0)),
                      pl.BlockSpec((None, tm, w), lambda q, i, core_ref: (q, i, 0))],
            out_specs=pl.BlockSpec((None, tm, w), lambda q, i, core_ref: (q, i, 0))),
        compiler_params=_params())(core, mine, theirs)


def _adamw(w, m, v, gparts, name, after=None):
    R, C = w.shape
    tm = _pick(R, (256, 128, 64, 32, 16, 8))
    order = [] if after is None else [after]

    def body(w_ref, m_ref, v_ref, g_ref, *rest):
        go, do, mo, vo = rest[len(order):]
        g = g_ref[0].astype(F32)
        for j in range(1, gparts.shape[0]):
            g = g + g_ref[j].astype(F32)
        mn = ADAM_B1 * m_ref[...] + (1.0 - ADAM_B1) * g
        vn = ADAM_B2 * v_ref[...] + (1.0 - ADAM_B2) * (g * g)
        m_hat = mn / (1.0 - ADAM_B1 ** ADAM_STEP)
        v_hat = vn / (1.0 - ADAM_B2 ** ADAM_STEP)
        go[...] = g
        do[...] = -ADAM_LR * (m_hat / (jnp.sqrt(v_hat) + ADAM_EPS) + ADAM_WD * w_ref[...])
        mo[...] = mn
        vo[...] = vn

    row = pl.BlockSpec((tm, C), lambda i: (i, 0))
    return pl.pallas_call(
        body, name=name, grid=(R // tm,),
        in_specs=[row, row, row, pl.BlockSpec((gparts.shape[0], tm, C), lambda i: (0, i, 0))] + [pl.BlockSpec(memory_space=pl.ANY)] * len(order),
        out_specs=[row] * 4, out_shape=[jax.ShapeDtypeStruct((R, C), F32)] * 4, compiler_params=_params())(w, m, v, gparts, *order)


def _pack(arrays):
    parts = []
    for a in arrays:
        f = a.reshape(1, -1)
        pad = _ceil_to(f.shape[1], SUBLANE * LANE) - f.shape[1]
        f = jnp.concatenate([f, jnp.zeros((1, pad), f.dtype)], axis=1) if pad else f
        parts.append(f.reshape(-1, LANE))
    rows = sum(p.shape[0] for p in parts)
    pad = _ceil_to(rows, 64) - rows
    return jnp.concatenate(parts + ([jnp.zeros((pad, LANE), parts[0].dtype)] if pad else []), axis=0)


def _unpack(buf, shapes):
    out, row = [], 0
    for s in shapes:
        size = 1
        for d in s:
            size *= d
        rows = _ceil_to(size, SUBLANE * LANE) // LANE
        out.append(buf[row:row + rows].reshape(1, -1)[:, :size].reshape(s))
        row += rows
    return out


def kernel(x, norm_mix_g, w_in, shift_mu, w0, w_lora_up, a0, a_lora_up, g_lora_up, k_k, k_a, r_k, lnx_g, lnx_b, w_proj_rwkv, sgu_ln_g, sgu_ln_b, sgu_w, sgu_b, w_proj_sgu, w_out, norm_ffn_g, w_ffn_gate, w_ffn_up, w_ffn_down, norm_final_g, loss_target, m_norm_mix_g, m_w_in, m_shift_mu, m_w0, m_w_lora_up, m_a0, m_a_lora_up, m_g_lora_up, m_k_k, m_k_a, m_r_k, m_lnx_g, m_lnx_b, m_w_proj_rwkv, m_sgu_ln_g, m_sgu_ln_b, m_sgu_w, m_sgu_b, m_w_proj_sgu, m_w_out, m_norm_ffn_g, m_w_ffn_gate, m_w_ffn_up, m_w_ffn_down, m_norm_final_g, v_norm_mix_g, v_w_in, v_shift_mu, v_w0, v_w_lora_up, v_a0, v_a_lora_up, v_g_lora_up, v_k_k, v_k_a, v_r_k, v_lnx_g, v_lnx_b, v_w_proj_rwkv, v_sgu_ln_g, v_sgu_ln_b, v_sgu_w, v_sgu_b, v_w_proj_sgu, v_w_out, v_norm_ffn_g, v_w_ffn_gate, v_w_ffn_up, v_w_ffn_down, v_norm_final_g):
    weights = dict(norm_mix_g=norm_mix_g, w_in=w_in, shift_mu=shift_mu, w0=w0, w_lora_up=w_lora_up, a0=a0, a_lora_up=a_lora_up,
                   g_lora_up=g_lora_up, k_k=k_k, k_a=k_a, r_k=r_k, lnx_g=lnx_g, lnx_b=lnx_b, w_proj_rwkv=w_proj_rwkv,
                   sgu_ln_g=sgu_ln_g, sgu_ln_b=sgu_ln_b, sgu_w=sgu_w, sgu_b=sgu_b, w_proj_sgu=w_proj_sgu, w_out=w_out,
                   norm_ffn_g=norm_ffn_g, w_ffn_gate=w_ffn_gate, w_ffn_up=w_ffn_up, w_ffn_down=w_ffn_down, norm_final_g=norm_final_g)
    m_in = dict(norm_mix_g=m_norm_mix_g, w_in=m_w_in, shift_mu=m_shift_mu, w0=m_w0, w_lora_up=m_w_lora_up, a0=m_a0,
                a_lora_up=m_a_lora_up, g_lora_up=m_g_lora_up, k_k=m_k_k, k_a=m_k_a, r_k=m_r_k, lnx_g=m_lnx_g, lnx_b=m_lnx_b,
                w_proj_rwkv=m_w_proj_rwkv, sgu_ln_g=m_sgu_ln_g, sgu_ln_b=m_sgu_ln_b, sgu_w=m_sgu_w, sgu_b=m_sgu_b,
                w_proj_sgu=m_w_proj_sgu, w_out=m_w_out, norm_ffn_g=m_norm_ffn_g, w_ffn_gate=m_w_ffn_gate, w_ffn_up=m_w_ffn_up,
                w_ffn_down=m_w_ffn_down, norm_final_g=m_norm_final_g)
    v_in = dict(norm_mix_g=v_norm_mix_g, w_in=v_w_in, shift_mu=v_shift_mu, w0=v_w0, w_lora_up=v_w_lora_up, a0=v_a0,
                a_lora_up=v_a_lora_up, g_lora_up=v_g_lora_up, k_k=v_k_k, k_a=v_k_a, r_k=v_r_k, lnx_g=v_lnx_g, lnx_b=v_lnx_b,
                w_proj_rwkv=v_w_proj_rwkv, sgu_ln_g=v_sgu_ln_g, sgu_ln_b=v_sgu_ln_b, sgu_w=v_sgu_w, sgu_b=v_sgu_b,
                w_proj_sgu=v_w_proj_sgu, w_out=v_w_out, norm_ffn_g=v_norm_ffn_g, w_ffn_gate=v_w_ffn_gate, w_ffn_up=v_w_ffn_up,
                w_ffn_down=v_w_ffn_down, norm_final_g=v_norm_final_g)
    names = list(weights)
    col_sharded = ("w_in", "w_lora_up", "a_lora_up", "g_lora_up", "w_proj_rwkv", "w_proj_sgu", "w_ffn_gate", "w_ffn_up")
    row_sharded = ("w_out", "w_ffn_down")
    sharded = [n for n in names if n in col_sharded or n in row_sharded]
    small = [n for n in names if n not in sharded]

    xs, tgt = x[0], loss_target[0]
    T, D = xs.shape
    RW = w0.shape[1]
    H = RW // HEAD
    SW = sgu_ln_g.shape[1]
    G = sgu_w.shape[1]
    assert 2 * SW == D, "the projection layout takes the SGU part to be as wide as a gate"
    lay = _rwkv_layout(RW, w_lora_up.shape[1], a_lora_up.shape[1], g_lora_up.shape[1], D)
    _, pw, _, rcp = lay
    icp = rcp + 3 * D
    b_ga, b_gb, b_z = rcp // D, rcp // D + 1, rcp // D + 2

    gather_groups = [["w_in", "w_lora_up", "a_lora_up", "g_lora_up"], ["w_proj_rwkv", "w_proj_sgu", "w_out"],
                     ["w_ffn_gate"], ["w_ffn_up"], ["w_ffn_down"]]
    gather, gather_token = _exchange_start([[(weights[n][0].astype(BF16), True) for n in grp] for grp in gather_groups],
                                           "gather_start", rels=SIBLING + SAME_CORE)
    full = {}
    relay_tokens = {}
    joined = lambda g: g.transpose(1, 0, 2).reshape(g.shape[1], -1)

    def relay_weights(gi, after, name):
        arrived = _exchange_wait(gather[gi], after, "gather_wait_ici_" + name, rels=SAME_CORE, local=False)
        gather[gi], relay_tokens[gi] = _relay_start(arrived, "gather_relay_" + name)

    def take_weights(gi, after, name):
        done = _exchange_wait(gather[gi], after, "gather_wait_d2d_" + name, rels=SIBLING)
        for n, g in zip(gather_groups[gi], done["lands"]):
            full[n] = g.reshape(-1, g.shape[2]) if n in row_sharded else g

    n1 = _rms_fwd(xs, norm_mix_g, "rms_mix", deps=[gather_token])
    relay_weights(0, n1, "in")
    take_weights(0, relay_tokens[0], "in")
    W_in = _w_in_to_proj(full["w_in"], lay, D, "w_in_layout")
    lora = [_pad_rows(joined(full[n]), rows) for n, rows in zip(("w_lora_up", "a_lora_up", "g_lora_up"), pw[3:])]
    mu_p = _pad_rwkv_cols(shift_mu, lay)
    rsmall = [w0, a0, k_k, k_a]
    hp = [lnx_g.reshape(H, 1, HEAD), lnx_b.reshape(H, 1, HEAD), r_k.reshape(H, 1, HEAD)]
    ws = sgu_w[0]
    bexp = jnp.repeat(sgu_b[0].T, SGU_GROUP, axis=1)
    gf = norm_final_g.reshape(1, D)

    proj = _matmul(n1, W_in, mode="nn", out_dtype=F32, name="proj_in")
    ga, gb = (proj, D, b_ga), (proj, D, b_gb)
    r_h, lw_h, k2_h, v_h, aa_h, bb_h, g_h = _rwkv_pre(proj, mu_p, rsmall, lora, lay, "rwkv_pre")
    wkv_in = [r_h, lw_h, k2_h, v_h, aa_h, bb_h]
    y_h, states = _wkv_fwd(*wkv_in, "wkv_fwd")
    relay_weights(1, y_h, "proj")
    relay_weights(2, relay_tokens[1], "ffn_gate")
    ya = _head_post(y_h, r_h, k2_h, v_h, g_h, hp, "head_post", deps=[relay_tokens[2]])
    relay_weights(3, ya, "ffn_up")
    yb = _sgu_fwd(proj, b_z, sgu_ln_g, sgu_ln_b, ws, bexp, "sgu_fwd")
    take_weights(1, ya, "proj")
    pa = _matmul(ya, full["w_proj_rwkv"], mode="nn", out_dtype=F32, name="proj_a", deps=[relay_tokens[3]])

    def merge_fn(pb_v, pa_v, ga_v, gb_v):
        return pb_v, _sigmoid(ga_v) * pa_v + _sigmoid(gb_v) * pb_v
    pb, merged = _matmul(yb, full["w_proj_sgu"], mode="nn", name="proj_b_merge",
                         epi=(merge_fn, [pa, (proj, b_ga * D), (proj, b_gb * D)], [F32, BF16]))
    h1 = _matmul(merged, full["w_out"], mode="nn", out_dtype=F32, name="out_proj", add=xs)
    n2 = _rms_fwd(h1, norm_ffn_g, "rms_ffn")
    relay_weights(4, n2, "ffn_down")
    take_weights(2, n2, "ffn_gate")
    gt = _matmul(n2, full["w_ffn_gate"], mode="nn", out_dtype=F32, name="ffn_gate", out_blocks=N_DEV, deps=[relay_tokens[4]])
    take_weights(3, gt, "ffn_up")

    def act_fn(up_v, gt_v):
        return up_v, gt_v * _sigmoid(gt_v) * up_v
    up, act = _matmul(n2, full["w_ffn_up"], mode="nn", name="ffn_up_act", out_blocks=N_DEV, epi=(act_fn, [gt], [F32, BF16]))
    take_weights(4, act, "ffn_down")
    h2 = _matmul(act, full["w_ffn_down"], mode="nn", out_dtype=F32, name="ffn_down", add=h1)

    def final_fn(rv, pv):
        (h_v, t_v), (g_v,) = rv, pv
        r = lax.rsqrt(_mean(h_v * h_v) + RMS_EPS)
        yn = h_v * r
        e = yn * g_v - t_v
        loss = 0.5 * jnp.sum(_mean(e * e))
        dout = e * (1.0 / D)
        dyg = dout * g_v
        dh = r * (dyg - yn * _mean(dyg * yn))
        return [dh, dh], [jnp.full((1, LANE), loss, F32), _colsum(dout * yn)]
    dh2, dh2_bf, loss_part, d_gf = _rowwise(final_fn, [h2, tgt], [gf], [(D, F32), (D, BF16)], [(1, LANE), (1, D)], name="final_loss")

    grads = {}

    def start_scatter(group, name, extra=()):
        blocks = [(grads[n].reshape(N_DEV, -1, grads[n].shape[1]) if n in row_sharded else grads[n], False) for n in group]
        (handle,), token = _exchange_start([blocks + list(extra)], name)
        return handle, token

    def dact_fn(d_v, gt_v, up_v):
        s = _sigmoid(gt_v)
        return d_v * up_v * (s * (1.0 + gt_v * (1.0 - s))), d_v * gt_v * s
    dgt, dup = _matmul(dh2_bf, full["w_ffn_down"], mode="nt", name="d_ffn_act", out_blocks=N_DEV,
                       epi=(dact_fn, [gt, up], [BF16, BF16]))
    grads["w_ffn_down"] = _matmul(act, dh2_bf, mode="tn", out_dtype=BF16, name="dw_ffn_down")
    dn2 = _matmul(dgt, full["w_ffn_gate"], mode="nt", out_dtype=F32, name="dn2_gate")
    dn2 = _matmul(dup, full["w_ffn_up"], mode="nt", out_dtype=F32, name="dn2_up", add=dn2)
    grads["w_ffn_gate"] = _matmul(n2, dgt, mode="tn", out_dtype=BF16, name="dw_ffn_gate", out_blocks=N_DEV)
    grads["w_ffn_up"] = _matmul(n2, dup, mode="tn", out_dtype=BF16, name="dw_ffn_up", out_blocks=N_DEV)
    scatter_groups = [["w_ffn_down", "w_ffn_gate", "w_ffn_up"], ["w_out", "w_proj_rwkv", "w_proj_sgu"],
                      ["w_in", "w_lora_up", "a_lora_up", "g_lora_up"]]
    scatter_ffn, token_ffn = start_scatter(scatter_groups[0], "scatter_start_ffn")
    dh1, dh1_bf, d_g2 = _rms_bwd(dn2, h1, dh2, norm_ffn_g, "rms_ffn_bwd", deps=[token_ffn])
    dmerged = _matmul(dh1_bf, full["w_out"], mode="nt", out_dtype=F32, name="d_merged")
    grads["w_out"] = _matmul(merged, dh1_bf, mode="tn", out_dtype=BF16, name="dw_out")

    def dmerge_fn(rv, pv):
        d_v, ga_v, gb_v, pa_v, pb_v = rv
        sa, sb = _sigmoid(ga_v), _sigmoid(gb_v)
        dgates = jnp.concatenate([d_v * pa_v * sa * (1.0 - sa), d_v * pb_v * sb * (1.0 - sb)], axis=1)
        return [dgates, d_v * sa, d_v * sb], []
    dproj, dpa, dpb = _rowwise(dmerge_fn, [dmerged, ga, gb, pa, pb], [],
                               [(2 * D, BF16, icp, b_ga // 2, None), (D, BF16), (D, BF16)], [], name="d_merge")
    dya = _matmul(dpa, full["w_proj_rwkv"], mode="nt", out_dtype=F32, name="d_ya")
    dyb = _matmul(dpb, full["w_proj_sgu"], mode="nt", out_dtype=F32, name="d_yb")
    grads["w_proj_rwkv"] = _matmul(ya, dpa, mode="tn", out_dtype=BF16, name="dw_proj_a", out_blocks=N_DEV)
    grads["w_proj_sgu"] = _matmul(yb, dpb, mode="tn", out_dtype=BF16, name="dw_proj_b", out_blocks=N_DEV)
    scatter_mid, token_mid = start_scatter(scatter_groups[1], "scatter_start_mid")
    dproj, d_lng, d_lnb, d_ws, d_bs = _sgu_bwd(proj, b_z, dyb, sgu_ln_g, sgu_ln_b, ws, bexp, dproj, "sgu_bwd")

    dy_h, dr1, dk1, dv1, dg_h, d_lnxg, d_lnxb, d_rk = _head_post_bwd(dya, y_h, r_h, k2_h, v_h, g_h, hp, "head_post_bwd",
                                                                     deps=[token_mid])
    dr2, dlw_h, dk2b, dv2, daa, dbb = _wkv_bwd(*wkv_in, states, dy_h, "wkv_bwd")
    dps, d_mu, d_w0, d_a0, d_kk, d_ka, d_wlw, d_wla, d_wlg = _rwkv_pre_bwd(
        proj, mu_p, rsmall, lora, [dr1, dr2, dk1, dk2b, dv1, dv2, dlw_h, daa, dbb, dg_h], lay, "rwkv_pre_bwd")
    dproj = _shift_bwd(dps, mu_p, dproj, "shift_bwd")
    split = lambda g: g.reshape(g.shape[0], N_DEV, -1).transpose(1, 0, 2)
    grads["w_in"] = _dw_in_from_proj(_matmul(n1, dproj, mode="tn", out_dtype=BF16, name="dw_in"), lay, D, w_in.shape[2], "dw_in_layout")
    grads["w_lora_up"] = split(d_wlw[:w_lora_up.shape[1]].astype(BF16))
    grads["a_lora_up"] = split(d_wla[:a_lora_up.shape[1]].astype(BF16))
    grads["g_lora_up"] = split(d_wlg[:g_lora_up.shape[1]].astype(BF16))
    out = {}

    def update_group(gi, handle, after, name):
        parts = _exchange_wait(handle, after, "scatter_wait_" + name, rels=SAME_CORE if handle["chips"] else ALL_PEERS)["lands"]
        for n, part in zip(scatter_groups[gi], parts):
            res = _adamw(weights[n][0], m_in[n][0], v_in[n][0], part, "adamw_" + n, after=after)
            out[n] = [t.reshape(weights[n].shape) for t in res]
            after = res[0]
        return after

    swap, token_swap = _sibling_swap([grads[n] for n in scatter_groups[2]], None, None, "scatter_in_swap_start")
    after = update_group(0, scatter_ffn, token_swap, "ffn")
    swap = _sibling_swap(None, swap, after, "scatter_in_swap_wait")
    core = lax.axis_index("c").astype(jnp.int32).reshape(1)
    chip_sums = [_pair_add(mine, theirs, core, "scatter_in_add_" + n)
                 for n, mine, theirs in zip(scatter_groups[2], swap["srcs"], swap["lands"])]
    (scatter_in,), token_in = _exchange_start([[(s, False) for s in chip_sums]], "scatter_start_in", rels=SAME_CORE, chips=True)
    dn1 = _matmul(dproj, W_in, mode="nt", out_dtype=F32, name="dn1", deps=[token_in])
    dx, _, d_g1 = _rms_bwd(dn1, xs, dh1, norm_mix_g, "rms_mix_bwd")
    small_grads = dict(norm_mix_g=d_g1, shift_mu=_unpad_rwkv_cols(d_mu, lay), w0=d_w0, a0=d_a0, k_k=d_kk, k_a=d_ka, r_k=d_rk,
                       lnx_g=d_lnxg, lnx_b=d_lnxb, sgu_ln_g=d_lng, sgu_ln_b=d_lnb, sgu_w=d_ws, sgu_b=d_bs[:, :G].T,
                       norm_ffn_g=d_g2, norm_final_g=d_gf)
    (gather_small,), after = _exchange_start([[(_pack([small_grads[n] for n in small]), True)]], "gather_small_start")
    after = update_group(1, scatter_mid, after, "mid")
    after = update_group(2, scatter_in, after, "in")
    packed = [_pack([d[n] for n in small]) for d in (weights, m_in, v_in)]
    small_parts = _exchange_wait(gather_small, after, "gather_small_wait")["lands"][0]
    res = _adamw(*packed, small_parts, "adamw_small")
    unpacked = [_unpack(t, [weights[n].shape for n in small]) for t in res]
    for i, n in enumerate(small):
        out[n] = [u[i] for u in unpacked]

    loss = lax.psum(loss_part[0, 0], ("x", "y", "c"))
    return (loss, dx[None], *[out[n][0] for n in names], *[out[n][1] for n in names],
            *[out[n][2] for n in names], *[out[n][3] for n in names])
```

```python
import jax
import jax.numpy as jnp
from jax import lax
from jax.experimental import pallas as pl
from jax.experimental.pallas import tpu as pltpu

F32 = jnp.float32
BF16 = jnp.bfloat16

N_DEV = 8
LANE = 128
SUBLANE = 8
HEAD = 64
SGU_CHUNK = 128
SGU_GROUP = 128
WKV_CHUNK = 64
RMS_EPS = 1e-6
LN_EPS = 1e-5
LNX_EPS = 64e-5
ADAM_LR, ADAM_B1, ADAM_B2, ADAM_EPS, ADAM_WD, ADAM_STEP = 0.001, 0.9, 0.999, 1e-08, 0.01, 10
VMEM_LIMIT_BYTES = 48 * 1024 * 1024
_SQRT_HALF = 0.7071067811865476
_INV_SQRT_2PI = 0.3989422804014327


def _pick(n, cands):
    for c in cands:
        if n % c == 0:
            return c
    return n


def _ceil_to(n, m):
    return -(-n // m) * m


def _params():
    return pltpu.CompilerParams(vmem_limit_bytes=VMEM_LIMIT_BYTES)


def _tile(n, cap):
    best = 0
    for d in range(LANE, min(n, cap) + 1, LANE):
        if n % d == 0:
            best = d
    return best or n


def _matmul_tiles(M, N, K, a_bytes, b_bytes, o_bytes, has_add, forced):
    tm = forced.get("m") or _tile(M, 1024)
    tn = forced.get("n") or _tile(N, 1024)
    tk = forced.get("k") or _tile(K, 2048)

    def vmem(tm, tn, tk):
        acc = tm * tn * 4 if tk < K else 0
        return 2 * (tm * tk * a_bytes + tk * tn * b_bytes + tm * tn * (o_bytes + (4 if has_add else 0))) + acc

    while vmem(tm, tn, tk) > (VMEM_LIMIT_BYTES * 3) // 4:
        if "k" not in forced and tk > 512 and _tile(K, tk // 2) < tk:
            tk = _tile(K, tk // 2)
        elif "m" not in forced and _tile(M, tm // 2) < tm:
            tm = _tile(M, tm // 2)
        else:
            break
    return tm, tn, tk


def _matmul(a, b, *, mode, out_dtype=F32, name, add=None, deps=(), out_blocks=0, epi=None):
    def view(x):
        return (x.shape[1], x.shape[0] * x.shape[2], x.shape[2]) if x.ndim == 3 else (x.shape[0], x.shape[1], 0)

    (ar, ac, aw), (br, bc, bw) = view(a), view(b)
    a_col, b_col = {"nn": ("k", "n"), "nt": ("k", "k"), "tn": ("m", "n")}[mode]
    if mode == "nn":
        M, K, K2, N = ar, ac, br, bc
    elif mode == "nt":
        M, K, N, K2 = ar, ac, br, bc
    else:
        K, M, K2, N = ar, ac, br, bc
    assert K == K2, (a.shape, b.shape, mode)
    forced = {}
    for dim, w in ((a_col, aw), (b_col, bw), ("n", N // out_blocks if out_blocks else 0)):
        if w:
            assert forced.get(dim, w) == w
            forced[dim] = w
    has_add = add is not None
    tile_bytes = (sum(jnp.dtype(d).itemsize for d in epi[2]) + sum((e[0] if isinstance(e, tuple) else e).dtype.itemsize for e in epi[1])
                  if epi is not None else jnp.dtype(out_dtype).itemsize)
    tm, tn, tk = _matmul_tiles(M, N, K, a.dtype.itemsize, b.dtype.itemsize, tile_bytes, has_add, forced)
    nk = K // tk
    dn = {"nn": (((1,), (0,)), ((), ())), "nt": (((1,), (1,)), ((), ())), "tn": (((0,), (0,)), ((), ()))}[mode]
    pick = {"m": lambda i, j, k: i, "n": lambda i, j, k: j, "k": lambda i, j, k: k}
    size = {"m": tm, "n": tn, "k": tk}

    def spec(blocked, row_dim, col_dim):
        rf, cf = pick[row_dim], pick[col_dim]
        if blocked:
            return pl.BlockSpec((None, size[row_dim], size[col_dim]), lambda i, j, k: (cf(i, j, k), rf(i, j, k), 0))
        return pl.BlockSpec((size[row_dim], size[col_dim]), lambda i, j, k: (rf(i, j, k), cf(i, j, k)))

    a_spec = spec(aw, "k" if mode == "tn" else "m", a_col)
    b_spec = spec(bw, "n" if mode == "nt" else "k", b_col)
    o_spec = spec(out_blocks, "m", "n")
    epi_fn, epi_ins, epi_dtypes = epi if epi is not None else (None, [], [out_dtype])
    epi_ins = [e if isinstance(e, tuple) else (e, None) for e in epi_ins]
    n_epi = len(epi_ins)
    n_in = 2 + has_add + n_epi + len(deps)
    n_out = len(epi_dtypes)

    def body(*refs):
        a_ref, b_ref = refs[0], refs[1]
        add_ref = refs[2] if has_add else None
        epi_refs = refs[2 + has_add:2 + has_add + n_epi]
        o_refs = refs[n_in:n_in + n_out]
        part = lax.dot_general(a_ref[...].astype(BF16), b_ref[...].astype(BF16), dn, preferred_element_type=F32)

        def finish(res):
            outs = epi_fn(res, *[e[...] for e in epi_refs]) if epi_fn is not None else (res,)
            for o_ref, val in zip(o_refs, outs):
                o_ref[...] = val.astype(o_ref.dtype)

        if nk == 1:
            finish(part + add_ref[...] if has_add else part)
            return
        acc_ref = refs[-1]
        kk = pl.program_id(2)

        @pl.when(kk == 0)
        def _():
            acc_ref[...] = part + add_ref[...] if has_add else part

        @pl.when(kk > 0)
        def _():
            acc_ref[...] += part

        @pl.when(kk == nk - 1)
        def _():
            finish(acc_ref[...])

    def epi_spec(arr, off):
        if off is None:
            return o_spec
        assert off % tn == 0
        return pl.BlockSpec((tm, tn), lambda i, j, k: (i, j + off // tn))

    ins = [a, b] + ([add] if has_add else []) + [arr for arr, _ in epi_ins] + list(deps)
    in_specs = ([a_spec, b_spec] + ([o_spec] if has_add else []) + [epi_spec(arr, off) for arr, off in epi_ins]
                + [pl.BlockSpec(d.shape, lambda i, j, k, nd=d.ndim: (0,) * nd) for d in deps])
    o_shape = (out_blocks, M, tn) if out_blocks else (M, N)
    res = pl.pallas_call(
        body, name=name, grid=(M // tm, N // tn, nk), in_specs=in_specs, out_specs=[o_spec] * n_out,
        out_shape=[jax.ShapeDtypeStruct(o_shape, dt) for dt in epi_dtypes],
        scratch_shapes=[pltpu.VMEM((tm, tn), F32)] if nk > 1 else [],
        compiler_params=_params())(*ins)
    return res[0] if epi is None else list(res)


def _rowwise(fn, rows, pars, row_outs, acc_outs, *, name, tm=256, deps=()):
    rows = [r if isinstance(r, tuple) else (r, r.shape[1], 0) for r in rows]
    row_outs = [o if len(o) == 5 else (o[0], o[1], o[0], 0, None) for o in row_outs]
    aliased = [(k, o[4]) for k, o in enumerate(row_outs) if o[4] is not None]
    R = rows[0][0].shape[0]
    if max(w for _, w, _ in rows) > 4096:
        tm = tm // 2
    tm = min(tm, R)
    assert R % tm == 0
    nr, npar = len(rows), len(pars)
    nro = len(row_outs)
    n_in = nr + npar + len(deps) + len(aliased)

    def body(*refs):
        rv = [r[...] for r in refs[:nr]]
        pv = [p[...] for p in refs[nr:nr + npar]]
        outs = refs[n_in:]
        ro, ao = fn(rv, pv)
        first = pl.program_id(0) == 0
        for o_ref, val in zip(outs[:nro], ro):
            o_ref[...] = val.astype(o_ref.dtype)

        @pl.when(first)
        def _():
            for o_ref, val in zip(outs[nro:], ao):
                o_ref[...] = val

        @pl.when(jnp.logical_not(first))
        def _():
            for o_ref, val in zip(outs[nro:], ao):
                o_ref[...] += val

    in_specs = ([pl.BlockSpec((tm, w), lambda i, cb=cb: (i, cb)) for _, w, cb in rows]
                + [pl.BlockSpec(p.shape, lambda i, nd=p.ndim: (0,) * nd) for p in list(pars) + list(deps)]
                + [pl.BlockSpec(memory_space=pl.ANY)] * len(aliased))
    out_shape = ([jax.ShapeDtypeStruct((R, full), dt) for _, dt, full, _, _ in row_outs]
                 + [jax.ShapeDtypeStruct(s, F32) for s in acc_outs])
    out_specs = ([pl.BlockSpec((tm, f), lambda i, cb=cb: (i, cb)) for f, _, _, cb, _ in row_outs]
                 + [pl.BlockSpec(s, lambda i, nd=len(s): (0,) * nd) for s in acc_outs])
    res = pl.pallas_call(body, name=name, grid=(R // tm,), in_specs=in_specs, out_specs=out_specs, out_shape=out_shape,
                         input_output_aliases={n_in - len(aliased) + q: k for q, (k, _) in enumerate(aliased)},
                         compiler_params=_params())(*[r for r, _, _ in rows], *pars, *deps, *[buf for _, buf in aliased])
    return list(res)


def _bdot(a, b, mode="nn"):
    dn = {"nn": (((1,), (0,)), ((), ())), "nt": (((1,), (1,)), ((), ())), "tn": (((0,), (0,)), ((), ()))}[mode]
    return lax.dot_general(a.astype(BF16), b.astype(BF16), dn, preferred_element_type=F32)


def _sigmoid(x):
    return jax.nn.sigmoid(x)


def _softplus(x):
    return jnp.maximum(x, 0.0) + jnp.log1p(jnp.exp(-jnp.abs(x)))


def _gelu(z):
    return 0.5 * z * (1.0 + lax.erf(z * _SQRT_HALF))


def _gelu_grad(z):
    return 0.5 * (1.0 + lax.erf(z * _SQRT_HALF)) + z * jnp.exp(-0.5 * z * z) * _INV_SQRT_2PI


def _mean(x):
    return jnp.mean(x, axis=-1, keepdims=True)


def _colsum(x):
    return jnp.sum(x, axis=0, keepdims=True)


def _rms_fwd(x, g, name, deps=()):
    def fn(rv, pv):
        (xv,), (gv,) = rv, pv
        r = lax.rsqrt(_mean(xv * xv) + RMS_EPS)
        return [xv * r * gv], []
    return _rowwise(fn, [x], [g], [(x.shape[1], BF16)], [], name=name, deps=deps)[0]


def _rms_bwd(dn, x, dres, g, name, deps=()):
    def fn(rv, pv):
        (dnv, xv, drv), (gv,) = rv, pv
        r = lax.rsqrt(_mean(xv * xv) + RMS_EPS)
        yn = xv * r
        dyg = dnv * gv
        dx = drv + r * (dyg - yn * _mean(dyg * yn))
        return [dx, dx], [_colsum(dnv * yn)]
    D = x.shape[1]
    return _rowwise(fn, [dn, x, dres], [g], [(D, F32), (D, BF16)], [(1, D)], name=name, deps=deps)


def _rwkv_layout(RW, Lw, La, Lg, D):
    widths = [RW, RW, RW, Lw, La, Lg]
    pw = [_ceil_to(w, LANE) for w in widths]
    pw[5] += _ceil_to(sum(pw), 2 * D) - sum(pw)
    offs = [sum(pw[:i]) for i in range(6)]
    return widths, pw, offs, sum(pw)


def _pad_rwkv_cols(a, lay):
    widths, pw, _, _ = lay
    pieces, src = [], 0
    for w, p in zip(widths, pw):
        pieces.append(a[:, src:src + w])
        if p > w:
            pieces.append(jnp.zeros((a.shape[0], p - w), a.dtype))
        src += w
    return jnp.concatenate(pieces, axis=1)


def _unpad_rwkv_cols(a, lay):
    widths, _, offs, _ = lay
    return jnp.concatenate([a[:, o:o + w] for o, w in zip(offs, widths)], axis=1)


def _proj_pieces(lay, D, cs):
    widths, _, offs, rcp = lay
    rc = sum(widths)
    segs = [(sum(widths[:j]), widths[j], offs[j]) for j in range(6)] + [(rc, D, rcp + 2 * D), (rc + D, D, rcp), (rc + 2 * D, D, rcp + D)]
    pieces = []
    for start, width, dst in segs:
        n = start
        while n < start + width:
            d, off = divmod(n, cs)
            take = min(cs - off, start + width - n)
            pieces.append((d, off, dst + n - start, take))
            n += take
    return pieces


def _w_in_to_proj(g, lay, D, name):
    nb, rows, cs = g.shape
    icp = lay[3] + 3 * D
    pieces = _proj_pieces(lay, D, cs)
    tm = _pick(rows, (256, 128, 64, 32, 16))

    def body(i_ref, o_ref):
        o_ref[...] = jnp.zeros_like(o_ref)
        for d, src, dst, w in pieces:
            o_ref[:, dst:dst + w] = i_ref[d, :, src:src + w]

    return pl.pallas_call(
        body, name=name, grid=(rows // tm,), in_specs=[pl.BlockSpec((nb, tm, cs), lambda i: (0, i, 0))],
        out_specs=pl.BlockSpec((tm, icp), lambda i: (i, 0)), out_shape=jax.ShapeDtypeStruct((rows, icp), g.dtype),
        compiler_params=_params())(g)


def _dw_in_from_proj(a, lay, D, cs, name):
    rows, icp = a.shape
    pieces = _proj_pieces(lay, D, cs)
    tm = _pick(rows, (256, 128, 64, 32, 16))

    def body(i_ref, o_ref):
        for d, src, dst, w in pieces:
            o_ref[d, :, src:src + w] = i_ref[:, dst:dst + w]

    return pl.pallas_call(
        body, name=name, grid=(rows // tm,), in_specs=[pl.BlockSpec((tm, icp), lambda i: (i, 0))],
        out_specs=pl.BlockSpec((N_DEV, tm, cs), lambda i: (0, i, 0)), out_shape=jax.ShapeDtypeStruct((N_DEV, rows, cs), a.dtype),
        compiler_params=_params())(a)


def _pad_rows(a, rows):
    return a if a.shape[0] == rows else jnp.concatenate([a, jnp.zeros((rows - a.shape[0], a.shape[1]), a.dtype)], axis=0)


def _token_shift(p, halo, mu, i):
    tm = p.shape[0]
    hid = lax.broadcasted_iota(jnp.int32, (SUBLANE, 1), 0)
    before = jnp.sum(jnp.where(hid == SUBLANE - 1, halo, 0.0), axis=0, keepdims=True)
    before = jnp.where(i == 0, 0.0, before)
    rid = lax.broadcasted_iota(jnp.int32, (tm, 1), 0)
    prev = jnp.where(rid == 0, before, pltpu.roll(p, 1, 0))
    d = prev - p
    return p + d * mu, d


def _rwkv_math(ps, w0, a0, k_k, k_a, wlw, wla, wlg, lay):
    _, pw, offs, _ = lay
    r, k, v, xw, xa, xg = (ps[:, offs[j]:offs[j] + pw[j]] for j in range(6))
    tw = jnp.tanh(xw)
    ww = w0 + _bdot(tw, wlw)
    lw = -jnp.exp(-_softplus(-ww) - 0.5)
    a = _sigmoid(a0 + _bdot(xa, wla))
    sg = _sigmoid(xg)
    g = _bdot(sg, wlg)
    return dict(r=r, k=k, v=v, xa=xa, tw=tw, ww=ww, lw=lw, a=a, sg=sg, g=g, kkp=k * k_k, k2=k * (1.0 + (a - 1.0) * k_a))


def _halo_specs(T, tm, width, after):
    hb = tm // SUBLANE
    last = T // SUBLANE - 1
    if after:
        return pl.BlockSpec((SUBLANE, width), lambda i: (jnp.minimum((i + 1) * hb, last), 0))
    return pl.BlockSpec((SUBLANE, width), lambda i: (jnp.maximum(i * hb - 1, 0), 0))


def _rowsum(x):
    return jnp.sum(x, axis=-1, keepdims=True)


def _kk_math(kkp):
    nrm = jnp.sqrt(_rowsum(kkp * kkp))
    inv = 1.0 / jnp.maximum(nrm, 1e-12)
    return nrm, inv, kkp * inv


def _rwkv_pre(p, mu, small, lora, lay, name):
    T, rcp = p.shape[0], lay[3]
    H = lay[0][0] // HEAD
    tm = min(128, T)

    def body(p_ref, ph_ref, mu_ref, w0_ref, a0_ref, kk_ref, ka_ref, wlw_ref, wla_ref, wlg_ref, r_o, lw_o, k2_o, v_o, aa_o, bb_o, g_o):
        ps, _ = _token_shift(p_ref[...], ph_ref[...], mu_ref[...], pl.program_id(0))
        q = _rwkv_math(ps, w0_ref[...], a0_ref[...], kk_ref[...], ka_ref[...], wlw_ref[...], wla_ref[...], wlg_ref[...], lay)
        for h in range(H):
            sl = slice(h * HEAD, (h + 1) * HEAD)
            for o_ref, key in ((r_o, "r"), (lw_o, "lw"), (k2_o, "k2"), (v_o, "v"), (g_o, "g")):
                o_ref[h] = q[key][:, sl]
            _, _, kk = _kk_math(q["kkp"][:, sl])
            aa_o[h] = -kk
            bb_o[h] = kk * q["a"][:, sl]

    whole = lambda arr: pl.BlockSpec(arr.shape, lambda i: (0, 0))
    return pl.pallas_call(
        body, name=name, grid=(T // tm,),
        in_specs=([pl.BlockSpec((tm, rcp), lambda i: (i, 0)), _halo_specs(T, tm, rcp, False), whole(mu)]
                  + [whole(s) for s in small] + [whole(w) for w in lora]),
        out_specs=[pl.BlockSpec((H, tm, HEAD), lambda i: (0, i, 0))] * 7, out_shape=[jax.ShapeDtypeStruct((H, T, HEAD), F32)] * 7,
        compiler_params=_params())(p, p, mu, *small, *lora)


def _rwkv_pre_bwd(p, mu, small, lora, hgrads, lay, name):
    T, rcp = p.shape[0], lay[3]
    widths, pw, offs, _ = lay
    RW = widths[0]
    H = RW // HEAD
    tm = min(128, T)

    def body(p_ref, ph_ref, mu_ref, w0_ref, a0_ref, kk_ref, ka_ref, wlw_ref, wla_ref, wlg_ref,
             dr1, dr2, dk1, dk2b, dv1, dv2, dlw_h, daa, dbb, dg_h,
             dps_ref, dmu_ref, dw0_ref, da0_ref, dkk_ref, dka_ref, dwlw_ref, dwla_ref, dwlg_ref,
             s_dr, s_dk2, s_dv, s_dlw, s_dkkp, s_da, s_dg):
        i = pl.program_id(0)
        ps, dprev = _token_shift(p_ref[...], ph_ref[...], mu_ref[...], i)
        k_k, k_a = kk_ref[...], ka_ref[...]
        q = _rwkv_math(ps, w0_ref[...], a0_ref[...], k_k, k_a, wlw_ref[...], wla_ref[...], wlg_ref[...], lay)
        k, a, lw, ww, tw, sg = q["k"], q["a"], q["lw"], q["ww"], q["tw"], q["sg"]
        for h in range(H):
            sl = slice(h * HEAD, (h + 1) * HEAD)
            s_dr[:, sl] = dr1[h] + dr2[h]
            s_dk2[:, sl] = dk1[h] + dk2b[h]
            s_dv[:, sl] = dv1[h] + dv2[h]
            s_dlw[:, sl] = dlw_h[h]
            s_dg[:, sl] = dg_h[h]
            nrm, inv, kk = _kk_math(q["kkp"][:, sl])
            dbb_h = dbb[h]
            dkk = dbb_h * a[:, sl] - daa[h]
            s_dkkp[:, sl] = jnp.where(nrm > 1e-12, inv * (dkk - kk * _rowsum(dkk * kk)), dkk * inv)
            s_da[:, sl] = dbb_h * kk
        dk2, dkkp, dg = s_dk2[...], s_dkkp[...], s_dg[...]
        dk = dk2 * (1.0 + (a - 1.0) * k_a) + dkkp * k_k
        da = s_da[...] + dk2 * k * k_a
        dpa = da * a * (1.0 - a)
        dww = s_dlw[...] * lw * _sigmoid(-ww)
        dxa = _bdot(dpa, wla_ref[...], "nt")
        dxw = _bdot(dww, wlw_ref[...], "nt") * (1.0 - tw * tw)
        dxg = _bdot(dg, wlg_ref[...], "nt") * sg * (1.0 - sg)
        segs = (s_dr[...], dk, s_dv[...], dxw, dxa, dxg)
        sums = [dmu_ref, dw0_ref, da0_ref, dkk_ref, dka_ref, dwlw_ref, dwla_ref, dwlg_ref]

        @pl.when(i == 0)
        def _():
            for s in sums:
                s[...] = jnp.zeros_like(s)

        for j, seg in enumerate(segs):
            sl = slice(offs[j], offs[j] + pw[j])
            dps_ref[:, sl] = seg
            dmu_ref[:, sl] += _colsum(seg * dprev[:, sl])
        dw0_ref[...] += _colsum(dww)
        da0_ref[...] += _colsum(dpa)
        dkk_ref[...] += _colsum(dkkp * k)
        dka_ref[...] += _colsum(dk2 * k * (a - 1.0))
        dwlw_ref[...] += _bdot(tw, dww, "tn")
        dwla_ref[...] += _bdot(q["xa"], dpa, "tn")
        dwlg_ref[...] += _bdot(sg, dg, "tn")

    whole = lambda arr: pl.BlockSpec(arr.shape, lambda i: (0, 0))
    row = lambda w: pl.BlockSpec((tm, w), lambda i: (i, 0))
    acc_shapes = [(1, rcp), (1, RW), (1, RW), (1, RW), (1, RW)] + [w.shape for w in lora]
    return pl.pallas_call(
        body, name=name, grid=(T // tm,),
        in_specs=([row(rcp), _halo_specs(T, tm, rcp, False), whole(mu)] + [whole(s) for s in small] + [whole(w) for w in lora]
                  + [pl.BlockSpec((H, tm, HEAD), lambda i: (0, i, 0))] * 10),
        out_specs=[row(rcp)] + [pl.BlockSpec(s, lambda i: (0, 0)) for s in acc_shapes],
        out_shape=[jax.ShapeDtypeStruct((T, rcp), F32)] + [jax.ShapeDtypeStruct(s, F32) for s in acc_shapes],
        scratch_shapes=[pltpu.VMEM((tm, RW), F32)] * 7, compiler_params=_params())(p, p, mu, *small, *lora, *hgrads)


def _shift_bwd(dps, mu, dproj, name):
    T, rcp = dps.shape
    tm = min(256, T)
    nt = T // tm

    def body(d_ref, dh_ref, mu_ref, buf_ref, o_ref):
        i = pl.program_id(0)
        d = d_ref[...]
        hid = lax.broadcasted_iota(jnp.int32, (SUBLANE, 1), 0)
        after = jnp.sum(jnp.where(hid == 0, dh_ref[...], 0.0), axis=0, keepdims=True)
        after = jnp.where(i == nt - 1, 0.0, after)
        rid = lax.broadcasted_iota(jnp.int32, (tm, 1), 0)
        nxt = jnp.where(rid == tm - 1, after, pltpu.roll(d, tm - 1, 0))
        mu_v = mu_ref[...]
        o_ref[...] = (d * (1.0 - mu_v) + nxt * mu_v).astype(BF16)

    row = pl.BlockSpec((tm, rcp), lambda i: (i, 0))
    return pl.pallas_call(
        body, name=name, grid=(nt,),
        in_specs=[row, _halo_specs(T, tm, rcp, True), pl.BlockSpec(mu.shape, lambda i: (0, 0)), pl.BlockSpec(memory_space=pl.ANY)],
        out_specs=row, out_shape=jax.ShapeDtypeStruct(dproj.shape, BF16), input_output_aliases={3: 0},
        compiler_params=_params())(dps, dps, mu, dproj)


def _head_post_math(y, r, k2, v, lg, lb, rk):
    yc = y - _mean(y)
    rstd = lax.rsqrt(_mean(yc * yc) + LNX_EPS)
    yn = yc * rstd
    s = _rowsum(r * k2 * rk)
    return yn, rstd, yn * lg + lb + s * v, s


def _head_post(y, r, k2, v, g, hp, name, deps=()):
    H, T, _ = y.shape
    tm = min(128, T)

    def body(y_ref, r_ref, k_ref, v_ref, g_ref, lg_ref, lb_ref, rk_ref, *rest):
        o_ref = rest[-1]
        _, _, t, _ = _head_post_math(y_ref[...], r_ref[...], k_ref[...], v_ref[...], lg_ref[...], lb_ref[...], rk_ref[...])
        out = (t * g_ref[...]).astype(BF16)
        for h in range(H):
            o_ref[:, h * HEAD:(h + 1) * HEAD] = out[h]

    blk = pl.BlockSpec((H, tm, HEAD), lambda i: (0, i, 0))
    par = pl.BlockSpec((H, 1, HEAD), lambda i: (0, 0, 0))
    return pl.pallas_call(
        body, name=name, grid=(T // tm,),
        in_specs=[blk] * 5 + [par] * 3 + [pl.BlockSpec(d.shape, lambda i, nd=d.ndim: (0,) * nd) for d in deps],
        out_specs=pl.BlockSpec((tm, H * HEAD), lambda i: (i, 0)),
        out_shape=jax.ShapeDtypeStruct((T, H * HEAD), BF16), compiler_params=_params())(y, r, k2, v, g, *hp, *deps)


def _head_post_bwd(dya, y, r, k2, v, g, hp, name, deps=()):
    H, T, _ = y.shape
    tm = min(128, T)
    hsum = lambda t: jnp.sum(t, axis=1, keepdims=True)

    def body(d_ref, y_ref, r_ref, k_ref, v_ref, g_ref, lg_ref, lb_ref, rk_ref, *rest):
        outs, d_s = rest[len(deps):len(deps) + 8], rest[-1]
        for h in range(H):
            d_s[h] = d_ref[:, h * HEAD:(h + 1) * HEAD]
        d_v, r_v, k_v, v_v, lg, rk = d_s[...], r_ref[...], k_ref[...], v_ref[...], lg_ref[...], rk_ref[...]
        yn, rstd, t, s = _head_post_math(y_ref[...], r_v, k_v, v_v, lg, lb_ref[...], rk)
        dyo = d_v * g_ref[...]
        dyn = dyo * lg
        ds = _rowsum(dyo * v_v)
        vals = (rstd * (dyn - _mean(dyn) - yn * _mean(dyn * yn)), ds * k_v * rk, ds * r_v * rk, dyo * s, d_v * t)
        for o_ref, val in zip(outs[:5], vals):
            o_ref[...] = val
        sums = (hsum(dyo * yn), hsum(dyo), hsum(ds * r_v * k_v))
        first = pl.program_id(0) == 0

        @pl.when(first)
        def _():
            for o_ref, val in zip(outs[5:], sums):
                o_ref[...] = val

        @pl.when(jnp.logical_not(first))
        def _():
            for o_ref, val in zip(outs[5:], sums):
                o_ref[...] += val

    blk = pl.BlockSpec((H, tm, HEAD), lambda i: (0, i, 0))
    par = pl.BlockSpec((H, 1, HEAD), lambda i: (0, 0, 0))
    return pl.pallas_call(
        body, name=name, grid=(T // tm,),
        in_specs=([pl.BlockSpec((tm, H * HEAD), lambda i: (i, 0))] + [blk] * 5 + [par] * 3
                  + [pl.BlockSpec(d.shape, lambda i, nd=d.ndim: (0,) * nd) for d in deps]),
        out_specs=[blk] * 5 + [par] * 3,
        out_shape=[jax.ShapeDtypeStruct((H, T, HEAD), F32)] * 5 + [jax.ShapeDtypeStruct((H, 1, HEAD), F32)] * 3,
        scratch_shapes=[pltpu.VMEM((H, tm, HEAD), F32)], compiler_params=_params())(dya, y, r, k2, v, g, *hp, *deps)


def _bmm(x, y, mode):
    dn = {"nn": (((2,), (1,)), ((0,), (0,))), "nt": (((2,), (2,)), ((0,), (0,))), "tn": (((1,), (1,)), ((0,), (0,)))}[mode]
    (xh, xl), (yh, yl) = _split(x), _split(y)
    dot = lambda p, q: lax.dot_general(p, q, dn, preferred_element_type=F32)
    out = dot(xh, yh)
    if yl is not None:
        out = out + dot(xh, yl)
    if xl is not None:
        out = out + dot(xl, yh)
    return out


def _split(x):
    if isinstance(x, tuple):
        return x
    hi = x.astype(BF16)
    return hi, (x - hi.astype(F32)).astype(BF16)


def _exact(x):
    return x.astype(BF16), None


def _round(x):
    return x if isinstance(x, tuple) else (x.astype(BF16), None)


def _rows(*xs):
    if isinstance(xs[0], tuple):
        return tuple(None if any(p is None for p in parts) else jnp.concatenate(parts, axis=1) for parts in zip(*xs))
    return jnp.concatenate(xs, axis=1)


def _wkv_chunk(r, lw, k, v, a, b):
    hb, C, _ = r.shape
    ti = lax.broadcasted_iota(jnp.int32, (C, C), 0)
    si = lax.broadcasted_iota(jnp.int32, (C, C), 1)
    linc, lstr, eye = (ti >= si).astype(F32), (ti > si).astype(F32), (ti == si).astype(F32)
    qmask = jnp.concatenate([jnp.concatenate([lstr, lstr], axis=1), jnp.concatenate([linc, linc], axis=1)], axis=0)
    lincb = _exact(jnp.broadcast_to(linc, (hb, C, C)))
    both = _exact(jnp.broadcast_to(jnp.concatenate([linc, lstr], axis=0), (hb, 2 * C, C)))
    ones = _exact(jnp.ones_like(v))
    lws = _split(lw)
    ci = _bmm(lincb, lws, "nn")
    cC = jnp.sum(lw, axis=1, keepdims=True)
    gi, ge, gn, gr = jnp.exp(ci), jnp.exp(ci - lw), jnp.exp(-ci), jnp.exp(cC - ci)
    q = dict(At=a * ge, Rt=r * gi, Bt=b * gn, Kt=k * gn, Bh=b * gr, Kh=k * gr)
    s = dict(AR=_round(_rows(q["At"], q["Rt"])), BK=_round(_rows(q["Bt"], q["Kt"])), BKh=_round(_rows(q["Bh"], q["Kh"])), v=_round(v))
    quad = _bmm(s["AR"], s["BK"], "nt") * qmask
    s["top"], s["bot"] = _round(quad[:, :C]), _round(quad[:, C:])
    A_ab = quad[:, :C, :C]
    Tm = eye + A_ab
    Pw = _round(A_ab)
    n = 1
    while 2 * n < C:
        Pw = _round(_bmm(Pw, Pw, "nn"))
        Tm = Tm + _bmm(_round(Tm), Pw, "nn")
        n *= 2
    s["Tm"] = _round(Tm)
    gC = jnp.exp(_bmm(lws, ones, "tn"))
    q.update(gi=gi, ge=ge, gn=gn, gr=gr, qmask=qmask, both=both, gC=gC, ones=ones, s=s)
    return q


def _wkv_u(s, H0s, C):
    arh = _bmm(s["AR"], H0s, "nn")
    zv = _rows(tuple(None if p is None else jnp.zeros_like(p) for p in s["v"]), s["v"])
    U = _bmm(s["Tm"], _round(arh[:, :C] + _bmm(s["top"], zv, "nn")), "nn")
    return arh, _rows(_round(U), s["v"])


def _wkv_fwd(r, lw, k, v, a, b, name):
    H, T, N = r.shape
    C = min(WKV_CHUNK, T)
    nc = T // C
    hb = _pick(H, (16, 8, 4, 2))

    def body(r_ref, lw_ref, k_ref, v_ref, a_ref, b_ref, y_ref, st_ref, h_ref):
        @pl.when(pl.program_id(1) == 0)
        def _():
            h_ref[...] = jnp.zeros_like(h_ref)

        H0 = h_ref[...]
        st_ref[0] = H0
        q = _wkv_chunk(r_ref[...], lw_ref[...], k_ref[...], v_ref[...], a_ref[...], b_ref[...])
        s = q["s"]
        arh, UV = _wkv_u(s, _round(H0), C)
        y_ref[...] = arh[:, C:] + _bmm(s["bot"], UV, "nn")
        h_ref[...] = q["gC"] * H0 + _bmm(s["BKh"], UV, "tn")

    blk = pl.BlockSpec((hb, C, N), lambda h, c: (h, c, 0))
    return pl.pallas_call(
        body, name=name, grid=(H // hb, nc), in_specs=[blk] * 6,
        out_specs=[blk, pl.BlockSpec((1, hb, N, N), lambda h, c: (c, h, 0, 0))],
        out_shape=[jax.ShapeDtypeStruct((H, T, N), F32), jax.ShapeDtypeStruct((nc, H, N, N), F32)],
        scratch_shapes=[pltpu.VMEM((hb, N, N), F32)], compiler_params=_params())(r, lw, k, v, a, b)


def _wkv_bwd(r, lw, k, v, a, b, states, dy, name):
    H, T, N = r.shape
    C = min(WKV_CHUNK, T)
    nc = T // C
    hb = _pick(H, (16, 8, 4, 2))

    def body(r_ref, lw_ref, k_ref, v_ref, a_ref, b_ref, st_ref, dy_ref, dr_ref, dlw_ref, dk_ref, dv_ref, da_ref, db_ref, dh_ref):
        @pl.when(pl.program_id(1) == 0)
        def _():
            dh_ref[...] = jnp.zeros_like(dh_ref)

        dHC = dh_ref[...]
        H0 = st_ref[0]
        q = _wkv_chunk(r_ref[...], lw_ref[...], k_ref[...], v_ref[...], a_ref[...], b_ref[...])
        s, gC = q["s"], q["gC"]
        H0s, dHs, dY = _round(H0), _round(dHC), _round(dy_ref[...])
        _, UV = _wkv_u(s, H0s, C)
        bot_dy = _bmm(s["bot"], dY, "tn")
        bkh_dh = _bmm(s["BKh"], dHs, "nn")
        dP = _round(_bmm(s["Tm"], _round(bot_dy[:, :C] + bkh_dh[:, :C]), "tn"))
        dv_ref[...] = bot_dy[:, C:] + bkh_dh[:, C:] + _bmm(s["top"], dP, "tn")[:, C:]
        dPY = _rows(dP, dY)
        dh_ref[...] = gC * dHC + _bmm(s["AR"], dPY, "tn")
        dquad = _round(_bmm(dPY, UV, "nt") * q["qmask"])
        dAR = _bmm(dPY, H0s, "nt") + _bmm(dquad, s["BK"], "nn")
        dBK = _bmm(dquad, s["AR"], "tn")
        dBKh = _bmm(UV, dHs, "nt")
        dAt, dRt, dBt, dKt, dBh, dKh = dAR[:, :C], dAR[:, C:], dBK[:, :C], dBK[:, C:], dBKh[:, :C], dBKh[:, C:]
        dr_ref[...] = dRt * q["gi"]
        da_ref[...] = dAt * q["ge"]
        db_ref[...] = dBt * q["gn"] + dBh * q["gr"]
        dk_ref[...] = dKt * q["gn"] + dKh * q["gr"]
        tail = dBh * q["Bh"] + dKh * q["Kh"]
        dci = dRt * q["Rt"] - dBt * q["Bt"] - dKt * q["Kt"] - tail
        dcC = jnp.sum(tail, axis=1, keepdims=True) + _bmm(q["ones"], H0 * dHC * gC, "nt")
        dlw_ref[...] = _bmm(q["both"], _rows(dci, dAt * q["At"]), "tn") + dcC

    blk = pl.BlockSpec((hb, C, N), lambda h, c: (h, nc - 1 - c, 0))
    st = pl.BlockSpec((1, hb, N, N), lambda h, c: (nc - 1 - c, h, 0, 0))
    return pl.pallas_call(
        body, name=name, grid=(H // hb, nc), in_specs=[blk] * 6 + [st, blk], out_specs=[blk] * 6,
        out_shape=[jax.ShapeDtypeStruct((H, T, N), F32)] * 6,
        scratch_shapes=[pltpu.VMEM((hb, N, N), F32)], compiler_params=_params())(r, lw, k, v, a, b, states, dy)


def _sgu_ln(z, SW, lng, lnb):
    ge = _gelu(z)
    u, vv = ge[:, :SW], ge[:, SW:]
    xc = vv - _mean(vv)
    rstd = lax.rsqrt(_mean(xc * xc) + LN_EPS)
    vn = xc * rstd
    return u, vn, rstd, vn * lng + lnb


def _causal(ws_ref, g):
    ti = lax.broadcasted_iota(jnp.int32, (SGU_CHUNK, SGU_CHUNK), 0)
    si = lax.broadcasted_iota(jnp.int32, (SGU_CHUNK, SGU_CHUNK), 1)
    return ti >= si, jnp.where(ti >= si, ws_ref[g], 0.0).astype(BF16)


def _sgu_fwd(proj, zblock, lng, lnb, ws, bexp, name):
    T, SW = proj.shape[0], lng.shape[1]
    G = ws.shape[0]
    tr = min(256, T)
    nch = tr // SGU_CHUNK

    def body(z_ref, lng_ref, lnb_ref, ws_ref, be_ref, o_ref):
        u, _, _, vl = _sgu_ln(z_ref[...], SW, lng_ref[...], lnb_ref[...])
        for g in range(G):
            cs = slice(g * SGU_GROUP, (g + 1) * SGU_GROUP)
            _, wc = _causal(ws_ref, g)
            for n in range(nch):
                rs = slice(n * SGU_CHUNK, (n + 1) * SGU_CHUNK)
                m = jnp.dot(wc, vl[rs, cs].astype(BF16), preferred_element_type=F32) + be_ref[:, cs]
                o_ref[rs, cs] = (u[rs, cs] * m).astype(BF16)

    whole = lambda arr: pl.BlockSpec(arr.shape, lambda i, nd=arr.ndim: (0,) * nd)
    return pl.pallas_call(
        body, name=name, grid=(T // tr,),
        in_specs=[pl.BlockSpec((tr, 2 * SW), lambda i: (i, zblock)), whole(lng), whole(lnb), whole(ws), whole(bexp)],
        out_specs=pl.BlockSpec((tr, SW), lambda i: (i, 0)), out_shape=jax.ShapeDtypeStruct((T, SW), BF16),
        compiler_params=_params())(proj, lng, lnb, ws, bexp)


def _sgu_bwd(proj, zblock, dyb, lng, lnb, ws, bexp, dproj, name):
    T, SW = proj.shape[0], lng.shape[1]
    G = ws.shape[0]
    tr = min(256, T)
    nch = tr // SGU_CHUNK
    nt = T // tr

    def body(z_ref, dy_ref, lng_ref, lnb_ref, ws_ref, be_ref, buf_ref, dz_ref, dlg_ref, dlb_ref, dws_ref, db_ref, du_s, dvl_s, dbacc_s):
        i = pl.program_id(0)
        zv = z_ref[...]
        lng_v = lng_ref[...]
        u, vn, rstd, vl = _sgu_ln(zv, SW, lng_v, lnb_ref[...])

        @pl.when(i == 0)
        def _():
            for s in (dlg_ref, dlb_ref, dws_ref, dbacc_s):
                s[...] = jnp.zeros_like(s)

        for g in range(G):
            cs = slice(g * SGU_GROUP, (g + 1) * SGU_GROUP)
            tri, wc = _causal(ws_ref, g)
            for n in range(nch):
                rs = slice(n * SGU_CHUNK, (n + 1) * SGU_CHUNK)
                blk = vl[rs, cs].astype(BF16)
                m = jnp.dot(wc, blk, preferred_element_type=F32) + be_ref[:, cs]
                dyv = dy_ref[rs, cs]
                du_s[rs, cs] = dyv * m
                dm = dyv * u[rs, cs]
                dvl_s[rs, cs] = _bdot(wc, dm, "tn")
                dws_ref[g] += jnp.where(tri, _bdot(dm, blk, "nt"), 0.0)
                dbacc_s[:, cs] += dm

        dvl = dvl_s[...]
        dlg_ref[...] += _colsum(dvl * vn)
        dlb_ref[...] += _colsum(dvl)
        dvn = dvl * lng_v
        dvv = rstd * (dvn - _mean(dvn) - vn * _mean(dvn * vn))
        gp = _gelu_grad(zv)
        dz_ref[:, :SW] = (du_s[...] * gp[:, :SW]).astype(BF16)
        dz_ref[:, SW:] = (dvv * gp[:, SW:]).astype(BF16)

        @pl.when(i == nt - 1)
        def _():
            lane = lax.broadcasted_iota(jnp.int32, (SGU_CHUNK, LANE), 1)
            out = jnp.zeros((SGU_CHUNK, LANE), F32)
            for g in range(G):
                col = jnp.sum(dbacc_s[:, g * SGU_GROUP:(g + 1) * SGU_GROUP], axis=1, keepdims=True)
                out = jnp.where(lane == g, col, out)
            db_ref[...] = out

    whole = lambda arr: pl.BlockSpec(arr.shape, lambda i, nd=arr.ndim: (0,) * nd)
    acc_shapes = [(1, SW), (1, SW), ws.shape, (SGU_CHUNK, LANE)]
    return pl.pallas_call(
        body, name=name, grid=(nt,),
        in_specs=[pl.BlockSpec((tr, 2 * SW), lambda i: (i, zblock)), pl.BlockSpec((tr, SW), lambda i: (i, 0)),
                  whole(lng), whole(lnb), whole(ws), whole(bexp), pl.BlockSpec(memory_space=pl.ANY)],
        out_specs=([pl.BlockSpec((tr, 2 * SW), lambda i: (i, zblock))]
                   + [pl.BlockSpec(s, lambda i, nd=len(s): (0,) * nd) for s in acc_shapes]),
        out_shape=[jax.ShapeDtypeStruct(dproj.shape, BF16)] + [jax.ShapeDtypeStruct(s, F32) for s in acc_shapes],
        scratch_shapes=[pltpu.VMEM((tr, SW), F32), pltpu.VMEM((tr, SW), F32), pltpu.VMEM((SGU_CHUNK, SW), F32)],
        input_output_aliases={6: 0}, compiler_params=_params())(proj, dyb, lng, lnb, ws, bexp, dproj)


_HBM = pl.BlockSpec(memory_space=pltpu.HBM)
_SEM = pl.BlockSpec(memory_space=pltpu.SEMAPHORE)
_DATAFLOW = pltpu.SideEffectType.DATAFLOW_SIDE_EFFECTING


def _mesh_place(chips=False):
    x, y, c = lax.axis_index("x"), lax.axis_index("y"), lax.axis_index("c")
    return x, y, c, (2 * x + y if chips else 4 * x + 2 * y + c)


def _peer(x, y, c, rel, chips=False):
    px = 1 - x if rel & 4 else x
    py = 1 - y if rel & 2 else y
    pc = 1 - c if rel & 1 else c
    return (px, py, pc), (2 * px + py if chips else 4 * px + 2 * py + pc)


ALL_PEERS = tuple(range(1, N_DEV))
SIBLING = (1,)
SAME_CORE = (2, 4, 6)
SIBLINGS_CORE = (3, 5, 7)


def _exchange_start(groups, name, rels=ALL_PEERS, chips=False):
    flat = [t for g in groups for t in g]
    sizes = [len(g) for g in groups]
    n, ng = len(flat), len(groups)
    srcs = [pltpu.with_memory_space_constraint(a, pltpu.HBM) for a, _ in flat]
    lands = [pltpu.with_memory_space_constraint(lax.empty(((N_DEV,) + a.shape) if isg else a.shape, a.dtype), pltpu.HBM)
             for a, isg in flat]

    def body(*refs):
        ins, lnd, sems, token = refs[:n], refs[n:2 * n], refs[2 * n:2 * n + 3 * ng], refs[-1]
        x, y, c, me = _mesh_place(chips)
        j0 = 0
        for gi, sz in enumerate(sizes):
            for rel in rels:
                dev, slot = _peer(x, y, c, rel, chips)
                for jj in range(sz):
                    j = j0 + jj
                    pltpu.make_async_remote_copy(
                        src_ref=ins[j] if flat[j][1] else ins[j].at[slot], dst_ref=lnd[j].at[me],
                        send_sem=sems[3 * gi].at[jj * (N_DEV - 1) + rel - 1], recv_sem=sems[3 * gi + 1].at[jj * (N_DEV - 1) + rel - 1],
                        device_id=dev, device_id_type=pl.DeviceIdType.MESH).start()
            for jj in range(sz):
                j = j0 + jj
                pltpu.make_async_copy(ins[j] if flat[j][1] else ins[j].at[me], lnd[j].at[me], sems[3 * gi + 2].at[jj]).start()
            j0 += sz
        token[...] = jnp.zeros_like(token)

    sem_shapes = [pltpu.SemaphoreType.DMA((k,)) for sz in sizes for k in (sz * (N_DEV - 1), sz * (N_DEV - 1), sz)]
    res = pl.pallas_call(
        body, name=name,
        out_shape=(*sem_shapes, *[pltpu.HBM(a.shape, a.dtype) for a in srcs], *[pltpu.HBM(a.shape, a.dtype) for a in lands],
                   jax.ShapeDtypeStruct((SUBLANE, LANE), F32)),
        in_specs=[_HBM] * (2 * n), out_specs=(*[_SEM] * (3 * ng), *[_HBM] * (2 * n), pl.BlockSpec(memory_space=pltpu.VMEM)),
        input_output_aliases={i: 3 * ng + i for i in range(2 * n)},
        compiler_params=pltpu.CompilerParams(has_side_effects=_DATAFLOW))(*srcs, *lands)
    sems, thru, token = res[:3 * ng], res[3 * ng:3 * ng + 2 * n], res[-1]
    handle, j0 = [], 0
    for gi, sz in enumerate(sizes):
        handle.append(dict(kinds=[k for _, k in groups[gi]], chips=chips, srcs=list(thru[j0:j0 + sz]), lands=list(thru[n + j0:n + j0 + sz]),
                           sems=list(sems[3 * gi:3 * gi + 3])))
        j0 += sz
    return handle, token


def _exchange_wait(group, after, name, rels=ALL_PEERS, local=True):
    kinds, sz = group["kinds"], len(group["kinds"])
    relay = group.get("relay", [])

    def body(*refs):
        ins, lnd, (ssem, rsem, lsem) = refs[:sz], refs[sz:2 * sz], refs[2 * sz:2 * sz + 3]
        x, y, c, me = _mesh_place(group["chips"])
        for rel in rels:
            dev, slot = _peer(x, y, c, rel, group["chips"])
            for jj in range(sz):
                cp = pltpu.make_async_remote_copy(
                    src_ref=ins[jj] if kinds[jj] else ins[jj].at[slot], dst_ref=lnd[jj].at[slot],
                    send_sem=ssem.at[jj * (N_DEV - 1) + rel - 1], recv_sem=rsem.at[jj * (N_DEV - 1) + rel - 1],
                    device_id=dev, device_id_type=pl.DeviceIdType.MESH)
                cp.wait_send()
                cp.wait_recv()
        if local:
            for jj in range(sz):
                pltpu.make_async_copy(ins[jj] if kinds[jj] else ins[jj].at[me], lnd[jj].at[me], lsem.at[jj]).wait()
        if relay:
            fsend, frecv = refs[2 * sz + 3:2 * sz + 5]
            dev = _peer(x, y, c, 1)[0]
            for q, (mine, theirs) in enumerate(zip(SAME_CORE, SIBLINGS_CORE)):
                for jj in range(sz):
                    cp = pltpu.make_async_remote_copy(
                        src_ref=lnd[jj].at[_peer(x, y, c, mine)[1]], dst_ref=lnd[jj].at[_peer(x, y, c, theirs)[1]],
                        send_sem=fsend.at[jj * len(SAME_CORE) + q], recv_sem=frecv.at[jj * len(SAME_CORE) + q],
                        device_id=dev, device_id_type=pl.DeviceIdType.MESH)
                    cp.wait_send()
                    cp.wait_recv()

    arrays = group["srcs"] + group["lands"]
    sems = group["sems"] + relay
    res = pl.pallas_call(
        body, name=name, out_shape=[pltpu.HBM(a.shape, a.dtype) for a in arrays],
        in_specs=[_HBM] * (2 * sz) + [_SEM] * len(sems) + [pl.BlockSpec(memory_space=pl.ANY)], out_specs=[_HBM] * (2 * sz),
        input_output_aliases={i: i for i in range(2 * sz)},
        compiler_params=pltpu.CompilerParams(has_side_effects=_DATAFLOW))(*arrays, *sems, after)
    return dict(group, srcs=list(res[:sz]), lands=list(res[sz:]), relay=[])


def _relay_start(group, name):
    sz = len(group["kinds"])
    nq = len(SAME_CORE)

    def body(*refs):
        lnd, fsend, frecv, token = refs[:sz], refs[sz], refs[sz + 1], refs[-1]
        x, y, c, _ = _mesh_place()
        dev = _peer(x, y, c, 1)[0]
        for q, rel in enumerate(SAME_CORE):
            slot = _peer(x, y, c, rel)[1]
            for jj in range(sz):
                pltpu.make_async_remote_copy(
                    src_ref=lnd[jj].at[slot], dst_ref=lnd[jj].at[slot], send_sem=fsend.at[jj * nq + q], recv_sem=frecv.at[jj * nq + q],
                    device_id=dev, device_id_type=pl.DeviceIdType.MESH).start()
        token[...] = jnp.zeros_like(token)

    lands = group["lands"]
    res = pl.pallas_call(
        body, name=name,
        out_shape=(pltpu.SemaphoreType.DMA((sz * nq,)), pltpu.SemaphoreType.DMA((sz * nq,)), *[pltpu.HBM(a.shape, a.dtype) for a in lands],
                   jax.ShapeDtypeStruct((SUBLANE, LANE), F32)),
        in_specs=[_HBM] * sz, out_specs=(_SEM, _SEM, *[_HBM] * sz, pl.BlockSpec(memory_space=pltpu.VMEM)),
        input_output_aliases={i: 2 + i for i in range(sz)},
        compiler_params=pltpu.CompilerParams(has_side_effects=_DATAFLOW))(*lands)
    return dict(group, lands=list(res[2:2 + sz]), relay=[res[0], res[1]]), res[-1]


def _sibling_swap(arrays, handle, after, name):
    start = handle is None
    n = len(arrays) if start else len(handle["srcs"])
    chips = N_DEV // 2
    if start:
        srcs = [pltpu.with_memory_space_constraint(a.reshape(chips, 2, *a.shape[1:]), pltpu.HBM) for a in arrays]
        lands = [pltpu.with_memory_space_constraint(lax.empty((chips,) + a.shape[1:], a.dtype), pltpu.HBM) for a in arrays]
    else:
        srcs, lands = handle["srcs"], handle["lands"]

    def body(*refs):
        ins, lnd, ssem, rsem = refs[:n], refs[n:2 * n], refs[2 * n], refs[2 * n + 1]
        x, y, c, _ = _mesh_place()
        dev = _peer(x, y, c, 1)[0]
        for q in range(chips):
            for j in range(n):
                cp = pltpu.make_async_remote_copy(
                    src_ref=ins[j].at[q, 1 - c], dst_ref=lnd[j].at[q], send_sem=ssem.at[j * chips + q], recv_sem=rsem.at[j * chips + q],
                    device_id=dev, device_id_type=pl.DeviceIdType.MESH)
                if start:
                    cp.start()
                else:
                    cp.wait_send()
                    cp.wait_recv()
        if start:
            refs[-1][...] = jnp.zeros_like(refs[-1])

    thru = [pltpu.HBM(a.shape, a.dtype) for a in srcs + lands]
    effect = pltpu.CompilerParams(has_side_effects=_DATAFLOW)
    if start:
        res = pl.pallas_call(
            body, name=name, out_shape=(pltpu.SemaphoreType.DMA((n * chips,)), pltpu.SemaphoreType.DMA((n * chips,)), *thru,
                                        jax.ShapeDtypeStruct((SUBLANE, LANE), F32)),
            in_specs=[_HBM] * (2 * n), out_specs=(_SEM, _SEM, *[_HBM] * (2 * n), pl.BlockSpec(memory_space=pltpu.VMEM)),
            input_output_aliases={i: 2 + i for i in range(2 * n)}, compiler_params=effect)(*srcs, *lands)
        return dict(srcs=list(res[2:2 + n]), lands=list(res[2 + n:2 + 2 * n]), sems=[res[0], res[1]]), res[-1]
    res = pl.pallas_call(
        body, name=name, out_shape=thru, in_specs=[_HBM] * (2 * n) + [_SEM, _SEM, pl.BlockSpec(memory_space=pl.ANY)],
        out_specs=[_HBM] * (2 * n), input_output_aliases={i: i for i in range(2 * n)}, compiler_params=effect)(
            *srcs, *lands, *handle["sems"], after)
    return dict(handle, srcs=list(res[:n]), lands=list(res[n:]))


def _pair_add(mine, theirs, core, name):
    chips, _, rows, w = mine.shape
    tm = _pick(rows, (256, 128, 64, 32, 16))

    def body(core_ref, a_ref, b_ref, o_ref):
        o_ref[...] = (a_ref[...].astype(F32) + b_ref[...].astype(F32)).astype(o_ref.dtype)

    return pl.pallas_call(
        body, name=name, out_shape=jax.ShapeDtypeStruct(theirs.shape, theirs.dtype),
        grid_spec=pltpu.PrefetchScalarGridSpec(
            num_scalar_prefetch=1, grid=(chips, rows // tm),
            in_specs=[pl.BlockSpec((None, None, tm, w), lambda q, i, core_ref: (q, core_ref[0], i, 0)),
                      pl.BlockSpec((None, tm, w), lambda q, i, core_ref: (q, i, 0))],
            out_specs=pl.BlockSpec((None, tm, w), lambda q, i, core_ref: (q, i, 0))),
        compiler_params=_params())(core, mine, theirs)


def _adamw(w, m, v, gparts, name, after=None):
    R, C = w.shape
    tm = _pick(R, (256, 128, 64, 32, 16, 8))
    order = [] if after is None else [after]

    def body(w_ref, m_ref, v_ref, g_ref, *rest):
        go, do, mo, vo = rest[len(order):]
        g = g_ref[0].astype(F32)
        for j in range(1, gparts.shape[0]):
            g = g + g_ref[j].astype(F32)
        mn = ADAM_B1 * m_ref[...] + (1.0 - ADAM_B1) * g
        vn = ADAM_B2 * v_ref[...] + (1.0 - ADAM_B2) * (g * g)
        m_hat = mn / (1.0 - ADAM_B1 ** ADAM_STEP)
        v_hat = vn / (1.0 - ADAM_B2 ** ADAM_STEP)
        go[...] = g
        do[...] = -ADAM_LR * (m_hat / (jnp.sqrt(v_hat) + ADAM_EPS) + ADAM_WD * w_ref[...])
        mo[...] = mn
        vo[...] = vn

    row = pl.BlockSpec((tm, C), lambda i: (i, 0))
    return pl.pallas_call(
        body, name=name, grid=(R // tm,),
        in_specs=[row, row, row, pl.BlockSpec((gparts.shape[0], tm, C), lambda i: (0, i, 0))] + [pl.BlockSpec(memory_space=pl.ANY)] * len(order),
        out_specs=[row] * 4, out_shape=[jax.ShapeDtypeStruct((R, C), F32)] * 4, compiler_params=_params())(w, m, v, gparts, *order)


def _pack(arrays):
    parts = []
    for a in arrays:
        f = a.reshape(1, -1)
        pad = _ceil_to(f.shape[1], SUBLANE * LANE) - f.shape[1]
        f = jnp.concatenate([f, jnp.zeros((1, pad), f.dtype)], axis=1) if pad else f
        parts.append(f.reshape(-1, LANE))
    rows = sum(p.shape[0] for p in parts)
    pad = _ceil_to(rows, 64) - rows
    return jnp.concatenate(parts + ([jnp.zeros((pad, LANE), parts[0].dtype)] if pad else []), axis=0)


def _unpack(buf, shapes):
    out, row = [], 0
    for s in shapes:
        size = 1
        for d in s:
            size *= d
        rows = _ceil_to(size, SUBLANE * LANE) // LANE
        out.append(buf[row:row + rows].reshape(1, -1)[:, :size].reshape(s))
        row += rows
    return out


def kernel(x, norm_mix_g, w_in, shift_mu, w0, w_lora_up, a0, a_lora_up, g_lora_up, k_k, k_a, r_k, lnx_g, lnx_b, w_proj_rwkv, sgu_ln_g, sgu_ln_b, sgu_w, sgu_b, w_proj_sgu, w_out, norm_ffn_g, w_ffn_gate, w_ffn_up, w_ffn_down, norm_final_g, loss_target, m_norm_mix_g, m_w_in, m_shift_mu, m_w0, m_w_lora_up, m_a0, m_a_lora_up, m_g_lora_up, m_k_k, m_k_a, m_r_k, m_lnx_g, m_lnx_b, m_w_proj_rwkv, m_sgu_ln_g, m_sgu_ln_b, m_sgu_w, m_sgu_b, m_w_proj_sgu, m_w_out, m_norm_ffn_g, m_w_ffn_gate, m_w_ffn_up, m_w_ffn_down, m_norm_final_g, v_norm_mix_g, v_w_in, v_shift_mu, v_w0, v_w_lora_up, v_a0, v_a_lora_up, v_g_lora_up, v_k_k, v_k_a, v_r_k, v_lnx_g, v_lnx_b, v_w_proj_rwkv, v_sgu_ln_g, v_sgu_ln_b, v_sgu_w, v_sgu_b, v_w_proj_sgu, v_w_out, v_norm_ffn_g, v_w_ffn_gate, v_w_ffn_up, v_w_ffn_down, v_norm_final_g):
    weights = dict(norm_mix_g=norm_mix_g, w_in=w_in, shift_mu=shift_mu, w0=w0, w_lora_up=w_lora_up, a0=a0, a_lora_up=a_lora_up,
                   g_lora_up=g_lora_up, k_k=k_k, k_a=k_a, r_k=r_k, lnx_g=lnx_g, lnx_b=lnx_b, w_proj_rwkv=w_proj_rwkv,
                   sgu_ln_g=sgu_ln_g, sgu_ln_b=sgu_ln_b, sgu_w=sgu_w, sgu_b=sgu_b, w_proj_sgu=w_proj_sgu, w_out=w_out,
                   norm_ffn_g=norm_ffn_g, w_ffn_gate=w_ffn_gate, w_ffn_up=w_ffn_up, w_ffn_down=w_ffn_down, norm_final_g=norm_final_g)
    m_in = dict(norm_mix_g=m_norm_mix_g, w_in=m_w_in, shift_mu=m_shift_mu, w0=m_w0, w_lora_up=m_w_lora_up, a0=m_a0,
                a_lora_up=m_a_lora_up, g_lora_up=m_g_lora_up, k_k=m_k_k, k_a=m_k_a, r_k=m_r_k, lnx_g=m_lnx_g, lnx_b=m_lnx_b,
                w_proj_rwkv=m_w_proj_rwkv, sgu_ln_g=m_sgu_ln_g, sgu_ln_b=m_sgu_ln_b, sgu_w=m_sgu_w, sgu_b=m_sgu_b,
                w_proj_sgu=m_w_proj_sgu, w_out=m_w_out, norm_ffn_g=m_norm_ffn_g, w_ffn_gate=m_w_ffn_gate, w_ffn_up=m_w_ffn_up,
                w_ffn_down=m_w_ffn_down, norm_final_g=m_norm_final_g)
    v_in = dict(norm_mix_g=v_norm_mix_g, w_in=v_w_in, shift_mu=v_shift_mu, w0=v_w0, w_lora_up=v_w_lora_up, a0=v_a0,
                a_lora_up=v_a_lora_up, g_lora_up=v_g_lora_up, k_k=v_k_k, k_a=v_k_a, r_k=v_r_k, lnx_g=v_lnx_g, lnx_b=v_lnx_b,
                w_proj_rwkv=v_w_proj_rwkv, sgu_ln_g=v_sgu_ln_g, sgu_ln_b=v_sgu_ln_b, sgu_w=v_sgu_w, sgu_b=v_sgu_b,
                w_proj_sgu=v_w_proj_sgu, w_out=v_w_out, norm_ffn_g=v_norm_ffn_g, w_ffn_gate=v_w_ffn_gate, w_ffn_up=v_w_ffn_up,
                w_ffn_down=v_w_ffn_down, norm_final_g=v_norm_final_g)
    names = list(weights)
    col_sharded = ("w_in", "w_lora_up", "a_lora_up", "g_lora_up", "w_proj_rwkv", "w_proj_sgu", "w_ffn_gate", "w_ffn_up")
    row_sharded = ("w_out", "w_ffn_down")
    sharded = [n for n in names if n in col_sharded or n in row_sharded]
    small = [n for n in names if n not in sharded]

    xs, tgt = x[0], loss_target[0]
    T, D = xs.shape
    RW = w0.shape[1]
    H = RW // HEAD
    SW = sgu_ln_g.shape[1]
    G = sgu_w.shape[1]
    assert 2 * SW == D, "the projection layout takes the SGU part to be as wide as a gate"
    lay = _rwkv_layout(RW, w_lora_up.shape[1], a_lora_up.shape[1], g_lora_up.shape[1], D)
    _, pw, _, rcp = lay
    icp = rcp + 3 * D
    b_ga, b_gb, b_z = rcp // D, rcp // D + 1, rcp // D + 2

    gather_groups = [["w_in", "w_lora_up", "a_lora_up", "g_lora_up"], ["w_proj_rwkv", "w_proj_sgu", "w_out"],
                     ["w_ffn_gate"], ["w_ffn_up"], ["w_ffn_down"]]
    gather, gather_token = _exchange_start([[(weights[n][0].astype(BF16), True) for n in grp] for grp in gather_groups],
                                           "gather_start", rels=SIBLING + SAME_CORE)
    full = {}
    relay_tokens = {}
    joined = lambda g: g.transpose(1, 0, 2).reshape(g.shape[1], -1)

    def relay_weights(gi, after, name):
        arrived = _exchange_wait(gather[gi], after, "gather_wait_ici_" + name, rels=SAME_CORE, local=False)
        gather[gi], relay_tokens[gi] = _relay_start(arrived, "gather_relay_" + name)

    def take_weights(gi, after, name):
        done = _exchange_wait(gather[gi], after, "gather_wait_d2d_" + name, rels=SIBLING)
        for n, g in zip(gather_groups[gi], done["lands"]):
            full[n] = g.reshape(-1, g.shape[2]) if n in row_sharded else g

    n1 = _rms_fwd(xs, norm_mix_g, "rms_mix", deps=[gather_token])
    relay_weights(0, n1, "in")
    take_weights(0, relay_tokens[0], "in")
    W_in = _w_in_to_proj(full["w_in"], lay, D, "w_in_layout")
    lora = [_pad_rows(joined(full[n]), rows) for n, rows in zip(("w_lora_up", "a_lora_up", "g_lora_up"), pw[3:])]
    mu_p = _pad_rwkv_cols(shift_mu, lay)
    rsmall = [w0, a0, k_k, k_a]
    hp = [lnx_g.reshape(H, 1, HEAD), lnx_b.reshape(H, 1, HEAD), r_k.reshape(H, 1, HEAD)]
    ws = sgu_w[0]
    bexp = jnp.repeat(sgu_b[0].T, SGU_GROUP, axis=1)
    gf = norm_final_g.reshape(1, D)

    proj = _matmul(n1, W_in, mode="nn", out_dtype=F32, name="proj_in")
    ga, gb = (proj, D, b_ga), (proj, D, b_gb)
    r_h, lw_h, k2_h, v_h, aa_h, bb_h, g_h = _rwkv_pre(proj, mu_p, rsmall, lora, lay, "rwkv_pre")
    wkv_in = [r_h, lw_h, k2_h, v_h, aa_h, bb_h]
    y_h, states = _wkv_fwd(*wkv_in, "wkv_fwd")
    relay_weights(1, y_h, "proj")
    relay_weights(2, relay_tokens[1], "ffn_gate")
    ya = _head_post(y_h, r_h, k2_h, v_h, g_h, hp, "head_post", deps=[relay_tokens[2]])
    relay_weights(3, ya, "ffn_up")
    yb = _sgu_fwd(proj, b_z, sgu_ln_g, sgu_ln_b, ws, bexp, "sgu_fwd")
    take_weights(1, ya, "proj")
    pa = _matmul(ya, full["w_proj_rwkv"], mode="nn", out_dtype=F32, name="proj_a", deps=[relay_tokens[3]])

    def merge_fn(pb_v, pa_v, ga_v, gb_v):
        return pb_v, _sigmoid(ga_v) * pa_v + _sigmoid(gb_v) * pb_v
    pb, merged = _matmul(yb, full["w_proj_sgu"], mode="nn", name="proj_b_merge",
                         epi=(merge_fn, [pa, (proj, b_ga * D), (proj, b_gb * D)], [F32, BF16]))
    h1 = _matmul(merged, full["w_out"], mode="nn", out_dtype=F32, name="out_proj", add=xs)
    n2 = _rms_fwd(h1, norm_ffn_g, "rms_ffn")
    relay_weights(4, n2, "ffn_down")
    take_weights(2, n2, "ffn_gate")
    gt = _matmul(n2, full["w_ffn_gate"], mode="nn", out_dtype=F32, name="ffn_gate", out_blocks=N_DEV, deps=[relay_tokens[4]])
    take_weights(3, gt, "ffn_up")

    def act_fn(up_v, gt_v):
        return up_v, gt_v * _sigmoid(gt_v) * up_v
    up, act = _matmul(n2, full["w_ffn_up"], mode="nn", name="ffn_up_act", out_blocks=N_DEV, epi=(act_fn, [gt], [F32, BF16]))
    take_weights(4, act, "ffn_down")
    h2 = _matmul(act, full["w_ffn_down"], mode="nn", out_dtype=F32, name="ffn_down", add=h1)

    def final_fn(rv, pv):
        (h_v, t_v), (g_v,) = rv, pv
        r = lax.rsqrt(_mean(h_v * h_v) + RMS_EPS)
        yn = h_v * r
        e = yn * g_v - t_v
        loss = 0.5 * jnp.sum(_mean(e * e))
        dout = e * (1.0 / D)
        dyg = dout * g_v
        dh = r * (dyg - yn * _mean(dyg * yn))
        return [dh, dh], [jnp.full((1, LANE), loss, F32), _colsum(dout * yn)]
    dh2, dh2_bf, loss_part, d_gf = _rowwise(final_fn, [h2, tgt], [gf], [(D, F32), (D, BF16)], [(1, LANE), (1, D)], name="final_loss")

    grads = {}

    def start_scatter(group, name, extra=()):
        blocks = [(grads[n].reshape(N_DEV, -1, grads[n].shape[1]) if n in row_sharded else grads[n], False) for n in group]
        (handle,), token = _exchange_start([blocks + list(extra)], name)
        return handle, token

    def dact_fn(d_v, gt_v, up_v):
        s = _sigmoid(gt_v)
        return d_v * up_v * (s * (1.0 + gt_v * (1.0 - s))), d_v * gt_v * s
    dgt, dup = _matmul(dh2_bf, full["w_ffn_down"], mode="nt", name="d_ffn_act", out_blocks=N_DEV,
                       epi=(dact_fn, [gt, up], [BF16, BF16]))
    grads["w_ffn_down"] = _matmul(act, dh2_bf, mode="tn", out_dtype=BF16, name="dw_ffn_down")
    dn2 = _matmul(dgt, full["w_ffn_gate"], mode="nt", out_dtype=F32, name="dn2_gate")
    dn2 = _matmul(dup, full["w_ffn_up"], mode="nt", out_dtype=F32, name="dn2_up", add=dn2)
    grads["w_ffn_gate"] = _matmul(n2, dgt, mode="tn", out_dtype=BF16, name="dw_ffn_gate", out_blocks=N_DEV)
    grads["w_ffn_up"] = _matmul(n2, dup, mode="tn", out_dtype=BF16, name="dw_ffn_up", out_blocks=N_DEV)
    scatter_groups = [["w_ffn_down", "w_ffn_gate", "w_ffn_up"], ["w_out", "w_proj_rwkv", "w_proj_sgu"],
                      ["w_in", "w_lora_up", "a_lora_up", "g_lora_up"]]
    scatter_ffn, token_ffn = start_scatter(scatter_groups[0], "scatter_start_ffn")
    dh1, dh1_bf, d_g2 = _rms_bwd(dn2, h1, dh2, norm_ffn_g, "rms_ffn_bwd", deps=[token_ffn])
    dmerged = _matmul(dh1_bf, full["w_out"], mode="nt", out_dtype=F32, name="d_merged")
    grads["w_out"] = _matmul(merged, dh1_bf, mode="tn", out_dtype=BF16, name="dw_out")

    def dmerge_fn(rv, pv):
        d_v, ga_v, gb_v, pa_v, pb_v = rv
        sa, sb = _sigmoid(ga_v), _sigmoid(gb_v)
        dgates = jnp.concatenate([d_v * pa_v * sa * (1.0 - sa), d_v * pb_v * sb * (1.0 - sb)], axis=1)
        return [dgates, d_v * sa, d_v * sb], []
    dproj, dpa, dpb = _rowwise(dmerge_fn, [dmerged, ga, gb, pa, pb], [],
                               [(2 * D, BF16, icp, b_ga // 2, None), (D, BF16), (D, BF16)], [], name="d_merge")
    dya = _matmul(dpa, full["w_proj_rwkv"], mode="nt", out_dtype=F32, name="d_ya")
    dyb = _matmul(dpb, full["w_proj_sgu"], mode="nt", out_dtype=F32, name="d_yb")
    grads["w_proj_rwkv"] = _matmul(ya, dpa, mode="tn", out_dtype=BF16, name="dw_proj_a", out_blocks=N_DEV)
    grads["w_proj_sgu"] = _matmul(yb, dpb, mode="tn", out_dtype=BF16, name="dw_proj_b", out_blocks=N_DEV)
    scatter_mid, token_mid = start_scatter(scatter_groups[1], "scatter_start_mid")
    dproj, d_lng, d_lnb, d_ws, d_bs = _sgu_bwd(proj, b_z, dyb, sgu_ln_g, sgu_ln_b, ws, bexp, dproj, "sgu_bwd")

    dy_h, dr1, dk1, dv1, dg_h, d_lnxg, d_lnxb, d_rk = _head_post_bwd(dya, y_h, r_h, k2_h, v_h, g_h, hp, "head_post_bwd",
                                                                     deps=[token_mid])
    dr2, dlw_h, dk2b, dv2, daa, dbb = _wkv_bwd(*wkv_in, states, dy_h, "wkv_bwd")
    dps, d_mu, d_w0, d_a0, d_kk, d_ka, d_wlw, d_wla, d_wlg = _rwkv_pre_bwd(
        proj, mu_p, rsmall, lora, [dr1, dr2, dk1, dk2b, dv1, dv2, dlw_h, daa, dbb, dg_h], lay, "rwkv_pre_bwd")
    dproj = _shift_bwd(dps, mu_p, dproj, "shift_bwd")
    split = lambda g: g.reshape(g.shape[0], N_DEV, -1).transpose(1, 0, 2)
    grads["w_in"] = _dw_in_from_proj(_matmul(n1, dproj, mode="tn", out_dtype=BF16, name="dw_in"), lay, D, w_in.shape[2], "dw_in_layout")
    grads["w_lora_up"] = split(d_wlw[:w_lora_up.shape[1]].astype(BF16))
    grads["a_lora_up"] = split(d_wla[:a_lora_up.shape[1]].astype(BF16))
    grads["g_lora_up"] = split(d_wlg[:g_lora_up.shape[1]].astype(BF16))
    out = {}

    arrived = {}

    def update(group, after):
        for n in group:
            res = _adamw(weights[n][0], m_in[n][0], v_in[n][0], arrived[n], "adamw_" + n, after=after)
            out[n] = [t.reshape(weights[n].shape) for t in res]
            after = res[0]
        return after

    def update_group(gi, handle, after, name, first=None):
        parts = _exchange_wait(handle, after, "scatter_wait_" + name, rels=SAME_CORE if handle["chips"] else ALL_PEERS)["lands"]
        arrived.update(zip(scatter_groups[gi], parts))
        return update(scatter_groups[gi][:first], after)

    swap, token_swap = _sibling_swap([grads[n] for n in scatter_groups[2]], None, None, "scatter_in_swap_start")
    after = update_group(0, scatter_ffn, token_swap, "ffn", first=2)
    swap = _sibling_swap(None, swap, after, "scatter_in_swap_wait")
    core = lax.axis_index("c").astype(jnp.int32).reshape(1)
    chip_sums = [_pair_add(mine, theirs, core, "scatter_in_add_" + n)
                 for n, mine, theirs in zip(scatter_groups[2], swap["srcs"], swap["lands"])]
    (scatter_in,), token_in = _exchange_start([[(s, False) for s in chip_sums]], "scatter_start_in", rels=SAME_CORE, chips=True)
    dn1 = _matmul(dproj, W_in, mode="nt", out_dtype=F32, name="dn1", deps=[token_in])
    dx, _, d_g1 = _rms_bwd(dn1, xs, dh1, norm_mix_g, "rms_mix_bwd")
    small_grads = dict(norm_mix_g=d_g1, shift_mu=_unpad_rwkv_cols(d_mu, lay), w0=d_w0, a0=d_a0, k_k=d_kk, k_a=d_ka, r_k=d_rk,
                       lnx_g=d_lnxg, lnx_b=d_lnxb, sgu_ln_g=d_lng, sgu_ln_b=d_lnb, sgu_w=d_ws, sgu_b=d_bs[:, :G].T,
                       norm_ffn_g=d_g2, norm_final_g=d_gf)
    (gather_small,), after = _exchange_start([[(_pack([small_grads[n] for n in small]), True)]], "gather_small_start")
    after = update(scatter_groups[0][2:], after)
    after = update_group(1, scatter_mid, after, "mid")
    after = update_group(2, scatter_in, after, "in")
    packed = [_pack([d[n] for n in small]) for d in (weights, m_in, v_in)]
    small_parts = _exchange_wait(gather_small, after, "gather_small_wait")["lands"][0]
    res = _adamw(*packed, small_parts, "adamw_small")
    unpacked = [_unpack(t, [weights[n].shape for n in small]) for t in res]
    for i, n in enumerate(small):
        out[n] = [u[i] for u in unpacked]

    loss = lax.psum(loss_part[0, 0], ("x", "y", "c"))
    return (loss, dx[None], *[out[n][0] for n in names], *[out[n][1] for n in names],
            *[out[n][2] for n in names], *[out[n][3] for n in names])
```

```python
import jax
import jax.numpy as jnp
from jax import lax
from jax.experimental import pallas as pl
from jax.experimental.pallas import tpu as pltpu

F32 = jnp.float32
BF16 = jnp.bfloat16

N_DEV = 8
LANE = 128
SUBLANE = 8
HEAD = 64
SGU_CHUNK = 128
SGU_GROUP = 128
WKV_CHUNK = 64
RMS_EPS = 1e-6
LN_EPS = 1e-5
LNX_EPS = 64e-5
ADAM_LR, ADAM_B1, ADAM_B2, ADAM_EPS, ADAM_WD, ADAM_STEP = 0.001, 0.9, 0.999, 1e-08, 0.01, 10
VMEM_LIMIT_BYTES = 48 * 1024 * 1024
_SQRT_HALF = 0.7071067811865476
_INV_SQRT_2PI = 0.3989422804014327


def _pick(n, cands):
    for c in cands:
        if n % c == 0:
            return c
    return n


def _ceil_to(n, m):
    return -(-n // m) * m


def _params():
    return pltpu.CompilerParams(vmem_limit_bytes=VMEM_LIMIT_BYTES)


def _tile(n, cap):
    best = 0
    for d in range(LANE, min(n, cap) + 1, LANE):
        if n % d == 0:
            best = d
    return best or n


def _matmul_tiles(M, N, K, a_bytes, b_bytes, o_bytes, has_add, forced):
    tm = forced.get("m") or _tile(M, 1024)
    tn = forced.get("n") or _tile(N, 1024)
    tk = forced.get("k") or _tile(K, 2048)

    def vmem(tm, tn, tk):
        acc = tm * tn * 4 if tk < K else 0
        return 2 * (tm * tk * a_bytes + tk * tn * b_bytes + tm * tn * (o_bytes + (4 if has_add else 0))) + acc

    while vmem(tm, tn, tk) > (VMEM_LIMIT_BYTES * 3) // 4:
        if "k" not in forced and tk > 512 and _tile(K, tk // 2) < tk:
            tk = _tile(K, tk // 2)
        elif "m" not in forced and _tile(M, tm // 2) < tm:
            tm = _tile(M, tm // 2)
        else:
            break
    return tm, tn, tk


def _matmul(a, b, *, mode, out_dtype=F32, name, add=None, deps=(), out_blocks=0, epi=None):
    def view(x):
        return (x.shape[1], x.shape[0] * x.shape[2], x.shape[2]) if x.ndim == 3 else (x.shape[0], x.shape[1], 0)

    (ar, ac, aw), (br, bc, bw) = view(a), view(b)
    a_col, b_col = {"nn": ("k", "n"), "nt": ("k", "k"), "tn": ("m", "n")}[mode]
    if mode == "nn":
        M, K, K2, N = ar, ac, br, bc
    elif mode == "nt":
        M, K, N, K2 = ar, ac, br, bc
    else:
        K, M, K2, N = ar, ac, br, bc
    assert K == K2, (a.shape, b.shape, mode)
    forced = {}
    for dim, w in ((a_col, aw), (b_col, bw), ("n", N // out_blocks if out_blocks else 0)):
        if w:
            assert forced.get(dim, w) == w
            forced[dim] = w
    has_add = add is not None
    tile_bytes = (sum(jnp.dtype(d).itemsize for d in epi[2]) + sum((e[0] if isinstance(e, tuple) else e).dtype.itemsize for e in epi[1])
                  if epi is not None else jnp.dtype(out_dtype).itemsize)
    tm, tn, tk = _matmul_tiles(M, N, K, a.dtype.itemsize, b.dtype.itemsize, tile_bytes, has_add, forced)
    kb = 1
    if "k" in forced and mode != "tn":
        lanes_ok = all(w or tk % LANE == 0 for w in (aw, bw if mode == "nt" else 1))
        kb = next(c for c in (4, 2, 1) if (K // tk) % c == 0 and (c == 1 or (lanes_ok and c * tk <= 1536)))
    nk = K // (tk * kb)
    dn = {"nn": (((1,), (0,)), ((), ())), "nt": (((1,), (1,)), ((), ())), "tn": (((0,), (0,)), ((), ()))}[mode]
    pick = {"m": lambda i, j, k: i, "n": lambda i, j, k: j, "k": lambda i, j, k: k}
    size = {"m": tm, "n": tn, "k": tk}

    def spec(blocked, row_dim, col_dim):
        rf, cf = pick[row_dim], pick[col_dim]
        reps = {d: (kb if d == "k" else 1) for d in (row_dim, col_dim)}
        if blocked:
            lead = kb if col_dim == "k" and kb > 1 else None
            return pl.BlockSpec((lead, size[row_dim], size[col_dim]), lambda i, j, k: (cf(i, j, k), rf(i, j, k), 0))
        return pl.BlockSpec((size[row_dim] * reps[row_dim], size[col_dim] * reps[col_dim]), lambda i, j, k: (rf(i, j, k), cf(i, j, k)))

    def k_part(ref, blocked, k_on_rows, j):
        if kb == 1:
            return ref[...]
        if blocked:
            return ref[j]
        return ref[j * tk:(j + 1) * tk, :] if k_on_rows else ref[:, j * tk:(j + 1) * tk]

    a_spec = spec(aw, "k" if mode == "tn" else "m", a_col)
    b_spec = spec(bw, "n" if mode == "nt" else "k", b_col)
    o_spec = spec(out_blocks, "m", "n")
    epi_fn, epi_ins, epi_dtypes = epi if epi is not None else (None, [], [out_dtype])
    epi_ins = [e if isinstance(e, tuple) else (e, None) for e in epi_ins]
    n_epi = len(epi_ins)
    n_in = 2 + has_add + n_epi + len(deps)
    n_out = len(epi_dtypes)

    def body(*refs):
        a_ref, b_ref = refs[0], refs[1]
        add_ref = refs[2] if has_add else None
        epi_refs = refs[2 + has_add:2 + has_add + n_epi]
        o_refs = refs[n_in:n_in + n_out]
        part = None
        for q in range(kb):
            a_q = k_part(a_ref, aw and a_col == "k", False, q)
            b_q = k_part(b_ref, bw and b_col == "k", mode == "nn", q)
            prod = lax.dot_general(a_q.astype(BF16), b_q.astype(BF16), dn, preferred_element_type=F32)
            part = prod if part is None else part + prod

        def finish(res):
            outs = epi_fn(res, *[e[...] for e in epi_refs]) if epi_fn is not None else (res,)
            for o_ref, val in zip(o_refs, outs):
                o_ref[...] = val.astype(o_ref.dtype)

        if nk == 1:
            finish(part + add_ref[...] if has_add else part)
            return
        acc_ref = refs[-1]
        kk = pl.program_id(2)

        @pl.when(kk == 0)
        def _():
            acc_ref[...] = part + add_ref[...] if has_add else part

        @pl.when(kk > 0)
        def _():
            acc_ref[...] += part

        @pl.when(kk == nk - 1)
        def _():
            finish(acc_ref[...])

    def epi_spec(arr, off):
        if off is None:
            return o_spec
        assert off % tn == 0
        return pl.BlockSpec((tm, tn), lambda i, j, k: (i, j + off // tn))

    ins = [a, b] + ([add] if has_add else []) + [arr for arr, _ in epi_ins] + list(deps)
    in_specs = ([a_spec, b_spec] + ([o_spec] if has_add else []) + [epi_spec(arr, off) for arr, off in epi_ins]
                + [pl.BlockSpec(d.shape, lambda i, j, k, nd=d.ndim: (0,) * nd) for d in deps])
    o_shape = (out_blocks, M, tn) if out_blocks else (M, N)
    res = pl.pallas_call(
        body, name=name, grid=(M // tm, N // tn, nk), in_specs=in_specs, out_specs=[o_spec] * n_out,
        out_shape=[jax.ShapeDtypeStruct(o_shape, dt) for dt in epi_dtypes],
        scratch_shapes=[pltpu.VMEM((tm, tn), F32)] if nk > 1 else [],
        compiler_params=_params())(*ins)
    return res[0] if epi is None else list(res)


def _rowwise(fn, rows, pars, row_outs, acc_outs, *, name, tm=256, deps=()):
    rows = [r if isinstance(r, tuple) else (r, r.shape[1], 0) for r in rows]
    row_outs = [o if len(o) == 5 else (o[0], o[1], o[0], 0, None) for o in row_outs]
    aliased = [(k, o[4]) for k, o in enumerate(row_outs) if o[4] is not None]
    R = rows[0][0].shape[0]
    if max(w for _, w, _ in rows) > 4096:
        tm = tm // 2
    tm = min(tm, R)
    assert R % tm == 0
    nr, npar = len(rows), len(pars)
    nro = len(row_outs)
    n_in = nr + npar + len(deps) + len(aliased)

    def body(*refs):
        rv = [r[...] for r in refs[:nr]]
        pv = [p[...] for p in refs[nr:nr + npar]]
        outs = refs[n_in:]
        ro, ao = fn(rv, pv)
        first = pl.program_id(0) == 0
        for o_ref, val in zip(outs[:nro], ro):
            o_ref[...] = val.astype(o_ref.dtype)

        @pl.when(first)
        def _():
            for o_ref, val in zip(outs[nro:], ao):
                o_ref[...] = val

        @pl.when(jnp.logical_not(first))
        def _():
            for o_ref, val in zip(outs[nro:], ao):
                o_ref[...] += val

    in_specs = ([pl.BlockSpec((tm, w), lambda i, cb=cb: (i, cb)) for _, w, cb in rows]
                + [pl.BlockSpec(p.shape, lambda i, nd=p.ndim: (0,) * nd) for p in list(pars) + list(deps)]
                + [pl.BlockSpec(memory_space=pl.ANY)] * len(aliased))
    out_shape = ([jax.ShapeDtypeStruct((R, full), dt) for _, dt, full, _, _ in row_outs]
                 + [jax.ShapeDtypeStruct(s, F32) for s in acc_outs])
    out_specs = ([pl.BlockSpec((tm, f), lambda i, cb=cb: (i, cb)) for f, _, _, cb, _ in row_outs]
                 + [pl.BlockSpec(s, lambda i, nd=len(s): (0,) * nd) for s in acc_outs])
    res = pl.pallas_call(body, name=name, grid=(R // tm,), in_specs=in_specs, out_specs=out_specs, out_shape=out_shape,
                         input_output_aliases={n_in - len(aliased) + q: k for q, (k, _) in enumerate(aliased)},
                         compiler_params=_params())(*[r for r, _, _ in rows], *pars, *deps, *[buf for _, buf in aliased])
    return list(res)


def _bdot(a, b, mode="nn"):
    dn = {"nn": (((1,), (0,)), ((), ())), "nt": (((1,), (1,)), ((), ())), "tn": (((0,), (0,)), ((), ()))}[mode]
    return lax.dot_general(a.astype(BF16), b.astype(BF16), dn, preferred_element_type=F32)


def _sigmoid(x):
    return jax.nn.sigmoid(x)


def _softplus(x):
    return jnp.maximum(x, 0.0) + jnp.log1p(jnp.exp(-jnp.abs(x)))


def _gelu(z):
    return 0.5 * z * (1.0 + lax.erf(z * _SQRT_HALF))


def _gelu_grad(z):
    return 0.5 * (1.0 + lax.erf(z * _SQRT_HALF)) + z * jnp.exp(-0.5 * z * z) * _INV_SQRT_2PI


def _mean(x):
    return jnp.mean(x, axis=-1, keepdims=True)


def _colsum(x):
    return jnp.sum(x, axis=0, keepdims=True)


def _rms_fwd(x, g, name, deps=()):
    def fn(rv, pv):
        (xv,), (gv,) = rv, pv
        r = lax.rsqrt(_mean(xv * xv) + RMS_EPS)
        return [xv * r * gv], []
    return _rowwise(fn, [x], [g], [(x.shape[1], BF16)], [], name=name, deps=deps)[0]


def _rms_bwd(dn, x, dres, g, name, deps=()):
    def fn(rv, pv):
        (dnv, xv, drv), (gv,) = rv, pv
        r = lax.rsqrt(_mean(xv * xv) + RMS_EPS)
        yn = xv * r
        dyg = dnv * gv
        dx = drv + r * (dyg - yn * _mean(dyg * yn))
        return [dx, dx], [_colsum(dnv * yn)]
    D = x.shape[1]
    return _rowwise(fn, [dn, x, dres], [g], [(D, F32), (D, BF16)], [(1, D)], name=name, deps=deps)


def _rwkv_layout(RW, Lw, La, Lg, D):
    widths = [RW, RW, RW, Lw, La, Lg]
    pw = [_ceil_to(w, LANE) for w in widths]
    pw[5] += _ceil_to(sum(pw), 2 * D) - sum(pw)
    offs = [sum(pw[:i]) for i in range(6)]
    return widths, pw, offs, sum(pw)


def _pad_rwkv_cols(a, lay):
    widths, pw, _, _ = lay
    pieces, src = [], 0
    for w, p in zip(widths, pw):
        pieces.append(a[:, src:src + w])
        if p > w:
            pieces.append(jnp.zeros((a.shape[0], p - w), a.dtype))
        src += w
    return jnp.concatenate(pieces, axis=1)


def _unpad_rwkv_cols(a, lay):
    widths, _, offs, _ = lay
    return jnp.concatenate([a[:, o:o + w] for o, w in zip(offs, widths)], axis=1)


def _proj_pieces(lay, D, cs):
    widths, _, offs, rcp = lay
    rc = sum(widths)
    segs = [(sum(widths[:j]), widths[j], offs[j]) for j in range(6)] + [(rc, D, rcp + 2 * D), (rc + D, D, rcp), (rc + 2 * D, D, rcp + D)]
    pieces = []
    for start, width, dst in segs:
        n = start
        while n < start + width:
            d, off = divmod(n, cs)
            take = min(cs - off, start + width - n)
            pieces.append((d, off, dst + n - start, take))
            n += take
    return pieces


def _w_in_to_proj(g, lay, D, name):
    nb, rows, cs = g.shape
    icp = lay[3] + 3 * D
    pieces = _proj_pieces(lay, D, cs)
    tm = _pick(rows, (256, 128, 64, 32, 16))

    def body(i_ref, o_ref):
        o_ref[...] = jnp.zeros_like(o_ref)
        for d, src, dst, w in pieces:
            o_ref[:, dst:dst + w] = i_ref[d, :, src:src + w]

    return pl.pallas_call(
        body, name=name, grid=(rows // tm,), in_specs=[pl.BlockSpec((nb, tm, cs), lambda i: (0, i, 0))],
        out_specs=pl.BlockSpec((tm, icp), lambda i: (i, 0)), out_shape=jax.ShapeDtypeStruct((rows, icp), g.dtype),
        compiler_params=_params())(g)


def _dw_in_from_proj(a, lay, D, cs, name):
    rows, icp = a.shape
    pieces = _proj_pieces(lay, D, cs)
    tm = _pick(rows, (256, 128, 64, 32, 16))

    def body(i_ref, o_ref):
        for d, src, dst, w in pieces:
            o_ref[d, :, src:src + w] = i_ref[:, dst:dst + w]

    return pl.pallas_call(
        body, name=name, grid=(rows // tm,), in_specs=[pl.BlockSpec((tm, icp), lambda i: (i, 0))],
        out_specs=pl.BlockSpec((N_DEV, tm, cs), lambda i: (0, i, 0)), out_shape=jax.ShapeDtypeStruct((N_DEV, rows, cs), a.dtype),
        compiler_params=_params())(a)


def _pad_rows(a, rows):
    return a if a.shape[0] == rows else jnp.concatenate([a, jnp.zeros((rows - a.shape[0], a.shape[1]), a.dtype)], axis=0)


def _token_shift(p, halo, mu, i):
    tm = p.shape[0]
    hid = lax.broadcasted_iota(jnp.int32, (SUBLANE, 1), 0)
    before = jnp.sum(jnp.where(hid == SUBLANE - 1, halo, 0.0), axis=0, keepdims=True)
    before = jnp.where(i == 0, 0.0, before)
    rid = lax.broadcasted_iota(jnp.int32, (tm, 1), 0)
    prev = jnp.where(rid == 0, before, pltpu.roll(p, 1, 0))
    d = prev - p
    return p + d * mu, d


def _rwkv_math(ps, w0, a0, k_k, k_a, wlw, wla, wlg, lay):
    _, pw, offs, _ = lay
    r, k, v, xw, xa, xg = (ps[:, offs[j]:offs[j] + pw[j]] for j in range(6))
    tw = jnp.tanh(xw)
    ww = w0 + _bdot(tw, wlw)
    lw = -jnp.exp(-_softplus(-ww) - 0.5)
    a = _sigmoid(a0 + _bdot(xa, wla))
    sg = _sigmoid(xg)
    g = _bdot(sg, wlg)
    return dict(r=r, k=k, v=v, xa=xa, tw=tw, ww=ww, lw=lw, a=a, sg=sg, g=g, kkp=k * k_k, k2=k * (1.0 + (a - 1.0) * k_a))


def _halo_specs(T, tm, width, after):
    hb = tm // SUBLANE
    last = T // SUBLANE - 1
    if after:
        return pl.BlockSpec((SUBLANE, width), lambda i: (jnp.minimum((i + 1) * hb, last), 0))
    return pl.BlockSpec((SUBLANE, width), lambda i: (jnp.maximum(i * hb - 1, 0), 0))


def _rowsum(x):
    return jnp.sum(x, axis=-1, keepdims=True)


def _kk_math(kkp):
    nrm = jnp.sqrt(_rowsum(kkp * kkp))
    inv = 1.0 / jnp.maximum(nrm, 1e-12)
    return nrm, inv, kkp * inv


def _rwkv_pre(p, mu, small, lora, lay, name):
    T, rcp = p.shape[0], lay[3]
    H = lay[0][0] // HEAD
    tm = min(128, T)

    def body(p_ref, ph_ref, mu_ref, w0_ref, a0_ref, kk_ref, ka_ref, wlw_ref, wla_ref, wlg_ref, r_o, lw_o, k2_o, v_o, aa_o, bb_o, g_o):
        ps, _ = _token_shift(p_ref[...], ph_ref[...], mu_ref[...], pl.program_id(0))
        q = _rwkv_math(ps, w0_ref[...], a0_ref[...], kk_ref[...], ka_ref[...], wlw_ref[...], wla_ref[...], wlg_ref[...], lay)
        for h in range(H):
            sl = slice(h * HEAD, (h + 1) * HEAD)
            for o_ref, key in ((r_o, "r"), (lw_o, "lw"), (k2_o, "k2"), (v_o, "v"), (g_o, "g")):
                o_ref[h] = q[key][:, sl]
            _, _, kk = _kk_math(q["kkp"][:, sl])
            aa_o[h] = -kk
            bb_o[h] = kk * q["a"][:, sl]

    whole = lambda arr: pl.BlockSpec(arr.shape, lambda i: (0, 0))
    return pl.pallas_call(
        body, name=name, grid=(T // tm,),
        in_specs=([pl.BlockSpec((tm, rcp), lambda i: (i, 0)), _halo_specs(T, tm, rcp, False), whole(mu)]
                  + [whole(s) for s in small] + [whole(w) for w in lora]),
        out_specs=[pl.BlockSpec((H, tm, HEAD), lambda i: (0, i, 0))] * 7, out_shape=[jax.ShapeDtypeStruct((H, T, HEAD), F32)] * 7,
        compiler_params=_params())(p, p, mu, *small, *lora)


def _rwkv_pre_bwd(p, mu, small, lora, hgrads, lay, name):
    T, rcp = p.shape[0], lay[3]
    widths, pw, offs, _ = lay
    RW = widths[0]
    H = RW // HEAD
    tm = min(128, T)

    def body(p_ref, ph_ref, mu_ref, w0_ref, a0_ref, kk_ref, ka_ref, wlw_ref, wla_ref, wlg_ref,
             dr1, dr2, dk1, dk2b, dv1, dv2, dlw_h, daa, dbb, dg_h,
             dps_ref, dmu_ref, dw0_ref, da0_ref, dkk_ref, dka_ref, dwlw_ref, dwla_ref, dwlg_ref,
             s_dr, s_dk2, s_dv, s_dlw, s_dkkp, s_da, s_dg):
        i = pl.program_id(0)
        ps, dprev = _token_shift(p_ref[...], ph_ref[...], mu_ref[...], i)
        k_k, k_a = kk_ref[...], ka_ref[...]
        q = _rwkv_math(ps, w0_ref[...], a0_ref[...], k_k, k_a, wlw_ref[...], wla_ref[...], wlg_ref[...], lay)
        k, a, lw, ww, tw, sg = q["k"], q["a"], q["lw"], q["ww"], q["tw"], q["sg"]
        for h in range(H):
            sl = slice(h * HEAD, (h + 1) * HEAD)
            s_dr[:, sl] = dr1[h] + dr2[h]
            s_dk2[:, sl] = dk1[h] + dk2b[h]
            s_dv[:, sl] = dv1[h] + dv2[h]
            s_dlw[:, sl] = dlw_h[h]
            s_dg[:, sl] = dg_h[h]
            nrm, inv, kk = _kk_math(q["kkp"][:, sl])
            dbb_h = dbb[h]
            dkk = dbb_h * a[:, sl] - daa[h]
            s_dkkp[:, sl] = jnp.where(nrm > 1e-12, inv * (dkk - kk * _rowsum(dkk * kk)), dkk * inv)
            s_da[:, sl] = dbb_h * kk
        dk2, dkkp, dg = s_dk2[...], s_dkkp[...], s_dg[...]
        dk = dk2 * (1.0 + (a - 1.0) * k_a) + dkkp * k_k
        da = s_da[...] + dk2 * k * k_a
        dpa = da * a * (1.0 - a)
        dww = s_dlw[...] * lw * _sigmoid(-ww)
        dxa = _bdot(dpa, wla_ref[...], "nt")
        dxw = _bdot(dww, wlw_ref[...], "nt") * (1.0 - tw * tw)
        dxg = _bdot(dg, wlg_ref[...], "nt") * sg * (1.0 - sg)
        segs = (s_dr[...], dk, s_dv[...], dxw, dxa, dxg)
        sums = [dmu_ref, dw0_ref, da0_ref, dkk_ref, dka_ref, dwlw_ref, dwla_ref, dwlg_ref]

        @pl.when(i == 0)
        def _():
            for s in sums:
                s[...] = jnp.zeros_like(s)

        for j, seg in enumerate(segs):
            sl = slice(offs[j], offs[j] + pw[j])
            dps_ref[:, sl] = seg
            dmu_ref[:, sl] += _colsum(seg * dprev[:, sl])
        dw0_ref[...] += _colsum(dww)
        da0_ref[...] += _colsum(dpa)
        dkk_ref[...] += _colsum(dkkp * k)
        dka_ref[...] += _colsum(dk2 * k * (a - 1.0))
        dwlw_ref[...] += _bdot(tw, dww, "tn")
        dwla_ref[...] += _bdot(q["xa"], dpa, "tn")
        dwlg_ref[...] += _bdot(sg, dg, "tn")

    whole = lambda arr: pl.BlockSpec(arr.shape, lambda i: (0, 0))
    row = lambda w: pl.BlockSpec((tm, w), lambda i: (i, 0))
    acc_shapes = [(1, rcp), (1, RW), (1, RW), (1, RW), (1, RW)] + [w.shape for w in lora]
    return pl.pallas_call(
        body, name=name, grid=(T // tm,),
        in_specs=([row(rcp), _halo_specs(T, tm, rcp, False), whole(mu)] + [whole(s) for s in small] + [whole(w) for w in lora]
                  + [pl.BlockSpec((H, tm, HEAD), lambda i: (0, i, 0))] * 10),
        out_specs=[row(rcp)] + [pl.BlockSpec(s, lambda i: (0, 0)) for s in acc_shapes],
        out_shape=[jax.ShapeDtypeStruct((T, rcp), F32)] + [jax.ShapeDtypeStruct(s, F32) for s in acc_shapes],
        scratch_shapes=[pltpu.VMEM((tm, RW), F32)] * 7, compiler_params=_params())(p, p, mu, *small, *lora, *hgrads)


def _shift_bwd(dps, mu, dproj, name):
    T, rcp = dps.shape
    tm = min(256, T)
    nt = T // tm

    def body(d_ref, dh_ref, mu_ref, buf_ref, o_ref):
        i = pl.program_id(0)
        d = d_ref[...]
        hid = lax.broadcasted_iota(jnp.int32, (SUBLANE, 1), 0)
        after = jnp.sum(jnp.where(hid == 0, dh_ref[...], 0.0), axis=0, keepdims=True)
        after = jnp.where(i == nt - 1, 0.0, after)
        rid = lax.broadcasted_iota(jnp.int32, (tm, 1), 0)
        nxt = jnp.where(rid == tm - 1, after, pltpu.roll(d, tm - 1, 0))
        mu_v = mu_ref[...]
        o_ref[...] = (d * (1.0 - mu_v) + nxt * mu_v).astype(BF16)

    row = pl.BlockSpec((tm, rcp), lambda i: (i, 0))
    return pl.pallas_call(
        body, name=name, grid=(nt,),
        in_specs=[row, _halo_specs(T, tm, rcp, True), pl.BlockSpec(mu.shape, lambda i: (0, 0)), pl.BlockSpec(memory_space=pl.ANY)],
        out_specs=row, out_shape=jax.ShapeDtypeStruct(dproj.shape, BF16), input_output_aliases={3: 0},
        compiler_params=_params())(dps, dps, mu, dproj)


def _head_post_math(y, r, k2, v, lg, lb, rk):
    yc = y - _mean(y)
    rstd = lax.rsqrt(_mean(yc * yc) + LNX_EPS)
    yn = yc * rstd
    s = _rowsum(r * k2 * rk)
    return yn, rstd, yn * lg + lb + s * v, s


def _head_post(y, r, k2, v, g, hp, name, deps=()):
    H, T, _ = y.shape
    tm = min(128, T)

    def body(y_ref, r_ref, k_ref, v_ref, g_ref, lg_ref, lb_ref, rk_ref, *rest):
        o_ref = rest[-1]
        _, _, t, _ = _head_post_math(y_ref[...], r_ref[...], k_ref[...], v_ref[...], lg_ref[...], lb_ref[...], rk_ref[...])
        out = (t * g_ref[...]).astype(BF16)
        for h in range(H):
            o_ref[:, h * HEAD:(h + 1) * HEAD] = out[h]

    blk = pl.BlockSpec((H, tm, HEAD), lambda i: (0, i, 0))
    par = pl.BlockSpec((H, 1, HEAD), lambda i: (0, 0, 0))
    return pl.pallas_call(
        body, name=name, grid=(T // tm,),
        in_specs=[blk] * 5 + [par] * 3 + [pl.BlockSpec(d.shape, lambda i, nd=d.ndim: (0,) * nd) for d in deps],
        out_specs=pl.BlockSpec((tm, H * HEAD), lambda i: (i, 0)),
        out_shape=jax.ShapeDtypeStruct((T, H * HEAD), BF16), compiler_params=_params())(y, r, k2, v, g, *hp, *deps)


def _head_post_bwd(dya, y, r, k2, v, g, hp, name, deps=()):
    H, T, _ = y.shape
    tm = min(128, T)
    hsum = lambda t: jnp.sum(t, axis=1, keepdims=True)

    def body(d_ref, y_ref, r_ref, k_ref, v_ref, g_ref, lg_ref, lb_ref, rk_ref, *rest):
        outs, d_s = rest[len(deps):len(deps) + 8], rest[-1]
        for h in range(H):
            d_s[h] = d_ref[:, h * HEAD:(h + 1) * HEAD]
        d_v, r_v, k_v, v_v, lg, rk = d_s[...], r_ref[...], k_ref[...], v_ref[...], lg_ref[...], rk_ref[...]
        yn, rstd, t, s = _head_post_math(y_ref[...], r_v, k_v, v_v, lg, lb_ref[...], rk)
        dyo = d_v * g_ref[...]
        dyn = dyo * lg
        ds = _rowsum(dyo * v_v)
        vals = (rstd * (dyn - _mean(dyn) - yn * _mean(dyn * yn)), ds * k_v * rk, ds * r_v * rk, dyo * s, d_v * t)
        for o_ref, val in zip(outs[:5], vals):
            o_ref[...] = val
        sums = (hsum(dyo * yn), hsum(dyo), hsum(ds * r_v * k_v))
        first = pl.program_id(0) == 0

        @pl.when(first)
        def _():
            for o_ref, val in zip(outs[5:], sums):
                o_ref[...] = val

        @pl.when(jnp.logical_not(first))
        def _():
            for o_ref, val in zip(outs[5:], sums):
                o_ref[...] += val

    blk = pl.BlockSpec((H, tm, HEAD), lambda i: (0, i, 0))
    par = pl.BlockSpec((H, 1, HEAD), lambda i: (0, 0, 0))
    return pl.pallas_call(
        body, name=name, grid=(T // tm,),
        in_specs=([pl.BlockSpec((tm, H * HEAD), lambda i: (i, 0))] + [blk] * 5 + [par] * 3
                  + [pl.BlockSpec(d.shape, lambda i, nd=d.ndim: (0,) * nd) for d in deps]),
        out_specs=[blk] * 5 + [par] * 3,
        out_shape=[jax.ShapeDtypeStruct((H, T, HEAD), F32)] * 5 + [jax.ShapeDtypeStruct((H, 1, HEAD), F32)] * 3,
        scratch_shapes=[pltpu.VMEM((H, tm, HEAD), F32)], compiler_params=_params())(dya, y, r, k2, v, g, *hp, *deps)


def _bmm(x, y, mode):
    dn = {"nn": (((2,), (1,)), ((0,), (0,))), "nt": (((2,), (2,)), ((0,), (0,))), "tn": (((1,), (1,)), ((0,), (0,)))}[mode]
    (xh, xl), (yh, yl) = _split(x), _split(y)
    dot = lambda p, q: lax.dot_general(p, q, dn, preferred_element_type=F32)
    out = dot(xh, yh)
    if yl is not None:
        out = out + dot(xh, yl)
    if xl is not None:
        out = out + dot(xl, yh)
    return out


def _split(x):
    if isinstance(x, tuple):
        return x
    hi = x.astype(BF16)
    return hi, (x - hi.astype(F32)).astype(BF16)


def _exact(x):
    return x.astype(BF16), None


def _round(x):
    return x if isinstance(x, tuple) else (x.astype(BF16), None)


def _rows(*xs):
    if isinstance(xs[0], tuple):
        return tuple(None if any(p is None for p in parts) else jnp.concatenate(parts, axis=1) for parts in zip(*xs))
    return jnp.concatenate(xs, axis=1)


def _wkv_chunk(r, lw, k, v, a, b):
    hb, C, _ = r.shape
    ti = lax.broadcasted_iota(jnp.int32, (C, C), 0)
    si = lax.broadcasted_iota(jnp.int32, (C, C), 1)
    linc, lstr, eye = (ti >= si).astype(F32), (ti > si).astype(F32), (ti == si).astype(F32)
    qmask = jnp.concatenate([jnp.concatenate([lstr, lstr], axis=1), jnp.concatenate([linc, linc], axis=1)], axis=0)
    lincb = _exact(jnp.broadcast_to(linc, (hb, C, C)))
    both = _exact(jnp.broadcast_to(jnp.concatenate([linc, lstr], axis=0), (hb, 2 * C, C)))
    ones = _exact(jnp.ones_like(v))
    lws = _split(lw)
    ci = _bmm(lincb, lws, "nn")
    cC = jnp.sum(lw, axis=1, keepdims=True)
    gi, ge, gn, gr = jnp.exp(ci), jnp.exp(ci - lw), jnp.exp(-ci), jnp.exp(cC - ci)
    q = dict(At=a * ge, Rt=r * gi, Bt=b * gn, Kt=k * gn, Bh=b * gr, Kh=k * gr)
    s = dict(AR=_round(_rows(q["At"], q["Rt"])), BK=_round(_rows(q["Bt"], q["Kt"])), BKh=_round(_rows(q["Bh"], q["Kh"])), v=_round(v))
    quad = _bmm(s["AR"], s["BK"], "nt") * qmask
    s["top"], s["bot"] = _round(quad[:, :C]), _round(quad[:, C:])
    A_ab = quad[:, :C, :C]
    Tm = eye + A_ab
    Pw = _round(A_ab)
    n = 1
    while 2 * n < C:
        Pw = _round(_bmm(Pw, Pw, "nn"))
        Tm = Tm + _bmm(_round(Tm), Pw, "nn")
        n *= 2
    s["Tm"] = _round(Tm)
    gC = jnp.exp(_bmm(lws, ones, "tn"))
    q.update(gi=gi, ge=ge, gn=gn, gr=gr, qmask=qmask, both=both, gC=gC, ones=ones, s=s)
    return q


def _wkv_u(s, H0s, C):
    arh = _bmm(s["AR"], H0s, "nn")
    zv = _rows(tuple(None if p is None else jnp.zeros_like(p) for p in s["v"]), s["v"])
    U = _bmm(s["Tm"], _round(arh[:, :C] + _bmm(s["top"], zv, "nn")), "nn")
    return arh, _rows(_round(U), s["v"])


def _wkv_fwd(r, lw, k, v, a, b, name):
    H, T, N = r.shape
    C = min(WKV_CHUNK, T)
    nc = T // C
    hb = _pick(H, (16, 8, 4, 2))

    def body(r_ref, lw_ref, k_ref, v_ref, a_ref, b_ref, y_ref, st_ref, h_ref):
        @pl.when(pl.program_id(1) == 0)
        def _():
            h_ref[...] = jnp.zeros_like(h_ref)

        H0 = h_ref[...]
        st_ref[0] = H0
        q = _wkv_chunk(r_ref[...], lw_ref[...], k_ref[...], v_ref[...], a_ref[...], b_ref[...])
        s = q["s"]
        arh, UV = _wkv_u(s, _round(H0), C)
        y_ref[...] = arh[:, C:] + _bmm(s["bot"], UV, "nn")
        h_ref[...] = q["gC"] * H0 + _bmm(s["BKh"], UV, "tn")

    blk = pl.BlockSpec((hb, C, N), lambda h, c: (h, c, 0))
    return pl.pallas_call(
        body, name=name, grid=(H // hb, nc), in_specs=[blk] * 6,
        out_specs=[blk, pl.BlockSpec((1, hb, N, N), lambda h, c: (c, h, 0, 0))],
        out_shape=[jax.ShapeDtypeStruct((H, T, N), F32), jax.ShapeDtypeStruct((nc, H, N, N), F32)],
        scratch_shapes=[pltpu.VMEM((hb, N, N), F32)], compiler_params=_params())(r, lw, k, v, a, b)


def _wkv_bwd(r, lw, k, v, a, b, states, dy, name):
    H, T, N = r.shape
    C = min(WKV_CHUNK, T)
    nc = T // C
    hb = _pick(H, (16, 8, 4, 2))

    def body(r_ref, lw_ref, k_ref, v_ref, a_ref, b_ref, st_ref, dy_ref, dr_ref, dlw_ref, dk_ref, dv_ref, da_ref, db_ref, dh_ref):
        @pl.when(pl.program_id(1) == 0)
        def _():
            dh_ref[...] = jnp.zeros_like(dh_ref)

        dHC = dh_ref[...]
        H0 = st_ref[0]
        q = _wkv_chunk(r_ref[...], lw_ref[...], k_ref[...], v_ref[...], a_ref[...], b_ref[...])
        s, gC = q["s"], q["gC"]
        H0s, dHs, dY = _round(H0), _round(dHC), _round(dy_ref[...])
        _, UV = _wkv_u(s, H0s, C)
        bot_dy = _bmm(s["bot"], dY, "tn")
        bkh_dh = _bmm(s["BKh"], dHs, "nn")
        dP = _round(_bmm(s["Tm"], _round(bot_dy[:, :C] + bkh_dh[:, :C]), "tn"))
        dv_ref[...] = bot_dy[:, C:] + bkh_dh[:, C:] + _bmm(s["top"], dP, "tn")[:, C:]
        dPY = _rows(dP, dY)
        dh_ref[...] = gC * dHC + _bmm(s["AR"], dPY, "tn")
        dquad = _round(_bmm(dPY, UV, "nt") * q["qmask"])
        dAR = _bmm(dPY, H0s, "nt") + _bmm(dquad, s["BK"], "nn")
        dBK = _bmm(dquad, s["AR"], "tn")
        dBKh = _bmm(UV, dHs, "nt")
        dAt, dRt, dBt, dKt, dBh, dKh = dAR[:, :C], dAR[:, C:], dBK[:, :C], dBK[:, C:], dBKh[:, :C], dBKh[:, C:]
        dr_ref[...] = dRt * q["gi"]
        da_ref[...] = dAt * q["ge"]
        db_ref[...] = dBt * q["gn"] + dBh * q["gr"]
        dk_ref[...] = dKt * q["gn"] + dKh * q["gr"]
        tail = dBh * q["Bh"] + dKh * q["Kh"]
        dci = dRt * q["Rt"] - dBt * q["Bt"] - dKt * q["Kt"] - tail
        dcC = jnp.sum(tail, axis=1, keepdims=True) + _bmm(q["ones"], H0 * dHC * gC, "nt")
        dlw_ref[...] = _bmm(q["both"], _rows(dci, dAt * q["At"]), "tn") + dcC

    blk = pl.BlockSpec((hb, C, N), lambda h, c: (h, nc - 1 - c, 0))
    st = pl.BlockSpec((1, hb, N, N), lambda h, c: (nc - 1 - c, h, 0, 0))
    return pl.pallas_call(
        body, name=name, grid=(H // hb, nc), in_specs=[blk] * 6 + [st, blk], out_specs=[blk] * 6,
        out_shape=[jax.ShapeDtypeStruct((H, T, N), F32)] * 6,
        scratch_shapes=[pltpu.VMEM((hb, N, N), F32)], compiler_params=_params())(r, lw, k, v, a, b, states, dy)


def _sgu_ln(z, SW, lng, lnb):
    ge = _gelu(z)
    u, vv = ge[:, :SW], ge[:, SW:]
    xc = vv - _mean(vv)
    rstd = lax.rsqrt(_mean(xc * xc) + LN_EPS)
    vn = xc * rstd
    return u, vn, rstd, vn * lng + lnb


def _causal(ws_ref, g):
    ti = lax.broadcasted_iota(jnp.int32, (SGU_CHUNK, SGU_CHUNK), 0)
    si = lax.broadcasted_iota(jnp.int32, (SGU_CHUNK, SGU_CHUNK), 1)
    return ti >= si, jnp.where(ti >= si, ws_ref[g], 0.0).astype(BF16)


def _sgu_fwd(proj, zblock, lng, lnb, ws, bexp, name):
    T, SW = proj.shape[0], lng.shape[1]
    G = ws.shape[0]
    tr = min(256, T)
    nch = tr // SGU_CHUNK

    def body(z_ref, lng_ref, lnb_ref, ws_ref, be_ref, o_ref):
        u, _, _, vl = _sgu_ln(z_ref[...], SW, lng_ref[...], lnb_ref[...])
        for g in range(G):
            cs = slice(g * SGU_GROUP, (g + 1) * SGU_GROUP)
            _, wc = _causal(ws_ref, g)
            for n in range(nch):
                rs = slice(n * SGU_CHUNK, (n + 1) * SGU_CHUNK)
                m = jnp.dot(wc, vl[rs, cs].astype(BF16), preferred_element_type=F32) + be_ref[:, cs]
                o_ref[rs, cs] = (u[rs, cs] * m).astype(BF16)

    whole = lambda arr: pl.BlockSpec(arr.shape, lambda i, nd=arr.ndim: (0,) * nd)
    return pl.pallas_call(
        body, name=name, grid=(T // tr,),
        in_specs=[pl.BlockSpec((tr, 2 * SW), lambda i: (i, zblock)), whole(lng), whole(lnb), whole(ws), whole(bexp)],
        out_specs=pl.BlockSpec((tr, SW), lambda i: (i, 0)), out_shape=jax.ShapeDtypeStruct((T, SW), BF16),
        compiler_params=_params())(proj, lng, lnb, ws, bexp)


def _sgu_bwd(proj, zblock, dyb, lng, lnb, ws, bexp, dproj, name):
    T, SW = proj.shape[0], lng.shape[1]
    G = ws.shape[0]
    tr = min(256, T)
    nch = tr // SGU_CHUNK
    nt = T // tr

    def body(z_ref, dy_ref, lng_ref, lnb_ref, ws_ref, be_ref, buf_ref, dz_ref, dlg_ref, dlb_ref, dws_ref, db_ref, du_s, dvl_s, dbacc_s):
        i = pl.program_id(0)
        zv = z_ref[...]
        lng_v = lng_ref[...]
        u, vn, rstd, vl = _sgu_ln(zv, SW, lng_v, lnb_ref[...])

        @pl.when(i == 0)
        def _():
            for s in (dlg_ref, dlb_ref, dws_ref, dbacc_s):
                s[...] = jnp.zeros_like(s)

        for g in range(G):
            cs = slice(g * SGU_GROUP, (g + 1) * SGU_GROUP)
            tri, wc = _causal(ws_ref, g)
            for n in range(nch):
                rs = slice(n * SGU_CHUNK, (n + 1) * SGU_CHUNK)
                blk = vl[rs, cs].astype(BF16)
                m = jnp.dot(wc, blk, preferred_element_type=F32) + be_ref[:, cs]
                dyv = dy_ref[rs, cs]
                du_s[rs, cs] = dyv * m
                dm = dyv * u[rs, cs]
                dvl_s[rs, cs] = _bdot(wc, dm, "tn")
                dws_ref[g] += jnp.where(tri, _bdot(dm, blk, "nt"), 0.0)
                dbacc_s[:, cs] += dm

        dvl = dvl_s[...]
        dlg_ref[...] += _colsum(dvl * vn)
        dlb_ref[...] += _colsum(dvl)
        dvn = dvl * lng_v
        dvv = rstd * (dvn - _mean(dvn) - vn * _mean(dvn * vn))
        gp = _gelu_grad(zv)
        dz_ref[:, :SW] = (du_s[...] * gp[:, :SW]).astype(BF16)
        dz_ref[:, SW:] = (dvv * gp[:, SW:]).astype(BF16)

        @pl.when(i == nt - 1)
        def _():
            lane = lax.broadcasted_iota(jnp.int32, (SGU_CHUNK, LANE), 1)
            out = jnp.zeros((SGU_CHUNK, LANE), F32)
            for g in range(G):
                col = jnp.sum(dbacc_s[:, g * SGU_GROUP:(g + 1) * SGU_GROUP], axis=1, keepdims=True)
                out = jnp.where(lane == g, col, out)
            db_ref[...] = out

    whole = lambda arr: pl.BlockSpec(arr.shape, lambda i, nd=arr.ndim: (0,) * nd)
    acc_shapes = [(1, SW), (1, SW), ws.shape, (SGU_CHUNK, LANE)]
    return pl.pallas_call(
        body, name=name, grid=(nt,),
        in_specs=[pl.BlockSpec((tr, 2 * SW), lambda i: (i, zblock)), pl.BlockSpec((tr, SW), lambda i: (i, 0)),
                  whole(lng), whole(lnb), whole(ws), whole(bexp), pl.BlockSpec(memory_space=pl.ANY)],
        out_specs=([pl.BlockSpec((tr, 2 * SW), lambda i: (i, zblock))]
                   + [pl.BlockSpec(s, lambda i, nd=len(s): (0,) * nd) for s in acc_shapes]),
        out_shape=[jax.ShapeDtypeStruct(dproj.shape, BF16)] + [jax.ShapeDtypeStruct(s, F32) for s in acc_shapes],
        scratch_shapes=[pltpu.VMEM((tr, SW), F32), pltpu.VMEM((tr, SW), F32), pltpu.VMEM((SGU_CHUNK, SW), F32)],
        input_output_aliases={6: 0}, compiler_params=_params())(proj, dyb, lng, lnb, ws, bexp, dproj)


_HBM = pl.BlockSpec(memory_space=pltpu.HBM)
_SEM = pl.BlockSpec(memory_space=pltpu.SEMAPHORE)
_DATAFLOW = pltpu.SideEffectType.DATAFLOW_SIDE_EFFECTING


def _mesh_place(chips=False):
    x, y, c = lax.axis_index("x"), lax.axis_index("y"), lax.axis_index("c")
    return x, y, c, (2 * x + y if chips else 4 * x + 2 * y + c)


def _peer(x, y, c, rel, chips=False):
    px = 1 - x if rel & 4 else x
    py = 1 - y if rel & 2 else y
    pc = 1 - c if rel & 1 else c
    return (px, py, pc), (2 * px + py if chips else 4 * px + 2 * py + pc)


ALL_PEERS = tuple(range(1, N_DEV))
SIBLING = (1,)
SAME_CORE = (2, 4, 6)
SIBLINGS_CORE = (3, 5, 7)


def _exchange_start(groups, name, rels=ALL_PEERS, chips=False):
    flat = [t for g in groups for t in g]
    sizes = [len(g) for g in groups]
    n, ng = len(flat), len(groups)
    srcs = [pltpu.with_memory_space_constraint(a, pltpu.HBM) for a, _ in flat]
    lands = [pltpu.with_memory_space_constraint(lax.empty(((N_DEV,) + a.shape) if isg else a.shape, a.dtype), pltpu.HBM)
             for a, isg in flat]

    def body(*refs):
        ins, lnd, sems, token = refs[:n], refs[n:2 * n], refs[2 * n:2 * n + 3 * ng], refs[-1]
        x, y, c, me = _mesh_place(chips)
        j0 = 0
        for gi, sz in enumerate(sizes):
            for rel in rels:
                dev, slot = _peer(x, y, c, rel, chips)
                for jj in range(sz):
                    j = j0 + jj
                    pltpu.make_async_remote_copy(
                        src_ref=ins[j] if flat[j][1] else ins[j].at[slot], dst_ref=lnd[j].at[me],
                        send_sem=sems[3 * gi].at[jj * (N_DEV - 1) + rel - 1], recv_sem=sems[3 * gi + 1].at[jj * (N_DEV - 1) + rel - 1],
                        device_id=dev, device_id_type=pl.DeviceIdType.MESH).start()
            for jj in range(sz):
                j = j0 + jj
                pltpu.make_async_copy(ins[j] if flat[j][1] else ins[j].at[me], lnd[j].at[me], sems[3 * gi + 2].at[jj]).start()
            j0 += sz
        token[...] = jnp.zeros_like(token)

    sem_shapes = [pltpu.SemaphoreType.DMA((k,)) for sz in sizes for k in (sz * (N_DEV - 1), sz * (N_DEV - 1), sz)]
    res = pl.pallas_call(
        body, name=name,
        out_shape=(*sem_shapes, *[pltpu.HBM(a.shape, a.dtype) for a in srcs], *[pltpu.HBM(a.shape, a.dtype) for a in lands],
                   jax.ShapeDtypeStruct((SUBLANE, LANE), F32)),
        in_specs=[_HBM] * (2 * n), out_specs=(*[_SEM] * (3 * ng), *[_HBM] * (2 * n), pl.BlockSpec(memory_space=pltpu.VMEM)),
        input_output_aliases={i: 3 * ng + i for i in range(2 * n)},
        compiler_params=pltpu.CompilerParams(has_side_effects=_DATAFLOW))(*srcs, *lands)
    sems, thru, token = res[:3 * ng], res[3 * ng:3 * ng + 2 * n], res[-1]
    handle, j0 = [], 0
    for gi, sz in enumerate(sizes):
        handle.append(dict(kinds=[k for _, k in groups[gi]], chips=chips, srcs=list(thru[j0:j0 + sz]), lands=list(thru[n + j0:n + j0 + sz]),
                           sems=list(sems[3 * gi:3 * gi + 3])))
        j0 += sz
    return handle, token


def _exchange_wait(group, after, name, rels=ALL_PEERS, local=True):
    kinds, sz = group["kinds"], len(group["kinds"])
    relay = group.get("relay", [])

    def body(*refs):
        ins, lnd, (ssem, rsem, lsem) = refs[:sz], refs[sz:2 * sz], refs[2 * sz:2 * sz + 3]
        x, y, c, me = _mesh_place(group["chips"])
        for rel in rels:
            dev, slot = _peer(x, y, c, rel, group["chips"])
            for jj in range(sz):
                cp = pltpu.make_async_remote_copy(
                    src_ref=ins[jj] if kinds[jj] else ins[jj].at[slot], dst_ref=lnd[jj].at[slot],
                    send_sem=ssem.at[jj * (N_DEV - 1) + rel - 1], recv_sem=rsem.at[jj * (N_DEV - 1) + rel - 1],
                    device_id=dev, device_id_type=pl.DeviceIdType.MESH)
                cp.wait_send()
                cp.wait_recv()
        if local:
            for jj in range(sz):
                pltpu.make_async_copy(ins[jj] if kinds[jj] else ins[jj].at[me], lnd[jj].at[me], lsem.at[jj]).wait()
        if relay:
            fsend, frecv = refs[2 * sz + 3:2 * sz + 5]
            dev = _peer(x, y, c, 1)[0]
            for q, (mine, theirs) in enumerate(zip(SAME_CORE, SIBLINGS_CORE)):
                for jj in range(sz):
                    cp = pltpu.make_async_remote_copy(
                        src_ref=lnd[jj].at[_peer(x, y, c, mine)[1]], dst_ref=lnd[jj].at[_peer(x, y, c, theirs)[1]],
                        send_sem=fsend.at[jj * len(SAME_CORE) + q], recv_sem=frecv.at[jj * len(SAME_CORE) + q],
                        device_id=dev, device_id_type=pl.DeviceIdType.MESH)
                    cp.wait_send()
                    cp.wait_recv()

    arrays = group["srcs"] + group["lands"]
    sems = group["sems"] + relay
    res = pl.pallas_call(
        body, name=name, out_shape=[pltpu.HBM(a.shape, a.dtype) for a in arrays],
        in_specs=[_HBM] * (2 * sz) + [_SEM] * len(sems) + [pl.BlockSpec(memory_space=pl.ANY)], out_specs=[_HBM] * (2 * sz),
        input_output_aliases={i: i for i in range(2 * sz)},
        compiler_params=pltpu.CompilerParams(has_side_effects=_DATAFLOW))(*arrays, *sems, after)
    return dict(group, srcs=list(res[:sz]), lands=list(res[sz:]), relay=[])


def _relay_start(group, name):
    sz = len(group["kinds"])
    nq = len(SAME_CORE)

    def body(*refs):
        lnd, fsend, frecv, token = refs[:sz], refs[sz], refs[sz + 1], refs[-1]
        x, y, c, _ = _mesh_place()
        dev = _peer(x, y, c, 1)[0]
        for q, rel in enumerate(SAME_CORE):
            slot = _peer(x, y, c, rel)[1]
            for jj in range(sz):
                pltpu.make_async_remote_copy(
                    src_ref=lnd[jj].at[slot], dst_ref=lnd[jj].at[slot], send_sem=fsend.at[jj * nq + q], recv_sem=frecv.at[jj * nq + q],
                    device_id=dev, device_id_type=pl.DeviceIdType.MESH).start()
        token[...] = jnp.zeros_like(token)

    lands = group["lands"]
    res = pl.pallas_call(
        body, name=name,
        out_shape=(pltpu.SemaphoreType.DMA((sz * nq,)), pltpu.SemaphoreType.DMA((sz * nq,)), *[pltpu.HBM(a.shape, a.dtype) for a in lands],
                   jax.ShapeDtypeStruct((SUBLANE, LANE), F32)),
        in_specs=[_HBM] * sz, out_specs=(_SEM, _SEM, *[_HBM] * sz, pl.BlockSpec(memory_space=pltpu.VMEM)),
        input_output_aliases={i: 2 + i for i in range(sz)},
        compiler_params=pltpu.CompilerParams(has_side_effects=_DATAFLOW))(*lands)
    return dict(group, lands=list(res[2:2 + sz]), relay=[res[0], res[1]]), res[-1]


def _sibling_swap(arrays, handle, after, name):
    start = handle is None
    n = len(arrays) if start else len(handle["srcs"])
    chips = N_DEV // 2
    if start:
        srcs = [pltpu.with_memory_space_constraint(a.reshape(chips, 2, *a.shape[1:]), pltpu.HBM) for a in arrays]
        lands = [pltpu.with_memory_space_constraint(lax.empty((chips,) + a.shape[1:], a.dtype), pltpu.HBM) for a in arrays]
    else:
        srcs, lands = handle["srcs"], handle["lands"]

    def body(*refs):
        ins, lnd, ssem, rsem = refs[:n], refs[n:2 * n], refs[2 * n], refs[2 * n + 1]
        x, y, c, _ = _mesh_place()
        dev = _peer(x, y, c, 1)[0]
        for q in range(chips):
            for j in range(n):
                cp = pltpu.make_async_remote_copy(
                    src_ref=ins[j].at[q, 1 - c], dst_ref=lnd[j].at[q], send_sem=ssem.at[j * chips + q], recv_sem=rsem.at[j * chips + q],
                    device_id=dev, device_id_type=pl.DeviceIdType.MESH)
                if start:
                    cp.start()
                else:
                    cp.wait_send()
                    cp.wait_recv()
        if start:
            refs[-1][...] = jnp.zeros_like(refs[-1])

    thru = [pltpu.HBM(a.shape, a.dtype) for a in srcs + lands]
    effect = pltpu.CompilerParams(has_side_effects=_DATAFLOW)
    if start:
        res = pl.pallas_call(
            body, name=name, out_shape=(pltpu.SemaphoreType.DMA((n * chips,)), pltpu.SemaphoreType.DMA((n * chips,)), *thru,
                                        jax.ShapeDtypeStruct((SUBLANE, LANE), F32)),
            in_specs=[_HBM] * (2 * n), out_specs=(_SEM, _SEM, *[_HBM] * (2 * n), pl.BlockSpec(memory_space=pltpu.VMEM)),
            input_output_aliases={i: 2 + i for i in range(2 * n)}, compiler_params=effect)(*srcs, *lands)
        return dict(srcs=list(res[2:2 + n]), lands=list(res[2 + n:2 + 2 * n]), sems=[res[0], res[1]]), res[-1]
    res = pl.pallas_call(
        body, name=name, out_shape=thru, in_specs=[_HBM] * (2 * n) + [_SEM, _SEM, pl.BlockSpec(memory_space=pl.ANY)],
        out_specs=[_HBM] * (2 * n), input_output_aliases={i: i for i in range(2 * n)}, compiler_params=effect)(
            *srcs, *lands, *handle["sems"], after)
    return dict(handle, srcs=list(res[:n]), lands=list(res[n:]))


def _pair_add(mine, theirs, core, name):
    chips, _, rows, w = mine.shape
    tm = _pick(rows, (256, 128, 64, 32, 16))

    def body(core_ref, a_ref, b_ref, o_ref):
        o_ref[...] = (a_ref[...].astype(F32) + b_ref[...].astype(F32)).astype(o_ref.dtype)

    return pl.pallas_call(
        body, name=name, out_shape=jax.ShapeDtypeStruct(theirs.shape, theirs.dtype),
        grid_spec=pltpu.PrefetchScalarGridSpec(
            num_scalar_prefetch=1, grid=(chips, rows // tm),
            in_specs=[pl.BlockSpec((None, None, tm, w), lambda q, i, core_ref: (q, core_ref[0], i, 0)),
                      pl.BlockSpec((None, tm, w), lambda q, i, core_ref: (q, i, 0))],
            out_specs=pl.BlockSpec((None, tm, w), lambda q, i, core_ref: (q, i, 0))),
        compiler_params=_params())(core, mine, theirs)


def _adamw(w, m, v, gparts, name, after=None):
    R, C = w.shape
    tm = _pick(R, (256, 128, 64, 32, 16, 8))
    order = [] if after is None else [after]

    def body(w_ref, m_ref, v_ref, g_ref, *rest):
        go, do, mo, vo = rest[len(order):]
        g = g_ref[0].astype(F32)
        for j in range(1, gparts.shape[0]):
            g = g + g_ref[j].astype(F32)
        mn = ADAM_B1 * m_ref[...] + (1.0 - ADAM_B1) * g
        vn = ADAM_B2 * v_ref[...] + (1.0 - ADAM_B2) * (g * g)
        m_hat = mn / (1.0 - ADAM_B1 ** ADAM_STEP)
        v_hat = vn / (1.0 - ADAM_B2 ** ADAM_STEP)
        go[...] = g
        do[...] = -ADAM_LR * (m_hat / (jnp.sqrt(v_hat) + ADAM_EPS) + ADAM_WD * w_ref[...])
        mo[...] = mn
        vo[...] = vn

    row = pl.BlockSpec((tm, C), lambda i: (i, 0))
    return pl.pallas_call(
        body, name=name, grid=(R // tm,),
        in_specs=[row, row, row, pl.BlockSpec((gparts.shape[0], tm, C), lambda i: (0, i, 0))] + [pl.BlockSpec(memory_space=pl.ANY)] * len(order),
        out_specs=[row] * 4, out_shape=[jax.ShapeDtypeStruct((R, C), F32)] * 4, compiler_params=_params())(w, m, v, gparts, *order)


def _pack(arrays):
    parts = []
    for a in arrays:
        f = a.reshape(1, -1)
        pad = _ceil_to(f.shape[1], SUBLANE * LANE) - f.shape[1]
        f = jnp.concatenate([f, jnp.zeros((1, pad), f.dtype)], axis=1) if pad else f
        parts.append(f.reshape(-1, LANE))
    rows = sum(p.shape[0] for p in parts)
    pad = _ceil_to(rows, 64) - rows
    return jnp.concatenate(parts + ([jnp.zeros((pad, LANE), parts[0].dtype)] if pad else []), axis=0)


def _unpack(buf, shapes):
    out, row = [], 0
    for s in shapes:
        size = 1
        for d in s:
            size *= d
        rows = _ceil_to(size, SUBLANE * LANE) // LANE
        out.append(buf[row:row + rows].reshape(1, -1)[:, :size].reshape(s))
        row += rows
    return out


def kernel(x, norm_mix_g, w_in, shift_mu, w0, w_lora_up, a0, a_lora_up, g_lora_up, k_k, k_a, r_k, lnx_g, lnx_b, w_proj_rwkv, sgu_ln_g, sgu_ln_b, sgu_w, sgu_b, w_proj_sgu, w_out, norm_ffn_g, w_ffn_gate, w_ffn_up, w_ffn_down, norm_final_g, loss_target, m_norm_mix_g, m_w_in, m_shift_mu, m_w0, m_w_lora_up, m_a0, m_a_lora_up, m_g_lora_up, m_k_k, m_k_a, m_r_k, m_lnx_g, m_lnx_b, m_w_proj_rwkv, m_sgu_ln_g, m_sgu_ln_b, m_sgu_w, m_sgu_b, m_w_proj_sgu, m_w_out, m_norm_ffn_g, m_w_ffn_gate, m_w_ffn_up, m_w_ffn_down, m_norm_final_g, v_norm_mix_g, v_w_in, v_shift_mu, v_w0, v_w_lora_up, v_a0, v_a_lora_up, v_g_lora_up, v_k_k, v_k_a, v_r_k, v_lnx_g, v_lnx_b, v_w_proj_rwkv, v_sgu_ln_g, v_sgu_ln_b, v_sgu_w, v_sgu_b, v_w_proj_sgu, v_w_out, v_norm_ffn_g, v_w_ffn_gate, v_w_ffn_up, v_w_ffn_down, v_norm_final_g):
    weights = dict(norm_mix_g=norm_mix_g, w_in=w_in, shift_mu=shift_mu, w0=w0, w_lora_up=w_lora_up, a0=a0, a_lora_up=a_lora_up,
                   g_lora_up=g_lora_up, k_k=k_k, k_a=k_a, r_k=r_k, lnx_g=lnx_g, lnx_b=lnx_b, w_proj_rwkv=w_proj_rwkv,
                   sgu_ln_g=sgu_ln_g, sgu_ln_b=sgu_ln_b, sgu_w=sgu_w, sgu_b=sgu_b, w_proj_sgu=w_proj_sgu, w_out=w_out,
                   norm_ffn_g=norm_ffn_g, w_ffn_gate=w_ffn_gate, w_ffn_up=w_ffn_up, w_ffn_down=w_ffn_down, norm_final_g=norm_final_g)
    m_in = dict(norm_mix_g=m_norm_mix_g, w_in=m_w_in, shift_mu=m_shift_mu, w0=m_w0, w_lora_up=m_w_lora_up, a0=m_a0,
                a_lora_up=m_a_lora_up, g_lora_up=m_g_lora_up, k_k=m_k_k, k_a=m_k_a, r_k=m_r_k, lnx_g=m_lnx_g, lnx_b=m_lnx_b,
                w_proj_rwkv=m_w_proj_rwkv, sgu_ln_g=m_sgu_ln_g, sgu_ln_b=m_sgu_ln_b, sgu_w=m_sgu_w, sgu_b=m_sgu_b,
                w_proj_sgu=m_w_proj_sgu, w_out=m_w_out, norm_ffn_g=m_norm_ffn_g, w_ffn_gate=m_w_ffn_gate, w_ffn_up=m_w_ffn_up,
                w_ffn_down=m_w_ffn_down, norm_final_g=m_norm_final_g)
    v_in = dict(norm_mix_g=v_norm_mix_g, w_in=v_w_in, shift_mu=v_shift_mu, w0=v_w0, w_lora_up=v_w_lora_up, a0=v_a0,
                a_lora_up=v_a_lora_up, g_lora_up=v_g_lora_up, k_k=v_k_k, k_a=v_k_a, r_k=v_r_k, lnx_g=v_lnx_g, lnx_b=v_lnx_b,
                w_proj_rwkv=v_w_proj_rwkv, sgu_ln_g=v_sgu_ln_g, sgu_ln_b=v_sgu_ln_b, sgu_w=v_sgu_w, sgu_b=v_sgu_b,
                w_proj_sgu=v_w_proj_sgu, w_out=v_w_out, norm_ffn_g=v_norm_ffn_g, w_ffn_gate=v_w_ffn_gate, w_ffn_up=v_w_ffn_up,
                w_ffn_down=v_w_ffn_down, norm_final_g=v_norm_final_g)
    names = list(weights)
    col_sharded = ("w_in", "w_lora_up", "a_lora_up", "g_lora_up", "w_proj_rwkv", "w_proj_sgu", "w_ffn_gate", "w_ffn_up")
    row_sharded = ("w_out", "w_ffn_down")
    sharded = [n for n in names if n in col_sharded or n in row_sharded]
    small = [n for n in names if n not in sharded]

    xs, tgt = x[0], loss_target[0]
    T, D = xs.shape
    RW = w0.shape[1]
    H = RW // HEAD
    SW = sgu_ln_g.shape[1]
    G = sgu_w.shape[1]
    assert 2 * SW == D, "the projection layout takes the SGU part to be as wide as a gate"
    lay = _rwkv_layout(RW, w_lora_up.shape[1], a_lora_up.shape[1], g_lora_up.shape[1], D)
    _, pw, _, rcp = lay
    icp = rcp + 3 * D
    b_ga, b_gb, b_z = rcp // D, rcp // D + 1, rcp // D + 2

    gather_groups = [["w_in", "w_lora_up", "a_lora_up", "g_lora_up"], ["w_proj_rwkv", "w_proj_sgu", "w_out"],
                     ["w_ffn_gate"], ["w_ffn_up"], ["w_ffn_down"]]
    gather, gather_token = _exchange_start([[(weights[n][0].astype(BF16), True) for n in grp] for grp in gather_groups],
                                           "gather_start", rels=SIBLING + SAME_CORE)
    full = {}
    relay_tokens = {}
    joined = lambda g: g.transpose(1, 0, 2).reshape(g.shape[1], -1)

    def relay_weights(gi, after, name):
        arrived = _exchange_wait(gather[gi], after, "gather_wait_ici_" + name, rels=SAME_CORE, local=False)
        gather[gi], relay_tokens[gi] = _relay_start(arrived, "gather_relay_" + name)

    def take_weights(gi, after, name):
        done = _exchange_wait(gather[gi], after, "gather_wait_d2d_" + name, rels=SIBLING)
        for n, g in zip(gather_groups[gi], done["lands"]):
            full[n] = g.reshape(-1, g.shape[2]) if n in row_sharded else g

    n1 = _rms_fwd(xs, norm_mix_g, "rms_mix", deps=[gather_token])
    relay_weights(0, n1, "in")
    take_weights(0, relay_tokens[0], "in")
    W_in = _w_in_to_proj(full["w_in"], lay, D, "w_in_layout")
    lora = [_pad_rows(joined(full[n]), rows) for n, rows in zip(("w_lora_up", "a_lora_up", "g_lora_up"), pw[3:])]
    mu_p = _pad_rwkv_cols(shift_mu, lay)
    rsmall = [w0, a0, k_k, k_a]
    hp = [lnx_g.reshape(H, 1, HEAD), lnx_b.reshape(H, 1, HEAD), r_k.reshape(H, 1, HEAD)]
    ws = sgu_w[0]
    bexp = jnp.repeat(sgu_b[0].T, SGU_GROUP, axis=1)
    gf = norm_final_g.reshape(1, D)

    proj = _matmul(n1, W_in, mode="nn", out_dtype=F32, name="proj_in")
    ga, gb = (proj, D, b_ga), (proj, D, b_gb)
    r_h, lw_h, k2_h, v_h, aa_h, bb_h, g_h = _rwkv_pre(proj, mu_p, rsmall, lora, lay, "rwkv_pre")
    wkv_in = [r_h, lw_h, k2_h, v_h, aa_h, bb_h]
    y_h, states = _wkv_fwd(*wkv_in, "wkv_fwd")
    relay_weights(1, y_h, "proj")
    relay_weights(2, relay_tokens[1], "ffn_gate")
    ya = _head_post(y_h, r_h, k2_h, v_h, g_h, hp, "head_post", deps=[relay_tokens[2]])
    relay_weights(3, ya, "ffn_up")
    yb = _sgu_fwd(proj, b_z, sgu_ln_g, sgu_ln_b, ws, bexp, "sgu_fwd")
    take_weights(1, ya, "proj")
    pa = _matmul(ya, full["w_proj_rwkv"], mode="nn", out_dtype=F32, name="proj_a", deps=[relay_tokens[3]])

    def merge_fn(pb_v, pa_v, ga_v, gb_v):
        return pb_v, _sigmoid(ga_v) * pa_v + _sigmoid(gb_v) * pb_v
    pb, merged = _matmul(yb, full["w_proj_sgu"], mode="nn", name="proj_b_merge",
                         epi=(merge_fn, [pa, (proj, b_ga * D), (proj, b_gb * D)], [F32, BF16]))
    h1 = _matmul(merged, full["w_out"], mode="nn", out_dtype=F32, name="out_proj", add=xs)
    n2 = _rms_fwd(h1, norm_ffn_g, "rms_ffn")
    relay_weights(4, n2, "ffn_down")
    take_weights(2, n2, "ffn_gate")
    gt = _matmul(n2, full["w_ffn_gate"], mode="nn", out_dtype=F32, name="ffn_gate", out_blocks=N_DEV, deps=[relay_tokens[4]])
    take_weights(3, gt, "ffn_up")

    def act_fn(up_v, gt_v):
        return up_v, gt_v * _sigmoid(gt_v) * up_v
    up, act = _matmul(n2, full["w_ffn_up"], mode="nn", name="ffn_up_act", out_blocks=N_DEV, epi=(act_fn, [gt], [F32, BF16]))
    take_weights(4, act, "ffn_down")
    h2 = _matmul(act, full["w_ffn_down"], mode="nn", out_dtype=F32, name="ffn_down", add=h1)

    def final_fn(rv, pv):
        (h_v, t_v), (g_v,) = rv, pv
        r = lax.rsqrt(_mean(h_v * h_v) + RMS_EPS)
        yn = h_v * r
        e = yn * g_v - t_v
        loss = 0.5 * jnp.sum(_mean(e * e))
        dout = e * (1.0 / D)
        dyg = dout * g_v
        dh = r * (dyg - yn * _mean(dyg * yn))
        return [dh, dh], [jnp.full((1, LANE), loss, F32), _colsum(dout * yn)]
    dh2, dh2_bf, loss_part, d_gf = _rowwise(final_fn, [h2, tgt], [gf], [(D, F32), (D, BF16)], [(1, LANE), (1, D)], name="final_loss")

    grads = {}

    def start_scatter(group, name, extra=()):
        blocks = [(grads[n].reshape(N_DEV, -1, grads[n].shape[1]) if n in row_sharded else grads[n], False) for n in group]
        (handle,), token = _exchange_start([blocks + list(extra)], name)
        return handle, token

    def dact_fn(d_v, gt_v, up_v):
        s = _sigmoid(gt_v)
        return d_v * up_v * (s * (1.0 + gt_v * (1.0 - s))), d_v * gt_v * s
    dgt, dup = _matmul(dh2_bf, full["w_ffn_down"], mode="nt", name="d_ffn_act", out_blocks=N_DEV,
                       epi=(dact_fn, [gt, up], [BF16, BF16]))
    grads["w_ffn_down"] = _matmul(act, dh2_bf, mode="tn", out_dtype=BF16, name="dw_ffn_down")
    dn2 = _matmul(dgt, full["w_ffn_gate"], mode="nt", out_dtype=F32, name="dn2_gate")
    dn2 = _matmul(dup, full["w_ffn_up"], mode="nt", out_dtype=F32, name="dn2_up", add=dn2)
    grads["w_ffn_gate"] = _matmul(n2, dgt, mode="tn", out_dtype=BF16, name="dw_ffn_gate", out_blocks=N_DEV)
    grads["w_ffn_up"] = _matmul(n2, dup, mode="tn", out_dtype=BF16, name="dw_ffn_up", out_blocks=N_DEV)
    scatter_groups = [["w_ffn_down", "w_ffn_gate", "w_ffn_up"], ["w_out", "w_proj_rwkv", "w_proj_sgu"],
                      ["w_in", "w_lora_up", "a_lora_up", "g_lora_up"]]
    scatter_ffn, token_ffn = start_scatter(scatter_groups[0], "scatter_start_ffn")
    dh1, dh1_bf, d_g2 = _rms_bwd(dn2, h1, dh2, norm_ffn_g, "rms_ffn_bwd", deps=[token_ffn])
    dmerged = _matmul(dh1_bf, full["w_out"], mode="nt", out_dtype=F32, name="d_merged")
    grads["w_out"] = _matmul(merged, dh1_bf, mode="tn", out_dtype=BF16, name="dw_out")

    def dmerge_fn(rv, pv):
        d_v, ga_v, gb_v, pa_v, pb_v = rv
        sa, sb = _sigmoid(ga_v), _sigmoid(gb_v)
        dgates = jnp.concatenate([d_v * pa_v * sa * (1.0 - sa), d_v * pb_v * sb * (1.0 - sb)], axis=1)
        return [dgates, d_v * sa, d_v * sb], []
    dproj, dpa, dpb = _rowwise(dmerge_fn, [dmerged, ga, gb, pa, pb], [],
                               [(2 * D, BF16, icp, b_ga // 2, None), (D, BF16), (D, BF16)], [], name="d_merge")
    dya = _matmul(dpa, full["w_proj_rwkv"], mode="nt", out_dtype=F32, name="d_ya")
    dyb = _matmul(dpb, full["w_proj_sgu"], mode="nt", out_dtype=F32, name="d_yb")
    grads["w_proj_rwkv"] = _matmul(ya, dpa, mode="tn", out_dtype=BF16, name="dw_proj_a", out_blocks=N_DEV)
    grads["w_proj_sgu"] = _matmul(yb, dpb, mode="tn", out_dtype=BF16, name="dw_proj_b", out_blocks=N_DEV)
    scatter_mid, token_mid = start_scatter(scatter_groups[1], "scatter_start_mid")
    dproj, d_lng, d_lnb, d_ws, d_bs = _sgu_bwd(proj, b_z, dyb, sgu_ln_g, sgu_ln_b, ws, bexp, dproj, "sgu_bwd")

    dy_h, dr1, dk1, dv1, dg_h, d_lnxg, d_lnxb, d_rk = _head_post_bwd(dya, y_h, r_h, k2_h, v_h, g_h, hp, "head_post_bwd",
                                                                     deps=[token_mid])
    dr2, dlw_h, dk2b, dv2, daa, dbb = _wkv_bwd(*wkv_in, states, dy_h, "wkv_bwd")
    dps, d_mu, d_w0, d_a0, d_kk, d_ka, d_wlw, d_wla, d_wlg = _rwkv_pre_bwd(
        proj, mu_p, rsmall, lora, [dr1, dr2, dk1, dk2b, dv1, dv2, dlw_h, daa, dbb, dg_h], lay, "rwkv_pre_bwd")
    dproj = _shift_bwd(dps, mu_p, dproj, "shift_bwd")
    split = lambda g: g.reshape(g.shape[0], N_DEV, -1).transpose(1, 0, 2)
    grads["w_in"] = _dw_in_from_proj(_matmul(n1, dproj, mode="tn", out_dtype=BF16, name="dw_in"), lay, D, w_in.shape[2], "dw_in_layout")
    grads["w_lora_up"] = split(d_wlw[:w_lora_up.shape[1]].astype(BF16))
    grads["a_lora_up"] = split(d_wla[:a_lora_up.shape[1]].astype(BF16))
    grads["g_lora_up"] = split(d_wlg[:g_lora_up.shape[1]].astype(BF16))
    out = {}

    arrived = {}

    def update(group, after):
        for n in group:
            res = _adamw(weights[n][0], m_in[n][0], v_in[n][0], arrived[n], "adamw_" + n, after=after)
            out[n] = [t.reshape(weights[n].shape) for t in res]
            after = res[0]
        return after

    def update_group(gi, handle, after, name, first=None):
        parts = _exchange_wait(handle, after, "scatter_wait_" + name, rels=SAME_CORE if handle["chips"] else ALL_PEERS)["lands"]
        arrived.update(zip(scatter_groups[gi], parts))
        return update(scatter_groups[gi][:first], after)

    swap, token_swap = _sibling_swap([grads[n] for n in scatter_groups[2]], None, None, "scatter_in_swap_start")
    after = update_group(0, scatter_ffn, token_swap, "ffn", first=2)
    swap = _sibling_swap(None, swap, after, "scatter_in_swap_wait")
    core = lax.axis_index("c").astype(jnp.int32).reshape(1)
    chip_sums = [_pair_add(mine, theirs, core, "scatter_in_add_" + n)
                 for n, mine, theirs in zip(scatter_groups[2], swap["srcs"], swap["lands"])]
    (scatter_in,), token_in = _exchange_start([[(s, False) for s in chip_sums]], "scatter_start_in", rels=SAME_CORE, chips=True)
    dn1 = _matmul(dproj, W_in, mode="nt", out_dtype=F32, name="dn1", deps=[token_in])
    dx, _, d_g1 = _rms_bwd(dn1, xs, dh1, norm_mix_g, "rms_mix_bwd")
    small_grads = dict(norm_mix_g=d_g1, shift_mu=_unpad_rwkv_cols(d_mu, lay), w0=d_w0, a0=d_a0, k_k=d_kk, k_a=d_ka, r_k=d_rk,
                       lnx_g=d_lnxg, lnx_b=d_lnxb, sgu_ln_g=d_lng, sgu_ln_b=d_lnb, sgu_w=d_ws, sgu_b=d_bs[:, :G].T,
                       norm_ffn_g=d_g2, norm_final_g=d_gf)
    (gather_small,), after = _exchange_start([[(_pack([small_grads[n] for n in small]), True)]], "gather_small_start")
    after = update(scatter_groups[0][2:], after)
    after = update_group(1, scatter_mid, after, "mid")
    after = update_group(2, scatter_in, after, "in")
    packed = [_pack([d[n] for n in small]) for d in (weights, m_in, v_in)]
    small_parts = _exchange_wait(gather_small, after, "gather_small_wait")["lands"][0]
    res = _adamw(*packed, small_parts, "adamw_small")
    unpacked = [_unpack(t, [weights[n].shape for n in small]) for t in res]
    for i, n in enumerate(small):
        out[n] = [u[i] for u in unpacked]

    loss = lax.psum(loss_part[0, 0], ("x", "y", "c"))
    return (loss, dx[None], *[out[n][0] for n in names], *[out[n][1] for n in names],
            *[out[n][2] for n in names], *[out[n][3] for n in names])
```

```python
import jax
import jax.numpy as jnp
from jax import lax
from jax.experimental import pallas as pl
from jax.experimental.pallas import tpu as pltpu

F32 = jnp.float32
BF16 = jnp.bfloat16

N_DEV = 8
LANE = 128
SUBLANE = 8
HEAD = 64
SGU_CHUNK = 128
SGU_GROUP = 128
WKV_CHUNK = 64
RMS_EPS = 1e-6
LN_EPS = 1e-5
LNX_EPS = 64e-5
ADAM_LR, ADAM_B1, ADAM_B2, ADAM_EPS, ADAM_WD, ADAM_STEP = 0.001, 0.9, 0.999, 1e-08, 0.01, 10
VMEM_LIMIT_BYTES = 48 * 1024 * 1024
_SQRT_HALF = 0.7071067811865476
_INV_SQRT_2PI = 0.3989422804014327


def _pick(n, cands):
    for c in cands:
        if n % c == 0:
            return c
    return n


def _ceil_to(n, m):
    return -(-n // m) * m


def _params():
    return pltpu.CompilerParams(vmem_limit_bytes=VMEM_LIMIT_BYTES)


def _tile(n, cap):
    best = 0
    for d in range(LANE, min(n, cap) + 1, LANE):
        if n % d == 0:
            best = d
    return best or n


def _matmul_tiles(M, N, K, a_bytes, b_bytes, o_bytes, has_add, forced):
    tm = forced.get("m") or _tile(M, 1024)
    tn = forced.get("n") or _tile(N, 1024)
    tk = forced.get("k") or _tile(K, 2048)

    def vmem(tm, tn, tk):
        acc = tm * tn * 4 if tk < K else 0
        return 2 * (tm * tk * a_bytes + tk * tn * b_bytes + tm * tn * (o_bytes + (4 if has_add else 0))) + acc

    while vmem(tm, tn, tk) > (VMEM_LIMIT_BYTES * 3) // 4:
        if "k" not in forced and tk > 512 and _tile(K, tk // 2) < tk:
            tk = _tile(K, tk // 2)
        elif "m" not in forced and _tile(M, tm // 2) < tm:
            tm = _tile(M, tm // 2)
        else:
            break
    return tm, tn, tk


def _matmul(a, b, *, mode, out_dtype=F32, name, add=None, deps=(), out_blocks=0, epi=None):
    def view(x):
        return (x.shape[1], x.shape[0] * x.shape[2], x.shape[2]) if x.ndim == 3 else (x.shape[0], x.shape[1], 0)

    (ar, ac, aw), (br, bc, bw) = view(a), view(b)
    a_col, b_col = {"nn": ("k", "n"), "nt": ("k", "k"), "tn": ("m", "n")}[mode]
    if mode == "nn":
        M, K, K2, N = ar, ac, br, bc
    elif mode == "nt":
        M, K, N, K2 = ar, ac, br, bc
    else:
        K, M, K2, N = ar, ac, br, bc
    assert K == K2, (a.shape, b.shape, mode)
    forced = {}
    for dim, w in ((a_col, aw), (b_col, bw), ("n", N // out_blocks if out_blocks else 0)):
        if w:
            assert forced.get(dim, w) == w
            forced[dim] = w
    has_add = add is not None
    tile_bytes = (sum(jnp.dtype(d).itemsize for d in epi[2]) + sum((e[0] if isinstance(e, tuple) else e).dtype.itemsize for e in epi[1])
                  if epi is not None else jnp.dtype(out_dtype).itemsize)
    tm, tn, tk = _matmul_tiles(M, N, K, a.dtype.itemsize, b.dtype.itemsize, tile_bytes, has_add, forced)
    kb = 1
    if "k" in forced and mode != "tn":
        lanes_ok = all(w or tk % LANE == 0 for w in (aw, bw if mode == "nt" else 1))
        kb = next(c for c in (4, 2, 1) if (K // tk) % c == 0 and (c == 1 or (lanes_ok and c * tk <= 1536)))
    nk = K // (tk * kb)
    dn = {"nn": (((1,), (0,)), ((), ())), "nt": (((1,), (1,)), ((), ())), "tn": (((0,), (0,)), ((), ()))}[mode]
    pick = {"m": lambda i, j, k: i, "n": lambda i, j, k: j, "k": lambda i, j, k: k}
    size = {"m": tm, "n": tn, "k": tk}

    def spec(blocked, row_dim, col_dim):
        rf, cf = pick[row_dim], pick[col_dim]
        reps = {d: (kb if d == "k" else 1) for d in (row_dim, col_dim)}
        if blocked:
            lead = kb if col_dim == "k" and kb > 1 else None
            return pl.BlockSpec((lead, size[row_dim], size[col_dim]), lambda i, j, k: (cf(i, j, k), rf(i, j, k), 0))
        return pl.BlockSpec((size[row_dim] * reps[row_dim], size[col_dim] * reps[col_dim]), lambda i, j, k: (rf(i, j, k), cf(i, j, k)))

    def k_part(ref, blocked, k_on_rows, j):
        if kb == 1:
            return ref[...]
        if blocked:
            return ref[j]
        return ref[j * tk:(j + 1) * tk, :] if k_on_rows else ref[:, j * tk:(j + 1) * tk]

    a_spec = spec(aw, "k" if mode == "tn" else "m", a_col)
    b_spec = spec(bw, "n" if mode == "nt" else "k", b_col)
    o_spec = spec(out_blocks, "m", "n")
    epi_fn, epi_ins, epi_dtypes = epi if epi is not None else (None, [], [out_dtype])
    epi_ins = [e if isinstance(e, tuple) else (e, None) for e in epi_ins]
    n_epi = len(epi_ins)
    n_in = 2 + has_add + n_epi + len(deps)
    n_out = len(epi_dtypes)

    def body(*refs):
        a_ref, b_ref = refs[0], refs[1]
        add_ref = refs[2] if has_add else None
        epi_refs = refs[2 + has_add:2 + has_add + n_epi]
        o_refs = refs[n_in:n_in + n_out]
        part = None
        for q in range(kb):
            a_q = k_part(a_ref, aw and a_col == "k", False, q)
            b_q = k_part(b_ref, bw and b_col == "k", mode == "nn", q)
            prod = lax.dot_general(a_q.astype(BF16), b_q.astype(BF16), dn, preferred_element_type=F32)
            part = prod if part is None else part + prod

        def finish(res):
            outs = epi_fn(res, *[e[...] for e in epi_refs]) if epi_fn is not None else (res,)
            for o_ref, val in zip(o_refs, outs):
                o_ref[...] = val.astype(o_ref.dtype)

        if nk == 1:
            finish(part + add_ref[...] if has_add else part)
            return
        acc_ref = refs[-1]
        kk = pl.program_id(2)

        @pl.when(kk == 0)
        def _():
            acc_ref[...] = part + add_ref[...] if has_add else part

        @pl.when(kk > 0)
        def _():
            acc_ref[...] += part

        @pl.when(kk == nk - 1)
        def _():
            finish(acc_ref[...])

    def epi_spec(arr, off):
        if off is None:
            return o_spec
        assert off % tn == 0
        return pl.BlockSpec((tm, tn), lambda i, j, k: (i, j + off // tn))

    ins = [a, b] + ([add] if has_add else []) + [arr for arr, _ in epi_ins] + list(deps)
    in_specs = ([a_spec, b_spec] + ([o_spec] if has_add else []) + [epi_spec(arr, off) for arr, off in epi_ins]
                + [pl.BlockSpec(d.shape, lambda i, j, k, nd=d.ndim: (0,) * nd) for d in deps])
    o_shape = (out_blocks, M, tn) if out_blocks else (M, N)
    res = pl.pallas_call(
        body, name=name, grid=(M // tm, N // tn, nk), in_specs=in_specs, out_specs=[o_spec] * n_out,
        out_shape=[jax.ShapeDtypeStruct(o_shape, dt) for dt in epi_dtypes],
        scratch_shapes=[pltpu.VMEM((tm, tn), F32)] if nk > 1 else [],
        compiler_params=_params())(*ins)
    return res[0] if epi is None else list(res)


def _rowwise(fn, rows, pars, row_outs, acc_outs, *, name, tm=256, deps=()):
    rows = [r if isinstance(r, tuple) else (r, r.shape[1], 0) for r in rows]
    row_outs = [o if len(o) == 5 else (o[0], o[1], o[0], 0, None) for o in row_outs]
    aliased = [(k, o[4]) for k, o in enumerate(row_outs) if o[4] is not None]
    R = rows[0][0].shape[0]
    if max(w for _, w, _ in rows) > 4096:
        tm = tm // 2
    tm = min(tm, R)
    assert R % tm == 0
    nr, npar = len(rows), len(pars)
    nro = len(row_outs)
    n_in = nr + npar + len(deps) + len(aliased)

    def body(*refs):
        rv = [r[...] for r in refs[:nr]]
        pv = [p[...] for p in refs[nr:nr + npar]]
        outs = refs[n_in:]
        ro, ao = fn(rv, pv)
        first = pl.program_id(0) == 0
        for o_ref, val in zip(outs[:nro], ro):
            o_ref[...] = val.astype(o_ref.dtype)

        @pl.when(first)
        def _():
            for o_ref, val in zip(outs[nro:], ao):
                o_ref[...] = val

        @pl.when(jnp.logical_not(first))
        def _():
            for o_ref, val in zip(outs[nro:], ao):
                o_ref[...] += val

    in_specs = ([pl.BlockSpec((tm, w), lambda i, cb=cb: (i, cb)) for _, w, cb in rows]
                + [pl.BlockSpec(p.shape, lambda i, nd=p.ndim: (0,) * nd) for p in list(pars) + list(deps)]
                + [pl.BlockSpec(memory_space=pl.ANY)] * len(aliased))
    out_shape = ([jax.ShapeDtypeStruct((R, full), dt) for _, dt, full, _, _ in row_outs]
                 + [jax.ShapeDtypeStruct(s, F32) for s in acc_outs])
    out_specs = ([pl.BlockSpec((tm, f), lambda i, cb=cb: (i, cb)) for f, _, _, cb, _ in row_outs]
                 + [pl.BlockSpec(s, lambda i, nd=len(s): (0,) * nd) for s in acc_outs])
    res = pl.pallas_call(body, name=name, grid=(R // tm,), in_specs=in_specs, out_specs=out_specs, out_shape=out_shape,
                         input_output_aliases={n_in - len(aliased) + q: k for q, (k, _) in enumerate(aliased)},
                         compiler_params=_params())(*[r for r, _, _ in rows], *pars, *deps, *[buf for _, buf in aliased])
    return list(res)


def _bdot(a, b, mode="nn"):
    dn = {"nn": (((1,), (0,)), ((), ())), "nt": (((1,), (1,)), ((), ())), "tn": (((0,), (0,)), ((), ()))}[mode]
    return lax.dot_general(a.astype(BF16), b.astype(BF16), dn, preferred_element_type=F32)


def _sigmoid(x):
    return jax.nn.sigmoid(x)


def _softplus(x):
    return jnp.maximum(x, 0.0) + jnp.log1p(jnp.exp(-jnp.abs(x)))


def _gelu(z):
    return 0.5 * z * (1.0 + lax.erf(z * _SQRT_HALF))


def _gelu_grad(z):
    return 0.5 * (1.0 + lax.erf(z * _SQRT_HALF)) + z * jnp.exp(-0.5 * z * z) * _INV_SQRT_2PI


def _mean(x):
    return jnp.mean(x, axis=-1, keepdims=True)


def _colsum(x):
    return jnp.sum(x, axis=0, keepdims=True)


def _rms_fwd(x, g, name, deps=()):
    def fn(rv, pv):
        (xv,), (gv,) = rv, pv
        r = lax.rsqrt(_mean(xv * xv) + RMS_EPS)
        return [xv * r * gv], []
    return _rowwise(fn, [x], [g], [(x.shape[1], BF16)], [], name=name, deps=deps)[0]


def _rms_bwd(dn, x, dres, g, name, deps=()):
    def fn(rv, pv):
        (dnv, xv, drv), (gv,) = rv, pv
        r = lax.rsqrt(_mean(xv * xv) + RMS_EPS)
        yn = xv * r
        dyg = dnv * gv
        dx = drv + r * (dyg - yn * _mean(dyg * yn))
        return [dx, dx], [_colsum(dnv * yn)]
    D = x.shape[1]
    return _rowwise(fn, [dn, x, dres], [g], [(D, F32), (D, BF16)], [(1, D)], name=name, deps=deps)


def _rwkv_layout(RW, Lw, La, Lg, D):
    widths = [RW, RW, RW, Lw, La, Lg]
    pw = [_ceil_to(w, LANE) for w in widths]
    pw[5] += _ceil_to(sum(pw), 2 * D) - sum(pw)
    offs = [sum(pw[:i]) for i in range(6)]
    return widths, pw, offs, sum(pw)


def _pad_rwkv_cols(a, lay):
    widths, pw, _, _ = lay
    pieces, src = [], 0
    for w, p in zip(widths, pw):
        pieces.append(a[:, src:src + w])
        if p > w:
            pieces.append(jnp.zeros((a.shape[0], p - w), a.dtype))
        src += w
    return jnp.concatenate(pieces, axis=1)


def _unpad_rwkv_cols(a, lay):
    widths, _, offs, _ = lay
    return jnp.concatenate([a[:, o:o + w] for o, w in zip(offs, widths)], axis=1)


def _proj_pieces(lay, D, cs):
    widths, _, offs, rcp = lay
    rc = sum(widths)
    segs = [(sum(widths[:j]), widths[j], offs[j]) for j in range(6)] + [(rc, D, rcp + 2 * D), (rc + D, D, rcp), (rc + 2 * D, D, rcp + D)]
    pieces = []
    for start, width, dst in segs:
        n = start
        while n < start + width:
            d, off = divmod(n, cs)
            take = min(cs - off, start + width - n)
            pieces.append((d, off, dst + n - start, take))
            n += take
    return pieces


def _w_in_to_proj(g, lay, D, name):
    nb, rows, cs = g.shape
    icp = lay[3] + 3 * D
    pieces = _proj_pieces(lay, D, cs)
    tm = _pick(rows, (256, 128, 64, 32, 16))

    def body(i_ref, o_ref):
        o_ref[...] = jnp.zeros_like(o_ref)
        for d, src, dst, w in pieces:
            o_ref[:, dst:dst + w] = i_ref[d, :, src:src + w]

    return pl.pallas_call(
        body, name=name, grid=(rows // tm,), in_specs=[pl.BlockSpec((nb, tm, cs), lambda i: (0, i, 0))],
        out_specs=pl.BlockSpec((tm, icp), lambda i: (i, 0)), out_shape=jax.ShapeDtypeStruct((rows, icp), g.dtype),
        compiler_params=_params())(g)


def _dw_in_from_proj(a, lay, D, cs, name):
    rows, icp = a.shape
    pieces = _proj_pieces(lay, D, cs)
    tm = _pick(rows, (256, 128, 64, 32, 16))

    def body(i_ref, o_ref):
        for d, src, dst, w in pieces:
            o_ref[d, :, src:src + w] = i_ref[:, dst:dst + w]

    return pl.pallas_call(
        body, name=name, grid=(rows // tm,), in_specs=[pl.BlockSpec((tm, icp), lambda i: (i, 0))],
        out_specs=pl.BlockSpec((N_DEV, tm, cs), lambda i: (0, i, 0)), out_shape=jax.ShapeDtypeStruct((N_DEV, rows, cs), a.dtype),
        compiler_params=_params())(a)


def _pad_rows(a, rows):
    return a if a.shape[0] == rows else jnp.concatenate([a, jnp.zeros((rows - a.shape[0], a.shape[1]), a.dtype)], axis=0)


def _token_shift(p, halo, mu, i):
    tm = p.shape[0]
    hid = lax.broadcasted_iota(jnp.int32, (SUBLANE, 1), 0)
    before = jnp.sum(jnp.where(hid == SUBLANE - 1, halo, 0.0), axis=0, keepdims=True)
    before = jnp.where(i == 0, 0.0, before)
    rid = lax.broadcasted_iota(jnp.int32, (tm, 1), 0)
    prev = jnp.where(rid == 0, before, pltpu.roll(p, 1, 0))
    d = prev - p
    return p + d * mu, d


def _rwkv_math(ps, w0, a0, k_k, k_a, wlw, wla, wlg, lay):
    _, pw, offs, _ = lay
    r, k, v, xw, xa, xg = (ps[:, offs[j]:offs[j] + pw[j]] for j in range(6))
    tw = jnp.tanh(xw)
    ww = w0 + _bdot(tw, wlw)
    lw = -jnp.exp(-_softplus(-ww) - 0.5)
    a = _sigmoid(a0 + _bdot(xa, wla))
    sg = _sigmoid(xg)
    g = _bdot(sg, wlg)
    return dict(r=r, k=k, v=v, xa=xa, tw=tw, ww=ww, lw=lw, a=a, sg=sg, g=g, kkp=k * k_k, k2=k * (1.0 + (a - 1.0) * k_a))


def _halo_specs(T, tm, width, after):
    hb = tm // SUBLANE
    last = T // SUBLANE - 1
    if after:
        return pl.BlockSpec((SUBLANE, width), lambda i: (jnp.minimum((i + 1) * hb, last), 0))
    return pl.BlockSpec((SUBLANE, width), lambda i: (jnp.maximum(i * hb - 1, 0), 0))


def _rowsum(x):
    return jnp.sum(x, axis=-1, keepdims=True)


def _kk_math(kkp):
    nrm = jnp.sqrt(_rowsum(kkp * kkp))
    inv = 1.0 / jnp.maximum(nrm, 1e-12)
    return nrm, inv, kkp * inv


def _rwkv_pre(p, mu, small, lora, lay, name):
    T, rcp = p.shape[0], lay[3]
    H = lay[0][0] // HEAD
    tm = min(128, T)

    def body(p_ref, ph_ref, mu_ref, w0_ref, a0_ref, kk_ref, ka_ref, wlw_ref, wla_ref, wlg_ref, r_o, lw_o, k2_o, v_o, aa_o, bb_o, g_o):
        ps, _ = _token_shift(p_ref[...], ph_ref[...], mu_ref[...], pl.program_id(0))
        q = _rwkv_math(ps, w0_ref[...], a0_ref[...], kk_ref[...], ka_ref[...], wlw_ref[...], wla_ref[...], wlg_ref[...], lay)
        for h in range(H):
            sl = slice(h * HEAD, (h + 1) * HEAD)
            for o_ref, key in ((r_o, "r"), (lw_o, "lw"), (k2_o, "k2"), (v_o, "v"), (g_o, "g")):
                o_ref[h] = q[key][:, sl]
            _, _, kk = _kk_math(q["kkp"][:, sl])
            aa_o[h] = -kk
            bb_o[h] = kk * q["a"][:, sl]

    whole = lambda arr: pl.BlockSpec(arr.shape, lambda i: (0, 0))
    return pl.pallas_call(
        body, name=name, grid=(T // tm,),
        in_specs=([pl.BlockSpec((tm, rcp), lambda i: (i, 0)), _halo_specs(T, tm, rcp, False), whole(mu)]
                  + [whole(s) for s in small] + [whole(w) for w in lora]),
        out_specs=[pl.BlockSpec((H, tm, HEAD), lambda i: (0, i, 0))] * 7, out_shape=[jax.ShapeDtypeStruct((H, T, HEAD), F32)] * 7,
        compiler_params=_params())(p, p, mu, *small, *lora)


def _rwkv_pre_bwd(p, mu, small, lora, hgrads, lay, name):
    T, rcp = p.shape[0], lay[3]
    widths, pw, offs, _ = lay
    RW = widths[0]
    H = RW // HEAD
    tm = min(128, T)

    def body(p_ref, ph_ref, mu_ref, w0_ref, a0_ref, kk_ref, ka_ref, wlw_ref, wla_ref, wlg_ref,
             dr1, dr2, dk1, dk2b, dv1, dv2, dlw_h, daa, dbb, dg_h,
             dps_ref, dmu_ref, dw0_ref, da0_ref, dkk_ref, dka_ref, dwlw_ref, dwla_ref, dwlg_ref,
             s_dr, s_dk2, s_dv, s_dlw, s_dkkp, s_da, s_dg):
        i = pl.program_id(0)
        ps, dprev = _token_shift(p_ref[...], ph_ref[...], mu_ref[...], i)
        k_k, k_a = kk_ref[...], ka_ref[...]
        q = _rwkv_math(ps, w0_ref[...], a0_ref[...], k_k, k_a, wlw_ref[...], wla_ref[...], wlg_ref[...], lay)
        k, a, lw, ww, tw, sg = q["k"], q["a"], q["lw"], q["ww"], q["tw"], q["sg"]
        for h in range(H):
            sl = slice(h * HEAD, (h + 1) * HEAD)
            s_dr[:, sl] = dr1[h] + dr2[h]
            s_dk2[:, sl] = dk1[h] + dk2b[h]
            s_dv[:, sl] = dv1[h] + dv2[h]
            s_dlw[:, sl] = dlw_h[h]
            s_dg[:, sl] = dg_h[h]
            nrm, inv, kk = _kk_math(q["kkp"][:, sl])
            dbb_h = dbb[h]
            dkk = dbb_h * a[:, sl] - daa[h]
            s_dkkp[:, sl] = jnp.where(nrm > 1e-12, inv * (dkk - kk * _rowsum(dkk * kk)), dkk * inv)
            s_da[:, sl] = dbb_h * kk
        dk2, dkkp, dg = s_dk2[...], s_dkkp[...], s_dg[...]
        dk = dk2 * (1.0 + (a - 1.0) * k_a) + dkkp * k_k
        da = s_da[...] + dk2 * k * k_a
        dpa = da * a * (1.0 - a)
        dww = s_dlw[...] * lw * _sigmoid(-ww)
        dxa = _bdot(dpa, wla_ref[...], "nt")
        dxw = _bdot(dww, wlw_ref[...], "nt") * (1.0 - tw * tw)
        dxg = _bdot(dg, wlg_ref[...], "nt") * sg * (1.0 - sg)
        segs = (s_dr[...], dk, s_dv[...], dxw, dxa, dxg)
        sums = [dmu_ref, dw0_ref, da0_ref, dkk_ref, dka_ref, dwlw_ref, dwla_ref, dwlg_ref]

        @pl.when(i == 0)
        def _():
            for s in sums:
                s[...] = jnp.zeros_like(s)

        for j, seg in enumerate(segs):
            sl = slice(offs[j], offs[j] + pw[j])
            dps_ref[:, sl] = seg
            dmu_ref[:, sl] += _colsum(seg * dprev[:, sl])
        dw0_ref[...] += _colsum(dww)
        da0_ref[...] += _colsum(dpa)
        dkk_ref[...] += _colsum(dkkp * k)
        dka_ref[...] += _colsum(dk2 * k * (a - 1.0))
        dwlw_ref[...] += _bdot(tw, dww, "tn")
        dwla_ref[...] += _bdot(q["xa"], dpa, "tn")
        dwlg_ref[...] += _bdot(sg, dg, "tn")

    whole = lambda arr: pl.BlockSpec(arr.shape, lambda i: (0, 0))
    row = lambda w: pl.BlockSpec((tm, w), lambda i: (i, 0))
    acc_shapes = [(1, rcp), (1, RW), (1, RW), (1, RW), (1, RW)] + [w.shape for w in lora]
    return pl.pallas_call(
        body, name=name, grid=(T // tm,),
        in_specs=([row(rcp), _halo_specs(T, tm, rcp, False), whole(mu)] + [whole(s) for s in small] + [whole(w) for w in lora]
                  + [pl.BlockSpec((H, tm, HEAD), lambda i: (0, i, 0))] * 10),
        out_specs=[row(rcp)] + [pl.BlockSpec(s, lambda i: (0, 0)) for s in acc_shapes],
        out_shape=[jax.ShapeDtypeStruct((T, rcp), F32)] + [jax.ShapeDtypeStruct(s, F32) for s in acc_shapes],
        scratch_shapes=[pltpu.VMEM((tm, RW), F32)] * 7, compiler_params=_params())(p, p, mu, *small, *lora, *hgrads)


def _shift_bwd(dps, mu, dproj, name):
    T, rcp = dps.shape
    tm = min(256, T)
    nt = T // tm

    def body(d_ref, dh_ref, mu_ref, buf_ref, o_ref):
        i = pl.program_id(0)
        d = d_ref[...]
        hid = lax.broadcasted_iota(jnp.int32, (SUBLANE, 1), 0)
        after = jnp.sum(jnp.where(hid == 0, dh_ref[...], 0.0), axis=0, keepdims=True)
        after = jnp.where(i == nt - 1, 0.0, after)
        rid = lax.broadcasted_iota(jnp.int32, (tm, 1), 0)
        nxt = jnp.where(rid == tm - 1, after, pltpu.roll(d, tm - 1, 0))
        mu_v = mu_ref[...]
        o_ref[...] = (d * (1.0 - mu_v) + nxt * mu_v).astype(BF16)

    row = pl.BlockSpec((tm, rcp), lambda i: (i, 0))
    return pl.pallas_call(
        body, name=name, grid=(nt,),
        in_specs=[row, _halo_specs(T, tm, rcp, True), pl.BlockSpec(mu.shape, lambda i: (0, 0)), pl.BlockSpec(memory_space=pl.ANY)],
        out_specs=row, out_shape=jax.ShapeDtypeStruct(dproj.shape, BF16), input_output_aliases={3: 0},
        compiler_params=_params())(dps, dps, mu, dproj)


def _head_post_math(y, r, k2, v, lg, lb, rk):
    yc = y - _mean(y)
    rstd = lax.rsqrt(_mean(yc * yc) + LNX_EPS)
    yn = yc * rstd
    s = _rowsum(r * k2 * rk)
    return yn, rstd, yn * lg + lb + s * v, s


def _head_post(y, r, k2, v, g, hp, name, deps=()):
    H, T, _ = y.shape
    tm = min(128, T)

    def body(y_ref, r_ref, k_ref, v_ref, g_ref, lg_ref, lb_ref, rk_ref, *rest):
        o_ref = rest[-1]
        _, _, t, _ = _head_post_math(y_ref[...], r_ref[...], k_ref[...], v_ref[...], lg_ref[...], lb_ref[...], rk_ref[...])
        out = (t * g_ref[...]).astype(BF16)
        for h in range(H):
            o_ref[:, h * HEAD:(h + 1) * HEAD] = out[h]

    blk = pl.BlockSpec((H, tm, HEAD), lambda i: (0, i, 0))
    par = pl.BlockSpec((H, 1, HEAD), lambda i: (0, 0, 0))
    return pl.pallas_call(
        body, name=name, grid=(T // tm,),
        in_specs=[blk] * 5 + [par] * 3 + [pl.BlockSpec(d.shape, lambda i, nd=d.ndim: (0,) * nd) for d in deps],
        out_specs=pl.BlockSpec((tm, H * HEAD), lambda i: (i, 0)),
        out_shape=jax.ShapeDtypeStruct((T, H * HEAD), BF16), compiler_params=_params())(y, r, k2, v, g, *hp, *deps)


def _head_post_bwd(dya, y, r, k2, v, g, hp, name, deps=()):
    H, T, _ = y.shape
    tm = min(128, T)
    hsum = lambda t: jnp.sum(t, axis=1, keepdims=True)

    def body(d_ref, y_ref, r_ref, k_ref, v_ref, g_ref, lg_ref, lb_ref, rk_ref, *rest):
        outs, d_s = rest[len(deps):len(deps) + 8], rest[-1]
        for h in range(H):
            d_s[h] = d_ref[:, h * HEAD:(h + 1) * HEAD]
        d_v, r_v, k_v, v_v, lg, rk = d_s[...], r_ref[...], k_ref[...], v_ref[...], lg_ref[...], rk_ref[...]
        yn, rstd, t, s = _head_post_math(y_ref[...], r_v, k_v, v_v, lg, lb_ref[...], rk)
        dyo = d_v * g_ref[...]
        dyn = dyo * lg
        ds = _rowsum(dyo * v_v)
        vals = (rstd * (dyn - _mean(dyn) - yn * _mean(dyn * yn)), ds * k_v * rk, ds * r_v * rk, dyo * s, d_v * t)
        for o_ref, val in zip(outs[:5], vals):
            o_ref[...] = val
        sums = (hsum(dyo * yn), hsum(dyo), hsum(ds * r_v * k_v))
        first = pl.program_id(0) == 0

        @pl.when(first)
        def _():
            for o_ref, val in zip(outs[5:], sums):
                o_ref[...] = val

        @pl.when(jnp.logical_not(first))
        def _():
            for o_ref, val in zip(outs[5:], sums):
                o_ref[...] += val

    blk = pl.BlockSpec((H, tm, HEAD), lambda i: (0, i, 0))
    par = pl.BlockSpec((H, 1, HEAD), lambda i: (0, 0, 0))
    return pl.pallas_call(
        body, name=name, grid=(T // tm,),
        in_specs=([pl.BlockSpec((tm, H * HEAD), lambda i: (i, 0))] + [blk] * 5 + [par] * 3
                  + [pl.BlockSpec(d.shape, lambda i, nd=d.ndim: (0,) * nd) for d in deps]),
        out_specs=[blk] * 5 + [par] * 3,
        out_shape=[jax.ShapeDtypeStruct((H, T, HEAD), F32)] * 5 + [jax.ShapeDtypeStruct((H, 1, HEAD), F32)] * 3,
        scratch_shapes=[pltpu.VMEM((H, tm, HEAD), F32)], compiler_params=_params())(dya, y, r, k2, v, g, *hp, *deps)


def _bmm(x, y, mode):
    dn = {"nn": (((2,), (1,)), ((0,), (0,))), "nt": (((2,), (2,)), ((0,), (0,))), "tn": (((1,), (1,)), ((0,), (0,)))}[mode]
    (xh, xl), (yh, yl) = _split(x), _split(y)
    dot = lambda p, q: lax.dot_general(p, q, dn, preferred_element_type=F32)
    out = dot(xh, yh)
    if yl is not None:
        out = out + dot(xh, yl)
    if xl is not None:
        out = out + dot(xl, yh)
    return out


def _split(x):
    if isinstance(x, tuple):
        return x
    hi = x.astype(BF16)
    return hi, (x - hi.astype(F32)).astype(BF16)


def _exact(x):
    return x.astype(BF16), None


def _round(x):
    return x if isinstance(x, tuple) else (x.astype(BF16), None)


def _rows(*xs):
    if isinstance(xs[0], tuple):
        return tuple(None if any(p is None for p in parts) else jnp.concatenate(parts, axis=1) for parts in zip(*xs))
    return jnp.concatenate(xs, axis=1)


def _wkv_chunk(r, lw, k, v, a, b):
    hb, C, _ = r.shape
    ti = lax.broadcasted_iota(jnp.int32, (C, C), 0)
    si = lax.broadcasted_iota(jnp.int32, (C, C), 1)
    linc, lstr, eye = (ti >= si).astype(F32), (ti > si).astype(F32), (ti == si).astype(F32)
    qmask = jnp.concatenate([jnp.concatenate([lstr, lstr], axis=1), jnp.concatenate([linc, linc], axis=1)], axis=0)
    lincb = _exact(jnp.broadcast_to(linc, (hb, C, C)))
    both = _exact(jnp.broadcast_to(jnp.concatenate([linc, lstr], axis=0), (hb, 2 * C, C)))
    ones = _exact(jnp.ones_like(v))
    lws = _split(lw)
    ci = _bmm(lincb, lws, "nn")
    cC = jnp.sum(lw, axis=1, keepdims=True)
    gi, ge, gn, gr = jnp.exp(ci), jnp.exp(ci - lw), jnp.exp(-ci), jnp.exp(cC - ci)
    q = dict(At=a * ge, Rt=r * gi, Bt=b * gn, Kt=k * gn, Bh=b * gr, Kh=k * gr)
    s = dict(AR=_round(_rows(q["At"], q["Rt"])), BK=_round(_rows(q["Bt"], q["Kt"])), BKh=_round(_rows(q["Bh"], q["Kh"])), v=_round(v))
    quad = _bmm(s["AR"], s["BK"], "nt") * qmask
    s["top"], s["bot"] = _round(quad[:, :C]), _round(quad[:, C:])
    A_ab = quad[:, :C, :C]
    Tm = eye + A_ab
    Pw = _round(A_ab)
    n = 1
    while 2 * n < C:
        Pw = _round(_bmm(Pw, Pw, "nn"))
        Tm = Tm + _bmm(_round(Tm), Pw, "nn")
        n *= 2
    s["Tm"] = _round(Tm)
    gC = jnp.exp(_bmm(lws, ones, "tn"))
    q.update(gi=gi, ge=ge, gn=gn, gr=gr, qmask=qmask, both=both, gC=gC, ones=ones, s=s)
    return q


def _wkv_u(s, H0s, C):
    arh = _bmm(s["AR"], H0s, "nn")
    zv = _rows(tuple(None if p is None else jnp.zeros_like(p) for p in s["v"]), s["v"])
    U = _bmm(s["Tm"], _round(arh[:, :C] + _bmm(s["top"], zv, "nn")), "nn")
    return arh, _rows(_round(U), s["v"])


def _wkv_fwd(r, lw, k, v, a, b, name):
    H, T, N = r.shape
    C = min(WKV_CHUNK, T)
    nc = T // C
    hb = _pick(H, (16, 8, 4, 2))

    def body(r_ref, lw_ref, k_ref, v_ref, a_ref, b_ref, y_ref, st_ref, h_ref):
        @pl.when(pl.program_id(1) == 0)
        def _():
            h_ref[...] = jnp.zeros_like(h_ref)

        H0 = h_ref[...]
        st_ref[0] = H0
        q = _wkv_chunk(r_ref[...], lw_ref[...], k_ref[...], v_ref[...], a_ref[...], b_ref[...])
        s = q["s"]
        arh, UV = _wkv_u(s, _round(H0), C)
        y_ref[...] = arh[:, C:] + _bmm(s["bot"], UV, "nn")
        h_ref[...] = q["gC"] * H0 + _bmm(s["BKh"], UV, "tn")

    blk = pl.BlockSpec((hb, C, N), lambda h, c: (h, c, 0))
    return pl.pallas_call(
        body, name=name, grid=(H // hb, nc), in_specs=[blk] * 6,
        out_specs=[blk, pl.BlockSpec((1, hb, N, N), lambda h, c: (c, h, 0, 0))],
        out_shape=[jax.ShapeDtypeStruct((H, T, N), F32), jax.ShapeDtypeStruct((nc, H, N, N), F32)],
        scratch_shapes=[pltpu.VMEM((hb, N, N), F32)], compiler_params=_params())(r, lw, k, v, a, b)


def _wkv_bwd(r, lw, k, v, a, b, states, dy, name):
    H, T, N = r.shape
    C = min(WKV_CHUNK, T)
    nc = T // C
    hb = _pick(H, (16, 8, 4, 2))

    def body(r_ref, lw_ref, k_ref, v_ref, a_ref, b_ref, st_ref, dy_ref, dr_ref, dlw_ref, dk_ref, dv_ref, da_ref, db_ref, dh_ref):
        @pl.when(pl.program_id(1) == 0)
        def _():
            dh_ref[...] = jnp.zeros_like(dh_ref)

        dHC = dh_ref[...]
        H0 = st_ref[0]
        q = _wkv_chunk(r_ref[...], lw_ref[...], k_ref[...], v_ref[...], a_ref[...], b_ref[...])
        s, gC = q["s"], q["gC"]
        H0s, dHs, dY = _round(H0), _round(dHC), _round(dy_ref[...])
        _, UV = _wkv_u(s, H0s, C)
        bot_dy = _bmm(s["bot"], dY, "tn")
        bkh_dh = _bmm(s["BKh"], dHs, "nn")
        dP = _round(_bmm(s["Tm"], _round(bot_dy[:, :C] + bkh_dh[:, :C]), "tn"))
        dv_ref[...] = bot_dy[:, C:] + bkh_dh[:, C:] + _bmm(s["top"], dP, "tn")[:, C:]
        dPY = _rows(dP, dY)
        dh_ref[...] = gC * dHC + _bmm(s["AR"], dPY, "tn")
        dquad = _round(_bmm(dPY, UV, "nt") * q["qmask"])
        dAR = _bmm(dPY, H0s, "nt") + _bmm(dquad, s["BK"], "nn")
        dBK = _bmm(dquad, s["AR"], "tn")
        dBKh = _bmm(UV, dHs, "nt")
        dAt, dRt, dBt, dKt, dBh, dKh = dAR[:, :C], dAR[:, C:], dBK[:, :C], dBK[:, C:], dBKh[:, :C], dBKh[:, C:]
        dr_ref[...] = dRt * q["gi"]
        da_ref[...] = dAt * q["ge"]
        db_ref[...] = dBt * q["gn"] + dBh * q["gr"]
        dk_ref[...] = dKt * q["gn"] + dKh * q["gr"]
        tail = dBh * q["Bh"] + dKh * q["Kh"]
        dci = dRt * q["Rt"] - dBt * q["Bt"] - dKt * q["Kt"] - tail
        dcC = jnp.sum(tail, axis=1, keepdims=True) + _bmm(q["ones"], H0 * dHC * gC, "nt")
        dlw_ref[...] = _bmm(q["both"], _rows(dci, dAt * q["At"]), "tn") + dcC

    blk = pl.BlockSpec((hb, C, N), lambda h, c: (h, nc - 1 - c, 0))
    st = pl.BlockSpec((1, hb, N, N), lambda h, c: (nc - 1 - c, h, 0, 0))
    return pl.pallas_call(
        body, name=name, grid=(H // hb, nc), in_specs=[blk] * 6 + [st, blk], out_specs=[blk] * 6,
        out_shape=[jax.ShapeDtypeStruct((H, T, N), F32)] * 6,
        scratch_shapes=[pltpu.VMEM((hb, N, N), F32)], compiler_params=_params())(r, lw, k, v, a, b, states, dy)


def _sgu_ln(z, SW, lng, lnb):
    ge = _gelu(z)
    u, vv = ge[:, :SW], ge[:, SW:]
    xc = vv - _mean(vv)
    rstd = lax.rsqrt(_mean(xc * xc) + LN_EPS)
    vn = xc * rstd
    return u, vn, rstd, vn * lng + lnb


def _causal(ws_ref, g):
    ti = lax.broadcasted_iota(jnp.int32, (SGU_CHUNK, SGU_CHUNK), 0)
    si = lax.broadcasted_iota(jnp.int32, (SGU_CHUNK, SGU_CHUNK), 1)
    return ti >= si, jnp.where(ti >= si, ws_ref[g], 0.0).astype(BF16)


def _sgu_fwd(proj, zblock, lng, lnb, ws, bexp, name):
    T, SW = proj.shape[0], lng.shape[1]
    G = ws.shape[0]
    tr = min(256, T)
    nch = tr // SGU_CHUNK

    def body(z_ref, lng_ref, lnb_ref, ws_ref, be_ref, o_ref):
        u, _, _, vl = _sgu_ln(z_ref[...], SW, lng_ref[...], lnb_ref[...])
        for g in range(G):
            cs = slice(g * SGU_GROUP, (g + 1) * SGU_GROUP)
            _, wc = _causal(ws_ref, g)
            for n in range(nch):
                rs = slice(n * SGU_CHUNK, (n + 1) * SGU_CHUNK)
                m = jnp.dot(wc, vl[rs, cs].astype(BF16), preferred_element_type=F32) + be_ref[:, cs]
                o_ref[rs, cs] = (u[rs, cs] * m).astype(BF16)

    whole = lambda arr: pl.BlockSpec(arr.shape, lambda i, nd=arr.ndim: (0,) * nd)
    return pl.pallas_call(
        body, name=name, grid=(T // tr,),
        in_specs=[pl.BlockSpec((tr, 2 * SW), lambda i: (i, zblock)), whole(lng), whole(lnb), whole(ws), whole(bexp)],
        out_specs=pl.BlockSpec((tr, SW), lambda i: (i, 0)), out_shape=jax.ShapeDtypeStruct((T, SW), BF16),
        compiler_params=_params())(proj, lng, lnb, ws, bexp)


def _sgu_bwd(proj, zblock, dyb, lng, lnb, ws, bexp, dproj, name):
    T, SW = proj.shape[0], lng.shape[1]
    G = ws.shape[0]
    tr = min(256, T)
    nch = tr // SGU_CHUNK
    nt = T // tr

    def body(z_ref, dy_ref, lng_ref, lnb_ref, ws_ref, be_ref, buf_ref, dz_ref, dlg_ref, dlb_ref, dws_ref, db_ref, du_s, dvl_s, dbacc_s):
        i = pl.program_id(0)
        zv = z_ref[...]
        lng_v = lng_ref[...]
        u, vn, rstd, vl = _sgu_ln(zv, SW, lng_v, lnb_ref[...])

        @pl.when(i == 0)
        def _():
            for s in (dlg_ref, dlb_ref, dws_ref, dbacc_s):
                s[...] = jnp.zeros_like(s)

        for g in range(G):
            cs = slice(g * SGU_GROUP, (g + 1) * SGU_GROUP)
            tri, wc = _causal(ws_ref, g)
            for n in range(nch):
                rs = slice(n * SGU_CHUNK, (n + 1) * SGU_CHUNK)
                blk = vl[rs, cs].astype(BF16)
                m = jnp.dot(wc, blk, preferred_element_type=F32) + be_ref[:, cs]
                dyv = dy_ref[rs, cs]
                du_s[rs, cs] = dyv * m
                dm = dyv * u[rs, cs]
                dvl_s[rs, cs] = _bdot(wc, dm, "tn")
                dws_ref[g] += jnp.where(tri, _bdot(dm, blk, "nt"), 0.0)
                dbacc_s[:, cs] += dm

        dvl = dvl_s[...]
        dlg_ref[...] += _colsum(dvl * vn)
        dlb_ref[...] += _colsum(dvl)
        dvn = dvl * lng_v
        dvv = rstd * (dvn - _mean(dvn) - vn * _mean(dvn * vn))
        gp = _gelu_grad(zv)
        dz_ref[:, :SW] = (du_s[...] * gp[:, :SW]).astype(BF16)
        dz_ref[:, SW:] = (dvv * gp[:, SW:]).astype(BF16)

        @pl.when(i == nt - 1)
        def _():
            lane = lax.broadcasted_iota(jnp.int32, (SGU_CHUNK, LANE), 1)
            out = jnp.zeros((SGU_CHUNK, LANE), F32)
            for g in range(G):
                col = jnp.sum(dbacc_s[:, g * SGU_GROUP:(g + 1) * SGU_GROUP], axis=1, keepdims=True)
                out = jnp.where(lane == g, col, out)
            db_ref[...] = out

    whole = lambda arr: pl.BlockSpec(arr.shape, lambda i, nd=arr.ndim: (0,) * nd)
    acc_shapes = [(1, SW), (1, SW), ws.shape, (SGU_CHUNK, LANE)]
    return pl.pallas_call(
        body, name=name, grid=(nt,),
        in_specs=[pl.BlockSpec((tr, 2 * SW), lambda i: (i, zblock)), pl.BlockSpec((tr, SW), lambda i: (i, 0)),
                  whole(lng), whole(lnb), whole(ws), whole(bexp), pl.BlockSpec(memory_space=pl.ANY)],
        out_specs=([pl.BlockSpec((tr, 2 * SW), lambda i: (i, zblock))]
                   + [pl.BlockSpec(s, lambda i, nd=len(s): (0,) * nd) for s in acc_shapes]),
        out_shape=[jax.ShapeDtypeStruct(dproj.shape, BF16)] + [jax.ShapeDtypeStruct(s, F32) for s in acc_shapes],
        scratch_shapes=[pltpu.VMEM((tr, SW), F32), pltpu.VMEM((tr, SW), F32), pltpu.VMEM((SGU_CHUNK, SW), F32)],
        input_output_aliases={6: 0}, compiler_params=_params())(proj, dyb, lng, lnb, ws, bexp, dproj)


_HBM = pl.BlockSpec(memory_space=pltpu.HBM)
_SEM = pl.BlockSpec(memory_space=pltpu.SEMAPHORE)
_DATAFLOW = pltpu.SideEffectType.DATAFLOW_SIDE_EFFECTING


def _mesh_place(chips=False):
    x, y, c = lax.axis_index("x"), lax.axis_index("y"), lax.axis_index("c")
    return x, y, c, (2 * x + y if chips else 4 * x + 2 * y + c)


def _peer(x, y, c, rel, chips=False):
    px = 1 - x if rel & 4 else x
    py = 1 - y if rel & 2 else y
    pc = 1 - c if rel & 1 else c
    return (px, py, pc), (2 * px + py if chips else 4 * px + 2 * py + pc)


ALL_PEERS = tuple(range(1, N_DEV))
SIBLING = (1,)
SAME_CORE = (2, 4, 6)
SIBLINGS_CORE = (3, 5, 7)


def _exchange_start(groups, name, rels=ALL_PEERS, chips=False):
    flat = [t for g in groups for t in g]
    sizes = [len(g) for g in groups]
    n, ng = len(flat), len(groups)
    srcs = [pltpu.with_memory_space_constraint(a, pltpu.HBM) for a, _ in flat]
    lands = [pltpu.with_memory_space_constraint(lax.empty(((N_DEV,) + a.shape) if isg else a.shape, a.dtype), pltpu.HBM)
             for a, isg in flat]

    def body(*refs):
        ins, lnd, sems, token = refs[:n], refs[n:2 * n], refs[2 * n:2 * n + 3 * ng], refs[-1]
        x, y, c, me = _mesh_place(chips)
        j0 = 0
        for gi, sz in enumerate(sizes):
            for rel in rels:
                dev, slot = _peer(x, y, c, rel, chips)
                for jj in range(sz):
                    j = j0 + jj
                    pltpu.make_async_remote_copy(
                        src_ref=ins[j] if flat[j][1] else ins[j].at[slot], dst_ref=lnd[j].at[me],
                        send_sem=sems[3 * gi].at[jj * (N_DEV - 1) + rel - 1], recv_sem=sems[3 * gi + 1].at[jj * (N_DEV - 1) + rel - 1],
                        device_id=dev, device_id_type=pl.DeviceIdType.MESH).start()
            for jj in range(sz):
                j = j0 + jj
                pltpu.make_async_copy(ins[j] if flat[j][1] else ins[j].at[me], lnd[j].at[me], sems[3 * gi + 2].at[jj]).start()
            j0 += sz
        token[...] = jnp.zeros_like(token)

    sem_shapes = [pltpu.SemaphoreType.DMA((k,)) for sz in sizes for k in (sz * (N_DEV - 1), sz * (N_DEV - 1), sz)]
    res = pl.pallas_call(
        body, name=name,
        out_shape=(*sem_shapes, *[pltpu.HBM(a.shape, a.dtype) for a in srcs], *[pltpu.HBM(a.shape, a.dtype) for a in lands],
                   jax.ShapeDtypeStruct((SUBLANE, LANE), F32)),
        in_specs=[_HBM] * (2 * n), out_specs=(*[_SEM] * (3 * ng), *[_HBM] * (2 * n), pl.BlockSpec(memory_space=pltpu.VMEM)),
        input_output_aliases={i: 3 * ng + i for i in range(2 * n)},
        compiler_params=pltpu.CompilerParams(has_side_effects=_DATAFLOW))(*srcs, *lands)
    sems, thru, token = res[:3 * ng], res[3 * ng:3 * ng + 2 * n], res[-1]
    handle, j0 = [], 0
    for gi, sz in enumerate(sizes):
        handle.append(dict(kinds=[k for _, k in groups[gi]], chips=chips, srcs=list(thru[j0:j0 + sz]), lands=list(thru[n + j0:n + j0 + sz]),
                           sems=list(sems[3 * gi:3 * gi + 3])))
        j0 += sz
    return handle, token


def _exchange_wait(group, after, name, rels=ALL_PEERS, local=True):
    kinds, sz = group["kinds"], len(group["kinds"])
    relay = group.get("relay", [])

    def body(*refs):
        ins, lnd, (ssem, rsem, lsem) = refs[:sz], refs[sz:2 * sz], refs[2 * sz:2 * sz + 3]
        x, y, c, me = _mesh_place(group["chips"])
        for rel in rels:
            dev, slot = _peer(x, y, c, rel, group["chips"])
            for jj in range(sz):
                cp = pltpu.make_async_remote_copy(
                    src_ref=ins[jj] if kinds[jj] else ins[jj].at[slot], dst_ref=lnd[jj].at[slot],
                    send_sem=ssem.at[jj * (N_DEV - 1) + rel - 1], recv_sem=rsem.at[jj * (N_DEV - 1) + rel - 1],
                    device_id=dev, device_id_type=pl.DeviceIdType.MESH)
                cp.wait_send()
                cp.wait_recv()
        if local:
            for jj in range(sz):
                pltpu.make_async_copy(ins[jj] if kinds[jj] else ins[jj].at[me], lnd[jj].at[me], lsem.at[jj]).wait()
        if relay:
            fsend, frecv = refs[2 * sz + 3:2 * sz + 5]
            dev = _peer(x, y, c, 1)[0]
            for q, (mine, theirs) in enumerate(zip(SAME_CORE, SIBLINGS_CORE)):
                for jj in range(sz):
                    cp = pltpu.make_async_remote_copy(
                        src_ref=lnd[jj].at[_peer(x, y, c, mine)[1]], dst_ref=lnd[jj].at[_peer(x, y, c, theirs)[1]],
                        send_sem=fsend.at[jj * len(SAME_CORE) + q], recv_sem=frecv.at[jj * len(SAME_CORE) + q],
                        device_id=dev, device_id_type=pl.DeviceIdType.MESH)
                    cp.wait_send()
                    cp.wait_recv()

    arrays = group["srcs"] + group["lands"]
    sems = group["sems"] + relay
    res = pl.pallas_call(
        body, name=name, out_shape=[pltpu.HBM(a.shape, a.dtype) for a in arrays],
        in_specs=[_HBM] * (2 * sz) + [_SEM] * len(sems) + [pl.BlockSpec(memory_space=pl.ANY)], out_specs=[_HBM] * (2 * sz),
        input_output_aliases={i: i for i in range(2 * sz)},
        compiler_params=pltpu.CompilerParams(has_side_effects=_DATAFLOW))(*arrays, *sems, after)
    return dict(group, srcs=list(res[:sz]), lands=list(res[sz:]), relay=[])


def _relay_start(group, name):
    sz = len(group["kinds"])
    nq = len(SAME_CORE)

    def body(*refs):
        lnd, fsend, frecv, token = refs[:sz], refs[sz], refs[sz + 1], refs[-1]
        x, y, c, _ = _mesh_place()
        dev = _peer(x, y, c, 1)[0]
        for q, rel in enumerate(SAME_CORE):
            slot = _peer(x, y, c, rel)[1]
            for jj in range(sz):
                pltpu.make_async_remote_copy(
                    src_ref=lnd[jj].at[slot], dst_ref=lnd[jj].at[slot], send_sem=fsend.at[jj * nq + q], recv_sem=frecv.at[jj * nq + q],
                    device_id=dev, device_id_type=pl.DeviceIdType.MESH).start()
        token[...] = jnp.zeros_like(token)

    lands = group["lands"]
    res = pl.pallas_call(
        body, name=name,
        out_shape=(pltpu.SemaphoreType.DMA((sz * nq,)), pltpu.SemaphoreType.DMA((sz * nq,)), *[pltpu.HBM(a.shape, a.dtype) for a in lands],
                   jax.ShapeDtypeStruct((SUBLANE, LANE), F32)),
        in_specs=[_HBM] * sz, out_specs=(_SEM, _SEM, *[_HBM] * sz, pl.BlockSpec(memory_space=pltpu.VMEM)),
        input_output_aliases={i: 2 + i for i in range(sz)},
        compiler_params=pltpu.CompilerParams(has_side_effects=_DATAFLOW))(*lands)
    return dict(group, lands=list(res[2:2 + sz]), relay=[res[0], res[1]]), res[-1]


def _sibling_swap(arrays, handle, after, name):
    start = handle is None
    n = len(arrays) if start else len(handle["srcs"])
    chips = N_DEV // 2
    if start:
        srcs = [pltpu.with_memory_space_constraint(a.reshape(chips, 2, *a.shape[1:]), pltpu.HBM) for a in arrays]
        lands = [pltpu.with_memory_space_constraint(lax.empty((chips,) + a.shape[1:], a.dtype), pltpu.HBM) for a in arrays]
    else:
        srcs, lands = handle["srcs"], handle["lands"]

    def body(*refs):
        ins, lnd, ssem, rsem = refs[:n], refs[n:2 * n], refs[2 * n], refs[2 * n + 1]
        x, y, c, _ = _mesh_place()
        dev = _peer(x, y, c, 1)[0]
        for q in range(chips):
            for j in range(n):
                cp = pltpu.make_async_remote_copy(
                    src_ref=ins[j].at[q, 1 - c], dst_ref=lnd[j].at[q], send_sem=ssem.at[j * chips + q], recv_sem=rsem.at[j * chips + q],
                    device_id=dev, device_id_type=pl.DeviceIdType.MESH)
                if start:
                    cp.start()
                else:
                    cp.wait_send()
                    cp.wait_recv()
        if start:
            refs[-1][...] = jnp.zeros_like(refs[-1])

    thru = [pltpu.HBM(a.shape, a.dtype) for a in srcs + lands]
    effect = pltpu.CompilerParams(has_side_effects=_DATAFLOW)
    if start:
        res = pl.pallas_call(
            body, name=name, out_shape=(pltpu.SemaphoreType.DMA((n * chips,)), pltpu.SemaphoreType.DMA((n * chips,)), *thru,
                                        jax.ShapeDtypeStruct((SUBLANE, LANE), F32)),
            in_specs=[_HBM] * (2 * n), out_specs=(_SEM, _SEM, *[_HBM] * (2 * n), pl.BlockSpec(memory_space=pltpu.VMEM)),
            input_output_aliases={i: 2 + i for i in range(2 * n)}, compiler_params=effect)(*srcs, *lands)
        return dict(srcs=list(res[2:2 + n]), lands=list(res[2 + n:2 + 2 * n]), sems=[res[0], res[1]]), res[-1]
    res = pl.pallas_call(
        body, name=name, out_shape=thru, in_specs=[_HBM] * (2 * n) + [_SEM, _SEM, pl.BlockSpec(memory_space=pl.ANY)],
        out_specs=[_HBM] * (2 * n), input_output_aliases={i: i for i in range(2 * n)}, compiler_params=effect)(
            *srcs, *lands, *handle["sems"], after)
    return dict(handle, srcs=list(res[:n]), lands=list(res[n:]))


def _pair_add(mine, theirs, core, name):
    chips, _, rows, w = mine.shape
    tm = _pick(rows, (256, 128, 64, 32, 16))

    def body(core_ref, a_ref, b_ref, o_ref):
        o_ref[...] = (a_ref[...].astype(F32) + b_ref[...].astype(F32)).astype(o_ref.dtype)

    return pl.pallas_call(
        body, name=name, out_shape=jax.ShapeDtypeStruct(theirs.shape, theirs.dtype),
        grid_spec=pltpu.PrefetchScalarGridSpec(
            num_scalar_prefetch=1, grid=(chips, rows // tm),
            in_specs=[pl.BlockSpec((None, None, tm, w), lambda q, i, core_ref: (q, core_ref[0], i, 0)),
                      pl.BlockSpec((None, tm, w), lambda q, i, core_ref: (q, i, 0))],
            out_specs=pl.BlockSpec((None, tm, w), lambda q, i, core_ref: (q, i, 0))),
        compiler_params=_params())(core, mine, theirs)


def _adamw(w, m, v, gparts, name, after=None):
    R, C = w.shape[-2:]
    tm = _pick(R, (256, 128, 64, 32, 16, 8))
    order = [] if after is None else [after]

    def body(w_ref, m_ref, v_ref, g_ref, *rest):
        go, do, mo, vo = rest[len(order):]
        g = g_ref[0].astype(F32)
        for j in range(1, gparts.shape[0]):
            g = g + g_ref[j].astype(F32)
        mn = ADAM_B1 * m_ref[...] + (1.0 - ADAM_B1) * g
        vn = ADAM_B2 * v_ref[...] + (1.0 - ADAM_B2) * (g * g)
        m_hat = mn / (1.0 - ADAM_B1 ** ADAM_STEP)
        v_hat = vn / (1.0 - ADAM_B2 ** ADAM_STEP)
        go[...] = g
        do[...] = -ADAM_LR * (m_hat / (jnp.sqrt(v_hat) + ADAM_EPS) + ADAM_WD * w_ref[...])
        mo[...] = mn
        vo[...] = vn

    row = pl.BlockSpec((None, tm, C), lambda i: (0, i, 0)) if w.ndim == 3 else pl.BlockSpec((tm, C), lambda i: (i, 0))
    return pl.pallas_call(
        body, name=name, grid=(R // tm,),
        in_specs=[row, row, row, pl.BlockSpec((gparts.shape[0], tm, C), lambda i: (0, i, 0))] + [pl.BlockSpec(memory_space=pl.ANY)] * len(order),
        out_specs=[row] * 4, out_shape=[jax.ShapeDtypeStruct(w.shape, F32)] * 4, compiler_params=_params())(w, m, v, gparts, *order)


def _pack(arrays):
    parts = []
    for a in arrays:
        f = a.reshape(1, -1)
        pad = _ceil_to(f.shape[1], SUBLANE * LANE) - f.shape[1]
        f = jnp.concatenate([f, jnp.zeros((1, pad), f.dtype)], axis=1) if pad else f
        parts.append(f.reshape(-1, LANE))
    rows = sum(p.shape[0] for p in parts)
    pad = _ceil_to(rows, 64) - rows
    return jnp.concatenate(parts + ([jnp.zeros((pad, LANE), parts[0].dtype)] if pad else []), axis=0)


def _unpack(buf, shapes):
    out, row = [], 0
    for s in shapes:
        size = 1
        for d in s:
            size *= d
        rows = _ceil_to(size, SUBLANE * LANE) // LANE
        out.append(buf[row:row + rows].reshape(1, -1)[:, :size].reshape(s))
        row += rows
    return out


def kernel(x, norm_mix_g, w_in, shift_mu, w0, w_lora_up, a0, a_lora_up, g_lora_up, k_k, k_a, r_k, lnx_g, lnx_b, w_proj_rwkv, sgu_ln_g, sgu_ln_b, sgu_w, sgu_b, w_proj_sgu, w_out, norm_ffn_g, w_ffn_gate, w_ffn_up, w_ffn_down, norm_final_g, loss_target, m_norm_mix_g, m_w_in, m_shift_mu, m_w0, m_w_lora_up, m_a0, m_a_lora_up, m_g_lora_up, m_k_k, m_k_a, m_r_k, m_lnx_g, m_lnx_b, m_w_proj_rwkv, m_sgu_ln_g, m_sgu_ln_b, m_sgu_w, m_sgu_b, m_w_proj_sgu, m_w_out, m_norm_ffn_g, m_w_ffn_gate, m_w_ffn_up, m_w_ffn_down, m_norm_final_g, v_norm_mix_g, v_w_in, v_shift_mu, v_w0, v_w_lora_up, v_a0, v_a_lora_up, v_g_lora_up, v_k_k, v_k_a, v_r_k, v_lnx_g, v_lnx_b, v_w_proj_rwkv, v_sgu_ln_g, v_sgu_ln_b, v_sgu_w, v_sgu_b, v_w_proj_sgu, v_w_out, v_norm_ffn_g, v_w_ffn_gate, v_w_ffn_up, v_w_ffn_down, v_norm_final_g):
    weights = dict(norm_mix_g=norm_mix_g, w_in=w_in, shift_mu=shift_mu, w0=w0, w_lora_up=w_lora_up, a0=a0, a_lora_up=a_lora_up,
                   g_lora_up=g_lora_up, k_k=k_k, k_a=k_a, r_k=r_k, lnx_g=lnx_g, lnx_b=lnx_b, w_proj_rwkv=w_proj_rwkv,
                   sgu_ln_g=sgu_ln_g, sgu_ln_b=sgu_ln_b, sgu_w=sgu_w, sgu_b=sgu_b, w_proj_sgu=w_proj_sgu, w_out=w_out,
                   norm_ffn_g=norm_ffn_g, w_ffn_gate=w_ffn_gate, w_ffn_up=w_ffn_up, w_ffn_down=w_ffn_down, norm_final_g=norm_final_g)
    m_in = dict(norm_mix_g=m_norm_mix_g, w_in=m_w_in, shift_mu=m_shift_mu, w0=m_w0, w_lora_up=m_w_lora_up, a0=m_a0,
                a_lora_up=m_a_lora_up, g_lora_up=m_g_lora_up, k_k=m_k_k, k_a=m_k_a, r_k=m_r_k, lnx_g=m_lnx_g, lnx_b=m_lnx_b,
                w_proj_rwkv=m_w_proj_rwkv, sgu_ln_g=m_sgu_ln_g, sgu_ln_b=m_sgu_ln_b, sgu_w=m_sgu_w, sgu_b=m_sgu_b,
                w_proj_sgu=m_w_proj_sgu, w_out=m_w_out, norm_ffn_g=m_norm_ffn_g, w_ffn_gate=m_w_ffn_gate, w_ffn_up=m_w_ffn_up,
                w_ffn_down=m_w_ffn_down, norm_final_g=m_norm_final_g)
    v_in = dict(norm_mix_g=v_norm_mix_g, w_in=v_w_in, shift_mu=v_shift_mu, w0=v_w0, w_lora_up=v_w_lora_up, a0=v_a0,
                a_lora_up=v_a_lora_up, g_lora_up=v_g_lora_up, k_k=v_k_k, k_a=v_k_a, r_k=v_r_k, lnx_g=v_lnx_g, lnx_b=v_lnx_b,
                w_proj_rwkv=v_w_proj_rwkv, sgu_ln_g=v_sgu_ln_g, sgu_ln_b=v_sgu_ln_b, sgu_w=v_sgu_w, sgu_b=v_sgu_b,
                w_proj_sgu=v_w_proj_sgu, w_out=v_w_out, norm_ffn_g=v_norm_ffn_g, w_ffn_gate=v_w_ffn_gate, w_ffn_up=v_w_ffn_up,
                w_ffn_down=v_w_ffn_down, norm_final_g=v_norm_final_g)
    names = list(weights)
    col_sharded = ("w_in", "w_lora_up", "a_lora_up", "g_lora_up", "w_proj_rwkv", "w_proj_sgu", "w_ffn_gate", "w_ffn_up")
    row_sharded = ("w_out", "w_ffn_down")
    sharded = [n for n in names if n in col_sharded or n in row_sharded]
    small = [n for n in names if n not in sharded]

    xs, tgt = x[0], loss_target[0]
    T, D = xs.shape
    RW = w0.shape[1]
    H = RW // HEAD
    SW = sgu_ln_g.shape[1]
    G = sgu_w.shape[1]
    assert 2 * SW == D, "the projection layout takes the SGU part to be as wide as a gate"
    lay = _rwkv_layout(RW, w_lora_up.shape[1], a_lora_up.shape[1], g_lora_up.shape[1], D)
    _, pw, _, rcp = lay
    icp = rcp + 3 * D
    b_ga, b_gb, b_z = rcp // D, rcp // D + 1, rcp // D + 2

    gather_groups = [["w_in", "w_lora_up", "a_lora_up", "g_lora_up"], ["w_proj_rwkv", "w_proj_sgu", "w_out"],
                     ["w_ffn_gate"], ["w_ffn_up"], ["w_ffn_down"]]
    gather, gather_token = _exchange_start([[(weights[n][0].astype(BF16), True) for n in grp] for grp in gather_groups],
                                           "gather_start", rels=SIBLING + SAME_CORE)
    full = {}
    relay_tokens = {}
    joined = lambda g: g.transpose(1, 0, 2).reshape(g.shape[1], -1)

    def relay_weights(gi, after, name):
        arrived = _exchange_wait(gather[gi], after, "gather_wait_ici_" + name, rels=SAME_CORE, local=False)
        gather[gi], relay_tokens[gi] = _relay_start(arrived, "gather_relay_" + name)

    def take_weights(gi, after, name):
        done = _exchange_wait(gather[gi], after, "gather_wait_d2d_" + name, rels=SIBLING)
        for n, g in zip(gather_groups[gi], done["lands"]):
            full[n] = g.reshape(-1, g.shape[2]) if n in row_sharded else g

    packed = [_pack([d[n] for n in small] + [gather_token]) for d in (weights, m_in, v_in)]
    n1 = _rms_fwd(xs, norm_mix_g, "rms_mix", deps=[gather_token, *packed])
    relay_weights(0, n1, "in")
    take_weights(0, relay_tokens[0], "in")
    W_in = _w_in_to_proj(full["w_in"], lay, D, "w_in_layout")
    lora = [_pad_rows(joined(full[n]), rows) for n, rows in zip(("w_lora_up", "a_lora_up", "g_lora_up"), pw[3:])]
    mu_p = _pad_rwkv_cols(shift_mu, lay)
    rsmall = [w0, a0, k_k, k_a]
    hp = [lnx_g.reshape(H, 1, HEAD), lnx_b.reshape(H, 1, HEAD), r_k.reshape(H, 1, HEAD)]
    ws = sgu_w[0]
    bexp = jnp.repeat(sgu_b[0].T, SGU_GROUP, axis=1)
    gf = norm_final_g.reshape(1, D)

    proj = _matmul(n1, W_in, mode="nn", out_dtype=F32, name="proj_in")
    ga, gb = (proj, D, b_ga), (proj, D, b_gb)
    r_h, lw_h, k2_h, v_h, aa_h, bb_h, g_h = _rwkv_pre(proj, mu_p, rsmall, lora, lay, "rwkv_pre")
    wkv_in = [r_h, lw_h, k2_h, v_h, aa_h, bb_h]
    y_h, states = _wkv_fwd(*wkv_in, "wkv_fwd")
    relay_weights(1, y_h, "proj")
    relay_weights(2, relay_tokens[1], "ffn_gate")
    ya = _head_post(y_h, r_h, k2_h, v_h, g_h, hp, "head_post", deps=[relay_tokens[2]])
    relay_weights(3, ya, "ffn_up")
    yb = _sgu_fwd(proj, b_z, sgu_ln_g, sgu_ln_b, ws, bexp, "sgu_fwd")
    take_weights(1, ya, "proj")
    pa = _matmul(ya, full["w_proj_rwkv"], mode="nn", out_dtype=F32, name="proj_a", deps=[relay_tokens[3]])

    def merge_fn(pb_v, pa_v, ga_v, gb_v):
        return pb_v, _sigmoid(ga_v) * pa_v + _sigmoid(gb_v) * pb_v
    pb, merged = _matmul(yb, full["w_proj_sgu"], mode="nn", name="proj_b_merge",
                         epi=(merge_fn, [pa, (proj, b_ga * D), (proj, b_gb * D)], [F32, BF16]))
    h1 = _matmul(merged, full["w_out"], mode="nn", out_dtype=F32, name="out_proj", add=xs)
    n2 = _rms_fwd(h1, norm_ffn_g, "rms_ffn")
    relay_weights(4, n2, "ffn_down")
    take_weights(2, n2, "ffn_gate")
    gt = _matmul(n2, full["w_ffn_gate"], mode="nn", out_dtype=F32, name="ffn_gate", out_blocks=N_DEV, deps=[relay_tokens[4]])
    take_weights(3, gt, "ffn_up")

    def act_fn(up_v, gt_v):
        return up_v, gt_v * _sigmoid(gt_v) * up_v
    up, act = _matmul(n2, full["w_ffn_up"], mode="nn", name="ffn_up_act", out_blocks=N_DEV, epi=(act_fn, [gt], [F32, BF16]))
    take_weights(4, act, "ffn_down")
    h2 = _matmul(act, full["w_ffn_down"], mode="nn", out_dtype=F32, name="ffn_down", add=h1)

    def final_fn(rv, pv):
        (h_v, t_v), (g_v,) = rv, pv
        r = lax.rsqrt(_mean(h_v * h_v) + RMS_EPS)
        yn = h_v * r
        e = yn * g_v - t_v
        loss = 0.5 * jnp.sum(_mean(e * e))
        dout = e * (1.0 / D)
        dyg = dout * g_v
        dh = r * (dyg - yn * _mean(dyg * yn))
        return [dh, dh], [jnp.full((1, LANE), loss, F32), _colsum(dout * yn)]
    dh2, dh2_bf, loss_part, d_gf = _rowwise(final_fn, [h2, tgt], [gf], [(D, F32), (D, BF16)], [(1, LANE), (1, D)], name="final_loss")

    grads = {}

    def start_scatter(group, name, extra=()):
        blocks = [(grads[n].reshape(N_DEV, -1, grads[n].shape[1]) if n in row_sharded else grads[n], False) for n in group]
        (handle,), token = _exchange_start([blocks + list(extra)], name)
        return handle, token

    def dact_fn(d_v, gt_v, up_v):
        s = _sigmoid(gt_v)
        return d_v * up_v * (s * (1.0 + gt_v * (1.0 - s))), d_v * gt_v * s
    dgt, dup = _matmul(dh2_bf, full["w_ffn_down"], mode="nt", name="d_ffn_act", out_blocks=N_DEV,
                       epi=(dact_fn, [gt, up], [BF16, BF16]))
    grads["w_ffn_down"] = _matmul(act, dh2_bf, mode="tn", out_dtype=BF16, name="dw_ffn_down")
    dn2 = _matmul(dgt, full["w_ffn_gate"], mode="nt", out_dtype=F32, name="dn2_gate")
    dn2 = _matmul(dup, full["w_ffn_up"], mode="nt", out_dtype=F32, name="dn2_up", add=dn2)
    grads["w_ffn_gate"] = _matmul(n2, dgt, mode="tn", out_dtype=BF16, name="dw_ffn_gate", out_blocks=N_DEV)
    grads["w_ffn_up"] = _matmul(n2, dup, mode="tn", out_dtype=BF16, name="dw_ffn_up", out_blocks=N_DEV)
    scatter_groups = [["w_ffn_down", "w_ffn_gate", "w_ffn_up"], ["w_out", "w_proj_rwkv", "w_proj_sgu"],
                      ["w_in", "w_lora_up", "a_lora_up", "g_lora_up"]]
    scatter_ffn, token_ffn = start_scatter(scatter_groups[0], "scatter_start_ffn")
    dh1, dh1_bf, d_g2 = _rms_bwd(dn2, h1, dh2, norm_ffn_g, "rms_ffn_bwd", deps=[token_ffn])
    dmerged = _matmul(dh1_bf, full["w_out"], mode="nt", out_dtype=F32, name="d_merged")
    grads["w_out"] = _matmul(merged, dh1_bf, mode="tn", out_dtype=BF16, name="dw_out")

    def dmerge_fn(rv, pv):
        d_v, ga_v, gb_v, pa_v, pb_v = rv
        sa, sb = _sigmoid(ga_v), _sigmoid(gb_v)
        dgates = jnp.concatenate([d_v * pa_v * sa * (1.0 - sa), d_v * pb_v * sb * (1.0 - sb)], axis=1)
        return [dgates, d_v * sa, d_v * sb], []
    dproj, dpa, dpb = _rowwise(dmerge_fn, [dmerged, ga, gb, pa, pb], [],
                               [(2 * D, BF16, icp, b_ga // 2, None), (D, BF16), (D, BF16)], [], name="d_merge")
    dya = _matmul(dpa, full["w_proj_rwkv"], mode="nt", out_dtype=F32, name="d_ya")
    dyb = _matmul(dpb, full["w_proj_sgu"], mode="nt", out_dtype=F32, name="d_yb")
    grads["w_proj_rwkv"] = _matmul(ya, dpa, mode="tn", out_dtype=BF16, name="dw_proj_a", out_blocks=N_DEV)
    grads["w_proj_sgu"] = _matmul(yb, dpb, mode="tn", out_dtype=BF16, name="dw_proj_b", out_blocks=N_DEV)
    scatter_mid, token_mid = start_scatter(scatter_groups[1], "scatter_start_mid")
    dproj, d_lng, d_lnb, d_ws, d_bs = _sgu_bwd(proj, b_z, dyb, sgu_ln_g, sgu_ln_b, ws, bexp, dproj, "sgu_bwd")

    dy_h, dr1, dk1, dv1, dg_h, d_lnxg, d_lnxb, d_rk = _head_post_bwd(dya, y_h, r_h, k2_h, v_h, g_h, hp, "head_post_bwd",
                                                                     deps=[token_mid])
    dr2, dlw_h, dk2b, dv2, daa, dbb = _wkv_bwd(*wkv_in, states, dy_h, "wkv_bwd")
    dps, d_mu, d_w0, d_a0, d_kk, d_ka, d_wlw, d_wla, d_wlg = _rwkv_pre_bwd(
        proj, mu_p, rsmall, lora, [dr1, dr2, dk1, dk2b, dv1, dv2, dlw_h, daa, dbb, dg_h], lay, "rwkv_pre_bwd")
    dproj = _shift_bwd(dps, mu_p, dproj, "shift_bwd")
    split = lambda g: g.reshape(g.shape[0], N_DEV, -1).transpose(1, 0, 2)
    grads["w_in"] = _dw_in_from_proj(_matmul(n1, dproj, mode="tn", out_dtype=BF16, name="dw_in"), lay, D, w_in.shape[2], "dw_in_layout")
    grads["w_lora_up"] = split(d_wlw[:w_lora_up.shape[1]].astype(BF16))
    grads["a_lora_up"] = split(d_wla[:a_lora_up.shape[1]].astype(BF16))
    grads["g_lora_up"] = split(d_wlg[:g_lora_up.shape[1]].astype(BF16))
    out = {}

    arrived = {}

    def update(group, after):
        for n in group:
            out[n] = _adamw(weights[n], m_in[n], v_in[n], arrived[n], "adamw_" + n, after=after)
            after = out[n][0]
        return after

    def update_group(gi, handle, after, name, first=None):
        parts = _exchange_wait(handle, after, "scatter_wait_" + name, rels=SAME_CORE if handle["chips"] else ALL_PEERS)["lands"]
        arrived.update(zip(scatter_groups[gi], parts))
        return update(scatter_groups[gi][:first], after)

    swap, token_swap = _sibling_swap([grads[n] for n in scatter_groups[2]], None, None, "scatter_in_swap_start")
    after = update_group(0, scatter_ffn, token_swap, "ffn", first=2)
    swap = _sibling_swap(None, swap, after, "scatter_in_swap_wait")
    core = lax.axis_index("c").astype(jnp.int32).reshape(1)
    chip_sums = [_pair_add(mine, theirs, core, "scatter_in_add_" + n)
                 for n, mine, theirs in zip(scatter_groups[2], swap["srcs"], swap["lands"])]
    (scatter_in,), token_in = _exchange_start([[(s, False) for s in chip_sums]], "scatter_start_in", rels=SAME_CORE, chips=True)
    dn1 = _matmul(dproj, W_in, mode="nt", out_dtype=F32, name="dn1", deps=[token_in])
    dx, _, d_g1 = _rms_bwd(dn1, xs, dh1, norm_mix_g, "rms_mix_bwd")
    small_grads = dict(norm_mix_g=d_g1, shift_mu=_unpad_rwkv_cols(d_mu, lay), w0=d_w0, a0=d_a0, k_k=d_kk, k_a=d_ka, r_k=d_rk,
                       lnx_g=d_lnxg, lnx_b=d_lnxb, sgu_ln_g=d_lng, sgu_ln_b=d_lnb, sgu_w=d_ws, sgu_b=d_bs[:, :G].T,
                       norm_ffn_g=d_g2, norm_final_g=d_gf)
    (gather_small,), after = _exchange_start([[(_pack([small_grads[n] for n in small] + [jnp.zeros_like(gather_token)]), True)]],
                                             "gather_small_start")
    after = update(scatter_groups[0][2:], after)
    after = update_group(1, scatter_mid, after, "mid")
    after = update_group(2, scatter_in, after, "in")
    small_parts =_exchange_wait(gather_small, after, "gather_small_wait")["lands"][0]
    res = _adamw(*packed, small_parts, "adamw_small")
    unpacked = [_unpack(t, [weights[n].shape for n in small]) for t in res]
    for i, n in enumerate(small):
        out[n] = [u[i] for u in unpacked]

    loss = lax.psum(loss_part[0, 0], ("x", "y", "c"))
    return (loss, dx[None], *[out[n][0] for n in names], *[out[n][1] for n in names],
            *[out[n][2] for n in names], *[out[n][3] for n in names])
```

```python
import jax
import jax.numpy as jnp
from jax import lax
from jax.experimental import pallas as pl
from jax.experimental.pallas import tpu as pltpu

F32 = jnp.float32
BF16 = jnp.bfloat16

N_DEV = 8
LANE = 128
SUBLANE = 8
HEAD = 64
SGU_CHUNK = 128
SGU_GROUP = 128
WKV_CHUNK = 64
RMS_EPS = 1e-6
LN_EPS = 1e-5
LNX_EPS = 64e-5
ADAM_LR, ADAM_B1, ADAM_B2, ADAM_EPS, ADAM_WD, ADAM_STEP = 0.001, 0.9, 0.999, 1e-08, 0.01, 10
VMEM_LIMIT_BYTES = 48 * 1024 * 1024
_SQRT_HALF = 0.7071067811865476
_INV_SQRT_2PI = 0.3989422804014327


def _pick(n, cands):
    for c in cands:
        if n % c == 0:
            return c
    return n


def _ceil_to(n, m):
    return -(-n // m) * m


def _params():
    return pltpu.CompilerParams(vmem_limit_bytes=VMEM_LIMIT_BYTES)


def _tile(n, cap):
    best = 0
    for d in range(LANE, min(n, cap) + 1, LANE):
        if n % d == 0:
            best = d
    return best or n


def _matmul_tiles(M, N, K, a_bytes, b_bytes, o_bytes, has_add, forced):
    tm = forced.get("m") or _tile(M, 1024)
    tn = forced.get("n") or _tile(N, 1024)
    tk = forced.get("k") or _tile(K, 2048)

    def vmem(tm, tn, tk):
        acc = tm * tn * 4 if tk < K else 0
        return 2 * (tm * tk * a_bytes + tk * tn * b_bytes + tm * tn * (o_bytes + (4 if has_add else 0))) + acc

    while vmem(tm, tn, tk) > (VMEM_LIMIT_BYTES * 3) // 4:
        if "k" not in forced and tk > 512 and _tile(K, tk // 2) < tk:
            tk = _tile(K, tk // 2)
        elif "m" not in forced and _tile(M, tm // 2) < tm:
            tm = _tile(M, tm // 2)
        else:
            break
    return tm, tn, tk


def _matmul(a, b, *, mode, out_dtype=F32, name, add=None, deps=(), out_blocks=0, epi=None, b2=None):
    def view(x):
        return (x.shape[1], x.shape[0] * x.shape[2], x.shape[2]) if x.ndim == 3 else (x.shape[0], x.shape[1], 0)

    (ar, ac, aw), (br, bc, bw) = view(a), view(b)
    a_col, b_col = {"nn": ("k", "n"), "nt": ("k", "k"), "tn": ("m", "n")}[mode]
    if mode == "nn":
        M, K, K2, N = ar, ac, br, bc
    elif mode == "nt":
        M, K, N, K2 = ar, ac, br, bc
    else:
        K, M, K2, N = ar, ac, br, bc
    assert K == K2, (a.shape, b.shape, mode)
    forced = {}
    for dim, w in ((a_col, aw), (b_col, bw), ("n", N // out_blocks if out_blocks else 0)):
        if w:
            assert forced.get(dim, w) == w
            forced[dim] = w
    has_add = add is not None
    tile_bytes = (sum(jnp.dtype(d).itemsize for d in epi[2]) + sum((e[0] if isinstance(e, tuple) else e).dtype.itemsize for e in epi[1])
                  if epi is not None else jnp.dtype(out_dtype).itemsize)
    tm, tn, tk = _matmul_tiles(M, N, K, a.dtype.itemsize, b.dtype.itemsize, tile_bytes, has_add, forced)
    kb = 1
    if "k" in forced and mode != "tn":
        lanes_ok = all(w or tk % LANE == 0 for w in (aw, bw if mode == "nt" else 1))
        kb = next(c for c in (4, 2, 1) if (K // tk) % c == 0 and (c == 1 or (lanes_ok and c * tk <= 1536)))
    nk = K // (tk * kb)
    dn = {"nn": (((1,), (0,)), ((), ())), "nt": (((1,), (1,)), ((), ())), "tn": (((0,), (0,)), ((), ()))}[mode]
    pick = {"m": lambda i, j, k: i, "n": lambda i, j, k: j, "k": lambda i, j, k: k}
    size = {"m": tm, "n": tn, "k": tk}

    def spec(blocked, row_dim, col_dim):
        rf, cf = pick[row_dim], pick[col_dim]
        reps = {d: (kb if d == "k" else 1) for d in (row_dim, col_dim)}
        if blocked:
            lead = kb if col_dim == "k" and kb > 1 else None
            return pl.BlockSpec((lead, size[row_dim], size[col_dim]), lambda i, j, k: (cf(i, j, k), rf(i, j, k), 0))
        return pl.BlockSpec((size[row_dim] * reps[row_dim], size[col_dim] * reps[col_dim]), lambda i, j, k: (rf(i, j, k), cf(i, j, k)))

    def k_part(ref, blocked, k_on_rows, j):
        if kb == 1:
            return ref[...]
        if blocked:
            return ref[j]
        return ref[j * tk:(j + 1) * tk, :] if k_on_rows else ref[:, j * tk:(j + 1) * tk]

    a_spec = spec(aw, "k" if mode == "tn" else "m", a_col)
    b_spec = spec(bw, "n" if mode == "nt" else "k", b_col)
    o_spec = spec(out_blocks, "m", "n")
    epi_fn, epi_ins, epi_dtypes = epi if epi is not None else (None, [], [out_dtype])
    epi_ins = [e if isinstance(e, tuple) else (e, None) for e in epi_ins]
    n_epi = len(epi_ins)
    twin = b2 is not None
    assert not twin or (nk == 1 and kb == 1 and epi is not None and b2.shape == b.shape)
    n_in = 2 + twin + has_add + n_epi + len(deps)
    n_out = len(epi_dtypes)

    def body(*refs):
        a_ref, b_ref = refs[0], refs[1]
        add_ref = refs[2 + twin] if has_add else None
        epi_refs = refs[2 + twin + has_add:2 + twin + has_add + n_epi]
        o_refs = refs[n_in:n_in + n_out]
        part = None
        for q in range(kb):
            a_q = k_part(a_ref, aw and a_col == "k", False, q)
            b_q = k_part(b_ref, bw and b_col == "k", mode == "nn", q)
            prod = lax.dot_general(a_q.astype(BF16), b_q.astype(BF16), dn, preferred_element_type=F32)
            part = prod if part is None else part + prod
        second = [lax.dot_general(a_ref[...].astype(BF16), refs[2][...].astype(BF16), dn, preferred_element_type=F32)] if twin else []

        def finish(res):
            outs = epi_fn(res, *second, *[e[...] for e in epi_refs]) if epi_fn is not None else (res,)
            for o_ref, val in zip(o_refs, outs):
                o_ref[...] = val.astype(o_ref.dtype)

        if nk == 1:
            finish(part + add_ref[...] if has_add else part)
            return
        acc_ref = refs[-1]
        kk = pl.program_id(2)

        @pl.when(kk == 0)
        def _():
            acc_ref[...] = part + add_ref[...] if has_add else part

        @pl.when(kk > 0)
        def _():
            acc_ref[...] += part

        @pl.when(kk == nk - 1)
        def _():
            finish(acc_ref[...])

    def epi_spec(arr, off):
        if off is None:
            return o_spec
        assert off % tn == 0
        return pl.BlockSpec((tm, tn), lambda i, j, k: (i, j + off // tn))

    ins = [a, b] + ([b2] if twin else []) + ([add] if has_add else []) + [arr for arr, _ in epi_ins] + list(deps)
    in_specs = ([a_spec, b_spec] + ([b_spec] if twin else []) + ([o_spec] if has_add else []) + [epi_spec(arr, off) for arr, off in epi_ins]
                + [pl.BlockSpec(d.shape, lambda i, j, k, nd=d.ndim: (0,) * nd) for d in deps])
    o_shape = (out_blocks, M, tn) if out_blocks else (M, N)
    res = pl.pallas_call(
        body, name=name, grid=(M // tm, N // tn, nk), in_specs=in_specs, out_specs=[o_spec] * n_out,
        out_shape=[jax.ShapeDtypeStruct(o_shape, dt) for dt in epi_dtypes],
        scratch_shapes=[pltpu.VMEM((tm, tn), F32)] if nk > 1 else [],
        compiler_params=_params())(*ins)
    return res[0] if epi is None else list(res)


def _rowwise(fn, rows, pars, row_outs, acc_outs, *, name, tm=256, deps=()):
    rows = [r if isinstance(r, tuple) else (r, r.shape[1], 0) for r in rows]
    row_outs = [o if len(o) == 5 else (o[0], o[1], o[0], 0, None) for o in row_outs]
    aliased = [(k, o[4]) for k, o in enumerate(row_outs) if o[4] is not None]
    R = rows[0][0].shape[0]
    if max(w for _, w, _ in rows) > 4096:
        tm = tm // 2
    tm = min(tm, R)
    assert R % tm == 0
    nr, npar = len(rows), len(pars)
    nro = len(row_outs)
    n_in = nr + npar + len(deps) + len(aliased)

    def body(*refs):
        rv = [r[...] for r in refs[:nr]]
        pv = [p[...] for p in refs[nr:nr + npar]]
        outs = refs[n_in:]
        ro, ao = fn(rv, pv)
        first = pl.program_id(0) == 0
        for o_ref, val in zip(outs[:nro], ro):
            o_ref[...] = val.astype(o_ref.dtype)

        @pl.when(first)
        def _():
            for o_ref, val in zip(outs[nro:], ao):
                o_ref[...] = val

        @pl.when(jnp.logical_not(first))
        def _():
            for o_ref, val in zip(outs[nro:], ao):
                o_ref[...] += val

    in_specs = ([pl.BlockSpec((tm, w), lambda i, cb=cb: (i, cb)) for _, w, cb in rows]
                + [pl.BlockSpec(p.shape, lambda i, nd=p.ndim: (0,) * nd) for p in list(pars) + list(deps)]
                + [pl.BlockSpec(memory_space=pl.ANY)] * len(aliased))
    out_shape = ([jax.ShapeDtypeStruct((R, full), dt) for _, dt, full, _, _ in row_outs]
                 + [jax.ShapeDtypeStruct(s, F32) for s in acc_outs])
    out_specs = ([pl.BlockSpec((tm, f), lambda i, cb=cb: (i, cb)) for f, _, _, cb, _ in row_outs]
                 + [pl.BlockSpec(s, lambda i, nd=len(s): (0,) * nd) for s in acc_outs])
    res = pl.pallas_call(body, name=name, grid=(R // tm,), in_specs=in_specs, out_specs=out_specs, out_shape=out_shape,
                         input_output_aliases={n_in - len(aliased) + q: k for q, (k, _) in enumerate(aliased)},
                         compiler_params=_params())(*[r for r, _, _ in rows], *pars, *deps, *[buf for _, buf in aliased])
    return list(res)


def _bdot(a, b, mode="nn"):
    dn = {"nn": (((1,), (0,)), ((), ())), "nt": (((1,), (1,)), ((), ())), "tn": (((0,), (0,)), ((), ()))}[mode]
    return lax.dot_general(a.astype(BF16), b.astype(BF16), dn, preferred_element_type=F32)


def _sigmoid(x):
    return jax.nn.sigmoid(x)


def _softplus(x):
    return jnp.maximum(x, 0.0) + jnp.log1p(jnp.exp(-jnp.abs(x)))


def _gelu(z):
    return 0.5 * z * (1.0 + lax.erf(z * _SQRT_HALF))


def _gelu_grad(z):
    return 0.5 * (1.0 + lax.erf(z * _SQRT_HALF)) + z * jnp.exp(-0.5 * z * z) * _INV_SQRT_2PI


def _mean(x):
    return jnp.mean(x, axis=-1, keepdims=True)


def _colsum(x):
    return jnp.sum(x, axis=0, keepdims=True)


def _rms_fwd(x, g, name, deps=()):
    def fn(rv, pv):
        (xv,), (gv,) = rv, pv
        r = lax.rsqrt(_mean(xv * xv) + RMS_EPS)
        return [xv * r * gv], []
    return _rowwise(fn, [x], [g], [(x.shape[1], BF16)], [], name=name, deps=deps)[0]


def _rms_bwd(dn, x, dres, g, name, deps=()):
    def fn(rv, pv):
        (dnv, xv, drv), (gv,) = rv, pv
        r = lax.rsqrt(_mean(xv * xv) + RMS_EPS)
        yn = xv * r
        dyg = dnv * gv
        dx = drv + r * (dyg - yn * _mean(dyg * yn))
        return [dx, dx], [_colsum(dnv * yn)]
    D = x.shape[1]
    return _rowwise(fn, [dn, x, dres], [g], [(D, F32), (D, BF16)], [(1, D)], name=name, deps=deps)


def _rwkv_layout(RW, Lw, La, Lg, D):
    widths = [RW, RW, RW, Lw, La, Lg]
    pw = [_ceil_to(w, LANE) for w in widths]
    pw[5] += _ceil_to(sum(pw), 2 * D) - sum(pw)
    offs = [sum(pw[:i]) for i in range(6)]
    return widths, pw, offs, sum(pw)


def _pad_rwkv_cols(a, lay):
    widths, pw, _, _ = lay
    pieces, src = [], 0
    for w, p in zip(widths, pw):
        pieces.append(a[:, src:src + w])
        if p > w:
            pieces.append(jnp.zeros((a.shape[0], p - w), a.dtype))
        src += w
    return jnp.concatenate(pieces, axis=1)


def _unpad_rwkv_cols(a, lay):
    widths, _, offs, _ = lay
    return jnp.concatenate([a[:, o:o + w] for o, w in zip(offs, widths)], axis=1)


def _proj_pieces(lay, D, cs):
    widths, _, offs, rcp = lay
    rc = sum(widths)
    segs = [(sum(widths[:j]), widths[j], offs[j]) for j in range(6)] + [(rc, D, rcp + 2 * D), (rc + D, D, rcp), (rc + 2 * D, D, rcp + D)]
    pieces = []
    for start, width, dst in segs:
        n = start
        while n < start + width:
            d, off = divmod(n, cs)
            take = min(cs - off, start + width - n)
            pieces.append((d, off, dst + n - start, take))
            n += take
    return pieces


def _w_in_to_proj(g, lay, D, name):
    nb, rows, cs = g.shape
    icp = lay[3] + 3 * D
    pieces = _proj_pieces(lay, D, cs)
    tm = _pick(rows, (256, 128, 64, 32, 16))

    def body(i_ref, o_ref):
        o_ref[...] = jnp.zeros_like(o_ref)
        for d, src, dst, w in pieces:
            o_ref[:, dst:dst + w] = i_ref[d, :, src:src + w]

    return pl.pallas_call(
        body, name=name, grid=(rows // tm,), in_specs=[pl.BlockSpec((nb, tm, cs), lambda i: (0, i, 0))],
        out_specs=pl.BlockSpec((tm, icp), lambda i: (i, 0)), out_shape=jax.ShapeDtypeStruct((rows, icp), g.dtype),
        compiler_params=_params())(g)


def _dw_in_from_proj(a, lay, D, cs, name):
    rows, icp = a.shape
    pieces = _proj_pieces(lay, D, cs)
    tm = _pick(rows, (256, 128, 64, 32, 16))

    def body(i_ref, o_ref):
        for d, src, dst, w in pieces:
            o_ref[d, :, src:src + w] = i_ref[:, dst:dst + w]

    return pl.pallas_call(
        body, name=name, grid=(rows // tm,), in_specs=[pl.BlockSpec((tm, icp), lambda i: (i, 0))],
        out_specs=pl.BlockSpec((N_DEV, tm, cs), lambda i: (0, i, 0)), out_shape=jax.ShapeDtypeStruct((N_DEV, rows, cs), a.dtype),
        compiler_params=_params())(a)


def _pad_rows(a, rows):
    return a if a.shape[0] == rows else jnp.concatenate([a, jnp.zeros((rows - a.shape[0], a.shape[1]), a.dtype)], axis=0)


def _token_shift(p, halo, mu, i):
    tm = p.shape[0]
    hid = lax.broadcasted_iota(jnp.int32, (SUBLANE, 1), 0)
    before = jnp.sum(jnp.where(hid == SUBLANE - 1, halo, 0.0), axis=0, keepdims=True)
    before = jnp.where(i == 0, 0.0, before)
    rid = lax.broadcasted_iota(jnp.int32, (tm, 1), 0)
    prev = jnp.where(rid == 0, before, pltpu.roll(p, 1, 0))
    d = prev - p
    return p + d * mu, d


def _rwkv_math(ps, w0, a0, k_k, k_a, wlw, wla, wlg, lay):
    _, pw, offs, _ = lay
    r, k, v, xw, xa, xg = (ps[:, offs[j]:offs[j] + pw[j]] for j in range(6))
    tw = jnp.tanh(xw)
    ww = w0 + _bdot(tw, wlw)
    lw = -jnp.exp(-_softplus(-ww) - 0.5)
    a = _sigmoid(a0 + _bdot(xa, wla))
    sg = _sigmoid(xg)
    g = _bdot(sg, wlg)
    return dict(r=r, k=k, v=v, xa=xa, tw=tw, ww=ww, lw=lw, a=a, sg=sg, g=g, kkp=k * k_k, k2=k * (1.0 + (a - 1.0) * k_a))


def _halo_specs(T, tm, width, after):
    hb = tm // SUBLANE
    last = T // SUBLANE - 1
    if after:
        return pl.BlockSpec((SUBLANE, width), lambda i: (jnp.minimum((i + 1) * hb, last), 0))
    return pl.BlockSpec((SUBLANE, width), lambda i: (jnp.maximum(i * hb - 1, 0), 0))


def _rowsum(x):
    return jnp.sum(x, axis=-1, keepdims=True)


def _kk_math(kkp):
    nrm = jnp.sqrt(_rowsum(kkp * kkp))
    inv = 1.0 / jnp.maximum(nrm, 1e-12)
    return nrm, inv, kkp * inv


def _rwkv_pre(p, mu, small, lora, lay, name):
    T, rcp = p.shape[0], lay[3]
    H = lay[0][0] // HEAD
    tm = min(128, T)

    def body(p_ref, ph_ref, mu_ref, w0_ref, a0_ref, kk_ref, ka_ref, wlw_ref, wla_ref, wlg_ref, r_o, lw_o, k2_o, v_o, aa_o, bb_o, g_o):
        ps, _ = _token_shift(p_ref[...], ph_ref[...], mu_ref[...], pl.program_id(0))
        q = _rwkv_math(ps, w0_ref[...], a0_ref[...], kk_ref[...], ka_ref[...], wlw_ref[...], wla_ref[...], wlg_ref[...], lay)
        for h in range(H):
            sl = slice(h * HEAD, (h + 1) * HEAD)
            for o_ref, key in ((r_o, "r"), (lw_o, "lw"), (k2_o, "k2"), (v_o, "v"), (g_o, "g")):
                o_ref[h] = q[key][:, sl]
            _, _, kk = _kk_math(q["kkp"][:, sl])
            aa_o[h] = -kk
            bb_o[h] = kk * q["a"][:, sl]

    whole = lambda arr: pl.BlockSpec(arr.shape, lambda i: (0, 0))
    return pl.pallas_call(
        body, name=name, grid=(T // tm,),
        in_specs=([pl.BlockSpec((tm, rcp), lambda i: (i, 0)), _halo_specs(T, tm, rcp, False), whole(mu)]
                  + [whole(s) for s in small] + [whole(w) for w in lora]),
        out_specs=[pl.BlockSpec((H, tm, HEAD), lambda i: (0, i, 0))] * 7, out_shape=[jax.ShapeDtypeStruct((H, T, HEAD), F32)] * 7,
        compiler_params=_params())(p, p, mu, *small, *lora)


def _rwkv_pre_bwd(p, mu, small, lora, hgrads, lay, name):
    T, rcp = p.shape[0], lay[3]
    widths, pw, offs, _ = lay
    RW = widths[0]
    H = RW // HEAD
    tm = min(128, T)

    def body(p_ref, ph_ref, mu_ref, w0_ref, a0_ref, kk_ref, ka_ref, wlw_ref, wla_ref, wlg_ref,
             dr1, dr2, dk1, dk2b, dv1, dv2, dlw_h, daa, dbb, dg_h,
             dps_ref, dmu_ref, dw0_ref, da0_ref, dkk_ref, dka_ref, dwlw_ref, dwla_ref, dwlg_ref,
             s_dr, s_dk2, s_dv, s_dlw, s_dkkp, s_da, s_dg):
        i = pl.program_id(0)
        ps, dprev = _token_shift(p_ref[...], ph_ref[...], mu_ref[...], i)
        k_k, k_a = kk_ref[...], ka_ref[...]
        q = _rwkv_math(ps, w0_ref[...], a0_ref[...], k_k, k_a, wlw_ref[...], wla_ref[...], wlg_ref[...], lay)
        k, a, lw, ww, tw, sg = q["k"], q["a"], q["lw"], q["ww"], q["tw"], q["sg"]
        for h in range(H):
            sl = slice(h * HEAD, (h + 1) * HEAD)
            s_dr[:, sl] = dr1[h] + dr2[h]
            s_dk2[:, sl] = dk1[h] + dk2b[h]
            s_dv[:, sl] = dv1[h] + dv2[h]
            s_dlw[:, sl] = dlw_h[h]
            s_dg[:, sl] = dg_h[h]
            nrm, inv, kk = _kk_math(q["kkp"][:, sl])
            dbb_h = dbb[h]
            dkk = dbb_h * a[:, sl] - daa[h]
            s_dkkp[:, sl] = jnp.where(nrm > 1e-12, inv * (dkk - kk * _rowsum(dkk * kk)), dkk * inv)
            s_da[:, sl] = dbb_h * kk
        dk2, dkkp, dg = s_dk2[...], s_dkkp[...], s_dg[...]
        dk = dk2 * (1.0 + (a - 1.0) * k_a) + dkkp * k_k
        da = s_da[...] + dk2 * k * k_a
        dpa = da * a * (1.0 - a)
        dww = s_dlw[...] * lw * _sigmoid(-ww)
        dxa = _bdot(dpa, wla_ref[...], "nt")
        dxw = _bdot(dww, wlw_ref[...], "nt") * (1.0 - tw * tw)
        dxg = _bdot(dg, wlg_ref[...], "nt") * sg * (1.0 - sg)
        segs = (s_dr[...], dk, s_dv[...], dxw, dxa, dxg)
        sums = [dmu_ref, dw0_ref, da0_ref, dkk_ref, dka_ref, dwlw_ref, dwla_ref, dwlg_ref]

        @pl.when(i == 0)
        def _():
            for s in sums:
                s[...] = jnp.zeros_like(s)

        for j, seg in enumerate(segs):
            sl = slice(offs[j], offs[j] + pw[j])
            dps_ref[:, sl] = seg
            dmu_ref[:, sl] += _colsum(seg * dprev[:, sl])
        dw0_ref[...] += _colsum(dww)
        da0_ref[...] += _colsum(dpa)
        dkk_ref[...] += _colsum(dkkp * k)
        dka_ref[...] += _colsum(dk2 * k * (a - 1.0))
        dwlw_ref[...] += _bdot(tw, dww, "tn")
        dwla_ref[...] += _bdot(q["xa"], dpa, "tn")
        dwlg_ref[...] += _bdot(sg, dg, "tn")

    whole = lambda arr: pl.BlockSpec(arr.shape, lambda i: (0, 0))
    row = lambda w: pl.BlockSpec((tm, w), lambda i: (i, 0))
    acc_shapes = [(1, rcp), (1, RW), (1, RW), (1, RW), (1, RW)] + [w.shape for w in lora]
    return pl.pallas_call(
        body, name=name, grid=(T // tm,),
        in_specs=([row(rcp), _halo_specs(T, tm, rcp, False), whole(mu)] + [whole(s) for s in small] + [whole(w) for w in lora]
                  + [pl.BlockSpec((H, tm, HEAD), lambda i: (0, i, 0))] * 10),
        out_specs=[row(rcp)] + [pl.BlockSpec(s, lambda i: (0, 0)) for s in acc_shapes],
        out_shape=[jax.ShapeDtypeStruct((T, rcp), F32)] + [jax.ShapeDtypeStruct(s, F32) for s in acc_shapes],
        scratch_shapes=[pltpu.VMEM((tm, RW), F32)] * 7, compiler_params=_params())(p, p, mu, *small, *lora, *hgrads)


def _shift_bwd(dps, mu, dproj, name):
    T, rcp = dps.shape
    tm = min(256, T)
    nt = T // tm

    def body(d_ref, dh_ref, mu_ref, buf_ref, o_ref):
        i = pl.program_id(0)
        d = d_ref[...]
        hid = lax.broadcasted_iota(jnp.int32, (SUBLANE, 1), 0)
        after = jnp.sum(jnp.where(hid == 0, dh_ref[...], 0.0), axis=0, keepdims=True)
        after = jnp.where(i == nt - 1, 0.0, after)
        rid = lax.broadcasted_iota(jnp.int32, (tm, 1), 0)
        nxt = jnp.where(rid == tm - 1, after, pltpu.roll(d, tm - 1, 0))
        mu_v = mu_ref[...]
        o_ref[...] = (d * (1.0 - mu_v) + nxt * mu_v).astype(BF16)

    row = pl.BlockSpec((tm, rcp), lambda i: (i, 0))
    return pl.pallas_call(
        body, name=name, grid=(nt,),
        in_specs=[row, _halo_specs(T, tm, rcp, True), pl.BlockSpec(mu.shape, lambda i: (0, 0)), pl.BlockSpec(memory_space=pl.ANY)],
        out_specs=row, out_shape=jax.ShapeDtypeStruct(dproj.shape, BF16), input_output_aliases={3: 0},
        compiler_params=_params())(dps, dps, mu, dproj)


def _head_post_math(y, r, k2, v, lg, lb, rk):
    yc = y - _mean(y)
    rstd = lax.rsqrt(_mean(yc * yc) + LNX_EPS)
    yn = yc * rstd
    s = _rowsum(r * k2 * rk)
    return yn, rstd, yn * lg + lb + s * v, s


def _head_post(y, r, k2, v, g, hp, name, deps=()):
    H, T, _ = y.shape
    tm = min(128, T)

    def body(y_ref, r_ref, k_ref, v_ref, g_ref, lg_ref, lb_ref, rk_ref, *rest):
        o_ref = rest[-1]
        _, _, t, _ = _head_post_math(y_ref[...], r_ref[...], k_ref[...], v_ref[...], lg_ref[...], lb_ref[...], rk_ref[...])
        out = (t * g_ref[...]).astype(BF16)
        for h in range(H):
            o_ref[:, h * HEAD:(h + 1) * HEAD] = out[h]

    blk = pl.BlockSpec((H, tm, HEAD), lambda i: (0, i, 0))
    par = pl.BlockSpec((H, 1, HEAD), lambda i: (0, 0, 0))
    return pl.pallas_call(
        body, name=name, grid=(T // tm,),
        in_specs=[blk] * 5 + [par] * 3 + [pl.BlockSpec(d.shape, lambda i, nd=d.ndim: (0,) * nd) for d in deps],
        out_specs=pl.BlockSpec((tm, H * HEAD), lambda i: (i, 0)),
        out_shape=jax.ShapeDtypeStruct((T, H * HEAD), BF16), compiler_params=_params())(y, r, k2, v, g, *hp, *deps)


def _head_post_bwd(dya, y, r, k2, v, g, hp, name, deps=()):
    H, T, _ = y.shape
    tm = min(128, T)
    hsum = lambda t: jnp.sum(t, axis=1, keepdims=True)

    def body(d_ref, y_ref, r_ref, k_ref, v_ref, g_ref, lg_ref, lb_ref, rk_ref, *rest):
        outs, d_s = rest[len(deps):len(deps) + 8], rest[-1]
        for h in range(H):
            d_s[h] = d_ref[:, h * HEAD:(h + 1) * HEAD]
        d_v, r_v, k_v, v_v, lg, rk = d_s[...], r_ref[...], k_ref[...], v_ref[...], lg_ref[...], rk_ref[...]
        yn, rstd, t, s = _head_post_math(y_ref[...], r_v, k_v, v_v, lg, lb_ref[...], rk)
        dyo = d_v * g_ref[...]
        dyn = dyo * lg
        ds = _rowsum(dyo * v_v)
        vals = (rstd * (dyn - _mean(dyn) - yn * _mean(dyn * yn)), ds * k_v * rk, ds * r_v * rk, dyo * s, d_v * t)
        for o_ref, val in zip(outs[:5], vals):
            o_ref[...] = val
        sums = (hsum(dyo * yn), hsum(dyo), hsum(ds * r_v * k_v))
        first = pl.program_id(0) == 0

        @pl.when(first)
        def _():
            for o_ref, val in zip(outs[5:], sums):
                o_ref[...] = val

        @pl.when(jnp.logical_not(first))
        def _():
            for o_ref, val in zip(outs[5:], sums):
                o_ref[...] += val

    blk = pl.BlockSpec((H, tm, HEAD), lambda i: (0, i, 0))
    par = pl.BlockSpec((H, 1, HEAD), lambda i: (0, 0, 0))
    return pl.pallas_call(
        body, name=name, grid=(T // tm,),
        in_specs=([pl.BlockSpec((tm, H * HEAD), lambda i: (i, 0))] + [blk] * 5 + [par] * 3
                  + [pl.BlockSpec(d.shape, lambda i, nd=d.ndim: (0,) * nd) for d in deps]),
        out_specs=[blk] * 5 + [par] * 3,
        out_shape=[jax.ShapeDtypeStruct((H, T, HEAD), F32)] * 5 + [jax.ShapeDtypeStruct((H, 1, HEAD), F32)] * 3,
        scratch_shapes=[pltpu.VMEM((H, tm, HEAD), F32)], compiler_params=_params())(dya, y, r, k2, v, g, *hp, *deps)


def _bmm(x, y, mode):
    dn = {"nn": (((2,), (1,)), ((0,), (0,))), "nt": (((2,), (2,)), ((0,), (0,))), "tn": (((1,), (1,)), ((0,), (0,)))}[mode]
    (xh, xl), (yh, yl) = _split(x), _split(y)
    dot = lambda p, q: lax.dot_general(p, q, dn, preferred_element_type=F32)
    out = dot(xh, yh)
    if yl is not None:
        out = out + dot(xh, yl)
    if xl is not None:
        out = out + dot(xl, yh)
    return out


def _split(x):
    if isinstance(x, tuple):
        return x
    hi = x.astype(BF16)
    return hi, (x - hi.astype(F32)).astype(BF16)


def _exact(x):
    return x.astype(BF16), None


def _round(x):
    return x if isinstance(x, tuple) else (x.astype(BF16), None)


def _rows(*xs):
    if isinstance(xs[0], tuple):
        return tuple(None if any(p is None for p in parts) else jnp.concatenate(parts, axis=1) for parts in zip(*xs))
    return jnp.concatenate(xs, axis=1)


def _wkv_chunk(r, lw, k, v, a, b):
    hb, C, _ = r.shape
    ti = lax.broadcasted_iota(jnp.int32, (C, C), 0)
    si = lax.broadcasted_iota(jnp.int32, (C, C), 1)
    linc, lstr, eye = (ti >= si).astype(F32), (ti > si).astype(F32), (ti == si).astype(F32)
    qmask = jnp.concatenate([jnp.concatenate([lstr, lstr], axis=1), jnp.concatenate([linc, linc], axis=1)], axis=0)
    lincb = _exact(jnp.broadcast_to(linc, (hb, C, C)))
    both = _exact(jnp.broadcast_to(jnp.concatenate([linc, lstr], axis=0), (hb, 2 * C, C)))
    ones = _exact(jnp.ones_like(v))
    lws = _split(lw)
    ci = _bmm(lincb, lws, "nn")
    cC = jnp.sum(lw, axis=1, keepdims=True)
    gi, ge, gn, gr = jnp.exp(ci), jnp.exp(ci - lw), jnp.exp(-ci), jnp.exp(cC - ci)
    q = dict(At=a * ge, Rt=r * gi, Bt=b * gn, Kt=k * gn, Bh=b * gr, Kh=k * gr)
    s = dict(AR=_round(_rows(q["At"], q["Rt"])), BK=_round(_rows(q["Bt"], q["Kt"])), BKh=_round(_rows(q["Bh"], q["Kh"])), v=_round(v))
    quad = _bmm(s["AR"], s["BK"], "nt") * qmask
    s["top"], s["bot"] = _round(quad[:, :C]), _round(quad[:, C:])
    A_ab = quad[:, :C, :C]
    Tm = eye + A_ab
    Pw = _round(A_ab)
    n = 1
    while 2 * n < C:
        Pw = _round(_bmm(Pw, Pw, "nn"))
        Tm = Tm + _bmm(_round(Tm), Pw, "nn")
        n *= 2
    s["Tm"] = _round(Tm)
    gC = jnp.exp(_bmm(lws, ones, "tn"))
    q.update(gi=gi, ge=ge, gn=gn, gr=gr, qmask=qmask, both=both, gC=gC, ones=ones, s=s)
    return q


def _wkv_u(s, H0s, C):
    arh = _bmm(s["AR"], H0s, "nn")
    zv = _rows(tuple(None if p is None else jnp.zeros_like(p) for p in s["v"]), s["v"])
    U = _bmm(s["Tm"], _round(arh[:, :C] + _bmm(s["top"], zv, "nn")), "nn")
    return arh, _rows(_round(U), s["v"])


def _wkv_fwd(r, lw, k, v, a, b, name):
    H, T, N = r.shape
    C = min(WKV_CHUNK, T)
    nc = T // C
    hb = _pick(H, (16, 8, 4, 2))

    def body(r_ref, lw_ref, k_ref, v_ref, a_ref, b_ref, y_ref, st_ref, h_ref):
        @pl.when(pl.program_id(1) == 0)
        def _():
            h_ref[...] = jnp.zeros_like(h_ref)

        H0 = h_ref[...]
        st_ref[0] = H0
        q = _wkv_chunk(r_ref[...], lw_ref[...], k_ref[...], v_ref[...], a_ref[...], b_ref[...])
        s = q["s"]
        arh, UV = _wkv_u(s, _round(H0), C)
        y_ref[...] = arh[:, C:] + _bmm(s["bot"], UV, "nn")
        h_ref[...] = q["gC"] * H0 + _bmm(s["BKh"], UV, "tn")

    blk = pl.BlockSpec((hb, C, N), lambda h, c: (h, c, 0))
    return pl.pallas_call(
        body, name=name, grid=(H // hb, nc), in_specs=[blk] * 6,
        out_specs=[blk, pl.BlockSpec((1, hb, N, N), lambda h, c: (c, h, 0, 0))],
        out_shape=[jax.ShapeDtypeStruct((H, T, N), F32), jax.ShapeDtypeStruct((nc, H, N, N), F32)],
        scratch_shapes=[pltpu.VMEM((hb, N, N), F32)], compiler_params=_params())(r, lw, k, v, a, b)


def _wkv_bwd(r, lw, k, v, a, b, states, dy, name):
    H, T, N = r.shape
    C = min(WKV_CHUNK, T)
    nc = T // C
    hb = _pick(H, (16, 8, 4, 2))

    def body(r_ref, lw_ref, k_ref, v_ref, a_ref, b_ref, st_ref, dy_ref, dr_ref, dlw_ref, dk_ref, dv_ref, da_ref, db_ref, dh_ref):
        @pl.when(pl.program_id(1) == 0)
        def _():
            dh_ref[...] = jnp.zeros_like(dh_ref)

        dHC = dh_ref[...]
        H0 = st_ref[0]
        q = _wkv_chunk(r_ref[...], lw_ref[...], k_ref[...], v_ref[...], a_ref[...], b_ref[...])
        s, gC = q["s"], q["gC"]
        H0s, dHs, dY = _round(H0), _round(dHC), _round(dy_ref[...])
        _, UV = _wkv_u(s, H0s, C)
        bot_dy = _bmm(s["bot"], dY, "tn")
        bkh_dh = _bmm(s["BKh"], dHs, "nn")
        dP = _round(_bmm(s["Tm"], _round(bot_dy[:, :C] + bkh_dh[:, :C]), "tn"))
        dv_ref[...] = bot_dy[:, C:] + bkh_dh[:, C:] + _bmm(s["top"], dP, "tn")[:, C:]
        dPY = _rows(dP, dY)
        dh_ref[...] = gC * dHC + _bmm(s["AR"], dPY, "tn")
        dquad = _round(_bmm(dPY, UV, "nt") * q["qmask"])
        dAR = _bmm(dPY, H0s, "nt") + _bmm(dquad, s["BK"], "nn")
        dBK = _bmm(dquad, s["AR"], "tn")
        dBKh = _bmm(UV, dHs, "nt")
        dAt, dRt, dBt, dKt, dBh, dKh = dAR[:, :C], dAR[:, C:], dBK[:, :C], dBK[:, C:], dBKh[:, :C], dBKh[:, C:]
        dr_ref[...] = dRt * q["gi"]
        da_ref[...] = dAt * q["ge"]
        db_ref[...] = dBt * q["gn"] + dBh * q["gr"]
        dk_ref[...] = dKt * q["gn"] + dKh * q["gr"]
        tail = dBh * q["Bh"] + dKh * q["Kh"]
        dci = dRt * q["Rt"] - dBt * q["Bt"] - dKt * q["Kt"] - tail
        dcC = jnp.sum(tail, axis=1, keepdims=True) + _bmm(q["ones"], H0 * dHC * gC, "nt")
        dlw_ref[...] = _bmm(q["both"], _rows(dci, dAt * q["At"]), "tn") + dcC

    blk = pl.BlockSpec((hb, C, N), lambda h, c: (h, nc - 1 - c, 0))
    st = pl.BlockSpec((1, hb, N, N), lambda h, c: (nc - 1 - c, h, 0, 0))
    return pl.pallas_call(
        body, name=name, grid=(H // hb, nc), in_specs=[blk] * 6 + [st, blk], out_specs=[blk] * 6,
        out_shape=[jax.ShapeDtypeStruct((H, T, N), F32)] * 6,
        scratch_shapes=[pltpu.VMEM((hb, N, N), F32)], compiler_params=_params())(r, lw, k, v, a, b, states, dy)


def _sgu_ln(z, SW, lng, lnb):
    ge = _gelu(z)
    u, vv = ge[:, :SW], ge[:, SW:]
    xc = vv - _mean(vv)
    rstd = lax.rsqrt(_mean(xc * xc) + LN_EPS)
    vn = xc * rstd
    return u, vn, rstd, vn * lng + lnb


def _causal(ws_ref, g):
    ti = lax.broadcasted_iota(jnp.int32, (SGU_CHUNK, SGU_CHUNK), 0)
    si = lax.broadcasted_iota(jnp.int32, (SGU_CHUNK, SGU_CHUNK), 1)
    return ti >= si, jnp.where(ti >= si, ws_ref[g], 0.0).astype(BF16)


def _sgu_fwd(proj, zblock, lng, lnb, ws, bexp, name):
    T, SW = proj.shape[0], lng.shape[1]
    G = ws.shape[0]
    tr = min(256, T)
    nch = tr // SGU_CHUNK

    def body(z_ref, lng_ref, lnb_ref, ws_ref, be_ref, o_ref):
        u, _, _, vl = _sgu_ln(z_ref[...], SW, lng_ref[...], lnb_ref[...])
        for g in range(G):
            cs = slice(g * SGU_GROUP, (g + 1) * SGU_GROUP)
            _, wc = _causal(ws_ref, g)
            for n in range(nch):
                rs = slice(n * SGU_CHUNK, (n + 1) * SGU_CHUNK)
                m = jnp.dot(wc, vl[rs, cs].astype(BF16), preferred_element_type=F32) + be_ref[:, cs]
                o_ref[rs, cs] = (u[rs, cs] * m).astype(BF16)

    whole = lambda arr: pl.BlockSpec(arr.shape, lambda i, nd=arr.ndim: (0,) * nd)
    return pl.pallas_call(
        body, name=name, grid=(T // tr,),
        in_specs=[pl.BlockSpec((tr, 2 * SW), lambda i: (i, zblock)), whole(lng), whole(lnb), whole(ws), whole(bexp)],
        out_specs=pl.BlockSpec((tr, SW), lambda i: (i, 0)), out_shape=jax.ShapeDtypeStruct((T, SW), BF16),
        compiler_params=_params())(proj, lng, lnb, ws, bexp)


def _sgu_bwd(proj, zblock, dyb, lng, lnb, ws, bexp, dproj, name):
    T, SW = proj.shape[0], lng.shape[1]
    G = ws.shape[0]
    tr = min(256, T)
    nch = tr // SGU_CHUNK
    nt = T // tr

    def body(z_ref, dy_ref, lng_ref, lnb_ref, ws_ref, be_ref, buf_ref, dz_ref, dlg_ref, dlb_ref, dws_ref, db_ref, du_s, dvl_s, dbacc_s):
        i = pl.program_id(0)
        zv = z_ref[...]
        lng_v = lng_ref[...]
        u, vn, rstd, vl = _sgu_ln(zv, SW, lng_v, lnb_ref[...])

        @pl.when(i == 0)
        def _():
            for s in (dlg_ref, dlb_ref, dws_ref, dbacc_s):
                s[...] = jnp.zeros_like(s)

        for g in range(G):
            cs = slice(g * SGU_GROUP, (g + 1) * SGU_GROUP)
            tri, wc = _causal(ws_ref, g)
            for n in range(nch):
                rs = slice(n * SGU_CHUNK, (n + 1) * SGU_CHUNK)
                blk = vl[rs, cs].astype(BF16)
                m = jnp.dot(wc, blk, preferred_element_type=F32) + be_ref[:, cs]
                dyv = dy_ref[rs, cs]
                du_s[rs, cs] = dyv * m
                dm = dyv * u[rs, cs]
                dvl_s[rs, cs] = _bdot(wc, dm, "tn")
                dws_ref[g] += jnp.where(tri, _bdot(dm, blk, "nt"), 0.0)
                dbacc_s[:, cs] += dm

        dvl = dvl_s[...]
        dlg_ref[...] += _colsum(dvl * vn)
        dlb_ref[...] += _colsum(dvl)
        dvn = dvl * lng_v
        dvv = rstd * (dvn - _mean(dvn) - vn * _mean(dvn * vn))
        gp = _gelu_grad(zv)
        dz_ref[:, :SW] = (du_s[...] * gp[:, :SW]).astype(BF16)
        dz_ref[:, SW:] = (dvv * gp[:, SW:]).astype(BF16)

        @pl.when(i == nt - 1)
        def _():
            lane = lax.broadcasted_iota(jnp.int32, (SGU_CHUNK, LANE), 1)
            out = jnp.zeros((SGU_CHUNK, LANE), F32)
            for g in range(G):
                col = jnp.sum(dbacc_s[:, g * SGU_GROUP:(g + 1) * SGU_GROUP], axis=1, keepdims=True)
                out = jnp.where(lane == g, col, out)
            db_ref[...] = out

    whole = lambda arr: pl.BlockSpec(arr.shape, lambda i, nd=arr.ndim: (0,) * nd)
    acc_shapes = [(1, SW), (1, SW), ws.shape, (SGU_CHUNK, LANE)]
    return pl.pallas_call(
        body, name=name, grid=(nt,),
        in_specs=[pl.BlockSpec((tr, 2 * SW), lambda i: (i, zblock)), pl.BlockSpec((tr, SW), lambda i: (i, 0)),
                  whole(lng), whole(lnb), whole(ws), whole(bexp), pl.BlockSpec(memory_space=pl.ANY)],
        out_specs=([pl.BlockSpec((tr, 2 * SW), lambda i: (i, zblock))]
                   + [pl.BlockSpec(s, lambda i, nd=len(s): (0,) * nd) for s in acc_shapes]),
        out_shape=[jax.ShapeDtypeStruct(dproj.shape, BF16)] + [jax.ShapeDtypeStruct(s, F32) for s in acc_shapes],
        scratch_shapes=[pltpu.VMEM((tr, SW), F32), pltpu.VMEM((tr, SW), F32), pltpu.VMEM((SGU_CHUNK, SW), F32)],
        input_output_aliases={6: 0}, compiler_params=_params())(proj, dyb, lng, lnb, ws, bexp, dproj)


_HBM = pl.BlockSpec(memory_space=pltpu.HBM)
_SEM = pl.BlockSpec(memory_space=pltpu.SEMAPHORE)
_DATAFLOW = pltpu.SideEffectType.DATAFLOW_SIDE_EFFECTING


def _mesh_place(chips=False):
    x, y, c = lax.axis_index("x"), lax.axis_index("y"), lax.axis_index("c")
    return x, y, c, (2 * x + y if chips else 4 * x + 2 * y + c)


def _peer(x, y, c, rel, chips=False):
    px = 1 - x if rel & 4 else x
    py = 1 - y if rel & 2 else y
    pc = 1 - c if rel & 1 else c
    return (px, py, pc), (2 * px + py if chips else 4 * px + 2 * py + pc)


ALL_PEERS = tuple(range(1, N_DEV))
SIBLING = (1,)
SAME_CORE = (2, 4, 6)
SIBLINGS_CORE = (3, 5, 7)


def _exchange_start(groups, name, rels=ALL_PEERS, chips=False):
    flat = [t for g in groups for t in g]
    sizes = [len(g) for g in groups]
    n, ng = len(flat), len(groups)
    srcs = [pltpu.with_memory_space_constraint(a, pltpu.HBM) for a, _ in flat]
    lands = [pltpu.with_memory_space_constraint(lax.empty(((N_DEV,) + a.shape) if isg else a.shape, a.dtype), pltpu.HBM)
             for a, isg in flat]

    def body(*refs):
        ins, lnd, sems, token = refs[:n], refs[n:2 * n], refs[2 * n:2 * n + 3 * ng], refs[-1]
        x, y, c, me = _mesh_place(chips)
        j0 = 0
        for gi, sz in enumerate(sizes):
            for rel in rels:
                dev, slot = _peer(x, y, c, rel, chips)
                for jj in range(sz):
                    j = j0 + jj
                    pltpu.make_async_remote_copy(
                        src_ref=ins[j] if flat[j][1] else ins[j].at[slot], dst_ref=lnd[j].at[me],
                        send_sem=sems[3 * gi].at[jj * (N_DEV - 1) + rel - 1], recv_sem=sems[3 * gi + 1].at[jj * (N_DEV - 1) + rel - 1],
                        device_id=dev, device_id_type=pl.DeviceIdType.MESH).start()
            for jj in range(sz):
                j = j0 + jj
                pltpu.make_async_copy(ins[j] if flat[j][1] else ins[j].at[me], lnd[j].at[me], sems[3 * gi + 2].at[jj]).start()
            j0 += sz
        token[...] = jnp.zeros_like(token)

    sem_shapes = [pltpu.SemaphoreType.DMA((k,)) for sz in sizes for k in (sz * (N_DEV - 1), sz * (N_DEV - 1), sz)]
    res = pl.pallas_call(
        body, name=name,
        out_shape=(*sem_shapes, *[pltpu.HBM(a.shape, a.dtype) for a in srcs], *[pltpu.HBM(a.shape, a.dtype) for a in lands],
                   jax.ShapeDtypeStruct((SUBLANE, LANE), F32)),
        in_specs=[_HBM] * (2 * n), out_specs=(*[_SEM] * (3 * ng), *[_HBM] * (2 * n), pl.BlockSpec(memory_space=pltpu.VMEM)),
        input_output_aliases={i: 3 * ng + i for i in range(2 * n)},
        compiler_params=pltpu.CompilerParams(has_side_effects=_DATAFLOW))(*srcs, *lands)
    sems, thru, token = res[:3 * ng], res[3 * ng:3 * ng + 2 * n], res[-1]
    handle, j0 = [], 0
    for gi, sz in enumerate(sizes):
        handle.append(dict(kinds=[k for _, k in groups[gi]], chips=chips, srcs=list(thru[j0:j0 + sz]), lands=list(thru[n + j0:n + j0 + sz]),
                           sems=list(sems[3 * gi:3 * gi + 3])))
        j0 += sz
    return handle, token


def _exchange_wait(group, after, name, rels=ALL_PEERS, local=True):
    kinds, sz = group["kinds"], len(group["kinds"])
    relay = group.get("relay", [])

    def body(*refs):
        ins, lnd, (ssem, rsem, lsem) = refs[:sz], refs[sz:2 * sz], refs[2 * sz:2 * sz + 3]
        x, y, c, me = _mesh_place(group["chips"])
        for rel in rels:
            dev, slot = _peer(x, y, c, rel, group["chips"])
            for jj in range(sz):
                cp = pltpu.make_async_remote_copy(
                    src_ref=ins[jj] if kinds[jj] else ins[jj].at[slot], dst_ref=lnd[jj].at[slot],
                    send_sem=ssem.at[jj * (N_DEV - 1) + rel - 1], recv_sem=rsem.at[jj * (N_DEV - 1) + rel - 1],
                    device_id=dev, device_id_type=pl.DeviceIdType.MESH)
                cp.wait_send()
                cp.wait_recv()
        if local:
            for jj in range(sz):
                pltpu.make_async_copy(ins[jj] if kinds[jj] else ins[jj].at[me], lnd[jj].at[me], lsem.at[jj]).wait()
        if relay:
            fsend, frecv = refs[2 * sz + 3:2 * sz + 5]
            dev = _peer(x, y, c, 1)[0]
            for q, (mine, theirs) in enumerate(zip(SAME_CORE, SIBLINGS_CORE)):
                for jj in range(sz):
                    cp = pltpu.make_async_remote_copy(
                        src_ref=lnd[jj].at[_peer(x, y, c, mine)[1]], dst_ref=lnd[jj].at[_peer(x, y, c, theirs)[1]],
                        send_sem=fsend.at[jj * len(SAME_CORE) + q], recv_sem=frecv.at[jj * len(SAME_CORE) + q],
                        device_id=dev, device_id_type=pl.DeviceIdType.MESH)
                    cp.wait_send()
                    cp.wait_recv()

    arrays = group["srcs"] + group["lands"]
    sems = group["sems"] + relay
    res = pl.pallas_call(
        body, name=name, out_shape=[pltpu.HBM(a.shape, a.dtype) for a in arrays],
        in_specs=[_HBM] * (2 * sz) + [_SEM] * len(sems) + [pl.BlockSpec(memory_space=pl.ANY)], out_specs=[_HBM] * (2 * sz),
        input_output_aliases={i: i for i in range(2 * sz)},
        compiler_params=pltpu.CompilerParams(has_side_effects=_DATAFLOW))(*arrays, *sems, after)
    return dict(group, srcs=list(res[:sz]), lands=list(res[sz:]), relay=[])


def _relay_start(group, name):
    sz = len(group["kinds"])
    nq = len(SAME_CORE)

    def body(*refs):
        lnd, fsend, frecv, token = refs[:sz], refs[sz], refs[sz + 1], refs[-1]
        x, y, c, _ = _mesh_place()
        dev = _peer(x, y, c, 1)[0]
        for q, rel in enumerate(SAME_CORE):
            slot = _peer(x, y, c, rel)[1]
            for jj in range(sz):
                pltpu.make_async_remote_copy(
                    src_ref=lnd[jj].at[slot], dst_ref=lnd[jj].at[slot], send_sem=fsend.at[jj * nq + q], recv_sem=frecv.at[jj * nq + q],
                    device_id=dev, device_id_type=pl.DeviceIdType.MESH).start()
        token[...] = jnp.zeros_like(token)

    lands = group["lands"]
    res = pl.pallas_call(
        body, name=name,
        out_shape=(pltpu.SemaphoreType.DMA((sz * nq,)), pltpu.SemaphoreType.DMA((sz * nq,)), *[pltpu.HBM(a.shape, a.dtype) for a in lands],
                   jax.ShapeDtypeStruct((SUBLANE, LANE), F32)),
        in_specs=[_HBM] * sz, out_specs=(_SEM, _SEM, *[_HBM] * sz, pl.BlockSpec(memory_space=pltpu.VMEM)),
        input_output_aliases={i: 2 + i for i in range(sz)},
        compiler_params=pltpu.CompilerParams(has_side_effects=_DATAFLOW))(*lands)
    return dict(group, lands=list(res[2:2 + sz]), relay=[res[0], res[1]]), res[-1]


def _sibling_swap(arrays, handle, after, name):
    start = handle is None
    n = len(arrays) if start else len(handle["srcs"])
    chips = N_DEV // 2
    if start:
        srcs = [pltpu.with_memory_space_constraint(a.reshape(chips, 2, *a.shape[1:]), pltpu.HBM) for a in arrays]
        lands = [pltpu.with_memory_space_constraint(lax.empty((chips,) + a.shape[1:], a.dtype), pltpu.HBM) for a in arrays]
    else:
        srcs, lands = handle["srcs"], handle["lands"]

    def body(*refs):
        ins, lnd, ssem, rsem = refs[:n], refs[n:2 * n], refs[2 * n], refs[2 * n + 1]
        x, y, c, _ = _mesh_place()
        dev = _peer(x, y, c, 1)[0]
        for q in range(chips):
            for j in range(n):
                cp = pltpu.make_async_remote_copy(
                    src_ref=ins[j].at[q, 1 - c], dst_ref=lnd[j].at[q], send_sem=ssem.at[j * chips + q], recv_sem=rsem.at[j * chips + q],
                    device_id=dev, device_id_type=pl.DeviceIdType.MESH)
                if start:
                    cp.start()
                else:
                    cp.wait_send()
                    cp.wait_recv()
        if start:
            refs[-1][...] = jnp.zeros_like(refs[-1])

    thru = [pltpu.HBM(a.shape, a.dtype) for a in srcs + lands]
    effect = pltpu.CompilerParams(has_side_effects=_DATAFLOW)
    if start:
        res = pl.pallas_call(
            body, name=name, out_shape=(pltpu.SemaphoreType.DMA((n * chips,)), pltpu.SemaphoreType.DMA((n * chips,)), *thru,
                                        jax.ShapeDtypeStruct((SUBLANE, LANE), F32)),
            in_specs=[_HBM] * (2 * n), out_specs=(_SEM, _SEM, *[_HBM] * (2 * n), pl.BlockSpec(memory_space=pltpu.VMEM)),
            input_output_aliases={i: 2 + i for i in range(2 * n)}, compiler_params=effect)(*srcs, *lands)
        return dict(srcs=list(res[2:2 + n]), lands=list(res[2 + n:2 + 2 * n]), sems=[res[0], res[1]]), res[-1]
    res = pl.pallas_call(
        body, name=name, out_shape=thru, in_specs=[_HBM] * (2 * n) + [_SEM, _SEM, pl.BlockSpec(memory_space=pl.ANY)],
        out_specs=[_HBM] * (2 * n), input_output_aliases={i: i for i in range(2 * n)}, compiler_params=effect)(
            *srcs, *lands, *handle["sems"], after)
    return dict(handle, srcs=list(res[:n]), lands=list(res[n:]))


def _pair_add(mine, theirs, core, name):
    chips, _, rows, w = mine.shape
    tm = _pick(rows, (256, 128, 64, 32, 16))

    def body(core_ref, a_ref, b_ref, o_ref):
        o_ref[...] = (a_ref[...].astype(F32) + b_ref[...].astype(F32)).astype(o_ref.dtype)

    return pl.pallas_call(
        body, name=name, out_shape=jax.ShapeDtypeStruct(theirs.shape, theirs.dtype),
        grid_spec=pltpu.PrefetchScalarGridSpec(
            num_scalar_prefetch=1, grid=(chips, rows // tm),
            in_specs=[pl.BlockSpec((None, None, tm, w), lambda q, i, core_ref: (q, core_ref[0], i, 0)),
                      pl.BlockSpec((None, tm, w), lambda q, i, core_ref: (q, i, 0))],
            out_specs=pl.BlockSpec((None, tm, w), lambda q, i, core_ref: (q, i, 0))),
        compiler_params=_params())(core, mine, theirs)


def _adamw(w, m, v, gparts, name, after=None):
    R, C = w.shape
    tm = _pick(R, (256, 128, 64, 32, 16, 8))
    order = [] if after is None else [after]

    def body(w_ref, m_ref, v_ref, g_ref, *rest):
        go, do, mo, vo = rest[len(order):]
        g = g_ref[0].astype(F32)
        for j in range(1, gparts.shape[0]):
            g = g + g_ref[j].astype(F32)
        mn = ADAM_B1 * m_ref[...] + (1.0 - ADAM_B1) * g
        vn = ADAM_B2 * v_ref[...] + (1.0 - ADAM_B2) * (g * g)
        m_hat = mn / (1.0 - ADAM_B1 ** ADAM_STEP)
        v_hat = vn / (1.0 - ADAM_B2 ** ADAM_STEP)
        go[...] = g
        do[...] = -ADAM_LR * (m_hat / (jnp.sqrt(v_hat) + ADAM_EPS) + ADAM_WD * w_ref[...])
        mo[...] = mn
        vo[...] = vn

    row = pl.BlockSpec((tm, C), lambda i: (i, 0))
    return pl.pallas_call(
        body, name=name, grid=(R // tm,),
        in_specs=[row, row, row, pl.BlockSpec((gparts.shape[0], tm, C), lambda i: (0, i, 0))] + [pl.BlockSpec(memory_space=pl.ANY)] * len(order),
        out_specs=[row] * 4, out_shape=[jax.ShapeDtypeStruct((R, C), F32)] * 4, compiler_params=_params())(w, m, v, gparts, *order)


def _pack(arrays):
    parts = []
    for a in arrays:
        f = a.reshape(1, -1)
        pad = _ceil_to(f.shape[1], SUBLANE * LANE) - f.shape[1]
        f = jnp.concatenate([f, jnp.zeros((1, pad), f.dtype)], axis=1) if pad else f
        parts.append(f.reshape(-1, LANE))
    rows = sum(p.shape[0] for p in parts)
    pad = _ceil_to(rows, 64) - rows
    return jnp.concatenate(parts + ([jnp.zeros((pad, LANE), parts[0].dtype)] if pad else []), axis=0)


def _unpack(buf, shapes):
    out, row = [], 0
    for s in shapes:
        size = 1
        for d in s:
            size *= d
        rows = _ceil_to(size, SUBLANE * LANE) // LANE
        out.append(buf[row:row + rows].reshape(1, -1)[:, :size].reshape(s))
        row += rows
    return out


def kernel(x, norm_mix_g, w_in, shift_mu, w0, w_lora_up, a0, a_lora_up, g_lora_up, k_k, k_a, r_k, lnx_g, lnx_b, w_proj_rwkv, sgu_ln_g, sgu_ln_b, sgu_w, sgu_b, w_proj_sgu, w_out, norm_ffn_g, w_ffn_gate, w_ffn_up, w_ffn_down, norm_final_g, loss_target, m_norm_mix_g, m_w_in, m_shift_mu, m_w0, m_w_lora_up, m_a0, m_a_lora_up, m_g_lora_up, m_k_k, m_k_a, m_r_k, m_lnx_g, m_lnx_b, m_w_proj_rwkv, m_sgu_ln_g, m_sgu_ln_b, m_sgu_w, m_sgu_b, m_w_proj_sgu, m_w_out, m_norm_ffn_g, m_w_ffn_gate, m_w_ffn_up, m_w_ffn_down, m_norm_final_g, v_norm_mix_g, v_w_in, v_shift_mu, v_w0, v_w_lora_up, v_a0, v_a_lora_up, v_g_lora_up, v_k_k, v_k_a, v_r_k, v_lnx_g, v_lnx_b, v_w_proj_rwkv, v_sgu_ln_g, v_sgu_ln_b, v_sgu_w, v_sgu_b, v_w_proj_sgu, v_w_out, v_norm_ffn_g, v_w_ffn_gate, v_w_ffn_up, v_w_ffn_down, v_norm_final_g):
    weights = dict(norm_mix_g=norm_mix_g, w_in=w_in, shift_mu=shift_mu, w0=w0, w_lora_up=w_lora_up, a0=a0, a_lora_up=a_lora_up,
                   g_lora_up=g_lora_up, k_k=k_k, k_a=k_a, r_k=r_k, lnx_g=lnx_g, lnx_b=lnx_b, w_proj_rwkv=w_proj_rwkv,
                   sgu_ln_g=sgu_ln_g, sgu_ln_b=sgu_ln_b, sgu_w=sgu_w, sgu_b=sgu_b, w_proj_sgu=w_proj_sgu, w_out=w_out,
                   norm_ffn_g=norm_ffn_g, w_ffn_gate=w_ffn_gate, w_ffn_up=w_ffn_up, w_ffn_down=w_ffn_down, norm_final_g=norm_final_g)
    m_in = dict(norm_mix_g=m_norm_mix_g, w_in=m_w_in, shift_mu=m_shift_mu, w0=m_w0, w_lora_up=m_w_lora_up, a0=m_a0,
                a_lora_up=m_a_lora_up, g_lora_up=m_g_lora_up, k_k=m_k_k, k_a=m_k_a, r_k=m_r_k, lnx_g=m_lnx_g, lnx_b=m_lnx_b,
                w_proj_rwkv=m_w_proj_rwkv, sgu_ln_g=m_sgu_ln_g, sgu_ln_b=m_sgu_ln_b, sgu_w=m_sgu_w, sgu_b=m_sgu_b,
                w_proj_sgu=m_w_proj_sgu, w_out=m_w_out, norm_ffn_g=m_norm_ffn_g, w_ffn_gate=m_w_ffn_gate, w_ffn_up=m_w_ffn_up,
                w_ffn_down=m_w_ffn_down, norm_final_g=m_norm_final_g)
    v_in = dict(norm_mix_g=v_norm_mix_g, w_in=v_w_in, shift_mu=v_shift_mu, w0=v_w0, w_lora_up=v_w_lora_up, a0=v_a0,
                a_lora_up=v_a_lora_up, g_lora_up=v_g_lora_up, k_k=v_k_k, k_a=v_k_a, r_k=v_r_k, lnx_g=v_lnx_g, lnx_b=v_lnx_b,
                w_proj_rwkv=v_w_proj_rwkv, sgu_ln_g=v_sgu_ln_g, sgu_ln_b=v_sgu_ln_b, sgu_w=v_sgu_w, sgu_b=v_sgu_b,
                w_proj_sgu=v_w_proj_sgu, w_out=v_w_out, norm_ffn_g=v_norm_ffn_g, w_ffn_gate=v_w_ffn_gate, w_ffn_up=v_w_ffn_up,
                w_ffn_down=v_w_ffn_down, norm_final_g=v_norm_final_g)
    names = list(weights)
    col_sharded = ("w_in", "w_lora_up", "a_lora_up", "g_lora_up", "w_proj_rwkv", "w_proj_sgu", "w_ffn_gate", "w_ffn_up")
    row_sharded = ("w_out", "w_ffn_down")
    sharded = [n for n in names if n in col_sharded or n in row_sharded]
    small = [n for n in names if n not in sharded]

    xs, tgt = x[0], loss_target[0]
    T, D = xs.shape
    RW = w0.shape[1]
    H = RW // HEAD
    SW = sgu_ln_g.shape[1]
    G = sgu_w.shape[1]
    assert 2 * SW == D, "the projection layout takes the SGU part to be as wide as a gate"
    lay = _rwkv_layout(RW, w_lora_up.shape[1], a_lora_up.shape[1], g_lora_up.shape[1], D)
    _, pw, _, rcp = lay
    icp = rcp + 3 * D
    b_ga, b_gb, b_z = rcp // D, rcp // D + 1, rcp // D + 2

    gather_groups = [["w_in", "w_lora_up", "a_lora_up", "g_lora_up"], ["w_proj_rwkv", "w_proj_sgu", "w_out"],
                     ["w_ffn_gate"], ["w_ffn_up"], ["w_ffn_down"]]
    gather, gather_token = _exchange_start([[(weights[n][0].astype(BF16), True) for n in grp] for grp in gather_groups],
                                           "gather_start", rels=SIBLING + SAME_CORE)
    full = {}
    relay_tokens = {}
    joined = lambda g: g.transpose(1, 0, 2).reshape(g.shape[1], -1)

    def relay_weights(gi, after, name):
        arrived = _exchange_wait(gather[gi], after, "gather_wait_ici_" + name, rels=SAME_CORE, local=False)
        gather[gi], relay_tokens[gi] = _relay_start(arrived, "gather_relay_" + name)

    def take_weights(gi, after, name):
        done = _exchange_wait(gather[gi], after, "gather_wait_d2d_" + name, rels=SIBLING)
        for n, g in zip(gather_groups[gi], done["lands"]):
            full[n] = g.reshape(-1, g.shape[2]) if n in row_sharded else g

    packed = [_pack([d[n] for n in small] + [gather_token]) for d in (weights, m_in, v_in)]
    n1 = _rms_fwd(xs, norm_mix_g, "rms_mix", deps=[gather_token, *packed])
    relay_weights(0, n1, "in")
    take_weights(0, relay_tokens[0], "in")
    W_in = _w_in_to_proj(full["w_in"], lay, D, "w_in_layout")
    lora = [_pad_rows(joined(full[n]), rows) for n, rows in zip(("w_lora_up", "a_lora_up", "g_lora_up"), pw[3:])]
    mu_p = _pad_rwkv_cols(shift_mu, lay)
    rsmall = [w0, a0, k_k, k_a]
    hp = [lnx_g.reshape(H, 1, HEAD), lnx_b.reshape(H, 1, HEAD), r_k.reshape(H, 1, HEAD)]
    ws = sgu_w[0]
    bexp = jnp.repeat(sgu_b[0].T, SGU_GROUP, axis=1)
    gf = norm_final_g.reshape(1, D)

    proj = _matmul(n1, W_in, mode="nn", out_dtype=F32, name="proj_in")
    ga, gb = (proj, D, b_ga), (proj, D, b_gb)
    r_h, lw_h, k2_h, v_h, aa_h, bb_h, g_h = _rwkv_pre(proj, mu_p, rsmall, lora, lay, "rwkv_pre")
    wkv_in = [r_h, lw_h, k2_h, v_h, aa_h, bb_h]
    y_h, states = _wkv_fwd(*wkv_in, "wkv_fwd")
    relay_weights(1, y_h, "proj")
    relay_weights(2, relay_tokens[1], "ffn_gate")
    ya = _head_post(y_h, r_h, k2_h, v_h, g_h, hp, "head_post", deps=[relay_tokens[2]])
    relay_weights(3, ya, "ffn_up")
    yb = _sgu_fwd(proj, b_z, sgu_ln_g, sgu_ln_b, ws, bexp, "sgu_fwd")
    take_weights(1, ya, "proj")
    pa = _matmul(ya, full["w_proj_rwkv"], mode="nn", out_dtype=F32, name="proj_a", deps=[relay_tokens[3]])

    def merge_fn(pb_v, pa_v, ga_v, gb_v):
        return pb_v, _sigmoid(ga_v) * pa_v + _sigmoid(gb_v) * pb_v
    pb, merged = _matmul(yb, full["w_proj_sgu"], mode="nn", name="proj_b_merge",
                         epi=(merge_fn, [pa, (proj, b_ga * D), (proj, b_gb * D)], [F32, BF16]))
    h1 = _matmul(merged, full["w_out"], mode="nn", out_dtype=F32, name="out_proj", add=xs)
    n2 = _rms_fwd(h1, norm_ffn_g, "rms_ffn")
    relay_weights(4, n2, "ffn_down")
    take_weights(2, n2, "ffn_gate")
    take_weights(3, n2, "ffn_up")

    def act_fn(gt_v, up_v):
        return gt_v, up_v, gt_v * _sigmoid(gt_v) * up_v
    gt, up, act = _matmul(n2, full["w_ffn_gate"], b2=full["w_ffn_up"], mode="nn", name="ffn_gate_up_act", out_blocks=N_DEV,
                          epi=(act_fn, [], [BF16, BF16, BF16]), deps=[relay_tokens[4]])
    take_weights(4, act, "ffn_down")
    h2 = _matmul(act, full["w_ffn_down"], mode="nn", out_dtype=F32, name="ffn_down", add=h1)

    def final_fn(rv, pv):
        (h_v, t_v), (g_v,) = rv, pv
        r = lax.rsqrt(_mean(h_v * h_v) + RMS_EPS)
        yn = h_v * r
        e = yn * g_v - t_v
        loss = 0.5 * jnp.sum(_mean(e * e))
        dout = e * (1.0 / D)
        dyg = dout * g_v
        dh = r * (dyg - yn * _mean(dyg * yn))
        return [dh, dh], [jnp.full((1, LANE), loss, F32), _colsum(dout * yn)]
    dh2, dh2_bf, loss_part, d_gf = _rowwise(final_fn, [h2, tgt], [gf], [(D, F32), (D, BF16)], [(1, LANE), (1, D)], name="final_loss")

    grads = {}

    def start_scatter(group, name, extra=()):
        blocks = [(grads[n].reshape(N_DEV, -1, grads[n].shape[1]) if n in row_sharded else grads[n], False) for n in group]
        (handle,), token = _exchange_start([blocks + list(extra)], name)
        return handle, token

    def dact_fn(d_v, gt_v, up_v):
        gt_v, up_v = gt_v.astype(F32), up_v.astype(F32)
        s = _sigmoid(gt_v)
        return d_v * up_v * (s * (1.0 + gt_v * (1.0 - s))), d_v * gt_v * s
    dgt, dup = _matmul(dh2_bf, full["w_ffn_down"], mode="nt", name="d_ffn_act", out_blocks=N_DEV,
                       epi=(dact_fn, [gt, up], [BF16, BF16]))
    grads["w_ffn_down"] = _matmul(act, dh2_bf, mode="tn", out_dtype=BF16, name="dw_ffn_down")
    dn2 = _matmul(dgt, full["w_ffn_gate"], mode="nt", out_dtype=F32, name="dn2_gate")
    dn2 = _matmul(dup, full["w_ffn_up"], mode="nt", out_dtype=F32, name="dn2_up", add=dn2)
    grads["w_ffn_gate"] = _matmul(n2, dgt, mode="tn", out_dtype=BF16, name="dw_ffn_gate", out_blocks=N_DEV)
    grads["w_ffn_up"] = _matmul(n2, dup, mode="tn", out_dtype=BF16, name="dw_ffn_up", out_blocks=N_DEV)
    scatter_groups = [["w_ffn_down", "w_ffn_gate", "w_ffn_up"], ["w_out", "w_proj_rwkv", "w_proj_sgu"],
                      ["w_in", "w_lora_up", "a_lora_up", "g_lora_up"]]
    scatter_ffn, token_ffn = start_scatter(scatter_groups[0], "scatter_start_ffn")
    dh1, dh1_bf, d_g2 = _rms_bwd(dn2, h1, dh2, norm_ffn_g, "rms_ffn_bwd", deps=[token_ffn])
    dmerged = _matmul(dh1_bf, full["w_out"], mode="nt", out_dtype=F32, name="d_merged")
    grads["w_out"] = _matmul(merged, dh1_bf, mode="tn", out_dtype=BF16, name="dw_out")

    def dmerge_fn(rv, pv):
        d_v, ga_v, gb_v, pa_v, pb_v = rv
        sa, sb = _sigmoid(ga_v), _sigmoid(gb_v)
        dgates = jnp.concatenate([d_v * pa_v * sa * (1.0 - sa), d_v * pb_v * sb * (1.0 - sb)], axis=1)
        return [dgates, d_v * sa, d_v * sb], []
    dproj, dpa, dpb = _rowwise(dmerge_fn, [dmerged, ga, gb, pa, pb], [],
                               [(2 * D, BF16, icp, b_ga // 2, None), (D, BF16), (D, BF16)], [], name="d_merge")
    dya = _matmul(dpa, full["w_proj_rwkv"], mode="nt", out_dtype=F32, name="d_ya")
    dyb = _matmul(dpb, full["w_proj_sgu"], mode="nt", out_dtype=F32, name="d_yb")
    grads["w_proj_rwkv"] = _matmul(ya, dpa, mode="tn", out_dtype=BF16, name="dw_proj_a", out_blocks=N_DEV)
    grads["w_proj_sgu"] = _matmul(yb, dpb, mode="tn", out_dtype=BF16, name="dw_proj_b", out_blocks=N_DEV)
    scatter_mid, token_mid = start_scatter(scatter_groups[1], "scatter_start_mid")
    dproj, d_lng, d_lnb, d_ws, d_bs = _sgu_bwd(proj, b_z, dyb, sgu_ln_g, sgu_ln_b, ws, bexp, dproj, "sgu_bwd")

    dy_h, dr1, dk1, dv1, dg_h, d_lnxg, d_lnxb, d_rk = _head_post_bwd(dya, y_h, r_h, k2_h, v_h, g_h, hp, "head_post_bwd",
                                                                     deps=[token_mid])
    dr2, dlw_h, dk2b, dv2, daa, dbb = _wkv_bwd(*wkv_in, states, dy_h, "wkv_bwd")
    dps, d_mu, d_w0, d_a0, d_kk, d_ka, d_wlw, d_wla, d_wlg = _rwkv_pre_bwd(
        proj, mu_p, rsmall, lora, [dr1, dr2, dk1, dk2b, dv1, dv2, dlw_h, daa, dbb, dg_h], lay, "rwkv_pre_bwd")
    dproj = _shift_bwd(dps, mu_p, dproj, "shift_bwd")
    split = lambda g: g.reshape(g.shape[0], N_DEV, -1).transpose(1, 0, 2)
    grads["w_in"] = _dw_in_from_proj(_matmul(n1, dproj, mode="tn", out_dtype=BF16, name="dw_in"), lay, D, w_in.shape[2], "dw_in_layout")
    grads["w_lora_up"] = split(d_wlw[:w_lora_up.shape[1]].astype(BF16))
    grads["a_lora_up"] = split(d_wla[:a_lora_up.shape[1]].astype(BF16))
    grads["g_lora_up"] = split(d_wlg[:g_lora_up.shape[1]].astype(BF16))
    out = {}

    arrived = {}

    def update(group, after):
        for n in group:
            res = _adamw(weights[n][0], m_in[n][0], v_in[n][0], arrived[n], "adamw_" + n, after=after)
            out[n] = [t.reshape(weights[n].shape) for t in res]
            after = res[0]
        return after

    def update_group(gi, handle, after, name, first=None):
        parts = _exchange_wait(handle, after, "scatter_wait_" + name, rels=SAME_CORE if handle["chips"] else ALL_PEERS)["lands"]
        arrived.update(zip(scatter_groups[gi], parts))
        return update(scatter_groups[gi][:first], after)

    swap, token_swap = _sibling_swap([grads[n] for n in scatter_groups[2]], None, None, "scatter_in_swap_start")
    after = update_group(0, scatter_ffn, token_swap, "ffn", first=2)
    swap = _sibling_swap(None, swap, after, "scatter_in_swap_wait")
    core = lax.axis_index("c").astype(jnp.int32).reshape(1)
    chip_sums = [_pair_add(mine, theirs, core, "scatter_in_add_" + n)
                 for n, mine, theirs in zip(scatter_groups[2], swap["srcs"], swap["lands"])]
    (scatter_in,), token_in = _exchange_start([[(s, False) for s in chip_sums]], "scatter_start_in", rels=SAME_CORE, chips=True)
    dn1 = _matmul(dproj, W_in, mode="nt", out_dtype=F32, name="dn1", deps=[token_in])
    dx, _, d_g1 = _rms_bwd(dn1, xs, dh1, norm_mix_g, "rms_mix_bwd")
    small_grads = dict(norm_mix_g=d_g1, shift_mu=_unpad_rwkv_cols(d_mu, lay), w0=d_w0, a0=d_a0, k_k=d_kk, k_a=d_ka, r_k=d_rk,
                       lnx_g=d_lnxg, lnx_b=d_lnxb, sgu_ln_g=d_lng, sgu_ln_b=d_lnb, sgu_w=d_ws, sgu_b=d_bs[:, :G].T,
                       norm_ffn_g=d_g2, norm_final_g=d_gf)
    (gather_small,), after = _exchange_start([[(_pack([small_grads[n] for n in small] + [jnp.zeros_like(gather_token)]), True)]],
                                             "gather_small_start")
    after = update(scatter_groups[0][2:], after)
    after = update_group(1, scatter_mid, after, "mid")
    after = update_group(2, scatter_in, after, "in")
    small_parts =_exchange_wait(gather_small, after, "gather_small_wait")["lands"][0]
    res = _adamw(*packed, small_parts, "adamw_small")
    unpacked = [_unpack(t, [weights[n].shape for n in small]) for t in res]
    for i, n in enumerate(small):
        out[n] = [u[i] for u in unpacked]

    loss = lax.psum(loss_part[0, 0], ("x", "y", "c"))
    return (loss, dx[None], *[out[n][0] for n in names], *[out[n][1] for n in names],
            *[out[n][2] for n in names], *[out[n][3] for n in names])
```

```python
import jax
import jax.numpy as jnp
from jax import lax
from jax.experimental import pallas as pl
from jax.experimental.pallas import tpu as pltpu

F32 = jnp.float32
BF16 = jnp.bfloat16

N_DEV = 8
LANE = 128
SUBLANE = 8
HEAD = 64
SGU_CHUNK = 128
SGU_GROUP = 128
WKV_CHUNK = 64
RMS_EPS = 1e-6
LN_EPS = 1e-5
LNX_EPS = 64e-5
ADAM_LR, ADAM_B1, ADAM_B2, ADAM_EPS, ADAM_WD, ADAM_STEP = 0.001, 0.9, 0.999, 1e-08, 0.01, 10
VMEM_LIMIT_BYTES = 48 * 1024 * 1024
_SQRT_HALF = 0.7071067811865476
_INV_SQRT_2PI = 0.3989422804014327


def _pick(n, cands):
    for c in cands:
        if n % c == 0:
            return c
    return n


def _ceil_to(n, m):
    return -(-n // m) * m


def _params():
    return pltpu.CompilerParams(vmem_limit_bytes=VMEM_LIMIT_BYTES)


def _tile(n, cap):
    best = 0
    for d in range(LANE, min(n, cap) + 1, LANE):
        if n % d == 0:
            best = d
    return best or n


def _matmul_tiles(M, N, K, a_bytes, b_bytes, o_bytes, has_add, forced):
    tm = forced.get("m") or _tile(M, 1024)
    tn = forced.get("n") or _tile(N, 1024)
    tk = forced.get("k") or _tile(K, 2048)

    def vmem(tm, tn, tk):
        acc = tm * tn * 4 if tk < K else 0
        return 2 * (tm * tk * a_bytes + tk * tn * b_bytes + tm * tn * (o_bytes + (4 if has_add else 0))) + acc

    while vmem(tm, tn, tk) > (VMEM_LIMIT_BYTES * 3) // 4:
        if "k" not in forced and tk > 512 and _tile(K, tk // 2) < tk:
            tk = _tile(K, tk // 2)
        elif "m" not in forced and _tile(M, tm // 2) < tm:
            tm = _tile(M, tm // 2)
        else:
            break
    return tm, tn, tk


def _matmul(a, b, *, mode, out_dtype=F32, name, add=None, deps=(), out_blocks=0, epi=None, b2=None):
    def view(x):
        return (x.shape[1], x.shape[0] * x.shape[2], x.shape[2]) if x.ndim == 3 else (x.shape[0], x.shape[1], 0)

    (ar, ac, aw), (br, bc, bw) = view(a), view(b)
    a_col, b_col = {"nn": ("k", "n"), "nt": ("k", "k"), "tn": ("m", "n")}[mode]
    if mode == "nn":
        M, K, K2, N = ar, ac, br, bc
    elif mode == "nt":
        M, K, N, K2 = ar, ac, br, bc
    else:
        K, M, K2, N = ar, ac, br, bc
    assert K == K2, (a.shape, b.shape, mode)
    forced = {}
    for dim, w in ((a_col, aw), (b_col, bw), ("n", N // out_blocks if out_blocks else 0)):
        if w:
            assert forced.get(dim, w) == w
            forced[dim] = w
    has_add = add is not None
    tile_bytes = (sum(jnp.dtype(d).itemsize for d in epi[2]) + sum((e[0] if isinstance(e, tuple) else e).dtype.itemsize for e in epi[1])
                  if epi is not None else jnp.dtype(out_dtype).itemsize)
    tm, tn, tk = _matmul_tiles(M, N, K, a.dtype.itemsize, b.dtype.itemsize, tile_bytes, has_add, forced)
    kb = 1
    if "k" in forced and mode != "tn":
        lanes_ok = all(w or tk % LANE == 0 for w in (aw, bw if mode == "nt" else 1))
        kb = next(c for c in (4, 2, 1) if (K // tk) % c == 0 and (c == 1 or (lanes_ok and c * tk <= 1536)))
    nk = K // (tk * kb)
    dn = {"nn": (((1,), (0,)), ((), ())), "nt": (((1,), (1,)), ((), ())), "tn": (((0,), (0,)), ((), ()))}[mode]
    pick = {"m": lambda i, j, k: i, "n": lambda i, j, k: j, "k": lambda i, j, k: k}
    size = {"m": tm, "n": tn, "k": tk}

    def spec(blocked, row_dim, col_dim):
        rf, cf = pick[row_dim], pick[col_dim]
        reps = {d: (kb if d == "k" else 1) for d in (row_dim, col_dim)}
        if blocked:
            lead = kb if col_dim == "k" and kb > 1 else None
            return pl.BlockSpec((lead, size[row_dim], size[col_dim]), lambda i, j, k: (cf(i, j, k), rf(i, j, k), 0))
        return pl.BlockSpec((size[row_dim] * reps[row_dim], size[col_dim] * reps[col_dim]), lambda i, j, k: (rf(i, j, k), cf(i, j, k)))

    def k_part(ref, blocked, k_on_rows, j):
        if kb == 1:
            return ref[...]
        if blocked:
            return ref[j]
        return ref[j * tk:(j + 1) * tk, :] if k_on_rows else ref[:, j * tk:(j + 1) * tk]

    a_spec = spec(aw, "k" if mode == "tn" else "m", a_col)
    b_spec = spec(bw, "n" if mode == "nt" else "k", b_col)
    o_spec = spec(out_blocks, "m", "n")
    epi_fn, epi_ins, epi_dtypes = epi if epi is not None else (None, [], [out_dtype])
    epi_ins = [e if isinstance(e, tuple) else (e, None) for e in epi_ins]
    n_epi = len(epi_ins)
    twin = b2 is not None
    assert not twin or (nk == 1 and kb == 1 and epi is not None and b2.shape == b.shape)
    n_in = 2 + twin + has_add + n_epi + len(deps)
    n_out = len(epi_dtypes)

    def body(*refs):
        a_ref, b_ref = refs[0], refs[1]
        add_ref = refs[2 + twin] if has_add else None
        epi_refs = refs[2 + twin + has_add:2 + twin + has_add + n_epi]
        o_refs = refs[n_in:n_in + n_out]
        part = None
        for q in range(kb):
            a_q = k_part(a_ref, aw and a_col == "k", False, q)
            b_q = k_part(b_ref, bw and b_col == "k", mode == "nn", q)
            prod = lax.dot_general(a_q.astype(BF16), b_q.astype(BF16), dn, preferred_element_type=F32)
            part = prod if part is None else part + prod
        second = [lax.dot_general(a_ref[...].astype(BF16), refs[2][...].astype(BF16), dn, preferred_element_type=F32)] if twin else []

        def finish(res):
            outs = epi_fn(res, *second, *[e[...] for e in epi_refs]) if epi_fn is not None else (res,)
            for o_ref, val in zip(o_refs, outs):
                o_ref[...] = val.astype(o_ref.dtype)

        if nk == 1:
            finish(part + add_ref[...] if has_add else part)
            return
        acc_ref = refs[-1]
        kk = pl.program_id(2)

        @pl.when(kk == 0)
        def _():
            acc_ref[...] = part + add_ref[...] if has_add else part

        @pl.when(kk > 0)
        def _():
            acc_ref[...] += part

        @pl.when(kk == nk - 1)
        def _():
            finish(acc_ref[...])

    def epi_spec(arr, off):
        if off is None:
            return o_spec
        assert off % tn == 0
        return pl.BlockSpec((tm, tn), lambda i, j, k: (i, j + off // tn))

    ins = [a, b] + ([b2] if twin else []) + ([add] if has_add else []) + [arr for arr, _ in epi_ins] + list(deps)
    in_specs = ([a_spec, b_spec] + ([b_spec] if twin else []) + ([o_spec] if has_add else []) + [epi_spec(arr, off) for arr, off in epi_ins]
                + [pl.BlockSpec(d.shape, lambda i, j, k, nd=d.ndim: (0,) * nd) for d in deps])
    o_shape = (out_blocks, M, tn) if out_blocks else (M, N)
    res = pl.pallas_call(
        body, name=name, grid=(M // tm, N // tn, nk), in_specs=in_specs, out_specs=[o_spec] * n_out,
        out_shape=[jax.ShapeDtypeStruct(o_shape, dt) for dt in epi_dtypes],
        scratch_shapes=[pltpu.VMEM((tm, tn), F32)] if nk > 1 else [],
        compiler_params=_params())(*ins)
    return res[0] if epi is None else list(res)


def _rowwise(fn, rows, pars, row_outs, acc_outs, *, name, tm=256, deps=()):
    rows = [r if isinstance(r, tuple) else (r, r.shape[1], 0) for r in rows]
    row_outs = [o if len(o) == 5 else (o[0], o[1], o[0], 0, None) for o in row_outs]
    aliased = [(k, o[4]) for k, o in enumerate(row_outs) if o[4] is not None]
    R = rows[0][0].shape[0]
    if max(w for _, w, _ in rows) > 4096:
        tm = tm // 2
    tm = min(tm, R)
    assert R % tm == 0
    nr, npar = len(rows), len(pars)
    nro = len(row_outs)
    n_in = nr + npar + len(deps) + len(aliased)

    def body(*refs):
        rv = [r[...] for r in refs[:nr]]
        pv = [p[...] for p in refs[nr:nr + npar]]
        outs = refs[n_in:]
        ro, ao = fn(rv, pv)
        first = pl.program_id(0) == 0
        for o_ref, val in zip(outs[:nro], ro):
            o_ref[...] = val.astype(o_ref.dtype)

        @pl.when(first)
        def _():
            for o_ref, val in zip(outs[nro:], ao):
                o_ref[...] = val

        @pl.when(jnp.logical_not(first))
        def _():
            for o_ref, val in zip(outs[nro:], ao):
                o_ref[...] += val

    in_specs = ([pl.BlockSpec((tm, w), lambda i, cb=cb: (i, cb)) for _, w, cb in rows]
                + [pl.BlockSpec(p.shape, lambda i, nd=p.ndim: (0,) * nd) for p in list(pars) + list(deps)]
                + [pl.BlockSpec(memory_space=pl.ANY)] * len(aliased))
    out_shape = ([jax.ShapeDtypeStruct((R, full), dt) for _, dt, full, _, _ in row_outs]
                 + [jax.ShapeDtypeStruct(s, F32) for s in acc_outs])
    out_specs = ([pl.BlockSpec((tm, f), lambda i, cb=cb: (i, cb)) for f, _, _, cb, _ in row_outs]
                 + [pl.BlockSpec(s, lambda i, nd=len(s): (0,) * nd) for s in acc_outs])
    res = pl.pallas_call(body, name=name, grid=(R // tm,), in_specs=in_specs, out_specs=out_specs, out_shape=out_shape,
                         input_output_aliases={n_in - len(aliased) + q: k for q, (k, _) in enumerate(aliased)},
                         compiler_params=_params())(*[r for r, _, _ in rows], *pars, *deps, *[buf for _, buf in aliased])
    return list(res)


def _bdot(a, b, mode="nn"):
    dn = {"nn": (((1,), (0,)), ((), ())), "nt": (((1,), (1,)), ((), ())), "tn": (((0,), (0,)), ((), ()))}[mode]
    return lax.dot_general(a.astype(BF16), b.astype(BF16), dn, preferred_element_type=F32)


def _sigmoid(x):
    return jax.nn.sigmoid(x)


def _softplus(x):
    return jnp.maximum(x, 0.0) + jnp.log1p(jnp.exp(-jnp.abs(x)))


def _gelu(z):
    return 0.5 * z * (1.0 + lax.erf(z * _SQRT_HALF))


def _gelu_grad(z):
    return 0.5 * (1.0 + lax.erf(z * _SQRT_HALF)) + z * jnp.exp(-0.5 * z * z) * _INV_SQRT_2PI


def _mean(x):
    return jnp.mean(x, axis=-1, keepdims=True)


def _colsum(x):
    return jnp.sum(x, axis=0, keepdims=True)


def _rms_fwd(x, g, name, deps=()):
    def fn(rv, pv):
        (xv,), (gv,) = rv, pv
        r = lax.rsqrt(_mean(xv * xv) + RMS_EPS)
        return [xv * r * gv], []
    return _rowwise(fn, [x], [g], [(x.shape[1], BF16)], [], name=name, deps=deps)[0]


def _rms_bwd(dn, x, dres, g, name, deps=()):
    def fn(rv, pv):
        (dnv, xv, drv), (gv,) = rv, pv
        r = lax.rsqrt(_mean(xv * xv) + RMS_EPS)
        yn = xv * r
        dyg = dnv * gv
        dx = drv + r * (dyg - yn * _mean(dyg * yn))
        return [dx, dx], [_colsum(dnv * yn)]
    D = x.shape[1]
    return _rowwise(fn, [dn, x, dres], [g], [(D, F32), (D, BF16)], [(1, D)], name=name, deps=deps)


def _rwkv_layout(RW, Lw, La, Lg, D):
    widths = [RW, RW, RW, Lw, La, Lg]
    pw = [_ceil_to(w, LANE) for w in widths]
    pw[5] += _ceil_to(sum(pw), 2 * D) - sum(pw)
    offs = [sum(pw[:i]) for i in range(6)]
    return widths, pw, offs, sum(pw)


def _pad_rwkv_cols(a, lay):
    widths, pw, _, _ = lay
    pieces, src = [], 0
    for w, p in zip(widths, pw):
        pieces.append(a[:, src:src + w])
        if p > w:
            pieces.append(jnp.zeros((a.shape[0], p - w), a.dtype))
        src += w
    return jnp.concatenate(pieces, axis=1)


def _unpad_rwkv_cols(a, lay):
    widths, _, offs, _ = lay
    return jnp.concatenate([a[:, o:o + w] for o, w in zip(offs, widths)], axis=1)


def _proj_pieces(lay, D, cs):
    widths, _, offs, rcp = lay
    rc = sum(widths)
    segs = [(sum(widths[:j]), widths[j], offs[j]) for j in range(6)] + [(rc, D, rcp + 2 * D), (rc + D, D, rcp), (rc + 2 * D, D, rcp + D)]
    pieces = []
    for start, width, dst in segs:
        n = start
        while n < start + width:
            d, off = divmod(n, cs)
            take = min(cs - off, start + width - n)
            pieces.append((d, off, dst + n - start, take))
            n += take
    return pieces


def _w_in_to_proj(g, lay, D, name):
    nb, rows, cs = g.shape
    icp = lay[3] + 3 * D
    pieces = _proj_pieces(lay, D, cs)
    tm = _pick(rows, (256, 128, 64, 32, 16))

    def body(i_ref, o_ref):
        o_ref[...] = jnp.zeros_like(o_ref)
        for d, src, dst, w in pieces:
            o_ref[:, dst:dst + w] = i_ref[d, :, src:src + w]

    return pl.pallas_call(
        body, name=name, grid=(rows // tm,), in_specs=[pl.BlockSpec((nb, tm, cs), lambda i: (0, i, 0))],
        out_specs=pl.BlockSpec((tm, icp), lambda i: (i, 0)), out_shape=jax.ShapeDtypeStruct((rows, icp), g.dtype),
        compiler_params=_params())(g)


def _dw_in_from_proj(a, lay, D, cs, name):
    rows, icp = a.shape
    pieces = _proj_pieces(lay, D, cs)
    tm = _pick(rows, (256, 128, 64, 32, 16))

    def body(i_ref, o_ref):
        for d, src, dst, w in pieces:
            o_ref[d, :, src:src + w] = i_ref[:, dst:dst + w]

    return pl.pallas_call(
        body, name=name, grid=(rows // tm,), in_specs=[pl.BlockSpec((tm, icp), lambda i: (i, 0))],
        out_specs=pl.BlockSpec((N_DEV, tm, cs), lambda i: (0, i, 0)), out_shape=jax.ShapeDtypeStruct((N_DEV, rows, cs), a.dtype),
        compiler_params=_params())(a)


def _pad_rows(a, rows):
    return a if a.shape[0] == rows else jnp.concatenate([a, jnp.zeros((rows - a.shape[0], a.shape[1]), a.dtype)], axis=0)


def _token_shift(p, halo, mu, i):
    tm = p.shape[0]
    hid = lax.broadcasted_iota(jnp.int32, (SUBLANE, 1), 0)
    before = jnp.sum(jnp.where(hid == SUBLANE - 1, halo, 0.0), axis=0, keepdims=True)
    before = jnp.where(i == 0, 0.0, before)
    rid = lax.broadcasted_iota(jnp.int32, (tm, 1), 0)
    prev = jnp.where(rid == 0, before, pltpu.roll(p, 1, 0))
    d = prev - p
    return p + d * mu, d


def _rwkv_math(ps, w0, a0, k_k, k_a, wlw, wla, wlg, lay):
    _, pw, offs, _ = lay
    r, k, v, xw, xa, xg = (ps[:, offs[j]:offs[j] + pw[j]] for j in range(6))
    tw = jnp.tanh(xw)
    ww = w0 + _bdot(tw, wlw)
    lw = -jnp.exp(-_softplus(-ww) - 0.5)
    a = _sigmoid(a0 + _bdot(xa, wla))
    sg = _sigmoid(xg)
    g = _bdot(sg, wlg)
    return dict(r=r, k=k, v=v, xa=xa, tw=tw, ww=ww, lw=lw, a=a, sg=sg, g=g, kkp=k * k_k, k2=k * (1.0 + (a - 1.0) * k_a))


def _halo_specs(T, tm, width, after):
    hb = tm // SUBLANE
    last = T // SUBLANE - 1
    if after:
        return pl.BlockSpec((SUBLANE, width), lambda i: (jnp.minimum((i + 1) * hb, last), 0))
    return pl.BlockSpec((SUBLANE, width), lambda i: (jnp.maximum(i * hb - 1, 0), 0))


def _rowsum(x):
    return jnp.sum(x, axis=-1, keepdims=True)


def _kk_math(kkp):
    nrm = jnp.sqrt(_rowsum(kkp * kkp))
    inv = 1.0 / jnp.maximum(nrm, 1e-12)
    return nrm, inv, kkp * inv


def _rwkv_pre(p, mu, small, lora, lay, name):
    T, rcp = p.shape[0], lay[3]
    H = lay[0][0] // HEAD
    tm = min(128, T)

    def body(p_ref, ph_ref, mu_ref, w0_ref, a0_ref, kk_ref, ka_ref, wlw_ref, wla_ref, wlg_ref, r_o, lw_o, k2_o, v_o, aa_o, bb_o, g_o):
        ps, _ = _token_shift(p_ref[...], ph_ref[...], mu_ref[...], pl.program_id(0))
        q = _rwkv_math(ps, w0_ref[...], a0_ref[...], kk_ref[...], ka_ref[...], wlw_ref[...], wla_ref[...], wlg_ref[...], lay)
        for h in range(H):
            sl = slice(h * HEAD, (h + 1) * HEAD)
            for o_ref, key in ((r_o, "r"), (lw_o, "lw"), (k2_o, "k2"), (v_o, "v"), (g_o, "g")):
                o_ref[h] = q[key][:, sl]
            _, _, kk = _kk_math(q["kkp"][:, sl])
            aa_o[h] = -kk
            bb_o[h] = kk * q["a"][:, sl]

    whole = lambda arr: pl.BlockSpec(arr.shape, lambda i: (0, 0))
    return pl.pallas_call(
        body, name=name, grid=(T // tm,),
        in_specs=([pl.BlockSpec((tm, rcp), lambda i: (i, 0)), _halo_specs(T, tm, rcp, False), whole(mu)]
                  + [whole(s) for s in small] + [whole(w) for w in lora]),
        out_specs=[pl.BlockSpec((H, tm, HEAD), lambda i: (0, i, 0))] * 7, out_shape=[jax.ShapeDtypeStruct((H, T, HEAD), F32)] * 7,
        compiler_params=_params())(p, p, mu, *small, *lora)


def _rwkv_pre_bwd(p, mu, small, lora, hgrads, lay, name):
    T, rcp = p.shape[0], lay[3]
    widths, pw, offs, _ = lay
    RW = widths[0]
    H = RW // HEAD
    tm = min(128, T)

    def body(p_ref, ph_ref, mu_ref, w0_ref, a0_ref, kk_ref, ka_ref, wlw_ref, wla_ref, wlg_ref,
             dr1, dr2, dk1, dk2b, dv1, dv2, dlw_h, daa, dbb, dg_h,
             dps_ref, dmu_ref, dw0_ref, da0_ref, dkk_ref, dka_ref, dwlw_ref, dwla_ref, dwlg_ref,
             s_dr, s_dk2, s_dv, s_dlw, s_dkkp, s_da, s_dg):
        i = pl.program_id(0)
        ps, dprev = _token_shift(p_ref[...], ph_ref[...], mu_ref[...], i)
        k_k, k_a = kk_ref[...], ka_ref[...]
        q = _rwkv_math(ps, w0_ref[...], a0_ref[...], k_k, k_a, wlw_ref[...], wla_ref[...], wlg_ref[...], lay)
        k, a, lw, ww, tw, sg = q["k"], q["a"], q["lw"], q["ww"], q["tw"], q["sg"]
        for h in range(H):
            sl = slice(h * HEAD, (h + 1) * HEAD)
            s_dr[:, sl] = dr1[h] + dr2[h]
            s_dk2[:, sl] = dk1[h] + dk2b[h]
            s_dv[:, sl] = dv1[h] + dv2[h]
            s_dlw[:, sl] = dlw_h[h]
            s_dg[:, sl] = dg_h[h]
            nrm, inv, kk = _kk_math(q["kkp"][:, sl])
            dbb_h = dbb[h]
            dkk = dbb_h * a[:, sl] - daa[h]
            s_dkkp[:, sl] = jnp.where(nrm > 1e-12, inv * (dkk - kk * _rowsum(dkk * kk)), dkk * inv)
            s_da[:, sl] = dbb_h * kk
        dk2, dkkp, dg = s_dk2[...], s_dkkp[...], s_dg[...]
        dk = dk2 * (1.0 + (a - 1.0) * k_a) + dkkp * k_k
        da = s_da[...] + dk2 * k * k_a
        dpa = da * a * (1.0 - a)
        dww = s_dlw[...] * lw * _sigmoid(-ww)
        dxa = _bdot(dpa, wla_ref[...], "nt")
        dxw = _bdot(dww, wlw_ref[...], "nt") * (1.0 - tw * tw)
        dxg = _bdot(dg, wlg_ref[...], "nt") * sg * (1.0 - sg)
        segs = (s_dr[...], dk, s_dv[...], dxw, dxa, dxg)
        sums = [dmu_ref, dw0_ref, da0_ref, dkk_ref, dka_ref, dwlw_ref, dwla_ref, dwlg_ref]

        @pl.when(i == 0)
        def _():
            for s in sums:
                s[...] = jnp.zeros_like(s)

        for j, seg in enumerate(segs):
            sl = slice(offs[j], offs[j] + pw[j])
            dps_ref[:, sl] = seg
            dmu_ref[:, sl] += _colsum(seg * dprev[:, sl])
        dw0_ref[...] += _colsum(dww)
        da0_ref[...] += _colsum(dpa)
        dkk_ref[...] += _colsum(dkkp * k)
        dka_ref[...] += _colsum(dk2 * k * (a - 1.0))
        dwlw_ref[...] += _bdot(tw, dww, "tn")
        dwla_ref[...] += _bdot(q["xa"], dpa, "tn")
        dwlg_ref[...] += _bdot(sg, dg, "tn")

    whole = lambda arr: pl.BlockSpec(arr.shape, lambda i: (0, 0))
    row = lambda w: pl.BlockSpec((tm, w), lambda i: (i, 0))
    acc_shapes = [(1, rcp), (1, RW), (1, RW), (1, RW), (1, RW)] + [w.shape for w in lora]
    return pl.pallas_call(
        body, name=name, grid=(T // tm,),
        in_specs=([row(rcp), _halo_specs(T, tm, rcp, False), whole(mu)] + [whole(s) for s in small] + [whole(w) for w in lora]
                  + [pl.BlockSpec((H, tm, HEAD), lambda i: (0, i, 0))] * 10),
        out_specs=[row(rcp)] + [pl.BlockSpec(s, lambda i: (0, 0)) for s in acc_shapes],
        out_shape=[jax.ShapeDtypeStruct((T, rcp), F32)] + [jax.ShapeDtypeStruct(s, F32) for s in acc_shapes],
        scratch_shapes=[pltpu.VMEM((tm, RW), F32)] * 7, compiler_params=_params())(p, p, mu, *small, *lora, *hgrads)


def _shift_bwd(dps, mu, dproj, name):
    T, rcp = dps.shape
    tm = min(256, T)
    nt = T // tm

    def body(d_ref, dh_ref, mu_ref, buf_ref, o_ref):
        i = pl.program_id(0)
        d = d_ref[...]
        hid = lax.broadcasted_iota(jnp.int32, (SUBLANE, 1), 0)
        after = jnp.sum(jnp.where(hid == 0, dh_ref[...], 0.0), axis=0, keepdims=True)
        after = jnp.where(i == nt - 1, 0.0, after)
        rid = lax.broadcasted_iota(jnp.int32, (tm, 1), 0)
        nxt = jnp.where(rid == tm - 1, after, pltpu.roll(d, tm - 1, 0))
        mu_v = mu_ref[...]
        o_ref[...] = (d * (1.0 - mu_v) + nxt * mu_v).astype(BF16)

    row = pl.BlockSpec((tm, rcp), lambda i: (i, 0))
    return pl.pallas_call(
        body, name=name, grid=(nt,),
        in_specs=[row, _halo_specs(T, tm, rcp, True), pl.BlockSpec(mu.shape, lambda i: (0, 0)), pl.BlockSpec(memory_space=pl.ANY)],
        out_specs=row, out_shape=jax.ShapeDtypeStruct(dproj.shape, BF16), input_output_aliases={3: 0},
        compiler_params=_params())(dps, dps, mu, dproj)


def _head_post_math(y, r, k2, v, lg, lb, rk):
    yc = y - _mean(y)
    rstd = lax.rsqrt(_mean(yc * yc) + LNX_EPS)
    yn = yc * rstd
    s = _rowsum(r * k2 * rk)
    return yn, rstd, yn * lg + lb + s * v, s


def _head_post(y, r, k2, v, g, hp, name, deps=()):
    H, T, _ = y.shape
    tm = min(128, T)

    def body(y_ref, r_ref, k_ref, v_ref, g_ref, lg_ref, lb_ref, rk_ref, *rest):
        o_ref = rest[-1]
        _, _, t, _ = _head_post_math(y_ref[...], r_ref[...], k_ref[...], v_ref[...], lg_ref[...], lb_ref[...], rk_ref[...])
        out = (t * g_ref[...]).astype(BF16)
        for h in range(H):
            o_ref[:, h * HEAD:(h + 1) * HEAD] = out[h]

    blk = pl.BlockSpec((H, tm, HEAD), lambda i: (0, i, 0))
    par = pl.BlockSpec((H, 1, HEAD), lambda i: (0, 0, 0))
    return pl.pallas_call(
        body, name=name, grid=(T // tm,),
        in_specs=[blk] * 5 + [par] * 3 + [pl.BlockSpec(d.shape, lambda i, nd=d.ndim: (0,) * nd) for d in deps],
        out_specs=pl.BlockSpec((tm, H * HEAD), lambda i: (i, 0)),
        out_shape=jax.ShapeDtypeStruct((T, H * HEAD), BF16), compiler_params=_params())(y, r, k2, v, g, *hp, *deps)


def _head_post_bwd(dya, y, r, k2, v, g, hp, name, deps=()):
    H, T, _ = y.shape
    tm = min(128, T)
    hsum = lambda t: jnp.sum(t, axis=1, keepdims=True)

    def body(d_ref, y_ref, r_ref, k_ref, v_ref, g_ref, lg_ref, lb_ref, rk_ref, *rest):
        outs, d_s = rest[len(deps):len(deps) + 8], rest[-1]
        for h in range(H):
            d_s[h] = d_ref[:, h * HEAD:(h + 1) * HEAD]
        d_v, r_v, k_v, v_v, lg, rk = d_s[...], r_ref[...], k_ref[...], v_ref[...], lg_ref[...], rk_ref[...]
        yn, rstd, t, s = _head_post_math(y_ref[...], r_v, k_v, v_v, lg, lb_ref[...], rk)
        dyo = d_v * g_ref[...]
        dyn = dyo * lg
        ds = _rowsum(dyo * v_v)
        vals = (rstd * (dyn - _mean(dyn) - yn * _mean(dyn * yn)), ds * k_v * rk, ds * r_v * rk, dyo * s, d_v * t)
        for o_ref, val in zip(outs[:5], vals):
            o_ref[...] = val
        sums = (hsum(dyo * yn), hsum(dyo), hsum(ds * r_v * k_v))
        first = pl.program_id(0) == 0

        @pl.when(first)
        def _():
            for o_ref, val in zip(outs[5:], sums):
                o_ref[...] = val

        @pl.when(jnp.logical_not(first))
        def _():
            for o_ref, val in zip(outs[5:], sums):
                o_ref[...] += val

    blk = pl.BlockSpec((H, tm, HEAD), lambda i: (0, i, 0))
    par = pl.BlockSpec((H, 1, HEAD), lambda i: (0, 0, 0))
    return pl.pallas_call(
        body, name=name, grid=(T // tm,),
        in_specs=([pl.BlockSpec((tm, H * HEAD), lambda i: (i, 0))] + [blk] * 5 + [par] * 3
                  + [pl.BlockSpec(d.shape, lambda i, nd=d.ndim: (0,) * nd) for d in deps]),
        out_specs=[blk] * 5 + [par] * 3,
        out_shape=[jax.ShapeDtypeStruct((H, T, HEAD), F32)] * 5 + [jax.ShapeDtypeStruct((H, 1, HEAD), F32)] * 3,
        scratch_shapes=[pltpu.VMEM((H, tm, HEAD), F32)], compiler_params=_params())(dya, y, r, k2, v, g, *hp, *deps)


def _bmm(x, y, mode):
    dn = {"nn": (((2,), (1,)), ((0,), (0,))), "nt": (((2,), (2,)), ((0,), (0,))), "tn": (((1,), (1,)), ((0,), (0,)))}[mode]
    (xh, xl), (yh, yl) = _split(x), _split(y)
    dot = lambda p, q: lax.dot_general(p, q, dn, preferred_element_type=F32)
    out = dot(xh, yh)
    if yl is not None:
        out = out + dot(xh, yl)
    if xl is not None:
        out = out + dot(xl, yh)
    return out


def _split(x):
    if isinstance(x, tuple):
        return x
    hi = x.astype(BF16)
    return hi, (x - hi.astype(F32)).astype(BF16)


def _exact(x):
    return x.astype(BF16), None


def _round(x):
    return x if isinstance(x, tuple) else (x.astype(BF16), None)


def _rows(*xs):
    if isinstance(xs[0], tuple):
        return tuple(None if any(p is None for p in parts) else jnp.concatenate(parts, axis=1) for parts in zip(*xs))
    return jnp.concatenate(xs, axis=1)


def _wkv_chunk(r, lw, k, v, a, b):
    hb, C, _ = r.shape
    ti = lax.broadcasted_iota(jnp.int32, (C, C), 0)
    si = lax.broadcasted_iota(jnp.int32, (C, C), 1)
    linc, lstr, eye = (ti >= si).astype(F32), (ti > si).astype(F32), (ti == si).astype(F32)
    qmask = jnp.concatenate([jnp.concatenate([lstr, lstr], axis=1), jnp.concatenate([linc, linc], axis=1)], axis=0)
    lincb = _exact(jnp.broadcast_to(linc, (hb, C, C)))
    both = _exact(jnp.broadcast_to(jnp.concatenate([linc, lstr], axis=0), (hb, 2 * C, C)))
    ones = _exact(jnp.ones_like(v))
    lws = _split(lw)
    ci = _bmm(lincb, lws, "nn")
    cC = jnp.sum(lw, axis=1, keepdims=True)
    gi, ge, gn, gr = jnp.exp(ci), jnp.exp(ci - lw), jnp.exp(-ci), jnp.exp(cC - ci)
    q = dict(At=a * ge, Rt=r * gi, Bt=b * gn, Kt=k * gn, Bh=b * gr, Kh=k * gr)
    s = dict(AR=_round(_rows(q["At"], q["Rt"])), BK=_round(_rows(q["Bt"], q["Kt"])), BKh=_round(_rows(q["Bh"], q["Kh"])), v=_round(v))
    quad = _bmm(s["AR"], s["BK"], "nt") * qmask
    s["top"], s["bot"] = _round(quad[:, :C]), _round(quad[:, C:])
    A_ab = quad[:, :C, :C]
    Tm = eye + A_ab
    Pw = _round(A_ab)
    n = 1
    while 2 * n < C:
        Pw = _round(_bmm(Pw, Pw, "nn"))
        Tm = Tm + _bmm(_round(Tm), Pw, "nn")
        n *= 2
    s["Tm"] = _round(Tm)
    gC = jnp.exp(_bmm(lws, ones, "tn"))
    q.update(gi=gi, ge=ge, gn=gn, gr=gr, qmask=qmask, both=both, gC=gC, ones=ones, s=s)
    return q


def _wkv_u(s, H0s, C):
    arh = _bmm(s["AR"], H0s, "nn")
    zv = _rows(tuple(None if p is None else jnp.zeros_like(p) for p in s["v"]), s["v"])
    U = _bmm(s["Tm"], _round(arh[:, :C] + _bmm(s["top"], zv, "nn")), "nn")
    return arh, _rows(_round(U), s["v"])


def _wkv_fwd(r, lw, k, v, a, b, name):
    H, T, N = r.shape
    C = min(WKV_CHUNK, T)
    nc = T // C
    hb = _pick(H, (16, 8, 4, 2))

    def body(r_ref, lw_ref, k_ref, v_ref, a_ref, b_ref, y_ref, st_ref, h_ref):
        @pl.when(pl.program_id(1) == 0)
        def _():
            h_ref[...] = jnp.zeros_like(h_ref)

        H0 = h_ref[...]
        st_ref[0] = H0
        q = _wkv_chunk(r_ref[...], lw_ref[...], k_ref[...], v_ref[...], a_ref[...], b_ref[...])
        s = q["s"]
        arh, UV = _wkv_u(s, _round(H0), C)
        y_ref[...] = arh[:, C:] + _bmm(s["bot"], UV, "nn")
        h_ref[...] = q["gC"] * H0 + _bmm(s["BKh"], UV, "tn")

    blk = pl.BlockSpec((hb, C, N), lambda h, c: (h, c, 0))
    return pl.pallas_call(
        body, name=name, grid=(H // hb, nc), in_specs=[blk] * 6,
        out_specs=[blk, pl.BlockSpec((1, hb, N, N), lambda h, c: (c, h, 0, 0))],
        out_shape=[jax.ShapeDtypeStruct((H, T, N), F32), jax.ShapeDtypeStruct((nc, H, N, N), F32)],
        scratch_shapes=[pltpu.VMEM((hb, N, N), F32)], compiler_params=_params())(r, lw, k, v, a, b)


def _wkv_bwd(r, lw, k, v, a, b, states, dy, name):
    H, T, N = r.shape
    C = min(WKV_CHUNK, T)
    nc = T // C
    hb = _pick(H, (16, 8, 4, 2))

    def body(r_ref, lw_ref, k_ref, v_ref, a_ref, b_ref, st_ref, dy_ref, dr_ref, dlw_ref, dk_ref, dv_ref, da_ref, db_ref, dh_ref):
        @pl.when(pl.program_id(1) == 0)
        def _():
            dh_ref[...] = jnp.zeros_like(dh_ref)

        dHC = dh_ref[...]
        H0 = st_ref[0]
        q = _wkv_chunk(r_ref[...], lw_ref[...], k_ref[...], v_ref[...], a_ref[...], b_ref[...])
        s, gC = q["s"], q["gC"]
        H0s, dHs, dY = _round(H0), _round(dHC), _round(dy_ref[...])
        _, UV = _wkv_u(s, H0s, C)
        bot_dy = _bmm(s["bot"], dY, "tn")
        bkh_dh = _bmm(s["BKh"], dHs, "nn")
        dP = _round(_bmm(s["Tm"], _round(bot_dy[:, :C] + bkh_dh[:, :C]), "tn"))
        dv_ref[...] = bot_dy[:, C:] + bkh_dh[:, C:] + _bmm(s["top"], dP, "tn")[:, C:]
        dPY = _rows(dP, dY)
        dh_ref[...] = gC * dHC + _bmm(s["AR"], dPY, "tn")
        dquad = _round(_bmm(dPY, UV, "nt") * q["qmask"])
        dAR = _bmm(dPY, H0s, "nt") + _bmm(dquad, s["BK"], "nn")
        dBK = _bmm(dquad, s["AR"], "tn")
        dBKh = _bmm(UV, dHs, "nt")
        dAt, dRt, dBt, dKt, dBh, dKh = dAR[:, :C], dAR[:, C:], dBK[:, :C], dBK[:, C:], dBKh[:, :C], dBKh[:, C:]
        dr_ref[...] = dRt * q["gi"]
        da_ref[...] = dAt * q["ge"]
        db_ref[...] = dBt * q["gn"] + dBh * q["gr"]
        dk_ref[...] = dKt * q["gn"] + dKh * q["gr"]
        tail = dBh * q["Bh"] + dKh * q["Kh"]
        dci = dRt * q["Rt"] - dBt * q["Bt"] - dKt * q["Kt"] - tail
        dcC = jnp.sum(tail, axis=1, keepdims=True) + _bmm(q["ones"], H0 * dHC * gC, "nt")
        dlw_ref[...] = _bmm(q["both"], _rows(dci, dAt * q["At"]), "tn") + dcC

    blk = pl.BlockSpec((hb, C, N), lambda h, c: (h, nc - 1 - c, 0))
    st = pl.BlockSpec((1, hb, N, N), lambda h, c: (nc - 1 - c, h, 0, 0))
    return pl.pallas_call(
        body, name=name, grid=(H // hb, nc), in_specs=[blk] * 6 + [st, blk], out_specs=[blk] * 6,
        out_shape=[jax.ShapeDtypeStruct((H, T, N), F32)] * 6,
        scratch_shapes=[pltpu.VMEM((hb, N, N), F32)], compiler_params=_params())(r, lw, k, v, a, b, states, dy)


def _sgu_ln(z, SW, lng, lnb):
    ge = _gelu(z)
    u, vv = ge[:, :SW], ge[:, SW:]
    xc = vv - _mean(vv)
    rstd = lax.rsqrt(_mean(xc * xc) + LN_EPS)
    vn = xc * rstd
    return u, vn, rstd, vn * lng + lnb


def _causal(ws_ref, g):
    ti = lax.broadcasted_iota(jnp.int32, (SGU_CHUNK, SGU_CHUNK), 0)
    si = lax.broadcasted_iota(jnp.int32, (SGU_CHUNK, SGU_CHUNK), 1)
    return ti >= si, jnp.where(ti >= si, ws_ref[g], 0.0).astype(BF16)


def _sgu_fwd(proj, zblock, lng, lnb, ws, bexp, name):
    T, SW = proj.shape[0], lng.shape[1]
    G = ws.shape[0]
    tr = min(256, T)
    nch = tr // SGU_CHUNK

    def body(z_ref, lng_ref, lnb_ref, ws_ref, be_ref, o_ref):
        u, _, _, vl = _sgu_ln(z_ref[...], SW, lng_ref[...], lnb_ref[...])
        for g in range(G):
            cs = slice(g * SGU_GROUP, (g + 1) * SGU_GROUP)
            _, wc = _causal(ws_ref, g)
            for n in range(nch):
                rs = slice(n * SGU_CHUNK, (n + 1) * SGU_CHUNK)
                m = jnp.dot(wc, vl[rs, cs].astype(BF16), preferred_element_type=F32) + be_ref[:, cs]
                o_ref[rs, cs] = (u[rs, cs] * m).astype(BF16)

    whole = lambda arr: pl.BlockSpec(arr.shape, lambda i, nd=arr.ndim: (0,) * nd)
    return pl.pallas_call(
        body, name=name, grid=(T // tr,),
        in_specs=[pl.BlockSpec((tr, 2 * SW), lambda i: (i, zblock)), whole(lng), whole(lnb), whole(ws), whole(bexp)],
        out_specs=pl.BlockSpec((tr, SW), lambda i: (i, 0)), out_shape=jax.ShapeDtypeStruct((T, SW), BF16),
        compiler_params=_params())(proj, lng, lnb, ws, bexp)


def _sgu_bwd(proj, zblock, dyb, lng, lnb, ws, bexp, dproj, name):
    T, SW = proj.shape[0], lng.shape[1]
    G = ws.shape[0]
    tr = min(256, T)
    nch = tr // SGU_CHUNK
    nt = T // tr

    def body(z_ref, dy_ref, lng_ref, lnb_ref, ws_ref, be_ref, buf_ref, dz_ref, dlg_ref, dlb_ref, dws_ref, db_ref, du_s, dvl_s, dbacc_s):
        i = pl.program_id(0)
        zv = z_ref[...]
        lng_v = lng_ref[...]
        u, vn, rstd, vl = _sgu_ln(zv, SW, lng_v, lnb_ref[...])

        @pl.when(i == 0)
        def _():
            for s in (dlg_ref, dlb_ref, dws_ref, dbacc_s):
                s[...] = jnp.zeros_like(s)

        for g in range(G):
            cs = slice(g * SGU_GROUP, (g + 1) * SGU_GROUP)
            tri, wc = _causal(ws_ref, g)
            for n in range(nch):
                rs = slice(n * SGU_CHUNK, (n + 1) * SGU_CHUNK)
                blk = vl[rs, cs].astype(BF16)
                m = jnp.dot(wc, blk, preferred_element_type=F32) + be_ref[:, cs]
                dyv = dy_ref[rs, cs]
                du_s[rs, cs] = dyv * m
                dm = dyv * u[rs, cs]
                dvl_s[rs, cs] = _bdot(wc, dm, "tn")
                dws_ref[g] += jnp.where(tri, _bdot(dm, blk, "nt"), 0.0)
                dbacc_s[:, cs] += dm

        dvl = dvl_s[...]
        dlg_ref[...] += _colsum(dvl * vn)
        dlb_ref[...] += _colsum(dvl)
        dvn = dvl * lng_v
        dvv = rstd * (dvn - _mean(dvn) - vn * _mean(dvn * vn))
        gp = _gelu_grad(zv)
        dz_ref[:, :SW] = (du_s[...] * gp[:, :SW]).astype(BF16)
        dz_ref[:, SW:] = (dvv * gp[:, SW:]).astype(BF16)

        @pl.when(i == nt - 1)
        def _():
            lane = lax.broadcasted_iota(jnp.int32, (SGU_CHUNK, LANE), 1)
            out = jnp.zeros((SGU_CHUNK, LANE), F32)
            for g in range(G):
                col = jnp.sum(dbacc_s[:, g * SGU_GROUP:(g + 1) * SGU_GROUP], axis=1, keepdims=True)
                out = jnp.where(lane == g, col, out)
            db_ref[...] = out

    whole = lambda arr: pl.BlockSpec(arr.shape, lambda i, nd=arr.ndim: (0,) * nd)
    acc_shapes = [(1, SW), (1, SW), ws.shape, (SGU_CHUNK, LANE)]
    return pl.pallas_call(
        body, name=name, grid=(nt,),
        in_specs=[pl.BlockSpec((tr, 2 * SW), lambda i: (i, zblock)), pl.BlockSpec((tr, SW), lambda i: (i, 0)),
                  whole(lng), whole(lnb), whole(ws), whole(bexp), pl.BlockSpec(memory_space=pl.ANY)],
        out_specs=([pl.BlockSpec((tr, 2 * SW), lambda i: (i, zblock))]
                   + [pl.BlockSpec(s, lambda i, nd=len(s): (0,) * nd) for s in acc_shapes]),
        out_shape=[jax.ShapeDtypeStruct(dproj.shape, BF16)] + [jax.ShapeDtypeStruct(s, F32) for s in acc_shapes],
        scratch_shapes=[pltpu.VMEM((tr, SW), F32), pltpu.VMEM((tr, SW), F32), pltpu.VMEM((SGU_CHUNK, SW), F32)],
        input_output_aliases={6: 0}, compiler_params=_params())(proj, dyb, lng, lnb, ws, bexp, dproj)


_HBM = pl.BlockSpec(memory_space=pltpu.HBM)
_SEM = pl.BlockSpec(memory_space=pltpu.SEMAPHORE)
_DATAFLOW = pltpu.SideEffectType.DATAFLOW_SIDE_EFFECTING


def _mesh_place(chips=False):
    x, y, c = lax.axis_index("x"), lax.axis_index("y"), lax.axis_index("c")
    return x, y, c, (2 * x + y if chips else 4 * x + 2 * y + c)


def _peer(x, y, c, rel, chips=False):
    px = 1 - x if rel & 4 else x
    py = 1 - y if rel & 2 else y
    pc = 1 - c if rel & 1 else c
    return (px, py, pc), (2 * px + py if chips else 4 * px + 2 * py + pc)


ALL_PEERS = tuple(range(1, N_DEV))
SIBLING = (1,)
SAME_CORE = (2, 4, 6)
SIBLINGS_CORE = (3, 5, 7)


def _exchange_start(groups, name, rels=ALL_PEERS, chips=False):
    flat = [t for g in groups for t in g]
    sizes = [len(g) for g in groups]
    n, ng = len(flat), len(groups)
    srcs = [pltpu.with_memory_space_constraint(a, pltpu.HBM) for a, _ in flat]
    lands = [pltpu.with_memory_space_constraint(lax.empty(((N_DEV,) + a.shape) if isg else a.shape, a.dtype), pltpu.HBM)
             for a, isg in flat]

    def body(*refs):
        ins, lnd, sems, token = refs[:n], refs[n:2 * n], refs[2 * n:2 * n + 3 * ng], refs[-1]
        x, y, c, me = _mesh_place(chips)
        j0 = 0
        for gi, sz in enumerate(sizes):
            for rel in rels:
                dev, slot = _peer(x, y, c, rel, chips)
                for jj in range(sz):
                    j = j0 + jj
                    pltpu.make_async_remote_copy(
                        src_ref=ins[j] if flat[j][1] else ins[j].at[slot], dst_ref=lnd[j].at[me],
                        send_sem=sems[3 * gi].at[jj * (N_DEV - 1) + rel - 1], recv_sem=sems[3 * gi + 1].at[jj * (N_DEV - 1) + rel - 1],
                        device_id=dev, device_id_type=pl.DeviceIdType.MESH).start()
            for jj in range(sz):
                j = j0 + jj
                pltpu.make_async_copy(ins[j] if flat[j][1] else ins[j].at[me], lnd[j].at[me], sems[3 * gi + 2].at[jj]).start()
            j0 += sz
        token[...] = jnp.zeros_like(token)

    sem_shapes = [pltpu.SemaphoreType.DMA((k,)) for sz in sizes for k in (sz * (N_DEV - 1), sz * (N_DEV - 1), sz)]
    res = pl.pallas_call(
        body, name=name,
        out_shape=(*sem_shapes, *[pltpu.HBM(a.shape, a.dtype) for a in srcs], *[pltpu.HBM(a.shape, a.dtype) for a in lands],
                   jax.ShapeDtypeStruct((SUBLANE, LANE), F32)),
        in_specs=[_HBM] * (2 * n), out_specs=(*[_SEM] * (3 * ng), *[_HBM] * (2 * n), pl.BlockSpec(memory_space=pltpu.VMEM)),
        input_output_aliases={i: 3 * ng + i for i in range(2 * n)},
        compiler_params=pltpu.CompilerParams(has_side_effects=_DATAFLOW))(*srcs, *lands)
    sems, thru, token = res[:3 * ng], res[3 * ng:3 * ng + 2 * n], res[-1]
    handle, j0 = [], 0
    for gi, sz in enumerate(sizes):
        handle.append(dict(kinds=[k for _, k in groups[gi]], chips=chips, srcs=list(thru[j0:j0 + sz]), lands=list(thru[n + j0:n + j0 + sz]),
                           sems=list(sems[3 * gi:3 * gi + 3])))
        j0 += sz
    return handle, token


def _exchange_wait(group, after, name, rels=ALL_PEERS, local=True):
    kinds, sz = group["kinds"], len(group["kinds"])
    relay = group.get("relay", [])

    def body(*refs):
        ins, lnd, (ssem, rsem, lsem) = refs[:sz], refs[sz:2 * sz], refs[2 * sz:2 * sz + 3]
        x, y, c, me = _mesh_place(group["chips"])
        for rel in rels:
            dev, slot = _peer(x, y, c, rel, group["chips"])
            for jj in range(sz):
                cp = pltpu.make_async_remote_copy(
                    src_ref=ins[jj] if kinds[jj] else ins[jj].at[slot], dst_ref=lnd[jj].at[slot],
                    send_sem=ssem.at[jj * (N_DEV - 1) + rel - 1], recv_sem=rsem.at[jj * (N_DEV - 1) + rel - 1],
                    device_id=dev, device_id_type=pl.DeviceIdType.MESH)
                cp.wait_send()
                cp.wait_recv()
        if local:
            for jj in range(sz):
                pltpu.make_async_copy(ins[jj] if kinds[jj] else ins[jj].at[me], lnd[jj].at[me], lsem.at[jj]).wait()
        if relay:
            fsend, frecv = refs[2 * sz + 3:2 * sz + 5]
            dev = _peer(x, y, c, 1)[0]
            for q, (mine, theirs) in enumerate(zip(SAME_CORE, SIBLINGS_CORE)):
                for jj in range(sz):
                    cp = pltpu.make_async_remote_copy(
                        src_ref=lnd[jj].at[_peer(x, y, c, mine)[1]], dst_ref=lnd[jj].at[_peer(x, y, c, theirs)[1]],
                        send_sem=fsend.at[jj * len(SAME_CORE) + q], recv_sem=frecv.at[jj * len(SAME_CORE) + q],
                        device_id=dev, device_id_type=pl.DeviceIdType.MESH)
                    cp.wait_send()
                    cp.wait_recv()

    arrays = group["srcs"] + group["lands"]
    sems = group["sems"] + relay
    res = pl.pallas_call(
        body, name=name, out_shape=[pltpu.HBM(a.shape, a.dtype) for a in arrays],
        in_specs=[_HBM] * (2 * sz) + [_SEM] * len(sems) + [pl.BlockSpec(memory_space=pl.ANY)], out_specs=[_HBM] * (2 * sz),
        input_output_aliases={i: i for i in range(2 * sz)},
        compiler_params=pltpu.CompilerParams(has_side_effects=_DATAFLOW))(*arrays, *sems, after)
    return dict(group, srcs=list(res[:sz]), lands=list(res[sz:]), relay=[])


def _relay_start(group, name):
    sz = len(group["kinds"])
    nq = len(SAME_CORE)

    def body(*refs):
        lnd, fsend, frecv, token = refs[:sz], refs[sz], refs[sz + 1], refs[-1]
        x, y, c, _ = _mesh_place()
        dev = _peer(x, y, c, 1)[0]
        for q, rel in enumerate(SAME_CORE):
            slot = _peer(x, y, c, rel)[1]
            for jj in range(sz):
                pltpu.make_async_remote_copy(
                    src_ref=lnd[jj].at[slot], dst_ref=lnd[jj].at[slot], send_sem=fsend.at[jj * nq + q], recv_sem=frecv.at[jj * nq + q],
                    device_id=dev, device_id_type=pl.DeviceIdType.MESH).start()
        token[...] = jnp.zeros_like(token)

    lands = group["lands"]
    res = pl.pallas_call(
        body, name=name,
        out_shape=(pltpu.SemaphoreType.DMA((sz * nq,)), pltpu.SemaphoreType.DMA((sz * nq,)), *[pltpu.HBM(a.shape, a.dtype) for a in lands],
                   jax.ShapeDtypeStruct((SUBLANE, LANE), F32)),
        in_specs=[_HBM] * sz, out_specs=(_SEM, _SEM, *[_HBM] * sz, pl.BlockSpec(memory_space=pltpu.VMEM)),
        input_output_aliases={i: 2 + i for i in range(sz)},
        compiler_params=pltpu.CompilerParams(has_side_effects=_DATAFLOW))(*lands)
    return dict(group, lands=list(res[2:2 + sz]), relay=[res[0], res[1]]), res[-1]


def _sibling_swap(arrays, handle, after, name):
    start = handle is None
    n = len(arrays) if start else len(handle["srcs"])
    chips = N_DEV // 2
    if start:
        srcs = [pltpu.with_memory_space_constraint(a.reshape(chips, 2, *a.shape[1:]), pltpu.HBM) for a in arrays]
        lands = [pltpu.with_memory_space_constraint(lax.empty((chips,) + a.shape[1:], a.dtype), pltpu.HBM) for a in arrays]
    else:
        srcs, lands = handle["srcs"], handle["lands"]

    def body(*refs):
        ins, lnd, ssem, rsem = refs[:n], refs[n:2 * n], refs[2 * n], refs[2 * n + 1]
        x, y, c, _ = _mesh_place()
        dev = _peer(x, y, c, 1)[0]
        for q in range(chips):
            for j in range(n):
                cp = pltpu.make_async_remote_copy(
                    src_ref=ins[j].at[q, 1 - c], dst_ref=lnd[j].at[q], send_sem=ssem.at[j * chips + q], recv_sem=rsem.at[j * chips + q],
                    device_id=dev, device_id_type=pl.DeviceIdType.MESH)
                if start:
                    cp.start()
                else:
                    cp.wait_send()
                    cp.wait_recv()
        if start:
            refs[-1][...] = jnp.zeros_like(refs[-1])

    thru = [pltpu.HBM(a.shape, a.dtype) for a in srcs + lands]
    effect = pltpu.CompilerParams(has_side_effects=_DATAFLOW)
    if start:
        res = pl.pallas_call(
            body, name=name, out_shape=(pltpu.SemaphoreType.DMA((n * chips,)), pltpu.SemaphoreType.DMA((n * chips,)), *thru,
                                        jax.ShapeDtypeStruct((SUBLANE, LANE), F32)),
            in_specs=[_HBM] * (2 * n), out_specs=(_SEM, _SEM, *[_HBM] * (2 * n), pl.BlockSpec(memory_space=pltpu.VMEM)),
            input_output_aliases={i: 2 + i for i in range(2 * n)}, compiler_params=effect)(*srcs, *lands)
        return dict(srcs=list(res[2:2 + n]), lands=list(res[2 + n:2 + 2 * n]), sems=[res[0], res[1]]), res[-1]
    res = pl.pallas_call(
        body, name=name, out_shape=thru, in_specs=[_HBM] * (2 * n) + [_SEM, _SEM, pl.BlockSpec(memory_space=pl.ANY)],
        out_specs=[_HBM] * (2 * n), input_output_aliases={i: i for i in range(2 * n)}, compiler_params=effect)(
            *srcs, *lands, *handle["sems"], after)
    return dict(handle, srcs=list(res[:n]), lands=list(res[n:]))


def _pair_add(mine, theirs, core, name):
    chips, _, rows, w = mine.shape
    tm = _pick(rows, (256, 128, 64, 32, 16))

    def body(core_ref, a_ref, b_ref, o_ref):
        o_ref[...] = (a_ref[...].astype(F32) + b_ref[...].astype(F32)).astype(o_ref.dtype)

    return pl.pallas_call(
        body, name=name, out_shape=jax.ShapeDtypeStruct(theirs.shape, theirs.dtype),
        grid_spec=pltpu.PrefetchScalarGridSpec(
            num_scalar_prefetch=1, grid=(chips, rows // tm),
            in_specs=[pl.BlockSpec((None, None, tm, w), lambda q, i, core_ref: (q, core_ref[0], i, 0)),
                      pl.BlockSpec((None, tm, w), lambda q, i, core_ref: (q, i, 0))],
            out_specs=pl.BlockSpec((None, tm, w), lambda q, i, core_ref: (q, i, 0))),
        compiler_params=_params())(core, mine, theirs)


def _adamw(w, m, v, gparts, name, after=None):
    R, C = w.shape
    tm = _pick(R, (256, 128, 64, 32, 16, 8))
    order = [] if after is None else [after]

    def body(w_ref, m_ref, v_ref, g_ref, *rest):
        go, do, mo, vo = rest[len(order):]
        g = g_ref[0].astype(F32)
        for j in range(1, gparts.shape[0]):
            g = g + g_ref[j].astype(F32)
        mn = ADAM_B1 * m_ref[...] + (1.0 - ADAM_B1) * g
        vn = ADAM_B2 * v_ref[...] + (1.0 - ADAM_B2) * (g * g)
        m_hat = mn / (1.0 - ADAM_B1 ** ADAM_STEP)
        v_hat = vn / (1.0 - ADAM_B2 ** ADAM_STEP)
        go[...] = g
        do[...] = -ADAM_LR * (m_hat / (jnp.sqrt(v_hat) + ADAM_EPS) + ADAM_WD * w_ref[...])
        mo[...] = mn
        vo[...] = vn

    row = pl.BlockSpec((tm, C), lambda i: (i, 0))
    return pl.pallas_call(
        body, name=name, grid=(R // tm,),
        in_specs=[row, row, row, pl.BlockSpec((gparts.shape[0], tm, C), lambda i: (0, i, 0))] + [pl.BlockSpec(memory_space=pl.ANY)] * len(order),
        out_specs=[row] * 4, out_shape=[jax.ShapeDtypeStruct((R, C), F32)] * 4, compiler_params=_params())(w, m, v, gparts, *order)


def _pack(arrays):
    parts = []
    for a in arrays:
        f = a.reshape(1, -1)
        pad = _ceil_to(f.shape[1], SUBLANE * LANE) - f.shape[1]
        f = jnp.concatenate([f, jnp.zeros((1, pad), f.dtype)], axis=1) if pad else f
        parts.append(f.reshape(-1, LANE))
    rows = sum(p.shape[0] for p in parts)
    pad = _ceil_to(rows, 64) - rows
    return jnp.concatenate(parts + ([jnp.zeros((pad, LANE), parts[0].dtype)] if pad else []), axis=0)


def _unpack(buf, shapes):
    out, row = [], 0
    for s in shapes:
        size = 1
        for d in s:
            size *= d
        rows = _ceil_to(size, SUBLANE * LANE) // LANE
        out.append(buf[row:row + rows].reshape(1, -1)[:, :size].reshape(s))
        row += rows
    return out


def kernel(x, norm_mix_g, w_in, shift_mu, w0, w_lora_up, a0, a_lora_up, g_lora_up, k_k, k_a, r_k, lnx_g, lnx_b, w_proj_rwkv, sgu_ln_g, sgu_ln_b, sgu_w, sgu_b, w_proj_sgu, w_out, norm_ffn_g, w_ffn_gate, w_ffn_up, w_ffn_down, norm_final_g, loss_target, m_norm_mix_g, m_w_in, m_shift_mu, m_w0, m_w_lora_up, m_a0, m_a_lora_up, m_g_lora_up, m_k_k, m_k_a, m_r_k, m_lnx_g, m_lnx_b, m_w_proj_rwkv, m_sgu_ln_g, m_sgu_ln_b, m_sgu_w, m_sgu_b, m_w_proj_sgu, m_w_out, m_norm_ffn_g, m_w_ffn_gate, m_w_ffn_up, m_w_ffn_down, m_norm_final_g, v_norm_mix_g, v_w_in, v_shift_mu, v_w0, v_w_lora_up, v_a0, v_a_lora_up, v_g_lora_up, v_k_k, v_k_a, v_r_k, v_lnx_g, v_lnx_b, v_w_proj_rwkv, v_sgu_ln_g, v_sgu_ln_b, v_sgu_w, v_sgu_b, v_w_proj_sgu, v_w_out, v_norm_ffn_g, v_w_ffn_gate, v_w_ffn_up, v_w_ffn_down, v_norm_final_g):
    weights = dict(norm_mix_g=norm_mix_g, w_in=w_in, shift_mu=shift_mu, w0=w0, w_lora_up=w_lora_up, a0=a0, a_lora_up=a_lora_up,
                   g_lora_up=g_lora_up, k_k=k_k, k_a=k_a, r_k=r_k, lnx_g=lnx_g, lnx_b=lnx_b, w_proj_rwkv=w_proj_rwkv,
                   sgu_ln_g=sgu_ln_g, sgu_ln_b=sgu_ln_b, sgu_w=sgu_w, sgu_b=sgu_b, w_proj_sgu=w_proj_sgu, w_out=w_out,
                   norm_ffn_g=norm_ffn_g, w_ffn_gate=w_ffn_gate, w_ffn_up=w_ffn_up, w_ffn_down=w_ffn_down, norm_final_g=norm_final_g)
    m_in = dict(norm_mix_g=m_norm_mix_g, w_in=m_w_in, shift_mu=m_shift_mu, w0=m_w0, w_lora_up=m_w_lora_up, a0=m_a0,
                a_lora_up=m_a_lora_up, g_lora_up=m_g_lora_up, k_k=m_k_k, k_a=m_k_a, r_k=m_r_k, lnx_g=m_lnx_g, lnx_b=m_lnx_b,
                w_proj_rwkv=m_w_proj_rwkv, sgu_ln_g=m_sgu_ln_g, sgu_ln_b=m_sgu_ln_b, sgu_w=m_sgu_w, sgu_b=m_sgu_b,
                w_proj_sgu=m_w_proj_sgu, w_out=m_w_out, norm_ffn_g=m_norm_ffn_g, w_ffn_gate=m_w_ffn_gate, w_ffn_up=m_w_ffn_up,
                w_ffn_down=m_w_ffn_down, norm_final_g=m_norm_final_g)
    v_in = dict(norm_mix_g=v_norm_mix_g, w_in=v_w_in, shift_mu=v_shift_mu, w0=v_w0, w_lora_up=v_w_lora_up, a0=v_a0,
                a_lora_up=v_a_lora_up, g_lora_up=v_g_lora_up, k_k=v_k_k, k_a=v_k_a, r_k=v_r_k, lnx_g=v_lnx_g, lnx_b=v_lnx_b,
                w_proj_rwkv=v_w_proj_rwkv, sgu_ln_g=v_sgu_ln_g, sgu_ln_b=v_sgu_ln_b, sgu_w=v_sgu_w, sgu_b=v_sgu_b,
                w_proj_sgu=v_w_proj_sgu, w_out=v_w_out, norm_ffn_g=v_norm_ffn_g, w_ffn_gate=v_w_ffn_gate, w_ffn_up=v_w_ffn_up,
                w_ffn_down=v_w_ffn_down, norm_final_g=v_norm_final_g)
    names = list(weights)
    col_sharded = ("w_in", "w_lora_up", "a_lora_up", "g_lora_up", "w_proj_rwkv", "w_proj_sgu", "w_ffn_gate", "w_ffn_up")
    row_sharded = ("w_out", "w_ffn_down")
    sharded = [n for n in names if n in col_sharded or n in row_sharded]
    small = [n for n in names if n not in sharded]

    xs, tgt = x[0], loss_target[0]
    T, D = xs.shape
    RW = w0.shape[1]
    H = RW // HEAD
    SW = sgu_ln_g.shape[1]
    G = sgu_w.shape[1]
    assert 2 * SW == D, "the projection layout takes the SGU part to be as wide as a gate"
    lay = _rwkv_layout(RW, w_lora_up.shape[1], a_lora_up.shape[1], g_lora_up.shape[1], D)
    _, pw, _, rcp = lay
    icp = rcp + 3 * D
    b_ga, b_gb, b_z = rcp // D, rcp // D + 1, rcp // D + 2

    gather_groups = [["w_in", "w_lora_up", "a_lora_up", "g_lora_up"], ["w_proj_rwkv", "w_proj_sgu", "w_out"],
                     ["w_ffn_gate"], ["w_ffn_up"], ["w_ffn_down"]]
    gather, gather_token = _exchange_start([[(weights[n][0].astype(BF16), True) for n in grp] for grp in gather_groups],
                                           "gather_start", rels=SIBLING + SAME_CORE)
    full = {}
    relay_tokens = {}
    joined = lambda g: g.transpose(1, 0, 2).reshape(g.shape[1], -1)

    def relay_weights(gi, after, name):
        arrived = _exchange_wait(gather[gi], after, "gather_wait_ici_" + name, rels=SAME_CORE, local=False)
        gather[gi], relay_tokens[gi] = _relay_start(arrived, "gather_relay_" + name)

    def take_weights(gi, after, name):
        done = _exchange_wait(gather[gi], after, "gather_wait_d2d_" + name, rels=SIBLING)
        for n, g in zip(gather_groups[gi], done["lands"]):
            full[n] = g.reshape(-1, g.shape[2]) if n in row_sharded else g

    packed = [_pack([d[n] for n in small] + [gather_token]) for d in (weights, m_in, v_in)]
    n1 = _rms_fwd(xs, norm_mix_g, "rms_mix", deps=[gather_token, *packed])
    relay_weights(0, n1, "in")
    take_weights(0, relay_tokens[0], "in")
    W_in = _w_in_to_proj(full["w_in"], lay, D, "w_in_layout")
    lora = [_pad_rows(joined(full[n]), rows) for n, rows in zip(("w_lora_up", "a_lora_up", "g_lora_up"), pw[3:])]
    mu_p = _pad_rwkv_cols(shift_mu, lay)
    rsmall = [w0, a0, k_k, k_a]
    hp = [lnx_g.reshape(H, 1, HEAD), lnx_b.reshape(H, 1, HEAD), r_k.reshape(H, 1, HEAD)]
    ws = sgu_w[0]
    bexp = jnp.repeat(sgu_b[0].T, SGU_GROUP, axis=1)
    gf = norm_final_g.reshape(1, D)

    proj = _matmul(n1, W_in, mode="nn", out_dtype=F32, name="proj_in")
    ga, gb = (proj, D, b_ga), (proj, D, b_gb)
    r_h, lw_h, k2_h, v_h, aa_h, bb_h, g_h = _rwkv_pre(proj, mu_p, rsmall, lora, lay, "rwkv_pre")
    wkv_in = [r_h, lw_h, k2_h, v_h, aa_h, bb_h]
    y_h, states = _wkv_fwd(*wkv_in, "wkv_fwd")
    relay_weights(1, y_h, "proj")
    relay_weights(2, relay_tokens[1], "ffn_gate")
    ya = _head_post(y_h, r_h, k2_h, v_h, g_h, hp, "head_post", deps=[relay_tokens[2]])
    relay_weights(3, ya, "ffn_up")
    yb = _sgu_fwd(proj, b_z, sgu_ln_g, sgu_ln_b, ws, bexp, "sgu_fwd")
    take_weights(1, ya, "proj")
    pa = _matmul(ya, full["w_proj_rwkv"], mode="nn", out_dtype=F32, name="proj_a", deps=[relay_tokens[3]])

    def merge_fn(pb_v, pa_v, ga_v, gb_v):
        return pb_v, _sigmoid(ga_v) * pa_v + _sigmoid(gb_v) * pb_v
    pb, merged = _matmul(yb, full["w_proj_sgu"], mode="nn", name="proj_b_merge",
                         epi=(merge_fn, [pa, (proj, b_ga * D), (proj, b_gb * D)], [F32, BF16]))
    h1 = _matmul(merged, full["w_out"], mode="nn", out_dtype=F32, name="out_proj", add=xs)
    n2 = _rms_fwd(h1, norm_ffn_g, "rms_ffn")
    relay_weights(4, n2, "ffn_down")
    take_weights(2, n2, "ffn_gate")
    take_weights(3, n2, "ffn_up")

    def act_fn(gt_v, up_v):
        return gt_v, up_v, gt_v * _sigmoid(gt_v) * up_v
    gt, up, act = _matmul(n2, full["w_ffn_gate"], b2=full["w_ffn_up"], mode="nn", name="ffn_gate_up_act", out_blocks=N_DEV,
                          epi=(act_fn, [], [BF16, BF16, BF16]), deps=[relay_tokens[4]])
    take_weights(4, act, "ffn_down")
    h2 = _matmul(act, full["w_ffn_down"], mode="nn", out_dtype=F32, name="ffn_down", add=h1)

    def final_fn(rv, pv):
        (h_v, t_v), (g_v,) = rv, pv
        r = lax.rsqrt(_mean(h_v * h_v) + RMS_EPS)
        yn = h_v * r
        e = yn * g_v - t_v
        loss = 0.5 * jnp.sum(_mean(e * e))
        dout = e * (1.0 / D)
        dyg = dout * g_v
        dh = r * (dyg - yn * _mean(dyg * yn))
        return [dh, dh], [jnp.full((1, LANE), loss, F32), _colsum(dout * yn)]
    dh2, dh2_bf, loss_part, d_gf = _rowwise(final_fn, [h2, tgt], [gf], [(D, F32), (D, BF16)], [(1, LANE), (1, D)], name="final_loss")

    grads = {}

    def start_scatter(group, name, extra=()):
        blocks = [(grads[n].reshape(N_DEV, -1, grads[n].shape[1]) if n in row_sharded else grads[n], False) for n in group]
        (handle,), token = _exchange_start([blocks + list(extra)], name)
        return handle, token

    def dact_fn(d_v, gt_v, up_v):
        gt_v, up_v = gt_v.astype(F32), up_v.astype(F32)
        s = _sigmoid(gt_v)
        return d_v * up_v * (s * (1.0 + gt_v * (1.0 - s))), d_v * gt_v * s
    dgt, dup = _matmul(dh2_bf, full["w_ffn_down"], mode="nt", name="d_ffn_act", out_blocks=N_DEV,
                       epi=(dact_fn, [gt, up], [BF16, BF16]))
    scatter_groups = dict(ffn_down=["w_ffn_down"], ffn_gate=["w_ffn_gate"], ffn_up=["w_ffn_up"],
                          mid=["w_out", "w_proj_rwkv", "w_proj_sgu"], last=["w_in", "w_lora_up", "a_lora_up", "g_lora_up"])
    scatters = {}
    grads["w_ffn_down"] = _matmul(act, dh2_bf, mode="tn", out_dtype=BF16, name="dw_ffn_down")
    scatters["ffn_down"], token = start_scatter(scatter_groups["ffn_down"], "scatter_start_ffn_down")
    dn2 = _matmul(dgt, full["w_ffn_gate"], mode="nt", out_dtype=F32, name="dn2_gate", deps=[token])
    grads["w_ffn_gate"] = _matmul(n2, dgt, mode="tn", out_dtype=BF16, name="dw_ffn_gate", out_blocks=N_DEV)
    scatters["ffn_gate"], token = start_scatter(scatter_groups["ffn_gate"], "scatter_start_ffn_gate")
    grads["w_ffn_up"] = _matmul(n2, dup, mode="tn", out_dtype=BF16, name="dw_ffn_up", out_blocks=N_DEV, deps=[token])
    scatters["ffn_up"], token = start_scatter(scatter_groups["ffn_up"], "scatter_start_ffn_up")
    dn2 = _matmul(dup, full["w_ffn_up"], mode="nt", out_dtype=F32, name="dn2_up", add=dn2, deps=[token])
    dh1, dh1_bf, d_g2 = _rms_bwd(dn2, h1, dh2, norm_ffn_g, "rms_ffn_bwd")
    dmerged = _matmul(dh1_bf, full["w_out"], mode="nt", out_dtype=F32, name="d_merged")
    grads["w_out"] = _matmul(merged, dh1_bf, mode="tn", out_dtype=BF16, name="dw_out")

    def dmerge_fn(rv, pv):
        d_v, ga_v, gb_v, pa_v, pb_v = rv
        sa, sb = _sigmoid(ga_v), _sigmoid(gb_v)
        dgates = jnp.concatenate([d_v * pa_v * sa * (1.0 - sa), d_v * pb_v * sb * (1.0 - sb)], axis=1)
        return [dgates, d_v * sa, d_v * sb], []
    dproj, dpa, dpb = _rowwise(dmerge_fn, [dmerged, ga, gb, pa, pb], [],
                               [(2 * D, BF16, icp, b_ga // 2, None), (D, BF16), (D, BF16)], [], name="d_merge")
    dya = _matmul(dpa, full["w_proj_rwkv"], mode="nt", out_dtype=F32, name="d_ya")
    dyb = _matmul(dpb, full["w_proj_sgu"], mode="nt", out_dtype=F32, name="d_yb")
    grads["w_proj_rwkv"] = _matmul(ya, dpa, mode="tn", out_dtype=BF16, name="dw_proj_a", out_blocks=N_DEV)
    grads["w_proj_sgu"] = _matmul(yb, dpb, mode="tn", out_dtype=BF16, name="dw_proj_b", out_blocks=N_DEV)
    scatters["mid"], token_mid = start_scatter(scatter_groups["mid"], "scatter_start_mid")
    dproj, d_lng, d_lnb, d_ws, d_bs = _sgu_bwd(proj, b_z, dyb, sgu_ln_g, sgu_ln_b, ws, bexp, dproj, "sgu_bwd")

    dy_h, dr1, dk1, dv1, dg_h, d_lnxg, d_lnxb, d_rk = _head_post_bwd(dya, y_h, r_h, k2_h, v_h, g_h, hp, "head_post_bwd",
                                                                     deps=[token_mid])
    dr2, dlw_h, dk2b, dv2, daa, dbb = _wkv_bwd(*wkv_in, states, dy_h, "wkv_bwd")
    dps, d_mu, d_w0, d_a0, d_kk, d_ka, d_wlw, d_wla, d_wlg = _rwkv_pre_bwd(
        proj, mu_p, rsmall, lora, [dr1, dr2, dk1, dk2b, dv1, dv2, dlw_h, daa, dbb, dg_h], lay, "rwkv_pre_bwd")
    dproj = _shift_bwd(dps, mu_p, dproj, "shift_bwd")
    split = lambda g: g.reshape(g.shape[0], N_DEV, -1).transpose(1, 0, 2)
    grads["w_in"] = _dw_in_from_proj(_matmul(n1, dproj, mode="tn", out_dtype=BF16, name="dw_in"), lay, D, w_in.shape[2], "dw_in_layout")
    grads["w_lora_up"] = split(d_wlw[:w_lora_up.shape[1]].astype(BF16))
    grads["a_lora_up"] = split(d_wla[:a_lora_up.shape[1]].astype(BF16))
    grads["g_lora_up"] = split(d_wlg[:g_lora_up.shape[1]].astype(BF16))
    out = {}

    def update_group(key, after):
        handle = scatters[key]
        parts = _exchange_wait(handle, after, "scatter_wait_" + key, rels=SAME_CORE if handle["chips"] else ALL_PEERS)["lands"]
        for n, part in zip(scatter_groups[key], parts):
            res = _adamw(weights[n][0], m_in[n][0], v_in[n][0], part, "adamw_" + n, after=after)
            out[n] = [t.reshape(weights[n].shape) for t in res]
            after = res[0]
        return after

    swap, token_swap = _sibling_swap([grads[n] for n in scatter_groups["last"]], None, None, "scatter_last_swap_start")
    after = update_group("ffn_gate", update_group("ffn_down", token_swap))
    swap = _sibling_swap(None, swap, after, "scatter_last_swap_wait")
    core = lax.axis_index("c").astype(jnp.int32).reshape(1)
    chip_sums = [_pair_add(mine, theirs, core, "scatter_last_add_" + n)
                 for n, mine, theirs in zip(scatter_groups["last"], swap["srcs"], swap["lands"])]
    (scatters["last"],), token_in = _exchange_start([[(s, False) for s in chip_sums]], "scatter_start_last", rels=SAME_CORE, chips=True)
    dn1 = _matmul(dproj, W_in, mode="nt", out_dtype=F32, name="dn1", deps=[token_in])
    dx, _, d_g1 = _rms_bwd(dn1, xs, dh1, norm_mix_g, "rms_mix_bwd")
    small_grads = dict(norm_mix_g=d_g1, shift_mu=_unpad_rwkv_cols(d_mu, lay), w0=d_w0, a0=d_a0, k_k=d_kk, k_a=d_ka, r_k=d_rk,
                       lnx_g=d_lnxg, lnx_b=d_lnxb, sgu_ln_g=d_lng, sgu_ln_b=d_lnb, sgu_w=d_ws, sgu_b=d_bs[:, :G].T,
                       norm_ffn_g=d_g2, norm_final_g=d_gf)
    (gather_small,), after = _exchange_start([[(_pack([small_grads[n] for n in small] + [jnp.zeros_like(gather_token)]), True)]],
                                             "gather_small_start")
    for key in ("ffn_up", "mid", "last"):
        after = update_group(key, after)
    small_parts = _exchange_wait(gather_small, after, "gather_small_wait")["lands"][0]
    res = _adamw(*packed, small_parts, "adamw_small")
    unpacked = [_unpack(t, [weights[n].shape for n in small]) for t in res]
    for i, n in enumerate(small):
        out[n] = [u[i] for u in unpacked]

    loss = lax.psum(loss_part[0, 0], ("x", "y", "c"))
    return (loss, dx[None], *[out[n][0] for n in names], *[out[n][1] for n in names],
            *[out[n][2] for n in names], *[out[n][3] for n in names])
```

```python
import jax
import jax.numpy as jnp
from jax import lax
from jax.experimental import pallas as pl
from jax.experimental.pallas import tpu as pltpu

F32 = jnp.float32
BF16 = jnp.bfloat16

N_DEV = 8
LANE = 128
SUBLANE = 8
HEAD = 64
SGU_CHUNK = 128
SGU_GROUP = 128
WKV_CHUNK = 64
RMS_EPS = 1e-6
LN_EPS = 1e-5
LNX_EPS = 64e-5
ADAM_LR, ADAM_B1, ADAM_B2, ADAM_EPS, ADAM_WD, ADAM_STEP = 0.001, 0.9, 0.999, 1e-08, 0.01, 10
VMEM_LIMIT_BYTES = 48 * 1024 * 1024
_SQRT_HALF = 0.7071067811865476
_INV_SQRT_2PI = 0.3989422804014327


def _pick(n, cands):
    for c in cands:
        if n % c == 0:
            return c
    return n


def _ceil_to(n, m):
    return -(-n // m) * m


def _params():
    return pltpu.CompilerParams(vmem_limit_bytes=VMEM_LIMIT_BYTES)


def _tile(n, cap):
    best = 0
    for d in range(LANE, min(n, cap) + 1, LANE):
        if n % d == 0:
            best = d
    return best or n


def _matmul_tiles(M, N, K, a_bytes, b_bytes, o_bytes, has_add, forced):
    tm = forced.get("m") or _tile(M, 1024)
    tn = forced.get("n") or _tile(N, 1024)
    tk = forced.get("k") or _tile(K, 2048)

    def vmem(tm, tn, tk):
        acc = tm * tn * 4 if tk < K else 0
        return 2 * (tm * tk * a_bytes + tk * tn * b_bytes + tm * tn * (o_bytes + (4 if has_add else 0))) + acc

    while vmem(tm, tn, tk) > (VMEM_LIMIT_BYTES * 3) // 4:
        if "k" not in forced and tk > 512 and _tile(K, tk // 2) < tk:
            tk = _tile(K, tk // 2)
        elif "m" not in forced and _tile(M, tm // 2) < tm:
            tm = _tile(M, tm // 2)
        else:
            break
    return tm, tn, tk


def _matmul(a, b, *, mode, out_dtype=F32, name, add=None, deps=(), out_blocks=0, epi=None, b2=None):
    def view(x):
        return (x.shape[1], x.shape[0] * x.shape[2], x.shape[2]) if x.ndim == 3 else (x.shape[0], x.shape[1], 0)

    (ar, ac, aw), (br, bc, bw) = view(a), view(b)
    a_col, b_col = {"nn": ("k", "n"), "nt": ("k", "k"), "tn": ("m", "n")}[mode]
    if mode == "nn":
        M, K, K2, N = ar, ac, br, bc
    elif mode == "nt":
        M, K, N, K2 = ar, ac, br, bc
    else:
        K, M, K2, N = ar, ac, br, bc
    assert K == K2, (a.shape, b.shape, mode)
    forced = {}
    for dim, w in ((a_col, aw), (b_col, bw), ("n", N // out_blocks if out_blocks else 0)):
        if w:
            assert forced.get(dim, w) == w
            forced[dim] = w
    has_add = add is not None
    tile_bytes = (sum(jnp.dtype(d).itemsize for d in epi[2]) + sum((e[0] if isinstance(e, tuple) else e).dtype.itemsize for e in epi[1])
                  if epi is not None else jnp.dtype(out_dtype).itemsize)
    tm, tn, tk = _matmul_tiles(M, N, K, a.dtype.itemsize, b.dtype.itemsize, tile_bytes, has_add, forced)
    kb = 1
    if "k" in forced and mode != "tn":
        lanes_ok = all(w or tk % LANE == 0 for w in (aw, bw if mode == "nt" else 1))
        kb = next(c for c in (4, 2, 1) if (K // tk) % c == 0 and (c == 1 or (lanes_ok and c * tk <= 1536)))
    nk = K // (tk * kb)
    dn = {"nn": (((1,), (0,)), ((), ())), "nt": (((1,), (1,)), ((), ())), "tn": (((0,), (0,)), ((), ()))}[mode]
    pick = {"m": lambda i, j, k: i, "n": lambda i, j, k: j, "k": lambda i, j, k: k}
    size = {"m": tm, "n": tn, "k": tk}

    def spec(blocked, row_dim, col_dim):
        rf, cf = pick[row_dim], pick[col_dim]
        reps = {d: (kb if d == "k" else 1) for d in (row_dim, col_dim)}
        if blocked:
            lead = kb if col_dim == "k" and kb > 1 else None
            return pl.BlockSpec((lead, size[row_dim], size[col_dim]), lambda i, j, k: (cf(i, j, k), rf(i, j, k), 0))
        return pl.BlockSpec((size[row_dim] * reps[row_dim], size[col_dim] * reps[col_dim]), lambda i, j, k: (rf(i, j, k), cf(i, j, k)))

    def k_part(ref, blocked, k_on_rows, j):
        if kb == 1:
            return ref[...]
        if blocked:
            return ref[j]
        return ref[j * tk:(j + 1) * tk, :] if k_on_rows else ref[:, j * tk:(j + 1) * tk]

    a_spec = spec(aw, "k" if mode == "tn" else "m", a_col)
    b_spec = spec(bw, "n" if mode == "nt" else "k", b_col)
    o_spec = spec(out_blocks, "m", "n")
    epi_fn, epi_ins, epi_dtypes = epi if epi is not None else (None, [], [out_dtype])
    epi_ins = [e if isinstance(e, tuple) else (e, None) for e in epi_ins]
    n_epi = len(epi_ins)
    twin = b2 is not None
    assert not twin or (nk == 1 and kb == 1 and epi is not None and b2.shape == b.shape)
    n_in = 2 + twin + has_add + n_epi + len(deps)
    n_out = len(epi_dtypes)

    def body(*refs):
        a_ref, b_ref = refs[0], refs[1]
        add_ref = refs[2 + twin] if has_add else None
        epi_refs = refs[2 + twin + has_add:2 + twin + has_add + n_epi]
        o_refs = refs[n_in:n_in + n_out]
        part = None
        for q in range(kb):
            a_q = k_part(a_ref, aw and a_col == "k", False, q)
            b_q = k_part(b_ref, bw and b_col == "k", mode == "nn", q)
            prod = lax.dot_general(a_q.astype(BF16), b_q.astype(BF16), dn, preferred_element_type=F32)
            part = prod if part is None else part + prod
        second = [lax.dot_general(a_ref[...].astype(BF16), refs[2][...].astype(BF16), dn, preferred_element_type=F32)] if twin else []

        def finish(res):
            outs = epi_fn(res, *second, *[e[...] for e in epi_refs]) if epi_fn is not None else (res,)
            for o_ref, val in zip(o_refs, outs):
                o_ref[...] = val.astype(o_ref.dtype)

        if nk == 1:
            finish(part + add_ref[...] if has_add else part)
            return
        acc_ref = refs[-1]
        kk = pl.program_id(2)

        @pl.when(kk == 0)
        def _():
            acc_ref[...] = part + add_ref[...] if has_add else part

        @pl.when(kk > 0)
        def _():
            acc_ref[...] += part

        @pl.when(kk == nk - 1)
        def _():
            finish(acc_ref[...])

    def epi_spec(arr, off):
        if off is None:
            return o_spec
        assert off % tn == 0
        return pl.BlockSpec((tm, tn), lambda i, j, k: (i, j + off // tn))

    ins = [a, b] + ([b2] if twin else []) + ([add] if has_add else []) + [arr for arr, _ in epi_ins] + list(deps)
    in_specs = ([a_spec, b_spec] + ([b_spec] if twin else []) + ([o_spec] if has_add else []) + [epi_spec(arr, off) for arr, off in epi_ins]
                + [pl.BlockSpec(d.shape, lambda i, j, k, nd=d.ndim: (0,) * nd) for d in deps])
    o_shape = (out_blocks, M, tn) if out_blocks else (M, N)
    res = pl.pallas_call(
        body, name=name, grid=(M // tm, N // tn, nk), in_specs=in_specs, out_specs=[o_spec] * n_out,
        out_shape=[jax.ShapeDtypeStruct(o_shape, dt) for dt in epi_dtypes],
        scratch_shapes=[pltpu.VMEM((tm, tn), F32)] if nk > 1 else [],
        compiler_params=_params())(*ins)
    return res[0] if epi is None else list(res)


def _rowwise(fn, rows, pars, row_outs, acc_outs, *, name, tm=256, deps=()):
    rows = [r if isinstance(r, tuple) else (r, r.shape[1], 0) for r in rows]
    row_outs = [o if len(o) == 5 else (o[0], o[1], o[0], 0, None) for o in row_outs]
    aliased = [(k, o[4]) for k, o in enumerate(row_outs) if o[4] is not None]
    R = rows[0][0].shape[0]
    if max(w for _, w, _ in rows) > 4096:
        tm = tm // 2
    tm = min(tm, R)
    assert R % tm == 0
    nr, npar = len(rows), len(pars)
    nro = len(row_outs)
    n_in = nr + npar + len(deps) + len(aliased)

    def body(*refs):
        rv = [r[...] for r in refs[:nr]]
        pv = [p[...] for p in refs[nr:nr + npar]]
        outs = refs[n_in:]
        ro, ao = fn(rv, pv)
        first = pl.program_id(0) == 0
        for o_ref, val in zip(outs[:nro], ro):
            o_ref[...] = val.astype(o_ref.dtype)

        @pl.when(first)
        def _():
            for o_ref, val in zip(outs[nro:], ao):
                o_ref[...] = val

        @pl.when(jnp.logical_not(first))
        def _():
            for o_ref, val in zip(outs[nro:], ao):
                o_ref[...] += val

    in_specs = ([pl.BlockSpec((tm, w), lambda i, cb=cb: (i, cb)) for _, w, cb in rows]
                + [pl.BlockSpec(p.shape, lambda i, nd=p.ndim: (0,) * nd) for p in list(pars) + list(deps)]
                + [pl.BlockSpec(memory_space=pl.ANY)] * len(aliased))
    out_shape = ([jax.ShapeDtypeStruct((R, full), dt) for _, dt, full, _, _ in row_outs]
                 + [jax.ShapeDtypeStruct(s, F32) for s in acc_outs])
    out_specs = ([pl.BlockSpec((tm, f), lambda i, cb=cb: (i, cb)) for f, _, _, cb, _ in row_outs]
                 + [pl.BlockSpec(s, lambda i, nd=len(s): (0,) * nd) for s in acc_outs])
    res = pl.pallas_call(body, name=name, grid=(R // tm,), in_specs=in_specs, out_specs=out_specs, out_shape=out_shape,
                         input_output_aliases={n_in - len(aliased) + q: k for q, (k, _) in enumerate(aliased)},
                         compiler_params=_params())(*[r for r, _, _ in rows], *pars, *deps, *[buf for _, buf in aliased])
    return list(res)


def _bdot(a, b, mode="nn"):
    dn = {"nn": (((1,), (0,)), ((), ())), "nt": (((1,), (1,)), ((), ())), "tn": (((0,), (0,)), ((), ()))}[mode]
    return lax.dot_general(a.astype(BF16), b.astype(BF16), dn, preferred_element_type=F32)


def _sigmoid(x):
    return jax.nn.sigmoid(x)


def _softplus(x):
    return jnp.maximum(x, 0.0) + jnp.log1p(jnp.exp(-jnp.abs(x)))


def _gelu(z):
    return 0.5 * z * (1.0 + lax.erf(z * _SQRT_HALF))


def _gelu_grad(z):
    return 0.5 * (1.0 + lax.erf(z * _SQRT_HALF)) + z * jnp.exp(-0.5 * z * z) * _INV_SQRT_2PI


def _mean(x):
    return jnp.mean(x, axis=-1, keepdims=True)


def _colsum(x):
    return jnp.sum(x, axis=0, keepdims=True)


def _rms_fwd(x, g, name, deps=(), halves=False):
    D = x.shape[1]

    def fn(rv, pv):
        (xv,), (gv,) = rv, pv
        n = xv * lax.rsqrt(_mean(xv * xv) + RMS_EPS) * gv
        return ([n, n[:, :D // 2], n[:, D // 2:]] if halves else [n]), []
    outs = [(D, BF16)] + ([(D // 2, BF16)] * 2 if halves else [])
    res = _rowwise(fn, [x], [g], outs, [], name=name, deps=deps)
    return res if halves else res[0]


def _rms_bwd(dn, x, dres, g, name, deps=()):
    parts = list(dn) if isinstance(dn, (list, tuple)) else [dn]

    def fn(rv, pv):
        (xv, drv), (gv,) = rv[len(parts):], pv
        dnv = rv[0] if len(parts) == 1 else jnp.concatenate(rv[:len(parts)], axis=1)
        r = lax.rsqrt(_mean(xv * xv) + RMS_EPS)
        yn = xv * r
        dyg = dnv * gv
        dx = drv + r * (dyg - yn * _mean(dyg * yn))
        return [dx, dx], [_colsum(dnv * yn)]
    D = x.shape[1]
    return _rowwise(fn, parts + [x, dres], [g], [(D, F32), (D, BF16)], [(1, D)], name=name, deps=deps)


def _rwkv_layout(RW, Lw, La, Lg, D):
    widths = [RW, RW, RW, Lw, La, Lg]
    pw = [_ceil_to(w, LANE) for w in widths]
    pw[5] += _ceil_to(sum(pw), 2 * D) - sum(pw)
    offs = [sum(pw[:i]) for i in range(6)]
    return widths, pw, offs, sum(pw)


def _pad_rwkv_cols(a, lay):
    widths, pw, _, _ = lay
    pieces, src = [], 0
    for w, p in zip(widths, pw):
        pieces.append(a[:, src:src + w])
        if p > w:
            pieces.append(jnp.zeros((a.shape[0], p - w), a.dtype))
        src += w
    return jnp.concatenate(pieces, axis=1)


def _unpad_rwkv_cols(a, lay):
    widths, _, offs, _ = lay
    return jnp.concatenate([a[:, o:o + w] for o, w in zip(offs, widths)], axis=1)


def _proj_pieces(lay, D, cs):
    widths, _, offs, rcp = lay
    rc = sum(widths)
    segs = [(sum(widths[:j]), widths[j], offs[j]) for j in range(6)] + [(rc, D, rcp + 2 * D), (rc + D, D, rcp), (rc + 2 * D, D, rcp + D)]
    pieces = []
    for start, width, dst in segs:
        n = start
        while n < start + width:
            d, off = divmod(n, cs)
            take = min(cs - off, start + width - n)
            pieces.append((d, off, dst + n - start, take))
            n += take
    return pieces


def _w_in_to_proj(g, lay, D, name):
    nb, rows, cs = g.shape
    icp = lay[3] + 3 * D
    pieces = _proj_pieces(lay, D, cs)
    tm = _pick(rows, (256, 128, 64, 32, 16))

    def body(i_ref, o_ref):
        o_ref[...] = jnp.zeros_like(o_ref)
        for d, src, dst, w in pieces:
            o_ref[:, dst:dst + w] = i_ref[d, :, src:src + w]

    return pl.pallas_call(
        body, name=name, grid=(rows // tm,), in_specs=[pl.BlockSpec((nb, tm, cs), lambda i: (0, i, 0))],
        out_specs=pl.BlockSpec((tm, icp), lambda i: (i, 0)), out_shape=jax.ShapeDtypeStruct((rows, icp), g.dtype),
        compiler_params=_params())(g)


def _dw_in_from_proj(a, lay, D, cs, name):
    rows, icp = a.shape
    pieces = _proj_pieces(lay, D, cs)
    tm = _pick(rows, (256, 128, 64, 32, 16))

    def body(i_ref, o_ref):
        for d, src, dst, w in pieces:
            o_ref[d, :, src:src + w] = i_ref[:, dst:dst + w]

    return pl.pallas_call(
        body, name=name, grid=(rows // tm,), in_specs=[pl.BlockSpec((tm, icp), lambda i: (i, 0))],
        out_specs=pl.BlockSpec((N_DEV, tm, cs), lambda i: (0, i, 0)), out_shape=jax.ShapeDtypeStruct((N_DEV, rows, cs), a.dtype),
        compiler_params=_params())(a)


def _pad_rows(a, rows):
    return a if a.shape[0] == rows else jnp.concatenate([a, jnp.zeros((rows - a.shape[0], a.shape[1]), a.dtype)], axis=0)


def _token_shift(p, halo, mu, i):
    tm = p.shape[0]
    hid = lax.broadcasted_iota(jnp.int32, (SUBLANE, 1), 0)
    before = jnp.sum(jnp.where(hid == SUBLANE - 1, halo, 0.0), axis=0, keepdims=True)
    before = jnp.where(i == 0, 0.0, before)
    rid = lax.broadcasted_iota(jnp.int32, (tm, 1), 0)
    prev = jnp.where(rid == 0, before, pltpu.roll(p, 1, 0))
    d = prev - p
    return p + d * mu, d


def _rwkv_math(ps, w0, a0, k_k, k_a, wlw, wla, wlg, lay):
    _, pw, offs, _ = lay
    r, k, v, xw, xa, xg = (ps[:, offs[j]:offs[j] + pw[j]] for j in range(6))
    tw = jnp.tanh(xw)
    ww = w0 + _bdot(tw, wlw)
    lw = -jnp.exp(-_softplus(-ww) - 0.5)
    a = _sigmoid(a0 + _bdot(xa, wla))
    sg = _sigmoid(xg)
    g = _bdot(sg, wlg)
    return dict(r=r, k=k, v=v, xa=xa, tw=tw, ww=ww, lw=lw, a=a, sg=sg, g=g, kkp=k * k_k, k2=k * (1.0 + (a - 1.0) * k_a))


def _halo_specs(T, tm, width, after):
    hb = tm // SUBLANE
    last = T // SUBLANE - 1
    if after:
        return pl.BlockSpec((SUBLANE, width), lambda i: (jnp.minimum((i + 1) * hb, last), 0))
    return pl.BlockSpec((SUBLANE, width), lambda i: (jnp.maximum(i * hb - 1, 0), 0))


def _rowsum(x):
    return jnp.sum(x, axis=-1, keepdims=True)


def _kk_math(kkp):
    nrm = jnp.sqrt(_rowsum(kkp * kkp))
    inv = 1.0 / jnp.maximum(nrm, 1e-12)
    return nrm, inv, kkp * inv


def _rwkv_pre(p, mu, small, lora, lay, name):
    T, rcp = p.shape[0], lay[3]
    H = lay[0][0] // HEAD
    tm = min(128, T)

    def body(p_ref, ph_ref, mu_ref, w0_ref, a0_ref, kk_ref, ka_ref, wlw_ref, wla_ref, wlg_ref, r_o, lw_o, k2_o, v_o, aa_o, bb_o, g_o):
        ps, _ = _token_shift(p_ref[...], ph_ref[...], mu_ref[...], pl.program_id(0))
        q = _rwkv_math(ps, w0_ref[...], a0_ref[...], kk_ref[...], ka_ref[...], wlw_ref[...], wla_ref[...], wlg_ref[...], lay)
        for h in range(H):
            sl = slice(h * HEAD, (h + 1) * HEAD)
            for o_ref, key in ((r_o, "r"), (lw_o, "lw"), (k2_o, "k2"), (v_o, "v"), (g_o, "g")):
                o_ref[h] = q[key][:, sl]
            _, _, kk = _kk_math(q["kkp"][:, sl])
            aa_o[h] = -kk
            bb_o[h] = kk * q["a"][:, sl]

    whole = lambda arr: pl.BlockSpec(arr.shape, lambda i: (0, 0))
    return pl.pallas_call(
        body, name=name, grid=(T // tm,),
        in_specs=([pl.BlockSpec((tm, rcp), lambda i: (i, 0)), _halo_specs(T, tm, rcp, False), whole(mu)]
                  + [whole(s) for s in small] + [whole(w) for w in lora]),
        out_specs=[pl.BlockSpec((H, tm, HEAD), lambda i: (0, i, 0))] * 7, out_shape=[jax.ShapeDtypeStruct((H, T, HEAD), F32)] * 7,
        compiler_params=_params())(p, p, mu, *small, *lora)


def _rwkv_pre_bwd(p, mu, small, lora, hgrads, lay, name):
    T, rcp = p.shape[0], lay[3]
    widths, pw, offs, _ = lay
    RW = widths[0]
    H = RW // HEAD
    tm = min(128, T)

    def body(p_ref, ph_ref, mu_ref, w0_ref, a0_ref, kk_ref, ka_ref, wlw_ref, wla_ref, wlg_ref,
             dr1, dr2, dk1, dk2b, dv1, dv2, dlw_h, daa, dbb, dg_h,
             dps_ref, dmu_ref, dw0_ref, da0_ref, dkk_ref, dka_ref, dwlw_ref, dwla_ref, dwlg_ref,
             s_dr, s_dk2, s_dv, s_dlw, s_dkkp, s_da, s_dg):
        i = pl.program_id(0)
        ps, dprev = _token_shift(p_ref[...], ph_ref[...], mu_ref[...], i)
        k_k, k_a = kk_ref[...], ka_ref[...]
        q = _rwkv_math(ps, w0_ref[...], a0_ref[...], k_k, k_a, wlw_ref[...], wla_ref[...], wlg_ref[...], lay)
        k, a, lw, ww, tw, sg = q["k"], q["a"], q["lw"], q["ww"], q["tw"], q["sg"]
        for h in range(H):
            sl = slice(h * HEAD, (h + 1) * HEAD)
            s_dr[:, sl] = dr1[h] + dr2[h]
            s_dk2[:, sl] = dk1[h] + dk2b[h]
            s_dv[:, sl] = dv1[h] + dv2[h]
            s_dlw[:, sl] = dlw_h[h]
            s_dg[:, sl] = dg_h[h]
            nrm, inv, kk = _kk_math(q["kkp"][:, sl])
            dbb_h = dbb[h]
            dkk = dbb_h * a[:, sl] - daa[h]
            s_dkkp[:, sl] = jnp.where(nrm > 1e-12, inv * (dkk - kk * _rowsum(dkk * kk)), dkk * inv)
            s_da[:, sl] = dbb_h * kk
        dk2, dkkp, dg = s_dk2[...], s_dkkp[...], s_dg[...]
        dk = dk2 * (1.0 + (a - 1.0) * k_a) + dkkp * k_k
        da = s_da[...] + dk2 * k * k_a
        dpa = da * a * (1.0 - a)
        dww = s_dlw[...] * lw * _sigmoid(-ww)
        dxa = _bdot(dpa, wla_ref[...], "nt")
        dxw = _bdot(dww, wlw_ref[...], "nt") * (1.0 - tw * tw)
        dxg = _bdot(dg, wlg_ref[...], "nt") * sg * (1.0 - sg)
        segs = (s_dr[...], dk, s_dv[...], dxw, dxa, dxg)
        sums = [dmu_ref, dw0_ref, da0_ref, dkk_ref, dka_ref, dwlw_ref, dwla_ref, dwlg_ref]

        @pl.when(i == 0)
        def _():
            for s in sums:
                s[...] = jnp.zeros_like(s)

        for j, seg in enumerate(segs):
            sl = slice(offs[j], offs[j] + pw[j])
            dps_ref[:, sl] = seg
            dmu_ref[:, sl] += _colsum(seg * dprev[:, sl])
        dw0_ref[...] += _colsum(dww)
        da0_ref[...] += _colsum(dpa)
        dkk_ref[...] += _colsum(dkkp * k)
        dka_ref[...] += _colsum(dk2 * k * (a - 1.0))
        dwlw_ref[...] += _bdot(tw, dww, "tn")
        dwla_ref[...] += _bdot(q["xa"], dpa, "tn")
        dwlg_ref[...] += _bdot(sg, dg, "tn")

    whole = lambda arr: pl.BlockSpec(arr.shape, lambda i: (0, 0))
    row = lambda w: pl.BlockSpec((tm, w), lambda i: (i, 0))
    acc_shapes = [(1, rcp), (1, RW), (1, RW), (1, RW), (1, RW)] + [w.shape for w in lora]
    return pl.pallas_call(
        body, name=name, grid=(T // tm,),
        in_specs=([row(rcp), _halo_specs(T, tm, rcp, False), whole(mu)] + [whole(s) for s in small] + [whole(w) for w in lora]
                  + [pl.BlockSpec((H, tm, HEAD), lambda i: (0, i, 0))] * 10),
        out_specs=[row(rcp)] + [pl.BlockSpec(s, lambda i: (0, 0)) for s in acc_shapes],
        out_shape=[jax.ShapeDtypeStruct((T, rcp), F32)] + [jax.ShapeDtypeStruct(s, F32) for s in acc_shapes],
        scratch_shapes=[pltpu.VMEM((tm, RW), F32)] * 7, compiler_params=_params())(p, p, mu, *small, *lora, *hgrads)


def _shift_bwd(dps, mu, dproj, name):
    T, rcp = dps.shape
    tm = min(256, T)
    nt = T // tm

    def body(d_ref, dh_ref, mu_ref, buf_ref, o_ref):
        i = pl.program_id(0)
        d = d_ref[...]
        hid = lax.broadcasted_iota(jnp.int32, (SUBLANE, 1), 0)
        after = jnp.sum(jnp.where(hid == 0, dh_ref[...], 0.0), axis=0, keepdims=True)
        after = jnp.where(i == nt - 1, 0.0, after)
        rid = lax.broadcasted_iota(jnp.int32, (tm, 1), 0)
        nxt = jnp.where(rid == tm - 1, after, pltpu.roll(d, tm - 1, 0))
        mu_v = mu_ref[...]
        o_ref[...] = (d * (1.0 - mu_v) + nxt * mu_v).astype(BF16)

    row = pl.BlockSpec((tm, rcp), lambda i: (i, 0))
    return pl.pallas_call(
        body, name=name, grid=(nt,),
        in_specs=[row, _halo_specs(T, tm, rcp, True), pl.BlockSpec(mu.shape, lambda i: (0, 0)), pl.BlockSpec(memory_space=pl.ANY)],
        out_specs=row, out_shape=jax.ShapeDtypeStruct(dproj.shape, BF16), input_output_aliases={3: 0},
        compiler_params=_params())(dps, dps, mu, dproj)


def _head_post_math(y, r, k2, v, lg, lb, rk):
    yc = y - _mean(y)
    rstd = lax.rsqrt(_mean(yc * yc) + LNX_EPS)
    yn = yc * rstd
    s = _rowsum(r * k2 * rk)
    return yn, rstd, yn * lg + lb + s * v, s


def _head_post(y, r, k2, v, g, hp, name, deps=()):
    H, T, _ = y.shape
    tm = min(128, T)

    def body(y_ref, r_ref, k_ref, v_ref, g_ref, lg_ref, lb_ref, rk_ref, *rest):
        o_ref = rest[-1]
        _, _, t, _ = _head_post_math(y_ref[...], r_ref[...], k_ref[...], v_ref[...], lg_ref[...], lb_ref[...], rk_ref[...])
        out = (t * g_ref[...]).astype(BF16)
        for h in range(H):
            o_ref[:, h * HEAD:(h + 1) * HEAD] = out[h]

    blk = pl.BlockSpec((H, tm, HEAD), lambda i: (0, i, 0))
    par = pl.BlockSpec((H, 1, HEAD), lambda i: (0, 0, 0))
    return pl.pallas_call(
        body, name=name, grid=(T // tm,),
        in_specs=[blk] * 5 + [par] * 3 + [pl.BlockSpec(d.shape, lambda i, nd=d.ndim: (0,) * nd) for d in deps],
        out_specs=pl.BlockSpec((tm, H * HEAD), lambda i: (i, 0)),
        out_shape=jax.ShapeDtypeStruct((T, H * HEAD), BF16), compiler_params=_params())(y, r, k2, v, g, *hp, *deps)


def _head_post_bwd(dya, y, r, k2, v, g, hp, name, deps=()):
    H, T, _ = y.shape
    tm = min(128, T)
    hsum = lambda t: jnp.sum(t, axis=1, keepdims=True)

    def body(d_ref, y_ref, r_ref, k_ref, v_ref, g_ref, lg_ref, lb_ref, rk_ref, *rest):
        outs, d_s = rest[len(deps):len(deps) + 8], rest[-1]
        for h in range(H):
            d_s[h] = d_ref[:, h * HEAD:(h + 1) * HEAD]
        d_v, r_v, k_v, v_v, lg, rk = d_s[...], r_ref[...], k_ref[...], v_ref[...], lg_ref[...], rk_ref[...]
        yn, rstd, t, s = _head_post_math(y_ref[...], r_v, k_v, v_v, lg, lb_ref[...], rk)
        dyo = d_v * g_ref[...]
        dyn = dyo * lg
        ds = _rowsum(dyo * v_v)
        vals = (rstd * (dyn - _mean(dyn) - yn * _mean(dyn * yn)), ds * k_v * rk, ds * r_v * rk, dyo * s, d_v * t)
        for o_ref, val in zip(outs[:5], vals):
            o_ref[...] = val
        sums = (hsum(dyo * yn), hsum(dyo), hsum(ds * r_v * k_v))
        first = pl.program_id(0) == 0

        @pl.when(first)
        def _():
            for o_ref, val in zip(outs[5:], sums):
                o_ref[...] = val

        @pl.when(jnp.logical_not(first))
        def _():
            for o_ref, val in zip(outs[5:], sums):
                o_ref[...] += val

    blk = pl.BlockSpec((H, tm, HEAD), lambda i: (0, i, 0))
    par = pl.BlockSpec((H, 1, HEAD), lambda i: (0, 0, 0))
    return pl.pallas_call(
        body, name=name, grid=(T // tm,),
        in_specs=([pl.BlockSpec((tm, H * HEAD), lambda i: (i, 0))] + [blk] * 5 + [par] * 3
                  + [pl.BlockSpec(d.shape, lambda i, nd=d.ndim: (0,) * nd) for d in deps]),
        out_specs=[blk] * 5 + [par] * 3,
        out_shape=[jax.ShapeDtypeStruct((H, T, HEAD), F32)] * 5 + [jax.ShapeDtypeStruct((H, 1, HEAD), F32)] * 3,
        scratch_shapes=[pltpu.VMEM((H, tm, HEAD), F32)], compiler_params=_params())(dya, y, r, k2, v, g, *hp, *deps)


def _bmm(x, y, mode):
    dn = {"nn": (((2,), (1,)), ((0,), (0,))), "nt": (((2,), (2,)), ((0,), (0,))), "tn": (((1,), (1,)), ((0,), (0,)))}[mode]
    (xh, xl), (yh, yl) = _split(x), _split(y)
    dot = lambda p, q: lax.dot_general(p, q, dn, preferred_element_type=F32)
    out = dot(xh, yh)
    if yl is not None:
        out = out + dot(xh, yl)
    if xl is not None:
        out = out + dot(xl, yh)
    return out


def _split(x):
    if isinstance(x, tuple):
        return x
    hi = x.astype(BF16)
    return hi, (x - hi.astype(F32)).astype(BF16)


def _exact(x):
    return x.astype(BF16), None


def _round(x):
    return x if isinstance(x, tuple) else (x.astype(BF16), None)


def _rows(*xs):
    if isinstance(xs[0], tuple):
        return tuple(None if any(p is None for p in parts) else jnp.concatenate(parts, axis=1) for parts in zip(*xs))
    return jnp.concatenate(xs, axis=1)


def _wkv_chunk(r, lw, k, v, a, b):
    hb, C, _ = r.shape
    ti = lax.broadcasted_iota(jnp.int32, (C, C), 0)
    si = lax.broadcasted_iota(jnp.int32, (C, C), 1)
    linc, lstr, eye = (ti >= si).astype(F32), (ti > si).astype(F32), (ti == si).astype(F32)
    qmask = jnp.concatenate([jnp.concatenate([lstr, lstr], axis=1), jnp.concatenate([linc, linc], axis=1)], axis=0)
    lincb = _exact(jnp.broadcast_to(linc, (hb, C, C)))
    both = _exact(jnp.broadcast_to(jnp.concatenate([linc, lstr], axis=0), (hb, 2 * C, C)))
    ones = _exact(jnp.ones_like(v))
    lws = _split(lw)
    ci = _bmm(lincb, lws, "nn")
    cC = jnp.sum(lw, axis=1, keepdims=True)
    gi, ge, gn, gr = jnp.exp(ci), jnp.exp(ci - lw), jnp.exp(-ci), jnp.exp(cC - ci)
    q = dict(At=a * ge, Rt=r * gi, Bt=b * gn, Kt=k * gn, Bh=b * gr, Kh=k * gr)
    s = dict(AR=_round(_rows(q["At"], q["Rt"])), BK=_round(_rows(q["Bt"], q["Kt"])), BKh=_round(_rows(q["Bh"], q["Kh"])), v=_round(v))
    quad = _bmm(s["AR"], s["BK"], "nt") * qmask
    s["top"], s["bot"] = _round(quad[:, :C]), _round(quad[:, C:])
    A_ab = quad[:, :C, :C]
    Tm = eye + A_ab
    Pw = _round(A_ab)
    n = 1
    while 2 * n < C:
        Pw = _round(_bmm(Pw, Pw, "nn"))
        Tm = Tm + _bmm(_round(Tm), Pw, "nn")
        n *= 2
    s["Tm"] = _round(Tm)
    gC = jnp.exp(_bmm(lws, ones, "tn"))
    q.update(gi=gi, ge=ge, gn=gn, gr=gr, qmask=qmask, both=both, gC=gC, ones=ones, s=s)
    return q


def _wkv_u(s, H0s, C):
    arh = _bmm(s["AR"], H0s, "nn")
    zv = _rows(tuple(None if p is None else jnp.zeros_like(p) for p in s["v"]), s["v"])
    U = _bmm(s["Tm"], _round(arh[:, :C] + _bmm(s["top"], zv, "nn")), "nn")
    return arh, _rows(_round(U), s["v"])


def _wkv_fwd(r, lw, k, v, a, b, name):
    H, T, N = r.shape
    C = min(WKV_CHUNK, T)
    nc = T // C
    hb = _pick(H, (16, 8, 4, 2))

    def body(r_ref, lw_ref, k_ref, v_ref, a_ref, b_ref, y_ref, st_ref, h_ref):
        @pl.when(pl.program_id(1) == 0)
        def _():
            h_ref[...] = jnp.zeros_like(h_ref)

        H0 = h_ref[...]
        st_ref[0] = H0
        q = _wkv_chunk(r_ref[...], lw_ref[...], k_ref[...], v_ref[...], a_ref[...], b_ref[...])
        s = q["s"]
        arh, UV = _wkv_u(s, _round(H0), C)
        y_ref[...] = arh[:, C:] + _bmm(s["bot"], UV, "nn")
        h_ref[...] = q["gC"] * H0 + _bmm(s["BKh"], UV, "tn")

    blk = pl.BlockSpec((hb, C, N), lambda h, c: (h, c, 0))
    return pl.pallas_call(
        body, name=name, grid=(H // hb, nc), in_specs=[blk] * 6,
        out_specs=[blk, pl.BlockSpec((1, hb, N, N), lambda h, c: (c, h, 0, 0))],
        out_shape=[jax.ShapeDtypeStruct((H, T, N), F32), jax.ShapeDtypeStruct((nc, H, N, N), F32)],
        scratch_shapes=[pltpu.VMEM((hb, N, N), F32)], compiler_params=_params())(r, lw, k, v, a, b)


def _wkv_bwd(r, lw, k, v, a, b, states, dy, name):
    H, T, N = r.shape
    C = min(WKV_CHUNK, T)
    nc = T // C
    hb = _pick(H, (16, 8, 4, 2))

    def body(r_ref, lw_ref, k_ref, v_ref, a_ref, b_ref, st_ref, dy_ref, dr_ref, dlw_ref, dk_ref, dv_ref, da_ref, db_ref, dh_ref):
        @pl.when(pl.program_id(1) == 0)
        def _():
            dh_ref[...] = jnp.zeros_like(dh_ref)

        dHC = dh_ref[...]
        H0 = st_ref[0]
        q = _wkv_chunk(r_ref[...], lw_ref[...], k_ref[...], v_ref[...], a_ref[...], b_ref[...])
        s, gC = q["s"], q["gC"]
        H0s, dHs, dY = _round(H0), _round(dHC), _round(dy_ref[...])
        _, UV = _wkv_u(s, H0s, C)
        bot_dy = _bmm(s["bot"], dY, "tn")
        bkh_dh = _bmm(s["BKh"], dHs, "nn")
        dP = _round(_bmm(s["Tm"], _round(bot_dy[:, :C] + bkh_dh[:, :C]), "tn"))
        dv_ref[...] = bot_dy[:, C:] + bkh_dh[:, C:] + _bmm(s["top"], dP, "tn")[:, C:]
        dPY = _rows(dP, dY)
        dh_ref[...] = gC * dHC + _bmm(s["AR"], dPY, "tn")
        dquad = _round(_bmm(dPY, UV, "nt") * q["qmask"])
        dAR = _bmm(dPY, H0s, "nt") + _bmm(dquad, s["BK"], "nn")
        dBK = _bmm(dquad, s["AR"], "tn")
        dBKh = _bmm(UV, dHs, "nt")
        dAt, dRt, dBt, dKt, dBh, dKh = dAR[:, :C], dAR[:, C:], dBK[:, :C], dBK[:, C:], dBKh[:, :C], dBKh[:, C:]
        dr_ref[...] = dRt * q["gi"]
        da_ref[...] = dAt * q["ge"]
        db_ref[...] = dBt * q["gn"] + dBh * q["gr"]
        dk_ref[...] = dKt * q["gn"] + dKh * q["gr"]
        tail = dBh * q["Bh"] + dKh * q["Kh"]
        dci = dRt * q["Rt"] - dBt * q["Bt"] - dKt * q["Kt"] - tail
        dcC = jnp.sum(tail, axis=1, keepdims=True) + _bmm(q["ones"], H0 * dHC * gC, "nt")
        dlw_ref[...] = _bmm(q["both"], _rows(dci, dAt * q["At"]), "tn") + dcC

    blk = pl.BlockSpec((hb, C, N), lambda h, c: (h, nc - 1 - c, 0))
    st = pl.BlockSpec((1, hb, N, N), lambda h, c: (nc - 1 - c, h, 0, 0))
    return pl.pallas_call(
        body, name=name, grid=(H // hb, nc), in_specs=[blk] * 6 + [st, blk], out_specs=[blk] * 6,
        out_shape=[jax.ShapeDtypeStruct((H, T, N), F32)] * 6,
        scratch_shapes=[pltpu.VMEM((hb, N, N), F32)], compiler_params=_params())(r, lw, k, v, a, b, states, dy)


def _sgu_ln(z, SW, lng, lnb):
    ge = _gelu(z)
    u, vv = ge[:, :SW], ge[:, SW:]
    xc = vv - _mean(vv)
    rstd = lax.rsqrt(_mean(xc * xc) + LN_EPS)
    vn = xc * rstd
    return u, vn, rstd, vn * lng + lnb


def _causal(ws_ref, g):
    ti = lax.broadcasted_iota(jnp.int32, (SGU_CHUNK, SGU_CHUNK), 0)
    si = lax.broadcasted_iota(jnp.int32, (SGU_CHUNK, SGU_CHUNK), 1)
    return ti >= si, jnp.where(ti >= si, ws_ref[g], 0.0).astype(BF16)


def _sgu_fwd(proj, zblock, lng, lnb, ws, bexp, name):
    T, SW = proj.shape[0], lng.shape[1]
    G = ws.shape[0]
    tr = min(256, T)
    nch = tr // SGU_CHUNK

    def body(z_ref, lng_ref, lnb_ref, ws_ref, be_ref, o_ref):
        u, _, _, vl = _sgu_ln(z_ref[...], SW, lng_ref[...], lnb_ref[...])
        for g in range(G):
            cs = slice(g * SGU_GROUP, (g + 1) * SGU_GROUP)
            _, wc = _causal(ws_ref, g)
            for n in range(nch):
                rs = slice(n * SGU_CHUNK, (n + 1) * SGU_CHUNK)
                m = jnp.dot(wc, vl[rs, cs].astype(BF16), preferred_element_type=F32) + be_ref[:, cs]
                o_ref[rs, cs] = (u[rs, cs] * m).astype(BF16)

    whole = lambda arr: pl.BlockSpec(arr.shape, lambda i, nd=arr.ndim: (0,) * nd)
    return pl.pallas_call(
        body, name=name, grid=(T // tr,),
        in_specs=[pl.BlockSpec((tr, 2 * SW), lambda i: (i, zblock)), whole(lng), whole(lnb), whole(ws), whole(bexp)],
        out_specs=pl.BlockSpec((tr, SW), lambda i: (i, 0)), out_shape=jax.ShapeDtypeStruct((T, SW), BF16),
        compiler_params=_params())(proj, lng, lnb, ws, bexp)


def _sgu_bwd(proj, zblock, dyb, lng, lnb, ws, bexp, dproj, name):
    T, SW = proj.shape[0], lng.shape[1]
    G = ws.shape[0]
    tr = min(256, T)
    nch = tr // SGU_CHUNK
    nt = T // tr

    def body(z_ref, dy_ref, lng_ref, lnb_ref, ws_ref, be_ref, buf_ref, dz_ref, dlg_ref, dlb_ref, dws_ref, db_ref, du_s, dvl_s, dbacc_s):
        i = pl.program_id(0)
        zv = z_ref[...]
        lng_v = lng_ref[...]
        u, vn, rstd, vl = _sgu_ln(zv, SW, lng_v, lnb_ref[...])

        @pl.when(i == 0)
        def _():
            for s in (dlg_ref, dlb_ref, dws_ref, dbacc_s):
                s[...] = jnp.zeros_like(s)

        for g in range(G):
            cs = slice(g * SGU_GROUP, (g + 1) * SGU_GROUP)
            tri, wc = _causal(ws_ref, g)
            for n in range(nch):
                rs = slice(n * SGU_CHUNK, (n + 1) * SGU_CHUNK)
                blk = vl[rs, cs].astype(BF16)
                m = jnp.dot(wc, blk, preferred_element_type=F32) + be_ref[:, cs]
                dyv = dy_ref[rs, cs]
                du_s[rs, cs] = dyv * m
                dm = dyv * u[rs, cs]
                dvl_s[rs, cs] = _bdot(wc, dm, "tn")
                dws_ref[g] += jnp.where(tri, _bdot(dm, blk, "nt"), 0.0)
                dbacc_s[:, cs] += dm

        dvl = dvl_s[...]
        dlg_ref[...] += _colsum(dvl * vn)
        dlb_ref[...] += _colsum(dvl)
        dvn = dvl * lng_v
        dvv = rstd * (dvn - _mean(dvn) - vn * _mean(dvn * vn))
        gp = _gelu_grad(zv)
        dz_ref[:, :SW] = (du_s[...] * gp[:, :SW]).astype(BF16)
        dz_ref[:, SW:] = (dvv * gp[:, SW:]).astype(BF16)

        @pl.when(i == nt - 1)
        def _():
            lane = lax.broadcasted_iota(jnp.int32, (SGU_CHUNK, LANE), 1)
            out = jnp.zeros((SGU_CHUNK, LANE), F32)
            for g in range(G):
                col = jnp.sum(dbacc_s[:, g * SGU_GROUP:(g + 1) * SGU_GROUP], axis=1, keepdims=True)
                out = jnp.where(lane == g, col, out)
            db_ref[...] = out

    whole = lambda arr: pl.BlockSpec(arr.shape, lambda i, nd=arr.ndim: (0,) * nd)
    acc_shapes = [(1, SW), (1, SW), ws.shape, (SGU_CHUNK, LANE)]
    return pl.pallas_call(
        body, name=name, grid=(nt,),
        in_specs=[pl.BlockSpec((tr, 2 * SW), lambda i: (i, zblock)), pl.BlockSpec((tr, SW), lambda i: (i, 0)),
                  whole(lng), whole(lnb), whole(ws), whole(bexp), pl.BlockSpec(memory_space=pl.ANY)],
        out_specs=([pl.BlockSpec((tr, 2 * SW), lambda i: (i, zblock))]
                   + [pl.BlockSpec(s, lambda i, nd=len(s): (0,) * nd) for s in acc_shapes]),
        out_shape=[jax.ShapeDtypeStruct(dproj.shape, BF16)] + [jax.ShapeDtypeStruct(s, F32) for s in acc_shapes],
        scratch_shapes=[pltpu.VMEM((tr, SW), F32), pltpu.VMEM((tr, SW), F32), pltpu.VMEM((SGU_CHUNK, SW), F32)],
        input_output_aliases={6: 0}, compiler_params=_params())(proj, dyb, lng, lnb, ws, bexp, dproj)


_HBM = pl.BlockSpec(memory_space=pltpu.HBM)
_SEM = pl.BlockSpec(memory_space=pltpu.SEMAPHORE)
_DATAFLOW = pltpu.SideEffectType.DATAFLOW_SIDE_EFFECTING


def _mesh_place(chips=False):
    x, y, c = lax.axis_index("x"), lax.axis_index("y"), lax.axis_index("c")
    return x, y, c, (2 * x + y if chips else 4 * x + 2 * y + c)


def _peer(x, y, c, rel, chips=False):
    px = 1 - x if rel & 4 else x
    py = 1 - y if rel & 2 else y
    pc = 1 - c if rel & 1 else c
    return (px, py, pc), (2 * px + py if chips else 4 * px + 2 * py + pc)


ALL_PEERS = tuple(range(1, N_DEV))
SIBLING = (1,)
SAME_CORE = (2, 4, 6)
SIBLINGS_CORE = (3, 5, 7)


def _exchange_start(groups, name, rels=ALL_PEERS, chips=False):
    flat = [t for g in groups for t in g]
    sizes = [len(g) for g in groups]
    n, ng = len(flat), len(groups)
    srcs = [pltpu.with_memory_space_constraint(a, pltpu.HBM) for a, _ in flat]
    lands = [pltpu.with_memory_space_constraint(lax.empty(((N_DEV,) + a.shape) if isg else a.shape, a.dtype), pltpu.HBM)
             for a, isg in flat]

    def body(*refs):
        ins, lnd, sems, token = refs[:n], refs[n:2 * n], refs[2 * n:2 * n + 3 * ng], refs[-1]
        x, y, c, me = _mesh_place(chips)
        j0 = 0
        for gi, sz in enumerate(sizes):
            for rel in rels:
                dev, slot = _peer(x, y, c, rel, chips)
                for jj in range(sz):
                    j = j0 + jj
                    pltpu.make_async_remote_copy(
                        src_ref=ins[j] if flat[j][1] else ins[j].at[slot], dst_ref=lnd[j].at[me],
                        send_sem=sems[3 * gi].at[jj * (N_DEV - 1) + rel - 1], recv_sem=sems[3 * gi + 1].at[jj * (N_DEV - 1) + rel - 1],
                        device_id=dev, device_id_type=pl.DeviceIdType.MESH).start()
            for jj in range(sz):
                j = j0 + jj
                pltpu.make_async_copy(ins[j] if flat[j][1] else ins[j].at[me], lnd[j].at[me], sems[3 * gi + 2].at[jj]).start()
            j0 += sz
        token[...] = jnp.zeros_like(token)

    sem_shapes = [pltpu.SemaphoreType.DMA((k,)) for sz in sizes for k in (sz * (N_DEV - 1), sz * (N_DEV - 1), sz)]
    res = pl.pallas_call(
        body, name=name,
        out_shape=(*sem_shapes, *[pltpu.HBM(a.shape, a.dtype) for a in srcs], *[pltpu.HBM(a.shape, a.dtype) for a in lands],
                   jax.ShapeDtypeStruct((SUBLANE, LANE), F32)),
        in_specs=[_HBM] * (2 * n), out_specs=(*[_SEM] * (3 * ng), *[_HBM] * (2 * n), pl.BlockSpec(memory_space=pltpu.VMEM)),
        input_output_aliases={i: 3 * ng + i for i in range(2 * n)},
        compiler_params=pltpu.CompilerParams(has_side_effects=_DATAFLOW))(*srcs, *lands)
    sems, thru, token = res[:3 * ng], res[3 * ng:3 * ng + 2 * n], res[-1]
    handle, j0 = [], 0
    for gi, sz in enumerate(sizes):
        handle.append(dict(kinds=[k for _, k in groups[gi]], chips=chips, srcs=list(thru[j0:j0 + sz]), lands=list(thru[n + j0:n + j0 + sz]),
                           sems=list(sems[3 * gi:3 * gi + 3])))
        j0 += sz
    return handle, token


def _exchange_wait(group, after, name, rels=ALL_PEERS, local=True):
    kinds, sz = group["kinds"], len(group["kinds"])
    relay = group.get("relay", [])

    def body(*refs):
        ins, lnd, (ssem, rsem, lsem) = refs[:sz], refs[sz:2 * sz], refs[2 * sz:2 * sz + 3]
        x, y, c, me = _mesh_place(group["chips"])
        for rel in rels:
            dev, slot = _peer(x, y, c, rel, group["chips"])
            for jj in range(sz):
                cp = pltpu.make_async_remote_copy(
                    src_ref=ins[jj] if kinds[jj] else ins[jj].at[slot], dst_ref=lnd[jj].at[slot],
                    send_sem=ssem.at[jj * (N_DEV - 1) + rel - 1], recv_sem=rsem.at[jj * (N_DEV - 1) + rel - 1],
                    device_id=dev, device_id_type=pl.DeviceIdType.MESH)
                cp.wait_send()
                cp.wait_recv()
        if local:
            for jj in range(sz):
                pltpu.make_async_copy(ins[jj] if kinds[jj] else ins[jj].at[me], lnd[jj].at[me], lsem.at[jj]).wait()
        if relay:
            fsend, frecv = refs[2 * sz + 3:2 * sz + 5]
            dev = _peer(x, y, c, 1)[0]
            for q, (mine, theirs) in enumerate(zip(SAME_CORE, SIBLINGS_CORE)):
                for jj in range(sz):
                    cp = pltpu.make_async_remote_copy(
                        src_ref=lnd[jj].at[_peer(x, y, c, mine)[1]], dst_ref=lnd[jj].at[_peer(x, y, c, theirs)[1]],
                        send_sem=fsend.at[jj * len(SAME_CORE) + q], recv_sem=frecv.at[jj * len(SAME_CORE) + q],
                        device_id=dev, device_id_type=pl.DeviceIdType.MESH)
                    cp.wait_send()
                    cp.wait_recv()

    arrays = group["srcs"] + group["lands"]
    sems = group["sems"] + relay
    res = pl.pallas_call(
        body, name=name, out_shape=[pltpu.HBM(a.shape, a.dtype) for a in arrays],
        in_specs=[_HBM] * (2 * sz) + [_SEM] * len(sems) + [pl.BlockSpec(memory_space=pl.ANY)], out_specs=[_HBM] * (2 * sz),
        input_output_aliases={i: i for i in range(2 * sz)},
        compiler_params=pltpu.CompilerParams(has_side_effects=_DATAFLOW))(*arrays, *sems, after)
    return dict(group, srcs=list(res[:sz]), lands=list(res[sz:]), relay=[])


def _relay_start(group, name):
    sz = len(group["kinds"])
    nq = len(SAME_CORE)

    def body(*refs):
        lnd, fsend, frecv, token = refs[:sz], refs[sz], refs[sz + 1], refs[-1]
        x, y, c, _ = _mesh_place()
        dev = _peer(x, y, c, 1)[0]
        for q, rel in enumerate(SAME_CORE):
            slot = _peer(x, y, c, rel)[1]
            for jj in range(sz):
                pltpu.make_async_remote_copy(
                    src_ref=lnd[jj].at[slot], dst_ref=lnd[jj].at[slot], send_sem=fsend.at[jj * nq + q], recv_sem=frecv.at[jj * nq + q],
                    device_id=dev, device_id_type=pl.DeviceIdType.MESH).start()
        token[...] = jnp.zeros_like(token)

    lands = group["lands"]
    res = pl.pallas_call(
        body, name=name,
        out_shape=(pltpu.SemaphoreType.DMA((sz * nq,)), pltpu.SemaphoreType.DMA((sz * nq,)), *[pltpu.HBM(a.shape, a.dtype) for a in lands],
                   jax.ShapeDtypeStruct((SUBLANE, LANE), F32)),
        in_specs=[_HBM] * sz, out_specs=(_SEM, _SEM, *[_HBM] * sz, pl.BlockSpec(memory_space=pltpu.VMEM)),
        input_output_aliases={i: 2 + i for i in range(sz)},
        compiler_params=pltpu.CompilerParams(has_side_effects=_DATAFLOW))(*lands)
    return dict(group, lands=list(res[2:2 + sz]), relay=[res[0], res[1]]), res[-1]


def _sibling_swap(arrays, handle, after, name):
    start = handle is None
    n = len(arrays) if start else len(handle["srcs"])
    chips = N_DEV // 2
    if start:
        srcs = [pltpu.with_memory_space_constraint(a.reshape(chips, 2, *a.shape[1:]), pltpu.HBM) for a in arrays]
        lands = [pltpu.with_memory_space_constraint(lax.empty((chips,) + a.shape[1:], a.dtype), pltpu.HBM) for a in arrays]
    else:
        srcs, lands = handle["srcs"], handle["lands"]

    def body(*refs):
        ins, lnd, ssem, rsem = refs[:n], refs[n:2 * n], refs[2 * n], refs[2 * n + 1]
        x, y, c, _ = _mesh_place()
        dev = _peer(x, y, c, 1)[0]
        for q in range(chips):
            for j in range(n):
                cp = pltpu.make_async_remote_copy(
                    src_ref=ins[j].at[q, 1 - c], dst_ref=lnd[j].at[q], send_sem=ssem.at[j * chips + q], recv_sem=rsem.at[j * chips + q],
                    device_id=dev, device_id_type=pl.DeviceIdType.MESH)
                if start:
                    cp.start()
                else:
                    cp.wait_send()
                    cp.wait_recv()
        if start:
            refs[-1][...] = jnp.zeros_like(refs[-1])

    thru = [pltpu.HBM(a.shape, a.dtype) for a in srcs + lands]
    effect = pltpu.CompilerParams(has_side_effects=_DATAFLOW)
    if start:
        res = pl.pallas_call(
            body, name=name, out_shape=(pltpu.SemaphoreType.DMA((n * chips,)), pltpu.SemaphoreType.DMA((n * chips,)), *thru,
                                        jax.ShapeDtypeStruct((SUBLANE, LANE), F32)),
            in_specs=[_HBM] * (2 * n), out_specs=(_SEM, _SEM, *[_HBM] * (2 * n), pl.BlockSpec(memory_space=pltpu.VMEM)),
            input_output_aliases={i: 2 + i for i in range(2 * n)}, compiler_params=effect)(*srcs, *lands)
        return dict(srcs=list(res[2:2 + n]), lands=list(res[2 + n:2 + 2 * n]), sems=[res[0], res[1]]), res[-1]
    res = pl.pallas_call(
        body, name=name, out_shape=thru, in_specs=[_HBM] * (2 * n) + [_SEM, _SEM, pl.BlockSpec(memory_space=pl.ANY)],
        out_specs=[_HBM] * (2 * n), input_output_aliases={i: i for i in range(2 * n)}, compiler_params=effect)(
            *srcs, *lands, *handle["sems"], after)
    return dict(handle, srcs=list(res[:n]), lands=list(res[n:]))


def _pair_add(mine, theirs, core, name):
    chips, _, rows, w = mine.shape
    tm = _pick(rows, (256, 128, 64, 32, 16))

    def body(core_ref, a_ref, b_ref, o_ref):
        o_ref[...] = (a_ref[...].astype(F32) + b_ref[...].astype(F32)).astype(o_ref.dtype)

    return pl.pallas_call(
        body, name=name, out_shape=jax.ShapeDtypeStruct(theirs.shape, theirs.dtype),
        grid_spec=pltpu.PrefetchScalarGridSpec(
            num_scalar_prefetch=1, grid=(chips, rows // tm),
            in_specs=[pl.BlockSpec((None, None, tm, w), lambda q, i, core_ref: (q, core_ref[0], i, 0)),
                      pl.BlockSpec((None, tm, w), lambda q, i, core_ref: (q, i, 0))],
            out_specs=pl.BlockSpec((None, tm, w), lambda q, i, core_ref: (q, i, 0))),
        compiler_params=_params())(core, mine, theirs)


def _adamw(w, m, v, gparts, name, after=None):
    R, C = w.shape
    tm = _pick(R, (256, 128, 64, 32, 16, 8))
    order = [] if after is None else [after]

    def body(w_ref, m_ref, v_ref, g_ref, *rest):
        go, do, mo, vo = rest[len(order):]
        g = g_ref[0].astype(F32)
        for j in range(1, gparts.shape[0]):
            g = g + g_ref[j].astype(F32)
        mn = ADAM_B1 * m_ref[...] + (1.0 - ADAM_B1) * g
        vn = ADAM_B2 * v_ref[...] + (1.0 - ADAM_B2) * (g * g)
        m_hat = mn / (1.0 - ADAM_B1 ** ADAM_STEP)
        v_hat = vn / (1.0 - ADAM_B2 ** ADAM_STEP)
        go[...] = g
        do[...] = -ADAM_LR * (m_hat / (jnp.sqrt(v_hat) + ADAM_EPS) + ADAM_WD * w_ref[...])
        mo[...] = mn
        vo[...] = vn

    row = pl.BlockSpec((tm, C), lambda i: (i, 0))
    return pl.pallas_call(
        body, name=name, grid=(R // tm,),
        in_specs=[row, row, row, pl.BlockSpec((gparts.shape[0], tm, C), lambda i: (0, i, 0))] + [pl.BlockSpec(memory_space=pl.ANY)] * len(order),
        out_specs=[row] * 4, out_shape=[jax.ShapeDtypeStruct((R, C), F32)] * 4, compiler_params=_params())(w, m, v, gparts, *order)


def _pack(arrays):
    parts = []
    for a in arrays:
        f = a.reshape(1, -1)
        pad = _ceil_to(f.shape[1], SUBLANE * LANE) - f.shape[1]
        f = jnp.concatenate([f, jnp.zeros((1, pad), f.dtype)], axis=1) if pad else f
        parts.append(f.reshape(-1, LANE))
    rows = sum(p.shape[0] for p in parts)
    pad = _ceil_to(rows, 64) - rows
    return jnp.concatenate(parts + ([jnp.zeros((pad, LANE), parts[0].dtype)] if pad else []), axis=0)


def _unpack(buf, shapes):
    out, row = [], 0
    for s in shapes:
        size = 1
        for d in s:
            size *= d
        rows = _ceil_to(size, SUBLANE * LANE) // LANE
        out.append(buf[row:row + rows].reshape(1, -1)[:, :size].reshape(s))
        row += rows
    return out


def kernel(x, norm_mix_g, w_in, shift_mu, w0, w_lora_up, a0, a_lora_up, g_lora_up, k_k, k_a, r_k, lnx_g, lnx_b, w_proj_rwkv, sgu_ln_g, sgu_ln_b, sgu_w, sgu_b, w_proj_sgu, w_out, norm_ffn_g, w_ffn_gate, w_ffn_up, w_ffn_down, norm_final_g, loss_target, m_norm_mix_g, m_w_in, m_shift_mu, m_w0, m_w_lora_up, m_a0, m_a_lora_up, m_g_lora_up, m_k_k, m_k_a, m_r_k, m_lnx_g, m_lnx_b, m_w_proj_rwkv, m_sgu_ln_g, m_sgu_ln_b, m_sgu_w, m_sgu_b, m_w_proj_sgu, m_w_out, m_norm_ffn_g, m_w_ffn_gate, m_w_ffn_up, m_w_ffn_down, m_norm_final_g, v_norm_mix_g, v_w_in, v_shift_mu, v_w0, v_w_lora_up, v_a0, v_a_lora_up, v_g_lora_up, v_k_k, v_k_a, v_r_k, v_lnx_g, v_lnx_b, v_w_proj_rwkv, v_sgu_ln_g, v_sgu_ln_b, v_sgu_w, v_sgu_b, v_w_proj_sgu, v_w_out, v_norm_ffn_g, v_w_ffn_gate, v_w_ffn_up, v_w_ffn_down, v_norm_final_g):
    weights = dict(norm_mix_g=norm_mix_g, w_in=w_in, shift_mu=shift_mu, w0=w0, w_lora_up=w_lora_up, a0=a0, a_lora_up=a_lora_up,
                   g_lora_up=g_lora_up, k_k=k_k, k_a=k_a, r_k=r_k, lnx_g=lnx_g, lnx_b=lnx_b, w_proj_rwkv=w_proj_rwkv,
                   sgu_ln_g=sgu_ln_g, sgu_ln_b=sgu_ln_b, sgu_w=sgu_w, sgu_b=sgu_b, w_proj_sgu=w_proj_sgu, w_out=w_out,
                   norm_ffn_g=norm_ffn_g, w_ffn_gate=w_ffn_gate, w_ffn_up=w_ffn_up, w_ffn_down=w_ffn_down, norm_final_g=norm_final_g)
    m_in = dict(norm_mix_g=m_norm_mix_g, w_in=m_w_in, shift_mu=m_shift_mu, w0=m_w0, w_lora_up=m_w_lora_up, a0=m_a0,
                a_lora_up=m_a_lora_up, g_lora_up=m_g_lora_up, k_k=m_k_k, k_a=m_k_a, r_k=m_r_k, lnx_g=m_lnx_g, lnx_b=m_lnx_b,
                w_proj_rwkv=m_w_proj_rwkv, sgu_ln_g=m_sgu_ln_g, sgu_ln_b=m_sgu_ln_b, sgu_w=m_sgu_w, sgu_b=m_sgu_b,
                w_proj_sgu=m_w_proj_sgu, w_out=m_w_out, norm_ffn_g=m_norm_ffn_g, w_ffn_gate=m_w_ffn_gate, w_ffn_up=m_w_ffn_up,
                w_ffn_down=m_w_ffn_down, norm_final_g=m_norm_final_g)
    v_in = dict(norm_mix_g=v_norm_mix_g, w_in=v_w_in, shift_mu=v_shift_mu, w0=v_w0, w_lora_up=v_w_lora_up, a0=v_a0,
                a_lora_up=v_a_lora_up, g_lora_up=v_g_lora_up, k_k=v_k_k, k_a=v_k_a, r_k=v_r_k, lnx_g=v_lnx_g, lnx_b=v_lnx_b,
                w_proj_rwkv=v_w_proj_rwkv, sgu_ln_g=v_sgu_ln_g, sgu_ln_b=v_sgu_ln_b, sgu_w=v_sgu_w, sgu_b=v_sgu_b,
                w_proj_sgu=v_w_proj_sgu, w_out=v_w_out, norm_ffn_g=v_norm_ffn_g, w_ffn_gate=v_w_ffn_gate, w_ffn_up=v_w_ffn_up,
                w_ffn_down=v_w_ffn_down, norm_final_g=v_norm_final_g)
    names = list(weights)
    col_sharded = ("w_in", "w_lora_up", "a_lora_up", "g_lora_up", "w_proj_rwkv", "w_proj_sgu", "w_ffn_gate", "w_ffn_up")
    row_sharded = ("w_out", "w_ffn_down")
    sharded = [n for n in names if n in col_sharded or n in row_sharded]
    small = [n for n in names if n not in sharded]

    xs, tgt = x[0], loss_target[0]
    T, D = xs.shape
    RW = w0.shape[1]
    H = RW // HEAD
    SW = sgu_ln_g.shape[1]
    G = sgu_w.shape[1]
    assert 2 * SW == D, "the projection layout takes the SGU part to be as wide as a gate"
    lay = _rwkv_layout(RW, w_lora_up.shape[1], a_lora_up.shape[1], g_lora_up.shape[1], D)
    _, pw, _, rcp = lay
    icp = rcp + 3 * D
    b_ga, b_gb, b_z = rcp // D, rcp // D + 1, rcp // D + 2

    half = D // 2
    sources = {n: weights[n][0].astype(BF16) for n in sharded if n != "w_in"}
    sources["w_in_top"], sources["w_in_bot"] = w_in[0, :half].astype(BF16), w_in[0, half:].astype(BF16)
    gather_groups = dict(in_top=["w_in_top", "w_lora_up", "a_lora_up", "g_lora_up"], in_bot=["w_in_bot"],
                         proj=["w_proj_rwkv", "w_proj_sgu", "w_out"], ffn_gate=["w_ffn_gate"], ffn_up=["w_ffn_up"],
                         ffn_down=["w_ffn_down"])
    handles, gather_token = _exchange_start([[(sources[n], True) for n in grp] for grp in gather_groups.values()],
                                            "gather_start", rels=SIBLING + SAME_CORE)
    gather = dict(zip(gather_groups, handles))
    full = {}
    relay_tokens = {}
    joined = lambda g: g.transpose(1, 0, 2).reshape(g.shape[1], -1)

    def relay_weights(key, after):
        arrived = _exchange_wait(gather[key], after, "gather_wait_ici_" + key, rels=SAME_CORE, local=False)
        gather[key], relay_tokens[key] = _relay_start(arrived, "gather_relay_" + key)

    def take_weights(key, after):
        done = _exchange_wait(gather[key], after, "gather_wait_d2d_" + key, rels=SIBLING)
        for n, g in zip(gather_groups[key], done["lands"]):
            full[n] = g.reshape(-1, g.shape[2]) if n in row_sharded else g

    packed = [_pack([d[n] for n in small] + [gather_token]) for d in (weights, m_in, v_in)]
    n1, n1_top, n1_bot = _rms_fwd(xs, norm_mix_g, "rms_mix", deps=[gather_token, *packed], halves=True)
    relay_weights("in_top", n1)
    take_weights("in_top", relay_tokens["in_top"])
    W_top = _w_in_to_proj(full["w_in_top"], lay, D, "w_in_layout_top")
    lora = [_pad_rows(joined(full[n]), rows) for n, rows in zip(("w_lora_up", "a_lora_up", "g_lora_up"), pw[3:])]
    mu_p = _pad_rwkv_cols(shift_mu, lay)
    rsmall = [w0, a0, k_k, k_a]
    hp = [lnx_g.reshape(H, 1, HEAD), lnx_b.reshape(H, 1, HEAD), r_k.reshape(H, 1, HEAD)]
    ws = sgu_w[0]
    bexp = jnp.repeat(sgu_b[0].T, SGU_GROUP, axis=1)
    gf = norm_final_g.reshape(1, D)

    proj = _matmul(n1_top, W_top, mode="nn", out_dtype=F32, name="proj_in_top")
    relay_weights("in_bot", proj)
    take_weights("in_bot", relay_tokens["in_bot"])
    W_bot = _w_in_to_proj(full["w_in_bot"], lay, D, "w_in_layout_bot")
    proj = _matmul(n1_bot, W_bot, mode="nn", out_dtype=F32, name="proj_in_bot", add=proj)
    ga, gb = (proj, D, b_ga), (proj, D, b_gb)
    r_h, lw_h, k2_h, v_h, aa_h, bb_h, g_h = _rwkv_pre(proj, mu_p, rsmall, lora, lay, "rwkv_pre")
    wkv_in = [r_h, lw_h, k2_h, v_h, aa_h, bb_h]
    y_h, states = _wkv_fwd(*wkv_in, "wkv_fwd")
    relay_weights("proj", y_h)
    relay_weights("ffn_gate", relay_tokens["proj"])
    ya = _head_post(y_h, r_h, k2_h, v_h, g_h, hp, "head_post", deps=[relay_tokens["ffn_gate"]])
    relay_weights("ffn_up", ya)
    yb = _sgu_fwd(proj, b_z, sgu_ln_g, sgu_ln_b, ws, bexp, "sgu_fwd")
    take_weights("proj", ya)
    pa = _matmul(ya, full["w_proj_rwkv"], mode="nn", out_dtype=F32, name="proj_a", deps=[relay_tokens["ffn_up"]])

    def merge_fn(pb_v, pa_v, ga_v, gb_v):
        return pb_v, _sigmoid(ga_v) * pa_v + _sigmoid(gb_v) * pb_v
    pb, merged = _matmul(yb, full["w_proj_sgu"], mode="nn", name="proj_b_merge",
                         epi=(merge_fn, [pa, (proj, b_ga * D), (proj, b_gb * D)], [F32, BF16]))
    h1 = _matmul(merged, full["w_out"], mode="nn", out_dtype=F32, name="out_proj", add=xs)
    n2 = _rms_fwd(h1, norm_ffn_g, "rms_ffn")
    relay_weights("ffn_down", n2)
    take_weights("ffn_gate", n2)
    take_weights("ffn_up", n2)

    def act_fn(gt_v, up_v):
        return gt_v, up_v, gt_v * _sigmoid(gt_v) * up_v
    gt, up, act = _matmul(n2, full["w_ffn_gate"], b2=full["w_ffn_up"], mode="nn", name="ffn_gate_up_act", out_blocks=N_DEV,
                          epi=(act_fn, [], [BF16, BF16, BF16]), deps=[relay_tokens["ffn_down"]])
    take_weights("ffn_down", act)
    h2 = _matmul(act, full["w_ffn_down"], mode="nn", out_dtype=F32, name="ffn_down", add=h1)

    def final_fn(rv, pv):
        (h_v, t_v), (g_v,) = rv, pv
        r = lax.rsqrt(_mean(h_v * h_v) + RMS_EPS)
        yn = h_v * r
        e = yn * g_v - t_v
        loss = 0.5 * jnp.sum(_mean(e * e))
        dout = e * (1.0 / D)
        dyg = dout * g_v
        dh = r * (dyg - yn * _mean(dyg * yn))
        return [dh, dh], [jnp.full((1, LANE), loss, F32), _colsum(dout * yn)]
    dh2, dh2_bf, loss_part, d_gf = _rowwise(final_fn, [h2, tgt], [gf], [(D, F32), (D, BF16)], [(1, LANE), (1, D)], name="final_loss")

    grads = {}

    def start_scatter(group, name, extra=()):
        blocks = [(grads[n].reshape(N_DEV, -1, grads[n].shape[1]) if n in row_sharded else grads[n], False) for n in group]
        (handle,), token = _exchange_start([blocks + list(extra)], name)
        return handle, token

    def dact_fn(d_v, gt_v, up_v):
        gt_v, up_v = gt_v.astype(F32), up_v.astype(F32)
        s = _sigmoid(gt_v)
        return d_v * up_v * (s * (1.0 + gt_v * (1.0 - s))), d_v * gt_v * s
    dgt, dup = _matmul(dh2_bf, full["w_ffn_down"], mode="nt", name="d_ffn_act", out_blocks=N_DEV,
                       epi=(dact_fn, [gt, up], [BF16, BF16]))
    scatter_groups = dict(ffn_down=["w_ffn_down"], ffn_gate=["w_ffn_gate"], ffn_up=["w_ffn_up"],
                          mid=["w_out", "w_proj_rwkv", "w_proj_sgu"], last=["w_in", "w_lora_up", "a_lora_up", "g_lora_up"])
    scatters = {}
    grads["w_ffn_down"] = _matmul(act, dh2_bf, mode="tn", out_dtype=BF16, name="dw_ffn_down")
    scatters["ffn_down"], token = start_scatter(scatter_groups["ffn_down"], "scatter_start_ffn_down")
    dn2 = _matmul(dgt, full["w_ffn_gate"], mode="nt", out_dtype=F32, name="dn2_gate", deps=[token])
    grads["w_ffn_gate"] = _matmul(n2, dgt, mode="tn", out_dtype=BF16, name="dw_ffn_gate", out_blocks=N_DEV)
    scatters["ffn_gate"], token = start_scatter(scatter_groups["ffn_gate"], "scatter_start_ffn_gate")
    grads["w_ffn_up"] = _matmul(n2, dup, mode="tn", out_dtype=BF16, name="dw_ffn_up", out_blocks=N_DEV, deps=[token])
    scatters["ffn_up"], token = start_scatter(scatter_groups["ffn_up"], "scatter_start_ffn_up")
    dn2 = _matmul(dup, full["w_ffn_up"], mode="nt", out_dtype=F32, name="dn2_up", add=dn2, deps=[token])
    dh1, dh1_bf, d_g2 = _rms_bwd(dn2, h1, dh2, norm_ffn_g, "rms_ffn_bwd")
    dmerged = _matmul(dh1_bf, full["w_out"], mode="nt", out_dtype=F32, name="d_merged")
    grads["w_out"] = _matmul(merged, dh1_bf, mode="tn", out_dtype=BF16, name="dw_out")

    def dmerge_fn(rv, pv):
        d_v, ga_v, gb_v, pa_v, pb_v = rv
        sa, sb = _sigmoid(ga_v), _sigmoid(gb_v)
        dgates = jnp.concatenate([d_v * pa_v * sa * (1.0 - sa), d_v * pb_v * sb * (1.0 - sb)], axis=1)
        return [dgates, d_v * sa, d_v * sb], []
    dproj, dpa, dpb = _rowwise(dmerge_fn, [dmerged, ga, gb, pa, pb], [],
                               [(2 * D, BF16, icp, b_ga // 2, None), (D, BF16), (D, BF16)], [], name="d_merge")
    dya = _matmul(dpa, full["w_proj_rwkv"], mode="nt", out_dtype=F32, name="d_ya")
    dyb = _matmul(dpb, full["w_proj_sgu"], mode="nt", out_dtype=F32, name="d_yb")
    grads["w_proj_rwkv"] = _matmul(ya, dpa, mode="tn", out_dtype=BF16, name="dw_proj_a", out_blocks=N_DEV)
    grads["w_proj_sgu"] = _matmul(yb, dpb, mode="tn", out_dtype=BF16, name="dw_proj_b", out_blocks=N_DEV)
    scatters["mid"], token_mid = start_scatter(scatter_groups["mid"], "scatter_start_mid")
    dproj, d_lng, d_lnb, d_ws, d_bs = _sgu_bwd(proj, b_z, dyb, sgu_ln_g, sgu_ln_b, ws, bexp, dproj, "sgu_bwd")

    dy_h, dr1, dk1, dv1, dg_h, d_lnxg, d_lnxb, d_rk = _head_post_bwd(dya, y_h, r_h, k2_h, v_h, g_h, hp, "head_post_bwd",
                                                                     deps=[token_mid])
    dr2, dlw_h, dk2b, dv2, daa, dbb = _wkv_bwd(*wkv_in, states, dy_h, "wkv_bwd")
    dps, d_mu, d_w0, d_a0, d_kk, d_ka, d_wlw, d_wla, d_wlg = _rwkv_pre_bwd(
        proj, mu_p, rsmall, lora, [dr1, dr2, dk1, dk2b, dv1, dv2, dlw_h, daa, dbb, dg_h], lay, "rwkv_pre_bwd")
    dproj = _shift_bwd(dps, mu_p, dproj, "shift_bwd")
    split = lambda g: g.reshape(g.shape[0], N_DEV, -1).transpose(1, 0, 2)
    grads["w_in"] = _dw_in_from_proj(_matmul(n1, dproj, mode="tn", out_dtype=BF16, name="dw_in"), lay, D, w_in.shape[2], "dw_in_layout")
    grads["w_lora_up"] = split(d_wlw[:w_lora_up.shape[1]].astype(BF16))
    grads["a_lora_up"] = split(d_wla[:a_lora_up.shape[1]].astype(BF16))
    grads["g_lora_up"] = split(d_wlg[:g_lora_up.shape[1]].astype(BF16))
    out = {}

    def update_group(key, after):
        handle = scatters[key]
        parts = _exchange_wait(handle, after, "scatter_wait_" + key, rels=SAME_CORE if handle["chips"] else ALL_PEERS)["lands"]
        for n, part in zip(scatter_groups[key], parts):
            res = _adamw(weights[n][0], m_in[n][0], v_in[n][0], part, "adamw_" + n, after=after)
            out[n] = [t.reshape(weights[n].shape) for t in res]
            after = res[0]
        return after

    swap, token_swap = _sibling_swap([grads[n] for n in scatter_groups["last"]], None, None, "scatter_last_swap_start")
    after = update_group("ffn_gate", update_group("ffn_down", token_swap))
    swap = _sibling_swap(None, swap, after, "scatter_last_swap_wait")
    core = lax.axis_index("c").astype(jnp.int32).reshape(1)
    chip_sums = [_pair_add(mine, theirs, core, "scatter_last_add_" + n)
                 for n, mine, theirs in zip(scatter_groups["last"], swap["srcs"], swap["lands"])]
    (scatters["last"],), token_in = _exchange_start([[(s, False) for s in chip_sums]], "scatter_start_last", rels=SAME_CORE, chips=True)
    dn1 = [_matmul(dproj, W_top, mode="nt", out_dtype=F32, name="dn1_top", deps=[token_in]),
           _matmul(dproj, W_bot, mode="nt", out_dtype=F32, name="dn1_bot")]
    dx, _, d_g1 = _rms_bwd(dn1, xs, dh1, norm_mix_g, "rms_mix_bwd")
    small_grads = dict(norm_mix_g=d_g1, shift_mu=_unpad_rwkv_cols(d_mu, lay), w0=d_w0, a0=d_a0, k_k=d_kk, k_a=d_ka, r_k=d_rk,
                       lnx_g=d_lnxg, lnx_b=d_lnxb, sgu_ln_g=d_lng, sgu_ln_b=d_lnb, sgu_w=d_ws, sgu_b=d_bs[:, :G].T,
                       norm_ffn_g=d_g2, norm_final_g=d_gf)
    (gather_small,), after = _exchange_start([[(_pack([small_grads[n] for n in small] + [jnp.zeros_like(gather_token)]), True)]],
                                             "gather_small_start")
    for key in ("ffn_up", "mid", "last"):
        after = update_group(key, after)
    small_parts = _exchange_wait(gather_small, after, "gather_small_wait")["lands"][0]
    res = _adamw(*packed, small_parts, "adamw_small")
    unpacked = [_unpack(t, [weights[n].shape for n in small]) for t in res]
    for i, n in enumerate(small):
        out[n] = [u[i] for u in unpacked]

    loss = lax.psum(loss_part[0, 0], ("x", "y", "c"))
    return (loss, dx[None], *[out[n][0] for n in names], *[out[n][1] for n in names],
            *[out[n][2] for n in names], *[out[n][3] for n in names])
```

```python
import jax
import jax.numpy as jnp
from jax import lax
from jax.experimental import pallas as pl
from jax.experimental.pallas import tpu as pltpu

F32 = jnp.float32
BF16 = jnp.bfloat16

N_DEV = 8
LANE = 128
SUBLANE = 8
HEAD = 64
SGU_CHUNK = 128
SGU_GROUP = 128
WKV_CHUNK = 64
RMS_EPS = 1e-6
LN_EPS = 1e-5
LNX_EPS = 64e-5
ADAM_LR, ADAM_B1, ADAM_B2, ADAM_EPS, ADAM_WD, ADAM_STEP = 0.001, 0.9, 0.999, 1e-08, 0.01, 10
VMEM_LIMIT_BYTES = 48 * 1024 * 1024
_SQRT_HALF = 0.7071067811865476
_INV_SQRT_2PI = 0.3989422804014327


def _pick(n, cands):
    for c in cands:
        if n % c == 0:
            return c
    return n


def _ceil_to(n, m):
    return -(-n // m) * m


def _params():
    return pltpu.CompilerParams(vmem_limit_bytes=VMEM_LIMIT_BYTES)


def _tile(n, cap):
    best = 0
    for d in range(LANE, min(n, cap) + 1, LANE):
        if n % d == 0:
            best = d
    return best or n


def _matmul_tiles(M, N, K, a_bytes, b_bytes, o_bytes, has_add, forced):
    tm = forced.get("m") or _tile(M, 1024)
    tn = forced.get("n") or _tile(N, 1024)
    tk = forced.get("k") or _tile(K, 2048)

    def vmem(tm, tn, tk):
        acc = tm * tn * 4 if tk < K else 0
        return 2 * (tm * tk * a_bytes + tk * tn * b_bytes + tm * tn * (o_bytes + (4 if has_add else 0))) + acc

    while vmem(tm, tn, tk) > (VMEM_LIMIT_BYTES * 3) // 4:
        if "k" not in forced and tk > 512 and _tile(K, tk // 2) < tk:
            tk = _tile(K, tk // 2)
        elif "m" not in forced and _tile(M, tm // 2) < tm:
            tm = _tile(M, tm // 2)
        else:
            break
    return tm, tn, tk


def _matmul(a, b, *, mode, out_dtype=F32, name, add=None, deps=(), out_blocks=0, epi=None, b2=None):
    def view(x):
        return (x.shape[1], x.shape[0] * x.shape[2], x.shape[2]) if x.ndim == 3 else (x.shape[0], x.shape[1], 0)

    (ar, ac, aw), (br, bc, bw) = view(a), view(b)
    a_col, b_col = {"nn": ("k", "n"), "nt": ("k", "k"), "tn": ("m", "n")}[mode]
    if mode == "nn":
        M, K, K2, N = ar, ac, br, bc
    elif mode == "nt":
        M, K, N, K2 = ar, ac, br, bc
    else:
        K, M, K2, N = ar, ac, br, bc
    assert K == K2, (a.shape, b.shape, mode)
    forced = {}
    for dim, w in ((a_col, aw), (b_col, bw), ("n", N // out_blocks if out_blocks else 0)):
        if w:
            assert forced.get(dim, w) == w
            forced[dim] = w
    has_add = add is not None
    tile_bytes = (sum(jnp.dtype(d).itemsize for d in epi[2]) + sum((e[0] if isinstance(e, tuple) else e).dtype.itemsize for e in epi[1])
                  if epi is not None else jnp.dtype(out_dtype).itemsize)
    tm, tn, tk = _matmul_tiles(M, N, K, a.dtype.itemsize, b.dtype.itemsize, tile_bytes, has_add, forced)
    kb = 1
    if "k" in forced and mode != "tn":
        lanes_ok = all(w or tk % LANE == 0 for w in (aw, bw if mode == "nt" else 1))
        kb = next(c for c in (4, 2, 1) if (K // tk) % c == 0 and (c == 1 or (lanes_ok and c * tk <= 1536)))
    nk = K // (tk * kb)
    dn = {"nn": (((1,), (0,)), ((), ())), "nt": (((1,), (1,)), ((), ())), "tn": (((0,), (0,)), ((), ()))}[mode]
    pick = {"m": lambda i, j, k: i, "n": lambda i, j, k: j, "k": lambda i, j, k: k}
    size = {"m": tm, "n": tn, "k": tk}

    def spec(blocked, row_dim, col_dim):
        rf, cf = pick[row_dim], pick[col_dim]
        reps = {d: (kb if d == "k" else 1) for d in (row_dim, col_dim)}
        if blocked:
            lead = kb if col_dim == "k" and kb > 1 else None
            return pl.BlockSpec((lead, size[row_dim], size[col_dim]), lambda i, j, k: (cf(i, j, k), rf(i, j, k), 0))
        return pl.BlockSpec((size[row_dim] * reps[row_dim], size[col_dim] * reps[col_dim]), lambda i, j, k: (rf(i, j, k), cf(i, j, k)))

    def k_part(ref, blocked, k_on_rows, j):
        if kb == 1:
            return ref[...]
        if blocked:
            return ref[j]
        return ref[j * tk:(j + 1) * tk, :] if k_on_rows else ref[:, j * tk:(j + 1) * tk]

    a_spec = spec(aw, "k" if mode == "tn" else "m", a_col)
    b_spec = spec(bw, "n" if mode == "nt" else "k", b_col)
    o_spec = spec(out_blocks, "m", "n")
    epi_fn, epi_ins, epi_dtypes = epi if epi is not None else (None, [], [out_dtype])
    epi_ins = [e if isinstance(e, tuple) else (e, None) for e in epi_ins]
    n_epi = len(epi_ins)
    twin = b2 is not None
    assert not twin or (nk == 1 and kb == 1 and epi is not None and b2.shape == b.shape)
    n_in = 2 + twin + has_add + n_epi + len(deps)
    n_out = len(epi_dtypes)

    def body(*refs):
        a_ref, b_ref = refs[0], refs[1]
        add_ref = refs[2 + twin] if has_add else None
        epi_refs = refs[2 + twin + has_add:2 + twin + has_add + n_epi]
        o_refs = refs[n_in:n_in + n_out]
        part = None
        for q in range(kb):
            a_q = k_part(a_ref, aw and a_col == "k", False, q)
            b_q = k_part(b_ref, bw and b_col == "k", mode == "nn", q)
            prod = lax.dot_general(a_q.astype(BF16), b_q.astype(BF16), dn, preferred_element_type=F32)
            part = prod if part is None else part + prod
        second = [lax.dot_general(a_ref[...].astype(BF16), refs[2][...].astype(BF16), dn, preferred_element_type=F32)] if twin else []

        def finish(res):
            outs = epi_fn(res, *second, *[e[...] for e in epi_refs]) if epi_fn is not None else (res,)
            for o_ref, val in zip(o_refs, outs):
                o_ref[...] = val.astype(o_ref.dtype)

        if nk == 1:
            finish(part + add_ref[...] if has_add else part)
            return
        acc_ref = refs[-1]
        kk = pl.program_id(2)

        @pl.when(kk == 0)
        def _():
            acc_ref[...] = part + add_ref[...] if has_add else part

        @pl.when(kk > 0)
        def _():
            acc_ref[...] += part

        @pl.when(kk == nk - 1)
        def _():
            finish(acc_ref[...])

    def epi_spec(arr, off):
        if off is None:
            return o_spec
        assert off % tn == 0
        return pl.BlockSpec((tm, tn), lambda i, j, k: (i, j + off // tn))

    ins = [a, b] + ([b2] if twin else []) + ([add] if has_add else []) + [arr for arr, _ in epi_ins] + list(deps)
    in_specs = ([a_spec, b_spec] + ([b_spec] if twin else []) + ([o_spec] if has_add else []) + [epi_spec(arr, off) for arr, off in epi_ins]
                + [pl.BlockSpec(d.shape, lambda i, j, k, nd=d.ndim: (0,) * nd) for d in deps])
    o_shape = (out_blocks, M, tn) if out_blocks else (M, N)
    res = pl.pallas_call(
        body, name=name, grid=(M // tm, N // tn, nk), in_specs=in_specs, out_specs=[o_spec] * n_out,
        out_shape=[jax.ShapeDtypeStruct(o_shape, dt) for dt in epi_dtypes],
        scratch_shapes=[pltpu.VMEM((tm, tn), F32)] if nk > 1 else [],
        compiler_params=_params())(*ins)
    return res[0] if epi is None else list(res)


def _rowwise(fn, rows, pars, row_outs, acc_outs, *, name, tm=256, deps=()):
    rows = [r if isinstance(r, tuple) else (r, r.shape[1], 0) for r in rows]
    row_outs = [o if len(o) == 5 else (o[0], o[1], o[0], 0, None) for o in row_outs]
    aliased = [(k, o[4]) for k, o in enumerate(row_outs) if o[4] is not None]
    R = rows[0][0].shape[0]
    if max(w for _, w, _ in rows) > 4096:
        tm = tm // 2
    tm = min(tm, R)
    assert R % tm == 0
    nr, npar = len(rows), len(pars)
    nro = len(row_outs)
    n_in = nr + npar + len(deps) + len(aliased)

    def body(*refs):
        rv = [r[...] for r in refs[:nr]]
        pv = [p[...] for p in refs[nr:nr + npar]]
        outs = refs[n_in:]
        ro, ao = fn(rv, pv)
        first = pl.program_id(0) == 0
        for o_ref, val in zip(outs[:nro], ro):
            o_ref[...] = val.astype(o_ref.dtype)

        @pl.when(first)
        def _():
            for o_ref, val in zip(outs[nro:], ao):
                o_ref[...] = val

        @pl.when(jnp.logical_not(first))
        def _():
            for o_ref, val in zip(outs[nro:], ao):
                o_ref[...] += val

    in_specs = ([pl.BlockSpec((tm, w), lambda i, cb=cb: (i, cb)) for _, w, cb in rows]
                + [pl.BlockSpec(p.shape, lambda i, nd=p.ndim: (0,) * nd) for p in list(pars) + list(deps)]
                + [pl.BlockSpec(memory_space=pl.ANY)] * len(aliased))
    out_shape = ([jax.ShapeDtypeStruct((R, full), dt) for _, dt, full, _, _ in row_outs]
                 + [jax.ShapeDtypeStruct(s, F32) for s in acc_outs])
    out_specs = ([pl.BlockSpec((tm, f), lambda i, cb=cb: (i, cb)) for f, _, _, cb, _ in row_outs]
                 + [pl.BlockSpec(s, lambda i, nd=len(s): (0,) * nd) for s in acc_outs])
    res = pl.pallas_call(body, name=name, grid=(R // tm,), in_specs=in_specs, out_specs=out_specs, out_shape=out_shape,
                         input_output_aliases={n_in - len(aliased) + q: k for q, (k, _) in enumerate(aliased)},
                         compiler_params=_params())(*[r for r, _, _ in rows], *pars, *deps, *[buf for _, buf in aliased])
    return list(res)


def _bdot(a, b, mode="nn"):
    dn = {"nn": (((1,), (0,)), ((), ())), "nt": (((1,), (1,)), ((), ())), "tn": (((0,), (0,)), ((), ()))}[mode]
    return lax.dot_general(a.astype(BF16), b.astype(BF16), dn, preferred_element_type=F32)


def _sigmoid(x):
    return jax.nn.sigmoid(x)


def _softplus(x):
    return jnp.maximum(x, 0.0) + jnp.log1p(jnp.exp(-jnp.abs(x)))


def _gelu(z):
    return 0.5 * z * (1.0 + lax.erf(z * _SQRT_HALF))


def _gelu_grad(z):
    return 0.5 * (1.0 + lax.erf(z * _SQRT_HALF)) + z * jnp.exp(-0.5 * z * z) * _INV_SQRT_2PI


def _mean(x):
    return jnp.mean(x, axis=-1, keepdims=True)


def _colsum(x):
    return jnp.sum(x, axis=0, keepdims=True)


def _rms_fwd(x, g, name, deps=()):
    def fn(rv, pv):
        (xv,), (gv,) = rv, pv
        r = lax.rsqrt(_mean(xv * xv) + RMS_EPS)
        return [xv * r * gv], []
    return _rowwise(fn, [x], [g], [(x.shape[1], BF16)], [], name=name, deps=deps)[0]


def _rms_bwd(dn, x, dres, g, name, deps=()):
    def fn(rv, pv):
        (dnv, xv, drv), (gv,) = rv, pv
        r = lax.rsqrt(_mean(xv * xv) + RMS_EPS)
        yn = xv * r
        dyg = dnv * gv
        dx = drv + r * (dyg - yn * _mean(dyg * yn))
        return [dx, dx], [_colsum(dnv * yn)]
    D = x.shape[1]
    return _rowwise(fn, [dn, x, dres], [g], [(D, F32), (D, BF16)], [(1, D)], name=name, deps=deps)


def _rwkv_layout(RW, Lw, La, Lg, D):
    widths = [RW, RW, RW, Lw, La, Lg]
    pw = [_ceil_to(w, LANE) for w in widths]
    pw[5] += _ceil_to(sum(pw), 2 * D) - sum(pw)
    offs = [sum(pw[:i]) for i in range(6)]
    return widths, pw, offs, sum(pw)


def _pad_rwkv_cols(a, lay):
    widths, pw, _, _ = lay
    pieces, src = [], 0
    for w, p in zip(widths, pw):
        pieces.append(a[:, src:src + w])
        if p > w:
            pieces.append(jnp.zeros((a.shape[0], p - w), a.dtype))
        src += w
    return jnp.concatenate(pieces, axis=1)


def _unpad_rwkv_cols(a, lay):
    widths, _, offs, _ = lay
    return jnp.concatenate([a[:, o:o + w] for o, w in zip(offs, widths)], axis=1)


def _proj_pieces(lay, D, cs):
    widths, _, offs, rcp = lay
    rc = sum(widths)
    segs = [(sum(widths[:j]), widths[j], offs[j]) for j in range(6)] + [(rc, D, rcp + 2 * D), (rc + D, D, rcp), (rc + 2 * D, D, rcp + D)]
    pieces = []
    for start, width, dst in segs:
        n = start
        while n < start + width:
            d, off = divmod(n, cs)
            take = min(cs - off, start + width - n)
            pieces.append((d, off, dst + n - start, take))
            n += take
    return pieces


def _w_in_to_proj(g, lay, D, name):
    nb, rows, cs = g.shape
    icp = lay[3] + 3 * D
    pieces = _proj_pieces(lay, D, cs)
    tm = _pick(rows, (256, 128, 64, 32, 16))

    def body(i_ref, o_ref):
        o_ref[...] = jnp.zeros_like(o_ref)
        for d, src, dst, w in pieces:
            o_ref[:, dst:dst + w] = i_ref[d, :, src:src + w]

    return pl.pallas_call(
        body, name=name, grid=(rows // tm,), in_specs=[pl.BlockSpec((nb, tm, cs), lambda i: (0, i, 0))],
        out_specs=pl.BlockSpec((tm, icp), lambda i: (i, 0)), out_shape=jax.ShapeDtypeStruct((rows, icp), g.dtype),
        compiler_params=_params())(g)


def _dw_in_from_proj(a, lay, D, cs, name):
    rows, icp = a.shape
    pieces = _proj_pieces(lay, D, cs)
    tm = _pick(rows, (256, 128, 64, 32, 16))

    def body(i_ref, o_ref):
        for d, src, dst, w in pieces:
            o_ref[d, :, src:src + w] = i_ref[:, dst:dst + w]

    return pl.pallas_call(
        body, name=name, grid=(rows // tm,), in_specs=[pl.BlockSpec((tm, icp), lambda i: (i, 0))],
        out_specs=pl.BlockSpec((N_DEV, tm, cs), lambda i: (0, i, 0)), out_shape=jax.ShapeDtypeStruct((N_DEV, rows, cs), a.dtype),
        compiler_params=_params())(a)


def _pad_rows(a, rows):
    return a if a.shape[0] == rows else jnp.concatenate([a, jnp.zeros((rows - a.shape[0], a.shape[1]), a.dtype)], axis=0)


def _token_shift(p, halo, mu, i):
    tm = p.shape[0]
    hid = lax.broadcasted_iota(jnp.int32, (SUBLANE, 1), 0)
    before = jnp.sum(jnp.where(hid == SUBLANE - 1, halo, 0.0), axis=0, keepdims=True)
    before = jnp.where(i == 0, 0.0, before)
    rid = lax.broadcasted_iota(jnp.int32, (tm, 1), 0)
    prev = jnp.where(rid == 0, before, pltpu.roll(p, 1, 0))
    d = prev - p
    return p + d * mu, d


def _rwkv_math(ps, w0, a0, k_k, k_a, wlw, wla, wlg, lay):
    _, pw, offs, _ = lay
    r, k, v, xw, xa, xg = (ps[:, offs[j]:offs[j] + pw[j]] for j in range(6))
    tw = jnp.tanh(xw)
    ww = w0 + _bdot(tw, wlw)
    lw = -jnp.exp(-_softplus(-ww) - 0.5)
    a = _sigmoid(a0 + _bdot(xa, wla))
    sg = _sigmoid(xg)
    g = _bdot(sg, wlg)
    return dict(r=r, k=k, v=v, xa=xa, tw=tw, ww=ww, lw=lw, a=a, sg=sg, g=g, kkp=k * k_k, k2=k * (1.0 + (a - 1.0) * k_a))


def _halo_specs(T, tm, width, after):
    hb = tm // SUBLANE
    last = T // SUBLANE - 1
    if after:
        return pl.BlockSpec((SUBLANE, width), lambda i: (jnp.minimum((i + 1) * hb, last), 0))
    return pl.BlockSpec((SUBLANE, width), lambda i: (jnp.maximum(i * hb - 1, 0), 0))


def _rowsum(x):
    return jnp.sum(x, axis=-1, keepdims=True)


def _kk_math(kkp):
    nrm = jnp.sqrt(_rowsum(kkp * kkp))
    inv = 1.0 / jnp.maximum(nrm, 1e-12)
    return nrm, inv, kkp * inv


def _rwkv_pre(p, mu, small, lora, lay, name):
    T, rcp = p.shape[0], lay[3]
    H = lay[0][0] // HEAD
    tm = min(128, T)

    def body(p_ref, ph_ref, mu_ref, w0_ref, a0_ref, kk_ref, ka_ref, wlw_ref, wla_ref, wlg_ref, r_o, lw_o, k2_o, v_o, aa_o, bb_o, g_o):
        ps, _ = _token_shift(p_ref[...], ph_ref[...], mu_ref[...], pl.program_id(0))
        q = _rwkv_math(ps, w0_ref[...], a0_ref[...], kk_ref[...], ka_ref[...], wlw_ref[...], wla_ref[...], wlg_ref[...], lay)
        for h in range(H):
            sl = slice(h * HEAD, (h + 1) * HEAD)
            for o_ref, key in ((r_o, "r"), (lw_o, "lw"), (k2_o, "k2"), (v_o, "v"), (g_o, "g")):
                o_ref[h] = q[key][:, sl]
            _, _, kk = _kk_math(q["kkp"][:, sl])
            aa_o[h] = -kk
            bb_o[h] = kk * q["a"][:, sl]

    whole = lambda arr: pl.BlockSpec(arr.shape, lambda i: (0, 0))
    return pl.pallas_call(
        body, name=name, grid=(T // tm,),
        in_specs=([pl.BlockSpec((tm, rcp), lambda i: (i, 0)), _halo_specs(T, tm, rcp, False), whole(mu)]
                  + [whole(s) for s in small] + [whole(w) for w in lora]),
        out_specs=[pl.BlockSpec((H, tm, HEAD), lambda i: (0, i, 0))] * 7, out_shape=[jax.ShapeDtypeStruct((H, T, HEAD), F32)] * 7,
        compiler_params=_params())(p, p, mu, *small, *lora)


def _rwkv_pre_bwd(p, mu, small, lora, hgrads, lay, name):
    T, rcp = p.shape[0], lay[3]
    widths, pw, offs, _ = lay
    RW = widths[0]
    H = RW // HEAD
    tm = min(128, T)

    def body(p_ref, ph_ref, mu_ref, w0_ref, a0_ref, kk_ref, ka_ref, wlw_ref, wla_ref, wlg_ref,
             dr1, dr2, dk1, dk2b, dv1, dv2, dlw_h, daa, dbb, dg_h,
             dps_ref, dmu_ref, dw0_ref, da0_ref, dkk_ref, dka_ref, dwlw_ref, dwla_ref, dwlg_ref,
             s_dr, s_dk2, s_dv, s_dlw, s_dkkp, s_da, s_dg):
        i = pl.program_id(0)
        ps, dprev = _token_shift(p_ref[...], ph_ref[...], mu_ref[...], i)
        k_k, k_a = kk_ref[...], ka_ref[...]
        q = _rwkv_math(ps, w0_ref[...], a0_ref[...], k_k, k_a, wlw_ref[...], wla_ref[...], wlg_ref[...], lay)
        k, a, lw, ww, tw, sg = q["k"], q["a"], q["lw"], q["ww"], q["tw"], q["sg"]
        for h in range(H):
            sl = slice(h * HEAD, (h + 1) * HEAD)
            s_dr[:, sl] = dr1[h] + dr2[h]
            s_dk2[:, sl] = dk1[h] + dk2b[h]
            s_dv[:, sl] = dv1[h] + dv2[h]
            s_dlw[:, sl] = dlw_h[h]
            s_dg[:, sl] = dg_h[h]
            nrm, inv, kk = _kk_math(q["kkp"][:, sl])
            dbb_h = dbb[h]
            dkk = dbb_h * a[:, sl] - daa[h]
            s_dkkp[:, sl] = jnp.where(nrm > 1e-12, inv * (dkk - kk * _rowsum(dkk * kk)), dkk * inv)
            s_da[:, sl] = dbb_h * kk
        dk2, dkkp, dg = s_dk2[...], s_dkkp[...], s_dg[...]
        dk = dk2 * (1.0 + (a - 1.0) * k_a) + dkkp * k_k
        da = s_da[...] + dk2 * k * k_a
        dpa = da * a * (1.0 - a)
        dww = s_dlw[...] * lw * _sigmoid(-ww)
        dxa = _bdot(dpa, wla_ref[...], "nt")
        dxw = _bdot(dww, wlw_ref[...], "nt") * (1.0 - tw * tw)
        dxg = _bdot(dg, wlg_ref[...], "nt") * sg * (1.0 - sg)
        segs = (s_dr[...], dk, s_dv[...], dxw, dxa, dxg)
        sums = [dmu_ref, dw0_ref, da0_ref, dkk_ref, dka_ref, dwlw_ref, dwla_ref, dwlg_ref]

        @pl.when(i == 0)
        def _():
            for s in sums:
                s[...] = jnp.zeros_like(s)

        for j, seg in enumerate(segs):
            sl = slice(offs[j], offs[j] + pw[j])
            dps_ref[:, sl] = seg
            dmu_ref[:, sl] += _colsum(seg * dprev[:, sl])
        dw0_ref[...] += _colsum(dww)
        da0_ref[...] += _colsum(dpa)
        dkk_ref[...] += _colsum(dkkp * k)
        dka_ref[...] += _colsum(dk2 * k * (a - 1.0))
        dwlw_ref[...] += _bdot(tw, dww, "tn")
        dwla_ref[...] += _bdot(q["xa"], dpa, "tn")
        dwlg_ref[...] += _bdot(sg, dg, "tn")

    whole = lambda arr: pl.BlockSpec(arr.shape, lambda i: (0, 0))
    row = lambda w: pl.BlockSpec((tm, w), lambda i: (i, 0))
    acc_shapes = [(1, rcp), (1, RW), (1, RW), (1, RW), (1, RW)] + [w.shape for w in lora]
    return pl.pallas_call(
        body, name=name, grid=(T // tm,),
        in_specs=([row(rcp), _halo_specs(T, tm, rcp, False), whole(mu)] + [whole(s) for s in small] + [whole(w) for w in lora]
                  + [pl.BlockSpec((H, tm, HEAD), lambda i: (0, i, 0))] * 10),
        out_specs=[row(rcp)] + [pl.BlockSpec(s, lambda i: (0, 0)) for s in acc_shapes],
        out_shape=[jax.ShapeDtypeStruct((T, rcp), F32)] + [jax.ShapeDtypeStruct(s, F32) for s in acc_shapes],
        scratch_shapes=[pltpu.VMEM((tm, RW), F32)] * 7, compiler_params=_params())(p, p, mu, *small, *lora, *hgrads)


def _shift_bwd(dps, mu, dproj, name):
    T, rcp = dps.shape
    tm = min(256, T)
    nt = T // tm

    def body(d_ref, dh_ref, mu_ref, buf_ref, o_ref):
        i = pl.program_id(0)
        d = d_ref[...]
        hid = lax.broadcasted_iota(jnp.int32, (SUBLANE, 1), 0)
        after = jnp.sum(jnp.where(hid == 0, dh_ref[...], 0.0), axis=0, keepdims=True)
        after = jnp.where(i == nt - 1, 0.0, after)
        rid = lax.broadcasted_iota(jnp.int32, (tm, 1), 0)
        nxt = jnp.where(rid == tm - 1, after, pltpu.roll(d, tm - 1, 0))
        mu_v = mu_ref[...]
        o_ref[...] = (d * (1.0 - mu_v) + nxt * mu_v).astype(BF16)

    row = pl.BlockSpec((tm, rcp), lambda i: (i, 0))
    return pl.pallas_call(
        body, name=name, grid=(nt,),
        in_specs=[row, _halo_specs(T, tm, rcp, True), pl.BlockSpec(mu.shape, lambda i: (0, 0)), pl.BlockSpec(memory_space=pl.ANY)],
        out_specs=row, out_shape=jax.ShapeDtypeStruct(dproj.shape, BF16), input_output_aliases={3: 0},
        compiler_params=_params())(dps, dps, mu, dproj)


def _head_post_math(y, r, k2, v, lg, lb, rk):
    yc = y - _mean(y)
    rstd = lax.rsqrt(_mean(yc * yc) + LNX_EPS)
    yn = yc * rstd
    s = _rowsum(r * k2 * rk)
    return yn, rstd, yn * lg + lb + s * v, s


def _head_post(y, r, k2, v, g, hp, name, deps=()):
    H, T, _ = y.shape
    tm = min(128, T)

    def body(y_ref, r_ref, k_ref, v_ref, g_ref, lg_ref, lb_ref, rk_ref, *rest):
        o_ref = rest[-1]
        _, _, t, _ = _head_post_math(y_ref[...], r_ref[...], k_ref[...], v_ref[...], lg_ref[...], lb_ref[...], rk_ref[...])
        out = (t * g_ref[...]).astype(BF16)
        for h in range(H):
            o_ref[:, h * HEAD:(h + 1) * HEAD] = out[h]

    blk = pl.BlockSpec((H, tm, HEAD), lambda i: (0, i, 0))
    par = pl.BlockSpec((H, 1, HEAD), lambda i: (0, 0, 0))
    return pl.pallas_call(
        body, name=name, grid=(T // tm,),
        in_specs=[blk] * 5 + [par] * 3 + [pl.BlockSpec(d.shape, lambda i, nd=d.ndim: (0,) * nd) for d in deps],
        out_specs=pl.BlockSpec((tm, H * HEAD), lambda i: (i, 0)),
        out_shape=jax.ShapeDtypeStruct((T, H * HEAD), BF16), compiler_params=_params())(y, r, k2, v, g, *hp, *deps)


def _head_post_bwd(dya, y, r, k2, v, g, hp, name, deps=()):
    H, T, _ = y.shape
    tm = min(128, T)
    hsum = lambda t: jnp.sum(t, axis=1, keepdims=True)

    def body(d_ref, y_ref, r_ref, k_ref, v_ref, g_ref, lg_ref, lb_ref, rk_ref, *rest):
        outs, d_s = rest[len(deps):len(deps) + 8], rest[-1]
        for h in range(H):
            d_s[h] = d_ref[:, h * HEAD:(h + 1) * HEAD]
        d_v, r_v, k_v, v_v, lg, rk = d_s[...], r_ref[...], k_ref[...], v_ref[...], lg_ref[...], rk_ref[...]
        yn, rstd, t, s = _head_post_math(y_ref[...], r_v, k_v, v_v, lg, lb_ref[...], rk)
        dyo = d_v * g_ref[...]
        dyn = dyo * lg
        ds = _rowsum(dyo * v_v)
        vals = (rstd * (dyn - _mean(dyn) - yn * _mean(dyn * yn)), ds * k_v * rk, ds * r_v * rk, dyo * s, d_v * t)
        for o_ref, val in zip(outs[:5], vals):
            o_ref[...] = val
        sums = (hsum(dyo * yn), hsum(dyo), hsum(ds * r_v * k_v))
        first = pl.program_id(0) == 0

        @pl.when(first)
        def _():
            for o_ref, val in zip(outs[5:], sums):
                o_ref[...] = val

        @pl.when(jnp.logical_not(first))
        def _():
            for o_ref, val in zip(outs[5:], sums):
                o_ref[...] += val

    blk = pl.BlockSpec((H, tm, HEAD), lambda i: (0, i, 0))
    par = pl.BlockSpec((H, 1, HEAD), lambda i: (0, 0, 0))
    return pl.pallas_call(
        body, name=name, grid=(T // tm,),
        in_specs=([pl.BlockSpec((tm, H * HEAD), lambda i: (i, 0))] + [blk] * 5 + [par] * 3
                  + [pl.BlockSpec(d.shape, lambda i, nd=d.ndim: (0,) * nd) for d in deps]),
        out_specs=[blk] * 5 + [par] * 3,
        out_shape=[jax.ShapeDtypeStruct((H, T, HEAD), F32)] * 5 + [jax.ShapeDtypeStruct((H, 1, HEAD), F32)] * 3,
        scratch_shapes=[pltpu.VMEM((H, tm, HEAD), F32)], compiler_params=_params())(dya, y, r, k2, v, g, *hp, *deps)


def _bmm(x, y, mode):
    dn = {"nn": (((2,), (1,)), ((0,), (0,))), "nt": (((2,), (2,)), ((0,), (0,))), "tn": (((1,), (1,)), ((0,), (0,)))}[mode]
    (xh, xl), (yh, yl) = _split(x), _split(y)
    dot = lambda p, q: lax.dot_general(p, q, dn, preferred_element_type=F32)
    out = dot(xh, yh)
    if yl is not None:
        out = out + dot(xh, yl)
    if xl is not None:
        out = out + dot(xl, yh)
    return out


def _split(x):
    if isinstance(x, tuple):
        return x
    hi = x.astype(BF16)
    return hi, (x - hi.astype(F32)).astype(BF16)


def _exact(x):
    return x.astype(BF16), None


def _round(x):
    return x if isinstance(x, tuple) else (x.astype(BF16), None)


def _rows(*xs):
    if isinstance(xs[0], tuple):
        return tuple(None if any(p is None for p in parts) else jnp.concatenate(parts, axis=1) for parts in zip(*xs))
    return jnp.concatenate(xs, axis=1)


def _wkv_chunk(r, lw, k, v, a, b):
    hb, C, _ = r.shape
    ti = lax.broadcasted_iota(jnp.int32, (C, C), 0)
    si = lax.broadcasted_iota(jnp.int32, (C, C), 1)
    linc, lstr, eye = (ti >= si).astype(F32), (ti > si).astype(F32), (ti == si).astype(F32)
    qmask = jnp.concatenate([jnp.concatenate([lstr, lstr], axis=1), jnp.concatenate([linc, linc], axis=1)], axis=0)
    lincb = _exact(jnp.broadcast_to(linc, (hb, C, C)))
    both = _exact(jnp.broadcast_to(jnp.concatenate([linc, lstr], axis=0), (hb, 2 * C, C)))
    ones = _exact(jnp.ones_like(v))
    lws = _split(lw)
    ci = _bmm(lincb, lws, "nn")
    cC = jnp.sum(lw, axis=1, keepdims=True)
    gi, ge, gn, gr = jnp.exp(ci), jnp.exp(ci - lw), jnp.exp(-ci), jnp.exp(cC - ci)
    q = dict(At=a * ge, Rt=r * gi, Bt=b * gn, Kt=k * gn, Bh=b * gr, Kh=k * gr)
    s = dict(AR=_round(_rows(q["At"], q["Rt"])), BK=_round(_rows(q["Bt"], q["Kt"])), BKh=_round(_rows(q["Bh"], q["Kh"])), v=_round(v))
    quad = _bmm(s["AR"], s["BK"], "nt") * qmask
    s["top"], s["bot"] = _round(quad[:, :C]), _round(quad[:, C:])
    A_ab = quad[:, :C, :C]
    Tm = eye + A_ab
    Pw = _round(A_ab)
    n = 1
    while 2 * n < C:
        Pw = _round(_bmm(Pw, Pw, "nn"))
        Tm = Tm + _bmm(_round(Tm), Pw, "nn")
        n *= 2
    s["Tm"] = _round(Tm)
    gC = jnp.exp(_bmm(lws, ones, "tn"))
    q.update(gi=gi, ge=ge, gn=gn, gr=gr, qmask=qmask, both=both, gC=gC, ones=ones, s=s)
    return q


def _wkv_u(s, H0s, C):
    arh = _bmm(s["AR"], H0s, "nn")
    zv = _rows(tuple(None if p is None else jnp.zeros_like(p) for p in s["v"]), s["v"])
    U = _bmm(s["Tm"], _round(arh[:, :C] + _bmm(s["top"], zv, "nn")), "nn")
    return arh, _rows(_round(U), s["v"])


def _wkv_fwd(r, lw, k, v, a, b, name):
    H, T, N = r.shape
    C = min(WKV_CHUNK, T)
    nc = T // C
    hb = _pick(H, (16, 8, 4, 2))

    def body(r_ref, lw_ref, k_ref, v_ref, a_ref, b_ref, y_ref, st_ref, h_ref):
        @pl.when(pl.program_id(1) == 0)
        def _():
            h_ref[...] = jnp.zeros_like(h_ref)

        H0 = h_ref[...]
        st_ref[0] = H0
        q = _wkv_chunk(r_ref[...], lw_ref[...], k_ref[...], v_ref[...], a_ref[...], b_ref[...])
        s = q["s"]
        arh, UV = _wkv_u(s, _round(H0), C)
        y_ref[...] = arh[:, C:] + _bmm(s["bot"], UV, "nn")
        h_ref[...] = q["gC"] * H0 + _bmm(s["BKh"], UV, "tn")

    blk = pl.BlockSpec((hb, C, N), lambda h, c: (h, c, 0))
    return pl.pallas_call(
        body, name=name, grid=(H // hb, nc), in_specs=[blk] * 6,
        out_specs=[blk, pl.BlockSpec((1, hb, N, N), lambda h, c: (c, h, 0, 0))],
        out_shape=[jax.ShapeDtypeStruct((H, T, N), F32), jax.ShapeDtypeStruct((nc, H, N, N), F32)],
        scratch_shapes=[pltpu.VMEM((hb, N, N), F32)], compiler_params=_params())(r, lw, k, v, a, b)


def _wkv_bwd(r, lw, k, v, a, b, states, dy, name):
    H, T, N = r.shape
    C = min(WKV_CHUNK, T)
    nc = T // C
    hb = _pick(H, (16, 8, 4, 2))

    def body(r_ref, lw_ref, k_ref, v_ref, a_ref, b_ref, st_ref, dy_ref, dr_ref, dlw_ref, dk_ref, dv_ref, da_ref, db_ref, dh_ref):
        @pl.when(pl.program_id(1) == 0)
        def _():
            dh_ref[...] = jnp.zeros_like(dh_ref)

        dHC = dh_ref[...]
        H0 = st_ref[0]
        q = _wkv_chunk(r_ref[...], lw_ref[...], k_ref[...], v_ref[...], a_ref[...], b_ref[...])
        s, gC = q["s"], q["gC"]
        H0s, dHs, dY = _round(H0), _round(dHC), _round(dy_ref[...])
        _, UV = _wkv_u(s, H0s, C)
        bot_dy = _bmm(s["bot"], dY, "tn")
        bkh_dh = _bmm(s["BKh"], dHs, "nn")
        dP = _round(_bmm(s["Tm"], _round(bot_dy[:, :C] + bkh_dh[:, :C]), "tn"))
        dv_ref[...] = bot_dy[:, C:] + bkh_dh[:, C:] + _bmm(s["top"], dP, "tn")[:, C:]
        dPY = _rows(dP, dY)
        dh_ref[...] = gC * dHC + _bmm(s["AR"], dPY, "tn")
        dquad = _round(_bmm(dPY, UV, "nt") * q["qmask"])
        dAR = _bmm(dPY, H0s, "nt") + _bmm(dquad, s["BK"], "nn")
        dBK = _bmm(dquad, s["AR"], "tn")
        dBKh = _bmm(UV, dHs, "nt")
        dAt, dRt, dBt, dKt, dBh, dKh = dAR[:, :C], dAR[:, C:], dBK[:, :C], dBK[:, C:], dBKh[:, :C], dBKh[:, C:]
        dr_ref[...] = dRt * q["gi"]
        da_ref[...] = dAt * q["ge"]
        db_ref[...] = dBt * q["gn"] + dBh * q["gr"]
        dk_ref[...] = dKt * q["gn"] + dKh * q["gr"]
        tail = dBh * q["Bh"] + dKh * q["Kh"]
        dci = dRt * q["Rt"] - dBt * q["Bt"] - dKt * q["Kt"] - tail
        dcC = jnp.sum(tail, axis=1, keepdims=True) + _bmm(q["ones"], H0 * dHC * gC, "nt")
        dlw_ref[...] = _bmm(q["both"], _rows(dci, dAt * q["At"]), "tn") + dcC

    blk = pl.BlockSpec((hb, C, N), lambda h, c: (h, nc - 1 - c, 0))
    st = pl.BlockSpec((1, hb, N, N), lambda h, c: (nc - 1 - c, h, 0, 0))
    return pl.pallas_call(
        body, name=name, grid=(H // hb, nc), in_specs=[blk] * 6 + [st, blk], out_specs=[blk] * 6,
        out_shape=[jax.ShapeDtypeStruct((H, T, N), F32)] * 6,
        scratch_shapes=[pltpu.VMEM((hb, N, N), F32)], compiler_params=_params())(r, lw, k, v, a, b, states, dy)


def _sgu_ln(z, SW, lng, lnb):
    ge = _gelu(z)
    u, vv = ge[:, :SW], ge[:, SW:]
    xc = vv - _mean(vv)
    rstd = lax.rsqrt(_mean(xc * xc) + LN_EPS)
    vn = xc * rstd
    return u, vn, rstd, vn * lng + lnb


def _causal(ws_ref, g):
    ti = lax.broadcasted_iota(jnp.int32, (SGU_CHUNK, SGU_CHUNK), 0)
    si = lax.broadcasted_iota(jnp.int32, (SGU_CHUNK, SGU_CHUNK), 1)
    return ti >= si, jnp.where(ti >= si, ws_ref[g], 0.0).astype(BF16)


def _sgu_fwd(proj, zblock, lng, lnb, ws, bexp, name):
    T, SW = proj.shape[0], lng.shape[1]
    G = ws.shape[0]
    tr = min(256, T)
    nch = tr // SGU_CHUNK

    def body(z_ref, lng_ref, lnb_ref, ws_ref, be_ref, o_ref):
        u, _, _, vl = _sgu_ln(z_ref[...], SW, lng_ref[...], lnb_ref[...])
        for g in range(G):
            cs = slice(g * SGU_GROUP, (g + 1) * SGU_GROUP)
            _, wc = _causal(ws_ref, g)
            for n in range(nch):
                rs = slice(n * SGU_CHUNK, (n + 1) * SGU_CHUNK)
                m = jnp.dot(wc, vl[rs, cs].astype(BF16), preferred_element_type=F32) + be_ref[:, cs]
                o_ref[rs, cs] = (u[rs, cs] * m).astype(BF16)

    whole = lambda arr: pl.BlockSpec(arr.shape, lambda i, nd=arr.ndim: (0,) * nd)
    return pl.pallas_call(
        body, name=name, grid=(T // tr,),
        in_specs=[pl.BlockSpec((tr, 2 * SW), lambda i: (i, zblock)), whole(lng), whole(lnb), whole(ws), whole(bexp)],
        out_specs=pl.BlockSpec((tr, SW), lambda i: (i, 0)), out_shape=jax.ShapeDtypeStruct((T, SW), BF16),
        compiler_params=_params())(proj, lng, lnb, ws, bexp)


def _sgu_bwd(proj, zblock, dyb, lng, lnb, ws, bexp, dproj, name):
    T, SW = proj.shape[0], lng.shape[1]
    G = ws.shape[0]
    tr = min(256, T)
    nch = tr // SGU_CHUNK
    nt = T // tr

    def body(z_ref, dy_ref, lng_ref, lnb_ref, ws_ref, be_ref, buf_ref, dz_ref, dlg_ref, dlb_ref, dws_ref, db_ref, du_s, dvl_s, dbacc_s):
        i = pl.program_id(0)
        zv = z_ref[...]
        lng_v = lng_ref[...]
        u, vn, rstd, vl = _sgu_ln(zv, SW, lng_v, lnb_ref[...])

        @pl.when(i == 0)
        def _():
            for s in (dlg_ref, dlb_ref, dws_ref, dbacc_s):
                s[...] = jnp.zeros_like(s)

        for g in range(G):
            cs = slice(g * SGU_GROUP, (g + 1) * SGU_GROUP)
            tri, wc = _causal(ws_ref, g)
            for n in range(nch):
                rs = slice(n * SGU_CHUNK, (n + 1) * SGU_CHUNK)
                blk = vl[rs, cs].astype(BF16)
                m = jnp.dot(wc, blk, preferred_element_type=F32) + be_ref[:, cs]
                dyv = dy_ref[rs, cs]
                du_s[rs, cs] = dyv * m
                dm = dyv * u[rs, cs]
                dvl_s[rs, cs] = _bdot(wc, dm, "tn")
                dws_ref[g] += jnp.where(tri, _bdot(dm, blk, "nt"), 0.0)
                dbacc_s[:, cs] += dm

        dvl = dvl_s[...]
        dlg_ref[...] += _colsum(dvl * vn)
        dlb_ref[...] += _colsum(dvl)
        dvn = dvl * lng_v
        dvv = rstd * (dvn - _mean(dvn) - vn * _mean(dvn * vn))
        gp = _gelu_grad(zv)
        dz_ref[:, :SW] = (du_s[...] * gp[:, :SW]).astype(BF16)
        dz_ref[:, SW:] = (dvv * gp[:, SW:]).astype(BF16)

        @pl.when(i == nt - 1)
        def _():
            lane = lax.broadcasted_iota(jnp.int32, (SGU_CHUNK, LANE), 1)
            out = jnp.zeros((SGU_CHUNK, LANE), F32)
            for g in range(G):
                col = jnp.sum(dbacc_s[:, g * SGU_GROUP:(g + 1) * SGU_GROUP], axis=1, keepdims=True)
                out = jnp.where(lane == g, col, out)
            db_ref[...] = out

    whole = lambda arr: pl.BlockSpec(arr.shape, lambda i, nd=arr.ndim: (0,) * nd)
    acc_shapes = [(1, SW), (1, SW), ws.shape, (SGU_CHUNK, LANE)]
    return pl.pallas_call(
        body, name=name, grid=(nt,),
        in_specs=[pl.BlockSpec((tr, 2 * SW), lambda i: (i, zblock)), pl.BlockSpec((tr, SW), lambda i: (i, 0)),
                  whole(lng), whole(lnb), whole(ws), whole(bexp), pl.BlockSpec(memory_space=pl.ANY)],
        out_specs=([pl.BlockSpec((tr, 2 * SW), lambda i: (i, zblock))]
                   + [pl.BlockSpec(s, lambda i, nd=len(s): (0,) * nd) for s in acc_shapes]),
        out_shape=[jax.ShapeDtypeStruct(dproj.shape, BF16)] + [jax.ShapeDtypeStruct(s, F32) for s in acc_shapes],
        scratch_shapes=[pltpu.VMEM((tr, SW), F32), pltpu.VMEM((tr, SW), F32), pltpu.VMEM((SGU_CHUNK, SW), F32)],
        input_output_aliases={6: 0}, compiler_params=_params())(proj, dyb, lng, lnb, ws, bexp, dproj)


_HBM = pl.BlockSpec(memory_space=pltpu.HBM)
_SEM = pl.BlockSpec(memory_space=pltpu.SEMAPHORE)
_DATAFLOW = pltpu.SideEffectType.DATAFLOW_SIDE_EFFECTING


def _mesh_place(chips=False):
    x, y, c = lax.axis_index("x"), lax.axis_index("y"), lax.axis_index("c")
    return x, y, c, (2 * x + y if chips else 4 * x + 2 * y + c)


def _peer(x, y, c, rel, chips=False):
    px = 1 - x if rel & 4 else x
    py = 1 - y if rel & 2 else y
    pc = 1 - c if rel & 1 else c
    return (px, py, pc), (2 * px + py if chips else 4 * px + 2 * py + pc)


ALL_PEERS = tuple(range(1, N_DEV))
SIBLING = (1,)
SAME_CORE = (2, 4, 6)
SIBLINGS_CORE = (3, 5, 7)


def _exchange_start(groups, name, rels=ALL_PEERS, chips=False):
    flat = [t for g in groups for t in g]
    sizes = [len(g) for g in groups]
    n, ng = len(flat), len(groups)
    srcs = [pltpu.with_memory_space_constraint(a, pltpu.HBM) for a, _ in flat]
    lands = [pltpu.with_memory_space_constraint(lax.empty(((N_DEV,) + a.shape) if isg else a.shape, a.dtype), pltpu.HBM)
             for a, isg in flat]

    def body(*refs):
        ins, lnd, sems, token = refs[:n], refs[n:2 * n], refs[2 * n:2 * n + 3 * ng], refs[-1]
        x, y, c, me = _mesh_place(chips)
        j0 = 0
        for gi, sz in enumerate(sizes):
            for rel in rels:
                dev, slot = _peer(x, y, c, rel, chips)
                for jj in range(sz):
                    j = j0 + jj
                    pltpu.make_async_remote_copy(
                        src_ref=ins[j] if flat[j][1] else ins[j].at[slot], dst_ref=lnd[j].at[me],
                        send_sem=sems[3 * gi].at[jj * (N_DEV - 1) + rel - 1], recv_sem=sems[3 * gi + 1].at[jj * (N_DEV - 1) + rel - 1],
                        device_id=dev, device_id_type=pl.DeviceIdType.MESH).start()
            for jj in range(sz):
                j = j0 + jj
                pltpu.make_async_copy(ins[j] if flat[j][1] else ins[j].at[me], lnd[j].at[me], sems[3 * gi + 2].at[jj]).start()
            j0 += sz
        token[...] = jnp.zeros_like(token)

    sem_shapes = [pltpu.SemaphoreType.DMA((k,)) for sz in sizes for k in (sz * (N_DEV - 1), sz * (N_DEV - 1), sz)]
    res = pl.pallas_call(
        body, name=name,
        out_shape=(*sem_shapes, *[pltpu.HBM(a.shape, a.dtype) for a in srcs], *[pltpu.HBM(a.shape, a.dtype) for a in lands],
                   jax.ShapeDtypeStruct((SUBLANE, LANE), F32)),
        in_specs=[_HBM] * (2 * n), out_specs=(*[_SEM] * (3 * ng), *[_HBM] * (2 * n), pl.BlockSpec(memory_space=pltpu.VMEM)),
        input_output_aliases={i: 3 * ng + i for i in range(2 * n)},
        compiler_params=pltpu.CompilerParams(has_side_effects=_DATAFLOW))(*srcs, *lands)
    sems, thru, token = res[:3 * ng], res[3 * ng:3 * ng + 2 * n], res[-1]
    handle, j0 = [], 0
    for gi, sz in enumerate(sizes):
        handle.append(dict(kinds=[k for _, k in groups[gi]], chips=chips, srcs=list(thru[j0:j0 + sz]), lands=list(thru[n + j0:n + j0 + sz]),
                           sems=list(sems[3 * gi:3 * gi + 3])))
        j0 += sz
    return handle, token


def _exchange_wait(group, after, name, rels=ALL_PEERS, local=True):
    kinds, sz = group["kinds"], len(group["kinds"])
    relay = group.get("relay", [])

    def body(*refs):
        ins, lnd, (ssem, rsem, lsem) = refs[:sz], refs[sz:2 * sz], refs[2 * sz:2 * sz + 3]
        x, y, c, me = _mesh_place(group["chips"])
        for rel in rels:
            dev, slot = _peer(x, y, c, rel, group["chips"])
            for jj in range(sz):
                cp = pltpu.make_async_remote_copy(
                    src_ref=ins[jj] if kinds[jj] else ins[jj].at[slot], dst_ref=lnd[jj].at[slot],
                    send_sem=ssem.at[jj * (N_DEV - 1) + rel - 1], recv_sem=rsem.at[jj * (N_DEV - 1) + rel - 1],
                    device_id=dev, device_id_type=pl.DeviceIdType.MESH)
                cp.wait_send()
                cp.wait_recv()
        if local:
            for jj in range(sz):
                pltpu.make_async_copy(ins[jj] if kinds[jj] else ins[jj].at[me], lnd[jj].at[me], lsem.at[jj]).wait()
        if relay:
            fsend, frecv = refs[2 * sz + 3:2 * sz + 5]
            dev = _peer(x, y, c, 1)[0]
            for q, (mine, theirs) in enumerate(zip(SAME_CORE, SIBLINGS_CORE)):
                for jj in range(sz):
                    cp = pltpu.make_async_remote_copy(
                        src_ref=lnd[jj].at[_peer(x, y, c, mine)[1]], dst_ref=lnd[jj].at[_peer(x, y, c, theirs)[1]],
                        send_sem=fsend.at[jj * len(SAME_CORE) + q], recv_sem=frecv.at[jj * len(SAME_CORE) + q],
                        device_id=dev, device_id_type=pl.DeviceIdType.MESH)
                    cp.wait_send()
                    cp.wait_recv()

    arrays = group["srcs"] + group["lands"]
    sems = group["sems"] + relay
    res = pl.pallas_call(
        body, name=name, out_shape=[pltpu.HBM(a.shape, a.dtype) for a in arrays],
        in_specs=[_HBM] * (2 * sz) + [_SEM] * len(sems) + [pl.BlockSpec(memory_space=pl.ANY)], out_specs=[_HBM] * (2 * sz),
        input_output_aliases={i: i for i in range(2 * sz)},
        compiler_params=pltpu.CompilerParams(has_side_effects=_DATAFLOW))(*arrays, *sems, after)
    return dict(group, srcs=list(res[:sz]), lands=list(res[sz:]), relay=[])


def _relay_start(group, name):
    sz = len(group["kinds"])
    nq = len(SAME_CORE)

    def body(*refs):
        lnd, fsend, frecv, token = refs[:sz], refs[sz], refs[sz + 1], refs[-1]
        x, y, c, _ = _mesh_place()
        dev = _peer(x, y, c, 1)[0]
        for q, rel in enumerate(SAME_CORE):
            slot = _peer(x, y, c, rel)[1]
            for jj in range(sz):
                pltpu.make_async_remote_copy(
                    src_ref=lnd[jj].at[slot], dst_ref=lnd[jj].at[slot], send_sem=fsend.at[jj * nq + q], recv_sem=frecv.at[jj * nq + q],
                    device_id=dev, device_id_type=pl.DeviceIdType.MESH).start()
        token[...] = jnp.zeros_like(token)

    lands = group["lands"]
    res = pl.pallas_call(
        body, name=name,
        out_shape=(pltpu.SemaphoreType.DMA((sz * nq,)), pltpu.SemaphoreType.DMA((sz * nq,)), *[pltpu.HBM(a.shape, a.dtype) for a in lands],
                   jax.ShapeDtypeStruct((SUBLANE, LANE), F32)),
        in_specs=[_HBM] * sz, out_specs=(_SEM, _SEM, *[_HBM] * sz, pl.BlockSpec(memory_space=pltpu.VMEM)),
        input_output_aliases={i: 2 + i for i in range(sz)},
        compiler_params=pltpu.CompilerParams(has_side_effects=_DATAFLOW))(*lands)
    return dict(group, lands=list(res[2:2 + sz]), relay=[res[0], res[1]]), res[-1]


def _sibling_swap(arrays, handle, after, name):
    start = handle is None
    n = len(arrays) if start else len(handle["srcs"])
    chips = N_DEV // 2
    if start:
        srcs = [pltpu.with_memory_space_constraint(a.reshape(chips, 2, *a.shape[1:]), pltpu.HBM) for a in arrays]
        lands = [pltpu.with_memory_space_constraint(lax.empty((chips,) + a.shape[1:], a.dtype), pltpu.HBM) for a in arrays]
    else:
        srcs, lands = handle["srcs"], handle["lands"]

    def body(*refs):
        ins, lnd, ssem, rsem = refs[:n], refs[n:2 * n], refs[2 * n], refs[2 * n + 1]
        x, y, c, _ = _mesh_place()
        dev = _peer(x, y, c, 1)[0]
        for q in range(chips):
            for j in range(n):
                cp = pltpu.make_async_remote_copy(
                    src_ref=ins[j].at[q, 1 - c], dst_ref=lnd[j].at[q], send_sem=ssem.at[j * chips + q], recv_sem=rsem.at[j * chips + q],
                    device_id=dev, device_id_type=pl.DeviceIdType.MESH)
                if start:
                    cp.start()
                else:
                    cp.wait_send()
                    cp.wait_recv()
        if start:
            refs[-1][...] = jnp.zeros_like(refs[-1])

    thru = [pltpu.HBM(a.shape, a.dtype) for a in srcs + lands]
    effect = pltpu.CompilerParams(has_side_effects=_DATAFLOW)
    if start:
        res = pl.pallas_call(
            body, name=name, out_shape=(pltpu.SemaphoreType.DMA((n * chips,)), pltpu.SemaphoreType.DMA((n * chips,)), *thru,
                                        jax.ShapeDtypeStruct((SUBLANE, LANE), F32)),
            in_specs=[_HBM] * (2 * n), out_specs=(_SEM, _SEM, *[_HBM] * (2 * n), pl.BlockSpec(memory_space=pltpu.VMEM)),
            input_output_aliases={i: 2 + i for i in range(2 * n)}, compiler_params=effect)(*srcs, *lands)
        return dict(srcs=list(res[2:2 + n]), lands=list(res[2 + n:2 + 2 * n]), sems=[res[0], res[1]]), res[-1]
    res = pl.pallas_call(
        body, name=name, out_shape=thru, in_specs=[_HBM] * (2 * n) + [_SEM, _SEM, pl.BlockSpec(memory_space=pl.ANY)],
        out_specs=[_HBM] * (2 * n), input_output_aliases={i: i for i in range(2 * n)}, compiler_params=effect)(
            *srcs, *lands, *handle["sems"], after)
    return dict(handle, srcs=list(res[:n]), lands=list(res[n:]))


def _pair_add(mine, theirs, core, name):
    chips, _, rows, w = mine.shape
    tm = _pick(rows, (256, 128, 64, 32, 16))

    def body(core_ref, a_ref, b_ref, o_ref):
        o_ref[...] = (a_ref[...].astype(F32) + b_ref[...].astype(F32)).astype(o_ref.dtype)

    return pl.pallas_call(
        body, name=name, out_shape=jax.ShapeDtypeStruct(theirs.shape, theirs.dtype),
        grid_spec=pltpu.PrefetchScalarGridSpec(
            num_scalar_prefetch=1, grid=(chips, rows // tm),
            in_specs=[pl.BlockSpec((None, None, tm, w), lambda q, i, core_ref: (q, core_ref[0], i, 0)),
                      pl.BlockSpec((None, tm, w), lambda q, i, core_ref: (q, i, 0))],
            out_specs=pl.BlockSpec((None, tm, w), lambda q, i, core_ref: (q, i, 0))),
        compiler_params=_params())(core, mine, theirs)


def _adamw(w, m, v, gparts, name, after=None):
    R, C = w.shape
    tm = _pick(R, (256, 128, 64, 32, 16, 8))
    order = [] if after is None else [after]

    def body(w_ref, m_ref, v_ref, g_ref, *rest):
        go, do, mo, vo = rest[len(order):]
        g = g_ref[0].astype(F32)
        for j in range(1, gparts.shape[0]):
            g = g + g_ref[j].astype(F32)
        mn = ADAM_B1 * m_ref[...] + (1.0 - ADAM_B1) * g
        vn = ADAM_B2 * v_ref[...] + (1.0 - ADAM_B2) * (g * g)
        m_hat = mn / (1.0 - ADAM_B1 ** ADAM_STEP)
        v_hat = vn / (1.0 - ADAM_B2 ** ADAM_STEP)
        go[...] = g
        do[...] = -ADAM_LR * (m_hat / (jnp.sqrt(v_hat) + ADAM_EPS) + ADAM_WD * w_ref[...])
        mo[...] = mn
        vo[...] = vn

    row = pl.BlockSpec((tm, C), lambda i: (i, 0))
    return pl.pallas_call(
        body, name=name, grid=(R // tm,),
        in_specs=[row, row, row, pl.BlockSpec((gparts.shape[0], tm, C), lambda i: (0, i, 0))] + [pl.BlockSpec(memory_space=pl.ANY)] * len(order),
        out_specs=[row] * 4, out_shape=[jax.ShapeDtypeStruct((R, C), F32)] * 4, compiler_params=_params())(w, m, v, gparts, *order)


def _pack(arrays):
    parts = []
    for a in arrays:
        f = a.reshape(1, -1)
        pad = _ceil_to(f.shape[1], SUBLANE * LANE) - f.shape[1]
        f = jnp.concatenate([f, jnp.zeros((1, pad), f.dtype)], axis=1) if pad else f
        parts.append(f.reshape(-1, LANE))
    rows = sum(p.shape[0] for p in parts)
    pad = _ceil_to(rows, 64) - rows
    return jnp.concatenate(parts + ([jnp.zeros((pad, LANE), parts[0].dtype)] if pad else []), axis=0)


def _unpack(buf, shapes):
    out, row = [], 0
    for s in shapes:
        size = 1
        for d in s:
            size *= d
        rows = _ceil_to(size, SUBLANE * LANE) // LANE
        out.append(buf[row:row + rows].reshape(1, -1)[:, :size].reshape(s))
        row += rows
    return out


def kernel(x, norm_mix_g, w_in, shift_mu, w0, w_lora_up, a0, a_lora_up, g_lora_up, k_k, k_a, r_k, lnx_g, lnx_b, w_proj_rwkv, sgu_ln_g, sgu_ln_b, sgu_w, sgu_b, w_proj_sgu, w_out, norm_ffn_g, w_ffn_gate, w_ffn_up, w_ffn_down, norm_final_g, loss_target, m_norm_mix_g, m_w_in, m_shift_mu, m_w0, m_w_lora_up, m_a0, m_a_lora_up, m_g_lora_up, m_k_k, m_k_a, m_r_k, m_lnx_g, m_lnx_b, m_w_proj_rwkv, m_sgu_ln_g, m_sgu_ln_b, m_sgu_w, m_sgu_b, m_w_proj_sgu, m_w_out, m_norm_ffn_g, m_w_ffn_gate, m_w_ffn_up, m_w_ffn_down, m_norm_final_g, v_norm_mix_g, v_w_in, v_shift_mu, v_w0, v_w_lora_up, v_a0, v_a_lora_up, v_g_lora_up, v_k_k, v_k_a, v_r_k, v_lnx_g, v_lnx_b, v_w_proj_rwkv, v_sgu_ln_g, v_sgu_ln_b, v_sgu_w, v_sgu_b, v_w_proj_sgu, v_w_out, v_norm_ffn_g, v_w_ffn_gate, v_w_ffn_up, v_w_ffn_down, v_norm_final_g):
    weights = dict(norm_mix_g=norm_mix_g, w_in=w_in, shift_mu=shift_mu, w0=w0, w_lora_up=w_lora_up, a0=a0, a_lora_up=a_lora_up,
                   g_lora_up=g_lora_up, k_k=k_k, k_a=k_a, r_k=r_k, lnx_g=lnx_g, lnx_b=lnx_b, w_proj_rwkv=w_proj_rwkv,
                   sgu_ln_g=sgu_ln_g, sgu_ln_b=sgu_ln_b, sgu_w=sgu_w, sgu_b=sgu_b, w_proj_sgu=w_proj_sgu, w_out=w_out,
                   norm_ffn_g=norm_ffn_g, w_ffn_gate=w_ffn_gate, w_ffn_up=w_ffn_up, w_ffn_down=w_ffn_down, norm_final_g=norm_final_g)
    m_in = dict(norm_mix_g=m_norm_mix_g, w_in=m_w_in, shift_mu=m_shift_mu, w0=m_w0, w_lora_up=m_w_lora_up, a0=m_a0,
                a_lora_up=m_a_lora_up, g_lora_up=m_g_lora_up, k_k=m_k_k, k_a=m_k_a, r_k=m_r_k, lnx_g=m_lnx_g, lnx_b=m_lnx_b,
                w_proj_rwkv=m_w_proj_rwkv, sgu_ln_g=m_sgu_ln_g, sgu_ln_b=m_sgu_ln_b, sgu_w=m_sgu_w, sgu_b=m_sgu_b,
                w_proj_sgu=m_w_proj_sgu, w_out=m_w_out, norm_ffn_g=m_norm_ffn_g, w_ffn_gate=m_w_ffn_gate, w_ffn_up=m_w_ffn_up,
                w_ffn_down=m_w_ffn_down, norm_final_g=m_norm_final_g)
    v_in = dict(norm_mix_g=v_norm_mix_g, w_in=v_w_in, shift_mu=v_shift_mu, w0=v_w0, w_lora_up=v_w_lora_up, a0=v_a0,
                a_lora_up=v_a_lora_up, g_lora_up=v_g_lora_up, k_k=v_k_k, k_a=v_k_a, r_k=v_r_k, lnx_g=v_lnx_g, lnx_b=v_lnx_b,
                w_proj_rwkv=v_w_proj_rwkv, sgu_ln_g=v_sgu_ln_g, sgu_ln_b=v_sgu_ln_b, sgu_w=v_sgu_w, sgu_b=v_sgu_b,
                w_proj_sgu=v_w_proj_sgu, w_out=v_w_out, norm_ffn_g=v_norm_ffn_g, w_ffn_gate=v_w_ffn_gate, w_ffn_up=v_w_ffn_up,
                w_ffn_down=v_w_ffn_down, norm_final_g=v_norm_final_g)
    names = list(weights)
    col_sharded = ("w_in", "w_lora_up", "a_lora_up", "g_lora_up", "w_proj_rwkv", "w_proj_sgu", "w_ffn_gate", "w_ffn_up")
    row_sharded = ("w_out", "w_ffn_down")
    sharded = [n for n in names if n in col_sharded or n in row_sharded]
    small = [n for n in names if n not in sharded]

    xs, tgt = x[0], loss_target[0]
    T, D = xs.shape
    RW = w0.shape[1]
    H = RW // HEAD
    SW = sgu_ln_g.shape[1]
    G = sgu_w.shape[1]
    assert 2 * SW == D, "the projection layout takes the SGU part to be as wide as a gate"
    lay = _rwkv_layout(RW, w_lora_up.shape[1], a_lora_up.shape[1], g_lora_up.shape[1], D)
    _, pw, _, rcp = lay
    icp = rcp + 3 * D
    b_ga, b_gb, b_z = rcp // D, rcp // D + 1, rcp // D + 2

    gather_groups = dict(win=["w_in", "w_lora_up", "a_lora_up", "g_lora_up"], proj=["w_proj_rwkv", "w_proj_sgu", "w_out"],
                         ffn_gate_up=["w_ffn_gate", "w_ffn_up"], ffn_down=["w_ffn_down"])
    handles, gather_token = _exchange_start([[(weights[n][0].astype(BF16), True) for n in grp] for grp in gather_groups.values()],
                                            "gather_start", rels=SIBLING + SAME_CORE)
    gather = dict(zip(gather_groups, handles))
    full = {}
    relay_tokens = {}
    joined = lambda g: g.transpose(1, 0, 2).reshape(g.shape[1], -1)

    def relay_weights(key, after):
        arrived = _exchange_wait(gather[key], after, "gather_wait_ici_" + key, rels=SAME_CORE, local=False)
        gather[key], relay_tokens[key] = _relay_start(arrived, "gather_relay_" + key)

    def take_weights(key, after):
        done = _exchange_wait(gather[key], after, "gather_wait_d2d_" + key, rels=SIBLING)
        for n, g in zip(gather_groups[key], done["lands"]):
            full[n] = g.reshape(-1, g.shape[2]) if n in row_sharded else g

    packed = [_pack([d[n] for n in small] + [gather_token]) for d in (weights, m_in, v_in)]
    n1 = _rms_fwd(xs, norm_mix_g, "rms_mix", deps=[gather_token, *packed])
    relay_weights("win", n1)
    take_weights("win", relay_tokens["win"])
    W_in = _w_in_to_proj(full["w_in"], lay, D, "w_in_layout")
    lora = [_pad_rows(joined(full[n]), rows) for n, rows in zip(("w_lora_up", "a_lora_up", "g_lora_up"), pw[3:])]
    mu_p = _pad_rwkv_cols(shift_mu, lay)
    rsmall = [w0, a0, k_k, k_a]
    hp = [lnx_g.reshape(H, 1, HEAD), lnx_b.reshape(H, 1, HEAD), r_k.reshape(H, 1, HEAD)]
    ws = sgu_w[0]
    bexp = jnp.repeat(sgu_b[0].T, SGU_GROUP, axis=1)
    gf = norm_final_g.reshape(1, D)

    proj = _matmul(n1, W_in, mode="nn", out_dtype=F32, name="proj_in")
    ga, gb = (proj, D, b_ga), (proj, D, b_gb)
    r_h, lw_h, k2_h, v_h, aa_h, bb_h, g_h = _rwkv_pre(proj, mu_p, rsmall, lora, lay, "rwkv_pre")
    wkv_in = [r_h, lw_h, k2_h, v_h, aa_h, bb_h]
    y_h, states = _wkv_fwd(*wkv_in, "wkv_fwd")
    relay_weights("proj", y_h)
    ya = _head_post(y_h, r_h, k2_h, v_h, g_h, hp, "head_post", deps=[relay_tokens["proj"]])
    relay_weights("ffn_gate_up", ya)
    yb = _sgu_fwd(proj, b_z, sgu_ln_g, sgu_ln_b, ws, bexp, "sgu_fwd")
    take_weights("proj", ya)
    pa = _matmul(ya, full["w_proj_rwkv"], mode="nn", out_dtype=F32, name="proj_a", deps=[relay_tokens["ffn_gate_up"]])

    def merge_fn(pb_v, pa_v, ga_v, gb_v):
        return pb_v, _sigmoid(ga_v) * pa_v + _sigmoid(gb_v) * pb_v
    pb, merged = _matmul(yb, full["w_proj_sgu"], mode="nn", name="proj_b_merge",
                         epi=(merge_fn, [pa, (proj, b_ga * D), (proj, b_gb * D)], [F32, BF16]))
    h1 = _matmul(merged, full["w_out"], mode="nn", out_dtype=F32, name="out_proj", add=xs)
    n2 = _rms_fwd(h1, norm_ffn_g, "rms_ffn")
    relay_weights("ffn_down", n2)
    take_weights("ffn_gate_up", n2)

    def act_fn(gt_v, up_v):
        return gt_v, up_v, gt_v * _sigmoid(gt_v) * up_v
    gt, up, act = _matmul(n2, full["w_ffn_gate"], b2=full["w_ffn_up"], mode="nn", name="ffn_gate_up_act", out_blocks=N_DEV,
                          epi=(act_fn, [], [BF16, BF16, BF16]), deps=[relay_tokens["ffn_down"]])
    take_weights("ffn_down", act)
    h2 = _matmul(act, full["w_ffn_down"], mode="nn", out_dtype=F32, name="ffn_down", add=h1)

    def final_fn(rv, pv):
        (h_v, t_v), (g_v,) = rv, pv
        r = lax.rsqrt(_mean(h_v * h_v) + RMS_EPS)
        yn = h_v * r
        e = yn * g_v - t_v
        loss = 0.5 * jnp.sum(_mean(e * e))
        dout = e * (1.0 / D)
        dyg = dout * g_v
        dh = r * (dyg - yn * _mean(dyg * yn))
        return [dh, dh], [jnp.full((1, LANE), loss, F32), _colsum(dout * yn)]
    dh2, dh2_bf, loss_part, d_gf = _rowwise(final_fn, [h2, tgt], [gf], [(D, F32), (D, BF16)], [(1, LANE), (1, D)], name="final_loss")

    grads = {}

    def start_scatter(group, name, extra=()):
        blocks = [(grads[n].reshape(N_DEV, -1, grads[n].shape[1]) if n in row_sharded else grads[n], False) for n in group]
        (handle,), token = _exchange_start([blocks + list(extra)], name)
        return handle, token

    def dact_fn(d_v, gt_v, up_v):
        gt_v, up_v = gt_v.astype(F32), up_v.astype(F32)
        s = _sigmoid(gt_v)
        return d_v * up_v * (s * (1.0 + gt_v * (1.0 - s))), d_v * gt_v * s
    dgt, dup = _matmul(dh2_bf, full["w_ffn_down"], mode="nt", name="d_ffn_act", out_blocks=N_DEV,
                       epi=(dact_fn, [gt, up], [BF16, BF16]))
    scatter_groups = dict(ffn_down=["w_ffn_down"], ffn_gate=["w_ffn_gate"], ffn_up=["w_ffn_up"],
                          mid=["w_out", "w_proj_rwkv", "w_proj_sgu"], last=["w_in", "w_lora_up", "a_lora_up", "g_lora_up"])
    scatters = {}
    grads["w_ffn_down"] = _matmul(act, dh2_bf, mode="tn", out_dtype=BF16, name="dw_ffn_down")
    scatters["ffn_down"], token = start_scatter(scatter_groups["ffn_down"], "scatter_start_ffn_down")
    dn2 = _matmul(dgt, full["w_ffn_gate"], mode="nt", out_dtype=F32, name="dn2_gate", deps=[token])
    grads["w_ffn_gate"] = _matmul(n2, dgt, mode="tn", out_dtype=BF16, name="dw_ffn_gate", out_blocks=N_DEV)
    scatters["ffn_gate"], token = start_scatter(scatter_groups["ffn_gate"], "scatter_start_ffn_gate")
    grads["w_ffn_up"] = _matmul(n2, dup, mode="tn", out_dtype=BF16, name="dw_ffn_up", out_blocks=N_DEV, deps=[token])
    scatters["ffn_up"], token = start_scatter(scatter_groups["ffn_up"], "scatter_start_ffn_up")
    dn2 = _matmul(dup, full["w_ffn_up"], mode="nt", out_dtype=F32, name="dn2_up", add=dn2, deps=[token])
    dh1, dh1_bf, d_g2 = _rms_bwd(dn2, h1, dh2, norm_ffn_g, "rms_ffn_bwd")
    dmerged = _matmul(dh1_bf, full["w_out"], mode="nt", out_dtype=F32, name="d_merged")
    grads["w_out"] = _matmul(merged, dh1_bf, mode="tn", out_dtype=BF16, name="dw_out")

    def dmerge_fn(rv, pv):
        d_v, ga_v, gb_v, pa_v, pb_v = rv
        sa, sb = _sigmoid(ga_v), _sigmoid(gb_v)
        dgates = jnp.concatenate([d_v * pa_v * sa * (1.0 - sa), d_v * pb_v * sb * (1.0 - sb)], axis=1)
        return [dgates, d_v * sa, d_v * sb], []
    dproj, dpa, dpb = _rowwise(dmerge_fn, [dmerged, ga, gb, pa, pb], [],
                               [(2 * D, BF16, icp, b_ga // 2, None), (D, BF16), (D, BF16)], [], name="d_merge")
    dya = _matmul(dpa, full["w_proj_rwkv"], mode="nt", out_dtype=F32, name="d_ya")
    dyb = _matmul(dpb, full["w_proj_sgu"], mode="nt", out_dtype=F32, name="d_yb")
    grads["w_proj_rwkv"] = _matmul(ya, dpa, mode="tn", out_dtype=BF16, name="dw_proj_a", out_blocks=N_DEV)
    grads["w_proj_sgu"] = _matmul(yb, dpb, mode="tn", out_dtype=BF16, name="dw_proj_b", out_blocks=N_DEV)
    scatters["mid"], token_mid = start_scatter(scatter_groups["mid"], "scatter_start_mid")
    dproj, d_lng, d_lnb, d_ws, d_bs = _sgu_bwd(proj, b_z, dyb, sgu_ln_g, sgu_ln_b, ws, bexp, dproj, "sgu_bwd")

    dy_h, dr1, dk1, dv1, dg_h, d_lnxg, d_lnxb, d_rk = _head_post_bwd(dya, y_h, r_h, k2_h, v_h, g_h, hp, "head_post_bwd",
                                                                     deps=[token_mid])
    dr2, dlw_h, dk2b, dv2, daa, dbb = _wkv_bwd(*wkv_in, states, dy_h, "wkv_bwd")
    dps, d_mu, d_w0, d_a0, d_kk, d_ka, d_wlw, d_wla, d_wlg = _rwkv_pre_bwd(
        proj, mu_p, rsmall, lora, [dr1, dr2, dk1, dk2b, dv1, dv2, dlw_h, daa, dbb, dg_h], lay, "rwkv_pre_bwd")
    dproj = _shift_bwd(dps, mu_p, dproj, "shift_bwd")
    split = lambda g: g.reshape(g.shape[0], N_DEV, -1).transpose(1, 0, 2)
    grads["w_in"] = _dw_in_from_proj(_matmul(n1, dproj, mode="tn", out_dtype=BF16, name="dw_in"), lay, D, w_in.shape[2], "dw_in_layout")
    grads["w_lora_up"] = split(d_wlw[:w_lora_up.shape[1]].astype(BF16))
    grads["a_lora_up"] = split(d_wla[:a_lora_up.shape[1]].astype(BF16))
    grads["g_lora_up"] = split(d_wlg[:g_lora_up.shape[1]].astype(BF16))
    out = {}

    def update_group(key, after):
        handle = scatters[key]
        parts = _exchange_wait(handle, after, "scatter_wait_" + key, rels=SAME_CORE if handle["chips"] else ALL_PEERS)["lands"]
        for n, part in zip(scatter_groups[key], parts):
            res = _adamw(weights[n][0], m_in[n][0], v_in[n][0], part, "adamw_" + n, after=after)
            out[n] = [t.reshape(weights[n].shape) for t in res]
            after = res[0]
        return after

    swap, token_swap = _sibling_swap([grads[n] for n in scatter_groups["last"]], None, None, "scatter_last_swap_start")
    after = update_group("ffn_gate", update_group("ffn_down", token_swap))
    swap = _sibling_swap(None, swap, after, "scatter_last_swap_wait")
    core = lax.axis_index("c").astype(jnp.int32).reshape(1)
    chip_sums = [_pair_add(mine, theirs, core, "scatter_last_add_" + n)
                 for n, mine, theirs in zip(scatter_groups["last"], swap["srcs"], swap["lands"])]
    (scatters["last"],), token_in = _exchange_start([[(s, False) for s in chip_sums]], "scatter_start_last", rels=SAME_CORE, chips=True)
    dn1 = _matmul(dproj, W_in, mode="nt", out_dtype=F32, name="dn1", deps=[token_in])
    dx, _, d_g1 = _rms_bwd(dn1, xs, dh1, norm_mix_g, "rms_mix_bwd")
    small_grads = dict(norm_mix_g=d_g1, shift_mu=_unpad_rwkv_cols(d_mu, lay), w0=d_w0, a0=d_a0, k_k=d_kk, k_a=d_ka, r_k=d_rk,
                       lnx_g=d_lnxg, lnx_b=d_lnxb, sgu_ln_g=d_lng, sgu_ln_b=d_lnb, sgu_w=d_ws, sgu_b=d_bs[:, :G].T,
                       norm_ffn_g=d_g2, norm_final_g=d_gf)
    (gather_small,), after = _exchange_start([[(_pack([small_grads[n] for n in small] + [jnp.zeros_like(gather_token)]), True)]],
                                             "gather_small_start")
    for key in ("ffn_up", "mid", "last"):
        after = update_group(key, after)
    small_parts = _exchange_wait(gather_small, after, "gather_small_wait")["lands"][0]
    res = _adamw(*packed, small_parts, "adamw_small")
    unpacked = [_unpack(t, [weights[n].shape for n in small]) for t in res]
    for i, n in enumerate(small):
        out[n] = [u[i] for u in unpacked]

    loss = lax.psum(loss_part[0, 0], ("x", "y", "c"))
    return (loss, dx[None], *[out[n][0] for n in names], *[out[n][1] for n in names],
            *[out[n][2] for n in names], *[out[n][3] for n in names])
```

```python
import jax
import jax.numpy as jnp
from jax import lax
from jax.experimental import pallas as pl
from jax.experimental.pallas import tpu as pltpu

F32 = jnp.float32
BF16 = jnp.bfloat16

N_DEV = 8
LANE = 128
SUBLANE = 8
HEAD = 64
SGU_CHUNK = 128
SGU_GROUP = 128
WKV_CHUNK = 64
RMS_EPS = 1e-6
LN_EPS = 1e-5
LNX_EPS = 64e-5
ADAM_LR, ADAM_B1, ADAM_B2, ADAM_EPS, ADAM_WD, ADAM_STEP = 0.001, 0.9, 0.999, 1e-08, 0.01, 10
VMEM_LIMIT_BYTES = 48 * 1024 * 1024
_SQRT_HALF = 0.7071067811865476
_INV_SQRT_2PI = 0.3989422804014327


def _pick(n, cands):
    for c in cands:
        if n % c == 0:
            return c
    return n


def _ceil_to(n, m):
    return -(-n // m) * m


def _params():
    return pltpu.CompilerParams(vmem_limit_bytes=VMEM_LIMIT_BYTES)


def _tile(n, cap):
    best = 0
    for d in range(LANE, min(n, cap) + 1, LANE):
        if n % d == 0:
            best = d
    return best or n


def _matmul_tiles(M, N, K, a_bytes, b_bytes, o_bytes, has_add, forced):
    tm = forced.get("m") or _tile(M, 1024)
    tn = forced.get("n") or _tile(N, 1024)
    tk = forced.get("k") or _tile(K, 2048)

    def vmem(tm, tn, tk):
        acc = tm * tn * 4 if tk < K else 0
        return 2 * (tm * tk * a_bytes + tk * tn * b_bytes + tm * tn * (o_bytes + (4 if has_add else 0))) + acc

    while vmem(tm, tn, tk) > (VMEM_LIMIT_BYTES * 3) // 4:
        if "k" not in forced and tk > 512 and _tile(K, tk // 2) < tk:
            tk = _tile(K, tk // 2)
        elif "m" not in forced and _tile(M, tm // 2) < tm:
            tm = _tile(M, tm // 2)
        else:
            break
    return tm, tn, tk


def _matmul(a, b, *, mode, out_dtype=F32, name, add=None, deps=(), out_blocks=0, epi=None, b2=None):
    def view(x):
        return (x.shape[1], x.shape[0] * x.shape[2], x.shape[2]) if x.ndim == 3 else (x.shape[0], x.shape[1], 0)

    (ar, ac, aw), (br, bc, bw) = view(a), view(b)
    a_col, b_col = {"nn": ("k", "n"), "nt": ("k", "k"), "tn": ("m", "n")}[mode]
    if mode == "nn":
        M, K, K2, N = ar, ac, br, bc
    elif mode == "nt":
        M, K, N, K2 = ar, ac, br, bc
    else:
        K, M, K2, N = ar, ac, br, bc
    assert K == K2, (a.shape, b.shape, mode)
    forced = {}
    for dim, w in ((a_col, aw), (b_col, bw), ("n", N // out_blocks if out_blocks else 0)):
        if w:
            assert forced.get(dim, w) == w
            forced[dim] = w
    has_add = add is not None
    tile_bytes = (sum(jnp.dtype(d).itemsize for d in epi[2]) + sum((e[0] if isinstance(e, tuple) else e).dtype.itemsize for e in epi[1])
                  if epi is not None else jnp.dtype(out_dtype).itemsize)
    tm, tn, tk = _matmul_tiles(M, N, K, a.dtype.itemsize, b.dtype.itemsize, tile_bytes, has_add, forced)
    kb = 1
    if "k" in forced and mode != "tn":
        lanes_ok = all(w or tk % LANE == 0 for w in (aw, bw if mode == "nt" else 1))
        kb = next(c for c in (4, 2, 1) if (K // tk) % c == 0 and (c == 1 or (lanes_ok and c * tk <= 1536)))
    nk = K // (tk * kb)
    dn = {"nn": (((1,), (0,)), ((), ())), "nt": (((1,), (1,)), ((), ())), "tn": (((0,), (0,)), ((), ()))}[mode]
    pick = {"m": lambda i, j, k: i, "n": lambda i, j, k: j, "k": lambda i, j, k: k}
    size = {"m": tm, "n": tn, "k": tk}

    def spec(blocked, row_dim, col_dim):
        rf, cf = pick[row_dim], pick[col_dim]
        reps = {d: (kb if d == "k" else 1) for d in (row_dim, col_dim)}
        if blocked:
            lead = kb if col_dim == "k" and kb > 1 else None
            return pl.BlockSpec((lead, size[row_dim], size[col_dim]), lambda i, j, k: (cf(i, j, k), rf(i, j, k), 0))
        return pl.BlockSpec((size[row_dim] * reps[row_dim], size[col_dim] * reps[col_dim]), lambda i, j, k: (rf(i, j, k), cf(i, j, k)))

    def k_part(ref, blocked, k_on_rows, j):
        if kb == 1:
            return ref[...]
        if blocked:
            return ref[j]
        return ref[j * tk:(j + 1) * tk, :] if k_on_rows else ref[:, j * tk:(j + 1) * tk]

    a_spec = spec(aw, "k" if mode == "tn" else "m", a_col)
    b_spec = spec(bw, "n" if mode == "nt" else "k", b_col)
    o_spec = spec(out_blocks, "m", "n")
    epi_fn, epi_ins, epi_dtypes = epi if epi is not None else (None, [], [out_dtype])
    epi_ins = [e if isinstance(e, tuple) else (e, None) for e in epi_ins]
    n_epi = len(epi_ins)
    twin = b2 is not None
    assert not twin or (nk == 1 and kb == 1 and epi is not None and b2.shape == b.shape)
    n_in = 2 + twin + has_add + n_epi + len(deps)
    n_out = len(epi_dtypes)

    def body(*refs):
        a_ref, b_ref = refs[0], refs[1]
        add_ref = refs[2 + twin] if has_add else None
        epi_refs = refs[2 + twin + has_add:2 + twin + has_add + n_epi]
        o_refs = refs[n_in:n_in + n_out]
        part = None
        for q in range(kb):
            a_q = k_part(a_ref, aw and a_col == "k", False, q)
            b_q = k_part(b_ref, bw and b_col == "k", mode == "nn", q)
            prod = lax.dot_general(a_q.astype(BF16), b_q.astype(BF16), dn, preferred_element_type=F32)
            part = prod if part is None else part + prod
        second = [lax.dot_general(a_ref[...].astype(BF16), refs[2][...].astype(BF16), dn, preferred_element_type=F32)] if twin else []

        def finish(res):
            outs = epi_fn(res, *second, *[e[...] for e in epi_refs]) if epi_fn is not None else (res,)
            for o_ref, val in zip(o_refs, outs):
                o_ref[...] = val.astype(o_ref.dtype)

        if nk == 1:
            finish(part + add_ref[...] if has_add else part)
            return
        acc_ref = refs[-1]
        kk = pl.program_id(2)

        @pl.when(kk == 0)
        def _():
            acc_ref[...] = part + add_ref[...] if has_add else part

        @pl.when(kk > 0)
        def _():
            acc_ref[...] += part

        @pl.when(kk == nk - 1)
        def _():
            finish(acc_ref[...])

    def epi_spec(arr, off):
        if off is None:
            return o_spec
        assert off % tn == 0
        return pl.BlockSpec((tm, tn), lambda i, j, k: (i, j + off // tn))

    ins = [a, b] + ([b2] if twin else []) + ([add] if has_add else []) + [arr for arr, _ in epi_ins] + list(deps)
    in_specs = ([a_spec, b_spec] + ([b_spec] if twin else []) + ([o_spec] if has_add else []) + [epi_spec(arr, off) for arr, off in epi_ins]
                + [pl.BlockSpec(d.shape, lambda i, j, k, nd=d.ndim: (0,) * nd) for d in deps])
    o_shape = (out_blocks, M, tn) if out_blocks else (M, N)
    res = pl.pallas_call(
        body, name=name, grid=(M // tm, N // tn, nk), in_specs=in_specs, out_specs=[o_spec] * n_out,
        out_shape=[jax.ShapeDtypeStruct(o_shape, dt) for dt in epi_dtypes],
        scratch_shapes=[pltpu.VMEM((tm, tn), F32)] if nk > 1 else [],
        compiler_params=_params())(*ins)
    return res[0] if epi is None else list(res)


def _rowwise(fn, rows, pars, row_outs, acc_outs, *, name, tm=256, deps=()):
    rows = [r if isinstance(r, tuple) else (r, r.shape[1], 0) for r in rows]
    row_outs = [o if len(o) == 5 else (o[0], o[1], o[0], 0, None) for o in row_outs]
    aliased = [(k, o[4]) for k, o in enumerate(row_outs) if o[4] is not None]
    R = rows[0][0].shape[0]
    if max(w for _, w, _ in rows) > 4096:
        tm = tm // 2
    tm = min(tm, R)
    assert R % tm == 0
    nr, npar = len(rows), len(pars)
    nro = len(row_outs)
    n_in = nr + npar + len(deps) + len(aliased)

    def body(*refs):
        rv = [r[...] for r in refs[:nr]]
        pv = [p[...] for p in refs[nr:nr + npar]]
        outs = refs[n_in:]
        ro, ao = fn(rv, pv)
        first = pl.program_id(0) == 0
        for o_ref, val in zip(outs[:nro], ro):
            o_ref[...] = val.astype(o_ref.dtype)

        @pl.when(first)
        def _():
            for o_ref, val in zip(outs[nro:], ao):
                o_ref[...] = val

        @pl.when(jnp.logical_not(first))
        def _():
            for o_ref, val in zip(outs[nro:], ao):
                o_ref[...] += val

    in_specs = ([pl.BlockSpec((tm, w), lambda i, cb=cb: (i, cb)) for _, w, cb in rows]
                + [pl.BlockSpec(p.shape, lambda i, nd=p.ndim: (0,) * nd) for p in list(pars) + list(deps)]
                + [pl.BlockSpec(memory_space=pl.ANY)] * len(aliased))
    out_shape = ([jax.ShapeDtypeStruct((R, full), dt) for _, dt, full, _, _ in row_outs]
                 + [jax.ShapeDtypeStruct(s, F32) for s in acc_outs])
    out_specs = ([pl.BlockSpec((tm, f), lambda i, cb=cb: (i, cb)) for f, _, _, cb, _ in row_outs]
                 + [pl.BlockSpec(s, lambda i, nd=len(s): (0,) * nd) for s in acc_outs])
    res = pl.pallas_call(body, name=name, grid=(R // tm,), in_specs=in_specs, out_specs=out_specs, out_shape=out_shape,
                         input_output_aliases={n_in - len(aliased) + q: k for q, (k, _) in enumerate(aliased)},
                         compiler_params=_params())(*[r for r, _, _ in rows], *pars, *deps, *[buf for _, buf in aliased])
    return list(res)


def _bdot(a, b, mode="nn"):
    dn = {"nn": (((1,), (0,)), ((), ())), "nt": (((1,), (1,)), ((), ())), "tn": (((0,), (0,)), ((), ()))}[mode]
    return lax.dot_general(a.astype(BF16), b.astype(BF16), dn, preferred_element_type=F32)


def _sigmoid(x):
    return jax.nn.sigmoid(x)


def _softplus(x):
    return jnp.maximum(x, 0.0) + jnp.log1p(jnp.exp(-jnp.abs(x)))


def _gelu(z):
    return 0.5 * z * (1.0 + lax.erf(z * _SQRT_HALF))


def _gelu_grad(z):
    return 0.5 * (1.0 + lax.erf(z * _SQRT_HALF)) + z * jnp.exp(-0.5 * z * z) * _INV_SQRT_2PI


def _mean(x):
    return jnp.mean(x, axis=-1, keepdims=True)


def _colsum(x):
    return jnp.sum(x, axis=0, keepdims=True)


def _rms_fwd(x, g, name, deps=()):
    def fn(rv, pv):
        (xv,), (gv,) = rv, pv
        r = lax.rsqrt(_mean(xv * xv) + RMS_EPS)
        return [xv * r * gv], []
    return _rowwise(fn, [x], [g], [(x.shape[1], BF16)], [], name=name, deps=deps)[0]


def _rms_bwd(dn, x, dres, g, name, deps=()):
    def fn(rv, pv):
        (dnv, xv, drv), (gv,) = rv, pv
        r = lax.rsqrt(_mean(xv * xv) + RMS_EPS)
        yn = xv * r
        dyg = dnv * gv
        dx = drv + r * (dyg - yn * _mean(dyg * yn))
        return [dx, dx], [_colsum(dnv * yn)]
    D = x.shape[1]
    return _rowwise(fn, [dn, x, dres], [g], [(D, F32), (D, BF16)], [(1, D)], name=name, deps=deps)


def _rwkv_layout(RW, Lw, La, Lg, D):
    widths = [RW, RW, RW, Lw, La, Lg]
    pw = [_ceil_to(w, LANE) for w in widths]
    pw[5] += _ceil_to(sum(pw), 2 * D) - sum(pw)
    offs = [sum(pw[:i]) for i in range(6)]
    return widths, pw, offs, sum(pw)


def _pad_rwkv_cols(a, lay):
    widths, pw, _, _ = lay
    pieces, src = [], 0
    for w, p in zip(widths, pw):
        pieces.append(a[:, src:src + w])
        if p > w:
            pieces.append(jnp.zeros((a.shape[0], p - w), a.dtype))
        src += w
    return jnp.concatenate(pieces, axis=1)


def _unpad_rwkv_cols(a, lay):
    widths, _, offs, _ = lay
    return jnp.concatenate([a[:, o:o + w] for o, w in zip(offs, widths)], axis=1)


def _proj_pieces(lay, D, cs):
    widths, _, offs, rcp = lay
    rc = sum(widths)
    segs = [(sum(widths[:j]), widths[j], offs[j]) for j in range(6)] + [(rc, D, rcp + 2 * D), (rc + D, D, rcp), (rc + 2 * D, D, rcp + D)]
    pieces = []
    for start, width, dst in segs:
        n = start
        while n < start + width:
            d, off = divmod(n, cs)
            take = min(cs - off, start + width - n)
            pieces.append((d, off, dst + n - start, take))
            n += take
    return pieces


def _w_in_to_proj(g, lay, D, name):
    nb, rows, cs = g.shape
    icp = lay[3] + 3 * D
    pieces = _proj_pieces(lay, D, cs)
    tm = _pick(rows, (256, 128, 64, 32, 16))

    def body(i_ref, o_ref):
        o_ref[...] = jnp.zeros_like(o_ref)
        for d, src, dst, w in pieces:
            o_ref[:, dst:dst + w] = i_ref[d, :, src:src + w]

    return pl.pallas_call(
        body, name=name, grid=(rows // tm,), in_specs=[pl.BlockSpec((nb, tm, cs), lambda i: (0, i, 0))],
        out_specs=pl.BlockSpec((tm, icp), lambda i: (i, 0)), out_shape=jax.ShapeDtypeStruct((rows, icp), g.dtype),
        compiler_params=_params())(g)


def _dw_in_from_proj(a, lay, D, cs, name):
    rows, icp = a.shape
    pieces = _proj_pieces(lay, D, cs)
    tm = _pick(rows, (256, 128, 64, 32, 16))

    def body(i_ref, o_ref):
        for d, src, dst, w in pieces:
            o_ref[d, :, src:src + w] = i_ref[:, dst:dst + w]

    return pl.pallas_call(
        body, name=name, grid=(rows // tm,), in_specs=[pl.BlockSpec((tm, icp), lambda i: (i, 0))],
        out_specs=pl.BlockSpec((N_DEV, tm, cs), lambda i: (0, i, 0)), out_shape=jax.ShapeDtypeStruct((N_DEV, rows, cs), a.dtype),
        compiler_params=_params())(a)


def _pad_rows(a, rows):
    return a if a.shape[0] == rows else jnp.concatenate([a, jnp.zeros((rows - a.shape[0], a.shape[1]), a.dtype)], axis=0)


def _token_shift(p, halo, mu, i):
    tm = p.shape[0]
    hid = lax.broadcasted_iota(jnp.int32, (SUBLANE, 1), 0)
    before = jnp.sum(jnp.where(hid == SUBLANE - 1, halo, 0.0), axis=0, keepdims=True)
    before = jnp.where(i == 0, 0.0, before)
    rid = lax.broadcasted_iota(jnp.int32, (tm, 1), 0)
    prev = jnp.where(rid == 0, before, pltpu.roll(p, 1, 0))
    d = prev - p
    return p + d * mu, d


def _rwkv_math(ps, w0, a0, k_k, k_a, wlw, wla, wlg, lay):
    _, pw, offs, _ = lay
    r, k, v, xw, xa, xg = (ps[:, offs[j]:offs[j] + pw[j]] for j in range(6))
    tw = jnp.tanh(xw)
    ww = w0 + _bdot(tw, wlw)
    lw = -jnp.exp(-_softplus(-ww) - 0.5)
    a = _sigmoid(a0 + _bdot(xa, wla))
    sg = _sigmoid(xg)
    g = _bdot(sg, wlg)
    return dict(r=r, k=k, v=v, xa=xa, tw=tw, ww=ww, lw=lw, a=a, sg=sg, g=g, kkp=k * k_k, k2=k * (1.0 + (a - 1.0) * k_a))


def _halo_specs(T, tm, width, after):
    hb = tm // SUBLANE
    last = T // SUBLANE - 1
    if after:
        return pl.BlockSpec((SUBLANE, width), lambda i: (jnp.minimum((i + 1) * hb, last), 0))
    return pl.BlockSpec((SUBLANE, width), lambda i: (jnp.maximum(i * hb - 1, 0), 0))


def _rowsum(x):
    return jnp.sum(x, axis=-1, keepdims=True)


def _kk_math(kkp):
    nrm = jnp.sqrt(_rowsum(kkp * kkp))
    inv = 1.0 / jnp.maximum(nrm, 1e-12)
    return nrm, inv, kkp * inv


def _rwkv_pre(p, mu, small, lora, lay, name):
    T, rcp = p.shape[0], lay[3]
    H = lay[0][0] // HEAD
    tm = min(128, T)

    def body(p_ref, ph_ref, mu_ref, w0_ref, a0_ref, kk_ref, ka_ref, wlw_ref, wla_ref, wlg_ref, r_o, lw_o, k2_o, v_o, aa_o, bb_o, g_o):
        ps, _ = _token_shift(p_ref[...], ph_ref[...], mu_ref[...], pl.program_id(0))
        q = _rwkv_math(ps, w0_ref[...], a0_ref[...], kk_ref[...], ka_ref[...], wlw_ref[...], wla_ref[...], wlg_ref[...], lay)
        for h in range(H):
            sl = slice(h * HEAD, (h + 1) * HEAD)
            for o_ref, key in ((r_o, "r"), (lw_o, "lw"), (k2_o, "k2"), (v_o, "v"), (g_o, "g")):
                o_ref[h] = q[key][:, sl]
            _, _, kk = _kk_math(q["kkp"][:, sl])
            aa_o[h] = -kk
            bb_o[h] = kk * q["a"][:, sl]

    whole = lambda arr: pl.BlockSpec(arr.shape, lambda i: (0, 0))
    return pl.pallas_call(
        body, name=name, grid=(T // tm,),
        in_specs=([pl.BlockSpec((tm, rcp), lambda i: (i, 0)), _halo_specs(T, tm, rcp, False), whole(mu)]
                  + [whole(s) for s in small] + [whole(w) for w in lora]),
        out_specs=[pl.BlockSpec((H, tm, HEAD), lambda i: (0, i, 0))] * 7, out_shape=[jax.ShapeDtypeStruct((H, T, HEAD), F32)] * 7,
        compiler_params=_params())(p, p, mu, *small, *lora)


def _rwkv_pre_bwd(p, mu, small, lora, hgrads, lay, name):
    T, rcp = p.shape[0], lay[3]
    widths, pw, offs, _ = lay
    RW = widths[0]
    H = RW // HEAD
    tm = min(128, T)

    def body(p_ref, ph_ref, mu_ref, w0_ref, a0_ref, kk_ref, ka_ref, wlw_ref, wla_ref, wlg_ref,
             dr1, dr2, dk1, dk2b, dv1, dv2, dlw_h, daa, dbb, dg_h,
             dps_ref, dmu_ref, dw0_ref, da0_ref, dkk_ref, dka_ref, dwlw_ref, dwla_ref, dwlg_ref,
             s_dr, s_dk2, s_dv, s_dlw, s_dkkp, s_da, s_dg):
        i = pl.program_id(0)
        ps, dprev = _token_shift(p_ref[...], ph_ref[...], mu_ref[...], i)
        k_k, k_a = kk_ref[...], ka_ref[...]
        q = _rwkv_math(ps, w0_ref[...], a0_ref[...], k_k, k_a, wlw_ref[...], wla_ref[...], wlg_ref[...], lay)
        k, a, lw, ww, tw, sg = q["k"], q["a"], q["lw"], q["ww"], q["tw"], q["sg"]
        for h in range(H):
            sl = slice(h * HEAD, (h + 1) * HEAD)
            s_dr[:, sl] = dr1[h] + dr2[h]
            s_dk2[:, sl] = dk1[h] + dk2b[h]
            s_dv[:, sl] = dv1[h] + dv2[h]
            s_dlw[:, sl] = dlw_h[h]
            s_dg[:, sl] = dg_h[h]
            nrm, inv, kk = _kk_math(q["kkp"][:, sl])
            dbb_h = dbb[h]
            dkk = dbb_h * a[:, sl] - daa[h]
            s_dkkp[:, sl] = jnp.where(nrm > 1e-12, inv * (dkk - kk * _rowsum(dkk * kk)), dkk * inv)
            s_da[:, sl] = dbb_h * kk
        dk2, dkkp, dg = s_dk2[...], s_dkkp[...], s_dg[...]
        dk = dk2 * (1.0 + (a - 1.0) * k_a) + dkkp * k_k
        da = s_da[...] + dk2 * k * k_a
        dpa = da * a * (1.0 - a)
        dww = s_dlw[...] * lw * _sigmoid(-ww)
        dxa = _bdot(dpa, wla_ref[...], "nt")
        dxw = _bdot(dww, wlw_ref[...], "nt") * (1.0 - tw * tw)
        dxg = _bdot(dg, wlg_ref[...], "nt") * sg * (1.0 - sg)
        segs = (s_dr[...], dk, s_dv[...], dxw, dxa, dxg)
        sums = [dmu_ref, dw0_ref, da0_ref, dkk_ref, dka_ref, dwlw_ref, dwla_ref, dwlg_ref]

        @pl.when(i == 0)
        def _():
            for s in sums:
                s[...] = jnp.zeros_like(s)

        for j, seg in enumerate(segs):
            sl = slice(offs[j], offs[j] + pw[j])
            dps_ref[:, sl] = seg
            dmu_ref[:, sl] += _colsum(seg * dprev[:, sl])
        dw0_ref[...] += _colsum(dww)
        da0_ref[...] += _colsum(dpa)
        dkk_ref[...] += _colsum(dkkp * k)
        dka_ref[...] += _colsum(dk2 * k * (a - 1.0))
        dwlw_ref[...] += _bdot(tw, dww, "tn")
        dwla_ref[...] += _bdot(q["xa"], dpa, "tn")
        dwlg_ref[...] += _bdot(sg, dg, "tn")

    whole = lambda arr: pl.BlockSpec(arr.shape, lambda i: (0, 0))
    row = lambda w: pl.BlockSpec((tm, w), lambda i: (i, 0))
    acc_shapes = [(1, rcp), (1, RW), (1, RW), (1, RW), (1, RW)] + [w.shape for w in lora]
    return pl.pallas_call(
        body, name=name, grid=(T // tm,),
        in_specs=([row(rcp), _halo_specs(T, tm, rcp, False), whole(mu)] + [whole(s) for s in small] + [whole(w) for w in lora]
                  + [pl.BlockSpec((H, tm, HEAD), lambda i: (0, i, 0))] * 10),
        out_specs=[row(rcp)] + [pl.BlockSpec(s, lambda i: (0, 0)) for s in acc_shapes],
        out_shape=[jax.ShapeDtypeStruct((T, rcp), F32)] + [jax.ShapeDtypeStruct(s, F32) for s in acc_shapes],
        scratch_shapes=[pltpu.VMEM((tm, RW), F32)] * 7, compiler_params=_params())(p, p, mu, *small, *lora, *hgrads)


def _shift_bwd(dps, mu, dproj, name):
    T, rcp = dps.shape
    tm = min(256, T)
    nt = T // tm

    def body(d_ref, dh_ref, mu_ref, buf_ref, o_ref):
        i = pl.program_id(0)
        d = d_ref[...]
        hid = lax.broadcasted_iota(jnp.int32, (SUBLANE, 1), 0)
        after = jnp.sum(jnp.where(hid == 0, dh_ref[...], 0.0), axis=0, keepdims=True)
        after = jnp.where(i == nt - 1, 0.0, after)
        rid = lax.broadcasted_iota(jnp.int32, (tm, 1), 0)
        nxt = jnp.where(rid == tm - 1, after, pltpu.roll(d, tm - 1, 0))
        mu_v = mu_ref[...]
        o_ref[...] = (d * (1.0 - mu_v) + nxt * mu_v).astype(BF16)

    row = pl.BlockSpec((tm, rcp), lambda i: (i, 0))
    return pl.pallas_call(
        body, name=name, grid=(nt,),
        in_specs=[row, _halo_specs(T, tm, rcp, True), pl.BlockSpec(mu.shape, lambda i: (0, 0)), pl.BlockSpec(memory_space=pl.ANY)],
        out_specs=row, out_shape=jax.ShapeDtypeStruct(dproj.shape, BF16), input_output_aliases={3: 0},
        compiler_params=_params())(dps, dps, mu, dproj)


def _head_post_math(y, r, k2, v, lg, lb, rk):
    yc = y - _mean(y)
    rstd = lax.rsqrt(_mean(yc * yc) + LNX_EPS)
    yn = yc * rstd
    s = _rowsum(r * k2 * rk)
    return yn, rstd, yn * lg + lb + s * v, s


def _head_post(y, r, k2, v, g, hp, name, deps=()):
    H, T, _ = y.shape
    tm = min(128, T)

    def body(y_ref, r_ref, k_ref, v_ref, g_ref, lg_ref, lb_ref, rk_ref, *rest):
        o_ref = rest[-1]
        _, _, t, _ = _head_post_math(y_ref[...], r_ref[...], k_ref[...], v_ref[...], lg_ref[...], lb_ref[...], rk_ref[...])
        out = (t * g_ref[...]).astype(BF16)
        for h in range(H):
            o_ref[:, h * HEAD:(h + 1) * HEAD] = out[h]

    blk = pl.BlockSpec((H, tm, HEAD), lambda i: (0, i, 0))
    par = pl.BlockSpec((H, 1, HEAD), lambda i: (0, 0, 0))
    return pl.pallas_call(
        body, name=name, grid=(T // tm,),
        in_specs=[blk] * 5 + [par] * 3 + [pl.BlockSpec(d.shape, lambda i, nd=d.ndim: (0,) * nd) for d in deps],
        out_specs=pl.BlockSpec((tm, H * HEAD), lambda i: (i, 0)),
        out_shape=jax.ShapeDtypeStruct((T, H * HEAD), BF16), compiler_params=_params())(y, r, k2, v, g, *hp, *deps)


def _head_post_bwd(dya, y, r, k2, v, g, hp, name, deps=()):
    H, T, _ = y.shape
    tm = min(128, T)
    hsum = lambda t: jnp.sum(t, axis=1, keepdims=True)

    def body(d_ref, y_ref, r_ref, k_ref, v_ref, g_ref, lg_ref, lb_ref, rk_ref, *rest):
        outs, d_s = rest[len(deps):len(deps) + 8], rest[-1]
        for h in range(H):
            d_s[h] = d_ref[:, h * HEAD:(h + 1) * HEAD]
        d_v, r_v, k_v, v_v, lg, rk = d_s[...], r_ref[...], k_ref[...], v_ref[...], lg_ref[...], rk_ref[...]
        yn, rstd, t, s = _head_post_math(y_ref[...], r_v, k_v, v_v, lg, lb_ref[...], rk)
        dyo = d_v * g_ref[...]
        dyn = dyo * lg
        ds = _rowsum(dyo * v_v)
        vals = (rstd * (dyn - _mean(dyn) - yn * _mean(dyn * yn)), ds * k_v * rk, ds * r_v * rk, dyo * s, d_v * t)
        for o_ref, val in zip(outs[:5], vals):
            o_ref[...] = val
        sums = (hsum(dyo * yn), hsum(dyo), hsum(ds * r_v * k_v))
        first = pl.program_id(0) == 0

        @pl.when(first)
        def _():
            for o_ref, val in zip(outs[5:], sums):
                o_ref[...] = val

        @pl.when(jnp.logical_not(first))
        def _():
            for o_ref, val in zip(outs[5:], sums):
                o_ref[...] += val

    blk = pl.BlockSpec((H, tm, HEAD), lambda i: (0, i, 0))
    par = pl.BlockSpec((H, 1, HEAD), lambda i: (0, 0, 0))
    return pl.pallas_call(
        body, name=name, grid=(T // tm,),
        in_specs=([pl.BlockSpec((tm, H * HEAD), lambda i: (i, 0))] + [blk] * 5 + [par] * 3
                  + [pl.BlockSpec(d.shape, lambda i, nd=d.ndim: (0,) * nd) for d in deps]),
        out_specs=[blk] * 5 + [par] * 3,
        out_shape=[jax.ShapeDtypeStruct((H, T, HEAD), F32)] * 5 + [jax.ShapeDtypeStruct((H, 1, HEAD), F32)] * 3,
        scratch_shapes=[pltpu.VMEM((H, tm, HEAD), F32)], compiler_params=_params())(dya, y, r, k2, v, g, *hp, *deps)


def _bmm(x, y, mode):
    dn = {"nn": (((2,), (1,)), ((0,), (0,))), "nt": (((2,), (2,)), ((0,), (0,))), "tn": (((1,), (1,)), ((0,), (0,)))}[mode]
    (xh, xl), (yh, yl) = _split(x), _split(y)
    dot = lambda p, q: lax.dot_general(p, q, dn, preferred_element_type=F32)
    out = dot(xh, yh)
    if yl is not None:
        out = out + dot(xh, yl)
    if xl is not None:
        out = out + dot(xl, yh)
    return out


def _split(x):
    if isinstance(x, tuple):
        return x
    hi = x.astype(BF16)
    return hi, (x - hi.astype(F32)).astype(BF16)


def _exact(x):
    return x.astype(BF16), None


def _round(x):
    return x if isinstance(x, tuple) else (x.astype(BF16), None)


def _rows(*xs):
    if isinstance(xs[0], tuple):
        return tuple(None if any(p is None for p in parts) else jnp.concatenate(parts, axis=1) for parts in zip(*xs))
    return jnp.concatenate(xs, axis=1)


def _wkv_chunk(r, lw, k, v, a, b, inverse=None):
    hb, C, _ = r.shape
    ti = lax.broadcasted_iota(jnp.int32, (C, C), 0)
    si = lax.broadcasted_iota(jnp.int32, (C, C), 1)
    linc, lstr, eye = (ti >= si).astype(F32), (ti > si).astype(F32), (ti == si).astype(F32)
    qmask = jnp.concatenate([jnp.concatenate([lstr, lstr], axis=1), jnp.concatenate([linc, linc], axis=1)], axis=0)
    lincb = _exact(jnp.broadcast_to(linc, (hb, C, C)))
    both = _exact(jnp.broadcast_to(jnp.concatenate([linc, lstr], axis=0), (hb, 2 * C, C)))
    ones = _exact(jnp.ones_like(v))
    lws = _split(lw)
    ci = _bmm(lincb, lws, "nn")
    cC = jnp.sum(lw, axis=1, keepdims=True)
    gi, ge, gn, gr = jnp.exp(ci), jnp.exp(ci - lw), jnp.exp(-ci), jnp.exp(cC - ci)
    q = dict(At=a * ge, Rt=r * gi, Bt=b * gn, Kt=k * gn, Bh=b * gr, Kh=k * gr)
    s = dict(AR=_round(_rows(q["At"], q["Rt"])), BK=_round(_rows(q["Bt"], q["Kt"])), BKh=_round(_rows(q["Bh"], q["Kh"])), v=_round(v))
    quad = _bmm(s["AR"], s["BK"], "nt") * qmask
    s["top"], s["bot"] = _round(quad[:, :C]), _round(quad[:, C:])
    if inverse is None:
        A_ab = quad[:, :C, :C]
        Tm = eye + A_ab
        Pw = _round(A_ab)
        n = 1
        while 2 * n < C:
            Pw = _round(_bmm(Pw, Pw, "nn"))
            Tm = Tm + _bmm(_round(Tm), Pw, "nn")
            n *= 2
        inverse = Tm
    s["Tm"] = _round(inverse)
    gC = jnp.exp(_bmm(lws, ones, "tn"))
    q.update(gi=gi, ge=ge, gn=gn, gr=gr, qmask=qmask, both=both, gC=gC, ones=ones, s=s)
    return q


def _wkv_u(s, H0s, C):
    arh = _bmm(s["AR"], H0s, "nn")
    zv = _rows(tuple(None if p is None else jnp.zeros_like(p) for p in s["v"]), s["v"])
    U = _bmm(s["Tm"], _round(arh[:, :C] + _bmm(s["top"], zv, "nn")), "nn")
    return arh, _rows(_round(U), s["v"])


def _wkv_fwd(r, lw, k, v, a, b, name):
    H, T, N = r.shape
    C = min(WKV_CHUNK, T)
    nc = T // C
    hb = _pick(H, (16, 8, 4, 2))

    def body(r_ref, lw_ref, k_ref, v_ref, a_ref, b_ref, y_ref, st_ref, inv_ref, u_ref, h_ref):
        @pl.when(pl.program_id(1) == 0)
        def _():
            h_ref[...] = jnp.zeros_like(h_ref)

        H0 = h_ref[...]
        st_ref[0] = H0
        q = _wkv_chunk(r_ref[...], lw_ref[...], k_ref[...], v_ref[...], a_ref[...], b_ref[...])
        s = q["s"]
        arh, UV = _wkv_u(s, _round(H0), C)
        inv_ref[0] = s["Tm"][0]
        u_ref[...] = UV[0][:, :C]
        y_ref[...] = arh[:, C:] + _bmm(s["bot"], UV, "nn")
        h_ref[...] = q["gC"] * H0 + _bmm(s["BKh"], UV, "tn")

    blk = pl.BlockSpec((hb, C, N), lambda h, c: (h, c, 0))
    per_chunk = lambda w: pl.BlockSpec((1, hb, w, w), lambda h, c: (c, h, 0, 0))
    return pl.pallas_call(
        body, name=name, grid=(H // hb, nc), in_specs=[blk] * 6, out_specs=[blk, per_chunk(N), per_chunk(C), blk],
        out_shape=[jax.ShapeDtypeStruct((H, T, N), F32), jax.ShapeDtypeStruct((nc, H, N, N), F32),
                   jax.ShapeDtypeStruct((nc, H, C, C), BF16), jax.ShapeDtypeStruct((H, T, N), BF16)],
        scratch_shapes=[pltpu.VMEM((hb, N, N), F32)], compiler_params=_params())(r, lw, k, v, a, b)


def _wkv_bwd(r, lw, k, v, a, b, states, inverses, u, dy, name):
    H, T, N = r.shape
    C = min(WKV_CHUNK, T)
    nc = T // C
    hb = _pick(H, (16, 8, 4, 2))

    def body(r_ref, lw_ref, k_ref, v_ref, a_ref, b_ref, st_ref, inv_ref, u_ref, dy_ref,
             dr_ref, dlw_ref, dk_ref, dv_ref, da_ref, db_ref, dh_ref):
        @pl.when(pl.program_id(1) == 0)
        def _():
            dh_ref[...] = jnp.zeros_like(dh_ref)

        dHC = dh_ref[...]
        H0 = st_ref[0]
        q = _wkv_chunk(r_ref[...], lw_ref[...], k_ref[...], v_ref[...], a_ref[...], b_ref[...], inverse=inv_ref[0])
        s, gC = q["s"], q["gC"]
        H0s, dHs, dY = _round(H0), _round(dHC), _round(dy_ref[...])
        UV = _rows(_round(u_ref[...]), s["v"])
        bot_dy = _bmm(s["bot"], dY, "tn")
        bkh_dh = _bmm(s["BKh"], dHs, "nn")
        dP = _round(_bmm(s["Tm"], _round(bot_dy[:, :C] + bkh_dh[:, :C]), "tn"))
        dv_ref[...] = bot_dy[:, C:] + bkh_dh[:, C:] + _bmm(s["top"], dP, "tn")[:, C:]
        dPY = _rows(dP, dY)
        dh_ref[...] = gC * dHC + _bmm(s["AR"], dPY, "tn")
        dquad = _round(_bmm(dPY, UV, "nt") * q["qmask"])
        dAR = _bmm(dPY, H0s, "nt") + _bmm(dquad, s["BK"], "nn")
        dBK = _bmm(dquad, s["AR"], "tn")
        dBKh = _bmm(UV, dHs, "nt")
        dAt, dRt, dBt, dKt, dBh, dKh = dAR[:, :C], dAR[:, C:], dBK[:, :C], dBK[:, C:], dBKh[:, :C], dBKh[:, C:]
        dr_ref[...] = dRt * q["gi"]
        da_ref[...] = dAt * q["ge"]
        db_ref[...] = dBt * q["gn"] + dBh * q["gr"]
        dk_ref[...] = dKt * q["gn"] + dKh * q["gr"]
        tail = dBh * q["Bh"] + dKh * q["Kh"]
        dci = dRt * q["Rt"] - dBt * q["Bt"] - dKt * q["Kt"] - tail
        dcC = jnp.sum(tail, axis=1, keepdims=True) + _bmm(q["ones"], H0 * dHC * gC, "nt")
        dlw_ref[...] = _bmm(q["both"], _rows(dci, dAt * q["At"]), "tn") + dcC

    blk = pl.BlockSpec((hb, C, N), lambda h, c: (h, nc - 1 - c, 0))
    per_chunk = lambda w: pl.BlockSpec((1, hb, w, w), lambda h, c: (nc - 1 - c, h, 0, 0))
    return pl.pallas_call(
        body, name=name, grid=(H // hb, nc), in_specs=[blk] * 6 + [per_chunk(N), per_chunk(C), blk, blk], out_specs=[blk] * 6,
        out_shape=[jax.ShapeDtypeStruct((H, T, N), F32)] * 6,
        scratch_shapes=[pltpu.VMEM((hb, N, N), F32)], compiler_params=_params())(r, lw, k, v, a, b, states, inverses, u, dy)


def _sgu_ln(z, SW, lng, lnb):
    ge = _gelu(z)
    u, vv = ge[:, :SW], ge[:, SW:]
    xc = vv - _mean(vv)
    rstd = lax.rsqrt(_mean(xc * xc) + LN_EPS)
    vn = xc * rstd
    return u, vn, rstd, vn * lng + lnb


def _causal(ws_ref, g):
    ti = lax.broadcasted_iota(jnp.int32, (SGU_CHUNK, SGU_CHUNK), 0)
    si = lax.broadcasted_iota(jnp.int32, (SGU_CHUNK, SGU_CHUNK), 1)
    return ti >= si, jnp.where(ti >= si, ws_ref[g], 0.0).astype(BF16)


def _sgu_fwd(proj, zblock, lng, lnb, ws, bexp, name):
    T, SW = proj.shape[0], lng.shape[1]
    G = ws.shape[0]
    tr = min(256, T)
    nch = tr // SGU_CHUNK

    def body(z_ref, lng_ref, lnb_ref, ws_ref, be_ref, o_ref):
        u, _, _, vl = _sgu_ln(z_ref[...], SW, lng_ref[...], lnb_ref[...])
        for g in range(G):
            cs = slice(g * SGU_GROUP, (g + 1) * SGU_GROUP)
            _, wc = _causal(ws_ref, g)
            for n in range(nch):
                rs = slice(n * SGU_CHUNK, (n + 1) * SGU_CHUNK)
                m = jnp.dot(wc, vl[rs, cs].astype(BF16), preferred_element_type=F32) + be_ref[:, cs]
                o_ref[rs, cs] = (u[rs, cs] * m).astype(BF16)

    whole = lambda arr: pl.BlockSpec(arr.shape, lambda i, nd=arr.ndim: (0,) * nd)
    return pl.pallas_call(
        body, name=name, grid=(T // tr,),
        in_specs=[pl.BlockSpec((tr, 2 * SW), lambda i: (i, zblock)), whole(lng), whole(lnb), whole(ws), whole(bexp)],
        out_specs=pl.BlockSpec((tr, SW), lambda i: (i, 0)), out_shape=jax.ShapeDtypeStruct((T, SW), BF16),
        compiler_params=_params())(proj, lng, lnb, ws, bexp)


def _sgu_bwd(proj, zblock, dyb, lng, lnb, ws, bexp, dproj, name):
    T, SW = proj.shape[0], lng.shape[1]
    G = ws.shape[0]
    tr = min(256, T)
    nch = tr // SGU_CHUNK
    nt = T // tr

    def body(z_ref, dy_ref, lng_ref, lnb_ref, ws_ref, be_ref, buf_ref, dz_ref, dlg_ref, dlb_ref, dws_ref, db_ref, du_s, dvl_s, dbacc_s):
        i = pl.program_id(0)
        zv = z_ref[...]
        lng_v = lng_ref[...]
        u, vn, rstd, vl = _sgu_ln(zv, SW, lng_v, lnb_ref[...])

        @pl.when(i == 0)
        def _():
            for s in (dlg_ref, dlb_ref, dws_ref, dbacc_s):
                s[...] = jnp.zeros_like(s)

        for g in range(G):
            cs = slice(g * SGU_GROUP, (g + 1) * SGU_GROUP)
            tri, wc = _causal(ws_ref, g)
            for n in range(nch):
                rs = slice(n * SGU_CHUNK, (n + 1) * SGU_CHUNK)
                blk = vl[rs, cs].astype(BF16)
                m = jnp.dot(wc, blk, preferred_element_type=F32) + be_ref[:, cs]
                dyv = dy_ref[rs, cs]
                du_s[rs, cs] = dyv * m
                dm = dyv * u[rs, cs]
                dvl_s[rs, cs] = _bdot(wc, dm, "tn")
                dws_ref[g] += jnp.where(tri, _bdot(dm, blk, "nt"), 0.0)
                dbacc_s[:, cs] += dm

        dvl = dvl_s[...]
        dlg_ref[...] += _colsum(dvl * vn)
        dlb_ref[...] += _colsum(dvl)
        dvn = dvl * lng_v
        dvv = rstd * (dvn - _mean(dvn) - vn * _mean(dvn * vn))
        gp = _gelu_grad(zv)
        dz_ref[:, :SW] = (du_s[...] * gp[:, :SW]).astype(BF16)
        dz_ref[:, SW:] = (dvv * gp[:, SW:]).astype(BF16)

        @pl.when(i == nt - 1)
        def _():
            lane = lax.broadcasted_iota(jnp.int32, (SGU_CHUNK, LANE), 1)
            out = jnp.zeros((SGU_CHUNK, LANE), F32)
            for g in range(G):
                col = jnp.sum(dbacc_s[:, g * SGU_GROUP:(g + 1) * SGU_GROUP], axis=1, keepdims=True)
                out = jnp.where(lane == g, col, out)
            db_ref[...] = out

    whole = lambda arr: pl.BlockSpec(arr.shape, lambda i, nd=arr.ndim: (0,) * nd)
    acc_shapes = [(1, SW), (1, SW), ws.shape, (SGU_CHUNK, LANE)]
    return pl.pallas_call(
        body, name=name, grid=(nt,),
        in_specs=[pl.BlockSpec((tr, 2 * SW), lambda i: (i, zblock)), pl.BlockSpec((tr, SW), lambda i: (i, 0)),
                  whole(lng), whole(lnb), whole(ws), whole(bexp), pl.BlockSpec(memory_space=pl.ANY)],
        out_specs=([pl.BlockSpec((tr, 2 * SW), lambda i: (i, zblock))]
                   + [pl.BlockSpec(s, lambda i, nd=len(s): (0,) * nd) for s in acc_shapes]),
        out_shape=[jax.ShapeDtypeStruct(dproj.shape, BF16)] + [jax.ShapeDtypeStruct(s, F32) for s in acc_shapes],
        scratch_shapes=[pltpu.VMEM((tr, SW), F32), pltpu.VMEM((tr, SW), F32), pltpu.VMEM((SGU_CHUNK, SW), F32)],
        input_output_aliases={6: 0}, compiler_params=_params())(proj, dyb, lng, lnb, ws, bexp, dproj)


_HBM = pl.BlockSpec(memory_space=pltpu.HBM)
_SEM = pl.BlockSpec(memory_space=pltpu.SEMAPHORE)
_DATAFLOW = pltpu.SideEffectType.DATAFLOW_SIDE_EFFECTING


def _mesh_place(chips=False):
    x, y, c = lax.axis_index("x"), lax.axis_index("y"), lax.axis_index("c")
    return x, y, c, (2 * x + y if chips else 4 * x + 2 * y + c)


def _peer(x, y, c, rel, chips=False):
    px = 1 - x if rel & 4 else x
    py = 1 - y if rel & 2 else y
    pc = 1 - c if rel & 1 else c
    return (px, py, pc), (2 * px + py if chips else 4 * px + 2 * py + pc)


ALL_PEERS = tuple(range(1, N_DEV))
SIBLING = (1,)
SAME_CORE = (2, 4, 6)
SIBLINGS_CORE = (3, 5, 7)


def _exchange_start(groups, name, rels=ALL_PEERS, chips=False):
    flat = [t for g in groups for t in g]
    sizes = [len(g) for g in groups]
    n, ng = len(flat), len(groups)
    srcs = [pltpu.with_memory_space_constraint(a, pltpu.HBM) for a, _ in flat]
    lands = [pltpu.with_memory_space_constraint(lax.empty(((N_DEV,) + a.shape) if isg else a.shape, a.dtype), pltpu.HBM)
             for a, isg in flat]

    def body(*refs):
        ins, lnd, sems, token = refs[:n], refs[n:2 * n], refs[2 * n:2 * n + 3 * ng], refs[-1]
        x, y, c, me = _mesh_place(chips)
        j0 = 0
        for gi, sz in enumerate(sizes):
            for rel in rels:
                dev, slot = _peer(x, y, c, rel, chips)
                for jj in range(sz):
                    j = j0 + jj
                    pltpu.make_async_remote_copy(
                        src_ref=ins[j] if flat[j][1] else ins[j].at[slot], dst_ref=lnd[j].at[me],
                        send_sem=sems[3 * gi].at[jj * (N_DEV - 1) + rel - 1], recv_sem=sems[3 * gi + 1].at[jj * (N_DEV - 1) + rel - 1],
                        device_id=dev, device_id_type=pl.DeviceIdType.MESH).start()
            for jj in range(sz):
                j = j0 + jj
                pltpu.make_async_copy(ins[j] if flat[j][1] else ins[j].at[me], lnd[j].at[me], sems[3 * gi + 2].at[jj]).start()
            j0 += sz
        token[...] = jnp.zeros_like(token)

    sem_shapes = [pltpu.SemaphoreType.DMA((k,)) for sz in sizes for k in (sz * (N_DEV - 1), sz * (N_DEV - 1), sz)]
    res = pl.pallas_call(
        body, name=name,
        out_shape=(*sem_shapes, *[pltpu.HBM(a.shape, a.dtype) for a in srcs], *[pltpu.HBM(a.shape, a.dtype) for a in lands],
                   jax.ShapeDtypeStruct((SUBLANE, LANE), F32)),
        in_specs=[_HBM] * (2 * n), out_specs=(*[_SEM] * (3 * ng), *[_HBM] * (2 * n), pl.BlockSpec(memory_space=pltpu.VMEM)),
        input_output_aliases={i: 3 * ng + i for i in range(2 * n)},
        compiler_params=pltpu.CompilerParams(has_side_effects=_DATAFLOW))(*srcs, *lands)
    sems, thru, token = res[:3 * ng], res[3 * ng:3 * ng + 2 * n], res[-1]
    handle, j0 = [], 0
    for gi, sz in enumerate(sizes):
        handle.append(dict(kinds=[k for _, k in groups[gi]], chips=chips, srcs=list(thru[j0:j0 + sz]), lands=list(thru[n + j0:n + j0 + sz]),
                           sems=list(sems[3 * gi:3 * gi + 3])))
        j0 += sz
    return handle, token


def _exchange_wait(group, after, name, rels=ALL_PEERS, local=True):
    kinds, sz = group["kinds"], len(group["kinds"])
    relay = group.get("relay", [])

    def body(*refs):
        ins, lnd, (ssem, rsem, lsem) = refs[:sz], refs[sz:2 * sz], refs[2 * sz:2 * sz + 3]
        x, y, c, me = _mesh_place(group["chips"])
        for rel in rels:
            dev, slot = _peer(x, y, c, rel, group["chips"])
            for jj in range(sz):
                cp = pltpu.make_async_remote_copy(
                    src_ref=ins[jj] if kinds[jj] else ins[jj].at[slot], dst_ref=lnd[jj].at[slot],
                    send_sem=ssem.at[jj * (N_DEV - 1) + rel - 1], recv_sem=rsem.at[jj * (N_DEV - 1) + rel - 1],
                    device_id=dev, device_id_type=pl.DeviceIdType.MESH)
                cp.wait_send()
                cp.wait_recv()
        if local:
            for jj in range(sz):
                pltpu.make_async_copy(ins[jj] if kinds[jj] else ins[jj].at[me], lnd[jj].at[me], lsem.at[jj]).wait()
        if relay:
            fsend, frecv = refs[2 * sz + 3:2 * sz + 5]
            dev = _peer(x, y, c, 1)[0]
            for q, (mine, theirs) in enumerate(zip(SAME_CORE, SIBLINGS_CORE)):
                for jj in range(sz):
                    cp = pltpu.make_async_remote_copy(
                        src_ref=lnd[jj].at[_peer(x, y, c, mine)[1]], dst_ref=lnd[jj].at[_peer(x, y, c, theirs)[1]],
                        send_sem=fsend.at[jj * len(SAME_CORE) + q], recv_sem=frecv.at[jj * len(SAME_CORE) + q],
                        device_id=dev, device_id_type=pl.DeviceIdType.MESH)
                    cp.wait_send()
                    cp.wait_recv()

    arrays = group["srcs"] + group["lands"]
    sems = group["sems"] + relay
    res = pl.pallas_call(
        body, name=name, out_shape=[pltpu.HBM(a.shape, a.dtype) for a in arrays],
        in_specs=[_HBM] * (2 * sz) + [_SEM] * len(sems) + [pl.BlockSpec(memory_space=pl.ANY)], out_specs=[_HBM] * (2 * sz),
        input_output_aliases={i: i for i in range(2 * sz)},
        compiler_params=pltpu.CompilerParams(has_side_effects=_DATAFLOW))(*arrays, *sems, after)
    return dict(group, srcs=list(res[:sz]), lands=list(res[sz:]), relay=[])


def _relay_start(group, name):
    sz = len(group["kinds"])
    nq = len(SAME_CORE)

    def body(*refs):
        lnd, fsend, frecv, token = refs[:sz], refs[sz], refs[sz + 1], refs[-1]
        x, y, c, _ = _mesh_place()
        dev = _peer(x, y, c, 1)[0]
        for q, rel in enumerate(SAME_CORE):
            slot = _peer(x, y, c, rel)[1]
            for jj in range(sz):
                pltpu.make_async_remote_copy(
                    src_ref=lnd[jj].at[slot], dst_ref=lnd[jj].at[slot], send_sem=fsend.at[jj * nq + q], recv_sem=frecv.at[jj * nq + q],
                    device_id=dev, device_id_type=pl.DeviceIdType.MESH).start()
        token[...] = jnp.zeros_like(token)

    lands = group["lands"]
    res = pl.pallas_call(
        body, name=name,
        out_shape=(pltpu.SemaphoreType.DMA((sz * nq,)), pltpu.SemaphoreType.DMA((sz * nq,)), *[pltpu.HBM(a.shape, a.dtype) for a in lands],
                   jax.ShapeDtypeStruct((SUBLANE, LANE), F32)),
        in_specs=[_HBM] * sz, out_specs=(_SEM, _SEM, *[_HBM] * sz, pl.BlockSpec(memory_space=pltpu.VMEM)),
        input_output_aliases={i: 2 + i for i in range(sz)},
        compiler_params=pltpu.CompilerParams(has_side_effects=_DATAFLOW))(*lands)
    return dict(group, lands=list(res[2:2 + sz]), relay=[res[0], res[1]]), res[-1]


def _sibling_swap(arrays, handle, after, name):
    start = handle is None
    n = len(arrays) if start else len(handle["srcs"])
    chips = N_DEV // 2
    if start:
        srcs = [pltpu.with_memory_space_constraint(a.reshape(chips, 2, *a.shape[1:]), pltpu.HBM) for a in arrays]
        lands = [pltpu.with_memory_space_constraint(lax.empty((chips,) + a.shape[1:], a.dtype), pltpu.HBM) for a in arrays]
    else:
        srcs, lands = handle["srcs"], handle["lands"]

    def body(*refs):
        ins, lnd, ssem, rsem = refs[:n], refs[n:2 * n], refs[2 * n], refs[2 * n + 1]
        x, y, c, _ = _mesh_place()
        dev = _peer(x, y, c, 1)[0]
        for q in range(chips):
            for j in range(n):
                cp = pltpu.make_async_remote_copy(
                    src_ref=ins[j].at[q, 1 - c], dst_ref=lnd[j].at[q], send_sem=ssem.at[j * chips + q], recv_sem=rsem.at[j * chips + q],
                    device_id=dev, device_id_type=pl.DeviceIdType.MESH)
                if start:
                    cp.start()
                else:
                    cp.wait_send()
                    cp.wait_recv()
        if start:
            refs[-1][...] = jnp.zeros_like(refs[-1])

    thru = [pltpu.HBM(a.shape, a.dtype) for a in srcs + lands]
    effect = pltpu.CompilerParams(has_side_effects=_DATAFLOW)
    if start:
        res = pl.pallas_call(
            body, name=name, out_shape=(pltpu.SemaphoreType.DMA((n * chips,)), pltpu.SemaphoreType.DMA((n * chips,)), *thru,
                                        jax.ShapeDtypeStruct((SUBLANE, LANE), F32)),
            in_specs=[_HBM] * (2 * n), out_specs=(_SEM, _SEM, *[_HBM] * (2 * n), pl.BlockSpec(memory_space=pltpu.VMEM)),
            input_output_aliases={i: 2 + i for i in range(2 * n)}, compiler_params=effect)(*srcs, *lands)
        return dict(srcs=list(res[2:2 + n]), lands=list(res[2 + n:2 + 2 * n]), sems=[res[0], res[1]]), res[-1]
    res = pl.pallas_call(
        body, name=name, out_shape=thru, in_specs=[_HBM] * (2 * n) + [_SEM, _SEM, pl.BlockSpec(memory_space=pl.ANY)],
        out_specs=[_HBM] * (2 * n), input_output_aliases={i: i for i in range(2 * n)}, compiler_params=effect)(
            *srcs, *lands, *handle["sems"], after)
    return dict(handle, srcs=list(res[:n]), lands=list(res[n:]))


def _pair_add(mine, theirs, core, name):
    chips, _, rows, w = mine.shape
    tm = _pick(rows, (256, 128, 64, 32, 16))

    def body(core_ref, a_ref, b_ref, o_ref):
        o_ref[...] = (a_ref[...].astype(F32) + b_ref[...].astype(F32)).astype(o_ref.dtype)

    return pl.pallas_call(
        body, name=name, out_shape=jax.ShapeDtypeStruct(theirs.shape, theirs.dtype),
        grid_spec=pltpu.PrefetchScalarGridSpec(
            num_scalar_prefetch=1, grid=(chips, rows // tm),
            in_specs=[pl.BlockSpec((None, None, tm, w), lambda q, i, core_ref: (q, core_ref[0], i, 0)),
                      pl.BlockSpec((None, tm, w), lambda q, i, core_ref: (q, i, 0))],
            out_specs=pl.BlockSpec((None, tm, w), lambda q, i, core_ref: (q, i, 0))),
        compiler_params=_params())(core, mine, theirs)


def _adamw(w, m, v, gparts, name, after=None):
    R, C = w.shape
    tm = _pick(R, (256, 128, 64, 32, 16, 8))
    order = [] if after is None else [after]

    def body(w_ref, m_ref, v_ref, g_ref, *rest):
        go, do, mo, vo = rest[len(order):]
        g = g_ref[0].astype(F32)
        for j in range(1, gparts.shape[0]):
            g = g + g_ref[j].astype(F32)
        mn = ADAM_B1 * m_ref[...] + (1.0 - ADAM_B1) * g
        vn = ADAM_B2 * v_ref[...] + (1.0 - ADAM_B2) * (g * g)
        m_hat = mn / (1.0 - ADAM_B1 ** ADAM_STEP)
        v_hat = vn / (1.0 - ADAM_B2 ** ADAM_STEP)
        go[...] = g
        do[...] = -ADAM_LR * (m_hat / (jnp.sqrt(v_hat) + ADAM_EPS) + ADAM_WD * w_ref[...])
        mo[...] = mn
        vo[...] = vn

    row = pl.BlockSpec((tm, C), lambda i: (i, 0))
    return pl.pallas_call(
        body, name=name, grid=(R // tm,),
        in_specs=[row, row, row, pl.BlockSpec((gparts.shape[0], tm, C), lambda i: (0, i, 0))] + [pl.BlockSpec(memory_space=pl.ANY)] * len(order),
        out_specs=[row] * 4, out_shape=[jax.ShapeDtypeStruct((R, C), F32)] * 4, compiler_params=_params())(w, m, v, gparts, *order)


def _pack(arrays):
    parts = []
    for a in arrays:
        f = a.reshape(1, -1)
        pad = _ceil_to(f.shape[1], SUBLANE * LANE) - f.shape[1]
        f = jnp.concatenate([f, jnp.zeros((1, pad), f.dtype)], axis=1) if pad else f
        parts.append(f.reshape(-1, LANE))
    rows = sum(p.shape[0] for p in parts)
    pad = _ceil_to(rows, 64) - rows
    return jnp.concatenate(parts + ([jnp.zeros((pad, LANE), parts[0].dtype)] if pad else []), axis=0)


def _unpack(buf, shapes):
    out, row = [], 0
    for s in shapes:
        size = 1
        for d in s:
            size *= d
        rows = _ceil_to(size, SUBLANE * LANE) // LANE
        out.append(buf[row:row + rows].reshape(1, -1)[:, :size].reshape(s))
        row += rows
    return out


def kernel(x, norm_mix_g, w_in, shift_mu, w0, w_lora_up, a0, a_lora_up, g_lora_up, k_k, k_a, r_k, lnx_g, lnx_b, w_proj_rwkv, sgu_ln_g, sgu_ln_b, sgu_w, sgu_b, w_proj_sgu, w_out, norm_ffn_g, w_ffn_gate, w_ffn_up, w_ffn_down, norm_final_g, loss_target, m_norm_mix_g, m_w_in, m_shift_mu, m_w0, m_w_lora_up, m_a0, m_a_lora_up, m_g_lora_up, m_k_k, m_k_a, m_r_k, m_lnx_g, m_lnx_b, m_w_proj_rwkv, m_sgu_ln_g, m_sgu_ln_b, m_sgu_w, m_sgu_b, m_w_proj_sgu, m_w_out, m_norm_ffn_g, m_w_ffn_gate, m_w_ffn_up, m_w_ffn_down, m_norm_final_g, v_norm_mix_g, v_w_in, v_shift_mu, v_w0, v_w_lora_up, v_a0, v_a_lora_up, v_g_lora_up, v_k_k, v_k_a, v_r_k, v_lnx_g, v_lnx_b, v_w_proj_rwkv, v_sgu_ln_g, v_sgu_ln_b, v_sgu_w, v_sgu_b, v_w_proj_sgu, v_w_out, v_norm_ffn_g, v_w_ffn_gate, v_w_ffn_up, v_w_ffn_down, v_norm_final_g):
    weights = dict(norm_mix_g=norm_mix_g, w_in=w_in, shift_mu=shift_mu, w0=w0, w_lora_up=w_lora_up, a0=a0, a_lora_up=a_lora_up,
                   g_lora_up=g_lora_up, k_k=k_k, k_a=k_a, r_k=r_k, lnx_g=lnx_g, lnx_b=lnx_b, w_proj_rwkv=w_proj_rwkv,
                   sgu_ln_g=sgu_ln_g, sgu_ln_b=sgu_ln_b, sgu_w=sgu_w, sgu_b=sgu_b, w_proj_sgu=w_proj_sgu, w_out=w_out,
                   norm_ffn_g=norm_ffn_g, w_ffn_gate=w_ffn_gate, w_ffn_up=w_ffn_up, w_ffn_down=w_ffn_down, norm_final_g=norm_final_g)
    m_in = dict(norm_mix_g=m_norm_mix_g, w_in=m_w_in, shift_mu=m_shift_mu, w0=m_w0, w_lora_up=m_w_lora_up, a0=m_a0,
                a_lora_up=m_a_lora_up, g_lora_up=m_g_lora_up, k_k=m_k_k, k_a=m_k_a, r_k=m_r_k, lnx_g=m_lnx_g, lnx_b=m_lnx_b,
                w_proj_rwkv=m_w_proj_rwkv, sgu_ln_g=m_sgu_ln_g, sgu_ln_b=m_sgu_ln_b, sgu_w=m_sgu_w, sgu_b=m_sgu_b,
                w_proj_sgu=m_w_proj_sgu, w_out=m_w_out, norm_ffn_g=m_norm_ffn_g, w_ffn_gate=m_w_ffn_gate, w_ffn_up=m_w_ffn_up,
                w_ffn_down=m_w_ffn_down, norm_final_g=m_norm_final_g)
    v_in = dict(norm_mix_g=v_norm_mix_g, w_in=v_w_in, shift_mu=v_shift_mu, w0=v_w0, w_lora_up=v_w_lora_up, a0=v_a0,
                a_lora_up=v_a_lora_up, g_lora_up=v_g_lora_up, k_k=v_k_k, k_a=v_k_a, r_k=v_r_k, lnx_g=v_lnx_g, lnx_b=v_lnx_b,
                w_proj_rwkv=v_w_proj_rwkv, sgu_ln_g=v_sgu_ln_g, sgu_ln_b=v_sgu_ln_b, sgu_w=v_sgu_w, sgu_b=v_sgu_b,
                w_proj_sgu=v_w_proj_sgu, w_out=v_w_out, norm_ffn_g=v_norm_ffn_g, w_ffn_gate=v_w_ffn_gate, w_ffn_up=v_w_ffn_up,
                w_ffn_down=v_w_ffn_down, norm_final_g=v_norm_final_g)
    names = list(weights)
    col_sharded = ("w_in", "w_lora_up", "a_lora_up", "g_lora_up", "w_proj_rwkv", "w_proj_sgu", "w_ffn_gate", "w_ffn_up")
    row_sharded = ("w_out", "w_ffn_down")
    sharded = [n for n in names if n in col_sharded or n in row_sharded]
    small = [n for n in names if n not in sharded]

    xs, tgt = x[0], loss_target[0]
    T, D = xs.shape
    RW = w0.shape[1]
    H = RW // HEAD
    SW = sgu_ln_g.shape[1]
    G = sgu_w.shape[1]
    assert 2 * SW == D, "the projection layout takes the SGU part to be as wide as a gate"
    lay = _rwkv_layout(RW, w_lora_up.shape[1], a_lora_up.shape[1], g_lora_up.shape[1], D)
    _, pw, _, rcp = lay
    icp = rcp + 3 * D
    b_ga, b_gb, b_z = rcp // D, rcp // D + 1, rcp // D + 2

    gather_groups = dict(win=["w_in", "w_lora_up", "a_lora_up", "g_lora_up"], proj=["w_proj_rwkv", "w_proj_sgu", "w_out"],
                         ffn_gate_up=["w_ffn_gate", "w_ffn_up"], ffn_down=["w_ffn_down"])
    handles, gather_token = _exchange_start([[(weights[n][0].astype(BF16), True) for n in grp] for grp in gather_groups.values()],
                                            "gather_start", rels=SIBLING + SAME_CORE)
    gather = dict(zip(gather_groups, handles))
    full = {}
    relay_tokens = {}
    joined = lambda g: g.transpose(1, 0, 2).reshape(g.shape[1], -1)

    def relay_weights(key, after):
        arrived = _exchange_wait(gather[key], after, "gather_wait_ici_" + key, rels=SAME_CORE, local=False)
        gather[key], relay_tokens[key] = _relay_start(arrived, "gather_relay_" + key)

    def take_weights(key, after):
        done = _exchange_wait(gather[key], after, "gather_wait_d2d_" + key, rels=SIBLING)
        for n, g in zip(gather_groups[key], done["lands"]):
            full[n] = g.reshape(-1, g.shape[2]) if n in row_sharded else g

    packed = [_pack([d[n] for n in small] + [gather_token]) for d in (weights, m_in, v_in)]
    n1 = _rms_fwd(xs, norm_mix_g, "rms_mix", deps=[gather_token, *packed])
    relay_weights("win", n1)
    take_weights("win", relay_tokens["win"])
    W_in = _w_in_to_proj(full["w_in"], lay, D, "w_in_layout")
    lora = [_pad_rows(joined(full[n]), rows) for n, rows in zip(("w_lora_up", "a_lora_up", "g_lora_up"), pw[3:])]
    mu_p = _pad_rwkv_cols(shift_mu, lay)
    rsmall = [w0, a0, k_k, k_a]
    hp = [lnx_g.reshape(H, 1, HEAD), lnx_b.reshape(H, 1, HEAD), r_k.reshape(H, 1, HEAD)]
    ws = sgu_w[0]
    bexp = jnp.repeat(sgu_b[0].T, SGU_GROUP, axis=1)
    gf = norm_final_g.reshape(1, D)

    proj = _matmul(n1, W_in, mode="nn", out_dtype=F32, name="proj_in")
    ga, gb = (proj, D, b_ga), (proj, D, b_gb)
    r_h, lw_h, k2_h, v_h, aa_h, bb_h, g_h = _rwkv_pre(proj, mu_p, rsmall, lora, lay, "rwkv_pre")
    wkv_in = [r_h, lw_h, k2_h, v_h, aa_h, bb_h]
    y_h, *wkv_saved = _wkv_fwd(*wkv_in, "wkv_fwd")
    relay_weights("proj", y_h)
    ya = _head_post(y_h, r_h, k2_h, v_h, g_h, hp, "head_post", deps=[relay_tokens["proj"]])
    relay_weights("ffn_gate_up", ya)
    yb = _sgu_fwd(proj, b_z, sgu_ln_g, sgu_ln_b, ws, bexp, "sgu_fwd")
    take_weights("proj", ya)
    pa = _matmul(ya, full["w_proj_rwkv"], mode="nn", out_dtype=F32, name="proj_a", deps=[relay_tokens["ffn_gate_up"]])

    def merge_fn(pb_v, pa_v, ga_v, gb_v):
        return pb_v, _sigmoid(ga_v) * pa_v + _sigmoid(gb_v) * pb_v
    pb, merged = _matmul(yb, full["w_proj_sgu"], mode="nn", name="proj_b_merge",
                         epi=(merge_fn, [pa, (proj, b_ga * D), (proj, b_gb * D)], [F32, BF16]))
    h1 = _matmul(merged, full["w_out"], mode="nn", out_dtype=F32, name="out_proj", add=xs)
    n2 = _rms_fwd(h1, norm_ffn_g, "rms_ffn")
    relay_weights("ffn_down", n2)
    take_weights("ffn_gate_up", n2)

    def act_fn(gt_v, up_v):
        return gt_v, up_v, gt_v * _sigmoid(gt_v) * up_v
    gt, up, act = _matmul(n2, full["w_ffn_gate"], b2=full["w_ffn_up"], mode="nn", name="ffn_gate_up_act", out_blocks=N_DEV,
                          epi=(act_fn, [], [BF16, BF16, BF16]), deps=[relay_tokens["ffn_down"]])
    take_weights("ffn_down", act)
    h2 = _matmul(act, full["w_ffn_down"], mode="nn", out_dtype=F32, name="ffn_down", add=h1)

    def final_fn(rv, pv):
        (h_v, t_v), (g_v,) = rv, pv
        r = lax.rsqrt(_mean(h_v * h_v) + RMS_EPS)
        yn = h_v * r
        e = yn * g_v - t_v
        loss = 0.5 * jnp.sum(_mean(e * e))
        dout = e * (1.0 / D)
        dyg = dout * g_v
        dh = r * (dyg - yn * _mean(dyg * yn))
        return [dh, dh], [jnp.full((1, LANE), loss, F32), _colsum(dout * yn)]
    dh2, dh2_bf, loss_part, d_gf = _rowwise(final_fn, [h2, tgt], [gf], [(D, F32), (D, BF16)], [(1, LANE), (1, D)], name="final_loss")

    grads = {}

    def start_scatter(group, name, extra=()):
        blocks = [(grads[n].reshape(N_DEV, -1, grads[n].shape[1]) if n in row_sharded else grads[n], False) for n in group]
        (handle,), token = _exchange_start([blocks + list(extra)], name)
        return handle, token

    def dact_fn(d_v, gt_v, up_v):
        gt_v, up_v = gt_v.astype(F32), up_v.astype(F32)
        s = _sigmoid(gt_v)
        return d_v * up_v * (s * (1.0 + gt_v * (1.0 - s))), d_v * gt_v * s
    dgt, dup = _matmul(dh2_bf, full["w_ffn_down"], mode="nt", name="d_ffn_act", out_blocks=N_DEV,
                       epi=(dact_fn, [gt, up], [BF16, BF16]))
    scatter_groups = dict(ffn_down=["w_ffn_down"], ffn_gate=["w_ffn_gate"], ffn_up=["w_ffn_up"],
                          mid=["w_out", "w_proj_rwkv", "w_proj_sgu"], last=["w_in", "w_lora_up", "a_lora_up", "g_lora_up"])
    scatters = {}
    grads["w_ffn_down"] = _matmul(act, dh2_bf, mode="tn", out_dtype=BF16, name="dw_ffn_down")
    scatters["ffn_down"], token = start_scatter(scatter_groups["ffn_down"], "scatter_start_ffn_down")
    dn2 = _matmul(dgt, full["w_ffn_gate"], mode="nt", out_dtype=F32, name="dn2_gate", deps=[token])
    grads["w_ffn_gate"] = _matmul(n2, dgt, mode="tn", out_dtype=BF16, name="dw_ffn_gate", out_blocks=N_DEV)
    scatters["ffn_gate"], token = start_scatter(scatter_groups["ffn_gate"], "scatter_start_ffn_gate")
    grads["w_ffn_up"] = _matmul(n2, dup, mode="tn", out_dtype=BF16, name="dw_ffn_up", out_blocks=N_DEV, deps=[token])
    scatters["ffn_up"], token = start_scatter(scatter_groups["ffn_up"], "scatter_start_ffn_up")
    dn2 = _matmul(dup, full["w_ffn_up"], mode="nt", out_dtype=F32, name="dn2_up", add=dn2, deps=[token])
    dh1, dh1_bf, d_g2 = _rms_bwd(dn2, h1, dh2, norm_ffn_g, "rms_ffn_bwd")
    dmerged = _matmul(dh1_bf, full["w_out"], mode="nt", out_dtype=F32, name="d_merged")
    grads["w_out"] = _matmul(merged, dh1_bf, mode="tn", out_dtype=BF16, name="dw_out")

    def dmerge_fn(rv, pv):
        d_v, ga_v, gb_v, pa_v, pb_v = rv
        sa, sb = _sigmoid(ga_v), _sigmoid(gb_v)
        dgates = jnp.concatenate([d_v * pa_v * sa * (1.0 - sa), d_v * pb_v * sb * (1.0 - sb)], axis=1)
        return [dgates, d_v * sa, d_v * sb], []
    dproj, dpa, dpb = _rowwise(dmerge_fn, [dmerged, ga, gb, pa, pb], [],
                               [(2 * D, BF16, icp, b_ga // 2, None), (D, BF16), (D, BF16)], [], name="d_merge")
    dya = _matmul(dpa, full["w_proj_rwkv"], mode="nt", out_dtype=F32, name="d_ya")
    dyb = _matmul(dpb, full["w_proj_sgu"], mode="nt", out_dtype=F32, name="d_yb")
    grads["w_proj_rwkv"] = _matmul(ya, dpa, mode="tn", out_dtype=BF16, name="dw_proj_a", out_blocks=N_DEV)
    grads["w_proj_sgu"] = _matmul(yb, dpb, mode="tn", out_dtype=BF16, name="dw_proj_b", out_blocks=N_DEV)
    scatters["mid"], token_mid = start_scatter(scatter_groups["mid"], "scatter_start_mid")
    dproj, d_lng, d_lnb, d_ws, d_bs = _sgu_bwd(proj, b_z, dyb, sgu_ln_g, sgu_ln_b, ws, bexp, dproj, "sgu_bwd")

    dy_h, dr1, dk1, dv1, dg_h, d_lnxg, d_lnxb, d_rk = _head_post_bwd(dya, y_h, r_h, k2_h, v_h, g_h, hp, "head_post_bwd",
                                                                     deps=[token_mid])
    dr2, dlw_h, dk2b, dv2, daa, dbb = _wkv_bwd(*wkv_in, *wkv_saved, dy_h, "wkv_bwd")
    dps, d_mu, d_w0, d_a0, d_kk, d_ka, d_wlw, d_wla, d_wlg = _rwkv_pre_bwd(
        proj, mu_p, rsmall, lora, [dr1, dr2, dk1, dk2b, dv1, dv2, dlw_h, daa, dbb, dg_h], lay, "rwkv_pre_bwd")
    dproj = _shift_bwd(dps, mu_p, dproj, "shift_bwd")
    split = lambda g: g.reshape(g.shape[0], N_DEV, -1).transpose(1, 0, 2)
    grads["w_in"] = _dw_in_from_proj(_matmul(n1, dproj, mode="tn", out_dtype=BF16, name="dw_in"), lay, D, w_in.shape[2], "dw_in_layout")
    grads["w_lora_up"] = split(d_wlw[:w_lora_up.shape[1]].astype(BF16))
    grads["a_lora_up"] = split(d_wla[:a_lora_up.shape[1]].astype(BF16))
    grads["g_lora_up"] = split(d_wlg[:g_lora_up.shape[1]].astype(BF16))
    out = {}

    def update_group(key, after):
        handle = scatters[key]
        parts = _exchange_wait(handle, after, "scatter_wait_" + key, rels=SAME_CORE if handle["chips"] else ALL_PEERS)["lands"]
        for n, part in zip(scatter_groups[key], parts):
            res = _adamw(weights[n][0], m_in[n][0], v_in[n][0], part, "adamw_" + n, after=after)
            out[n] = [t.reshape(weights[n].shape) for t in res]
            after = res[0]
        return after

    swap, token_swap = _sibling_swap([grads[n] for n in scatter_groups["last"]], None, None, "scatter_last_swap_start")
    after = update_group("ffn_gate", update_group("ffn_down", token_swap))
    swap = _sibling_swap(None, swap, after, "scatter_last_swap_wait")
    core = lax.axis_index("c").astype(jnp.int32).reshape(1)
    chip_sums = [_pair_add(mine, theirs, core, "scatter_last_add_" + n)
                 for n, mine, theirs in zip(scatter_groups["last"], swap["srcs"], swap["lands"])]
    (scatters["last"],), token_in = _exchange_start([[(s, False) for s in chip_sums]], "scatter_start_last", rels=SAME_CORE, chips=True)
    dn1 = _matmul(dproj, W_in, mode="nt", out_dtype=F32, name="dn1", deps=[token_in])
    dx, _, d_g1 = _rms_bwd(dn1, xs, dh1, norm_mix_g, "rms_mix_bwd")
    small_grads = dict(norm_mix_g=d_g1, shift_mu=_unpad_rwkv_cols(d_mu, lay), w0=d_w0, a0=d_a0, k_k=d_kk, k_a=d_ka, r_k=d_rk,
                       lnx_g=d_lnxg, lnx_b=d_lnxb, sgu_ln_g=d_lng, sgu_ln_b=d_lnb, sgu_w=d_ws, sgu_b=d_bs[:, :G].T,
                       norm_ffn_g=d_g2, norm_final_g=d_gf)
    (gather_small,), after = _exchange_start([[(_pack([small_grads[n] for n in small] + [jnp.zeros_like(gather_token)]), True)]],
                                             "gather_small_start")
    for key in ("ffn_up", "mid", "last"):
        after = update_group(key, after)
    small_parts = _exchange_wait(gather_small, after, "gather_small_wait")["lands"][0]
    res = _adamw(*packed, small_parts, "adamw_small")
    unpacked = [_unpack(t, [weights[n].shape for n in small]) for t in res]
    for i, n in enumerate(small):
        out[n] = [u[i] for u in unpacked]

    loss = lax.psum(loss_part[0, 0], ("x", "y", "c"))
    return (loss, dx[None], *[out[n][0] for n in names], *[out[n][1] for n in names],
            *[out[n][2] for n in names], *[out[n][3] for n in names])
```

```python
import jax
import jax.numpy as jnp
from jax import lax
from jax.experimental import pallas as pl
from jax.experimental.pallas import tpu as pltpu

F32 = jnp.float32
BF16 = jnp.bfloat16

N_DEV = 8
LANE = 128
SUBLANE = 8
HEAD = 64
SGU_CHUNK = 128
SGU_GROUP = 128
WKV_CHUNK = 64
RMS_EPS = 1e-6
LN_EPS = 1e-5
LNX_EPS = 64e-5
ADAM_LR, ADAM_B1, ADAM_B2, ADAM_EPS, ADAM_WD, ADAM_STEP = 0.001, 0.9, 0.999, 1e-08, 0.01, 10
VMEM_LIMIT_BYTES = 48 * 1024 * 1024
_SQRT_HALF = 0.7071067811865476
_INV_SQRT_2PI = 0.3989422804014327


def _pick(n, cands):
    for c in cands:
        if n % c == 0:
            return c
    return n


def _ceil_to(n, m):
    return -(-n // m) * m


def _params():
    return pltpu.CompilerParams(vmem_limit_bytes=VMEM_LIMIT_BYTES)


def _tile(n, cap):
    best = 0
    for d in range(LANE, min(n, cap) + 1, LANE):
        if n % d == 0:
            best = d
    return best or n


def _matmul_tiles(M, N, K, a_bytes, b_bytes, o_bytes, has_add, forced):
    tm = forced.get("m") or _tile(M, 1024)
    tn = forced.get("n") or _tile(N, 1024)
    tk = forced.get("k") or _tile(K, 2048)

    def vmem(tm, tn, tk):
        acc = tm * tn * 4 if tk < K else 0
        return 2 * (tm * tk * a_bytes + tk * tn * b_bytes + tm * tn * (o_bytes + (4 if has_add else 0))) + acc

    while vmem(tm, tn, tk) > (VMEM_LIMIT_BYTES * 3) // 4:
        if "k" not in forced and tk > 512 and _tile(K, tk // 2) < tk:
            tk = _tile(K, tk // 2)
        elif "m" not in forced and _tile(M, tm // 2) < tm:
            tm = _tile(M, tm // 2)
        else:
            break
    return tm, tn, tk


def _matmul(a, b, *, mode, out_dtype=F32, name, add=None, deps=(), out_blocks=0, epi=None, b2=None):
    def view(x):
        return (x.shape[1], x.shape[0] * x.shape[2], x.shape[2]) if x.ndim == 3 else (x.shape[0], x.shape[1], 0)

    (ar, ac, aw), (br, bc, bw) = view(a), view(b)
    a_col, b_col = {"nn": ("k", "n"), "nt": ("k", "k"), "tn": ("m", "n")}[mode]
    if mode == "nn":
        M, K, K2, N = ar, ac, br, bc
    elif mode == "nt":
        M, K, N, K2 = ar, ac, br, bc
    else:
        K, M, K2, N = ar, ac, br, bc
    assert K == K2, (a.shape, b.shape, mode)
    forced = {}
    for dim, w in ((a_col, aw), (b_col, bw), ("n", N // out_blocks if out_blocks else 0)):
        if w:
            assert forced.get(dim, w) == w
            forced[dim] = w
    has_add = add is not None
    tile_bytes = (sum(jnp.dtype(d).itemsize for d in epi[2]) + sum((e[0] if isinstance(e, tuple) else e).dtype.itemsize for e in epi[1])
                  if epi is not None else jnp.dtype(out_dtype).itemsize)
    tm, tn, tk = _matmul_tiles(M, N, K, a.dtype.itemsize, b.dtype.itemsize, tile_bytes, has_add, forced)
    kb = 1
    if "k" in forced and mode != "tn":
        lanes_ok = all(w or tk % LANE == 0 for w in (aw, bw if mode == "nt" else 1))
        kb = next(c for c in (4, 2, 1) if (K // tk) % c == 0 and (c == 1 or (lanes_ok and c * tk <= 1536)))
    nk = K // (tk * kb)
    dn = {"nn": (((1,), (0,)), ((), ())), "nt": (((1,), (1,)), ((), ())), "tn": (((0,), (0,)), ((), ()))}[mode]
    pick = {"m": lambda i, j, k: i, "n": lambda i, j, k: j, "k": lambda i, j, k: k}
    size = {"m": tm, "n": tn, "k": tk}

    def spec(blocked, row_dim, col_dim):
        rf, cf = pick[row_dim], pick[col_dim]
        reps = {d: (kb if d == "k" else 1) for d in (row_dim, col_dim)}
        if blocked:
            lead = kb if col_dim == "k" and kb > 1 else None
            return pl.BlockSpec((lead, size[row_dim], size[col_dim]), lambda i, j, k: (cf(i, j, k), rf(i, j, k), 0))
        return pl.BlockSpec((size[row_dim] * reps[row_dim], size[col_dim] * reps[col_dim]), lambda i, j, k: (rf(i, j, k), cf(i, j, k)))

    def k_part(ref, blocked, k_on_rows, j):
        if kb == 1:
            return ref[...]
        if blocked:
            return ref[j]
        return ref[j * tk:(j + 1) * tk, :] if k_on_rows else ref[:, j * tk:(j + 1) * tk]

    a_spec = spec(aw, "k" if mode == "tn" else "m", a_col)
    b_spec = spec(bw, "n" if mode == "nt" else "k", b_col)
    o_spec = spec(out_blocks, "m", "n")
    epi_fn, epi_ins, epi_dtypes = epi if epi is not None else (None, [], [out_dtype])
    epi_ins = [e if isinstance(e, tuple) else (e, None) for e in epi_ins]
    n_epi = len(epi_ins)
    twin = b2 is not None
    assert not twin or (nk == 1 and kb == 1 and epi is not None and b2.shape == b.shape)
    n_in = 2 + twin + has_add + n_epi + len(deps)
    n_out = len(epi_dtypes)

    def body(*refs):
        a_ref, b_ref = refs[0], refs[1]
        add_ref = refs[2 + twin] if has_add else None
        epi_refs = refs[2 + twin + has_add:2 + twin + has_add + n_epi]
        o_refs = refs[n_in:n_in + n_out]
        part = None
        for q in range(kb):
            a_q = k_part(a_ref, aw and a_col == "k", False, q)
            b_q = k_part(b_ref, bw and b_col == "k", mode == "nn", q)
            prod = lax.dot_general(a_q.astype(BF16), b_q.astype(BF16), dn, preferred_element_type=F32)
            part = prod if part is None else part + prod
        second = [lax.dot_general(a_ref[...].astype(BF16), refs[2][...].astype(BF16), dn, preferred_element_type=F32)] if twin else []

        def finish(res):
            outs = epi_fn(res, *second, *[e[...] for e in epi_refs]) if epi_fn is not None else (res,)
            for o_ref, val in zip(o_refs, outs):
                o_ref[...] = val.astype(o_ref.dtype)

        if nk == 1:
            finish(part + add_ref[...] if has_add else part)
            return
        acc_ref = refs[-1]
        kk = pl.program_id(2)

        @pl.when(kk == 0)
        def _():
            acc_ref[...] = part + add_ref[...] if has_add else part

        @pl.when(kk > 0)
        def _():
            acc_ref[...] += part

        @pl.when(kk == nk - 1)
        def _():
            finish(acc_ref[...])

    def epi_spec(arr, off):
        if off is None:
            return o_spec
        assert off % tn == 0
        return pl.BlockSpec((tm, tn), lambda i, j, k: (i, j + off // tn))

    ins = [a, b] + ([b2] if twin else []) + ([add] if has_add else []) + [arr for arr, _ in epi_ins] + list(deps)
    in_specs = ([a_spec, b_spec] + ([b_spec] if twin else []) + ([o_spec] if has_add else []) + [epi_spec(arr, off) for arr, off in epi_ins]
                + [pl.BlockSpec(d.shape, lambda i, j, k, nd=d.ndim: (0,) * nd) for d in deps])
    o_shape = (out_blocks, M, tn) if out_blocks else (M, N)
    res = pl.pallas_call(
        body, name=name, grid=(M // tm, N // tn, nk), in_specs=in_specs, out_specs=[o_spec] * n_out,
        out_shape=[jax.ShapeDtypeStruct(o_shape, dt) for dt in epi_dtypes],
        scratch_shapes=[pltpu.VMEM((tm, tn), F32)] if nk > 1 else [],
        compiler_params=_params())(*ins)
    return res[0] if epi is None else list(res)


def _rowwise(fn, rows, pars, row_outs, acc_outs, *, name, tm=256, deps=()):
    rows = [r if isinstance(r, tuple) else (r, r.shape[1], 0) for r in rows]
    row_outs = [o if len(o) == 5 else (o[0], o[1], o[0], 0, None) for o in row_outs]
    aliased = [(k, o[4]) for k, o in enumerate(row_outs) if o[4] is not None]
    R = rows[0][0].shape[0]
    if max(w for _, w, _ in rows) > 4096:
        tm = tm // 2
    tm = min(tm, R)
    assert R % tm == 0
    nr, npar = len(rows), len(pars)
    nro = len(row_outs)
    n_in = nr + npar + len(deps) + len(aliased)

    def body(*refs):
        rv = [r[...] for r in refs[:nr]]
        pv = [p[...] for p in refs[nr:nr + npar]]
        outs = refs[n_in:]
        ro, ao = fn(rv, pv)
        first = pl.program_id(0) == 0
        for o_ref, val in zip(outs[:nro], ro):
            o_ref[...] = val.astype(o_ref.dtype)

        @pl.when(first)
        def _():
            for o_ref, val in zip(outs[nro:], ao):
                o_ref[...] = val

        @pl.when(jnp.logical_not(first))
        def _():
            for o_ref, val in zip(outs[nro:], ao):
                o_ref[...] += val

    in_specs = ([pl.BlockSpec((tm, w), lambda i, cb=cb: (i, cb)) for _, w, cb in rows]
                + [pl.BlockSpec(p.shape, lambda i, nd=p.ndim: (0,) * nd) for p in list(pars) + list(deps)]
                + [pl.BlockSpec(memory_space=pl.ANY)] * len(aliased))
    out_shape = ([jax.ShapeDtypeStruct((R, full), dt) for _, dt, full, _, _ in row_outs]
                 + [jax.ShapeDtypeStruct(s, F32) for s in acc_outs])
    out_specs = ([pl.BlockSpec((tm, f), lambda i, cb=cb: (i, cb)) for f, _, _, cb, _ in row_outs]
                 + [pl.BlockSpec(s, lambda i, nd=len(s): (0,) * nd) for s in acc_outs])
    res = pl.pallas_call(body, name=name, grid=(R // tm,), in_specs=in_specs, out_specs=out_specs, out_shape=out_shape,
                         input_output_aliases={n_in - len(aliased) + q: k for q, (k, _) in enumerate(aliased)},
                         compiler_params=_params())(*[r for r, _, _ in rows], *pars, *deps, *[buf for _, buf in aliased])
    return list(res)


def _bdot(a, b, mode="nn"):
    dn = {"nn": (((1,), (0,)), ((), ())), "nt": (((1,), (1,)), ((), ())), "tn": (((0,), (0,)), ((), ()))}[mode]
    return lax.dot_general(a.astype(BF16), b.astype(BF16), dn, preferred_element_type=F32)


def _sigmoid(x):
    return jax.nn.sigmoid(x)


def _softplus(x):
    return jnp.maximum(x, 0.0) + jnp.log1p(jnp.exp(-jnp.abs(x)))


def _gelu(z):
    return 0.5 * z * (1.0 + lax.erf(z * _SQRT_HALF))


def _gelu_grad(z):
    return 0.5 * (1.0 + lax.erf(z * _SQRT_HALF)) + z * jnp.exp(-0.5 * z * z) * _INV_SQRT_2PI


def _mean(x):
    return jnp.mean(x, axis=-1, keepdims=True)


def _colsum(x):
    return jnp.sum(x, axis=0, keepdims=True)


def _rms_fwd(x, g, name, deps=()):
    def fn(rv, pv):
        (xv,), (gv,) = rv, pv
        r = lax.rsqrt(_mean(xv * xv) + RMS_EPS)
        return [xv * r * gv], []
    return _rowwise(fn, [x], [g], [(x.shape[1], BF16)], [], name=name, deps=deps)[0]


def _rms_bwd(dn, x, dres, g, name, deps=()):
    def fn(rv, pv):
        (dnv, xv, drv), (gv,) = rv, pv
        r = lax.rsqrt(_mean(xv * xv) + RMS_EPS)
        yn = xv * r
        dyg = dnv * gv
        dx = drv + r * (dyg - yn * _mean(dyg * yn))
        return [dx, dx], [_colsum(dnv * yn)]
    D = x.shape[1]
    return _rowwise(fn, [dn, x, dres], [g], [(D, F32), (D, BF16)], [(1, D)], name=name, deps=deps)


def _rwkv_layout(RW, Lw, La, Lg, D):
    widths = [RW, RW, RW, Lw, La, Lg]
    pw = [_ceil_to(w, LANE) for w in widths]
    pw[5] += _ceil_to(sum(pw), 2 * D) - sum(pw)
    offs = [sum(pw[:i]) for i in range(6)]
    return widths, pw, offs, sum(pw)


def _pad_rwkv_cols(a, lay):
    widths, pw, _, _ = lay
    pieces, src = [], 0
    for w, p in zip(widths, pw):
        pieces.append(a[:, src:src + w])
        if p > w:
            pieces.append(jnp.zeros((a.shape[0], p - w), a.dtype))
        src += w
    return jnp.concatenate(pieces, axis=1)


def _unpad_rwkv_cols(a, lay):
    widths, _, offs, _ = lay
    return jnp.concatenate([a[:, o:o + w] for o, w in zip(offs, widths)], axis=1)


def _proj_pieces(lay, D, cs):
    widths, _, offs, rcp = lay
    rc = sum(widths)
    segs = [(sum(widths[:j]), widths[j], offs[j]) for j in range(6)] + [(rc, D, rcp + 2 * D), (rc + D, D, rcp), (rc + 2 * D, D, rcp + D)]
    pieces = []
    for start, width, dst in segs:
        n = start
        while n < start + width:
            d, off = divmod(n, cs)
            take = min(cs - off, start + width - n)
            pieces.append((d, off, dst + n - start, take))
            n += take
    return pieces


def _w_in_to_proj(g, lay, D, name):
    nb, rows, cs = g.shape
    icp = lay[3] + 3 * D
    pieces = _proj_pieces(lay, D, cs)
    tm = _pick(rows, (256, 128, 64, 32, 16))

    def body(i_ref, o_ref):
        o_ref[...] = jnp.zeros_like(o_ref)
        for d, src, dst, w in pieces:
            o_ref[:, dst:dst + w] = i_ref[d, :, src:src + w]

    return pl.pallas_call(
        body, name=name, grid=(rows // tm,), in_specs=[pl.BlockSpec((nb, tm, cs), lambda i: (0, i, 0))],
        out_specs=pl.BlockSpec((tm, icp), lambda i: (i, 0)), out_shape=jax.ShapeDtypeStruct((rows, icp), g.dtype),
        compiler_params=_params())(g)


def _dw_in_from_proj(a, lay, D, cs, name):
    rows, icp = a.shape
    pieces = _proj_pieces(lay, D, cs)
    tm = _pick(rows, (256, 128, 64, 32, 16))

    def body(i_ref, o_ref):
        for d, src, dst, w in pieces:
            o_ref[d, :, src:src + w] = i_ref[:, dst:dst + w]

    return pl.pallas_call(
        body, name=name, grid=(rows // tm,), in_specs=[pl.BlockSpec((tm, icp), lambda i: (i, 0))],
        out_specs=pl.BlockSpec((N_DEV, tm, cs), lambda i: (0, i, 0)), out_shape=jax.ShapeDtypeStruct((N_DEV, rows, cs), a.dtype),
        compiler_params=_params())(a)


def _pad_rows(a, rows):
    return a if a.shape[0] == rows else jnp.concatenate([a, jnp.zeros((rows - a.shape[0], a.shape[1]), a.dtype)], axis=0)


def _token_shift(p, halo, mu, i):
    tm = p.shape[0]
    hid = lax.broadcasted_iota(jnp.int32, (SUBLANE, 1), 0)
    before = jnp.sum(jnp.where(hid == SUBLANE - 1, halo, 0.0), axis=0, keepdims=True)
    before = jnp.where(i == 0, 0.0, before)
    rid = lax.broadcasted_iota(jnp.int32, (tm, 1), 0)
    prev = jnp.where(rid == 0, before, pltpu.roll(p, 1, 0))
    d = prev - p
    return p + d * mu, d


def _rwkv_math(ps, w0, a0, k_k, k_a, wlw, wla, wlg, lay):
    _, pw, offs, _ = lay
    r, k, v, xw, xa, xg = (ps[:, offs[j]:offs[j] + pw[j]] for j in range(6))
    tw = jnp.tanh(xw)
    ww = w0 + _bdot(tw, wlw)
    lw = -jnp.exp(-_softplus(-ww) - 0.5)
    a = _sigmoid(a0 + _bdot(xa, wla))
    sg = _sigmoid(xg)
    g = _bdot(sg, wlg)
    return dict(r=r, k=k, v=v, xa=xa, tw=tw, ww=ww, lw=lw, a=a, sg=sg, g=g, kkp=k * k_k, k2=k * (1.0 + (a - 1.0) * k_a))


def _halo_specs(T, tm, width, after):
    hb = tm // SUBLANE
    last = T // SUBLANE - 1
    if after:
        return pl.BlockSpec((SUBLANE, width), lambda i: (jnp.minimum((i + 1) * hb, last), 0))
    return pl.BlockSpec((SUBLANE, width), lambda i: (jnp.maximum(i * hb - 1, 0), 0))


def _rowsum(x):
    return jnp.sum(x, axis=-1, keepdims=True)


def _kk_math(kkp):
    nrm = jnp.sqrt(_rowsum(kkp * kkp))
    inv = 1.0 / jnp.maximum(nrm, 1e-12)
    return nrm, inv, kkp * inv


def _rwkv_pre(p, mu, small, lora, lay, name):
    T, rcp = p.shape[0], lay[3]
    H = lay[0][0] // HEAD
    tm = min(128, T)

    def body(p_ref, ph_ref, mu_ref, w0_ref, a0_ref, kk_ref, ka_ref, wlw_ref, wla_ref, wlg_ref, r_o, lw_o, k2_o, v_o, aa_o, bb_o, g_o):
        ps, _ = _token_shift(p_ref[...], ph_ref[...], mu_ref[...], pl.program_id(0))
        q = _rwkv_math(ps, w0_ref[...], a0_ref[...], kk_ref[...], ka_ref[...], wlw_ref[...], wla_ref[...], wlg_ref[...], lay)
        for h in range(H):
            sl = slice(h * HEAD, (h + 1) * HEAD)
            for o_ref, key in ((r_o, "r"), (lw_o, "lw"), (k2_o, "k2"), (v_o, "v"), (g_o, "g")):
                o_ref[h] = q[key][:, sl]
            _, _, kk = _kk_math(q["kkp"][:, sl])
            aa_o[h] = -kk
            bb_o[h] = kk * q["a"][:, sl]

    whole = lambda arr: pl.BlockSpec(arr.shape, lambda i: (0, 0))
    return pl.pallas_call(
        body, name=name, grid=(T // tm,),
        in_specs=([pl.BlockSpec((tm, rcp), lambda i: (i, 0)), _halo_specs(T, tm, rcp, False), whole(mu)]
                  + [whole(s) for s in small] + [whole(w) for w in lora]),
        out_specs=[pl.BlockSpec((H, tm, HEAD), lambda i: (0, i, 0))] * 7, out_shape=[jax.ShapeDtypeStruct((H, T, HEAD), F32)] * 7,
        compiler_params=_params())(p, p, mu, *small, *lora)


def _rwkv_pre_bwd(p, mu, small, lora, hgrads, lay, name):
    T, rcp = p.shape[0], lay[3]
    widths, pw, offs, _ = lay
    RW = widths[0]
    H = RW // HEAD
    tm = min(128, T)

    def body(p_ref, ph_ref, mu_ref, w0_ref, a0_ref, kk_ref, ka_ref, wlw_ref, wla_ref, wlg_ref,
             dr_h, dk2_h, dv_h, dlw_h, daa, dbb, dg_h,
             dps_ref, dmu_ref, dw0_ref, da0_ref, dkk_ref, dka_ref, dwlw_ref, dwla_ref, dwlg_ref,
             s_dr, s_dk2, s_dv, s_dlw, s_dkkp, s_da, s_dg):
        i = pl.program_id(0)
        ps, dprev = _token_shift(p_ref[...], ph_ref[...], mu_ref[...], i)
        k_k, k_a = kk_ref[...], ka_ref[...]
        q = _rwkv_math(ps, w0_ref[...], a0_ref[...], k_k, k_a, wlw_ref[...], wla_ref[...], wlg_ref[...], lay)
        k, a, lw, ww, tw, sg = q["k"], q["a"], q["lw"], q["ww"], q["tw"], q["sg"]
        for h in range(H):
            sl = slice(h * HEAD, (h + 1) * HEAD)
            s_dr[:, sl] = dr_h[h]
            s_dk2[:, sl] = dk2_h[h]
            s_dv[:, sl] = dv_h[h]
            s_dlw[:, sl] = dlw_h[h]
            s_dg[:, sl] = dg_h[h]
            nrm, inv, kk = _kk_math(q["kkp"][:, sl])
            dbb_h = dbb[h]
            dkk = dbb_h * a[:, sl] - daa[h]
            s_dkkp[:, sl] = jnp.where(nrm > 1e-12, inv * (dkk - kk * _rowsum(dkk * kk)), dkk * inv)
            s_da[:, sl] = dbb_h * kk
        dk2, dkkp, dg = s_dk2[...], s_dkkp[...], s_dg[...]
        dk = dk2 * (1.0 + (a - 1.0) * k_a) + dkkp * k_k
        da = s_da[...] + dk2 * k * k_a
        dpa = da * a * (1.0 - a)
        dww = s_dlw[...] * lw * _sigmoid(-ww)
        dxa = _bdot(dpa, wla_ref[...], "nt")
        dxw = _bdot(dww, wlw_ref[...], "nt") * (1.0 - tw * tw)
        dxg = _bdot(dg, wlg_ref[...], "nt") * sg * (1.0 - sg)
        segs = (s_dr[...], dk, s_dv[...], dxw, dxa, dxg)
        sums = [dmu_ref, dw0_ref, da0_ref, dkk_ref, dka_ref, dwlw_ref, dwla_ref, dwlg_ref]

        @pl.when(i == 0)
        def _():
            for s in sums:
                s[...] = jnp.zeros_like(s)

        for j, seg in enumerate(segs):
            sl = slice(offs[j], offs[j] + pw[j])
            dps_ref[:, sl] = seg
            dmu_ref[:, sl] += _colsum(seg * dprev[:, sl])
        dw0_ref[...] += _colsum(dww)
        da0_ref[...] += _colsum(dpa)
        dkk_ref[...] += _colsum(dkkp * k)
        dka_ref[...] += _colsum(dk2 * k * (a - 1.0))
        dwlw_ref[...] += _bdot(tw, dww, "tn")
        dwla_ref[...] += _bdot(q["xa"], dpa, "tn")
        dwlg_ref[...] += _bdot(sg, dg, "tn")

    whole = lambda arr: pl.BlockSpec(arr.shape, lambda i: (0, 0))
    row = lambda w: pl.BlockSpec((tm, w), lambda i: (i, 0))
    acc_shapes = [(1, rcp), (1, RW), (1, RW), (1, RW), (1, RW)] + [w.shape for w in lora]
    return pl.pallas_call(
        body, name=name, grid=(T // tm,),
        in_specs=([row(rcp), _halo_specs(T, tm, rcp, False), whole(mu)] + [whole(s) for s in small] + [whole(w) for w in lora]
                  + [pl.BlockSpec((H, tm, HEAD), lambda i: (0, i, 0))] * 7),
        out_specs=[row(rcp)] + [pl.BlockSpec(s, lambda i: (0, 0)) for s in acc_shapes],
        out_shape=[jax.ShapeDtypeStruct((T, rcp), F32)] + [jax.ShapeDtypeStruct(s, F32) for s in acc_shapes],
        scratch_shapes=[pltpu.VMEM((tm, RW), F32)] * 7, compiler_params=_params())(p, p, mu, *small, *lora, *hgrads)


def _shift_bwd(dps, mu, dproj, name):
    T, rcp = dps.shape
    tm = min(256, T)
    nt = T // tm

    def body(d_ref, dh_ref, mu_ref, buf_ref, o_ref):
        i = pl.program_id(0)
        d = d_ref[...]
        hid = lax.broadcasted_iota(jnp.int32, (SUBLANE, 1), 0)
        after = jnp.sum(jnp.where(hid == 0, dh_ref[...], 0.0), axis=0, keepdims=True)
        after = jnp.where(i == nt - 1, 0.0, after)
        rid = lax.broadcasted_iota(jnp.int32, (tm, 1), 0)
        nxt = jnp.where(rid == tm - 1, after, pltpu.roll(d, tm - 1, 0))
        mu_v = mu_ref[...]
        o_ref[...] = (d * (1.0 - mu_v) + nxt * mu_v).astype(BF16)

    row = pl.BlockSpec((tm, rcp), lambda i: (i, 0))
    return pl.pallas_call(
        body, name=name, grid=(nt,),
        in_specs=[row, _halo_specs(T, tm, rcp, True), pl.BlockSpec(mu.shape, lambda i: (0, 0)), pl.BlockSpec(memory_space=pl.ANY)],
        out_specs=row, out_shape=jax.ShapeDtypeStruct(dproj.shape, BF16), input_output_aliases={3: 0},
        compiler_params=_params())(dps, dps, mu, dproj)


def _head_post_math(y, r, k2, v, lg, lb, rk):
    yc = y - _mean(y)
    rstd = lax.rsqrt(_mean(yc * yc) + LNX_EPS)
    yn = yc * rstd
    s = _rowsum(r * k2 * rk)
    return yn, rstd, yn * lg + lb + s * v, s


def _head_post(y, r, k2, v, g, hp, name, deps=()):
    H, T, _ = y.shape
    tm = min(128, T)

    def body(y_ref, r_ref, k_ref, v_ref, g_ref, lg_ref, lb_ref, rk_ref, *rest):
        o_ref = rest[-1]
        _, _, t, _ = _head_post_math(y_ref[...], r_ref[...], k_ref[...], v_ref[...], lg_ref[...], lb_ref[...], rk_ref[...])
        out = (t * g_ref[...]).astype(BF16)
        for h in range(H):
            o_ref[:, h * HEAD:(h + 1) * HEAD] = out[h]

    blk = pl.BlockSpec((H, tm, HEAD), lambda i: (0, i, 0))
    par = pl.BlockSpec((H, 1, HEAD), lambda i: (0, 0, 0))
    return pl.pallas_call(
        body, name=name, grid=(T // tm,),
        in_specs=[blk] * 5 + [par] * 3 + [pl.BlockSpec(d.shape, lambda i, nd=d.ndim: (0,) * nd) for d in deps],
        out_specs=pl.BlockSpec((tm, H * HEAD), lambda i: (i, 0)),
        out_shape=jax.ShapeDtypeStruct((T, H * HEAD), BF16), compiler_params=_params())(y, r, k2, v, g, *hp, *deps)


def _bmm(x, y, mode):
    dn = {"nn": (((2,), (1,)), ((0,), (0,))), "nt": (((2,), (2,)), ((0,), (0,))), "tn": (((1,), (1,)), ((0,), (0,)))}[mode]
    (xh, xl), (yh, yl) = _split(x), _split(y)
    dot = lambda p, q: lax.dot_general(p, q, dn, preferred_element_type=F32)
    out = dot(xh, yh)
    if yl is not None:
        out = out + dot(xh, yl)
    if xl is not None:
        out = out + dot(xl, yh)
    return out


def _split(x):
    if isinstance(x, tuple):
        return x
    hi = x.astype(BF16)
    return hi, (x - hi.astype(F32)).astype(BF16)


def _exact(x):
    return x.astype(BF16), None


def _round(x):
    return x if isinstance(x, tuple) else (x.astype(BF16), None)


def _rows(*xs):
    if isinstance(xs[0], tuple):
        return tuple(None if any(p is None for p in parts) else jnp.concatenate(parts, axis=1) for parts in zip(*xs))
    return jnp.concatenate(xs, axis=1)


def _wkv_chunk(r, lw, k, v, a, b, inverse=None):
    hb, C, _ = r.shape
    ti = lax.broadcasted_iota(jnp.int32, (C, C), 0)
    si = lax.broadcasted_iota(jnp.int32, (C, C), 1)
    linc, lstr, eye = (ti >= si).astype(F32), (ti > si).astype(F32), (ti == si).astype(F32)
    qmask = jnp.concatenate([jnp.concatenate([lstr, lstr], axis=1), jnp.concatenate([linc, linc], axis=1)], axis=0)
    lincb = _exact(jnp.broadcast_to(linc, (hb, C, C)))
    both = _exact(jnp.broadcast_to(jnp.concatenate([linc, lstr], axis=0), (hb, 2 * C, C)))
    ones = _exact(jnp.ones_like(v))
    lws = _split(lw)
    ci = _bmm(lincb, lws, "nn")
    cC = jnp.sum(lw, axis=1, keepdims=True)
    gi, ge, gn, gr = jnp.exp(ci), jnp.exp(ci - lw), jnp.exp(-ci), jnp.exp(cC - ci)
    q = dict(At=a * ge, Rt=r * gi, Bt=b * gn, Kt=k * gn, Bh=b * gr, Kh=k * gr)
    s = dict(AR=_round(_rows(q["At"], q["Rt"])), BK=_round(_rows(q["Bt"], q["Kt"])), BKh=_round(_rows(q["Bh"], q["Kh"])), v=_round(v))
    quad = _bmm(s["AR"], s["BK"], "nt") * qmask
    s["top"], s["bot"] = _round(quad[:, :C]), _round(quad[:, C:])
    if inverse is None:
        A_ab = quad[:, :C, :C]
        Tm = eye + A_ab
        Pw = _round(A_ab)
        n = 1
        while 2 * n < C:
            Pw = _round(_bmm(Pw, Pw, "nn"))
            Tm = Tm + _bmm(_round(Tm), Pw, "nn")
            n *= 2
        inverse = Tm
    s["Tm"] = _round(inverse)
    gC = jnp.exp(_bmm(lws, ones, "tn"))
    q.update(gi=gi, ge=ge, gn=gn, gr=gr, qmask=qmask, both=both, gC=gC, ones=ones, s=s)
    return q


def _wkv_u(s, H0s, C):
    arh = _bmm(s["AR"], H0s, "nn")
    zv = _rows(tuple(None if p is None else jnp.zeros_like(p) for p in s["v"]), s["v"])
    U = _bmm(s["Tm"], _round(arh[:, :C] + _bmm(s["top"], zv, "nn")), "nn")
    return arh, _rows(_round(U), s["v"])


def _wkv_fwd(r, lw, k, v, a, b, name):
    H, T, N = r.shape
    C = min(WKV_CHUNK, T)
    nc = T // C
    hb = _pick(H, (16, 8, 4, 2))

    def body(r_ref, lw_ref, k_ref, v_ref, a_ref, b_ref, y_ref, st_ref, inv_ref, u_ref, h_ref):
        @pl.when(pl.program_id(1) == 0)
        def _():
            h_ref[...] = jnp.zeros_like(h_ref)

        H0 = h_ref[...]
        st_ref[0] = H0
        q = _wkv_chunk(r_ref[...], lw_ref[...], k_ref[...], v_ref[...], a_ref[...], b_ref[...])
        s = q["s"]
        arh, UV = _wkv_u(s, _round(H0), C)
        inv_ref[0] = s["Tm"][0]
        u_ref[...] = UV[0][:, :C]
        y_ref[...] = arh[:, C:] + _bmm(s["bot"], UV, "nn")
        h_ref[...] = q["gC"] * H0 + _bmm(s["BKh"], UV, "tn")

    blk = pl.BlockSpec((hb, C, N), lambda h, c: (h, c, 0))
    per_chunk = lambda w: pl.BlockSpec((1, hb, w, w), lambda h, c: (c, h, 0, 0))
    return pl.pallas_call(
        body, name=name, grid=(H // hb, nc), in_specs=[blk] * 6, out_specs=[blk, per_chunk(N), per_chunk(C), blk],
        out_shape=[jax.ShapeDtypeStruct((H, T, N), F32), jax.ShapeDtypeStruct((nc, H, N, N), F32),
                   jax.ShapeDtypeStruct((nc, H, C, C), BF16), jax.ShapeDtypeStruct((H, T, N), BF16)],
        scratch_shapes=[pltpu.VMEM((hb, N, N), F32)], compiler_params=_params())(r, lw, k, v, a, b)


def _wkv_bwd(r, lw, k, v, a, b, states, inverses, u, y, g, hp, dya, name, deps=()):
    H, T, N = r.shape
    C = min(WKV_CHUNK, T)
    nc = T // C
    hb = _pick(H, (16, 8, 4, 2))
    hsum = lambda t: jnp.sum(t, axis=1, keepdims=True)

    def body(r_ref, lw_ref, k_ref, v_ref, a_ref, b_ref, st_ref, inv_ref, u_ref, y_ref, g_ref, lg_ref, lb_ref, rk_ref, dya_ref, *rest):
        (dr_ref, dlw_ref, dk_ref, dv_ref, da_ref, db_ref, dg_ref, dlg_ref, dlb_ref, drk_ref, dh_ref, d_s) = rest[len(deps):]
        first = pl.program_id(1) == 0

        @pl.when(first)
        def _():
            dh_ref[...] = jnp.zeros_like(dh_ref)

        for h in range(hb):
            d_s[h] = dya_ref[:, h * N:(h + 1) * N]
        d_v, r_v, k_v, v_v, lg, rk = d_s[...], r_ref[...], k_ref[...], v_ref[...], lg_ref[...], rk_ref[...]
        yn, rstd, t, bonus = _head_post_math(y_ref[...], r_v, k_v, v_v, lg, lb_ref[...], rk)
        dyo = d_v * g_ref[...]
        dyn = dyo * lg
        ds = _rowsum(dyo * v_v)
        dy = rstd * (dyn - _mean(dyn) - yn * _mean(dyn * yn))
        dg_ref[...] = d_v * t
        sums = (hsum(dyo * yn), hsum(dyo), hsum(ds * r_v * k_v))

        @pl.when(first)
        def _():
            for o_ref, val in zip((dlg_ref, dlb_ref, drk_ref), sums):
                o_ref[...] = val

        @pl.when(jnp.logical_not(first))
        def _():
            for o_ref, val in zip((dlg_ref, dlb_ref, drk_ref), sums):
                o_ref[...] += val

        dHC = dh_ref[...]
        H0 = st_ref[0]
        q = _wkv_chunk(r_v, lw_ref[...], k_v, v_v, a_ref[...], b_ref[...], inverse=inv_ref[0])
        s, gC = q["s"], q["gC"]
        H0s, dHs, dY = _round(H0), _round(dHC), _round(dy)
        UV = _rows(_round(u_ref[...]), s["v"])
        bot_dy = _bmm(s["bot"], dY, "tn")
        bkh_dh = _bmm(s["BKh"], dHs, "nn")
        dP = _round(_bmm(s["Tm"], _round(bot_dy[:, :C] + bkh_dh[:, :C]), "tn"))
        dv_ref[...] = bot_dy[:, C:] + bkh_dh[:, C:] + _bmm(s["top"], dP, "tn")[:, C:] + dyo * bonus
        dPY = _rows(dP, dY)
        dh_ref[...] = gC * dHC + _bmm(s["AR"], dPY, "tn")
        dquad = _round(_bmm(dPY, UV, "nt") * q["qmask"])
        dAR = _bmm(dPY, H0s, "nt") + _bmm(dquad, s["BK"], "nn")
        dBK = _bmm(dquad, s["AR"], "tn")
        dBKh = _bmm(UV, dHs, "nt")
        dAt, dRt, dBt, dKt, dBh, dKh = dAR[:, :C], dAR[:, C:], dBK[:, :C], dBK[:, C:], dBKh[:, :C], dBKh[:, C:]
        dr_ref[...] = dRt * q["gi"] + ds * k_v * rk
        da_ref[...] = dAt * q["ge"]
        db_ref[...] = dBt * q["gn"] + dBh * q["gr"]
        dk_ref[...] = dKt * q["gn"] + dKh * q["gr"] + ds * r_v * rk
        tail = dBh * q["Bh"] + dKh * q["Kh"]
        dci = dRt * q["Rt"] - dBt * q["Bt"] - dKt * q["Kt"] - tail
        dcC = jnp.sum(tail, axis=1, keepdims=True) + _bmm(q["ones"], H0 * dHC * gC, "nt")
        dlw_ref[...] = _bmm(q["both"], _rows(dci, dAt * q["At"]), "tn") + dcC

    blk = pl.BlockSpec((hb, C, N), lambda h, c: (h, nc - 1 - c, 0))
    per_chunk = lambda w: pl.BlockSpec((1, hb, w, w), lambda h, c: (nc - 1 - c, h, 0, 0))
    par = pl.BlockSpec((hb, 1, N), lambda h, c: (h, 0, 0))
    return pl.pallas_call(
        body, name=name, grid=(H // hb, nc),
        in_specs=([blk] * 6 + [per_chunk(N), per_chunk(C), blk, blk, blk] + [par] * 3
                  + [pl.BlockSpec((C, hb * N), lambda h, c: (nc - 1 - c, h))]
                  + [pl.BlockSpec(d.shape, lambda h, c, nd=d.ndim: (0,) * nd) for d in deps]),
        out_specs=[blk] * 7 + [par] * 3,
        out_shape=[jax.ShapeDtypeStruct((H, T, N), F32)] * 7 + [jax.ShapeDtypeStruct((H, 1, N), F32)] * 3,
        scratch_shapes=[pltpu.VMEM((hb, N, N), F32), pltpu.VMEM((hb, C, N), F32)],
        compiler_params=_params())(r, lw, k, v, a, b, states, inverses, u, y, g, *hp, dya, *deps)


def _sgu_ln(z, SW, lng, lnb):
    ge = _gelu(z)
    u, vv = ge[:, :SW], ge[:, SW:]
    xc = vv - _mean(vv)
    rstd = lax.rsqrt(_mean(xc * xc) + LN_EPS)
    vn = xc * rstd
    return u, vn, rstd, vn * lng + lnb


def _causal(ws_ref, g):
    ti = lax.broadcasted_iota(jnp.int32, (SGU_CHUNK, SGU_CHUNK), 0)
    si = lax.broadcasted_iota(jnp.int32, (SGU_CHUNK, SGU_CHUNK), 1)
    return ti >= si, jnp.where(ti >= si, ws_ref[g], 0.0).astype(BF16)


def _sgu_fwd(proj, zblock, lng, lnb, ws, bexp, name):
    T, SW = proj.shape[0], lng.shape[1]
    G = ws.shape[0]
    tr = min(256, T)
    nch = tr // SGU_CHUNK

    def body(z_ref, lng_ref, lnb_ref, ws_ref, be_ref, o_ref):
        u, _, _, vl = _sgu_ln(z_ref[...], SW, lng_ref[...], lnb_ref[...])
        for g in range(G):
            cs = slice(g * SGU_GROUP, (g + 1) * SGU_GROUP)
            _, wc = _causal(ws_ref, g)
            for n in range(nch):
                rs = slice(n * SGU_CHUNK, (n + 1) * SGU_CHUNK)
                m = jnp.dot(wc, vl[rs, cs].astype(BF16), preferred_element_type=F32) + be_ref[:, cs]
                o_ref[rs, cs] = (u[rs, cs] * m).astype(BF16)

    whole = lambda arr: pl.BlockSpec(arr.shape, lambda i, nd=arr.ndim: (0,) * nd)
    return pl.pallas_call(
        body, name=name, grid=(T // tr,),
        in_specs=[pl.BlockSpec((tr, 2 * SW), lambda i: (i, zblock)), whole(lng), whole(lnb), whole(ws), whole(bexp)],
        out_specs=pl.BlockSpec((tr, SW), lambda i: (i, 0)), out_shape=jax.ShapeDtypeStruct((T, SW), BF16),
        compiler_params=_params())(proj, lng, lnb, ws, bexp)


def _sgu_bwd(proj, zblock, dyb, lng, lnb, ws, bexp, dproj, name):
    T, SW = proj.shape[0], lng.shape[1]
    G = ws.shape[0]
    tr = min(256, T)
    nch = tr // SGU_CHUNK
    nt = T // tr

    def body(z_ref, dy_ref, lng_ref, lnb_ref, ws_ref, be_ref, buf_ref, dz_ref, dlg_ref, dlb_ref, dws_ref, db_ref, du_s, dvl_s, dbacc_s):
        i = pl.program_id(0)
        zv = z_ref[...]
        lng_v = lng_ref[...]
        u, vn, rstd, vl = _sgu_ln(zv, SW, lng_v, lnb_ref[...])

        @pl.when(i == 0)
        def _():
            for s in (dlg_ref, dlb_ref, dws_ref, dbacc_s):
                s[...] = jnp.zeros_like(s)

        for g in range(G):
            cs = slice(g * SGU_GROUP, (g + 1) * SGU_GROUP)
            tri, wc = _causal(ws_ref, g)
            for n in range(nch):
                rs = slice(n * SGU_CHUNK, (n + 1) * SGU_CHUNK)
                blk = vl[rs, cs].astype(BF16)
                m = jnp.dot(wc, blk, preferred_element_type=F32) + be_ref[:, cs]
                dyv = dy_ref[rs, cs]
                du_s[rs, cs] = dyv * m
                dm = dyv * u[rs, cs]
                dvl_s[rs, cs] = _bdot(wc, dm, "tn")
                dws_ref[g] += jnp.where(tri, _bdot(dm, blk, "nt"), 0.0)
                dbacc_s[:, cs] += dm

        dvl = dvl_s[...]
        dlg_ref[...] += _colsum(dvl * vn)
        dlb_ref[...] += _colsum(dvl)
        dvn = dvl * lng_v
        dvv = rstd * (dvn - _mean(dvn) - vn * _mean(dvn * vn))
        gp = _gelu_grad(zv)
        dz_ref[:, :SW] = (du_s[...] * gp[:, :SW]).astype(BF16)
        dz_ref[:, SW:] = (dvv * gp[:, SW:]).astype(BF16)

        @pl.when(i == nt - 1)
        def _():
            lane = lax.broadcasted_iota(jnp.int32, (SGU_CHUNK, LANE), 1)
            out = jnp.zeros((SGU_CHUNK, LANE), F32)
            for g in range(G):
                col = jnp.sum(dbacc_s[:, g * SGU_GROUP:(g + 1) * SGU_GROUP], axis=1, keepdims=True)
                out = jnp.where(lane == g, col, out)
            db_ref[...] = out

    whole = lambda arr: pl.BlockSpec(arr.shape, lambda i, nd=arr.ndim: (0,) * nd)
    acc_shapes = [(1, SW), (1, SW), ws.shape, (SGU_CHUNK, LANE)]
    return pl.pallas_call(
        body, name=name, grid=(nt,),
        in_specs=[pl.BlockSpec((tr, 2 * SW), lambda i: (i, zblock)), pl.BlockSpec((tr, SW), lambda i: (i, 0)),
                  whole(lng), whole(lnb), whole(ws), whole(bexp), pl.BlockSpec(memory_space=pl.ANY)],
        out_specs=([pl.BlockSpec((tr, 2 * SW), lambda i: (i, zblock))]
                   + [pl.BlockSpec(s, lambda i, nd=len(s): (0,) * nd) for s in acc_shapes]),
        out_shape=[jax.ShapeDtypeStruct(dproj.shape, BF16)] + [jax.ShapeDtypeStruct(s, F32) for s in acc_shapes],
        scratch_shapes=[pltpu.VMEM((tr, SW), F32), pltpu.VMEM((tr, SW), F32), pltpu.VMEM((SGU_CHUNK, SW), F32)],
        input_output_aliases={6: 0}, compiler_params=_params())(proj, dyb, lng, lnb, ws, bexp, dproj)


_HBM = pl.BlockSpec(memory_space=pltpu.HBM)
_SEM = pl.BlockSpec(memory_space=pltpu.SEMAPHORE)
_DATAFLOW = pltpu.SideEffectType.DATAFLOW_SIDE_EFFECTING


def _mesh_place(chips=False):
    x, y, c = lax.axis_index("x"), lax.axis_index("y"), lax.axis_index("c")
    return x, y, c, (2 * x + y if chips else 4 * x + 2 * y + c)


def _peer(x, y, c, rel, chips=False):
    px = 1 - x if rel & 4 else x
    py = 1 - y if rel & 2 else y
    pc = 1 - c if rel & 1 else c
    return (px, py, pc), (2 * px + py if chips else 4 * px + 2 * py + pc)


ALL_PEERS = tuple(range(1, N_DEV))
SIBLING = (1,)
SAME_CORE = (2, 4, 6)
SIBLINGS_CORE = (3, 5, 7)


def _exchange_start(groups, name, rels=ALL_PEERS, chips=False):
    flat = [t for g in groups for t in g]
    sizes = [len(g) for g in groups]
    n, ng = len(flat), len(groups)
    srcs = [pltpu.with_memory_space_constraint(a, pltpu.HBM) for a, _ in flat]
    lands = [pltpu.with_memory_space_constraint(lax.empty(((N_DEV,) + a.shape) if isg else a.shape, a.dtype), pltpu.HBM)
             for a, isg in flat]

    def body(*refs):
        ins, lnd, sems, token = refs[:n], refs[n:2 * n], refs[2 * n:2 * n + 3 * ng], refs[-1]
        x, y, c, me = _mesh_place(chips)
        j0 = 0
        for gi, sz in enumerate(sizes):
            for rel in rels:
                dev, slot = _peer(x, y, c, rel, chips)
                for jj in range(sz):
                    j = j0 + jj
                    pltpu.make_async_remote_copy(
                        src_ref=ins[j] if flat[j][1] else ins[j].at[slot], dst_ref=lnd[j].at[me],
                        send_sem=sems[3 * gi].at[jj * (N_DEV - 1) + rel - 1], recv_sem=sems[3 * gi + 1].at[jj * (N_DEV - 1) + rel - 1],
                        device_id=dev, device_id_type=pl.DeviceIdType.MESH).start()
            for jj in range(sz):
                j = j0 + jj
                pltpu.make_async_copy(ins[j] if flat[j][1] else ins[j].at[me], lnd[j].at[me], sems[3 * gi + 2].at[jj]).start()
            j0 += sz
        token[...] = jnp.zeros_like(token)

    sem_shapes = [pltpu.SemaphoreType.DMA((k,)) for sz in sizes for k in (sz * (N_DEV - 1), sz * (N_DEV - 1), sz)]
    res = pl.pallas_call(
        body, name=name,
        out_shape=(*sem_shapes, *[pltpu.HBM(a.shape, a.dtype) for a in srcs], *[pltpu.HBM(a.shape, a.dtype) for a in lands],
                   jax.ShapeDtypeStruct((SUBLANE, LANE), F32)),
        in_specs=[_HBM] * (2 * n), out_specs=(*[_SEM] * (3 * ng), *[_HBM] * (2 * n), pl.BlockSpec(memory_space=pltpu.VMEM)),
        input_output_aliases={i: 3 * ng + i for i in range(2 * n)},
        compiler_params=pltpu.CompilerParams(has_side_effects=_DATAFLOW))(*srcs, *lands)
    sems, thru, token = res[:3 * ng], res[3 * ng:3 * ng + 2 * n], res[-1]
    handle, j0 = [], 0
    for gi, sz in enumerate(sizes):
        handle.append(dict(kinds=[k for _, k in groups[gi]], chips=chips, srcs=list(thru[j0:j0 + sz]), lands=list(thru[n + j0:n + j0 + sz]),
                           sems=list(sems[3 * gi:3 * gi + 3])))
        j0 += sz
    return handle, token


def _exchange_wait(group, after, name, rels=ALL_PEERS, local=True):
    kinds, sz = group["kinds"], len(group["kinds"])
    relay = group.get("relay", [])

    def body(*refs):
        ins, lnd, (ssem, rsem, lsem) = refs[:sz], refs[sz:2 * sz], refs[2 * sz:2 * sz + 3]
        x, y, c, me = _mesh_place(group["chips"])
        for rel in rels:
            dev, slot = _peer(x, y, c, rel, group["chips"])
            for jj in range(sz):
                cp = pltpu.make_async_remote_copy(
                    src_ref=ins[jj] if kinds[jj] else ins[jj].at[slot], dst_ref=lnd[jj].at[slot],
                    send_sem=ssem.at[jj * (N_DEV - 1) + rel - 1], recv_sem=rsem.at[jj * (N_DEV - 1) + rel - 1],
                    device_id=dev, device_id_type=pl.DeviceIdType.MESH)
                cp.wait_send()
                cp.wait_recv()
        if local:
            for jj in range(sz):
                pltpu.make_async_copy(ins[jj] if kinds[jj] else ins[jj].at[me], lnd[jj].at[me], lsem.at[jj]).wait()
        if relay:
            fsend, frecv = refs[2 * sz + 3:2 * sz + 5]
            dev = _peer(x, y, c, 1)[0]
            for q, (mine, theirs) in enumerate(zip(SAME_CORE, SIBLINGS_CORE)):
                for jj in range(sz):
                    cp = pltpu.make_async_remote_copy(
                        src_ref=lnd[jj].at[_peer(x, y, c, mine)[1]], dst_ref=lnd[jj].at[_peer(x, y, c, theirs)[1]],
                        send_sem=fsend.at[jj * len(SAME_CORE) + q], recv_sem=frecv.at[jj * len(SAME_CORE) + q],
                        device_id=dev, device_id_type=pl.DeviceIdType.MESH)
                    cp.wait_send()
                    cp.wait_recv()

    arrays = group["srcs"] + group["lands"]
    sems = group["sems"] + relay
    res = pl.pallas_call(
        body, name=name, out_shape=[pltpu.HBM(a.shape, a.dtype) for a in arrays],
        in_specs=[_HBM] * (2 * sz) + [_SEM] * len(sems) + [pl.BlockSpec(memory_space=pl.ANY)], out_specs=[_HBM] * (2 * sz),
        input_output_aliases={i: i for i in range(2 * sz)},
        compiler_params=pltpu.CompilerParams(has_side_effects=_DATAFLOW))(*arrays, *sems, after)
    return dict(group, srcs=list(res[:sz]), lands=list(res[sz:]), relay=[])


def _relay_start(group, name):
    sz = len(group["kinds"])
    nq = len(SAME_CORE)

    def body(*refs):
        lnd, fsend, frecv, token = refs[:sz], refs[sz], refs[sz + 1], refs[-1]
        x, y, c, _ = _mesh_place()
        dev = _peer(x, y, c, 1)[0]
        for q, rel in enumerate(SAME_CORE):
            slot = _peer(x, y, c, rel)[1]
            for jj in range(sz):
                pltpu.make_async_remote_copy(
                    src_ref=lnd[jj].at[slot], dst_ref=lnd[jj].at[slot], send_sem=fsend.at[jj * nq + q], recv_sem=frecv.at[jj * nq + q],
                    device_id=dev, device_id_type=pl.DeviceIdType.MESH).start()
        token[...] = jnp.zeros_like(token)

    lands = group["lands"]
    res = pl.pallas_call(
        body, name=name,
        out_shape=(pltpu.SemaphoreType.DMA((sz * nq,)), pltpu.SemaphoreType.DMA((sz * nq,)), *[pltpu.HBM(a.shape, a.dtype) for a in lands],
                   jax.ShapeDtypeStruct((SUBLANE, LANE), F32)),
        in_specs=[_HBM] * sz, out_specs=(_SEM, _SEM, *[_HBM] * sz, pl.BlockSpec(memory_space=pltpu.VMEM)),
        input_output_aliases={i: 2 + i for i in range(sz)},
        compiler_params=pltpu.CompilerParams(has_side_effects=_DATAFLOW))(*lands)
    return dict(group, lands=list(res[2:2 + sz]), relay=[res[0], res[1]]), res[-1]


def _sibling_swap(arrays, handle, after, name):
    start = handle is None
    n = len(arrays) if start else len(handle["srcs"])
    chips = N_DEV // 2
    if start:
        srcs = [pltpu.with_memory_space_constraint(a.reshape(chips, 2, *a.shape[1:]), pltpu.HBM) for a in arrays]
        lands = [pltpu.with_memory_space_constraint(lax.empty((chips,) + a.shape[1:], a.dtype), pltpu.HBM) for a in arrays]
    else:
        srcs, lands = handle["srcs"], handle["lands"]

    def body(*refs):
        ins, lnd, ssem, rsem = refs[:n], refs[n:2 * n], refs[2 * n], refs[2 * n + 1]
        x, y, c, _ = _mesh_place()
        dev = _peer(x, y, c, 1)[0]
        for q in range(chips):
            for j in range(n):
                cp = pltpu.make_async_remote_copy(
                    src_ref=ins[j].at[q, 1 - c], dst_ref=lnd[j].at[q], send_sem=ssem.at[j * chips + q], recv_sem=rsem.at[j * chips + q],
                    device_id=dev, device_id_type=pl.DeviceIdType.MESH)
                if start:
                    cp.start()
                else:
                    cp.wait_send()
                    cp.wait_recv()
        if start:
            refs[-1][...] = jnp.zeros_like(refs[-1])

    thru = [pltpu.HBM(a.shape, a.dtype) for a in srcs + lands]
    effect = pltpu.CompilerParams(has_side_effects=_DATAFLOW)
    if start:
        res = pl.pallas_call(
            body, name=name, out_shape=(pltpu.SemaphoreType.DMA((n * chips,)), pltpu.SemaphoreType.DMA((n * chips,)), *thru,
                                        jax.ShapeDtypeStruct((SUBLANE, LANE), F32)),
            in_specs=[_HBM] * (2 * n), out_specs=(_SEM, _SEM, *[_HBM] * (2 * n), pl.BlockSpec(memory_space=pltpu.VMEM)),
            input_output_aliases={i: 2 + i for i in range(2 * n)}, compiler_params=effect)(*srcs, *lands)
        return dict(srcs=list(res[2:2 + n]), lands=list(res[2 + n:2 + 2 * n]), sems=[res[0], res[1]]), res[-1]
    res = pl.pallas_call(
        body, name=name, out_shape=thru, in_specs=[_HBM] * (2 * n) + [_SEM, _SEM, pl.BlockSpec(memory_space=pl.ANY)],
        out_specs=[_HBM] * (2 * n), input_output_aliases={i: i for i in range(2 * n)}, compiler_params=effect)(
            *srcs, *lands, *handle["sems"], after)
    return dict(handle, srcs=list(res[:n]), lands=list(res[n:]))


def _pair_add(mine, theirs, core, name):
    chips, _, rows, w = mine.shape
    tm = _pick(rows, (256, 128, 64, 32, 16))

    def body(core_ref, a_ref, b_ref, o_ref):
        o_ref[...] = (a_ref[...].astype(F32) + b_ref[...].astype(F32)).astype(o_ref.dtype)

    return pl.pallas_call(
        body, name=name, out_shape=jax.ShapeDtypeStruct(theirs.shape, theirs.dtype),
        grid_spec=pltpu.PrefetchScalarGridSpec(
            num_scalar_prefetch=1, grid=(chips, rows // tm),
            in_specs=[pl.BlockSpec((None, None, tm, w), lambda q, i, core_ref: (q, core_ref[0], i, 0)),
                      pl.BlockSpec((None, tm, w), lambda q, i, core_ref: (q, i, 0))],
            out_specs=pl.BlockSpec((None, tm, w), lambda q, i, core_ref: (q, i, 0))),
        compiler_params=_params())(core, mine, theirs)


def _adamw(w, m, v, gparts, name, after=None):
    R, C = w.shape
    tm = _pick(R, (256, 128, 64, 32, 16, 8))
    order = [] if after is None else [after]

    def body(w_ref, m_ref, v_ref, g_ref, *rest):
        go, do, mo, vo = rest[len(order):]
        g = g_ref[0].astype(F32)
        for j in range(1, gparts.shape[0]):
            g = g + g_ref[j].astype(F32)
        mn = ADAM_B1 * m_ref[...] + (1.0 - ADAM_B1) * g
        vn = ADAM_B2 * v_ref[...] + (1.0 - ADAM_B2) * (g * g)
        m_hat = mn / (1.0 - ADAM_B1 ** ADAM_STEP)
        v_hat = vn / (1.0 - ADAM_B2 ** ADAM_STEP)
        go[...] = g
        do[...] = -ADAM_LR * (m_hat / (jnp.sqrt(v_hat) + ADAM_EPS) + ADAM_WD * w_ref[...])
        mo[...] = mn
        vo[...] = vn

    row = pl.BlockSpec((tm, C), lambda i: (i, 0))
    return pl.pallas_call(
        body, name=name, grid=(R // tm,),
        in_specs=[row, row, row, pl.BlockSpec((gparts.shape[0], tm, C), lambda i: (0, i, 0))] + [pl.BlockSpec(memory_space=pl.ANY)] * len(order),
        out_specs=[row] * 4, out_shape=[jax.ShapeDtypeStruct((R, C), F32)] * 4, compiler_params=_params())(w, m, v, gparts, *order)


def _pack(arrays):
    parts = []
    for a in arrays:
        f = a.reshape(1, -1)
        pad = _ceil_to(f.shape[1], SUBLANE * LANE) - f.shape[1]
        f = jnp.concatenate([f, jnp.zeros((1, pad), f.dtype)], axis=1) if pad else f
        parts.append(f.reshape(-1, LANE))
    rows = sum(p.shape[0] for p in parts)
    pad = _ceil_to(rows, 64) - rows
    return jnp.concatenate(parts + ([jnp.zeros((pad, LANE), parts[0].dtype)] if pad else []), axis=0)


def _unpack(buf, shapes):
    out, row = [], 0
    for s in shapes:
        size = 1
        for d in s:
            size *= d
        rows = _ceil_to(size, SUBLANE * LANE) // LANE
        out.append(buf[row:row + rows].reshape(1, -1)[:, :size].reshape(s))
        row += rows
    return out


def kernel(x, norm_mix_g, w_in, shift_mu, w0, w_lora_up, a0, a_lora_up, g_lora_up, k_k, k_a, r_k, lnx_g, lnx_b, w_proj_rwkv, sgu_ln_g, sgu_ln_b, sgu_w, sgu_b, w_proj_sgu, w_out, norm_ffn_g, w_ffn_gate, w_ffn_up, w_ffn_down, norm_final_g, loss_target, m_norm_mix_g, m_w_in, m_shift_mu, m_w0, m_w_lora_up, m_a0, m_a_lora_up, m_g_lora_up, m_k_k, m_k_a, m_r_k, m_lnx_g, m_lnx_b, m_w_proj_rwkv, m_sgu_ln_g, m_sgu_ln_b, m_sgu_w, m_sgu_b, m_w_proj_sgu, m_w_out, m_norm_ffn_g, m_w_ffn_gate, m_w_ffn_up, m_w_ffn_down, m_norm_final_g, v_norm_mix_g, v_w_in, v_shift_mu, v_w0, v_w_lora_up, v_a0, v_a_lora_up, v_g_lora_up, v_k_k, v_k_a, v_r_k, v_lnx_g, v_lnx_b, v_w_proj_rwkv, v_sgu_ln_g, v_sgu_ln_b, v_sgu_w, v_sgu_b, v_w_proj_sgu, v_w_out, v_norm_ffn_g, v_w_ffn_gate, v_w_ffn_up, v_w_ffn_down, v_norm_final_g):
    weights = dict(norm_mix_g=norm_mix_g, w_in=w_in, shift_mu=shift_mu, w0=w0, w_lora_up=w_lora_up, a0=a0, a_lora_up=a_lora_up,
                   g_lora_up=g_lora_up, k_k=k_k, k_a=k_a, r_k=r_k, lnx_g=lnx_g, lnx_b=lnx_b, w_proj_rwkv=w_proj_rwkv,
                   sgu_ln_g=sgu_ln_g, sgu_ln_b=sgu_ln_b, sgu_w=sgu_w, sgu_b=sgu_b, w_proj_sgu=w_proj_sgu, w_out=w_out,
                   norm_ffn_g=norm_ffn_g, w_ffn_gate=w_ffn_gate, w_ffn_up=w_ffn_up, w_ffn_down=w_ffn_down, norm_final_g=norm_final_g)
    m_in = dict(norm_mix_g=m_norm_mix_g, w_in=m_w_in, shift_mu=m_shift_mu, w0=m_w0, w_lora_up=m_w_lora_up, a0=m_a0,
                a_lora_up=m_a_lora_up, g_lora_up=m_g_lora_up, k_k=m_k_k, k_a=m_k_a, r_k=m_r_k, lnx_g=m_lnx_g, lnx_b=m_lnx_b,
                w_proj_rwkv=m_w_proj_rwkv, sgu_ln_g=m_sgu_ln_g, sgu_ln_b=m_sgu_ln_b, sgu_w=m_sgu_w, sgu_b=m_sgu_b,
                w_proj_sgu=m_w_proj_sgu, w_out=m_w_out, norm_ffn_g=m_norm_ffn_g, w_ffn_gate=m_w_ffn_gate, w_ffn_up=m_w_ffn_up,
                w_ffn_down=m_w_ffn_down, norm_final_g=m_norm_final_g)
    v_in = dict(norm_mix_g=v_norm_mix_g, w_in=v_w_in, shift_mu=v_shift_mu, w0=v_w0, w_lora_up=v_w_lora_up, a0=v_a0,
                a_lora_up=v_a_lora_up, g_lora_up=v_g_lora_up, k_k=v_k_k, k_a=v_k_a, r_k=v_r_k, lnx_g=v_lnx_g, lnx_b=v_lnx_b,
                w_proj_rwkv=v_w_proj_rwkv, sgu_ln_g=v_sgu_ln_g, sgu_ln_b=v_sgu_ln_b, sgu_w=v_sgu_w, sgu_b=v_sgu_b,
                w_proj_sgu=v_w_proj_sgu, w_out=v_w_out, norm_ffn_g=v_norm_ffn_g, w_ffn_gate=v_w_ffn_gate, w_ffn_up=v_w_ffn_up,
                w_ffn_down=v_w_ffn_down, norm_final_g=v_norm_final_g)
    names = list(weights)
    col_sharded = ("w_in", "w_lora_up", "a_lora_up", "g_lora_up", "w_proj_rwkv", "w_proj_sgu", "w_ffn_gate", "w_ffn_up")
    row_sharded = ("w_out", "w_ffn_down")
    sharded = [n for n in names if n in col_sharded or n in row_sharded]
    small = [n for n in names if n not in sharded]

    xs, tgt = x[0], loss_target[0]
    T, D = xs.shape
    RW = w0.shape[1]
    H = RW // HEAD
    SW = sgu_ln_g.shape[1]
    G = sgu_w.shape[1]
    assert 2 * SW == D, "the projection layout takes the SGU part to be as wide as a gate"
    lay = _rwkv_layout(RW, w_lora_up.shape[1], a_lora_up.shape[1], g_lora_up.shape[1], D)
    _, pw, _, rcp = lay
    icp = rcp + 3 * D
    b_ga, b_gb, b_z = rcp // D, rcp // D + 1, rcp // D + 2

    gather_groups = dict(win=["w_in", "w_lora_up", "a_lora_up", "g_lora_up"], proj=["w_proj_rwkv", "w_proj_sgu", "w_out"],
                         ffn_gate_up=["w_ffn_gate", "w_ffn_up"], ffn_down=["w_ffn_down"])
    handles, gather_token = _exchange_start([[(weights[n][0].astype(BF16), True) for n in grp] for grp in gather_groups.values()],
                                            "gather_start", rels=SIBLING + SAME_CORE)
    gather = dict(zip(gather_groups, handles))
    full = {}
    relay_tokens = {}
    joined = lambda g: g.transpose(1, 0, 2).reshape(g.shape[1], -1)

    def relay_weights(key, after):
        arrived = _exchange_wait(gather[key], after, "gather_wait_ici_" + key, rels=SAME_CORE, local=False)
        gather[key], relay_tokens[key] = _relay_start(arrived, "gather_relay_" + key)

    def take_weights(key, after):
        done = _exchange_wait(gather[key], after, "gather_wait_d2d_" + key, rels=SIBLING)
        for n, g in zip(gather_groups[key], done["lands"]):
            full[n] = g.reshape(-1, g.shape[2]) if n in row_sharded else g

    packed = [_pack([d[n] for n in small] + [gather_token]) for d in (weights, m_in, v_in)]
    n1 = _rms_fwd(xs, norm_mix_g, "rms_mix", deps=[gather_token, *packed])
    relay_weights("win", n1)
    take_weights("win", relay_tokens["win"])
    W_in = _w_in_to_proj(full["w_in"], lay, D, "w_in_layout")
    lora = [_pad_rows(joined(full[n]), rows) for n, rows in zip(("w_lora_up", "a_lora_up", "g_lora_up"), pw[3:])]
    mu_p = _pad_rwkv_cols(shift_mu, lay)
    rsmall = [w0, a0, k_k, k_a]
    hp = [lnx_g.reshape(H, 1, HEAD), lnx_b.reshape(H, 1, HEAD), r_k.reshape(H, 1, HEAD)]
    ws = sgu_w[0]
    bexp = jnp.repeat(sgu_b[0].T, SGU_GROUP, axis=1)
    gf = norm_final_g.reshape(1, D)

    proj = _matmul(n1, W_in, mode="nn", out_dtype=F32, name="proj_in")
    ga, gb = (proj, D, b_ga), (proj, D, b_gb)
    r_h, lw_h, k2_h, v_h, aa_h, bb_h, g_h = _rwkv_pre(proj, mu_p, rsmall, lora, lay, "rwkv_pre")
    wkv_in = [r_h, lw_h, k2_h, v_h, aa_h, bb_h]
    y_h, *wkv_saved = _wkv_fwd(*wkv_in, "wkv_fwd")
    relay_weights("proj", y_h)
    ya = _head_post(y_h, r_h, k2_h, v_h, g_h, hp, "head_post", deps=[relay_tokens["proj"]])
    relay_weights("ffn_gate_up", ya)
    yb = _sgu_fwd(proj, b_z, sgu_ln_g, sgu_ln_b, ws, bexp, "sgu_fwd")
    take_weights("proj", ya)
    pa = _matmul(ya, full["w_proj_rwkv"], mode="nn", out_dtype=F32, name="proj_a", deps=[relay_tokens["ffn_gate_up"]])

    def merge_fn(pb_v, pa_v, ga_v, gb_v):
        return pb_v, _sigmoid(ga_v) * pa_v + _sigmoid(gb_v) * pb_v
    pb, merged = _matmul(yb, full["w_proj_sgu"], mode="nn", name="proj_b_merge",
                         epi=(merge_fn, [pa, (proj, b_ga * D), (proj, b_gb * D)], [F32, BF16]))
    h1 = _matmul(merged, full["w_out"], mode="nn", out_dtype=F32, name="out_proj", add=xs)
    n2 = _rms_fwd(h1, norm_ffn_g, "rms_ffn")
    relay_weights("ffn_down", n2)
    take_weights("ffn_gate_up", n2)

    def act_fn(gt_v, up_v):
        return gt_v, up_v, gt_v * _sigmoid(gt_v) * up_v
    gt, up, act = _matmul(n2, full["w_ffn_gate"], b2=full["w_ffn_up"], mode="nn", name="ffn_gate_up_act", out_blocks=N_DEV,
                          epi=(act_fn, [], [BF16, BF16, BF16]), deps=[relay_tokens["ffn_down"]])
    take_weights("ffn_down", act)
    h2 = _matmul(act, full["w_ffn_down"], mode="nn", out_dtype=F32, name="ffn_down", add=h1)

    def final_fn(rv, pv):
        (h_v, t_v), (g_v,) = rv, pv
        r = lax.rsqrt(_mean(h_v * h_v) + RMS_EPS)
        yn = h_v * r
        e = yn * g_v - t_v
        loss = 0.5 * jnp.sum(_mean(e * e))
        dout = e * (1.0 / D)
        dyg = dout * g_v
        dh = r * (dyg - yn * _mean(dyg * yn))
        return [dh, dh], [jnp.full((1, LANE), loss, F32), _colsum(dout * yn)]
    dh2, dh2_bf, loss_part, d_gf = _rowwise(final_fn, [h2, tgt], [gf], [(D, F32), (D, BF16)], [(1, LANE), (1, D)], name="final_loss")

    grads = {}

    def start_scatter(group, name, extra=()):
        blocks = [(grads[n].reshape(N_DEV, -1, grads[n].shape[1]) if n in row_sharded else grads[n], False) for n in group]
        (handle,), token = _exchange_start([blocks + list(extra)], name)
        return handle, token

    def dact_fn(d_v, gt_v, up_v):
        gt_v, up_v = gt_v.astype(F32), up_v.astype(F32)
        s = _sigmoid(gt_v)
        return d_v * up_v * (s * (1.0 + gt_v * (1.0 - s))), d_v * gt_v * s
    dgt, dup = _matmul(dh2_bf, full["w_ffn_down"], mode="nt", name="d_ffn_act", out_blocks=N_DEV,
                       epi=(dact_fn, [gt, up], [BF16, BF16]))
    scatter_groups = dict(ffn_down=["w_ffn_down"], ffn_gate=["w_ffn_gate"], ffn_up=["w_ffn_up"],
                          mid=["w_out", "w_proj_rwkv", "w_proj_sgu"], last=["w_in", "w_lora_up", "a_lora_up", "g_lora_up"])
    scatters = {}
    grads["w_ffn_down"] = _matmul(act, dh2_bf, mode="tn", out_dtype=BF16, name="dw_ffn_down")
    scatters["ffn_down"], token = start_scatter(scatter_groups["ffn_down"], "scatter_start_ffn_down")
    dn2 = _matmul(dgt, full["w_ffn_gate"], mode="nt", out_dtype=F32, name="dn2_gate", deps=[token])
    grads["w_ffn_gate"] = _matmul(n2, dgt, mode="tn", out_dtype=BF16, name="dw_ffn_gate", out_blocks=N_DEV)
    scatters["ffn_gate"], token = start_scatter(scatter_groups["ffn_gate"], "scatter_start_ffn_gate")
    grads["w_ffn_up"] = _matmul(n2, dup, mode="tn", out_dtype=BF16, name="dw_ffn_up", out_blocks=N_DEV, deps=[token])
    scatters["ffn_up"], token = start_scatter(scatter_groups["ffn_up"], "scatter_start_ffn_up")
    dn2 = _matmul(dup, full["w_ffn_up"], mode="nt", out_dtype=F32, name="dn2_up", add=dn2, deps=[token])
    dh1, dh1_bf, d_g2 = _rms_bwd(dn2, h1, dh2, norm_ffn_g, "rms_ffn_bwd")
    dmerged = _matmul(dh1_bf, full["w_out"], mode="nt", out_dtype=F32, name="d_merged")
    grads["w_out"] = _matmul(merged, dh1_bf, mode="tn", out_dtype=BF16, name="dw_out")

    def dmerge_fn(rv, pv):
        d_v, ga_v, gb_v, pa_v, pb_v = rv
        sa, sb = _sigmoid(ga_v), _sigmoid(gb_v)
        dgates = jnp.concatenate([d_v * pa_v * sa * (1.0 - sa), d_v * pb_v * sb * (1.0 - sb)], axis=1)
        return [dgates, d_v * sa, d_v * sb], []
    dproj, dpa, dpb = _rowwise(dmerge_fn, [dmerged, ga, gb, pa, pb], [],
                               [(2 * D, BF16, icp, b_ga // 2, None), (D, BF16), (D, BF16)], [], name="d_merge")
    dya = _matmul(dpa, full["w_proj_rwkv"], mode="nt", out_dtype=F32, name="d_ya")
    dyb = _matmul(dpb, full["w_proj_sgu"], mode="nt", out_dtype=F32, name="d_yb")
    grads["w_proj_rwkv"] = _matmul(ya, dpa, mode="tn", out_dtype=BF16, name="dw_proj_a", out_blocks=N_DEV)
    grads["w_proj_sgu"] = _matmul(yb, dpb, mode="tn", out_dtype=BF16, name="dw_proj_b", out_blocks=N_DEV)
    scatters["mid"], token_mid = start_scatter(scatter_groups["mid"], "scatter_start_mid")
    dproj, d_lng, d_lnb, d_ws, d_bs = _sgu_bwd(proj, b_z, dyb, sgu_ln_g, sgu_ln_b, ws, bexp, dproj, "sgu_bwd")

    dr_h, dlw_h, dk2_h, dv_h, daa, dbb, dg_h, d_lnxg, d_lnxb, d_rk = _wkv_bwd(
        *wkv_in, *wkv_saved, y_h, g_h, hp, dya, "wkv_bwd", deps=[token_mid])
    dps, d_mu, d_w0, d_a0, d_kk, d_ka, d_wlw, d_wla, d_wlg = _rwkv_pre_bwd(
        proj, mu_p, rsmall, lora, [dr_h, dk2_h, dv_h, dlw_h, daa, dbb, dg_h], lay, "rwkv_pre_bwd")
    dproj = _shift_bwd(dps, mu_p, dproj, "shift_bwd")
    split = lambda g: g.reshape(g.shape[0], N_DEV, -1).transpose(1, 0, 2)
    grads["w_in"] = _dw_in_from_proj(_matmul(n1, dproj, mode="tn", out_dtype=BF16, name="dw_in"), lay, D, w_in.shape[2], "dw_in_layout")
    grads["w_lora_up"] = split(d_wlw[:w_lora_up.shape[1]].astype(BF16))
    grads["a_lora_up"] = split(d_wla[:a_lora_up.shape[1]].astype(BF16))
    grads["g_lora_up"] = split(d_wlg[:g_lora_up.shape[1]].astype(BF16))
    out = {}

    def update_group(key, after):
        handle = scatters[key]
        parts = _exchange_wait(handle, after, "scatter_wait_" + key, rels=SAME_CORE if handle["chips"] else ALL_PEERS)["lands"]
        for n, part in zip(scatter_groups[key], parts):
            res = _adamw(weights[n][0], m_in[n][0], v_in[n][0], part, "adamw_" + n, after=after)
            out[n] = [t.reshape(weights[n].shape) for t in res]
            after = res[0]
        return after

    swap, token_swap = _sibling_swap([grads[n] for n in scatter_groups["last"]], None, None, "scatter_last_swap_start")
    after = update_group("ffn_gate", update_group("ffn_down", token_swap))
    swap = _sibling_swap(None, swap, after, "scatter_last_swap_wait")
    core = lax.axis_index("c").astype(jnp.int32).reshape(1)
    chip_sums = [_pair_add(mine, theirs, core, "scatter_last_add_" + n)
                 for n, mine, theirs in zip(scatter_groups["last"], swap["srcs"], swap["lands"])]
    (scatters["last"],), token_in = _exchange_start([[(s, False) for s in chip_sums]], "scatter_start_last", rels=SAME_CORE, chips=True)
    dn1 = _matmul(dproj, W_in, mode="nt", out_dtype=F32, name="dn1", deps=[token_in])
    dx, _, d_g1 = _rms_bwd(dn1, xs, dh1, norm_mix_g, "rms_mix_bwd")
    small_grads = dict(norm_mix_g=d_g1, shift_mu=_unpad_rwkv_cols(d_mu, lay), w0=d_w0, a0=d_a0, k_k=d_kk, k_a=d_ka, r_k=d_rk,
                       lnx_g=d_lnxg, lnx_b=d_lnxb, sgu_ln_g=d_lng, sgu_ln_b=d_lnb, sgu_w=d_ws, sgu_b=d_bs[:, :G].T,
                       norm_ffn_g=d_g2, norm_final_g=d_gf)
    (gather_small,), after = _exchange_start([[(_pack([small_grads[n] for n in small] + [jnp.zeros_like(gather_token)]), True)]],
                                             "gather_small_start")
    for key in ("ffn_up", "mid", "last"):
        after = update_group(key, after)
    small_parts = _exchange_wait(gather_small, after, "gather_small_wait")["lands"][0]
    res = _adamw(*packed, small_parts, "adamw_small")
    unpacked = [_unpack(t, [weights[n].shape for n in small]) for t in res]
    for i, n in enumerate(small):
        out[n] = [u[i] for u in unpacked]

    loss = lax.psum(loss_part[0, 0], ("x", "y", "c"))
    return (loss, dx[None], *[out[n][0] for n in names], *[out[n][1] for n in names],
            *[out[n][2] for n in names], *[out[n][3] for n in names])
```

```python
import jax
import jax.numpy as jnp
from jax import lax
from jax.experimental import pallas as pl
from jax.experimental.pallas import tpu as pltpu

F32 = jnp.float32
BF16 = jnp.bfloat16

N_DEV = 8
LANE = 128
SUBLANE = 8
HEAD = 64
SGU_CHUNK = 128
SGU_GROUP = 128
WKV_CHUNK = 64
RMS_EPS = 1e-6
LN_EPS = 1e-5
LNX_EPS = 64e-5
ADAM_LR, ADAM_B1, ADAM_B2, ADAM_EPS, ADAM_WD, ADAM_STEP = 0.001, 0.9, 0.999, 1e-08, 0.01, 10
VMEM_LIMIT_BYTES = 48 * 1024 * 1024
_SQRT_HALF = 0.7071067811865476
_INV_SQRT_2PI = 0.3989422804014327


def _pick(n, cands):
    for c in cands:
        if n % c == 0:
            return c
    return n


def _ceil_to(n, m):
    return -(-n // m) * m


def _params():
    return pltpu.CompilerParams(vmem_limit_bytes=VMEM_LIMIT_BYTES)


def _tile(n, cap):
    best = 0
    for d in range(LANE, min(n, cap) + 1, LANE):
        if n % d == 0:
            best = d
    return best or n


def _matmul_tiles(M, N, K, a_bytes, b_bytes, o_bytes, has_add, forced):
    tm = forced.get("m") or _tile(M, 1024)
    tn = forced.get("n") or _tile(N, 1024)
    tk = forced.get("k") or _tile(K, 2048)

    def vmem(tm, tn, tk):
        acc = tm * tn * 4 if tk < K else 0
        return 2 * (tm * tk * a_bytes + tk * tn * b_bytes + tm * tn * (o_bytes + (4 if has_add else 0))) + acc

    while vmem(tm, tn, tk) > (VMEM_LIMIT_BYTES * 3) // 4:
        if "k" not in forced and tk > 512 and _tile(K, tk // 2) < tk:
            tk = _tile(K, tk // 2)
        elif "m" not in forced and _tile(M, tm // 2) < tm:
            tm = _tile(M, tm // 2)
        else:
            break
    return tm, tn, tk


def _matmul(a, b, *, mode, out_dtype=F32, name, add=None, deps=(), out_blocks=0, epi=None, b2=None):
    def view(x):
        return (x.shape[1], x.shape[0] * x.shape[2], x.shape[2]) if x.ndim == 3 else (x.shape[0], x.shape[1], 0)

    (ar, ac, aw), (br, bc, bw) = view(a), view(b)
    a_col, b_col = {"nn": ("k", "n"), "nt": ("k", "k"), "tn": ("m", "n")}[mode]
    if mode == "nn":
        M, K, K2, N = ar, ac, br, bc
    elif mode == "nt":
        M, K, N, K2 = ar, ac, br, bc
    else:
        K, M, K2, N = ar, ac, br, bc
    assert K == K2, (a.shape, b.shape, mode)
    forced = {}
    for dim, w in ((a_col, aw), (b_col, bw), ("n", N // out_blocks if out_blocks else 0)):
        if w:
            assert forced.get(dim, w) == w
            forced[dim] = w
    has_add = add is not None
    tile_bytes = (sum(jnp.dtype(d).itemsize for d in epi[2]) + sum((e[0] if isinstance(e, tuple) else e).dtype.itemsize for e in epi[1])
                  if epi is not None else jnp.dtype(out_dtype).itemsize)
    tm, tn, tk = _matmul_tiles(M, N, K, a.dtype.itemsize, b.dtype.itemsize, tile_bytes, has_add, forced)
    kb = 1
    if "k" in forced and mode != "tn":
        lanes_ok = all(w or tk % LANE == 0 for w in (aw, bw if mode == "nt" else 1))
        kb = next(c for c in (4, 2, 1) if (K // tk) % c == 0 and (c == 1 or (lanes_ok and c * tk <= 1536)))
    nk = K // (tk * kb)
    dn = {"nn": (((1,), (0,)), ((), ())), "nt": (((1,), (1,)), ((), ())), "tn": (((0,), (0,)), ((), ()))}[mode]
    pick = {"m": lambda i, j, k: i, "n": lambda i, j, k: j, "k": lambda i, j, k: k}
    size = {"m": tm, "n": tn, "k": tk}

    def spec(blocked, row_dim, col_dim):
        rf, cf = pick[row_dim], pick[col_dim]
        reps = {d: (kb if d == "k" else 1) for d in (row_dim, col_dim)}
        if blocked:
            lead = kb if col_dim == "k" and kb > 1 else None
            return pl.BlockSpec((lead, size[row_dim], size[col_dim]), lambda i, j, k: (cf(i, j, k), rf(i, j, k), 0))
        return pl.BlockSpec((size[row_dim] * reps[row_dim], size[col_dim] * reps[col_dim]), lambda i, j, k: (rf(i, j, k), cf(i, j, k)))

    def k_part(ref, blocked, k_on_rows, j):
        if kb == 1:
            return ref[...]
        if blocked:
            return ref[j]
        return ref[j * tk:(j + 1) * tk, :] if k_on_rows else ref[:, j * tk:(j + 1) * tk]

    a_spec = spec(aw, "k" if mode == "tn" else "m", a_col)
    b_spec = spec(bw, "n" if mode == "nt" else "k", b_col)
    o_spec = spec(out_blocks, "m", "n")
    epi_fn, epi_ins, epi_dtypes = epi if epi is not None else (None, [], [out_dtype])
    epi_ins = [e if isinstance(e, tuple) else (e, None) for e in epi_ins]
    n_epi = len(epi_ins)
    twin = b2 is not None
    assert not twin or (nk == 1 and kb == 1 and epi is not None and b2.shape == b.shape)
    n_in = 2 + twin + has_add + n_epi + len(deps)
    n_out = len(epi_dtypes)

    def body(*refs):
        a_ref, b_ref = refs[0], refs[1]
        add_ref = refs[2 + twin] if has_add else None
        epi_refs = refs[2 + twin + has_add:2 + twin + has_add + n_epi]
        o_refs = refs[n_in:n_in + n_out]
        part = None
        for q in range(kb):
            a_q = k_part(a_ref, aw and a_col == "k", False, q)
            b_q = k_part(b_ref, bw and b_col == "k", mode == "nn", q)
            prod = lax.dot_general(a_q.astype(BF16), b_q.astype(BF16), dn, preferred_element_type=F32)
            part = prod if part is None else part + prod
        second = [lax.dot_general(a_ref[...].astype(BF16), refs[2][...].astype(BF16), dn, preferred_element_type=F32)] if twin else []

        def finish(res):
            outs = epi_fn(res, *second, *[e[...] for e in epi_refs]) if epi_fn is not None else (res,)
            for o_ref, val in zip(o_refs, outs):
                o_ref[...] = val.astype(o_ref.dtype)

        if nk == 1:
            finish(part + add_ref[...] if has_add else part)
            return
        acc_ref = refs[-1]
        kk = pl.program_id(2)

        @pl.when(kk == 0)
        def _():
            acc_ref[...] = part + add_ref[...] if has_add else part

        @pl.when(kk > 0)
        def _():
            acc_ref[...] += part

        @pl.when(kk == nk - 1)
        def _():
            finish(acc_ref[...])

    def epi_spec(arr, off):
        if off is None:
            return o_spec
        assert off % tn == 0
        return pl.BlockSpec((tm, tn), lambda i, j, k: (i, j + off // tn))

    ins = [a, b] + ([b2] if twin else []) + ([add] if has_add else []) + [arr for arr, _ in epi_ins] + list(deps)
    in_specs = ([a_spec, b_spec] + ([b_spec] if twin else []) + ([o_spec] if has_add else []) + [epi_spec(arr, off) for arr, off in epi_ins]
                + [pl.BlockSpec(d.shape, lambda i, j, k, nd=d.ndim: (0,) * nd) for d in deps])
    o_shape = (out_blocks, M, tn) if out_blocks else (M, N)
    res = pl.pallas_call(
        body, name=name, grid=(M // tm, N // tn, nk), in_specs=in_specs, out_specs=[o_spec] * n_out,
        out_shape=[jax.ShapeDtypeStruct(o_shape, dt) for dt in epi_dtypes],
        scratch_shapes=[pltpu.VMEM((tm, tn), F32)] if nk > 1 else [],
        compiler_params=_params())(*ins)
    return res[0] if epi is None else list(res)


def _rowwise(fn, rows, pars, row_outs, acc_outs, *, name, tm=256, deps=()):
    rows = [r if isinstance(r, tuple) else (r, r.shape[1], 0) for r in rows]
    row_outs = [o if len(o) == 5 else (o[0], o[1], o[0], 0, None) for o in row_outs]
    aliased = [(k, o[4]) for k, o in enumerate(row_outs) if o[4] is not None]
    R = rows[0][0].shape[0]
    if max(w for _, w, _ in rows) > 4096:
        tm = tm // 2
    tm = min(tm, R)
    assert R % tm == 0
    nr, npar = len(rows), len(pars)
    nro = len(row_outs)
    n_in = nr + npar + len(deps) + len(aliased)

    def body(*refs):
        rv = [r[...] for r in refs[:nr]]
        pv = [p[...] for p in refs[nr:nr + npar]]
        outs = refs[n_in:]
        ro, ao = fn(rv, pv)
        first = pl.program_id(0) == 0
        for o_ref, val in zip(outs[:nro], ro):
            o_ref[...] = val.astype(o_ref.dtype)

        @pl.when(first)
        def _():
            for o_ref, val in zip(outs[nro:], ao):
                o_ref[...] = val

        @pl.when(jnp.logical_not(first))
        def _():
            for o_ref, val in zip(outs[nro:], ao):
                o_ref[...] += val

    in_specs = ([pl.BlockSpec((tm, w), lambda i, cb=cb: (i, cb)) for _, w, cb in rows]
                + [pl.BlockSpec(p.shape, lambda i, nd=p.ndim: (0,) * nd) for p in list(pars) + list(deps)]
                + [pl.BlockSpec(memory_space=pl.ANY)] * len(aliased))
    out_shape = ([jax.ShapeDtypeStruct((R, full), dt) for _, dt, full, _, _ in row_outs]
                 + [jax.ShapeDtypeStruct(s, F32) for s in acc_outs])
    out_specs = ([pl.BlockSpec((tm, f), lambda i, cb=cb: (i, cb)) for f, _, _, cb, _ in row_outs]
                 + [pl.BlockSpec(s, lambda i, nd=len(s): (0,) * nd) for s in acc_outs])
    res = pl.pallas_call(body, name=name, grid=(R // tm,), in_specs=in_specs, out_specs=out_specs, out_shape=out_shape,
                         input_output_aliases={n_in - len(aliased) + q: k for q, (k, _) in enumerate(aliased)},
                         compiler_params=_params())(*[r for r, _, _ in rows], *pars, *deps, *[buf for _, buf in aliased])
    return list(res)


def _bdot(a, b, mode="nn"):
    dn = {"nn": (((1,), (0,)), ((), ())), "nt": (((1,), (1,)), ((), ())), "tn": (((0,), (0,)), ((), ()))}[mode]
    return lax.dot_general(a.astype(BF16), b.astype(BF16), dn, preferred_element_type=F32)


def _sigmoid(x):
    return jax.nn.sigmoid(x)


def _softplus(x):
    return jnp.maximum(x, 0.0) + jnp.log1p(jnp.exp(-jnp.abs(x)))


def _gelu(z):
    return 0.5 * z * (1.0 + lax.erf(z * _SQRT_HALF))


def _gelu_grad(z):
    return 0.5 * (1.0 + lax.erf(z * _SQRT_HALF)) + z * jnp.exp(-0.5 * z * z) * _INV_SQRT_2PI


def _mean(x):
    return jnp.mean(x, axis=-1, keepdims=True)


def _colsum(x):
    return jnp.sum(x, axis=0, keepdims=True)


def _rms_fwd(x, g, name, deps=()):
    def fn(rv, pv):
        (xv,), (gv,) = rv, pv
        r = lax.rsqrt(_mean(xv * xv) + RMS_EPS)
        return [xv * r * gv], []
    return _rowwise(fn, [x], [g], [(x.shape[1], BF16)], [], name=name, deps=deps)[0]


def _rms_bwd(dn, x, dres, g, name, deps=()):
    def fn(rv, pv):
        (dnv, xv, drv), (gv,) = rv, pv
        r = lax.rsqrt(_mean(xv * xv) + RMS_EPS)
        yn = xv * r
        dyg = dnv * gv
        dx = drv + r * (dyg - yn * _mean(dyg * yn))
        return [dx, dx], [_colsum(dnv * yn)]
    D = x.shape[1]
    return _rowwise(fn, [dn, x, dres], [g], [(D, F32), (D, BF16)], [(1, D)], name=name, deps=deps)


def _rwkv_layout(RW, Lw, La, Lg, D):
    widths = [RW, RW, RW, Lw, La, Lg]
    pw = [_ceil_to(w, LANE) for w in widths]
    pw[5] += _ceil_to(sum(pw), 2 * D) - sum(pw)
    offs = [sum(pw[:i]) for i in range(6)]
    return widths, pw, offs, sum(pw)


def _pad_rwkv_cols(a, lay):
    widths, pw, _, _ = lay
    pieces, src = [], 0
    for w, p in zip(widths, pw):
        pieces.append(a[:, src:src + w])
        if p > w:
            pieces.append(jnp.zeros((a.shape[0], p - w), a.dtype))
        src += w
    return jnp.concatenate(pieces, axis=1)


def _unpad_rwkv_cols(a, lay):
    widths, _, offs, _ = lay
    return jnp.concatenate([a[:, o:o + w] for o, w in zip(offs, widths)], axis=1)


def _proj_pieces(lay, D, cs):
    widths, _, offs, rcp = lay
    rc = sum(widths)
    segs = [(sum(widths[:j]), widths[j], offs[j]) for j in range(6)] + [(rc, D, rcp + 2 * D), (rc + D, D, rcp), (rc + 2 * D, D, rcp + D)]
    pieces = []
    for start, width, dst in segs:
        n = start
        while n < start + width:
            d, off = divmod(n, cs)
            take = min(cs - off, start + width - n)
            pieces.append((d, off, dst + n - start, take))
            n += take
    return pieces


def _w_in_to_proj(g, lay, D, name):
    nb, rows, cs = g.shape
    icp = lay[3] + 3 * D
    pieces = _proj_pieces(lay, D, cs)
    tm = _pick(rows, (256, 128, 64, 32, 16))

    def body(i_ref, o_ref):
        o_ref[...] = jnp.zeros_like(o_ref)
        for d, src, dst, w in pieces:
            o_ref[:, dst:dst + w] = i_ref[d, :, src:src + w]

    return pl.pallas_call(
        body, name=name, grid=(rows // tm,), in_specs=[pl.BlockSpec((nb, tm, cs), lambda i: (0, i, 0))],
        out_specs=pl.BlockSpec((tm, icp), lambda i: (i, 0)), out_shape=jax.ShapeDtypeStruct((rows, icp), g.dtype),
        compiler_params=_params())(g)


def _dw_in_from_proj(a, lay, D, cs, name):
    rows, icp = a.shape
    pieces = _proj_pieces(lay, D, cs)
    tm = _pick(rows, (256, 128, 64, 32, 16))

    def body(i_ref, o_ref):
        for d, src, dst, w in pieces:
            o_ref[d, :, src:src + w] = i_ref[:, dst:dst + w]

    return pl.pallas_call(
        body, name=name, grid=(rows // tm,), in_specs=[pl.BlockSpec((tm, icp), lambda i: (i, 0))],
        out_specs=pl.BlockSpec((N_DEV, tm, cs), lambda i: (0, i, 0)), out_shape=jax.ShapeDtypeStruct((N_DEV, rows, cs), a.dtype),
        compiler_params=_params())(a)


def _pad_rows(a, rows):
    return a if a.shape[0] == rows else jnp.concatenate([a, jnp.zeros((rows - a.shape[0], a.shape[1]), a.dtype)], axis=0)


def _token_shift(p, halo, mu, i):
    tm = p.shape[0]
    hid = lax.broadcasted_iota(jnp.int32, (SUBLANE, 1), 0)
    before = jnp.sum(jnp.where(hid == SUBLANE - 1, halo, 0.0), axis=0, keepdims=True)
    before = jnp.where(i == 0, 0.0, before)
    rid = lax.broadcasted_iota(jnp.int32, (tm, 1), 0)
    prev = jnp.where(rid == 0, before, pltpu.roll(p, 1, 0))
    d = prev - p
    return p + d * mu, d


def _rwkv_math(ps, w0, a0, k_k, k_a, wlw, wla, wlg, lay):
    _, pw, offs, _ = lay
    r, k, v, xw, xa, xg = (ps[:, offs[j]:offs[j] + pw[j]] for j in range(6))
    tw = jnp.tanh(xw)
    ww = w0 + _bdot(tw, wlw)
    lw = -jnp.exp(-_softplus(-ww) - 0.5)
    a = _sigmoid(a0 + _bdot(xa, wla))
    sg = _sigmoid(xg)
    g = _bdot(sg, wlg)
    return dict(r=r, k=k, v=v, xa=xa, tw=tw, ww=ww, lw=lw, a=a, sg=sg, g=g, kkp=k * k_k, k2=k * (1.0 + (a - 1.0) * k_a))


def _halo_spec(tm, width):
    hb = tm // SUBLANE
    return pl.BlockSpec((SUBLANE, width), lambda i: (jnp.maximum(i * hb - 1, 0), 0))


def _rowsum(x):
    return jnp.sum(x, axis=-1, keepdims=True)


def _kk_math(kkp):
    nrm = jnp.sqrt(_rowsum(kkp * kkp))
    inv = 1.0 / jnp.maximum(nrm, 1e-12)
    return nrm, inv, kkp * inv


def _rwkv_pre(p, mu, small, lora, lay, name):
    T, rcp = p.shape[0], lay[3]
    H = lay[0][0] // HEAD
    tm = min(128, T)

    def body(p_ref, ph_ref, mu_ref, w0_ref, a0_ref, kk_ref, ka_ref, wlw_ref, wla_ref, wlg_ref, r_o, lw_o, k2_o, v_o, aa_o, bb_o, g_o):
        ps, _ = _token_shift(p_ref[...], ph_ref[...], mu_ref[...], pl.program_id(0))
        q = _rwkv_math(ps, w0_ref[...], a0_ref[...], kk_ref[...], ka_ref[...], wlw_ref[...], wla_ref[...], wlg_ref[...], lay)
        for h in range(H):
            sl = slice(h * HEAD, (h + 1) * HEAD)
            for o_ref, key in ((r_o, "r"), (lw_o, "lw"), (k2_o, "k2"), (v_o, "v"), (g_o, "g")):
                o_ref[h] = q[key][:, sl]
            _, _, kk = _kk_math(q["kkp"][:, sl])
            aa_o[h] = -kk
            bb_o[h] = kk * q["a"][:, sl]

    whole = lambda arr: pl.BlockSpec(arr.shape, lambda i: (0, 0))
    return pl.pallas_call(
        body, name=name, grid=(T // tm,),
        in_specs=([pl.BlockSpec((tm, rcp), lambda i: (i, 0)), _halo_spec(tm, rcp), whole(mu)]
                  + [whole(s) for s in small] + [whole(w) for w in lora]),
        out_specs=[pl.BlockSpec((H, tm, HEAD), lambda i: (0, i, 0))] * 7, out_shape=[jax.ShapeDtypeStruct((H, T, HEAD), F32)] * 7,
        compiler_params=_params())(p, p, mu, *small, *lora)


def _rwkv_pre_bwd(p, mu, small, lora, hgrads, dproj, lay, name):
    T, rcp = p.shape[0], lay[3]
    widths, pw, offs, _ = lay
    RW = widths[0]
    H = RW // HEAD
    tm = min(128, T)
    nt = T // tm
    hb = tm // SUBLANE

    def body(p_ref, ph_ref, mu_ref, w0_ref, a0_ref, kk_ref, ka_ref, wlw_ref, wla_ref, wlg_ref,
             dr_h, dk2_h, dv_h, dlw_h, daa, dbb, dg_h, buf_ref,
             dp_ref, dmu_ref, dw0_ref, da0_ref, dkk_ref, dka_ref, dwlw_ref, dwla_ref, dwlg_ref,
             s_dr, s_dk2, s_dv, s_dlw, s_dkkp, s_da, s_dg, dps_ref, next_ref):
        i = pl.program_id(0)
        ps, dprev = _token_shift(p_ref[...], ph_ref[...], mu_ref[...], nt - 1 - i)
        k_k, k_a = kk_ref[...], ka_ref[...]
        q = _rwkv_math(ps, w0_ref[...], a0_ref[...], k_k, k_a, wlw_ref[...], wla_ref[...], wlg_ref[...], lay)
        k, a, lw, ww, tw, sg = q["k"], q["a"], q["lw"], q["ww"], q["tw"], q["sg"]
        for h in range(H):
            sl = slice(h * HEAD, (h + 1) * HEAD)
            s_dr[:, sl] = dr_h[h]
            s_dk2[:, sl] = dk2_h[h]
            s_dv[:, sl] = dv_h[h]
            s_dlw[:, sl] = dlw_h[h]
            s_dg[:, sl] = dg_h[h]
            nrm, inv, kk = _kk_math(q["kkp"][:, sl])
            dbb_h = dbb[h]
            dkk = dbb_h * a[:, sl] - daa[h]
            s_dkkp[:, sl] = jnp.where(nrm > 1e-12, inv * (dkk - kk * _rowsum(dkk * kk)), dkk * inv)
            s_da[:, sl] = dbb_h * kk
        dk2, dkkp, dg = s_dk2[...], s_dkkp[...], s_dg[...]
        dk = dk2 * (1.0 + (a - 1.0) * k_a) + dkkp * k_k
        da = s_da[...] + dk2 * k * k_a
        dpa = da * a * (1.0 - a)
        dww = s_dlw[...] * lw * _sigmoid(-ww)
        dxa = _bdot(dpa, wla_ref[...], "nt")
        dxw = _bdot(dww, wlw_ref[...], "nt") * (1.0 - tw * tw)
        dxg = _bdot(dg, wlg_ref[...], "nt") * sg * (1.0 - sg)
        segs = (s_dr[...], dk, s_dv[...], dxw, dxa, dxg)
        sums = [dmu_ref, dw0_ref, da0_ref, dkk_ref, dka_ref, dwlw_ref, dwla_ref, dwlg_ref]

        @pl.when(i == 0)
        def _():
            for s in sums + [next_ref]:
                s[...] = jnp.zeros_like(s)

        for j, seg in enumerate(segs):
            sl = slice(offs[j], offs[j] + pw[j])
            dps_ref[:, sl] = seg
            dmu_ref[:, sl] += _colsum(seg * dprev[:, sl])
        dw0_ref[...] += _colsum(dww)
        da0_ref[...] += _colsum(dpa)
        dkk_ref[...] += _colsum(dkkp * k)
        dka_ref[...] += _colsum(dk2 * k * (a - 1.0))
        dwlw_ref[...] += _bdot(tw, dww, "tn")
        dwla_ref[...] += _bdot(q["xa"], dpa, "tn")
        dwlg_ref[...] += _bdot(sg, dg, "tn")
        dps = dps_ref[...]
        rid = lax.broadcasted_iota(jnp.int32, (tm, 1), 0)
        nxt = jnp.where(rid == tm - 1, next_ref[...], pltpu.roll(dps, tm - 1, 0))
        mu_v = mu_ref[...]
        dp_ref[...] = (dps * (1.0 - mu_v) + nxt * mu_v).astype(BF16)
        next_ref[...] = _colsum(jnp.where(rid == 0, dps, 0.0))

    whole = lambda arr: pl.BlockSpec(arr.shape, lambda i: (0, 0))
    row = lambda w: pl.BlockSpec((tm, w), lambda i: (nt - 1 - i, 0))
    acc_shapes = [(1, rcp), (1, RW), (1, RW), (1, RW), (1, RW)] + [w.shape for w in lora]
    return pl.pallas_call(
        body, name=name, grid=(nt,),
        in_specs=([row(rcp), pl.BlockSpec((SUBLANE, rcp), lambda i: (jnp.maximum((nt - 1 - i) * hb - 1, 0), 0)), whole(mu)]
                  + [whole(s) for s in small] + [whole(w) for w in lora]
                  + [pl.BlockSpec((H, tm, HEAD), lambda i: (0, nt - 1 - i, 0))] * 7 + [pl.BlockSpec(memory_space=pl.ANY)]),
        out_specs=[row(rcp)] + [pl.BlockSpec(s, lambda i: (0, 0)) for s in acc_shapes],
        out_shape=[jax.ShapeDtypeStruct(dproj.shape, BF16)] + [jax.ShapeDtypeStruct(s, F32) for s in acc_shapes],
        scratch_shapes=[pltpu.VMEM((tm, RW), F32)] * 7 + [pltpu.VMEM((tm, rcp), F32), pltpu.VMEM((1, rcp), F32)],
        input_output_aliases={10 + 7: 0}, compiler_params=_params())(p, p, mu, *small, *lora, *hgrads, dproj)


def _head_post_math(y, r, k2, v, lg, lb, rk):
    yc = y - _mean(y)
    rstd = lax.rsqrt(_mean(yc * yc) + LNX_EPS)
    yn = yc * rstd
    s = _rowsum(r * k2 * rk)
    return yn, rstd, yn * lg + lb + s * v, s


def _head_post(y, r, k2, v, g, hp, name, deps=()):
    H, T, _ = y.shape
    tm = min(128, T)

    def body(y_ref, r_ref, k_ref, v_ref, g_ref, lg_ref, lb_ref, rk_ref, *rest):
        o_ref = rest[-1]
        _, _, t, _ = _head_post_math(y_ref[...], r_ref[...], k_ref[...], v_ref[...], lg_ref[...], lb_ref[...], rk_ref[...])
        out = (t * g_ref[...]).astype(BF16)
        for h in range(H):
            o_ref[:, h * HEAD:(h + 1) * HEAD] = out[h]

    blk = pl.BlockSpec((H, tm, HEAD), lambda i: (0, i, 0))
    par = pl.BlockSpec((H, 1, HEAD), lambda i: (0, 0, 0))
    return pl.pallas_call(
        body, name=name, grid=(T // tm,),
        in_specs=[blk] * 5 + [par] * 3 + [pl.BlockSpec(d.shape, lambda i, nd=d.ndim: (0,) * nd) for d in deps],
        out_specs=pl.BlockSpec((tm, H * HEAD), lambda i: (i, 0)),
        out_shape=jax.ShapeDtypeStruct((T, H * HEAD), BF16), compiler_params=_params())(y, r, k2, v, g, *hp, *deps)


def _bmm(x, y, mode):
    dn = {"nn": (((2,), (1,)), ((0,), (0,))), "nt": (((2,), (2,)), ((0,), (0,))), "tn": (((1,), (1,)), ((0,), (0,)))}[mode]
    (xh, xl), (yh, yl) = _split(x), _split(y)
    dot = lambda p, q: lax.dot_general(p, q, dn, preferred_element_type=F32)
    out = dot(xh, yh)
    if yl is not None:
        out = out + dot(xh, yl)
    if xl is not None:
        out = out + dot(xl, yh)
    return out


def _split(x):
    if isinstance(x, tuple):
        return x
    hi = x.astype(BF16)
    return hi, (x - hi.astype(F32)).astype(BF16)


def _exact(x):
    return x.astype(BF16), None


def _round(x):
    return x if isinstance(x, tuple) else (x.astype(BF16), None)


def _rows(*xs):
    if isinstance(xs[0], tuple):
        return tuple(None if any(p is None for p in parts) else jnp.concatenate(parts, axis=1) for parts in zip(*xs))
    return jnp.concatenate(xs, axis=1)


def _wkv_chunk(r, lw, k, v, a, b, inverse=None):
    hb, C, _ = r.shape
    ti = lax.broadcasted_iota(jnp.int32, (C, C), 0)
    si = lax.broadcasted_iota(jnp.int32, (C, C), 1)
    linc, lstr, eye = (ti >= si).astype(F32), (ti > si).astype(F32), (ti == si).astype(F32)
    qmask = jnp.concatenate([jnp.concatenate([lstr, lstr], axis=1), jnp.concatenate([linc, linc], axis=1)], axis=0)
    lincb = _exact(jnp.broadcast_to(linc, (hb, C, C)))
    both = _exact(jnp.broadcast_to(jnp.concatenate([linc, lstr], axis=0), (hb, 2 * C, C)))
    ones = _exact(jnp.ones_like(v))
    lws = _split(lw)
    ci = _bmm(lincb, lws, "nn")
    cC = jnp.sum(lw, axis=1, keepdims=True)
    gi, ge, gn, gr = jnp.exp(ci), jnp.exp(ci - lw), jnp.exp(-ci), jnp.exp(cC - ci)
    q = dict(At=a * ge, Rt=r * gi, Bt=b * gn, Kt=k * gn, Bh=b * gr, Kh=k * gr)
    s = dict(AR=_round(_rows(q["At"], q["Rt"])), BK=_round(_rows(q["Bt"], q["Kt"])), BKh=_round(_rows(q["Bh"], q["Kh"])), v=_round(v))
    quad = _bmm(s["AR"], s["BK"], "nt") * qmask
    s["top"], s["bot"] = _round(quad[:, :C]), _round(quad[:, C:])
    if inverse is None:
        A_ab = quad[:, :C, :C]
        Tm = eye + A_ab
        Pw = _round(A_ab)
        n = 1
        while 2 * n < C:
            Pw = _round(_bmm(Pw, Pw, "nn"))
            Tm = Tm + _bmm(_round(Tm), Pw, "nn")
            n *= 2
        inverse = Tm
    s["Tm"] = _round(inverse)
    gC = jnp.exp(_bmm(lws, ones, "tn"))
    q.update(gi=gi, ge=ge, gn=gn, gr=gr, qmask=qmask, both=both, gC=gC, ones=ones, s=s)
    return q


def _wkv_u(s, H0s, C):
    arh = _bmm(s["AR"], H0s, "nn")
    zv = _rows(tuple(None if p is None else jnp.zeros_like(p) for p in s["v"]), s["v"])
    U = _bmm(s["Tm"], _round(arh[:, :C] + _bmm(s["top"], zv, "nn")), "nn")
    return arh, _rows(_round(U), s["v"])


def _wkv_fwd(r, lw, k, v, a, b, name):
    H, T, N = r.shape
    C = min(WKV_CHUNK, T)
    nc = T // C
    hb = _pick(H, (16, 8, 4, 2))

    def body(r_ref, lw_ref, k_ref, v_ref, a_ref, b_ref, y_ref, st_ref, inv_ref, u_ref, h_ref):
        @pl.when(pl.program_id(1) == 0)
        def _():
            h_ref[...] = jnp.zeros_like(h_ref)

        H0 = h_ref[...]
        st_ref[0] = H0
        q = _wkv_chunk(r_ref[...], lw_ref[...], k_ref[...], v_ref[...], a_ref[...], b_ref[...])
        s = q["s"]
        arh, UV = _wkv_u(s, _round(H0), C)
        inv_ref[0] = s["Tm"][0]
        u_ref[...] = UV[0][:, :C]
        y_ref[...] = arh[:, C:] + _bmm(s["bot"], UV, "nn")
        h_ref[...] = q["gC"] * H0 + _bmm(s["BKh"], UV, "tn")

    blk = pl.BlockSpec((hb, C, N), lambda h, c: (h, c, 0))
    per_chunk = lambda w: pl.BlockSpec((1, hb, w, w), lambda h, c: (c, h, 0, 0))
    return pl.pallas_call(
        body, name=name, grid=(H // hb, nc), in_specs=[blk] * 6, out_specs=[blk, per_chunk(N), per_chunk(C), blk],
        out_shape=[jax.ShapeDtypeStruct((H, T, N), F32), jax.ShapeDtypeStruct((nc, H, N, N), F32),
                   jax.ShapeDtypeStruct((nc, H, C, C), BF16), jax.ShapeDtypeStruct((H, T, N), BF16)],
        scratch_shapes=[pltpu.VMEM((hb, N, N), F32)], compiler_params=_params())(r, lw, k, v, a, b)


def _wkv_bwd(r, lw, k, v, a, b, states, inverses, u, y, g, hp, dya, name, deps=()):
    H, T, N = r.shape
    C = min(WKV_CHUNK, T)
    nc = T // C
    hb = _pick(H, (16, 8, 4, 2))
    hsum = lambda t: jnp.sum(t, axis=1, keepdims=True)

    def body(r_ref, lw_ref, k_ref, v_ref, a_ref, b_ref, st_ref, inv_ref, u_ref, y_ref, g_ref, lg_ref, lb_ref, rk_ref, dya_ref, *rest):
        (dr_ref, dlw_ref, dk_ref, dv_ref, da_ref, db_ref, dg_ref, dlg_ref, dlb_ref, drk_ref, dh_ref, d_s) = rest[len(deps):]
        first = pl.program_id(1) == 0

        @pl.when(first)
        def _():
            dh_ref[...] = jnp.zeros_like(dh_ref)

        for h in range(hb):
            d_s[h] = dya_ref[:, h * N:(h + 1) * N]
        d_v, r_v, k_v, v_v, lg, rk = d_s[...], r_ref[...], k_ref[...], v_ref[...], lg_ref[...], rk_ref[...]
        yn, rstd, t, bonus = _head_post_math(y_ref[...], r_v, k_v, v_v, lg, lb_ref[...], rk)
        dyo = d_v * g_ref[...]
        dyn = dyo * lg
        ds = _rowsum(dyo * v_v)
        dy = rstd * (dyn - _mean(dyn) - yn * _mean(dyn * yn))
        dg_ref[...] = d_v * t
        sums = (hsum(dyo * yn), hsum(dyo), hsum(ds * r_v * k_v))

        @pl.when(first)
        def _():
            for o_ref, val in zip((dlg_ref, dlb_ref, drk_ref), sums):
                o_ref[...] = val

        @pl.when(jnp.logical_not(first))
        def _():
            for o_ref, val in zip((dlg_ref, dlb_ref, drk_ref), sums):
                o_ref[...] += val

        dHC = dh_ref[...]
        H0 = st_ref[0]
        q = _wkv_chunk(r_v, lw_ref[...], k_v, v_v, a_ref[...], b_ref[...], inverse=inv_ref[0])
        s, gC = q["s"], q["gC"]
        H0s, dHs, dY = _round(H0), _round(dHC), _round(dy)
        UV = _rows(_round(u_ref[...]), s["v"])
        bot_dy = _bmm(s["bot"], dY, "tn")
        bkh_dh = _bmm(s["BKh"], dHs, "nn")
        dP = _round(_bmm(s["Tm"], _round(bot_dy[:, :C] + bkh_dh[:, :C]), "tn"))
        dv_ref[...] = bot_dy[:, C:] + bkh_dh[:, C:] + _bmm(s["top"], dP, "tn")[:, C:] + dyo * bonus
        dPY = _rows(dP, dY)
        dh_ref[...] = gC * dHC + _bmm(s["AR"], dPY, "tn")
        dquad = _round(_bmm(dPY, UV, "nt") * q["qmask"])
        dAR = _bmm(dPY, H0s, "nt") + _bmm(dquad, s["BK"], "nn")
        dBK = _bmm(dquad, s["AR"], "tn")
        dBKh = _bmm(UV, dHs, "nt")
        dAt, dRt, dBt, dKt, dBh, dKh = dAR[:, :C], dAR[:, C:], dBK[:, :C], dBK[:, C:], dBKh[:, :C], dBKh[:, C:]
        dr_ref[...] = dRt * q["gi"] + ds * k_v * rk
        da_ref[...] = dAt * q["ge"]
        db_ref[...] = dBt * q["gn"] + dBh * q["gr"]
        dk_ref[...] = dKt * q["gn"] + dKh * q["gr"] + ds * r_v * rk
        tail = dBh * q["Bh"] + dKh * q["Kh"]
        dci = dRt * q["Rt"] - dBt * q["Bt"] - dKt * q["Kt"] - tail
        dcC = jnp.sum(tail, axis=1, keepdims=True) + _bmm(q["ones"], H0 * dHC * gC, "nt")
        dlw_ref[...] = _bmm(q["both"], _rows(dci, dAt * q["At"]), "tn") + dcC

    blk = pl.BlockSpec((hb, C, N), lambda h, c: (h, nc - 1 - c, 0))
    per_chunk = lambda w: pl.BlockSpec((1, hb, w, w), lambda h, c: (nc - 1 - c, h, 0, 0))
    par = pl.BlockSpec((hb, 1, N), lambda h, c: (h, 0, 0))
    return pl.pallas_call(
        body, name=name, grid=(H // hb, nc),
        in_specs=([blk] * 6 + [per_chunk(N), per_chunk(C), blk, blk, blk] + [par] * 3
                  + [pl.BlockSpec((C, hb * N), lambda h, c: (nc - 1 - c, h))]
                  + [pl.BlockSpec(d.shape, lambda h, c, nd=d.ndim: (0,) * nd) for d in deps]),
        out_specs=[blk] * 7 + [par] * 3,
        out_shape=[jax.ShapeDtypeStruct((H, T, N), F32)] * 7 + [jax.ShapeDtypeStruct((H, 1, N), F32)] * 3,
        scratch_shapes=[pltpu.VMEM((hb, N, N), F32), pltpu.VMEM((hb, C, N), F32)],
        compiler_params=_params())(r, lw, k, v, a, b, states, inverses, u, y, g, *hp, dya, *deps)


def _sgu_ln(z, SW, lng, lnb):
    ge = _gelu(z)
    u, vv = ge[:, :SW], ge[:, SW:]
    xc = vv - _mean(vv)
    rstd = lax.rsqrt(_mean(xc * xc) + LN_EPS)
    vn = xc * rstd
    return u, vn, rstd, vn * lng + lnb


def _causal(ws_ref, g):
    ti = lax.broadcasted_iota(jnp.int32, (SGU_CHUNK, SGU_CHUNK), 0)
    si = lax.broadcasted_iota(jnp.int32, (SGU_CHUNK, SGU_CHUNK), 1)
    return ti >= si, jnp.where(ti >= si, ws_ref[g], 0.0).astype(BF16)


def _sgu_fwd(proj, zblock, lng, lnb, ws, bexp, name):
    T, SW = proj.shape[0], lng.shape[1]
    G = ws.shape[0]
    tr = min(256, T)
    nch = tr // SGU_CHUNK

    def body(z_ref, lng_ref, lnb_ref, ws_ref, be_ref, o_ref):
        u, _, _, vl = _sgu_ln(z_ref[...], SW, lng_ref[...], lnb_ref[...])
        for g in range(G):
            cs = slice(g * SGU_GROUP, (g + 1) * SGU_GROUP)
            _, wc = _causal(ws_ref, g)
            for n in range(nch):
                rs = slice(n * SGU_CHUNK, (n + 1) * SGU_CHUNK)
                m = jnp.dot(wc, vl[rs, cs].astype(BF16), preferred_element_type=F32) + be_ref[:, cs]
                o_ref[rs, cs] = (u[rs, cs] * m).astype(BF16)

    whole = lambda arr: pl.BlockSpec(arr.shape, lambda i, nd=arr.ndim: (0,) * nd)
    return pl.pallas_call(
        body, name=name, grid=(T // tr,),
        in_specs=[pl.BlockSpec((tr, 2 * SW), lambda i: (i, zblock)), whole(lng), whole(lnb), whole(ws), whole(bexp)],
        out_specs=pl.BlockSpec((tr, SW), lambda i: (i, 0)), out_shape=jax.ShapeDtypeStruct((T, SW), BF16),
        compiler_params=_params())(proj, lng, lnb, ws, bexp)


def _sgu_bwd(proj, zblock, dyb, lng, lnb, ws, bexp, dproj, name):
    T, SW = proj.shape[0], lng.shape[1]
    G = ws.shape[0]
    tr = min(256, T)
    nch = tr // SGU_CHUNK
    nt = T // tr

    def body(z_ref, dy_ref, lng_ref, lnb_ref, ws_ref, be_ref, buf_ref, dz_ref, dlg_ref, dlb_ref, dws_ref, db_ref, du_s, dvl_s, dbacc_s):
        i = pl.program_id(0)
        zv = z_ref[...]
        lng_v = lng_ref[...]
        u, vn, rstd, vl = _sgu_ln(zv, SW, lng_v, lnb_ref[...])

        @pl.when(i == 0)
        def _():
            for s in (dlg_ref, dlb_ref, dws_ref, dbacc_s):
                s[...] = jnp.zeros_like(s)

        for g in range(G):
            cs = slice(g * SGU_GROUP, (g + 1) * SGU_GROUP)
            tri, wc = _causal(ws_ref, g)
            for n in range(nch):
                rs = slice(n * SGU_CHUNK, (n + 1) * SGU_CHUNK)
                blk = vl[rs, cs].astype(BF16)
                m = jnp.dot(wc, blk, preferred_element_type=F32) + be_ref[:, cs]
                dyv = dy_ref[rs, cs]
                du_s[rs, cs] = dyv * m
                dm = dyv * u[rs, cs]
                dvl_s[rs, cs] = _bdot(wc, dm, "tn")
                dws_ref[g] += jnp.where(tri, _bdot(dm, blk, "nt"), 0.0)
                dbacc_s[:, cs] += dm

        dvl = dvl_s[...]
        dlg_ref[...] += _colsum(dvl * vn)
        dlb_ref[...] += _colsum(dvl)
        dvn = dvl * lng_v
        dvv = rstd * (dvn - _mean(dvn) - vn * _mean(dvn * vn))
        gp = _gelu_grad(zv)
        dz_ref[:, :SW] = (du_s[...] * gp[:, :SW]).astype(BF16)
        dz_ref[:, SW:] = (dvv * gp[:, SW:]).astype(BF16)

        @pl.when(i == nt - 1)
        def _():
            lane = lax.broadcasted_iota(jnp.int32, (SGU_CHUNK, LANE), 1)
            out = jnp.zeros((SGU_CHUNK, LANE), F32)
            for g in range(G):
                col = jnp.sum(dbacc_s[:, g * SGU_GROUP:(g + 1) * SGU_GROUP], axis=1, keepdims=True)
                out = jnp.where(lane == g, col, out)
            db_ref[...] = out

    whole = lambda arr: pl.BlockSpec(arr.shape, lambda i, nd=arr.ndim: (0,) * nd)
    acc_shapes = [(1, SW), (1, SW), ws.shape, (SGU_CHUNK, LANE)]
    return pl.pallas_call(
        body, name=name, grid=(nt,),
        in_specs=[pl.BlockSpec((tr, 2 * SW), lambda i: (i, zblock)), pl.BlockSpec((tr, SW), lambda i: (i, 0)),
                  whole(lng), whole(lnb), whole(ws), whole(bexp), pl.BlockSpec(memory_space=pl.ANY)],
        out_specs=([pl.BlockSpec((tr, 2 * SW), lambda i: (i, zblock))]
                   + [pl.BlockSpec(s, lambda i, nd=len(s): (0,) * nd) for s in acc_shapes]),
        out_shape=[jax.ShapeDtypeStruct(dproj.shape, BF16)] + [jax.ShapeDtypeStruct(s, F32) for s in acc_shapes],
        scratch_shapes=[pltpu.VMEM((tr, SW), F32), pltpu.VMEM((tr, SW), F32), pltpu.VMEM((SGU_CHUNK, SW), F32)],
        input_output_aliases={6: 0}, compiler_params=_params())(proj, dyb, lng, lnb, ws, bexp, dproj)


_HBM = pl.BlockSpec(memory_space=pltpu.HBM)
_SEM = pl.BlockSpec(memory_space=pltpu.SEMAPHORE)
_DATAFLOW = pltpu.SideEffectType.DATAFLOW_SIDE_EFFECTING


def _mesh_place(chips=False):
    x, y, c = lax.axis_index("x"), lax.axis_index("y"), lax.axis_index("c")
    return x, y, c, (2 * x + y if chips else 4 * x + 2 * y + c)


def _peer(x, y, c, rel, chips=False):
    px = 1 - x if rel & 4 else x
    py = 1 - y if rel & 2 else y
    pc = 1 - c if rel & 1 else c
    return (px, py, pc), (2 * px + py if chips else 4 * px + 2 * py + pc)


ALL_PEERS = tuple(range(1, N_DEV))
SIBLING = (1,)
SAME_CORE = (2, 4, 6)
SIBLINGS_CORE = (3, 5, 7)


def _exchange_start(groups, name, rels=ALL_PEERS, chips=False):
    flat = [t for g in groups for t in g]
    sizes = [len(g) for g in groups]
    n, ng = len(flat), len(groups)
    srcs = [pltpu.with_memory_space_constraint(a, pltpu.HBM) for a, _ in flat]
    lands = [pltpu.with_memory_space_constraint(lax.empty(((N_DEV,) + a.shape) if isg else a.shape, a.dtype), pltpu.HBM)
             for a, isg in flat]

    def body(*refs):
        ins, lnd, sems, token = refs[:n], refs[n:2 * n], refs[2 * n:2 * n + 3 * ng], refs[-1]
        x, y, c, me = _mesh_place(chips)
        j0 = 0
        for gi, sz in enumerate(sizes):
            for rel in rels:
                dev, slot = _peer(x, y, c, rel, chips)
                for jj in range(sz):
                    j = j0 + jj
                    pltpu.make_async_remote_copy(
                        src_ref=ins[j] if flat[j][1] else ins[j].at[slot], dst_ref=lnd[j].at[me],
                        send_sem=sems[3 * gi].at[jj * (N_DEV - 1) + rel - 1], recv_sem=sems[3 * gi + 1].at[jj * (N_DEV - 1) + rel - 1],
                        device_id=dev, device_id_type=pl.DeviceIdType.MESH).start()
            for jj in range(sz):
                j = j0 + jj
                pltpu.make_async_copy(ins[j] if flat[j][1] else ins[j].at[me], lnd[j].at[me], sems[3 * gi + 2].at[jj]).start()
            j0 += sz
        token[...] = jnp.zeros_like(token)

    sem_shapes = [pltpu.SemaphoreType.DMA((k,)) for sz in sizes for k in (sz * (N_DEV - 1), sz * (N_DEV - 1), sz)]
    res = pl.pallas_call(
        body, name=name,
        out_shape=(*sem_shapes, *[pltpu.HBM(a.shape, a.dtype) for a in srcs], *[pltpu.HBM(a.shape, a.dtype) for a in lands],
                   jax.ShapeDtypeStruct((SUBLANE, LANE), F32)),
        in_specs=[_HBM] * (2 * n), out_specs=(*[_SEM] * (3 * ng), *[_HBM] * (2 * n), pl.BlockSpec(memory_space=pltpu.VMEM)),
        input_output_aliases={i: 3 * ng + i for i in range(2 * n)},
        compiler_params=pltpu.CompilerParams(has_side_effects=_DATAFLOW))(*srcs, *lands)
    sems, thru, token = res[:3 * ng], res[3 * ng:3 * ng + 2 * n], res[-1]
    handle, j0 = [], 0
    for gi, sz in enumerate(sizes):
        handle.append(dict(kinds=[k for _, k in groups[gi]], chips=chips, srcs=list(thru[j0:j0 + sz]), lands=list(thru[n + j0:n + j0 + sz]),
                           sems=list(sems[3 * gi:3 * gi + 3])))
        j0 += sz
    return handle, token


def _exchange_wait(group, after, name, rels=ALL_PEERS, local=True):
    kinds, sz = group["kinds"], len(group["kinds"])
    relay = group.get("relay", [])

    def body(*refs):
        ins, lnd, (ssem, rsem, lsem) = refs[:sz], refs[sz:2 * sz], refs[2 * sz:2 * sz + 3]
        x, y, c, me = _mesh_place(group["chips"])
        for rel in rels:
            dev, slot = _peer(x, y, c, rel, group["chips"])
            for jj in range(sz):
                cp = pltpu.make_async_remote_copy(
                    src_ref=ins[jj] if kinds[jj] else ins[jj].at[slot], dst_ref=lnd[jj].at[slot],
                    send_sem=ssem.at[jj * (N_DEV - 1) + rel - 1], recv_sem=rsem.at[jj * (N_DEV - 1) + rel - 1],
                    device_id=dev, device_id_type=pl.DeviceIdType.MESH)
                cp.wait_send()
                cp.wait_recv()
        if local:
            for jj in range(sz):
                pltpu.make_async_copy(ins[jj] if kinds[jj] else ins[jj].at[me], lnd[jj].at[me], lsem.at[jj]).wait()
        if relay:
            fsend, frecv = refs[2 * sz + 3:2 * sz + 5]
            dev = _peer(x, y, c, 1)[0]
            for q, (mine, theirs) in enumerate(zip(SAME_CORE, SIBLINGS_CORE)):
                for jj in range(sz):
                    cp = pltpu.make_async_remote_copy(
                        src_ref=lnd[jj].at[_peer(x, y, c, mine)[1]], dst_ref=lnd[jj].at[_peer(x, y, c, theirs)[1]],
                        send_sem=fsend.at[jj * len(SAME_CORE) + q], recv_sem=frecv.at[jj * len(SAME_CORE) + q],
                        device_id=dev, device_id_type=pl.DeviceIdType.MESH)
                    cp.wait_send()
                    cp.wait_recv()

    arrays = group["srcs"] + group["lands"]
    sems = group["sems"] + relay
    res = pl.pallas_call(
        body, name=name, out_shape=[pltpu.HBM(a.shape, a.dtype) for a in arrays],
        in_specs=[_HBM] * (2 * sz) + [_SEM] * len(sems) + [pl.BlockSpec(memory_space=pl.ANY)], out_specs=[_HBM] * (2 * sz),
        input_output_aliases={i: i for i in range(2 * sz)},
        compiler_params=pltpu.CompilerParams(has_side_effects=_DATAFLOW))(*arrays, *sems, after)
    return dict(group, srcs=list(res[:sz]), lands=list(res[sz:]), relay=[])


def _relay_start(group, name):
    sz = len(group["kinds"])
    nq = len(SAME_CORE)

    def body(*refs):
        lnd, fsend, frecv, token = refs[:sz], refs[sz], refs[sz + 1], refs[-1]
        x, y, c, _ = _mesh_place()
        dev = _peer(x, y, c, 1)[0]
        for q, rel in enumerate(SAME_CORE):
            slot = _peer(x, y, c, rel)[1]
            for jj in range(sz):
                pltpu.make_async_remote_copy(
                    src_ref=lnd[jj].at[slot], dst_ref=lnd[jj].at[slot], send_sem=fsend.at[jj * nq + q], recv_sem=frecv.at[jj * nq + q],
                    device_id=dev, device_id_type=pl.DeviceIdType.MESH).start()
        token[...] = jnp.zeros_like(token)

    lands = group["lands"]
    res = pl.pallas_call(
        body, name=name,
        out_shape=(pltpu.SemaphoreType.DMA((sz * nq,)), pltpu.SemaphoreType.DMA((sz * nq,)), *[pltpu.HBM(a.shape, a.dtype) for a in lands],
                   jax.ShapeDtypeStruct((SUBLANE, LANE), F32)),
        in_specs=[_HBM] * sz, out_specs=(_SEM, _SEM, *[_HBM] * sz, pl.BlockSpec(memory_space=pltpu.VMEM)),
        input_output_aliases={i: 2 + i for i in range(sz)},
        compiler_params=pltpu.CompilerParams(has_side_effects=_DATAFLOW))(*lands)
    return dict(group, lands=list(res[2:2 + sz]), relay=[res[0], res[1]]), res[-1]


def _sibling_swap(arrays, handle, after, name):
    start = handle is None
    n = len(arrays) if start else len(handle["srcs"])
    chips = N_DEV // 2
    if start:
        srcs = [pltpu.with_memory_space_constraint(a.reshape(chips, 2, *a.shape[1:]), pltpu.HBM) for a in arrays]
        lands = [pltpu.with_memory_space_constraint(lax.empty((chips,) + a.shape[1:], a.dtype), pltpu.HBM) for a in arrays]
    else:
        srcs, lands = handle["srcs"], handle["lands"]

    def body(*refs):
        ins, lnd, ssem, rsem = refs[:n], refs[n:2 * n], refs[2 * n], refs[2 * n + 1]
        x, y, c, _ = _mesh_place()
        dev = _peer(x, y, c, 1)[0]
        for q in range(chips):
            for j in range(n):
                cp = pltpu.make_async_remote_copy(
                    src_ref=ins[j].at[q, 1 - c], dst_ref=lnd[j].at[q], send_sem=ssem.at[j * chips + q], recv_sem=rsem.at[j * chips + q],
                    device_id=dev, device_id_type=pl.DeviceIdType.MESH)
                if start:
                    cp.start()
                else:
                    cp.wait_send()
                    cp.wait_recv()
        if start:
            refs[-1][...] = jnp.zeros_like(refs[-1])

    thru = [pltpu.HBM(a.shape, a.dtype) for a in srcs + lands]
    effect = pltpu.CompilerParams(has_side_effects=_DATAFLOW)
    if start:
        res = pl.pallas_call(
            body, name=name, out_shape=(pltpu.SemaphoreType.DMA((n * chips,)), pltpu.SemaphoreType.DMA((n * chips,)), *thru,
                                        jax.ShapeDtypeStruct((SUBLANE, LANE), F32)),
            in_specs=[_HBM] * (2 * n), out_specs=(_SEM, _SEM, *[_HBM] * (2 * n), pl.BlockSpec(memory_space=pltpu.VMEM)),
            input_output_aliases={i: 2 + i for i in range(2 * n)}, compiler_params=effect)(*srcs, *lands)
        return dict(srcs=list(res[2:2 + n]), lands=list(res[2 + n:2 + 2 * n]), sems=[res[0], res[1]]), res[-1]
    res = pl.pallas_call(
        body, name=name, out_shape=thru, in_specs=[_HBM] * (2 * n) + [_SEM, _SEM, pl.BlockSpec(memory_space=pl.ANY)],
        out_specs=[_HBM] * (2 * n), input_output_aliases={i: i for i in range(2 * n)}, compiler_params=effect)(
            *srcs, *lands, *handle["sems"], after)
    return dict(handle, srcs=list(res[:n]), lands=list(res[n:]))


def _pair_add(mine, theirs, core, name):
    chips, _, rows, w = mine.shape
    tm = _pick(rows, (256, 128, 64, 32, 16))

    def body(core_ref, a_ref, b_ref, o_ref):
        o_ref[...] = (a_ref[...].astype(F32) + b_ref[...].astype(F32)).astype(o_ref.dtype)

    return pl.pallas_call(
        body, name=name, out_shape=jax.ShapeDtypeStruct(theirs.shape, theirs.dtype),
        grid_spec=pltpu.PrefetchScalarGridSpec(
            num_scalar_prefetch=1, grid=(chips, rows // tm),
            in_specs=[pl.BlockSpec((None, None, tm, w), lambda q, i, core_ref: (q, core_ref[0], i, 0)),
                      pl.BlockSpec((None, tm, w), lambda q, i, core_ref: (q, i, 0))],
            out_specs=pl.BlockSpec((None, tm, w), lambda q, i, core_ref: (q, i, 0))),
        compiler_params=_params())(core, mine, theirs)


def _adamw(w, m, v, gparts, name, after=None):
    R, C = w.shape
    tm = _pick(R, (256, 128, 64, 32, 16, 8))
    order = [] if after is None else [after]

    def body(w_ref, m_ref, v_ref, g_ref, *rest):
        go, do, mo, vo = rest[len(order):]
        g = g_ref[0].astype(F32)
        for j in range(1, gparts.shape[0]):
            g = g + g_ref[j].astype(F32)
        mn = ADAM_B1 * m_ref[...] + (1.0 - ADAM_B1) * g
        vn = ADAM_B2 * v_ref[...] + (1.0 - ADAM_B2) * (g * g)
        m_hat = mn / (1.0 - ADAM_B1 ** ADAM_STEP)
        v_hat = vn / (1.0 - ADAM_B2 ** ADAM_STEP)
        go[...] = g
        do[...] = -ADAM_LR * (m_hat / (jnp.sqrt(v_hat) + ADAM_EPS) + ADAM_WD * w_ref[...])
        mo[...] = mn
        vo[...] = vn

    row = pl.BlockSpec((tm, C), lambda i: (i, 0))
    return pl.pallas_call(
        body, name=name, grid=(R // tm,),
        in_specs=[row, row, row, pl.BlockSpec((gparts.shape[0], tm, C), lambda i: (0, i, 0))] + [pl.BlockSpec(memory_space=pl.ANY)] * len(order),
        out_specs=[row] * 4, out_shape=[jax.ShapeDtypeStruct((R, C), F32)] * 4, compiler_params=_params())(w, m, v, gparts, *order)


def _pack(arrays):
    parts = []
    for a in arrays:
        f = a.reshape(1, -1)
        pad = _ceil_to(f.shape[1], SUBLANE * LANE) - f.shape[1]
        f = jnp.concatenate([f, jnp.zeros((1, pad), f.dtype)], axis=1) if pad else f
        parts.append(f.reshape(-1, LANE))
    rows = sum(p.shape[0] for p in parts)
    pad = _ceil_to(rows, 64) - rows
    return jnp.concatenate(parts + ([jnp.zeros((pad, LANE), parts[0].dtype)] if pad else []), axis=0)


def _unpack(buf, shapes):
    out, row = [], 0
    for s in shapes:
        size = 1
        for d in s:
            size *= d
        rows = _ceil_to(size, SUBLANE * LANE) // LANE
        out.append(buf[row:row + rows].reshape(1, -1)[:, :size].reshape(s))
        row += rows
    return out


def kernel(x, norm_mix_g, w_in, shift_mu, w0, w_lora_up, a0, a_lora_up, g_lora_up, k_k, k_a, r_k, lnx_g, lnx_b, w_proj_rwkv, sgu_ln_g, sgu_ln_b, sgu_w, sgu_b, w_proj_sgu, w_out, norm_ffn_g, w_ffn_gate, w_ffn_up, w_ffn_down, norm_final_g, loss_target, m_norm_mix_g, m_w_in, m_shift_mu, m_w0, m_w_lora_up, m_a0, m_a_lora_up, m_g_lora_up, m_k_k, m_k_a, m_r_k, m_lnx_g, m_lnx_b, m_w_proj_rwkv, m_sgu_ln_g, m_sgu_ln_b, m_sgu_w, m_sgu_b, m_w_proj_sgu, m_w_out, m_norm_ffn_g, m_w_ffn_gate, m_w_ffn_up, m_w_ffn_down, m_norm_final_g, v_norm_mix_g, v_w_in, v_shift_mu, v_w0, v_w_lora_up, v_a0, v_a_lora_up, v_g_lora_up, v_k_k, v_k_a, v_r_k, v_lnx_g, v_lnx_b, v_w_proj_rwkv, v_sgu_ln_g, v_sgu_ln_b, v_sgu_w, v_sgu_b, v_w_proj_sgu, v_w_out, v_norm_ffn_g, v_w_ffn_gate, v_w_ffn_up, v_w_ffn_down, v_norm_final_g):
    weights = dict(norm_mix_g=norm_mix_g, w_in=w_in, shift_mu=shift_mu, w0=w0, w_lora_up=w_lora_up, a0=a0, a_lora_up=a_lora_up,
                   g_lora_up=g_lora_up, k_k=k_k, k_a=k_a, r_k=r_k, lnx_g=lnx_g, lnx_b=lnx_b, w_proj_rwkv=w_proj_rwkv,
                   sgu_ln_g=sgu_ln_g, sgu_ln_b=sgu_ln_b, sgu_w=sgu_w, sgu_b=sgu_b, w_proj_sgu=w_proj_sgu, w_out=w_out,
                   norm_ffn_g=norm_ffn_g, w_ffn_gate=w_ffn_gate, w_ffn_up=w_ffn_up, w_ffn_down=w_ffn_down, norm_final_g=norm_final_g)
    m_in = dict(norm_mix_g=m_norm_mix_g, w_in=m_w_in, shift_mu=m_shift_mu, w0=m_w0, w_lora_up=m_w_lora_up, a0=m_a0,
                a_lora_up=m_a_lora_up, g_lora_up=m_g_lora_up, k_k=m_k_k, k_a=m_k_a, r_k=m_r_k, lnx_g=m_lnx_g, lnx_b=m_lnx_b,
                w_proj_rwkv=m_w_proj_rwkv, sgu_ln_g=m_sgu_ln_g, sgu_ln_b=m_sgu_ln_b, sgu_w=m_sgu_w, sgu_b=m_sgu_b,
                w_proj_sgu=m_w_proj_sgu, w_out=m_w_out, norm_ffn_g=m_norm_ffn_g, w_ffn_gate=m_w_ffn_gate, w_ffn_up=m_w_ffn_up,
                w_ffn_down=m_w_ffn_down, norm_final_g=m_norm_final_g)
    v_in = dict(norm_mix_g=v_norm_mix_g, w_in=v_w_in, shift_mu=v_shift_mu, w0=v_w0, w_lora_up=v_w_lora_up, a0=v_a0,
                a_lora_up=v_a_lora_up, g_lora_up=v_g_lora_up, k_k=v_k_k, k_a=v_k_a, r_k=v_r_k, lnx_g=v_lnx_g, lnx_b=v_lnx_b,
                w_proj_rwkv=v_w_proj_rwkv, sgu_ln_g=v_sgu_ln_g, sgu_ln_b=v_sgu_ln_b, sgu_w=v_sgu_w, sgu_b=v_sgu_b,
                w_proj_sgu=v_w_proj_sgu, w_out=v_w_out, norm_ffn_g=v_norm_ffn_g, w_ffn_gate=v_w_ffn_gate, w_ffn_up=v_w_ffn_up,
                w_ffn_down=v_w_ffn_down, norm_final_g=v_norm_final_g)
    names = list(weights)
    col_sharded = ("w_in", "w_lora_up", "a_lora_up", "g_lora_up", "w_proj_rwkv", "w_proj_sgu", "w_ffn_gate", "w_ffn_up")
    row_sharded = ("w_out", "w_ffn_down")
    sharded = [n for n in names if n in col_sharded or n in row_sharded]
    small = [n for n in names if n not in sharded]

    xs, tgt = x[0], loss_target[0]
    T, D = xs.shape
    RW = w0.shape[1]
    H = RW // HEAD
    SW = sgu_ln_g.shape[1]
    G = sgu_w.shape[1]
    assert 2 * SW == D, "the projection layout takes the SGU part to be as wide as a gate"
    lay = _rwkv_layout(RW, w_lora_up.shape[1], a_lora_up.shape[1], g_lora_up.shape[1], D)
    _, pw, _, rcp = lay
    icp = rcp + 3 * D
    b_ga, b_gb, b_z = rcp // D, rcp // D + 1, rcp // D + 2

    gather_groups = dict(win=["w_in", "w_lora_up", "a_lora_up", "g_lora_up"], proj=["w_proj_rwkv", "w_proj_sgu", "w_out"],
                         ffn_gate_up=["w_ffn_gate", "w_ffn_up"], ffn_down=["w_ffn_down"])
    handles, gather_token = _exchange_start([[(weights[n][0].astype(BF16), True) for n in grp] for grp in gather_groups.values()],
                                            "gather_start", rels=SIBLING + SAME_CORE)
    gather = dict(zip(gather_groups, handles))
    full = {}
    relay_tokens = {}
    joined = lambda g: g.transpose(1, 0, 2).reshape(g.shape[1], -1)

    def relay_weights(key, after):
        arrived = _exchange_wait(gather[key], after, "gather_wait_ici_" + key, rels=SAME_CORE, local=False)
        gather[key], relay_tokens[key] = _relay_start(arrived, "gather_relay_" + key)

    def take_weights(key, after):
        done = _exchange_wait(gather[key], after, "gather_wait_d2d_" + key, rels=SIBLING)
        for n, g in zip(gather_groups[key], done["lands"]):
            full[n] = g.reshape(-1, g.shape[2]) if n in row_sharded else g

    packed = [_pack([d[n] for n in small] + [gather_token]) for d in (weights, m_in, v_in)]
    n1 = _rms_fwd(xs, norm_mix_g, "rms_mix", deps=[gather_token, *packed])
    relay_weights("win", n1)
    take_weights("win", relay_tokens["win"])
    W_in = _w_in_to_proj(full["w_in"], lay, D, "w_in_layout")
    lora = [_pad_rows(joined(full[n]), rows) for n, rows in zip(("w_lora_up", "a_lora_up", "g_lora_up"), pw[3:])]
    mu_p = _pad_rwkv_cols(shift_mu, lay)
    rsmall = [w0, a0, k_k, k_a]
    hp = [lnx_g.reshape(H, 1, HEAD), lnx_b.reshape(H, 1, HEAD), r_k.reshape(H, 1, HEAD)]
    ws = sgu_w[0]
    bexp = jnp.repeat(sgu_b[0].T, SGU_GROUP, axis=1)
    gf = norm_final_g.reshape(1, D)

    proj = _matmul(n1, W_in, mode="nn", out_dtype=F32, name="proj_in")
    ga, gb = (proj, D, b_ga), (proj, D, b_gb)
    r_h, lw_h, k2_h, v_h, aa_h, bb_h, g_h = _rwkv_pre(proj, mu_p, rsmall, lora, lay, "rwkv_pre")
    wkv_in = [r_h, lw_h, k2_h, v_h, aa_h, bb_h]
    y_h, *wkv_saved = _wkv_fwd(*wkv_in, "wkv_fwd")
    relay_weights("proj", y_h)
    ya = _head_post(y_h, r_h, k2_h, v_h, g_h, hp, "head_post", deps=[relay_tokens["proj"]])
    relay_weights("ffn_gate_up", ya)
    yb = _sgu_fwd(proj, b_z, sgu_ln_g, sgu_ln_b, ws, bexp, "sgu_fwd")
    take_weights("proj", ya)
    pa = _matmul(ya, full["w_proj_rwkv"], mode="nn", out_dtype=F32, name="proj_a", deps=[relay_tokens["ffn_gate_up"]])

    def merge_fn(pb_v, pa_v, ga_v, gb_v):
        return pb_v, _sigmoid(ga_v) * pa_v + _sigmoid(gb_v) * pb_v
    pb, merged = _matmul(yb, full["w_proj_sgu"], mode="nn", name="proj_b_merge",
                         epi=(merge_fn, [pa, (proj, b_ga * D), (proj, b_gb * D)], [F32, BF16]))
    h1 = _matmul(merged, full["w_out"], mode="nn", out_dtype=F32, name="out_proj", add=xs)
    n2 = _rms_fwd(h1, norm_ffn_g, "rms_ffn")
    relay_weights("ffn_down", n2)
    take_weights("ffn_gate_up", n2)

    def act_fn(gt_v, up_v):
        return gt_v, up_v, gt_v * _sigmoid(gt_v) * up_v
    gt, up, act = _matmul(n2, full["w_ffn_gate"], b2=full["w_ffn_up"], mode="nn", name="ffn_gate_up_act", out_blocks=N_DEV,
                          epi=(act_fn, [], [BF16, BF16, BF16]), deps=[relay_tokens["ffn_down"]])
    take_weights("ffn_down", act)
    h2 = _matmul(act, full["w_ffn_down"], mode="nn", out_dtype=F32, name="ffn_down", add=h1)

    def final_fn(rv, pv):
        (h_v, t_v), (g_v,) = rv, pv
        r = lax.rsqrt(_mean(h_v * h_v) + RMS_EPS)
        yn = h_v * r
        e = yn * g_v - t_v
        loss = 0.5 * jnp.sum(_mean(e * e))
        dout = e * (1.0 / D)
        dyg = dout * g_v
        dh = r * (dyg - yn * _mean(dyg * yn))
        return [dh, dh], [jnp.full((1, LANE), loss, F32), _colsum(dout * yn)]
    dh2, dh2_bf, loss_part, d_gf = _rowwise(final_fn, [h2, tgt], [gf], [(D, F32), (D, BF16)], [(1, LANE), (1, D)], name="final_loss")

    grads = {}

    def start_scatter(group, name, extra=()):
        blocks = [(grads[n].reshape(N_DEV, -1, grads[n].shape[1]) if n in row_sharded else grads[n], False) for n in group]
        (handle,), token = _exchange_start([blocks + list(extra)], name)
        return handle, token

    def dact_fn(d_v, gt_v, up_v):
        gt_v, up_v = gt_v.astype(F32), up_v.astype(F32)
        s = _sigmoid(gt_v)
        return d_v * up_v * (s * (1.0 + gt_v * (1.0 - s))), d_v * gt_v * s
    dgt, dup = _matmul(dh2_bf, full["w_ffn_down"], mode="nt", name="d_ffn_act", out_blocks=N_DEV,
                       epi=(dact_fn, [gt, up], [BF16, BF16]))
    scatter_groups = dict(ffn_down=["w_ffn_down"], ffn_gate=["w_ffn_gate"], ffn_up=["w_ffn_up"],
                          mid=["w_out", "w_proj_rwkv", "w_proj_sgu"], last=["w_in", "w_lora_up", "a_lora_up", "g_lora_up"])
    scatters = {}
    grads["w_ffn_down"] = _matmul(act, dh2_bf, mode="tn", out_dtype=BF16, name="dw_ffn_down")
    scatters["ffn_down"], token = start_scatter(scatter_groups["ffn_down"], "scatter_start_ffn_down")
    dn2 = _matmul(dgt, full["w_ffn_gate"], mode="nt", out_dtype=F32, name="dn2_gate", deps=[token])
    grads["w_ffn_gate"] = _matmul(n2, dgt, mode="tn", out_dtype=BF16, name="dw_ffn_gate", out_blocks=N_DEV)
    scatters["ffn_gate"], token = start_scatter(scatter_groups["ffn_gate"], "scatter_start_ffn_gate")
    grads["w_ffn_up"] = _matmul(n2, dup, mode="tn", out_dtype=BF16, name="dw_ffn_up", out_blocks=N_DEV, deps=[token])
    scatters["ffn_up"], token = start_scatter(scatter_groups["ffn_up"], "scatter_start_ffn_up")
    dn2 = _matmul(dup, full["w_ffn_up"], mode="nt", out_dtype=F32, name="dn2_up", add=dn2, deps=[token])
    dh1, dh1_bf, d_g2 = _rms_bwd(dn2, h1, dh2, norm_ffn_g, "rms_ffn_bwd")
    dmerged = _matmul(dh1_bf, full["w_out"], mode="nt", out_dtype=F32, name="d_merged")
    grads["w_out"] = _matmul(merged, dh1_bf, mode="tn", out_dtype=BF16, name="dw_out")

    def dmerge_fn(rv, pv):
        d_v, ga_v, gb_v, pa_v, pb_v = rv
        sa, sb = _sigmoid(ga_v), _sigmoid(gb_v)
        dgates = jnp.concatenate([d_v * pa_v * sa * (1.0 - sa), d_v * pb_v * sb * (1.0 - sb)], axis=1)
        return [dgates, d_v * sa, d_v * sb], []
    dproj, dpa, dpb = _rowwise(dmerge_fn, [dmerged, ga, gb, pa, pb], [],
                               [(2 * D, BF16, icp, b_ga // 2, None), (D, BF16), (D, BF16)], [], name="d_merge")
    dya = _matmul(dpa, full["w_proj_rwkv"], mode="nt", out_dtype=F32, name="d_ya")
    dyb = _matmul(dpb, full["w_proj_sgu"], mode="nt", out_dtype=F32, name="d_yb")
    grads["w_proj_rwkv"] = _matmul(ya, dpa, mode="tn", out_dtype=BF16, name="dw_proj_a", out_blocks=N_DEV)
    grads["w_proj_sgu"] = _matmul(yb, dpb, mode="tn", out_dtype=BF16, name="dw_proj_b", out_blocks=N_DEV)
    scatters["mid"], token_mid = start_scatter(scatter_groups["mid"], "scatter_start_mid")
    dproj, d_lng, d_lnb, d_ws, d_bs = _sgu_bwd(proj, b_z, dyb, sgu_ln_g, sgu_ln_b, ws, bexp, dproj, "sgu_bwd")

    dr_h, dlw_h, dk2_h, dv_h, daa, dbb, dg_h, d_lnxg, d_lnxb, d_rk = _wkv_bwd(
        *wkv_in, *wkv_saved, y_h, g_h, hp, dya, "wkv_bwd", deps=[token_mid])
    dproj, d_mu, d_w0, d_a0, d_kk, d_ka, d_wlw, d_wla, d_wlg = _rwkv_pre_bwd(
        proj, mu_p, rsmall, lora, [dr_h, dk2_h, dv_h, dlw_h, daa, dbb, dg_h], dproj, lay, "rwkv_pre_bwd")
    split = lambda g: g.reshape(g.shape[0], N_DEV, -1).transpose(1, 0, 2)
    grads["w_in"] = _dw_in_from_proj(_matmul(n1, dproj, mode="tn", out_dtype=BF16, name="dw_in"), lay, D, w_in.shape[2], "dw_in_layout")
    grads["w_lora_up"] = split(d_wlw[:w_lora_up.shape[1]].astype(BF16))
    grads["a_lora_up"] = split(d_wla[:a_lora_up.shape[1]].astype(BF16))
    grads["g_lora_up"] = split(d_wlg[:g_lora_up.shape[1]].astype(BF16))
    out = {}

    def update_group(key, after):
        handle = scatters[key]
        parts = _exchange_wait(handle, after, "scatter_wait_" + key, rels=SAME_CORE if handle["chips"] else ALL_PEERS)["lands"]
        for n, part in zip(scatter_groups[key], parts):
            res = _adamw(weights[n][0], m_in[n][0], v_in[n][0], part, "adamw_" + n, after=after)
            out[n] = [t.reshape(weights[n].shape) for t in res]
            after = res[0]
        return after

    swap, token_swap = _sibling_swap([grads[n] for n in scatter_groups["last"]], None, None, "scatter_last_swap_start")
    after = update_group("ffn_gate", update_group("ffn_down", token_swap))
    swap = _sibling_swap(None, swap, after, "scatter_last_swap_wait")
    core = lax.axis_index("c").astype(jnp.int32).reshape(1)
    chip_sums = [_pair_add(mine, theirs, core, "scatter_last_add_" + n)
                 for n, mine, theirs in zip(scatter_groups["last"], swap["srcs"], swap["lands"])]
    (scatters["last"],), token_in = _exchange_start([[(s, False) for s in chip_sums]], "scatter_start_last", rels=SAME_CORE, chips=True)
    dn1 = _matmul(dproj, W_in, mode="nt", out_dtype=F32, name="dn1", deps=[token_in])
    dx, _, d_g1 = _rms_bwd(dn1, xs, dh1, norm_mix_g, "rms_mix_bwd")
    small_grads = dict(norm_mix_g=d_g1, shift_mu=_unpad_rwkv_cols(d_mu, lay), w0=d_w0, a0=d_a0, k_k=d_kk, k_a=d_ka, r_k=d_rk,
                       lnx_g=d_lnxg, lnx_b=d_lnxb, sgu_ln_g=d_lng, sgu_ln_b=d_lnb, sgu_w=d_ws, sgu_b=d_bs[:, :G].T,
                       norm_ffn_g=d_g2, norm_final_g=d_gf)
    (gather_small,), after = _exchange_start([[(_pack([small_grads[n] for n in small] + [jnp.zeros_like(gather_token)]), True)]],
                                             "gather_small_start")
    for key in ("ffn_up", "mid", "last"):
        after = update_group(key, after)
    small_parts = _exchange_wait(gather_small, after, "gather_small_wait")["lands"][0]
    res = _adamw(*packed, small_parts, "adamw_small")
    unpacked = [_unpack(t, [weights[n].shape for n in small]) for t in res]
    for i, n in enumerate(small):
        out[n] = [u[i] for u in unpacked]

    loss = lax.psum(loss_part[0, 0], ("x", "y", "c"))
    return (loss, dx[None], *[out[n][0] for n in names], *[out[n][1] for n in names],
            *[out[n][2] for n in names], *[out[n][3] for n in names])
```

```python
import jax
import jax.numpy as jnp
from jax import lax
from jax.experimental import pallas as pl
from jax.experimental.pallas import tpu as pltpu

F32 = jnp.float32
BF16 = jnp.bfloat16

N_DEV = 8
LANE = 128
SUBLANE = 8
HEAD = 64
SGU_CHUNK = 128
SGU_GROUP = 128
WKV_CHUNK = 64
RMS_EPS = 1e-6
LN_EPS = 1e-5
LNX_EPS = 64e-5
ADAM_LR, ADAM_B1, ADAM_B2, ADAM_EPS, ADAM_WD, ADAM_STEP = 0.001, 0.9, 0.999, 1e-08, 0.01, 10
VMEM_LIMIT_BYTES = 48 * 1024 * 1024
_SQRT_HALF = 0.7071067811865476
_INV_SQRT_2PI = 0.3989422804014327


def _pick(n, cands):
    for c in cands:
        if n % c == 0:
            return c
    return n


def _ceil_to(n, m):
    return -(-n // m) * m


def _params():
    return pltpu.CompilerParams(vmem_limit_bytes=VMEM_LIMIT_BYTES)


def _tile(n, cap):
    best = 0
    for d in range(LANE, min(n, cap) + 1, LANE):
        if n % d == 0:
            best = d
    return best or n


def _matmul_tiles(M, N, K, a_bytes, b_bytes, o_bytes, has_add, forced):
    tm = forced.get("m") or _tile(M, 1024)
    tn = forced.get("n") or _tile(N, 1024)
    tk = forced.get("k") or _tile(K, 2048)

    def vmem(tm, tn, tk):
        acc = tm * tn * 4 if tk < K else 0
        return 2 * (tm * tk * a_bytes + tk * tn * b_bytes + tm * tn * (o_bytes + (4 if has_add else 0))) + acc

    while vmem(tm, tn, tk) > (VMEM_LIMIT_BYTES * 3) // 4:
        if "k" not in forced and tk > 512 and _tile(K, tk // 2) < tk:
            tk = _tile(K, tk // 2)
        elif "m" not in forced and _tile(M, tm // 2) < tm:
            tm = _tile(M, tm // 2)
        else:
            break
    return tm, tn, tk


def _matmul(a, b, *, mode, out_dtype=F32, name, add=None, deps=(), out_blocks=0, epi=None, b2=None):
    def view(x):
        return (x.shape[1], x.shape[0] * x.shape[2], x.shape[2]) if x.ndim == 3 else (x.shape[0], x.shape[1], 0)

    (ar, ac, aw), (br, bc, bw) = view(a), view(b)
    a_col, b_col = {"nn": ("k", "n"), "nt": ("k", "k"), "tn": ("m", "n")}[mode]
    if mode == "nn":
        M, K, K2, N = ar, ac, br, bc
    elif mode == "nt":
        M, K, N, K2 = ar, ac, br, bc
    else:
        K, M, K2, N = ar, ac, br, bc
    assert K == K2, (a.shape, b.shape, mode)
    forced = {}
    for dim, w in ((a_col, aw), (b_col, bw), ("n", N // out_blocks if out_blocks else 0)):
        if w:
            assert forced.get(dim, w) == w
            forced[dim] = w
    has_add = add is not None
    tile_bytes = (sum(jnp.dtype(d).itemsize for d in epi[2]) + sum((e[0] if isinstance(e, tuple) else e).dtype.itemsize for e in epi[1])
                  if epi is not None else jnp.dtype(out_dtype).itemsize)
    tm, tn, tk = _matmul_tiles(M, N, K, a.dtype.itemsize, b.dtype.itemsize, tile_bytes, has_add, forced)
    kb = 1
    if "k" in forced and mode != "tn":
        lanes_ok = all(w or tk % LANE == 0 for w in (aw, bw if mode == "nt" else 1))
        kb = next(c for c in (4, 2, 1) if (K // tk) % c == 0 and (c == 1 or (lanes_ok and c * tk <= 1536)))
    nk = K // (tk * kb)
    dn = {"nn": (((1,), (0,)), ((), ())), "nt": (((1,), (1,)), ((), ())), "tn": (((0,), (0,)), ((), ()))}[mode]
    pick = {"m": lambda i, j, k: i, "n": lambda i, j, k: j, "k": lambda i, j, k: k}
    size = {"m": tm, "n": tn, "k": tk}

    def spec(blocked, row_dim, col_dim):
        rf, cf = pick[row_dim], pick[col_dim]
        reps = {d: (kb if d == "k" else 1) for d in (row_dim, col_dim)}
        if blocked:
            lead = kb if col_dim == "k" and kb > 1 else None
            return pl.BlockSpec((lead, size[row_dim], size[col_dim]), lambda i, j, k: (cf(i, j, k), rf(i, j, k), 0))
        return pl.BlockSpec((size[row_dim] * reps[row_dim], size[col_dim] * reps[col_dim]), lambda i, j, k: (rf(i, j, k), cf(i, j, k)))

    def k_part(ref, blocked, k_on_rows, j):
        if kb == 1:
            return ref[...]
        if blocked:
            return ref[j]
        return ref[j * tk:(j + 1) * tk, :] if k_on_rows else ref[:, j * tk:(j + 1) * tk]

    a_spec = spec(aw, "k" if mode == "tn" else "m", a_col)
    b_spec = spec(bw, "n" if mode == "nt" else "k", b_col)
    o_spec = spec(out_blocks, "m", "n")
    epi_fn, epi_ins, epi_dtypes = epi if epi is not None else (None, [], [out_dtype])
    epi_ins = [e if isinstance(e, tuple) else (e, None) for e in epi_ins]
    n_epi = len(epi_ins)
    twin = b2 is not None
    assert not twin or (nk == 1 and kb == 1 and epi is not None and b2.shape == b.shape)
    n_in = 2 + twin + has_add + n_epi + len(deps)
    n_out = len(epi_dtypes)

    def body(*refs):
        a_ref, b_ref = refs[0], refs[1]
        add_ref = refs[2 + twin] if has_add else None
        epi_refs = refs[2 + twin + has_add:2 + twin + has_add + n_epi]
        o_refs = refs[n_in:n_in + n_out]
        part = None
        for q in range(kb):
            a_q = k_part(a_ref, aw and a_col == "k", False, q)
            b_q = k_part(b_ref, bw and b_col == "k", mode == "nn", q)
            prod = lax.dot_general(a_q.astype(BF16), b_q.astype(BF16), dn, preferred_element_type=F32)
            part = prod if part is None else part + prod
        second = [lax.dot_general(a_ref[...].astype(BF16), refs[2][...].astype(BF16), dn, preferred_element_type=F32)] if twin else []

        def finish(res):
            outs = epi_fn(res, *second, *[e[...] for e in epi_refs]) if epi_fn is not None else (res,)
            for o_ref, val in zip(o_refs, outs):
                o_ref[...] = val.astype(o_ref.dtype)

        if nk == 1:
            finish(part + add_ref[...] if has_add else part)
            return
        acc_ref = refs[-1]
        kk = pl.program_id(2)

        @pl.when(kk == 0)
        def _():
            acc_ref[...] = part + add_ref[...] if has_add else part

        @pl.when(kk > 0)
        def _():
            acc_ref[...] += part

        @pl.when(kk == nk - 1)
        def _():
            finish(acc_ref[...])

    def epi_spec(arr, off):
        if off is None:
            return o_spec
        assert off % tn == 0
        return pl.BlockSpec((tm, tn), lambda i, j, k: (i, j + off // tn))

    ins = [a, b] + ([b2] if twin else []) + ([add] if has_add else []) + [arr for arr, _ in epi_ins] + list(deps)
    in_specs = ([a_spec, b_spec] + ([b_spec] if twin else []) + ([o_spec] if has_add else []) + [epi_spec(arr, off) for arr, off in epi_ins]
                + [pl.BlockSpec(d.shape, lambda i, j, k, nd=d.ndim: (0,) * nd) for d in deps])
    o_shape = (out_blocks, M, tn) if out_blocks else (M, N)
    res = pl.pallas_call(
        body, name=name, grid=(M // tm, N // tn, nk), in_specs=in_specs, out_specs=[o_spec] * n_out,
        out_shape=[jax.ShapeDtypeStruct(o_shape, dt) for dt in epi_dtypes],
        scratch_shapes=[pltpu.VMEM((tm, tn), F32)] if nk > 1 else [],
        compiler_params=_params())(*ins)
    return res[0] if epi is None else list(res)


def _rowwise(fn, rows, pars, row_outs, acc_outs, *, name, tm=256, deps=()):
    rows = [r if isinstance(r, tuple) else (r, r.shape[1], 0) for r in rows]
    row_outs = [o if len(o) == 5 else (o[0], o[1], o[0], 0, None) for o in row_outs]
    aliased = [(k, o[4]) for k, o in enumerate(row_outs) if o[4] is not None]
    R = rows[0][0].shape[0]
    if max(w for _, w, _ in rows) > 4096:
        tm = tm // 2
    tm = min(tm, R)
    assert R % tm == 0
    nr, npar = len(rows), len(pars)
    nro = len(row_outs)
    n_in = nr + npar + len(deps) + len(aliased)

    def body(*refs):
        rv = [r[...] for r in refs[:nr]]
        pv = [p[...] for p in refs[nr:nr + npar]]
        outs = refs[n_in:]
        ro, ao = fn(rv, pv)
        first = pl.program_id(0) == 0
        for o_ref, val in zip(outs[:nro], ro):
            o_ref[...] = val.astype(o_ref.dtype)

        @pl.when(first)
        def _():
            for o_ref, val in zip(outs[nro:], ao):
                o_ref[...] = val

        @pl.when(jnp.logical_not(first))
        def _():
            for o_ref, val in zip(outs[nro:], ao):
                o_ref[...] += val

    in_specs = ([pl.BlockSpec((tm, w), lambda i, cb=cb: (i, cb)) for _, w, cb in rows]
                + [pl.BlockSpec(p.shape, lambda i, nd=p.ndim: (0,) * nd) for p in list(pars) + list(deps)]
                + [pl.BlockSpec(memory_space=pl.ANY)] * len(aliased))
    out_shape = ([jax.ShapeDtypeStruct((R, full), dt) for _, dt, full, _, _ in row_outs]
                 + [jax.ShapeDtypeStruct(s, F32) for s in acc_outs])
    out_specs = ([pl.BlockSpec((tm, f), lambda i, cb=cb: (i, cb)) for f, _, _, cb, _ in row_outs]
                 + [pl.BlockSpec(s, lambda i, nd=len(s): (0,) * nd) for s in acc_outs])
    res = pl.pallas_call(body, name=name, grid=(R // tm,), in_specs=in_specs, out_specs=out_specs, out_shape=out_shape,
                         input_output_aliases={n_in - len(aliased) + q: k for q, (k, _) in enumerate(aliased)},
                         compiler_params=_params())(*[r for r, _, _ in rows], *pars, *deps, *[buf for _, buf in aliased])
    return list(res)


def _bdot(a, b, mode="nn"):
    dn = {"nn": (((1,), (0,)), ((), ())), "nt": (((1,), (1,)), ((), ())), "tn": (((0,), (0,)), ((), ()))}[mode]
    return lax.dot_general(a.astype(BF16), b.astype(BF16), dn, preferred_element_type=F32)


def _sigmoid(x):
    return jax.nn.sigmoid(x)


def _softplus(x):
    return jnp.maximum(x, 0.0) + jnp.log1p(jnp.exp(-jnp.abs(x)))


def _gelu(z):
    return 0.5 * z * (1.0 + lax.erf(z * _SQRT_HALF))


def _gelu_grad(z):
    return 0.5 * (1.0 + lax.erf(z * _SQRT_HALF)) + z * jnp.exp(-0.5 * z * z) * _INV_SQRT_2PI


def _mean(x):
    return jnp.mean(x, axis=-1, keepdims=True)


def _colsum(x):
    return jnp.sum(x, axis=0, keepdims=True)


def _rms_fwd(x, g, name, deps=()):
    def fn(rv, pv):
        (xv,), (gv,) = rv, pv
        r = lax.rsqrt(_mean(xv * xv) + RMS_EPS)
        return [xv * r * gv], []
    return _rowwise(fn, [x], [g], [(x.shape[1], BF16)], [], name=name, deps=deps)[0]


def _rms_bwd(dn, x, dres, g, name, deps=()):
    def fn(rv, pv):
        (dnv, xv, drv), (gv,) = rv, pv
        r = lax.rsqrt(_mean(xv * xv) + RMS_EPS)
        yn = xv * r
        dyg = dnv * gv
        dx = drv + r * (dyg - yn * _mean(dyg * yn))
        return [dx, dx], [_colsum(dnv * yn)]
    D = x.shape[1]
    return _rowwise(fn, [dn, x, dres], [g], [(D, F32), (D, BF16)], [(1, D)], name=name, deps=deps)


def _rwkv_layout(RW, Lw, La, Lg, D):
    widths = [RW, RW, RW, Lw, La, Lg]
    pw = [_ceil_to(w, LANE) for w in widths]
    pw[5] += _ceil_to(sum(pw), 2 * D) - sum(pw)
    offs = [sum(pw[:i]) for i in range(6)]
    return widths, pw, offs, sum(pw)


def _pad_rwkv_cols(a, lay):
    widths, pw, _, _ = lay
    pieces, src = [], 0
    for w, p in zip(widths, pw):
        pieces.append(a[:, src:src + w])
        if p > w:
            pieces.append(jnp.zeros((a.shape[0], p - w), a.dtype))
        src += w
    return jnp.concatenate(pieces, axis=1)


def _unpad_rwkv_cols(a, lay):
    widths, _, offs, _ = lay
    return jnp.concatenate([a[:, o:o + w] for o, w in zip(offs, widths)], axis=1)


def _proj_pieces(lay, D, cs):
    widths, _, offs, rcp = lay
    rc = sum(widths)
    segs = [(sum(widths[:j]), widths[j], offs[j]) for j in range(6)] + [(rc, D, rcp + 2 * D), (rc + D, D, rcp), (rc + 2 * D, D, rcp + D)]
    pieces = []
    for start, width, dst in segs:
        n = start
        while n < start + width:
            d, off = divmod(n, cs)
            take = min(cs - off, start + width - n)
            pieces.append((d, off, dst + n - start, take))
            n += take
    return pieces


def _w_in_to_proj(g, lay, D, name):
    nb, rows, cs = g.shape
    icp = lay[3] + 3 * D
    pieces = _proj_pieces(lay, D, cs)
    tm = _pick(rows, (256, 128, 64, 32, 16))

    def body(i_ref, o_ref):
        o_ref[...] = jnp.zeros_like(o_ref)
        for d, src, dst, w in pieces:
            o_ref[:, dst:dst + w] = i_ref[d, :, src:src + w]

    return pl.pallas_call(
        body, name=name, grid=(rows // tm,), in_specs=[pl.BlockSpec((nb, tm, cs), lambda i: (0, i, 0))],
        out_specs=pl.BlockSpec((tm, icp), lambda i: (i, 0)), out_shape=jax.ShapeDtypeStruct((rows, icp), g.dtype),
        compiler_params=_params())(g)


def _dw_in_from_proj(a, lay, D, cs, name):
    rows, icp = a.shape
    pieces = _proj_pieces(lay, D, cs)
    tm = _pick(rows, (256, 128, 64, 32, 16))

    def body(i_ref, o_ref):
        for d, src, dst, w in pieces:
            o_ref[d, :, src:src + w] = i_ref[:, dst:dst + w]

    return pl.pallas_call(
        body, name=name, grid=(rows // tm,), in_specs=[pl.BlockSpec((tm, icp), lambda i: (i, 0))],
        out_specs=pl.BlockSpec((N_DEV, tm, cs), lambda i: (0, i, 0)), out_shape=jax.ShapeDtypeStruct((N_DEV, rows, cs), a.dtype),
        compiler_params=_params())(a)


def _pad_rows(a, rows):
    return a if a.shape[0] == rows else jnp.concatenate([a, jnp.zeros((rows - a.shape[0], a.shape[1]), a.dtype)], axis=0)


def _token_shift(p, halo, mu, i):
    tm = p.shape[0]
    hid = lax.broadcasted_iota(jnp.int32, (SUBLANE, 1), 0)
    before = jnp.sum(jnp.where(hid == SUBLANE - 1, halo, 0.0), axis=0, keepdims=True)
    before = jnp.where(i == 0, 0.0, before)
    rid = lax.broadcasted_iota(jnp.int32, (tm, 1), 0)
    prev = jnp.where(rid == 0, before, pltpu.roll(p, 1, 0))
    d = prev - p
    return p + d * mu, d


def _rwkv_math(ps, w0, a0, k_k, k_a, wlw, wla, wlg, lay):
    _, pw, offs, _ = lay
    r, k, v, xw, xa, xg = (ps[:, offs[j]:offs[j] + pw[j]] for j in range(6))
    tw = jnp.tanh(xw)
    ww = w0 + _bdot(tw, wlw)
    lw = -jnp.exp(-_softplus(-ww) - 0.5)
    a = _sigmoid(a0 + _bdot(xa, wla))
    sg = _sigmoid(xg)
    g = _bdot(sg, wlg)
    return dict(r=r, k=k, v=v, xa=xa, tw=tw, ww=ww, lw=lw, a=a, sg=sg, g=g, kkp=k * k_k, k2=k * (1.0 + (a - 1.0) * k_a))


def _halo_spec(tm, width):
    hb = tm // SUBLANE
    return pl.BlockSpec((SUBLANE, width), lambda i: (jnp.maximum(i * hb - 1, 0), 0))


def _rowsum(x):
    return jnp.sum(x, axis=-1, keepdims=True)


def _kk_math(kkp):
    nrm = jnp.sqrt(_rowsum(kkp * kkp))
    inv = 1.0 / jnp.maximum(nrm, 1e-12)
    return nrm, inv, kkp * inv


def _rwkv_pre(p, mu, small, lora, lay, name):
    T, rcp = p.shape[0], lay[3]
    H = lay[0][0] // HEAD
    tm = min(128, T)

    def body(p_ref, ph_ref, mu_ref, w0_ref, a0_ref, kk_ref, ka_ref, wlw_ref, wla_ref, wlg_ref, r_o, lw_o, k2_o, v_o, aa_o, bb_o, g_o):
        ps, _ = _token_shift(p_ref[...], ph_ref[...], mu_ref[...], pl.program_id(0))
        q = _rwkv_math(ps, w0_ref[...], a0_ref[...], kk_ref[...], ka_ref[...], wlw_ref[...], wla_ref[...], wlg_ref[...], lay)
        for h in range(H):
            sl = slice(h * HEAD, (h + 1) * HEAD)
            for o_ref, key in ((r_o, "r"), (lw_o, "lw"), (k2_o, "k2"), (v_o, "v"), (g_o, "g")):
                o_ref[h] = q[key][:, sl]
            _, _, kk = _kk_math(q["kkp"][:, sl])
            aa_o[h] = -kk
            bb_o[h] = kk * q["a"][:, sl]

    whole = lambda arr: pl.BlockSpec(arr.shape, lambda i: (0, 0))
    return pl.pallas_call(
        body, name=name, grid=(T // tm,),
        in_specs=([pl.BlockSpec((tm, rcp), lambda i: (i, 0)), _halo_spec(tm, rcp), whole(mu)]
                  + [whole(s) for s in small] + [whole(w) for w in lora]),
        out_specs=[pl.BlockSpec((H, tm, HEAD), lambda i: (0, i, 0))] * 7, out_shape=[jax.ShapeDtypeStruct((H, T, HEAD), F32)] * 7,
        compiler_params=_params())(p, p, mu, *small, *lora)


def _rwkv_pre_bwd(p, mu, small, lora, hgrads, dproj, lay, name):
    T, rcp = p.shape[0], lay[3]
    widths, pw, offs, _ = lay
    RW = widths[0]
    H = RW // HEAD
    tm = min(128, T)
    nt = T // tm
    hb = tm // SUBLANE

    def body(p_ref, ph_ref, mu_ref, w0_ref, a0_ref, kk_ref, ka_ref, wlw_ref, wla_ref, wlg_ref,
             dr_h, dk2_h, dv_h, dlw_h, daa, dbb, dg_h, buf_ref,
             dp_ref, dmu_ref, dw0_ref, da0_ref, dkk_ref, dka_ref, dwlw_ref, dwla_ref, dwlg_ref,
             s_dr, s_dk2, s_dv, s_dlw, s_dkkp, s_da, s_dg, dps_ref, next_ref):
        i = pl.program_id(0)
        ps, dprev = _token_shift(p_ref[...], ph_ref[...], mu_ref[...], nt - 1 - i)
        k_k, k_a = kk_ref[...], ka_ref[...]
        q = _rwkv_math(ps, w0_ref[...], a0_ref[...], k_k, k_a, wlw_ref[...], wla_ref[...], wlg_ref[...], lay)
        k, a, lw, ww, tw, sg = q["k"], q["a"], q["lw"], q["ww"], q["tw"], q["sg"]
        for h in range(H):
            sl = slice(h * HEAD, (h + 1) * HEAD)
            s_dr[:, sl] = dr_h[h]
            s_dk2[:, sl] = dk2_h[h]
            s_dv[:, sl] = dv_h[h]
            s_dlw[:, sl] = dlw_h[h]
            s_dg[:, sl] = dg_h[h]
            nrm, inv, kk = _kk_math(q["kkp"][:, sl])
            dbb_h = dbb[h]
            dkk = dbb_h * a[:, sl] - daa[h]
            s_dkkp[:, sl] = jnp.where(nrm > 1e-12, inv * (dkk - kk * _rowsum(dkk * kk)), dkk * inv)
            s_da[:, sl] = dbb_h * kk
        dk2, dkkp, dg = s_dk2[...], s_dkkp[...], s_dg[...]
        dk = dk2 * (1.0 + (a - 1.0) * k_a) + dkkp * k_k
        da = s_da[...] + dk2 * k * k_a
        dpa = da * a * (1.0 - a)
        dww = s_dlw[...] * lw * _sigmoid(-ww)
        dxa = _bdot(dpa, wla_ref[...], "nt")
        dxw = _bdot(dww, wlw_ref[...], "nt") * (1.0 - tw * tw)
        dxg = _bdot(dg, wlg_ref[...], "nt") * sg * (1.0 - sg)
        segs = (s_dr[...], dk, s_dv[...], dxw, dxa, dxg)
        sums = [dmu_ref, dw0_ref, da0_ref, dkk_ref, dka_ref, dwlw_ref, dwla_ref, dwlg_ref]

        @pl.when(i == 0)
        def _():
            for s in sums + [next_ref]:
                s[...] = jnp.zeros_like(s)

        for j, seg in enumerate(segs):
            sl = slice(offs[j], offs[j] + pw[j])
            dps_ref[:, sl] = seg
            dmu_ref[:, sl] += _colsum(seg * dprev[:, sl])
        dw0_ref[...] += _colsum(dww)
        da0_ref[...] += _colsum(dpa)
        dkk_ref[...] += _colsum(dkkp * k)
        dka_ref[...] += _colsum(dk2 * k * (a - 1.0))
        dwlw_ref[...] += _bdot(tw, dww, "tn")
        dwla_ref[...] += _bdot(q["xa"], dpa, "tn")
        dwlg_ref[...] += _bdot(sg, dg, "tn")
        dps = dps_ref[...]
        rid = lax.broadcasted_iota(jnp.int32, (tm, 1), 0)
        nxt = jnp.where(rid == tm - 1, next_ref[...], pltpu.roll(dps, tm - 1, 0))
        mu_v = mu_ref[...]
        dp_ref[...] = (dps * (1.0 - mu_v) + nxt * mu_v).astype(BF16)
        next_ref[...] = _colsum(jnp.where(rid == 0, dps, 0.0))

    whole = lambda arr: pl.BlockSpec(arr.shape, lambda i: (0, 0))
    row = lambda w: pl.BlockSpec((tm, w), lambda i: (nt - 1 - i, 0))
    acc_shapes = [(1, rcp), (1, RW), (1, RW), (1, RW), (1, RW)] + [w.shape for w in lora]
    return pl.pallas_call(
        body, name=name, grid=(nt,),
        in_specs=([row(rcp), pl.BlockSpec((SUBLANE, rcp), lambda i: (jnp.maximum((nt - 1 - i) * hb - 1, 0), 0)), whole(mu)]
                  + [whole(s) for s in small] + [whole(w) for w in lora]
                  + [pl.BlockSpec((H, tm, HEAD), lambda i: (0, nt - 1 - i, 0))] * 7 + [pl.BlockSpec(memory_space=pl.ANY)]),
        out_specs=[row(rcp)] + [pl.BlockSpec(s, lambda i: (0, 0)) for s in acc_shapes],
        out_shape=[jax.ShapeDtypeStruct(dproj.shape, BF16)] + [jax.ShapeDtypeStruct(s, F32) for s in acc_shapes],
        scratch_shapes=[pltpu.VMEM((tm, RW), F32)] * 7 + [pltpu.VMEM((tm, rcp), F32), pltpu.VMEM((1, rcp), F32)],
        input_output_aliases={10 + 7: 0}, compiler_params=_params())(p, p, mu, *small, *lora, *hgrads, dproj)


def _head_post_math(y, r, k2, v, lg, lb, rk):
    yc = y - _mean(y)
    rstd = lax.rsqrt(_mean(yc * yc) + LNX_EPS)
    yn = yc * rstd
    s = _rowsum(r * k2 * rk)
    return yn, rstd, yn * lg + lb + s * v, s


def _head_post(y, r, k2, v, g, hp, name, deps=()):
    H, T, _ = y.shape
    tm = min(128, T)

    def body(y_ref, r_ref, k_ref, v_ref, g_ref, lg_ref, lb_ref, rk_ref, *rest):
        o_ref = rest[-1]
        _, _, t, _ = _head_post_math(y_ref[...], r_ref[...], k_ref[...], v_ref[...], lg_ref[...], lb_ref[...], rk_ref[...])
        out = (t * g_ref[...]).astype(BF16)
        for h in range(H):
            o_ref[:, h * HEAD:(h + 1) * HEAD] = out[h]

    blk = pl.BlockSpec((H, tm, HEAD), lambda i: (0, i, 0))
    par = pl.BlockSpec((H, 1, HEAD), lambda i: (0, 0, 0))
    return pl.pallas_call(
        body, name=name, grid=(T // tm,),
        in_specs=[blk] * 5 + [par] * 3 + [pl.BlockSpec(d.shape, lambda i, nd=d.ndim: (0,) * nd) for d in deps],
        out_specs=pl.BlockSpec((tm, H * HEAD), lambda i: (i, 0)),
        out_shape=jax.ShapeDtypeStruct((T, H * HEAD), BF16), compiler_params=_params())(y, r, k2, v, g, *hp, *deps)


def _bmm(x, y, mode):
    dn = {"nn": (((2,), (1,)), ((0,), (0,))), "nt": (((2,), (2,)), ((0,), (0,))), "tn": (((1,), (1,)), ((0,), (0,)))}[mode]
    (xh, xl), (yh, yl) = _split(x), _split(y)
    dot = lambda p, q: lax.dot_general(p, q, dn, preferred_element_type=F32)
    out = dot(xh, yh)
    if yl is not None:
        out = out + dot(xh, yl)
    if xl is not None:
        out = out + dot(xl, yh)
    return out


def _split(x):
    if isinstance(x, tuple):
        return x
    hi = x.astype(BF16)
    return hi, (x - hi.astype(F32)).astype(BF16)


def _exact(x):
    return x.astype(BF16), None


def _round(x):
    return x if isinstance(x, tuple) else (x.astype(BF16), None)


def _rows(*xs):
    if isinstance(xs[0], tuple):
        return tuple(None if any(p is None for p in parts) else jnp.concatenate(parts, axis=1) for parts in zip(*xs))
    return jnp.concatenate(xs, axis=1)


def _wkv_chunk(r, lw, k, v, a, b, inverse=None):
    hb, C, _ = r.shape
    ti = lax.broadcasted_iota(jnp.int32, (C, C), 0)
    si = lax.broadcasted_iota(jnp.int32, (C, C), 1)
    linc, lstr, eye = (ti >= si).astype(F32), (ti > si).astype(F32), (ti == si).astype(F32)
    qmask = jnp.concatenate([jnp.concatenate([lstr, lstr], axis=1), jnp.concatenate([linc, linc], axis=1)], axis=0)
    lincb = _exact(jnp.broadcast_to(linc, (hb, C, C)))
    both = _exact(jnp.broadcast_to(jnp.concatenate([linc, lstr], axis=0), (hb, 2 * C, C)))
    ones = _exact(jnp.ones_like(v))
    lws = _split(lw)
    ci = _bmm(lincb, lws, "nn")
    cC = jnp.sum(lw, axis=1, keepdims=True)
    gi, ge, gn, gr = jnp.exp(ci), jnp.exp(ci - lw), jnp.exp(-ci), jnp.exp(cC - ci)
    q = dict(At=a * ge, Rt=r * gi, Bt=b * gn, Kt=k * gn, Bh=b * gr, Kh=k * gr)
    s = dict(AR=_round(_rows(q["At"], q["Rt"])), BK=_round(_rows(q["Bt"], q["Kt"])), BKh=_round(_rows(q["Bh"], q["Kh"])), v=_round(v))
    quad = _bmm(s["AR"], s["BK"], "nt") * qmask
    s["top"], s["bot"] = _round(quad[:, :C]), _round(quad[:, C:])
    if inverse is None:
        A_ab = quad[:, :C, :C]
        Tm = eye + A_ab
        Pw = _round(A_ab)
        n = 1
        while 2 * n < C:
            Pw = _round(_bmm(Pw, Pw, "nn"))
            Tm = Tm + _bmm(_round(Tm), Pw, "nn")
            n *= 2
        inverse = Tm
    s["Tm"] = _round(inverse)
    gC = jnp.exp(_bmm(lws, ones, "tn"))
    q.update(gi=gi, ge=ge, gn=gn, gr=gr, qmask=qmask, both=both, gC=gC, ones=ones, s=s)
    return q


def _wkv_u(s, H0s, C):
    arh = _bmm(s["AR"], H0s, "nn")
    zv = _rows(tuple(None if p is None else jnp.zeros_like(p) for p in s["v"]), s["v"])
    U = _bmm(s["Tm"], _round(arh[:, :C] + _bmm(s["top"], zv, "nn")), "nn")
    return arh, _rows(_round(U), s["v"])


def _wkv_fwd(r, lw, k, v, a, b, name):
    H, T, N = r.shape
    C = min(WKV_CHUNK, T)
    nc = T // C
    hb = _pick(H, (16, 8, 4, 2))

    def body(r_ref, lw_ref, k_ref, v_ref, a_ref, b_ref, y_ref, st_ref, inv_ref, u_ref, h_ref):
        @pl.when(pl.program_id(1) == 0)
        def _():
            h_ref[...] = jnp.zeros_like(h_ref)

        H0 = h_ref[...]
        st_ref[0] = H0
        q = _wkv_chunk(r_ref[...], lw_ref[...], k_ref[...], v_ref[...], a_ref[...], b_ref[...])
        s = q["s"]
        arh, UV = _wkv_u(s, _round(H0), C)
        inv_ref[0] = s["Tm"][0]
        u_ref[...] = UV[0][:, :C]
        y_ref[...] = arh[:, C:] + _bmm(s["bot"], UV, "nn")
        h_ref[...] = q["gC"] * H0 + _bmm(s["BKh"], UV, "tn")

    blk = pl.BlockSpec((hb, C, N), lambda h, c: (h, c, 0))
    per_chunk = lambda w: pl.BlockSpec((1, hb, w, w), lambda h, c: (c, h, 0, 0))
    return pl.pallas_call(
        body, name=name, grid=(H // hb, nc), in_specs=[blk] * 6, out_specs=[blk, per_chunk(N), per_chunk(C), blk],
        out_shape=[jax.ShapeDtypeStruct((H, T, N), F32), jax.ShapeDtypeStruct((nc, H, N, N), F32),
                   jax.ShapeDtypeStruct((nc, H, C, C), BF16), jax.ShapeDtypeStruct((H, T, N), BF16)],
        scratch_shapes=[pltpu.VMEM((hb, N, N), F32)], compiler_params=_params())(r, lw, k, v, a, b)


def _wkv_bwd(r, lw, k, v, a, b, states, inverses, u, y, g, hp, dya, name, deps=()):
    H, T, N = r.shape
    C = min(WKV_CHUNK, T)
    nc = T // C
    hb = _pick(H, (16, 8, 4, 2))
    hsum = lambda t: jnp.sum(t, axis=1, keepdims=True)

    def body(r_ref, lw_ref, k_ref, v_ref, a_ref, b_ref, st_ref, inv_ref, u_ref, y_ref, g_ref, lg_ref, lb_ref, rk_ref, dya_ref, *rest):
        (dr_ref, dlw_ref, dk_ref, dv_ref, da_ref, db_ref, dg_ref, dlg_ref, dlb_ref, drk_ref, dh_ref, d_s) = rest[len(deps):]
        first = pl.program_id(1) == 0

        @pl.when(first)
        def _():
            dh_ref[...] = jnp.zeros_like(dh_ref)

        for h in range(hb):
            d_s[h] = dya_ref[:, h * N:(h + 1) * N]
        d_v, r_v, k_v, v_v, lg, rk = d_s[...], r_ref[...], k_ref[...], v_ref[...], lg_ref[...], rk_ref[...]
        yn, rstd, t, bonus = _head_post_math(y_ref[...], r_v, k_v, v_v, lg, lb_ref[...], rk)
        dyo = d_v * g_ref[...]
        dyn = dyo * lg
        ds = _rowsum(dyo * v_v)
        dy = rstd * (dyn - _mean(dyn) - yn * _mean(dyn * yn))
        dg_ref[...] = d_v * t
        sums = (hsum(dyo * yn), hsum(dyo), hsum(ds * r_v * k_v))

        @pl.when(first)
        def _():
            for o_ref, val in zip((dlg_ref, dlb_ref, drk_ref), sums):
                o_ref[...] = val

        @pl.when(jnp.logical_not(first))
        def _():
            for o_ref, val in zip((dlg_ref, dlb_ref, drk_ref), sums):
                o_ref[...] += val

        dHC = dh_ref[...]
        H0 = st_ref[0]
        q = _wkv_chunk(r_v, lw_ref[...], k_v, v_v, a_ref[...], b_ref[...], inverse=inv_ref[0])
        s, gC = q["s"], q["gC"]
        H0s, dHs, dY = _round(H0), _round(dHC), _round(dy)
        UV = _rows(_round(u_ref[...]), s["v"])
        bot_dy = _bmm(s["bot"], dY, "tn")
        bkh_dh = _bmm(s["BKh"], dHs, "nn")
        dP = _round(_bmm(s["Tm"], _round(bot_dy[:, :C] + bkh_dh[:, :C]), "tn"))
        dv_ref[...] = bot_dy[:, C:] + bkh_dh[:, C:] + _bmm(s["top"], dP, "tn")[:, C:] + dyo * bonus
        dPY = _rows(dP, dY)
        dh_ref[...] = gC * dHC + _bmm(s["AR"], dPY, "tn")
        dquad = _round(_bmm(dPY, UV, "nt") * q["qmask"])
        dAR = _bmm(dPY, H0s, "nt") + _bmm(dquad, s["BK"], "nn")
        dBK = _bmm(dquad, s["AR"], "tn")
        dBKh = _bmm(UV, dHs, "nt")
        dAt, dRt, dBt, dKt, dBh, dKh = dAR[:, :C], dAR[:, C:], dBK[:, :C], dBK[:, C:], dBKh[:, :C], dBKh[:, C:]
        dr_ref[...] = dRt * q["gi"] + ds * k_v * rk
        da_ref[...] = dAt * q["ge"]
        db_ref[...] = dBt * q["gn"] + dBh * q["gr"]
        dk_ref[...] = dKt * q["gn"] + dKh * q["gr"] + ds * r_v * rk
        tail = dBh * q["Bh"] + dKh * q["Kh"]
        dci = dRt * q["Rt"] - dBt * q["Bt"] - dKt * q["Kt"] - tail
        dcC = jnp.sum(tail, axis=1, keepdims=True) + _bmm(q["ones"], H0 * dHC * gC, "nt")
        dlw_ref[...] = _bmm(q["both"], _rows(dci, dAt * q["At"]), "tn") + dcC

    blk = pl.BlockSpec((hb, C, N), lambda h, c: (h, nc - 1 - c, 0))
    per_chunk = lambda w: pl.BlockSpec((1, hb, w, w), lambda h, c: (nc - 1 - c, h, 0, 0))
    par = pl.BlockSpec((hb, 1, N), lambda h, c: (h, 0, 0))
    return pl.pallas_call(
        body, name=name, grid=(H // hb, nc),
        in_specs=([blk] * 6 + [per_chunk(N), per_chunk(C), blk, blk, blk] + [par] * 3
                  + [pl.BlockSpec((C, hb * N), lambda h, c: (nc - 1 - c, h))]
                  + [pl.BlockSpec(d.shape, lambda h, c, nd=d.ndim: (0,) * nd) for d in deps]),
        out_specs=[blk] * 7 + [par] * 3,
        out_shape=[jax.ShapeDtypeStruct((H, T, N), F32)] * 7 + [jax.ShapeDtypeStruct((H, 1, N), F32)] * 3,
        scratch_shapes=[pltpu.VMEM((hb, N, N), F32), pltpu.VMEM((hb, C, N), F32)],
        compiler_params=_params())(r, lw, k, v, a, b, states, inverses, u, y, g, *hp, dya, *deps)


def _sgu_ln(z, SW, lng, lnb):
    ge = _gelu(z)
    u, vv = ge[:, :SW], ge[:, SW:]
    xc = vv - _mean(vv)
    rstd = lax.rsqrt(_mean(xc * xc) + LN_EPS)
    vn = xc * rstd
    return u, vn, rstd, vn * lng + lnb


def _causal(ws_ref, g):
    ti = lax.broadcasted_iota(jnp.int32, (SGU_CHUNK, SGU_CHUNK), 0)
    si = lax.broadcasted_iota(jnp.int32, (SGU_CHUNK, SGU_CHUNK), 1)
    return ti >= si, jnp.where(ti >= si, ws_ref[g], 0.0).astype(BF16)


def _sgu_fwd(proj, zblock, lng, lnb, ws, bexp, name):
    T, SW = proj.shape[0], lng.shape[1]
    G = ws.shape[0]
    tr = min(256, T)
    nch = tr // SGU_CHUNK

    def body(z_ref, lng_ref, lnb_ref, ws_ref, be_ref, o_ref):
        u, _, _, vl = _sgu_ln(z_ref[...], SW, lng_ref[...], lnb_ref[...])
        for g in range(G):
            cs = slice(g * SGU_GROUP, (g + 1) * SGU_GROUP)
            _, wc = _causal(ws_ref, g)
            for n in range(nch):
                rs = slice(n * SGU_CHUNK, (n + 1) * SGU_CHUNK)
                m = jnp.dot(wc, vl[rs, cs].astype(BF16), preferred_element_type=F32) + be_ref[:, cs]
                o_ref[rs, cs] = (u[rs, cs] * m).astype(BF16)

    whole = lambda arr: pl.BlockSpec(arr.shape, lambda i, nd=arr.ndim: (0,) * nd)
    return pl.pallas_call(
        body, name=name, grid=(T // tr,),
        in_specs=[pl.BlockSpec((tr, 2 * SW), lambda i: (i, zblock)), whole(lng), whole(lnb), whole(ws), whole(bexp)],
        out_specs=pl.BlockSpec((tr, SW), lambda i: (i, 0)), out_shape=jax.ShapeDtypeStruct((T, SW), BF16),
        compiler_params=_params())(proj, lng, lnb, ws, bexp)


def _sgu_bwd(proj, zblock, dyb, lng, lnb, ws, bexp, dproj, name):
    T, SW = proj.shape[0], lng.shape[1]
    G = ws.shape[0]
    tr = min(256, T)
    nch = tr // SGU_CHUNK
    nt = T // tr

    def body(z_ref, dy_ref, lng_ref, lnb_ref, ws_ref, be_ref, buf_ref, dz_ref, dlg_ref, dlb_ref, dws_ref, db_ref, du_s, dvl_s, dbacc_s):
        i = pl.program_id(0)
        zv = z_ref[...]
        lng_v = lng_ref[...]
        u, vn, rstd, vl = _sgu_ln(zv, SW, lng_v, lnb_ref[...])

        @pl.when(i == 0)
        def _():
            for s in (dlg_ref, dlb_ref, dws_ref, dbacc_s):
                s[...] = jnp.zeros_like(s)

        for g in range(G):
            cs = slice(g * SGU_GROUP, (g + 1) * SGU_GROUP)
            tri, wc = _causal(ws_ref, g)
            for n in range(nch):
                rs = slice(n * SGU_CHUNK, (n + 1) * SGU_CHUNK)
                blk = vl[rs, cs].astype(BF16)
                m = jnp.dot(wc, blk, preferred_element_type=F32) + be_ref[:, cs]
                dyv = dy_ref[rs, cs]
                du_s[rs, cs] = dyv * m
                dm = dyv * u[rs, cs]
                dvl_s[rs, cs] = _bdot(wc, dm, "tn")
                dws_ref[g] += jnp.where(tri, _bdot(dm, blk, "nt"), 0.0)
                dbacc_s[:, cs] += dm

        dvl = dvl_s[...]
        dlg_ref[...] += _colsum(dvl * vn)
        dlb_ref[...] += _colsum(dvl)
        dvn = dvl * lng_v
        dvv = rstd * (dvn - _mean(dvn) - vn * _mean(dvn * vn))
        gp = _gelu_grad(zv)
        dz_ref[:, :SW] = (du_s[...] * gp[:, :SW]).astype(BF16)
        dz_ref[:, SW:] = (dvv * gp[:, SW:]).astype(BF16)

        @pl.when(i == nt - 1)
        def _():
            lane = lax.broadcasted_iota(jnp.int32, (SGU_CHUNK, LANE), 1)
            out = jnp.zeros((SGU_CHUNK, LANE), F32)
            for g in range(G):
                col = jnp.sum(dbacc_s[:, g * SGU_GROUP:(g + 1) * SGU_GROUP], axis=1, keepdims=True)
                out = jnp.where(lane == g, col, out)
            db_ref[...] = out

    whole = lambda arr: pl.BlockSpec(arr.shape, lambda i, nd=arr.ndim: (0,) * nd)
    acc_shapes = [(1, SW), (1, SW), ws.shape, (SGU_CHUNK, LANE)]
    return pl.pallas_call(
        body, name=name, grid=(nt,),
        in_specs=[pl.BlockSpec((tr, 2 * SW), lambda i: (i, zblock)), pl.BlockSpec((tr, SW), lambda i: (i, 0)),
                  whole(lng), whole(lnb), whole(ws), whole(bexp), pl.BlockSpec(memory_space=pl.ANY)],
        out_specs=([pl.BlockSpec((tr, 2 * SW), lambda i: (i, zblock))]
                   + [pl.BlockSpec(s, lambda i, nd=len(s): (0,) * nd) for s in acc_shapes]),
        out_shape=[jax.ShapeDtypeStruct(dproj.shape, BF16)] + [jax.ShapeDtypeStruct(s, F32) for s in acc_shapes],
        scratch_shapes=[pltpu.VMEM((tr, SW), F32), pltpu.VMEM((tr, SW), F32), pltpu.VMEM((SGU_CHUNK, SW), F32)],
        input_output_aliases={6: 0}, compiler_params=_params())(proj, dyb, lng, lnb, ws, bexp, dproj)


_HBM = pl.BlockSpec(memory_space=pltpu.HBM)
_SEM = pl.BlockSpec(memory_space=pltpu.SEMAPHORE)
_DATAFLOW = pltpu.SideEffectType.DATAFLOW_SIDE_EFFECTING


def _mesh_place(chips=False):
    x, y, c = lax.axis_index("x"), lax.axis_index("y"), lax.axis_index("c")
    return x, y, c, (2 * x + y if chips else 4 * x + 2 * y + c)


def _peer(x, y, c, rel, chips=False):
    px = 1 - x if rel & 4 else x
    py = 1 - y if rel & 2 else y
    pc = 1 - c if rel & 1 else c
    return (px, py, pc), (2 * px + py if chips else 4 * px + 2 * py + pc)


ALL_PEERS = tuple(range(1, N_DEV))
SIBLING = (1,)
SAME_CORE = (2, 4, 6)
SIBLINGS_CORE = (3, 5, 7)


def _exchange_start(groups, name, rels=ALL_PEERS, chips=False):
    flat = [t for g in groups for t in g]
    sizes = [len(g) for g in groups]
    n, ng = len(flat), len(groups)
    srcs = [pltpu.with_memory_space_constraint(a, pltpu.HBM) for a, _ in flat]
    lands = [pltpu.with_memory_space_constraint(lax.empty(((N_DEV,) + a.shape) if isg else a.shape, a.dtype), pltpu.HBM)
             for a, isg in flat]

    def body(*refs):
        ins, lnd, sems, token = refs[:n], refs[n:2 * n], refs[2 * n:2 * n + 3 * ng], refs[-1]
        x, y, c, me = _mesh_place(chips)
        j0 = 0
        for gi, sz in enumerate(sizes):
            for rel in rels:
                dev, slot = _peer(x, y, c, rel, chips)
                for jj in range(sz):
                    j = j0 + jj
                    pltpu.make_async_remote_copy(
                        src_ref=ins[j] if flat[j][1] else ins[j].at[slot], dst_ref=lnd[j].at[me],
                        send_sem=sems[3 * gi].at[jj * (N_DEV - 1) + rel - 1], recv_sem=sems[3 * gi + 1].at[jj * (N_DEV - 1) + rel - 1],
                        device_id=dev, device_id_type=pl.DeviceIdType.MESH).start()
            for jj in range(sz):
                j = j0 + jj
                pltpu.make_async_copy(ins[j] if flat[j][1] else ins[j].at[me], lnd[j].at[me], sems[3 * gi + 2].at[jj]).start()
            j0 += sz
        token[...] = jnp.zeros_like(token)

    sem_shapes = [pltpu.SemaphoreType.DMA((k,)) for sz in sizes for k in (sz * (N_DEV - 1), sz * (N_DEV - 1), sz)]
    res = pl.pallas_call(
        body, name=name,
        out_shape=(*sem_shapes, *[pltpu.HBM(a.shape, a.dtype) for a in srcs], *[pltpu.HBM(a.shape, a.dtype) for a in lands],
                   jax.ShapeDtypeStruct((SUBLANE, LANE), F32)),
        in_specs=[_HBM] * (2 * n), out_specs=(*[_SEM] * (3 * ng), *[_HBM] * (2 * n), pl.BlockSpec(memory_space=pltpu.VMEM)),
        input_output_aliases={i: 3 * ng + i for i in range(2 * n)},
        compiler_params=pltpu.CompilerParams(has_side_effects=_DATAFLOW))(*srcs, *lands)
    sems, thru, token = res[:3 * ng], res[3 * ng:3 * ng + 2 * n], res[-1]
    handle, j0 = [], 0
    for gi, sz in enumerate(sizes):
        handle.append(dict(kinds=[k for _, k in groups[gi]], chips=chips, srcs=list(thru[j0:j0 + sz]), lands=list(thru[n + j0:n + j0 + sz]),
                           sems=list(sems[3 * gi:3 * gi + 3])))
        j0 += sz
    return handle, token


def _exchange_wait(group, after, name, rels=ALL_PEERS, local=True):
    kinds, sz = group["kinds"], len(group["kinds"])
    relay = group.get("relay", [])

    def body(*refs):
        ins, lnd, (ssem, rsem, lsem) = refs[:sz], refs[sz:2 * sz], refs[2 * sz:2 * sz + 3]
        x, y, c, me = _mesh_place(group["chips"])
        for rel in rels:
            dev, slot = _peer(x, y, c, rel, group["chips"])
            for jj in range(sz):
                cp = pltpu.make_async_remote_copy(
                    src_ref=ins[jj] if kinds[jj] else ins[jj].at[slot], dst_ref=lnd[jj].at[slot],
                    send_sem=ssem.at[jj * (N_DEV - 1) + rel - 1], recv_sem=rsem.at[jj * (N_DEV - 1) + rel - 1],
                    device_id=dev, device_id_type=pl.DeviceIdType.MESH)
                cp.wait_send()
                cp.wait_recv()
        if local:
            for jj in range(sz):
                pltpu.make_async_copy(ins[jj] if kinds[jj] else ins[jj].at[me], lnd[jj].at[me], lsem.at[jj]).wait()
        if relay:
            fsend, frecv = refs[2 * sz + 3:2 * sz + 5]
            dev = _peer(x, y, c, 1)[0]
            for q, (mine, theirs) in enumerate(zip(SAME_CORE, SIBLINGS_CORE)):
                for jj in range(sz):
                    cp = pltpu.make_async_remote_copy(
                        src_ref=lnd[jj].at[_peer(x, y, c, mine)[1]], dst_ref=lnd[jj].at[_peer(x, y, c, theirs)[1]],
                        send_sem=fsend.at[jj * len(SAME_CORE) + q], recv_sem=frecv.at[jj * len(SAME_CORE) + q],
                        device_id=dev, device_id_type=pl.DeviceIdType.MESH)
                    cp.wait_send()
                    cp.wait_recv()

    arrays = group["srcs"] + group["lands"]
    sems = group["sems"] + relay
    res = pl.pallas_call(
        body, name=name, out_shape=[pltpu.HBM(a.shape, a.dtype) for a in arrays],
        in_specs=[_HBM] * (2 * sz) + [_SEM] * len(sems) + [pl.BlockSpec(memory_space=pl.ANY)], out_specs=[_HBM] * (2 * sz),
        input_output_aliases={i: i for i in range(2 * sz)},
        compiler_params=pltpu.CompilerParams(has_side_effects=_DATAFLOW))(*arrays, *sems, after)
    return dict(group, srcs=list(res[:sz]), lands=list(res[sz:]), relay=[])


def _relay_start(group, name):
    sz = len(group["kinds"])
    nq = len(SAME_CORE)

    def body(*refs):
        lnd, fsend, frecv, token = refs[:sz], refs[sz], refs[sz + 1], refs[-1]
        x, y, c, _ = _mesh_place()
        dev = _peer(x, y, c, 1)[0]
        for q, rel in enumerate(SAME_CORE):
            slot = _peer(x, y, c, rel)[1]
            for jj in range(sz):
                pltpu.make_async_remote_copy(
                    src_ref=lnd[jj].at[slot], dst_ref=lnd[jj].at[slot], send_sem=fsend.at[jj * nq + q], recv_sem=frecv.at[jj * nq + q],
                    device_id=dev, device_id_type=pl.DeviceIdType.MESH).start()
        token[...] = jnp.zeros_like(token)

    lands = group["lands"]
    res = pl.pallas_call(
        body, name=name,
        out_shape=(pltpu.SemaphoreType.DMA((sz * nq,)), pltpu.SemaphoreType.DMA((sz * nq,)), *[pltpu.HBM(a.shape, a.dtype) for a in lands],
                   jax.ShapeDtypeStruct((SUBLANE, LANE), F32)),
        in_specs=[_HBM] * sz, out_specs=(_SEM, _SEM, *[_HBM] * sz, pl.BlockSpec(memory_space=pltpu.VMEM)),
        input_output_aliases={i: 2 + i for i in range(sz)},
        compiler_params=pltpu.CompilerParams(has_side_effects=_DATAFLOW))(*lands)
    return dict(group, lands=list(res[2:2 + sz]), relay=[res[0], res[1]]), res[-1]


def _sibling_swap(arrays, handle, after, name):
    start = handle is None
    n = len(arrays) if start else len(handle["srcs"])
    chips = N_DEV // 2
    if start:
        srcs = [pltpu.with_memory_space_constraint(a.reshape(chips, 2, *a.shape[1:]), pltpu.HBM) for a in arrays]
        lands = [pltpu.with_memory_space_constraint(lax.empty((chips,) + a.shape[1:], a.dtype), pltpu.HBM) for a in arrays]
    else:
        srcs, lands = handle["srcs"], handle["lands"]

    def body(*refs):
        ins, lnd, ssem, rsem = refs[:n], refs[n:2 * n], refs[2 * n], refs[2 * n + 1]
        x, y, c, _ = _mesh_place()
        dev = _peer(x, y, c, 1)[0]
        for q in range(chips):
            for j in range(n):
                cp = pltpu.make_async_remote_copy(
                    src_ref=ins[j].at[q, 1 - c], dst_ref=lnd[j].at[q], send_sem=ssem.at[j * chips + q], recv_sem=rsem.at[j * chips + q],
                    device_id=dev, device_id_type=pl.DeviceIdType.MESH)
                if start:
                    cp.start()
                else:
                    cp.wait_send()
                    cp.wait_recv()
        if start:
            refs[-1][...] = jnp.zeros_like(refs[-1])

    thru = [pltpu.HBM(a.shape, a.dtype) for a in srcs + lands]
    effect = pltpu.CompilerParams(has_side_effects=_DATAFLOW)
    if start:
        res = pl.pallas_call(
            body, name=name, out_shape=(pltpu.SemaphoreType.DMA((n * chips,)), pltpu.SemaphoreType.DMA((n * chips,)), *thru,
                                        jax.ShapeDtypeStruct((SUBLANE, LANE), F32)),
            in_specs=[_HBM] * (2 * n), out_specs=(_SEM, _SEM, *[_HBM] * (2 * n), pl.BlockSpec(memory_space=pltpu.VMEM)),
            input_output_aliases={i: 2 + i for i in range(2 * n)}, compiler_params=effect)(*srcs, *lands)
        return dict(srcs=list(res[2:2 + n]), lands=list(res[2 + n:2 + 2 * n]), sems=[res[0], res[1]]), res[-1]
    res = pl.pallas_call(
        body, name=name, out_shape=thru, in_specs=[_HBM] * (2 * n) + [_SEM, _SEM, pl.BlockSpec(memory_space=pl.ANY)],
        out_specs=[_HBM] * (2 * n), input_output_aliases={i: i for i in range(2 * n)}, compiler_params=effect)(
            *srcs, *lands, *handle["sems"], after)
    return dict(handle, srcs=list(res[:n]), lands=list(res[n:]))


def _pair_add(mine, theirs, core, name):
    chips, _, rows, w = mine.shape
    tm = _pick(rows, (256, 128, 64, 32, 16))

    def body(core_ref, a_ref, b_ref, o_ref):
        o_ref[...] = (a_ref[...].astype(F32) + b_ref[...].astype(F32)).astype(o_ref.dtype)

    return pl.pallas_call(
        body, name=name, out_shape=jax.ShapeDtypeStruct(theirs.shape, theirs.dtype),
        grid_spec=pltpu.PrefetchScalarGridSpec(
            num_scalar_prefetch=1, grid=(chips, rows // tm),
            in_specs=[pl.BlockSpec((None, None, tm, w), lambda q, i, core_ref: (q, core_ref[0], i, 0)),
                      pl.BlockSpec((None, tm, w), lambda q, i, core_ref: (q, i, 0))],
            out_specs=pl.BlockSpec((None, tm, w), lambda q, i, core_ref: (q, i, 0))),
        compiler_params=_params())(core, mine, theirs)


def _adamw(w, m, v, gparts, name, after=None):
    R, C = w.shape
    tm = _pick(R, (256, 128, 64, 32, 16, 8))
    order = [] if after is None else [after]

    def body(w_ref, m_ref, v_ref, g_ref, *rest):
        go, do, mo, vo = rest[len(order):]
        g = g_ref[0].astype(F32)
        for j in range(1, gparts.shape[0]):
            g = g + g_ref[j].astype(F32)
        mn = ADAM_B1 * m_ref[...] + (1.0 - ADAM_B1) * g
        vn = ADAM_B2 * v_ref[...] + (1.0 - ADAM_B2) * (g * g)
        m_hat = mn / (1.0 - ADAM_B1 ** ADAM_STEP)
        v_hat = vn / (1.0 - ADAM_B2 ** ADAM_STEP)
        go[...] = g
        do[...] = -ADAM_LR * (m_hat / (jnp.sqrt(v_hat) + ADAM_EPS) + ADAM_WD * w_ref[...])
        mo[...] = mn
        vo[...] = vn

    row = pl.BlockSpec((tm, C), lambda i: (i, 0))
    return pl.pallas_call(
        body, name=name, grid=(R // tm,),
        in_specs=[row, row, row, pl.BlockSpec((gparts.shape[0], tm, C), lambda i: (0, i, 0))] + [pl.BlockSpec(memory_space=pl.ANY)] * len(order),
        out_specs=[row] * 4, out_shape=[jax.ShapeDtypeStruct((R, C), F32)] * 4, compiler_params=_params())(w, m, v, gparts, *order)


def _pack(arrays):
    parts = []
    for a in arrays:
        f = a.reshape(1, -1)
        pad = _ceil_to(f.shape[1], SUBLANE * LANE) - f.shape[1]
        f = jnp.concatenate([f, jnp.zeros((1, pad), f.dtype)], axis=1) if pad else f
        parts.append(f.reshape(-1, LANE))
    rows = sum(p.shape[0] for p in parts)
    pad = _ceil_to(rows, 64) - rows
    return jnp.concatenate(parts + ([jnp.zeros((pad, LANE), parts[0].dtype)] if pad else []), axis=0)


def _unpack(buf, shapes):
    out, row = [], 0
    for s in shapes:
        size = 1
        for d in s:
            size *= d
        rows = _ceil_to(size, SUBLANE * LANE) // LANE
        out.append(buf[row:row + rows].reshape(1, -1)[:, :size].reshape(s))
        row += rows
    return out


def kernel(x, norm_mix_g, w_in, shift_mu, w0, w_lora_up, a0, a_lora_up, g_lora_up, k_k, k_a, r_k, lnx_g, lnx_b, w_proj_rwkv, sgu_ln_g, sgu_ln_b, sgu_w, sgu_b, w_proj_sgu, w_out, norm_ffn_g, w_ffn_gate, w_ffn_up, w_ffn_down, norm_final_g, loss_target, m_norm_mix_g, m_w_in, m_shift_mu, m_w0, m_w_lora_up, m_a0, m_a_lora_up, m_g_lora_up, m_k_k, m_k_a, m_r_k, m_lnx_g, m_lnx_b, m_w_proj_rwkv, m_sgu_ln_g, m_sgu_ln_b, m_sgu_w, m_sgu_b, m_w_proj_sgu, m_w_out, m_norm_ffn_g, m_w_ffn_gate, m_w_ffn_up, m_w_ffn_down, m_norm_final_g, v_norm_mix_g, v_w_in, v_shift_mu, v_w0, v_w_lora_up, v_a0, v_a_lora_up, v_g_lora_up, v_k_k, v_k_a, v_r_k, v_lnx_g, v_lnx_b, v_w_proj_rwkv, v_sgu_ln_g, v_sgu_ln_b, v_sgu_w, v_sgu_b, v_w_proj_sgu, v_w_out, v_norm_ffn_g, v_w_ffn_gate, v_w_ffn_up, v_w_ffn_down, v_norm_final_g):
    weights = dict(norm_mix_g=norm_mix_g, w_in=w_in, shift_mu=shift_mu, w0=w0, w_lora_up=w_lora_up, a0=a0, a_lora_up=a_lora_up,
                   g_lora_up=g_lora_up, k_k=k_k, k_a=k_a, r_k=r_k, lnx_g=lnx_g, lnx_b=lnx_b, w_proj_rwkv=w_proj_rwkv,
                   sgu_ln_g=sgu_ln_g, sgu_ln_b=sgu_ln_b, sgu_w=sgu_w, sgu_b=sgu_b, w_proj_sgu=w_proj_sgu, w_out=w_out,
                   norm_ffn_g=norm_ffn_g, w_ffn_gate=w_ffn_gate, w_ffn_up=w_ffn_up, w_ffn_down=w_ffn_down, norm_final_g=norm_final_g)
    m_in = dict(norm_mix_g=m_norm_mix_g, w_in=m_w_in, shift_mu=m_shift_mu, w0=m_w0, w_lora_up=m_w_lora_up, a0=m_a0,
                a_lora_up=m_a_lora_up, g_lora_up=m_g_lora_up, k_k=m_k_k, k_a=m_k_a, r_k=m_r_k, lnx_g=m_lnx_g, lnx_b=m_lnx_b,
                w_proj_rwkv=m_w_proj_rwkv, sgu_ln_g=m_sgu_ln_g, sgu_ln_b=m_sgu_ln_b, sgu_w=m_sgu_w, sgu_b=m_sgu_b,
                w_proj_sgu=m_w_proj_sgu, w_out=m_w_out, norm_ffn_g=m_norm_ffn_g, w_ffn_gate=m_w_ffn_gate, w_ffn_up=m_w_ffn_up,
                w_ffn_down=m_w_ffn_down, norm_final_g=m_norm_final_g)
    v_in = dict(norm_mix_g=v_norm_mix_g, w_in=v_w_in, shift_mu=v_shift_mu, w0=v_w0, w_lora_up=v_w_lora_up, a0=v_a0,
                a_lora_up=v_a_lora_up, g_lora_up=v_g_lora_up, k_k=v_k_k, k_a=v_k_a, r_k=v_r_k, lnx_g=v_lnx_g, lnx_b=v_lnx_b,
                w_proj_rwkv=v_w_proj_rwkv, sgu_ln_g=v_sgu_ln_g, sgu_ln_b=v_sgu_ln_b, sgu_w=v_sgu_w, sgu_b=v_sgu_b,
                w_proj_sgu=v_w_proj_sgu, w_out=v_w_out, norm_ffn_g=v_norm_ffn_g, w_ffn_gate=v_w_ffn_gate, w_ffn_up=v_w_ffn_up,
                w_ffn_down=v_w_ffn_down, norm_final_g=v_norm_final_g)
    names = list(weights)
    col_sharded = ("w_in", "w_lora_up", "a_lora_up", "g_lora_up", "w_proj_rwkv", "w_proj_sgu", "w_ffn_gate", "w_ffn_up")
    row_sharded = ("w_out", "w_ffn_down")
    sharded = [n for n in names if n in col_sharded or n in row_sharded]
    small = [n for n in names if n not in sharded]

    xs, tgt = x[0], loss_target[0]
    T, D = xs.shape
    RW = w0.shape[1]
    H = RW // HEAD
    SW = sgu_ln_g.shape[1]
    G = sgu_w.shape[1]
    assert 2 * SW == D, "the projection layout takes the SGU part to be as wide as a gate"
    lay = _rwkv_layout(RW, w_lora_up.shape[1], a_lora_up.shape[1], g_lora_up.shape[1], D)
    _, pw, _, rcp = lay
    icp = rcp + 3 * D
    b_ga, b_gb, b_z = rcp // D, rcp // D + 1, rcp // D + 2

    gather_groups = dict(win=["w_in", "w_lora_up", "a_lora_up", "g_lora_up"], proj=["w_proj_rwkv", "w_proj_sgu", "w_out"],
                         ffn_gate_up=["w_ffn_gate", "w_ffn_up"], ffn_down=["w_ffn_down"])
    handles, gather_token = _exchange_start([[(weights[n][0].astype(BF16), True) for n in grp] for grp in gather_groups.values()],
                                            "gather_start", rels=SIBLING + SAME_CORE)
    gather = dict(zip(gather_groups, handles))
    full = {}
    relay_tokens = {}
    joined = lambda g: g.transpose(1, 0, 2).reshape(g.shape[1], -1)

    def relay_weights(key, after):
        arrived = _exchange_wait(gather[key], after, "gather_wait_ici_" + key, rels=SAME_CORE, local=False)
        gather[key], relay_tokens[key] = _relay_start(arrived, "gather_relay_" + key)

    def take_weights(key, after):
        done = _exchange_wait(gather[key], after, "gather_wait_d2d_" + key, rels=SIBLING)
        for n, g in zip(gather_groups[key], done["lands"]):
            full[n] = g.reshape(-1, g.shape[2]) if n in row_sharded else g

    packed = [_pack([d[n] for n in small] + [gather_token]) for d in (weights, m_in, v_in)]
    n1 = _rms_fwd(xs, norm_mix_g, "rms_mix", deps=[gather_token, *packed])
    relay_weights("win", n1)
    take_weights("win", relay_tokens["win"])
    W_in = _w_in_to_proj(full["w_in"], lay, D, "w_in_layout")
    lora = [_pad_rows(joined(full[n]), rows) for n, rows in zip(("w_lora_up", "a_lora_up", "g_lora_up"), pw[3:])]
    mu_p = _pad_rwkv_cols(shift_mu, lay)
    rsmall = [w0, a0, k_k, k_a]
    hp = [lnx_g.reshape(H, 1, HEAD), lnx_b.reshape(H, 1, HEAD), r_k.reshape(H, 1, HEAD)]
    ws = sgu_w[0]
    bexp = jnp.repeat(sgu_b[0].T, SGU_GROUP, axis=1)
    gf = norm_final_g.reshape(1, D)

    proj = _matmul(n1, W_in, mode="nn", out_dtype=F32, name="proj_in")
    ga, gb = (proj, D, b_ga), (proj, D, b_gb)
    r_h, lw_h, k2_h, v_h, aa_h, bb_h, g_h = _rwkv_pre(proj, mu_p, rsmall, lora, lay, "rwkv_pre")
    wkv_in = [r_h, lw_h, k2_h, v_h, aa_h, bb_h]
    y_h, *wkv_saved = _wkv_fwd(*wkv_in, "wkv_fwd")
    relay_weights("proj", y_h)
    ya = _head_post(y_h, r_h, k2_h, v_h, g_h, hp, "head_post", deps=[relay_tokens["proj"]])
    relay_weights("ffn_gate_up", ya)
    yb = _sgu_fwd(proj, b_z, sgu_ln_g, sgu_ln_b, ws, bexp, "sgu_fwd")
    take_weights("proj", ya)
    pa = _matmul(ya, full["w_proj_rwkv"], mode="nn", out_dtype=F32, name="proj_a", deps=[relay_tokens["ffn_gate_up"]])

    def merge_fn(pb_v, pa_v, ga_v, gb_v):
        return pb_v, _sigmoid(ga_v) * pa_v + _sigmoid(gb_v) * pb_v
    pb, merged = _matmul(yb, full["w_proj_sgu"], mode="nn", name="proj_b_merge",
                         epi=(merge_fn, [pa, (proj, b_ga * D), (proj, b_gb * D)], [F32, BF16]))
    h1 = _matmul(merged, full["w_out"], mode="nn", out_dtype=F32, name="out_proj", add=xs)
    n2 = _rms_fwd(h1, norm_ffn_g, "rms_ffn")
    relay_weights("ffn_down", n2)
    take_weights("ffn_gate_up", n2)

    def act_fn(gt_v, up_v):
        return gt_v, up_v, gt_v * _sigmoid(gt_v) * up_v
    gt, up, act = _matmul(n2, full["w_ffn_gate"], b2=full["w_ffn_up"], mode="nn", name="ffn_gate_up_act", out_blocks=N_DEV,
                          epi=(act_fn, [], [BF16, BF16, BF16]), deps=[relay_tokens["ffn_down"]])
    take_weights("ffn_down", act)
    h2 = _matmul(act, full["w_ffn_down"], mode="nn", out_dtype=F32, name="ffn_down", add=h1)

    def final_fn(rv, pv):
        (h_v, t_v), (g_v,) = rv, pv
        r = lax.rsqrt(_mean(h_v * h_v) + RMS_EPS)
        yn = h_v * r
        e = yn * g_v - t_v
        loss = 0.5 * jnp.sum(_mean(e * e))
        dout = e * (1.0 / D)
        dyg = dout * g_v
        dh = r * (dyg - yn * _mean(dyg * yn))
        return [dh, dh], [jnp.full((1, LANE), loss, F32), _colsum(dout * yn)]
    dh2, dh2_bf, loss_part, d_gf = _rowwise(final_fn, [h2, tgt], [gf], [(D, F32), (D, BF16)], [(1, LANE), (1, D)], name="final_loss")

    grads = {}

    def start_scatter(group, name):
        blocks = [(grads[n].reshape(N_DEV, -1, grads[n].shape[1]) if n in row_sharded else grads[n], False) for n in group]
        (handle,), token = _exchange_start([blocks], name)
        return handle, token

    loss = lax.psum(loss_part[0, 0], ("x", "y", "c"))

    def dact_fn(d_v, gt_v, up_v):
        gt_v, up_v = gt_v.astype(F32), up_v.astype(F32)
        s = _sigmoid(gt_v)
        return d_v * up_v * (s * (1.0 + gt_v * (1.0 - s))), d_v * gt_v * s
    dgt, dup = _matmul(dh2_bf, full["w_ffn_down"], mode="nt", name="d_ffn_act", out_blocks=N_DEV,
                       epi=(dact_fn, [gt, up], [BF16, BF16]))
    scatter_groups = dict(ffn_down=["w_ffn_down"], ffn_gate_up=["w_ffn_gate", "w_ffn_up"],
                          mid=["w_out", "w_proj_rwkv", "w_proj_sgu"], last=["w_in", "w_lora_up", "a_lora_up", "g_lora_up"])
    scatters = {}
    grads["w_ffn_down"] = _matmul(act, dh2_bf, mode="tn", out_dtype=BF16, name="dw_ffn_down")
    scatters["ffn_down"], token = start_scatter(scatter_groups["ffn_down"], "scatter_start_ffn_down")
    dn2 = _matmul(dgt, full["w_ffn_gate"], mode="nt", out_dtype=F32, name="dn2_gate", deps=[token])
    grads["w_ffn_gate"] = _matmul(n2, dgt, mode="tn", out_dtype=BF16, name="dw_ffn_gate", out_blocks=N_DEV)
    grads["w_ffn_up"] = _matmul(n2, dup, mode="tn", out_dtype=BF16, name="dw_ffn_up", out_blocks=N_DEV)
    scatters["ffn_gate_up"], token = start_scatter(scatter_groups["ffn_gate_up"], "scatter_start_ffn_gate_up")
    dn2 = _matmul(dup, full["w_ffn_up"], mode="nt", out_dtype=F32, name="dn2_up", add=dn2, deps=[token])
    dh1, dh1_bf, d_g2 = _rms_bwd(dn2, h1, dh2, norm_ffn_g, "rms_ffn_bwd")
    dmerged = _matmul(dh1_bf, full["w_out"], mode="nt", out_dtype=F32, name="d_merged")
    grads["w_out"] = _matmul(merged, dh1_bf, mode="tn", out_dtype=BF16, name="dw_out")

    def dmerge_fn(rv, pv):
        d_v, ga_v, gb_v, pa_v, pb_v = rv
        sa, sb = _sigmoid(ga_v), _sigmoid(gb_v)
        dgates = jnp.concatenate([d_v * pa_v * sa * (1.0 - sa), d_v * pb_v * sb * (1.0 - sb)], axis=1)
        return [dgates, d_v * sa, d_v * sb], []
    dproj, dpa, dpb = _rowwise(dmerge_fn, [dmerged, ga, gb, pa, pb], [],
                               [(2 * D, BF16, icp, b_ga // 2, None), (D, BF16), (D, BF16)], [], name="d_merge")
    dya = _matmul(dpa, full["w_proj_rwkv"], mode="nt", out_dtype=F32, name="d_ya")
    dyb = _matmul(dpb, full["w_proj_sgu"], mode="nt", out_dtype=F32, name="d_yb")
    grads["w_proj_rwkv"] = _matmul(ya, dpa, mode="tn", out_dtype=BF16, name="dw_proj_a", out_blocks=N_DEV)
    grads["w_proj_sgu"] = _matmul(yb, dpb, mode="tn", out_dtype=BF16, name="dw_proj_b", out_blocks=N_DEV)
    scatters["mid"], token_mid = start_scatter(scatter_groups["mid"], "scatter_start_mid")
    dproj, d_lng, d_lnb, d_ws, d_bs = _sgu_bwd(proj, b_z, dyb, sgu_ln_g, sgu_ln_b, ws, bexp, dproj, "sgu_bwd")

    dr_h, dlw_h, dk2_h, dv_h, daa, dbb, dg_h, d_lnxg, d_lnxb, d_rk = _wkv_bwd(
        *wkv_in, *wkv_saved, y_h, g_h, hp, dya, "wkv_bwd", deps=[token_mid])
    dproj, d_mu, d_w0, d_a0, d_kk, d_ka, d_wlw, d_wla, d_wlg = _rwkv_pre_bwd(
        proj, mu_p, rsmall, lora, [dr_h, dk2_h, dv_h, dlw_h, daa, dbb, dg_h], dproj, lay, "rwkv_pre_bwd")
    split = lambda g: g.reshape(g.shape[0], N_DEV, -1).transpose(1, 0, 2)
    grads["w_in"] = _dw_in_from_proj(_matmul(n1, dproj, mode="tn", out_dtype=BF16, name="dw_in"), lay, D, w_in.shape[2], "dw_in_layout")
    grads["w_lora_up"] = split(d_wlw[:w_lora_up.shape[1]].astype(BF16))
    grads["a_lora_up"] = split(d_wla[:a_lora_up.shape[1]].astype(BF16))
    grads["g_lora_up"] = split(d_wlg[:g_lora_up.shape[1]].astype(BF16))
    out = {}

    def update_group(key, after):
        handle = scatters[key]
        parts = _exchange_wait(handle, after, "scatter_wait_" + key, rels=SAME_CORE if handle["chips"] else ALL_PEERS)["lands"]
        for n, part in zip(scatter_groups[key], parts):
            res = _adamw(weights[n][0], m_in[n][0], v_in[n][0], part, "adamw_" + n, after=after)
            out[n] = [t.reshape(weights[n].shape) for t in res]
            after = res[0]
        return after

    swap, token_swap = _sibling_swap([grads[n] for n in scatter_groups["last"]], None, None, "scatter_last_swap_start")
    after = update_group("mid", update_group("ffn_down", token_swap))
    swap = _sibling_swap(None, swap, after, "scatter_last_swap_wait")
    core = lax.axis_index("c").astype(jnp.int32).reshape(1)
    chip_sums = [_pair_add(mine, theirs, core, "scatter_last_add_" + n)
                 for n, mine, theirs in zip(scatter_groups["last"], swap["srcs"], swap["lands"])]
    (scatters["last"],), token_in = _exchange_start([[(s, False) for s in chip_sums]], "scatter_start_last", rels=SAME_CORE, chips=True)
    dn1 = _matmul(dproj, W_in, mode="nt", out_dtype=F32, name="dn1", deps=[token_in])
    dx, _, d_g1 = _rms_bwd(dn1, xs, dh1, norm_mix_g, "rms_mix_bwd", deps=[loss.reshape(1, 1)])
    small_grads = dict(norm_mix_g=d_g1, shift_mu=_unpad_rwkv_cols(d_mu, lay), w0=d_w0, a0=d_a0, k_k=d_kk, k_a=d_ka, r_k=d_rk,
                       lnx_g=d_lnxg, lnx_b=d_lnxb, sgu_ln_g=d_lng, sgu_ln_b=d_lnb, sgu_w=d_ws, sgu_b=d_bs[:, :G].T,
                       norm_ffn_g=d_g2, norm_final_g=d_gf)
    (gather_small,), after = _exchange_start([[(_pack([small_grads[n] for n in small] + [jnp.zeros_like(gather_token)]), True)]],
                                             "gather_small_start")
    for key in ("ffn_gate_up", "last"):
        after = update_group(key, after)
    small_parts = _exchange_wait(gather_small, after, "gather_small_wait")["lands"][0]
    res = _adamw(*packed, small_parts, "adamw_small")
    unpacked = [_unpack(t, [weights[n].shape for n in small]) for t in res]
    for i, n in enumerate(small):
        out[n] = [u[i] for u in unpacked]

    return (loss, dx[None], *[out[n][0] for n in names], *[out[n][1] for n in names],
            *[out[n][2] for n in names], *[out[n][3] for n in names])
```

```python
import jax
import jax.numpy as jnp
from jax import lax
from jax.experimental import pallas as pl
from jax.experimental.pallas import tpu as pltpu

F32 = jnp.float32
BF16 = jnp.bfloat16

N_DEV = 8
LANE = 128
SUBLANE = 8
HEAD = 64
SGU_CHUNK = 128
SGU_GROUP = 128
WKV_CHUNK = 64
RMS_EPS = 1e-6
LN_EPS = 1e-5
LNX_EPS = 64e-5
ADAM_LR, ADAM_B1, ADAM_B2, ADAM_EPS, ADAM_WD, ADAM_STEP = 0.001, 0.9, 0.999, 1e-08, 0.01, 10
VMEM_LIMIT_BYTES = 48 * 1024 * 1024
_SQRT_HALF = 0.7071067811865476
_INV_SQRT_2PI = 0.3989422804014327


def _pick(n, cands):
    for c in cands:
        if n % c == 0:
            return c
    return n


def _ceil_to(n, m):
    return -(-n // m) * m


def _params():
    return pltpu.CompilerParams(vmem_limit_bytes=VMEM_LIMIT_BYTES)


def _tile(n, cap):
    best = 0
    for d in range(LANE, min(n, cap) + 1, LANE):
        if n % d == 0:
            best = d
    return best or n


def _matmul_tiles(M, N, K, a_bytes, b_bytes, o_bytes, has_add, forced):
    tm = forced.get("m") or _tile(M, 1024)
    tn = forced.get("n") or _tile(N, 1024)
    tk = forced.get("k") or _tile(K, 2048)

    def vmem(tm, tn, tk):
        acc = tm * tn * 4 if tk < K else 0
        return 2 * (tm * tk * a_bytes + tk * tn * b_bytes + tm * tn * (o_bytes + (4 if has_add else 0))) + acc

    while vmem(tm, tn, tk) > (VMEM_LIMIT_BYTES * 3) // 4:
        if "k" not in forced and tk > 512 and _tile(K, tk // 2) < tk:
            tk = _tile(K, tk // 2)
        elif "m" not in forced and _tile(M, tm // 2) < tm:
            tm = _tile(M, tm // 2)
        else:
            break
    return tm, tn, tk


def _matmul(a, b, *, mode, out_dtype=F32, name, add=None, deps=(), out_blocks=0, epi=None, b2=None):
    def view(x):
        return (x.shape[1], x.shape[0] * x.shape[2], x.shape[2]) if x.ndim == 3 else (x.shape[0], x.shape[1], 0)

    (ar, ac, aw), (br, bc, bw) = view(a), view(b)
    a_col, b_col = {"nn": ("k", "n"), "nt": ("k", "k"), "tn": ("m", "n")}[mode]
    if mode == "nn":
        M, K, K2, N = ar, ac, br, bc
    elif mode == "nt":
        M, K, N, K2 = ar, ac, br, bc
    else:
        K, M, K2, N = ar, ac, br, bc
    assert K == K2, (a.shape, b.shape, mode)
    forced = {}
    for dim, w in ((a_col, aw), (b_col, bw), ("n", N // out_blocks if out_blocks else 0)):
        if w:
            assert forced.get(dim, w) == w
            forced[dim] = w
    has_add = add is not None
    tile_bytes = (sum(jnp.dtype(d).itemsize for d in epi[2]) + sum((e[0] if isinstance(e, tuple) else e).dtype.itemsize for e in epi[1])
                  if epi is not None else jnp.dtype(out_dtype).itemsize)
    tm, tn, tk = _matmul_tiles(M, N, K, a.dtype.itemsize, b.dtype.itemsize, tile_bytes, has_add, forced)
    kb = 1
    if "k" in forced and mode != "tn":
        lanes_ok = all(w or tk % LANE == 0 for w in (aw, bw if mode == "nt" else 1))
        kb = next(c for c in (4, 2, 1) if (K // tk) % c == 0 and (c == 1 or (lanes_ok and c * tk <= 1536)))
    nk = K // (tk * kb)
    dn = {"nn": (((1,), (0,)), ((), ())), "nt": (((1,), (1,)), ((), ())), "tn": (((0,), (0,)), ((), ()))}[mode]
    pick = {"m": lambda i, j, k: i, "n": lambda i, j, k: j, "k": lambda i, j, k: k}
    size = {"m": tm, "n": tn, "k": tk}

    def spec(blocked, row_dim, col_dim):
        rf, cf = pick[row_dim], pick[col_dim]
        reps = {d: (kb if d == "k" else 1) for d in (row_dim, col_dim)}
        if blocked:
            lead = kb if col_dim == "k" and kb > 1 else None
            return pl.BlockSpec((lead, size[row_dim], size[col_dim]), lambda i, j, k: (cf(i, j, k), rf(i, j, k), 0))
        return pl.BlockSpec((size[row_dim] * reps[row_dim], size[col_dim] * reps[col_dim]), lambda i, j, k: (rf(i, j, k), cf(i, j, k)))

    def k_part(ref, blocked, k_on_rows, j):
        if kb == 1:
            return ref[...]
        if blocked:
            return ref[j]
        return ref[j * tk:(j + 1) * tk, :] if k_on_rows else ref[:, j * tk:(j + 1) * tk]

    a_spec = spec(aw, "k" if mode == "tn" else "m", a_col)
    b_spec = spec(bw, "n" if mode == "nt" else "k", b_col)
    o_spec = spec(out_blocks, "m", "n")
    epi_fn, epi_ins, epi_dtypes = epi if epi is not None else (None, [], [out_dtype])
    epi_ins = [e if isinstance(e, tuple) else (e, None) for e in epi_ins]
    n_epi = len(epi_ins)
    twin = b2 is not None
    assert not twin or (nk == 1 and kb == 1 and epi is not None and b2.shape == b.shape)
    n_in = 2 + twin + has_add + n_epi + len(deps)
    n_out = len(epi_dtypes)

    def body(*refs):
        a_ref, b_ref = refs[0], refs[1]
        add_ref = refs[2 + twin] if has_add else None
        epi_refs = refs[2 + twin + has_add:2 + twin + has_add + n_epi]
        o_refs = refs[n_in:n_in + n_out]
        part = None
        for q in range(kb):
            a_q = k_part(a_ref, aw and a_col == "k", False, q)
            b_q = k_part(b_ref, bw and b_col == "k", mode == "nn", q)
            prod = lax.dot_general(a_q.astype(BF16), b_q.astype(BF16), dn, preferred_element_type=F32)
            part = prod if part is None else part + prod
        second = [lax.dot_general(a_ref[...].astype(BF16), refs[2][...].astype(BF16), dn, preferred_element_type=F32)] if twin else []

        def finish(res):
            outs = epi_fn(res, *second, *[e[...] for e in epi_refs]) if epi_fn is not None else (res,)
            for o_ref, val in zip(o_refs, outs):
                o_ref[...] = val.astype(o_ref.dtype)

        if nk == 1:
            finish(part + add_ref[...] if has_add else part)
            return
        acc_ref = refs[-1]
        kk = pl.program_id(2)

        @pl.when(kk == 0)
        def _():
            acc_ref[...] = part + add_ref[...] if has_add else part

        @pl.when(kk > 0)
        def _():
            acc_ref[...] += part

        @pl.when(kk == nk - 1)
        def _():
            finish(acc_ref[...])

    def epi_spec(arr, off):
        if off is None:
            return o_spec
        assert off % tn == 0
        return pl.BlockSpec((tm, tn), lambda i, j, k: (i, j + off // tn))

    ins = [a, b] + ([b2] if twin else []) + ([add] if has_add else []) + [arr for arr, _ in epi_ins] + list(deps)
    in_specs = ([a_spec, b_spec] + ([b_spec] if twin else []) + ([o_spec] if has_add else []) + [epi_spec(arr, off) for arr, off in epi_ins]
                + [pl.BlockSpec(d.shape, lambda i, j, k, nd=d.ndim: (0,) * nd) for d in deps])
    o_shape = (out_blocks, M, tn) if out_blocks else (M, N)
    res = pl.pallas_call(
        body, name=name, grid=(M // tm, N // tn, nk), in_specs=in_specs, out_specs=[o_spec] * n_out,
        out_shape=[jax.ShapeDtypeStruct(o_shape, dt) for dt in epi_dtypes],
        scratch_shapes=[pltpu.VMEM((tm, tn), F32)] if nk > 1 else [],
        compiler_params=_params())(*ins)
    return res[0] if epi is None else list(res)


def _rowwise(fn, rows, pars, row_outs, acc_outs, *, name, tm=256, deps=()):
    rows = [r if isinstance(r, tuple) else (r, r.shape[1], 0) for r in rows]
    row_outs = [o if len(o) == 5 else (o[0], o[1], o[0], 0, None) for o in row_outs]
    aliased = [(k, o[4]) for k, o in enumerate(row_outs) if o[4] is not None]
    R = rows[0][0].shape[0]
    if max(w for _, w, _ in rows) > 4096:
        tm = tm // 2
    tm = min(tm, R)
    assert R % tm == 0
    nr, npar = len(rows), len(pars)
    nro = len(row_outs)
    n_in = nr + npar + len(deps) + len(aliased)

    def body(*refs):
        rv = [r[...] for r in refs[:nr]]
        pv = [p[...] for p in refs[nr:nr + npar]]
        outs = refs[n_in:]
        ro, ao = fn(rv, pv)
        first = pl.program_id(0) == 0
        for o_ref, val in zip(outs[:nro], ro):
            o_ref[...] = val.astype(o_ref.dtype)

        @pl.when(first)
        def _():
            for o_ref, val in zip(outs[nro:], ao):
                o_ref[...] = val

        @pl.when(jnp.logical_not(first))
        def _():
            for o_ref, val in zip(outs[nro:], ao):
                o_ref[...] += val

    in_specs = ([pl.BlockSpec((tm, w), lambda i, cb=cb: (i, cb)) for _, w, cb in rows]
                + [pl.BlockSpec(p.shape, lambda i, nd=p.ndim: (0,) * nd) for p in list(pars) + list(deps)]
                + [pl.BlockSpec(memory_space=pl.ANY)] * len(aliased))
    out_shape = ([jax.ShapeDtypeStruct((R, full), dt) for _, dt, full, _, _ in row_outs]
                 + [jax.ShapeDtypeStruct(s, F32) for s in acc_outs])
    out_specs = ([pl.BlockSpec((tm, f), lambda i, cb=cb: (i, cb)) for f, _, _, cb, _ in row_outs]
                 + [pl.BlockSpec(s, lambda i, nd=len(s): (0,) * nd) for s in acc_outs])
    res = pl.pallas_call(body, name=name, grid=(R // tm,), in_specs=in_specs, out_specs=out_specs, out_shape=out_shape,
                         input_output_aliases={n_in - len(aliased) + q: k for q, (k, _) in enumerate(aliased)},
                         compiler_params=_params())(*[r for r, _, _ in rows], *pars, *deps, *[buf for _, buf in aliased])
    return list(res)


def _bdot(a, b, mode="nn"):
    dn = {"nn": (((1,), (0,)), ((), ())), "nt": (((1,), (1,)), ((), ())), "tn": (((0,), (0,)), ((), ()))}[mode]
    return lax.dot_general(a.astype(BF16), b.astype(BF16), dn, preferred_element_type=F32)


def _sigmoid(x):
    return jax.nn.sigmoid(x)


def _softplus(x):
    return jnp.maximum(x, 0.0) + jnp.log1p(jnp.exp(-jnp.abs(x)))


def _gelu(z):
    return 0.5 * z * (1.0 + lax.erf(z * _SQRT_HALF))


def _gelu_grad(z):
    return 0.5 * (1.0 + lax.erf(z * _SQRT_HALF)) + z * jnp.exp(-0.5 * z * z) * _INV_SQRT_2PI


def _mean(x):
    return jnp.mean(x, axis=-1, keepdims=True)


def _colsum(x):
    return jnp.sum(x, axis=0, keepdims=True)


def _rms_fwd(x, g, name, deps=()):
    def fn(rv, pv):
        (xv,), (gv,) = rv, pv
        r = lax.rsqrt(_mean(xv * xv) + RMS_EPS)
        return [xv * r * gv], []
    return _rowwise(fn, [x], [g], [(x.shape[1], BF16)], [], name=name, deps=deps)[0]


def _rms_bwd(dn, x, dres, g, name, deps=()):
    def fn(rv, pv):
        (dnv, xv, drv), (gv,) = rv, pv
        r = lax.rsqrt(_mean(xv * xv) + RMS_EPS)
        yn = xv * r
        dyg = dnv * gv
        dx = drv + r * (dyg - yn * _mean(dyg * yn))
        return [dx, dx], [_colsum(dnv * yn)]
    D = x.shape[1]
    return _rowwise(fn, [dn, x, dres], [g], [(D, F32), (D, BF16)], [(1, D)], name=name, deps=deps)


def _rwkv_layout(RW, Lw, La, Lg, D):
    widths = [RW, RW, RW, Lw, La, Lg]
    pw = [_ceil_to(w, LANE) for w in widths]
    pw[5] += _ceil_to(sum(pw), 2 * D) - sum(pw)
    offs = [sum(pw[:i]) for i in range(6)]
    return widths, pw, offs, sum(pw)


def _pad_rwkv_cols(a, lay):
    widths, pw, _, _ = lay
    pieces, src = [], 0
    for w, p in zip(widths, pw):
        pieces.append(a[:, src:src + w])
        if p > w:
            pieces.append(jnp.zeros((a.shape[0], p - w), a.dtype))
        src += w
    return jnp.concatenate(pieces, axis=1)


def _unpad_rwkv_cols(a, lay):
    widths, _, offs, _ = lay
    return jnp.concatenate([a[:, o:o + w] for o, w in zip(offs, widths)], axis=1)


def _proj_pieces(lay, D, cs):
    widths, _, offs, rcp = lay
    rc = sum(widths)
    segs = [(sum(widths[:j]), widths[j], offs[j]) for j in range(6)] + [(rc, D, rcp + 2 * D), (rc + D, D, rcp), (rc + 2 * D, D, rcp + D)]
    pieces = []
    for start, width, dst in segs:
        n = start
        while n < start + width:
            d, off = divmod(n, cs)
            take = min(cs - off, start + width - n)
            pieces.append((d, off, dst + n - start, take))
            n += take
    return pieces


def _w_in_to_proj(g, lay, D, name):
    nb, rows, cs = g.shape
    icp = lay[3] + 3 * D
    pieces = _proj_pieces(lay, D, cs)
    tm = _pick(rows, (256, 128, 64, 32, 16))

    def body(i_ref, o_ref):
        o_ref[...] = jnp.zeros_like(o_ref)
        for d, src, dst, w in pieces:
            o_ref[:, dst:dst + w] = i_ref[d, :, src:src + w]

    return pl.pallas_call(
        body, name=name, grid=(rows // tm,), in_specs=[pl.BlockSpec((nb, tm, cs), lambda i: (0, i, 0))],
        out_specs=pl.BlockSpec((tm, icp), lambda i: (i, 0)), out_shape=jax.ShapeDtypeStruct((rows, icp), g.dtype),
        compiler_params=_params())(g)


def _dw_in_from_proj(a, lay, D, cs, name):
    rows, icp = a.shape
    pieces = _proj_pieces(lay, D, cs)
    tm = _pick(rows, (256, 128, 64, 32, 16))

    def body(i_ref, o_ref):
        for d, src, dst, w in pieces:
            o_ref[d, :, src:src + w] = i_ref[:, dst:dst + w]

    return pl.pallas_call(
        body, name=name, grid=(rows // tm,), in_specs=[pl.BlockSpec((tm, icp), lambda i: (i, 0))],
        out_specs=pl.BlockSpec((N_DEV, tm, cs), lambda i: (0, i, 0)), out_shape=jax.ShapeDtypeStruct((N_DEV, rows, cs), a.dtype),
        compiler_params=_params())(a)


def _pad_rows(a, rows):
    return a if a.shape[0] == rows else jnp.concatenate([a, jnp.zeros((rows - a.shape[0], a.shape[1]), a.dtype)], axis=0)


def _token_shift(p, halo, mu, i):
    tm = p.shape[0]
    hid = lax.broadcasted_iota(jnp.int32, (SUBLANE, 1), 0)
    before = jnp.sum(jnp.where(hid == SUBLANE - 1, halo, 0.0), axis=0, keepdims=True)
    before = jnp.where(i == 0, 0.0, before)
    rid = lax.broadcasted_iota(jnp.int32, (tm, 1), 0)
    prev = jnp.where(rid == 0, before, pltpu.roll(p, 1, 0))
    d = prev - p
    return p + d * mu, d


def _rwkv_math(ps, w0, a0, k_k, k_a, wlw, wla, wlg, lay):
    _, pw, offs, _ = lay
    r, k, v, xw, xa, xg = (ps[:, offs[j]:offs[j] + pw[j]] for j in range(6))
    tw = jnp.tanh(xw)
    ww = w0 + _bdot(tw, wlw)
    lw = -jnp.exp(-_softplus(-ww) - 0.5)
    a = _sigmoid(a0 + _bdot(xa, wla))
    sg = _sigmoid(xg)
    g = _bdot(sg, wlg)
    return dict(r=r, k=k, v=v, xa=xa, tw=tw, ww=ww, lw=lw, a=a, sg=sg, g=g, kkp=k * k_k, k2=k * (1.0 + (a - 1.0) * k_a))


def _halo_spec(tm, width):
    hb = tm // SUBLANE
    return pl.BlockSpec((SUBLANE, width), lambda i: (jnp.maximum(i * hb - 1, 0), 0))


def _rowsum(x):
    return jnp.sum(x, axis=-1, keepdims=True)


def _kk_math(kkp):
    nrm = jnp.sqrt(_rowsum(kkp * kkp))
    inv = 1.0 / jnp.maximum(nrm, 1e-12)
    return nrm, inv, kkp * inv


def _rwkv_pre(p, mu, small, lora, lay, name):
    T, rcp = p.shape[0], lay[3]
    H = lay[0][0] // HEAD
    tm = min(128, T)

    def body(p_ref, ph_ref, mu_ref, w0_ref, a0_ref, kk_ref, ka_ref, wlw_ref, wla_ref, wlg_ref, r_o, lw_o, k2_o, v_o, aa_o, bb_o, g_o):
        ps, _ = _token_shift(p_ref[...], ph_ref[...], mu_ref[...], pl.program_id(0))
        q = _rwkv_math(ps, w0_ref[...], a0_ref[...], kk_ref[...], ka_ref[...], wlw_ref[...], wla_ref[...], wlg_ref[...], lay)
        for h in range(H):
            sl = slice(h * HEAD, (h + 1) * HEAD)
            for o_ref, key in ((r_o, "r"), (lw_o, "lw"), (k2_o, "k2"), (v_o, "v"), (g_o, "g")):
                o_ref[h] = q[key][:, sl]
            _, _, kk = _kk_math(q["kkp"][:, sl])
            aa_o[h] = -kk
            bb_o[h] = kk * q["a"][:, sl]

    whole = lambda arr: pl.BlockSpec(arr.shape, lambda i: (0, 0))
    return pl.pallas_call(
        body, name=name, grid=(T // tm,),
        in_specs=([pl.BlockSpec((tm, rcp), lambda i: (i, 0)), _halo_spec(tm, rcp), whole(mu)]
                  + [whole(s) for s in small] + [whole(w) for w in lora]),
        out_specs=[pl.BlockSpec((H, tm, HEAD), lambda i: (0, i, 0))] * 7, out_shape=[jax.ShapeDtypeStruct((H, T, HEAD), F32)] * 7,
        compiler_params=_params())(p, p, mu, *small, *lora)


def _rwkv_pre_bwd(p, mu, small, lora, hgrads, dproj, lay, name):
    T, rcp = p.shape[0], lay[3]
    widths, pw, offs, _ = lay
    RW = widths[0]
    H = RW // HEAD
    tm = min(128, T)
    nt = T // tm
    hb = tm // SUBLANE

    def body(p_ref, ph_ref, mu_ref, w0_ref, a0_ref, kk_ref, ka_ref, wlw_ref, wla_ref, wlg_ref,
             dr_h, dk2_h, dv_h, dlw_h, daa, dbb, dg_h, buf_ref,
             dp_ref, dmu_ref, dw0_ref, da0_ref, dkk_ref, dka_ref, dwlw_ref, dwla_ref, dwlg_ref,
             s_dr, s_dk2, s_dv, s_dlw, s_dkkp, s_da, s_dg, dps_ref, next_ref):
        i = pl.program_id(0)
        ps, dprev = _token_shift(p_ref[...], ph_ref[...], mu_ref[...], nt - 1 - i)
        k_k, k_a = kk_ref[...], ka_ref[...]
        q = _rwkv_math(ps, w0_ref[...], a0_ref[...], k_k, k_a, wlw_ref[...], wla_ref[...], wlg_ref[...], lay)
        k, a, lw, ww, tw, sg = q["k"], q["a"], q["lw"], q["ww"], q["tw"], q["sg"]
        for h in range(H):
            sl = slice(h * HEAD, (h + 1) * HEAD)
            s_dr[:, sl] = dr_h[h]
            s_dk2[:, sl] = dk2_h[h]
            s_dv[:, sl] = dv_h[h]
            s_dlw[:, sl] = dlw_h[h]
            s_dg[:, sl] = dg_h[h]
            nrm, inv, kk = _kk_math(q["kkp"][:, sl])
            dbb_h = dbb[h]
            dkk = dbb_h * a[:, sl] - daa[h]
            s_dkkp[:, sl] = jnp.where(nrm > 1e-12, inv * (dkk - kk * _rowsum(dkk * kk)), dkk * inv)
            s_da[:, sl] = dbb_h * kk
        dk2, dkkp, dg = s_dk2[...], s_dkkp[...], s_dg[...]
        dk = dk2 * (1.0 + (a - 1.0) * k_a) + dkkp * k_k
        da = s_da[...] + dk2 * k * k_a
        dpa = da * a * (1.0 - a)
        dww = s_dlw[...] * lw * _sigmoid(-ww)
        dxa = _bdot(dpa, wla_ref[...], "nt")
        dxw = _bdot(dww, wlw_ref[...], "nt") * (1.0 - tw * tw)
        dxg = _bdot(dg, wlg_ref[...], "nt") * sg * (1.0 - sg)
        segs = (s_dr[...], dk, s_dv[...], dxw, dxa, dxg)
        sums = [dmu_ref, dw0_ref, da0_ref, dkk_ref, dka_ref, dwlw_ref, dwla_ref, dwlg_ref]

        @pl.when(i == 0)
        def _():
            for s in sums + [next_ref]:
                s[...] = jnp.zeros_like(s)

        for j, seg in enumerate(segs):
            sl = slice(offs[j], offs[j] + pw[j])
            dps_ref[:, sl] = seg
            dmu_ref[:, sl] += _colsum(seg * dprev[:, sl])
        dw0_ref[...] += _colsum(dww)
        da0_ref[...] += _colsum(dpa)
        dkk_ref[...] += _colsum(dkkp * k)
        dka_ref[...] += _colsum(dk2 * k * (a - 1.0))
        dwlw_ref[...] += _bdot(tw, dww, "tn")
        dwla_ref[...] += _bdot(q["xa"], dpa, "tn")
        dwlg_ref[...] += _bdot(sg, dg, "tn")
        dps = dps_ref[...]
        rid = lax.broadcasted_iota(jnp.int32, (tm, 1), 0)
        nxt = jnp.where(rid == tm - 1, next_ref[...], pltpu.roll(dps, tm - 1, 0))
        mu_v = mu_ref[...]
        dp_ref[...] = (dps * (1.0 - mu_v) + nxt * mu_v).astype(BF16)
        next_ref[...] = _colsum(jnp.where(rid == 0, dps, 0.0))

    whole = lambda arr: pl.BlockSpec(arr.shape, lambda i: (0, 0))
    row = lambda w: pl.BlockSpec((tm, w), lambda i: (nt - 1 - i, 0))
    acc_shapes = [(1, rcp), (1, RW), (1, RW), (1, RW), (1, RW)] + [w.shape for w in lora]
    return pl.pallas_call(
        body, name=name, grid=(nt,),
        in_specs=([row(rcp), pl.BlockSpec((SUBLANE, rcp), lambda i: (jnp.maximum((nt - 1 - i) * hb - 1, 0), 0)), whole(mu)]
                  + [whole(s) for s in small] + [whole(w) for w in lora]
                  + [pl.BlockSpec((H, tm, HEAD), lambda i: (0, nt - 1 - i, 0))] * 7 + [pl.BlockSpec(memory_space=pl.ANY)]),
        out_specs=[row(rcp)] + [pl.BlockSpec(s, lambda i: (0, 0)) for s in acc_shapes],
        out_shape=[jax.ShapeDtypeStruct(dproj.shape, BF16)] + [jax.ShapeDtypeStruct(s, F32) for s in acc_shapes],
        scratch_shapes=[pltpu.VMEM((tm, RW), F32)] * 7 + [pltpu.VMEM((tm, rcp), F32), pltpu.VMEM((1, rcp), F32)],
        input_output_aliases={10 + 7: 0}, compiler_params=_params())(p, p, mu, *small, *lora, *hgrads, dproj)


def _head_post_math(y, r, k2, v, lg, lb, rk):
    yc = y - _mean(y)
    rstd = lax.rsqrt(_mean(yc * yc) + LNX_EPS)
    yn = yc * rstd
    s = _rowsum(r * k2 * rk)
    return yn, rstd, yn * lg + lb + s * v, s


def _head_post(y, r, k2, v, g, hp, name, deps=()):
    H, T, _ = y.shape
    tm = min(128, T)

    def body(y_ref, r_ref, k_ref, v_ref, g_ref, lg_ref, lb_ref, rk_ref, *rest):
        o_ref = rest[-1]
        _, _, t, _ = _head_post_math(y_ref[...], r_ref[...], k_ref[...], v_ref[...], lg_ref[...], lb_ref[...], rk_ref[...])
        out = (t * g_ref[...]).astype(BF16)
        for h in range(H):
            o_ref[:, h * HEAD:(h + 1) * HEAD] = out[h]

    blk = pl.BlockSpec((H, tm, HEAD), lambda i: (0, i, 0))
    par = pl.BlockSpec((H, 1, HEAD), lambda i: (0, 0, 0))
    return pl.pallas_call(
        body, name=name, grid=(T // tm,),
        in_specs=[blk] * 5 + [par] * 3 + [pl.BlockSpec(d.shape, lambda i, nd=d.ndim: (0,) * nd) for d in deps],
        out_specs=pl.BlockSpec((tm, H * HEAD), lambda i: (i, 0)),
        out_shape=jax.ShapeDtypeStruct((T, H * HEAD), BF16), compiler_params=_params())(y, r, k2, v, g, *hp, *deps)


def _bmm(x, y, mode):
    dn = {"nn": (((2,), (1,)), ((0,), (0,))), "nt": (((2,), (2,)), ((0,), (0,))), "tn": (((1,), (1,)), ((0,), (0,)))}[mode]
    (xh, xl), (yh, yl) = _split(x), _split(y)
    dot = lambda p, q: lax.dot_general(p, q, dn, preferred_element_type=F32)
    out = dot(xh, yh)
    if yl is not None:
        out = out + dot(xh, yl)
    if xl is not None:
        out = out + dot(xl, yh)
    return out


def _split(x):
    if isinstance(x, tuple):
        return x
    hi = x.astype(BF16)
    return hi, (x - hi.astype(F32)).astype(BF16)


def _exact(x):
    return x.astype(BF16), None


def _round(x):
    return x if isinstance(x, tuple) else (x.astype(BF16), None)


def _rows(*xs):
    if isinstance(xs[0], tuple):
        return tuple(None if any(p is None for p in parts) else jnp.concatenate(parts, axis=1) for parts in zip(*xs))
    return jnp.concatenate(xs, axis=1)


def _wkv_chunk(r, lw, k, v, a, b, inverse=None):
    hb, C, _ = r.shape
    ti = lax.broadcasted_iota(jnp.int32, (C, C), 0)
    si = lax.broadcasted_iota(jnp.int32, (C, C), 1)
    linc, lstr, eye = (ti >= si).astype(F32), (ti > si).astype(F32), (ti == si).astype(F32)
    qmask = jnp.concatenate([jnp.concatenate([lstr, lstr], axis=1), jnp.concatenate([linc, linc], axis=1)], axis=0)
    lincb = _exact(jnp.broadcast_to(linc, (hb, C, C)))
    both = _exact(jnp.broadcast_to(jnp.concatenate([linc, lstr], axis=0), (hb, 2 * C, C)))
    ones = _exact(jnp.ones_like(v))
    lws = _split(lw)
    ci = _bmm(lincb, lws, "nn")
    cC = jnp.sum(lw, axis=1, keepdims=True)
    gi, ge, gn, gr = jnp.exp(ci), jnp.exp(ci - lw), jnp.exp(-ci), jnp.exp(cC - ci)
    q = dict(At=a * ge, Rt=r * gi, Bt=b * gn, Kt=k * gn, Bh=b * gr, Kh=k * gr)
    s = dict(AR=_round(_rows(q["At"], q["Rt"])), BK=_round(_rows(q["Bt"], q["Kt"])), BKh=_round(_rows(q["Bh"], q["Kh"])), v=_round(v))
    quad = _bmm(s["AR"], s["BK"], "nt") * qmask
    s["top"], s["bot"] = _round(quad[:, :C]), _round(quad[:, C:])
    if inverse is None:
        A_ab = quad[:, :C, :C]
        Tm = eye + A_ab
        Pw = _round(A_ab)
        n = 1
        while 2 * n < C:
            Pw = _round(_bmm(Pw, Pw, "nn"))
            Tm = Tm + _bmm(_round(Tm), Pw, "nn")
            n *= 2
        inverse = Tm
    s["Tm"] = _round(inverse)
    gC = jnp.exp(_bmm(lws, ones, "tn"))
    q.update(gi=gi, ge=ge, gn=gn, gr=gr, qmask=qmask, both=both, gC=gC, ones=ones, s=s)
    return q


def _wkv_u(s, H0s, C):
    arh = _bmm(s["AR"], H0s, "nn")
    zv = _rows(tuple(None if p is None else jnp.zeros_like(p) for p in s["v"]), s["v"])
    U = _bmm(s["Tm"], _round(arh[:, :C] + _bmm(s["top"], zv, "nn")), "nn")
    return arh, _rows(_round(U), s["v"])


def _wkv_fwd(r, lw, k, v, a, b, name):
    H, T, N = r.shape
    C = min(WKV_CHUNK, T)
    nc = T // C
    hb = _pick(H, (16, 8, 4, 2))

    def body(r_ref, lw_ref, k_ref, v_ref, a_ref, b_ref, y_ref, st_ref, inv_ref, u_ref, h_ref):
        @pl.when(pl.program_id(1) == 0)
        def _():
            h_ref[...] = jnp.zeros_like(h_ref)

        H0 = h_ref[...]
        st_ref[0] = H0
        q = _wkv_chunk(r_ref[...], lw_ref[...], k_ref[...], v_ref[...], a_ref[...], b_ref[...])
        s = q["s"]
        arh, UV = _wkv_u(s, _round(H0), C)
        inv_ref[0] = s["Tm"][0]
        u_ref[...] = UV[0][:, :C]
        y_ref[...] = arh[:, C:] + _bmm(s["bot"], UV, "nn")
        h_ref[...] = q["gC"] * H0 + _bmm(s["BKh"], UV, "tn")

    blk = pl.BlockSpec((hb, C, N), lambda h, c: (h, c, 0))
    per_chunk = lambda w: pl.BlockSpec((1, hb, w, w), lambda h, c: (c, h, 0, 0))
    return pl.pallas_call(
        body, name=name, grid=(H // hb, nc), in_specs=[blk] * 6, out_specs=[blk, per_chunk(N), per_chunk(C), blk],
        out_shape=[jax.ShapeDtypeStruct((H, T, N), F32), jax.ShapeDtypeStruct((nc, H, N, N), F32),
                   jax.ShapeDtypeStruct((nc, H, C, C), BF16), jax.ShapeDtypeStruct((H, T, N), BF16)],
        scratch_shapes=[pltpu.VMEM((hb, N, N), F32)], compiler_params=_params())(r, lw, k, v, a, b)


def _wkv_bwd(r, lw, k, v, a, b, states, inverses, u, y, g, hp, dya, name, deps=()):
    H, T, N = r.shape
    C = min(WKV_CHUNK, T)
    nc = T // C
    hb = _pick(H, (16, 8, 4, 2))
    hsum = lambda t: jnp.sum(t, axis=1, keepdims=True)

    def body(r_ref, lw_ref, k_ref, v_ref, a_ref, b_ref, st_ref, inv_ref, u_ref, y_ref, g_ref, lg_ref, lb_ref, rk_ref, dya_ref, *rest):
        (dr_ref, dlw_ref, dk_ref, dv_ref, da_ref, db_ref, dg_ref, dlg_ref, dlb_ref, drk_ref, dh_ref, d_s) = rest[len(deps):]
        first = pl.program_id(1) == 0

        @pl.when(first)
        def _():
            dh_ref[...] = jnp.zeros_like(dh_ref)

        for h in range(hb):
            d_s[h] = dya_ref[:, h * N:(h + 1) * N]
        d_v, r_v, k_v, v_v, lg, rk = d_s[...], r_ref[...], k_ref[...], v_ref[...], lg_ref[...], rk_ref[...]
        yn, rstd, t, bonus = _head_post_math(y_ref[...], r_v, k_v, v_v, lg, lb_ref[...], rk)
        dyo = d_v * g_ref[...]
        dyn = dyo * lg
        ds = _rowsum(dyo * v_v)
        dy = rstd * (dyn - _mean(dyn) - yn * _mean(dyn * yn))
        dg_ref[...] = d_v * t
        sums = (hsum(dyo * yn), hsum(dyo), hsum(ds * r_v * k_v))

        @pl.when(first)
        def _():
            for o_ref, val in zip((dlg_ref, dlb_ref, drk_ref), sums):
                o_ref[...] = val

        @pl.when(jnp.logical_not(first))
        def _():
            for o_ref, val in zip((dlg_ref, dlb_ref, drk_ref), sums):
                o_ref[...] += val

        dHC = dh_ref[...]
        H0 = st_ref[0]
        q = _wkv_chunk(r_v, lw_ref[...], k_v, v_v, a_ref[...], b_ref[...], inverse=inv_ref[0])
        s, gC = q["s"], q["gC"]
        H0s, dHs, dY = _round(H0), _round(dHC), _round(dy)
        UV = _rows(_round(u_ref[...]), s["v"])
        bot_dy = _bmm(s["bot"], dY, "tn")
        bkh_dh = _bmm(s["BKh"], dHs, "nn")
        dP = _round(_bmm(s["Tm"], _round(bot_dy[:, :C] + bkh_dh[:, :C]), "tn"))
        dv_ref[...] = bot_dy[:, C:] + bkh_dh[:, C:] + _bmm(s["top"], dP, "tn")[:, C:] + dyo * bonus
        dPY = _rows(dP, dY)
        dh_ref[...] = gC * dHC + _bmm(s["AR"], dPY, "tn")
        dquad = _round(_bmm(dPY, UV, "nt") * q["qmask"])
        dAR = _bmm(dPY, H0s, "nt") + _bmm(dquad, s["BK"], "nn")
        dBK = _bmm(dquad, s["AR"], "tn")
        dBKh = _bmm(UV, dHs, "nt")
        dAt, dRt, dBt, dKt, dBh, dKh = dAR[:, :C], dAR[:, C:], dBK[:, :C], dBK[:, C:], dBKh[:, :C], dBKh[:, C:]
        dr_ref[...] = dRt * q["gi"] + ds * k_v * rk
        da_ref[...] = dAt * q["ge"]
        db_ref[...] = dBt * q["gn"] + dBh * q["gr"]
        dk_ref[...] = dKt * q["gn"] + dKh * q["gr"] + ds * r_v * rk
        tail = dBh * q["Bh"] + dKh * q["Kh"]
        dci = dRt * q["Rt"] - dBt * q["Bt"] - dKt * q["Kt"] - tail
        dcC = jnp.sum(tail, axis=1, keepdims=True) + _bmm(q["ones"], H0 * dHC * gC, "nt")
        dlw_ref[...] = _bmm(q["both"], _rows(dci, dAt * q["At"]), "tn") + dcC

    blk = pl.BlockSpec((hb, C, N), lambda h, c: (h, nc - 1 - c, 0))
    per_chunk = lambda w: pl.BlockSpec((1, hb, w, w), lambda h, c: (nc - 1 - c, h, 0, 0))
    par = pl.BlockSpec((hb, 1, N), lambda h, c: (h, 0, 0))
    return pl.pallas_call(
        body, name=name, grid=(H // hb, nc),
        in_specs=([blk] * 6 + [per_chunk(N), per_chunk(C), blk, blk, blk] + [par] * 3
                  + [pl.BlockSpec((C, hb * N), lambda h, c: (nc - 1 - c, h))]
                  + [pl.BlockSpec(d.shape, lambda h, c, nd=d.ndim: (0,) * nd) for d in deps]),
        out_specs=[blk] * 7 + [par] * 3,
        out_shape=[jax.ShapeDtypeStruct((H, T, N), F32)] * 7 + [jax.ShapeDtypeStruct((H, 1, N), F32)] * 3,
        scratch_shapes=[pltpu.VMEM((hb, N, N), F32), pltpu.VMEM((hb, C, N), F32)],
        compiler_params=_params())(r, lw, k, v, a, b, states, inverses, u, y, g, *hp, dya, *deps)


def _sgu_ln(z, SW, lng, lnb):
    ge = _gelu(z)
    u, vv = ge[:, :SW], ge[:, SW:]
    xc = vv - _mean(vv)
    rstd = lax.rsqrt(_mean(xc * xc) + LN_EPS)
    vn = xc * rstd
    return u, vn, rstd, vn * lng + lnb


def _causal(ws_ref, g):
    ti = lax.broadcasted_iota(jnp.int32, (SGU_CHUNK, SGU_CHUNK), 0)
    si = lax.broadcasted_iota(jnp.int32, (SGU_CHUNK, SGU_CHUNK), 1)
    return ti >= si, jnp.where(ti >= si, ws_ref[g], 0.0).astype(BF16)


def _sgu_fwd(proj, zblock, lng, lnb, ws, bexp, name):
    T, SW = proj.shape[0], lng.shape[1]
    G = ws.shape[0]
    tr = min(256, T)
    nch = tr // SGU_CHUNK

    def body(z_ref, lng_ref, lnb_ref, ws_ref, be_ref, o_ref):
        u, _, _, vl = _sgu_ln(z_ref[...], SW, lng_ref[...], lnb_ref[...])
        for g in range(G):
            cs = slice(g * SGU_GROUP, (g + 1) * SGU_GROUP)
            _, wc = _causal(ws_ref, g)
            for n in range(nch):
                rs = slice(n * SGU_CHUNK, (n + 1) * SGU_CHUNK)
                m = jnp.dot(wc, vl[rs, cs].astype(BF16), preferred_element_type=F32) + be_ref[:, cs]
                o_ref[rs, cs] = (u[rs, cs] * m).astype(BF16)

    whole = lambda arr: pl.BlockSpec(arr.shape, lambda i, nd=arr.ndim: (0,) * nd)
    return pl.pallas_call(
        body, name=name, grid=(T // tr,),
        in_specs=[pl.BlockSpec((tr, 2 * SW), lambda i: (i, zblock)), whole(lng), whole(lnb), whole(ws), whole(bexp)],
        out_specs=pl.BlockSpec((tr, SW), lambda i: (i, 0)), out_shape=jax.ShapeDtypeStruct((T, SW), BF16),
        compiler_params=_params())(proj, lng, lnb, ws, bexp)


def _sgu_bwd(proj, zblock, dyb, lng, lnb, ws, bexp, dproj, name):
    T, SW = proj.shape[0], lng.shape[1]
    G = ws.shape[0]
    tr = min(256, T)
    nch = tr // SGU_CHUNK
    nt = T // tr

    def body(z_ref, dy_ref, lng_ref, lnb_ref, ws_ref, be_ref, buf_ref, dz_ref, dlg_ref, dlb_ref, dws_ref, db_ref, du_s, dvl_s, dbacc_s):
        i = pl.program_id(0)
        zv = z_ref[...]
        lng_v = lng_ref[...]
        u, vn, rstd, vl = _sgu_ln(zv, SW, lng_v, lnb_ref[...])

        @pl.when(i == 0)
        def _():
            for s in (dlg_ref, dlb_ref, dws_ref, dbacc_s):
                s[...] = jnp.zeros_like(s)

        for g in range(G):
            cs = slice(g * SGU_GROUP, (g + 1) * SGU_GROUP)
            tri, wc = _causal(ws_ref, g)
            for n in range(nch):
                rs = slice(n * SGU_CHUNK, (n + 1) * SGU_CHUNK)
                blk = vl[rs, cs].astype(BF16)
                m = jnp.dot(wc, blk, preferred_element_type=F32) + be_ref[:, cs]
                dyv = dy_ref[rs, cs]
                du_s[rs, cs] = dyv * m
                dm = dyv * u[rs, cs]
                dvl_s[rs, cs] = _bdot(wc, dm, "tn")
                dws_ref[g] += jnp.where(tri, _bdot(dm, blk, "nt"), 0.0)
                dbacc_s[:, cs] += dm

        dvl = dvl_s[...]
        dlg_ref[...] += _colsum(dvl * vn)
        dlb_ref[...] += _colsum(dvl)
        dvn = dvl * lng_v
        dvv = rstd * (dvn - _mean(dvn) - vn * _mean(dvn * vn))
        gp = _gelu_grad(zv)
        dz_ref[:, :SW] = (du_s[...] * gp[:, :SW]).astype(BF16)
        dz_ref[:, SW:] = (dvv * gp[:, SW:]).astype(BF16)

        @pl.when(i == nt - 1)
        def _():
            lane = lax.broadcasted_iota(jnp.int32, (SGU_CHUNK, LANE), 1)
            out = jnp.zeros((SGU_CHUNK, LANE), F32)
            for g in range(G):
                col = jnp.sum(dbacc_s[:, g * SGU_GROUP:(g + 1) * SGU_GROUP], axis=1, keepdims=True)
                out = jnp.where(lane == g, col, out)
            db_ref[...] = out

    whole = lambda arr: pl.BlockSpec(arr.shape, lambda i, nd=arr.ndim: (0,) * nd)
    acc_shapes = [(1, SW), (1, SW), ws.shape, (SGU_CHUNK, LANE)]
    return pl.pallas_call(
        body, name=name, grid=(nt,),
        in_specs=[pl.BlockSpec((tr, 2 * SW), lambda i: (i, zblock)), pl.BlockSpec((tr, SW), lambda i: (i, 0)),
                  whole(lng), whole(lnb), whole(ws), whole(bexp), pl.BlockSpec(memory_space=pl.ANY)],
        out_specs=([pl.BlockSpec((tr, 2 * SW), lambda i: (i, zblock))]
                   + [pl.BlockSpec(s, lambda i, nd=len(s): (0,) * nd) for s in acc_shapes]),
        out_shape=[jax.ShapeDtypeStruct(dproj.shape, BF16)] + [jax.ShapeDtypeStruct(s, F32) for s in acc_shapes],
        scratch_shapes=[pltpu.VMEM((tr, SW), F32), pltpu.VMEM((tr, SW), F32), pltpu.VMEM((SGU_CHUNK, SW), F32)],
        input_output_aliases={6: 0}, compiler_params=_params())(proj, dyb, lng, lnb, ws, bexp, dproj)


_HBM = pl.BlockSpec(memory_space=pltpu.HBM)
_SEM = pl.BlockSpec(memory_space=pltpu.SEMAPHORE)
_DATAFLOW = pltpu.SideEffectType.DATAFLOW_SIDE_EFFECTING


def _mesh_place(chips=False):
    x, y, c = lax.axis_index("x"), lax.axis_index("y"), lax.axis_index("c")
    return x, y, c, (2 * x + y if chips else 4 * x + 2 * y + c)


def _peer(x, y, c, rel, chips=False):
    px = 1 - x if rel & 4 else x
    py = 1 - y if rel & 2 else y
    pc = 1 - c if rel & 1 else c
    return (px, py, pc), (2 * px + py if chips else 4 * px + 2 * py + pc)


ALL_PEERS = tuple(range(1, N_DEV))
SIBLING = (1,)
SAME_CORE = (2, 4, 6)
SIBLINGS_CORE = (3, 5, 7)


def _exchange_start(groups, name, rels=ALL_PEERS, chips=False):
    flat = [t for g in groups for t in g]
    sizes = [len(g) for g in groups]
    n, ng = len(flat), len(groups)
    srcs = [pltpu.with_memory_space_constraint(a, pltpu.HBM) for a, _ in flat]
    lands = [pltpu.with_memory_space_constraint(lax.empty(((N_DEV,) + a.shape) if isg else a.shape, a.dtype), pltpu.HBM)
             for a, isg in flat]

    def body(*refs):
        ins, lnd, sems, token = refs[:n], refs[n:2 * n], refs[2 * n:2 * n + 3 * ng], refs[-1]
        x, y, c, me = _mesh_place(chips)
        j0 = 0
        for gi, sz in enumerate(sizes):
            for rel in rels:
                dev, slot = _peer(x, y, c, rel, chips)
                for jj in range(sz):
                    j = j0 + jj
                    pltpu.make_async_remote_copy(
                        src_ref=ins[j] if flat[j][1] else ins[j].at[slot], dst_ref=lnd[j].at[me],
                        send_sem=sems[3 * gi].at[jj * (N_DEV - 1) + rel - 1], recv_sem=sems[3 * gi + 1].at[jj * (N_DEV - 1) + rel - 1],
                        device_id=dev, device_id_type=pl.DeviceIdType.MESH).start()
            for jj in range(sz):
                j = j0 + jj
                pltpu.make_async_copy(ins[j] if flat[j][1] else ins[j].at[me], lnd[j].at[me], sems[3 * gi + 2].at[jj]).start()
            j0 += sz
        token[...] = jnp.zeros_like(token)

    sem_shapes = [pltpu.SemaphoreType.DMA((k,)) for sz in sizes for k in (sz * (N_DEV - 1), sz * (N_DEV - 1), sz)]
    res = pl.pallas_call(
        body, name=name,
        out_shape=(*sem_shapes, *[pltpu.HBM(a.shape, a.dtype) for a in srcs], *[pltpu.HBM(a.shape, a.dtype) for a in lands],
                   jax.ShapeDtypeStruct((SUBLANE, LANE), F32)),
        in_specs=[_HBM] * (2 * n), out_specs=(*[_SEM] * (3 * ng), *[_HBM] * (2 * n), pl.BlockSpec(memory_space=pltpu.VMEM)),
        input_output_aliases={i: 3 * ng + i for i in range(2 * n)},
        compiler_params=pltpu.CompilerParams(has_side_effects=_DATAFLOW))(*srcs, *lands)
    sems, thru, token = res[:3 * ng], res[3 * ng:3 * ng + 2 * n], res[-1]
    handle, j0 = [], 0
    for gi, sz in enumerate(sizes):
        handle.append(dict(kinds=[k for _, k in groups[gi]], chips=chips, srcs=list(thru[j0:j0 + sz]), lands=list(thru[n + j0:n + j0 + sz]),
                           sems=list(sems[3 * gi:3 * gi + 3])))
        j0 += sz
    return handle, token


def _exchange_wait(group, after, name, rels=ALL_PEERS, local=True):
    kinds, sz = group["kinds"], len(group["kinds"])
    relay = group.get("relay", [])

    def body(*refs):
        ins, lnd, (ssem, rsem, lsem) = refs[:sz], refs[sz:2 * sz], refs[2 * sz:2 * sz + 3]
        x, y, c, me = _mesh_place(group["chips"])
        for rel in rels:
            dev, slot = _peer(x, y, c, rel, group["chips"])
            for jj in range(sz):
                cp = pltpu.make_async_remote_copy(
                    src_ref=ins[jj] if kinds[jj] else ins[jj].at[slot], dst_ref=lnd[jj].at[slot],
                    send_sem=ssem.at[jj * (N_DEV - 1) + rel - 1], recv_sem=rsem.at[jj * (N_DEV - 1) + rel - 1],
                    device_id=dev, device_id_type=pl.DeviceIdType.MESH)
                cp.wait_send()
                cp.wait_recv()
        if local:
            for jj in range(sz):
                pltpu.make_async_copy(ins[jj] if kinds[jj] else ins[jj].at[me], lnd[jj].at[me], lsem.at[jj]).wait()
        if relay:
            fsend, frecv = refs[2 * sz + 3:2 * sz + 5]
            dev = _peer(x, y, c, 1)[0]
            for q, (mine, theirs) in enumerate(zip(SAME_CORE, SIBLINGS_CORE)):
                for jj in range(sz):
                    cp = pltpu.make_async_remote_copy(
                        src_ref=lnd[jj].at[_peer(x, y, c, mine)[1]], dst_ref=lnd[jj].at[_peer(x, y, c, theirs)[1]],
                        send_sem=fsend.at[jj * len(SAME_CORE) + q], recv_sem=frecv.at[jj * len(SAME_CORE) + q],
                        device_id=dev, device_id_type=pl.DeviceIdType.MESH)
                    cp.wait_send()
                    cp.wait_recv()

    arrays = group["srcs"] + group["lands"]
    sems = group["sems"] + relay
    res = pl.pallas_call(
        body, name=name, out_shape=[pltpu.HBM(a.shape, a.dtype) for a in arrays],
        in_specs=[_HBM] * (2 * sz) + [_SEM] * len(sems) + [pl.BlockSpec(memory_space=pl.ANY)], out_specs=[_HBM] * (2 * sz),
        input_output_aliases={i: i for i in range(2 * sz)},
        compiler_params=pltpu.CompilerParams(has_side_effects=_DATAFLOW))(*arrays, *sems, after)
    return dict(group, srcs=list(res[:sz]), lands=list(res[sz:]), relay=[])


def _relay_start(group, name):
    sz = len(group["kinds"])
    nq = len(SAME_CORE)

    def body(*refs):
        lnd, fsend, frecv, token = refs[:sz], refs[sz], refs[sz + 1], refs[-1]
        x, y, c, _ = _mesh_place()
        dev = _peer(x, y, c, 1)[0]
        for q, rel in enumerate(SAME_CORE):
            slot = _peer(x, y, c, rel)[1]
            for jj in range(sz):
                pltpu.make_async_remote_copy(
                    src_ref=lnd[jj].at[slot], dst_ref=lnd[jj].at[slot], send_sem=fsend.at[jj * nq + q], recv_sem=frecv.at[jj * nq + q],
                    device_id=dev, device_id_type=pl.DeviceIdType.MESH).start()
        token[...] = jnp.zeros_like(token)

    lands = group["lands"]
    res = pl.pallas_call(
        body, name=name,
        out_shape=(pltpu.SemaphoreType.DMA((sz * nq,)), pltpu.SemaphoreType.DMA((sz * nq,)), *[pltpu.HBM(a.shape, a.dtype) for a in lands],
                   jax.ShapeDtypeStruct((SUBLANE, LANE), F32)),
        in_specs=[_HBM] * sz, out_specs=(_SEM, _SEM, *[_HBM] * sz, pl.BlockSpec(memory_space=pltpu.VMEM)),
        input_output_aliases={i: 2 + i for i in range(sz)},
        compiler_params=pltpu.CompilerParams(has_side_effects=_DATAFLOW))(*lands)
    return dict(group, lands=list(res[2:2 + sz]), relay=[res[0], res[1]]), res[-1]


def _sibling_swap(arrays, handle, after, name):
    start = handle is None
    n = len(arrays) if start else len(handle["srcs"])
    chips = N_DEV // 2
    if start:
        srcs = [pltpu.with_memory_space_constraint(a.reshape(chips, 2, *a.shape[1:]), pltpu.HBM) for a in arrays]
        lands = [pltpu.with_memory_space_constraint(lax.empty((chips,) + a.shape[1:], a.dtype), pltpu.HBM) for a in arrays]
    else:
        srcs, lands = handle["srcs"], handle["lands"]

    def body(*refs):
        ins, lnd, ssem, rsem = refs[:n], refs[n:2 * n], refs[2 * n], refs[2 * n + 1]
        x, y, c, _ = _mesh_place()
        dev = _peer(x, y, c, 1)[0]
        for q in range(chips):
            for j in range(n):
                cp = pltpu.make_async_remote_copy(
                    src_ref=ins[j].at[q, 1 - c], dst_ref=lnd[j].at[q], send_sem=ssem.at[j * chips + q], recv_sem=rsem.at[j * chips + q],
                    device_id=dev, device_id_type=pl.DeviceIdType.MESH)
                if start:
                    cp.start()
                else:
                    cp.wait_send()
                    cp.wait_recv()
        if start:
            refs[-1][...] = jnp.zeros_like(refs[-1])

    thru = [pltpu.HBM(a.shape, a.dtype) for a in srcs + lands]
    effect = pltpu.CompilerParams(has_side_effects=_DATAFLOW)
    if start:
        res = pl.pallas_call(
            body, name=name, out_shape=(pltpu.SemaphoreType.DMA((n * chips,)), pltpu.SemaphoreType.DMA((n * chips,)), *thru,
                                        jax.ShapeDtypeStruct((SUBLANE, LANE), F32)),
            in_specs=[_HBM] * (2 * n), out_specs=(_SEM, _SEM, *[_HBM] * (2 * n), pl.BlockSpec(memory_space=pltpu.VMEM)),
            input_output_aliases={i: 2 + i for i in range(2 * n)}, compiler_params=effect)(*srcs, *lands)
        return dict(srcs=list(res[2:2 + n]), lands=list(res[2 + n:2 + 2 * n]), sems=[res[0], res[1]]), res[-1]
    res = pl.pallas_call(
        body, name=name, out_shape=thru, in_specs=[_HBM] * (2 * n) + [_SEM, _SEM, pl.BlockSpec(memory_space=pl.ANY)],
        out_specs=[_HBM] * (2 * n), input_output_aliases={i: i for i in range(2 * n)}, compiler_params=effect)(
            *srcs, *lands, *handle["sems"], after)
    return dict(handle, srcs=list(res[:n]), lands=list(res[n:]))


def _pair_add(mine, theirs, core, name):
    chips, _, rows, w = mine.shape
    tm = _pick(rows, (256, 128, 64, 32, 16))

    def body(core_ref, a_ref, b_ref, o_ref):
        o_ref[...] = (a_ref[...].astype(F32) + b_ref[...].astype(F32)).astype(o_ref.dtype)

    return pl.pallas_call(
        body, name=name, out_shape=jax.ShapeDtypeStruct(theirs.shape, theirs.dtype),
        grid_spec=pltpu.PrefetchScalarGridSpec(
            num_scalar_prefetch=1, grid=(chips, rows // tm),
            in_specs=[pl.BlockSpec((None, None, tm, w), lambda q, i, core_ref: (q, core_ref[0], i, 0)),
                      pl.BlockSpec((None, tm, w), lambda q, i, core_ref: (q, i, 0))],
            out_specs=pl.BlockSpec((None, tm, w), lambda q, i, core_ref: (q, i, 0))),
        compiler_params=_params())(core, mine, theirs)


def _adamw(w, m, v, gparts, name, after=None):
    R, C = w.shape
    tm = _pick(R, (256, 128, 64, 32, 16, 8))
    order = [] if after is None else [after]

    def body(w_ref, m_ref, v_ref, g_ref, *rest):
        go, do, mo, vo = rest[len(order):]
        g = g_ref[0].astype(F32)
        for j in range(1, gparts.shape[0]):
            g = g + g_ref[j].astype(F32)
        mn = ADAM_B1 * m_ref[...] + (1.0 - ADAM_B1) * g
        vn = ADAM_B2 * v_ref[...] + (1.0 - ADAM_B2) * (g * g)
        m_hat = mn / (1.0 - ADAM_B1 ** ADAM_STEP)
        v_hat = vn / (1.0 - ADAM_B2 ** ADAM_STEP)
        go[...] = g
        do[...] = -ADAM_LR * (m_hat / (jnp.sqrt(v_hat) + ADAM_EPS) + ADAM_WD * w_ref[...])
        mo[...] = mn
        vo[...] = vn

    row = pl.BlockSpec((tm, C), lambda i: (i, 0))
    return pl.pallas_call(
        body, name=name, grid=(R // tm,),
        in_specs=[row, row, row, pl.BlockSpec((gparts.shape[0], tm, C), lambda i: (0, i, 0))] + [pl.BlockSpec(memory_space=pl.ANY)] * len(order),
        out_specs=[row] * 4, out_shape=[jax.ShapeDtypeStruct((R, C), F32)] * 4, compiler_params=_params())(w, m, v, gparts, *order)


def _pack(arrays):
    parts = []
    for a in arrays:
        f = a.reshape(1, -1)
        pad = _ceil_to(f.shape[1], SUBLANE * LANE) - f.shape[1]
        f = jnp.concatenate([f, jnp.zeros((1, pad), f.dtype)], axis=1) if pad else f
        parts.append(f.reshape(-1, LANE))
    rows = sum(p.shape[0] for p in parts)
    pad = _ceil_to(rows, 64) - rows
    return jnp.concatenate(parts + ([jnp.zeros((pad, LANE), parts[0].dtype)] if pad else []), axis=0)


def _unpack(buf, shapes):
    out, row = [], 0
    for s in shapes:
        size = 1
        for d in s:
            size *= d
        rows = _ceil_to(size, SUBLANE * LANE) // LANE
        out.append(buf[row:row + rows].reshape(1, -1)[:, :size].reshape(s))
        row += rows
    return out


def kernel(x, norm_mix_g, w_in, shift_mu, w0, w_lora_up, a0, a_lora_up, g_lora_up, k_k, k_a, r_k, lnx_g, lnx_b, w_proj_rwkv, sgu_ln_g, sgu_ln_b, sgu_w, sgu_b, w_proj_sgu, w_out, norm_ffn_g, w_ffn_gate, w_ffn_up, w_ffn_down, norm_final_g, loss_target, m_norm_mix_g, m_w_in, m_shift_mu, m_w0, m_w_lora_up, m_a0, m_a_lora_up, m_g_lora_up, m_k_k, m_k_a, m_r_k, m_lnx_g, m_lnx_b, m_w_proj_rwkv, m_sgu_ln_g, m_sgu_ln_b, m_sgu_w, m_sgu_b, m_w_proj_sgu, m_w_out, m_norm_ffn_g, m_w_ffn_gate, m_w_ffn_up, m_w_ffn_down, m_norm_final_g, v_norm_mix_g, v_w_in, v_shift_mu, v_w0, v_w_lora_up, v_a0, v_a_lora_up, v_g_lora_up, v_k_k, v_k_a, v_r_k, v_lnx_g, v_lnx_b, v_w_proj_rwkv, v_sgu_ln_g, v_sgu_ln_b, v_sgu_w, v_sgu_b, v_w_proj_sgu, v_w_out, v_norm_ffn_g, v_w_ffn_gate, v_w_ffn_up, v_w_ffn_down, v_norm_final_g):
    weights = dict(norm_mix_g=norm_mix_g, w_in=w_in, shift_mu=shift_mu, w0=w0, w_lora_up=w_lora_up, a0=a0, a_lora_up=a_lora_up,
                   g_lora_up=g_lora_up, k_k=k_k, k_a=k_a, r_k=r_k, lnx_g=lnx_g, lnx_b=lnx_b, w_proj_rwkv=w_proj_rwkv,
                   sgu_ln_g=sgu_ln_g, sgu_ln_b=sgu_ln_b, sgu_w=sgu_w, sgu_b=sgu_b, w_proj_sgu=w_proj_sgu, w_out=w_out,
                   norm_ffn_g=norm_ffn_g, w_ffn_gate=w_ffn_gate, w_ffn_up=w_ffn_up, w_ffn_down=w_ffn_down, norm_final_g=norm_final_g)
    m_in = dict(norm_mix_g=m_norm_mix_g, w_in=m_w_in, shift_mu=m_shift_mu, w0=m_w0, w_lora_up=m_w_lora_up, a0=m_a0,
                a_lora_up=m_a_lora_up, g_lora_up=m_g_lora_up, k_k=m_k_k, k_a=m_k_a, r_k=m_r_k, lnx_g=m_lnx_g, lnx_b=m_lnx_b,
                w_proj_rwkv=m_w_proj_rwkv, sgu_ln_g=m_sgu_ln_g, sgu_ln_b=m_sgu_ln_b, sgu_w=m_sgu_w, sgu_b=m_sgu_b,
                w_proj_sgu=m_w_proj_sgu, w_out=m_w_out, norm_ffn_g=m_norm_ffn_g, w_ffn_gate=m_w_ffn_gate, w_ffn_up=m_w_ffn_up,
                w_ffn_down=m_w_ffn_down, norm_final_g=m_norm_final_g)
    v_in = dict(norm_mix_g=v_norm_mix_g, w_in=v_w_in, shift_mu=v_shift_mu, w0=v_w0, w_lora_up=v_w_lora_up, a0=v_a0,
                a_lora_up=v_a_lora_up, g_lora_up=v_g_lora_up, k_k=v_k_k, k_a=v_k_a, r_k=v_r_k, lnx_g=v_lnx_g, lnx_b=v_lnx_b,
                w_proj_rwkv=v_w_proj_rwkv, sgu_ln_g=v_sgu_ln_g, sgu_ln_b=v_sgu_ln_b, sgu_w=v_sgu_w, sgu_b=v_sgu_b,
                w_proj_sgu=v_w_proj_sgu, w_out=v_w_out, norm_ffn_g=v_norm_ffn_g, w_ffn_gate=v_w_ffn_gate, w_ffn_up=v_w_ffn_up,
                w_ffn_down=v_w_ffn_down, norm_final_g=v_norm_final_g)
    names = list(weights)
    col_sharded = ("w_in", "w_lora_up", "a_lora_up", "g_lora_up", "w_proj_rwkv", "w_proj_sgu", "w_ffn_gate", "w_ffn_up")
    row_sharded = ("w_out", "w_ffn_down")
    sharded = [n for n in names if n in col_sharded or n in row_sharded]
    small = [n for n in names if n not in sharded]

    xs, tgt = x[0], loss_target[0]
    T, D = xs.shape
    RW = w0.shape[1]
    H = RW // HEAD
    SW = sgu_ln_g.shape[1]
    G = sgu_w.shape[1]
    assert 2 * SW == D, "the projection layout takes the SGU part to be as wide as a gate"
    lay = _rwkv_layout(RW, w_lora_up.shape[1], a_lora_up.shape[1], g_lora_up.shape[1], D)
    _, pw, _, rcp = lay
    icp = rcp + 3 * D
    b_ga, b_gb, b_z = rcp // D, rcp // D + 1, rcp // D + 2

    gather_groups = dict(win=["w_in", "w_lora_up", "a_lora_up", "g_lora_up"], proj=["w_proj_rwkv", "w_proj_sgu", "w_out"],
                         ffn_gate_up=["w_ffn_gate", "w_ffn_up"], ffn_down=["w_ffn_down"])
    handles, gather_token = _exchange_start([[(weights[n][0].astype(BF16), True) for n in grp] for grp in gather_groups.values()],
                                            "gather_start", rels=SIBLING + SAME_CORE)
    gather = dict(zip(gather_groups, handles))
    full = {}
    relay_tokens = {}
    joined = lambda g: g.transpose(1, 0, 2).reshape(g.shape[1], -1)

    def relay_weights(key, after):
        arrived = _exchange_wait(gather[key], after, "gather_wait_ici_" + key, rels=SAME_CORE, local=False)
        gather[key], relay_tokens[key] = _relay_start(arrived, "gather_relay_" + key)

    def take_weights(key, after):
        done = _exchange_wait(gather[key], after, "gather_wait_d2d_" + key, rels=SIBLING)
        for n, g in zip(gather_groups[key], done["lands"]):
            full[n] = g.reshape(-1, g.shape[2]) if n in row_sharded else g

    packed = [_pack([d[n] for n in small] + [gather_token]) for d in (weights, m_in, v_in)]
    n1 = _rms_fwd(xs, norm_mix_g, "rms_mix", deps=[gather_token, *packed])
    relay_weights("win", n1)
    take_weights("win", relay_tokens["win"])
    W_in = _w_in_to_proj(full["w_in"], lay, D, "w_in_layout")
    lora = [_pad_rows(joined(full[n]), rows) for n, rows in zip(("w_lora_up", "a_lora_up", "g_lora_up"), pw[3:])]
    mu_p = _pad_rwkv_cols(shift_mu, lay)
    rsmall = [w0, a0, k_k, k_a]
    hp = [lnx_g.reshape(H, 1, HEAD), lnx_b.reshape(H, 1, HEAD), r_k.reshape(H, 1, HEAD)]
    ws = sgu_w[0]
    bexp = jnp.repeat(sgu_b[0].T, SGU_GROUP, axis=1)
    gf = norm_final_g.reshape(1, D)

    proj = _matmul(n1, W_in, mode="nn", out_dtype=F32, name="proj_in")
    ga, gb = (proj, D, b_ga), (proj, D, b_gb)
    r_h, lw_h, k2_h, v_h, aa_h, bb_h, g_h = _rwkv_pre(proj, mu_p, rsmall, lora, lay, "rwkv_pre")
    wkv_in = [r_h, lw_h, k2_h, v_h, aa_h, bb_h]
    y_h, *wkv_saved = _wkv_fwd(*wkv_in, "wkv_fwd")
    relay_weights("proj", y_h)
    ya = _head_post(y_h, r_h, k2_h, v_h, g_h, hp, "head_post", deps=[relay_tokens["proj"]])
    relay_weights("ffn_gate_up", ya)
    yb = _sgu_fwd(proj, b_z, sgu_ln_g, sgu_ln_b, ws, bexp, "sgu_fwd")
    take_weights("proj", ya)
    pa = _matmul(ya, full["w_proj_rwkv"], mode="nn", out_dtype=F32, name="proj_a", deps=[relay_tokens["ffn_gate_up"]])

    def merge_fn(pb_v, pa_v, ga_v, gb_v):
        return pb_v, _sigmoid(ga_v) * pa_v + _sigmoid(gb_v) * pb_v
    pb, merged = _matmul(yb, full["w_proj_sgu"], mode="nn", name="proj_b_merge",
                         epi=(merge_fn, [pa, (proj, b_ga * D), (proj, b_gb * D)], [F32, BF16]))
    h1 = _matmul(merged, full["w_out"], mode="nn", out_dtype=F32, name="out_proj", add=xs)
    n2 = _rms_fwd(h1, norm_ffn_g, "rms_ffn")
    relay_weights("ffn_down", n2)
    take_weights("ffn_gate_up", n2)

    def act_fn(gt_v, up_v):
        return gt_v, up_v, gt_v * _sigmoid(gt_v) * up_v
    gt, up, act = _matmul(n2, full["w_ffn_gate"], b2=full["w_ffn_up"], mode="nn", name="ffn_gate_up_act", out_blocks=N_DEV,
                          epi=(act_fn, [], [BF16, BF16, BF16]), deps=[relay_tokens["ffn_down"]])
    take_weights("ffn_down", act)
    h2 = _matmul(act, full["w_ffn_down"], mode="nn", out_dtype=F32, name="ffn_down", add=h1)

    def final_fn(rv, pv):
        (h_v, t_v), (g_v,) = rv, pv
        r = lax.rsqrt(_mean(h_v * h_v) + RMS_EPS)
        yn = h_v * r
        e = yn * g_v - t_v
        loss = 0.5 * jnp.sum(_mean(e * e))
        dout = e * (1.0 / D)
        dyg = dout * g_v
        dh = r * (dyg - yn * _mean(dyg * yn))
        return [dh, dh], [jnp.full((1, LANE), loss, F32), _colsum(dout * yn)]
    dh2, dh2_bf, loss_part, d_gf = _rowwise(final_fn, [h2, tgt], [gf], [(D, F32), (D, BF16)], [(1, LANE), (1, D)], name="final_loss")

    grads = {}

    def start_scatter(group, name):
        blocks = [(grads[n].reshape(N_DEV, -1, grads[n].shape[1]) if n in row_sharded else grads[n], False) for n in group]
        (handle,), token = _exchange_start([blocks], name)
        return handle, token

    def dact_fn(d_v, gt_v, up_v):
        gt_v, up_v = gt_v.astype(F32), up_v.astype(F32)
        s = _sigmoid(gt_v)
        return d_v * up_v * (s * (1.0 + gt_v * (1.0 - s))), d_v * gt_v * s
    dgt, dup = _matmul(dh2_bf, full["w_ffn_down"], mode="nt", name="d_ffn_act", out_blocks=N_DEV,
                       epi=(dact_fn, [gt, up], [BF16, BF16]))
    scatter_groups = dict(ffn_down=["w_ffn_down"], ffn_gate_up=["w_ffn_gate", "w_ffn_up"],
                          mid=["w_out", "w_proj_rwkv", "w_proj_sgu"], last=["w_in", "w_lora_up", "a_lora_up", "g_lora_up"])
    scatters = {}
    grads["w_ffn_down"] = _matmul(act, dh2_bf, mode="tn", out_dtype=BF16, name="dw_ffn_down")
    scatters["ffn_down"], token = start_scatter(scatter_groups["ffn_down"], "scatter_start_ffn_down")
    dn2 = _matmul(dgt, full["w_ffn_gate"], mode="nt", out_dtype=F32, name="dn2_gate", deps=[token])
    grads["w_ffn_gate"] = _matmul(n2, dgt, mode="tn", out_dtype=BF16, name="dw_ffn_gate", out_blocks=N_DEV)
    grads["w_ffn_up"] = _matmul(n2, dup, mode="tn", out_dtype=BF16, name="dw_ffn_up", out_blocks=N_DEV)
    scatters["ffn_gate_up"], token = start_scatter(scatter_groups["ffn_gate_up"], "scatter_start_ffn_gate_up")
    dn2 = _matmul(dup, full["w_ffn_up"], mode="nt", out_dtype=F32, name="dn2_up", add=dn2, deps=[token])
    dh1, dh1_bf, d_g2 = _rms_bwd(dn2, h1, dh2, norm_ffn_g, "rms_ffn_bwd")
    dmerged = _matmul(dh1_bf, full["w_out"], mode="nt", out_dtype=F32, name="d_merged")
    grads["w_out"] = _matmul(merged, dh1_bf, mode="tn", out_dtype=BF16, name="dw_out")

    def dmerge_fn(rv, pv):
        d_v, ga_v, gb_v, pa_v, pb_v = rv
        sa, sb = _sigmoid(ga_v), _sigmoid(gb_v)
        dgates = jnp.concatenate([d_v * pa_v * sa * (1.0 - sa), d_v * pb_v * sb * (1.0 - sb)], axis=1)
        return [dgates, d_v * sa, d_v * sb], []
    dproj, dpa, dpb = _rowwise(dmerge_fn, [dmerged, ga, gb, pa, pb], [],
                               [(2 * D, BF16, icp, b_ga // 2, None), (D, BF16), (D, BF16)], [], name="d_merge")
    dya = _matmul(dpa, full["w_proj_rwkv"], mode="nt", out_dtype=F32, name="d_ya")
    dyb = _matmul(dpb, full["w_proj_sgu"], mode="nt", out_dtype=F32, name="d_yb")
    grads["w_proj_rwkv"] = _matmul(ya, dpa, mode="tn", out_dtype=BF16, name="dw_proj_a", out_blocks=N_DEV)
    grads["w_proj_sgu"] = _matmul(yb, dpb, mode="tn", out_dtype=BF16, name="dw_proj_b", out_blocks=N_DEV)
    scatters["mid"], token_mid = start_scatter(scatter_groups["mid"], "scatter_start_mid")
    dproj, d_lng, d_lnb, d_ws, d_bs = _sgu_bwd(proj, b_z, dyb, sgu_ln_g, sgu_ln_b, ws, bexp, dproj, "sgu_bwd")

    dr_h, dlw_h, dk2_h, dv_h, daa, dbb, dg_h, d_lnxg, d_lnxb, d_rk = _wkv_bwd(
        *wkv_in, *wkv_saved, y_h, g_h, hp, dya, "wkv_bwd", deps=[token_mid])
    dproj, d_mu, d_w0, d_a0, d_kk, d_ka, d_wlw, d_wla, d_wlg = _rwkv_pre_bwd(
        proj, mu_p, rsmall, lora, [dr_h, dk2_h, dv_h, dlw_h, daa, dbb, dg_h], dproj, lay, "rwkv_pre_bwd")
    split = lambda g: g.reshape(g.shape[0], N_DEV, -1).transpose(1, 0, 2)
    grads["w_in"] = _dw_in_from_proj(_matmul(n1, dproj, mode="tn", out_dtype=BF16, name="dw_in"), lay, D, w_in.shape[2], "dw_in_layout")
    grads["w_lora_up"] = split(d_wlw[:w_lora_up.shape[1]].astype(BF16))
    grads["a_lora_up"] = split(d_wla[:a_lora_up.shape[1]].astype(BF16))
    grads["g_lora_up"] = split(d_wlg[:g_lora_up.shape[1]].astype(BF16))
    out = {}

    def update_group(key, after):
        handle = scatters[key]
        parts = _exchange_wait(handle, after, "scatter_wait_" + key, rels=SAME_CORE if handle["chips"] else ALL_PEERS)["lands"]
        for n, part in zip(scatter_groups[key], parts):
            res = _adamw(weights[n][0], m_in[n][0], v_in[n][0], part, "adamw_" + n, after=after)
            out[n] = [t.reshape(weights[n].shape) for t in res]
            after = res[0]
        return after

    swap, token_swap = _sibling_swap([grads[n] for n in scatter_groups["last"]], None, None, "scatter_last_swap_start")
    after = update_group("mid", update_group("ffn_down", token_swap))
    swap = _sibling_swap(None, swap, after, "scatter_last_swap_wait")
    core = lax.axis_index("c").astype(jnp.int32).reshape(1)
    chip_sums = [_pair_add(mine, theirs, core, "scatter_last_add_" + n)
                 for n, mine, theirs in zip(scatter_groups["last"], swap["srcs"], swap["lands"])]
    (scatters["last"],), token_in = _exchange_start([[(s, False) for s in chip_sums]], "scatter_start_last", rels=SAME_CORE, chips=True)
    dn1 = _matmul(dproj, W_in, mode="nt", out_dtype=F32, name="dn1", deps=[token_in])
    dx, _, d_g1 = _rms_bwd(dn1, xs, dh1, norm_mix_g, "rms_mix_bwd")
    small_grads = dict(norm_mix_g=d_g1, shift_mu=_unpad_rwkv_cols(d_mu, lay), w0=d_w0, a0=d_a0, k_k=d_kk, k_a=d_ka, r_k=d_rk,
                       lnx_g=d_lnxg, lnx_b=d_lnxb, sgu_ln_g=d_lng, sgu_ln_b=d_lnb, sgu_w=d_ws, sgu_b=d_bs[:, :G].T,
                       norm_ffn_g=d_g2, norm_final_g=d_gf)
    (gather_small,), after = _exchange_start([[(_pack([small_grads[n] for n in small] + [jnp.zeros_like(gather_token)]), True)]],
                                             "gather_small_start")
    for key in ("ffn_gate_up", "last"):
        after = update_group(key, after)
    small_parts = _exchange_wait(gather_small, after, "gather_small_wait")["lands"][0]
    res = _adamw(*packed, small_parts, "adamw_small")
    unpacked = [_unpack(t, [weights[n].shape for n in small]) for t in res]
    for i, n in enumerate(small):
        out[n] = [u[i] for u in unpacked]

    loss = lax.psum(loss_part[0, 0], ("x", "y", "c"))
    return (loss, dx[None], *[out[n][0] for n in names], *[out[n][1] for n in names],
            *[out[n][2] for n in names], *[out[n][3] for n in names])
```

```python
import jax
import jax.numpy as jnp
from jax import lax
from jax.experimental import pallas as pl
from jax.experimental.pallas import tpu as pltpu

F32 = jnp.float32
BF16 = jnp.bfloat16

N_DEV = 8
LANE = 128
SUBLANE = 8
HEAD = 64
SGU_CHUNK = 128
SGU_GROUP = 128
WKV_CHUNK = 64
RMS_EPS = 1e-6
LN_EPS = 1e-5
LNX_EPS = 64e-5
ADAM_LR, ADAM_B1, ADAM_B2, ADAM_EPS, ADAM_WD, ADAM_STEP = 0.001, 0.9, 0.999, 1e-08, 0.01, 10
VMEM_LIMIT_BYTES = 48 * 1024 * 1024
_SQRT_HALF = 0.7071067811865476
_INV_SQRT_2PI = 0.3989422804014327


def _pick(n, cands):
    for c in cands:
        if n % c == 0:
            return c
    return n


def _ceil_to(n, m):
    return -(-n // m) * m


def _params():
    return pltpu.CompilerParams(vmem_limit_bytes=VMEM_LIMIT_BYTES)


def _tile(n, cap):
    best = 0
    for d in range(LANE, min(n, cap) + 1, LANE):
        if n % d == 0:
            best = d
    return best or n


def _matmul_tiles(M, N, K, a_bytes, b_bytes, o_bytes, has_add, forced):
    tm = forced.get("m") or _tile(M, 1024)
    tn = forced.get("n") or _tile(N, 1024)
    tk = forced.get("k") or _tile(K, 2048)

    def vmem(tm, tn, tk):
        acc = tm * tn * 4 if tk < K else 0
        return 2 * (tm * tk * a_bytes + tk * tn * b_bytes + tm * tn * (o_bytes + (4 if has_add else 0))) + acc

    while vmem(tm, tn, tk) > (VMEM_LIMIT_BYTES * 3) // 4:
        if "k" not in forced and tk > 512 and _tile(K, tk // 2) < tk:
            tk = _tile(K, tk // 2)
        elif "m" not in forced and _tile(M, tm // 2) < tm:
            tm = _tile(M, tm // 2)
        else:
            break
    return tm, tn, tk


def _matmul(a, b, *, mode, out_dtype=F32, name, add=None, deps=(), out_blocks=0, epi=None, b2=None):
    def view(x):
        return (x.shape[1], x.shape[0] * x.shape[2], x.shape[2]) if x.ndim == 3 else (x.shape[0], x.shape[1], 0)

    (ar, ac, aw), (br, bc, bw) = view(a), view(b)
    a_col, b_col = {"nn": ("k", "n"), "nt": ("k", "k"), "tn": ("m", "n")}[mode]
    if mode == "nn":
        M, K, K2, N = ar, ac, br, bc
    elif mode == "nt":
        M, K, N, K2 = ar, ac, br, bc
    else:
        K, M, K2, N = ar, ac, br, bc
    assert K == K2, (a.shape, b.shape, mode)
    forced = {}
    for dim, w in ((a_col, aw), (b_col, bw), ("n", N // out_blocks if out_blocks else 0)):
        if w:
            assert forced.get(dim, w) == w
            forced[dim] = w
    has_add = add is not None
    tile_bytes = (sum(jnp.dtype(d).itemsize for d in epi[2]) + sum((e[0] if isinstance(e, tuple) else e).dtype.itemsize for e in epi[1])
                  if epi is not None else jnp.dtype(out_dtype).itemsize)
    tm, tn, tk = _matmul_tiles(M, N, K, a.dtype.itemsize, b.dtype.itemsize, tile_bytes, has_add, forced)
    kb = 1
    if "k" in forced and mode != "tn":
        lanes_ok = all(w or tk % LANE == 0 for w in (aw, bw if mode == "nt" else 1))
        kb = next(c for c in (4, 2, 1) if (K // tk) % c == 0 and (c == 1 or (lanes_ok and c * tk <= 1536)))
    nk = K // (tk * kb)
    dn = {"nn": (((1,), (0,)), ((), ())), "nt": (((1,), (1,)), ((), ())), "tn": (((0,), (0,)), ((), ()))}[mode]
    pick = {"m": lambda i, j, k: i, "n": lambda i, j, k: j, "k": lambda i, j, k: k}
    size = {"m": tm, "n": tn, "k": tk}

    def spec(blocked, row_dim, col_dim):
        rf, cf = pick[row_dim], pick[col_dim]
        reps = {d: (kb if d == "k" else 1) for d in (row_dim, col_dim)}
        if blocked:
            lead = kb if col_dim == "k" and kb > 1 else None
            return pl.BlockSpec((lead, size[row_dim], size[col_dim]), lambda i, j, k: (cf(i, j, k), rf(i, j, k), 0))
        return pl.BlockSpec((size[row_dim] * reps[row_dim], size[col_dim] * reps[col_dim]), lambda i, j, k: (rf(i, j, k), cf(i, j, k)))

    def k_part(ref, blocked, k_on_rows, j):
        if kb == 1:
            return ref[...]
        if blocked:
            return ref[j]
        return ref[j * tk:(j + 1) * tk, :] if k_on_rows else ref[:, j * tk:(j + 1) * tk]

    a_spec = spec(aw, "k" if mode == "tn" else "m", a_col)
    b_spec = spec(bw, "n" if mode == "nt" else "k", b_col)
    o_spec = spec(out_blocks, "m", "n")
    epi_fn, epi_ins, epi_dtypes = epi if epi is not None else (None, [], [out_dtype])
    epi_ins = [e if isinstance(e, tuple) else (e, None) for e in epi_ins]
    n_epi = len(epi_ins)
    twin = b2 is not None
    assert not twin or (nk == 1 and kb == 1 and epi is not None and b2.shape == b.shape)
    n_in = 2 + twin + has_add + n_epi + len(deps)
    n_out = len(epi_dtypes)

    def body(*refs):
        a_ref, b_ref = refs[0], refs[1]
        add_ref = refs[2 + twin] if has_add else None
        epi_refs = refs[2 + twin + has_add:2 + twin + has_add + n_epi]
        o_refs = refs[n_in:n_in + n_out]
        part = None
        for q in range(kb):
            a_q = k_part(a_ref, aw and a_col == "k", False, q)
            b_q = k_part(b_ref, bw and b_col == "k", mode == "nn", q)
            prod = lax.dot_general(a_q.astype(BF16), b_q.astype(BF16), dn, preferred_element_type=F32)
            part = prod if part is None else part + prod
        second = [lax.dot_general(a_ref[...].astype(BF16), refs[2][...].astype(BF16), dn, preferred_element_type=F32)] if twin else []

        def finish(res):
            outs = epi_fn(res, *second, *[e[...] for e in epi_refs]) if epi_fn is not None else (res,)
            for o_ref, val in zip(o_refs, outs):
                o_ref[...] = val.astype(o_ref.dtype)

        if nk == 1:
            finish(part + add_ref[...] if has_add else part)
            return
        acc_ref = refs[-1]
        kk = pl.program_id(2)

        @pl.when(kk == 0)
        def _():
            acc_ref[...] = part + add_ref[...] if has_add else part

        @pl.when(kk > 0)
        def _():
            acc_ref[...] += part

        @pl.when(kk == nk - 1)
        def _():
            finish(acc_ref[...])

    def epi_spec(arr, off):
        if off is None:
            return o_spec
        assert off % tn == 0
        return pl.BlockSpec((tm, tn), lambda i, j, k: (i, j + off // tn))

    ins = [a, b] + ([b2] if twin else []) + ([add] if has_add else []) + [arr for arr, _ in epi_ins] + list(deps)
    in_specs = ([a_spec, b_spec] + ([b_spec] if twin else []) + ([o_spec] if has_add else []) + [epi_spec(arr, off) for arr, off in epi_ins]
                + [pl.BlockSpec(d.shape, lambda i, j, k, nd=d.ndim: (0,) * nd) for d in deps])
    o_shape = (out_blocks, M, tn) if out_blocks else (M, N)
    res = pl.pallas_call(
        body, name=name, grid=(M // tm, N // tn, nk), in_specs=in_specs, out_specs=[o_spec] * n_out,
        out_shape=[jax.ShapeDtypeStruct(o_shape, dt) for dt in epi_dtypes],
        scratch_shapes=[pltpu.VMEM((tm, tn), F32)] if nk > 1 else [],
        compiler_params=_params())(*ins)
    return res[0] if epi is None else list(res)


def _rowwise(fn, rows, pars, row_outs, acc_outs, *, name, tm=256, deps=()):
    rows = [r if isinstance(r, tuple) else (r, r.shape[1], 0) for r in rows]
    row_outs = [o if len(o) == 5 else (o[0], o[1], o[0], 0, None) for o in row_outs]
    aliased = [(k, o[4]) for k, o in enumerate(row_outs) if o[4] is not None]
    R = rows[0][0].shape[0]
    if max(w for _, w, _ in rows) > 4096:
        tm = tm // 2
    tm = min(tm, R)
    assert R % tm == 0
    nr, npar = len(rows), len(pars)
    nro = len(row_outs)
    n_in = nr + npar + len(deps) + len(aliased)

    def body(*refs):
        rv = [r[...] for r in refs[:nr]]
        pv = [p[...] for p in refs[nr:nr + npar]]
        outs = refs[n_in:]
        ro, ao = fn(rv, pv)
        first = pl.program_id(0) == 0
        for o_ref, val in zip(outs[:nro], ro):
            o_ref[...] = val.astype(o_ref.dtype)

        @pl.when(first)
        def _():
            for o_ref, val in zip(outs[nro:], ao):
                o_ref[...] = val

        @pl.when(jnp.logical_not(first))
        def _():
            for o_ref, val in zip(outs[nro:], ao):
                o_ref[...] += val

    in_specs = ([pl.BlockSpec((tm, w), lambda i, cb=cb: (i, cb)) for _, w, cb in rows]
                + [pl.BlockSpec(p.shape, lambda i, nd=p.ndim: (0,) * nd) for p in list(pars) + list(deps)]
                + [pl.BlockSpec(memory_space=pl.ANY)] * len(aliased))
    out_shape = ([jax.ShapeDtypeStruct((R, full), dt) for _, dt, full, _, _ in row_outs]
                 + [jax.ShapeDtypeStruct(s, F32) for s in acc_outs])
    out_specs = ([pl.BlockSpec((tm, f), lambda i, cb=cb: (i, cb)) for f, _, _, cb, _ in row_outs]
                 + [pl.BlockSpec(s, lambda i, nd=len(s): (0,) * nd) for s in acc_outs])
    res = pl.pallas_call(body, name=name, grid=(R // tm,), in_specs=in_specs, out_specs=out_specs, out_shape=out_shape,
                         input_output_aliases={n_in - len(aliased) + q: k for q, (k, _) in enumerate(aliased)},
                         compiler_params=_params())(*[r for r, _, _ in rows], *pars, *deps, *[buf for _, buf in aliased])
    return list(res)


def _bdot(a, b, mode="nn"):
    dn = {"nn": (((1,), (0,)), ((), ())), "nt": (((1,), (1,)), ((), ())), "tn": (((0,), (0,)), ((), ()))}[mode]
    return lax.dot_general(a.astype(BF16), b.astype(BF16), dn, preferred_element_type=F32)


def _sigmoid(x):
    return jax.nn.sigmoid(x)


def _softplus(x):
    return jnp.maximum(x, 0.0) + jnp.log1p(jnp.exp(-jnp.abs(x)))


def _gelu(z):
    return 0.5 * z * (1.0 + lax.erf(z * _SQRT_HALF))


def _gelu_grad(z):
    return 0.5 * (1.0 + lax.erf(z * _SQRT_HALF)) + z * jnp.exp(-0.5 * z * z) * _INV_SQRT_2PI


def _mean(x):
    return jnp.mean(x, axis=-1, keepdims=True)


def _colsum(x):
    return jnp.sum(x, axis=0, keepdims=True)


def _rms_fwd(x, g, name, deps=()):
    def fn(rv, pv):
        (xv,), (gv,) = rv, pv
        r = lax.rsqrt(_mean(xv * xv) + RMS_EPS)
        return [xv * r * gv], []
    return _rowwise(fn, [x], [g], [(x.shape[1], BF16)], [], name=name, deps=deps)[0]


def _rms_bwd(dn, x, dres, g, name, deps=()):
    def fn(rv, pv):
        (dnv, xv, drv), (gv,) = rv, pv
        r = lax.rsqrt(_mean(xv * xv) + RMS_EPS)
        yn = xv * r
        dyg = dnv * gv
        dx = drv + r * (dyg - yn * _mean(dyg * yn))
        return [dx, dx], [_colsum(dnv * yn)]
    D = x.shape[1]
    return _rowwise(fn, [dn, x, dres], [g], [(D, F32), (D, BF16)], [(1, D)], name=name, deps=deps)


def _rwkv_layout(RW, Lw, La, Lg, D):
    widths = [RW, RW, RW, Lw, La, Lg]
    pw = [_ceil_to(w, LANE) for w in widths]
    pw[5] += _ceil_to(sum(pw), 2 * D) - sum(pw)
    offs = [sum(pw[:i]) for i in range(6)]
    return widths, pw, offs, sum(pw)


def _pad_rwkv_cols(a, lay):
    widths, pw, _, _ = lay
    pieces, src = [], 0
    for w, p in zip(widths, pw):
        pieces.append(a[:, src:src + w])
        if p > w:
            pieces.append(jnp.zeros((a.shape[0], p - w), a.dtype))
        src += w
    return jnp.concatenate(pieces, axis=1)


def _unpad_rwkv_cols(a, lay):
    widths, _, offs, _ = lay
    return jnp.concatenate([a[:, o:o + w] for o, w in zip(offs, widths)], axis=1)


def _proj_pieces(lay, D, cs):
    widths, _, offs, rcp = lay
    rc = sum(widths)
    segs = [(sum(widths[:j]), widths[j], offs[j]) for j in range(6)] + [(rc, D, rcp + 2 * D), (rc + D, D, rcp), (rc + 2 * D, D, rcp + D)]
    pieces = []
    for start, width, dst in segs:
        n = start
        while n < start + width:
            d, off = divmod(n, cs)
            take = min(cs - off, start + width - n)
            pieces.append((d, off, dst + n - start, take))
            n += take
    return pieces


def _w_in_to_proj(g, lay, D, name):
    nb, rows, cs = g.shape
    icp = lay[3] + 3 * D
    pieces = _proj_pieces(lay, D, cs)
    tm = _pick(rows, (256, 128, 64, 32, 16))

    def body(i_ref, o_ref):
        o_ref[...] = jnp.zeros_like(o_ref)
        for d, src, dst, w in pieces:
            o_ref[:, dst:dst + w] = i_ref[d, :, src:src + w]

    return pl.pallas_call(
        body, name=name, grid=(rows // tm,), in_specs=[pl.BlockSpec((nb, tm, cs), lambda i: (0, i, 0))],
        out_specs=pl.BlockSpec((tm, icp), lambda i: (i, 0)), out_shape=jax.ShapeDtypeStruct((rows, icp), g.dtype),
        compiler_params=_params())(g)


def _dw_in_from_proj(a, lay, D, cs, name):
    rows, icp = a.shape
    pieces = _proj_pieces(lay, D, cs)
    tm = _pick(rows, (256, 128, 64, 32, 16))

    def body(i_ref, o_ref):
        for d, src, dst, w in pieces:
            o_ref[d, :, src:src + w] = i_ref[:, dst:dst + w]

    return pl.pallas_call(
        body, name=name, grid=(rows // tm,), in_specs=[pl.BlockSpec((tm, icp), lambda i: (i, 0))],
        out_specs=pl.BlockSpec((N_DEV, tm, cs), lambda i: (0, i, 0)), out_shape=jax.ShapeDtypeStruct((N_DEV, rows, cs), a.dtype),
        compiler_params=_params())(a)


def _pad_rows(a, rows):
    return a if a.shape[0] == rows else jnp.concatenate([a, jnp.zeros((rows - a.shape[0], a.shape[1]), a.dtype)], axis=0)


def _token_shift(p, halo, mu, i):
    tm = p.shape[0]
    hid = lax.broadcasted_iota(jnp.int32, (SUBLANE, 1), 0)
    before = jnp.sum(jnp.where(hid == SUBLANE - 1, halo, 0.0), axis=0, keepdims=True)
    before = jnp.where(i == 0, 0.0, before)
    rid = lax.broadcasted_iota(jnp.int32, (tm, 1), 0)
    prev = jnp.where(rid == 0, before, pltpu.roll(p, 1, 0))
    d = prev - p
    return p + d * mu, d


def _rwkv_math(ps, w0, a0, k_k, k_a, wlw, wla, wlg, lay):
    _, pw, offs, _ = lay
    r, k, v, xw, xa, xg = (ps[:, offs[j]:offs[j] + pw[j]] for j in range(6))
    tw = jnp.tanh(xw)
    ww = w0 + _bdot(tw, wlw)
    lw = -jnp.exp(-_softplus(-ww) - 0.5)
    a = _sigmoid(a0 + _bdot(xa, wla))
    sg = _sigmoid(xg)
    g = _bdot(sg, wlg)
    return dict(r=r, k=k, v=v, xa=xa, tw=tw, ww=ww, lw=lw, a=a, sg=sg, g=g, kkp=k * k_k, k2=k * (1.0 + (a - 1.0) * k_a))


def _halo_spec(tm, width):
    hb = tm // SUBLANE
    return pl.BlockSpec((SUBLANE, width), lambda i: (jnp.maximum(i * hb - 1, 0), 0))


def _rowsum(x):
    return jnp.sum(x, axis=-1, keepdims=True)


def _kk_math(kkp):
    nrm = jnp.sqrt(_rowsum(kkp * kkp))
    inv = 1.0 / jnp.maximum(nrm, 1e-12)
    return nrm, inv, kkp * inv


def _rwkv_pre(p, mu, small, lora, lay, name):
    T, rcp = p.shape[0], lay[3]
    H = lay[0][0] // HEAD
    tm = min(256, T)

    def body(p_ref, ph_ref, mu_ref, w0_ref, a0_ref, kk_ref, ka_ref, wlw_ref, wla_ref, wlg_ref, r_o, lw_o, k2_o, v_o, aa_o, bb_o, g_o):
        ps, _ = _token_shift(p_ref[...], ph_ref[...], mu_ref[...], pl.program_id(0))
        q = _rwkv_math(ps, w0_ref[...], a0_ref[...], kk_ref[...], ka_ref[...], wlw_ref[...], wla_ref[...], wlg_ref[...], lay)
        for h in range(H):
            sl = slice(h * HEAD, (h + 1) * HEAD)
            for o_ref, key in ((r_o, "r"), (lw_o, "lw"), (k2_o, "k2"), (v_o, "v"), (g_o, "g")):
                o_ref[h] = q[key][:, sl]
            _, _, kk = _kk_math(q["kkp"][:, sl])
            aa_o[h] = -kk
            bb_o[h] = kk * q["a"][:, sl]

    whole = lambda arr: pl.BlockSpec(arr.shape, lambda i: (0, 0))
    return pl.pallas_call(
        body, name=name, grid=(T // tm,),
        in_specs=([pl.BlockSpec((tm, rcp), lambda i: (i, 0)), _halo_spec(tm, rcp), whole(mu)]
                  + [whole(s) for s in small] + [whole(w) for w in lora]),
        out_specs=[pl.BlockSpec((H, tm, HEAD), lambda i: (0, i, 0))] * 7, out_shape=[jax.ShapeDtypeStruct((H, T, HEAD), F32)] * 7,
        compiler_params=_params())(p, p, mu, *small, *lora)


def _rwkv_pre_bwd(p, mu, small, lora, hgrads, dproj, lay, name):
    T, rcp = p.shape[0], lay[3]
    widths, pw, offs, _ = lay
    RW = widths[0]
    H = RW // HEAD
    tm = min(128, T)
    nt = T // tm
    hb = tm // SUBLANE

    def body(p_ref, ph_ref, mu_ref, w0_ref, a0_ref, kk_ref, ka_ref, wlw_ref, wla_ref, wlg_ref,
             dr_h, dk2_h, dv_h, dlw_h, daa, dbb, dg_h, buf_ref,
             dp_ref, dmu_ref, dw0_ref, da0_ref, dkk_ref, dka_ref, dwlw_ref, dwla_ref, dwlg_ref,
             s_dr, s_dk2, s_dv, s_dlw, s_dkkp, s_da, s_dg, dps_ref, next_ref):
        i = pl.program_id(0)
        ps, dprev = _token_shift(p_ref[...], ph_ref[...], mu_ref[...], nt - 1 - i)
        k_k, k_a = kk_ref[...], ka_ref[...]
        q = _rwkv_math(ps, w0_ref[...], a0_ref[...], k_k, k_a, wlw_ref[...], wla_ref[...], wlg_ref[...], lay)
        k, a, lw, ww, tw, sg = q["k"], q["a"], q["lw"], q["ww"], q["tw"], q["sg"]
        for h in range(H):
            sl = slice(h * HEAD, (h + 1) * HEAD)
            s_dr[:, sl] = dr_h[h]
            s_dk2[:, sl] = dk2_h[h]
            s_dv[:, sl] = dv_h[h]
            s_dlw[:, sl] = dlw_h[h]
            s_dg[:, sl] = dg_h[h]
            nrm, inv, kk = _kk_math(q["kkp"][:, sl])
            dbb_h = dbb[h]
            dkk = dbb_h * a[:, sl] - daa[h]
            s_dkkp[:, sl] = jnp.where(nrm > 1e-12, inv * (dkk - kk * _rowsum(dkk * kk)), dkk * inv)
            s_da[:, sl] = dbb_h * kk
        dk2, dkkp, dg = s_dk2[...], s_dkkp[...], s_dg[...]
        dk = dk2 * (1.0 + (a - 1.0) * k_a) + dkkp * k_k
        da = s_da[...] + dk2 * k * k_a
        dpa = da * a * (1.0 - a)
        dww = s_dlw[...] * lw * _sigmoid(-ww)
        dxa = _bdot(dpa, wla_ref[...], "nt")
        dxw = _bdot(dww, wlw_ref[...], "nt") * (1.0 - tw * tw)
        dxg = _bdot(dg, wlg_ref[...], "nt") * sg * (1.0 - sg)
        segs = (s_dr[...], dk, s_dv[...], dxw, dxa, dxg)
        sums = [dmu_ref, dw0_ref, da0_ref, dkk_ref, dka_ref, dwlw_ref, dwla_ref, dwlg_ref]

        @pl.when(i == 0)
        def _():
            for s in sums + [next_ref]:
                s[...] = jnp.zeros_like(s)

        for j, seg in enumerate(segs):
            sl = slice(offs[j], offs[j] + pw[j])
            dps_ref[:, sl] = seg
            dmu_ref[:, sl] += _colsum(seg * dprev[:, sl])
        dw0_ref[...] += _colsum(dww)
        da0_ref[...] += _colsum(dpa)
        dkk_ref[...] += _colsum(dkkp * k)
        dka_ref[...] += _colsum(dk2 * k * (a - 1.0))
        dwlw_ref[...] += _bdot(tw, dww, "tn")
        dwla_ref[...] += _bdot(q["xa"], dpa, "tn")
        dwlg_ref[...] += _bdot(sg, dg, "tn")
        dps = dps_ref[...]
        rid = lax.broadcasted_iota(jnp.int32, (tm, 1), 0)
        nxt = jnp.where(rid == tm - 1, next_ref[...], pltpu.roll(dps, tm - 1, 0))
        mu_v = mu_ref[...]
        dp_ref[...] = (dps * (1.0 - mu_v) + nxt * mu_v).astype(BF16)
        next_ref[...] = _colsum(jnp.where(rid == 0, dps, 0.0))

    whole = lambda arr: pl.BlockSpec(arr.shape, lambda i: (0, 0))
    row = lambda w: pl.BlockSpec((tm, w), lambda i: (nt - 1 - i, 0))
    acc_shapes = [(1, rcp), (1, RW), (1, RW), (1, RW), (1, RW)] + [w.shape for w in lora]
    return pl.pallas_call(
        body, name=name, grid=(nt,),
        in_specs=([row(rcp), pl.BlockSpec((SUBLANE, rcp), lambda i: (jnp.maximum((nt - 1 - i) * hb - 1, 0), 0)), whole(mu)]
                  + [whole(s) for s in small] + [whole(w) for w in lora]
                  + [pl.BlockSpec((H, tm, HEAD), lambda i: (0, nt - 1 - i, 0))] * 7 + [pl.BlockSpec(memory_space=pl.ANY)]),
        out_specs=[row(rcp)] + [pl.BlockSpec(s, lambda i: (0, 0)) for s in acc_shapes],
        out_shape=[jax.ShapeDtypeStruct(dproj.shape, BF16)] + [jax.ShapeDtypeStruct(s, F32) for s in acc_shapes],
        scratch_shapes=[pltpu.VMEM((tm, RW), F32)] * 7 + [pltpu.VMEM((tm, rcp), F32), pltpu.VMEM((1, rcp), F32)],
        input_output_aliases={10 + 7: 0}, compiler_params=_params())(p, p, mu, *small, *lora, *hgrads, dproj)


def _head_post_math(y, r, k2, v, lg, lb, rk):
    yc = y - _mean(y)
    rstd = lax.rsqrt(_mean(yc * yc) + LNX_EPS)
    yn = yc * rstd
    s = _rowsum(r * k2 * rk)
    return yn, rstd, yn * lg + lb + s * v, s


def _head_post(y, r, k2, v, g, hp, name, deps=()):
    H, T, _ = y.shape
    tm = min(256, T)

    def body(y_ref, r_ref, k_ref, v_ref, g_ref, lg_ref, lb_ref, rk_ref, *rest):
        o_ref = rest[-1]
        _, _, t, _ = _head_post_math(y_ref[...], r_ref[...], k_ref[...], v_ref[...], lg_ref[...], lb_ref[...], rk_ref[...])
        out = (t * g_ref[...]).astype(BF16)
        for h in range(H):
            o_ref[:, h * HEAD:(h + 1) * HEAD] = out[h]

    blk = pl.BlockSpec((H, tm, HEAD), lambda i: (0, i, 0))
    par = pl.BlockSpec((H, 1, HEAD), lambda i: (0, 0, 0))
    return pl.pallas_call(
        body, name=name, grid=(T // tm,),
        in_specs=[blk] * 5 + [par] * 3 + [pl.BlockSpec(d.shape, lambda i, nd=d.ndim: (0,) * nd) for d in deps],
        out_specs=pl.BlockSpec((tm, H * HEAD), lambda i: (i, 0)),
        out_shape=jax.ShapeDtypeStruct((T, H * HEAD), BF16), compiler_params=_params())(y, r, k2, v, g, *hp, *deps)


def _bmm(x, y, mode):
    dn = {"nn": (((2,), (1,)), ((0,), (0,))), "nt": (((2,), (2,)), ((0,), (0,))), "tn": (((1,), (1,)), ((0,), (0,)))}[mode]
    (xh, xl), (yh, yl) = _split(x), _split(y)
    dot = lambda p, q: lax.dot_general(p, q, dn, preferred_element_type=F32)
    out = dot(xh, yh)
    if yl is not None:
        out = out + dot(xh, yl)
    if xl is not None:
        out = out + dot(xl, yh)
    return out


def _split(x):
    if isinstance(x, tuple):
        return x
    hi = x.astype(BF16)
    return hi, (x - hi.astype(F32)).astype(BF16)


def _exact(x):
    return x.astype(BF16), None


def _round(x):
    return x if isinstance(x, tuple) else (x.astype(BF16), None)


def _rows(*xs):
    if isinstance(xs[0], tuple):
        return tuple(None if any(p is None for p in parts) else jnp.concatenate(parts, axis=1) for parts in zip(*xs))
    return jnp.concatenate(xs, axis=1)


def _wkv_chunk(r, lw, k, v, a, b, inverse=None):
    hb, C, _ = r.shape
    ti = lax.broadcasted_iota(jnp.int32, (C, C), 0)
    si = lax.broadcasted_iota(jnp.int32, (C, C), 1)
    linc, lstr, eye = (ti >= si).astype(F32), (ti > si).astype(F32), (ti == si).astype(F32)
    qmask = jnp.concatenate([jnp.concatenate([lstr, lstr], axis=1), jnp.concatenate([linc, linc], axis=1)], axis=0)
    lincb = _exact(jnp.broadcast_to(linc, (hb, C, C)))
    both = _exact(jnp.broadcast_to(jnp.concatenate([linc, lstr], axis=0), (hb, 2 * C, C)))
    ones = _exact(jnp.ones_like(v))
    lws = _split(lw)
    ci = _bmm(lincb, lws, "nn")
    cC = jnp.sum(lw, axis=1, keepdims=True)
    gi, ge, gn, gr = jnp.exp(ci), jnp.exp(ci - lw), jnp.exp(-ci), jnp.exp(cC - ci)
    q = dict(At=a * ge, Rt=r * gi, Bt=b * gn, Kt=k * gn, Bh=b * gr, Kh=k * gr)
    s = dict(AR=_round(_rows(q["At"], q["Rt"])), BK=_round(_rows(q["Bt"], q["Kt"])), BKh=_round(_rows(q["Bh"], q["Kh"])), v=_round(v))
    quad = _bmm(s["AR"], s["BK"], "nt") * qmask
    s["top"], s["bot"] = _round(quad[:, :C]), _round(quad[:, C:])
    if inverse is None:
        A_ab = quad[:, :C, :C]
        Tm = eye + A_ab
        Pw = _round(A_ab)
        n = 1
        while 2 * n < C:
            Pw = _round(_bmm(Pw, Pw, "nn"))
            Tm = Tm + _bmm(_round(Tm), Pw, "nn")
            n *= 2
        inverse = Tm
    s["Tm"] = _round(inverse)
    gC = jnp.exp(_bmm(lws, ones, "tn"))
    q.update(gi=gi, ge=ge, gn=gn, gr=gr, qmask=qmask, both=both, gC=gC, ones=ones, s=s)
    return q


def _wkv_u(s, H0s, C):
    arh = _bmm(s["AR"], H0s, "nn")
    zv = _rows(tuple(None if p is None else jnp.zeros_like(p) for p in s["v"]), s["v"])
    U = _bmm(s["Tm"], _round(arh[:, :C] + _bmm(s["top"], zv, "nn")), "nn")
    return arh, _rows(_round(U), s["v"])


def _wkv_fwd(r, lw, k, v, a, b, name):
    H, T, N = r.shape
    C = min(WKV_CHUNK, T)
    nc = T // C
    hb = _pick(H, (16, 8, 4, 2))

    def body(r_ref, lw_ref, k_ref, v_ref, a_ref, b_ref, y_ref, st_ref, inv_ref, u_ref, h_ref):
        @pl.when(pl.program_id(1) == 0)
        def _():
            h_ref[...] = jnp.zeros_like(h_ref)

        H0 = h_ref[...]
        st_ref[0] = H0
        q = _wkv_chunk(r_ref[...], lw_ref[...], k_ref[...], v_ref[...], a_ref[...], b_ref[...])
        s = q["s"]
        arh, UV = _wkv_u(s, _round(H0), C)
        inv_ref[0] = s["Tm"][0]
        u_ref[...] = UV[0][:, :C]
        y_ref[...] = arh[:, C:] + _bmm(s["bot"], UV, "nn")
        h_ref[...] = q["gC"] * H0 + _bmm(s["BKh"], UV, "tn")

    blk = pl.BlockSpec((hb, C, N), lambda h, c: (h, c, 0))
    per_chunk = lambda w: pl.BlockSpec((1, hb, w, w), lambda h, c: (c, h, 0, 0))
    return pl.pallas_call(
        body, name=name, grid=(H // hb, nc), in_specs=[blk] * 6, out_specs=[blk, per_chunk(N), per_chunk(C), blk],
        out_shape=[jax.ShapeDtypeStruct((H, T, N), F32), jax.ShapeDtypeStruct((nc, H, N, N), F32),
                   jax.ShapeDtypeStruct((nc, H, C, C), BF16), jax.ShapeDtypeStruct((H, T, N), BF16)],
        scratch_shapes=[pltpu.VMEM((hb, N, N), F32)], compiler_params=_params())(r, lw, k, v, a, b)


def _wkv_bwd(r, lw, k, v, a, b, states, inverses, u, y, g, hp, dya, name, deps=()):
    H, T, N = r.shape
    C = min(WKV_CHUNK, T)
    nc = T // C
    hb = _pick(H, (16, 8, 4, 2))
    hsum = lambda t: jnp.sum(t, axis=1, keepdims=True)

    def body(r_ref, lw_ref, k_ref, v_ref, a_ref, b_ref, st_ref, inv_ref, u_ref, y_ref, g_ref, lg_ref, lb_ref, rk_ref, dya_ref, *rest):
        (dr_ref, dlw_ref, dk_ref, dv_ref, da_ref, db_ref, dg_ref, dlg_ref, dlb_ref, drk_ref, dh_ref, d_s) = rest[len(deps):]
        first = pl.program_id(1) == 0

        @pl.when(first)
        def _():
            dh_ref[...] = jnp.zeros_like(dh_ref)

        for h in range(hb):
            d_s[h] = dya_ref[:, h * N:(h + 1) * N]
        d_v, r_v, k_v, v_v, lg, rk = d_s[...], r_ref[...], k_ref[...], v_ref[...], lg_ref[...], rk_ref[...]
        yn, rstd, t, bonus = _head_post_math(y_ref[...], r_v, k_v, v_v, lg, lb_ref[...], rk)
        dyo = d_v * g_ref[...]
        dyn = dyo * lg
        ds = _rowsum(dyo * v_v)
        dy = rstd * (dyn - _mean(dyn) - yn * _mean(dyn * yn))
        dg_ref[...] = d_v * t
        sums = (hsum(dyo * yn), hsum(dyo), hsum(ds * r_v * k_v))

        @pl.when(first)
        def _():
            for o_ref, val in zip((dlg_ref, dlb_ref, drk_ref), sums):
                o_ref[...] = val

        @pl.when(jnp.logical_not(first))
        def _():
            for o_ref, val in zip((dlg_ref, dlb_ref, drk_ref), sums):
                o_ref[...] += val

        dHC = dh_ref[...]
        H0 = st_ref[0]
        q = _wkv_chunk(r_v, lw_ref[...], k_v, v_v, a_ref[...], b_ref[...], inverse=inv_ref[0])
        s, gC = q["s"], q["gC"]
        H0s, dHs, dY = _round(H0), _round(dHC), _round(dy)
        UV = _rows(_round(u_ref[...]), s["v"])
        bot_dy = _bmm(s["bot"], dY, "tn")
        bkh_dh = _bmm(s["BKh"], dHs, "nn")
        dP = _round(_bmm(s["Tm"], _round(bot_dy[:, :C] + bkh_dh[:, :C]), "tn"))
        dv_ref[...] = bot_dy[:, C:] + bkh_dh[:, C:] + _bmm(s["top"], dP, "tn")[:, C:] + dyo * bonus
        dPY = _rows(dP, dY)
        dh_ref[...] = gC * dHC + _bmm(s["AR"], dPY, "tn")
        dquad = _round(_bmm(dPY, UV, "nt") * q["qmask"])
        dAR = _bmm(dPY, H0s, "nt") + _bmm(dquad, s["BK"], "nn")
        dBK = _bmm(dquad, s["AR"], "tn")
        dBKh = _bmm(UV, dHs, "nt")
        dAt, dRt, dBt, dKt, dBh, dKh = dAR[:, :C], dAR[:, C:], dBK[:, :C], dBK[:, C:], dBKh[:, :C], dBKh[:, C:]
        dr_ref[...] = dRt * q["gi"] + ds * k_v * rk
        da_ref[...] = dAt * q["ge"]
        db_ref[...] = dBt * q["gn"] + dBh * q["gr"]
        dk_ref[...] = dKt * q["gn"] + dKh * q["gr"] + ds * r_v * rk
        tail = dBh * q["Bh"] + dKh * q["Kh"]
        dci = dRt * q["Rt"] - dBt * q["Bt"] - dKt * q["Kt"] - tail
        dcC = jnp.sum(tail, axis=1, keepdims=True) + _bmm(q["ones"], H0 * dHC * gC, "nt")
        dlw_ref[...] = _bmm(q["both"], _rows(dci, dAt * q["At"]), "tn") + dcC

    blk = pl.BlockSpec((hb, C, N), lambda h, c: (h, nc - 1 - c, 0))
    per_chunk = lambda w: pl.BlockSpec((1, hb, w, w), lambda h, c: (nc - 1 - c, h, 0, 0))
    par = pl.BlockSpec((hb, 1, N), lambda h, c: (h, 0, 0))
    return pl.pallas_call(
        body, name=name, grid=(H // hb, nc),
        in_specs=([blk] * 6 + [per_chunk(N), per_chunk(C), blk, blk, blk] + [par] * 3
                  + [pl.BlockSpec((C, hb * N), lambda h, c: (nc - 1 - c, h))]
                  + [pl.BlockSpec(d.shape, lambda h, c, nd=d.ndim: (0,) * nd) for d in deps]),
        out_specs=[blk] * 7 + [par] * 3,
        out_shape=[jax.ShapeDtypeStruct((H, T, N), F32)] * 7 + [jax.ShapeDtypeStruct((H, 1, N), F32)] * 3,
        scratch_shapes=[pltpu.VMEM((hb, N, N), F32), pltpu.VMEM((hb, C, N), F32)],
        compiler_params=_params())(r, lw, k, v, a, b, states, inverses, u, y, g, *hp, dya, *deps)


def _sgu_ln(z, SW, lng, lnb):
    ge = _gelu(z)
    u, vv = ge[:, :SW], ge[:, SW:]
    xc = vv - _mean(vv)
    rstd = lax.rsqrt(_mean(xc * xc) + LN_EPS)
    vn = xc * rstd
    return u, vn, rstd, vn * lng + lnb


def _causal(ws_ref, g):
    ti = lax.broadcasted_iota(jnp.int32, (SGU_CHUNK, SGU_CHUNK), 0)
    si = lax.broadcasted_iota(jnp.int32, (SGU_CHUNK, SGU_CHUNK), 1)
    return ti >= si, jnp.where(ti >= si, ws_ref[g], 0.0).astype(BF16)


def _sgu_fwd(proj, zblock, lng, lnb, ws, bexp, name):
    T, SW = proj.shape[0], lng.shape[1]
    G = ws.shape[0]
    tr = min(256, T)
    nch = tr // SGU_CHUNK

    def body(z_ref, lng_ref, lnb_ref, ws_ref, be_ref, o_ref):
        u, _, _, vl = _sgu_ln(z_ref[...], SW, lng_ref[...], lnb_ref[...])
        for g in range(G):
            cs = slice(g * SGU_GROUP, (g + 1) * SGU_GROUP)
            _, wc = _causal(ws_ref, g)
            for n in range(nch):
                rs = slice(n * SGU_CHUNK, (n + 1) * SGU_CHUNK)
                m = jnp.dot(wc, vl[rs, cs].astype(BF16), preferred_element_type=F32) + be_ref[:, cs]
                o_ref[rs, cs] = (u[rs, cs] * m).astype(BF16)

    whole = lambda arr: pl.BlockSpec(arr.shape, lambda i, nd=arr.ndim: (0,) * nd)
    return pl.pallas_call(
        body, name=name, grid=(T // tr,),
        in_specs=[pl.BlockSpec((tr, 2 * SW), lambda i: (i, zblock)), whole(lng), whole(lnb), whole(ws), whole(bexp)],
        out_specs=pl.BlockSpec((tr, SW), lambda i: (i, 0)), out_shape=jax.ShapeDtypeStruct((T, SW), BF16),
        compiler_params=_params())(proj, lng, lnb, ws, bexp)


def _sgu_bwd(proj, zblock, dyb, lng, lnb, ws, bexp, dproj, name):
    T, SW = proj.shape[0], lng.shape[1]
    G = ws.shape[0]
    tr = min(256, T)
    nch = tr // SGU_CHUNK
    nt = T // tr

    def body(z_ref, dy_ref, lng_ref, lnb_ref, ws_ref, be_ref, buf_ref, dz_ref, dlg_ref, dlb_ref, dws_ref, db_ref, du_s, dvl_s, dbacc_s):
        i = pl.program_id(0)
        zv = z_ref[...]
        lng_v = lng_ref[...]
        u, vn, rstd, vl = _sgu_ln(zv, SW, lng_v, lnb_ref[...])

        @pl.when(i == 0)
        def _():
            for s in (dlg_ref, dlb_ref, dws_ref, dbacc_s):
                s[...] = jnp.zeros_like(s)

        for g in range(G):
            cs = slice(g * SGU_GROUP, (g + 1) * SGU_GROUP)
            tri, wc = _causal(ws_ref, g)
            for n in range(nch):
                rs = slice(n * SGU_CHUNK, (n + 1) * SGU_CHUNK)
                blk = vl[rs, cs].astype(BF16)
                m = jnp.dot(wc, blk, preferred_element_type=F32) + be_ref[:, cs]
                dyv = dy_ref[rs, cs]
                du_s[rs, cs] = dyv * m
                dm = dyv * u[rs, cs]
                dvl_s[rs, cs] = _bdot(wc, dm, "tn")
                dws_ref[g] += jnp.where(tri, _bdot(dm, blk, "nt"), 0.0)
                dbacc_s[:, cs] += dm

        dvl = dvl_s[...]
        dlg_ref[...] += _colsum(dvl * vn)
        dlb_ref[...] += _colsum(dvl)
        dvn = dvl * lng_v
        dvv = rstd * (dvn - _mean(dvn) - vn * _mean(dvn * vn))
        gp = _gelu_grad(zv)
        dz_ref[:, :SW] = (du_s[...] * gp[:, :SW]).astype(BF16)
        dz_ref[:, SW:] = (dvv * gp[:, SW:]).astype(BF16)

        @pl.when(i == nt - 1)
        def _():
            lane = lax.broadcasted_iota(jnp.int32, (SGU_CHUNK, LANE), 1)
            out = jnp.zeros((SGU_CHUNK, LANE), F32)
            for g in range(G):
                col = jnp.sum(dbacc_s[:, g * SGU_GROUP:(g + 1) * SGU_GROUP], axis=1, keepdims=True)
                out = jnp.where(lane == g, col, out)
            db_ref[...] = out

    whole = lambda arr: pl.BlockSpec(arr.shape, lambda i, nd=arr.ndim: (0,) * nd)
    acc_shapes = [(1, SW), (1, SW), ws.shape, (SGU_CHUNK, LANE)]
    return pl.pallas_call(
        body, name=name, grid=(nt,),
        in_specs=[pl.BlockSpec((tr, 2 * SW), lambda i: (i, zblock)), pl.BlockSpec((tr, SW), lambda i: (i, 0)),
                  whole(lng), whole(lnb), whole(ws), whole(bexp), pl.BlockSpec(memory_space=pl.ANY)],
        out_specs=([pl.BlockSpec((tr, 2 * SW), lambda i: (i, zblock))]
                   + [pl.BlockSpec(s, lambda i, nd=len(s): (0,) * nd) for s in acc_shapes]),
        out_shape=[jax.ShapeDtypeStruct(dproj.shape, BF16)] + [jax.ShapeDtypeStruct(s, F32) for s in acc_shapes],
        scratch_shapes=[pltpu.VMEM((tr, SW), F32), pltpu.VMEM((tr, SW), F32), pltpu.VMEM((SGU_CHUNK, SW), F32)],
        input_output_aliases={6: 0}, compiler_params=_params())(proj, dyb, lng, lnb, ws, bexp, dproj)


_HBM = pl.BlockSpec(memory_space=pltpu.HBM)
_SEM = pl.BlockSpec(memory_space=pltpu.SEMAPHORE)
_DATAFLOW = pltpu.SideEffectType.DATAFLOW_SIDE_EFFECTING


def _mesh_place(chips=False):
    x, y, c = lax.axis_index("x"), lax.axis_index("y"), lax.axis_index("c")
    return x, y, c, (2 * x + y if chips else 4 * x + 2 * y + c)


def _peer(x, y, c, rel, chips=False):
    px = 1 - x if rel & 4 else x
    py = 1 - y if rel & 2 else y
    pc = 1 - c if rel & 1 else c
    return (px, py, pc), (2 * px + py if chips else 4 * px + 2 * py + pc)


ALL_PEERS = tuple(range(1, N_DEV))
SIBLING = (1,)
SAME_CORE = (2, 4, 6)
SIBLINGS_CORE = (3, 5, 7)


def _exchange_start(groups, name, rels=ALL_PEERS, chips=False):
    flat = [t for g in groups for t in g]
    sizes = [len(g) for g in groups]
    n, ng = len(flat), len(groups)
    srcs = [pltpu.with_memory_space_constraint(a, pltpu.HBM) for a, _ in flat]
    lands = [pltpu.with_memory_space_constraint(lax.empty(((N_DEV,) + a.shape) if isg else a.shape, a.dtype), pltpu.HBM)
             for a, isg in flat]

    def body(*refs):
        ins, lnd, sems, token = refs[:n], refs[n:2 * n], refs[2 * n:2 * n + 3 * ng], refs[-1]
        x, y, c, me = _mesh_place(chips)
        j0 = 0
        for gi, sz in enumerate(sizes):
            for rel in rels:
                dev, slot = _peer(x, y, c, rel, chips)
                for jj in range(sz):
                    j = j0 + jj
                    pltpu.make_async_remote_copy(
                        src_ref=ins[j] if flat[j][1] else ins[j].at[slot], dst_ref=lnd[j].at[me],
                        send_sem=sems[3 * gi].at[jj * (N_DEV - 1) + rel - 1], recv_sem=sems[3 * gi + 1].at[jj * (N_DEV - 1) + rel - 1],
                        device_id=dev, device_id_type=pl.DeviceIdType.MESH).start()
            for jj in range(sz):
                j = j0 + jj
                pltpu.make_async_copy(ins[j] if flat[j][1] else ins[j].at[me], lnd[j].at[me], sems[3 * gi + 2].at[jj]).start()
            j0 += sz
        token[...] = jnp.zeros_like(token)

    sem_shapes = [pltpu.SemaphoreType.DMA((k,)) for sz in sizes for k in (sz * (N_DEV - 1), sz * (N_DEV - 1), sz)]
    res = pl.pallas_call(
        body, name=name,
        out_shape=(*sem_shapes, *[pltpu.HBM(a.shape, a.dtype) for a in srcs], *[pltpu.HBM(a.shape, a.dtype) for a in lands],
                   jax.ShapeDtypeStruct((SUBLANE, LANE), F32)),
        in_specs=[_HBM] * (2 * n), out_specs=(*[_SEM] * (3 * ng), *[_HBM] * (2 * n), pl.BlockSpec(memory_space=pltpu.VMEM)),
        input_output_aliases={i: 3 * ng + i for i in range(2 * n)},
        compiler_params=pltpu.CompilerParams(has_side_effects=_DATAFLOW))(*srcs, *lands)
    sems, thru, token = res[:3 * ng], res[3 * ng:3 * ng + 2 * n], res[-1]
    handle, j0 = [], 0
    for gi, sz in enumerate(sizes):
        handle.append(dict(kinds=[k for _, k in groups[gi]], chips=chips, srcs=list(thru[j0:j0 + sz]), lands=list(thru[n + j0:n + j0 + sz]),
                           sems=list(sems[3 * gi:3 * gi + 3])))
        j0 += sz
    return handle, token


def _exchange_wait(group, after, name, rels=ALL_PEERS, local=True):
    kinds, sz = group["kinds"], len(group["kinds"])
    relay = group.get("relay", [])

    def body(*refs):
        ins, lnd, (ssem, rsem, lsem) = refs[:sz], refs[sz:2 * sz], refs[2 * sz:2 * sz + 3]
        x, y, c, me = _mesh_place(group["chips"])
        for rel in rels:
            dev, slot = _peer(x, y, c, rel, group["chips"])
            for jj in range(sz):
                cp = pltpu.make_async_remote_copy(
                    src_ref=ins[jj] if kinds[jj] else ins[jj].at[slot], dst_ref=lnd[jj].at[slot],
                    send_sem=ssem.at[jj * (N_DEV - 1) + rel - 1], recv_sem=rsem.at[jj * (N_DEV - 1) + rel - 1],
                    device_id=dev, device_id_type=pl.DeviceIdType.MESH)
                cp.wait_send()
                cp.wait_recv()
        if local:
            for jj in range(sz):
                pltpu.make_async_copy(ins[jj] if kinds[jj] else ins[jj].at[me], lnd[jj].at[me], lsem.at[jj]).wait()
        if relay:
            fsend, frecv = refs[2 * sz + 3:2 * sz + 5]
            dev = _peer(x, y, c, 1)[0]
            for q, (mine, theirs) in enumerate(zip(SAME_CORE, SIBLINGS_CORE)):
                for jj in range(sz):
                    cp = pltpu.make_async_remote_copy(
                        src_ref=lnd[jj].at[_peer(x, y, c, mine)[1]], dst_ref=lnd[jj].at[_peer(x, y, c, theirs)[1]],
                        send_sem=fsend.at[jj * len(SAME_CORE) + q], recv_sem=frecv.at[jj * len(SAME_CORE) + q],
                        device_id=dev, device_id_type=pl.DeviceIdType.MESH)
                    cp.wait_send()
                    cp.wait_recv()

    arrays = group["srcs"] + group["lands"]
    sems = group["sems"] + relay
    res = pl.pallas_call(
        body, name=name, out_shape=[pltpu.HBM(a.shape, a.dtype) for a in arrays],
        in_specs=[_HBM] * (2 * sz) + [_SEM] * len(sems) + [pl.BlockSpec(memory_space=pl.ANY)], out_specs=[_HBM] * (2 * sz),
        input_output_aliases={i: i for i in range(2 * sz)},
        compiler_params=pltpu.CompilerParams(has_side_effects=_DATAFLOW))(*arrays, *sems, after)
    return dict(group, srcs=list(res[:sz]), lands=list(res[sz:]), relay=[])


def _relay_start(group, name):
    sz = len(group["kinds"])
    nq = len(SAME_CORE)

    def body(*refs):
        lnd, fsend, frecv, token = refs[:sz], refs[sz], refs[sz + 1], refs[-1]
        x, y, c, _ = _mesh_place()
        dev = _peer(x, y, c, 1)[0]
        for q, rel in enumerate(SAME_CORE):
            slot = _peer(x, y, c, rel)[1]
            for jj in range(sz):
                pltpu.make_async_remote_copy(
                    src_ref=lnd[jj].at[slot], dst_ref=lnd[jj].at[slot], send_sem=fsend.at[jj * nq + q], recv_sem=frecv.at[jj * nq + q],
                    device_id=dev, device_id_type=pl.DeviceIdType.MESH).start()
        token[...] = jnp.zeros_like(token)

    lands = group["lands"]
    res = pl.pallas_call(
        body, name=name,
        out_shape=(pltpu.SemaphoreType.DMA((sz * nq,)), pltpu.SemaphoreType.DMA((sz * nq,)), *[pltpu.HBM(a.shape, a.dtype) for a in lands],
                   jax.ShapeDtypeStruct((SUBLANE, LANE), F32)),
        in_specs=[_HBM] * sz, out_specs=(_SEM, _SEM, *[_HBM] * sz, pl.BlockSpec(memory_space=pltpu.VMEM)),
        input_output_aliases={i: 2 + i for i in range(sz)},
        compiler_params=pltpu.CompilerParams(has_side_effects=_DATAFLOW))(*lands)
    return dict(group, lands=list(res[2:2 + sz]), relay=[res[0], res[1]]), res[-1]


def _sibling_swap(arrays, handle, after, name):
    start = handle is None
    n = len(arrays) if start else len(handle["srcs"])
    chips = N_DEV // 2
    if start:
        srcs = [pltpu.with_memory_space_constraint(a.reshape(chips, 2, *a.shape[1:]), pltpu.HBM) for a in arrays]
        lands = [pltpu.with_memory_space_constraint(lax.empty((chips,) + a.shape[1:], a.dtype), pltpu.HBM) for a in arrays]
    else:
        srcs, lands = handle["srcs"], handle["lands"]

    def body(*refs):
        ins, lnd, ssem, rsem = refs[:n], refs[n:2 * n], refs[2 * n], refs[2 * n + 1]
        x, y, c, _ = _mesh_place()
        dev = _peer(x, y, c, 1)[0]
        for q in range(chips):
            for j in range(n):
                cp = pltpu.make_async_remote_copy(
                    src_ref=ins[j].at[q, 1 - c], dst_ref=lnd[j].at[q], send_sem=ssem.at[j * chips + q], recv_sem=rsem.at[j * chips + q],
                    device_id=dev, device_id_type=pl.DeviceIdType.MESH)
                if start:
                    cp.start()
                else:
                    cp.wait_send()
                    cp.wait_recv()
        if start:
            refs[-1][...] = jnp.zeros_like(refs[-1])

    thru = [pltpu.HBM(a.shape, a.dtype) for a in srcs + lands]
    effect = pltpu.CompilerParams(has_side_effects=_DATAFLOW)
    if start:
        res = pl.pallas_call(
            body, name=name, out_shape=(pltpu.SemaphoreType.DMA((n * chips,)), pltpu.SemaphoreType.DMA((n * chips,)), *thru,
                                        jax.ShapeDtypeStruct((SUBLANE, LANE), F32)),
            in_specs=[_HBM] * (2 * n), out_specs=(_SEM, _SEM, *[_HBM] * (2 * n), pl.BlockSpec(memory_space=pltpu.VMEM)),
            input_output_aliases={i: 2 + i for i in range(2 * n)}, compiler_params=effect)(*srcs, *lands)
        return dict(srcs=list(res[2:2 + n]), lands=list(res[2 + n:2 + 2 * n]), sems=[res[0], res[1]]), res[-1]
    res = pl.pallas_call(
        body, name=name, out_shape=thru, in_specs=[_HBM] * (2 * n) + [_SEM, _SEM, pl.BlockSpec(memory_space=pl.ANY)],
        out_specs=[_HBM] * (2 * n), input_output_aliases={i: i for i in range(2 * n)}, compiler_params=effect)(
            *srcs, *lands, *handle["sems"], after)
    return dict(handle, srcs=list(res[:n]), lands=list(res[n:]))


def _pair_add(mine, theirs, core, name):
    chips, _, rows, w = mine.shape
    tm = _pick(rows, (256, 128, 64, 32, 16))

    def body(core_ref, a_ref, b_ref, o_ref):
        o_ref[...] = (a_ref[...].astype(F32) + b_ref[...].astype(F32)).astype(o_ref.dtype)

    return pl.pallas_call(
        body, name=name, out_shape=jax.ShapeDtypeStruct(theirs.shape, theirs.dtype),
        grid_spec=pltpu.PrefetchScalarGridSpec(
            num_scalar_prefetch=1, grid=(chips, rows // tm),
            in_specs=[pl.BlockSpec((None, None, tm, w), lambda q, i, core_ref: (q, core_ref[0], i, 0)),
                      pl.BlockSpec((None, tm, w), lambda q, i, core_ref: (q, i, 0))],
            out_specs=pl.BlockSpec((None, tm, w), lambda q, i, core_ref: (q, i, 0))),
        compiler_params=_params())(core, mine, theirs)


def _adamw(w, m, v, gparts, name, after=None):
    R, C = w.shape
    tm = _pick(R, (256, 128, 64, 32, 16, 8))
    order = [] if after is None else [after]

    def body(w_ref, m_ref, v_ref, g_ref, *rest):
        go, do, mo, vo = rest[len(order):]
        g = g_ref[0].astype(F32)
        for j in range(1, gparts.shape[0]):
            g = g + g_ref[j].astype(F32)
        mn = ADAM_B1 * m_ref[...] + (1.0 - ADAM_B1) * g
        vn = ADAM_B2 * v_ref[...] + (1.0 - ADAM_B2) * (g * g)
        m_hat = mn / (1.0 - ADAM_B1 ** ADAM_STEP)
        v_hat = vn / (1.0 - ADAM_B2 ** ADAM_STEP)
        go[...] = g
        do[...] = -ADAM_LR * (m_hat / (jnp.sqrt(v_hat) + ADAM_EPS) + ADAM_WD * w_ref[...])
        mo[...] = mn
        vo[...] = vn

    row = pl.BlockSpec((tm, C), lambda i: (i, 0))
    return pl.pallas_call(
        body, name=name, grid=(R // tm,),
        in_specs=[row, row, row, pl.BlockSpec((gparts.shape[0], tm, C), lambda i: (0, i, 0))] + [pl.BlockSpec(memory_space=pl.ANY)] * len(order),
        out_specs=[row] * 4, out_shape=[jax.ShapeDtypeStruct((R, C), F32)] * 4, compiler_params=_params())(w, m, v, gparts, *order)


def _pack(arrays):
    parts = []
    for a in arrays:
        f = a.reshape(1, -1)
        pad = _ceil_to(f.shape[1], SUBLANE * LANE) - f.shape[1]
        f = jnp.concatenate([f, jnp.zeros((1, pad), f.dtype)], axis=1) if pad else f
        parts.append(f.reshape(-1, LANE))
    rows = sum(p.shape[0] for p in parts)
    pad = _ceil_to(rows, 64) - rows
    return jnp.concatenate(parts + ([jnp.zeros((pad, LANE), parts[0].dtype)] if pad else []), axis=0)


def _unpack(buf, shapes):
    out, row = [], 0
    for s in shapes:
        size = 1
        for d in s:
            size *= d
        rows = _ceil_to(size, SUBLANE * LANE) // LANE
        out.append(buf[row:row + rows].reshape(1, -1)[:, :size].reshape(s))
        row += rows
    return out


def kernel(x, norm_mix_g, w_in, shift_mu, w0, w_lora_up, a0, a_lora_up, g_lora_up, k_k, k_a, r_k, lnx_g, lnx_b, w_proj_rwkv, sgu_ln_g, sgu_ln_b, sgu_w, sgu_b, w_proj_sgu, w_out, norm_ffn_g, w_ffn_gate, w_ffn_up, w_ffn_down, norm_final_g, loss_target, m_norm_mix_g, m_w_in, m_shift_mu, m_w0, m_w_lora_up, m_a0, m_a_lora_up, m_g_lora_up, m_k_k, m_k_a, m_r_k, m_lnx_g, m_lnx_b, m_w_proj_rwkv, m_sgu_ln_g, m_sgu_ln_b, m_sgu_w, m_sgu_b, m_w_proj_sgu, m_w_out, m_norm_ffn_g, m_w_ffn_gate, m_w_ffn_up, m_w_ffn_down, m_norm_final_g, v_norm_mix_g, v_w_in, v_shift_mu, v_w0, v_w_lora_up, v_a0, v_a_lora_up, v_g_lora_up, v_k_k, v_k_a, v_r_k, v_lnx_g, v_lnx_b, v_w_proj_rwkv, v_sgu_ln_g, v_sgu_ln_b, v_sgu_w, v_sgu_b, v_w_proj_sgu, v_w_out, v_norm_ffn_g, v_w_ffn_gate, v_w_ffn_up, v_w_ffn_down, v_norm_final_g):
    weights = dict(norm_mix_g=norm_mix_g, w_in=w_in, shift_mu=shift_mu, w0=w0, w_lora_up=w_lora_up, a0=a0, a_lora_up=a_lora_up,
                   g_lora_up=g_lora_up, k_k=k_k, k_a=k_a, r_k=r_k, lnx_g=lnx_g, lnx_b=lnx_b, w_proj_rwkv=w_proj_rwkv,
                   sgu_ln_g=sgu_ln_g, sgu_ln_b=sgu_ln_b, sgu_w=sgu_w, sgu_b=sgu_b, w_proj_sgu=w_proj_sgu, w_out=w_out,
                   norm_ffn_g=norm_ffn_g, w_ffn_gate=w_ffn_gate, w_ffn_up=w_ffn_up, w_ffn_down=w_ffn_down, norm_final_g=norm_final_g)
    m_in = dict(norm_mix_g=m_norm_mix_g, w_in=m_w_in, shift_mu=m_shift_mu, w0=m_w0, w_lora_up=m_w_lora_up, a0=m_a0,
                a_lora_up=m_a_lora_up, g_lora_up=m_g_lora_up, k_k=m_k_k, k_a=m_k_a, r_k=m_r_k, lnx_g=m_lnx_g, lnx_b=m_lnx_b,
                w_proj_rwkv=m_w_proj_rwkv, sgu_ln_g=m_sgu_ln_g, sgu_ln_b=m_sgu_ln_b, sgu_w=m_sgu_w, sgu_b=m_sgu_b,
                w_proj_sgu=m_w_proj_sgu, w_out=m_w_out, norm_ffn_g=m_norm_ffn_g, w_ffn_gate=m_w_ffn_gate, w_ffn_up=m_w_ffn_up,
                w_ffn_down=m_w_ffn_down, norm_final_g=m_norm_final_g)
    v_in = dict(norm_mix_g=v_norm_mix_g, w_in=v_w_in, shift_mu=v_shift_mu, w0=v_w0, w_lora_up=v_w_lora_up, a0=v_a0,
                a_lora_up=v_a_lora_up, g_lora_up=v_g_lora_up, k_k=v_k_k, k_a=v_k_a, r_k=v_r_k, lnx_g=v_lnx_g, lnx_b=v_lnx_b,
                w_proj_rwkv=v_w_proj_rwkv, sgu_ln_g=v_sgu_ln_g, sgu_ln_b=v_sgu_ln_b, sgu_w=v_sgu_w, sgu_b=v_sgu_b,
                w_proj_sgu=v_w_proj_sgu, w_out=v_w_out, norm_ffn_g=v_norm_ffn_g, w_ffn_gate=v_w_ffn_gate, w_ffn_up=v_w_ffn_up,
                w_ffn_down=v_w_ffn_down, norm_final_g=v_norm_final_g)
    names = list(weights)
    col_sharded = ("w_in", "w_lora_up", "a_lora_up", "g_lora_up", "w_proj_rwkv", "w_proj_sgu", "w_ffn_gate", "w_ffn_up")
    row_sharded = ("w_out", "w_ffn_down")
    sharded = [n for n in names if n in col_sharded or n in row_sharded]
    small = [n for n in names if n not in sharded]

    xs, tgt = x[0], loss_target[0]
    T, D = xs.shape
    RW = w0.shape[1]
    H = RW // HEAD
    SW = sgu_ln_g.shape[1]
    G = sgu_w.shape[1]
    assert 2 * SW == D, "the projection layout takes the SGU part to be as wide as a gate"
    lay = _rwkv_layout(RW, w_lora_up.shape[1], a_lora_up.shape[1], g_lora_up.shape[1], D)
    _, pw, _, rcp = lay
    icp = rcp + 3 * D
    b_ga, b_gb, b_z = rcp // D, rcp // D + 1, rcp // D + 2

    gather_groups = dict(win=["w_in", "w_lora_up", "a_lora_up", "g_lora_up"], proj=["w_proj_rwkv", "w_proj_sgu", "w_out"],
                         ffn_gate_up=["w_ffn_gate", "w_ffn_up"], ffn_down=["w_ffn_down"])
    handles, gather_token = _exchange_start([[(weights[n][0].astype(BF16), True) for n in grp] for grp in gather_groups.values()],
                                            "gather_start", rels=SIBLING + SAME_CORE)
    gather = dict(zip(gather_groups, handles))
    full = {}
    relay_tokens = {}
    joined = lambda g: g.transpose(1, 0, 2).reshape(g.shape[1], -1)

    def relay_weights(key, after):
        arrived = _exchange_wait(gather[key], after, "gather_wait_ici_" + key, rels=SAME_CORE, local=False)
        gather[key], relay_tokens[key] = _relay_start(arrived, "gather_relay_" + key)

    def take_weights(key, after):
        done = _exchange_wait(gather[key], after, "gather_wait_d2d_" + key, rels=SIBLING)
        for n, g in zip(gather_groups[key], done["lands"]):
            full[n] = g.reshape(-1, g.shape[2]) if n in row_sharded else g

    packed = [_pack([d[n] for n in small] + [gather_token]) for d in (weights, m_in, v_in)]
    n1 = _rms_fwd(xs, norm_mix_g, "rms_mix", deps=[gather_token, *packed])
    relay_weights("win", n1)
    take_weights("win", relay_tokens["win"])
    W_in = _w_in_to_proj(full["w_in"], lay, D, "w_in_layout")
    lora = [_pad_rows(joined(full[n]), rows) for n, rows in zip(("w_lora_up", "a_lora_up", "g_lora_up"), pw[3:])]
    mu_p = _pad_rwkv_cols(shift_mu, lay)
    rsmall = [w0, a0, k_k, k_a]
    hp = [lnx_g.reshape(H, 1, HEAD), lnx_b.reshape(H, 1, HEAD), r_k.reshape(H, 1, HEAD)]
    ws = sgu_w[0]
    bexp = jnp.repeat(sgu_b[0].T, SGU_GROUP, axis=1)
    gf = norm_final_g.reshape(1, D)

    proj = _matmul(n1, W_in, mode="nn", out_dtype=F32, name="proj_in")
    ga, gb = (proj, D, b_ga), (proj, D, b_gb)
    r_h, lw_h, k2_h, v_h, aa_h, bb_h, g_h = _rwkv_pre(proj, mu_p, rsmall, lora, lay, "rwkv_pre")
    wkv_in = [r_h, lw_h, k2_h, v_h, aa_h, bb_h]
    y_h, *wkv_saved = _wkv_fwd(*wkv_in, "wkv_fwd")
    relay_weights("proj", y_h)
    ya = _head_post(y_h, r_h, k2_h, v_h, g_h, hp, "head_post", deps=[relay_tokens["proj"]])
    relay_weights("ffn_gate_up", ya)
    yb = _sgu_fwd(proj, b_z, sgu_ln_g, sgu_ln_b, ws, bexp, "sgu_fwd")
    take_weights("proj", ya)
    pa = _matmul(ya, full["w_proj_rwkv"], mode="nn", out_dtype=F32, name="proj_a", deps=[relay_tokens["ffn_gate_up"]])

    def merge_fn(pb_v, pa_v, ga_v, gb_v):
        return pb_v, _sigmoid(ga_v) * pa_v + _sigmoid(gb_v) * pb_v
    pb, merged = _matmul(yb, full["w_proj_sgu"], mode="nn", name="proj_b_merge",
                         epi=(merge_fn, [pa, (proj, b_ga * D), (proj, b_gb * D)], [F32, BF16]))
    h1 = _matmul(merged, full["w_out"], mode="nn", out_dtype=F32, name="out_proj", add=xs)
    n2 = _rms_fwd(h1, norm_ffn_g, "rms_ffn")
    relay_weights("ffn_down", n2)
    take_weights("ffn_gate_up", n2)

    def act_fn(gt_v, up_v):
        return gt_v, up_v, gt_v * _sigmoid(gt_v) * up_v
    gt, up, act = _matmul(n2, full["w_ffn_gate"], b2=full["w_ffn_up"], mode="nn", name="ffn_gate_up_act", out_blocks=N_DEV,
                          epi=(act_fn, [], [BF16, BF16, BF16]), deps=[relay_tokens["ffn_down"]])
    take_weights("ffn_down", act)
    h2 = _matmul(act, full["w_ffn_down"], mode="nn", out_dtype=F32, name="ffn_down", add=h1)

    def final_fn(rv, pv):
        (h_v, t_v), (g_v,) = rv, pv
        r = lax.rsqrt(_mean(h_v * h_v) + RMS_EPS)
        yn = h_v * r
        e = yn * g_v - t_v
        loss = 0.5 * jnp.sum(_mean(e * e))
        dout = e * (1.0 / D)
        dyg = dout * g_v
        dh = r * (dyg - yn * _mean(dyg * yn))
        return [dh, dh], [jnp.full((1, LANE), loss, F32), _colsum(dout * yn)]
    dh2, dh2_bf, loss_part, d_gf = _rowwise(final_fn, [h2, tgt], [gf], [(D, F32), (D, BF16)], [(1, LANE), (1, D)], name="final_loss")

    grads = {}

    def start_scatter(group, name):
        blocks = [(grads[n].reshape(N_DEV, -1, grads[n].shape[1]) if n in row_sharded else grads[n], False) for n in group]
        (handle,), token = _exchange_start([blocks], name)
        return handle, token

    def dact_fn(d_v, gt_v, up_v):
        gt_v, up_v = gt_v.astype(F32), up_v.astype(F32)
        s = _sigmoid(gt_v)
        return d_v * up_v * (s * (1.0 + gt_v * (1.0 - s))), d_v * gt_v * s
    dgt, dup = _matmul(dh2_bf, full["w_ffn_down"], mode="nt", name="d_ffn_act", out_blocks=N_DEV,
                       epi=(dact_fn, [gt, up], [BF16, BF16]))
    scatter_groups = dict(ffn_down=["w_ffn_down"], ffn_gate=["w_ffn_gate"], ffn_up=["w_ffn_up"],
                          mid=["w_out", "w_proj_rwkv", "w_proj_sgu"], last=["w_in", "w_lora_up", "a_lora_up", "g_lora_up"])
    scatters = {}
    grads["w_ffn_down"] = _matmul(act, dh2_bf, mode="tn", out_dtype=BF16, name="dw_ffn_down")
    scatters["ffn_down"], token = start_scatter(scatter_groups["ffn_down"], "scatter_start_ffn_down")
    dn2 = _matmul(dgt, full["w_ffn_gate"], mode="nt", out_dtype=F32, name="dn2_gate", deps=[token])
    grads["w_ffn_gate"] = _matmul(n2, dgt, mode="tn", out_dtype=BF16, name="dw_ffn_gate", out_blocks=N_DEV)
    scatters["ffn_gate"], token = start_scatter(scatter_groups["ffn_gate"], "scatter_start_ffn_gate")
    grads["w_ffn_up"] = _matmul(n2, dup, mode="tn", out_dtype=BF16, name="dw_ffn_up", out_blocks=N_DEV, deps=[token])
    scatters["ffn_up"], token = start_scatter(scatter_groups["ffn_up"], "scatter_start_ffn_up")
    dn2 = _matmul(dup, full["w_ffn_up"], mode="nt", out_dtype=F32, name="dn2_up", add=dn2, deps=[token])
    dh1, dh1_bf, d_g2 = _rms_bwd(dn2, h1, dh2, norm_ffn_g, "rms_ffn_bwd")
    dmerged = _matmul(dh1_bf, full["w_out"], mode="nt", out_dtype=F32, name="d_merged")
    grads["w_out"] = _matmul(merged, dh1_bf, mode="tn", out_dtype=BF16, name="dw_out")

    def dmerge_fn(rv, pv):
        d_v, ga_v, gb_v, pa_v, pb_v = rv
        sa, sb = _sigmoid(ga_v), _sigmoid(gb_v)
        dgates = jnp.concatenate([d_v * pa_v * sa * (1.0 - sa), d_v * pb_v * sb * (1.0 - sb)], axis=1)
        return [dgates, d_v * sa, d_v * sb], []
    dproj, dpa, dpb = _rowwise(dmerge_fn, [dmerged, ga, gb, pa, pb], [],
                               [(2 * D, BF16, icp, b_ga // 2, None), (D, BF16), (D, BF16)], [], name="d_merge")
    dya = _matmul(dpa, full["w_proj_rwkv"], mode="nt", out_dtype=F32, name="d_ya")
    dyb = _matmul(dpb, full["w_proj_sgu"], mode="nt", out_dtype=F32, name="d_yb")
    grads["w_proj_rwkv"] = _matmul(ya, dpa, mode="tn", out_dtype=BF16, name="dw_proj_a", out_blocks=N_DEV)
    grads["w_proj_sgu"] = _matmul(yb, dpb, mode="tn", out_dtype=BF16, name="dw_proj_b", out_blocks=N_DEV)
    scatters["mid"], token_mid = start_scatter(scatter_groups["mid"], "scatter_start_mid")
    dproj, d_lng, d_lnb, d_ws, d_bs = _sgu_bwd(proj, b_z, dyb, sgu_ln_g, sgu_ln_b, ws, bexp, dproj, "sgu_bwd")

    dr_h, dlw_h, dk2_h, dv_h, daa, dbb, dg_h, d_lnxg, d_lnxb, d_rk = _wkv_bwd(
        *wkv_in, *wkv_saved, y_h, g_h, hp, dya, "wkv_bwd", deps=[token_mid])
    dproj, d_mu, d_w0, d_a0, d_kk, d_ka, d_wlw, d_wla, d_wlg = _rwkv_pre_bwd(
        proj, mu_p, rsmall, lora, [dr_h, dk2_h, dv_h, dlw_h, daa, dbb, dg_h], dproj, lay, "rwkv_pre_bwd")
    split = lambda g: g.reshape(g.shape[0], N_DEV, -1).transpose(1, 0, 2)
    grads["w_in"] = _dw_in_from_proj(_matmul(n1, dproj, mode="tn", out_dtype=BF16, name="dw_in"), lay, D, w_in.shape[2], "dw_in_layout")
    grads["w_lora_up"] = split(d_wlw[:w_lora_up.shape[1]].astype(BF16))
    grads["a_lora_up"] = split(d_wla[:a_lora_up.shape[1]].astype(BF16))
    grads["g_lora_up"] = split(d_wlg[:g_lora_up.shape[1]].astype(BF16))
    out = {}

    def update_group(key, after):
        handle = scatters[key]
        parts = _exchange_wait(handle, after, "scatter_wait_" + key, rels=SAME_CORE if handle["chips"] else ALL_PEERS)["lands"]
        for n, part in zip(scatter_groups[key], parts):
            res = _adamw(weights[n][0], m_in[n][0], v_in[n][0], part, "adamw_" + n, after=after)
            out[n] = [t.reshape(weights[n].shape) for t in res]
            after = res[0]
        return after

    swap, token_swap = _sibling_swap([grads[n] for n in scatter_groups["last"]], None, None, "scatter_last_swap_start")
    after = update_group("ffn_gate", update_group("ffn_down", token_swap))
    swap = _sibling_swap(None, swap, after, "scatter_last_swap_wait")
    core = lax.axis_index("c").astype(jnp.int32).reshape(1)
    chip_sums = [_pair_add(mine, theirs, core, "scatter_last_add_" + n)
                 for n, mine, theirs in zip(scatter_groups["last"], swap["srcs"], swap["lands"])]
    (scatters["last"],), token_in = _exchange_start([[(s, False) for s in chip_sums]], "scatter_start_last", rels=SAME_CORE, chips=True)
    dn1 = _matmul(dproj, W_in, mode="nt", out_dtype=F32, name="dn1", deps=[token_in])
    dx, _, d_g1 = _rms_bwd(dn1, xs, dh1, norm_mix_g, "rms_mix_bwd")
    small_grads = dict(norm_mix_g=d_g1, shift_mu=_unpad_rwkv_cols(d_mu, lay), w0=d_w0, a0=d_a0, k_k=d_kk, k_a=d_ka, r_k=d_rk,
                       lnx_g=d_lnxg, lnx_b=d_lnxb, sgu_ln_g=d_lng, sgu_ln_b=d_lnb, sgu_w=d_ws, sgu_b=d_bs[:, :G].T,
                       norm_ffn_g=d_g2, norm_final_g=d_gf)
    (gather_small,), after = _exchange_start([[(_pack([small_grads[n] for n in small] + [jnp.zeros_like(gather_token)]), True)]],
                                             "gather_small_start")
    for key in ("ffn_up", "mid", "last"):
        after = update_group(key, after)
    small_parts = _exchange_wait(gather_small, after, "gather_small_wait")["lands"][0]
    res = _adamw(*packed, small_parts, "adamw_small")
    unpacked = [_unpack(t, [weights[n].shape for n in small]) for t in res]
    for i, n in enumerate(small):
        out[n] = [u[i] for u in unpacked]

    loss = lax.psum(loss_part[0, 0], ("x", "y", "c"))
    return (loss, dx[None], *[out[n][0] for n in names], *[out[n][1] for n in names],
            *[out[n][2] for n in names], *[out[n][3] for n in names])
```

```python
import jax
import jax.numpy as jnp
from jax import lax
from jax.experimental import pallas as pl
from jax.experimental.pallas import tpu as pltpu

F32 = jnp.float32
BF16 = jnp.bfloat16

N_DEV = 8
LANE = 128
SUBLANE = 8
HEAD = 64
SGU_CHUNK = 128
SGU_GROUP = 128
WKV_CHUNK = 64
RMS_EPS = 1e-6
LN_EPS = 1e-5
LNX_EPS = 64e-5
ADAM_LR, ADAM_B1, ADAM_B2, ADAM_EPS, ADAM_WD, ADAM_STEP = 0.001, 0.9, 0.999, 1e-08, 0.01, 10
VMEM_LIMIT_BYTES = 48 * 1024 * 1024
_SQRT_HALF = 0.7071067811865476
_INV_SQRT_2PI = 0.3989422804014327


def _pick(n, cands):
    for c in cands:
        if n % c == 0:
            return c
    return n


def _ceil_to(n, m):
    return -(-n // m) * m


def _params():
    return pltpu.CompilerParams(vmem_limit_bytes=VMEM_LIMIT_BYTES)


def _tile(n, cap):
    best = 0
    for d in range(LANE, min(n, cap) + 1, LANE):
        if n % d == 0:
            best = d
    return best or n


def _matmul_tiles(M, N, K, a_bytes, b_bytes, o_bytes, has_add, forced):
    tm = forced.get("m") or _tile(M, 1024)
    tn = forced.get("n") or _tile(N, 1024)
    tk = forced.get("k") or _tile(K, 2048)

    def vmem(tm, tn, tk):
        acc = tm * tn * 4 if tk < K else 0
        return 2 * (tm * tk * a_bytes + tk * tn * b_bytes + tm * tn * (o_bytes + (4 if has_add else 0))) + acc

    while vmem(tm, tn, tk) > (VMEM_LIMIT_BYTES * 3) // 4:
        if "k" not in forced and tk > 512 and _tile(K, tk // 2) < tk:
            tk = _tile(K, tk // 2)
        elif "m" not in forced and _tile(M, tm // 2) < tm:
            tm = _tile(M, tm // 2)
        else:
            break
    return tm, tn, tk


def _matmul(a, b, *, mode, out_dtype=F32, name, add=None, deps=(), out_blocks=0, epi=None, b2=None):
    def view(x):
        return (x.shape[1], x.shape[0] * x.shape[2], x.shape[2]) if x.ndim == 3 else (x.shape[0], x.shape[1], 0)

    (ar, ac, aw), (br, bc, bw) = view(a), view(b)
    a_col, b_col = {"nn": ("k", "n"), "nt": ("k", "k"), "tn": ("m", "n")}[mode]
    if mode == "nn":
        M, K, K2, N = ar, ac, br, bc
    elif mode == "nt":
        M, K, N, K2 = ar, ac, br, bc
    else:
        K, M, K2, N = ar, ac, br, bc
    assert K == K2, (a.shape, b.shape, mode)
    forced = {}
    for dim, w in ((a_col, aw), (b_col, bw), ("n", N // out_blocks if out_blocks else 0)):
        if w:
            assert forced.get(dim, w) == w
            forced[dim] = w
    has_add = add is not None
    tile_bytes = (sum(jnp.dtype(d).itemsize for d in epi[2]) + sum((e[0] if isinstance(e, tuple) else e).dtype.itemsize for e in epi[1])
                  if epi is not None else jnp.dtype(out_dtype).itemsize)
    tm, tn, tk = _matmul_tiles(M, N, K, a.dtype.itemsize, b.dtype.itemsize, tile_bytes, has_add, forced)
    kb = 1
    if "k" in forced and mode != "tn":
        lanes_ok = all(w or tk % LANE == 0 for w in (aw, bw if mode == "nt" else 1))
        kb = next(c for c in (4, 2, 1) if (K // tk) % c == 0 and (c == 1 or (lanes_ok and c * tk <= 1536)))
    nk = K // (tk * kb)
    dn = {"nn": (((1,), (0,)), ((), ())), "nt": (((1,), (1,)), ((), ())), "tn": (((0,), (0,)), ((), ()))}[mode]
    pick = {"m": lambda i, j, k: i, "n": lambda i, j, k: j, "k": lambda i, j, k: k}
    size = {"m": tm, "n": tn, "k": tk}

    def spec(blocked, row_dim, col_dim):
        rf, cf = pick[row_dim], pick[col_dim]
        reps = {d: (kb if d == "k" else 1) for d in (row_dim, col_dim)}
        if blocked:
            lead = kb if col_dim == "k" and kb > 1 else None
            return pl.BlockSpec((lead, size[row_dim], size[col_dim]), lambda i, j, k: (cf(i, j, k), rf(i, j, k), 0))
        return pl.BlockSpec((size[row_dim] * reps[row_dim], size[col_dim] * reps[col_dim]), lambda i, j, k: (rf(i, j, k), cf(i, j, k)))

    def k_part(ref, blocked, k_on_rows, j):
        if kb == 1:
            return ref[...]
        if blocked:
            return ref[j]
        return ref[j * tk:(j + 1) * tk, :] if k_on_rows else ref[:, j * tk:(j + 1) * tk]

    a_spec = spec(aw, "k" if mode == "tn" else "m", a_col)
    b_spec = spec(bw, "n" if mode == "nt" else "k", b_col)
    o_spec = spec(out_blocks, "m", "n")
    epi_fn, epi_ins, epi_dtypes = epi if epi is not None else (None, [], [out_dtype])
    epi_ins = [e if isinstance(e, tuple) else (e, None) for e in epi_ins]
    n_epi = len(epi_ins)
    twin = b2 is not None
    assert not twin or (nk == 1 and kb == 1 and epi is not None and b2.shape == b.shape)
    n_in = 2 + twin + has_add + n_epi + len(deps)
    n_out = len(epi_dtypes)

    def body(*refs):
        a_ref, b_ref = refs[0], refs[1]
        add_ref = refs[2 + twin] if has_add else None
        epi_refs = refs[2 + twin + has_add:2 + twin + has_add + n_epi]
        o_refs = refs[n_in:n_in + n_out]
        part = None
        for q in range(kb):
            a_q = k_part(a_ref, aw and a_col == "k", False, q)
            b_q = k_part(b_ref, bw and b_col == "k", mode == "nn", q)
            prod = lax.dot_general(a_q.astype(BF16), b_q.astype(BF16), dn, preferred_element_type=F32)
            part = prod if part is None else part + prod
        second = [lax.dot_general(a_ref[...].astype(BF16), refs[2][...].astype(BF16), dn, preferred_element_type=F32)] if twin else []

        def finish(res):
            outs = epi_fn(res, *second, *[e[...] for e in epi_refs]) if epi_fn is not None else (res,)
            for o_ref, val in zip(o_refs, outs):
                o_ref[...] = val.astype(o_ref.dtype)

        if nk == 1:
            finish(part + add_ref[...] if has_add else part)
            return
        acc_ref = refs[-1]
        kk = pl.program_id(2)

        @pl.when(kk == 0)
        def _():
            acc_ref[...] = part + add_ref[...] if has_add else part

        @pl.when(kk > 0)
        def _():
            acc_ref[...] += part

        @pl.when(kk == nk - 1)
        def _():
            finish(acc_ref[...])

    def epi_spec(arr, off):
        if off is None:
            return o_spec
        assert off % tn == 0
        return pl.BlockSpec((tm, tn), lambda i, j, k: (i, j + off // tn))

    ins = [a, b] + ([b2] if twin else []) + ([add] if has_add else []) + [arr for arr, _ in epi_ins] + list(deps)
    in_specs = ([a_spec, b_spec] + ([b_spec] if twin else []) + ([o_spec] if has_add else []) + [epi_spec(arr, off) for arr, off in epi_ins]
                + [pl.BlockSpec(d.shape, lambda i, j, k, nd=d.ndim: (0,) * nd) for d in deps])
    o_shape = (out_blocks, M, tn) if out_blocks else (M, N)
    res = pl.pallas_call(
        body, name=name, grid=(M // tm, N // tn, nk), in_specs=in_specs, out_specs=[o_spec] * n_out,
        out_shape=[jax.ShapeDtypeStruct(o_shape, dt) for dt in epi_dtypes],
        scratch_shapes=[pltpu.VMEM((tm, tn), F32)] if nk > 1 else [],
        compiler_params=_params())(*ins)
    return res[0] if epi is None else list(res)


def _rowwise(fn, rows, pars, row_outs, acc_outs, *, name, tm=256, deps=()):
    rows = [r if isinstance(r, tuple) else (r, r.shape[1], 0) for r in rows]
    row_outs = [o if len(o) == 5 else (o[0], o[1], o[0], 0, None) for o in row_outs]
    aliased = [(k, o[4]) for k, o in enumerate(row_outs) if o[4] is not None]
    R = rows[0][0].shape[0]
    if max(w for _, w, _ in rows) > 4096:
        tm = tm // 2
    per_row = 2 * (sum(w * a.dtype.itemsize for a, w, _ in rows) + sum(o[0] * jnp.dtype(o[1]).itemsize for o in row_outs))
    if 2 * tm * per_row <= VMEM_LIMIT_BYTES // 2 and R % (2 * tm) == 0:
        tm = 2 * tm
    tm = min(tm, R)
    assert R % tm == 0
    nr, npar = len(rows), len(pars)
    nro = len(row_outs)
    n_in = nr + npar + len(deps) + len(aliased)

    def body(*refs):
        rv = [r[...] for r in refs[:nr]]
        pv = [p[...] for p in refs[nr:nr + npar]]
        outs = refs[n_in:]
        ro, ao = fn(rv, pv)
        first = pl.program_id(0) == 0
        for o_ref, val in zip(outs[:nro], ro):
            o_ref[...] = val.astype(o_ref.dtype)

        @pl.when(first)
        def _():
            for o_ref, val in zip(outs[nro:], ao):
                o_ref[...] = val

        @pl.when(jnp.logical_not(first))
        def _():
            for o_ref, val in zip(outs[nro:], ao):
                o_ref[...] += val

    in_specs = ([pl.BlockSpec((tm, w), lambda i, cb=cb: (i, cb)) for _, w, cb in rows]
                + [pl.BlockSpec(p.shape, lambda i, nd=p.ndim: (0,) * nd) for p in list(pars) + list(deps)]
                + [pl.BlockSpec(memory_space=pl.ANY)] * len(aliased))
    out_shape = ([jax.ShapeDtypeStruct((R, full), dt) for _, dt, full, _, _ in row_outs]
                 + [jax.ShapeDtypeStruct(s, F32) for s in acc_outs])
    out_specs = ([pl.BlockSpec((tm, f), lambda i, cb=cb: (i, cb)) for f, _, _, cb, _ in row_outs]
                 + [pl.BlockSpec(s, lambda i, nd=len(s): (0,) * nd) for s in acc_outs])
    res = pl.pallas_call(body, name=name, grid=(R // tm,), in_specs=in_specs, out_specs=out_specs, out_shape=out_shape,
                         input_output_aliases={n_in - len(aliased) + q: k for q, (k, _) in enumerate(aliased)},
                         compiler_params=_params())(*[r for r, _, _ in rows], *pars, *deps, *[buf for _, buf in aliased])
    return list(res)


def _bdot(a, b, mode="nn"):
    dn = {"nn": (((1,), (0,)), ((), ())), "nt": (((1,), (1,)), ((), ())), "tn": (((0,), (0,)), ((), ()))}[mode]
    return lax.dot_general(a.astype(BF16), b.astype(BF16), dn, preferred_element_type=F32)


def _sigmoid(x):
    return jax.nn.sigmoid(x)


def _softplus(x):
    return jnp.maximum(x, 0.0) + jnp.log1p(jnp.exp(-jnp.abs(x)))


def _gelu(z):
    return 0.5 * z * (1.0 + lax.erf(z * _SQRT_HALF))


def _gelu_grad(z):
    return 0.5 * (1.0 + lax.erf(z * _SQRT_HALF)) + z * jnp.exp(-0.5 * z * z) * _INV_SQRT_2PI


def _mean(x):
    return jnp.mean(x, axis=-1, keepdims=True)


def _colsum(x):
    return jnp.sum(x, axis=0, keepdims=True)


def _rms_fwd(x, g, name, deps=()):
    def fn(rv, pv):
        (xv,), (gv,) = rv, pv
        r = lax.rsqrt(_mean(xv * xv) + RMS_EPS)
        return [xv * r * gv], []
    return _rowwise(fn, [x], [g], [(x.shape[1], BF16)], [], name=name, deps=deps)[0]


def _rms_bwd(dn, x, dres, g, name, deps=()):
    def fn(rv, pv):
        (dnv, xv, drv), (gv,) = rv, pv
        r = lax.rsqrt(_mean(xv * xv) + RMS_EPS)
        yn = xv * r
        dyg = dnv * gv
        dx = drv + r * (dyg - yn * _mean(dyg * yn))
        return [dx, dx], [_colsum(dnv * yn)]
    D = x.shape[1]
    return _rowwise(fn, [dn, x, dres], [g], [(D, F32), (D, BF16)], [(1, D)], name=name, deps=deps)


def _rwkv_layout(RW, Lw, La, Lg, D):
    widths = [RW, RW, RW, Lw, La, Lg]
    pw = [_ceil_to(w, LANE) for w in widths]
    pw[5] += _ceil_to(sum(pw), 2 * D) - sum(pw)
    offs = [sum(pw[:i]) for i in range(6)]
    return widths, pw, offs, sum(pw)


def _pad_rwkv_cols(a, lay):
    widths, pw, _, _ = lay
    pieces, src = [], 0
    for w, p in zip(widths, pw):
        pieces.append(a[:, src:src + w])
        if p > w:
            pieces.append(jnp.zeros((a.shape[0], p - w), a.dtype))
        src += w
    return jnp.concatenate(pieces, axis=1)


def _unpad_rwkv_cols(a, lay):
    widths, _, offs, _ = lay
    return jnp.concatenate([a[:, o:o + w] for o, w in zip(offs, widths)], axis=1)


def _proj_pieces(lay, D, cs):
    widths, _, offs, rcp = lay
    rc = sum(widths)
    segs = [(sum(widths[:j]), widths[j], offs[j]) for j in range(6)] + [(rc, D, rcp + 2 * D), (rc + D, D, rcp), (rc + 2 * D, D, rcp + D)]
    pieces = []
    for start, width, dst in segs:
        n = start
        while n < start + width:
            d, off = divmod(n, cs)
            take = min(cs - off, start + width - n)
            pieces.append((d, off, dst + n - start, take))
            n += take
    return pieces


def _w_in_to_proj(g, lay, D, name):
    nb, rows, cs = g.shape
    icp = lay[3] + 3 * D
    pieces = _proj_pieces(lay, D, cs)
    tm = _pick(rows, (256, 128, 64, 32, 16))

    def body(i_ref, o_ref):
        o_ref[...] = jnp.zeros_like(o_ref)
        for d, src, dst, w in pieces:
            o_ref[:, dst:dst + w] = i_ref[d, :, src:src + w]

    return pl.pallas_call(
        body, name=name, grid=(rows // tm,), in_specs=[pl.BlockSpec((nb, tm, cs), lambda i: (0, i, 0))],
        out_specs=pl.BlockSpec((tm, icp), lambda i: (i, 0)), out_shape=jax.ShapeDtypeStruct((rows, icp), g.dtype),
        compiler_params=_params())(g)


def _dw_in_from_proj(a, lay, D, cs, name):
    rows, icp = a.shape
    pieces = _proj_pieces(lay, D, cs)
    tm = _pick(rows, (256, 128, 64, 32, 16))

    def body(i_ref, o_ref):
        for d, src, dst, w in pieces:
            o_ref[d, :, src:src + w] = i_ref[:, dst:dst + w]

    return pl.pallas_call(
        body, name=name, grid=(rows // tm,), in_specs=[pl.BlockSpec((tm, icp), lambda i: (i, 0))],
        out_specs=pl.BlockSpec((N_DEV, tm, cs), lambda i: (0, i, 0)), out_shape=jax.ShapeDtypeStruct((N_DEV, rows, cs), a.dtype),
        compiler_params=_params())(a)


def _pad_rows(a, rows):
    return a if a.shape[0] == rows else jnp.concatenate([a, jnp.zeros((rows - a.shape[0], a.shape[1]), a.dtype)], axis=0)


def _token_shift(p, halo, mu, i):
    tm = p.shape[0]
    hid = lax.broadcasted_iota(jnp.int32, (SUBLANE, 1), 0)
    before = jnp.sum(jnp.where(hid == SUBLANE - 1, halo, 0.0), axis=0, keepdims=True)
    before = jnp.where(i == 0, 0.0, before)
    rid = lax.broadcasted_iota(jnp.int32, (tm, 1), 0)
    prev = jnp.where(rid == 0, before, pltpu.roll(p, 1, 0))
    d = prev - p
    return p + d * mu, d


def _rwkv_math(ps, w0, a0, k_k, k_a, wlw, wla, wlg, lay):
    _, pw, offs, _ = lay
    r, k, v, xw, xa, xg = (ps[:, offs[j]:offs[j] + pw[j]] for j in range(6))
    tw = jnp.tanh(xw)
    ww = w0 + _bdot(tw, wlw)
    lw = -jnp.exp(-_softplus(-ww) - 0.5)
    a = _sigmoid(a0 + _bdot(xa, wla))
    sg = _sigmoid(xg)
    g = _bdot(sg, wlg)
    return dict(r=r, k=k, v=v, xa=xa, tw=tw, ww=ww, lw=lw, a=a, sg=sg, g=g, kkp=k * k_k, k2=k * (1.0 + (a - 1.0) * k_a))


def _halo_spec(tm, width):
    hb = tm // SUBLANE
    return pl.BlockSpec((SUBLANE, width), lambda i: (jnp.maximum(i * hb - 1, 0), 0))


def _rowsum(x):
    return jnp.sum(x, axis=-1, keepdims=True)


def _kk_math(kkp):
    nrm = jnp.sqrt(_rowsum(kkp * kkp))
    inv = 1.0 / jnp.maximum(nrm, 1e-12)
    return nrm, inv, kkp * inv


def _rwkv_pre(p, mu, small, lora, lay, name):
    T, rcp = p.shape[0], lay[3]
    H = lay[0][0] // HEAD
    tm = min(256, T)

    def body(p_ref, ph_ref, mu_ref, w0_ref, a0_ref, kk_ref, ka_ref, wlw_ref, wla_ref, wlg_ref, r_o, lw_o, k2_o, v_o, aa_o, bb_o, g_o):
        ps, _ = _token_shift(p_ref[...], ph_ref[...], mu_ref[...], pl.program_id(0))
        q = _rwkv_math(ps, w0_ref[...], a0_ref[...], kk_ref[...], ka_ref[...], wlw_ref[...], wla_ref[...], wlg_ref[...], lay)
        for h in range(H):
            sl = slice(h * HEAD, (h + 1) * HEAD)
            for o_ref, key in ((r_o, "r"), (lw_o, "lw"), (k2_o, "k2"), (v_o, "v"), (g_o, "g")):
                o_ref[h] = q[key][:, sl]
            _, _, kk = _kk_math(q["kkp"][:, sl])
            aa_o[h] = -kk
            bb_o[h] = kk * q["a"][:, sl]

    whole = lambda arr: pl.BlockSpec(arr.shape, lambda i: (0, 0))
    return pl.pallas_call(
        body, name=name, grid=(T // tm,),
        in_specs=([pl.BlockSpec((tm, rcp), lambda i: (i, 0)), _halo_spec(tm, rcp), whole(mu)]
                  + [whole(s) for s in small] + [whole(w) for w in lora]),
        out_specs=[pl.BlockSpec((H, tm, HEAD), lambda i: (0, i, 0))] * 7, out_shape=[jax.ShapeDtypeStruct((H, T, HEAD), F32)] * 7,
        compiler_params=_params())(p, p, mu, *small, *lora)


def _rwkv_pre_bwd(p, mu, small, lora, hgrads, dproj, lay, name):
    T, rcp = p.shape[0], lay[3]
    widths, pw, offs, _ = lay
    RW = widths[0]
    H = RW // HEAD
    tm = min(128, T)
    nt = T // tm
    hb = tm // SUBLANE

    def body(p_ref, ph_ref, mu_ref, w0_ref, a0_ref, kk_ref, ka_ref, wlw_ref, wla_ref, wlg_ref,
             dr_h, dk2_h, dv_h, dlw_h, daa, dbb, dg_h, buf_ref,
             dp_ref, dmu_ref, dw0_ref, da0_ref, dkk_ref, dka_ref, dwlw_ref, dwla_ref, dwlg_ref,
             s_dr, s_dk2, s_dv, s_dlw, s_dkkp, s_da, s_dg, dps_ref, next_ref):
        i = pl.program_id(0)
        ps, dprev = _token_shift(p_ref[...], ph_ref[...], mu_ref[...], nt - 1 - i)
        k_k, k_a = kk_ref[...], ka_ref[...]
        q = _rwkv_math(ps, w0_ref[...], a0_ref[...], k_k, k_a, wlw_ref[...], wla_ref[...], wlg_ref[...], lay)
        k, a, lw, ww, tw, sg = q["k"], q["a"], q["lw"], q["ww"], q["tw"], q["sg"]
        for h in range(H):
            sl = slice(h * HEAD, (h + 1) * HEAD)
            s_dr[:, sl] = dr_h[h]
            s_dk2[:, sl] = dk2_h[h]
            s_dv[:, sl] = dv_h[h]
            s_dlw[:, sl] = dlw_h[h]
            s_dg[:, sl] = dg_h[h]
            nrm, inv, kk = _kk_math(q["kkp"][:, sl])
            dbb_h = dbb[h]
            dkk = dbb_h * a[:, sl] - daa[h]
            s_dkkp[:, sl] = jnp.where(nrm > 1e-12, inv * (dkk - kk * _rowsum(dkk * kk)), dkk * inv)
            s_da[:, sl] = dbb_h * kk
        dk2, dkkp, dg = s_dk2[...], s_dkkp[...], s_dg[...]
        dk = dk2 * (1.0 + (a - 1.0) * k_a) + dkkp * k_k
        da = s_da[...] + dk2 * k * k_a
        dpa = da * a * (1.0 - a)
        dww = s_dlw[...] * lw * _sigmoid(-ww)
        dxa = _bdot(dpa, wla_ref[...], "nt")
        dxw = _bdot(dww, wlw_ref[...], "nt") * (1.0 - tw * tw)
        dxg = _bdot(dg, wlg_ref[...], "nt") * sg * (1.0 - sg)
        segs = (s_dr[...], dk, s_dv[...], dxw, dxa, dxg)
        sums = [dmu_ref, dw0_ref, da0_ref, dkk_ref, dka_ref, dwlw_ref, dwla_ref, dwlg_ref]

        @pl.when(i == 0)
        def _():
            for s in sums + [next_ref]:
                s[...] = jnp.zeros_like(s)

        for j, seg in enumerate(segs):
            sl = slice(offs[j], offs[j] + pw[j])
            dps_ref[:, sl] = seg
            dmu_ref[:, sl] += _colsum(seg * dprev[:, sl])
        dw0_ref[...] += _colsum(dww)
        da0_ref[...] += _colsum(dpa)
        dkk_ref[...] += _colsum(dkkp * k)
        dka_ref[...] += _colsum(dk2 * k * (a - 1.0))
        dwlw_ref[...] += _bdot(tw, dww, "tn")
        dwla_ref[...] += _bdot(q["xa"], dpa, "tn")
        dwlg_ref[...] += _bdot(sg, dg, "tn")
        dps = dps_ref[...]
        rid = lax.broadcasted_iota(jnp.int32, (tm, 1), 0)
        nxt = jnp.where(rid == tm - 1, next_ref[...], pltpu.roll(dps, tm - 1, 0))
        mu_v = mu_ref[...]
        dp_ref[...] = (dps * (1.0 - mu_v) + nxt * mu_v).astype(BF16)
        next_ref[...] = _colsum(jnp.where(rid == 0, dps, 0.0))

    whole = lambda arr: pl.BlockSpec(arr.shape, lambda i: (0, 0))
    row = lambda w: pl.BlockSpec((tm, w), lambda i: (nt - 1 - i, 0))
    acc_shapes = [(1, rcp), (1, RW), (1, RW), (1, RW), (1, RW)] + [w.shape for w in lora]
    return pl.pallas_call(
        body, name=name, grid=(nt,),
        in_specs=([row(rcp), pl.BlockSpec((SUBLANE, rcp), lambda i: (jnp.maximum((nt - 1 - i) * hb - 1, 0), 0)), whole(mu)]
                  + [whole(s) for s in small] + [whole(w) for w in lora]
                  + [pl.BlockSpec((H, tm, HEAD), lambda i: (0, nt - 1 - i, 0))] * 7 + [pl.BlockSpec(memory_space=pl.ANY)]),
        out_specs=[row(rcp)] + [pl.BlockSpec(s, lambda i: (0, 0)) for s in acc_shapes],
        out_shape=[jax.ShapeDtypeStruct(dproj.shape, BF16)] + [jax.ShapeDtypeStruct(s, F32) for s in acc_shapes],
        scratch_shapes=[pltpu.VMEM((tm, RW), F32)] * 7 + [pltpu.VMEM((tm, rcp), F32), pltpu.VMEM((1, rcp), F32)],
        input_output_aliases={10 + 7: 0}, compiler_params=_params())(p, p, mu, *small, *lora, *hgrads, dproj)


def _head_post_math(y, r, k2, v, lg, lb, rk):
    yc = y - _mean(y)
    rstd = lax.rsqrt(_mean(yc * yc) + LNX_EPS)
    yn = yc * rstd
    s = _rowsum(r * k2 * rk)
    return yn, rstd, yn * lg + lb + s * v, s


def _head_post(y, r, k2, v, g, hp, name, deps=()):
    H, T, _ = y.shape
    tm = min(256, T)

    def body(y_ref, r_ref, k_ref, v_ref, g_ref, lg_ref, lb_ref, rk_ref, *rest):
        o_ref = rest[-1]
        _, _, t, _ = _head_post_math(y_ref[...], r_ref[...], k_ref[...], v_ref[...], lg_ref[...], lb_ref[...], rk_ref[...])
        out = (t * g_ref[...]).astype(BF16)
        for h in range(H):
            o_ref[:, h * HEAD:(h + 1) * HEAD] = out[h]

    blk = pl.BlockSpec((H, tm, HEAD), lambda i: (0, i, 0))
    par = pl.BlockSpec((H, 1, HEAD), lambda i: (0, 0, 0))
    return pl.pallas_call(
        body, name=name, grid=(T // tm,),
        in_specs=[blk] * 5 + [par] * 3 + [pl.BlockSpec(d.shape, lambda i, nd=d.ndim: (0,) * nd) for d in deps],
        out_specs=pl.BlockSpec((tm, H * HEAD), lambda i: (i, 0)),
        out_shape=jax.ShapeDtypeStruct((T, H * HEAD), BF16), compiler_params=_params())(y, r, k2, v, g, *hp, *deps)


def _bmm(x, y, mode):
    dn = {"nn": (((2,), (1,)), ((0,), (0,))), "nt": (((2,), (2,)), ((0,), (0,))), "tn": (((1,), (1,)), ((0,), (0,)))}[mode]
    (xh, xl), (yh, yl) = _split(x), _split(y)
    dot = lambda p, q: lax.dot_general(p, q, dn, preferred_element_type=F32)
    out = dot(xh, yh)
    if yl is not None:
        out = out + dot(xh, yl)
    if xl is not None:
        out = out + dot(xl, yh)
    return out


def _split(x):
    if isinstance(x, tuple):
        return x
    hi = x.astype(BF16)
    return hi, (x - hi.astype(F32)).astype(BF16)


def _exact(x):
    return x.astype(BF16), None


def _round(x):
    return x if isinstance(x, tuple) else (x.astype(BF16), None)


def _rows(*xs):
    if isinstance(xs[0], tuple):
        return tuple(None if any(p is None for p in parts) else jnp.concatenate(parts, axis=1) for parts in zip(*xs))
    return jnp.concatenate(xs, axis=1)


def _wkv_chunk(r, lw, k, v, a, b, inverse=None):
    hb, C, _ = r.shape
    ti = lax.broadcasted_iota(jnp.int32, (C, C), 0)
    si = lax.broadcasted_iota(jnp.int32, (C, C), 1)
    linc, lstr, eye = (ti >= si).astype(F32), (ti > si).astype(F32), (ti == si).astype(F32)
    qmask = jnp.concatenate([jnp.concatenate([lstr, lstr], axis=1), jnp.concatenate([linc, linc], axis=1)], axis=0)
    lincb = _exact(jnp.broadcast_to(linc, (hb, C, C)))
    both = _exact(jnp.broadcast_to(jnp.concatenate([linc, lstr], axis=0), (hb, 2 * C, C)))
    ones = _exact(jnp.ones_like(v))
    lws = _split(lw)
    ci = _bmm(lincb, lws, "nn")
    cC = jnp.sum(lw, axis=1, keepdims=True)
    gi, ge, gn, gr = jnp.exp(ci), jnp.exp(ci - lw), jnp.exp(-ci), jnp.exp(cC - ci)
    q = dict(At=a * ge, Rt=r * gi, Bt=b * gn, Kt=k * gn, Bh=b * gr, Kh=k * gr)
    s = dict(AR=_round(_rows(q["At"], q["Rt"])), BK=_round(_rows(q["Bt"], q["Kt"])), BKh=_round(_rows(q["Bh"], q["Kh"])), v=_round(v))
    quad = _bmm(s["AR"], s["BK"], "nt") * qmask
    s["top"], s["bot"] = _round(quad[:, :C]), _round(quad[:, C:])
    if inverse is None:
        A_ab = quad[:, :C, :C]
        Tm = eye + A_ab
        Pw = _round(A_ab)
        n = 1
        while 2 * n < C:
            Pw = _round(_bmm(Pw, Pw, "nn"))
            Tm = Tm + _bmm(_round(Tm), Pw, "nn")
            n *= 2
        inverse = Tm
    s["Tm"] = _round(inverse)
    gC = jnp.exp(_bmm(lws, ones, "tn"))
    q.update(gi=gi, ge=ge, gn=gn, gr=gr, qmask=qmask, both=both, gC=gC, ones=ones, s=s)
    return q


def _wkv_u(s, H0s, C):
    arh = _bmm(s["AR"], H0s, "nn")
    zv = _rows(tuple(None if p is None else jnp.zeros_like(p) for p in s["v"]), s["v"])
    U = _bmm(s["Tm"], _round(arh[:, :C] + _bmm(s["top"], zv, "nn")), "nn")
    return arh, _rows(_round(U), s["v"])


def _wkv_fwd(r, lw, k, v, a, b, name):
    H, T, N = r.shape
    C = min(WKV_CHUNK, T)
    nc = T // C
    hb = _pick(H, (16, 8, 4, 2))

    def body(r_ref, lw_ref, k_ref, v_ref, a_ref, b_ref, y_ref, st_ref, inv_ref, u_ref, h_ref):
        @pl.when(pl.program_id(1) == 0)
        def _():
            h_ref[...] = jnp.zeros_like(h_ref)

        H0 = h_ref[...]
        st_ref[0] = H0
        q = _wkv_chunk(r_ref[...], lw_ref[...], k_ref[...], v_ref[...], a_ref[...], b_ref[...])
        s = q["s"]
        arh, UV = _wkv_u(s, _round(H0), C)
        inv_ref[0] = s["Tm"][0]
        u_ref[...] = UV[0][:, :C]
        y_ref[...] = arh[:, C:] + _bmm(s["bot"], UV, "nn")
        h_ref[...] = q["gC"] * H0 + _bmm(s["BKh"], UV, "tn")

    blk = pl.BlockSpec((hb, C, N), lambda h, c: (h, c, 0))
    per_chunk = lambda w: pl.BlockSpec((1, hb, w, w), lambda h, c: (c, h, 0, 0))
    return pl.pallas_call(
        body, name=name, grid=(H // hb, nc), in_specs=[blk] * 6, out_specs=[blk, per_chunk(N), per_chunk(C), blk],
        out_shape=[jax.ShapeDtypeStruct((H, T, N), F32), jax.ShapeDtypeStruct((nc, H, N, N), F32),
                   jax.ShapeDtypeStruct((nc, H, C, C), BF16), jax.ShapeDtypeStruct((H, T, N), BF16)],
        scratch_shapes=[pltpu.VMEM((hb, N, N), F32)], compiler_params=_params())(r, lw, k, v, a, b)


def _wkv_bwd(r, lw, k, v, a, b, states, inverses, u, y, g, hp, dya, name, deps=()):
    H, T, N = r.shape
    C = min(WKV_CHUNK, T)
    nc = T // C
    hb = _pick(H, (16, 8, 4, 2))
    hsum = lambda t: jnp.sum(t, axis=1, keepdims=True)

    def body(r_ref, lw_ref, k_ref, v_ref, a_ref, b_ref, st_ref, inv_ref, u_ref, y_ref, g_ref, lg_ref, lb_ref, rk_ref, dya_ref, *rest):
        (dr_ref, dlw_ref, dk_ref, dv_ref, da_ref, db_ref, dg_ref, dlg_ref, dlb_ref, drk_ref, dh_ref, d_s) = rest[len(deps):]
        first = pl.program_id(1) == 0

        @pl.when(first)
        def _():
            dh_ref[...] = jnp.zeros_like(dh_ref)

        for h in range(hb):
            d_s[h] = dya_ref[:, h * N:(h + 1) * N]
        d_v, r_v, k_v, v_v, lg, rk = d_s[...], r_ref[...], k_ref[...], v_ref[...], lg_ref[...], rk_ref[...]
        yn, rstd, t, bonus = _head_post_math(y_ref[...], r_v, k_v, v_v, lg, lb_ref[...], rk)
        dyo = d_v * g_ref[...]
        dyn = dyo * lg
        ds = _rowsum(dyo * v_v)
        dy = rstd * (dyn - _mean(dyn) - yn * _mean(dyn * yn))
        dg_ref[...] = d_v * t
        sums = (hsum(dyo * yn), hsum(dyo), hsum(ds * r_v * k_v))

        @pl.when(first)
        def _():
            for o_ref, val in zip((dlg_ref, dlb_ref, drk_ref), sums):
                o_ref[...] = val

        @pl.when(jnp.logical_not(first))
        def _():
            for o_ref, val in zip((dlg_ref, dlb_ref, drk_ref), sums):
                o_ref[...] += val

        dHC = dh_ref[...]
        H0 = st_ref[0]
        q = _wkv_chunk(r_v, lw_ref[...], k_v, v_v, a_ref[...], b_ref[...], inverse=inv_ref[0])
        s, gC = q["s"], q["gC"]
        H0s, dHs, dY = _round(H0), _round(dHC), _round(dy)
        UV = _rows(_round(u_ref[...]), s["v"])
        bot_dy = _bmm(s["bot"], dY, "tn")
        bkh_dh = _bmm(s["BKh"], dHs, "nn")
        dP = _round(_bmm(s["Tm"], _round(bot_dy[:, :C] + bkh_dh[:, :C]), "tn"))
        dv_ref[...] = bot_dy[:, C:] + bkh_dh[:, C:] + _bmm(s["top"], dP, "tn")[:, C:] + dyo * bonus
        dPY = _rows(dP, dY)
        dh_ref[...] = gC * dHC + _bmm(s["AR"], dPY, "tn")
        dquad = _round(_bmm(dPY, UV, "nt") * q["qmask"])
        dAR = _bmm(dPY, H0s, "nt") + _bmm(dquad, s["BK"], "nn")
        dBK = _bmm(dquad, s["AR"], "tn")
        dBKh = _bmm(UV, dHs, "nt")
        dAt, dRt, dBt, dKt, dBh, dKh = dAR[:, :C], dAR[:, C:], dBK[:, :C], dBK[:, C:], dBKh[:, :C], dBKh[:, C:]
        dr_ref[...] = dRt * q["gi"] + ds * k_v * rk
        da_ref[...] = dAt * q["ge"]
        db_ref[...] = dBt * q["gn"] + dBh * q["gr"]
        dk_ref[...] = dKt * q["gn"] + dKh * q["gr"] + ds * r_v * rk
        tail = dBh * q["Bh"] + dKh * q["Kh"]
        dci = dRt * q["Rt"] - dBt * q["Bt"] - dKt * q["Kt"] - tail
        dcC = jnp.sum(tail, axis=1, keepdims=True) + _bmm(q["ones"], H0 * dHC * gC, "nt")
        dlw_ref[...] = _bmm(q["both"], _rows(dci, dAt * q["At"]), "tn") + dcC

    blk = pl.BlockSpec((hb, C, N), lambda h, c: (h, nc - 1 - c, 0))
    per_chunk = lambda w: pl.BlockSpec((1, hb, w, w), lambda h, c: (nc - 1 - c, h, 0, 0))
    par = pl.BlockSpec((hb, 1, N), lambda h, c: (h, 0, 0))
    return pl.pallas_call(
        body, name=name, grid=(H // hb, nc),
        in_specs=([blk] * 6 + [per_chunk(N), per_chunk(C), blk, blk, blk] + [par] * 3
                  + [pl.BlockSpec((C, hb * N), lambda h, c: (nc - 1 - c, h))]
                  + [pl.BlockSpec(d.shape, lambda h, c, nd=d.ndim: (0,) * nd) for d in deps]),
        out_specs=[blk] * 7 + [par] * 3,
        out_shape=[jax.ShapeDtypeStruct((H, T, N), F32)] * 7 + [jax.ShapeDtypeStruct((H, 1, N), F32)] * 3,
        scratch_shapes=[pltpu.VMEM((hb, N, N), F32), pltpu.VMEM((hb, C, N), F32)],
        compiler_params=_params())(r, lw, k, v, a, b, states, inverses, u, y, g, *hp, dya, *deps)


def _sgu_ln(z, SW, lng, lnb):
    ge = _gelu(z)
    u, vv = ge[:, :SW], ge[:, SW:]
    xc = vv - _mean(vv)
    rstd = lax.rsqrt(_mean(xc * xc) + LN_EPS)
    vn = xc * rstd
    return u, vn, rstd, vn * lng + lnb


def _causal(ws_ref, g):
    ti = lax.broadcasted_iota(jnp.int32, (SGU_CHUNK, SGU_CHUNK), 0)
    si = lax.broadcasted_iota(jnp.int32, (SGU_CHUNK, SGU_CHUNK), 1)
    return ti >= si, jnp.where(ti >= si, ws_ref[g], 0.0).astype(BF16)


def _sgu_fwd(proj, zblock, lng, lnb, ws, bexp, name):
    T, SW = proj.shape[0], lng.shape[1]
    G = ws.shape[0]
    tr = min(512, T)
    nch = tr // SGU_CHUNK

    def body(z_ref, lng_ref, lnb_ref, ws_ref, be_ref, o_ref):
        u, _, _, vl = _sgu_ln(z_ref[...], SW, lng_ref[...], lnb_ref[...])
        for g in range(G):
            cs = slice(g * SGU_GROUP, (g + 1) * SGU_GROUP)
            _, wc = _causal(ws_ref, g)
            for n in range(nch):
                rs = slice(n * SGU_CHUNK, (n + 1) * SGU_CHUNK)
                m = jnp.dot(wc, vl[rs, cs].astype(BF16), preferred_element_type=F32) + be_ref[:, cs]
                o_ref[rs, cs] = (u[rs, cs] * m).astype(BF16)

    whole = lambda arr: pl.BlockSpec(arr.shape, lambda i, nd=arr.ndim: (0,) * nd)
    return pl.pallas_call(
        body, name=name, grid=(T // tr,),
        in_specs=[pl.BlockSpec((tr, 2 * SW), lambda i: (i, zblock)), whole(lng), whole(lnb), whole(ws), whole(bexp)],
        out_specs=pl.BlockSpec((tr, SW), lambda i: (i, 0)), out_shape=jax.ShapeDtypeStruct((T, SW), BF16),
        compiler_params=_params())(proj, lng, lnb, ws, bexp)


def _sgu_bwd(proj, zblock, dyb, lng, lnb, ws, bexp, dproj, name):
    T, SW = proj.shape[0], lng.shape[1]
    G = ws.shape[0]
    tr = min(512, T)
    nch = tr // SGU_CHUNK
    nt = T // tr

    def body(z_ref, dy_ref, lng_ref, lnb_ref, ws_ref, be_ref, buf_ref, dz_ref, dlg_ref, dlb_ref, dws_ref, db_ref, du_s, dvl_s, dbacc_s):
        i = pl.program_id(0)
        zv = z_ref[...]
        lng_v = lng_ref[...]
        u, vn, rstd, vl = _sgu_ln(zv, SW, lng_v, lnb_ref[...])

        @pl.when(i == 0)
        def _():
            for s in (dlg_ref, dlb_ref, dws_ref, dbacc_s):
                s[...] = jnp.zeros_like(s)

        for g in range(G):
            cs = slice(g * SGU_GROUP, (g + 1) * SGU_GROUP)
            tri, wc = _causal(ws_ref, g)
            for n in range(nch):
                rs = slice(n * SGU_CHUNK, (n + 1) * SGU_CHUNK)
                blk = vl[rs, cs].astype(BF16)
                m = jnp.dot(wc, blk, preferred_element_type=F32) + be_ref[:, cs]
                dyv = dy_ref[rs, cs]
                du_s[rs, cs] = dyv * m
                dm = dyv * u[rs, cs]
                dvl_s[rs, cs] = _bdot(wc, dm, "tn")
                dws_ref[g] += jnp.where(tri, _bdot(dm, blk, "nt"), 0.0)
                dbacc_s[:, cs] += dm

        dvl = dvl_s[...]
        dlg_ref[...] += _colsum(dvl * vn)
        dlb_ref[...] += _colsum(dvl)
        dvn = dvl * lng_v
        dvv = rstd * (dvn - _mean(dvn) - vn * _mean(dvn * vn))
        gp = _gelu_grad(zv)
        dz_ref[:, :SW] = (du_s[...] * gp[:, :SW]).astype(BF16)
        dz_ref[:, SW:] = (dvv * gp[:, SW:]).astype(BF16)

        @pl.when(i == nt - 1)
        def _():
            lane = lax.broadcasted_iota(jnp.int32, (SGU_CHUNK, LANE), 1)
            out = jnp.zeros((SGU_CHUNK, LANE), F32)
            for g in range(G):
                col = jnp.sum(dbacc_s[:, g * SGU_GROUP:(g + 1) * SGU_GROUP], axis=1, keepdims=True)
                out = jnp.where(lane == g, col, out)
            db_ref[...] = out

    whole = lambda arr: pl.BlockSpec(arr.shape, lambda i, nd=arr.ndim: (0,) * nd)
    acc_shapes = [(1, SW), (1, SW), ws.shape, (SGU_CHUNK, LANE)]
    return pl.pallas_call(
        body, name=name, grid=(nt,),
        in_specs=[pl.BlockSpec((tr, 2 * SW), lambda i: (i, zblock)), pl.BlockSpec((tr, SW), lambda i: (i, 0)),
                  whole(lng), whole(lnb), whole(ws), whole(bexp), pl.BlockSpec(memory_space=pl.ANY)],
        out_specs=([pl.BlockSpec((tr, 2 * SW), lambda i: (i, zblock))]
                   + [pl.BlockSpec(s, lambda i, nd=len(s): (0,) * nd) for s in acc_shapes]),
        out_shape=[jax.ShapeDtypeStruct(dproj.shape, BF16)] + [jax.ShapeDtypeStruct(s, F32) for s in acc_shapes],
        scratch_shapes=[pltpu.VMEM((tr, SW), F32), pltpu.VMEM((tr, SW), F32), pltpu.VMEM((SGU_CHUNK, SW), F32)],
        input_output_aliases={6: 0}, compiler_params=_params())(proj, dyb, lng, lnb, ws, bexp, dproj)


_HBM = pl.BlockSpec(memory_space=pltpu.HBM)
_SEM = pl.BlockSpec(memory_space=pltpu.SEMAPHORE)
_DATAFLOW = pltpu.SideEffectType.DATAFLOW_SIDE_EFFECTING


def _mesh_place(chips=False):
    x, y, c = lax.axis_index("x"), lax.axis_index("y"), lax.axis_index("c")
    return x, y, c, (2 * x + y if chips else 4 * x + 2 * y + c)


def _peer(x, y, c, rel, chips=False):
    px = 1 - x if rel & 4 else x
    py = 1 - y if rel & 2 else y
    pc = 1 - c if rel & 1 else c
    return (px, py, pc), (2 * px + py if chips else 4 * px + 2 * py + pc)


ALL_PEERS = tuple(range(1, N_DEV))
SIBLING = (1,)
SAME_CORE = (2, 4, 6)
SIBLINGS_CORE = (3, 5, 7)


def _exchange_start(groups, name, rels=ALL_PEERS, chips=False):
    flat = [t for g in groups for t in g]
    sizes = [len(g) for g in groups]
    n, ng = len(flat), len(groups)
    srcs = [pltpu.with_memory_space_constraint(a, pltpu.HBM) for a, _ in flat]
    lands = [pltpu.with_memory_space_constraint(lax.empty(((N_DEV,) + a.shape) if isg else a.shape, a.dtype), pltpu.HBM)
             for a, isg in flat]

    def body(*refs):
        ins, lnd, sems, token = refs[:n], refs[n:2 * n], refs[2 * n:2 * n + 3 * ng], refs[-1]
        x, y, c, me = _mesh_place(chips)
        j0 = 0
        for gi, sz in enumerate(sizes):
            for rel in rels:
                dev, slot = _peer(x, y, c, rel, chips)
                for jj in range(sz):
                    j = j0 + jj
                    pltpu.make_async_remote_copy(
                        src_ref=ins[j] if flat[j][1] else ins[j].at[slot], dst_ref=lnd[j].at[me],
                        send_sem=sems[3 * gi].at[jj * (N_DEV - 1) + rel - 1], recv_sem=sems[3 * gi + 1].at[jj * (N_DEV - 1) + rel - 1],
                        device_id=dev, device_id_type=pl.DeviceIdType.MESH).start()
            for jj in range(sz):
                j = j0 + jj
                pltpu.make_async_copy(ins[j] if flat[j][1] else ins[j].at[me], lnd[j].at[me], sems[3 * gi + 2].at[jj]).start()
            j0 += sz
        token[...] = jnp.zeros_like(token)

    sem_shapes = [pltpu.SemaphoreType.DMA((k,)) for sz in sizes for k in (sz * (N_DEV - 1), sz * (N_DEV - 1), sz)]
    res = pl.pallas_call(
        body, name=name,
        out_shape=(*sem_shapes, *[pltpu.HBM(a.shape, a.dtype) for a in srcs], *[pltpu.HBM(a.shape, a.dtype) for a in lands],
                   jax.ShapeDtypeStruct((SUBLANE, LANE), F32)),
        in_specs=[_HBM] * (2 * n), out_specs=(*[_SEM] * (3 * ng), *[_HBM] * (2 * n), pl.BlockSpec(memory_space=pltpu.VMEM)),
        input_output_aliases={i: 3 * ng + i for i in range(2 * n)},
        compiler_params=pltpu.CompilerParams(has_side_effects=_DATAFLOW))(*srcs, *lands)
    sems, thru, token = res[:3 * ng], res[3 * ng:3 * ng + 2 * n], res[-1]
    handle, j0 = [], 0
    for gi, sz in enumerate(sizes):
        handle.append(dict(kinds=[k for _, k in groups[gi]], chips=chips, srcs=list(thru[j0:j0 + sz]), lands=list(thru[n + j0:n + j0 + sz]),
                           sems=list(sems[3 * gi:3 * gi + 3])))
        j0 += sz
    return handle, token


def _exchange_wait(group, after, name, rels=ALL_PEERS, local=True):
    kinds, sz = group["kinds"], len(group["kinds"])
    relay = group.get("relay", [])

    def body(*refs):
        ins, lnd, (ssem, rsem, lsem) = refs[:sz], refs[sz:2 * sz], refs[2 * sz:2 * sz + 3]
        x, y, c, me = _mesh_place(group["chips"])
        for rel in rels:
            dev, slot = _peer(x, y, c, rel, group["chips"])
            for jj in range(sz):
                cp = pltpu.make_async_remote_copy(
                    src_ref=ins[jj] if kinds[jj] else ins[jj].at[slot], dst_ref=lnd[jj].at[slot],
                    send_sem=ssem.at[jj * (N_DEV - 1) + rel - 1], recv_sem=rsem.at[jj * (N_DEV - 1) + rel - 1],
                    device_id=dev, device_id_type=pl.DeviceIdType.MESH)
                cp.wait_send()
                cp.wait_recv()
        if local:
            for jj in range(sz):
                pltpu.make_async_copy(ins[jj] if kinds[jj] else ins[jj].at[me], lnd[jj].at[me], lsem.at[jj]).wait()
        if relay:
            fsend, frecv = refs[2 * sz + 3:2 * sz + 5]
            dev = _peer(x, y, c, 1)[0]
            for q, (mine, theirs) in enumerate(zip(SAME_CORE, SIBLINGS_CORE)):
                for jj in range(sz):
                    cp = pltpu.make_async_remote_copy(
                        src_ref=lnd[jj].at[_peer(x, y, c, mine)[1]], dst_ref=lnd[jj].at[_peer(x, y, c, theirs)[1]],
                        send_sem=fsend.at[jj * len(SAME_CORE) + q], recv_sem=frecv.at[jj * len(SAME_CORE) + q],
                        device_id=dev, device_id_type=pl.DeviceIdType.MESH)
                    cp.wait_send()
                    cp.wait_recv()

    arrays = group["srcs"] + group["lands"]
    sems = group["sems"] + relay
    res = pl.pallas_call(
        body, name=name, out_shape=[pltpu.HBM(a.shape, a.dtype) for a in arrays],
        in_specs=[_HBM] * (2 * sz) + [_SEM] * len(sems) + [pl.BlockSpec(memory_space=pl.ANY)], out_specs=[_HBM] * (2 * sz),
        input_output_aliases={i: i for i in range(2 * sz)},
        compiler_params=pltpu.CompilerParams(has_side_effects=_DATAFLOW))(*arrays, *sems, after)
    return dict(group, srcs=list(res[:sz]), lands=list(res[sz:]), relay=[])


def _relay_start(group, name):
    sz = len(group["kinds"])
    nq = len(SAME_CORE)

    def body(*refs):
        lnd, fsend, frecv, token = refs[:sz], refs[sz], refs[sz + 1], refs[-1]
        x, y, c, _ = _mesh_place()
        dev = _peer(x, y, c, 1)[0]
        for q, rel in enumerate(SAME_CORE):
            slot = _peer(x, y, c, rel)[1]
            for jj in range(sz):
                pltpu.make_async_remote_copy(
                    src_ref=lnd[jj].at[slot], dst_ref=lnd[jj].at[slot], send_sem=fsend.at[jj * nq + q], recv_sem=frecv.at[jj * nq + q],
                    device_id=dev, device_id_type=pl.DeviceIdType.MESH).start()
        token[...] = jnp.zeros_like(token)

    lands = group["lands"]
    res = pl.pallas_call(
        body, name=name,
        out_shape=(pltpu.SemaphoreType.DMA((sz * nq,)), pltpu.SemaphoreType.DMA((sz * nq,)), *[pltpu.HBM(a.shape, a.dtype) for a in lands],
                   jax.ShapeDtypeStruct((SUBLANE, LANE), F32)),
        in_specs=[_HBM] * sz, out_specs=(_SEM, _SEM, *[_HBM] * sz, pl.BlockSpec(memory_space=pltpu.VMEM)),
        input_output_aliases={i: 2 + i for i in range(sz)},
        compiler_params=pltpu.CompilerParams(has_side_effects=_DATAFLOW))(*lands)
    return dict(group, lands=list(res[2:2 + sz]), relay=[res[0], res[1]]), res[-1]


def _sibling_swap(arrays, handle, after, name):
    start = handle is None
    n = len(arrays) if start else len(handle["srcs"])
    chips = N_DEV // 2
    if start:
        srcs = [pltpu.with_memory_space_constraint(a.reshape(chips, 2, *a.shape[1:]), pltpu.HBM) for a in arrays]
        lands = [pltpu.with_memory_space_constraint(lax.empty((chips,) + a.shape[1:], a.dtype), pltpu.HBM) for a in arrays]
    else:
        srcs, lands = handle["srcs"], handle["lands"]

    def body(*refs):
        ins, lnd, ssem, rsem = refs[:n], refs[n:2 * n], refs[2 * n], refs[2 * n + 1]
        x, y, c, _ = _mesh_place()
        dev = _peer(x, y, c, 1)[0]
        for q in range(chips):
            for j in range(n):
                cp = pltpu.make_async_remote_copy(
                    src_ref=ins[j].at[q, 1 - c], dst_ref=lnd[j].at[q], send_sem=ssem.at[j * chips + q], recv_sem=rsem.at[j * chips + q],
                    device_id=dev, device_id_type=pl.DeviceIdType.MESH)
                if start:
                    cp.start()
                else:
                    cp.wait_send()
                    cp.wait_recv()
        if start:
            refs[-1][...] = jnp.zeros_like(refs[-1])

    thru = [pltpu.HBM(a.shape, a.dtype) for a in srcs + lands]
    effect = pltpu.CompilerParams(has_side_effects=_DATAFLOW)
    if start:
        res = pl.pallas_call(
            body, name=name, out_shape=(pltpu.SemaphoreType.DMA((n * chips,)), pltpu.SemaphoreType.DMA((n * chips,)), *thru,
                                        jax.ShapeDtypeStruct((SUBLANE, LANE), F32)),
            in_specs=[_HBM] * (2 * n), out_specs=(_SEM, _SEM, *[_HBM] * (2 * n), pl.BlockSpec(memory_space=pltpu.VMEM)),
            input_output_aliases={i: 2 + i for i in range(2 * n)}, compiler_params=effect)(*srcs, *lands)
        return dict(srcs=list(res[2:2 + n]), lands=list(res[2 + n:2 + 2 * n]), sems=[res[0], res[1]]), res[-1]
    res = pl.pallas_call(
        body, name=name, out_shape=thru, in_specs=[_HBM] * (2 * n) + [_SEM, _SEM, pl.BlockSpec(memory_space=pl.ANY)],
        out_specs=[_HBM] * (2 * n), input_output_aliases={i: i for i in range(2 * n)}, compiler_params=effect)(
            *srcs, *lands, *handle["sems"], after)
    return dict(handle, srcs=list(res[:n]), lands=list(res[n:]))


def _pair_add(mine, theirs, core, name):
    chips, _, rows, w = mine.shape
    tm = _pick(rows, (256, 128, 64, 32, 16))

    def body(core_ref, a_ref, b_ref, o_ref):
        o_ref[...] = (a_ref[...].astype(F32) + b_ref[...].astype(F32)).astype(o_ref.dtype)

    return pl.pallas_call(
        body, name=name, out_shape=jax.ShapeDtypeStruct(theirs.shape, theirs.dtype),
        grid_spec=pltpu.PrefetchScalarGridSpec(
            num_scalar_prefetch=1, grid=(chips, rows // tm),
            in_specs=[pl.BlockSpec((None, None, tm, w), lambda q, i, core_ref: (q, core_ref[0], i, 0)),
                      pl.BlockSpec((None, tm, w), lambda q, i, core_ref: (q, i, 0))],
            out_specs=pl.BlockSpec((None, tm, w), lambda q, i, core_ref: (q, i, 0))),
        compiler_params=_params())(core, mine, theirs)


def _adamw(w, m, v, gparts, name, after=None):
    R, C = w.shape
    tm = _pick(R, (256, 128, 64, 32, 16, 8))
    order = [] if after is None else [after]

    def body(w_ref, m_ref, v_ref, g_ref, *rest):
        go, do, mo, vo = rest[len(order):]
        g = g_ref[0].astype(F32)
        for j in range(1, gparts.shape[0]):
            g = g + g_ref[j].astype(F32)
        mn = ADAM_B1 * m_ref[...] + (1.0 - ADAM_B1) * g
        vn = ADAM_B2 * v_ref[...] + (1.0 - ADAM_B2) * (g * g)
        m_hat = mn / (1.0 - ADAM_B1 ** ADAM_STEP)
        v_hat = vn / (1.0 - ADAM_B2 ** ADAM_STEP)
        go[...] = g
        do[...] = -ADAM_LR * (m_hat / (jnp.sqrt(v_hat) + ADAM_EPS) + ADAM_WD * w_ref[...])
        mo[...] = mn
        vo[...] = vn

    row = pl.BlockSpec((tm, C), lambda i: (i, 0))
    return pl.pallas_call(
        body, name=name, grid=(R // tm,),
        in_specs=[row, row, row, pl.BlockSpec((gparts.shape[0], tm, C), lambda i: (0, i, 0))] + [pl.BlockSpec(memory_space=pl.ANY)] * len(order),
        out_specs=[row] * 4, out_shape=[jax.ShapeDtypeStruct((R, C), F32)] * 4, compiler_params=_params())(w, m, v, gparts, *order)


def _pack(arrays):
    parts = []
    for a in arrays:
        f = a.reshape(1, -1)
        pad = _ceil_to(f.shape[1], SUBLANE * LANE) - f.shape[1]
        f = jnp.concatenate([f, jnp.zeros((1, pad), f.dtype)], axis=1) if pad else f
        parts.append(f.reshape(-1, LANE))
    rows = sum(p.shape[0] for p in parts)
    pad = _ceil_to(rows, 64) - rows
    return jnp.concatenate(parts + ([jnp.zeros((pad, LANE), parts[0].dtype)] if pad else []), axis=0)


def _unpack(buf, shapes):
    out, row = [], 0
    for s in shapes:
        size = 1
        for d in s:
            size *= d
        rows = _ceil_to(size, SUBLANE * LANE) // LANE
        out.append(buf[row:row + rows].reshape(1, -1)[:, :size].reshape(s))
        row += rows
    return out


def kernel(x, norm_mix_g, w_in, shift_mu, w0, w_lora_up, a0, a_lora_up, g_lora_up, k_k, k_a, r_k, lnx_g, lnx_b, w_proj_rwkv, sgu_ln_g, sgu_ln_b, sgu_w, sgu_b, w_proj_sgu, w_out, norm_ffn_g, w_ffn_gate, w_ffn_up, w_ffn_down, norm_final_g, loss_target, m_norm_mix_g, m_w_in, m_shift_mu, m_w0, m_w_lora_up, m_a0, m_a_lora_up, m_g_lora_up, m_k_k, m_k_a, m_r_k, m_lnx_g, m_lnx_b, m_w_proj_rwkv, m_sgu_ln_g, m_sgu_ln_b, m_sgu_w, m_sgu_b, m_w_proj_sgu, m_w_out, m_norm_ffn_g, m_w_ffn_gate, m_w_ffn_up, m_w_ffn_down, m_norm_final_g, v_norm_mix_g, v_w_in, v_shift_mu, v_w0, v_w_lora_up, v_a0, v_a_lora_up, v_g_lora_up, v_k_k, v_k_a, v_r_k, v_lnx_g, v_lnx_b, v_w_proj_rwkv, v_sgu_ln_g, v_sgu_ln_b, v_sgu_w, v_sgu_b, v_w_proj_sgu, v_w_out, v_norm_ffn_g, v_w_ffn_gate, v_w_ffn_up, v_w_ffn_down, v_norm_final_g):
    weights = dict(norm_mix_g=norm_mix_g, w_in=w_in, shift_mu=shift_mu, w0=w0, w_lora_up=w_lora_up, a0=a0, a_lora_up=a_lora_up,
                   g_lora_up=g_lora_up, k_k=k_k, k_a=k_a, r_k=r_k, lnx_g=lnx_g, lnx_b=lnx_b, w_proj_rwkv=w_proj_rwkv,
                   sgu_ln_g=sgu_ln_g, sgu_ln_b=sgu_ln_b, sgu_w=sgu_w, sgu_b=sgu_b, w_proj_sgu=w_proj_sgu, w_out=w_out,
                   norm_ffn_g=norm_ffn_g, w_ffn_gate=w_ffn_gate, w_ffn_up=w_ffn_up, w_ffn_down=w_ffn_down, norm_final_g=norm_final_g)
    m_in = dict(norm_mix_g=m_norm_mix_g, w_in=m_w_in, shift_mu=m_shift_mu, w0=m_w0, w_lora_up=m_w_lora_up, a0=m_a0,
                a_lora_up=m_a_lora_up, g_lora_up=m_g_lora_up, k_k=m_k_k, k_a=m_k_a, r_k=m_r_k, lnx_g=m_lnx_g, lnx_b=m_lnx_b,
                w_proj_rwkv=m_w_proj_rwkv, sgu_ln_g=m_sgu_ln_g, sgu_ln_b=m_sgu_ln_b, sgu_w=m_sgu_w, sgu_b=m_sgu_b,
                w_proj_sgu=m_w_proj_sgu, w_out=m_w_out, norm_ffn_g=m_norm_ffn_g, w_ffn_gate=m_w_ffn_gate, w_ffn_up=m_w_ffn_up,
                w_ffn_down=m_w_ffn_down, norm_final_g=m_norm_final_g)
    v_in = dict(norm_mix_g=v_norm_mix_g, w_in=v_w_in, shift_mu=v_shift_mu, w0=v_w0, w_lora_up=v_w_lora_up, a0=v_a0,
                a_lora_up=v_a_lora_up, g_lora_up=v_g_lora_up, k_k=v_k_k, k_a=v_k_a, r_k=v_r_k, lnx_g=v_lnx_g, lnx_b=v_lnx_b,
                w_proj_rwkv=v_w_proj_rwkv, sgu_ln_g=v_sgu_ln_g, sgu_ln_b=v_sgu_ln_b, sgu_w=v_sgu_w, sgu_b=v_sgu_b,
                w_proj_sgu=v_w_proj_sgu, w_out=v_w_out, norm_ffn_g=v_norm_ffn_g, w_ffn_gate=v_w_ffn_gate, w_ffn_up=v_w_ffn_up,
                w_ffn_down=v_w_ffn_down, norm_final_g=v_norm_final_g)
    names = list(weights)
    col_sharded = ("w_in", "w_lora_up", "a_lora_up", "g_lora_up", "w_proj_rwkv", "w_proj_sgu", "w_ffn_gate", "w_ffn_up")
    row_sharded = ("w_out", "w_ffn_down")
    sharded = [n for n in names if n in col_sharded or n in row_sharded]
    small = [n for n in names if n not in sharded]

    xs, tgt = x[0], loss_target[0]
    T, D = xs.shape
    RW = w0.shape[1]
    H = RW // HEAD
    SW = sgu_ln_g.shape[1]
    G = sgu_w.shape[1]
    assert 2 * SW == D, "the projection layout takes the SGU part to be as wide as a gate"
    lay = _rwkv_layout(RW, w_lora_up.shape[1], a_lora_up.shape[1], g_lora_up.shape[1], D)
    _, pw, _, rcp = lay
    icp = rcp + 3 * D
    b_ga, b_gb, b_z = rcp // D, rcp // D + 1, rcp // D + 2

    gather_groups = dict(win=["w_in", "w_lora_up", "a_lora_up", "g_lora_up"], proj=["w_proj_rwkv", "w_proj_sgu", "w_out"],
                         ffn_gate_up=["w_ffn_gate", "w_ffn_up"], ffn_down=["w_ffn_down"])
    handles, gather_token = _exchange_start([[(weights[n][0].astype(BF16), True) for n in grp] for grp in gather_groups.values()],
                                            "gather_start", rels=SIBLING + SAME_CORE)
    gather = dict(zip(gather_groups, handles))
    full = {}
    relay_tokens = {}
    joined = lambda g: g.transpose(1, 0, 2).reshape(g.shape[1], -1)

    def relay_weights(key, after):
        arrived = _exchange_wait(gather[key], after, "gather_wait_ici_" + key, rels=SAME_CORE, local=False)
        gather[key], relay_tokens[key] = _relay_start(arrived, "gather_relay_" + key)

    def take_weights(key, after):
        done = _exchange_wait(gather[key], after, "gather_wait_d2d_" + key, rels=SIBLING)
        for n, g in zip(gather_groups[key], done["lands"]):
            full[n] = g.reshape(-1, g.shape[2]) if n in row_sharded else g

    packed = [_pack([d[n] for n in small] + [gather_token]) for d in (weights, m_in, v_in)]
    n1 = _rms_fwd(xs, norm_mix_g, "rms_mix", deps=[gather_token, *packed])
    relay_weights("win", n1)
    take_weights("win", relay_tokens["win"])
    W_in = _w_in_to_proj(full["w_in"], lay, D, "w_in_layout")
    lora = [_pad_rows(joined(full[n]), rows) for n, rows in zip(("w_lora_up", "a_lora_up", "g_lora_up"), pw[3:])]
    mu_p = _pad_rwkv_cols(shift_mu, lay)
    rsmall = [w0, a0, k_k, k_a]
    hp = [lnx_g.reshape(H, 1, HEAD), lnx_b.reshape(H, 1, HEAD), r_k.reshape(H, 1, HEAD)]
    ws = sgu_w[0]
    bexp = jnp.repeat(sgu_b[0].T, SGU_GROUP, axis=1)
    gf = norm_final_g.reshape(1, D)

    proj = _matmul(n1, W_in, mode="nn", out_dtype=F32, name="proj_in")
    ga, gb = (proj, D, b_ga), (proj, D, b_gb)
    r_h, lw_h, k2_h, v_h, aa_h, bb_h, g_h = _rwkv_pre(proj, mu_p, rsmall, lora, lay, "rwkv_pre")
    wkv_in = [r_h, lw_h, k2_h, v_h, aa_h, bb_h]
    y_h, *wkv_saved = _wkv_fwd(*wkv_in, "wkv_fwd")
    relay_weights("proj", y_h)
    ya = _head_post(y_h, r_h, k2_h, v_h, g_h, hp, "head_post", deps=[relay_tokens["proj"]])
    relay_weights("ffn_gate_up", ya)
    yb = _sgu_fwd(proj, b_z, sgu_ln_g, sgu_ln_b, ws, bexp, "sgu_fwd")
    take_weights("proj", ya)
    pa = _matmul(ya, full["w_proj_rwkv"], mode="nn", out_dtype=F32, name="proj_a", deps=[relay_tokens["ffn_gate_up"]])

    def merge_fn(pb_v, pa_v, ga_v, gb_v):
        return pb_v, _sigmoid(ga_v) * pa_v + _sigmoid(gb_v) * pb_v
    pb, merged = _matmul(yb, full["w_proj_sgu"], mode="nn", name="proj_b_merge",
                         epi=(merge_fn, [pa, (proj, b_ga * D), (proj, b_gb * D)], [F32, BF16]))
    h1 = _matmul(merged, full["w_out"], mode="nn", out_dtype=F32, name="out_proj", add=xs)
    n2 = _rms_fwd(h1, norm_ffn_g, "rms_ffn")
    relay_weights("ffn_down", n2)
    take_weights("ffn_gate_up", n2)

    def act_fn(gt_v, up_v):
        return gt_v, up_v, gt_v * _sigmoid(gt_v) * up_v
    gt, up, act = _matmul(n2, full["w_ffn_gate"], b2=full["w_ffn_up"], mode="nn", name="ffn_gate_up_act", out_blocks=N_DEV,
                          epi=(act_fn, [], [BF16, BF16, BF16]), deps=[relay_tokens["ffn_down"]])
    take_weights("ffn_down", act)
    h2 = _matmul(act, full["w_ffn_down"], mode="nn", out_dtype=F32, name="ffn_down", add=h1)

    def final_fn(rv, pv):
        (h_v, t_v), (g_v,) = rv, pv
        r = lax.rsqrt(_mean(h_v * h_v) + RMS_EPS)
        yn = h_v * r
        e = yn * g_v - t_v
        loss = 0.5 * jnp.sum(_mean(e * e))
        dout = e * (1.0 / D)
        dyg = dout * g_v
        dh = r * (dyg - yn * _mean(dyg * yn))
        return [dh, dh], [jnp.full((1, LANE), loss, F32), _colsum(dout * yn)]
    dh2, dh2_bf, loss_part, d_gf = _rowwise(final_fn, [h2, tgt], [gf], [(D, F32), (D, BF16)], [(1, LANE), (1, D)], name="final_loss")

    grads = {}

    def start_scatter(group, name):
        blocks = [(grads[n].reshape(N_DEV, -1, grads[n].shape[1]) if n in row_sharded else grads[n], False) for n in group]
        (handle,), token = _exchange_start([blocks], name)
        return handle, token

    def dact_fn(d_v, gt_v, up_v):
        gt_v, up_v = gt_v.astype(F32), up_v.astype(F32)
        s = _sigmoid(gt_v)
        return d_v * up_v * (s * (1.0 + gt_v * (1.0 - s))), d_v * gt_v * s
    dgt, dup = _matmul(dh2_bf, full["w_ffn_down"], mode="nt", name="d_ffn_act", out_blocks=N_DEV,
                       epi=(dact_fn, [gt, up], [BF16, BF16]))
    scatter_groups = dict(ffn_down=["w_ffn_down"], ffn_gate=["w_ffn_gate"], ffn_up=["w_ffn_up"],
                          mid=["w_out", "w_proj_rwkv", "w_proj_sgu"], last=["w_in", "w_lora_up", "a_lora_up", "g_lora_up"])
    scatters = {}
    grads["w_ffn_down"] = _matmul(act, dh2_bf, mode="tn", out_dtype=BF16, name="dw_ffn_down")
    scatters["ffn_down"], token = start_scatter(scatter_groups["ffn_down"], "scatter_start_ffn_down")
    dn2 = _matmul(dgt, full["w_ffn_gate"], mode="nt", out_dtype=F32, name="dn2_gate", deps=[token])
    grads["w_ffn_gate"] = _matmul(n2, dgt, mode="tn", out_dtype=BF16, name="dw_ffn_gate", out_blocks=N_DEV)
    scatters["ffn_gate"], token = start_scatter(scatter_groups["ffn_gate"], "scatter_start_ffn_gate")
    grads["w_ffn_up"] = _matmul(n2, dup, mode="tn", out_dtype=BF16, name="dw_ffn_up", out_blocks=N_DEV, deps=[token])
    scatters["ffn_up"], token = start_scatter(scatter_groups["ffn_up"], "scatter_start_ffn_up")
    dn2 = _matmul(dup, full["w_ffn_up"], mode="nt", out_dtype=F32, name="dn2_up", add=dn2, deps=[token])
    dh1, dh1_bf, d_g2 = _rms_bwd(dn2, h1, dh2, norm_ffn_g, "rms_ffn_bwd")
    dmerged = _matmul(dh1_bf, full["w_out"], mode="nt", out_dtype=F32, name="d_merged")
    grads["w_out"] = _matmul(merged, dh1_bf, mode="tn", out_dtype=BF16, name="dw_out")

    def dmerge_fn(rv, pv):
        d_v, ga_v, gb_v, pa_v, pb_v = rv
        sa, sb = _sigmoid(ga_v), _sigmoid(gb_v)
        dgates = jnp.concatenate([d_v * pa_v * sa * (1.0 - sa), d_v * pb_v * sb * (1.0 - sb)], axis=1)
        return [dgates, d_v * sa, d_v * sb], []
    dproj, dpa, dpb = _rowwise(dmerge_fn, [dmerged, ga, gb, pa, pb], [],
                               [(2 * D, BF16, icp, b_ga // 2, None), (D, BF16), (D, BF16)], [], name="d_merge")
    dya = _matmul(dpa, full["w_proj_rwkv"], mode="nt", out_dtype=F32, name="d_ya")
    dyb = _matmul(dpb, full["w_proj_sgu"], mode="nt", out_dtype=F32, name="d_yb")
    grads["w_proj_rwkv"] = _matmul(ya, dpa, mode="tn", out_dtype=BF16, name="dw_proj_a", out_blocks=N_DEV)
    grads["w_proj_sgu"] = _matmul(yb, dpb, mode="tn", out_dtype=BF16, name="dw_proj_b", out_blocks=N_DEV)
    scatters["mid"], token_mid = start_scatter(scatter_groups["mid"], "scatter_start_mid")
    dproj, d_lng, d_lnb, d_ws, d_bs = _sgu_bwd(proj, b_z, dyb, sgu_ln_g, sgu_ln_b, ws, bexp, dproj, "sgu_bwd")

    dr_h, dlw_h, dk2_h, dv_h, daa, dbb, dg_h, d_lnxg, d_lnxb, d_rk = _wkv_bwd(
        *wkv_in, *wkv_saved, y_h, g_h, hp, dya, "wkv_bwd", deps=[token_mid])
    dproj, d_mu, d_w0, d_a0, d_kk, d_ka, d_wlw, d_wla, d_wlg = _rwkv_pre_bwd(
        proj, mu_p, rsmall, lora, [dr_h, dk2_h, dv_h, dlw_h, daa, dbb, dg_h], dproj, lay, "rwkv_pre_bwd")
    split = lambda g: g.reshape(g.shape[0], N_DEV, -1).transpose(1, 0, 2)
    grads["w_in"] = _dw_in_from_proj(_matmul(n1, dproj, mode="tn", out_dtype=BF16, name="dw_in"), lay, D, w_in.shape[2], "dw_in_layout")
    grads["w_lora_up"] = split(d_wlw[:w_lora_up.shape[1]].astype(BF16))
    grads["a_lora_up"] = split(d_wla[:a_lora_up.shape[1]].astype(BF16))
    grads["g_lora_up"] = split(d_wlg[:g_lora_up.shape[1]].astype(BF16))
    out = {}

    def update_group(key, after):
        handle = scatters[key]
        parts = _exchange_wait(handle, after, "scatter_wait_" + key, rels=SAME_CORE if handle["chips"] else ALL_PEERS)["lands"]
        for n, part in zip(scatter_groups[key], parts):
            res = _adamw(weights[n][0], m_in[n][0], v_in[n][0], part, "adamw_" + n, after=after)
            out[n] = [t.reshape(weights[n].shape) for t in res]
            after = res[0]
        return after

    swap, token_swap = _sibling_swap([grads[n] for n in scatter_groups["last"]], None, None, "scatter_last_swap_start")
    after = update_group("ffn_gate", update_group("ffn_down", token_swap))
    swap = _sibling_swap(None, swap, after, "scatter_last_swap_wait")
    core = lax.axis_index("c").astype(jnp.int32).reshape(1)
    chip_sums = [_pair_add(mine, theirs, core, "scatter_last_add_" + n)
                 for n, mine, theirs in zip(scatter_groups["last"], swap["srcs"], swap["lands"])]
    (scatters["last"],), token_in = _exchange_start([[(s, False) for s in chip_sums]], "scatter_start_last", rels=SAME_CORE, chips=True)
    dn1 = _matmul(dproj, W_in, mode="nt", out_dtype=F32, name="dn1", deps=[token_in])
    dx, _, d_g1 = _rms_bwd(dn1, xs, dh1, norm_mix_g, "rms_mix_bwd")
    small_grads = dict(norm_mix_g=d_g1, shift_mu=_unpad_rwkv_cols(d_mu, lay), w0=d_w0, a0=d_a0, k_k=d_kk, k_a=d_ka, r_k=d_rk,
                       lnx_g=d_lnxg, lnx_b=d_lnxb, sgu_ln_g=d_lng, sgu_ln_b=d_lnb, sgu_w=d_ws, sgu_b=d_bs[:, :G].T,
                       norm_ffn_g=d_g2, norm_final_g=d_gf)
    (gather_small,), after = _exchange_start([[(_pack([small_grads[n] for n in small] + [jnp.zeros_like(gather_token)]), True)]],
                                             "gather_small_start")
    for key in ("ffn_up", "mid", "last"):
        after = update_group(key, after)
    small_parts = _exchange_wait(gather_small, after, "gather_small_wait")["lands"][0]
    res = _adamw(*packed, small_parts, "adamw_small")
    unpacked = [_unpack(t, [weights[n].shape for n in small]) for t in res]
    for i, n in enumerate(small):
        out[n] = [u[i] for u in unpacked]

    loss = lax.psum(loss_part[0, 0], ("x", "y", "c"))
    return (loss, dx[None], *[out[n][0] for n in names], *[out[n][1] for n in names],
            *[out[n][2] for n in names], *[out[n][3] for n in names])
```

```python
import jax
import jax.numpy as jnp
from jax import lax
from jax.experimental import pallas as pl
from jax.experimental.pallas import tpu as pltpu

F32 = jnp.float32
BF16 = jnp.bfloat16

N_DEV = 8
LANE = 128
SUBLANE = 8
HEAD = 64
SGU_CHUNK = 128
SGU_GROUP = 128
WKV_CHUNK = 64
RMS_EPS = 1e-6
LN_EPS = 1e-5
LNX_EPS = 64e-5
ADAM_LR, ADAM_B1, ADAM_B2, ADAM_EPS, ADAM_WD, ADAM_STEP = 0.001, 0.9, 0.999, 1e-08, 0.01, 10
VMEM_LIMIT_BYTES = 48 * 1024 * 1024
_SQRT_HALF = 0.7071067811865476
_INV_SQRT_2PI = 0.3989422804014327


def _pick(n, cands):
    for c in cands:
        if n % c == 0:
            return c
    return n


def _ceil_to(n, m):
    return -(-n // m) * m


def _params():
    return pltpu.CompilerParams(vmem_limit_bytes=VMEM_LIMIT_BYTES)


def _tile(n, cap):
    best = 0
    for d in range(LANE, min(n, cap) + 1, LANE):
        if n % d == 0:
            best = d
    return best or n


def _matmul_tiles(M, N, K, a_bytes, b_bytes, o_bytes, has_add, forced):
    tm = forced.get("m") or _tile(M, 1024)
    tn = forced.get("n") or _tile(N, 1024)
    tk = forced.get("k") or _tile(K, 2048)

    def vmem(tm, tn, tk):
        acc = tm * tn * 4 if tk < K else 0
        return 2 * (tm * tk * a_bytes + tk * tn * b_bytes + tm * tn * (o_bytes + (4 if has_add else 0))) + acc

    while vmem(tm, tn, tk) > (VMEM_LIMIT_BYTES * 3) // 4:
        if "k" not in forced and tk > 512 and _tile(K, tk // 2) < tk:
            tk = _tile(K, tk // 2)
        elif "m" not in forced and _tile(M, tm // 2) < tm:
            tm = _tile(M, tm // 2)
        else:
            break
    return tm, tn, tk


def _matmul(a, b, *, mode, out_dtype=F32, name, add=None, deps=(), out_blocks=0, epi=None, b2=None):
    def view(x):
        return (x.shape[1], x.shape[0] * x.shape[2], x.shape[2]) if x.ndim == 3 else (x.shape[0], x.shape[1], 0)

    (ar, ac, aw), (br, bc, bw) = view(a), view(b)
    a_col, b_col = {"nn": ("k", "n"), "nt": ("k", "k"), "tn": ("m", "n")}[mode]
    if mode == "nn":
        M, K, K2, N = ar, ac, br, bc
    elif mode == "nt":
        M, K, N, K2 = ar, ac, br, bc
    else:
        K, M, K2, N = ar, ac, br, bc
    assert K == K2, (a.shape, b.shape, mode)
    forced = {}
    for dim, w in ((a_col, aw), (b_col, bw), ("n", N // out_blocks if out_blocks else 0)):
        if w:
            assert forced.get(dim, w) == w
            forced[dim] = w
    has_add = add is not None
    tile_bytes = (sum(jnp.dtype(d).itemsize for d in epi[2]) + sum((e[0] if isinstance(e, tuple) else e).dtype.itemsize for e in epi[1])
                  if epi is not None else jnp.dtype(out_dtype).itemsize)
    tm, tn, tk = _matmul_tiles(M, N, K, a.dtype.itemsize, b.dtype.itemsize, tile_bytes, has_add, forced)
    kb = 1
    if "k" in forced and mode != "tn":
        lanes_ok = all(w or tk % LANE == 0 for w in (aw, bw if mode == "nt" else 1))
        kb = next(c for c in (4, 2, 1) if (K // tk) % c == 0 and (c == 1 or (lanes_ok and c * tk <= 1536)))
    nk = K // (tk * kb)
    dn = {"nn": (((1,), (0,)), ((), ())), "nt": (((1,), (1,)), ((), ())), "tn": (((0,), (0,)), ((), ()))}[mode]
    pick = {"m": lambda i, j, k: i, "n": lambda i, j, k: j, "k": lambda i, j, k: k}
    size = {"m": tm, "n": tn, "k": tk}

    def spec(blocked, row_dim, col_dim):
        rf, cf = pick[row_dim], pick[col_dim]
        reps = {d: (kb if d == "k" else 1) for d in (row_dim, col_dim)}
        if blocked:
            lead = kb if col_dim == "k" and kb > 1 else None
            return pl.BlockSpec((lead, size[row_dim], size[col_dim]), lambda i, j, k: (cf(i, j, k), rf(i, j, k), 0))
        return pl.BlockSpec((size[row_dim] * reps[row_dim], size[col_dim] * reps[col_dim]), lambda i, j, k: (rf(i, j, k), cf(i, j, k)))

    def k_part(ref, blocked, k_on_rows, j):
        if kb == 1:
            return ref[...]
        if blocked:
            return ref[j]
        return ref[j * tk:(j + 1) * tk, :] if k_on_rows else ref[:, j * tk:(j + 1) * tk]

    a_spec = spec(aw, "k" if mode == "tn" else "m", a_col)
    b_spec = spec(bw, "n" if mode == "nt" else "k", b_col)
    o_spec = spec(out_blocks, "m", "n")
    epi_fn, epi_ins, epi_dtypes = epi if epi is not None else (None, [], [out_dtype])
    epi_ins = [e if isinstance(e, tuple) else (e, None) for e in epi_ins]
    n_epi = len(epi_ins)
    twin = b2 is not None
    assert not twin or (nk == 1 and kb == 1 and epi is not None and b2.shape == b.shape)
    n_in = 2 + twin + has_add + n_epi + len(deps)
    n_out = len(epi_dtypes)

    def body(*refs):
        a_ref, b_ref = refs[0], refs[1]
        add_ref = refs[2 + twin] if has_add else None
        epi_refs = refs[2 + twin + has_add:2 + twin + has_add + n_epi]
        o_refs = refs[n_in:n_in + n_out]
        part = None
        for q in range(kb):
            a_q = k_part(a_ref, aw and a_col == "k", False, q)
            b_q = k_part(b_ref, bw and b_col == "k", mode == "nn", q)
            prod = lax.dot_general(a_q.astype(BF16), b_q.astype(BF16), dn, preferred_element_type=F32)
            part = prod if part is None else part + prod
        second = [lax.dot_general(a_ref[...].astype(BF16), refs[2][...].astype(BF16), dn, preferred_element_type=F32)] if twin else []

        def finish(res):
            outs = epi_fn(res, *second, *[e[...] for e in epi_refs]) if epi_fn is not None else (res,)
            for o_ref, val in zip(o_refs, outs):
                o_ref[...] = val.astype(o_ref.dtype)

        if nk == 1:
            finish(part + add_ref[...] if has_add else part)
            return
        acc_ref = refs[-1]
        kk = pl.program_id(2)

        @pl.when(kk == 0)
        def _():
            acc_ref[...] = part + add_ref[...] if has_add else part

        @pl.when(kk > 0)
        def _():
            acc_ref[...] += part

        @pl.when(kk == nk - 1)
        def _():
            finish(acc_ref[...])

    def epi_spec(arr, off):
        if off is None:
            return o_spec
        assert off % tn == 0
        return pl.BlockSpec((tm, tn), lambda i, j, k: (i, j + off // tn))

    ins = [a, b] + ([b2] if twin else []) + ([add] if has_add else []) + [arr for arr, _ in epi_ins] + list(deps)
    in_specs = ([a_spec, b_spec] + ([b_spec] if twin else []) + ([o_spec] if has_add else []) + [epi_spec(arr, off) for arr, off in epi_ins]
                + [pl.BlockSpec(d.shape, lambda i, j, k, nd=d.ndim: (0,) * nd) for d in deps])
    o_shape = (out_blocks, M, tn) if out_blocks else (M, N)
    res = pl.pallas_call(
        body, name=name, grid=(M // tm, N // tn, nk), in_specs=in_specs, out_specs=[o_spec] * n_out,
        out_shape=[jax.ShapeDtypeStruct(o_shape, dt) for dt in epi_dtypes],
        scratch_shapes=[pltpu.VMEM((tm, tn), F32)] if nk > 1 else [],
        compiler_params=_params())(*ins)
    return res[0] if epi is None else list(res)


def _rowwise(fn, rows, pars, row_outs, acc_outs, *, name, tm=256, deps=()):
    rows = [r if isinstance(r, tuple) else (r, r.shape[1], 0) for r in rows]
    row_outs = [o if len(o) == 5 else (o[0], o[1], o[0], 0, None) for o in row_outs]
    aliased = [(k, o[4]) for k, o in enumerate(row_outs) if o[4] is not None]
    R = rows[0][0].shape[0]
    if max(w for _, w, _ in rows) > 4096:
        tm = tm // 2
    tm = min(tm, R)
    assert R % tm == 0
    nr, npar = len(rows), len(pars)
    nro = len(row_outs)
    n_in = nr + npar + len(deps) + len(aliased)

    def body(*refs):
        rv = [r[...] for r in refs[:nr]]
        pv = [p[...] for p in refs[nr:nr + npar]]
        outs = refs[n_in:]
        ro, ao = fn(rv, pv)
        first = pl.program_id(0) == 0
        for o_ref, val in zip(outs[:nro], ro):
            o_ref[...] = val.astype(o_ref.dtype)

        @pl.when(first)
        def _():
            for o_ref, val in zip(outs[nro:], ao):
                o_ref[...] = val

        @pl.when(jnp.logical_not(first))
        def _():
            for o_ref, val in zip(outs[nro:], ao):
                o_ref[...] += val

    in_specs = ([pl.BlockSpec((tm, w), lambda i, cb=cb: (i, cb)) for _, w, cb in rows]
                + [pl.BlockSpec(p.shape, lambda i, nd=p.ndim: (0,) * nd) for p in list(pars) + list(deps)]
                + [pl.BlockSpec(memory_space=pl.ANY)] * len(aliased))
    out_shape = ([jax.ShapeDtypeStruct((R, full), dt) for _, dt, full, _, _ in row_outs]
                 + [jax.ShapeDtypeStruct(s, F32) for s in acc_outs])
    out_specs = ([pl.BlockSpec((tm, f), lambda i, cb=cb: (i, cb)) for f, _, _, cb, _ in row_outs]
                 + [pl.BlockSpec(s, lambda i, nd=len(s): (0,) * nd) for s in acc_outs])
    res = pl.pallas_call(body, name=name, grid=(R // tm,), in_specs=in_specs, out_specs=out_specs, out_shape=out_shape,
                         input_output_aliases={n_in - len(aliased) + q: k for q, (k, _) in enumerate(aliased)},
                         compiler_params=_params())(*[r for r, _, _ in rows], *pars, *deps, *[buf for _, buf in aliased])
    return list(res)


def _bdot(a, b, mode="nn"):
    dn = {"nn": (((1,), (0,)), ((), ())), "nt": (((1,), (1,)), ((), ())), "tn": (((0,), (0,)), ((), ()))}[mode]
    return lax.dot_general(a.astype(BF16), b.astype(BF16), dn, preferred_element_type=F32)


def _sigmoid(x):
    return jax.nn.sigmoid(x)


def _softplus(x):
    return jnp.maximum(x, 0.0) + jnp.log1p(jnp.exp(-jnp.abs(x)))


def _gelu(z):
    return 0.5 * z * (1.0 + lax.erf(z * _SQRT_HALF))


def _gelu_grad(z):
    return 0.5 * (1.0 + lax.erf(z * _SQRT_HALF)) + z * jnp.exp(-0.5 * z * z) * _INV_SQRT_2PI


def _mean(x):
    return jnp.mean(x, axis=-1, keepdims=True)


def _colsum(x):
    return jnp.sum(x, axis=0, keepdims=True)


def _rms_fwd(x, g, name, deps=()):
    def fn(rv, pv):
        (xv,), (gv,) = rv, pv
        r = lax.rsqrt(_mean(xv * xv) + RMS_EPS)
        return [xv * r * gv], []
    return _rowwise(fn, [x], [g], [(x.shape[1], BF16)], [], name=name, deps=deps)[0]


def _rms_bwd(dn, x, dres, g, name, deps=()):
    def fn(rv, pv):
        (dnv, xv, drv), (gv,) = rv, pv
        r = lax.rsqrt(_mean(xv * xv) + RMS_EPS)
        yn = xv * r
        dyg = dnv * gv
        dx = drv + r * (dyg - yn * _mean(dyg * yn))
        return [dx, dx], [_colsum(dnv * yn)]
    D = x.shape[1]
    return _rowwise(fn, [dn, x, dres], [g], [(D, F32), (D, BF16)], [(1, D)], name=name, deps=deps)


def _rwkv_layout(RW, Lw, La, Lg, D):
    widths = [RW, RW, RW, Lw, La, Lg]
    pw = [_ceil_to(w, LANE) for w in widths]
    pw[5] += _ceil_to(sum(pw), 2 * D) - sum(pw)
    offs = [sum(pw[:i]) for i in range(6)]
    return widths, pw, offs, sum(pw)


def _pad_rwkv_cols(a, lay):
    widths, pw, _, _ = lay
    pieces, src = [], 0
    for w, p in zip(widths, pw):
        pieces.append(a[:, src:src + w])
        if p > w:
            pieces.append(jnp.zeros((a.shape[0], p - w), a.dtype))
        src += w
    return jnp.concatenate(pieces, axis=1)


def _unpad_rwkv_cols(a, lay):
    widths, _, offs, _ = lay
    return jnp.concatenate([a[:, o:o + w] for o, w in zip(offs, widths)], axis=1)


def _proj_pieces(lay, D, cs):
    widths, _, offs, rcp = lay
    rc = sum(widths)
    segs = [(sum(widths[:j]), widths[j], offs[j]) for j in range(6)] + [(rc, D, rcp + 2 * D), (rc + D, D, rcp), (rc + 2 * D, D, rcp + D)]
    pieces = []
    for start, width, dst in segs:
        n = start
        while n < start + width:
            d, off = divmod(n, cs)
            take = min(cs - off, start + width - n)
            pieces.append((d, off, dst + n - start, take))
            n += take
    return pieces


def _w_in_to_proj(g, lay, D, name):
    nb, rows, cs = g.shape
    icp = lay[3] + 3 * D
    pieces = _proj_pieces(lay, D, cs)
    tm = _pick(rows, (256, 128, 64, 32, 16))

    def body(i_ref, o_ref):
        o_ref[...] = jnp.zeros_like(o_ref)
        for d, src, dst, w in pieces:
            o_ref[:, dst:dst + w] = i_ref[d, :, src:src + w]

    return pl.pallas_call(
        body, name=name, grid=(rows // tm,), in_specs=[pl.BlockSpec((nb, tm, cs), lambda i: (0, i, 0))],
        out_specs=pl.BlockSpec((tm, icp), lambda i: (i, 0)), out_shape=jax.ShapeDtypeStruct((rows, icp), g.dtype),
        compiler_params=_params())(g)


def _dw_in_from_proj(a, lay, D, cs, name):
    rows, icp = a.shape
    pieces = _proj_pieces(lay, D, cs)
    tm = _pick(rows, (256, 128, 64, 32, 16))

    def body(i_ref, o_ref):
        for d, src, dst, w in pieces:
            o_ref[d, :, src:src + w] = i_ref[:, dst:dst + w]

    return pl.pallas_call(
        body, name=name, grid=(rows // tm,), in_specs=[pl.BlockSpec((tm, icp), lambda i: (i, 0))],
        out_specs=pl.BlockSpec((N_DEV, tm, cs), lambda i: (0, i, 0)), out_shape=jax.ShapeDtypeStruct((N_DEV, rows, cs), a.dtype),
        compiler_params=_params())(a)


def _pad_rows(a, rows):
    return a if a.shape[0] == rows else jnp.concatenate([a, jnp.zeros((rows - a.shape[0], a.shape[1]), a.dtype)], axis=0)


def _token_shift(p, halo, mu, i):
    tm = p.shape[0]
    hid = lax.broadcasted_iota(jnp.int32, (SUBLANE, 1), 0)
    before = jnp.sum(jnp.where(hid == SUBLANE - 1, halo, 0.0), axis=0, keepdims=True)
    before = jnp.where(i == 0, 0.0, before)
    rid = lax.broadcasted_iota(jnp.int32, (tm, 1), 0)
    prev = jnp.where(rid == 0, before, pltpu.roll(p, 1, 0))
    d = prev - p
    return p + d * mu, d


def _rwkv_math(ps, w0, a0, k_k, k_a, wlw, wla, wlg, lay):
    _, pw, offs, _ = lay
    r, k, v, xw, xa, xg = (ps[:, offs[j]:offs[j] + pw[j]] for j in range(6))
    tw = jnp.tanh(xw)
    ww = w0 + _bdot(tw, wlw)
    lw = -jnp.exp(-_softplus(-ww) - 0.5)
    a = _sigmoid(a0 + _bdot(xa, wla))
    sg = _sigmoid(xg)
    g = _bdot(sg, wlg)
    return dict(r=r, k=k, v=v, xa=xa, tw=tw, ww=ww, lw=lw, a=a, sg=sg, g=g, kkp=k * k_k, k2=k * (1.0 + (a - 1.0) * k_a))


def _halo_spec(tm, width):
    hb = tm // SUBLANE
    return pl.BlockSpec((SUBLANE, width), lambda i: (jnp.maximum(i * hb - 1, 0), 0))


def _rowsum(x):
    return jnp.sum(x, axis=-1, keepdims=True)


def _kk_math(kkp):
    nrm = jnp.sqrt(_rowsum(kkp * kkp))
    inv = 1.0 / jnp.maximum(nrm, 1e-12)
    return nrm, inv, kkp * inv


def _rwkv_pre(p, mu, small, lora, lay, name):
    T, rcp = p.shape[0], lay[3]
    H = lay[0][0] // HEAD
    tm = min(256, T)

    def body(p_ref, ph_ref, mu_ref, w0_ref, a0_ref, kk_ref, ka_ref, wlw_ref, wla_ref, wlg_ref, r_o, lw_o, k2_o, v_o, aa_o, bb_o, g_o):
        ps, _ = _token_shift(p_ref[...], ph_ref[...], mu_ref[...], pl.program_id(0))
        q = _rwkv_math(ps, w0_ref[...], a0_ref[...], kk_ref[...], ka_ref[...], wlw_ref[...], wla_ref[...], wlg_ref[...], lay)
        for h in range(H):
            sl = slice(h * HEAD, (h + 1) * HEAD)
            for o_ref, key in ((r_o, "r"), (lw_o, "lw"), (k2_o, "k2"), (v_o, "v"), (g_o, "g")):
                o_ref[h] = q[key][:, sl]
            _, _, kk = _kk_math(q["kkp"][:, sl])
            aa_o[h] = -kk
            bb_o[h] = kk * q["a"][:, sl]

    whole = lambda arr: pl.BlockSpec(arr.shape, lambda i: (0, 0))
    return pl.pallas_call(
        body, name=name, grid=(T // tm,),
        in_specs=([pl.BlockSpec((tm, rcp), lambda i: (i, 0)), _halo_spec(tm, rcp), whole(mu)]
                  + [whole(s) for s in small] + [whole(w) for w in lora]),
        out_specs=[pl.BlockSpec((H, tm, HEAD), lambda i: (0, i, 0))] * 7, out_shape=[jax.ShapeDtypeStruct((H, T, HEAD), F32)] * 7,
        compiler_params=_params())(p, p, mu, *small, *lora)


def _rwkv_pre_bwd(p, mu, small, lora, hgrads, dproj, lay, name):
    T, rcp = p.shape[0], lay[3]
    widths, pw, offs, _ = lay
    RW = widths[0]
    H = RW // HEAD
    tm = min(128, T)
    nt = T // tm
    hb = tm // SUBLANE

    def body(p_ref, ph_ref, mu_ref, w0_ref, a0_ref, kk_ref, ka_ref, wlw_ref, wla_ref, wlg_ref,
             dr_h, dk2_h, dv_h, dlw_h, daa, dbb, dg_h, buf_ref,
             dp_ref, dmu_ref, dw0_ref, da0_ref, dkk_ref, dka_ref, dwlw_ref, dwla_ref, dwlg_ref,
             s_dr, s_dk2, s_dv, s_dlw, s_dkkp, s_da, s_dg, dps_ref, next_ref):
        i = pl.program_id(0)
        ps, dprev = _token_shift(p_ref[...], ph_ref[...], mu_ref[...], nt - 1 - i)
        k_k, k_a = kk_ref[...], ka_ref[...]
        q = _rwkv_math(ps, w0_ref[...], a0_ref[...], k_k, k_a, wlw_ref[...], wla_ref[...], wlg_ref[...], lay)
        k, a, lw, ww, tw, sg = q["k"], q["a"], q["lw"], q["ww"], q["tw"], q["sg"]
        for h in range(H):
            sl = slice(h * HEAD, (h + 1) * HEAD)
            s_dr[:, sl] = dr_h[h]
            s_dk2[:, sl] = dk2_h[h]
            s_dv[:, sl] = dv_h[h]
            s_dlw[:, sl] = dlw_h[h]
            s_dg[:, sl] = dg_h[h]
            nrm, inv, kk = _kk_math(q["kkp"][:, sl])
            dbb_h = dbb[h]
            dkk = dbb_h * a[:, sl] - daa[h]
            s_dkkp[:, sl] = jnp.where(nrm > 1e-12, inv * (dkk - kk * _rowsum(dkk * kk)), dkk * inv)
            s_da[:, sl] = dbb_h * kk
        dk2, dkkp, dg = s_dk2[...], s_dkkp[...], s_dg[...]
        dk = dk2 * (1.0 + (a - 1.0) * k_a) + dkkp * k_k
        da = s_da[...] + dk2 * k * k_a
        dpa = da * a * (1.0 - a)
        dww = s_dlw[...] * lw * _sigmoid(-ww)
        dxa = _bdot(dpa, wla_ref[...], "nt")
        dxw = _bdot(dww, wlw_ref[...], "nt") * (1.0 - tw * tw)
        dxg = _bdot(dg, wlg_ref[...], "nt") * sg * (1.0 - sg)
        segs = (s_dr[...], dk, s_dv[...], dxw, dxa, dxg)
        sums = [dmu_ref, dw0_ref, da0_ref, dkk_ref, dka_ref, dwlw_ref, dwla_ref, dwlg_ref]

        @pl.when(i == 0)
        def _():
            for s in sums + [next_ref]:
                s[...] = jnp.zeros_like(s)

        for j, seg in enumerate(segs):
            sl = slice(offs[j], offs[j] + pw[j])
            dps_ref[:, sl] = seg
            dmu_ref[:, sl] += _colsum(seg * dprev[:, sl])
        dw0_ref[...] += _colsum(dww)
        da0_ref[...] += _colsum(dpa)
        dkk_ref[...] += _colsum(dkkp * k)
        dka_ref[...] += _colsum(dk2 * k * (a - 1.0))
        dwlw_ref[...] += _bdot(tw, dww, "tn")
        dwla_ref[...] += _bdot(q["xa"], dpa, "tn")
        dwlg_ref[...] += _bdot(sg, dg, "tn")
        dps = dps_ref[...]
        rid = lax.broadcasted_iota(jnp.int32, (tm, 1), 0)
        nxt = jnp.where(rid == tm - 1, next_ref[...], pltpu.roll(dps, tm - 1, 0))
        mu_v = mu_ref[...]
        dp_ref[...] = (dps * (1.0 - mu_v) + nxt * mu_v).astype(BF16)
        next_ref[...] = _colsum(jnp.where(rid == 0, dps, 0.0))

    whole = lambda arr: pl.BlockSpec(arr.shape, lambda i: (0, 0))
    row = lambda w: pl.BlockSpec((tm, w), lambda i: (nt - 1 - i, 0))
    acc_shapes = [(1, rcp), (1, RW), (1, RW), (1, RW), (1, RW)] + [w.shape for w in lora]
    return pl.pallas_call(
        body, name=name, grid=(nt,),
        in_specs=([row(rcp), pl.BlockSpec((SUBLANE, rcp), lambda i: (jnp.maximum((nt - 1 - i) * hb - 1, 0), 0)), whole(mu)]
                  + [whole(s) for s in small] + [whole(w) for w in lora]
                  + [pl.BlockSpec((H, tm, HEAD), lambda i: (0, nt - 1 - i, 0))] * 7 + [pl.BlockSpec(memory_space=pl.ANY)]),
        out_specs=[row(rcp)] + [pl.BlockSpec(s, lambda i: (0, 0)) for s in acc_shapes],
        out_shape=[jax.ShapeDtypeStruct(dproj.shape, BF16)] + [jax.ShapeDtypeStruct(s, F32) for s in acc_shapes],
        scratch_shapes=[pltpu.VMEM((tm, RW), F32)] * 7 + [pltpu.VMEM((tm, rcp), F32), pltpu.VMEM((1, rcp), F32)],
        input_output_aliases={10 + 7: 0}, compiler_params=_params())(p, p, mu, *small, *lora, *hgrads, dproj)


def _head_post_math(y, r, k2, v, lg, lb, rk):
    yc = y - _mean(y)
    rstd = lax.rsqrt(_mean(yc * yc) + LNX_EPS)
    yn = yc * rstd
    s = _rowsum(r * k2 * rk)
    return yn, rstd, yn * lg + lb + s * v, s


def _head_post(y, r, k2, v, g, hp, name, deps=()):
    H, T, _ = y.shape
    tm = min(256, T)

    def body(y_ref, r_ref, k_ref, v_ref, g_ref, lg_ref, lb_ref, rk_ref, *rest):
        o_ref = rest[-1]
        _, _, t, _ = _head_post_math(y_ref[...], r_ref[...], k_ref[...], v_ref[...], lg_ref[...], lb_ref[...], rk_ref[...])
        out = (t * g_ref[...]).astype(BF16)
        for h in range(H):
            o_ref[:, h * HEAD:(h + 1) * HEAD] = out[h]

    blk = pl.BlockSpec((H, tm, HEAD), lambda i: (0, i, 0))
    par = pl.BlockSpec((H, 1, HEAD), lambda i: (0, 0, 0))
    return pl.pallas_call(
        body, name=name, grid=(T // tm,),
        in_specs=[blk] * 5 + [par] * 3 + [pl.BlockSpec(d.shape, lambda i, nd=d.ndim: (0,) * nd) for d in deps],
        out_specs=pl.BlockSpec((tm, H * HEAD), lambda i: (i, 0)),
        out_shape=jax.ShapeDtypeStruct((T, H * HEAD), BF16), compiler_params=_params())(y, r, k2, v, g, *hp, *deps)


def _bmm(x, y, mode):
    dn = {"nn": (((2,), (1,)), ((0,), (0,))), "nt": (((2,), (2,)), ((0,), (0,))), "tn": (((1,), (1,)), ((0,), (0,)))}[mode]
    (xh, xl), (yh, yl) = _split(x), _split(y)
    dot = lambda p, q: lax.dot_general(p, q, dn, preferred_element_type=F32)
    out = dot(xh, yh)
    if yl is not None:
        out = out + dot(xh, yl)
    if xl is not None:
        out = out + dot(xl, yh)
    return out


def _split(x):
    if isinstance(x, tuple):
        return x
    hi = x.astype(BF16)
    return hi, (x - hi.astype(F32)).astype(BF16)


def _exact(x):
    return x.astype(BF16), None


def _round(x):
    return x if isinstance(x, tuple) else (x.astype(BF16), None)


def _rows(*xs):
    if isinstance(xs[0], tuple):
        return tuple(None if any(p is None for p in parts) else jnp.concatenate(parts, axis=1) for parts in zip(*xs))
    return jnp.concatenate(xs, axis=1)


def _wkv_chunk(r, lw, k, v, a, b, inverse=None):
    hb, C, _ = r.shape
    ti = lax.broadcasted_iota(jnp.int32, (C, C), 0)
    si = lax.broadcasted_iota(jnp.int32, (C, C), 1)
    linc, lstr, eye = (ti >= si).astype(F32), (ti > si).astype(F32), (ti == si).astype(F32)
    qmask = jnp.concatenate([jnp.concatenate([lstr, lstr], axis=1), jnp.concatenate([linc, linc], axis=1)], axis=0)
    lincb = _exact(jnp.broadcast_to(linc, (hb, C, C)))
    both = _exact(jnp.broadcast_to(jnp.concatenate([linc, lstr], axis=0), (hb, 2 * C, C)))
    ones = _exact(jnp.ones_like(v))
    lws = _split(lw)
    ci = _bmm(lincb, lws, "nn")
    cC = jnp.sum(lw, axis=1, keepdims=True)
    gi, ge, gn, gr = jnp.exp(ci), jnp.exp(ci - lw), jnp.exp(-ci), jnp.exp(cC - ci)
    q = dict(At=a * ge, Rt=r * gi, Bt=b * gn, Kt=k * gn, Bh=b * gr, Kh=k * gr)
    s = dict(AR=_round(_rows(q["At"], q["Rt"])), BK=_round(_rows(q["Bt"], q["Kt"])), BKh=_round(_rows(q["Bh"], q["Kh"])), v=_round(v))
    quad = _bmm(s["AR"], s["BK"], "nt") * qmask
    s["top"], s["bot"] = _round(quad[:, :C]), _round(quad[:, C:])
    if inverse is None:
        A_ab = quad[:, :C, :C]
        Tm = eye + A_ab
        Pw = _round(A_ab)
        n = 1
        while 2 * n < C:
            Pw = _round(_bmm(Pw, Pw, "nn"))
            Tm = Tm + _bmm(_round(Tm), Pw, "nn")
            n *= 2
        inverse = Tm
    s["Tm"] = _round(inverse)
    gC = jnp.exp(_bmm(lws, ones, "tn"))
    q.update(gi=gi, ge=ge, gn=gn, gr=gr, qmask=qmask, both=both, gC=gC, ones=ones, s=s)
    return q


def _wkv_u(s, H0s, C):
    arh = _bmm(s["AR"], H0s, "nn")
    zv = _rows(tuple(None if p is None else jnp.zeros_like(p) for p in s["v"]), s["v"])
    U = _bmm(s["Tm"], _round(arh[:, :C] + _bmm(s["top"], zv, "nn")), "nn")
    return arh, _rows(_round(U), s["v"])


def _wkv_fwd(r, lw, k, v, a, b, name):
    H, T, N = r.shape
    C = min(WKV_CHUNK, T)
    nc = T // C
    hb = _pick(H, (16, 8, 4, 2))

    def body(r_ref, lw_ref, k_ref, v_ref, a_ref, b_ref, y_ref, st_ref, inv_ref, u_ref, h_ref):
        @pl.when(pl.program_id(1) == 0)
        def _():
            h_ref[...] = jnp.zeros_like(h_ref)

        H0 = h_ref[...]
        st_ref[0] = H0
        q = _wkv_chunk(r_ref[...], lw_ref[...], k_ref[...], v_ref[...], a_ref[...], b_ref[...])
        s = q["s"]
        arh, UV = _wkv_u(s, _round(H0), C)
        inv_ref[0] = s["Tm"][0]
        u_ref[...] = UV[0][:, :C]
        y_ref[...] = arh[:, C:] + _bmm(s["bot"], UV, "nn")
        h_ref[...] = q["gC"] * H0 + _bmm(s["BKh"], UV, "tn")

    blk = pl.BlockSpec((hb, C, N), lambda h, c: (h, c, 0))
    per_chunk = lambda w: pl.BlockSpec((1, hb, w, w), lambda h, c: (c, h, 0, 0))
    return pl.pallas_call(
        body, name=name, grid=(H // hb, nc), in_specs=[blk] * 6, out_specs=[blk, per_chunk(N), per_chunk(C), blk],
        out_shape=[jax.ShapeDtypeStruct((H, T, N), F32), jax.ShapeDtypeStruct((nc, H, N, N), F32),
                   jax.ShapeDtypeStruct((nc, H, C, C), BF16), jax.ShapeDtypeStruct((H, T, N), BF16)],
        scratch_shapes=[pltpu.VMEM((hb, N, N), F32)], compiler_params=_params())(r, lw, k, v, a, b)


def _wkv_bwd(r, lw, k, v, a, b, states, inverses, u, y, g, hp, dya, name, deps=()):
    H, T, N = r.shape
    C = min(WKV_CHUNK, T)
    nc = T // C
    hb = _pick(H, (16, 8, 4, 2))
    hsum = lambda t: jnp.sum(t, axis=1, keepdims=True)

    def body(r_ref, lw_ref, k_ref, v_ref, a_ref, b_ref, st_ref, inv_ref, u_ref, y_ref, g_ref, lg_ref, lb_ref, rk_ref, dya_ref, *rest):
        (dr_ref, dlw_ref, dk_ref, dv_ref, da_ref, db_ref, dg_ref, dlg_ref, dlb_ref, drk_ref, dh_ref, d_s) = rest[len(deps):]
        first = pl.program_id(1) == 0

        @pl.when(first)
        def _():
            dh_ref[...] = jnp.zeros_like(dh_ref)

        for h in range(hb):
            d_s[h] = dya_ref[:, h * N:(h + 1) * N]
        d_v, r_v, k_v, v_v, lg, rk = d_s[...], r_ref[...], k_ref[...], v_ref[...], lg_ref[...], rk_ref[...]
        yn, rstd, t, bonus = _head_post_math(y_ref[...], r_v, k_v, v_v, lg, lb_ref[...], rk)
        dyo = d_v * g_ref[...]
        dyn = dyo * lg
        ds = _rowsum(dyo * v_v)
        dy = rstd * (dyn - _mean(dyn) - yn * _mean(dyn * yn))
        dg_ref[...] = d_v * t
        sums = (hsum(dyo * yn), hsum(dyo), hsum(ds * r_v * k_v))

        @pl.when(first)
        def _():
            for o_ref, val in zip((dlg_ref, dlb_ref, drk_ref), sums):
                o_ref[...] = val

        @pl.when(jnp.logical_not(first))
        def _():
            for o_ref, val in zip((dlg_ref, dlb_ref, drk_ref), sums):
                o_ref[...] += val

        dHC = dh_ref[...]
        H0 = st_ref[0]
        q = _wkv_chunk(r_v, lw_ref[...], k_v, v_v, a_ref[...], b_ref[...], inverse=inv_ref[0])
        s, gC = q["s"], q["gC"]
        H0s, dHs, dY = _round(H0), _round(dHC), _round(dy)
        UV = _rows(_round(u_ref[...]), s["v"])
        bot_dy = _bmm(s["bot"], dY, "tn")
        bkh_dh = _bmm(s["BKh"], dHs, "nn")
        dP = _round(_bmm(s["Tm"], _round(bot_dy[:, :C] + bkh_dh[:, :C]), "tn"))
        dv_ref[...] = bot_dy[:, C:] + bkh_dh[:, C:] + _bmm(s["top"], dP, "tn")[:, C:] + dyo * bonus
        dPY = _rows(dP, dY)
        dh_ref[...] = gC * dHC + _bmm(s["AR"], dPY, "tn")
        dquad = _round(_bmm(dPY, UV, "nt") * q["qmask"])
        dAR = _bmm(dPY, H0s, "nt") + _bmm(dquad, s["BK"], "nn")
        dBK = _bmm(dquad, s["AR"], "tn")
        dBKh = _bmm(UV, dHs, "nt")
        dAt, dRt, dBt, dKt, dBh, dKh = dAR[:, :C], dAR[:, C:], dBK[:, :C], dBK[:, C:], dBKh[:, :C], dBKh[:, C:]
        dr_ref[...] = dRt * q["gi"] + ds * k_v * rk
        da_ref[...] = dAt * q["ge"]
        db_ref[...] = dBt * q["gn"] + dBh * q["gr"]
        dk_ref[...] = dKt * q["gn"] + dKh * q["gr"] + ds * r_v * rk
        tail = dBh * q["Bh"] + dKh * q["Kh"]
        dci = dRt * q["Rt"] - dBt * q["Bt"] - dKt * q["Kt"] - tail
        dcC = jnp.sum(tail, axis=1, keepdims=True) + _bmm(q["ones"], H0 * dHC * gC, "nt")
        dlw_ref[...] = _bmm(q["both"], _rows(dci, dAt * q["At"]), "tn") + dcC

    blk = pl.BlockSpec((hb, C, N), lambda h, c: (h, nc - 1 - c, 0))
    per_chunk = lambda w: pl.BlockSpec((1, hb, w, w), lambda h, c: (nc - 1 - c, h, 0, 0))
    par = pl.BlockSpec((hb, 1, N), lambda h, c: (h, 0, 0))
    return pl.pallas_call(
        body, name=name, grid=(H // hb, nc),
        in_specs=([blk] * 6 + [per_chunk(N), per_chunk(C), blk, blk, blk] + [par] * 3
                  + [pl.BlockSpec((C, hb * N), lambda h, c: (nc - 1 - c, h))]
                  + [pl.BlockSpec(d.shape, lambda h, c, nd=d.ndim: (0,) * nd) for d in deps]),
        out_specs=[blk] * 7 + [par] * 3,
        out_shape=[jax.ShapeDtypeStruct((H, T, N), F32)] * 7 + [jax.ShapeDtypeStruct((H, 1, N), F32)] * 3,
        scratch_shapes=[pltpu.VMEM((hb, N, N), F32), pltpu.VMEM((hb, C, N), F32)],
        compiler_params=_params())(r, lw, k, v, a, b, states, inverses, u, y, g, *hp, dya, *deps)


def _sgu_ln(z, SW, lng, lnb):
    ge = _gelu(z)
    u, vv = ge[:, :SW], ge[:, SW:]
    xc = vv - _mean(vv)
    rstd = lax.rsqrt(_mean(xc * xc) + LN_EPS)
    vn = xc * rstd
    return u, vn, rstd, vn * lng + lnb


def _causal(ws_ref, g):
    ti = lax.broadcasted_iota(jnp.int32, (SGU_CHUNK, SGU_CHUNK), 0)
    si = lax.broadcasted_iota(jnp.int32, (SGU_CHUNK, SGU_CHUNK), 1)
    return ti >= si, jnp.where(ti >= si, ws_ref[g], 0.0).astype(BF16)


def _sgu_fwd(proj, zblock, lng, lnb, ws, bexp, name):
    T, SW = proj.shape[0], lng.shape[1]
    G = ws.shape[0]
    tr = min(256, T)
    nch = tr // SGU_CHUNK

    def body(z_ref, lng_ref, lnb_ref, ws_ref, be_ref, o_ref):
        u, _, _, vl = _sgu_ln(z_ref[...], SW, lng_ref[...], lnb_ref[...])
        for g in range(G):
            cs = slice(g * SGU_GROUP, (g + 1) * SGU_GROUP)
            _, wc = _causal(ws_ref, g)
            for n in range(nch):
                rs = slice(n * SGU_CHUNK, (n + 1) * SGU_CHUNK)
                m = jnp.dot(wc, vl[rs, cs].astype(BF16), preferred_element_type=F32) + be_ref[:, cs]
                o_ref[rs, cs] = (u[rs, cs] * m).astype(BF16)

    whole = lambda arr: pl.BlockSpec(arr.shape, lambda i, nd=arr.ndim: (0,) * nd)
    return pl.pallas_call(
        body, name=name, grid=(T // tr,),
        in_specs=[pl.BlockSpec((tr, 2 * SW), lambda i: (i, zblock)), whole(lng), whole(lnb), whole(ws), whole(bexp)],
        out_specs=pl.BlockSpec((tr, SW), lambda i: (i, 0)), out_shape=jax.ShapeDtypeStruct((T, SW), BF16),
        compiler_params=_params())(proj, lng, lnb, ws, bexp)


def _sgu_bwd(proj, zblock, dyb, lng, lnb, ws, bexp, dproj, name):
    T, SW = proj.shape[0], lng.shape[1]
    G = ws.shape[0]
    tr = min(256, T)
    nch = tr // SGU_CHUNK
    nt = T // tr

    def body(z_ref, dy_ref, lng_ref, lnb_ref, ws_ref, be_ref, buf_ref, dz_ref, dlg_ref, dlb_ref, dws_ref, db_ref, du_s, dvl_s, dbacc_s):
        i = pl.program_id(0)
        zv = z_ref[...]
        lng_v = lng_ref[...]
        u, vn, rstd, vl = _sgu_ln(zv, SW, lng_v, lnb_ref[...])

        @pl.when(i == 0)
        def _():
            for s in (dlg_ref, dlb_ref, dws_ref, dbacc_s):
                s[...] = jnp.zeros_like(s)

        for g in range(G):
            cs = slice(g * SGU_GROUP, (g + 1) * SGU_GROUP)
            tri, wc = _causal(ws_ref, g)
            for n in range(nch):
                rs = slice(n * SGU_CHUNK, (n + 1) * SGU_CHUNK)
                blk = vl[rs, cs].astype(BF16)
                m = jnp.dot(wc, blk, preferred_element_type=F32) + be_ref[:, cs]
                dyv = dy_ref[rs, cs]
                du_s[rs, cs] = dyv * m
                dm = dyv * u[rs, cs]
                dvl_s[rs, cs] = _bdot(wc, dm, "tn")
                dws_ref[g] += jnp.where(tri, _bdot(dm, blk, "nt"), 0.0)
                dbacc_s[:, cs] += dm

        dvl = dvl_s[...]
        dlg_ref[...] += _colsum(dvl * vn)
        dlb_ref[...] += _colsum(dvl)
        dvn = dvl * lng_v
        dvv = rstd * (dvn - _mean(dvn) - vn * _mean(dvn * vn))
        gp = _gelu_grad(zv)
        dz_ref[:, :SW] = (du_s[...] * gp[:, :SW]).astype(BF16)
        dz_ref[:, SW:] = (dvv * gp[:, SW:]).astype(BF16)

        @pl.when(i == nt - 1)
        def _():
            lane = lax.broadcasted_iota(jnp.int32, (SGU_CHUNK, LANE), 1)
            out = jnp.zeros((SGU_CHUNK, LANE), F32)
            for g in range(G):
                col = jnp.sum(dbacc_s[:, g * SGU_GROUP:(g + 1) * SGU_GROUP], axis=1, keepdims=True)
                out = jnp.where(lane == g, col, out)
            db_ref[...] = out

    whole = lambda arr: pl.BlockSpec(arr.shape, lambda i, nd=arr.ndim: (0,) * nd)
    acc_shapes = [(1, SW), (1, SW), ws.shape, (SGU_CHUNK, LANE)]
    return pl.pallas_call(
        body, name=name, grid=(nt,),
        in_specs=[pl.BlockSpec((tr, 2 * SW), lambda i: (i, zblock)), pl.BlockSpec((tr, SW), lambda i: (i, 0)),
                  whole(lng), whole(lnb), whole(ws), whole(bexp), pl.BlockSpec(memory_space=pl.ANY)],
        out_specs=([pl.BlockSpec((tr, 2 * SW), lambda i: (i, zblock))]
                   + [pl.BlockSpec(s, lambda i, nd=len(s): (0,) * nd) for s in acc_shapes]),
        out_shape=[jax.ShapeDtypeStruct(dproj.shape, BF16)] + [jax.ShapeDtypeStruct(s, F32) for s in acc_shapes],
        scratch_shapes=[pltpu.VMEM((tr, SW), F32), pltpu.VMEM((tr, SW), F32), pltpu.VMEM((SGU_CHUNK, SW), F32)],
        input_output_aliases={6: 0}, compiler_params=_params())(proj, dyb, lng, lnb, ws, bexp, dproj)


_HBM = pl.BlockSpec(memory_space=pltpu.HBM)
_SEM = pl.BlockSpec(memory_space=pltpu.SEMAPHORE)
_DATAFLOW = pltpu.SideEffectType.DATAFLOW_SIDE_EFFECTING


def _mesh_place(chips=False):
    x, y, c = lax.axis_index("x"), lax.axis_index("y"), lax.axis_index("c")
    return x, y, c, (2 * x + y if chips else 4 * x + 2 * y + c)


def _peer(x, y, c, rel, chips=False):
    px = 1 - x if rel & 4 else x
    py = 1 - y if rel & 2 else y
    pc = 1 - c if rel & 1 else c
    return (px, py, pc), (2 * px + py if chips else 4 * px + 2 * py + pc)


ALL_PEERS = tuple(range(1, N_DEV))
SIBLING = (1,)
SAME_CORE = (2, 4, 6)
SIBLINGS_CORE = (3, 5, 7)


def _exchange_start(groups, name, rels=ALL_PEERS, chips=False):
    flat = [t for g in groups for t in g]
    sizes = [len(g) for g in groups]
    n, ng = len(flat), len(groups)
    srcs = [pltpu.with_memory_space_constraint(a, pltpu.HBM) for a, _ in flat]
    lands = [pltpu.with_memory_space_constraint(lax.empty(((N_DEV,) + a.shape) if isg else a.shape, a.dtype), pltpu.HBM)
             for a, isg in flat]

    def body(*refs):
        ins, lnd, sems, token = refs[:n], refs[n:2 * n], refs[2 * n:2 * n + 3 * ng], refs[-1]
        x, y, c, me = _mesh_place(chips)
        j0 = 0
        for gi, sz in enumerate(sizes):
            for rel in rels:
                dev, slot = _peer(x, y, c, rel, chips)
                for jj in range(sz):
                    j = j0 + jj
                    pltpu.make_async_remote_copy(
                        src_ref=ins[j] if flat[j][1] else ins[j].at[slot], dst_ref=lnd[j].at[me],
                        send_sem=sems[3 * gi].at[jj * (N_DEV - 1) + rel - 1], recv_sem=sems[3 * gi + 1].at[jj * (N_DEV - 1) + rel - 1],
                        device_id=dev, device_id_type=pl.DeviceIdType.MESH).start()
            for jj in range(sz):
                j = j0 + jj
                pltpu.make_async_copy(ins[j] if flat[j][1] else ins[j].at[me], lnd[j].at[me], sems[3 * gi + 2].at[jj]).start()
            j0 += sz
        token[...] = jnp.zeros_like(token)

    sem_shapes = [pltpu.SemaphoreType.DMA((k,)) for sz in sizes for k in (sz * (N_DEV - 1), sz * (N_DEV - 1), sz)]
    res = pl.pallas_call(
        body, name=name,
        out_shape=(*sem_shapes, *[pltpu.HBM(a.shape, a.dtype) for a in srcs], *[pltpu.HBM(a.shape, a.dtype) for a in lands],
                   jax.ShapeDtypeStruct((SUBLANE, LANE), F32)),
        in_specs=[_HBM] * (2 * n), out_specs=(*[_SEM] * (3 * ng), *[_HBM] * (2 * n), pl.BlockSpec(memory_space=pltpu.VMEM)),
        input_output_aliases={i: 3 * ng + i for i in range(2 * n)},
        compiler_params=pltpu.CompilerParams(has_side_effects=_DATAFLOW))(*srcs, *lands)
    sems, thru, token = res[:3 * ng], res[3 * ng:3 * ng + 2 * n], res[-1]
    handle, j0 = [], 0
    for gi, sz in enumerate(sizes):
        handle.append(dict(kinds=[k for _, k in groups[gi]], chips=chips, srcs=list(thru[j0:j0 + sz]), lands=list(thru[n + j0:n + j0 + sz]),
                           sems=list(sems[3 * gi:3 * gi + 3])))
        j0 += sz
    return handle, token


def _exchange_wait(group, after, name, rels=ALL_PEERS, local=True):
    kinds, sz = group["kinds"], len(group["kinds"])
    relay = group.get("relay", [])

    def body(*refs):
        ins, lnd, (ssem, rsem, lsem) = refs[:sz], refs[sz:2 * sz], refs[2 * sz:2 * sz + 3]
        x, y, c, me = _mesh_place(group["chips"])
        for rel in rels:
            dev, slot = _peer(x, y, c, rel, group["chips"])
            for jj in range(sz):
                cp = pltpu.make_async_remote_copy(
                    src_ref=ins[jj] if kinds[jj] else ins[jj].at[slot], dst_ref=lnd[jj].at[slot],
                    send_sem=ssem.at[jj * (N_DEV - 1) + rel - 1], recv_sem=rsem.at[jj * (N_DEV - 1) + rel - 1],
                    device_id=dev, device_id_type=pl.DeviceIdType.MESH)
                cp.wait_send()
                cp.wait_recv()
        if local:
            for jj in range(sz):
                pltpu.make_async_copy(ins[jj] if kinds[jj] else ins[jj].at[me], lnd[jj].at[me], lsem.at[jj]).wait()
        if relay:
            fsend, frecv = refs[2 * sz + 3:2 * sz + 5]
            dev = _peer(x, y, c, 1)[0]
            for q, (mine, theirs) in enumerate(zip(SAME_CORE, SIBLINGS_CORE)):
                for jj in range(sz):
                    cp = pltpu.make_async_remote_copy(
                        src_ref=lnd[jj].at[_peer(x, y, c, mine)[1]], dst_ref=lnd[jj].at[_peer(x, y, c, theirs)[1]],
                        send_sem=fsend.at[jj * len(SAME_CORE) + q], recv_sem=frecv.at[jj * len(SAME_CORE) + q],
                        device_id=dev, device_id_type=pl.DeviceIdType.MESH)
                    cp.wait_send()
                    cp.wait_recv()

    arrays = group["srcs"] + group["lands"]
    sems = group["sems"] + relay
    res = pl.pallas_call(
        body, name=name, out_shape=[pltpu.HBM(a.shape, a.dtype) for a in arrays],
        in_specs=[_HBM] * (2 * sz) + [_SEM] * len(sems) + [pl.BlockSpec(memory_space=pl.ANY)], out_specs=[_HBM] * (2 * sz),
        input_output_aliases={i: i for i in range(2 * sz)},
        compiler_params=pltpu.CompilerParams(has_side_effects=_DATAFLOW))(*arrays, *sems, after)
    return dict(group, srcs=list(res[:sz]), lands=list(res[sz:]), relay=[])


def _relay_start(group, name):
    sz = len(group["kinds"])
    nq = len(SAME_CORE)

    def body(*refs):
        lnd, fsend, frecv, token = refs[:sz], refs[sz], refs[sz + 1], refs[-1]
        x, y, c, _ = _mesh_place()
        dev = _peer(x, y, c, 1)[0]
        for q, rel in enumerate(SAME_CORE):
            slot = _peer(x, y, c, rel)[1]
            for jj in range(sz):
                pltpu.make_async_remote_copy(
                    src_ref=lnd[jj].at[slot], dst_ref=lnd[jj].at[slot], send_sem=fsend.at[jj * nq + q], recv_sem=frecv.at[jj * nq + q],
                    device_id=dev, device_id_type=pl.DeviceIdType.MESH).start()
        token[...] = jnp.zeros_like(token)

    lands = group["lands"]
    res = pl.pallas_call(
        body, name=name,
        out_shape=(pltpu.SemaphoreType.DMA((sz * nq,)), pltpu.SemaphoreType.DMA((sz * nq,)), *[pltpu.HBM(a.shape, a.dtype) for a in lands],
                   jax.ShapeDtypeStruct((SUBLANE, LANE), F32)),
        in_specs=[_HBM] * sz, out_specs=(_SEM, _SEM, *[_HBM] * sz, pl.BlockSpec(memory_space=pltpu.VMEM)),
        input_output_aliases={i: 2 + i for i in range(sz)},
        compiler_params=pltpu.CompilerParams(has_side_effects=_DATAFLOW))(*lands)
    return dict(group, lands=list(res[2:2 + sz]), relay=[res[0], res[1]]), res[-1]


def _sibling_swap(arrays, handle, after, name):
    start = handle is None
    n = len(arrays) if start else len(handle["srcs"])
    chips = N_DEV // 2
    if start:
        srcs = [pltpu.with_memory_space_constraint(a.reshape(chips, 2, *a.shape[1:]), pltpu.HBM) for a in arrays]
        lands = [pltpu.with_memory_space_constraint(lax.empty((chips,) + a.shape[1:], a.dtype), pltpu.HBM) for a in arrays]
    else:
        srcs, lands = handle["srcs"], handle["lands"]

    def body(*refs):
        ins, lnd, ssem, rsem = refs[:n], refs[n:2 * n], refs[2 * n], refs[2 * n + 1]
        x, y, c, _ = _mesh_place()
        dev = _peer(x, y, c, 1)[0]
        for q in range(chips):
            for j in range(n):
                cp = pltpu.make_async_remote_copy(
                    src_ref=ins[j].at[q, 1 - c], dst_ref=lnd[j].at[q], send_sem=ssem.at[j * chips + q], recv_sem=rsem.at[j * chips + q],
                    device_id=dev, device_id_type=pl.DeviceIdType.MESH)
                if start:
                    cp.start()
                else:
                    cp.wait_send()
                    cp.wait_recv()
        if start:
            refs[-1][...] = jnp.zeros_like(refs[-1])

    thru = [pltpu.HBM(a.shape, a.dtype) for a in srcs + lands]
    effect = pltpu.CompilerParams(has_side_effects=_DATAFLOW)
    if start:
        res = pl.pallas_call(
            body, name=name, out_shape=(pltpu.SemaphoreType.DMA((n * chips,)), pltpu.SemaphoreType.DMA((n * chips,)), *thru,
                                        jax.ShapeDtypeStruct((SUBLANE, LANE), F32)),
            in_specs=[_HBM] * (2 * n), out_specs=(_SEM, _SEM, *[_HBM] * (2 * n), pl.BlockSpec(memory_space=pltpu.VMEM)),
            input_output_aliases={i: 2 + i for i in range(2 * n)}, compiler_params=effect)(*srcs, *lands)
        return dict(srcs=list(res[2:2 + n]), lands=list(res[2 + n:2 + 2 * n]), sems=[res[0], res[1]]), res[-1]
    res = pl.pallas_call(
        body, name=name, out_shape=thru, in_specs=[_HBM] * (2 * n) + [_SEM, _SEM, pl.BlockSpec(memory_space=pl.ANY)],
        out_specs=[_HBM] * (2 * n), input_output_aliases={i: i for i in range(2 * n)}, compiler_params=effect)(
            *srcs, *lands, *handle["sems"], after)
    return dict(handle, srcs=list(res[:n]), lands=list(res[n:]))


def _pair_add(mine, theirs, core, name):
    chips, _, rows, w = mine.shape
    tm = _pick(rows, (256, 128, 64, 32, 16))

    def body(core_ref, a_ref, b_ref, o_ref):
        o_ref[...] = (a_ref[...].astype(F32) + b_ref[...].astype(F32)).astype(o_ref.dtype)

    return pl.pallas_call(
        body, name=name, out_shape=jax.ShapeDtypeStruct(theirs.shape, theirs.dtype),
        grid_spec=pltpu.PrefetchScalarGridSpec(
            num_scalar_prefetch=1, grid=(chips, rows // tm),
            in_specs=[pl.BlockSpec((None, None, tm, w), lambda q, i, core_ref: (q, core_ref[0], i, 0)),
                      pl.BlockSpec((None, tm, w), lambda q, i, core_ref: (q, i, 0))],
            out_specs=pl.BlockSpec((None, tm, w), lambda q, i, core_ref: (q, i, 0))),
        compiler_params=_params())(core, mine, theirs)


def _adamw(w, m, v, gparts, name, after=None):
    R, C = w.shape
    tm = _pick(R, (256, 128, 64, 32, 16, 8))
    order = [] if after is None else [after]

    def body(w_ref, m_ref, v_ref, g_ref, *rest):
        go, do, mo, vo = rest[len(order):]
        g = g_ref[0].astype(F32)
        for j in range(1, gparts.shape[0]):
            g = g + g_ref[j].astype(F32)
        mn = ADAM_B1 * m_ref[...] + (1.0 - ADAM_B1) * g
        vn = ADAM_B2 * v_ref[...] + (1.0 - ADAM_B2) * (g * g)
        m_hat = mn / (1.0 - ADAM_B1 ** ADAM_STEP)
        v_hat = vn / (1.0 - ADAM_B2 ** ADAM_STEP)
        go[...] = g
        do[...] = -ADAM_LR * (m_hat / (jnp.sqrt(v_hat) + ADAM_EPS) + ADAM_WD * w_ref[...])
        mo[...] = mn
        vo[...] = vn

    row = pl.BlockSpec((tm, C), lambda i: (i, 0))
    return pl.pallas_call(
        body, name=name, grid=(R // tm,),
        in_specs=[row, row, row, pl.BlockSpec((gparts.shape[0], tm, C), lambda i: (0, i, 0))] + [pl.BlockSpec(memory_space=pl.ANY)] * len(order),
        out_specs=[row] * 4, out_shape=[jax.ShapeDtypeStruct((R, C), F32)] * 4, compiler_params=_params())(w, m, v, gparts, *order)


def _pack(arrays):
    parts = []
    for a in arrays:
        f = a.reshape(1, -1)
        pad = _ceil_to(f.shape[1], SUBLANE * LANE) - f.shape[1]
        f = jnp.concatenate([f, jnp.zeros((1, pad), f.dtype)], axis=1) if pad else f
        parts.append(f.reshape(-1, LANE))
    rows = sum(p.shape[0] for p in parts)
    pad = _ceil_to(rows, 64) - rows
    return jnp.concatenate(parts + ([jnp.zeros((pad, LANE), parts[0].dtype)] if pad else []), axis=0)


def _unpack(buf, shapes):
    out, row = [], 0
    for s in shapes:
        size = 1
        for d in s:
            size *= d
        rows = _ceil_to(size, SUBLANE * LANE) // LANE
        out.append(buf[row:row + rows].reshape(1, -1)[:, :size].reshape(s))
        row += rows
    return out


def kernel(x, norm_mix_g, w_in, shift_mu, w0, w_lora_up, a0, a_lora_up, g_lora_up, k_k, k_a, r_k, lnx_g, lnx_b, w_proj_rwkv, sgu_ln_g, sgu_ln_b, sgu_w, sgu_b, w_proj_sgu, w_out, norm_ffn_g, w_ffn_gate, w_ffn_up, w_ffn_down, norm_final_g, loss_target, m_norm_mix_g, m_w_in, m_shift_mu, m_w0, m_w_lora_up, m_a0, m_a_lora_up, m_g_lora_up, m_k_k, m_k_a, m_r_k, m_lnx_g, m_lnx_b, m_w_proj_rwkv, m_sgu_ln_g, m_sgu_ln_b, m_sgu_w, m_sgu_b, m_w_proj_sgu, m_w_out, m_norm_ffn_g, m_w_ffn_gate, m_w_ffn_up, m_w_ffn_down, m_norm_final_g, v_norm_mix_g, v_w_in, v_shift_mu, v_w0, v_w_lora_up, v_a0, v_a_lora_up, v_g_lora_up, v_k_k, v_k_a, v_r_k, v_lnx_g, v_lnx_b, v_w_proj_rwkv, v_sgu_ln_g, v_sgu_ln_b, v_sgu_w, v_sgu_b, v_w_proj_sgu, v_w_out, v_norm_ffn_g, v_w_ffn_gate, v_w_ffn_up, v_w_ffn_down, v_norm_final_g):
    weights = dict(norm_mix_g=norm_mix_g, w_in=w_in, shift_mu=shift_mu, w0=w0, w_lora_up=w_lora_up, a0=a0, a_lora_up=a_lora_up,
                   g_lora_up=g_lora_up, k_k=k_k, k_a=k_a, r_k=r_k, lnx_g=lnx_g, lnx_b=lnx_b, w_proj_rwkv=w_proj_rwkv,
                   sgu_ln_g=sgu_ln_g, sgu_ln_b=sgu_ln_b, sgu_w=sgu_w, sgu_b=sgu_b, w_proj_sgu=w_proj_sgu, w_out=w_out,
                   norm_ffn_g=norm_ffn_g, w_ffn_gate=w_ffn_gate, w_ffn_up=w_ffn_up, w_ffn_down=w_ffn_down, norm_final_g=norm_final_g)
    m_in = dict(norm_mix_g=m_norm_mix_g, w_in=m_w_in, shift_mu=m_shift_mu, w0=m_w0, w_lora_up=m_w_lora_up, a0=m_a0,
                a_lora_up=m_a_lora_up, g_lora_up=m_g_lora_up, k_k=m_k_k, k_a=m_k_a, r_k=m_r_k, lnx_g=m_lnx_g, lnx_b=m_lnx_b,
                w_proj_rwkv=m_w_proj_rwkv, sgu_ln_g=m_sgu_ln_g, sgu_ln_b=m_sgu_ln_b, sgu_w=m_sgu_w, sgu_b=m_sgu_b,
                w_proj_sgu=m_w_proj_sgu, w_out=m_w_out, norm_ffn_g=m_norm_ffn_g, w_ffn_gate=m_w_ffn_gate, w_ffn_up=m_w_ffn_up,
                w_ffn_down=m_w_ffn_down, norm_final_g=m_norm_final_g)
    v_in = dict(norm_mix_g=v_norm_mix_g, w_in=v_w_in, shift_mu=v_shift_mu, w0=v_w0, w_lora_up=v_w_lora_up, a0=v_a0,
                a_lora_up=v_a_lora_up, g_lora_up=v_g_lora_up, k_k=v_k_k, k_a=v_k_a, r_k=v_r_k, lnx_g=v_lnx_g, lnx_b=v_lnx_b,
                w_proj_rwkv=v_w_proj_rwkv, sgu_ln_g=v_sgu_ln_g, sgu_ln_b=v_sgu_ln_b, sgu_w=v_sgu_w, sgu_b=v_sgu_b,
                w_proj_sgu=v_w_proj_sgu, w_out=v_w_out, norm_ffn_g=v_norm_ffn_g, w_ffn_gate=v_w_ffn_gate, w_ffn_up=v_w_ffn_up,
                w_ffn_down=v_w_ffn_down, norm_final_g=v_norm_final_g)
    names = list(weights)
    col_sharded = ("w_in", "w_lora_up", "a_lora_up", "g_lora_up", "w_proj_rwkv", "w_proj_sgu", "w_ffn_gate", "w_ffn_up")
    row_sharded = ("w_out", "w_ffn_down")
    sharded = [n for n in names if n in col_sharded or n in row_sharded]
    small = [n for n in names if n not in sharded]

    xs, tgt = x[0], loss_target[0]
    T, D = xs.shape
    RW = w0.shape[1]
    H = RW // HEAD
    SW = sgu_ln_g.shape[1]
    G = sgu_w.shape[1]
    assert 2 * SW == D, "the projection layout takes the SGU part to be as wide as a gate"
    lay = _rwkv_layout(RW, w_lora_up.shape[1], a_lora_up.shape[1], g_lora_up.shape[1], D)
    _, pw, _, rcp = lay
    icp = rcp + 3 * D
    b_ga, b_gb, b_z = rcp // D, rcp // D + 1, rcp // D + 2

    gather_groups = dict(win=["w_in", "w_lora_up", "a_lora_up", "g_lora_up"], proj=["w_proj_rwkv", "w_proj_sgu", "w_out"],
                         ffn_gate_up=["w_ffn_gate", "w_ffn_up"], ffn_down=["w_ffn_down"])
    handles, gather_token = _exchange_start([[(weights[n][0].astype(BF16), True) for n in grp] for grp in gather_groups.values()],
                                            "gather_start", rels=SIBLING + SAME_CORE)
    gather = dict(zip(gather_groups, handles))
    full = {}
    relay_tokens = {}
    joined = lambda g: g.transpose(1, 0, 2).reshape(g.shape[1], -1)

    def relay_weights(key, after):
        arrived = _exchange_wait(gather[key], after, "gather_wait_ici_" + key, rels=SAME_CORE, local=False)
        gather[key], relay_tokens[key] = _relay_start(arrived, "gather_relay_" + key)

    def take_weights(key, after):
        done = _exchange_wait(gather[key], after, "gather_wait_d2d_" + key, rels=SIBLING)
        for n, g in zip(gather_groups[key], done["lands"]):
            full[n] = g.reshape(-1, g.shape[2]) if n in row_sharded else g

    packed = [_pack([d[n] for n in small] + [gather_token]) for d in (weights, m_in, v_in)]
    n1 = _rms_fwd(xs, norm_mix_g, "rms_mix", deps=[gather_token, *packed])
    relay_weights("win", n1)
    take_weights("win", relay_tokens["win"])
    W_in = _w_in_to_proj(full["w_in"], lay, D, "w_in_layout")
    lora = [_pad_rows(joined(full[n]), rows) for n, rows in zip(("w_lora_up", "a_lora_up", "g_lora_up"), pw[3:])]
    mu_p = _pad_rwkv_cols(shift_mu, lay)
    rsmall = [w0, a0, k_k, k_a]
    hp = [lnx_g.reshape(H, 1, HEAD), lnx_b.reshape(H, 1, HEAD), r_k.reshape(H, 1, HEAD)]
    ws = sgu_w[0]
    bexp = jnp.repeat(sgu_b[0].T, SGU_GROUP, axis=1)
    gf = norm_final_g.reshape(1, D)

    proj = _matmul(n1, W_in, mode="nn", out_dtype=F32, name="proj_in")
    ga, gb = (proj, D, b_ga), (proj, D, b_gb)
    r_h, lw_h, k2_h, v_h, aa_h, bb_h, g_h = _rwkv_pre(proj, mu_p, rsmall, lora, lay, "rwkv_pre")
    wkv_in = [r_h, lw_h, k2_h, v_h, aa_h, bb_h]
    y_h, *wkv_saved = _wkv_fwd(*wkv_in, "wkv_fwd")
    relay_weights("proj", y_h)
    ya = _head_post(y_h, r_h, k2_h, v_h, g_h, hp, "head_post", deps=[relay_tokens["proj"]])
    relay_weights("ffn_gate_up", ya)
    yb = _sgu_fwd(proj, b_z, sgu_ln_g, sgu_ln_b, ws, bexp, "sgu_fwd")
    take_weights("proj", ya)
    pa = _matmul(ya, full["w_proj_rwkv"], mode="nn", out_dtype=F32, name="proj_a", deps=[relay_tokens["ffn_gate_up"]])

    def merge_fn(pb_v, pa_v, ga_v, gb_v):
        return pb_v, _sigmoid(ga_v) * pa_v + _sigmoid(gb_v) * pb_v
    pb, merged = _matmul(yb, full["w_proj_sgu"], mode="nn", name="proj_b_merge",
                         epi=(merge_fn, [pa, (proj, b_ga * D), (proj, b_gb * D)], [F32, BF16]))
    h1 = _matmul(merged, full["w_out"], mode="nn", out_dtype=F32, name="out_proj", add=xs)
    n2 = _rms_fwd(h1, norm_ffn_g, "rms_ffn")
    relay_weights("ffn_down", n2)
    take_weights("ffn_gate_up", n2)

    def act_fn(gt_v, up_v):
        return gt_v, up_v, gt_v * _sigmoid(gt_v) * up_v
    gt, up, act = _matmul(n2, full["w_ffn_gate"], b2=full["w_ffn_up"], mode="nn", name="ffn_gate_up_act", out_blocks=N_DEV,
                          epi=(act_fn, [], [BF16, BF16, BF16]), deps=[relay_tokens["ffn_down"]])
    take_weights("ffn_down", act)
    h2 = _matmul(act, full["w_ffn_down"], mode="nn", out_dtype=F32, name="ffn_down", add=h1)

    def final_fn(rv, pv):
        (h_v, t_v), (g_v,) = rv, pv
        r = lax.rsqrt(_mean(h_v * h_v) + RMS_EPS)
        yn = h_v * r
        e = yn * g_v - t_v
        loss = 0.5 * jnp.sum(_mean(e * e))
        dout = e * (1.0 / D)
        dyg = dout * g_v
        dh = r * (dyg - yn * _mean(dyg * yn))
        return [dh, dh], [jnp.full((1, LANE), loss, F32), _colsum(dout * yn)]
    dh2, dh2_bf, loss_part, d_gf = _rowwise(final_fn, [h2, tgt], [gf], [(D, F32), (D, BF16)], [(1, LANE), (1, D)], name="final_loss")

    grads = {}

    def start_scatter(group, name):
        blocks = [(grads[n].reshape(N_DEV, -1, grads[n].shape[1]) if n in row_sharded else grads[n], False) for n in group]
        (handle,), token = _exchange_start([blocks], name)
        return handle, token

    def dact_fn(d_v, gt_v, up_v):
        gt_v, up_v = gt_v.astype(F32), up_v.astype(F32)
        s = _sigmoid(gt_v)
        return d_v * up_v * (s * (1.0 + gt_v * (1.0 - s))), d_v * gt_v * s
    dgt, dup = _matmul(dh2_bf, full["w_ffn_down"], mode="nt", name="d_ffn_act", out_blocks=N_DEV,
                       epi=(dact_fn, [gt, up], [BF16, BF16]))
    scatter_groups = dict(ffn_down=["w_ffn_down"], ffn_gate=["w_ffn_gate"], ffn_up=["w_ffn_up"],
                          mid=["w_out", "w_proj_rwkv", "w_proj_sgu"], last=["w_in", "w_lora_up", "a_lora_up", "g_lora_up"])
    scatters = {}
    grads["w_ffn_down"] = _matmul(act, dh2_bf, mode="tn", out_dtype=BF16, name="dw_ffn_down")
    scatters["ffn_down"], token = start_scatter(scatter_groups["ffn_down"], "scatter_start_ffn_down")
    dn2 = _matmul(dgt, full["w_ffn_gate"], mode="nt", out_dtype=F32, name="dn2_gate", deps=[token])
    grads["w_ffn_gate"] = _matmul(n2, dgt, mode="tn", out_dtype=BF16, name="dw_ffn_gate", out_blocks=N_DEV)
    scatters["ffn_gate"], token = start_scatter(scatter_groups["ffn_gate"], "scatter_start_ffn_gate")
    grads["w_ffn_up"] = _matmul(n2, dup, mode="tn", out_dtype=BF16, name="dw_ffn_up", out_blocks=N_DEV, deps=[token])
    scatters["ffn_up"], token = start_scatter(scatter_groups["ffn_up"], "scatter_start_ffn_up")
    dn2 = _matmul(dup, full["w_ffn_up"], mode="nt", out_dtype=F32, name="dn2_up", add=dn2, deps=[token])
    dh1, dh1_bf, d_g2 = _rms_bwd(dn2, h1, dh2, norm_ffn_g, "rms_ffn_bwd")
    dmerged = _matmul(dh1_bf, full["w_out"], mode="nt", out_dtype=F32, name="d_merged")
    grads["w_out"] = _matmul(merged, dh1_bf, mode="tn", out_dtype=BF16, name="dw_out")

    def dmerge_fn(rv, pv):
        d_v, ga_v, gb_v, pa_v, pb_v = rv
        sa, sb = _sigmoid(ga_v), _sigmoid(gb_v)
        dgates = jnp.concatenate([d_v * pa_v * sa * (1.0 - sa), d_v * pb_v * sb * (1.0 - sb)], axis=1)
        return [dgates, d_v * sa, d_v * sb], []
    dproj, dpa, dpb = _rowwise(dmerge_fn, [dmerged, ga, gb, pa, pb], [],
                               [(2 * D, BF16, icp, b_ga // 2, None), (D, BF16), (D, BF16)], [], name="d_merge")
    dya = _matmul(dpa, full["w_proj_rwkv"], mode="nt", out_dtype=F32, name="d_ya")
    dyb = _matmul(dpb, full["w_proj_sgu"], mode="nt", out_dtype=F32, name="d_yb")
    grads["w_proj_rwkv"] = _matmul(ya, dpa, mode="tn", out_dtype=BF16, name="dw_proj_a", out_blocks=N_DEV)
    grads["w_proj_sgu"] = _matmul(yb, dpb, mode="tn", out_dtype=BF16, name="dw_proj_b", out_blocks=N_DEV)
    scatters["mid"], token_mid = start_scatter(scatter_groups["mid"], "scatter_start_mid")
    dproj, d_lng, d_lnb, d_ws, d_bs = _sgu_bwd(proj, b_z, dyb, sgu_ln_g, sgu_ln_b, ws, bexp, dproj, "sgu_bwd")

    dr_h, dlw_h, dk2_h, dv_h, daa, dbb, dg_h, d_lnxg, d_lnxb, d_rk = _wkv_bwd(
        *wkv_in, *wkv_saved, y_h, g_h, hp, dya, "wkv_bwd", deps=[token_mid])
    dproj, d_mu, d_w0, d_a0, d_kk, d_ka, d_wlw, d_wla, d_wlg = _rwkv_pre_bwd(
        proj, mu_p, rsmall, lora, [dr_h, dk2_h, dv_h, dlw_h, daa, dbb, dg_h], dproj, lay, "rwkv_pre_bwd")
    split = lambda g: g.reshape(g.shape[0], N_DEV, -1).transpose(1, 0, 2)
    grads["w_in"] = _dw_in_from_proj(_matmul(n1, dproj, mode="tn", out_dtype=BF16, name="dw_in"), lay, D, w_in.shape[2], "dw_in_layout")
    grads["w_lora_up"] = split(d_wlw[:w_lora_up.shape[1]].astype(BF16))
    grads["a_lora_up"] = split(d_wla[:a_lora_up.shape[1]].astype(BF16))
    grads["g_lora_up"] = split(d_wlg[:g_lora_up.shape[1]].astype(BF16))
    out = {}

    def update_group(key, after):
        handle = scatters[key]
        parts = _exchange_wait(handle, after, "scatter_wait_" + key, rels=SAME_CORE if handle["chips"] else ALL_PEERS)["lands"]
        for n, part in zip(scatter_groups[key], parts):
            res = _adamw(weights[n][0], m_in[n][0], v_in[n][0], part, "adamw_" + n, after=after)
            out[n] = [t.reshape(weights[n].shape) for t in res]
            after = res[0]
        return after

    swap, token_swap = _sibling_swap([grads[n] for n in scatter_groups["last"]], None, None, "scatter_last_swap_start")
    after = update_group("ffn_gate", update_group("ffn_down", token_swap))
    swap = _sibling_swap(None, swap, after, "scatter_last_swap_wait")
    core = lax.axis_index("c").astype(jnp.int32).reshape(1)
    chip_sums = [_pair_add(mine, theirs, core, "scatter_last_add_" + n)
                 for n, mine, theirs in zip(scatter_groups["last"], swap["srcs"], swap["lands"])]
    (scatters["last"],), token_in = _exchange_start([[(s, False) for s in chip_sums]], "scatter_start_last", rels=SAME_CORE, chips=True)
    dn1 = _matmul(dproj, W_in, mode="nt", out_dtype=F32, name="dn1", deps=[token_in])
    dx, _, d_g1 = _rms_bwd(dn1, xs, dh1, norm_mix_g, "rms_mix_bwd")
    small_grads = dict(norm_mix_g=d_g1, shift_mu=_unpad_rwkv_cols(d_mu, lay), w0=d_w0, a0=d_a0, k_k=d_kk, k_a=d_ka, r_k=d_rk,
                       lnx_g=d_lnxg, lnx_b=d_lnxb, sgu_ln_g=d_lng, sgu_ln_b=d_lnb, sgu_w=d_ws, sgu_b=d_bs[:, :G].T,
                       norm_ffn_g=d_g2, norm_final_g=d_gf)
    loss_rows = jnp.broadcast_to(loss_part, gather_token.shape)
    (gather_small,), after = _exchange_start([[(_pack([small_grads[n] for n in small] + [loss_rows]), True)]], "gather_small_start")
    for key in ("ffn_up", "mid", "last"):
        after = update_group(key, after)
    small_parts = _exchange_wait(gather_small, after, "gather_small_wait")["lands"][0]
    res = _adamw(*packed, small_parts, "adamw_small")
    unpacked = [_unpack(t, [weights[n].shape for n in small]) for t in res]
    for i, n in enumerate(small):
        out[n] = [u[i] for u in unpacked]

    loss = _unpack(res[0], [weights[n].shape for n in small] + [gather_token.shape])[-1][0, 0]
    return (loss, dx[None], *[out[n][0] for n in names], *[out[n][1] for n in names],
            *[out[n][2] for n in names], *[out[n][3] for n in names])
```

```python
import jax
import jax.numpy as jnp
from jax import lax
from jax.experimental import pallas as pl
from jax.experimental.pallas import tpu as pltpu

F32 = jnp.float32
BF16 = jnp.bfloat16

N_DEV = 8
LANE = 128
SUBLANE = 8
HEAD = 64
SGU_CHUNK = 128
SGU_GROUP = 128
WKV_CHUNK = 64
RMS_EPS = 1e-6
LN_EPS = 1e-5
LNX_EPS = 64e-5
ADAM_LR, ADAM_B1, ADAM_B2, ADAM_EPS, ADAM_WD, ADAM_STEP = 0.001, 0.9, 0.999, 1e-08, 0.01, 10
VMEM_LIMIT_BYTES = 48 * 1024 * 1024
ROW_BUFFERS = 3
_SQRT_HALF = 0.7071067811865476
_INV_SQRT_2PI = 0.3989422804014327


def _pick(n, cands):
    for c in cands:
        if n % c == 0:
            return c
    return n


def _ceil_to(n, m):
    return -(-n // m) * m


def _params():
    return pltpu.CompilerParams(vmem_limit_bytes=VMEM_LIMIT_BYTES)


def _tile(n, cap):
    best = 0
    for d in range(LANE, min(n, cap) + 1, LANE):
        if n % d == 0:
            best = d
    return best or n


def _matmul_tiles(M, N, K, a_bytes, b_bytes, o_bytes, has_add, forced):
    tm = forced.get("m") or _tile(M, 1024)
    tn = forced.get("n") or _tile(N, 1024)
    tk = forced.get("k") or _tile(K, 2048)

    def vmem(tm, tn, tk):
        acc = tm * tn * 4 if tk < K else 0
        return 2 * (tm * tk * a_bytes + tk * tn * b_bytes + tm * tn * (o_bytes + (4 if has_add else 0))) + acc

    while vmem(tm, tn, tk) > (VMEM_LIMIT_BYTES * 3) // 4:
        if "k" not in forced and tk > 512 and _tile(K, tk // 2) < tk:
            tk = _tile(K, tk // 2)
        elif "m" not in forced and _tile(M, tm // 2) < tm:
            tm = _tile(M, tm // 2)
        else:
            break
    return tm, tn, tk


def _matmul(a, b, *, mode, out_dtype=F32, name, add=None, deps=(), out_blocks=0, epi=None, b2=None):
    def view(x):
        return (x.shape[1], x.shape[0] * x.shape[2], x.shape[2]) if x.ndim == 3 else (x.shape[0], x.shape[1], 0)

    (ar, ac, aw), (br, bc, bw) = view(a), view(b)
    a_col, b_col = {"nn": ("k", "n"), "nt": ("k", "k"), "tn": ("m", "n")}[mode]
    if mode == "nn":
        M, K, K2, N = ar, ac, br, bc
    elif mode == "nt":
        M, K, N, K2 = ar, ac, br, bc
    else:
        K, M, K2, N = ar, ac, br, bc
    assert K == K2, (a.shape, b.shape, mode)
    forced = {}
    for dim, w in ((a_col, aw), (b_col, bw), ("n", N // out_blocks if out_blocks else 0)):
        if w:
            assert forced.get(dim, w) == w
            forced[dim] = w
    has_add = add is not None
    tile_bytes = (sum(jnp.dtype(d).itemsize for d in epi[2]) + sum((e[0] if isinstance(e, tuple) else e).dtype.itemsize for e in epi[1])
                  if epi is not None else jnp.dtype(out_dtype).itemsize)
    tm, tn, tk = _matmul_tiles(M, N, K, a.dtype.itemsize, b.dtype.itemsize, tile_bytes, has_add, forced)
    kb = 1
    if "k" in forced and mode != "tn":
        lanes_ok = all(w or tk % LANE == 0 for w in (aw, bw if mode == "nt" else 1))
        kb = next(c for c in (4, 2, 1) if (K // tk) % c == 0 and (c == 1 or (lanes_ok and c * tk <= 1536)))
    nk = K // (tk * kb)
    dn = {"nn": (((1,), (0,)), ((), ())), "nt": (((1,), (1,)), ((), ())), "tn": (((0,), (0,)), ((), ()))}[mode]
    pick = {"m": lambda i, j, k: i, "n": lambda i, j, k: j, "k": lambda i, j, k: k}
    size = {"m": tm, "n": tn, "k": tk}

    def spec(blocked, row_dim, col_dim):
        rf, cf = pick[row_dim], pick[col_dim]
        reps = {d: (kb if d == "k" else 1) for d in (row_dim, col_dim)}
        if blocked:
            lead = kb if col_dim == "k" and kb > 1 else None
            return pl.BlockSpec((lead, size[row_dim], size[col_dim]), lambda i, j, k: (cf(i, j, k), rf(i, j, k), 0))
        return pl.BlockSpec((size[row_dim] * reps[row_dim], size[col_dim] * reps[col_dim]), lambda i, j, k: (rf(i, j, k), cf(i, j, k)))

    def k_part(ref, blocked, k_on_rows, j):
        if kb == 1:
            return ref[...]
        if blocked:
            return ref[j]
        return ref[j * tk:(j + 1) * tk, :] if k_on_rows else ref[:, j * tk:(j + 1) * tk]

    a_spec = spec(aw, "k" if mode == "tn" else "m", a_col)
    b_spec = spec(bw, "n" if mode == "nt" else "k", b_col)
    o_spec = spec(out_blocks, "m", "n")
    epi_fn, epi_ins, epi_dtypes = epi if epi is not None else (None, [], [out_dtype])
    epi_ins = [e if isinstance(e, tuple) else (e, None) for e in epi_ins]
    n_epi = len(epi_ins)
    twin = b2 is not None
    assert not twin or (nk == 1 and kb == 1 and epi is not None and b2.shape == b.shape)
    n_in = 2 + twin + has_add + n_epi + len(deps)
    n_out = len(epi_dtypes)

    def body(*refs):
        a_ref, b_ref = refs[0], refs[1]
        add_ref = refs[2 + twin] if has_add else None
        epi_refs = refs[2 + twin + has_add:2 + twin + has_add + n_epi]
        o_refs = refs[n_in:n_in + n_out]
        part = None
        for q in range(kb):
            a_q = k_part(a_ref, aw and a_col == "k", False, q)
            b_q = k_part(b_ref, bw and b_col == "k", mode == "nn", q)
            prod = lax.dot_general(a_q.astype(BF16), b_q.astype(BF16), dn, preferred_element_type=F32)
            part = prod if part is None else part + prod
        second = [lax.dot_general(a_ref[...].astype(BF16), refs[2][...].astype(BF16), dn, preferred_element_type=F32)] if twin else []

        def finish(res):
            outs = epi_fn(res, *second, *[e[...] for e in epi_refs]) if epi_fn is not None else (res,)
            for o_ref, val in zip(o_refs, outs):
                o_ref[...] = val.astype(o_ref.dtype)

        if nk == 1:
            finish(part + add_ref[...] if has_add else part)
            return
        acc_ref = refs[-1]
        kk = pl.program_id(2)

        @pl.when(kk == 0)
        def _():
            acc_ref[...] = part + add_ref[...] if has_add else part

        @pl.when(kk > 0)
        def _():
            acc_ref[...] += part

        @pl.when(kk == nk - 1)
        def _():
            finish(acc_ref[...])

    def epi_spec(arr, off):
        if off is None:
            return o_spec
        assert off % tn == 0
        return pl.BlockSpec((tm, tn), lambda i, j, k: (i, j + off // tn))

    ins = [a, b] + ([b2] if twin else []) + ([add] if has_add else []) + [arr for arr, _ in epi_ins] + list(deps)
    in_specs = ([a_spec, b_spec] + ([b_spec] if twin else []) + ([o_spec] if has_add else []) + [epi_spec(arr, off) for arr, off in epi_ins]
                + [pl.BlockSpec(d.shape, lambda i, j, k, nd=d.ndim: (0,) * nd) for d in deps])
    o_shape = (out_blocks, M, tn) if out_blocks else (M, N)
    res = pl.pallas_call(
        body, name=name, grid=(M // tm, N // tn, nk), in_specs=in_specs, out_specs=[o_spec] * n_out,
        out_shape=[jax.ShapeDtypeStruct(o_shape, dt) for dt in epi_dtypes],
        scratch_shapes=[pltpu.VMEM((tm, tn), F32)] if nk > 1 else [],
        compiler_params=_params())(*ins)
    return res[0] if epi is None else list(res)


def _rowwise(fn, rows, pars, row_outs, acc_outs, *, name, tm=256, deps=()):
    rows = [r if isinstance(r, tuple) else (r, r.shape[1], 0) for r in rows]
    row_outs = [o if len(o) == 5 else (o[0], o[1], o[0], 0, None) for o in row_outs]
    aliased = [(k, o[4]) for k, o in enumerate(row_outs) if o[4] is not None]
    R = rows[0][0].shape[0]
    if max(w for _, w, _ in rows) > 4096:
        tm = tm // 2
    tm = min(tm, R)
    assert R % tm == 0
    nr, npar = len(rows), len(pars)
    nro = len(row_outs)
    n_in = nr + npar + len(deps) + len(aliased)

    steps = R // tm
    depth = min(ROW_BUFFERS, steps)

    def body(*refs):
        bufs, sems = refs[len(refs) - 2 * nr:len(refs) - nr], refs[len(refs) - nr:]
        step = pl.program_id(0)
        slot = step % depth

        def fetch(j, block, into):
            _, w, cb = rows[j]
            return pltpu.make_async_copy(refs[j].at[pl.ds(block * tm, tm), pl.ds(cb * w, w)], bufs[j].at[into], sems[j].at[into])

        @pl.when(step == 0)
        def _():
            for block in range(depth):
                for j in range(nr):
                    fetch(j, block, block).start()

        for j in range(nr):
            fetch(j, step, slot).wait()
        rv = [b[slot] for b in bufs]
        pv = [p[...] for p in refs[nr:nr + npar]]
        outs = refs[n_in:len(refs) - 2 * nr]
        ro, ao = fn(rv, pv)
        first = step == 0
        for o_ref, val in zip(outs[:nro], ro):
            o_ref[...] = val.astype(o_ref.dtype)

        @pl.when(step + depth < steps)
        def _():
            for j in range(nr):
                fetch(j, step + depth, slot).start()

        @pl.when(first)
        def _():
            for o_ref, val in zip(outs[nro:], ao):
                o_ref[...] = val

        @pl.when(jnp.logical_not(first))
        def _():
            for o_ref, val in zip(outs[nro:], ao):
                o_ref[...] += val

    in_specs = ([pl.BlockSpec(memory_space=pl.ANY)] * nr
                + [pl.BlockSpec(p.shape, lambda i, nd=p.ndim: (0,) * nd) for p in list(pars) + list(deps)]
                + [pl.BlockSpec(memory_space=pl.ANY)] * len(aliased))
    out_shape = ([jax.ShapeDtypeStruct((R, full), dt) for _, dt, full, _, _ in row_outs]
                 + [jax.ShapeDtypeStruct(s, F32) for s in acc_outs])
    out_specs = ([pl.BlockSpec((tm, f), lambda i, cb=cb: (i, cb)) for f, _, _, cb, _ in row_outs]
                 + [pl.BlockSpec(s, lambda i, nd=len(s): (0,) * nd) for s in acc_outs])
    res = pl.pallas_call(body, name=name, grid=(R // tm,), in_specs=in_specs, out_specs=out_specs, out_shape=out_shape,
                         input_output_aliases={n_in - len(aliased) + q: k for q, (k, _) in enumerate(aliased)},
                         scratch_shapes=([pltpu.VMEM((depth, tm, w), r.dtype) for r, w, _ in rows]
                                         + [pltpu.SemaphoreType.DMA((depth,))] * nr),
                         compiler_params=_params())(*[r for r, _, _ in rows], *pars, *deps, *[buf for _, buf in aliased])
    return list(res)


def _bdot(a, b, mode="nn"):
    dn = {"nn": (((1,), (0,)), ((), ())), "nt": (((1,), (1,)), ((), ())), "tn": (((0,), (0,)), ((), ()))}[mode]
    return lax.dot_general(a.astype(BF16), b.astype(BF16), dn, preferred_element_type=F32)


def _sigmoid(x):
    return jax.nn.sigmoid(x)


def _softplus(x):
    return jnp.maximum(x, 0.0) + jnp.log1p(jnp.exp(-jnp.abs(x)))


def _gelu(z):
    return 0.5 * z * (1.0 + lax.erf(z * _SQRT_HALF))


def _gelu_grad(z):
    return 0.5 * (1.0 + lax.erf(z * _SQRT_HALF)) + z * jnp.exp(-0.5 * z * z) * _INV_SQRT_2PI


def _mean(x):
    return jnp.mean(x, axis=-1, keepdims=True)


def _colsum(x):
    return jnp.sum(x, axis=0, keepdims=True)


def _rms_fwd(x, g, name, deps=()):
    def fn(rv, pv):
        (xv,), (gv,) = rv, pv
        r = lax.rsqrt(_mean(xv * xv) + RMS_EPS)
        return [xv * r * gv], []
    return _rowwise(fn, [x], [g], [(x.shape[1], BF16)], [], name=name, deps=deps)[0]


def _rms_bwd(dn, x, dres, g, name, deps=()):
    def fn(rv, pv):
        (dnv, xv, drv), (gv,) = rv, pv
        r = lax.rsqrt(_mean(xv * xv) + RMS_EPS)
        yn = xv * r
        dyg = dnv * gv
        dx = drv + r * (dyg - yn * _mean(dyg * yn))
        return [dx, dx], [_colsum(dnv * yn)]
    D = x.shape[1]
    return _rowwise(fn, [dn, x, dres], [g], [(D, F32), (D, BF16)], [(1, D)], name=name, deps=deps)


def _rwkv_layout(RW, Lw, La, Lg, D):
    widths = [RW, RW, RW, Lw, La, Lg]
    pw = [_ceil_to(w, LANE) for w in widths]
    pw[5] += _ceil_to(sum(pw), 2 * D) - sum(pw)
    offs = [sum(pw[:i]) for i in range(6)]
    return widths, pw, offs, sum(pw)


def _pad_rwkv_cols(a, lay):
    widths, pw, _, _ = lay
    pieces, src = [], 0
    for w, p in zip(widths, pw):
        pieces.append(a[:, src:src + w])
        if p > w:
            pieces.append(jnp.zeros((a.shape[0], p - w), a.dtype))
        src += w
    return jnp.concatenate(pieces, axis=1)


def _unpad_rwkv_cols(a, lay):
    widths, _, offs, _ = lay
    return jnp.concatenate([a[:, o:o + w] for o, w in zip(offs, widths)], axis=1)


def _proj_pieces(lay, D, cs):
    widths, _, offs, rcp = lay
    rc = sum(widths)
    segs = [(sum(widths[:j]), widths[j], offs[j]) for j in range(6)] + [(rc, D, rcp + 2 * D), (rc + D, D, rcp), (rc + 2 * D, D, rcp + D)]
    pieces = []
    for start, width, dst in segs:
        n = start
        while n < start + width:
            d, off = divmod(n, cs)
            take = min(cs - off, start + width - n)
            pieces.append((d, off, dst + n - start, take))
            n += take
    return pieces


def _w_in_to_proj(g, lay, D, name):
    nb, rows, cs = g.shape
    icp = lay[3] + 3 * D
    pieces = _proj_pieces(lay, D, cs)
    tm = _pick(rows, (256, 128, 64, 32, 16))

    def body(i_ref, o_ref):
        o_ref[...] = jnp.zeros_like(o_ref)
        for d, src, dst, w in pieces:
            o_ref[:, dst:dst + w] = i_ref[d, :, src:src + w]

    return pl.pallas_call(
        body, name=name, grid=(rows // tm,), in_specs=[pl.BlockSpec((nb, tm, cs), lambda i: (0, i, 0))],
        out_specs=pl.BlockSpec((tm, icp), lambda i: (i, 0)), out_shape=jax.ShapeDtypeStruct((rows, icp), g.dtype),
        compiler_params=_params())(g)


def _dw_in_from_proj(a, lay, D, cs, name):
    rows, icp = a.shape
    pieces = _proj_pieces(lay, D, cs)
    tm = _pick(rows, (256, 128, 64, 32, 16))

    def body(i_ref, o_ref):
        for d, src, dst, w in pieces:
            o_ref[d, :, src:src + w] = i_ref[:, dst:dst + w]

    return pl.pallas_call(
        body, name=name, grid=(rows // tm,), in_specs=[pl.BlockSpec((tm, icp), lambda i: (i, 0))],
        out_specs=pl.BlockSpec((N_DEV, tm, cs), lambda i: (0, i, 0)), out_shape=jax.ShapeDtypeStruct((N_DEV, rows, cs), a.dtype),
        compiler_params=_params())(a)


def _pad_rows(a, rows):
    return a if a.shape[0] == rows else jnp.concatenate([a, jnp.zeros((rows - a.shape[0], a.shape[1]), a.dtype)], axis=0)


def _token_shift(p, halo, mu, i):
    tm = p.shape[0]
    hid = lax.broadcasted_iota(jnp.int32, (SUBLANE, 1), 0)
    before = jnp.sum(jnp.where(hid == SUBLANE - 1, halo, 0.0), axis=0, keepdims=True)
    before = jnp.where(i == 0, 0.0, before)
    rid = lax.broadcasted_iota(jnp.int32, (tm, 1), 0)
    prev = jnp.where(rid == 0, before, pltpu.roll(p, 1, 0))
    d = prev - p
    return p + d * mu, d


def _rwkv_math(ps, w0, a0, k_k, k_a, wlw, wla, wlg, lay):
    _, pw, offs, _ = lay
    r, k, v, xw, xa, xg = (ps[:, offs[j]:offs[j] + pw[j]] for j in range(6))
    tw = jnp.tanh(xw)
    ww = w0 + _bdot(tw, wlw)
    lw = -jnp.exp(-_softplus(-ww) - 0.5)
    a = _sigmoid(a0 + _bdot(xa, wla))
    sg = _sigmoid(xg)
    g = _bdot(sg, wlg)
    return dict(r=r, k=k, v=v, xa=xa, tw=tw, ww=ww, lw=lw, a=a, sg=sg, g=g, kkp=k * k_k, k2=k * (1.0 + (a - 1.0) * k_a))


def _halo_spec(tm, width):
    hb = tm // SUBLANE
    return pl.BlockSpec((SUBLANE, width), lambda i: (jnp.maximum(i * hb - 1, 0), 0))


def _rowsum(x):
    return jnp.sum(x, axis=-1, keepdims=True)


def _kk_math(kkp):
    nrm = jnp.sqrt(_rowsum(kkp * kkp))
    inv = 1.0 / jnp.maximum(nrm, 1e-12)
    return nrm, inv, kkp * inv


def _rwkv_pre(p, mu, small, lora, lay, name):
    T, rcp = p.shape[0], lay[3]
    H = lay[0][0] // HEAD
    tm = min(256, T)

    def body(p_ref, ph_ref, mu_ref, w0_ref, a0_ref, kk_ref, ka_ref, wlw_ref, wla_ref, wlg_ref, r_o, lw_o, k2_o, v_o, aa_o, bb_o, g_o):
        ps, _ = _token_shift(p_ref[...], ph_ref[...], mu_ref[...], pl.program_id(0))
        q = _rwkv_math(ps, w0_ref[...], a0_ref[...], kk_ref[...], ka_ref[...], wlw_ref[...], wla_ref[...], wlg_ref[...], lay)
        for h in range(H):
            sl = slice(h * HEAD, (h + 1) * HEAD)
            for o_ref, key in ((r_o, "r"), (lw_o, "lw"), (k2_o, "k2"), (v_o, "v"), (g_o, "g")):
                o_ref[h] = q[key][:, sl]
            _, _, kk = _kk_math(q["kkp"][:, sl])
            aa_o[h] = -kk
            bb_o[h] = kk * q["a"][:, sl]

    whole = lambda arr: pl.BlockSpec(arr.shape, lambda i: (0, 0))
    return pl.pallas_call(
        body, name=name, grid=(T // tm,),
        in_specs=([pl.BlockSpec((tm, rcp), lambda i: (i, 0)), _halo_spec(tm, rcp), whole(mu)]
                  + [whole(s) for s in small] + [whole(w) for w in lora]),
        out_specs=[pl.BlockSpec((H, tm, HEAD), lambda i: (0, i, 0))] * 7, out_shape=[jax.ShapeDtypeStruct((H, T, HEAD), F32)] * 7,
        compiler_params=_params())(p, p, mu, *small, *lora)


def _rwkv_pre_bwd(p, mu, small, lora, hgrads, dproj, lay, name):
    T, rcp = p.shape[0], lay[3]
    widths, pw, offs, _ = lay
    RW = widths[0]
    H = RW // HEAD
    tm = min(128, T)
    nt = T // tm
    hb = tm // SUBLANE

    def body(p_ref, ph_ref, mu_ref, w0_ref, a0_ref, kk_ref, ka_ref, wlw_ref, wla_ref, wlg_ref,
             dr_h, dk2_h, dv_h, dlw_h, daa, dbb, dg_h, buf_ref,
             dp_ref, dmu_ref, dw0_ref, da0_ref, dkk_ref, dka_ref, dwlw_ref, dwla_ref, dwlg_ref,
             s_dr, s_dk2, s_dv, s_dlw, s_dkkp, s_da, s_dg, dps_ref, next_ref):
        i = pl.program_id(0)
        ps, dprev = _token_shift(p_ref[...], ph_ref[...], mu_ref[...], nt - 1 - i)
        k_k, k_a = kk_ref[...], ka_ref[...]
        q = _rwkv_math(ps, w0_ref[...], a0_ref[...], k_k, k_a, wlw_ref[...], wla_ref[...], wlg_ref[...], lay)
        k, a, lw, ww, tw, sg = q["k"], q["a"], q["lw"], q["ww"], q["tw"], q["sg"]
        for h in range(H):
            sl = slice(h * HEAD, (h + 1) * HEAD)
            s_dr[:, sl] = dr_h[h]
            s_dk2[:, sl] = dk2_h[h]
            s_dv[:, sl] = dv_h[h]
            s_dlw[:, sl] = dlw_h[h]
            s_dg[:, sl] = dg_h[h]
            nrm, inv, kk = _kk_math(q["kkp"][:, sl])
            dbb_h = dbb[h]
            dkk = dbb_h * a[:, sl] - daa[h]
            s_dkkp[:, sl] = jnp.where(nrm > 1e-12, inv * (dkk - kk * _rowsum(dkk * kk)), dkk * inv)
            s_da[:, sl] = dbb_h * kk
        dk2, dkkp, dg = s_dk2[...], s_dkkp[...], s_dg[...]
        dk = dk2 * (1.0 + (a - 1.0) * k_a) + dkkp * k_k
        da = s_da[...] + dk2 * k * k_a
        dpa = da * a * (1.0 - a)
        dww = s_dlw[...] * lw * _sigmoid(-ww)
        dxa = _bdot(dpa, wla_ref[...], "nt")
        dxw = _bdot(dww, wlw_ref[...], "nt") * (1.0 - tw * tw)
        dxg = _bdot(dg, wlg_ref[...], "nt") * sg * (1.0 - sg)
        segs = (s_dr[...], dk, s_dv[...], dxw, dxa, dxg)
        sums = [dmu_ref, dw0_ref, da0_ref, dkk_ref, dka_ref, dwlw_ref, dwla_ref, dwlg_ref]

        @pl.when(i == 0)
        def _():
            for s in sums + [next_ref]:
                s[...] = jnp.zeros_like(s)

        for j, seg in enumerate(segs):
            sl = slice(offs[j], offs[j] + pw[j])
            dps_ref[:, sl] = seg
            dmu_ref[:, sl] += _colsum(seg * dprev[:, sl])
        dw0_ref[...] += _colsum(dww)
        da0_ref[...] += _colsum(dpa)
        dkk_ref[...] += _colsum(dkkp * k)
        dka_ref[...] += _colsum(dk2 * k * (a - 1.0))
        dwlw_ref[...] += _bdot(tw, dww, "tn")
        dwla_ref[...] += _bdot(q["xa"], dpa, "tn")
        dwlg_ref[...] += _bdot(sg, dg, "tn")
        dps = dps_ref[...]
        rid = lax.broadcasted_iota(jnp.int32, (tm, 1), 0)
        nxt = jnp.where(rid == tm - 1, next_ref[...], pltpu.roll(dps, tm - 1, 0))
        mu_v = mu_ref[...]
        dp_ref[...] = (dps * (1.0 - mu_v) + nxt * mu_v).astype(BF16)
        next_ref[...] = _colsum(jnp.where(rid == 0, dps, 0.0))

    whole = lambda arr: pl.BlockSpec(arr.shape, lambda i: (0, 0))
    row = lambda w: pl.BlockSpec((tm, w), lambda i: (nt - 1 - i, 0))
    acc_shapes = [(1, rcp), (1, RW), (1, RW), (1, RW), (1, RW)] + [w.shape for w in lora]
    return pl.pallas_call(
        body, name=name, grid=(nt,),
        in_specs=([row(rcp), pl.BlockSpec((SUBLANE, rcp), lambda i: (jnp.maximum((nt - 1 - i) * hb - 1, 0), 0)), whole(mu)]
                  + [whole(s) for s in small] + [whole(w) for w in lora]
                  + [pl.BlockSpec((H, tm, HEAD), lambda i: (0, nt - 1 - i, 0))] * 7 + [pl.BlockSpec(memory_space=pl.ANY)]),
        out_specs=[row(rcp)] + [pl.BlockSpec(s, lambda i: (0, 0)) for s in acc_shapes],
        out_shape=[jax.ShapeDtypeStruct(dproj.shape, BF16)] + [jax.ShapeDtypeStruct(s, F32) for s in acc_shapes],
        scratch_shapes=[pltpu.VMEM((tm, RW), F32)] * 7 + [pltpu.VMEM((tm, rcp), F32), pltpu.VMEM((1, rcp), F32)],
        input_output_aliases={10 + 7: 0}, compiler_params=_params())(p, p, mu, *small, *lora, *hgrads, dproj)


def _head_post_math(y, r, k2, v, lg, lb, rk):
    yc = y - _mean(y)
    rstd = lax.rsqrt(_mean(yc * yc) + LNX_EPS)
    yn = yc * rstd
    s = _rowsum(r * k2 * rk)
    return yn, rstd, yn * lg + lb + s * v, s


def _head_post(y, r, k2, v, g, hp, name, deps=()):
    H, T, _ = y.shape
    tm = min(256, T)

    def body(y_ref, r_ref, k_ref, v_ref, g_ref, lg_ref, lb_ref, rk_ref, *rest):
        o_ref = rest[-1]
        _, _, t, _ = _head_post_math(y_ref[...], r_ref[...], k_ref[...], v_ref[...], lg_ref[...], lb_ref[...], rk_ref[...])
        out = (t * g_ref[...]).astype(BF16)
        for h in range(H):
            o_ref[:, h * HEAD:(h + 1) * HEAD] = out[h]

    blk = pl.BlockSpec((H, tm, HEAD), lambda i: (0, i, 0))
    par = pl.BlockSpec((H, 1, HEAD), lambda i: (0, 0, 0))
    return pl.pallas_call(
        body, name=name, grid=(T // tm,),
        in_specs=[blk] * 5 + [par] * 3 + [pl.BlockSpec(d.shape, lambda i, nd=d.ndim: (0,) * nd) for d in deps],
        out_specs=pl.BlockSpec((tm, H * HEAD), lambda i: (i, 0)),
        out_shape=jax.ShapeDtypeStruct((T, H * HEAD), BF16), compiler_params=_params())(y, r, k2, v, g, *hp, *deps)


def _bmm(x, y, mode):
    dn = {"nn": (((2,), (1,)), ((0,), (0,))), "nt": (((2,), (2,)), ((0,), (0,))), "tn": (((1,), (1,)), ((0,), (0,)))}[mode]
    (xh, xl), (yh, yl) = _split(x), _split(y)
    dot = lambda p, q: lax.dot_general(p, q, dn, preferred_element_type=F32)
    out = dot(xh, yh)
    if yl is not None:
        out = out + dot(xh, yl)
    if xl is not None:
        out = out + dot(xl, yh)
    return out


def _split(x):
    if isinstance(x, tuple):
        return x
    hi = x.astype(BF16)
    return hi, (x - hi.astype(F32)).astype(BF16)


def _exact(x):
    return x.astype(BF16), None


def _round(x):
    return x if isinstance(x, tuple) else (x.astype(BF16), None)


def _rows(*xs):
    if isinstance(xs[0], tuple):
        return tuple(None if any(p is None for p in parts) else jnp.concatenate(parts, axis=1) for parts in zip(*xs))
    return jnp.concatenate(xs, axis=1)


def _wkv_chunk(r, lw, k, v, a, b, inverse=None):
    hb, C, _ = r.shape
    ti = lax.broadcasted_iota(jnp.int32, (C, C), 0)
    si = lax.broadcasted_iota(jnp.int32, (C, C), 1)
    linc, lstr, eye = (ti >= si).astype(F32), (ti > si).astype(F32), (ti == si).astype(F32)
    qmask = jnp.concatenate([jnp.concatenate([lstr, lstr], axis=1), jnp.concatenate([linc, linc], axis=1)], axis=0)
    lincb = _exact(jnp.broadcast_to(linc, (hb, C, C)))
    both = _exact(jnp.broadcast_to(jnp.concatenate([linc, lstr], axis=0), (hb, 2 * C, C)))
    ones = _exact(jnp.ones_like(v))
    lws = _split(lw)
    ci = _bmm(lincb, lws, "nn")
    cC = jnp.sum(lw, axis=1, keepdims=True)
    gi, ge, gn, gr = jnp.exp(ci), jnp.exp(ci - lw), jnp.exp(-ci), jnp.exp(cC - ci)
    q = dict(At=a * ge, Rt=r * gi, Bt=b * gn, Kt=k * gn, Bh=b * gr, Kh=k * gr)
    s = dict(AR=_round(_rows(q["At"], q["Rt"])), BK=_round(_rows(q["Bt"], q["Kt"])), BKh=_round(_rows(q["Bh"], q["Kh"])), v=_round(v))
    quad = _bmm(s["AR"], s["BK"], "nt") * qmask
    s["top"], s["bot"] = _round(quad[:, :C]), _round(quad[:, C:])
    if inverse is None:
        A_ab = quad[:, :C, :C]
        Tm = eye + A_ab
        Pw = _round(A_ab)
        n = 1
        while 2 * n < C:
            Pw = _round(_bmm(Pw, Pw, "nn"))
            Tm = Tm + _bmm(_round(Tm), Pw, "nn")
            n *= 2
        inverse = Tm
    s["Tm"] = _round(inverse)
    gC = jnp.exp(_bmm(lws, ones, "tn"))
    q.update(gi=gi, ge=ge, gn=gn, gr=gr, qmask=qmask, both=both, gC=gC, ones=ones, s=s)
    return q


def _wkv_u(s, H0s, C):
    arh = _bmm(s["AR"], H0s, "nn")
    zv = _rows(tuple(None if p is None else jnp.zeros_like(p) for p in s["v"]), s["v"])
    U = _bmm(s["Tm"], _round(arh[:, :C] + _bmm(s["top"], zv, "nn")), "nn")
    return arh, _rows(_round(U), s["v"])


def _wkv_fwd(r, lw, k, v, a, b, name):
    H, T, N = r.shape
    C = min(WKV_CHUNK, T)
    nc = T // C
    hb = _pick(H, (16, 8, 4, 2))

    def body(r_ref, lw_ref, k_ref, v_ref, a_ref, b_ref, y_ref, st_ref, inv_ref, u_ref, h_ref):
        @pl.when(pl.program_id(1) == 0)
        def _():
            h_ref[...] = jnp.zeros_like(h_ref)

        H0 = h_ref[...]
        st_ref[0] = H0
        q = _wkv_chunk(r_ref[...], lw_ref[...], k_ref[...], v_ref[...], a_ref[...], b_ref[...])
        s = q["s"]
        arh, UV = _wkv_u(s, _round(H0), C)
        inv_ref[0] = s["Tm"][0]
        u_ref[...] = UV[0][:, :C]
        y_ref[...] = arh[:, C:] + _bmm(s["bot"], UV, "nn")
        h_ref[...] = q["gC"] * H0 + _bmm(s["BKh"], UV, "tn")

    blk = pl.BlockSpec((hb, C, N), lambda h, c: (h, c, 0))
    per_chunk = lambda w: pl.BlockSpec((1, hb, w, w), lambda h, c: (c, h, 0, 0))
    return pl.pallas_call(
        body, name=name, grid=(H // hb, nc), in_specs=[blk] * 6, out_specs=[blk, per_chunk(N), per_chunk(C), blk],
        out_shape=[jax.ShapeDtypeStruct((H, T, N), F32), jax.ShapeDtypeStruct((nc, H, N, N), F32),
                   jax.ShapeDtypeStruct((nc, H, C, C), BF16), jax.ShapeDtypeStruct((H, T, N), BF16)],
        scratch_shapes=[pltpu.VMEM((hb, N, N), F32)], compiler_params=_params())(r, lw, k, v, a, b)


def _wkv_bwd(r, lw, k, v, a, b, states, inverses, u, y, g, hp, dya, name, deps=()):
    H, T, N = r.shape
    C = min(WKV_CHUNK, T)
    nc = T // C
    hb = _pick(H, (16, 8, 4, 2))
    hsum = lambda t: jnp.sum(t, axis=1, keepdims=True)

    def body(r_ref, lw_ref, k_ref, v_ref, a_ref, b_ref, st_ref, inv_ref, u_ref, y_ref, g_ref, lg_ref, lb_ref, rk_ref, dya_ref, *rest):
        (dr_ref, dlw_ref, dk_ref, dv_ref, da_ref, db_ref, dg_ref, dlg_ref, dlb_ref, drk_ref, dh_ref, d_s) = rest[len(deps):]
        first = pl.program_id(1) == 0

        @pl.when(first)
        def _():
            dh_ref[...] = jnp.zeros_like(dh_ref)

        for h in range(hb):
            d_s[h] = dya_ref[:, h * N:(h + 1) * N]
        d_v, r_v, k_v, v_v, lg, rk = d_s[...], r_ref[...], k_ref[...], v_ref[...], lg_ref[...], rk_ref[...]
        yn, rstd, t, bonus = _head_post_math(y_ref[...], r_v, k_v, v_v, lg, lb_ref[...], rk)
        dyo = d_v * g_ref[...]
        dyn = dyo * lg
        ds = _rowsum(dyo * v_v)
        dy = rstd * (dyn - _mean(dyn) - yn * _mean(dyn * yn))
        dg_ref[...] = d_v * t
        sums = (hsum(dyo * yn), hsum(dyo), hsum(ds * r_v * k_v))

        @pl.when(first)
        def _():
            for o_ref, val in zip((dlg_ref, dlb_ref, drk_ref), sums):
                o_ref[...] = val

        @pl.when(jnp.logical_not(first))
        def _():
            for o_ref, val in zip((dlg_ref, dlb_ref, drk_ref), sums):
                o_ref[...] += val

        dHC = dh_ref[...]
        H0 = st_ref[0]
        q = _wkv_chunk(r_v, lw_ref[...], k_v, v_v, a_ref[...], b_ref[...], inverse=inv_ref[0])
        s, gC = q["s"], q["gC"]
        H0s, dHs, dY = _round(H0), _round(dHC), _round(dy)
        UV = _rows(_round(u_ref[...]), s["v"])
        bot_dy = _bmm(s["bot"], dY, "tn")
        bkh_dh = _bmm(s["BKh"], dHs, "nn")
        dP = _round(_bmm(s["Tm"], _round(bot_dy[:, :C] + bkh_dh[:, :C]), "tn"))
        dv_ref[...] = bot_dy[:, C:] + bkh_dh[:, C:] + _bmm(s["top"], dP, "tn")[:, C:] + dyo * bonus
        dPY = _rows(dP, dY)
        dh_ref[...] = gC * dHC + _bmm(s["AR"], dPY, "tn")
        dquad = _round(_bmm(dPY, UV, "nt") * q["qmask"])
        dAR = _bmm(dPY, H0s, "nt") + _bmm(dquad, s["BK"], "nn")
        dBK = _bmm(dquad, s["AR"], "tn")
        dBKh = _bmm(UV, dHs, "nt")
        dAt, dRt, dBt, dKt, dBh, dKh = dAR[:, :C], dAR[:, C:], dBK[:, :C], dBK[:, C:], dBKh[:, :C], dBKh[:, C:]
        dr_ref[...] = dRt * q["gi"] + ds * k_v * rk
        da_ref[...] = dAt * q["ge"]
        db_ref[...] = dBt * q["gn"] + dBh * q["gr"]
        dk_ref[...] = dKt * q["gn"] + dKh * q["gr"] + ds * r_v * rk
        tail = dBh * q["Bh"] + dKh * q["Kh"]
        dci = dRt * q["Rt"] - dBt * q["Bt"] - dKt * q["Kt"] - tail
        dcC = jnp.sum(tail, axis=1, keepdims=True) + _bmm(q["ones"], H0 * dHC * gC, "nt")
        dlw_ref[...] = _bmm(q["both"], _rows(dci, dAt * q["At"]), "tn") + dcC

    blk = pl.BlockSpec((hb, C, N), lambda h, c: (h, nc - 1 - c, 0))
    per_chunk = lambda w: pl.BlockSpec((1, hb, w, w), lambda h, c: (nc - 1 - c, h, 0, 0))
    par = pl.BlockSpec((hb, 1, N), lambda h, c: (h, 0, 0))
    return pl.pallas_call(
        body, name=name, grid=(H // hb, nc),
        in_specs=([blk] * 6 + [per_chunk(N), per_chunk(C), blk, blk, blk] + [par] * 3
                  + [pl.BlockSpec((C, hb * N), lambda h, c: (nc - 1 - c, h))]
                  + [pl.BlockSpec(d.shape, lambda h, c, nd=d.ndim: (0,) * nd) for d in deps]),
        out_specs=[blk] * 7 + [par] * 3,
        out_shape=[jax.ShapeDtypeStruct((H, T, N), F32)] * 7 + [jax.ShapeDtypeStruct((H, 1, N), F32)] * 3,
        scratch_shapes=[pltpu.VMEM((hb, N, N), F32), pltpu.VMEM((hb, C, N), F32)],
        compiler_params=_params())(r, lw, k, v, a, b, states, inverses, u, y, g, *hp, dya, *deps)


def _sgu_ln(z, SW, lng, lnb):
    ge = _gelu(z)
    u, vv = ge[:, :SW], ge[:, SW:]
    xc = vv - _mean(vv)
    rstd = lax.rsqrt(_mean(xc * xc) + LN_EPS)
    vn = xc * rstd
    return u, vn, rstd, vn * lng + lnb


def _causal(ws_ref, g):
    ti = lax.broadcasted_iota(jnp.int32, (SGU_CHUNK, SGU_CHUNK), 0)
    si = lax.broadcasted_iota(jnp.int32, (SGU_CHUNK, SGU_CHUNK), 1)
    return ti >= si, jnp.where(ti >= si, ws_ref[g], 0.0).astype(BF16)


def _sgu_fwd(proj, zblock, lng, lnb, ws, bexp, name):
    T, SW = proj.shape[0], lng.shape[1]
    G = ws.shape[0]
    tr = min(256, T)
    nch = tr // SGU_CHUNK

    def body(z_ref, lng_ref, lnb_ref, ws_ref, be_ref, o_ref):
        u, _, _, vl = _sgu_ln(z_ref[...], SW, lng_ref[...], lnb_ref[...])
        for g in range(G):
            cs = slice(g * SGU_GROUP, (g + 1) * SGU_GROUP)
            _, wc = _causal(ws_ref, g)
            for n in range(nch):
                rs = slice(n * SGU_CHUNK, (n + 1) * SGU_CHUNK)
                m = jnp.dot(wc, vl[rs, cs].astype(BF16), preferred_element_type=F32) + be_ref[:, cs]
                o_ref[rs, cs] = (u[rs, cs] * m).astype(BF16)

    whole = lambda arr: pl.BlockSpec(arr.shape, lambda i, nd=arr.ndim: (0,) * nd)
    return pl.pallas_call(
        body, name=name, grid=(T // tr,),
        in_specs=[pl.BlockSpec((tr, 2 * SW), lambda i: (i, zblock)), whole(lng), whole(lnb), whole(ws), whole(bexp)],
        out_specs=pl.BlockSpec((tr, SW), lambda i: (i, 0)), out_shape=jax.ShapeDtypeStruct((T, SW), BF16),
        compiler_params=_params())(proj, lng, lnb, ws, bexp)


def _sgu_bwd(proj, zblock, dyb, lng, lnb, ws, bexp, dproj, name):
    T, SW = proj.shape[0], lng.shape[1]
    G = ws.shape[0]
    tr = min(256, T)
    nch = tr // SGU_CHUNK
    nt = T // tr

    def body(z_ref, dy_ref, lng_ref, lnb_ref, ws_ref, be_ref, buf_ref, dz_ref, dlg_ref, dlb_ref, dws_ref, db_ref, du_s, dvl_s, dbacc_s):
        i = pl.program_id(0)
        zv = z_ref[...]
        lng_v = lng_ref[...]
        u, vn, rstd, vl = _sgu_ln(zv, SW, lng_v, lnb_ref[...])

        @pl.when(i == 0)
        def _():
            for s in (dlg_ref, dlb_ref, dws_ref, dbacc_s):
                s[...] = jnp.zeros_like(s)

        for g in range(G):
            cs = slice(g * SGU_GROUP, (g + 1) * SGU_GROUP)
            tri, wc = _causal(ws_ref, g)
            for n in range(nch):
                rs = slice(n * SGU_CHUNK, (n + 1) * SGU_CHUNK)
                blk = vl[rs, cs].astype(BF16)
                m = jnp.dot(wc, blk, preferred_element_type=F32) + be_ref[:, cs]
                dyv = dy_ref[rs, cs]
                du_s[rs, cs] = dyv * m
                dm = dyv * u[rs, cs]
                dvl_s[rs, cs] = _bdot(wc, dm, "tn")
                dws_ref[g] += jnp.where(tri, _bdot(dm, blk, "nt"), 0.0)
                dbacc_s[:, cs] += dm

        dvl = dvl_s[...]
        dlg_ref[...] += _colsum(dvl * vn)
        dlb_ref[...] += _colsum(dvl)
        dvn = dvl * lng_v
        dvv = rstd * (dvn - _mean(dvn) - vn * _mean(dvn * vn))
        gp = _gelu_grad(zv)
        dz_ref[:, :SW] = (du_s[...] * gp[:, :SW]).astype(BF16)
        dz_ref[:, SW:] = (dvv * gp[:, SW:]).astype(BF16)

        @pl.when(i == nt - 1)
        def _():
            lane = lax.broadcasted_iota(jnp.int32, (SGU_CHUNK, LANE), 1)
            out = jnp.zeros((SGU_CHUNK, LANE), F32)
            for g in range(G):
                col = jnp.sum(dbacc_s[:, g * SGU_GROUP:(g + 1) * SGU_GROUP], axis=1, keepdims=True)
                out = jnp.where(lane == g, col, out)
            db_ref[...] = out

    whole = lambda arr: pl.BlockSpec(arr.shape, lambda i, nd=arr.ndim: (0,) * nd)
    acc_shapes = [(1, SW), (1, SW), ws.shape, (SGU_CHUNK, LANE)]
    return pl.pallas_call(
        body, name=name, grid=(nt,),
        in_specs=[pl.BlockSpec((tr, 2 * SW), lambda i: (i, zblock)), pl.BlockSpec((tr, SW), lambda i: (i, 0)),
                  whole(lng), whole(lnb), whole(ws), whole(bexp), pl.BlockSpec(memory_space=pl.ANY)],
        out_specs=([pl.BlockSpec((tr, 2 * SW), lambda i: (i, zblock))]
                   + [pl.BlockSpec(s, lambda i, nd=len(s): (0,) * nd) for s in acc_shapes]),
        out_shape=[jax.ShapeDtypeStruct(dproj.shape, BF16)] + [jax.ShapeDtypeStruct(s, F32) for s in acc_shapes],
        scratch_shapes=[pltpu.VMEM((tr, SW), F32), pltpu.VMEM((tr, SW), F32), pltpu.VMEM((SGU_CHUNK, SW), F32)],
        input_output_aliases={6: 0}, compiler_params=_params())(proj, dyb, lng, lnb, ws, bexp, dproj)


_HBM = pl.BlockSpec(memory_space=pltpu.HBM)
_SEM = pl.BlockSpec(memory_space=pltpu.SEMAPHORE)
_DATAFLOW = pltpu.SideEffectType.DATAFLOW_SIDE_EFFECTING


def _mesh_place(chips=False):
    x, y, c = lax.axis_index("x"), lax.axis_index("y"), lax.axis_index("c")
    return x, y, c, (2 * x + y if chips else 4 * x + 2 * y + c)


def _peer(x, y, c, rel, chips=False):
    px = 1 - x if rel & 4 else x
    py = 1 - y if rel & 2 else y
    pc = 1 - c if rel & 1 else c
    return (px, py, pc), (2 * px + py if chips else 4 * px + 2 * py + pc)


ALL_PEERS = tuple(range(1, N_DEV))
SIBLING = (1,)
SAME_CORE = (2, 4, 6)
SIBLINGS_CORE = (3, 5, 7)


def _exchange_start(groups, name, rels=ALL_PEERS, chips=False):
    flat = [t for g in groups for t in g]
    sizes = [len(g) for g in groups]
    n, ng = len(flat), len(groups)
    srcs = [pltpu.with_memory_space_constraint(a, pltpu.HBM) for a, _ in flat]
    lands = [pltpu.with_memory_space_constraint(lax.empty(((N_DEV,) + a.shape) if isg else a.shape, a.dtype), pltpu.HBM)
             for a, isg in flat]

    def body(*refs):
        ins, lnd, sems, token = refs[:n], refs[n:2 * n], refs[2 * n:2 * n + 3 * ng], refs[-1]
        x, y, c, me = _mesh_place(chips)
        j0 = 0
        for gi, sz in enumerate(sizes):
            for rel in rels:
                dev, slot = _peer(x, y, c, rel, chips)
                for jj in range(sz):
                    j = j0 + jj
                    pltpu.make_async_remote_copy(
                        src_ref=ins[j] if flat[j][1] else ins[j].at[slot], dst_ref=lnd[j].at[me],
                        send_sem=sems[3 * gi].at[jj * (N_DEV - 1) + rel - 1], recv_sem=sems[3 * gi + 1].at[jj * (N_DEV - 1) + rel - 1],
                        device_id=dev, device_id_type=pl.DeviceIdType.MESH).start()
            for jj in range(sz):
                j = j0 + jj
                pltpu.make_async_copy(ins[j] if flat[j][1] else ins[j].at[me], lnd[j].at[me], sems[3 * gi + 2].at[jj]).start()
            j0 += sz
        token[...] = jnp.zeros_like(token)

    sem_shapes = [pltpu.SemaphoreType.DMA((k,)) for sz in sizes for k in (sz * (N_DEV - 1), sz * (N_DEV - 1), sz)]
    res = pl.pallas_call(
        body, name=name,
        out_shape=(*sem_shapes, *[pltpu.HBM(a.shape, a.dtype) for a in srcs], *[pltpu.HBM(a.shape, a.dtype) for a in lands],
                   jax.ShapeDtypeStruct((SUBLANE, LANE), F32)),
        in_specs=[_HBM] * (2 * n), out_specs=(*[_SEM] * (3 * ng), *[_HBM] * (2 * n), pl.BlockSpec(memory_space=pltpu.VMEM)),
        input_output_aliases={i: 3 * ng + i for i in range(2 * n)},
        compiler_params=pltpu.CompilerParams(has_side_effects=_DATAFLOW))(*srcs, *lands)
    sems, thru, token = res[:3 * ng], res[3 * ng:3 * ng + 2 * n], res[-1]
    handle, j0 = [], 0
    for gi, sz in enumerate(sizes):
        handle.append(dict(kinds=[k for _, k in groups[gi]], chips=chips, srcs=list(thru[j0:j0 + sz]), lands=list(thru[n + j0:n + j0 + sz]),
                           sems=list(sems[3 * gi:3 * gi + 3])))
        j0 += sz
    return handle, token


def _exchange_wait(group, after, name, rels=ALL_PEERS, local=True):
    kinds, sz = group["kinds"], len(group["kinds"])
    relay = group.get("relay", [])

    def body(*refs):
        ins, lnd, (ssem, rsem, lsem) = refs[:sz], refs[sz:2 * sz], refs[2 * sz:2 * sz + 3]
        x, y, c, me = _mesh_place(group["chips"])
        for rel in rels:
            dev, slot = _peer(x, y, c, rel, group["chips"])
            for jj in range(sz):
                cp = pltpu.make_async_remote_copy(
                    src_ref=ins[jj] if kinds[jj] else ins[jj].at[slot], dst_ref=lnd[jj].at[slot],
                    send_sem=ssem.at[jj * (N_DEV - 1) + rel - 1], recv_sem=rsem.at[jj * (N_DEV - 1) + rel - 1],
                    device_id=dev, device_id_type=pl.DeviceIdType.MESH)
                cp.wait_send()
                cp.wait_recv()
        if local:
            for jj in range(sz):
                pltpu.make_async_copy(ins[jj] if kinds[jj] else ins[jj].at[me], lnd[jj].at[me], lsem.at[jj]).wait()
        if relay:
            fsend, frecv = refs[2 * sz + 3:2 * sz + 5]
            dev = _peer(x, y, c, 1)[0]
            for q, (mine, theirs) in enumerate(zip(SAME_CORE, SIBLINGS_CORE)):
                for jj in range(sz):
                    cp = pltpu.make_async_remote_copy(
                        src_ref=lnd[jj].at[_peer(x, y, c, mine)[1]], dst_ref=lnd[jj].at[_peer(x, y, c, theirs)[1]],
                        send_sem=fsend.at[jj * len(SAME_CORE) + q], recv_sem=frecv.at[jj * len(SAME_CORE) + q],
                        device_id=dev, device_id_type=pl.DeviceIdType.MESH)
                    cp.wait_send()
                    cp.wait_recv()

    arrays = group["srcs"] + group["lands"]
    sems = group["sems"] + relay
    res = pl.pallas_call(
        body, name=name, out_shape=[pltpu.HBM(a.shape, a.dtype) for a in arrays],
        in_specs=[_HBM] * (2 * sz) + [_SEM] * len(sems) + [pl.BlockSpec(memory_space=pl.ANY)], out_specs=[_HBM] * (2 * sz),
        input_output_aliases={i: i for i in range(2 * sz)},
        compiler_params=pltpu.CompilerParams(has_side_effects=_DATAFLOW))(*arrays, *sems, after)
    return dict(group, srcs=list(res[:sz]), lands=list(res[sz:]), relay=[])


def _relay_start(group, name):
    sz = len(group["kinds"])
    nq = len(SAME_CORE)

    def body(*refs):
        lnd, fsend, frecv, token = refs[:sz], refs[sz], refs[sz + 1], refs[-1]
        x, y, c, _ = _mesh_place()
        dev = _peer(x, y, c, 1)[0]
        for q, rel in enumerate(SAME_CORE):
            slot = _peer(x, y, c, rel)[1]
            for jj in range(sz):
                pltpu.make_async_remote_copy(
                    src_ref=lnd[jj].at[slot], dst_ref=lnd[jj].at[slot], send_sem=fsend.at[jj * nq + q], recv_sem=frecv.at[jj * nq + q],
                    device_id=dev, device_id_type=pl.DeviceIdType.MESH).start()
        token[...] = jnp.zeros_like(token)

    lands = group["lands"]
    res = pl.pallas_call(
        body, name=name,
        out_shape=(pltpu.SemaphoreType.DMA((sz * nq,)), pltpu.SemaphoreType.DMA((sz * nq,)), *[pltpu.HBM(a.shape, a.dtype) for a in lands],
                   jax.ShapeDtypeStruct((SUBLANE, LANE), F32)),
        in_specs=[_HBM] * sz, out_specs=(_SEM, _SEM, *[_HBM] * sz, pl.BlockSpec(memory_space=pltpu.VMEM)),
        input_output_aliases={i: 2 + i for i in range(sz)},
        compiler_params=pltpu.CompilerParams(has_side_effects=_DATAFLOW))(*lands)
    return dict(group, lands=list(res[2:2 + sz]), relay=[res[0], res[1]]), res[-1]


def _sibling_swap(arrays, handle, after, name):
    start = handle is None
    n = len(arrays) if start else len(handle["srcs"])
    chips = N_DEV // 2
    if start:
        srcs = [pltpu.with_memory_space_constraint(a.reshape(chips, 2, *a.shape[1:]), pltpu.HBM) for a in arrays]
        lands = [pltpu.with_memory_space_constraint(lax.empty((chips,) + a.shape[1:], a.dtype), pltpu.HBM) for a in arrays]
    else:
        srcs, lands = handle["srcs"], handle["lands"]

    def body(*refs):
        ins, lnd, ssem, rsem = refs[:n], refs[n:2 * n], refs[2 * n], refs[2 * n + 1]
        x, y, c, _ = _mesh_place()
        dev = _peer(x, y, c, 1)[0]
        for q in range(chips):
            for j in range(n):
                cp = pltpu.make_async_remote_copy(
                    src_ref=ins[j].at[q, 1 - c], dst_ref=lnd[j].at[q], send_sem=ssem.at[j * chips + q], recv_sem=rsem.at[j * chips + q],
                    device_id=dev, device_id_type=pl.DeviceIdType.MESH)
                if start:
                    cp.start()
                else:
                    cp.wait_send()
                    cp.wait_recv()
        if start:
            refs[-1][...] = jnp.zeros_like(refs[-1])

    thru = [pltpu.HBM(a.shape, a.dtype) for a in srcs + lands]
    effect = pltpu.CompilerParams(has_side_effects=_DATAFLOW)
    if start:
        res = pl.pallas_call(
            body, name=name, out_shape=(pltpu.SemaphoreType.DMA((n * chips,)), pltpu.SemaphoreType.DMA((n * chips,)), *thru,
                                        jax.ShapeDtypeStruct((SUBLANE, LANE), F32)),
            in_specs=[_HBM] * (2 * n), out_specs=(_SEM, _SEM, *[_HBM] * (2 * n), pl.BlockSpec(memory_space=pltpu.VMEM)),
            input_output_aliases={i: 2 + i for i in range(2 * n)}, compiler_params=effect)(*srcs, *lands)
        return dict(srcs=list(res[2:2 + n]), lands=list(res[2 + n:2 + 2 * n]), sems=[res[0], res[1]]), res[-1]
    res = pl.pallas_call(
        body, name=name, out_shape=thru, in_specs=[_HBM] * (2 * n) + [_SEM, _SEM, pl.BlockSpec(memory_space=pl.ANY)],
        out_specs=[_HBM] * (2 * n), input_output_aliases={i: i for i in range(2 * n)}, compiler_params=effect)(
            *srcs, *lands, *handle["sems"], after)
    return dict(handle, srcs=list(res[:n]), lands=list(res[n:]))


def _pair_add(mine, theirs, core, name):
    chips, _, rows, w = mine.shape
    tm = _pick(rows, (256, 128, 64, 32, 16))

    def body(core_ref, a_ref, b_ref, o_ref):
        o_ref[...] = (a_ref[...].astype(F32) + b_ref[...].astype(F32)).astype(o_ref.dtype)

    return pl.pallas_call(
        body, name=name, out_shape=jax.ShapeDtypeStruct(theirs.shape, theirs.dtype),
        grid_spec=pltpu.PrefetchScalarGridSpec(
            num_scalar_prefetch=1, grid=(chips, rows // tm),
            in_specs=[pl.BlockSpec((None, None, tm, w), lambda q, i, core_ref: (q, core_ref[0], i, 0)),
                      pl.BlockSpec((None, tm, w), lambda q, i, core_ref: (q, i, 0))],
            out_specs=pl.BlockSpec((None, tm, w), lambda q, i, core_ref: (q, i, 0))),
        compiler_params=_params())(core, mine, theirs)


def _adamw(w, m, v, gparts, name, after=None):
    R, C = w.shape
    tm = _pick(R, (256, 128, 64, 32, 16, 8))
    order = [] if after is None else [after]

    def body(w_ref, m_ref, v_ref, g_ref, *rest):
        go, do, mo, vo = rest[len(order):]
        g = g_ref[0].astype(F32)
        for j in range(1, gparts.shape[0]):
            g = g + g_ref[j].astype(F32)
        mn = ADAM_B1 * m_ref[...] + (1.0 - ADAM_B1) * g
        vn = ADAM_B2 * v_ref[...] + (1.0 - ADAM_B2) * (g * g)
        m_hat = mn / (1.0 - ADAM_B1 ** ADAM_STEP)
        v_hat = vn / (1.0 - ADAM_B2 ** ADAM_STEP)
        go[...] = g
        do[...] = -ADAM_LR * (m_hat / (jnp.sqrt(v_hat) + ADAM_EPS) + ADAM_WD * w_ref[...])
        mo[...] = mn
        vo[...] = vn

    row = pl.BlockSpec((tm, C), lambda i: (i, 0))
    return pl.pallas_call(
        body, name=name, grid=(R // tm,),
        in_specs=[row, row, row, pl.BlockSpec((gparts.shape[0], tm, C), lambda i: (0, i, 0))] + [pl.BlockSpec(memory_space=pl.ANY)] * len(order),
        out_specs=[row] * 4, out_shape=[jax.ShapeDtypeStruct((R, C), F32)] * 4, compiler_params=_params())(w, m, v, gparts, *order)


def _pack(arrays):
    parts = []
    for a in arrays:
        f = a.reshape(1, -1)
        pad = _ceil_to(f.shape[1], SUBLANE * LANE) - f.shape[1]
        f = jnp.concatenate([f, jnp.zeros((1, pad), f.dtype)], axis=1) if pad else f
        parts.append(f.reshape(-1, LANE))
    rows = sum(p.shape[0] for p in parts)
    pad = _ceil_to(rows, 64) - rows
    return jnp.concatenate(parts + ([jnp.zeros((pad, LANE), parts[0].dtype)] if pad else []), axis=0)


def _unpack(buf, shapes):
    out, row = [], 0
    for s in shapes:
        size = 1
        for d in s:
            size *= d
        rows = _ceil_to(size, SUBLANE * LANE) // LANE
        out.append(buf[row:row + rows].reshape(1, -1)[:, :size].reshape(s))
        row += rows
    return out


def kernel(x, norm_mix_g, w_in, shift_mu, w0, w_lora_up, a0, a_lora_up, g_lora_up, k_k, k_a, r_k, lnx_g, lnx_b, w_proj_rwkv, sgu_ln_g, sgu_ln_b, sgu_w, sgu_b, w_proj_sgu, w_out, norm_ffn_g, w_ffn_gate, w_ffn_up, w_ffn_down, norm_final_g, loss_target, m_norm_mix_g, m_w_in, m_shift_mu, m_w0, m_w_lora_up, m_a0, m_a_lora_up, m_g_lora_up, m_k_k, m_k_a, m_r_k, m_lnx_g, m_lnx_b, m_w_proj_rwkv, m_sgu_ln_g, m_sgu_ln_b, m_sgu_w, m_sgu_b, m_w_proj_sgu, m_w_out, m_norm_ffn_g, m_w_ffn_gate, m_w_ffn_up, m_w_ffn_down, m_norm_final_g, v_norm_mix_g, v_w_in, v_shift_mu, v_w0, v_w_lora_up, v_a0, v_a_lora_up, v_g_lora_up, v_k_k, v_k_a, v_r_k, v_lnx_g, v_lnx_b, v_w_proj_rwkv, v_sgu_ln_g, v_sgu_ln_b, v_sgu_w, v_sgu_b, v_w_proj_sgu, v_w_out, v_norm_ffn_g, v_w_ffn_gate, v_w_ffn_up, v_w_ffn_down, v_norm_final_g):
    weights = dict(norm_mix_g=norm_mix_g, w_in=w_in, shift_mu=shift_mu, w0=w0, w_lora_up=w_lora_up, a0=a0, a_lora_up=a_lora_up,
                   g_lora_up=g_lora_up, k_k=k_k, k_a=k_a, r_k=r_k, lnx_g=lnx_g, lnx_b=lnx_b, w_proj_rwkv=w_proj_rwkv,
                   sgu_ln_g=sgu_ln_g, sgu_ln_b=sgu_ln_b, sgu_w=sgu_w, sgu_b=sgu_b, w_proj_sgu=w_proj_sgu, w_out=w_out,
                   norm_ffn_g=norm_ffn_g, w_ffn_gate=w_ffn_gate, w_ffn_up=w_ffn_up, w_ffn_down=w_ffn_down, norm_final_g=norm_final_g)
    m_in = dict(norm_mix_g=m_norm_mix_g, w_in=m_w_in, shift_mu=m_shift_mu, w0=m_w0, w_lora_up=m_w_lora_up, a0=m_a0,
                a_lora_up=m_a_lora_up, g_lora_up=m_g_lora_up, k_k=m_k_k, k_a=m_k_a, r_k=m_r_k, lnx_g=m_lnx_g, lnx_b=m_lnx_b,
                w_proj_rwkv=m_w_proj_rwkv, sgu_ln_g=m_sgu_ln_g, sgu_ln_b=m_sgu_ln_b, sgu_w=m_sgu_w, sgu_b=m_sgu_b,
                w_proj_sgu=m_w_proj_sgu, w_out=m_w_out, norm_ffn_g=m_norm_ffn_g, w_ffn_gate=m_w_ffn_gate, w_ffn_up=m_w_ffn_up,
                w_ffn_down=m_w_ffn_down, norm_final_g=m_norm_final_g)
    v_in = dict(norm_mix_g=v_norm_mix_g, w_in=v_w_in, shift_mu=v_shift_mu, w0=v_w0, w_lora_up=v_w_lora_up, a0=v_a0,
                a_lora_up=v_a_lora_up, g_lora_up=v_g_lora_up, k_k=v_k_k, k_a=v_k_a, r_k=v_r_k, lnx_g=v_lnx_g, lnx_b=v_lnx_b,
                w_proj_rwkv=v_w_proj_rwkv, sgu_ln_g=v_sgu_ln_g, sgu_ln_b=v_sgu_ln_b, sgu_w=v_sgu_w, sgu_b=v_sgu_b,
                w_proj_sgu=v_w_proj_sgu, w_out=v_w_out, norm_ffn_g=v_norm_ffn_g, w_ffn_gate=v_w_ffn_gate, w_ffn_up=v_w_ffn_up,
                w_ffn_down=v_w_ffn_down, norm_final_g=v_norm_final_g)
    names = list(weights)
    col_sharded = ("w_in", "w_lora_up", "a_lora_up", "g_lora_up", "w_proj_rwkv", "w_proj_sgu", "w_ffn_gate", "w_ffn_up")
    row_sharded = ("w_out", "w_ffn_down")
    sharded = [n for n in names if n in col_sharded or n in row_sharded]
    small = [n for n in names if n not in sharded]

    xs, tgt = x[0], loss_target[0]
    T, D = xs.shape
    RW = w0.shape[1]
    H = RW // HEAD
    SW = sgu_ln_g.shape[1]
    G = sgu_w.shape[1]
    assert 2 * SW == D, "the projection layout takes the SGU part to be as wide as a gate"
    lay = _rwkv_layout(RW, w_lora_up.shape[1], a_lora_up.shape[1], g_lora_up.shape[1], D)
    _, pw, _, rcp = lay
    icp = rcp + 3 * D
    b_ga, b_gb, b_z = rcp // D, rcp // D + 1, rcp // D + 2

    gather_groups = dict(win=["w_in", "w_lora_up", "a_lora_up", "g_lora_up"], proj=["w_proj_rwkv", "w_proj_sgu", "w_out"],
                         ffn_gate_up=["w_ffn_gate", "w_ffn_up"], ffn_down=["w_ffn_down"])
    handles, gather_token = _exchange_start([[(weights[n][0].astype(BF16), True) for n in grp] for grp in gather_groups.values()],
                                            "gather_start", rels=SIBLING + SAME_CORE)
    gather = dict(zip(gather_groups, handles))
    full = {}
    relay_tokens = {}
    joined = lambda g: g.transpose(1, 0, 2).reshape(g.shape[1], -1)

    def relay_weights(key, after):
        arrived = _exchange_wait(gather[key], after, "gather_wait_ici_" + key, rels=SAME_CORE, local=False)
        gather[key], relay_tokens[key] = _relay_start(arrived, "gather_relay_" + key)

    def take_weights(key, after):
        done = _exchange_wait(gather[key], after, "gather_wait_d2d_" + key, rels=SIBLING)
        for n, g in zip(gather_groups[key], done["lands"]):
            full[n] = g.reshape(-1, g.shape[2]) if n in row_sharded else g

    packed = [_pack([d[n] for n in small] + [gather_token]) for d in (weights, m_in, v_in)]
    n1 = _rms_fwd(xs, norm_mix_g, "rms_mix", deps=[gather_token, *packed])
    relay_weights("win", n1)
    take_weights("win", relay_tokens["win"])
    W_in = _w_in_to_proj(full["w_in"], lay, D, "w_in_layout")
    lora = [_pad_rows(joined(full[n]), rows) for n, rows in zip(("w_lora_up", "a_lora_up", "g_lora_up"), pw[3:])]
    mu_p = _pad_rwkv_cols(shift_mu, lay)
    rsmall = [w0, a0, k_k, k_a]
    hp = [lnx_g.reshape(H, 1, HEAD), lnx_b.reshape(H, 1, HEAD), r_k.reshape(H, 1, HEAD)]
    ws = sgu_w[0]
    bexp = jnp.repeat(sgu_b[0].T, SGU_GROUP, axis=1)
    gf = norm_final_g.reshape(1, D)

    proj = _matmul(n1, W_in, mode="nn", out_dtype=F32, name="proj_in")
    ga, gb = (proj, D, b_ga), (proj, D, b_gb)
    r_h, lw_h, k2_h, v_h, aa_h, bb_h, g_h = _rwkv_pre(proj, mu_p, rsmall, lora, lay, "rwkv_pre")
    wkv_in = [r_h, lw_h, k2_h, v_h, aa_h, bb_h]
    y_h, *wkv_saved = _wkv_fwd(*wkv_in, "wkv_fwd")
    relay_weights("proj", y_h)
    ya = _head_post(y_h, r_h, k2_h, v_h, g_h, hp, "head_post", deps=[relay_tokens["proj"]])
    relay_weights("ffn_gate_up", ya)
    yb = _sgu_fwd(proj, b_z, sgu_ln_g, sgu_ln_b, ws, bexp, "sgu_fwd")
    take_weights("proj", ya)
    pa = _matmul(ya, full["w_proj_rwkv"], mode="nn", out_dtype=F32, name="proj_a", deps=[relay_tokens["ffn_gate_up"]])

    def merge_fn(pb_v, pa_v, ga_v, gb_v):
        return pb_v, _sigmoid(ga_v) * pa_v + _sigmoid(gb_v) * pb_v
    pb, merged = _matmul(yb, full["w_proj_sgu"], mode="nn", name="proj_b_merge",
                         epi=(merge_fn, [pa, (proj, b_ga * D), (proj, b_gb * D)], [F32, BF16]))
    h1 = _matmul(merged, full["w_out"], mode="nn", out_dtype=F32, name="out_proj", add=xs)
    n2 = _rms_fwd(h1, norm_ffn_g, "rms_ffn")
    relay_weights("ffn_down", n2)
    take_weights("ffn_gate_up", n2)

    def act_fn(gt_v, up_v):
        return gt_v, up_v, gt_v * _sigmoid(gt_v) * up_v
    gt, up, act = _matmul(n2, full["w_ffn_gate"], b2=full["w_ffn_up"], mode="nn", name="ffn_gate_up_act", out_blocks=N_DEV,
                          epi=(act_fn, [], [BF16, BF16, BF16]), deps=[relay_tokens["ffn_down"]])
    take_weights("ffn_down", act)
    h2 = _matmul(act, full["w_ffn_down"], mode="nn", out_dtype=F32, name="ffn_down", add=h1)

    def final_fn(rv, pv):
        (h_v, t_v), (g_v,) = rv, pv
        r = lax.rsqrt(_mean(h_v * h_v) + RMS_EPS)
        yn = h_v * r
        e = yn * g_v - t_v
        loss = 0.5 * jnp.sum(_mean(e * e))
        dout = e * (1.0 / D)
        dyg = dout * g_v
        dh = r * (dyg - yn * _mean(dyg * yn))
        return [dh, dh], [jnp.full((1, LANE), loss, F32), _colsum(dout * yn)]
    dh2, dh2_bf, loss_part, d_gf = _rowwise(final_fn, [h2, tgt], [gf], [(D, F32), (D, BF16)], [(1, LANE), (1, D)], name="final_loss")

    grads = {}

    def start_scatter(group, name):
        blocks = [(grads[n].reshape(N_DEV, -1, grads[n].shape[1]) if n in row_sharded else grads[n], False) for n in group]
        (handle,), token = _exchange_start([blocks], name)
        return handle, token

    def dact_fn(d_v, gt_v, up_v):
        gt_v, up_v = gt_v.astype(F32), up_v.astype(F32)
        s = _sigmoid(gt_v)
        return d_v * up_v * (s * (1.0 + gt_v * (1.0 - s))), d_v * gt_v * s
    dgt, dup = _matmul(dh2_bf, full["w_ffn_down"], mode="nt", name="d_ffn_act", out_blocks=N_DEV,
                       epi=(dact_fn, [gt, up], [BF16, BF16]))
    scatter_groups = dict(ffn_down=["w_ffn_down"], ffn_gate=["w_ffn_gate"], ffn_up=["w_ffn_up"],
                          mid=["w_out", "w_proj_rwkv", "w_proj_sgu"], last=["w_in", "w_lora_up", "a_lora_up", "g_lora_up"])
    scatters = {}
    grads["w_ffn_down"] = _matmul(act, dh2_bf, mode="tn", out_dtype=BF16, name="dw_ffn_down")
    scatters["ffn_down"], token = start_scatter(scatter_groups["ffn_down"], "scatter_start_ffn_down")
    dn2 = _matmul(dgt, full["w_ffn_gate"], mode="nt", out_dtype=F32, name="dn2_gate", deps=[token])
    grads["w_ffn_gate"] = _matmul(n2, dgt, mode="tn", out_dtype=BF16, name="dw_ffn_gate", out_blocks=N_DEV)
    scatters["ffn_gate"], token = start_scatter(scatter_groups["ffn_gate"], "scatter_start_ffn_gate")
    grads["w_ffn_up"] = _matmul(n2, dup, mode="tn", out_dtype=BF16, name="dw_ffn_up", out_blocks=N_DEV, deps=[token])
    scatters["ffn_up"], token = start_scatter(scatter_groups["ffn_up"], "scatter_start_ffn_up")
    dn2 = _matmul(dup, full["w_ffn_up"], mode="nt", out_dtype=F32, name="dn2_up", add=dn2, deps=[token])
    dh1, dh1_bf, d_g2 = _rms_bwd(dn2, h1, dh2, norm_ffn_g, "rms_ffn_bwd")
    dmerged = _matmul(dh1_bf, full["w_out"], mode="nt", out_dtype=F32, name="d_merged")
    grads["w_out"] = _matmul(merged, dh1_bf, mode="tn", out_dtype=BF16, name="dw_out")

    def dmerge_fn(rv, pv):
        d_v, ga_v, gb_v, pa_v, pb_v = rv
        sa, sb = _sigmoid(ga_v), _sigmoid(gb_v)
        dgates = jnp.concatenate([d_v * pa_v * sa * (1.0 - sa), d_v * pb_v * sb * (1.0 - sb)], axis=1)
        return [dgates, d_v * sa, d_v * sb], []
    dproj, dpa, dpb = _rowwise(dmerge_fn, [dmerged, ga, gb, pa, pb], [],
                               [(2 * D, BF16, icp, b_ga // 2, None), (D, BF16), (D, BF16)], [], name="d_merge")
    dya = _matmul(dpa, full["w_proj_rwkv"], mode="nt", out_dtype=F32, name="d_ya")
    dyb = _matmul(dpb, full["w_proj_sgu"], mode="nt", out_dtype=F32, name="d_yb")
    grads["w_proj_rwkv"] = _matmul(ya, dpa, mode="tn", out_dtype=BF16, name="dw_proj_a", out_blocks=N_DEV)
    grads["w_proj_sgu"] = _matmul(yb, dpb, mode="tn", out_dtype=BF16, name="dw_proj_b", out_blocks=N_DEV)
    scatters["mid"], token_mid = start_scatter(scatter_groups["mid"], "scatter_start_mid")
    dproj, d_lng, d_lnb, d_ws, d_bs = _sgu_bwd(proj, b_z, dyb, sgu_ln_g, sgu_ln_b, ws, bexp, dproj, "sgu_bwd")

    dr_h, dlw_h, dk2_h, dv_h, daa, dbb, dg_h, d_lnxg, d_lnxb, d_rk = _wkv_bwd(
        *wkv_in, *wkv_saved, y_h, g_h, hp, dya, "wkv_bwd", deps=[token_mid])
    dproj, d_mu, d_w0, d_a0, d_kk, d_ka, d_wlw, d_wla, d_wlg = _rwkv_pre_bwd(
        proj, mu_p, rsmall, lora, [dr_h, dk2_h, dv_h, dlw_h, daa, dbb, dg_h], dproj, lay, "rwkv_pre_bwd")
    split = lambda g: g.reshape(g.shape[0], N_DEV, -1).transpose(1, 0, 2)
    grads["w_in"] = _dw_in_from_proj(_matmul(n1, dproj, mode="tn", out_dtype=BF16, name="dw_in"), lay, D, w_in.shape[2], "dw_in_layout")
    grads["w_lora_up"] = split(d_wlw[:w_lora_up.shape[1]].astype(BF16))
    grads["a_lora_up"] = split(d_wla[:a_lora_up.shape[1]].astype(BF16))
    grads["g_lora_up"] = split(d_wlg[:g_lora_up.shape[1]].astype(BF16))
    out = {}

    def update_group(key, after):
        handle = scatters[key]
        parts = _exchange_wait(handle, after, "scatter_wait_" + key, rels=SAME_CORE if handle["chips"] else ALL_PEERS)["lands"]
        for n, part in zip(scatter_groups[key], parts):
            res = _adamw(weights[n][0], m_in[n][0], v_in[n][0], part, "adamw_" + n, after=after)
            out[n] = [t.reshape(weights[n].shape) for t in res]
            after = res[0]
        return after

    swap, token_swap = _sibling_swap([grads[n] for n in scatter_groups["last"]], None, None, "scatter_last_swap_start")
    after = update_group("ffn_gate", update_group("ffn_down", token_swap))
    swap = _sibling_swap(None, swap, after, "scatter_last_swap_wait")
    core = lax.axis_index("c").astype(jnp.int32).reshape(1)
    chip_sums = [_pair_add(mine, theirs, core, "scatter_last_add_" + n)
                 for n, mine, theirs in zip(scatter_groups["last"], swap["srcs"], swap["lands"])]
    (scatters["last"],), token_in = _exchange_start([[(s, False) for s in chip_sums]], "scatter_start_last", rels=SAME_CORE, chips=True)
    dn1 = _matmul(dproj, W_in, mode="nt", out_dtype=F32, name="dn1", deps=[token_in])
    dx, _, d_g1 = _rms_bwd(dn1, xs, dh1, norm_mix_g, "rms_mix_bwd")
    small_grads = dict(norm_mix_g=d_g1, shift_mu=_unpad_rwkv_cols(d_mu, lay), w0=d_w0, a0=d_a0, k_k=d_kk, k_a=d_ka, r_k=d_rk,
                       lnx_g=d_lnxg, lnx_b=d_lnxb, sgu_ln_g=d_lng, sgu_ln_b=d_lnb, sgu_w=d_ws, sgu_b=d_bs[:, :G].T,
                       norm_ffn_g=d_g2, norm_final_g=d_gf)
    (gather_small,), after = _exchange_start([[(_pack([small_grads[n] for n in small] + [jnp.zeros_like(gather_token)]), True)]],
                                             "gather_small_start")
    for key in ("ffn_up", "mid", "last"):
        after = update_group(key, after)
    small_parts = _exchange_wait(gather_small, after, "gather_small_wait")["lands"][0]
    res = _adamw(*packed, small_parts, "adamw_small")
    unpacked = [_unpack(t, [weights[n].shape for n in small]) for t in res]
    for i, n in enumerate(small):
        out[n] = [u[i] for u in unpacked]

    loss = lax.psum(loss_part[0, 0], ("x", "y", "c"))
    return (loss, dx[None], *[out[n][0] for n in names], *[out[n][1] for n in names],
            *[out[n][2] for n in names], *[out[n][3] for n in names])
```
